```python
import math
import jax, jax.numpy as jnp
from jax import lax
import numpy as np

D_MODEL = 1024
BATCH = 8
SEQ = 4096
DEPTH = 2

N_META = 16
MLA_HEADS = 8
QK_NOPE = 128
QK_ROPE = 64
V_HEAD = 128
Q_RANK = 256
KV_RANK = 128
ROPE_THETA = 10000.0
Q_BLOCK = 128
LRU_WIDTH = D_MODEL
LRU_BLOCKS = 8
LRU_BLOCK_DIM = LRU_WIDTH // LRU_BLOCKS
LRU_CONV = 4
LRU_C = 8.0
D_FF = 2816
FFN_CONV = 3
DN_ALPHA = (2.0 * DEPTH) ** 0.25
DN_BETA = (8.0 * DEPTH) ** -0.25
LN_EPS = 1e-5
RMS_EPS = 1e-6

IN_PARTS = [Q_RANK, KV_RANK, QK_ROPE, LRU_WIDTH, LRU_WIDTH, D_MODEL, D_MODEL]
IN_COLS = sum(IN_PARTS)
IN_SPLITS = [int(v) for v in np.cumsum(IN_PARTS)[:-1]]

kernel_name = 'hybrid_mla_rglru_convffn_encoder'


def layer_norm(x, g, b):
    xf = x.astype(jnp.float32)
    mu = jnp.mean(xf, axis=-1, keepdims=True)
    xc = xf - mu
    var = jnp.mean(jnp.square(xc), axis=-1, keepdims=True)
    return (xc * lax.rsqrt(var + LN_EPS) * g.astype(jnp.float32) + b.astype(jnp.float32)).astype(x.dtype)


def rms_norm(x, g):
    xf = x.astype(jnp.float32)
    ms = jnp.mean(jnp.square(xf), axis=-1, keepdims=True)
    return (xf * lax.rsqrt(ms + RMS_EPS) * g.astype(jnp.float32)).astype(x.dtype)


def apply_rope(x, cos, sin):
    half = x.shape[-1] // 2
    x1, x2 = x[..., :half], x[..., half:]
    return jnp.concatenate([x1 * cos - x2 * sin, x2 * cos + x1 * sin], axis=-1)


def dwconv(x, w, b, pad_left):
    k, c = w.shape
    y = lax.conv_general_dilated(x, w[:, None, :].astype(x.dtype), window_strides=(1,),
                                 padding=[(pad_left, k - 1 - pad_left)],
                                 dimension_numbers=('NWC', 'WIO', 'NWC'), feature_group_count=c)
    return y + b.astype(x.dtype)


def mla_branch(cq, ckv, kr, q_norm, kv_norm, w_uq, w_uk, w_uv, cos, sin):
    bsz, t = cq.shape[0], cq.shape[1]
    cq = rms_norm(cq, q_norm)
    ckv = rms_norm(ckv, kv_norm)
    q = jnp.einsum('btr,rhd->bthd', cq, w_uq)
    q_nope = q[..., :QK_NOPE]
    q_rope = apply_rope(q[..., QK_NOPE:], cos[:, None, :], sin[:, None, :])
    k_nope = jnp.einsum('btr,rhd->bthd', ckv, w_uk)
    v = jnp.einsum('btr,rhd->bthd', ckv, w_uv)
    k_rope = apply_rope(kr, cos, sin)
    scale = 1.0 / math.sqrt(QK_NOPE + QK_ROPE)
    n_blk = -(-t // Q_BLOCK)
    pad = n_blk * Q_BLOCK - t

    def to_blocks(a):
        a = jnp.pad(a * scale, ((0, 0), (0, pad), (0, 0), (0, 0)))
        a = a.reshape(bsz, n_blk, Q_BLOCK, a.shape[2], a.shape[3])
        return jnp.moveaxis(a, 1, 0)

    def attend(blk):
        qn, qr = blk
        s = jnp.einsum('bqhd,bkhd->bhqk', qn, k_nope) + jnp.einsum('bqhr,bkr->bhqk', qr, k_rope)
        p = jax.nn.softmax(s.astype(jnp.float32), axis=-1).astype(v.dtype)
        return jnp.einsum('bhqk,bkhd->bqhd', p, v)

    o = lax.map(attend, (to_blocks(q_nope), to_blocks(q_rope)))
    o = jnp.moveaxis(o, 0, 1).reshape(bsz, n_blk * Q_BLOCK, MLA_HEADS * V_HEAD)
    return o[:, :t]


def _lin_comb(c1, c2):
    a1, b1 = c1
    a2, b2 = c2
    return a1 * a2, a2 * b1 + b2


def rg_lru(xc, w_rg, b_rg, w_ig, b_ig, lam):
    bsz, t, w = xc.shape
    xg = xc.reshape(bsz, t, LRU_BLOCKS, LRU_BLOCK_DIM)
    r = jax.nn.sigmoid(jnp.einsum('btgi,dgij->dbtgj', xg, w_rg).reshape(2, bsz, t, w) + b_rg[:, None, None, :])
    i = jax.nn.sigmoid(jnp.einsum('btgi,dgij->dbtgj', xg, w_ig).reshape(2, bsz, t, w) + b_ig[:, None, None, :])
    log_a = -LRU_C * r.astype(jnp.float32) * jax.nn.softplus(-lam.astype(jnp.float32))[:, None, None, :]
    a = jnp.exp(log_a)
    u = jnp.sqrt(-jnp.expm1(2.0 * log_a)) * (i * xc[None]).astype(jnp.float32)
    h_fwd = lax.associative_scan(_lin_comb, (a[0], u[0]), axis=1)[1]
    h_bwd = lax.associative_scan(_lin_comb, (a[1], u[1]), axis=1, reverse=True)[1]
    return (h_fwd + h_bwd).astype(xc.dtype)


def _normal(key, shape, fan_in, scale=1.0):
    return jax.random.normal(key, shape, jnp.float32) * (scale * fan_in ** -0.5)


def _fwd_setup_inputs(seed: int = 0) -> dict:
    key = jax.random.key(seed)
    ks = jax.random.split(key, 32)
    L = DEPTH
    u = jax.random.uniform(ks[17], (L, 2, LRU_WIDTH), jnp.float32, 0.9, 0.999)
    s = u ** (1.0 / LRU_C)
    lam = jnp.log(s) - jnp.log1p(-s)
    return {
        'x': jax.random.normal(ks[0], (BATCH, SEQ, D_MODEL), jnp.float32),
        'meta_tokens': jax.random.normal(ks[1], (N_META, D_MODEL), jnp.float32),
        'ln0_g': 1.0 + 0.01 * jax.random.normal(ks[2], (D_MODEL,), jnp.float32),
        'ln0_b': 0.01 * jax.random.normal(ks[3], (D_MODEL,), jnp.float32),
        'w_in': _normal(ks[4], (L, D_MODEL, IN_COLS), D_MODEL),
        'q_norm': 1.0 + 0.01 * jax.random.normal(ks[5], (L, Q_RANK), jnp.float32),
        'kv_norm': 1.0 + 0.01 * jax.random.normal(ks[6], (L, KV_RANK), jnp.float32),
        'w_uq': _normal(ks[7], (L, Q_RANK, MLA_HEADS, QK_NOPE + QK_ROPE), Q_RANK),
        'w_uk': _normal(ks[8], (L, KV_RANK, MLA_HEADS, QK_NOPE), KV_RANK),
        'w_uv': _normal(ks[9], (L, KV_RANK, MLA_HEADS, V_HEAD), KV_RANK, DN_BETA),
        'w_o_mla': _normal(ks[10], (L, MLA_HEADS * V_HEAD, D_MODEL), MLA_HEADS * V_HEAD, DN_BETA),
        'lru_conv_w': _normal(ks[11], (L, LRU_CONV, LRU_WIDTH), LRU_CONV),
        'lru_conv_b': 0.01 * jax.random.normal(ks[12], (L, LRU_WIDTH), jnp.float32),
        'w_rg': _normal(ks[13], (L, 2, LRU_BLOCKS, LRU_BLOCK_DIM, LRU_BLOCK_DIM), LRU_BLOCK_DIM),
        'b_rg': 0.01 * jax.random.normal(ks[14], (L, 2, LRU_WIDTH), jnp.float32),
        'w_ig': _normal(ks[15], (L, 2, LRU_BLOCKS, LRU_BLOCK_DIM, LRU_BLOCK_DIM), LRU_BLOCK_DIM),
        'b_ig': 0.01 * jax.random.normal(ks[16], (L, 2, LRU_WIDTH), jnp.float32),
        'lru_lambda': lam,
        'w_o_lru': _normal(ks[18], (L, LRU_WIDTH, D_MODEL), LRU_WIDTH, DN_BETA),
        'w_out': _normal(ks[19], (L, D_MODEL, D_MODEL), D_MODEL, DN_BETA),
        'ln1_g': 1.0 + 0.01 * jax.random.normal(ks[20], (L, D_MODEL), jnp.float32),
        'ln1_b': 0.01 * jax.random.normal(ks[21], (L, D_MODEL), jnp.float32),
        'w_up': _normal(ks[22], (L, D_MODEL, 2 * D_FF), D_MODEL),
        'ffn_conv_w': _normal(ks[23], (L, FFN_CONV, 2 * D_FF), FFN_CONV),
        'ffn_conv_b': 0.01 * jax.random.normal(ks[24], (L, 2 * D_FF), jnp.float32),
        'w_down': _normal(ks[25], (L, D_FF, D_MODEL), D_FF, DN_BETA),
        'ln2_g': 1.0 + 0.01 * jax.random.normal(ks[26], (L, D_MODEL), jnp.float32),
        'ln2_b': 0.01 * jax.random.normal(ks[27], (L, D_MODEL), jnp.float32),
    }


def _fwd_reference(x, meta_tokens, ln0_g, ln0_b, w_in, q_norm, kv_norm, w_uq, w_uk, w_uv, w_o_mla,
              lru_conv_w, lru_conv_b, w_rg, b_rg, w_ig, b_ig, lru_lambda, w_o_lru, w_out,
              ln1_g, ln1_b, w_up, ffn_conv_w, ffn_conv_b, w_down, ln2_g, ln2_b):
    bsz = x.shape[0]
    meta = jnp.broadcast_to(meta_tokens[None].astype(x.dtype), (bsz, N_META, x.shape[-1]))
    h = layer_norm(jnp.concatenate([meta, x], axis=1), ln0_g, ln0_b)
    t = h.shape[1]
    half = QK_ROPE // 2
    inv_freq = jnp.exp(-math.log(ROPE_THETA) * jnp.arange(half, dtype=jnp.float32) / half)
    ang = jnp.arange(t, dtype=jnp.float32)[:, None] * inv_freq[None, :]
    cos = jnp.cos(ang).astype(h.dtype)
    sin = jnp.sin(ang).astype(h.dtype)

    for l in range(DEPTH):
        proj = h @ w_in[l]
        cq, ckv, kr, lru_g, lru_x, g_mla, g_lru = jnp.split(proj, IN_SPLITS, axis=-1)
        y_mla = mla_branch(cq, ckv, kr, q_norm[l], kv_norm[l], w_uq[l], w_uk[l], w_uv[l], cos, sin) @ w_o_mla[l]
        xc = dwconv(lru_x, lru_conv_w[l], lru_conv_b[l], LRU_CONV // 2)
        y_lru = (jax.nn.gelu(lru_g) * rg_lru(xc, w_rg[l], b_rg[l], w_ig[l], b_ig[l], lru_lambda[l])) @ w_o_lru[l]
        z = jax.nn.sigmoid(g_mla) * y_mla + jax.nn.sigmoid(g_lru) * y_lru
        h = layer_norm(DN_ALPHA * h + z @ w_out[l], ln1_g[l], ln1_b[l])
        up = dwconv(h @ w_up[l], ffn_conv_w[l], ffn_conv_b[l], FFN_CONV // 2)
        gate, val = jnp.split(up, [D_FF], axis=-1)
        f = (jax.nn.gelu(gate) * val) @ w_down[l]
        h = layer_norm(DN_ALPHA * h + f, ln2_g[l], ln2_b[l])

    return h[:, N_META:]


import jax as _jax
import jax.numpy as _jnp

TWIN_FORMAT = 'train_step'
FWD_PARAMS = ['x', 'meta_tokens', 'ln0_g', 'ln0_b', 'w_in', 'q_norm', 'kv_norm', 'w_uq', 'w_uk', 'w_uv', 'w_o_mla', 'lru_conv_w', 'lru_conv_b', 'w_rg', 'b_rg', 'w_ig', 'b_ig', 'lru_lambda', 'w_o_lru', 'w_out', 'ln1_g', 'ln1_b', 'w_up', 'ffn_conv_w', 'ffn_conv_b', 'w_down', 'ln2_g', 'ln2_b']
TWIN_WEIGHTS = ['meta_tokens', 'ln0_g', 'ln0_b', 'w_in', 'q_norm', 'kv_norm', 'w_uq', 'w_uk', 'w_uv', 'w_o_mla', 'lru_conv_w', 'lru_conv_b', 'w_rg', 'b_rg', 'w_ig', 'b_ig', 'lru_lambda', 'w_o_lru', 'w_out', 'ln1_g', 'ln1_b', 'w_up', 'ffn_conv_w', 'ffn_conv_b', 'w_down', 'ln2_g', 'ln2_b']
TWIN_DIFF_INPUT = 'x'
TWIN_INPUTS = ['x', 'meta_tokens', 'ln0_g', 'ln0_b', 'w_in', 'q_norm', 'kv_norm', 'w_uq', 'w_uk', 'w_uv', 'w_o_mla', 'lru_conv_w', 'lru_conv_b', 'w_rg', 'b_rg', 'w_ig', 'b_ig', 'lru_lambda', 'w_o_lru', 'w_out', 'ln1_g', 'ln1_b', 'w_up', 'ffn_conv_w', 'ffn_conv_b', 'w_down', 'ln2_g', 'ln2_b', 'loss_target', 'm_meta_tokens', 'm_ln0_g', 'm_ln0_b', 'm_w_in', 'm_q_norm', 'm_kv_norm', 'm_w_uq', 'm_w_uk', 'm_w_uv', 'm_w_o_mla', 'm_lru_conv_w', 'm_lru_conv_b', 'm_w_rg', 'm_b_rg', 'm_w_ig', 'm_b_ig', 'm_lru_lambda', 'm_w_o_lru', 'm_w_out', 'm_ln1_g', 'm_ln1_b', 'm_w_up', 'm_ffn_conv_w', 'm_ffn_conv_b', 'm_w_down', 'm_ln2_g', 'm_ln2_b', 'v_meta_tokens', 'v_ln0_g', 'v_ln0_b', 'v_w_in', 'v_q_norm', 'v_kv_norm', 'v_w_uq', 'v_w_uk', 'v_w_uv', 'v_w_o_mla', 'v_lru_conv_w', 'v_lru_conv_b', 'v_w_rg', 'v_b_rg', 'v_w_ig', 'v_b_ig', 'v_lru_lambda', 'v_w_o_lru', 'v_w_out', 'v_ln1_g', 'v_ln1_b', 'v_w_up', 'v_ffn_conv_w', 'v_ffn_conv_b', 'v_w_down', 'v_ln2_g', 'v_ln2_b']
TWIN_OUTPUTS = ['loss', 'grad_x', 'grad_meta_tokens', 'grad_ln0_g', 'grad_ln0_b', 'grad_w_in', 'grad_q_norm', 'grad_kv_norm', 'grad_w_uq', 'grad_w_uk', 'grad_w_uv', 'grad_w_o_mla', 'grad_lru_conv_w', 'grad_lru_conv_b', 'grad_w_rg', 'grad_b_rg', 'grad_w_ig', 'grad_b_ig', 'grad_lru_lambda', 'grad_w_o_lru', 'grad_w_out', 'grad_ln1_g', 'grad_ln1_b', 'grad_w_up', 'grad_ffn_conv_w', 'grad_ffn_conv_b', 'grad_w_down', 'grad_ln2_g', 'grad_ln2_b', 'delta_meta_tokens', 'delta_ln0_g', 'delta_ln0_b', 'delta_w_in', 'delta_q_norm', 'delta_kv_norm', 'delta_w_uq', 'delta_w_uk', 'delta_w_uv', 'delta_w_o_mla', 'delta_lru_conv_w', 'delta_lru_conv_b', 'delta_w_rg', 'delta_b_rg', 'delta_w_ig', 'delta_b_ig', 'delta_lru_lambda', 'delta_w_o_lru', 'delta_w_out', 'delta_ln1_g', 'delta_ln1_b', 'delta_w_up', 'delta_ffn_conv_w', 'delta_ffn_conv_b', 'delta_w_down', 'delta_ln2_g', 'delta_ln2_b', 'new_m_meta_tokens', 'new_m_ln0_g', 'new_m_ln0_b', 'new_m_w_in', 'new_m_q_norm', 'new_m_kv_norm', 'new_m_w_uq', 'new_m_w_uk', 'new_m_w_uv', 'new_m_w_o_mla', 'new_m_lru_conv_w', 'new_m_lru_conv_b', 'new_m_w_rg', 'new_m_b_rg', 'new_m_w_ig', 'new_m_b_ig', 'new_m_lru_lambda', 'new_m_w_o_lru', 'new_m_w_out', 'new_m_ln1_g', 'new_m_ln1_b', 'new_m_w_up', 'new_m_ffn_conv_w', 'new_m_ffn_conv_b', 'new_m_w_down', 'new_m_ln2_g', 'new_m_ln2_b', 'new_v_meta_tokens', 'new_v_ln0_g', 'new_v_ln0_b', 'new_v_w_in', 'new_v_q_norm', 'new_v_kv_norm', 'new_v_w_uq', 'new_v_w_uk', 'new_v_w_uv', 'new_v_w_o_mla', 'new_v_lru_conv_w', 'new_v_lru_conv_b', 'new_v_w_rg', 'new_v_b_rg', 'new_v_w_ig', 'new_v_b_ig', 'new_v_lru_lambda', 'new_v_w_o_lru', 'new_v_w_out', 'new_v_ln1_g', 'new_v_ln1_b', 'new_v_w_up', 'new_v_ffn_conv_w', 'new_v_ffn_conv_b', 'new_v_w_down', 'new_v_ln2_g', 'new_v_ln2_b']
TWIN_LEAF_KINDS = {'loss': 'loss', 'grad_x': 'grad_x', 'grad_meta_tokens': 'grad_w', 'grad_ln0_g': 'grad_w', 'grad_ln0_b': 'grad_w', 'grad_w_in': 'grad_w', 'grad_q_norm': 'grad_w', 'grad_kv_norm': 'grad_w', 'grad_w_uq': 'grad_w', 'grad_w_uk': 'grad_w', 'grad_w_uv': 'grad_w', 'grad_w_o_mla': 'grad_w', 'grad_lru_conv_w': 'grad_w', 'grad_lru_conv_b': 'grad_w', 'grad_w_rg': 'grad_w', 'grad_b_rg': 'grad_w', 'grad_w_ig': 'grad_w', 'grad_b_ig': 'grad_w', 'grad_lru_lambda': 'grad_w', 'grad_w_o_lru': 'grad_w', 'grad_w_out': 'grad_w', 'grad_ln1_g': 'grad_w', 'grad_ln1_b': 'grad_w', 'grad_w_up': 'grad_w', 'grad_ffn_conv_w': 'grad_w', 'grad_ffn_conv_b': 'grad_w', 'grad_w_down': 'grad_w', 'grad_ln2_g': 'grad_w', 'grad_ln2_b': 'grad_w', 'delta_meta_tokens': 'delta_w', 'delta_ln0_g': 'delta_w', 'delta_ln0_b': 'delta_w', 'delta_w_in': 'delta_w', 'delta_q_norm': 'delta_w', 'delta_kv_norm': 'delta_w', 'delta_w_uq': 'delta_w', 'delta_w_uk': 'delta_w', 'delta_w_uv': 'delta_w', 'delta_w_o_mla': 'delta_w', 'delta_lru_conv_w': 'delta_w', 'delta_lru_conv_b': 'delta_w', 'delta_w_rg': 'delta_w', 'delta_b_rg': 'delta_w', 'delta_w_ig': 'delta_w', 'delta_b_ig': 'delta_w', 'delta_lru_lambda': 'delta_w', 'delta_w_o_lru': 'delta_w', 'delta_w_out': 'delta_w', 'delta_ln1_g': 'delta_w', 'delta_ln1_b': 'delta_w', 'delta_w_up': 'delta_w', 'delta_ffn_conv_w': 'delta_w', 'delta_ffn_conv_b': 'delta_w', 'delta_w_down': 'delta_w', 'delta_ln2_g': 'delta_w', 'delta_ln2_b': 'delta_w', 'new_m_meta_tokens': 'new_m', 'new_m_ln0_g': 'new_m', 'new_m_ln0_b': 'new_m', 'new_m_w_in': 'new_m', 'new_m_q_norm': 'new_m', 'new_m_kv_norm': 'new_m', 'new_m_w_uq': 'new_m', 'new_m_w_uk': 'new_m', 'new_m_w_uv': 'new_m', 'new_m_w_o_mla': 'new_m', 'new_m_lru_conv_w': 'new_m', 'new_m_lru_conv_b': 'new_m', 'new_m_w_rg': 'new_m', 'new_m_b_rg': 'new_m', 'new_m_w_ig': 'new_m', 'new_m_b_ig': 'new_m', 'new_m_lru_lambda': 'new_m', 'new_m_w_o_lru': 'new_m', 'new_m_w_out': 'new_m', 'new_m_ln1_g': 'new_m', 'new_m_ln1_b': 'new_m', 'new_m_w_up': 'new_m', 'new_m_ffn_conv_w': 'new_m', 'new_m_ffn_conv_b': 'new_m', 'new_m_w_down': 'new_m', 'new_m_ln2_g': 'new_m', 'new_m_ln2_b': 'new_m', 'new_v_meta_tokens': 'new_v', 'new_v_ln0_g': 'new_v', 'new_v_ln0_b': 'new_v', 'new_v_w_in': 'new_v', 'new_v_q_norm': 'new_v', 'new_v_kv_norm': 'new_v', 'new_v_w_uq': 'new_v', 'new_v_w_uk': 'new_v', 'new_v_w_uv': 'new_v', 'new_v_w_o_mla': 'new_v', 'new_v_lru_conv_w': 'new_v', 'new_v_lru_conv_b': 'new_v', 'new_v_w_rg': 'new_v', 'new_v_b_rg': 'new_v', 'new_v_w_ig': 'new_v', 'new_v_b_ig': 'new_v', 'new_v_lru_lambda': 'new_v', 'new_v_w_o_lru': 'new_v', 'new_v_w_out': 'new_v', 'new_v_ln1_g': 'new_v', 'new_v_ln1_b': 'new_v', 'new_v_w_up': 'new_v', 'new_v_ffn_conv_w': 'new_v', 'new_v_ffn_conv_b': 'new_v', 'new_v_w_down': 'new_v', 'new_v_ln2_g': 'new_v', 'new_v_ln2_b': 'new_v'}


def _forward(args):
    return _fwd_reference(*[args[k] for k in FWD_PARAMS])


def _output_shape():
    out = _jax.eval_shape(lambda: _forward(_fwd_setup_inputs(0)))
    return out.shape, out.dtype

N_MICROBATCH = 1
ADAM_LR = 0.001
ADAM_B1 = 0.9
ADAM_B2 = 0.999
ADAM_EPS = 1e-08
ADAM_WD = 0.01
ADAM_STEP = 10
PER_EXAMPLE_BATCH_AXIS = {'x': 0, 'loss_target': 0}
SHARED_INPUTS = []
_WEIGHT_DTYPES = {'meta_tokens': _jnp.float32, 'ln0_g': _jnp.float32, 'ln0_b': _jnp.float32, 'w_in': _jnp.float32, 'q_norm': _jnp.float32, 'kv_norm': _jnp.float32, 'w_uq': _jnp.float32, 'w_uk': _jnp.float32, 'w_uv': _jnp.float32, 'w_o_mla': _jnp.float32, 'lru_conv_w': _jnp.float32, 'lru_conv_b': _jnp.float32, 'w_rg': _jnp.float32, 'b_rg': _jnp.float32, 'w_ig': _jnp.float32, 'b_ig': _jnp.float32, 'lru_lambda': _jnp.float32, 'w_o_lru': _jnp.float32, 'w_out': _jnp.float32, 'ln1_g': _jnp.float32, 'ln1_b': _jnp.float32, 'w_up': _jnp.float32, 'ffn_conv_w': _jnp.float32, 'ffn_conv_b': _jnp.float32, 'w_down': _jnp.float32, 'ln2_g': _jnp.float32, 'ln2_b': _jnp.float32}
MOMENT_SCALE = {'meta_tokens': 9.425679e-04, 'ln0_g': 3.913225e-01, 'ln0_b': 3.589142e-01, 'w_in': 8.032673e-03, 'q_norm': 2.105703e-03, 'kv_norm': 4.948038e-03, 'w_uq': 8.759003e-04, 'w_uk': 9.520384e-04, 'w_uv': 2.220354e-03, 'w_o_mla': 2.218195e-03, 'lru_conv_w': 1.179167e-02, 'lru_conv_b': 1.724958e-01, 'w_rg': 2.614918e-03, 'b_rg': 2.156890e-03, 'w_ig': 4.710494e-03, 'b_ig': 2.975733e-03, 'lru_lambda': 4.209236e-03, 'w_o_lru': 2.417678e-02, 'w_out': 2.380656e-02, 'ln1_g': 4.084223e-01, 'ln1_b': 2.823511e-01, 'w_up': 2.515773e-02, 'ffn_conv_w': 2.539506e-02, 'ffn_conv_b': 2.561182e-02, 'w_down': 8.227967e-02, 'ln2_g': 2.261605e+01, 'ln2_b': 4.876049e-01}


def _to_microbatches(a, axis):
    t = _jnp.moveaxis(a, axis, 0)
    t = t.reshape((N_MICROBATCH, t.shape[0] // N_MICROBATCH) + t.shape[1:])
    return _jnp.moveaxis(t, 1, axis + 1)


def setup_inputs(seed: int = 0) -> dict:
    inp = _fwd_setup_inputs(seed)
    key = _jax.random.fold_in(_jax.random.key(seed), 7919)
    shape, _ = _output_shape()
    out = dict(inp)
    out["loss_target"] = _jax.random.normal(_jax.random.fold_in(key, 0), shape, _jnp.float32)
    for i, name in enumerate(TWIN_WEIGHTS):
        w = inp[name].astype(_jnp.float32)
        if MOMENT_SCALE is None:
            s = _jnp.sqrt(_jnp.mean(_jnp.square(w)) + 1e-30)
        else:
            s = MOMENT_SCALE[name]
        km, kv = _jax.random.split(_jax.random.fold_in(key, i + 1))
        out[name] = w
        out["m_" + name] = s * _jax.random.normal(km, w.shape, _jnp.float32)
        out["v_" + name] = (s * s) * _jax.random.uniform(kv, w.shape, _jnp.float32, 0.5, 1.5)
    if N_MICROBATCH > 1:
        for name, axis in PER_EXAMPLE_BATCH_AXIS.items():
            out[name] = _to_microbatches(out[name], axis)
    return {'x': out['x'], 'meta_tokens': out['meta_tokens'], 'ln0_g': out['ln0_g'], 'ln0_b': out['ln0_b'], 'w_in': out['w_in'], 'q_norm': out['q_norm'], 'kv_norm': out['kv_norm'], 'w_uq': out['w_uq'], 'w_uk': out['w_uk'], 'w_uv': out['w_uv'], 'w_o_mla': out['w_o_mla'], 'lru_conv_w': out['lru_conv_w'], 'lru_conv_b': out['lru_conv_b'], 'w_rg': out['w_rg'], 'b_rg': out['b_rg'], 'w_ig': out['w_ig'], 'b_ig': out['b_ig'], 'lru_lambda': out['lru_lambda'], 'w_o_lru': out['w_o_lru'], 'w_out': out['w_out'], 'ln1_g': out['ln1_g'], 'ln1_b': out['ln1_b'], 'w_up': out['w_up'], 'ffn_conv_w': out['ffn_conv_w'], 'ffn_conv_b': out['ffn_conv_b'], 'w_down': out['w_down'], 'ln2_g': out['ln2_g'], 'ln2_b': out['ln2_b'], 'loss_target': out['loss_target'], 'm_meta_tokens': out['m_meta_tokens'], 'm_ln0_g': out['m_ln0_g'], 'm_ln0_b': out['m_ln0_b'], 'm_w_in': out['m_w_in'], 'm_q_norm': out['m_q_norm'], 'm_kv_norm': out['m_kv_norm'], 'm_w_uq': out['m_w_uq'], 'm_w_uk': out['m_w_uk'], 'm_w_uv': out['m_w_uv'], 'm_w_o_mla': out['m_w_o_mla'], 'm_lru_conv_w': out['m_lru_conv_w'], 'm_lru_conv_b': out['m_lru_conv_b'], 'm_w_rg': out['m_w_rg'], 'm_b_rg': out['m_b_rg'], 'm_w_ig': out['m_w_ig'], 'm_b_ig': out['m_b_ig'], 'm_lru_lambda': out['m_lru_lambda'], 'm_w_o_lru': out['m_w_o_lru'], 'm_w_out': out['m_w_out'], 'm_ln1_g': out['m_ln1_g'], 'm_ln1_b': out['m_ln1_b'], 'm_w_up': out['m_w_up'], 'm_ffn_conv_w': out['m_ffn_conv_w'], 'm_ffn_conv_b': out['m_ffn_conv_b'], 'm_w_down': out['m_w_down'], 'm_ln2_g': out['m_ln2_g'], 'm_ln2_b': out['m_ln2_b'], 'v_meta_tokens': out['v_meta_tokens'], 'v_ln0_g': out['v_ln0_g'], 'v_ln0_b': out['v_ln0_b'], 'v_w_in': out['v_w_in'], 'v_q_norm': out['v_q_norm'], 'v_kv_norm': out['v_kv_norm'], 'v_w_uq': out['v_w_uq'], 'v_w_uk': out['v_w_uk'], 'v_w_uv': out['v_w_uv'], 'v_w_o_mla': out['v_w_o_mla'], 'v_lru_conv_w': out['v_lru_conv_w'], 'v_lru_conv_b': out['v_lru_conv_b'], 'v_w_rg': out['v_w_rg'], 'v_b_rg': out['v_b_rg'], 'v_w_ig': out['v_w_ig'], 'v_b_ig': out['v_b_ig'], 'v_lru_lambda': out['v_lru_lambda'], 'v_w_o_lru': out['v_w_o_lru'], 'v_w_out': out['v_w_out'], 'v_ln1_g': out['v_ln1_g'], 'v_ln1_b': out['v_ln1_b'], 'v_w_up': out['v_w_up'], 'v_ffn_conv_w': out['v_ffn_conv_w'], 'v_ffn_conv_b': out['v_ffn_conv_b'], 'v_w_down': out['v_w_down'], 'v_ln2_g': out['v_ln2_g'], 'v_ln2_b': out['v_ln2_b']}


def _loss(weights, diff, rest, loss_target):
    with _jax.named_scope("forward"):
        args = {**rest, TWIN_DIFF_INPUT: diff, **{k: w.astype(_WEIGHT_DTYPES[k]) for k, w in weights.items()}}
        y = _forward(args)
    with _jax.named_scope("loss_head"):
        err = _jnp.square(y.astype(_jnp.float32) - loss_target)
        return 0.5 * _jnp.sum(_jnp.mean(err, axis=-1)) if err.ndim else 0.5 * err


def _adamw(w, g, m, v):
    m = ADAM_B1 * m + (1.0 - ADAM_B1) * g
    v = ADAM_B2 * v + (1.0 - ADAM_B2) * _jnp.square(g)
    m_hat = m / (1.0 - ADAM_B1 ** ADAM_STEP)
    v_hat = v / (1.0 - ADAM_B2 ** ADAM_STEP)
    delta = -ADAM_LR * (m_hat / (_jnp.sqrt(v_hat) + ADAM_EPS) + ADAM_WD * w)
    return delta, m, v


def reference(x, meta_tokens, ln0_g, ln0_b, w_in, q_norm, kv_norm, w_uq, w_uk, w_uv, w_o_mla, lru_conv_w, lru_conv_b, w_rg, b_rg, w_ig, b_ig, lru_lambda, w_o_lru, w_out, ln1_g, ln1_b, w_up, ffn_conv_w, ffn_conv_b, w_down, ln2_g, ln2_b, loss_target, m_meta_tokens, m_ln0_g, m_ln0_b, m_w_in, m_q_norm, m_kv_norm, m_w_uq, m_w_uk, m_w_uv, m_w_o_mla, m_lru_conv_w, m_lru_conv_b, m_w_rg, m_b_rg, m_w_ig, m_b_ig, m_lru_lambda, m_w_o_lru, m_w_out, m_ln1_g, m_ln1_b, m_w_up, m_ffn_conv_w, m_ffn_conv_b, m_w_down, m_ln2_g, m_ln2_b, v_meta_tokens, v_ln0_g, v_ln0_b, v_w_in, v_q_norm, v_kv_norm, v_w_uq, v_w_uk, v_w_uv, v_w_o_mla, v_lru_conv_w, v_lru_conv_b, v_w_rg, v_b_rg, v_w_ig, v_b_ig, v_lru_lambda, v_w_o_lru, v_w_out, v_ln1_g, v_ln1_b, v_w_up, v_ffn_conv_w, v_ffn_conv_b, v_w_down, v_ln2_g, v_ln2_b):
    given = dict(x=x, meta_tokens=meta_tokens, ln0_g=ln0_g, ln0_b=ln0_b, w_in=w_in, q_norm=q_norm, kv_norm=kv_norm, w_uq=w_uq, w_uk=w_uk, w_uv=w_uv, w_o_mla=w_o_mla, lru_conv_w=lru_conv_w, lru_conv_b=lru_conv_b, w_rg=w_rg, b_rg=b_rg, w_ig=w_ig, b_ig=b_ig, lru_lambda=lru_lambda, w_o_lru=w_o_lru, w_out=w_out, ln1_g=ln1_g, ln1_b=ln1_b, w_up=w_up, ffn_conv_w=ffn_conv_w, ffn_conv_b=ffn_conv_b, w_down=w_down, ln2_g=ln2_g, ln2_b=ln2_b, loss_target=loss_target, m_meta_tokens=m_meta_tokens, m_ln0_g=m_ln0_g, m_ln0_b=m_ln0_b, m_w_in=m_w_in, m_q_norm=m_q_norm, m_kv_norm=m_kv_norm, m_w_uq=m_w_uq, m_w_uk=m_w_uk, m_w_uv=m_w_uv, m_w_o_mla=m_w_o_mla, m_lru_conv_w=m_lru_conv_w, m_lru_conv_b=m_lru_conv_b, m_w_rg=m_w_rg, m_b_rg=m_b_rg, m_w_ig=m_w_ig, m_b_ig=m_b_ig, m_lru_lambda=m_lru_lambda, m_w_o_lru=m_w_o_lru, m_w_out=m_w_out, m_ln1_g=m_ln1_g, m_ln1_b=m_ln1_b, m_w_up=m_w_up, m_ffn_conv_w=m_ffn_conv_w, m_ffn_conv_b=m_ffn_conv_b, m_w_down=m_w_down, m_ln2_g=m_ln2_g, m_ln2_b=m_ln2_b, v_meta_tokens=v_meta_tokens, v_ln0_g=v_ln0_g, v_ln0_b=v_ln0_b, v_w_in=v_w_in, v_q_norm=v_q_norm, v_kv_norm=v_kv_norm, v_w_uq=v_w_uq, v_w_uk=v_w_uk, v_w_uv=v_w_uv, v_w_o_mla=v_w_o_mla, v_lru_conv_w=v_lru_conv_w, v_lru_conv_b=v_lru_conv_b, v_w_rg=v_w_rg, v_b_rg=v_b_rg, v_w_ig=v_w_ig, v_b_ig=v_b_ig, v_lru_lambda=v_lru_lambda, v_w_o_lru=v_w_o_lru, v_w_out=v_w_out, v_ln1_g=v_ln1_g, v_ln1_b=v_ln1_b, v_w_up=v_w_up, v_ffn_conv_w=v_ffn_conv_w, v_ffn_conv_b=v_ffn_conv_b, v_w_down=v_w_down, v_ln2_g=v_ln2_g, v_ln2_b=v_ln2_b)
    weights = {n: given[n] for n in TWIN_WEIGHTS}
    shared = {n: given[n] for n in SHARED_INPUTS}
    per_example = {n: given[n] for n in ['x']}
    grad_fn = _jax.value_and_grad(_loss, argnums=(0, 1))

    def one_microbatch(ex, loss_target):
        ex = dict(ex)
        diff = ex.pop(TWIN_DIFF_INPUT)
        return grad_fn(weights, diff, {**shared, **ex}, loss_target)

    if N_MICROBATCH == 1:
        loss, (grad_w, grad_x) = one_microbatch(per_example, given["loss_target"])
    else:
        def body(carry, xs):
            loss_sum, grad_sum = carry
            l_k, (gw_k, gx_k) = one_microbatch(xs[0], xs[1])
            with _jax.named_scope("update"):
                return (loss_sum + l_k, _jax.tree.map(_jnp.add, grad_sum, gw_k)), gx_k

        init = (_jnp.zeros((), _jnp.float32), _jax.tree.map(_jnp.zeros_like, weights))
        (loss, grad_w), grad_x = _jax.lax.scan(body, init, (per_example, given["loss_target"]))
    with _jax.named_scope("update"):
        delta_w, new_m, new_v = {}, {}, {}
        for n in TWIN_WEIGHTS:
            delta_w[n], new_m[n], new_v[n] = _adamw(weights[n], grad_w[n], given["m_" + n], given["v_" + n])
    return (loss, grad_x, *[grad_w[n] for n in TWIN_WEIGHTS], *[delta_w[n] for n in TWIN_WEIGHTS],
            *[new_m[n] for n in TWIN_WEIGHTS], *[new_v[n] for n in TWIN_WEIGHTS])
```

```python
import functools
import math

import jax
import jax.numpy as jnp
from jax import lax
from jax.experimental import pallas as pl
from jax.experimental.pallas import tpu as pltpu

F32 = jnp.float32
BF16 = jnp.bfloat16

N_DEV = 8
D_MODEL = 1024
N_META = 16
HEADS = 8
QK_NOPE = 128
QK_ROPE = 64
V_HEAD = 128
Q_RANK = 256
KV_RANK = 128
ROPE_THETA = 10000.0
LRU_BLOCKS = 8
LRU_C = 8.0
D_FF = 2816
DEPTH = 2
DN_ALPHA = (2.0 * DEPTH) ** 0.25
LN_EPS = 1e-5
RMS_EPS = 1e-6
ATT_SCALE = 1.0 / math.sqrt(QK_NOPE + QK_ROPE)
NEG_BIG = -1e30

ADAM_LR = 0.001
ADAM_B1 = 0.9
ADAM_B2 = 0.999
ADAM_EPS = 1e-08
ADAM_WD = 0.01
ADAM_STEP = 10

LANE = 128
SUBLANE = 8
VMEM_LIMIT = 56 * 1024 * 1024

PROJ_COLS = 4 * D_MODEL + Q_RANK + KV_RANK + 2 * QK_ROPE
C_LRU_G, C_LRU_X, C_G_MLA, C_G_LRU = 0, D_MODEL, 2 * D_MODEL, 3 * D_MODEL
C_CQ = 4 * D_MODEL
C_CKV = C_CQ + Q_RANK
C_KRP = C_CKV + KV_RANK

WEIGHTS = ['meta_tokens', 'ln0_g', 'ln0_b', 'w_in', 'q_norm', 'kv_norm', 'w_uq', 'w_uk', 'w_uv', 'w_o_mla',
           'lru_conv_w', 'lru_conv_b', 'w_rg', 'b_rg', 'w_ig', 'b_ig', 'lru_lambda', 'w_o_lru', 'w_out',
           'ln1_g', 'ln1_b', 'w_up', 'ffn_conv_w', 'ffn_conv_b', 'w_down', 'ln2_g', 'ln2_b']
SHARD_AXIS = {'meta_tokens': 1, 'w_in': 2, 'w_uq': 1, 'w_o_mla': 1, 'lru_conv_w': 2, 'b_rg': 2, 'b_ig': 2,
              'lru_lambda': 2, 'w_o_lru': 1, 'w_out': 1, 'w_up': 2, 'ffn_conv_w': 2, 'w_down': 1}
BIG = ['w_in', 'w_uq', 'w_o_mla', 'w_o_lru', 'w_out', 'w_up', 'w_down']
SMALL = ['meta_tokens', 'lru_conv_w', 'b_rg', 'b_ig', 'lru_lambda', 'ffn_conv_w']
SHARDED = BIG + SMALL
REPLICATED = [n for n in WEIGHTS if n not in SHARD_AXIS]


def _cparams(*sem):
    return pltpu.CompilerParams(dimension_semantics=sem, vmem_limit_bytes=VMEM_LIMIT)


def _tile(n, cap, unit=LANE):
    best = None
    t = unit
    while t <= min(n, cap):
        if n % t == 0:
            best = t
        t += unit
    return n if best is None else best


def _sigmoid(x):
    return 1.0 / (1.0 + jnp.exp(-x))


_GELU_C = math.sqrt(2.0 / math.pi)


def _gelu(x):
    t = jnp.tanh(_GELU_C * (x + 0.044715 * x * x * x))
    return 0.5 * x * (1.0 + t)


def _gelu_and_grad(x):
    t = jnp.tanh(_GELU_C * (x + 0.044715 * x * x * x))
    g = 0.5 * x * (1.0 + t)
    dg = 0.5 * (1.0 + t) + 0.5 * x * (1.0 - t * t) * _GELU_C * (1.0 + 3.0 * 0.044715 * x * x)
    return g, dg


def _softplus_neg(lam):
    z = jnp.exp(-jnp.abs(lam))
    w = 1.0 + z
    log1p = jnp.where(w == 1.0, z, jnp.log(w) * z / (w - 1.0))
    return jnp.maximum(-lam, 0.0) + log1p


def _row_ids(shape, row0=0):
    return lax.broadcasted_iota(jnp.int32, shape, 0) + row0


def _matmul(a, b, name, ta=False, tb=False, out_dtype=F32, tm_cap=1408, tn_cap=1024, tk_cap=2048):
    if ta:
        kdim, m = a.shape
    else:
        m, kdim = a.shape
    if tb:
        n, k2 = b.shape
    else:
        k2, n = b.shape
    assert kdim == k2, (a.shape, b.shape, ta, tb)
    tm, tn, tk = _tile(m, tm_cap), _tile(n, tn_cap), _tile(kdim, tk_cap)
    nk = kdim // tk

    def body(a_ref, b_ref, o_ref, acc_ref):
        k = pl.program_id(2)

        @pl.when(k == 0)
        def _():
            acc_ref[...] = jnp.zeros_like(acc_ref)

        dn = (((0 if ta else 1,), (1 if tb else 0,)), ((), ()))
        acc_ref[...] += lax.dot_general(a_ref[...].astype(BF16), b_ref[...].astype(BF16), dn,
                                        preferred_element_type=F32)

        @pl.when(k == nk - 1)
        def _():
            o_ref[...] = acc_ref[...].astype(o_ref.dtype)

    a_spec = pl.BlockSpec((tk, tm), lambda i, j, k: (k, i)) if ta else pl.BlockSpec((tm, tk), lambda i, j, k: (i, k))
    b_spec = pl.BlockSpec((tn, tk), lambda i, j, k: (j, k)) if tb else pl.BlockSpec((tk, tn), lambda i, j, k: (k, j))
    return pl.pallas_call(
        body, name=name,
        grid=(m // tm, n // tn, nk),
        in_specs=[a_spec, b_spec],
        out_specs=pl.BlockSpec((tm, tn), lambda i, j, k: (i, j)),
        out_shape=jax.ShapeDtypeStruct((m, n), out_dtype),
        scratch_shapes=[pltpu.VMEM((tm, tn), F32)],
        compiler_params=_cparams("parallel", "parallel", "arbitrary"),
    )(a, b)


class Rw:
    def __init__(self, arr, width=None, cb=0):
        self.arr, self.width, self.cb = arr, (arr.shape[1] if width is None else width), cb


class Pm:
    def __init__(self, arr):
        self.arr = arr


def _rows(fn, name, ins, outs, accs=(), tm_cap=384):
    tp = next(o.arr.shape[0] for o in ins if isinstance(o, Rw))
    tm = _tile(tp, tm_cap)
    n_in, n_out, n_acc = len(ins), len(outs), len(accs)

    def body(*refs):
        i = pl.program_id(0)
        res = fn(i * tm, *[r[...] for r in refs[:n_in]])
        if not isinstance(res, (tuple, list)):
            res = (res,)
        assert len(res) == n_out + n_acc, (name, len(res))
        for k in range(n_out):
            refs[n_in + k][...] = res[k].astype(refs[n_in + k].dtype)
        for k in range(n_acc):
            ref = refs[n_in + n_out + k]

            @pl.when(i == 0)
            def _():
                ref[...] = jnp.zeros_like(ref)

            ref[...] += res[n_out + k]

    in_specs = []
    for o in ins:
        if isinstance(o, Rw):
            in_specs.append(pl.BlockSpec((tm, o.width), functools.partial(lambda i, cb: (i, cb), cb=o.cb)))
        else:
            in_specs.append(pl.BlockSpec(o.arr.shape, functools.partial(lambda i, nd: (0,) * nd, nd=o.arr.ndim)))
    out_specs = [pl.BlockSpec((tm, w), lambda i: (i, 0)) for (w, _) in outs]
    out_specs += [pl.BlockSpec(s, functools.partial(lambda i, nd: (0,) * nd, nd=len(s))) for s in accs]
    out_shape = [jax.ShapeDtypeStruct((tp, w), dt) for (w, dt) in outs]
    out_shape += [jax.ShapeDtypeStruct(s, F32) for s in accs]
    res = pl.pallas_call(
        body, name=name, grid=(tp // tm,), in_specs=in_specs, out_specs=out_specs, out_shape=out_shape,
        compiler_params=_cparams("arbitrary"),
    )(*[o.arr for o in ins])
    return res


class Cl:
    def __init__(self, arr, col0=0):
        self.arr, self.col0 = arr, col0


def _cols(fn, name, ins, outs, ncols, tc):
    assert ncols % tc == 0
    n_in, n_out = len(ins), len(outs)

    def body(*refs):
        res = fn(*[r[...] for r in refs[:n_in]])
        if not isinstance(res, (tuple, list)):
            res = (res,)
        assert len(res) == n_out, (name, len(res))
        for k in range(n_out):
            refs[n_in + k][...] = res[k].astype(refs[n_in + k].dtype)

    in_specs = []
    for o in ins:
        assert o.col0 % tc == 0, (name, o.col0, tc)
        in_specs.append(pl.BlockSpec((o.arr.shape[0], tc), functools.partial(lambda j, off: (0, j + off), off=o.col0 // tc)))
    out_specs = [pl.BlockSpec((r, tc), lambda j: (0, j)) for (r, _) in outs]
    out_shape = [jax.ShapeDtypeStruct((r, ncols), dt) for (r, dt) in outs]
    return pl.pallas_call(
        body, name=name, grid=(ncols // tc,), in_specs=in_specs, out_specs=out_specs, out_shape=out_shape,
        compiler_params=_cparams("parallel"),
    )(*[o.arr for o in ins])


def _ln_stats(u):
    mu = jnp.mean(u, axis=-1, keepdims=True)
    xc = u - mu
    var = jnp.mean(xc * xc, axis=-1, keepdims=True)
    rstd = lax.rsqrt(var + LN_EPS)
    return xc * rstd, rstd


def _ln_fwd(terms, g, b, name):
    coefs = [c for c, _ in terms]

    def fn(row0, *blk):
        xs, (gg, bb) = blk[:len(coefs)], blk[len(coefs):]
        u = sum(c * x for c, x in zip(coefs, xs))
        xhat, _ = _ln_stats(u)
        return xhat * gg + bb

    d = terms[0][1].shape[1]
    return _rows(fn, name, [Rw(x) for _, x in terms] + [Pm(g.reshape(1, d)), Pm(b.reshape(1, d))], [(d, F32)])[0]


def _ln_bwd(dy_terms, u_terms, g, name):
    dc = [c for c, _ in dy_terms]
    uc = [c for c, _ in u_terms]
    d = u_terms[0][1].shape[1]

    def fn(row0, *blk):
        dys = blk[:len(dc)]
        xs = blk[len(dc):len(dc) + len(uc)]
        gg = blk[-1]
        dy = sum(c * x for c, x in zip(dc, dys))
        u = sum(c * x for c, x in zip(uc, xs))
        xhat, rstd = _ln_stats(u)
        gdy = dy * gg
        m1 = jnp.mean(gdy, axis=-1, keepdims=True)
        m2 = jnp.mean(gdy * xhat, axis=-1, keepdims=True)
        du = rstd * (gdy - m1 - xhat * m2)
        return du, jnp.sum(dy * xhat, axis=0, keepdims=True), jnp.sum(dy, axis=0, keepdims=True)

    ins = [Rw(x) for _, x in dy_terms] + [Rw(x) for _, x in u_terms] + [Pm(g.reshape(1, d))]
    return _rows(fn, name, ins, [(d, F32)], accs=[(1, d), (1, d)])


def _loss_head(y, tgt, t_real, name):
    d = y.shape[1]

    def fn(row0, yb, tb):
        rows = _row_ids(yb.shape, row0)
        live = (rows >= N_META) & (rows < t_real)
        diff = jnp.where(live, yb - tb, 0.0)
        return diff * (1.0 / d), jnp.sum(diff * diff, axis=0, keepdims=True) * (0.5 / d)

    return _rows(fn, name, [Rw(y), Rw(tgt)], [(d, F32)], accs=[(1, d)])


def _rms(x, g):
    r = lax.rsqrt(jnp.mean(x * x, axis=-1, keepdims=True) + RMS_EPS)
    return x * r * g


def _rms_bwd(dy, x, g):
    r = lax.rsqrt(jnp.mean(x * x, axis=-1, keepdims=True) + RMS_EPS)
    gdy = dy * g
    dx = r * gdy - x * (r * r * r) * jnp.mean(gdy * x, axis=-1, keepdims=True)
    return dx, jnp.sum(dy * x * r, axis=0, keepdims=True)


def _mla_norms(proj, qn, kvn, name):
    def fn(row0, cq, ckv, g1, g2):
        return _rms(cq, g1), _rms(ckv, g2)

    return _rows(fn, name, [Rw(proj, Q_RANK, C_CQ // Q_RANK), Rw(proj, KV_RANK, C_CKV // KV_RANK),
                            Pm(qn.reshape(1, Q_RANK)), Pm(kvn.reshape(1, KV_RANK))],
                 [(Q_RANK, F32), (KV_RANK, F32)])


def _mla_norms_bwd(dcqn, dckvn, proj, qn, kvn, name):
    def fn(row0, d1, d2, cq, ckv, g1, g2):
        dx1, dg1 = _rms_bwd(d1, cq, g1)
        dx2, dg2 = _rms_bwd(d2, ckv, g2)
        return dx1, dx2, dg1, dg2

    return _rows(fn, name, [Rw(dcqn), Rw(dckvn), Rw(proj, Q_RANK, C_CQ // Q_RANK), Rw(proj, KV_RANK, C_CKV // KV_RANK),
                            Pm(qn.reshape(1, Q_RANK)), Pm(kvn.reshape(1, KV_RANK))],
                 [(Q_RANK, F32), (KV_RANK, F32)], accs=[(1, Q_RANK), (1, KV_RANK)])


def _fold_rope(z):
    return z + pltpu.roll(z, QK_ROPE, 1)


def _mla_pack(qext, kv, proj, cs, name):
    tp = qext.shape[0]
    tm = _tile(tp, 384)

    def body(q_ref, kn_ref, v_ref, kr_ref, cs_ref, qo_ref, ko_ref, vo_ref):
        cs_ = cs_ref[...]
        low = lax.broadcasted_iota(jnp.int32, cs_.shape, 1) < QK_ROPE
        q = q_ref[...]
        qr = jnp.where(low, _fold_rope(q[:, QK_NOPE:] * cs_), 0.0)
        qo_ref[...] = (jnp.concatenate([q[:, :QK_NOPE], qr], axis=1) * ATT_SCALE).astype(BF16)
        kr = _fold_rope(kr_ref[...] * cs_)
        ko_ref[...] = jnp.concatenate([kn_ref[...], kr], axis=1).astype(BF16)
        vo_ref[...] = v_ref[...].astype(BF16)

    return pl.pallas_call(
        body, name=name, grid=(tp // tm, HEADS),
        in_specs=[pl.BlockSpec((tm, 2 * LANE), lambda i, h: (i, h)),
                  pl.BlockSpec((tm, LANE), lambda i, h: (i, h)),
                  pl.BlockSpec((tm, LANE), lambda i, h: (i, HEADS + h)),
                  pl.BlockSpec((tm, LANE), lambda i, h: (i, C_KRP // LANE)),
                  pl.BlockSpec((tm, LANE), lambda i, h: (i, 0))],
        out_specs=[pl.BlockSpec((tm, 2 * LANE), lambda i, h: (i, h)),
                   pl.BlockSpec((tm, 2 * LANE), lambda i, h: (i, h)),
                   pl.BlockSpec((tm, LANE), lambda i, h: (i, h))],
        out_shape=[jax.ShapeDtypeStruct((tp, HEADS * 2 * LANE), BF16),
                   jax.ShapeDtypeStruct((tp, HEADS * 2 * LANE), BF16),
                   jax.ShapeDtypeStruct((tp, HEADS * LANE), BF16)],
        compiler_params=_cparams("parallel", "arbitrary"),
    )(qext, kv, kv, proj, cs)


def _mla_unpack(dq, dk, cs, name):
    tp = dq.shape[0]
    tm = _tile(tp, 384)

    def body(dq_ref, dk_ref, cs_ref, dqe_ref, dkn_ref, dkr_ref):
        h = pl.program_id(1)
        cs_ = cs_ref[...]
        low = lax.broadcasted_iota(jnp.int32, cs_.shape, 1) < QK_ROPE
        dq_ = dq_ref[...] * ATT_SCALE
        dqr = _fold_rope(jnp.where(low, dq_[:, QK_NOPE:], 0.0)) * cs_
        dqe_ref[...] = jnp.concatenate([dq_[:, :QK_NOPE], dqr], axis=1)
        dk_ = dk_ref[...]
        dkn_ref[...] = dk_[:, :QK_NOPE]

        @pl.when(h == 0)
        def _():
            dkr_ref[...] = jnp.zeros_like(dkr_ref)

        dkr_ref[...] += _fold_rope(jnp.where(low, dk_[:, QK_NOPE:], 0.0)) * cs_

    return pl.pallas_call(
        body, name=name, grid=(tp // tm, HEADS),
        in_specs=[pl.BlockSpec((tm, 2 * LANE), lambda i, h: (i, h)),
                  pl.BlockSpec((tm, 2 * LANE), lambda i, h: (i, h)),
                  pl.BlockSpec((tm, LANE), lambda i, h: (i, 0))],
        out_specs=[pl.BlockSpec((tm, 2 * LANE), lambda i, h: (i, h)),
                   pl.BlockSpec((tm, LANE), lambda i, h: (i, h)),
                   pl.BlockSpec((tm, LANE), lambda i, h: (i, 0))],
        out_shape=[jax.ShapeDtypeStruct((tp, HEADS * 2 * LANE), F32),
                   jax.ShapeDtypeStruct((tp, HEADS * LANE), F32),
                   jax.ShapeDtypeStruct((tp, LANE), F32)],
        compiler_params=_cparams("parallel", "arbitrary"),
    )(dq, dk, cs)


def _attn_fwd(q, k, v, t_real, name):
    tp = q.shape[0]
    tq = _tile(tp, 384)

    def body(q_ref, k_ref, v_ref, o_ref, lse_ref):
        s = lax.dot_general(q_ref[...], k_ref[...], (((1,), (1,)), ((), ())), preferred_element_type=F32)
        cols = lax.broadcasted_iota(jnp.int32, s.shape, 1)
        s = jnp.where(cols < t_real, s, NEG_BIG)
        m = jnp.max(s, axis=-1, keepdims=True)
        p = jnp.exp(s - m)
        l = jnp.sum(p, axis=-1, keepdims=True)
        o = jnp.dot(p.astype(BF16), v_ref[...], preferred_element_type=F32)
        o_ref[...] = o / l
        lse_ref[...] = m + jnp.log(l)

    return pl.pallas_call(
        body, name=name, grid=(HEADS, tp // tq),
        in_specs=[pl.BlockSpec((tq, 2 * LANE), lambda h, i: (i, h)),
                  pl.BlockSpec((tp, 2 * LANE), lambda h, i: (0, h)),
                  pl.BlockSpec((tp, LANE), lambda h, i: (0, h))],
        out_specs=[pl.BlockSpec((tq, LANE), lambda h, i: (i, h)),
                   pl.BlockSpec((None, tq, 1), lambda h, i: (h, i, 0))],
        out_shape=[jax.ShapeDtypeStruct((tp, HEADS * LANE), F32),
                   jax.ShapeDtypeStruct((HEADS, tp, 1), F32)],
        compiler_params=_cparams("parallel", "parallel"),
    )(q, k, v)


def _attn_bwd(q, k, v, do, o, lse, t_real, name):
    tp = q.shape[0]
    tq = _tile(tp, 128)

    def body(q_ref, k_ref, v_ref, do_ref, o_ref, lse_ref, dq_ref, dk_ref, dv_ref):
        i = pl.program_id(1)
        qb, kb = q_ref[...], k_ref[...]
        s = lax.dot_general(qb, kb, (((1,), (1,)), ((), ())), preferred_element_type=F32)
        cols = lax.broadcasted_iota(jnp.int32, s.shape, 1)
        p = jnp.where(cols < t_real, jnp.exp(s - lse_ref[...]), 0.0)
        dob = do_ref[...]
        dob16 = dob.astype(BF16)
        dp = lax.dot_general(dob16, v_ref[...], (((1,), (1,)), ((), ())), preferred_element_type=F32)
        delta = jnp.sum(dob * o_ref[...], axis=-1, keepdims=True)
        ds = (p * (dp - delta)).astype(BF16)
        dq_ref[...] = jnp.dot(ds, kb, preferred_element_type=F32)

        @pl.when(i == 0)
        def _():
            dk_ref[...] = jnp.zeros_like(dk_ref)
            dv_ref[...] = jnp.zeros_like(dv_ref)

        dk_ref[...] += lax.dot_general(ds, qb, (((0,), (0,)), ((), ())), preferred_element_type=F32)
        dv_ref[...] += lax.dot_general(p.astype(BF16), dob16, (((0,), (0,)), ((), ())), preferred_element_type=F32)

    return pl.pallas_call(
        body, name=name, grid=(HEADS, tp // tq),
        in_specs=[pl.BlockSpec((tq, 2 * LANE), lambda h, i: (i, h)),
                  pl.BlockSpec((tp, 2 * LANE), lambda h, i: (0, h)),
                  pl.BlockSpec((tp, LANE), lambda h, i: (0, h)),
                  pl.BlockSpec((tq, LANE), lambda h, i: (i, h)),
                  pl.BlockSpec((tq, LANE), lambda h, i: (i, h)),
                  pl.BlockSpec((None, tq, 1), lambda h, i: (h, i, 0))],
        out_specs=[pl.BlockSpec((tq, 2 * LANE), lambda h, i: (i, h)),
                   pl.BlockSpec((tp, 2 * LANE), lambda h, i: (0, h)),
                   pl.BlockSpec((tp, LANE), lambda h, i: (0, h))],
        out_shape=[jax.ShapeDtypeStruct((tp, HEADS * 2 * LANE), F32),
                   jax.ShapeDtypeStruct((tp, HEADS * 2 * LANE), F32),
                   jax.ShapeDtypeStruct((tp, HEADS * LANE), F32)],
        compiler_params=_cparams("parallel", "arbitrary"),
    )(q, k, v, do, o, lse)


def _shift_rows(x, s):
    tp = x.shape[0]
    return x if s % tp == 0 else pltpu.roll(x, s % tp, 0)


def _conv_fwd_val(xm, w, b, pad_left):
    acc = b + w[0:1, :] * _shift_rows(xm, pad_left)
    for k in range(1, w.shape[0]):
        acc = acc + w[k:k + 1, :] * _shift_rows(xm, pad_left - k)
    return acc


def _conv_bwd_val(dy, xm, w, pad_left, live):
    kk = w.shape[0]
    dx = w[0:1, :] * _shift_rows(dy, -pad_left)
    dws = [jnp.sum(dy * _shift_rows(xm, pad_left), axis=0, keepdims=True)]
    for k in range(1, kk):
        dx = dx + w[k:k + 1, :] * _shift_rows(dy, k - pad_left)
        dws.append(jnp.sum(dy * _shift_rows(xm, pad_left - k), axis=0, keepdims=True))
    return jnp.where(live, dx, 0.0), jnp.concatenate(dws, axis=0), jnp.sum(dy, axis=0, keepdims=True)


def _lru_conv_fwd(proj, w, b, t_real, name):
    def fn(x, ww, bb):
        xm = jnp.where(_row_ids(x.shape) < t_real, x, 0.0)
        return _conv_fwd_val(xm, ww, bb, 2)

    return _cols(fn, name, [Cl(proj, C_LRU_X), Cl(w), Cl(b.reshape(1, -1))], [(proj.shape[0], F32)], D_MODEL, 128)[0]


def _lru_conv_bwd(dxc, proj, w, t_real, name):
    def fn(dy, x, ww):
        live = _row_ids(x.shape) < t_real
        xm = jnp.where(live, x, 0.0)
        dym = jnp.where(live, dy, 0.0)
        return _conv_bwd_val(dym, xm, ww, 2, live)

    return _cols(fn, name, [Cl(dxc), Cl(proj, C_LRU_X), Cl(w)],
                 [(proj.shape[0], F32), (w.shape[0], F32), (1, F32)], D_MODEL, 128)


def _ffn_conv_act(up, w, b, t_real, name):
    def fn(g, v, wg, wv, bg, bv):
        live = _row_ids(g.shape) < t_real
        gc = _conv_fwd_val(jnp.where(live, g, 0.0), wg, bg, 1)
        vc = _conv_fwd_val(jnp.where(live, v, 0.0), wv, bv, 1)
        return _gelu(gc) * vc

    b2 = b.reshape(1, -1)
    return _cols(fn, name, [Cl(up), Cl(up, D_FF), Cl(w), Cl(w, D_FF), Cl(b2), Cl(b2, D_FF)],
                 [(up.shape[0], F32)], D_FF, 128)[0]


def _ffn_conv_act_bwd(dm, up, w, b, t_real, name):
    def fn(dmb, g, v, wg, wv, bg, bv):
        live = _row_ids(g.shape) < t_real
        gm, vm = jnp.where(live, g, 0.0), jnp.where(live, v, 0.0)
        gc = _conv_fwd_val(gm, wg, bg, 1)
        vc = _conv_fwd_val(vm, wv, bv, 1)
        act, dact = _gelu_and_grad(gc)
        dmm = jnp.where(live, dmb, 0.0)
        dgx, dwg, dbg = _conv_bwd_val(dmm * vc * dact, gm, wg, 1, live)
        dvx, dwv, dbv = _conv_bwd_val(dmm * act, vm, wv, 1, live)
        return dgx, dvx, dwg, dwv, dbg, dbv

    b2 = b.reshape(1, -1)
    tp, kk = up.shape[0], w.shape[0]
    return _cols(fn, name, [Cl(dm), Cl(up), Cl(up, D_FF), Cl(w), Cl(w, D_FF), Cl(b2), Cl(b2, D_FF)],
                 [(tp, F32), (tp, F32), (kk, F32), (kk, F32), (1, F32), (1, F32)], D_FF, 128)


def _lru_gates_fwd(xc, wg, b4, lam, t_real, name):
    tp = xc.shape[0]
    tm = _tile(tp, 1408)

    def body(x_ref, w_ref, b_ref, lam_ref, r0_ref, r1_ref, i0_ref, i1_ref, a0_ref, a1_ref, u0_ref, u1_ref):
        x = x_ref[...]
        xb = x.astype(BF16)
        live = _row_ids(x.shape, pl.program_id(1) * tm) < t_real
        bb = b_ref[...]
        sp = _softplus_neg(lam_ref[...])
        gate = [_sigmoid(jnp.dot(xb, w_ref[k], preferred_element_type=F32) + bb[k:k + 1, :]) for k in range(4)]
        for d, (r_ref, i_ref, a_ref, u_ref) in enumerate(((r0_ref, i0_ref, a0_ref, u0_ref), (r1_ref, i1_ref, a1_ref, u1_ref))):
            r, ig = gate[d], gate[2 + d]
            a = jnp.exp(-LRU_C * r * sp[d:d + 1, :])
            r_ref[...] = r
            i_ref[...] = ig
            a_ref[...] = a
            u_ref[...] = jnp.where(live, jnp.sqrt(1.0 - a * a) * (ig * x), 0.0)

    blk = pl.BlockSpec((tm, LANE), lambda g, i: (i, g))
    return pl.pallas_call(
        body, name=name, grid=(LRU_BLOCKS, tp // tm),
        in_specs=[blk, pl.BlockSpec((None, 4, LANE, LANE), lambda g, i: (g, 0, 0, 0)),
                  pl.BlockSpec((4, LANE), lambda g, i: (0, g)), pl.BlockSpec((2, LANE), lambda g, i: (0, g))],
        out_specs=[blk] * 8,
        out_shape=[jax.ShapeDtypeStruct((tp, D_MODEL), F32)] * 8,
        compiler_params=_cparams("parallel", "parallel"),
    )(xc, wg, b4, lam)


def _lru_gates_bwd(l0, l1, da0, da1, r0, r1, i0, i1, a0, a1, xc, wg, lam, t_real, name):
    tp = xc.shape[0]
    tm = _tile(tp, 1408)

    def body(l0_ref, l1_ref, da0_ref, da1_ref, r0_ref, r1_ref, i0_ref, i1_ref, a0_ref, a1_ref, x_ref, w_ref, lam_ref,
             dx_ref, dw_ref, db_ref, dlam_ref):
        i = pl.program_id(1)
        x = x_ref[...]
        xb = x.astype(BF16)
        live = _row_ids(x.shape, i * tm) < t_real
        lam_ = lam_ref[...]
        sp = _softplus_neg(lam_)
        dsp_dlam = -_sigmoid(-lam_)
        dx = jnp.zeros_like(x)
        dpre = [None] * 4
        dlam_rows = []
        for d, (l_ref, da_ref, r_ref, i_ref, a_ref) in enumerate(((l0_ref, da0_ref, r0_ref, i0_ref, a0_ref),
                                                                  (l1_ref, da1_ref, r1_ref, i1_ref, a1_ref))):
            r, ig, a = r_ref[...], i_ref[...], a_ref[...]
            du = jnp.where(live, l_ref[...], 0.0)
            s = jnp.sqrt(1.0 - a * a)
            dv = du * s
            ds = du * (ig * x)
            dla = jnp.where(live, da_ref[...], 0.0) * a - ds * (a * a) / s
            dla = jnp.where(live, dla, 0.0)
            dr = dla * (-LRU_C) * sp[d:d + 1, :]
            dlam_rows.append(jnp.sum(dla * (-LRU_C) * r, axis=0, keepdims=True) * dsp_dlam[d:d + 1, :])
            dpre[d] = dr * r * (1.0 - r)
            dpre[2 + d] = dv * x * ig * (1.0 - ig)
            dx = dx + dv * ig

        @pl.when(i == 0)
        def _():
            dw_ref[...] = jnp.zeros_like(dw_ref)
            db_ref[...] = jnp.zeros_like(db_ref)
            dlam_ref[...] = jnp.zeros_like(dlam_ref)

        for k in range(4):
            pk = dpre[k].astype(BF16)
            dx = dx + lax.dot_general(pk, w_ref[k], (((1,), (1,)), ((), ())), preferred_element_type=F32)
            dw_ref[k] += lax.dot_general(xb, pk, (((0,), (0,)), ((), ())), preferred_element_type=F32)
        db_ref[...] += jnp.concatenate([jnp.sum(p, axis=0, keepdims=True) for p in dpre], axis=0)
        dlam_ref[...] += jnp.concatenate(dlam_rows, axis=0)
        dx_ref[...] = dx

    blk = pl.BlockSpec((tm, LANE), lambda g, i: (i, g))
    return pl.pallas_call(
        body, name=name, grid=(LRU_BLOCKS, tp // tm),
        in_specs=[blk] * 11 + [pl.BlockSpec((None, 4, LANE, LANE), lambda g, i: (g, 0, 0, 0)),
                               pl.BlockSpec((2, LANE), lambda g, i: (0, g))],
        out_specs=[blk, pl.BlockSpec((None, 4, LANE, LANE), lambda g, i: (g, 0, 0, 0)),
                   pl.BlockSpec((4, LANE), lambda g, i: (0, g)), pl.BlockSpec((2, LANE), lambda g, i: (0, g))],
        out_shape=[jax.ShapeDtypeStruct((tp, D_MODEL), F32), jax.ShapeDtypeStruct((LRU_BLOCKS, 4, LANE, LANE), F32),
                   jax.ShapeDtypeStruct((4, D_MODEL), F32), jax.ShapeDtypeStruct((2, D_MODEL), F32)],
        compiler_params=_cparams("parallel", "arbitrary"),
    )(l0, l1, da0, da1, r0, r1, i0, i1, a0, a1, xc, wg, lam)


def _tile_scan(a, u, reverse):
    rows = lax.broadcasted_iota(jnp.int32, a.shape, 0)
    for s in (1, 2, 4):
        if reverse:
            keep = rows < SUBLANE - s
            a_sh, u_sh = pltpu.roll(a, SUBLANE - s, 0), pltpu.roll(u, SUBLANE - s, 0)
        else:
            keep = rows >= s
            a_sh, u_sh = pltpu.roll(a, s, 0), pltpu.roll(u, s, 0)
        u = u + a * jnp.where(keep, u_sh, 0.0)
        a = a * jnp.where(keep, a_sh, 1.0)
    return a, u


def _scan_fwd(a0, u0, a1, u1, name):
    tp, d = a0.shape
    tc = 128
    nt = tp // SUBLANE

    def body(a0_ref, u0_ref, a1_ref, u1_ref, h0_ref, h1_ref):
        def step(t, carry):
            c0, c1 = carry
            f = pl.multiple_of(t * SUBLANE, SUBLANE)
            b = pl.multiple_of((nt - 1 - t) * SUBLANE, SUBLANE)
            pa, pu = _tile_scan(a0_ref[pl.ds(f, SUBLANE), :], u0_ref[pl.ds(f, SUBLANE), :], False)
            h = pu + pa * c0
            h0_ref[pl.ds(f, SUBLANE), :] = h
            c0 = h[SUBLANE - 1:SUBLANE, :]
            pa, pu = _tile_scan(a1_ref[pl.ds(b, SUBLANE), :], u1_ref[pl.ds(b, SUBLANE), :], True)
            h = pu + pa * c1
            h1_ref[pl.ds(b, SUBLANE), :] = h
            c1 = h[0:1, :]
            return c0, c1

        z = jnp.zeros((1, tc), F32)
        lax.fori_loop(0, nt, step, (z, z))

    blk = pl.BlockSpec((tp, tc), lambda j: (0, j))
    return pl.pallas_call(
        body, name=name, grid=(d // tc,), in_specs=[blk] * 4, out_specs=[blk] * 2,
        out_shape=[jax.ShapeDtypeStruct((tp, d), F32)] * 2,
        compiler_params=_cparams("parallel"),
    )(a0, u0, a1, u1)


def _scan_bwd(dh, a0, a1, h0, h1, name):
    tp, d = dh.shape
    tc = 128
    nt = tp // SUBLANE

    def body(dh_ref, a0_ref, a1_ref, h0_ref, h1_ref, l0_ref, l1_ref, da0_ref, da1_ref):
        rows8 = lax.broadcasted_iota(jnp.int32, (SUBLANE, tc), 0)

        def step(t, carry):
            c0, c1 = carry
            b = pl.multiple_of((nt - 1 - t) * SUBLANE, SUBLANE)
            f = pl.multiple_of(t * SUBLANE, SUBLANE)
            a = a0_ref[pl.ds(b, SUBLANE), :]
            a_next = jnp.where(rows8 < SUBLANE - 1, pltpu.roll(a, SUBLANE - 1, 0), 1.0)
            pa, pu = _tile_scan(a_next, dh_ref[pl.ds(b, SUBLANE), :], True)
            lam = pu + pa * c0
            l0_ref[pl.ds(b, SUBLANE), :] = lam
            c0 = a[0:1, :] * lam[0:1, :]
            a = a1_ref[pl.ds(f, SUBLANE), :]
            a_prev = jnp.where(rows8 >= 1, pltpu.roll(a, 1, 0), 1.0)
            pa, pu = _tile_scan(a_prev, dh_ref[pl.ds(f, SUBLANE), :], False)
            lam = pu + pa * c1
            l1_ref[pl.ds(f, SUBLANE), :] = lam
            c1 = a[SUBLANE - 1:SUBLANE, :] * lam[SUBLANE - 1:SUBLANE, :]
            return c0, c1

        z = jnp.zeros((1, tc), F32)
        lax.fori_loop(0, nt, step, (z, z))
        rows = lax.broadcasted_iota(jnp.int32, (tp, tc), 0)
        da0_ref[...] = l0_ref[...] * jnp.where(rows >= 1, pltpu.roll(h0_ref[...], 1, 0), 0.0)
        da1_ref[...] = l1_ref[...] * jnp.where(rows < tp - 1, pltpu.roll(h1_ref[...], tp - 1, 0), 0.0)

    blk = pl.BlockSpec((tp, tc), lambda j: (0, j))
    return pl.pallas_call(
        body, name=name, grid=(d // tc,), in_specs=[blk] * 5, out_specs=[blk] * 4,
        out_shape=[jax.ShapeDtypeStruct((tp, d), F32)] * 4,
        compiler_params=_cparams("parallel"),
    )(dh, a0, a1, h0, h1)


def _gated_h(proj, h0, h1, name):
    def fn(row0, lg, x0, x1):
        return _gelu(lg) * (x0 + x1)

    return _rows(fn, name, [Rw(proj, D_MODEL, C_LRU_G // D_MODEL), Rw(h0), Rw(h1)], [(D_MODEL, F32)])[0]


def _gated_h_bwd(dgh, proj, h0, h1, name):
    def fn(row0, dg, lg, x0, x1):
        act, dact = _gelu_and_grad(lg)
        return dg * (x0 + x1) * dact, dg * act

    return _rows(fn, name, [Rw(dgh), Rw(proj, D_MODEL, C_LRU_G // D_MODEL), Rw(h0), Rw(h1)], [(D_MODEL, F32)] * 2)


def _mix(proj, y_mla, y_lru, name):
    def fn(row0, gm, gl, ym, yl):
        return _sigmoid(gm) * ym + _sigmoid(gl) * yl

    return _rows(fn, name, [Rw(proj, D_MODEL, C_G_MLA // D_MODEL), Rw(proj, D_MODEL, C_G_LRU // D_MODEL), Rw(y_mla), Rw(y_lru)],
                 [(D_MODEL, F32)])[0]


def _mix_bwd(dz, proj, y_mla, y_lru, name):
    def fn(row0, dzb, gm, gl, ym, yl):
        sm, sl = _sigmoid(gm), _sigmoid(gl)
        return dzb * sm, dzb * sl, dzb * ym * sm * (1.0 - sm), dzb * yl * sl * (1.0 - sl)

    return _rows(fn, name, [Rw(dz), Rw(proj, D_MODEL, C_G_MLA // D_MODEL), Rw(proj, D_MODEL, C_G_LRU // D_MODEL),
                            Rw(y_mla), Rw(y_lru)], [(D_MODEL, F32)] * 4)


def _layer_fwd(h, w, cs, t_real, tag):
    proj = _matmul(h, w['w_in'], tag + "proj")
    cqn, ckvn = _mla_norms(proj, w['q_norm'], w['kv_norm'], tag + "mla_norms")
    qext = _matmul(cqn, w['w_q'], tag + "q_up")
    kv = _matmul(ckvn, w['w_kv'], tag + "kv_up")
    qc, kc, vb = _mla_pack(qext, kv, proj, cs, tag + "mla_pack")
    o, lse = _attn_fwd(qc, kc, vb, t_real, tag + "attn_fwd")
    y_mla = _matmul(o, w['w_o_mla'], tag + "o_mla")
    xc = _lru_conv_fwd(proj, w['lru_conv_w'], w['lru_conv_b'], t_real, tag + "lru_conv")
    r0, r1, i0, i1, a0, a1, u0, u1 = _lru_gates_fwd(xc, w['w_g'], w['b4'], w['lru_lambda'], t_real, tag + "lru_gates")
    h0, h1 = _scan_fwd(a0, u0, a1, u1, tag + "lru_scan")
    gh = _gated_h(proj, h0, h1, tag + "lru_gate_out")
    y_lru = _matmul(gh, w['w_o_lru'], tag + "o_lru")
    z = _mix(proj, y_mla, y_lru, tag + "mix")
    zo = _matmul(z, w['w_out'], tag + "w_out")
    hm = _ln_fwd([(DN_ALPHA, h), (1.0, zo)], w['ln1_g'], w['ln1_b'], tag + "ln1")
    up = _matmul(hm, w['w_up'], tag + "w_up")
    m = _ffn_conv_act(up, w['ffn_conv_w'], w['ffn_conv_b'], t_real, tag + "ffn_conv")
    f = _matmul(m, w['w_down'], tag + "w_down", tk_cap=1408)
    out = _ln_fwd([(DN_ALPHA, hm), (1.0, f)], w['ln2_g'], w['ln2_b'], tag + "ln2")
    saved = dict(h=h, proj=proj, cqn=cqn, ckvn=ckvn, qc=qc, kc=kc, vb=vb, o=o, lse=lse, y_mla=y_mla, xc=xc,
                 r0=r0, r1=r1, i0=i0, i1=i1, a0=a0, a1=a1, h0=h0, h1=h1, gh=gh, y_lru=y_lru, z=z, zo=zo, hm=hm,
                 up=up, m=m, f=f)
    return out, saved


def _layer_bwd(dout_terms, w, s, cs, t_real, tag):
    g = {}
    du2, dg2, db2 = _ln_bwd(dout_terms, [(DN_ALPHA, s['hm']), (1.0, s['f'])], w['ln2_g'], tag + "ln2_bwd")
    g['ln2_g'], g['ln2_b'] = dg2, db2
    dm = _matmul(du2, w['w_down'], tag + "w_down_dx", tb=True)
    g['w_down'] = _matmul(s['m'], du2, tag + "w_down_dw", ta=True, tk_cap=1408)
    dgp, dvp, dwg_, dwv_, dbg_, dbv_ = _ffn_conv_act_bwd(dm, s['up'], w['ffn_conv_w'], w['ffn_conv_b'], t_real, tag + "ffn_conv_bwd")
    g['ffn_conv_w'] = jnp.concatenate([dwg_, dwv_], axis=1)
    g['ffn_conv_b'] = jnp.concatenate([dbg_, dbv_], axis=1)
    dup = jnp.concatenate([dgp, dvp], axis=1)
    dhm_mm = _matmul(dup, w['w_up'], tag + "w_up_dx", tb=True, tk_cap=1408)
    g['w_up'] = _matmul(s['hm'], dup, tag + "w_up_dw", ta=True, tk_cap=1408)
    du1, dg1, db1 = _ln_bwd([(DN_ALPHA, du2), (1.0, dhm_mm)], [(DN_ALPHA, s['h']), (1.0, s['zo'])], w['ln1_g'], tag + "ln1_bwd")
    g['ln1_g'], g['ln1_b'] = dg1, db1
    dz = _matmul(du1, w['w_out'], tag + "w_out_dx", tb=True)
    g['w_out'] = _matmul(s['z'], du1, tag + "w_out_dw", ta=True, tk_cap=1408)
    dy_mla, dy_lru, dg_mla, dg_lru = _mix_bwd(dz, s['proj'], s['y_mla'], s['y_lru'], tag + "mix_bwd")
    do = _matmul(dy_mla, w['w_o_mla'], tag + "o_mla_dx", tb=True)
    g['w_o_mla'] = _matmul(s['o'], dy_mla, tag + "o_mla_dw", ta=True, tk_cap=1408)
    dqc, dkc, dv = _attn_bwd(s['qc'], s['kc'], s['vb'], do, s['o'], s['lse'], t_real, tag + "attn_bwd")
    dqext, dkn, dkrp = _mla_unpack(dqc, dkc, cs, tag + "mla_unpack")
    dkv = jnp.concatenate([dkn, dv], axis=1)
    dcqn = _matmul(dqext, w['w_q'], tag + "q_up_dx", tb=True)
    g['w_q'] = _matmul(s['cqn'], dqext, tag + "q_up_dw", ta=True, tk_cap=1408)
    dckvn = _matmul(dkv, w['w_kv'], tag + "kv_up_dx", tb=True)
    g['w_kv'] = _matmul(s['ckvn'], dkv, tag + "kv_up_dw", ta=True, tk_cap=1408)
    dcq, dckv, g['q_norm'], g['kv_norm'] = _mla_norms_bwd(dcqn, dckvn, s['proj'], w['q_norm'], w['kv_norm'], tag + "mla_norms_bwd")
    dgh = _matmul(dy_lru, w['w_o_lru'], tag + "o_lru_dx", tb=True)
    g['w_o_lru'] = _matmul(s['gh'], dy_lru, tag + "o_lru_dw", ta=True, tk_cap=1408)
    dlru_g, dhs = _gated_h_bwd(dgh, s['proj'], s['h0'], s['h1'], tag + "lru_gate_out_bwd")
    l0, l1, da0, da1 = _scan_bwd(dhs, s['a0'], s['a1'], s['h0'], s['h1'], tag + "lru_scan_bwd")
    dxc, g['w_g'], g['b4'], g['lru_lambda'] = _lru_gates_bwd(
        l0, l1, da0, da1, s['r0'], s['r1'], s['i0'], s['i1'], s['a0'], s['a1'], s['xc'], w['w_g'], w['lru_lambda'],
        t_real, tag + "lru_gates_bwd")
    dlru_x, g['lru_conv_w'], g['lru_conv_b'] = _lru_conv_bwd(dxc, s['proj'], w['lru_conv_w'], t_real, tag + "lru_conv_bwd")
    dproj = jnp.concatenate([dlru_g, dlru_x, dg_mla, dg_lru, dcq, dckv, dkrp], axis=1)
    dh_mm = _matmul(dproj, w['w_in'], tag + "proj_dx", tb=True, tk_cap=1536)
    g['w_in'] = _matmul(s['h'], dproj, tag + "proj_dw", ta=True, tk_cap=1408)
    return [(DN_ALPHA, du1), (1.0, dh_mm)], g


def _swap_halves(a, axis=-1):
    h1, h2 = jnp.split(a, 2, axis=axis)
    return jnp.concatenate([h2, h1], axis=axis)


def _layer_weights(full, l):
    w_in = full['w_in'][l]
    cq, ckv, kr, lg, lx, gm, gl = jnp.split(w_in, [256, 384, 448, 1472, 2496, 3520], axis=1)
    w = {'w_in': jnp.concatenate([lg, lx, gm, gl, cq, ckv, kr, _swap_halves(kr)], axis=1)}
    uq = full['w_uq'][l]
    w['w_q'] = jnp.concatenate([uq, _swap_halves(uq[..., QK_NOPE:])], axis=-1).reshape(Q_RANK, HEADS * 2 * LANE)
    w['w_kv'] = jnp.concatenate([full['w_uk'][l].reshape(KV_RANK, -1), full['w_uv'][l].reshape(KV_RANK, -1)], axis=1).astype(BF16)
    rg, ig = full['w_rg'][l], full['w_ig'][l]
    w['w_g'] = jnp.moveaxis(jnp.concatenate([rg, ig], axis=0), 0, 1).astype(BF16)
    w['b4'] = jnp.concatenate([full['b_rg'][l], full['b_ig'][l]], axis=0)
    for n in ('q_norm', 'kv_norm', 'w_o_mla', 'lru_conv_w', 'lru_conv_b', 'lru_lambda', 'w_o_lru', 'w_out', 'ln1_g',
              'ln1_b', 'w_up', 'ffn_conv_w', 'ffn_conv_b', 'w_down', 'ln2_g', 'ln2_b'):
        w[n] = full[n][l]
    return w


def _layer_grads(g):
    out = {}
    lg, lx, gm, gl, cq, ckv, kr, krs = jnp.split(g['w_in'], [1024, 2048, 3072, 4096, 4352, 4480, 4544], axis=1)
    out['w_in'] = jnp.concatenate([cq, ckv, kr + _swap_halves(krs), lg, lx, gm, gl], axis=1)
    gq = g['w_q'].reshape(Q_RANK, HEADS, 2 * LANE)
    out['w_uq'] = jnp.concatenate([gq[..., :QK_NOPE], gq[..., QK_NOPE:QK_NOPE + QK_ROPE] + _swap_halves(gq[..., QK_NOPE + QK_ROPE:])], axis=-1)
    out['w_uk'] = g['w_kv'][:, :HEADS * QK_NOPE].reshape(KV_RANK, HEADS, QK_NOPE)
    out['w_uv'] = g['w_kv'][:, HEADS * QK_NOPE:].reshape(KV_RANK, HEADS, V_HEAD)
    gg = jnp.moveaxis(g['w_g'], 1, 0)
    out['w_rg'], out['w_ig'] = gg[:2], gg[2:]
    out['b_rg'], out['b_ig'] = g['b4'][:2], g['b4'][2:]
    for n in ('q_norm', 'kv_norm', 'lru_conv_b', 'ln1_g', 'ln1_b', 'ffn_conv_b', 'ln2_g', 'ln2_b'):
        out[n] = g[n].reshape(-1)
    for n in ('w_o_mla', 'lru_conv_w', 'lru_lambda', 'w_o_lru', 'w_out', 'w_up', 'ffn_conv_w', 'w_down'):
        out[n] = g[n]
    return out


def _rope_table(tp):
    half = QK_ROPE // 2
    inv_freq = jnp.exp(-math.log(ROPE_THETA) * jnp.arange(half, dtype=F32) / half)
    ang = jnp.arange(tp, dtype=F32)[:, None] * inv_freq[None, :]
    c, s = jnp.cos(ang), jnp.sin(ang)
    return jnp.concatenate([c, c, -s, s], axis=1)


def _local_step(x, target, full, t_pad):
    seq = x.shape[0]
    t_real = N_META + seq
    zpad = jnp.zeros((t_pad - t_real, D_MODEL), F32)
    xin = jnp.concatenate([full['meta_tokens'], x, zpad], axis=0)
    tgt = jnp.concatenate([jnp.zeros((N_META, D_MODEL), F32), target, zpad], axis=0)
    cs = _rope_table(t_pad)
    h = _ln_fwd([(1.0, xin)], full['ln0_g'], full['ln0_b'], "ln0")
    saved, lw = [], []
    for l in range(DEPTH):
        lw.append(_layer_weights(full, l))
        h, s = _layer_fwd(h, lw[l], cs, t_real, "l%d_" % l)
        saved.append(s)
    dy, lossvec = _loss_head(h, tgt, t_real, "loss_head")
    loss = jnp.sum(lossvec)
    terms = [(1.0, dy)]
    layer_g = [None] * DEPTH
    for l in reversed(range(DEPTH)):
        terms, g = _layer_bwd(terms, lw[l], saved[l], cs, t_real, "l%d_" % l)
        layer_g[l] = _layer_grads(g)
    dxin, dg0, db0 = _ln_bwd(terms, [(1.0, xin)], full['ln0_g'], "ln0_bwd")
    grads = {n: jnp.stack([layer_g[l][n] for l in range(DEPTH)]) for n in layer_g[0]}
    grads['ln0_g'], grads['ln0_b'] = dg0.reshape(-1), db0.reshape(-1)
    grads['meta_tokens'] = dxin[:N_META]
    return loss, dxin[N_META:t_real], grads


def _all_gather(shard, name):
    n, lanes = shard.shape

    def body(x_ref, out_ref, send_sems, recv_sems, local_sem):
        x, y, c = lax.axis_index("x"), lax.axis_index("y"), lax.axis_index("c")
        me, sibling = (x, y, c), (x, y, 1 - c)
        chips = [(1 - x, y), (x, 1 - y), (1 - x, 1 - y)]

        def slot(px, py, pc):
            return out_ref.at[4 * px + 2 * py + pc]

        def copy(k, block, to, src=None):
            return pltpu.make_async_remote_copy(
                src_ref=slot(*block) if src is None else src, dst_ref=slot(*block),
                send_sem=send_sems.at[k], recv_sem=recv_sems.at[k],
                device_id=to, device_id_type=pl.DeviceIdType.MESH)

        mine = pltpu.make_async_copy(x_ref, slot(*me), local_sem)
        mine.start()
        first = [copy(0, me, sibling, src=x_ref)]
        first += [copy(1 + j, me, (*chip, c), src=x_ref) for j, chip in enumerate(chips)]
        for cp in first:
            cp.start()
        passed = [copy(4 + j, (*chip, c), sibling) for j, chip in enumerate(chips)]
        for j, chip in enumerate(chips):
            copy(1 + j, (*chip, c), me).wait_recv()
            passed[j].start()
        copy(0, sibling, me).wait_recv()
        for j, chip in enumerate(chips):
            copy(4 + j, (*chip, 1 - c), me).wait_recv()
        for cp in first + passed:
            cp.wait_send()
        mine.wait()

    return pl.pallas_call(
        body, name=name,
        out_shape=jax.ShapeDtypeStruct((N_DEV, n, lanes), shard.dtype),
        in_specs=[pl.BlockSpec(memory_space=pl.ANY)],
        out_specs=pl.BlockSpec(memory_space=pl.ANY),
        scratch_shapes=[pltpu.SemaphoreType.DMA((7,)), pltpu.SemaphoreType.DMA((7,)), pltpu.SemaphoreType.DMA],
    )(shard)


def _exchange(contrib, name):
    _, n, lanes = contrib.shape

    def body(in_ref, out_ref, send_sems, recv_sems, local_sem):
        x, y, c = lax.axis_index("x"), lax.axis_index("y"), lax.axis_index("c")
        me = 4 * x + 2 * y + c
        mine = pltpu.make_async_copy(in_ref.at[me], out_ref.at[me], local_sem)
        mine.start()
        copies = []
        for k in range(1, N_DEV):
            px = 1 - x if k & 4 else x
            py = 1 - y if k & 2 else y
            pc = 1 - c if k & 1 else c
            peer = 4 * px + 2 * py + pc
            copies.append(pltpu.make_async_remote_copy(
                src_ref=in_ref.at[peer], dst_ref=out_ref.at[me],
                send_sem=send_sems.at[k - 1], recv_sem=recv_sems.at[k - 1],
                device_id=(px, py, pc), device_id_type=pl.DeviceIdType.MESH))
        for cp in copies:
            cp.start()
        for cp in copies:
            cp.wait_recv()
        for cp in copies:
            cp.wait_send()
        mine.wait()

    return pl.pallas_call(
        body, name=name,
        out_shape=jax.ShapeDtypeStruct((N_DEV, n, lanes), contrib.dtype),
        in_specs=[pl.BlockSpec(memory_space=pl.ANY)],
        out_specs=pl.BlockSpec(memory_space=pl.ANY),
        scratch_shapes=[pltpu.SemaphoreType.DMA((7,)), pltpu.SemaphoreType.DMA((7,)), pltpu.SemaphoreType.DMA],
    )(contrib)


def _sum_blocks(pieces, name):
    _, n, lanes = pieces.shape
    tr = _tile(n, 2048, SUBLANE)

    def body(p_ref, o_ref):
        acc = p_ref[0]
        for k in range(1, N_DEV):
            acc = acc + p_ref[k]
        o_ref[...] = acc

    return pl.pallas_call(
        body, name=name, grid=(n // tr,),
        in_specs=[pl.BlockSpec((N_DEV, tr, lanes), lambda i: (0, i, 0))],
        out_specs=pl.BlockSpec((tr, lanes), lambda i: (i, 0)),
        out_shape=jax.ShapeDtypeStruct((n, lanes), F32),
        compiler_params=_cparams("parallel"),
    )(pieces)


def _adamw(w, g, m, v, name):
    n, lanes = w.shape
    tr = _tile(n, 2048, SUBLANE)
    c1 = 1.0 / (1.0 - ADAM_B1 ** ADAM_STEP)
    c2 = 1.0 / (1.0 - ADAM_B2 ** ADAM_STEP)

    def body(w_ref, g_ref, m_ref, v_ref, d_ref, nm_ref, nv_ref):
        gg = g_ref[...]
        nm = ADAM_B1 * m_ref[...] + (1.0 - ADAM_B1) * gg
        nv = ADAM_B2 * v_ref[...] + (1.0 - ADAM_B2) * (gg * gg)
        d_ref[...] = -ADAM_LR * ((nm * c1) / (jnp.sqrt(nv * c2) + ADAM_EPS) + ADAM_WD * w_ref[...])
        nm_ref[...] = nm
        nv_ref[...] = nv

    blk = pl.BlockSpec((tr, lanes), lambda i: (i, 0))
    return pl.pallas_call(
        body, name=name, grid=(n // tr,), in_specs=[blk] * 4, out_specs=[blk] * 3,
        out_shape=[jax.ShapeDtypeStruct((n, lanes), F32)] * 3,
        compiler_params=_cparams("parallel"),
    )(w, g, m, v)


PACK_ROWS = 2 * SUBLANE
ROW_ALIGN = PACK_ROWS * LANE


def _flat_rows(a, lead=()):
    nl = len(lead)
    flat = a.reshape(a.shape[:nl] + (-1,))
    size = flat.shape[-1]
    pad = (-size) % ROW_ALIGN
    if pad:
        flat = jnp.pad(flat, [(0, 0)] * nl + [(0, pad)])
    return flat.reshape(a.shape[:nl] + (-1, LANE))


def _pack(arrs, lead=()):
    return jnp.concatenate([_flat_rows(a, lead) for a in arrs], axis=len(lead))


def _unpack(flat, shapes, lead=()):
    nl = len(lead)
    out, r = [], 0
    for shp in shapes:
        size = math.prod(shp)
        rows = -(-size // ROW_ALIGN) * PACK_ROWS
        piece = lax.slice_in_dim(flat, r, r + rows, axis=nl)
        piece = piece.reshape(flat.shape[:nl] + (-1,))[..., :size]
        out.append(piece.reshape(flat.shape[:nl] + tuple(shp)))
        r += rows
    return out


def _to_shards(full, axis):
    shp = full.shape
    a = full.reshape(shp[:axis] + (N_DEV, shp[axis] // N_DEV) + shp[axis + 1:])
    return jnp.moveaxis(a, axis, 0)


def _from_shards(blocks, axis):
    a = jnp.moveaxis(blocks, 0, axis)
    shp = a.shape
    return a.reshape(shp[:axis] + (shp[axis] * shp[axis + 1],) + shp[axis + 2:])


def kernel(x, meta_tokens, ln0_g, ln0_b, w_in, q_norm, kv_norm, w_uq, w_uk, w_uv, w_o_mla, lru_conv_w, lru_conv_b, w_rg, b_rg, w_ig, b_ig, lru_lambda, w_o_lru, w_out, ln1_g, ln1_b, w_up, ffn_conv_w, ffn_conv_b, w_down, ln2_g, ln2_b, loss_target, m_meta_tokens, m_ln0_g, m_ln0_b, m_w_in, m_q_norm, m_kv_norm, m_w_uq, m_w_uk, m_w_uv, m_w_o_mla, m_lru_conv_w, m_lru_conv_b, m_w_rg, m_b_rg, m_w_ig, m_b_ig, m_lru_lambda, m_w_o_lru, m_w_out, m_ln1_g, m_ln1_b, m_w_up, m_ffn_conv_w, m_ffn_conv_b, m_w_down, m_ln2_g, m_ln2_b, v_meta_tokens, v_ln0_g, v_ln0_b, v_w_in, v_q_norm, v_kv_norm, v_w_uq, v_w_uk, v_w_uv, v_w_o_mla, v_lru_conv_w, v_lru_conv_b, v_w_rg, v_b_rg, v_w_ig, v_b_ig, v_lru_lambda, v_w_o_lru, v_w_out, v_ln1_g, v_ln1_b, v_w_up, v_ffn_conv_w, v_ffn_conv_b, v_w_down, v_ln2_g, v_ln2_b):
    args = (meta_tokens, ln0_g, ln0_b, w_in, q_norm, kv_norm, w_uq, w_uk, w_uv, w_o_mla, lru_conv_w, lru_conv_b, w_rg, b_rg, w_ig, b_ig, lru_lambda, w_o_lru, w_out, ln1_g, ln1_b, w_up, ffn_conv_w, ffn_conv_b, w_down, ln2_g, ln2_b)
    ms = (m_meta_tokens, m_ln0_g, m_ln0_b, m_w_in, m_q_norm, m_kv_norm, m_w_uq, m_w_uk, m_w_uv, m_w_o_mla, m_lru_conv_w, m_lru_conv_b, m_w_rg, m_b_rg, m_w_ig, m_b_ig, m_lru_lambda, m_w_o_lru, m_w_out, m_ln1_g, m_ln1_b, m_w_up, m_ffn_conv_w, m_ffn_conv_b, m_w_down, m_ln2_g, m_ln2_b)
    vs = (v_meta_tokens, v_ln0_g, v_ln0_b, v_w_in, v_q_norm, v_kv_norm, v_w_uq, v_w_uk, v_w_uv, v_w_o_mla, v_lru_conv_w, v_lru_conv_b, v_w_rg, v_b_rg, v_w_ig, v_b_ig, v_lru_lambda, v_w_o_lru, v_w_out, v_ln1_g, v_ln1_b, v_w_up, v_ffn_conv_w, v_ffn_conv_b, v_w_down, v_ln2_g, v_ln2_b)
    wd, md, vd = dict(zip(WEIGHTS, args)), dict(zip(WEIGHTS, ms)), dict(zip(WEIGHTS, vs))

    full = {n: wd[n] for n in REPLICATED}
    for names, dt, tag in ((BIG, BF16, "gather_big"), (SMALL, F32, "gather_small")):
        got = _all_gather(_pack([wd[n].astype(dt) for n in names]), tag)
        for n, blocks in zip(names, _unpack(got, [wd[n].shape for n in names], lead=(N_DEV,))):
            full[n] = _from_shards(blocks, SHARD_AXIS[n])

    seq = x.shape[1]
    t_pad = -(-(N_META + seq) // LANE) * LANE
    loss, grad_x, grads = _local_step(x[0], loss_target[0], full, t_pad)
    loss = lax.psum(loss, ("x", "y", "c"))

    rep_shapes = [wd[n].shape for n in REPLICATED]
    rep = _pack([grads[n] for n in REPLICATED])
    rep_rows = rep.shape[0]
    rep_pad = (-rep_rows) % (N_DEV * PACK_ROWS)
    rep = jnp.pad(rep, ((0, rep_pad), (0, 0))).reshape(N_DEV, -1, LANE)
    sh = _pack([_to_shards(grads[n], SHARD_AXIS[n]) for n in SHARDED], lead=(N_DEV,))
    n_sh = sh.shape[1]
    summed = _sum_blocks(_exchange(jnp.concatenate([sh, rep], axis=1), "grad_exchange"), "grad_sum")
    g_sh = summed[:n_sh]
    g_rep = _all_gather(summed[n_sh:], "gather_rep_grads").reshape(-1, LANE)[:rep_rows]

    d_sh, nm_sh, nv_sh = _adamw(_pack([wd[n] for n in SHARDED]), g_sh, _pack([md[n] for n in SHARDED]),
                                _pack([vd[n] for n in SHARDED]), "adamw_sharded")
    d_rep, nm_rep, nv_rep = _adamw(_pack([wd[n] for n in REPLICATED]), g_rep, _pack([md[n] for n in REPLICATED]),
                                   _pack([vd[n] for n in REPLICATED]), "adamw_replicated")
    sh_shapes = [wd[n].shape for n in SHARDED]
    outs = {}
    for kind, f_sh, f_rep in (("grad", g_sh, g_rep), ("delta", d_sh, d_rep), ("new_m", nm_sh, nm_rep), ("new_v", nv_sh, nv_rep)):
        for n, a in zip(SHARDED, _unpack(f_sh, sh_shapes)):
            outs[kind, n] = a
        for n, a in zip(REPLICATED, _unpack(f_rep, rep_shapes)):
            outs[kind, n] = a
    res = [loss, grad_x[None]]
    for kind in ("grad", "delta", "new_m", "new_v"):
        res += [outs[kind, n] for n in WEIGHTS]
    return tuple(res)
```

```python
import functools
import math

import jax
import jax.numpy as jnp
from jax import lax
from jax.experimental import pallas as pl
from jax.experimental.pallas import tpu as pltpu

F32 = jnp.float32
BF16 = jnp.bfloat16

N_DEV = 8
D_MODEL = 1024
N_META = 16
HEADS = 8
QK_NOPE = 128
QK_ROPE = 64
V_HEAD = 128
Q_RANK = 256
KV_RANK = 128
ROPE_THETA = 10000.0
LRU_BLOCKS = 8
LRU_C = 8.0
D_FF = 2816
DEPTH = 2
DN_ALPHA = (2.0 * DEPTH) ** 0.25
LN_EPS = 1e-5
RMS_EPS = 1e-6
ATT_SCALE = 1.0 / math.sqrt(QK_NOPE + QK_ROPE)
NEG_BIG = -1e30

ADAM_LR = 0.001
ADAM_B1 = 0.9
ADAM_B2 = 0.999
ADAM_EPS = 1e-08
ADAM_WD = 0.01
ADAM_STEP = 10

MIN_PAD_ROWS = 2
LANE = 128
SUBLANE = 8
VMEM_LIMIT = 56 * 1024 * 1024

PROJ_COLS = 4 * D_MODEL + Q_RANK + KV_RANK + 2 * QK_ROPE
C_LRU_G, C_LRU_X, C_G_MLA, C_G_LRU = 0, D_MODEL, 2 * D_MODEL, 3 * D_MODEL
C_CQ = 4 * D_MODEL
C_CKV = C_CQ + Q_RANK
C_KRP = C_CKV + KV_RANK

WEIGHTS = ['meta_tokens', 'ln0_g', 'ln0_b', 'w_in', 'q_norm', 'kv_norm', 'w_uq', 'w_uk', 'w_uv', 'w_o_mla',
           'lru_conv_w', 'lru_conv_b', 'w_rg', 'b_rg', 'w_ig', 'b_ig', 'lru_lambda', 'w_o_lru', 'w_out',
           'ln1_g', 'ln1_b', 'w_up', 'ffn_conv_w', 'ffn_conv_b', 'w_down', 'ln2_g', 'ln2_b']
SHARD_AXIS = {'meta_tokens': 1, 'w_in': 2, 'w_uq': 1, 'w_o_mla': 1, 'lru_conv_w': 2, 'b_rg': 2, 'b_ig': 2,
              'lru_lambda': 2, 'w_o_lru': 1, 'w_out': 1, 'w_up': 2, 'ffn_conv_w': 2, 'w_down': 1}
BIG = ['w_in', 'w_uq', 'w_o_mla', 'w_o_lru', 'w_out', 'w_up', 'w_down']
SHARDED = [n for n in WEIGHTS if n in SHARD_AXIS]
REPLICATED = [n for n in WEIGHTS if n not in SHARD_AXIS]
LARGE_REPLICATED = ['w_uk', 'w_uv', 'w_rg', 'w_ig']


def _cparams(*sem):
    return pltpu.CompilerParams(dimension_semantics=sem, vmem_limit_bytes=VMEM_LIMIT)


def _tile(n, cap, unit=LANE):
    best = None
    t = unit
    while t <= min(n, cap):
        if n % t == 0:
            best = t
        t += unit
    return n if best is None else best


def _sigmoid(x):
    return 1.0 / (1.0 + jnp.exp(-x))


_GELU_C = math.sqrt(2.0 / math.pi)


def _gelu(x):
    t = jnp.tanh(_GELU_C * (x + 0.044715 * x * x * x))
    return 0.5 * x * (1.0 + t)


def _gelu_and_grad(x):
    t = jnp.tanh(_GELU_C * (x + 0.044715 * x * x * x))
    g = 0.5 * x * (1.0 + t)
    dg = 0.5 * (1.0 + t) + 0.5 * x * (1.0 - t * t) * _GELU_C * (1.0 + 3.0 * 0.044715 * x * x)
    return g, dg


def _softplus_neg(lam):
    z = jnp.exp(-jnp.abs(lam))
    w = 1.0 + z
    log1p = jnp.where(w == 1.0, z, jnp.log(w) * z / (w - 1.0))
    return jnp.maximum(-lam, 0.0) + log1p


def _row_ids(shape, row0=0):
    return lax.broadcasted_iota(jnp.int32, shape, 0) + row0


def _matmul(a, b, name, ta=False, tb=False, out_dtype=F32, tm_cap=1408, tn_cap=1024, tk_cap=2048):
    if ta:
        kdim, m = a.shape
    else:
        m, kdim = a.shape
    if tb:
        n, k2 = b.shape
    else:
        k2, n = b.shape
    assert kdim == k2, (a.shape, b.shape, ta, tb)
    tm, tn, tk = _tile(m, tm_cap), _tile(n, tn_cap), _tile(kdim, tk_cap)
    nk = kdim // tk

    def body(a_ref, b_ref, o_ref, acc_ref):
        k = pl.program_id(2)

        @pl.when(k == 0)
        def _():
            acc_ref[...] = jnp.zeros_like(acc_ref)

        dn = (((0 if ta else 1,), (1 if tb else 0,)), ((), ()))
        acc_ref[...] += lax.dot_general(a_ref[...].astype(BF16), b_ref[...].astype(BF16), dn,
                                        preferred_element_type=F32)

        @pl.when(k == nk - 1)
        def _():
            o_ref[...] = acc_ref[...].astype(o_ref.dtype)

    a_spec = pl.BlockSpec((tk, tm), lambda i, j, k: (k, i)) if ta else pl.BlockSpec((tm, tk), lambda i, j, k: (i, k))
    b_spec = pl.BlockSpec((tn, tk), lambda i, j, k: (j, k)) if tb else pl.BlockSpec((tk, tn), lambda i, j, k: (k, j))
    return pl.pallas_call(
        body, name=name,
        grid=(m // tm, n // tn, nk),
        in_specs=[a_spec, b_spec],
        out_specs=pl.BlockSpec((tm, tn), lambda i, j, k: (i, j)),
        out_shape=jax.ShapeDtypeStruct((m, n), out_dtype),
        scratch_shapes=[pltpu.VMEM((tm, tn), F32)],
        compiler_params=_cparams("parallel", "parallel", "arbitrary"),
    )(a, b)


class Rw:
    def __init__(self, arr, width=None, cb=0):
        self.arr, self.width, self.cb = arr, (arr.shape[1] if width is None else width), cb


class Pm:
    def __init__(self, arr):
        self.arr = arr


def _rows(fn, name, ins, outs, accs=(), tm_cap=384):
    tp = next(o.arr.shape[0] for o in ins if isinstance(o, Rw))
    tm = _tile(tp, tm_cap)
    n_in, n_out, n_acc = len(ins), len(outs), len(accs)

    def body(*refs):
        i = pl.program_id(0)
        res = fn(i * tm, *[r[...] for r in refs[:n_in]])
        if not isinstance(res, (tuple, list)):
            res = (res,)
        assert len(res) == n_out + n_acc, (name, len(res))
        for k in range(n_out):
            refs[n_in + k][...] = res[k].astype(refs[n_in + k].dtype)
        for k in range(n_acc):
            ref = refs[n_in + n_out + k]

            @pl.when(i == 0)
            def _():
                ref[...] = jnp.zeros_like(ref)

            ref[...] += res[n_out + k]

    in_specs = []
    for o in ins:
        if isinstance(o, Rw):
            in_specs.append(pl.BlockSpec((tm, o.width), functools.partial(lambda i, cb: (i, cb), cb=o.cb)))
        else:
            in_specs.append(pl.BlockSpec(o.arr.shape, functools.partial(lambda i, nd: (0,) * nd, nd=o.arr.ndim)))
    out_specs = [pl.BlockSpec((tm, w), lambda i: (i, 0)) for (w, _) in outs]
    out_specs += [pl.BlockSpec(s, functools.partial(lambda i, nd: (0,) * nd, nd=len(s))) for s in accs]
    out_shape = [jax.ShapeDtypeStruct((tp, w), dt) for (w, dt) in outs]
    out_shape += [jax.ShapeDtypeStruct(s, F32) for s in accs]
    res = pl.pallas_call(
        body, name=name, grid=(tp // tm,), in_specs=in_specs, out_specs=out_specs, out_shape=out_shape,
        compiler_params=_cparams("arbitrary"),
    )(*[o.arr for o in ins])
    return res


class Cl:
    def __init__(self, arr, col0=0):
        self.arr, self.col0 = arr, col0


def _cols(fn, name, ins, outs, ncols, tc):
    assert ncols % tc == 0
    n_in, n_out = len(ins), len(outs)

    def body(*refs):
        res = fn(*[r[...] for r in refs[:n_in]])
        if not isinstance(res, (tuple, list)):
            res = (res,)
        assert len(res) == n_out, (name, len(res))
        for k in range(n_out):
            refs[n_in + k][...] = res[k].astype(refs[n_in + k].dtype)

    in_specs = []
    for o in ins:
        assert o.col0 % tc == 0, (name, o.col0, tc)
        in_specs.append(pl.BlockSpec((o.arr.shape[0], tc), functools.partial(lambda j, off: (0, j + off), off=o.col0 // tc)))
    out_specs = [pl.BlockSpec((r, tc), lambda j: (0, j)) for (r, _) in outs]
    out_shape = [jax.ShapeDtypeStruct((r, ncols), dt) for (r, dt) in outs]
    return pl.pallas_call(
        body, name=name, grid=(ncols // tc,), in_specs=in_specs, out_specs=out_specs, out_shape=out_shape,
        compiler_params=_cparams("parallel"),
    )(*[o.arr for o in ins])


def _ln_stats(u):
    mu = jnp.mean(u, axis=-1, keepdims=True)
    xc = u - mu
    var = jnp.mean(xc * xc, axis=-1, keepdims=True)
    rstd = lax.rsqrt(var + LN_EPS)
    return xc * rstd, rstd


def _ln_fwd(terms, g, b, name):
    coefs = [c for c, _ in terms]

    def fn(row0, *blk):
        xs, (gg, bb) = blk[:len(coefs)], blk[len(coefs):]
        u = sum(c * x for c, x in zip(coefs, xs))
        xhat, _ = _ln_stats(u)
        return xhat * gg + bb

    d = terms[0][1].shape[1]
    return _rows(fn, name, [Rw(x) for _, x in terms] + [Pm(g.reshape(1, d)), Pm(b.reshape(1, d))], [(d, F32)])[0]


def _ln_bwd(dy_terms, u_terms, g, name):
    dc = [c for c, _ in dy_terms]
    uc = [c for c, _ in u_terms]
    d = u_terms[0][1].shape[1]

    def fn(row0, *blk):
        dys = blk[:len(dc)]
        xs = blk[len(dc):len(dc) + len(uc)]
        gg = blk[-1]
        dy = sum(c * x for c, x in zip(dc, dys))
        u = sum(c * x for c, x in zip(uc, xs))
        xhat, rstd = _ln_stats(u)
        gdy = dy * gg
        m1 = jnp.mean(gdy, axis=-1, keepdims=True)
        m2 = jnp.mean(gdy * xhat, axis=-1, keepdims=True)
        du = rstd * (gdy - m1 - xhat * m2)
        return du, jnp.sum(dy * xhat, axis=0, keepdims=True), jnp.sum(dy, axis=0, keepdims=True)

    ins = [Rw(x) for _, x in dy_terms] + [Rw(x) for _, x in u_terms] + [Pm(g.reshape(1, d))]
    return _rows(fn, name, ins, [(d, F32)], accs=[(1, d), (1, d)])


def _loss_head(y, tgt, t_real, name):
    d = y.shape[1]

    def fn(row0, yb, tb):
        rows = _row_ids(yb.shape, row0)
        live = (rows >= N_META) & (rows < t_real)
        diff = jnp.where(live, yb - tb, 0.0)
        return diff * (1.0 / d), jnp.sum(diff * diff, axis=0, keepdims=True) * (0.5 / d)

    return _rows(fn, name, [Rw(y), Rw(tgt)], [(d, F32)], accs=[(1, d)])


def _rms(x, g):
    r = lax.rsqrt(jnp.mean(x * x, axis=-1, keepdims=True) + RMS_EPS)
    return x * r * g


def _rms_bwd(dy, x, g):
    r = lax.rsqrt(jnp.mean(x * x, axis=-1, keepdims=True) + RMS_EPS)
    gdy = dy * g
    dx = r * gdy - x * (r * r * r) * jnp.mean(gdy * x, axis=-1, keepdims=True)
    return dx, jnp.sum(dy * x * r, axis=0, keepdims=True)


def _mla_norms(proj, qn, kvn, name):
    def fn(row0, cq, ckv, g1, g2):
        return _rms(cq, g1), _rms(ckv, g2)

    return _rows(fn, name, [Rw(proj, Q_RANK, C_CQ // Q_RANK), Rw(proj, KV_RANK, C_CKV // KV_RANK),
                            Pm(qn.reshape(1, Q_RANK)), Pm(kvn.reshape(1, KV_RANK))],
                 [(Q_RANK, F32), (KV_RANK, F32)])


def _mla_norms_bwd(dcqn, dckvn, proj, qn, kvn, name):
    def fn(row0, d1, d2, cq, ckv, g1, g2):
        dx1, dg1 = _rms_bwd(d1, cq, g1)
        dx2, dg2 = _rms_bwd(d2, ckv, g2)
        return dx1, dx2, dg1, dg2

    return _rows(fn, name, [Rw(dcqn), Rw(dckvn), Rw(proj, Q_RANK, C_CQ // Q_RANK), Rw(proj, KV_RANK, C_CKV // KV_RANK),
                            Pm(qn.reshape(1, Q_RANK)), Pm(kvn.reshape(1, KV_RANK))],
                 [(Q_RANK, F32), (KV_RANK, F32)], accs=[(1, Q_RANK), (1, KV_RANK)])


def _fold_rope(z):
    return z + pltpu.roll(z, QK_ROPE, 1)


def _mla_pack(qext, kv, proj, cs, name):
    tp = qext.shape[0]
    tm = _tile(tp, 384)

    def body(q_ref, kn_ref, v_ref, kr_ref, cs_ref, qo_ref, ko_ref, vo_ref):
        cs_ = cs_ref[...]
        low = lax.broadcasted_iota(jnp.int32, cs_.shape, 1) < QK_ROPE
        q = q_ref[...]
        qr = jnp.where(low, _fold_rope(q[:, QK_NOPE:] * cs_), 0.0)
        qo_ref[...] = (jnp.concatenate([q[:, :QK_NOPE], qr], axis=1) * ATT_SCALE).astype(BF16)
        kr = _fold_rope(kr_ref[...] * cs_)
        ko_ref[...] = jnp.concatenate([kn_ref[...], kr], axis=1).astype(BF16)
        vo_ref[...] = v_ref[...].astype(BF16)

    return pl.pallas_call(
        body, name=name, grid=(tp // tm, HEADS),
        in_specs=[pl.BlockSpec((tm, 2 * LANE), lambda i, h: (i, h)),
                  pl.BlockSpec((tm, LANE), lambda i, h: (i, h)),
                  pl.BlockSpec((tm, LANE), lambda i, h: (i, HEADS + h)),
                  pl.BlockSpec((tm, LANE), lambda i, h: (i, C_KRP // LANE)),
                  pl.BlockSpec((tm, LANE), lambda i, h: (i, 0))],
        out_specs=[pl.BlockSpec((tm, 2 * LANE), lambda i, h: (i, h)),
                   pl.BlockSpec((tm, 2 * LANE), lambda i, h: (i, h)),
                   pl.BlockSpec((tm, LANE), lambda i, h: (i, h))],
        out_shape=[jax.ShapeDtypeStruct((tp, HEADS * 2 * LANE), BF16),
                   jax.ShapeDtypeStruct((tp, HEADS * 2 * LANE), BF16),
                   jax.ShapeDtypeStruct((tp, HEADS * LANE), BF16)],
        compiler_params=_cparams("parallel", "arbitrary"),
    )(qext, kv, kv, proj, cs)


def _mla_unpack(dq, dk, cs, name):
    tp = dq.shape[0]
    tm = _tile(tp, 384)

    def body(dq_ref, dk_ref, cs_ref, dqe_ref, dkn_ref, dkr_ref):
        h = pl.program_id(1)
        cs_ = cs_ref[...]
        low = lax.broadcasted_iota(jnp.int32, cs_.shape, 1) < QK_ROPE
        dq_ = dq_ref[...] * ATT_SCALE
        dqr = _fold_rope(jnp.where(low, dq_[:, QK_NOPE:], 0.0)) * cs_
        dqe_ref[...] = jnp.concatenate([dq_[:, :QK_NOPE], dqr], axis=1)
        dk_ = dk_ref[...]
        dkn_ref[...] = dk_[:, :QK_NOPE]

        @pl.when(h == 0)
        def _():
            dkr_ref[...] = jnp.zeros_like(dkr_ref)

        dkr_ref[...] += _fold_rope(jnp.where(low, dk_[:, QK_NOPE:], 0.0)) * cs_

    return pl.pallas_call(
        body, name=name, grid=(tp // tm, HEADS),
        in_specs=[pl.BlockSpec((tm, 2 * LANE), lambda i, h: (i, h)),
                  pl.BlockSpec((tm, 2 * LANE), lambda i, h: (i, h)),
                  pl.BlockSpec((tm, LANE), lambda i, h: (i, 0))],
        out_specs=[pl.BlockSpec((tm, 2 * LANE), lambda i, h: (i, h)),
                   pl.BlockSpec((tm, LANE), lambda i, h: (i, h)),
                   pl.BlockSpec((tm, LANE), lambda i, h: (i, 0))],
        out_shape=[jax.ShapeDtypeStruct((tp, HEADS * 2 * LANE), F32),
                   jax.ShapeDtypeStruct((tp, HEADS * LANE), F32),
                   jax.ShapeDtypeStruct((tp, LANE), F32)],
        compiler_params=_cparams("parallel", "arbitrary"),
    )(dq, dk, cs)


def _attn_fwd(q, k, v, t_real, name):
    tp = q.shape[0]
    tq = _tile(tp, 384)

    def body(q_ref, k_ref, v_ref, o_ref, lse_ref):
        s = lax.dot_general(q_ref[...], k_ref[...], (((1,), (1,)), ((), ())), preferred_element_type=F32)
        cols = lax.broadcasted_iota(jnp.int32, s.shape, 1)
        s = jnp.where(cols < t_real, s, NEG_BIG)
        m = jnp.max(s, axis=-1, keepdims=True)
        p = jnp.exp(s - m)
        l = jnp.sum(p, axis=-1, keepdims=True)
        o = jnp.dot(p.astype(BF16), v_ref[...], preferred_element_type=F32)
        o_ref[...] = o / l
        lse_ref[...] = m + jnp.log(l)

    return pl.pallas_call(
        body, name=name, grid=(HEADS, tp // tq),
        in_specs=[pl.BlockSpec((tq, 2 * LANE), lambda h, i: (i, h)),
                  pl.BlockSpec((tp, 2 * LANE), lambda h, i: (0, h)),
                  pl.BlockSpec((tp, LANE), lambda h, i: (0, h))],
        out_specs=[pl.BlockSpec((tq, LANE), lambda h, i: (i, h)),
                   pl.BlockSpec((None, tq, 1), lambda h, i: (h, i, 0))],
        out_shape=[jax.ShapeDtypeStruct((tp, HEADS * LANE), F32),
                   jax.ShapeDtypeStruct((HEADS, tp, 1), F32)],
        compiler_params=_cparams("parallel", "parallel"),
    )(q, k, v)


def _attn_bwd(q, k, v, do, o, lse, t_real, name):
    tp = q.shape[0]
    tq = _tile(tp, 128)

    def body(q_ref, k_ref, v_ref, do_ref, o_ref, lse_ref, dq_ref, dk_ref, dv_ref):
        i = pl.program_id(1)
        qb, kb = q_ref[...], k_ref[...]
        s = lax.dot_general(qb, kb, (((1,), (1,)), ((), ())), preferred_element_type=F32)
        cols = lax.broadcasted_iota(jnp.int32, s.shape, 1)
        p = jnp.where(cols < t_real, jnp.exp(s - lse_ref[...]), 0.0)
        dob = do_ref[...]
        dob16 = dob.astype(BF16)
        dp = lax.dot_general(dob16, v_ref[...], (((1,), (1,)), ((), ())), preferred_element_type=F32)
        delta = jnp.sum(dob * o_ref[...], axis=-1, keepdims=True)
        ds = (p * (dp - delta)).astype(BF16)
        dq_ref[...] = jnp.dot(ds, kb, preferred_element_type=F32)

        @pl.when(i == 0)
        def _():
            dk_ref[...] = jnp.zeros_like(dk_ref)
            dv_ref[...] = jnp.zeros_like(dv_ref)

        dk_ref[...] += lax.dot_general(ds, qb, (((0,), (0,)), ((), ())), preferred_element_type=F32)
        dv_ref[...] += lax.dot_general(p.astype(BF16), dob16, (((0,), (0,)), ((), ())), preferred_element_type=F32)

    return pl.pallas_call(
        body, name=name, grid=(HEADS, tp // tq),
        in_specs=[pl.BlockSpec((tq, 2 * LANE), lambda h, i: (i, h)),
                  pl.BlockSpec((tp, 2 * LANE), lambda h, i: (0, h)),
                  pl.BlockSpec((tp, LANE), lambda h, i: (0, h)),
                  pl.BlockSpec((tq, LANE), lambda h, i: (i, h)),
                  pl.BlockSpec((tq, LANE), lambda h, i: (i, h)),
                  pl.BlockSpec((None, tq, 1), lambda h, i: (h, i, 0))],
        out_specs=[pl.BlockSpec((tq, 2 * LANE), lambda h, i: (i, h)),
                   pl.BlockSpec((tp, 2 * LANE), lambda h, i: (0, h)),
                   pl.BlockSpec((tp, LANE), lambda h, i: (0, h))],
        out_shape=[jax.ShapeDtypeStruct((tp, HEADS * 2 * LANE), F32),
                   jax.ShapeDtypeStruct((tp, HEADS * 2 * LANE), F32),
                   jax.ShapeDtypeStruct((tp, HEADS * LANE), F32)],
        compiler_params=_cparams("parallel", "arbitrary"),
    )(q, k, v, do, o, lse)


def _shift_rows(x, s):
    tp = x.shape[0]
    return x if s % tp == 0 else pltpu.roll(x, s % tp, 0)


def _conv_fwd_val(xm, w, b, pad_left):
    acc = b + w[0:1, :] * _shift_rows(xm, pad_left)
    for k in range(1, w.shape[0]):
        acc = acc + w[k:k + 1, :] * _shift_rows(xm, pad_left - k)
    return acc


def _conv_bwd_val(dy, xm, w, pad_left, live):
    kk = w.shape[0]
    dx = w[0:1, :] * _shift_rows(dy, -pad_left)
    dws = [jnp.sum(dy * _shift_rows(xm, pad_left), axis=0, keepdims=True)]
    for k in range(1, kk):
        dx = dx + w[k:k + 1, :] * _shift_rows(dy, k - pad_left)
        dws.append(jnp.sum(dy * _shift_rows(xm, pad_left - k), axis=0, keepdims=True))
    return jnp.where(live, dx, 0.0), jnp.concatenate(dws, axis=0), jnp.sum(dy, axis=0, keepdims=True)


def _lru_conv_fwd(proj, w, b, t_real, name):
    def fn(x, ww, bb):
        xm = jnp.where(_row_ids(x.shape) < t_real, x, 0.0)
        return _conv_fwd_val(xm, ww, bb, 2)

    return _cols(fn, name, [Cl(proj, C_LRU_X), Cl(w), Cl(b.reshape(1, -1))], [(proj.shape[0], F32)], D_MODEL, 128)[0]


def _lru_conv_bwd(dxc, proj, w, t_real, name):
    def fn(dy, x, ww):
        live = _row_ids(x.shape) < t_real
        xm = jnp.where(live, x, 0.0)
        dym = jnp.where(live, dy, 0.0)
        return _conv_bwd_val(dym, xm, ww, 2, live)

    return _cols(fn, name, [Cl(dxc), Cl(proj, C_LRU_X), Cl(w)],
                 [(proj.shape[0], F32), (w.shape[0], F32), (1, F32)], D_MODEL, 128)


def _ffn_conv_act(up, w, b, t_real, name):
    def fn(g, v, wg, wv, bg, bv):
        live = _row_ids(g.shape) < t_real
        gc = _conv_fwd_val(jnp.where(live, g, 0.0), wg, bg, 1)
        vc = _conv_fwd_val(jnp.where(live, v, 0.0), wv, bv, 1)
        return _gelu(gc) * vc

    b2 = b.reshape(1, -1)
    return _cols(fn, name, [Cl(up), Cl(up, D_FF), Cl(w), Cl(w, D_FF), Cl(b2), Cl(b2, D_FF)],
                 [(up.shape[0], F32)], D_FF, 128)[0]


def _ffn_conv_act_bwd(dm, up, w, b, t_real, name):
    def fn(dmb, g, v, wg, wv, bg, bv):
        live = _row_ids(g.shape) < t_real
        gm, vm = jnp.where(live, g, 0.0), jnp.where(live, v, 0.0)
        gc = _conv_fwd_val(gm, wg, bg, 1)
        vc = _conv_fwd_val(vm, wv, bv, 1)
        act, dact = _gelu_and_grad(gc)
        dmm = jnp.where(live, dmb, 0.0)
        dgx, dwg, dbg = _conv_bwd_val(dmm * vc * dact, gm, wg, 1, live)
        dvx, dwv, dbv = _conv_bwd_val(dmm * act, vm, wv, 1, live)
        return dgx, dvx, dwg, dwv, dbg, dbv

    b2 = b.reshape(1, -1)
    tp, kk = up.shape[0], w.shape[0]
    return _cols(fn, name, [Cl(dm), Cl(up), Cl(up, D_FF), Cl(w), Cl(w, D_FF), Cl(b2), Cl(b2, D_FF)],
                 [(tp, F32), (tp, F32), (kk, F32), (kk, F32), (1, F32), (1, F32)], D_FF, 128)


def _lru_gates_fwd(xc, wg, b4, lam, t_real, name):
    tp = xc.shape[0]
    tm = _tile(tp, 1408)

    def body(x_ref, w_ref, b_ref, lam_ref, r0_ref, r1_ref, i0_ref, i1_ref, a0_ref, a1_ref, u0_ref, u1_ref):
        x = x_ref[...]
        xb = x.astype(BF16)
        live = _row_ids(x.shape, pl.program_id(1) * tm) < t_real
        bb = b_ref[...]
        sp = _softplus_neg(lam_ref[...])
        gate = [_sigmoid(jnp.dot(xb, w_ref[k], preferred_element_type=F32) + bb[k:k + 1, :]) for k in range(4)]
        for d, (r_ref, i_ref, a_ref, u_ref) in enumerate(((r0_ref, i0_ref, a0_ref, u0_ref), (r1_ref, i1_ref, a1_ref, u1_ref))):
            r, ig = gate[d], gate[2 + d]
            a = jnp.exp(-LRU_C * r * sp[d:d + 1, :])
            r_ref[...] = r
            i_ref[...] = ig
            a_ref[...] = a
            u_ref[...] = jnp.where(live, jnp.sqrt(1.0 - a * a) * (ig * x), 0.0)

    blk = pl.BlockSpec((tm, LANE), lambda g, i: (i, g))
    return pl.pallas_call(
        body, name=name, grid=(LRU_BLOCKS, tp // tm),
        in_specs=[blk, pl.BlockSpec((None, 4, LANE, LANE), lambda g, i: (g, 0, 0, 0)),
                  pl.BlockSpec((4, LANE), lambda g, i: (0, g)), pl.BlockSpec((2, LANE), lambda g, i: (0, g))],
        out_specs=[blk] * 8,
        out_shape=[jax.ShapeDtypeStruct((tp, D_MODEL), F32)] * 8,
        compiler_params=_cparams("parallel", "parallel"),
    )(xc, wg, b4, lam)


def _lru_gates_bwd(l0, l1, da0, da1, r0, r1, i0, i1, a0, a1, xc, wg, lam, t_real, name):
    tp = xc.shape[0]
    tm = _tile(tp, 1408)

    def body(l0_ref, l1_ref, da0_ref, da1_ref, r0_ref, r1_ref, i0_ref, i1_ref, a0_ref, a1_ref, x_ref, w_ref, lam_ref,
             dx_ref, dw_ref, db_ref, dlam_ref):
        i = pl.program_id(1)
        x = x_ref[...]
        xb = x.astype(BF16)
        live = _row_ids(x.shape, i * tm) < t_real
        lam_ = lam_ref[...]
        sp = _softplus_neg(lam_)
        dsp_dlam = -_sigmoid(-lam_)
        dx = jnp.zeros_like(x)
        dpre = [None] * 4
        dlam_rows = []
        for d, (l_ref, da_ref, r_ref, i_ref, a_ref) in enumerate(((l0_ref, da0_ref, r0_ref, i0_ref, a0_ref),
                                                                  (l1_ref, da1_ref, r1_ref, i1_ref, a1_ref))):
            r, ig, a = r_ref[...], i_ref[...], a_ref[...]
            du = jnp.where(live, l_ref[...], 0.0)
            s = jnp.sqrt(1.0 - a * a)
            dv = du * s
            ds = du * (ig * x)
            dla = jnp.where(live, da_ref[...], 0.0) * a - ds * (a * a) / s
            dla = jnp.where(live, dla, 0.0)
            dr = dla * (-LRU_C) * sp[d:d + 1, :]
            dlam_rows.append(jnp.sum(dla * (-LRU_C) * r, axis=0, keepdims=True) * dsp_dlam[d:d + 1, :])
            dpre[d] = dr * r * (1.0 - r)
            dpre[2 + d] = dv * x * ig * (1.0 - ig)
            dx = dx + dv * ig

        @pl.when(i == 0)
        def _():
            dw_ref[...] = jnp.zeros_like(dw_ref)
            db_ref[...] = jnp.zeros_like(db_ref)
            dlam_ref[...] = jnp.zeros_like(dlam_ref)

        for k in range(4):
            pk = dpre[k].astype(BF16)
            dx = dx + lax.dot_general(pk, w_ref[k], (((1,), (1,)), ((), ())), preferred_element_type=F32)
            dw_ref[k] += lax.dot_general(xb, pk, (((0,), (0,)), ((), ())), preferred_element_type=F32)
        db_ref[...] += jnp.concatenate([jnp.sum(p, axis=0, keepdims=True) for p in dpre], axis=0)
        dlam_ref[...] += jnp.concatenate(dlam_rows, axis=0)
        dx_ref[...] = dx

    blk = pl.BlockSpec((tm, LANE), lambda g, i: (i, g))
    return pl.pallas_call(
        body, name=name, grid=(LRU_BLOCKS, tp // tm),
        in_specs=[blk] * 11 + [pl.BlockSpec((None, 4, LANE, LANE), lambda g, i: (g, 0, 0, 0)),
                               pl.BlockSpec((2, LANE), lambda g, i: (0, g))],
        out_specs=[blk, pl.BlockSpec((None, 4, LANE, LANE), lambda g, i: (g, 0, 0, 0)),
                   pl.BlockSpec((4, LANE), lambda g, i: (0, g)), pl.BlockSpec((2, LANE), lambda g, i: (0, g))],
        out_shape=[jax.ShapeDtypeStruct((tp, D_MODEL), F32), jax.ShapeDtypeStruct((LRU_BLOCKS, 4, LANE, LANE), F32),
                   jax.ShapeDtypeStruct((4, D_MODEL), F32), jax.ShapeDtypeStruct((2, D_MODEL), F32)],
        compiler_params=_cparams("parallel", "arbitrary"),
    )(l0, l1, da0, da1, r0, r1, i0, i1, a0, a1, xc, wg, lam)


def _tile_scan(a, u, reverse):
    rows = lax.broadcasted_iota(jnp.int32, a.shape, 0)
    for s in (1, 2, 4):
        if reverse:
            keep = rows < SUBLANE - s
            a_sh, u_sh = pltpu.roll(a, SUBLANE - s, 0), pltpu.roll(u, SUBLANE - s, 0)
        else:
            keep = rows >= s
            a_sh, u_sh = pltpu.roll(a, s, 0), pltpu.roll(u, s, 0)
        u = u + a * jnp.where(keep, u_sh, 0.0)
        a = a * jnp.where(keep, a_sh, 1.0)
    return a, u


def _scan_fwd(a0, u0, a1, u1, name):
    tp, d = a0.shape
    tc = 128
    nt = tp // SUBLANE

    def body(a0_ref, u0_ref, a1_ref, u1_ref, h0_ref, h1_ref):
        def step(t, carry):
            c0, c1 = carry
            f = pl.multiple_of(t * SUBLANE, SUBLANE)
            b = pl.multiple_of((nt - 1 - t) * SUBLANE, SUBLANE)
            pa, pu = _tile_scan(a0_ref[pl.ds(f, SUBLANE), :], u0_ref[pl.ds(f, SUBLANE), :], False)
            h = pu + pa * c0
            h0_ref[pl.ds(f, SUBLANE), :] = h
            c0 = h[SUBLANE - 1:SUBLANE, :]
            pa, pu = _tile_scan(a1_ref[pl.ds(b, SUBLANE), :], u1_ref[pl.ds(b, SUBLANE), :], True)
            h = pu + pa * c1
            h1_ref[pl.ds(b, SUBLANE), :] = h
            c1 = h[0:1, :]
            return c0, c1

        z = jnp.zeros((1, tc), F32)
        lax.fori_loop(0, nt, step, (z, z))

    blk = pl.BlockSpec((tp, tc), lambda j: (0, j))
    return pl.pallas_call(
        body, name=name, grid=(d // tc,), in_specs=[blk] * 4, out_specs=[blk] * 2,
        out_shape=[jax.ShapeDtypeStruct((tp, d), F32)] * 2,
        compiler_params=_cparams("parallel"),
    )(a0, u0, a1, u1)


def _scan_bwd(dh, a0, a1, h0, h1, name):
    tp, d = dh.shape
    tc = 128
    nt = tp // SUBLANE

    def body(dh_ref, a0_ref, a1_ref, h0_ref, h1_ref, l0_ref, l1_ref, da0_ref, da1_ref):
        rows8 = lax.broadcasted_iota(jnp.int32, (SUBLANE, tc), 0)

        def step(t, carry):
            c0, c1 = carry
            b = pl.multiple_of((nt - 1 - t) * SUBLANE, SUBLANE)
            f = pl.multiple_of(t * SUBLANE, SUBLANE)
            a = a0_ref[pl.ds(b, SUBLANE), :]
            a_next = jnp.where(rows8 < SUBLANE - 1, pltpu.roll(a, SUBLANE - 1, 0), 1.0)
            pa, pu = _tile_scan(a_next, dh_ref[pl.ds(b, SUBLANE), :], True)
            lam = pu + pa * c0
            l0_ref[pl.ds(b, SUBLANE), :] = lam
            c0 = a[0:1, :] * lam[0:1, :]
            a = a1_ref[pl.ds(f, SUBLANE), :]
            a_prev = jnp.where(rows8 >= 1, pltpu.roll(a, 1, 0), 1.0)
            pa, pu = _tile_scan(a_prev, dh_ref[pl.ds(f, SUBLANE), :], False)
            lam = pu + pa * c1
            l1_ref[pl.ds(f, SUBLANE), :] = lam
            c1 = a[SUBLANE - 1:SUBLANE, :] * lam[SUBLANE - 1:SUBLANE, :]
            return c0, c1

        z = jnp.zeros((1, tc), F32)
        lax.fori_loop(0, nt, step, (z, z))
        rows = lax.broadcasted_iota(jnp.int32, (tp, tc), 0)
        da0_ref[...] = l0_ref[...] * jnp.where(rows >= 1, pltpu.roll(h0_ref[...], 1, 0), 0.0)
        da1_ref[...] = l1_ref[...] * jnp.where(rows < tp - 1, pltpu.roll(h1_ref[...], tp - 1, 0), 0.0)

    blk = pl.BlockSpec((tp, tc), lambda j: (0, j))
    return pl.pallas_call(
        body, name=name, grid=(d // tc,), in_specs=[blk] * 5, out_specs=[blk] * 4,
        out_shape=[jax.ShapeDtypeStruct((tp, d), F32)] * 4,
        compiler_params=_cparams("parallel"),
    )(dh, a0, a1, h0, h1)


def _gated_h(proj, h0, h1, name):
    def fn(row0, lg, x0, x1):
        return _gelu(lg) * (x0 + x1)

    return _rows(fn, name, [Rw(proj, D_MODEL, C_LRU_G // D_MODEL), Rw(h0), Rw(h1)], [(D_MODEL, F32)])[0]


def _gated_h_bwd(dgh, proj, h0, h1, name):
    def fn(row0, dg, lg, x0, x1):
        act, dact = _gelu_and_grad(lg)
        return dg * (x0 + x1) * dact, dg * act

    return _rows(fn, name, [Rw(dgh), Rw(proj, D_MODEL, C_LRU_G // D_MODEL), Rw(h0), Rw(h1)], [(D_MODEL, F32)] * 2)


def _mix(proj, y_mla, y_lru, name):
    def fn(row0, gm, gl, ym, yl):
        return _sigmoid(gm) * ym + _sigmoid(gl) * yl

    return _rows(fn, name, [Rw(proj, D_MODEL, C_G_MLA // D_MODEL), Rw(proj, D_MODEL, C_G_LRU // D_MODEL), Rw(y_mla), Rw(y_lru)],
                 [(D_MODEL, F32)])[0]


def _mix_bwd(dz, proj, y_mla, y_lru, name):
    def fn(row0, dzb, gm, gl, ym, yl):
        sm, sl = _sigmoid(gm), _sigmoid(gl)
        return dzb * sm, dzb * sl, dzb * ym * sm * (1.0 - sm), dzb * yl * sl * (1.0 - sl)

    return _rows(fn, name, [Rw(dz), Rw(proj, D_MODEL, C_G_MLA // D_MODEL), Rw(proj, D_MODEL, C_G_LRU // D_MODEL),
                            Rw(y_mla), Rw(y_lru)], [(D_MODEL, F32)] * 4)


def _layer_fwd(h, w, cs, t_real, tag):
    proj = _matmul(h, w['w_in'], tag + "proj")
    cqn, ckvn = _mla_norms(proj, w['q_norm'], w['kv_norm'], tag + "mla_norms")
    qext = _matmul(cqn, w['w_q'], tag + "q_up")
    kv = _matmul(ckvn, w['w_kv'], tag + "kv_up")
    qc, kc, vb = _mla_pack(qext, kv, proj, cs, tag + "mla_pack")
    o, lse = _attn_fwd(qc, kc, vb, t_real, tag + "attn_fwd")
    y_mla = _matmul(o, w['w_o_mla'], tag + "o_mla")
    xc = _lru_conv_fwd(proj, w['lru_conv_w'], w['lru_conv_b'], t_real, tag + "lru_conv")
    r0, r1, i0, i1, a0, a1, u0, u1 = _lru_gates_fwd(xc, w['w_g'], w['b4'], w['lru_lambda'], t_real, tag + "lru_gates")
    h0, h1 = _scan_fwd(a0, u0, a1, u1, tag + "lru_scan")
    gh = _gated_h(proj, h0, h1, tag + "lru_gate_out")
    y_lru = _matmul(gh, w['w_o_lru'], tag + "o_lru")
    z = _mix(proj, y_mla, y_lru, tag + "mix")
    zo = _matmul(z, w['w_out'], tag + "w_out")
    hm = _ln_fwd([(DN_ALPHA, h), (1.0, zo)], w['ln1_g'], w['ln1_b'], tag + "ln1")
    up = _matmul(hm, w['w_up'], tag + "w_up")
    m = _ffn_conv_act(up, w['ffn_conv_w'], w['ffn_conv_b'], t_real, tag + "ffn_conv")
    f = _matmul(m, w['w_down'], tag + "w_down", tk_cap=1408)
    out = _ln_fwd([(DN_ALPHA, hm), (1.0, f)], w['ln2_g'], w['ln2_b'], tag + "ln2")
    saved = dict(h=h, proj=proj, cqn=cqn, ckvn=ckvn, qc=qc, kc=kc, vb=vb, o=o, lse=lse, y_mla=y_mla, xc=xc,
                 r0=r0, r1=r1, i0=i0, i1=i1, a0=a0, a1=a1, h0=h0, h1=h1, gh=gh, y_lru=y_lru, z=z, zo=zo, hm=hm,
                 up=up, m=m, f=f)
    return out, saved


def _layer_bwd(dout_terms, w, s, cs, t_real, tag):
    g = {}
    du2, dg2, db2 = _ln_bwd(dout_terms, [(DN_ALPHA, s['hm']), (1.0, s['f'])], w['ln2_g'], tag + "ln2_bwd")
    g['ln2_g'], g['ln2_b'] = dg2, db2
    dm = _matmul(du2, w['w_down'], tag + "w_down_dx", tb=True)
    g['w_down'] = _matmul(s['m'], du2, tag + "w_down_dw", ta=True, tk_cap=1408)
    dgp, dvp, dwg_, dwv_, dbg_, dbv_ = _ffn_conv_act_bwd(dm, s['up'], w['ffn_conv_w'], w['ffn_conv_b'], t_real, tag + "ffn_conv_bwd")
    g['ffn_conv_w'] = jnp.concatenate([dwg_, dwv_], axis=1)
    g['ffn_conv_b'] = jnp.concatenate([dbg_, dbv_], axis=1)
    dup = jnp.concatenate([dgp, dvp], axis=1)
    dhm_mm = _matmul(dup, w['w_up'], tag + "w_up_dx", tb=True, tk_cap=1408)
    g['w_up'] = _matmul(s['hm'], dup, tag + "w_up_dw", ta=True, tk_cap=1408)
    du1, dg1, db1 = _ln_bwd([(DN_ALPHA, du2), (1.0, dhm_mm)], [(DN_ALPHA, s['h']), (1.0, s['zo'])], w['ln1_g'], tag + "ln1_bwd")
    g['ln1_g'], g['ln1_b'] = dg1, db1
    dz = _matmul(du1, w['w_out'], tag + "w_out_dx", tb=True)
    g['w_out'] = _matmul(s['z'], du1, tag + "w_out_dw", ta=True, tk_cap=1408)
    dy_mla, dy_lru, dg_mla, dg_lru = _mix_bwd(dz, s['proj'], s['y_mla'], s['y_lru'], tag + "mix_bwd")
    do = _matmul(dy_mla, w['w_o_mla'], tag + "o_mla_dx", tb=True)
    g['w_o_mla'] = _matmul(s['o'], dy_mla, tag + "o_mla_dw", ta=True, tk_cap=1408)
    dqc, dkc, dv = _attn_bwd(s['qc'], s['kc'], s['vb'], do, s['o'], s['lse'], t_real, tag + "attn_bwd")
    dqext, dkn, dkrp = _mla_unpack(dqc, dkc, cs, tag + "mla_unpack")
    dkv = jnp.concatenate([dkn, dv], axis=1)
    dcqn = _matmul(dqext, w['w_q'], tag + "q_up_dx", tb=True)
    g['w_q'] = _matmul(s['cqn'], dqext, tag + "q_up_dw", ta=True, tk_cap=1408)
    dckvn = _matmul(dkv, w['w_kv'], tag + "kv_up_dx", tb=True)
    g['w_kv'] = _matmul(s['ckvn'], dkv, tag + "kv_up_dw", ta=True, tk_cap=1408)
    dcq, dckv, g['q_norm'], g['kv_norm'] = _mla_norms_bwd(dcqn, dckvn, s['proj'], w['q_norm'], w['kv_norm'], tag + "mla_norms_bwd")
    dgh = _matmul(dy_lru, w['w_o_lru'], tag + "o_lru_dx", tb=True)
    g['w_o_lru'] = _matmul(s['gh'], dy_lru, tag + "o_lru_dw", ta=True, tk_cap=1408)
    dlru_g, dhs = _gated_h_bwd(dgh, s['proj'], s['h0'], s['h1'], tag + "lru_gate_out_bwd")
    l0, l1, da0, da1 = _scan_bwd(dhs, s['a0'], s['a1'], s['h0'], s['h1'], tag + "lru_scan_bwd")
    dxc, g['w_g'], g['b4'], g['lru_lambda'] = _lru_gates_bwd(
        l0, l1, da0, da1, s['r0'], s['r1'], s['i0'], s['i1'], s['a0'], s['a1'], s['xc'], w['w_g'], w['lru_lambda'],
        t_real, tag + "lru_gates_bwd")
    dlru_x, g['lru_conv_w'], g['lru_conv_b'] = _lru_conv_bwd(dxc, s['proj'], w['lru_conv_w'], t_real, tag + "lru_conv_bwd")
    dproj = jnp.concatenate([dlru_g, dlru_x, dg_mla, dg_lru, dcq, dckv, dkrp], axis=1)
    dh_mm = _matmul(dproj, w['w_in'], tag + "proj_dx", tb=True, tk_cap=1536)
    g['w_in'] = _matmul(s['h'], dproj, tag + "proj_dw", ta=True, tk_cap=1408)
    return [(DN_ALPHA, du1), (1.0, dh_mm)], g


def _swap_halves(a, axis=-1):
    h1, h2 = jnp.split(a, 2, axis=axis)
    return jnp.concatenate([h2, h1], axis=axis)


def _layer_weights(full, l):
    w_in = full['w_in'][l]
    cq, ckv, kr, lg, lx, gm, gl = jnp.split(w_in, [256, 384, 448, 1472, 2496, 3520], axis=1)
    w = {'w_in': jnp.concatenate([lg, lx, gm, gl, cq, ckv, kr, _swap_halves(kr)], axis=1)}
    uq = full['w_uq'][l]
    w['w_q'] = jnp.concatenate([uq, _swap_halves(uq[..., QK_NOPE:])], axis=-1).reshape(Q_RANK, HEADS * 2 * LANE)
    w['w_kv'] = jnp.concatenate([full['w_uk'][l].reshape(KV_RANK, -1), full['w_uv'][l].reshape(KV_RANK, -1)], axis=1).astype(BF16)
    rg, ig = full['w_rg'][l], full['w_ig'][l]
    w['w_g'] = jnp.moveaxis(jnp.concatenate([rg, ig], axis=0), 0, 1).astype(BF16)
    w['b4'] = jnp.concatenate([full['b_rg'][l], full['b_ig'][l]], axis=0)
    for n in ('q_norm', 'kv_norm', 'w_o_mla', 'lru_conv_w', 'lru_conv_b', 'lru_lambda', 'w_o_lru', 'w_out', 'ln1_g',
              'ln1_b', 'w_up', 'ffn_conv_w', 'ffn_conv_b', 'w_down', 'ln2_g', 'ln2_b'):
        w[n] = full[n][l]
    return w


def _layer_grads(g):
    out = {}
    lg, lx, gm, gl, cq, ckv, kr, krs = jnp.split(g['w_in'], [1024, 2048, 3072, 4096, 4352, 4480, 4544], axis=1)
    out['w_in'] = jnp.concatenate([cq, ckv, kr + _swap_halves(krs), lg, lx, gm, gl], axis=1)
    gq = g['w_q'].reshape(Q_RANK, HEADS, 2 * LANE)
    out['w_uq'] = jnp.concatenate([gq[..., :QK_NOPE], gq[..., QK_NOPE:QK_NOPE + QK_ROPE] + _swap_halves(gq[..., QK_NOPE + QK_ROPE:])], axis=-1)
    out['w_uk'] = g['w_kv'][:, :HEADS * QK_NOPE].reshape(KV_RANK, HEADS, QK_NOPE)
    out['w_uv'] = g['w_kv'][:, HEADS * QK_NOPE:].reshape(KV_RANK, HEADS, V_HEAD)
    gg = jnp.moveaxis(g['w_g'], 1, 0)
    out['w_rg'], out['w_ig'] = gg[:2], gg[2:]
    out['b_rg'], out['b_ig'] = g['b4'][:2], g['b4'][2:]
    for n in ('q_norm', 'kv_norm', 'lru_conv_b', 'ln1_g', 'ln1_b', 'ffn_conv_b', 'ln2_g', 'ln2_b'):
        out[n] = g[n].reshape(-1)
    for n in ('w_o_mla', 'lru_conv_w', 'lru_lambda', 'w_o_lru', 'w_out', 'w_up', 'ffn_conv_w', 'w_down'):
        out[n] = g[n]
    return out


def _rope_table(tp):
    half = QK_ROPE // 2
    inv_freq = jnp.exp(-math.log(ROPE_THETA) * jnp.arange(half, dtype=F32) / half)
    ang = jnp.arange(tp, dtype=F32)[:, None] * inv_freq[None, :]
    c, s = jnp.cos(ang), jnp.sin(ang)
    return jnp.concatenate([c, c, -s, s], axis=1)


def _local_step(x, target, full, t_pad):
    seq = x.shape[0]
    t_real = N_META + seq
    zpad = jnp.zeros((t_pad - t_real, D_MODEL), F32)
    xin = jnp.concatenate([full['meta_tokens'], x, zpad], axis=0)
    tgt = jnp.concatenate([jnp.zeros((N_META, D_MODEL), F32), target, zpad], axis=0)
    cs = _rope_table(t_pad)
    h = _ln_fwd([(1.0, xin)], full['ln0_g'], full['ln0_b'], "ln0")
    saved, lw = [], []
    for l in range(DEPTH):
        lw.append(_layer_weights(full, l))
        h, s = _layer_fwd(h, lw[l], cs, t_real, "l%d_" % l)
        saved.append(s)
    dy, lossvec = _loss_head(h, tgt, t_real, "loss_head")
    loss = jnp.sum(lossvec)
    terms = [(1.0, dy)]
    layer_g = [None] * DEPTH
    for l in reversed(range(DEPTH)):
        terms, g = _layer_bwd(terms, lw[l], saved[l], cs, t_real, "l%d_" % l)
        layer_g[l] = _layer_grads(g)
    dxin, dg0, db0 = _ln_bwd(terms, [(1.0, xin)], full['ln0_g'], "ln0_bwd")
    grads = {n: jnp.stack([layer_g[l][n] for l in range(DEPTH)]) for n in layer_g[0]}
    grads['ln0_g'], grads['ln0_b'] = dg0.reshape(-1), db0.reshape(-1)
    grads['meta_tokens'] = dxin[:N_META]
    return loss, dxin[N_META:t_real], grads


_HBM = pl.BlockSpec(memory_space=pl.ANY)


def _gather_many(shards, name):
    nt = len(shards)

    def body(*refs):
        x_refs, out_refs = refs[:nt], refs[nt:2 * nt]
        send_sems, recv_sems, local_sems = refs[2 * nt:]
        x, y, c = lax.axis_index("x"), lax.axis_index("y"), lax.axis_index("c")
        me, sibling = (x, y, c), (x, y, 1 - c)
        chips = [(1 - x, y), (x, 1 - y), (1 - x, 1 - y)]

        def copy(t, k, block, to, own=False):
            px, py, pc = block
            slot = out_refs[t].at[4 * px + 2 * py + pc]
            return pltpu.make_async_remote_copy(
                src_ref=x_refs[t] if own else slot, dst_ref=slot,
                send_sem=send_sems.at[7 * t + k], recv_sem=recv_sems.at[7 * t + k],
                device_id=to, device_id_type=pl.DeviceIdType.MESH)

        mine = [pltpu.make_async_copy(x_refs[t], out_refs[t].at[4 * x + 2 * y + c], local_sems.at[t]) for t in range(nt)]
        sent = []
        for t in range(nt):
            mine[t].start()
            first = [copy(t, 0, me, sibling, own=True)]
            first += [copy(t, 1 + j, me, (*chip, c), own=True) for j, chip in enumerate(chips)]
            for cp in first:
                cp.start()
            sent += first
        for j, chip in enumerate(chips):
            for t in range(nt):
                copy(t, 1 + j, (*chip, c), me).wait_recv()
                passed = copy(t, 4 + j, (*chip, c), sibling)
                passed.start()
                sent.append(passed)
        for t in range(nt):
            copy(t, 0, sibling, me).wait_recv()
            for j, chip in enumerate(chips):
                copy(t, 4 + j, (*chip, 1 - c), me).wait_recv()
        for cp in sent:
            cp.wait_send()
        for cp in mine:
            cp.wait()

    return pl.pallas_call(
        body, name=name,
        out_shape=[jax.ShapeDtypeStruct((N_DEV,) + a.shape, a.dtype) for a in shards],
        in_specs=[_HBM] * nt, out_specs=[_HBM] * nt,
        scratch_shapes=[pltpu.SemaphoreType.DMA((7 * nt,)), pltpu.SemaphoreType.DMA((7 * nt,)), pltpu.SemaphoreType.DMA((nt,))],
    )(*shards)


def _exchange_many(items, name):
    nt = len(items)

    def body(*refs):
        in_refs, out_refs = refs[:nt], refs[nt:2 * nt]
        send_sems, recv_sems, local_sems = refs[2 * nt:]
        x, y, c = lax.axis_index("x"), lax.axis_index("y"), lax.axis_index("c")
        me = 4 * x + 2 * y + c

        def src(t, p):
            return in_refs[t].at[p] if items[t][1] else in_refs[t]

        mine = [pltpu.make_async_copy(src(t, me), out_refs[t].at[me], local_sems.at[t]) for t in range(nt)]
        for cp in mine:
            cp.start()
        copies = []
        for k in range(1, N_DEV):
            px = 1 - x if k & 4 else x
            py = 1 - y if k & 2 else y
            pc = 1 - c if k & 1 else c
            for t in range(nt):
                copies.append(pltpu.make_async_remote_copy(
                    src_ref=src(t, 4 * px + 2 * py + pc), dst_ref=out_refs[t].at[me],
                    send_sem=send_sems.at[7 * t + k - 1], recv_sem=recv_sems.at[7 * t + k - 1],
                    device_id=(px, py, pc), device_id_type=pl.DeviceIdType.MESH))
        for cp in copies:
            cp.start()
        for cp in copies:
            cp.wait_recv()
        for cp in copies:
            cp.wait_send()
        for cp in mine:
            cp.wait()

    out_shape = [jax.ShapeDtypeStruct(a.shape if sc else (N_DEV,) + a.shape, a.dtype) for a, sc in items]
    return pl.pallas_call(
        body, name=name, out_shape=out_shape, in_specs=[_HBM] * nt, out_specs=[_HBM] * nt,
        scratch_shapes=[pltpu.SemaphoreType.DMA((7 * nt,)), pltpu.SemaphoreType.DMA((7 * nt,)), pltpu.SemaphoreType.DMA((nt,))],
    )(*[a for a, _ in items])


def _as_rows(shape):
    return (1, shape[0]) if len(shape) == 1 else (math.prod(shape[:-1]), shape[-1])


def _sum_adamw(pieces, w, m, v, name):
    shape = pieces.shape[1:]
    npc = pieces.shape[0]
    rows, cols = _as_rows(shape)
    cap = max(16, (1 << 18) // cols // 16 * 16)
    tr = _tile(rows, cap, 16)
    c1 = 1.0 / (1.0 - ADAM_B1 ** ADAM_STEP)
    c2 = 1.0 / (1.0 - ADAM_B2 ** ADAM_STEP)
    adam = w is not None

    def body(*refs):
        p_ref = refs[0]
        gg = p_ref[0].astype(F32)
        for k in range(1, npc):
            gg = gg + p_ref[k].astype(F32)
        if not adam:
            refs[1][...] = gg
            return
        w_ref, m_ref, v_ref, g_ref, d_ref, nm_ref, nv_ref = refs[1:]
        nm = ADAM_B1 * m_ref[...] + (1.0 - ADAM_B1) * gg
        nv = ADAM_B2 * v_ref[...] + (1.0 - ADAM_B2) * (gg * gg)
        g_ref[...] = gg
        d_ref[...] = -ADAM_LR * ((nm * c1) / (jnp.sqrt(nv * c2) + ADAM_EPS) + ADAM_WD * w_ref[...])
        nm_ref[...] = nm
        nv_ref[...] = nv

    blk = pl.BlockSpec((tr, cols), lambda i: (i, 0))
    ins = [pieces.reshape(npc, rows, cols)] + ([a.reshape(rows, cols) for a in (w, m, v)] if adam else [])
    n_out = 4 if adam else 1
    res = pl.pallas_call(
        body, name=name, grid=(rows // tr,),
        in_specs=[pl.BlockSpec((npc, tr, cols), lambda i: (0, i, 0))] + [blk] * (len(ins) - 1),
        out_specs=[blk] * n_out,
        out_shape=[jax.ShapeDtypeStruct((rows, cols), F32)] * n_out,
        compiler_params=_cparams("parallel"),
    )(*ins)
    return [r.reshape(shape) for r in res]


def _to_shards(full, axis):
    shp = full.shape
    a = full.reshape(shp[:axis] + (N_DEV, shp[axis] // N_DEV) + shp[axis + 1:])
    return jnp.moveaxis(a, axis, 0)


def _from_shards(blocks, axis):
    a = jnp.moveaxis(blocks, 0, axis)
    shp = a.shape
    return a.reshape(shp[:axis] + (shp[axis] * shp[axis + 1],) + shp[axis + 2:])


def kernel(x, meta_tokens, ln0_g, ln0_b, w_in, q_norm, kv_norm, w_uq, w_uk, w_uv, w_o_mla, lru_conv_w, lru_conv_b, w_rg, b_rg, w_ig, b_ig, lru_lambda, w_o_lru, w_out, ln1_g, ln1_b, w_up, ffn_conv_w, ffn_conv_b, w_down, ln2_g, ln2_b, loss_target, m_meta_tokens, m_ln0_g, m_ln0_b, m_w_in, m_q_norm, m_kv_norm, m_w_uq, m_w_uk, m_w_uv, m_w_o_mla, m_lru_conv_w, m_lru_conv_b, m_w_rg, m_b_rg, m_w_ig, m_b_ig, m_lru_lambda, m_w_o_lru, m_w_out, m_ln1_g, m_ln1_b, m_w_up, m_ffn_conv_w, m_ffn_conv_b, m_w_down, m_ln2_g, m_ln2_b, v_meta_tokens, v_ln0_g, v_ln0_b, v_w_in, v_q_norm, v_kv_norm, v_w_uq, v_w_uk, v_w_uv, v_w_o_mla, v_lru_conv_w, v_lru_conv_b, v_w_rg, v_b_rg, v_w_ig, v_b_ig, v_lru_lambda, v_w_o_lru, v_w_out, v_ln1_g, v_ln1_b, v_w_up, v_ffn_conv_w, v_ffn_conv_b, v_w_down, v_ln2_g, v_ln2_b):
    args = (meta_tokens, ln0_g, ln0_b, w_in, q_norm, kv_norm, w_uq, w_uk, w_uv, w_o_mla, lru_conv_w, lru_conv_b, w_rg, b_rg, w_ig, b_ig, lru_lambda, w_o_lru, w_out, ln1_g, ln1_b, w_up, ffn_conv_w, ffn_conv_b, w_down, ln2_g, ln2_b)
    ms = (m_meta_tokens, m_ln0_g, m_ln0_b, m_w_in, m_q_norm, m_kv_norm, m_w_uq, m_w_uk, m_w_uv, m_w_o_mla, m_lru_conv_w, m_lru_conv_b, m_w_rg, m_b_rg, m_w_ig, m_b_ig, m_lru_lambda, m_w_o_lru, m_w_out, m_ln1_g, m_ln1_b, m_w_up, m_ffn_conv_w, m_ffn_conv_b, m_w_down, m_ln2_g, m_ln2_b)
    vs = (v_meta_tokens, v_ln0_g, v_ln0_b, v_w_in, v_q_norm, v_kv_norm, v_w_uq, v_w_uk, v_w_uv, v_w_o_mla, v_lru_conv_w, v_lru_conv_b, v_w_rg, v_b_rg, v_w_ig, v_b_ig, v_lru_lambda, v_w_o_lru, v_w_out, v_ln1_g, v_ln1_b, v_w_up, v_ffn_conv_w, v_ffn_conv_b, v_w_down, v_ln2_g, v_ln2_b)
    wd, md, vd = dict(zip(WEIGHTS, args)), dict(zip(WEIGHTS, ms)), dict(zip(WEIGHTS, vs))

    full = {n: wd[n] for n in REPLICATED}
    got = _gather_many([wd[n].astype(BF16 if n in BIG else F32) for n in SHARDED], "gather_weights")
    for n, blocks in zip(SHARDED, got):
        full[n] = _from_shards(blocks, SHARD_AXIS[n])

    seq = x.shape[1]
    t_pad = -(-(N_META + seq + MIN_PAD_ROWS) // LANE) * LANE
    loss, grad_x, grads = _local_step(x[0], loss_target[0], full, t_pad)
    loss = lax.psum(loss, ("x", "y", "c"))

    items, kinds = [], []
    for n in WEIGHTS:
        g = grads[n]
        if n in SHARD_AXIS:
            g = _to_shards(g, SHARD_AXIS[n])
            items.append((g.astype(BF16) if n in BIG else g, True))
            kinds.append("sharded")
        elif n in LARGE_REPLICATED:
            rows, cols = _as_rows(g.shape)
            items.append((g.reshape(N_DEV, rows // N_DEV, cols).astype(BF16), True))
            kinds.append("large")
        else:
            items.append((g, False))
            kinds.append("small")
    pieces = dict(zip(WEIGHTS, _exchange_many(items, "grad_exchange")))

    large = [n for n, k in zip(WEIGHTS, kinds) if k == "large"]
    owned = [_sum_adamw(pieces[n], None, None, None, "grad_sum_" + n)[0] for n in large]
    for n, blocks in zip(large, _gather_many(owned, "gather_large_grads")):
        pieces[n] = blocks.reshape((1,) + wd[n].shape)
    outs = {n: _sum_adamw(pieces[n], wd[n], md[n], vd[n], "adamw_" + n) for n in WEIGHTS}
    res = [loss, grad_x[None]]
    for k in range(4):
        res += [outs[n][k] for n in WEIGHTS]
    return tuple(res)
```

```python
import functools
import math

import jax
import jax.numpy as jnp
from jax import lax
from jax.experimental import pallas as pl
from jax.experimental.pallas import tpu as pltpu

F32 = jnp.float32
BF16 = jnp.bfloat16

N_DEV = 8
D_MODEL = 1024
N_META = 16
HEADS = 8
QK_NOPE = 128
QK_ROPE = 64
V_HEAD = 128
Q_RANK = 256
KV_RANK = 128
ROPE_THETA = 10000.0
LRU_BLOCKS = 8
LRU_C = 8.0
D_FF = 2816
DEPTH = 2
DN_ALPHA = (2.0 * DEPTH) ** 0.25
LN_EPS = 1e-5
RMS_EPS = 1e-6
LN2 = math.log(2.0)
ATT_SCALE = 1.0 / math.sqrt(QK_NOPE + QK_ROPE) / LN2
NEG_BIG = -1e30

ADAM_LR = 0.001
ADAM_B1 = 0.9
ADAM_B2 = 0.999
ADAM_EPS = 1e-08
ADAM_WD = 0.01
ADAM_STEP = 10

MIN_PAD_ROWS = 2
LANE = 128
SUBLANE = 8
VMEM_LIMIT = 56 * 1024 * 1024

PROJ_COLS = 4 * D_MODEL + Q_RANK + KV_RANK + 2 * QK_ROPE
C_LRU_G, C_LRU_X, C_G_MLA, C_G_LRU = 0, D_MODEL, 2 * D_MODEL, 3 * D_MODEL
C_CQ = 4 * D_MODEL
C_CKV = C_CQ + Q_RANK
C_KRP = C_CKV + KV_RANK

WEIGHTS = ['meta_tokens', 'ln0_g', 'ln0_b', 'w_in', 'q_norm', 'kv_norm', 'w_uq', 'w_uk', 'w_uv', 'w_o_mla',
           'lru_conv_w', 'lru_conv_b', 'w_rg', 'b_rg', 'w_ig', 'b_ig', 'lru_lambda', 'w_o_lru', 'w_out',
           'ln1_g', 'ln1_b', 'w_up', 'ffn_conv_w', 'ffn_conv_b', 'w_down', 'ln2_g', 'ln2_b']
SHARD_AXIS = {'meta_tokens': 1, 'w_in': 2, 'w_uq': 1, 'w_o_mla': 1, 'lru_conv_w': 2, 'b_rg': 2, 'b_ig': 2,
              'lru_lambda': 2, 'w_o_lru': 1, 'w_out': 1, 'w_up': 2, 'ffn_conv_w': 2, 'w_down': 1}
BIG = ['w_in', 'w_uq', 'w_o_mla', 'w_o_lru', 'w_out', 'w_up', 'w_down']
SHARDED = [n for n in WEIGHTS if n in SHARD_AXIS]
REPLICATED = [n for n in WEIGHTS if n not in SHARD_AXIS]
LARGE_REPLICATED = ['w_uk', 'w_uv', 'w_rg', 'w_ig']


def _cparams(*sem):
    return pltpu.CompilerParams(dimension_semantics=sem, vmem_limit_bytes=VMEM_LIMIT)


def _tile(n, cap, unit=LANE):
    best = None
    t = unit
    while t <= min(n, cap):
        if n % t == 0:
            best = t
        t += unit
    return n if best is None else best


def _sigmoid(x):
    return 1.0 / (1.0 + jnp.exp(-x))


_GELU_C = math.sqrt(2.0 / math.pi)


def _gelu(x):
    t = jnp.tanh(_GELU_C * (x + 0.044715 * x * x * x))
    return 0.5 * x * (1.0 + t)


def _gelu_and_grad(x):
    t = jnp.tanh(_GELU_C * (x + 0.044715 * x * x * x))
    g = 0.5 * x * (1.0 + t)
    dg = 0.5 * (1.0 + t) + 0.5 * x * (1.0 - t * t) * _GELU_C * (1.0 + 3.0 * 0.044715 * x * x)
    return g, dg


def _softplus_neg(lam):
    z = jnp.exp(-jnp.abs(lam))
    w = 1.0 + z
    log1p = jnp.where(w == 1.0, z, jnp.log(w) * z / (w - 1.0))
    return jnp.maximum(-lam, 0.0) + log1p


def _row_ids(shape, row0=0):
    return lax.broadcasted_iota(jnp.int32, shape, 0) + row0


def _matmul(a, b, name, ta=False, tb=False, out_dtype=F32, tm_cap=1408, tn_cap=1024, tk_cap=2048):
    if ta:
        kdim, m = a.shape
    else:
        m, kdim = a.shape
    if tb:
        n, k2 = b.shape
    else:
        k2, n = b.shape
    assert kdim == k2, (a.shape, b.shape, ta, tb)
    tm, tn, tk = _tile(m, tm_cap), _tile(n, tn_cap), _tile(kdim, tk_cap)
    nk = kdim // tk

    def body(a_ref, b_ref, o_ref, acc_ref):
        k = pl.program_id(2)

        @pl.when(k == 0)
        def _():
            acc_ref[...] = jnp.zeros_like(acc_ref)

        dn = (((0 if ta else 1,), (1 if tb else 0,)), ((), ()))
        acc_ref[...] += lax.dot_general(a_ref[...].astype(BF16), b_ref[...].astype(BF16), dn,
                                        preferred_element_type=F32)

        @pl.when(k == nk - 1)
        def _():
            o_ref[...] = acc_ref[...].astype(o_ref.dtype)

    a_spec = pl.BlockSpec((tk, tm), lambda i, j, k: (k, i)) if ta else pl.BlockSpec((tm, tk), lambda i, j, k: (i, k))
    b_spec = pl.BlockSpec((tn, tk), lambda i, j, k: (j, k)) if tb else pl.BlockSpec((tk, tn), lambda i, j, k: (k, j))
    return pl.pallas_call(
        body, name=name,
        grid=(m // tm, n // tn, nk),
        in_specs=[a_spec, b_spec],
        out_specs=pl.BlockSpec((tm, tn), lambda i, j, k: (i, j)),
        out_shape=jax.ShapeDtypeStruct((m, n), out_dtype),
        scratch_shapes=[pltpu.VMEM((tm, tn), F32)],
        compiler_params=_cparams("parallel", "parallel", "arbitrary"),
    )(a, b)


class Rw:
    def __init__(self, arr, width=None, cb=0):
        self.arr, self.width, self.cb = arr, (arr.shape[1] if width is None else width), cb


class Pm:
    def __init__(self, arr):
        self.arr = arr


def _rows(fn, name, ins, outs, accs=(), tm_cap=384):
    tp = next(o.arr.shape[0] for o in ins if isinstance(o, Rw))
    tm = _tile(tp, tm_cap)
    n_in, n_out, n_acc = len(ins), len(outs), len(accs)

    def body(*refs):
        i = pl.program_id(0)
        res = fn(i * tm, *[r[...] for r in refs[:n_in]])
        if not isinstance(res, (tuple, list)):
            res = (res,)
        assert len(res) == n_out + n_acc, (name, len(res))
        for k in range(n_out):
            refs[n_in + k][...] = res[k].astype(refs[n_in + k].dtype)
        for k in range(n_acc):
            ref = refs[n_in + n_out + k]

            @pl.when(i == 0)
            def _():
                ref[...] = jnp.zeros_like(ref)

            ref[...] += res[n_out + k]

    in_specs = []
    for o in ins:
        if isinstance(o, Rw):
            in_specs.append(pl.BlockSpec((tm, o.width), functools.partial(lambda i, cb: (i, cb), cb=o.cb)))
        else:
            in_specs.append(pl.BlockSpec(o.arr.shape, functools.partial(lambda i, nd: (0,) * nd, nd=o.arr.ndim)))
    out_specs = [pl.BlockSpec((tm, w), lambda i: (i, 0)) for (w, _) in outs]
    out_specs += [pl.BlockSpec(s, functools.partial(lambda i, nd: (0,) * nd, nd=len(s))) for s in accs]
    out_shape = [jax.ShapeDtypeStruct((tp, w), dt) for (w, dt) in outs]
    out_shape += [jax.ShapeDtypeStruct(s, F32) for s in accs]
    res = pl.pallas_call(
        body, name=name, grid=(tp // tm,), in_specs=in_specs, out_specs=out_specs, out_shape=out_shape,
        compiler_params=_cparams("arbitrary"),
    )(*[o.arr for o in ins])
    return res


class Cl:
    def __init__(self, arr, col0=0):
        self.arr, self.col0 = arr, col0


def _cols(fn, name, ins, outs, ncols, tc):
    assert ncols % tc == 0
    n_in, n_out = len(ins), len(outs)

    def body(*refs):
        res = fn(*[r[...] for r in refs[:n_in]])
        if not isinstance(res, (tuple, list)):
            res = (res,)
        assert len(res) == n_out, (name, len(res))
        for k in range(n_out):
            refs[n_in + k][...] = res[k].astype(refs[n_in + k].dtype)

    in_specs = []
    for o in ins:
        assert o.col0 % tc == 0, (name, o.col0, tc)
        in_specs.append(pl.BlockSpec((o.arr.shape[0], tc), functools.partial(lambda j, off: (0, j + off), off=o.col0 // tc)))
    out_specs = [pl.BlockSpec((r, tc), lambda j: (0, j)) for (r, _) in outs]
    out_shape = [jax.ShapeDtypeStruct((r, ncols), dt) for (r, dt) in outs]
    return pl.pallas_call(
        body, name=name, grid=(ncols // tc,), in_specs=in_specs, out_specs=out_specs, out_shape=out_shape,
        compiler_params=_cparams("parallel"),
    )(*[o.arr for o in ins])


def _ln_stats(u):
    mu = jnp.mean(u, axis=-1, keepdims=True)
    xc = u - mu
    var = jnp.mean(xc * xc, axis=-1, keepdims=True)
    rstd = lax.rsqrt(var + LN_EPS)
    return xc * rstd, rstd


def _ln_fwd(terms, g, b, name):
    coefs = [c for c, _ in terms]

    def fn(row0, *blk):
        xs, (gg, bb) = blk[:len(coefs)], blk[len(coefs):]
        u = sum(c * x for c, x in zip(coefs, xs))
        xhat, _ = _ln_stats(u)
        return xhat * gg + bb

    d = terms[0][1].shape[1]
    return _rows(fn, name, [Rw(x) for _, x in terms] + [Pm(g.reshape(1, d)), Pm(b.reshape(1, d))], [(d, F32)])[0]


def _ln_bwd(dy_terms, u_terms, g, name):
    dc = [c for c, _ in dy_terms]
    uc = [c for c, _ in u_terms]
    d = u_terms[0][1].shape[1]

    def fn(row0, *blk):
        dys = blk[:len(dc)]
        xs = blk[len(dc):len(dc) + len(uc)]
        gg = blk[-1]
        dy = sum(c * x for c, x in zip(dc, dys))
        u = sum(c * x for c, x in zip(uc, xs))
        xhat, rstd = _ln_stats(u)
        gdy = dy * gg
        m1 = jnp.mean(gdy, axis=-1, keepdims=True)
        m2 = jnp.mean(gdy * xhat, axis=-1, keepdims=True)
        du = rstd * (gdy - m1 - xhat * m2)
        return du, jnp.sum(dy * xhat, axis=0, keepdims=True), jnp.sum(dy, axis=0, keepdims=True)

    ins = [Rw(x) for _, x in dy_terms] + [Rw(x) for _, x in u_terms] + [Pm(g.reshape(1, d))]
    return _rows(fn, name, ins, [(d, F32)], accs=[(1, d), (1, d)])


def _loss_head(y, tgt, t_real, name):
    d = y.shape[1]

    def fn(row0, yb, tb):
        rows = _row_ids(yb.shape, row0)
        live = (rows >= N_META) & (rows < t_real)
        diff = jnp.where(live, yb - tb, 0.0)
        return diff * (1.0 / d), jnp.sum(diff * diff, axis=0, keepdims=True) * (0.5 / d)

    return _rows(fn, name, [Rw(y), Rw(tgt)], [(d, F32)], accs=[(1, d)])


def _rms(x, g):
    r = lax.rsqrt(jnp.mean(x * x, axis=-1, keepdims=True) + RMS_EPS)
    return x * r * g


def _rms_bwd(dy, x, g):
    r = lax.rsqrt(jnp.mean(x * x, axis=-1, keepdims=True) + RMS_EPS)
    gdy = dy * g
    dx = r * gdy - x * (r * r * r) * jnp.mean(gdy * x, axis=-1, keepdims=True)
    return dx, jnp.sum(dy * x * r, axis=0, keepdims=True)


def _mla_norms(proj, qn, kvn, name):
    def fn(row0, cq, ckv, g1, g2):
        return _rms(cq, g1), _rms(ckv, g2)

    return _rows(fn, name, [Rw(proj, Q_RANK, C_CQ // Q_RANK), Rw(proj, KV_RANK, C_CKV // KV_RANK),
                            Pm(qn.reshape(1, Q_RANK)), Pm(kvn.reshape(1, KV_RANK))],
                 [(Q_RANK, F32), (KV_RANK, F32)])


def _mla_norms_bwd(dcqn, dckvn, proj, qn, kvn, name):
    def fn(row0, d1, d2, cq, ckv, g1, g2):
        dx1, dg1 = _rms_bwd(d1, cq, g1)
        dx2, dg2 = _rms_bwd(d2, ckv, g2)
        return dx1, dx2, dg1, dg2

    return _rows(fn, name, [Rw(dcqn), Rw(dckvn), Rw(proj, Q_RANK, C_CQ // Q_RANK), Rw(proj, KV_RANK, C_CKV // KV_RANK),
                            Pm(qn.reshape(1, Q_RANK)), Pm(kvn.reshape(1, KV_RANK))],
                 [(Q_RANK, F32), (KV_RANK, F32)], accs=[(1, Q_RANK), (1, KV_RANK)])


def _fold_rope(z):
    return z + pltpu.roll(z, QK_ROPE, 1)


def _mla_pack(qext, kv, proj, cs, name):
    tp = qext.shape[0]
    tm = _tile(tp, 384)

    def body(q_ref, kn_ref, v_ref, kr_ref, cs_ref, qo_ref, ko_ref, vo_ref):
        cs_ = cs_ref[...]
        low = lax.broadcasted_iota(jnp.int32, cs_.shape, 1) < QK_ROPE
        q = q_ref[...]
        qr = jnp.where(low, _fold_rope(q[:, QK_NOPE:] * cs_), 0.0)
        qo_ref[...] = (jnp.concatenate([q[:, :QK_NOPE], qr], axis=1) * ATT_SCALE).astype(BF16)
        kr = _fold_rope(kr_ref[...] * cs_)
        ko_ref[...] = jnp.concatenate([kn_ref[...], kr], axis=1).astype(BF16)
        vo_ref[...] = v_ref[...].astype(BF16)

    return pl.pallas_call(
        body, name=name, grid=(tp // tm, HEADS),
        in_specs=[pl.BlockSpec((tm, 2 * LANE), lambda i, h: (i, h)),
                  pl.BlockSpec((tm, LANE), lambda i, h: (i, h)),
                  pl.BlockSpec((tm, LANE), lambda i, h: (i, HEADS + h)),
                  pl.BlockSpec((tm, LANE), lambda i, h: (i, C_KRP // LANE)),
                  pl.BlockSpec((tm, LANE), lambda i, h: (i, 0))],
        out_specs=[pl.BlockSpec((tm, 2 * LANE), lambda i, h: (i, h)),
                   pl.BlockSpec((tm, 2 * LANE), lambda i, h: (i, h)),
                   pl.BlockSpec((tm, LANE), lambda i, h: (i, h))],
        out_shape=[jax.ShapeDtypeStruct((tp, HEADS * 2 * LANE), BF16),
                   jax.ShapeDtypeStruct((tp, HEADS * 2 * LANE), BF16),
                   jax.ShapeDtypeStruct((tp, HEADS * LANE), BF16)],
        compiler_params=_cparams("parallel", "arbitrary"),
    )(qext, kv, kv, proj, cs)


def _mla_unpack(dq, dk, cs, name):
    tp = dq.shape[0]
    tm = _tile(tp, 384)

    def body(dq_ref, dk_ref, cs_ref, dqe_ref, dkn_ref, dkr_ref):
        h = pl.program_id(1)
        cs_ = cs_ref[...]
        low = lax.broadcasted_iota(jnp.int32, cs_.shape, 1) < QK_ROPE
        dq_ = dq_ref[...] * ATT_SCALE
        dqr = _fold_rope(jnp.where(low, dq_[:, QK_NOPE:], 0.0)) * cs_
        dqe_ref[...] = jnp.concatenate([dq_[:, :QK_NOPE], dqr], axis=1)
        dk_ = dk_ref[...]
        dkn_ref[...] = dk_[:, :QK_NOPE]

        @pl.when(h == 0)
        def _():
            dkr_ref[...] = jnp.zeros_like(dkr_ref)

        dkr_ref[...] += _fold_rope(jnp.where(low, dk_[:, QK_NOPE:], 0.0)) * cs_

    return pl.pallas_call(
        body, name=name, grid=(tp // tm, HEADS),
        in_specs=[pl.BlockSpec((tm, 2 * LANE), lambda i, h: (i, h)),
                  pl.BlockSpec((tm, 2 * LANE), lambda i, h: (i, h)),
                  pl.BlockSpec((tm, LANE), lambda i, h: (i, 0))],
        out_specs=[pl.BlockSpec((tm, 2 * LANE), lambda i, h: (i, h)),
                   pl.BlockSpec((tm, LANE), lambda i, h: (i, h)),
                   pl.BlockSpec((tm, LANE), lambda i, h: (i, 0))],
        out_shape=[jax.ShapeDtypeStruct((tp, HEADS * 2 * LANE), F32),
                   jax.ShapeDtypeStruct((tp, HEADS * LANE), F32),
                   jax.ShapeDtypeStruct((tp, LANE), F32)],
        compiler_params=_cparams("parallel", "arbitrary"),
    )(dq, dk, cs)


def _attn_fwd(q, k, v, t_real, name):
    tp = q.shape[0]
    tq = _tile(tp, 384)
    tkc = _tile(tp, 1408)
    nkc = -(-t_real // tkc)

    def body(q_ref, k_ref, v_ref, o_ref, lse_ref):
        qb = q_ref[...]
        m = l = acc = None
        for c in range(nkc):
            s = lax.dot_general(qb, k_ref[c * tkc:(c + 1) * tkc, :], (((1,), (1,)), ((), ())), preferred_element_type=F32)
            if (c + 1) * tkc > t_real:
                cols = lax.broadcasted_iota(jnp.int32, s.shape, 1) + c * tkc
                s = jnp.where(cols < t_real, s, NEG_BIG)
            mc = jnp.max(s, axis=-1, keepdims=True)
            m_new = mc if c == 0 else jnp.maximum(m, mc)
            p = jnp.exp2(s - m_new)
            lc = jnp.sum(p, axis=-1, keepdims=True)
            pv = jnp.dot(p.astype(BF16), v_ref[c * tkc:(c + 1) * tkc, :], preferred_element_type=F32)
            if c == 0:
                l, acc = lc, pv
            else:
                alpha = jnp.exp2(m - m_new)
                l, acc = alpha * l + lc, alpha * acc + pv
            m = m_new
        o_ref[...] = acc / l
        lse_ref[...] = m + jnp.log2(l)

    return pl.pallas_call(
        body, name=name, grid=(HEADS, tp // tq),
        in_specs=[pl.BlockSpec((tq, 2 * LANE), lambda h, i: (i, h)),
                  pl.BlockSpec((tp, 2 * LANE), lambda h, i: (0, h)),
                  pl.BlockSpec((tp, LANE), lambda h, i: (0, h))],
        out_specs=[pl.BlockSpec((tq, LANE), lambda h, i: (i, h)),
                   pl.BlockSpec((None, tq, 1), lambda h, i: (h, i, 0))],
        out_shape=[jax.ShapeDtypeStruct((tp, HEADS * LANE), F32),
                   jax.ShapeDtypeStruct((HEADS, tp, 1), F32)],
        compiler_params=_cparams("parallel", "parallel"),
    )(q, k, v)


def _attn_bwd(q, k, v, do, o, lse, t_real, name):
    tp = q.shape[0]
    tq = _tile(tp, 384)
    tkc = _tile(tp, 1408)
    nkc = -(-t_real // tkc)

    def body(q_ref, k_ref, v_ref, do_ref, o_ref, lse_ref, dq_ref, dk_ref, dv_ref):
        i = pl.program_id(1)

        @pl.when(i == 0)
        def _():
            dk_ref[...] = jnp.zeros_like(dk_ref)
            dv_ref[...] = jnp.zeros_like(dv_ref)

        qb = q_ref[...]
        dob = do_ref[...]
        dob16 = dob.astype(BF16)
        dol2 = (dob * LN2).astype(BF16)
        delta = jnp.sum(dob * o_ref[...], axis=-1, keepdims=True) * LN2
        lse = lse_ref[...]
        dq = None
        for c in range(nkc):
            ks = slice(c * tkc, (c + 1) * tkc)
            kb = k_ref[ks, :]
            s = lax.dot_general(qb, kb, (((1,), (1,)), ((), ())), preferred_element_type=F32)
            p = jnp.exp2(s - lse)
            if (c + 1) * tkc > t_real:
                cols = lax.broadcasted_iota(jnp.int32, s.shape, 1) + c * tkc
                p = jnp.where(cols < t_real, p, 0.0)
            dp = lax.dot_general(dol2, v_ref[ks, :], (((1,), (1,)), ((), ())), preferred_element_type=F32)
            ds = (p * (dp - delta)).astype(BF16)
            dqc = jnp.dot(ds, kb, preferred_element_type=F32)
            dq = dqc if c == 0 else dq + dqc
            dk_ref[ks, :] += lax.dot_general(ds, qb, (((0,), (0,)), ((), ())), preferred_element_type=F32)
            dv_ref[ks, :] += lax.dot_general(p.astype(BF16), dob16, (((0,), (0,)), ((), ())), preferred_element_type=F32)
        dq_ref[...] = dq

    return pl.pallas_call(
        body, name=name, grid=(HEADS, tp // tq),
        in_specs=[pl.BlockSpec((tq, 2 * LANE), lambda h, i: (i, h)),
                  pl.BlockSpec((tp, 2 * LANE), lambda h, i: (0, h)),
                  pl.BlockSpec((tp, LANE), lambda h, i: (0, h)),
                  pl.BlockSpec((tq, LANE), lambda h, i: (i, h)),
                  pl.BlockSpec((tq, LANE), lambda h, i: (i, h)),
                  pl.BlockSpec((None, tq, 1), lambda h, i: (h, i, 0))],
        out_specs=[pl.BlockSpec((tq, 2 * LANE), lambda h, i: (i, h)),
                   pl.BlockSpec((tp, 2 * LANE), lambda h, i: (0, h)),
                   pl.BlockSpec((tp, LANE), lambda h, i: (0, h))],
        out_shape=[jax.ShapeDtypeStruct((tp, HEADS * 2 * LANE), F32),
                   jax.ShapeDtypeStruct((tp, HEADS * 2 * LANE), F32),
                   jax.ShapeDtypeStruct((tp, HEADS * LANE), F32)],
        compiler_params=_cparams("parallel", "arbitrary"),
    )(q, k, v, do, o, lse)


def _shift_rows(x, s):
    tp = x.shape[0]
    return x if s % tp == 0 else pltpu.roll(x, s % tp, 0)


def _conv_fwd_val(xm, w, b, pad_left):
    acc = b + w[0:1, :] * _shift_rows(xm, pad_left)
    for k in range(1, w.shape[0]):
        acc = acc + w[k:k + 1, :] * _shift_rows(xm, pad_left - k)
    return acc


def _conv_bwd_val(dy, xm, w, pad_left, live):
    kk = w.shape[0]
    dx = w[0:1, :] * _shift_rows(dy, -pad_left)
    dws = [jnp.sum(dy * _shift_rows(xm, pad_left), axis=0, keepdims=True)]
    for k in range(1, kk):
        dx = dx + w[k:k + 1, :] * _shift_rows(dy, k - pad_left)
        dws.append(jnp.sum(dy * _shift_rows(xm, pad_left - k), axis=0, keepdims=True))
    return jnp.where(live, dx, 0.0), jnp.concatenate(dws, axis=0), jnp.sum(dy, axis=0, keepdims=True)


def _lru_conv_fwd(proj, w, b, t_real, name):
    def fn(x, ww, bb):
        xm = jnp.where(_row_ids(x.shape) < t_real, x, 0.0)
        return _conv_fwd_val(xm, ww, bb, 2)

    return _cols(fn, name, [Cl(proj, C_LRU_X), Cl(w), Cl(b.reshape(1, -1))], [(proj.shape[0], F32)], D_MODEL, 128)[0]


def _lru_conv_bwd(dxc, proj, w, t_real, name):
    def fn(dy, x, ww):
        live = _row_ids(x.shape) < t_real
        xm = jnp.where(live, x, 0.0)
        dym = jnp.where(live, dy, 0.0)
        return _conv_bwd_val(dym, xm, ww, 2, live)

    return _cols(fn, name, [Cl(dxc), Cl(proj, C_LRU_X), Cl(w)],
                 [(proj.shape[0], F32), (w.shape[0], F32), (1, F32)], D_MODEL, 128)


def _ffn_conv_act(up, w, b, t_real, name):
    def fn(g, v, wg, wv, bg, bv):
        live = _row_ids(g.shape) < t_real
        gc = _conv_fwd_val(jnp.where(live, g, 0.0), wg, bg, 1)
        vc = _conv_fwd_val(jnp.where(live, v, 0.0), wv, bv, 1)
        return _gelu(gc) * vc

    b2 = b.reshape(1, -1)
    return _cols(fn, name, [Cl(up), Cl(up, D_FF), Cl(w), Cl(w, D_FF), Cl(b2), Cl(b2, D_FF)],
                 [(up.shape[0], F32)], D_FF, 128)[0]


def _ffn_conv_act_bwd(dm, up, w, b, t_real, name):
    def fn(dmb, g, v, wg, wv, bg, bv):
        live = _row_ids(g.shape) < t_real
        gm, vm = jnp.where(live, g, 0.0), jnp.where(live, v, 0.0)
        gc = _conv_fwd_val(gm, wg, bg, 1)
        vc = _conv_fwd_val(vm, wv, bv, 1)
        act, dact = _gelu_and_grad(gc)
        dmm = jnp.where(live, dmb, 0.0)
        dgx, dwg, dbg = _conv_bwd_val(dmm * vc * dact, gm, wg, 1, live)
        dvx, dwv, dbv = _conv_bwd_val(dmm * act, vm, wv, 1, live)
        return dgx, dvx, dwg, dwv, dbg, dbv

    b2 = b.reshape(1, -1)
    tp, kk = up.shape[0], w.shape[0]
    return _cols(fn, name, [Cl(dm), Cl(up), Cl(up, D_FF), Cl(w), Cl(w, D_FF), Cl(b2), Cl(b2, D_FF)],
                 [(tp, F32), (tp, F32), (kk, F32), (kk, F32), (1, F32), (1, F32)], D_FF, 128)


def _lru_gates_fwd(xc, wg, b4, lam, t_real, name):
    tp = xc.shape[0]
    tm = _tile(tp, 1408)

    def body(x_ref, w_ref, b_ref, lam_ref, r0_ref, r1_ref, i0_ref, i1_ref, a0_ref, a1_ref, u0_ref, u1_ref):
        x = x_ref[...]
        xb = x.astype(BF16)
        live = _row_ids(x.shape, pl.program_id(1) * tm) < t_real
        bb = b_ref[...]
        sp = _softplus_neg(lam_ref[...])
        gate = [_sigmoid(jnp.dot(xb, w_ref[k], preferred_element_type=F32) + bb[k:k + 1, :]) for k in range(4)]
        for d, (r_ref, i_ref, a_ref, u_ref) in enumerate(((r0_ref, i0_ref, a0_ref, u0_ref), (r1_ref, i1_ref, a1_ref, u1_ref))):
            r, ig = gate[d], gate[2 + d]
            a = jnp.exp(-LRU_C * r * sp[d:d + 1, :])
            r_ref[...] = r
            i_ref[...] = ig
            a_ref[...] = a
            u_ref[...] = jnp.where(live, jnp.sqrt(1.0 - a * a) * (ig * x), 0.0)

    blk = pl.BlockSpec((tm, LANE), lambda g, i: (i, g))
    return pl.pallas_call(
        body, name=name, grid=(LRU_BLOCKS, tp // tm),
        in_specs=[blk, pl.BlockSpec((None, 4, LANE, LANE), lambda g, i: (g, 0, 0, 0)),
                  pl.BlockSpec((4, LANE), lambda g, i: (0, g)), pl.BlockSpec((2, LANE), lambda g, i: (0, g))],
        out_specs=[blk] * 8,
        out_shape=[jax.ShapeDtypeStruct((tp, D_MODEL), F32)] * 8,
        compiler_params=_cparams("parallel", "parallel"),
    )(xc, wg, b4, lam)


def _lru_gates_bwd(l0, l1, da0, da1, r0, r1, i0, i1, a0, a1, xc, wg, lam, t_real, name):
    tp = xc.shape[0]
    tm = _tile(tp, 1408)

    def body(l0_ref, l1_ref, da0_ref, da1_ref, r0_ref, r1_ref, i0_ref, i1_ref, a0_ref, a1_ref, x_ref, w_ref, lam_ref,
             dx_ref, dw_ref, db_ref, dlam_ref):
        i = pl.program_id(1)
        x = x_ref[...]
        xb = x.astype(BF16)
        live = _row_ids(x.shape, i * tm) < t_real
        lam_ = lam_ref[...]
        sp = _softplus_neg(lam_)
        dsp_dlam = -_sigmoid(-lam_)
        dx = jnp.zeros_like(x)
        dpre = [None] * 4
        dlam_rows = []
        for d, (l_ref, da_ref, r_ref, i_ref, a_ref) in enumerate(((l0_ref, da0_ref, r0_ref, i0_ref, a0_ref),
                                                                  (l1_ref, da1_ref, r1_ref, i1_ref, a1_ref))):
            r, ig, a = r_ref[...], i_ref[...], a_ref[...]
            du = jnp.where(live, l_ref[...], 0.0)
            s = jnp.sqrt(1.0 - a * a)
            dv = du * s
            ds = du * (ig * x)
            dla = jnp.where(live, da_ref[...], 0.0) * a - ds * (a * a) / s
            dla = jnp.where(live, dla, 0.0)
            dr = dla * (-LRU_C) * sp[d:d + 1, :]
            dlam_rows.append(jnp.sum(dla * (-LRU_C) * r, axis=0, keepdims=True) * dsp_dlam[d:d + 1, :])
            dpre[d] = dr * r * (1.0 - r)
            dpre[2 + d] = dv * x * ig * (1.0 - ig)
            dx = dx + dv * ig

        @pl.when(i == 0)
        def _():
            dw_ref[...] = jnp.zeros_like(dw_ref)
            db_ref[...] = jnp.zeros_like(db_ref)
            dlam_ref[...] = jnp.zeros_like(dlam_ref)

        for k in range(4):
            pk = dpre[k].astype(BF16)
            dx = dx + lax.dot_general(pk, w_ref[k], (((1,), (1,)), ((), ())), preferred_element_type=F32)
            dw_ref[k] += lax.dot_general(xb, pk, (((0,), (0,)), ((), ())), preferred_element_type=F32)
        db_ref[...] += jnp.concatenate([jnp.sum(p, axis=0, keepdims=True) for p in dpre], axis=0)
        dlam_ref[...] += jnp.concatenate(dlam_rows, axis=0)
        dx_ref[...] = dx

    blk = pl.BlockSpec((tm, LANE), lambda g, i: (i, g))
    return pl.pallas_call(
        body, name=name, grid=(LRU_BLOCKS, tp // tm),
        in_specs=[blk] * 11 + [pl.BlockSpec((None, 4, LANE, LANE), lambda g, i: (g, 0, 0, 0)),
                               pl.BlockSpec((2, LANE), lambda g, i: (0, g))],
        out_specs=[blk, pl.BlockSpec((None, 4, LANE, LANE), lambda g, i: (g, 0, 0, 0)),
                   pl.BlockSpec((4, LANE), lambda g, i: (0, g)), pl.BlockSpec((2, LANE), lambda g, i: (0, g))],
        out_shape=[jax.ShapeDtypeStruct((tp, D_MODEL), F32), jax.ShapeDtypeStruct((LRU_BLOCKS, 4, LANE, LANE), F32),
                   jax.ShapeDtypeStruct((4, D_MODEL), F32), jax.ShapeDtypeStruct((2, D_MODEL), F32)],
        compiler_params=_cparams("parallel", "arbitrary"),
    )(l0, l1, da0, da1, r0, r1, i0, i1, a0, a1, xc, wg, lam)


def _tile_scan(a, u, reverse):
    rows = lax.broadcasted_iota(jnp.int32, a.shape, 0)
    for s in (1, 2, 4):
        if reverse:
            keep = rows < SUBLANE - s
            a_sh, u_sh = pltpu.roll(a, SUBLANE - s, 0), pltpu.roll(u, SUBLANE - s, 0)
        else:
            keep = rows >= s
            a_sh, u_sh = pltpu.roll(a, s, 0), pltpu.roll(u, s, 0)
        u = u + a * jnp.where(keep, u_sh, 0.0)
        a = a * jnp.where(keep, a_sh, 1.0)
    return a, u


def _scan_fwd(a0, u0, a1, u1, name):
    tp, d = a0.shape
    tc = 128
    nt = tp // SUBLANE

    def body(a0_ref, u0_ref, a1_ref, u1_ref, h0_ref, h1_ref):
        def step(t, carry):
            c0, c1 = carry
            f = pl.multiple_of(t * SUBLANE, SUBLANE)
            b = pl.multiple_of((nt - 1 - t) * SUBLANE, SUBLANE)
            pa, pu = _tile_scan(a0_ref[pl.ds(f, SUBLANE), :], u0_ref[pl.ds(f, SUBLANE), :], False)
            h = pu + pa * c0
            h0_ref[pl.ds(f, SUBLANE), :] = h
            c0 = h[SUBLANE - 1:SUBLANE, :]
            pa, pu = _tile_scan(a1_ref[pl.ds(b, SUBLANE), :], u1_ref[pl.ds(b, SUBLANE), :], True)
            h = pu + pa * c1
            h1_ref[pl.ds(b, SUBLANE), :] = h
            c1 = h[0:1, :]
            return c0, c1

        z = jnp.zeros((1, tc), F32)
        lax.fori_loop(0, nt, step, (z, z))

    blk = pl.BlockSpec((tp, tc), lambda j: (0, j))
    return pl.pallas_call(
        body, name=name, grid=(d // tc,), in_specs=[blk] * 4, out_specs=[blk] * 2,
        out_shape=[jax.ShapeDtypeStruct((tp, d), F32)] * 2,
        compiler_params=_cparams("parallel"),
    )(a0, u0, a1, u1)


def _scan_bwd(dh, a0, a1, h0, h1, name):
    tp, d = dh.shape
    tc = 128
    nt = tp // SUBLANE

    def body(dh_ref, a0_ref, a1_ref, h0_ref, h1_ref, l0_ref, l1_ref, da0_ref, da1_ref):
        rows8 = lax.broadcasted_iota(jnp.int32, (SUBLANE, tc), 0)

        def step(t, carry):
            c0, c1 = carry
            b = pl.multiple_of((nt - 1 - t) * SUBLANE, SUBLANE)
            f = pl.multiple_of(t * SUBLANE, SUBLANE)
            a = a0_ref[pl.ds(b, SUBLANE), :]
            a_next = jnp.where(rows8 < SUBLANE - 1, pltpu.roll(a, SUBLANE - 1, 0), 1.0)
            pa, pu = _tile_scan(a_next, dh_ref[pl.ds(b, SUBLANE), :], True)
            lam = pu + pa * c0
            l0_ref[pl.ds(b, SUBLANE), :] = lam
            c0 = a[0:1, :] * lam[0:1, :]
            a = a1_ref[pl.ds(f, SUBLANE), :]
            a_prev = jnp.where(rows8 >= 1, pltpu.roll(a, 1, 0), 1.0)
            pa, pu = _tile_scan(a_prev, dh_ref[pl.ds(f, SUBLANE), :], False)
            lam = pu + pa * c1
            l1_ref[pl.ds(f, SUBLANE), :] = lam
            c1 = a[SUBLANE - 1:SUBLANE, :] * lam[SUBLANE - 1:SUBLANE, :]
            return c0, c1

        z = jnp.zeros((1, tc), F32)
        lax.fori_loop(0, nt, step, (z, z))
        rows = lax.broadcasted_iota(jnp.int32, (tp, tc), 0)
        da0_ref[...] = l0_ref[...] * jnp.where(rows >= 1, pltpu.roll(h0_ref[...], 1, 0), 0.0)
        da1_ref[...] = l1_ref[...] * jnp.where(rows < tp - 1, pltpu.roll(h1_ref[...], tp - 1, 0), 0.0)

    blk = pl.BlockSpec((tp, tc), lambda j: (0, j))
    return pl.pallas_call(
        body, name=name, grid=(d // tc,), in_specs=[blk] * 5, out_specs=[blk] * 4,
        out_shape=[jax.ShapeDtypeStruct((tp, d), F32)] * 4,
        compiler_params=_cparams("parallel"),
    )(dh, a0, a1, h0, h1)


def _gated_h(proj, h0, h1, name):
    def fn(row0, lg, x0, x1):
        return _gelu(lg) * (x0 + x1)

    return _rows(fn, name, [Rw(proj, D_MODEL, C_LRU_G // D_MODEL), Rw(h0), Rw(h1)], [(D_MODEL, F32)])[0]


def _gated_h_bwd(dgh, proj, h0, h1, name):
    def fn(row0, dg, lg, x0, x1):
        act, dact = _gelu_and_grad(lg)
        return dg * (x0 + x1) * dact, dg * act

    return _rows(fn, name, [Rw(dgh), Rw(proj, D_MODEL, C_LRU_G // D_MODEL), Rw(h0), Rw(h1)], [(D_MODEL, F32)] * 2)


def _mix(proj, y_mla, y_lru, name):
    def fn(row0, gm, gl, ym, yl):
        return _sigmoid(gm) * ym + _sigmoid(gl) * yl

    return _rows(fn, name, [Rw(proj, D_MODEL, C_G_MLA // D_MODEL), Rw(proj, D_MODEL, C_G_LRU // D_MODEL), Rw(y_mla), Rw(y_lru)],
                 [(D_MODEL, F32)])[0]


def _mix_bwd(dz, proj, y_mla, y_lru, name):
    def fn(row0, dzb, gm, gl, ym, yl):
        sm, sl = _sigmoid(gm), _sigmoid(gl)
        return dzb * sm, dzb * sl, dzb * ym * sm * (1.0 - sm), dzb * yl * sl * (1.0 - sl)

    return _rows(fn, name, [Rw(dz), Rw(proj, D_MODEL, C_G_MLA // D_MODEL), Rw(proj, D_MODEL, C_G_LRU // D_MODEL),
                            Rw(y_mla), Rw(y_lru)], [(D_MODEL, F32)] * 4)


def _layer_fwd(h, w, cs, t_real, tag):
    proj = _matmul(h, w['w_in'], tag + "proj")
    cqn, ckvn = _mla_norms(proj, w['q_norm'], w['kv_norm'], tag + "mla_norms")
    qext = _matmul(cqn, w['w_q'], tag + "q_up")
    kv = _matmul(ckvn, w['w_kv'], tag + "kv_up")
    qc, kc, vb = _mla_pack(qext, kv, proj, cs, tag + "mla_pack")
    o, lse = _attn_fwd(qc, kc, vb, t_real, tag + "attn_fwd")
    y_mla = _matmul(o, w['w_o_mla'], tag + "o_mla")
    xc = _lru_conv_fwd(proj, w['lru_conv_w'], w['lru_conv_b'], t_real, tag + "lru_conv")
    r0, r1, i0, i1, a0, a1, u0, u1 = _lru_gates_fwd(xc, w['w_g'], w['b4'], w['lru_lambda'], t_real, tag + "lru_gates")
    h0, h1 = _scan_fwd(a0, u0, a1, u1, tag + "lru_scan")
    gh = _gated_h(proj, h0, h1, tag + "lru_gate_out")
    y_lru = _matmul(gh, w['w_o_lru'], tag + "o_lru")
    z = _mix(proj, y_mla, y_lru, tag + "mix")
    zo = _matmul(z, w['w_out'], tag + "w_out")
    hm = _ln_fwd([(DN_ALPHA, h), (1.0, zo)], w['ln1_g'], w['ln1_b'], tag + "ln1")
    up = _matmul(hm, w['w_up'], tag + "w_up")
    m = _ffn_conv_act(up, w['ffn_conv_w'], w['ffn_conv_b'], t_real, tag + "ffn_conv")
    f = _matmul(m, w['w_down'], tag + "w_down", tk_cap=1408)
    out = _ln_fwd([(DN_ALPHA, hm), (1.0, f)], w['ln2_g'], w['ln2_b'], tag + "ln2")
    saved = dict(h=h, proj=proj, cqn=cqn, ckvn=ckvn, qc=qc, kc=kc, vb=vb, o=o, lse=lse, y_mla=y_mla, xc=xc,
                 r0=r0, r1=r1, i0=i0, i1=i1, a0=a0, a1=a1, h0=h0, h1=h1, gh=gh, y_lru=y_lru, z=z, zo=zo, hm=hm,
                 up=up, m=m, f=f)
    return out, saved


def _layer_bwd(dout_terms, w, s, cs, t_real, tag):
    g = {}
    du2, dg2, db2 = _ln_bwd(dout_terms, [(DN_ALPHA, s['hm']), (1.0, s['f'])], w['ln2_g'], tag + "ln2_bwd")
    g['ln2_g'], g['ln2_b'] = dg2, db2
    dm = _matmul(du2, w['w_down'], tag + "w_down_dx", tb=True)
    g['w_down'] = _matmul(s['m'], du2, tag + "w_down_dw", ta=True, tk_cap=1408)
    dgp, dvp, dwg_, dwv_, dbg_, dbv_ = _ffn_conv_act_bwd(dm, s['up'], w['ffn_conv_w'], w['ffn_conv_b'], t_real, tag + "ffn_conv_bwd")
    g['ffn_conv_w'] = jnp.concatenate([dwg_, dwv_], axis=1)
    g['ffn_conv_b'] = jnp.concatenate([dbg_, dbv_], axis=1)
    dup = jnp.concatenate([dgp, dvp], axis=1)
    dhm_mm = _matmul(dup, w['w_up'], tag + "w_up_dx", tb=True, tk_cap=1408)
    g['w_up'] = _matmul(s['hm'], dup, tag + "w_up_dw", ta=True, tk_cap=1408)
    du1, dg1, db1 = _ln_bwd([(DN_ALPHA, du2), (1.0, dhm_mm)], [(DN_ALPHA, s['h']), (1.0, s['zo'])], w['ln1_g'], tag + "ln1_bwd")
    g['ln1_g'], g['ln1_b'] = dg1, db1
    dz = _matmul(du1, w['w_out'], tag + "w_out_dx", tb=True)
    g['w_out'] = _matmul(s['z'], du1, tag + "w_out_dw", ta=True, tk_cap=1408)
    dy_mla, dy_lru, dg_mla, dg_lru = _mix_bwd(dz, s['proj'], s['y_mla'], s['y_lru'], tag + "mix_bwd")
    do = _matmul(dy_mla, w['w_o_mla'], tag + "o_mla_dx", tb=True)
    g['w_o_mla'] = _matmul(s['o'], dy_mla, tag + "o_mla_dw", ta=True, tk_cap=1408)
    dqc, dkc, dv = _attn_bwd(s['qc'], s['kc'], s['vb'], do, s['o'], s['lse'], t_real, tag + "attn_bwd")
    dqext, dkn, dkrp = _mla_unpack(dqc, dkc, cs, tag + "mla_unpack")
    dkv = jnp.concatenate([dkn, dv], axis=1)
    dcqn = _matmul(dqext, w['w_q'], tag + "q_up_dx", tb=True)
    g['w_q'] = _matmul(s['cqn'], dqext, tag + "q_up_dw", ta=True, tk_cap=1408)
    dckvn = _matmul(dkv, w['w_kv'], tag + "kv_up_dx", tb=True)
    g['w_kv'] = _matmul(s['ckvn'], dkv, tag + "kv_up_dw", ta=True, tk_cap=1408)
    dcq, dckv, g['q_norm'], g['kv_norm'] = _mla_norms_bwd(dcqn, dckvn, s['proj'], w['q_norm'], w['kv_norm'], tag + "mla_norms_bwd")
    dgh = _matmul(dy_lru, w['w_o_lru'], tag + "o_lru_dx", tb=True)
    g['w_o_lru'] = _matmul(s['gh'], dy_lru, tag + "o_lru_dw", ta=True, tk_cap=1408)
    dlru_g, dhs = _gated_h_bwd(dgh, s['proj'], s['h0'], s['h1'], tag + "lru_gate_out_bwd")
    l0, l1, da0, da1 = _scan_bwd(dhs, s['a0'], s['a1'], s['h0'], s['h1'], tag + "lru_scan_bwd")
    dxc, g['w_g'], g['b4'], g['lru_lambda'] = _lru_gates_bwd(
        l0, l1, da0, da1, s['r0'], s['r1'], s['i0'], s['i1'], s['a0'], s['a1'], s['xc'], w['w_g'], w['lru_lambda'],
        t_real, tag + "lru_gates_bwd")
    dlru_x, g['lru_conv_w'], g['lru_conv_b'] = _lru_conv_bwd(dxc, s['proj'], w['lru_conv_w'], t_real, tag + "lru_conv_bwd")
    dproj = jnp.concatenate([dlru_g, dlru_x, dg_mla, dg_lru, dcq, dckv, dkrp], axis=1)
    dh_mm = _matmul(dproj, w['w_in'], tag + "proj_dx", tb=True, tk_cap=1536)
    g['w_in'] = _matmul(s['h'], dproj, tag + "proj_dw", ta=True, tk_cap=1408)
    return [(DN_ALPHA, du1), (1.0, dh_mm)], g


def _swap_halves(a, axis=-1):
    h1, h2 = jnp.split(a, 2, axis=axis)
    return jnp.concatenate([h2, h1], axis=axis)


def _layer_weights(full, l):
    w_in = full['w_in'][l]
    cq, ckv, kr, lg, lx, gm, gl = jnp.split(w_in, [256, 384, 448, 1472, 2496, 3520], axis=1)
    w = {'w_in': jnp.concatenate([lg, lx, gm, gl, cq, ckv, kr, _swap_halves(kr)], axis=1)}
    uq = full['w_uq'][l]
    w['w_q'] = jnp.concatenate([uq, _swap_halves(uq[..., QK_NOPE:])], axis=-1).reshape(Q_RANK, HEADS * 2 * LANE)
    w['w_kv'] = jnp.concatenate([full['w_uk'][l].reshape(KV_RANK, -1), full['w_uv'][l].reshape(KV_RANK, -1)], axis=1).astype(BF16)
    rg, ig = full['w_rg'][l], full['w_ig'][l]
    w['w_g'] = jnp.moveaxis(jnp.concatenate([rg, ig], axis=0), 0, 1).astype(BF16)
    w['b4'] = jnp.concatenate([full['b_rg'][l], full['b_ig'][l]], axis=0)
    for n in ('q_norm', 'kv_norm', 'w_o_mla', 'lru_conv_w', 'lru_conv_b', 'lru_lambda', 'w_o_lru', 'w_out', 'ln1_g',
              'ln1_b', 'w_up', 'ffn_conv_w', 'ffn_conv_b', 'w_down', 'ln2_g', 'ln2_b'):
        w[n] = full[n][l]
    return w


def _layer_grads(g):
    out = {}
    lg, lx, gm, gl, cq, ckv, kr, krs = jnp.split(g['w_in'], [1024, 2048, 3072, 4096, 4352, 4480, 4544], axis=1)
    out['w_in'] = jnp.concatenate([cq, ckv, kr + _swap_halves(krs), lg, lx, gm, gl], axis=1)
    gq = g['w_q'].reshape(Q_RANK, HEADS, 2 * LANE)
    out['w_uq'] = jnp.concatenate([gq[..., :QK_NOPE], gq[..., QK_NOPE:QK_NOPE + QK_ROPE] + _swap_halves(gq[..., QK_NOPE + QK_ROPE:])], axis=-1)
    out['w_uk'] = g['w_kv'][:, :HEADS * QK_NOPE].reshape(KV_RANK, HEADS, QK_NOPE)
    out['w_uv'] = g['w_kv'][:, HEADS * QK_NOPE:].reshape(KV_RANK, HEADS, V_HEAD)
    gg = jnp.moveaxis(g['w_g'], 1, 0)
    out['w_rg'], out['w_ig'] = gg[:2], gg[2:]
    out['b_rg'], out['b_ig'] = g['b4'][:2], g['b4'][2:]
    for n in ('q_norm', 'kv_norm', 'lru_conv_b', 'ln1_g', 'ln1_b', 'ffn_conv_b', 'ln2_g', 'ln2_b'):
        out[n] = g[n].reshape(-1)
    for n in ('w_o_mla', 'lru_conv_w', 'lru_lambda', 'w_o_lru', 'w_out', 'w_up', 'ffn_conv_w', 'w_down'):
        out[n] = g[n]
    return out


def _rope_table(tp):
    half = QK_ROPE // 2
    inv_freq = jnp.exp(-math.log(ROPE_THETA) * jnp.arange(half, dtype=F32) / half)
    ang = jnp.arange(tp, dtype=F32)[:, None] * inv_freq[None, :]
    c, s = jnp.cos(ang), jnp.sin(ang)
    return jnp.concatenate([c, c, -s, s], axis=1)


def _local_step(x, target, full, t_pad):
    seq = x.shape[0]
    t_real = N_META + seq
    zpad = jnp.zeros((t_pad - t_real, D_MODEL), F32)
    xin = jnp.concatenate([full['meta_tokens'], x, zpad], axis=0)
    tgt = jnp.concatenate([jnp.zeros((N_META, D_MODEL), F32), target, zpad], axis=0)
    cs = _rope_table(t_pad)
    h = _ln_fwd([(1.0, xin)], full['ln0_g'], full['ln0_b'], "ln0")
    saved, lw = [], []
    for l in range(DEPTH):
        lw.append(_layer_weights(full, l))
        h, s = _layer_fwd(h, lw[l], cs, t_real, "l%d_" % l)
        saved.append(s)
    dy, lossvec = _loss_head(h, tgt, t_real, "loss_head")
    loss = jnp.sum(lossvec)
    terms = [(1.0, dy)]
    layer_g = [None] * DEPTH
    for l in reversed(range(DEPTH)):
        terms, g = _layer_bwd(terms, lw[l], saved[l], cs, t_real, "l%d_" % l)
        layer_g[l] = _layer_grads(g)
    dxin, dg0, db0 = _ln_bwd(terms, [(1.0, xin)], full['ln0_g'], "ln0_bwd")
    grads = {n: jnp.stack([layer_g[l][n] for l in range(DEPTH)]) for n in layer_g[0]}
    grads['ln0_g'], grads['ln0_b'] = dg0.reshape(-1), db0.reshape(-1)
    grads['meta_tokens'] = dxin[:N_META]
    return loss, dxin[N_META:t_real], grads


_HBM = pl.BlockSpec(memory_space=pl.ANY)


def _gather_many(shards, name):
    nt = len(shards)

    def body(*refs):
        x_refs, out_refs = refs[:nt], refs[nt:2 * nt]
        send_sems, recv_sems, local_sems = refs[2 * nt:]
        x, y, c = lax.axis_index("x"), lax.axis_index("y"), lax.axis_index("c")
        me, sibling = (x, y, c), (x, y, 1 - c)
        chips = [(1 - x, y), (x, 1 - y), (1 - x, 1 - y)]

        def copy(t, k, block, to, own=False):
            px, py, pc = block
            slot = out_refs[t].at[4 * px + 2 * py + pc]
            return pltpu.make_async_remote_copy(
                src_ref=x_refs[t] if own else slot, dst_ref=slot,
                send_sem=send_sems.at[7 * t + k], recv_sem=recv_sems.at[7 * t + k],
                device_id=to, device_id_type=pl.DeviceIdType.MESH)

        mine = [pltpu.make_async_copy(x_refs[t], out_refs[t].at[4 * x + 2 * y + c], local_sems.at[t]) for t in range(nt)]
        sent = []
        for t in range(nt):
            mine[t].start()
            first = [copy(t, 0, me, sibling, own=True)]
            first += [copy(t, 1 + j, me, (*chip, c), own=True) for j, chip in enumerate(chips)]
            for cp in first:
                cp.start()
            sent += first
        for j, chip in enumerate(chips):
            for t in range(nt):
                copy(t, 1 + j, (*chip, c), me).wait_recv()
                passed = copy(t, 4 + j, (*chip, c), sibling)
                passed.start()
                sent.append(passed)
        for t in range(nt):
            copy(t, 0, sibling, me).wait_recv()
            for j, chip in enumerate(chips):
                copy(t, 4 + j, (*chip, 1 - c), me).wait_recv()
        for cp in sent:
            cp.wait_send()
        for cp in mine:
            cp.wait()

    return pl.pallas_call(
        body, name=name,
        out_shape=[jax.ShapeDtypeStruct((N_DEV,) + a.shape, a.dtype) for a in shards],
        in_specs=[_HBM] * nt, out_specs=[_HBM] * nt,
        scratch_shapes=[pltpu.SemaphoreType.DMA((7 * nt,)), pltpu.SemaphoreType.DMA((7 * nt,)), pltpu.SemaphoreType.DMA((nt,))],
    )(*shards)


def _exchange_many(items, name):
    nt = len(items)

    def body(*refs):
        in_refs, out_refs = refs[:nt], refs[nt:2 * nt]
        send_sems, recv_sems, local_sems = refs[2 * nt:]
        x, y, c = lax.axis_index("x"), lax.axis_index("y"), lax.axis_index("c")
        me = 4 * x + 2 * y + c

        def src(t, p):
            return in_refs[t].at[p] if items[t][1] else in_refs[t]

        mine = [pltpu.make_async_copy(src(t, me), out_refs[t].at[me], local_sems.at[t]) for t in range(nt)]
        for cp in mine:
            cp.start()
        copies = []
        for k in range(1, N_DEV):
            px = 1 - x if k & 4 else x
            py = 1 - y if k & 2 else y
            pc = 1 - c if k & 1 else c
            for t in range(nt):
                copies.append(pltpu.make_async_remote_copy(
                    src_ref=src(t, 4 * px + 2 * py + pc), dst_ref=out_refs[t].at[me],
                    send_sem=send_sems.at[7 * t + k - 1], recv_sem=recv_sems.at[7 * t + k - 1],
                    device_id=(px, py, pc), device_id_type=pl.DeviceIdType.MESH))
        for cp in copies:
            cp.start()
        for cp in copies:
            cp.wait_recv()
        for cp in copies:
            cp.wait_send()
        for cp in mine:
            cp.wait()

    out_shape = [jax.ShapeDtypeStruct(a.shape if sc else (N_DEV,) + a.shape, a.dtype) for a, sc in items]
    return pl.pallas_call(
        body, name=name, out_shape=out_shape, in_specs=[_HBM] * nt, out_specs=[_HBM] * nt,
        scratch_shapes=[pltpu.SemaphoreType.DMA((7 * nt,)), pltpu.SemaphoreType.DMA((7 * nt,)), pltpu.SemaphoreType.DMA((nt,))],
    )(*[a for a, _ in items])


def _as_rows(shape):
    return (1, shape[0]) if len(shape) == 1 else (math.prod(shape[:-1]), shape[-1])


def _sum_adamw(pieces, w, m, v, name):
    shape = pieces.shape[1:]
    npc = pieces.shape[0]
    rows, cols = _as_rows(shape)
    cap = max(16, (1 << 18) // cols // 16 * 16)
    tr = _tile(rows, cap, 16)
    c1 = 1.0 / (1.0 - ADAM_B1 ** ADAM_STEP)
    c2 = 1.0 / (1.0 - ADAM_B2 ** ADAM_STEP)
    adam = w is not None

    def body(*refs):
        p_ref = refs[0]
        gg = p_ref[0].astype(F32)
        for k in range(1, npc):
            gg = gg + p_ref[k].astype(F32)
        if not adam:
            refs[1][...] = gg
            return
        w_ref, m_ref, v_ref, g_ref, d_ref, nm_ref, nv_ref = refs[1:]
        nm = ADAM_B1 * m_ref[...] + (1.0 - ADAM_B1) * gg
        nv = ADAM_B2 * v_ref[...] + (1.0 - ADAM_B2) * (gg * gg)
        g_ref[...] = gg
        d_ref[...] = -ADAM_LR * ((nm * c1) / (jnp.sqrt(nv * c2) + ADAM_EPS) + ADAM_WD * w_ref[...])
        nm_ref[...] = nm
        nv_ref[...] = nv

    blk = pl.BlockSpec((tr, cols), lambda i: (i, 0))
    ins = [pieces.reshape(npc, rows, cols)] + ([a.reshape(rows, cols) for a in (w, m, v)] if adam else [])
    n_out = 4 if adam else 1
    res = pl.pallas_call(
        body, name=name, grid=(rows // tr,),
        in_specs=[pl.BlockSpec((npc, tr, cols), lambda i: (0, i, 0))] + [blk] * (len(ins) - 1),
        out_specs=[blk] * n_out,
        out_shape=[jax.ShapeDtypeStruct((rows, cols), F32)] * n_out,
        compiler_params=_cparams("parallel"),
    )(*ins)
    return [r.reshape(shape) for r in res]


def _to_shards(full, axis):
    shp = full.shape
    a = full.reshape(shp[:axis] + (N_DEV, shp[axis] // N_DEV) + shp[axis + 1:])
    return jnp.moveaxis(a, axis, 0)


def _from_shards(blocks, axis):
    a = jnp.moveaxis(blocks, 0, axis)
    shp = a.shape
    return a.reshape(shp[:axis] + (shp[axis] * shp[axis + 1],) + shp[axis + 2:])


def kernel(x, meta_tokens, ln0_g, ln0_b, w_in, q_norm, kv_norm, w_uq, w_uk, w_uv, w_o_mla, lru_conv_w, lru_conv_b, w_rg, b_rg, w_ig, b_ig, lru_lambda, w_o_lru, w_out, ln1_g, ln1_b, w_up, ffn_conv_w, ffn_conv_b, w_down, ln2_g, ln2_b, loss_target, m_meta_tokens, m_ln0_g, m_ln0_b, m_w_in, m_q_norm, m_kv_norm, m_w_uq, m_w_uk, m_w_uv, m_w_o_mla, m_lru_conv_w, m_lru_conv_b, m_w_rg, m_b_rg, m_w_ig, m_b_ig, m_lru_lambda, m_w_o_lru, m_w_out, m_ln1_g, m_ln1_b, m_w_up, m_ffn_conv_w, m_ffn_conv_b, m_w_down, m_ln2_g, m_ln2_b, v_meta_tokens, v_ln0_g, v_ln0_b, v_w_in, v_q_norm, v_kv_norm, v_w_uq, v_w_uk, v_w_uv, v_w_o_mla, v_lru_conv_w, v_lru_conv_b, v_w_rg, v_b_rg, v_w_ig, v_b_ig, v_lru_lambda, v_w_o_lru, v_w_out, v_ln1_g, v_ln1_b, v_w_up, v_ffn_conv_w, v_ffn_conv_b, v_w_down, v_ln2_g, v_ln2_b):
    args = (meta_tokens, ln0_g, ln0_b, w_in, q_norm, kv_norm, w_uq, w_uk, w_uv, w_o_mla, lru_conv_w, lru_conv_b, w_rg, b_rg, w_ig, b_ig, lru_lambda, w_o_lru, w_out, ln1_g, ln1_b, w_up, ffn_conv_w, ffn_conv_b, w_down, ln2_g, ln2_b)
    ms = (m_meta_tokens, m_ln0_g, m_ln0_b, m_w_in, m_q_norm, m_kv_norm, m_w_uq, m_w_uk, m_w_uv, m_w_o_mla, m_lru_conv_w, m_lru_conv_b, m_w_rg, m_b_rg, m_w_ig, m_b_ig, m_lru_lambda, m_w_o_lru, m_w_out, m_ln1_g, m_ln1_b, m_w_up, m_ffn_conv_w, m_ffn_conv_b, m_w_down, m_ln2_g, m_ln2_b)
    vs = (v_meta_tokens, v_ln0_g, v_ln0_b, v_w_in, v_q_norm, v_kv_norm, v_w_uq, v_w_uk, v_w_uv, v_w_o_mla, v_lru_conv_w, v_lru_conv_b, v_w_rg, v_b_rg, v_w_ig, v_b_ig, v_lru_lambda, v_w_o_lru, v_w_out, v_ln1_g, v_ln1_b, v_w_up, v_ffn_conv_w, v_ffn_conv_b, v_w_down, v_ln2_g, v_ln2_b)
    wd, md, vd = dict(zip(WEIGHTS, args)), dict(zip(WEIGHTS, ms)), dict(zip(WEIGHTS, vs))

    full = {n: wd[n] for n in REPLICATED}
    got = _gather_many([wd[n].astype(BF16 if n in BIG else F32) for n in SHARDED], "gather_weights")
    for n, blocks in zip(SHARDED, got):
        full[n] = _from_shards(blocks, SHARD_AXIS[n])

    seq = x.shape[1]
    t_pad = -(-(N_META + seq + MIN_PAD_ROWS) // LANE) * LANE
    loss, grad_x, grads = _local_step(x[0], loss_target[0], full, t_pad)
    loss = lax.psum(loss, ("x", "y", "c"))

    items, kinds = [], []
    for n in WEIGHTS:
        g = grads[n]
        if n in SHARD_AXIS:
            g = _to_shards(g, SHARD_AXIS[n])
            items.append((g.astype(BF16) if n in BIG else g, True))
            kinds.append("sharded")
        elif n in LARGE_REPLICATED:
            rows, cols = _as_rows(g.shape)
            items.append((g.reshape(N_DEV, rows // N_DEV, cols).astype(BF16), True))
            kinds.append("large")
        else:
            items.append((g, False))
            kinds.append("small")
    pieces = dict(zip(WEIGHTS, _exchange_many(items, "grad_exchange")))

    large = [n for n, k in zip(WEIGHTS, kinds) if k == "large"]
    owned = [_sum_adamw(pieces[n], None, None, None, "grad_sum_" + n)[0] for n in large]
    for n, blocks in zip(large, _gather_many(owned, "gather_large_grads")):
        pieces[n] = blocks.reshape((1,) + wd[n].shape)
    outs = {n: _sum_adamw(pieces[n], wd[n], md[n], vd[n], "adamw_" + n) for n in WEIGHTS}
    res = [loss, grad_x[None]]
    for k in range(4):
        res += [outs[n][k] for n in WEIGHTS]
    return tuple(res)
```

```python
import functools
import math

import jax
import jax.numpy as jnp
from jax import lax
from jax.experimental import pallas as pl
from jax.experimental.pallas import tpu as pltpu

F32 = jnp.float32
BF16 = jnp.bfloat16

N_DEV = 8
D_MODEL = 1024
N_META = 16
HEADS = 8
QK_NOPE = 128
QK_ROPE = 64
V_HEAD = 128
Q_RANK = 256
KV_RANK = 128
ROPE_THETA = 10000.0
LRU_BLOCKS = 8
LRU_C = 8.0
D_FF = 2816
DEPTH = 2
DN_ALPHA = (2.0 * DEPTH) ** 0.25
LN_EPS = 1e-5
RMS_EPS = 1e-6
LN2 = math.log(2.0)
ATT_SCALE = 1.0 / math.sqrt(QK_NOPE + QK_ROPE) / LN2
NEG_BIG = -1e30

ADAM_LR = 0.001
ADAM_B1 = 0.9
ADAM_B2 = 0.999
ADAM_EPS = 1e-08
ADAM_WD = 0.01
ADAM_STEP = 10

MIN_PAD_ROWS = 2
LANE = 128
SUBLANE = 8
VMEM_LIMIT = 56 * 1024 * 1024

PROJ_COLS = 4 * D_MODEL + Q_RANK + KV_RANK + 2 * QK_ROPE
C_LRU_G, C_LRU_X, C_G_MLA, C_G_LRU = 0, D_MODEL, 2 * D_MODEL, 3 * D_MODEL
C_CQ = 4 * D_MODEL
C_CKV = C_CQ + Q_RANK
C_KRP = C_CKV + KV_RANK

WEIGHTS = ['meta_tokens', 'ln0_g', 'ln0_b', 'w_in', 'q_norm', 'kv_norm', 'w_uq', 'w_uk', 'w_uv', 'w_o_mla',
           'lru_conv_w', 'lru_conv_b', 'w_rg', 'b_rg', 'w_ig', 'b_ig', 'lru_lambda', 'w_o_lru', 'w_out',
           'ln1_g', 'ln1_b', 'w_up', 'ffn_conv_w', 'ffn_conv_b', 'w_down', 'ln2_g', 'ln2_b']
SHARD_AXIS = {'meta_tokens': 1, 'w_in': 2, 'w_uq': 1, 'w_o_mla': 1, 'lru_conv_w': 2, 'b_rg': 2, 'b_ig': 2,
              'lru_lambda': 2, 'w_o_lru': 1, 'w_out': 1, 'w_up': 2, 'ffn_conv_w': 2, 'w_down': 1}
BIG = ['w_in', 'w_uq', 'w_o_mla', 'w_o_lru', 'w_out', 'w_up', 'w_down']
SHARDED = [n for n in WEIGHTS if n in SHARD_AXIS]
REPLICATED = [n for n in WEIGHTS if n not in SHARD_AXIS]
LARGE_REPLICATED = ['w_uk', 'w_uv', 'w_rg', 'w_ig']


def _cparams(*sem):
    return pltpu.CompilerParams(dimension_semantics=sem, vmem_limit_bytes=VMEM_LIMIT)


def _tile(n, cap, unit=LANE):
    best = None
    t = unit
    while t <= min(n, cap):
        if n % t == 0:
            best = t
        t += unit
    return n if best is None else best


def _sigmoid(x):
    return 1.0 / (1.0 + jnp.exp(-x))


_GELU_C = math.sqrt(2.0 / math.pi)


def _gelu(x):
    t = jnp.tanh(_GELU_C * (x + 0.044715 * x * x * x))
    return 0.5 * x * (1.0 + t)


def _gelu_and_grad(x):
    t = jnp.tanh(_GELU_C * (x + 0.044715 * x * x * x))
    g = 0.5 * x * (1.0 + t)
    dg = 0.5 * (1.0 + t) + 0.5 * x * (1.0 - t * t) * _GELU_C * (1.0 + 3.0 * 0.044715 * x * x)
    return g, dg


def _softplus_neg(lam):
    z = jnp.exp(-jnp.abs(lam))
    w = 1.0 + z
    log1p = jnp.where(w == 1.0, z, jnp.log(w) * z / (w - 1.0))
    return jnp.maximum(-lam, 0.0) + log1p


def _row_ids(shape, row0=0):
    return lax.broadcasted_iota(jnp.int32, shape, 0) + row0


def _matmul(a, b, name, ta=False, tb=False, out_dtype=F32, tm_cap=1408, tn_cap=1024, tk_cap=2048):
    if ta:
        kdim, m = a.shape
    else:
        m, kdim = a.shape
    if tb:
        n, k2 = b.shape
    else:
        k2, n = b.shape
    assert kdim == k2, (a.shape, b.shape, ta, tb)
    tm, tn, tk = _tile(m, tm_cap), _tile(n, tn_cap), _tile(kdim, tk_cap)
    nk = kdim // tk

    def body(a_ref, b_ref, o_ref, acc_ref):
        k = pl.program_id(2)

        @pl.when(k == 0)
        def _():
            acc_ref[...] = jnp.zeros_like(acc_ref)

        dn = (((0 if ta else 1,), (1 if tb else 0,)), ((), ()))
        acc_ref[...] += lax.dot_general(a_ref[...].astype(BF16), b_ref[...].astype(BF16), dn,
                                        preferred_element_type=F32)

        @pl.when(k == nk - 1)
        def _():
            o_ref[...] = acc_ref[...].astype(o_ref.dtype)

    a_spec = pl.BlockSpec((tk, tm), lambda i, j, k: (k, i)) if ta else pl.BlockSpec((tm, tk), lambda i, j, k: (i, k))
    b_spec = pl.BlockSpec((tn, tk), lambda i, j, k: (j, k)) if tb else pl.BlockSpec((tk, tn), lambda i, j, k: (k, j))
    return pl.pallas_call(
        body, name=name,
        grid=(m // tm, n // tn, nk),
        in_specs=[a_spec, b_spec],
        out_specs=pl.BlockSpec((tm, tn), lambda i, j, k: (i, j)),
        out_shape=jax.ShapeDtypeStruct((m, n), out_dtype),
        scratch_shapes=[pltpu.VMEM((tm, tn), F32)],
        compiler_params=_cparams("parallel", "parallel", "arbitrary"),
    )(a, b)


class Rw:
    def __init__(self, arr, width=None, cb=0):
        self.arr, self.width, self.cb = arr, (arr.shape[1] if width is None else width), cb


class Pm:
    def __init__(self, arr):
        self.arr = arr


def _rows(fn, name, ins, outs, accs=(), tm_cap=384):
    tp = next(o.arr.shape[0] for o in ins if isinstance(o, Rw))
    tm = _tile(tp, tm_cap)
    n_in, n_out, n_acc = len(ins), len(outs), len(accs)

    def body(*refs):
        i = pl.program_id(0)
        res = fn(i * tm, *[r[...] for r in refs[:n_in]])
        if not isinstance(res, (tuple, list)):
            res = (res,)
        assert len(res) == n_out + n_acc, (name, len(res))
        for k in range(n_out):
            refs[n_in + k][...] = res[k].astype(refs[n_in + k].dtype)
        for k in range(n_acc):
            ref = refs[n_in + n_out + k]

            @pl.when(i == 0)
            def _():
                ref[...] = jnp.zeros_like(ref)

            ref[...] += res[n_out + k]

    in_specs = []
    for o in ins:
        if isinstance(o, Rw):
            in_specs.append(pl.BlockSpec((tm, o.width), functools.partial(lambda i, cb: (i, cb), cb=o.cb)))
        else:
            in_specs.append(pl.BlockSpec(o.arr.shape, functools.partial(lambda i, nd: (0,) * nd, nd=o.arr.ndim)))
    out_specs = [pl.BlockSpec((tm, w), lambda i: (i, 0)) for (w, _) in outs]
    out_specs += [pl.BlockSpec(s, functools.partial(lambda i, nd: (0,) * nd, nd=len(s))) for s in accs]
    out_shape = [jax.ShapeDtypeStruct((tp, w), dt) for (w, dt) in outs]
    out_shape += [jax.ShapeDtypeStruct(s, F32) for s in accs]
    res = pl.pallas_call(
        body, name=name, grid=(tp // tm,), in_specs=in_specs, out_specs=out_specs, out_shape=out_shape,
        compiler_params=_cparams("arbitrary"),
    )(*[o.arr for o in ins])
    return res


class Cl:
    def __init__(self, arr, col0=0):
        self.arr, self.col0 = arr, col0


def _cols(fn, name, ins, outs, ncols, tc):
    assert ncols % tc == 0
    n_in, n_out = len(ins), len(outs)

    def body(*refs):
        res = fn(*[r[...] for r in refs[:n_in]])
        if not isinstance(res, (tuple, list)):
            res = (res,)
        assert len(res) == n_out, (name, len(res))
        for k in range(n_out):
            refs[n_in + k][...] = res[k].astype(refs[n_in + k].dtype)

    in_specs = []
    for o in ins:
        assert o.col0 % tc == 0, (name, o.col0, tc)
        in_specs.append(pl.BlockSpec((o.arr.shape[0], tc), functools.partial(lambda j, off: (0, j + off), off=o.col0 // tc)))
    out_specs = [pl.BlockSpec((r, tc), lambda j: (0, j)) for (r, _) in outs]
    out_shape = [jax.ShapeDtypeStruct((r, ncols), dt) for (r, dt) in outs]
    return pl.pallas_call(
        body, name=name, grid=(ncols // tc,), in_specs=in_specs, out_specs=out_specs, out_shape=out_shape,
        compiler_params=_cparams("parallel"),
    )(*[o.arr for o in ins])


def _ln_stats(u):
    mu = jnp.mean(u, axis=-1, keepdims=True)
    xc = u - mu
    var = jnp.mean(xc * xc, axis=-1, keepdims=True)
    rstd = lax.rsqrt(var + LN_EPS)
    return xc * rstd, rstd


def _ln_fwd(terms, g, b, name):
    coefs = [c for c, _ in terms]

    def fn(row0, *blk):
        xs, (gg, bb) = blk[:len(coefs)], blk[len(coefs):]
        u = sum(c * x for c, x in zip(coefs, xs))
        xhat, _ = _ln_stats(u)
        return xhat * gg + bb

    d = terms[0][1].shape[1]
    return _rows(fn, name, [Rw(x) for _, x in terms] + [Pm(g.reshape(1, d)), Pm(b.reshape(1, d))], [(d, F32)])[0]


def _ln_bwd(dy_terms, u_terms, g, name):
    dc = [c for c, _ in dy_terms]
    uc = [c for c, _ in u_terms]
    d = u_terms[0][1].shape[1]

    def fn(row0, *blk):
        dys = blk[:len(dc)]
        xs = blk[len(dc):len(dc) + len(uc)]
        gg = blk[-1]
        dy = sum(c * x for c, x in zip(dc, dys))
        u = sum(c * x for c, x in zip(uc, xs))
        xhat, rstd = _ln_stats(u)
        gdy = dy * gg
        m1 = jnp.mean(gdy, axis=-1, keepdims=True)
        m2 = jnp.mean(gdy * xhat, axis=-1, keepdims=True)
        du = rstd * (gdy - m1 - xhat * m2)
        return du, jnp.sum(dy * xhat, axis=0, keepdims=True), jnp.sum(dy, axis=0, keepdims=True)

    ins = [Rw(x) for _, x in dy_terms] + [Rw(x) for _, x in u_terms] + [Pm(g.reshape(1, d))]
    return _rows(fn, name, ins, [(d, F32)], accs=[(1, d), (1, d)])


def _loss_head(y, tgt, t_real, name):
    d = y.shape[1]

    def fn(row0, yb, tb):
        rows = _row_ids(yb.shape, row0)
        live = (rows >= N_META) & (rows < t_real)
        diff = jnp.where(live, yb - tb, 0.0)
        return diff * (1.0 / d), jnp.sum(diff * diff, axis=0, keepdims=True) * (0.5 / d)

    return _rows(fn, name, [Rw(y), Rw(tgt)], [(d, F32)], accs=[(1, d)])


def _rms(x, g):
    r = lax.rsqrt(jnp.mean(x * x, axis=-1, keepdims=True) + RMS_EPS)
    return x * r * g


def _rms_bwd(dy, x, g):
    r = lax.rsqrt(jnp.mean(x * x, axis=-1, keepdims=True) + RMS_EPS)
    gdy = dy * g
    dx = r * gdy - x * (r * r * r) * jnp.mean(gdy * x, axis=-1, keepdims=True)
    return dx, jnp.sum(dy * x * r, axis=0, keepdims=True)


def _mla_norms(proj, qn, kvn, name):
    def fn(row0, cq, ckv, g1, g2):
        return _rms(cq, g1), _rms(ckv, g2)

    return _rows(fn, name, [Rw(proj, Q_RANK, C_CQ // Q_RANK), Rw(proj, KV_RANK, C_CKV // KV_RANK),
                            Pm(qn.reshape(1, Q_RANK)), Pm(kvn.reshape(1, KV_RANK))],
                 [(Q_RANK, F32), (KV_RANK, F32)])


def _mla_norms_bwd(dcqn, dckvn, proj, qn, kvn, name):
    def fn(row0, d1, d2, cq, ckv, g1, g2):
        dx1, dg1 = _rms_bwd(d1, cq, g1)
        dx2, dg2 = _rms_bwd(d2, ckv, g2)
        return dx1, dx2, dg1, dg2

    return _rows(fn, name, [Rw(dcqn), Rw(dckvn), Rw(proj, Q_RANK, C_CQ // Q_RANK), Rw(proj, KV_RANK, C_CKV // KV_RANK),
                            Pm(qn.reshape(1, Q_RANK)), Pm(kvn.reshape(1, KV_RANK))],
                 [(Q_RANK, F32), (KV_RANK, F32)], accs=[(1, Q_RANK), (1, KV_RANK)])


def _fold_rope(z):
    return z + pltpu.roll(z, QK_ROPE, 1)


def _mla_pack(qext, kv, proj, cs, name):
    tp = qext.shape[0]
    tm = _tile(tp, 384)

    def body(q_ref, kn_ref, v_ref, kr_ref, cs_ref, qo_ref, ko_ref, vo_ref):
        cs_ = cs_ref[...]
        low = lax.broadcasted_iota(jnp.int32, cs_.shape, 1) < QK_ROPE
        q = q_ref[...]
        qr = jnp.where(low, _fold_rope(q[:, QK_NOPE:] * cs_), 0.0)
        qo_ref[...] = (jnp.concatenate([q[:, :QK_NOPE], qr], axis=1) * ATT_SCALE).astype(BF16)
        kr = _fold_rope(kr_ref[...] * cs_)
        ko_ref[...] = jnp.concatenate([kn_ref[...], kr], axis=1).astype(BF16)
        vo_ref[...] = v_ref[...].astype(BF16)

    return pl.pallas_call(
        body, name=name, grid=(tp // tm, HEADS),
        in_specs=[pl.BlockSpec((tm, 2 * LANE), lambda i, h: (i, h)),
                  pl.BlockSpec((tm, LANE), lambda i, h: (i, h)),
                  pl.BlockSpec((tm, LANE), lambda i, h: (i, HEADS + h)),
                  pl.BlockSpec((tm, LANE), lambda i, h: (i, C_KRP // LANE)),
                  pl.BlockSpec((tm, LANE), lambda i, h: (i, 0))],
        out_specs=[pl.BlockSpec((tm, 2 * LANE), lambda i, h: (i, h)),
                   pl.BlockSpec((tm, 2 * LANE), lambda i, h: (i, h)),
                   pl.BlockSpec((tm, LANE), lambda i, h: (i, h))],
        out_shape=[jax.ShapeDtypeStruct((tp, HEADS * 2 * LANE), BF16),
                   jax.ShapeDtypeStruct((tp, HEADS * 2 * LANE), BF16),
                   jax.ShapeDtypeStruct((tp, HEADS * LANE), BF16)],
        compiler_params=_cparams("parallel", "arbitrary"),
    )(qext, kv, kv, proj, cs)


def _mla_unpack(dq, dk, cs, name):
    tp = dq.shape[0]
    tm = _tile(tp, 384)

    def body(dq_ref, dk_ref, cs_ref, dqe_ref, dkn_ref, dkr_ref):
        h = pl.program_id(1)
        cs_ = cs_ref[...]
        low = lax.broadcasted_iota(jnp.int32, cs_.shape, 1) < QK_ROPE
        dq_ = dq_ref[...] * ATT_SCALE
        dqr = _fold_rope(jnp.where(low, dq_[:, QK_NOPE:], 0.0)) * cs_
        dqe_ref[...] = jnp.concatenate([dq_[:, :QK_NOPE], dqr], axis=1)
        dk_ = dk_ref[...]
        dkn_ref[...] = dk_[:, :QK_NOPE]

        @pl.when(h == 0)
        def _():
            dkr_ref[...] = jnp.zeros_like(dkr_ref)

        dkr_ref[...] += _fold_rope(jnp.where(low, dk_[:, QK_NOPE:], 0.0)) * cs_

    return pl.pallas_call(
        body, name=name, grid=(tp // tm, HEADS),
        in_specs=[pl.BlockSpec((tm, 2 * LANE), lambda i, h: (i, h)),
                  pl.BlockSpec((tm, 2 * LANE), lambda i, h: (i, h)),
                  pl.BlockSpec((tm, LANE), lambda i, h: (i, 0))],
        out_specs=[pl.BlockSpec((tm, 2 * LANE), lambda i, h: (i, h)),
                   pl.BlockSpec((tm, LANE), lambda i, h: (i, h)),
                   pl.BlockSpec((tm, LANE), lambda i, h: (i, 0))],
        out_shape=[jax.ShapeDtypeStruct((tp, HEADS * 2 * LANE), F32),
                   jax.ShapeDtypeStruct((tp, HEADS * LANE), F32),
                   jax.ShapeDtypeStruct((tp, LANE), F32)],
        compiler_params=_cparams("parallel", "arbitrary"),
    )(dq, dk, cs)


def _attn_fwd(q, k, v, t_real, name):
    tp = q.shape[0]
    tq = _tile(tp, 384)
    tkc = _tile(tp, 1408)
    nkc = -(-t_real // tkc)

    def body(q_ref, k_ref, v_ref, o_ref, lse_ref):
        qb = q_ref[...]
        m = l = acc = None
        for c in range(nkc):
            s = lax.dot_general(qb, k_ref[c * tkc:(c + 1) * tkc, :], (((1,), (1,)), ((), ())), preferred_element_type=F32)
            if (c + 1) * tkc > t_real:
                cols = lax.broadcasted_iota(jnp.int32, s.shape, 1) + c * tkc
                s = jnp.where(cols < t_real, s, NEG_BIG)
            mc = jnp.max(s, axis=-1, keepdims=True)
            m_new = mc if c == 0 else jnp.maximum(m, mc)
            p = jnp.exp2(s - m_new)
            lc = jnp.sum(p, axis=-1, keepdims=True)
            pv = jnp.dot(p.astype(BF16), v_ref[c * tkc:(c + 1) * tkc, :], preferred_element_type=F32)
            if c == 0:
                l, acc = lc, pv
            else:
                alpha = jnp.exp2(m - m_new)
                l, acc = alpha * l + lc, alpha * acc + pv
            m = m_new
        o_ref[...] = acc / l
        lse_ref[...] = m + jnp.log2(l)

    return pl.pallas_call(
        body, name=name, grid=(HEADS, tp // tq),
        in_specs=[pl.BlockSpec((tq, 2 * LANE), lambda h, i: (i, h)),
                  pl.BlockSpec((tp, 2 * LANE), lambda h, i: (0, h)),
                  pl.BlockSpec((tp, LANE), lambda h, i: (0, h))],
        out_specs=[pl.BlockSpec((tq, LANE), lambda h, i: (i, h)),
                   pl.BlockSpec((None, tq, 1), lambda h, i: (h, i, 0))],
        out_shape=[jax.ShapeDtypeStruct((tp, HEADS * LANE), F32),
                   jax.ShapeDtypeStruct((HEADS, tp, 1), F32)],
        compiler_params=_cparams("parallel", "parallel"),
    )(q, k, v)


def _attn_bwd(q, k, v, do, o, lse, t_real, name):
    tp = q.shape[0]
    tq = _tile(tp, 384)
    tkc = _tile(tp, 1408)
    nkc = -(-t_real // tkc)

    def body(q_ref, k_ref, v_ref, do_ref, o_ref, lse_ref, dq_ref, dk_ref, dv_ref):
        i = pl.program_id(1)

        @pl.when(i == 0)
        def _():
            dk_ref[...] = jnp.zeros_like(dk_ref)
            dv_ref[...] = jnp.zeros_like(dv_ref)

        qb = q_ref[...]
        dob = do_ref[...]
        dob16 = dob.astype(BF16)
        dol2 = (dob * LN2).astype(BF16)
        delta = jnp.sum(dob * o_ref[...], axis=-1, keepdims=True) * LN2
        lse = lse_ref[...]
        dq = None
        for c in range(nkc):
            ks = slice(c * tkc, (c + 1) * tkc)
            kb = k_ref[ks, :]
            s = lax.dot_general(qb, kb, (((1,), (1,)), ((), ())), preferred_element_type=F32)
            p = jnp.exp2(s - lse)
            if (c + 1) * tkc > t_real:
                cols = lax.broadcasted_iota(jnp.int32, s.shape, 1) + c * tkc
                p = jnp.where(cols < t_real, p, 0.0)
            dp = lax.dot_general(dol2, v_ref[ks, :], (((1,), (1,)), ((), ())), preferred_element_type=F32)
            ds = (p * (dp - delta)).astype(BF16)
            dqc = jnp.dot(ds, kb, preferred_element_type=F32)
            dq = dqc if c == 0 else dq + dqc
            dk_ref[ks, :] += lax.dot_general(ds, qb, (((0,), (0,)), ((), ())), preferred_element_type=F32)
            dv_ref[ks, :] += lax.dot_general(p.astype(BF16), dob16, (((0,), (0,)), ((), ())), preferred_element_type=F32)
        dq_ref[...] = dq

    return pl.pallas_call(
        body, name=name, grid=(HEADS, tp // tq),
        in_specs=[pl.BlockSpec((tq, 2 * LANE), lambda h, i: (i, h)),
                  pl.BlockSpec((tp, 2 * LANE), lambda h, i: (0, h)),
                  pl.BlockSpec((tp, LANE), lambda h, i: (0, h)),
                  pl.BlockSpec((tq, LANE), lambda h, i: (i, h)),
                  pl.BlockSpec((tq, LANE), lambda h, i: (i, h)),
                  pl.BlockSpec((None, tq, 1), lambda h, i: (h, i, 0))],
        out_specs=[pl.BlockSpec((tq, 2 * LANE), lambda h, i: (i, h)),
                   pl.BlockSpec((tp, 2 * LANE), lambda h, i: (0, h)),
                   pl.BlockSpec((tp, LANE), lambda h, i: (0, h))],
        out_shape=[jax.ShapeDtypeStruct((tp, HEADS * 2 * LANE), F32),
                   jax.ShapeDtypeStruct((tp, HEADS * 2 * LANE), F32),
                   jax.ShapeDtypeStruct((tp, HEADS * LANE), F32)],
        compiler_params=_cparams("parallel", "arbitrary"),
    )(q, k, v, do, o, lse)


def _shift_rows(x, s):
    tp = x.shape[0]
    return x if s % tp == 0 else pltpu.roll(x, s % tp, 0)


def _conv_fwd_val(xm, w, b, pad_left):
    acc = b + w[0:1, :] * _shift_rows(xm, pad_left)
    for k in range(1, w.shape[0]):
        acc = acc + w[k:k + 1, :] * _shift_rows(xm, pad_left - k)
    return acc


def _conv_bwd_val(dy, xm, w, pad_left, live):
    kk = w.shape[0]
    dx = w[0:1, :] * _shift_rows(dy, -pad_left)
    dws = [jnp.sum(dy * _shift_rows(xm, pad_left), axis=0, keepdims=True)]
    for k in range(1, kk):
        dx = dx + w[k:k + 1, :] * _shift_rows(dy, k - pad_left)
        dws.append(jnp.sum(dy * _shift_rows(xm, pad_left - k), axis=0, keepdims=True))
    return jnp.where(live, dx, 0.0), jnp.concatenate(dws, axis=0), jnp.sum(dy, axis=0, keepdims=True)


def _lru_conv_fwd(proj, w, b, t_real, name):
    def fn(x, ww, bb):
        xm = jnp.where(_row_ids(x.shape) < t_real, x, 0.0)
        return _conv_fwd_val(xm, ww, bb, 2)

    return _cols(fn, name, [Cl(proj, C_LRU_X), Cl(w), Cl(b.reshape(1, -1))], [(proj.shape[0], F32)], D_MODEL, 128)[0]


def _lru_conv_bwd(dxc, proj, w, t_real, name):
    def fn(dy, x, ww):
        live = _row_ids(x.shape) < t_real
        xm = jnp.where(live, x, 0.0)
        dym = jnp.where(live, dy, 0.0)
        return _conv_bwd_val(dym, xm, ww, 2, live)

    return _cols(fn, name, [Cl(dxc), Cl(proj, C_LRU_X), Cl(w)],
                 [(proj.shape[0], F32), (w.shape[0], F32), (1, F32)], D_MODEL, 128)


def _ffn_conv_act(up, w, b, t_real, name):
    def fn(g, v, wg, wv, bg, bv):
        live = _row_ids(g.shape) < t_real
        gc = _conv_fwd_val(jnp.where(live, g, 0.0), wg, bg, 1)
        vc = _conv_fwd_val(jnp.where(live, v, 0.0), wv, bv, 1)
        return _gelu(gc) * vc

    b2 = b.reshape(1, -1)
    return _cols(fn, name, [Cl(up), Cl(up, D_FF), Cl(w), Cl(w, D_FF), Cl(b2), Cl(b2, D_FF)],
                 [(up.shape[0], F32)], D_FF, 128)[0]


def _ffn_conv_act_bwd(dm, up, w, b, t_real, name):
    def fn(dmb, g, v, wg, wv, bg, bv):
        live = _row_ids(g.shape) < t_real
        gm, vm = jnp.where(live, g, 0.0), jnp.where(live, v, 0.0)
        gc = _conv_fwd_val(gm, wg, bg, 1)
        vc = _conv_fwd_val(vm, wv, bv, 1)
        act, dact = _gelu_and_grad(gc)
        dmm = jnp.where(live, dmb, 0.0)
        dgx, dwg, dbg = _conv_bwd_val(dmm * vc * dact, gm, wg, 1, live)
        dvx, dwv, dbv = _conv_bwd_val(dmm * act, vm, wv, 1, live)
        return dgx, dvx, dwg, dwv, dbg, dbv

    b2 = b.reshape(1, -1)
    tp, kk = up.shape[0], w.shape[0]
    return _cols(fn, name, [Cl(dm), Cl(up), Cl(up, D_FF), Cl(w), Cl(w, D_FF), Cl(b2), Cl(b2, D_FF)],
                 [(tp, F32), (tp, F32), (kk, F32), (kk, F32), (1, F32), (1, F32)], D_FF, 128)


def _lru_gates_fwd(xc, wg, b4, lam, t_real, name):
    tp = xc.shape[0]
    tm = _tile(tp, 1408)

    def body(x_ref, w_ref, b_ref, lam_ref, r0_ref, r1_ref, i0_ref, i1_ref, a0_ref, a1_ref, u0_ref, u1_ref):
        x = x_ref[...]
        xb = x.astype(BF16)
        live = _row_ids(x.shape, pl.program_id(1) * tm) < t_real
        bb = b_ref[...]
        sp = _softplus_neg(lam_ref[...])
        gate = [_sigmoid(jnp.dot(xb, w_ref[k], preferred_element_type=F32) + bb[k:k + 1, :]) for k in range(4)]
        for d, (r_ref, i_ref, a_ref, u_ref) in enumerate(((r0_ref, i0_ref, a0_ref, u0_ref), (r1_ref, i1_ref, a1_ref, u1_ref))):
            r, ig = gate[d], gate[2 + d]
            a = jnp.exp(-LRU_C * r * sp[d:d + 1, :])
            r_ref[...] = r
            i_ref[...] = ig
            a_ref[...] = a
            u_ref[...] = jnp.where(live, jnp.sqrt(1.0 - a * a) * (ig * x), 0.0)

    blk = pl.BlockSpec((tm, LANE), lambda g, i: (i, g))
    return pl.pallas_call(
        body, name=name, grid=(LRU_BLOCKS, tp // tm),
        in_specs=[blk, pl.BlockSpec((None, 4, LANE, LANE), lambda g, i: (g, 0, 0, 0)),
                  pl.BlockSpec((4, LANE), lambda g, i: (0, g)), pl.BlockSpec((2, LANE), lambda g, i: (0, g))],
        out_specs=[blk] * 8,
        out_shape=[jax.ShapeDtypeStruct((tp, D_MODEL), F32)] * 8,
        compiler_params=_cparams("parallel", "parallel"),
    )(xc, wg, b4, lam)


def _lru_gates_bwd(l0, l1, da0, da1, r0, r1, i0, i1, a0, a1, xc, wg, lam, t_real, name):
    tp = xc.shape[0]
    tm = _tile(tp, 1408)

    def body(l0_ref, l1_ref, da0_ref, da1_ref, r0_ref, r1_ref, i0_ref, i1_ref, a0_ref, a1_ref, x_ref, w_ref, lam_ref,
             dx_ref, dw_ref, db_ref, dlam_ref):
        i = pl.program_id(1)
        x = x_ref[...]
        xb = x.astype(BF16)
        live = _row_ids(x.shape, i * tm) < t_real
        lam_ = lam_ref[...]
        sp = _softplus_neg(lam_)
        dsp_dlam = -_sigmoid(-lam_)
        dx = jnp.zeros_like(x)
        dpre = [None] * 4
        dlam_rows = []
        for d, (l_ref, da_ref, r_ref, i_ref, a_ref) in enumerate(((l0_ref, da0_ref, r0_ref, i0_ref, a0_ref),
                                                                  (l1_ref, da1_ref, r1_ref, i1_ref, a1_ref))):
            r, ig, a = r_ref[...], i_ref[...], a_ref[...]
            du = jnp.where(live, l_ref[...], 0.0)
            s = jnp.sqrt(1.0 - a * a)
            dv = du * s
            ds = du * (ig * x)
            dla = jnp.where(live, da_ref[...], 0.0) * a - ds * (a * a) / s
            dla = jnp.where(live, dla, 0.0)
            dr = dla * (-LRU_C) * sp[d:d + 1, :]
            dlam_rows.append(jnp.sum(dla * (-LRU_C) * r, axis=0, keepdims=True) * dsp_dlam[d:d + 1, :])
            dpre[d] = dr * r * (1.0 - r)
            dpre[2 + d] = dv * x * ig * (1.0 - ig)
            dx = dx + dv * ig

        @pl.when(i == 0)
        def _():
            dw_ref[...] = jnp.zeros_like(dw_ref)
            db_ref[...] = jnp.zeros_like(db_ref)
            dlam_ref[...] = jnp.zeros_like(dlam_ref)

        for k in range(4):
            pk = dpre[k].astype(BF16)
            dx = dx + lax.dot_general(pk, w_ref[k], (((1,), (1,)), ((), ())), preferred_element_type=F32)
            dw_ref[k] += lax.dot_general(xb, pk, (((0,), (0,)), ((), ())), preferred_element_type=F32)
        db_ref[...] += jnp.concatenate([jnp.sum(p, axis=0, keepdims=True) for p in dpre], axis=0)
        dlam_ref[...] += jnp.concatenate(dlam_rows, axis=0)
        dx_ref[...] = dx

    blk = pl.BlockSpec((tm, LANE), lambda g, i: (i, g))
    return pl.pallas_call(
        body, name=name, grid=(LRU_BLOCKS, tp // tm),
        in_specs=[blk] * 11 + [pl.BlockSpec((None, 4, LANE, LANE), lambda g, i: (g, 0, 0, 0)),
                               pl.BlockSpec((2, LANE), lambda g, i: (0, g))],
        out_specs=[blk, pl.BlockSpec((None, 4, LANE, LANE), lambda g, i: (g, 0, 0, 0)),
                   pl.BlockSpec((4, LANE), lambda g, i: (0, g)), pl.BlockSpec((2, LANE), lambda g, i: (0, g))],
        out_shape=[jax.ShapeDtypeStruct((tp, D_MODEL), F32), jax.ShapeDtypeStruct((LRU_BLOCKS, 4, LANE, LANE), F32),
                   jax.ShapeDtypeStruct((4, D_MODEL), F32), jax.ShapeDtypeStruct((2, D_MODEL), F32)],
        compiler_params=_cparams("parallel", "arbitrary"),
    )(l0, l1, da0, da1, r0, r1, i0, i1, a0, a1, xc, wg, lam)


def _tile_scan(a, u, reverse):
    rows = lax.broadcasted_iota(jnp.int32, a.shape, 0)
    for s in (1, 2, 4):
        if reverse:
            keep = rows < SUBLANE - s
            a_sh, u_sh = pltpu.roll(a, SUBLANE - s, 0), pltpu.roll(u, SUBLANE - s, 0)
        else:
            keep = rows >= s
            a_sh, u_sh = pltpu.roll(a, s, 0), pltpu.roll(u, s, 0)
        u = u + a * jnp.where(keep, u_sh, 0.0)
        a = a * jnp.where(keep, a_sh, 1.0)
    return a, u


def _scan_fwd(a0, u0, a1, u1, name):
    tp, d = a0.shape
    tc = 128
    nt = tp // SUBLANE

    def body(a0_ref, u0_ref, a1_ref, u1_ref, h0_ref, h1_ref):
        def step(t, carry):
            c0, c1 = carry
            f = pl.multiple_of(t * SUBLANE, SUBLANE)
            b = pl.multiple_of((nt - 1 - t) * SUBLANE, SUBLANE)
            pa, pu = _tile_scan(a0_ref[pl.ds(f, SUBLANE), :], u0_ref[pl.ds(f, SUBLANE), :], False)
            h = pu + pa * c0
            h0_ref[pl.ds(f, SUBLANE), :] = h
            c0 = h[SUBLANE - 1:SUBLANE, :]
            pa, pu = _tile_scan(a1_ref[pl.ds(b, SUBLANE), :], u1_ref[pl.ds(b, SUBLANE), :], True)
            h = pu + pa * c1
            h1_ref[pl.ds(b, SUBLANE), :] = h
            c1 = h[0:1, :]
            return c0, c1

        z = jnp.zeros((1, tc), F32)
        lax.fori_loop(0, nt, step, (z, z))

    blk = pl.BlockSpec((tp, tc), lambda j: (0, j))
    return pl.pallas_call(
        body, name=name, grid=(d // tc,), in_specs=[blk] * 4, out_specs=[blk] * 2,
        out_shape=[jax.ShapeDtypeStruct((tp, d), F32)] * 2,
        compiler_params=_cparams("parallel"),
    )(a0, u0, a1, u1)


def _scan_bwd(dh, a0, a1, h0, h1, name):
    tp, d = dh.shape
    tc = 128
    nt = tp // SUBLANE

    def body(dh_ref, a0_ref, a1_ref, h0_ref, h1_ref, l0_ref, l1_ref, da0_ref, da1_ref):
        rows8 = lax.broadcasted_iota(jnp.int32, (SUBLANE, tc), 0)

        def step(t, carry):
            c0, c1 = carry
            b = pl.multiple_of((nt - 1 - t) * SUBLANE, SUBLANE)
            f = pl.multiple_of(t * SUBLANE, SUBLANE)
            a = a0_ref[pl.ds(b, SUBLANE), :]
            a_next = jnp.where(rows8 < SUBLANE - 1, pltpu.roll(a, SUBLANE - 1, 0), 1.0)
            pa, pu = _tile_scan(a_next, dh_ref[pl.ds(b, SUBLANE), :], True)
            lam = pu + pa * c0
            l0_ref[pl.ds(b, SUBLANE), :] = lam
            c0 = a[0:1, :] * lam[0:1, :]
            a = a1_ref[pl.ds(f, SUBLANE), :]
            a_prev = jnp.where(rows8 >= 1, pltpu.roll(a, 1, 0), 1.0)
            pa, pu = _tile_scan(a_prev, dh_ref[pl.ds(f, SUBLANE), :], False)
            lam = pu + pa * c1
            l1_ref[pl.ds(f, SUBLANE), :] = lam
            c1 = a[SUBLANE - 1:SUBLANE, :] * lam[SUBLANE - 1:SUBLANE, :]
            return c0, c1

        z = jnp.zeros((1, tc), F32)
        lax.fori_loop(0, nt, step, (z, z))
        rows = lax.broadcasted_iota(jnp.int32, (tp, tc), 0)
        da0_ref[...] = l0_ref[...] * jnp.where(rows >= 1, pltpu.roll(h0_ref[...], 1, 0), 0.0)
        da1_ref[...] = l1_ref[...] * jnp.where(rows < tp - 1, pltpu.roll(h1_ref[...], tp - 1, 0), 0.0)

    blk = pl.BlockSpec((tp, tc), lambda j: (0, j))
    return pl.pallas_call(
        body, name=name, grid=(d // tc,), in_specs=[blk] * 5, out_specs=[blk] * 4,
        out_shape=[jax.ShapeDtypeStruct((tp, d), F32)] * 4,
        compiler_params=_cparams("parallel"),
    )(dh, a0, a1, h0, h1)


def _gated_h(proj, h0, h1, name):
    def fn(row0, lg, x0, x1):
        return _gelu(lg) * (x0 + x1)

    return _rows(fn, name, [Rw(proj, D_MODEL, C_LRU_G // D_MODEL), Rw(h0), Rw(h1)], [(D_MODEL, F32)])[0]


def _gated_h_bwd(dgh, proj, h0, h1, name):
    def fn(row0, dg, lg, x0, x1):
        act, dact = _gelu_and_grad(lg)
        return dg * (x0 + x1) * dact, dg * act

    return _rows(fn, name, [Rw(dgh), Rw(proj, D_MODEL, C_LRU_G // D_MODEL), Rw(h0), Rw(h1)], [(D_MODEL, F32)] * 2)


def _mix(proj, y_mla, y_lru, name):
    def fn(row0, gm, gl, ym, yl):
        return _sigmoid(gm) * ym + _sigmoid(gl) * yl

    return _rows(fn, name, [Rw(proj, D_MODEL, C_G_MLA // D_MODEL), Rw(proj, D_MODEL, C_G_LRU // D_MODEL), Rw(y_mla), Rw(y_lru)],
                 [(D_MODEL, F32)])[0]


def _mix_bwd(dz, proj, y_mla, y_lru, name):
    def fn(row0, dzb, gm, gl, ym, yl):
        sm, sl = _sigmoid(gm), _sigmoid(gl)
        return dzb * sm, dzb * sl, dzb * ym * sm * (1.0 - sm), dzb * yl * sl * (1.0 - sl)

    return _rows(fn, name, [Rw(dz), Rw(proj, D_MODEL, C_G_MLA // D_MODEL), Rw(proj, D_MODEL, C_G_LRU // D_MODEL),
                            Rw(y_mla), Rw(y_lru)], [(D_MODEL, F32)] * 4)


def _layer_fwd(h, w_in, rest_of_weights, cs, t_real, tag):
    proj = _matmul(h, w_in, tag + "proj")
    w = dict(rest_of_weights(proj), w_in=w_in)
    cqn, ckvn = _mla_norms(proj, w['q_norm'], w['kv_norm'], tag + "mla_norms")
    qext = _matmul(cqn, w['w_q'], tag + "q_up")
    kv = _matmul(ckvn, w['w_kv'], tag + "kv_up")
    qc, kc, vb = _mla_pack(qext, kv, proj, cs, tag + "mla_pack")
    o, lse = _attn_fwd(qc, kc, vb, t_real, tag + "attn_fwd")
    y_mla = _matmul(o, w['w_o_mla'], tag + "o_mla")
    xc = _lru_conv_fwd(proj, w['lru_conv_w'], w['lru_conv_b'], t_real, tag + "lru_conv")
    r0, r1, i0, i1, a0, a1, u0, u1 = _lru_gates_fwd(xc, w['w_g'], w['b4'], w['lru_lambda'], t_real, tag + "lru_gates")
    h0, h1 = _scan_fwd(a0, u0, a1, u1, tag + "lru_scan")
    gh = _gated_h(proj, h0, h1, tag + "lru_gate_out")
    y_lru = _matmul(gh, w['w_o_lru'], tag + "o_lru")
    z = _mix(proj, y_mla, y_lru, tag + "mix")
    zo = _matmul(z, w['w_out'], tag + "w_out")
    hm = _ln_fwd([(DN_ALPHA, h), (1.0, zo)], w['ln1_g'], w['ln1_b'], tag + "ln1")
    up = _matmul(hm, w['w_up'], tag + "w_up")
    m = _ffn_conv_act(up, w['ffn_conv_w'], w['ffn_conv_b'], t_real, tag + "ffn_conv")
    f = _matmul(m, w['w_down'], tag + "w_down", tk_cap=1408)
    out = _ln_fwd([(DN_ALPHA, hm), (1.0, f)], w['ln2_g'], w['ln2_b'], tag + "ln2")
    saved = dict(w=w, h=h, proj=proj, cqn=cqn, ckvn=ckvn, qc=qc, kc=kc, vb=vb, o=o, lse=lse, y_mla=y_mla, xc=xc,
                 r0=r0, r1=r1, i0=i0, i1=i1, a0=a0, a1=a1, h0=h0, h1=h1, gh=gh, y_lru=y_lru, z=z, zo=zo, hm=hm,
                 up=up, m=m, f=f)
    return out, saved


def _after(a, tok):
    return a if tok is None else a + tok.astype(a.dtype)


def _layer_bwd(dout_terms, s, cs, t_real, tag, emit, tok):
    w = s['w']
    g = {}
    du2, dg2, db2 = _ln_bwd(dout_terms, [(DN_ALPHA, s['hm']), (1.0, s['f'])], _after(w['ln2_g'], tok), tag + "ln2_bwd")
    g['ln2_g'], g['ln2_b'] = dg2, db2
    dm = _matmul(du2, w['w_down'], tag + "w_down_dx", tb=True)
    g['w_down'] = _matmul(s['m'], du2, tag + "w_down_dw", ta=True, tk_cap=1408)
    dgp, dvp, dwg_, dwv_, dbg_, dbv_ = _ffn_conv_act_bwd(dm, s['up'], w['ffn_conv_w'], w['ffn_conv_b'], t_real, tag + "ffn_conv_bwd")
    g['ffn_conv_w'] = jnp.concatenate([dwg_, dwv_], axis=1)
    g['ffn_conv_b'] = jnp.concatenate([dbg_, dbv_], axis=1)
    dup = jnp.concatenate([dgp, dvp], axis=1)
    dhm_mm = _matmul(dup, w['w_up'], tag + "w_up_dx", tb=True, tk_cap=1408)
    g['w_up'] = _matmul(s['hm'], dup, tag + "w_up_dw", ta=True, tk_cap=1408)
    tok = emit('ffn', g)
    g = {}
    du1, dg1, db1 = _ln_bwd([(DN_ALPHA, du2), (1.0, dhm_mm)], [(DN_ALPHA, s['h']), (1.0, s['zo'])],
                            _after(w['ln1_g'], tok), tag + "ln1_bwd")
    g['ln1_g'], g['ln1_b'] = dg1, db1
    dz = _matmul(du1, w['w_out'], tag + "w_out_dx", tb=True)
    g['w_out'] = _matmul(s['z'], du1, tag + "w_out_dw", ta=True, tk_cap=1408)
    dy_mla, dy_lru, dg_mla, dg_lru = _mix_bwd(dz, s['proj'], s['y_mla'], s['y_lru'], tag + "mix_bwd")
    do = _matmul(dy_mla, w['w_o_mla'], tag + "o_mla_dx", tb=True)
    g['w_o_mla'] = _matmul(s['o'], dy_mla, tag + "o_mla_dw", ta=True, tk_cap=1408)
    dqc, dkc, dv = _attn_bwd(s['qc'], s['kc'], s['vb'], do, s['o'], s['lse'], t_real, tag + "attn_bwd")
    dqext, dkn, dkrp = _mla_unpack(dqc, dkc, cs, tag + "mla_unpack")
    dkv = jnp.concatenate([dkn, dv], axis=1)
    dcqn = _matmul(dqext, w['w_q'], tag + "q_up_dx", tb=True)
    g['w_q'] = _matmul(s['cqn'], dqext, tag + "q_up_dw", ta=True, tk_cap=1408)
    dckvn = _matmul(dkv, w['w_kv'], tag + "kv_up_dx", tb=True)
    g['w_kv'] = _matmul(s['ckvn'], dkv, tag + "kv_up_dw", ta=True, tk_cap=1408)
    dcq, dckv, g['q_norm'], g['kv_norm'] = _mla_norms_bwd(dcqn, dckvn, s['proj'], w['q_norm'], w['kv_norm'], tag + "mla_norms_bwd")
    dgh = _matmul(dy_lru, w['w_o_lru'], tag + "o_lru_dx", tb=True)
    g['w_o_lru'] = _matmul(s['gh'], dy_lru, tag + "o_lru_dw", ta=True, tk_cap=1408)
    dlru_g, dhs = _gated_h_bwd(dgh, s['proj'], s['h0'], s['h1'], tag + "lru_gate_out_bwd")
    l0, l1, da0, da1 = _scan_bwd(dhs, s['a0'], s['a1'], s['h0'], s['h1'], tag + "lru_scan_bwd")
    dxc, g['w_g'], g['b4'], g['lru_lambda'] = _lru_gates_bwd(
        l0, l1, da0, da1, s['r0'], s['r1'], s['i0'], s['i1'], s['a0'], s['a1'], s['xc'], w['w_g'], w['lru_lambda'],
        t_real, tag + "lru_gates_bwd")
    dlru_x, g['lru_conv_w'], g['lru_conv_b'] = _lru_conv_bwd(dxc, s['proj'], w['lru_conv_w'], t_real, tag + "lru_conv_bwd")
    tok = emit('mid', g)
    dproj = jnp.concatenate([dlru_g, dlru_x, dg_mla, dg_lru, dcq, dckv, _after(dkrp, tok)], axis=1)
    tok = emit('in', {'w_in': _matmul(s['h'], dproj, tag + "proj_dw", ta=True, tk_cap=1408)})
    dh_mm = _matmul(dproj, w['w_in'], tag + "proj_dx", tb=True, tk_cap=1536)
    return [(DN_ALPHA, du1), (1.0, dh_mm)], tok


def _swap_halves(a, axis=-1):
    h1, h2 = jnp.split(a, 2, axis=axis)
    return jnp.concatenate([h2, h1], axis=axis)


def _w_in_kernel(w_in):
    cq, ckv, kr, lg, lx, gm, gl = jnp.split(w_in, [256, 384, 448, 1472, 2496, 3520], axis=1)
    return jnp.concatenate([lg, lx, gm, gl, cq, ckv, kr, _swap_halves(kr)], axis=1)


def _layer_weights(fl):
    w = {}
    uq = fl['w_uq']
    w['w_q'] = jnp.concatenate([uq, _swap_halves(uq[..., QK_NOPE:])], axis=-1).reshape(Q_RANK, HEADS * 2 * LANE)
    w['w_kv'] = jnp.concatenate([fl['w_uk'].reshape(KV_RANK, -1), fl['w_uv'].reshape(KV_RANK, -1)], axis=1).astype(BF16)
    w['w_g'] = jnp.moveaxis(jnp.concatenate([fl['w_rg'], fl['w_ig']], axis=0), 0, 1).astype(BF16)
    w['b4'] = jnp.concatenate([fl['b_rg'], fl['b_ig']], axis=0)
    for n in ('q_norm', 'kv_norm', 'w_o_mla', 'lru_conv_w', 'lru_conv_b', 'lru_lambda', 'w_o_lru', 'w_out', 'ln1_g',
              'ln1_b', 'w_up', 'ffn_conv_w', 'ffn_conv_b', 'w_down', 'ln2_g', 'ln2_b'):
        w[n] = fl[n]
    return w


def _layer_grads(g):
    out = {}
    if 'w_in' in g:
        lg, lx, gm, gl, cq, ckv, kr, krs = jnp.split(g['w_in'], [1024, 2048, 3072, 4096, 4352, 4480, 4544], axis=1)
        out['w_in'] = jnp.concatenate([cq, ckv, kr + _swap_halves(krs), lg, lx, gm, gl], axis=1)
    if 'w_q' in g:
        gq = g['w_q'].reshape(Q_RANK, HEADS, 2 * LANE)
        out['w_uq'] = jnp.concatenate([gq[..., :QK_NOPE], gq[..., QK_NOPE:QK_NOPE + QK_ROPE] + _swap_halves(gq[..., QK_NOPE + QK_ROPE:])], axis=-1)
    if 'w_kv' in g:
        out['w_uk'] = g['w_kv'][:, :HEADS * QK_NOPE].reshape(KV_RANK, HEADS, QK_NOPE)
        out['w_uv'] = g['w_kv'][:, HEADS * QK_NOPE:].reshape(KV_RANK, HEADS, V_HEAD)
    if 'w_g' in g:
        gg = jnp.moveaxis(g['w_g'], 1, 0)
        out['w_rg'], out['w_ig'] = gg[:2], gg[2:]
    if 'b4' in g:
        out['b_rg'], out['b_ig'] = g['b4'][:2], g['b4'][2:]
    for n in ('q_norm', 'kv_norm', 'lru_conv_b', 'ln1_g', 'ln1_b', 'ffn_conv_b', 'ln2_g', 'ln2_b'):
        if n in g:
            out[n] = g[n].reshape(-1)
    for n in ('w_o_mla', 'lru_conv_w', 'lru_lambda', 'w_o_lru', 'w_out', 'w_up', 'ffn_conv_w', 'w_down'):
        if n in g:
            out[n] = g[n]
    return out


def _rope_table(tp):
    half = QK_ROPE // 2
    inv_freq = jnp.exp(-math.log(ROPE_THETA) * jnp.arange(half, dtype=F32) / half)
    ang = jnp.arange(tp, dtype=F32)[:, None] * inv_freq[None, :]
    c, s = jnp.cos(ang), jnp.sin(ang)
    return jnp.concatenate([c, c, -s, s], axis=1)


def _local_step(x, target, meta, ln0_g, ln0_b, layer_w, t_pad, emit):
    seq = x.shape[0]
    t_real = N_META + seq
    zpad = jnp.zeros((t_pad - t_real, D_MODEL), F32)
    xin = jnp.concatenate([meta, x, zpad], axis=0)
    tgt = jnp.concatenate([jnp.zeros((N_META, D_MODEL), F32), target, zpad], axis=0)
    cs = _rope_table(t_pad)
    h = _ln_fwd([(1.0, xin)], ln0_g, ln0_b, "ln0")
    saved = []
    for l in range(DEPTH):
        w_in, rest_of_weights = layer_w[l](h)
        h, s = _layer_fwd(h, w_in, rest_of_weights, cs, t_real, "l%d_" % l)
        saved.append(s)
    dy, lossvec = _loss_head(h, tgt, t_real, "loss_head")
    loss = jnp.sum(lossvec)
    terms, tok = [(1.0, dy)], None
    for l in reversed(range(DEPTH)):
        terms, tok = _layer_bwd(terms, saved[l], cs, t_real, "l%d_" % l,
                                functools.partial(lambda stage, g, l: emit(l, stage, _layer_grads(g)), l=l), tok)
    dxin, dg0, db0 = _ln_bwd(terms, [(1.0, xin)], _after(ln0_g, tok), "ln0_bwd")
    emit(None, 'head', {'meta_tokens': dxin[:N_META], 'ln0_g': dg0.reshape(-1), 'ln0_b': db0.reshape(-1)})
    return loss, dxin[N_META:t_real]


_HBM = pl.BlockSpec(memory_space=pltpu.HBM)
_SEM = pl.BlockSpec(memory_space=pltpu.SEMAPHORE)
_SIDE_EFFECT = pltpu.SideEffectType.DATAFLOW_SIDE_EFFECTING


def _peer_copies(src_refs, land_refs, scatters, send_sems, recv_sems):
    x, y, c = lax.axis_index("x"), lax.axis_index("y"), lax.axis_index("c")
    me = 4 * x + 2 * y + c
    copies = []
    for k in range(1, N_DEV):
        px = 1 - x if k & 4 else x
        py = 1 - y if k & 2 else y
        pc = 1 - c if k & 1 else c
        for t, (src, land) in enumerate(zip(src_refs, land_refs)):
            copies.append(pltpu.make_async_remote_copy(
                src_ref=src.at[4 * px + 2 * py + pc] if scatters[t] else src, dst_ref=land.at[me],
                send_sem=send_sems.at[7 * t + k - 1], recv_sem=recv_sems.at[7 * t + k - 1],
                device_id=(px, py, pc), device_id_type=pl.DeviceIdType.MESH))
    return me, copies


def _exchange_start(groups, name):
    flat = [it for grp in groups for it in grp]
    nt, ng = len(flat), len(groups)
    scatters = [sc for _, sc in flat]
    srcs = [pltpu.with_memory_space_constraint(a, pltpu.HBM) for a, _ in flat]
    land_shapes = [a.shape if sc else (N_DEV,) + a.shape for a, sc in flat]
    lands = [pltpu.with_memory_space_constraint(lax.empty(s, a.dtype), pltpu.HBM) for s, (a, _) in zip(land_shapes, flat)]
    bounds = [0]
    for grp in groups:
        bounds.append(bounds[-1] + len(grp))

    def body(*refs):
        src_refs, land_refs = refs[:nt], refs[nt:2 * nt]
        sem_refs = refs[2 * nt:2 * nt + 2 * ng]
        token_ref, local_sems = refs[4 * nt + 2 * ng], refs[4 * nt + 2 * ng + 1]
        mine = None
        for gi in range(ng):
            lo, hi = bounds[gi], bounds[gi + 1]
            me, copies = _peer_copies(src_refs[lo:hi], land_refs[lo:hi], scatters[lo:hi], sem_refs[2 * gi], sem_refs[2 * gi + 1])
            if mine is None:
                mine = [pltpu.make_async_copy(src_refs[t].at[me] if scatters[t] else src_refs[t], land_refs[t].at[me],
                                              local_sems.at[t]) for t in range(nt)]
                for cp in mine:
                    cp.start()
            for cp in copies:
                cp.start()
        token_ref[...] = jnp.zeros_like(token_ref)
        for cp in mine:
            cp.wait()

    out_shape = []
    for grp in groups:
        out_shape += [pltpu.SemaphoreType.DMA((7 * len(grp),)), pltpu.SemaphoreType.DMA((7 * len(grp),))]
    out_shape += [pltpu.HBM(a.shape, a.dtype) for a in srcs] + [pltpu.HBM(s, a.dtype) for s, a in zip(land_shapes, srcs)]
    out_shape += [jax.ShapeDtypeStruct((SUBLANE, LANE), F32)]
    res = pl.pallas_call(
        body, name=name, out_shape=out_shape,
        in_specs=[_HBM] * (2 * nt),
        out_specs=[_SEM] * (2 * ng) + [_HBM] * (2 * nt) + [pl.BlockSpec(memory_space=pltpu.VMEM)],
        input_output_aliases={t: 2 * ng + t for t in range(2 * nt)},
        scratch_shapes=[pltpu.SemaphoreType.DMA((nt,))],
        compiler_params=pltpu.CompilerParams(has_side_effects=_SIDE_EFFECT),
    )(*srcs, *lands)
    sems, thru, token = res[:2 * ng], res[2 * ng:2 * ng + 2 * nt], res[-1]
    states = []
    for gi in range(ng):
        lo, hi = bounds[gi], bounds[gi + 1]
        states.append((sems[2 * gi], sems[2 * gi + 1], thru[lo:hi], thru[nt + lo:nt + hi], scatters[lo:hi]))
    return states, token[0, 0]


def _exchange_wait(state, after, name):
    send_sems, recv_sems, srcs, lands, scatters = state
    n = len(srcs)

    def body(*refs):
        _, copies = _peer_copies(refs[:n], refs[n:2 * n], scatters, refs[2 * n], refs[2 * n + 1])
        for cp in copies:
            cp.wait_send()
        for cp in copies:
            cp.wait_recv()

    res = pl.pallas_call(
        body, name=name,
        out_shape=[pltpu.HBM(a.shape, a.dtype) for a in srcs] + [pltpu.HBM(a.shape, a.dtype) for a in lands],
        in_specs=[_HBM] * (2 * n) + [_SEM, _SEM, _HBM],
        out_specs=[_HBM] * (2 * n),
        input_output_aliases={t: t for t in range(2 * n)},
        compiler_params=pltpu.CompilerParams(has_side_effects=_SIDE_EFFECT),
    )(*srcs, *lands, send_sems, recv_sems, pltpu.with_memory_space_constraint(after, pltpu.HBM))
    return res[n:]


def _as_rows(shape):
    return (1, shape[0]) if len(shape) == 1 else (math.prod(shape[:-1]), shape[-1])


def _sum_adamw(pieces, w, m, v, name):
    shape = w.shape
    nl = len(pieces)
    if nl > 1 and _as_rows(shape[1:])[0] % 16:
        pieces, nl = [jnp.stack(pieces, axis=1)], 1
    rows, cols = _as_rows(shape)
    rl = rows // nl
    cap = max(16, (1 << 18) // cols // 16 * 16)
    tr = _tile(rl, cap, 16)
    nb = rl // tr
    c1 = 1.0 / (1.0 - ADAM_B1 ** ADAM_STEP)
    c2 = 1.0 / (1.0 - ADAM_B2 ** ADAM_STEP)

    def body(*refs):
        p_refs = refs[:nl]
        w_ref, m_ref, v_ref, g_ref, d_ref, nm_ref, nv_ref = refs[nl:]
        li = pl.program_id(0)

        def total(p_ref):
            acc = p_ref[0].astype(F32)
            for k in range(1, N_DEV):
                acc = acc + p_ref[k].astype(F32)
            return acc

        gg = total(p_refs[0])
        for l in range(1, nl):
            gg = jnp.where(li == l, total(p_refs[l]), gg)
        nm = ADAM_B1 * m_ref[...] + (1.0 - ADAM_B1) * gg
        nv = ADAM_B2 * v_ref[...] + (1.0 - ADAM_B2) * (gg * gg)
        g_ref[...] = gg
        d_ref[...] = -ADAM_LR * ((nm * c1) / (jnp.sqrt(nv * c2) + ADAM_EPS) + ADAM_WD * w_ref[...])
        nm_ref[...] = nm
        nv_ref[...] = nv

    blk = pl.BlockSpec((tr, cols), lambda li, i: (li * nb + i, 0))
    p_specs = [pl.BlockSpec((N_DEV, tr, cols), functools.partial(lambda li, i, l: (0, jnp.where(li == l, i, 0), 0), l=l))
               for l in range(nl)]
    res = pl.pallas_call(
        body, name=name, grid=(nl, nb),
        in_specs=p_specs + [blk] * 3, out_specs=[blk] * 4,
        out_shape=[jax.ShapeDtypeStruct((rows, cols), F32)] * 4,
        compiler_params=_cparams("parallel", "parallel"),
    )(*[p.reshape(N_DEV, rl, cols) for p in pieces], *[a.reshape(rows, cols) for a in (w, m, v)])
    return [r.reshape(shape) for r in res]


def _to_shards(full, axis):
    shp = full.shape
    a = full.reshape(shp[:axis] + (N_DEV, shp[axis] // N_DEV) + shp[axis + 1:])
    return jnp.moveaxis(a, axis, 0)


def _from_shards(blocks, axis):
    a = jnp.moveaxis(blocks, 0, axis)
    shp = a.shape
    return a.reshape(shp[:axis] + (shp[axis] * shp[axis + 1],) + shp[axis + 2:])


def kernel(x, meta_tokens, ln0_g, ln0_b, w_in, q_norm, kv_norm, w_uq, w_uk, w_uv, w_o_mla, lru_conv_w, lru_conv_b, w_rg, b_rg, w_ig, b_ig, lru_lambda, w_o_lru, w_out, ln1_g, ln1_b, w_up, ffn_conv_w, ffn_conv_b, w_down, ln2_g, ln2_b, loss_target, m_meta_tokens, m_ln0_g, m_ln0_b, m_w_in, m_q_norm, m_kv_norm, m_w_uq, m_w_uk, m_w_uv, m_w_o_mla, m_lru_conv_w, m_lru_conv_b, m_w_rg, m_b_rg, m_w_ig, m_b_ig, m_lru_lambda, m_w_o_lru, m_w_out, m_ln1_g, m_ln1_b, m_w_up, m_ffn_conv_w, m_ffn_conv_b, m_w_down, m_ln2_g, m_ln2_b, v_meta_tokens, v_ln0_g, v_ln0_b, v_w_in, v_q_norm, v_kv_norm, v_w_uq, v_w_uk, v_w_uv, v_w_o_mla, v_lru_conv_w, v_lru_conv_b, v_w_rg, v_b_rg, v_w_ig, v_b_ig, v_lru_lambda, v_w_o_lru, v_w_out, v_ln1_g, v_ln1_b, v_w_up, v_ffn_conv_w, v_ffn_conv_b, v_w_down, v_ln2_g, v_ln2_b):
    args = (meta_tokens, ln0_g, ln0_b, w_in, q_norm, kv_norm, w_uq, w_uk, w_uv, w_o_mla, lru_conv_w, lru_conv_b, w_rg, b_rg, w_ig, b_ig, lru_lambda, w_o_lru, w_out, ln1_g, ln1_b, w_up, ffn_conv_w, ffn_conv_b, w_down, ln2_g, ln2_b)
    ms = (m_meta_tokens, m_ln0_g, m_ln0_b, m_w_in, m_q_norm, m_kv_norm, m_w_uq, m_w_uk, m_w_uv, m_w_o_mla, m_lru_conv_w, m_lru_conv_b, m_w_rg, m_b_rg, m_w_ig, m_b_ig, m_lru_lambda, m_w_o_lru, m_w_out, m_ln1_g, m_ln1_b, m_w_up, m_ffn_conv_w, m_ffn_conv_b, m_w_down, m_ln2_g, m_ln2_b)
    vs = (v_meta_tokens, v_ln0_g, v_ln0_b, v_w_in, v_q_norm, v_kv_norm, v_w_uq, v_w_uk, v_w_uv, v_w_o_mla, v_lru_conv_w, v_lru_conv_b, v_w_rg, v_b_rg, v_w_ig, v_b_ig, v_lru_lambda, v_w_o_lru, v_w_out, v_ln1_g, v_ln1_b, v_w_up, v_ffn_conv_w, v_ffn_conv_b, v_w_down, v_ln2_g, v_ln2_b)
    wd, md, vd = dict(zip(WEIGHTS, args)), dict(zip(WEIGHTS, ms)), dict(zip(WEIGHTS, vs))

    def shard_axis(n, l):
        return SHARD_AXIS[n] - (0 if l is None else 1)

    def shard(n, l):
        a = wd[n] if l is None else wd[n][l]
        return a.astype(BF16) if n in BIG else a

    layered = [n for n in SHARDED if n != 'meta_tokens']
    first = [('meta_tokens', None), ('w_in', 0)]
    rest0 = [(n, 0) for n in layered if n != 'w_in']
    later = [(n, 1) for n in layered]
    gather, token = _exchange_start([[(shard(*k), False) for k in keys] for keys in (first, rest0, later)], "gather_start")

    def arrive(gi, keys, after, name):
        return {k: _from_shards(b, shard_axis(*k)) for k, b in zip(keys, _exchange_wait(gather[gi], after, name))}

    def layer_weights(got, l):
        fl = {n: wd[n][l] for n in REPLICATED if n not in ('ln0_g', 'ln0_b')}
        fl.update({n: a for (n, _), a in got.items() if n not in ('meta_tokens', 'w_in')})
        return _layer_weights(fl)

    ln0_g = _after(wd['ln0_g'], token)
    got_first = arrive(0, first, ln0_g, "gather_wait_first")

    def first_layer(h):
        return _w_in_kernel(got_first['w_in', 0]), lambda proj: layer_weights(arrive(1, rest0, proj, "gather_wait_l0"), 0)

    def second_layer(h):
        got = arrive(2, later, h, "gather_wait_l1")
        return _w_in_kernel(got['w_in', 1]), lambda proj: layer_weights(got, 1)

    sent = []
    pending = []

    def send(l, stage, grads):
        for n, g in grads.items():
            if n in SHARD_AXIS:
                g = _to_shards(g, shard_axis(n, l))
                pending.append(((n, l), (g.astype(BF16) if n in BIG else g, True)))
            else:
                pending.append(((n, l), (g.astype(BF16) if n in LARGE_REPLICATED else g, False)))
        if l == DEPTH - 1 and stage != 'in':
            return None
        (state,), tok = _exchange_start([[it for _, it in pending]], "grads_start_%s_%s" % (l, stage))
        sent.append(([k for k, _ in pending], state))
        pending.clear()
        return tok

    seq = x.shape[1]
    t_pad = -(-(N_META + seq + MIN_PAD_ROWS) // LANE) * LANE
    loss, grad_x = _local_step(x[0], loss_target[0], got_first['meta_tokens', None], ln0_g, wd['ln0_b'],
                               [first_layer, second_layer], t_pad, send)
    loss = lax.psum(loss, ("x", "y", "c"))

    pieces = {}
    for gi, (keys, state) in enumerate(sent):
        pieces.update(zip(keys, _exchange_wait(state, grad_x, "grads_wait_%d" % gi)))
    outs = {}
    for n in WEIGHTS:
        ps = [pieces[n, None]] if (n, None) in pieces else [pieces[n, l] for l in range(DEPTH)]
        outs[n] = _sum_adamw(ps, wd[n], md[n], vd[n], "adamw_" + n)
    res = [loss, grad_x[None]]
    for k in range(4):
        res += [outs[n][k] for n in WEIGHTS]
    return tuple(res)
```

```python
import functools
import math

import jax
import jax.numpy as jnp
from jax import lax
from jax.experimental import pallas as pl
from jax.experimental.pallas import tpu as pltpu

F32 = jnp.float32
BF16 = jnp.bfloat16

N_DEV = 8
D_MODEL = 1024
N_META = 16
HEADS = 8
QK_NOPE = 128
QK_ROPE = 64
V_HEAD = 128
Q_RANK = 256
KV_RANK = 128
ROPE_THETA = 10000.0
LRU_BLOCKS = 8
LRU_C = 8.0
D_FF = 2816
DEPTH = 2
DN_ALPHA = (2.0 * DEPTH) ** 0.25
LN_EPS = 1e-5
RMS_EPS = 1e-6
LN2 = math.log(2.0)
ATT_SCALE = 1.0 / math.sqrt(QK_NOPE + QK_ROPE) / LN2
NEG_BIG = -1e30

ADAM_LR = 0.001
ADAM_B1 = 0.9
ADAM_B2 = 0.999
ADAM_EPS = 1e-08
ADAM_WD = 0.01
ADAM_STEP = 10

MIN_PAD_ROWS = 2
LANE = 128
SUBLANE = 8
VMEM_LIMIT = 56 * 1024 * 1024

PROJ_COLS = 4 * D_MODEL + Q_RANK + KV_RANK + 2 * QK_ROPE
C_LRU_G, C_LRU_X, C_G_MLA, C_G_LRU = 0, D_MODEL, 2 * D_MODEL, 3 * D_MODEL
C_CQ = 4 * D_MODEL
C_CKV = C_CQ + Q_RANK
C_KRP = C_CKV + KV_RANK

WEIGHTS = ['meta_tokens', 'ln0_g', 'ln0_b', 'w_in', 'q_norm', 'kv_norm', 'w_uq', 'w_uk', 'w_uv', 'w_o_mla',
           'lru_conv_w', 'lru_conv_b', 'w_rg', 'b_rg', 'w_ig', 'b_ig', 'lru_lambda', 'w_o_lru', 'w_out',
           'ln1_g', 'ln1_b', 'w_up', 'ffn_conv_w', 'ffn_conv_b', 'w_down', 'ln2_g', 'ln2_b']
SHARD_AXIS = {'meta_tokens': 1, 'w_in': 2, 'w_uq': 1, 'w_o_mla': 1, 'lru_conv_w': 2, 'b_rg': 2, 'b_ig': 2,
              'lru_lambda': 2, 'w_o_lru': 1, 'w_out': 1, 'w_up': 2, 'ffn_conv_w': 2, 'w_down': 1}
BIG = ['w_in', 'w_uq', 'w_o_mla', 'w_o_lru', 'w_out', 'w_up', 'w_down']
SHARDED = [n for n in WEIGHTS if n in SHARD_AXIS]
REPLICATED = [n for n in WEIGHTS if n not in SHARD_AXIS]
LARGE_REPLICATED = ['w_uk', 'w_uv', 'w_rg', 'w_ig']


def _cparams(*sem):
    return pltpu.CompilerParams(dimension_semantics=sem, vmem_limit_bytes=VMEM_LIMIT)


def _tile(n, cap, unit=LANE):
    best = None
    t = unit
    while t <= min(n, cap):
        if n % t == 0:
            best = t
        t += unit
    return n if best is None else best


def _sigmoid(x):
    return 1.0 / (1.0 + jnp.exp(-x))


_GELU_C = math.sqrt(2.0 / math.pi)


def _gelu(x):
    t = jnp.tanh(_GELU_C * (x + 0.044715 * x * x * x))
    return 0.5 * x * (1.0 + t)


def _gelu_and_grad(x):
    t = jnp.tanh(_GELU_C * (x + 0.044715 * x * x * x))
    g = 0.5 * x * (1.0 + t)
    dg = 0.5 * (1.0 + t) + 0.5 * x * (1.0 - t * t) * _GELU_C * (1.0 + 3.0 * 0.044715 * x * x)
    return g, dg


def _softplus_neg(lam):
    z = jnp.exp(-jnp.abs(lam))
    w = 1.0 + z
    log1p = jnp.where(w == 1.0, z, jnp.log(w) * z / (w - 1.0))
    return jnp.maximum(-lam, 0.0) + log1p


def _row_ids(shape, row0=0):
    return lax.broadcasted_iota(jnp.int32, shape, 0) + row0


def _matmul(a, b, name, ta=False, tb=False, out_dtype=F32, tm_cap=1408, tn_cap=1024, tk_cap=2048):
    if ta:
        kdim, m = a.shape
    else:
        m, kdim = a.shape
    if tb:
        n, k2 = b.shape
    else:
        k2, n = b.shape
    assert kdim == k2, (a.shape, b.shape, ta, tb)
    tm, tn, tk = _tile(m, tm_cap), _tile(n, tn_cap), _tile(kdim, tk_cap)
    nk = kdim // tk

    def body(a_ref, b_ref, o_ref, acc_ref):
        k = pl.program_id(2)

        @pl.when(k == 0)
        def _():
            acc_ref[...] = jnp.zeros_like(acc_ref)

        dn = (((0 if ta else 1,), (1 if tb else 0,)), ((), ()))
        acc_ref[...] += lax.dot_general(a_ref[...].astype(BF16), b_ref[...].astype(BF16), dn,
                                        preferred_element_type=F32)

        @pl.when(k == nk - 1)
        def _():
            o_ref[...] = acc_ref[...].astype(o_ref.dtype)

    a_spec = pl.BlockSpec((tk, tm), lambda i, j, k: (k, i)) if ta else pl.BlockSpec((tm, tk), lambda i, j, k: (i, k))
    b_spec = pl.BlockSpec((tn, tk), lambda i, j, k: (j, k)) if tb else pl.BlockSpec((tk, tn), lambda i, j, k: (k, j))
    return pl.pallas_call(
        body, name=name,
        grid=(m // tm, n // tn, nk),
        in_specs=[a_spec, b_spec],
        out_specs=pl.BlockSpec((tm, tn), lambda i, j, k: (i, j)),
        out_shape=jax.ShapeDtypeStruct((m, n), out_dtype),
        scratch_shapes=[pltpu.VMEM((tm, tn), F32)],
        compiler_params=_cparams("parallel", "parallel", "arbitrary"),
    )(a, b)


class Rw:
    def __init__(self, arr, width=None, cb=0):
        self.arr, self.width, self.cb = arr, (arr.shape[1] if width is None else width), cb


class Pm:
    def __init__(self, arr):
        self.arr = arr


def _rows(fn, name, ins, outs, accs=(), tm_cap=384):
    tp = next(o.arr.shape[0] for o in ins if isinstance(o, Rw))
    tm = _tile(tp, tm_cap)
    n_in, n_out, n_acc = len(ins), len(outs), len(accs)

    def body(*refs):
        i = pl.program_id(0)
        res = fn(i * tm, *[r[...] for r in refs[:n_in]])
        if not isinstance(res, (tuple, list)):
            res = (res,)
        assert len(res) == n_out + n_acc, (name, len(res))
        for k in range(n_out):
            refs[n_in + k][...] = res[k].astype(refs[n_in + k].dtype)
        for k in range(n_acc):
            ref = refs[n_in + n_out + k]

            @pl.when(i == 0)
            def _():
                ref[...] = jnp.zeros_like(ref)

            ref[...] += res[n_out + k]

    in_specs = []
    for o in ins:
        if isinstance(o, Rw):
            in_specs.append(pl.BlockSpec((tm, o.width), functools.partial(lambda i, cb: (i, cb), cb=o.cb)))
        else:
            in_specs.append(pl.BlockSpec(o.arr.shape, functools.partial(lambda i, nd: (0,) * nd, nd=o.arr.ndim)))
    out_specs = [pl.BlockSpec((tm, w), lambda i: (i, 0)) for (w, _) in outs]
    out_specs += [pl.BlockSpec(s, functools.partial(lambda i, nd: (0,) * nd, nd=len(s))) for s in accs]
    out_shape = [jax.ShapeDtypeStruct((tp, w), dt) for (w, dt) in outs]
    out_shape += [jax.ShapeDtypeStruct(s, F32) for s in accs]
    res = pl.pallas_call(
        body, name=name, grid=(tp // tm,), in_specs=in_specs, out_specs=out_specs, out_shape=out_shape,
        compiler_params=_cparams("arbitrary"),
    )(*[o.arr for o in ins])
    return res


class Cl:
    def __init__(self, arr, col0=0):
        self.arr, self.col0 = arr, col0


def _cols(fn, name, ins, outs, ncols, tc):
    assert ncols % tc == 0
    n_in, n_out = len(ins), len(outs)

    def body(*refs):
        res = fn(*[r[...] for r in refs[:n_in]])
        if not isinstance(res, (tuple, list)):
            res = (res,)
        assert len(res) == n_out, (name, len(res))
        for k in range(n_out):
            refs[n_in + k][...] = res[k].astype(refs[n_in + k].dtype)

    in_specs = []
    for o in ins:
        assert o.col0 % tc == 0, (name, o.col0, tc)
        in_specs.append(pl.BlockSpec((o.arr.shape[0], tc), functools.partial(lambda j, off: (0, j + off), off=o.col0 // tc)))
    out_specs = [pl.BlockSpec((r, tc), lambda j: (0, j)) for (r, _) in outs]
    out_shape = [jax.ShapeDtypeStruct((r, ncols), dt) for (r, dt) in outs]
    return pl.pallas_call(
        body, name=name, grid=(ncols // tc,), in_specs=in_specs, out_specs=out_specs, out_shape=out_shape,
        compiler_params=_cparams("parallel"),
    )(*[o.arr for o in ins])


def _ln_stats(u):
    mu = jnp.mean(u, axis=-1, keepdims=True)
    xc = u - mu
    var = jnp.mean(xc * xc, axis=-1, keepdims=True)
    rstd = lax.rsqrt(var + LN_EPS)
    return xc * rstd, rstd


def _ln_fwd(terms, g, b, name):
    coefs = [c for c, _ in terms]

    def fn(row0, *blk):
        xs, (gg, bb) = blk[:len(coefs)], blk[len(coefs):]
        u = sum(c * x for c, x in zip(coefs, xs))
        xhat, _ = _ln_stats(u)
        return xhat * gg + bb

    d = terms[0][1].shape[1]
    return _rows(fn, name, [Rw(x) for _, x in terms] + [Pm(g.reshape(1, d)), Pm(b.reshape(1, d))], [(d, F32)])[0]


def _ln_bwd(dy_terms, u_terms, g, name):
    dc = [c for c, _ in dy_terms]
    uc = [c for c, _ in u_terms]
    d = u_terms[0][1].shape[1]

    def fn(row0, *blk):
        dys = blk[:len(dc)]
        xs = blk[len(dc):len(dc) + len(uc)]
        gg = blk[-1]
        dy = sum(c * x for c, x in zip(dc, dys))
        u = sum(c * x for c, x in zip(uc, xs))
        xhat, rstd = _ln_stats(u)
        gdy = dy * gg
        m1 = jnp.mean(gdy, axis=-1, keepdims=True)
        m2 = jnp.mean(gdy * xhat, axis=-1, keepdims=True)
        du = rstd * (gdy - m1 - xhat * m2)
        return du, jnp.sum(dy * xhat, axis=0, keepdims=True), jnp.sum(dy, axis=0, keepdims=True)

    ins = [Rw(x) for _, x in dy_terms] + [Rw(x) for _, x in u_terms] + [Pm(g.reshape(1, d))]
    return _rows(fn, name, ins, [(d, F32)], accs=[(1, d), (1, d)])


def _loss_head(y, tgt, t_real, name):
    d = y.shape[1]

    def fn(row0, yb, tb):
        rows = _row_ids(yb.shape, row0)
        live = (rows >= N_META) & (rows < t_real)
        diff = jnp.where(live, yb - tb, 0.0)
        return diff * (1.0 / d), jnp.sum(diff * diff, axis=0, keepdims=True) * (0.5 / d)

    return _rows(fn, name, [Rw(y), Rw(tgt)], [(d, F32)], accs=[(1, d)])


def _rms(x, g):
    r = lax.rsqrt(jnp.mean(x * x, axis=-1, keepdims=True) + RMS_EPS)
    return x * r * g


def _rms_bwd(dy, x, g):
    r = lax.rsqrt(jnp.mean(x * x, axis=-1, keepdims=True) + RMS_EPS)
    gdy = dy * g
    dx = r * gdy - x * (r * r * r) * jnp.mean(gdy * x, axis=-1, keepdims=True)
    return dx, jnp.sum(dy * x * r, axis=0, keepdims=True)


def _mla_norms(proj, qn, kvn, name):
    def fn(row0, cq, ckv, g1, g2):
        return _rms(cq, g1), _rms(ckv, g2)

    return _rows(fn, name, [Rw(proj, Q_RANK, C_CQ // Q_RANK), Rw(proj, KV_RANK, C_CKV // KV_RANK),
                            Pm(qn.reshape(1, Q_RANK)), Pm(kvn.reshape(1, KV_RANK))],
                 [(Q_RANK, F32), (KV_RANK, F32)])


def _mla_norms_bwd(dcqn, dckvn, proj, qn, kvn, name):
    def fn(row0, d1, d2, cq, ckv, g1, g2):
        dx1, dg1 = _rms_bwd(d1, cq, g1)
        dx2, dg2 = _rms_bwd(d2, ckv, g2)
        return dx1, dx2, dg1, dg2

    return _rows(fn, name, [Rw(dcqn), Rw(dckvn), Rw(proj, Q_RANK, C_CQ // Q_RANK), Rw(proj, KV_RANK, C_CKV // KV_RANK),
                            Pm(qn.reshape(1, Q_RANK)), Pm(kvn.reshape(1, KV_RANK))],
                 [(Q_RANK, F32), (KV_RANK, F32)], accs=[(1, Q_RANK), (1, KV_RANK)])


def _fold_rope(z):
    return z + pltpu.roll(z, QK_ROPE, 1)


def _mla_pack(qext, kv, proj, cs, name):
    tp = qext.shape[0]
    tm = _tile(tp, 384)

    def body(q_ref, kn_ref, v_ref, kr_ref, cs_ref, qo_ref, ko_ref, vo_ref):
        cs_ = cs_ref[...]
        low = lax.broadcasted_iota(jnp.int32, cs_.shape, 1) < QK_ROPE
        q = q_ref[...]
        qr = jnp.where(low, _fold_rope(q[:, QK_NOPE:] * cs_), 0.0)
        qo_ref[...] = (jnp.concatenate([q[:, :QK_NOPE], qr], axis=1) * ATT_SCALE).astype(BF16)
        kr = _fold_rope(kr_ref[...] * cs_)
        ko_ref[...] = jnp.concatenate([kn_ref[...], kr], axis=1).astype(BF16)
        vo_ref[...] = v_ref[...].astype(BF16)

    return pl.pallas_call(
        body, name=name, grid=(tp // tm, HEADS),
        in_specs=[pl.BlockSpec((tm, 2 * LANE), lambda i, h: (i, h)),
                  pl.BlockSpec((tm, LANE), lambda i, h: (i, h)),
                  pl.BlockSpec((tm, LANE), lambda i, h: (i, HEADS + h)),
                  pl.BlockSpec((tm, LANE), lambda i, h: (i, C_KRP // LANE)),
                  pl.BlockSpec((tm, LANE), lambda i, h: (i, 0))],
        out_specs=[pl.BlockSpec((tm, 2 * LANE), lambda i, h: (i, h)),
                   pl.BlockSpec((tm, 2 * LANE), lambda i, h: (i, h)),
                   pl.BlockSpec((tm, LANE), lambda i, h: (i, h))],
        out_shape=[jax.ShapeDtypeStruct((tp, HEADS * 2 * LANE), BF16),
                   jax.ShapeDtypeStruct((tp, HEADS * 2 * LANE), BF16),
                   jax.ShapeDtypeStruct((tp, HEADS * LANE), BF16)],
        compiler_params=_cparams("parallel", "arbitrary"),
    )(qext, kv, kv, proj, cs)


def _mla_unpack(dq, dk, cs, name):
    tp = dq.shape[0]
    tm = _tile(tp, 384)

    def body(dq_ref, dk_ref, cs_ref, dqe_ref, dkn_ref, dkr_ref):
        h = pl.program_id(1)
        cs_ = cs_ref[...]
        low = lax.broadcasted_iota(jnp.int32, cs_.shape, 1) < QK_ROPE
        dq_ = dq_ref[...] * ATT_SCALE
        dqr = _fold_rope(jnp.where(low, dq_[:, QK_NOPE:], 0.0)) * cs_
        dqe_ref[...] = jnp.concatenate([dq_[:, :QK_NOPE], dqr], axis=1)
        dk_ = dk_ref[...]
        dkn_ref[...] = dk_[:, :QK_NOPE]

        @pl.when(h == 0)
        def _():
            dkr_ref[...] = jnp.zeros_like(dkr_ref)

        dkr_ref[...] += _fold_rope(jnp.where(low, dk_[:, QK_NOPE:], 0.0)) * cs_

    return pl.pallas_call(
        body, name=name, grid=(tp // tm, HEADS),
        in_specs=[pl.BlockSpec((tm, 2 * LANE), lambda i, h: (i, h)),
                  pl.BlockSpec((tm, 2 * LANE), lambda i, h: (i, h)),
                  pl.BlockSpec((tm, LANE), lambda i, h: (i, 0))],
        out_specs=[pl.BlockSpec((tm, 2 * LANE), lambda i, h: (i, h)),
                   pl.BlockSpec((tm, LANE), lambda i, h: (i, h)),
                   pl.BlockSpec((tm, LANE), lambda i, h: (i, 0))],
        out_shape=[jax.ShapeDtypeStruct((tp, HEADS * 2 * LANE), F32),
                   jax.ShapeDtypeStruct((tp, HEADS * LANE), F32),
                   jax.ShapeDtypeStruct((tp, LANE), F32)],
        compiler_params=_cparams("parallel", "arbitrary"),
    )(dq, dk, cs)


def _attn_fwd(q, k, v, t_real, name):
    tp = q.shape[0]
    tq = _tile(tp, 384)
    tkc = _tile(tp, 1408)
    nkc = -(-t_real // tkc)

    def body(q_ref, k_ref, v_ref, o_ref, lse_ref):
        qb = q_ref[...]
        m = l = acc = None
        for c in range(nkc):
            s = lax.dot_general(qb, k_ref[c * tkc:(c + 1) * tkc, :], (((1,), (1,)), ((), ())), preferred_element_type=F32)
            if (c + 1) * tkc > t_real:
                cols = lax.broadcasted_iota(jnp.int32, s.shape, 1) + c * tkc
                s = jnp.where(cols < t_real, s, NEG_BIG)
            mc = jnp.max(s, axis=-1, keepdims=True)
            m_new = mc if c == 0 else jnp.maximum(m, mc)
            p = jnp.exp2(s - m_new)
            lc = jnp.sum(p, axis=-1, keepdims=True)
            pv = jnp.dot(p.astype(BF16), v_ref[c * tkc:(c + 1) * tkc, :], preferred_element_type=F32)
            if c == 0:
                l, acc = lc, pv
            else:
                alpha = jnp.exp2(m - m_new)
                l, acc = alpha * l + lc, alpha * acc + pv
            m = m_new
        o_ref[...] = acc / l
        lse_ref[...] = m + jnp.log2(l)

    return pl.pallas_call(
        body, name=name, grid=(HEADS, tp // tq),
        in_specs=[pl.BlockSpec((tq, 2 * LANE), lambda h, i: (i, h)),
                  pl.BlockSpec((tp, 2 * LANE), lambda h, i: (0, h)),
                  pl.BlockSpec((tp, LANE), lambda h, i: (0, h))],
        out_specs=[pl.BlockSpec((tq, LANE), lambda h, i: (i, h)),
                   pl.BlockSpec((None, tq, 1), lambda h, i: (h, i, 0))],
        out_shape=[jax.ShapeDtypeStruct((tp, HEADS * LANE), F32),
                   jax.ShapeDtypeStruct((HEADS, tp, 1), F32)],
        compiler_params=_cparams("parallel", "parallel"),
    )(q, k, v)


def _attn_bwd(q, k, v, do, o, lse, t_real, name):
    tp = q.shape[0]
    tq = _tile(tp, 384)
    tkc = _tile(tp, 1408)
    nkc = -(-t_real // tkc)

    def body(q_ref, k_ref, v_ref, do_ref, o_ref, lse_ref, dq_ref, dk_ref, dv_ref):
        i = pl.program_id(1)

        @pl.when(i == 0)
        def _():
            dk_ref[...] = jnp.zeros_like(dk_ref)
            dv_ref[...] = jnp.zeros_like(dv_ref)

        qb = q_ref[...]
        dob = do_ref[...]
        dob16 = dob.astype(BF16)
        dol2 = (dob * LN2).astype(BF16)
        delta = jnp.sum(dob * o_ref[...], axis=-1, keepdims=True) * LN2
        lse = lse_ref[...]
        dq = None
        for c in range(nkc):
            ks = slice(c * tkc, (c + 1) * tkc)
            kb = k_ref[ks, :]
            s = lax.dot_general(qb, kb, (((1,), (1,)), ((), ())), preferred_element_type=F32)
            p = jnp.exp2(s - lse)
            if (c + 1) * tkc > t_real:
                cols = lax.broadcasted_iota(jnp.int32, s.shape, 1) + c * tkc
                p = jnp.where(cols < t_real, p, 0.0)
            dp = lax.dot_general(dol2, v_ref[ks, :], (((1,), (1,)), ((), ())), preferred_element_type=F32)
            ds = (p * (dp - delta)).astype(BF16)
            dqc = jnp.dot(ds, kb, preferred_element_type=F32)
            dq = dqc if c == 0 else dq + dqc
            dk_ref[ks, :] += lax.dot_general(ds, qb, (((0,), (0,)), ((), ())), preferred_element_type=F32)
            dv_ref[ks, :] += lax.dot_general(p.astype(BF16), dob16, (((0,), (0,)), ((), ())), preferred_element_type=F32)
        dq_ref[...] = dq

    return pl.pallas_call(
        body, name=name, grid=(HEADS, tp // tq),
        in_specs=[pl.BlockSpec((tq, 2 * LANE), lambda h, i: (i, h)),
                  pl.BlockSpec((tp, 2 * LANE), lambda h, i: (0, h)),
                  pl.BlockSpec((tp, LANE), lambda h, i: (0, h)),
                  pl.BlockSpec((tq, LANE), lambda h, i: (i, h)),
                  pl.BlockSpec((tq, LANE), lambda h, i: (i, h)),
                  pl.BlockSpec((None, tq, 1), lambda h, i: (h, i, 0))],
        out_specs=[pl.BlockSpec((tq, 2 * LANE), lambda h, i: (i, h)),
                   pl.BlockSpec((tp, 2 * LANE), lambda h, i: (0, h)),
                   pl.BlockSpec((tp, LANE), lambda h, i: (0, h))],
        out_shape=[jax.ShapeDtypeStruct((tp, HEADS * 2 * LANE), F32),
                   jax.ShapeDtypeStruct((tp, HEADS * 2 * LANE), F32),
                   jax.ShapeDtypeStruct((tp, HEADS * LANE), F32)],
        compiler_params=_cparams("parallel", "arbitrary"),
    )(q, k, v, do, o, lse)


def _shift_rows(x, s):
    tp = x.shape[0]
    return x if s % tp == 0 else pltpu.roll(x, s % tp, 0)


def _conv_fwd_val(xm, w, b, pad_left):
    acc = b + w[0:1, :] * _shift_rows(xm, pad_left)
    for k in range(1, w.shape[0]):
        acc = acc + w[k:k + 1, :] * _shift_rows(xm, pad_left - k)
    return acc


def _conv_bwd_val(dy, xm, w, pad_left, live):
    kk = w.shape[0]
    dx = w[0:1, :] * _shift_rows(dy, -pad_left)
    dws = [jnp.sum(dy * _shift_rows(xm, pad_left), axis=0, keepdims=True)]
    for k in range(1, kk):
        dx = dx + w[k:k + 1, :] * _shift_rows(dy, k - pad_left)
        dws.append(jnp.sum(dy * _shift_rows(xm, pad_left - k), axis=0, keepdims=True))
    return jnp.where(live, dx, 0.0), jnp.concatenate(dws, axis=0), jnp.sum(dy, axis=0, keepdims=True)


def _lru_conv_fwd(proj, w, b, t_real, name):
    def fn(x, ww, bb):
        xm = jnp.where(_row_ids(x.shape) < t_real, x, 0.0)
        return _conv_fwd_val(xm, ww, bb, 2)

    return _cols(fn, name, [Cl(proj, C_LRU_X), Cl(w), Cl(b.reshape(1, -1))], [(proj.shape[0], F32)], D_MODEL, 128)[0]


def _lru_conv_bwd(dxc, proj, w, t_real, name):
    def fn(dy, x, ww):
        live = _row_ids(x.shape) < t_real
        xm = jnp.where(live, x, 0.0)
        dym = jnp.where(live, dy, 0.0)
        return _conv_bwd_val(dym, xm, ww, 2, live)

    return _cols(fn, name, [Cl(dxc), Cl(proj, C_LRU_X), Cl(w)],
                 [(proj.shape[0], F32), (w.shape[0], F32), (1, F32)], D_MODEL, 128)


def _ffn_conv_act(up, w, b, t_real, name):
    def fn(g, v, wg, wv, bg, bv):
        live = _row_ids(g.shape) < t_real
        gc = _conv_fwd_val(jnp.where(live, g, 0.0), wg, bg, 1)
        vc = _conv_fwd_val(jnp.where(live, v, 0.0), wv, bv, 1)
        return _gelu(gc) * vc

    b2 = b.reshape(1, -1)
    return _cols(fn, name, [Cl(up), Cl(up, D_FF), Cl(w), Cl(w, D_FF), Cl(b2), Cl(b2, D_FF)],
                 [(up.shape[0], F32)], D_FF, 128)[0]


def _ffn_conv_act_bwd(dm, up, w, b, t_real, name):
    def fn(dmb, g, v, wg, wv, bg, bv):
        live = _row_ids(g.shape) < t_real
        gm, vm = jnp.where(live, g, 0.0), jnp.where(live, v, 0.0)
        gc = _conv_fwd_val(gm, wg, bg, 1)
        vc = _conv_fwd_val(vm, wv, bv, 1)
        act, dact = _gelu_and_grad(gc)
        dmm = jnp.where(live, dmb, 0.0)
        dgx, dwg, dbg = _conv_bwd_val(dmm * vc * dact, gm, wg, 1, live)
        dvx, dwv, dbv = _conv_bwd_val(dmm * act, vm, wv, 1, live)
        return dgx, dvx, dwg, dwv, dbg, dbv

    b2 = b.reshape(1, -1)
    tp, kk = up.shape[0], w.shape[0]
    return _cols(fn, name, [Cl(dm), Cl(up), Cl(up, D_FF), Cl(w), Cl(w, D_FF), Cl(b2), Cl(b2, D_FF)],
                 [(tp, F32), (tp, F32), (kk, F32), (kk, F32), (1, F32), (1, F32)], D_FF, 128)


def _lru_gates_fwd(xc, wg, b4, lam, t_real, name):
    tp = xc.shape[0]
    tm = _tile(tp, 1408)

    def body(x_ref, w_ref, b_ref, lam_ref, r0_ref, r1_ref, i0_ref, i1_ref, a0_ref, a1_ref, u0_ref, u1_ref):
        x = x_ref[...]
        xb = x.astype(BF16)
        live = _row_ids(x.shape, pl.program_id(1) * tm) < t_real
        bb = b_ref[...]
        sp = _softplus_neg(lam_ref[...])
        gate = [_sigmoid(jnp.dot(xb, w_ref[k], preferred_element_type=F32) + bb[k:k + 1, :]) for k in range(4)]
        for d, (r_ref, i_ref, a_ref, u_ref) in enumerate(((r0_ref, i0_ref, a0_ref, u0_ref), (r1_ref, i1_ref, a1_ref, u1_ref))):
            r, ig = gate[d], gate[2 + d]
            a = jnp.exp(-LRU_C * r * sp[d:d + 1, :])
            r_ref[...] = r
            i_ref[...] = ig
            a_ref[...] = a
            u_ref[...] = jnp.where(live, jnp.sqrt(1.0 - a * a) * (ig * x), 0.0)

    blk = pl.BlockSpec((tm, LANE), lambda g, i: (i, g))
    return pl.pallas_call(
        body, name=name, grid=(LRU_BLOCKS, tp // tm),
        in_specs=[blk, pl.BlockSpec((None, 4, LANE, LANE), lambda g, i: (g, 0, 0, 0)),
                  pl.BlockSpec((4, LANE), lambda g, i: (0, g)), pl.BlockSpec((2, LANE), lambda g, i: (0, g))],
        out_specs=[blk] * 8,
        out_shape=[jax.ShapeDtypeStruct((tp, D_MODEL), F32)] * 8,
        compiler_params=_cparams("parallel", "parallel"),
    )(xc, wg, b4, lam)


def _lru_gates_bwd(l0, l1, da0, da1, r0, r1, i0, i1, a0, a1, xc, wg, lam, t_real, name):
    tp = xc.shape[0]
    tm = _tile(tp, 1408)

    def body(l0_ref, l1_ref, da0_ref, da1_ref, r0_ref, r1_ref, i0_ref, i1_ref, a0_ref, a1_ref, x_ref, w_ref, lam_ref,
             dx_ref, dw_ref, db_ref, dlam_ref):
        i = pl.program_id(1)
        x = x_ref[...]
        xb = x.astype(BF16)
        live = _row_ids(x.shape, i * tm) < t_real
        lam_ = lam_ref[...]
        sp = _softplus_neg(lam_)
        dsp_dlam = -_sigmoid(-lam_)
        dx = jnp.zeros_like(x)
        dpre = [None] * 4
        dlam_rows = []
        for d, (l_ref, da_ref, r_ref, i_ref, a_ref) in enumerate(((l0_ref, da0_ref, r0_ref, i0_ref, a0_ref),
                                                                  (l1_ref, da1_ref, r1_ref, i1_ref, a1_ref))):
            r, ig, a = r_ref[...], i_ref[...], a_ref[...]
            du = jnp.where(live, l_ref[...], 0.0)
            s = jnp.sqrt(1.0 - a * a)
            dv = du * s
            ds = du * (ig * x)
            dla = jnp.where(live, da_ref[...], 0.0) * a - ds * (a * a) / s
            dla = jnp.where(live, dla, 0.0)
            dr = dla * (-LRU_C) * sp[d:d + 1, :]
            dlam_rows.append(jnp.sum(dla * (-LRU_C) * r, axis=0, keepdims=True) * dsp_dlam[d:d + 1, :])
            dpre[d] = dr * r * (1.0 - r)
            dpre[2 + d] = dv * x * ig * (1.0 - ig)
            dx = dx + dv * ig

        @pl.when(i == 0)
        def _():
            dw_ref[...] = jnp.zeros_like(dw_ref)
            db_ref[...] = jnp.zeros_like(db_ref)
            dlam_ref[...] = jnp.zeros_like(dlam_ref)

        for k in range(4):
            pk = dpre[k].astype(BF16)
            dx = dx + lax.dot_general(pk, w_ref[k], (((1,), (1,)), ((), ())), preferred_element_type=F32)
            dw_ref[k] += lax.dot_general(xb, pk, (((0,), (0,)), ((), ())), preferred_element_type=F32)
        db_ref[...] += jnp.concatenate([jnp.sum(p, axis=0, keepdims=True) for p in dpre], axis=0)
        dlam_ref[...] += jnp.concatenate(dlam_rows, axis=0)
        dx_ref[...] = dx

    blk = pl.BlockSpec((tm, LANE), lambda g, i: (i, g))
    return pl.pallas_call(
        body, name=name, grid=(LRU_BLOCKS, tp // tm),
        in_specs=[blk] * 11 + [pl.BlockSpec((None, 4, LANE, LANE), lambda g, i: (g, 0, 0, 0)),
                               pl.BlockSpec((2, LANE), lambda g, i: (0, g))],
        out_specs=[blk, pl.BlockSpec((None, 4, LANE, LANE), lambda g, i: (g, 0, 0, 0)),
                   pl.BlockSpec((4, LANE), lambda g, i: (0, g)), pl.BlockSpec((2, LANE), lambda g, i: (0, g))],
        out_shape=[jax.ShapeDtypeStruct((tp, D_MODEL), F32), jax.ShapeDtypeStruct((LRU_BLOCKS, 4, LANE, LANE), F32),
                   jax.ShapeDtypeStruct((4, D_MODEL), F32), jax.ShapeDtypeStruct((2, D_MODEL), F32)],
        compiler_params=_cparams("parallel", "arbitrary"),
    )(l0, l1, da0, da1, r0, r1, i0, i1, a0, a1, xc, wg, lam)


def _tile_scan(a, u, reverse):
    rows = lax.broadcasted_iota(jnp.int32, a.shape, 0)
    for s in (1, 2, 4):
        if reverse:
            keep = rows < SUBLANE - s
            a_sh, u_sh = pltpu.roll(a, SUBLANE - s, 0), pltpu.roll(u, SUBLANE - s, 0)
        else:
            keep = rows >= s
            a_sh, u_sh = pltpu.roll(a, s, 0), pltpu.roll(u, s, 0)
        u = u + a * jnp.where(keep, u_sh, 0.0)
        a = a * jnp.where(keep, a_sh, 1.0)
    return a, u


def _scan_fwd(a0, u0, a1, u1, name):
    tp, d = a0.shape
    tc = 128
    nt = tp // SUBLANE

    def body(a0_ref, u0_ref, a1_ref, u1_ref, h0_ref, h1_ref):
        def step(t, carry):
            c0, c1 = carry
            f = pl.multiple_of(t * SUBLANE, SUBLANE)
            b = pl.multiple_of((nt - 1 - t) * SUBLANE, SUBLANE)
            pa, pu = _tile_scan(a0_ref[pl.ds(f, SUBLANE), :], u0_ref[pl.ds(f, SUBLANE), :], False)
            h = pu + pa * c0
            h0_ref[pl.ds(f, SUBLANE), :] = h
            c0 = h[SUBLANE - 1:SUBLANE, :]
            pa, pu = _tile_scan(a1_ref[pl.ds(b, SUBLANE), :], u1_ref[pl.ds(b, SUBLANE), :], True)
            h = pu + pa * c1
            h1_ref[pl.ds(b, SUBLANE), :] = h
            c1 = h[0:1, :]
            return c0, c1

        z = jnp.zeros((1, tc), F32)
        lax.fori_loop(0, nt, step, (z, z))

    blk = pl.BlockSpec((tp, tc), lambda j: (0, j))
    return pl.pallas_call(
        body, name=name, grid=(d // tc,), in_specs=[blk] * 4, out_specs=[blk] * 2,
        out_shape=[jax.ShapeDtypeStruct((tp, d), F32)] * 2,
        compiler_params=_cparams("parallel"),
    )(a0, u0, a1, u1)


def _scan_bwd(dh, a0, a1, h0, h1, name):
    tp, d = dh.shape
    tc = 128
    nt = tp // SUBLANE

    def body(dh_ref, a0_ref, a1_ref, h0_ref, h1_ref, l0_ref, l1_ref, da0_ref, da1_ref):
        rows8 = lax.broadcasted_iota(jnp.int32, (SUBLANE, tc), 0)

        def step(t, carry):
            c0, c1 = carry
            b = pl.multiple_of((nt - 1 - t) * SUBLANE, SUBLANE)
            f = pl.multiple_of(t * SUBLANE, SUBLANE)
            a = a0_ref[pl.ds(b, SUBLANE), :]
            a_next = jnp.where(rows8 < SUBLANE - 1, pltpu.roll(a, SUBLANE - 1, 0), 1.0)
            pa, pu = _tile_scan(a_next, dh_ref[pl.ds(b, SUBLANE), :], True)
            lam = pu + pa * c0
            l0_ref[pl.ds(b, SUBLANE), :] = lam
            c0 = a[0:1, :] * lam[0:1, :]
            a = a1_ref[pl.ds(f, SUBLANE), :]
            a_prev = jnp.where(rows8 >= 1, pltpu.roll(a, 1, 0), 1.0)
            pa, pu = _tile_scan(a_prev, dh_ref[pl.ds(f, SUBLANE), :], False)
            lam = pu + pa * c1
            l1_ref[pl.ds(f, SUBLANE), :] = lam
            c1 = a[SUBLANE - 1:SUBLANE, :] * lam[SUBLANE - 1:SUBLANE, :]
            return c0, c1

        z = jnp.zeros((1, tc), F32)
        lax.fori_loop(0, nt, step, (z, z))
        rows = lax.broadcasted_iota(jnp.int32, (tp, tc), 0)
        da0_ref[...] = l0_ref[...] * jnp.where(rows >= 1, pltpu.roll(h0_ref[...], 1, 0), 0.0)
        da1_ref[...] = l1_ref[...] * jnp.where(rows < tp - 1, pltpu.roll(h1_ref[...], tp - 1, 0), 0.0)

    blk = pl.BlockSpec((tp, tc), lambda j: (0, j))
    return pl.pallas_call(
        body, name=name, grid=(d // tc,), in_specs=[blk] * 5, out_specs=[blk] * 4,
        out_shape=[jax.ShapeDtypeStruct((tp, d), F32)] * 4,
        compiler_params=_cparams("parallel"),
    )(dh, a0, a1, h0, h1)


def _gated_h(proj, h0, h1, name):
    def fn(row0, lg, x0, x1):
        return _gelu(lg) * (x0 + x1)

    return _rows(fn, name, [Rw(proj, D_MODEL, C_LRU_G // D_MODEL), Rw(h0), Rw(h1)], [(D_MODEL, F32)])[0]


def _gated_h_bwd(dgh, proj, h0, h1, name):
    def fn(row0, dg, lg, x0, x1):
        act, dact = _gelu_and_grad(lg)
        return dg * (x0 + x1) * dact, dg * act

    return _rows(fn, name, [Rw(dgh), Rw(proj, D_MODEL, C_LRU_G // D_MODEL), Rw(h0), Rw(h1)], [(D_MODEL, F32)] * 2)


def _mix(proj, y_mla, y_lru, name):
    def fn(row0, gm, gl, ym, yl):
        return _sigmoid(gm) * ym + _sigmoid(gl) * yl

    return _rows(fn, name, [Rw(proj, D_MODEL, C_G_MLA // D_MODEL), Rw(proj, D_MODEL, C_G_LRU // D_MODEL), Rw(y_mla), Rw(y_lru)],
                 [(D_MODEL, F32)])[0]


def _mix_bwd(dz, proj, y_mla, y_lru, name):
    def fn(row0, dzb, gm, gl, ym, yl):
        sm, sl = _sigmoid(gm), _sigmoid(gl)
        return dzb * sm, dzb * sl, dzb * ym * sm * (1.0 - sm), dzb * yl * sl * (1.0 - sl)

    return _rows(fn, name, [Rw(dz), Rw(proj, D_MODEL, C_G_MLA // D_MODEL), Rw(proj, D_MODEL, C_G_LRU // D_MODEL),
                            Rw(y_mla), Rw(y_lru)], [(D_MODEL, F32)] * 4)


def _layer_fwd(h, w_in, rest_of_weights, cs, t_real, tag):
    proj = _matmul(h, w_in, tag + "proj")
    w = dict(rest_of_weights(proj), w_in=w_in)
    cqn, ckvn = _mla_norms(proj, w['q_norm'], w['kv_norm'], tag + "mla_norms")
    qext = _matmul(cqn, w['w_q'], tag + "q_up")
    kv = _matmul(ckvn, w['w_kv'], tag + "kv_up")
    qc, kc, vb = _mla_pack(qext, kv, proj, cs, tag + "mla_pack")
    o, lse = _attn_fwd(qc, kc, vb, t_real, tag + "attn_fwd")
    y_mla = _matmul(o, w['w_o_mla'], tag + "o_mla")
    xc = _lru_conv_fwd(proj, w['lru_conv_w'], w['lru_conv_b'], t_real, tag + "lru_conv")
    r0, r1, i0, i1, a0, a1, u0, u1 = _lru_gates_fwd(xc, w['w_g'], w['b4'], w['lru_lambda'], t_real, tag + "lru_gates")
    h0, h1 = _scan_fwd(a0, u0, a1, u1, tag + "lru_scan")
    gh = _gated_h(proj, h0, h1, tag + "lru_gate_out")
    y_lru = _matmul(gh, w['w_o_lru'], tag + "o_lru")
    z = _mix(proj, y_mla, y_lru, tag + "mix")
    zo = _matmul(z, w['w_out'], tag + "w_out")
    hm = _ln_fwd([(DN_ALPHA, h), (1.0, zo)], w['ln1_g'], w['ln1_b'], tag + "ln1")
    up = _matmul(hm, w['w_up'], tag + "w_up")
    m = _ffn_conv_act(up, w['ffn_conv_w'], w['ffn_conv_b'], t_real, tag + "ffn_conv")
    f = _matmul(m, w['w_down'], tag + "w_down", tk_cap=1408)
    out = _ln_fwd([(DN_ALPHA, hm), (1.0, f)], w['ln2_g'], w['ln2_b'], tag + "ln2")
    saved = dict(w=w, h=h, proj=proj, cqn=cqn, ckvn=ckvn, qc=qc, kc=kc, vb=vb, o=o, lse=lse, y_mla=y_mla, xc=xc,
                 r0=r0, r1=r1, i0=i0, i1=i1, a0=a0, a1=a1, h0=h0, h1=h1, gh=gh, y_lru=y_lru, z=z, zo=zo, hm=hm,
                 up=up, m=m, f=f)
    return out, saved


def _after(a, tok):
    return a if tok is None else a + tok.astype(a.dtype)


def _layer_bwd(dout_terms, s, cs, t_real, tag, emit, tok):
    w = s['w']
    g = {}
    du2, dg2, db2 = _ln_bwd(dout_terms, [(DN_ALPHA, s['hm']), (1.0, s['f'])], _after(w['ln2_g'], tok), tag + "ln2_bwd")
    g['ln2_g'], g['ln2_b'] = dg2, db2
    dm = _matmul(du2, w['w_down'], tag + "w_down_dx", tb=True)
    g['w_down'] = _matmul(s['m'], du2, tag + "w_down_dw", ta=True, tk_cap=1408)
    dgp, dvp, dwg_, dwv_, dbg_, dbv_ = _ffn_conv_act_bwd(dm, s['up'], w['ffn_conv_w'], w['ffn_conv_b'], t_real, tag + "ffn_conv_bwd")
    g['ffn_conv_w'] = jnp.concatenate([dwg_, dwv_], axis=1)
    g['ffn_conv_b'] = jnp.concatenate([dbg_, dbv_], axis=1)
    dup = jnp.concatenate([dgp, dvp], axis=1)
    dhm_mm = _matmul(dup, w['w_up'], tag + "w_up_dx", tb=True, tk_cap=1408)
    g['w_up'] = _matmul(s['hm'], dup, tag + "w_up_dw", ta=True, tk_cap=1408)
    tok = emit('ffn', g)
    g = {}
    du1, dg1, db1 = _ln_bwd([(DN_ALPHA, du2), (1.0, dhm_mm)], [(DN_ALPHA, s['h']), (1.0, s['zo'])],
                            _after(w['ln1_g'], tok), tag + "ln1_bwd")
    g['ln1_g'], g['ln1_b'] = dg1, db1
    dz = _matmul(du1, w['w_out'], tag + "w_out_dx", tb=True)
    g['w_out'] = _matmul(s['z'], du1, tag + "w_out_dw", ta=True, tk_cap=1408)
    dy_mla, dy_lru, dg_mla, dg_lru = _mix_bwd(dz, s['proj'], s['y_mla'], s['y_lru'], tag + "mix_bwd")
    do = _matmul(dy_mla, w['w_o_mla'], tag + "o_mla_dx", tb=True)
    g['w_o_mla'] = _matmul(s['o'], dy_mla, tag + "o_mla_dw", ta=True, tk_cap=1408)
    dqc, dkc, dv = _attn_bwd(s['qc'], s['kc'], s['vb'], do, s['o'], s['lse'], t_real, tag + "attn_bwd")
    dqext, dkn, dkrp = _mla_unpack(dqc, dkc, cs, tag + "mla_unpack")
    dkv = jnp.concatenate([dkn, dv], axis=1)
    dcqn = _matmul(dqext, w['w_q'], tag + "q_up_dx", tb=True)
    g['w_q'] = _matmul(s['cqn'], dqext, tag + "q_up_dw", ta=True, tk_cap=1408)
    dckvn = _matmul(dkv, w['w_kv'], tag + "kv_up_dx", tb=True)
    g['w_kv'] = _matmul(s['ckvn'], dkv, tag + "kv_up_dw", ta=True, tk_cap=1408)
    dcq, dckv, g['q_norm'], g['kv_norm'] = _mla_norms_bwd(dcqn, dckvn, s['proj'], w['q_norm'], w['kv_norm'], tag + "mla_norms_bwd")
    dgh = _matmul(dy_lru, w['w_o_lru'], tag + "o_lru_dx", tb=True)
    g['w_o_lru'] = _matmul(s['gh'], dy_lru, tag + "o_lru_dw", ta=True, tk_cap=1408)
    dlru_g, dhs = _gated_h_bwd(dgh, s['proj'], s['h0'], s['h1'], tag + "lru_gate_out_bwd")
    l0, l1, da0, da1 = _scan_bwd(dhs, s['a0'], s['a1'], s['h0'], s['h1'], tag + "lru_scan_bwd")
    dxc, g['w_g'], g['b4'], g['lru_lambda'] = _lru_gates_bwd(
        l0, l1, da0, da1, s['r0'], s['r1'], s['i0'], s['i1'], s['a0'], s['a1'], s['xc'], w['w_g'], w['lru_lambda'],
        t_real, tag + "lru_gates_bwd")
    dlru_x, g['lru_conv_w'], g['lru_conv_b'] = _lru_conv_bwd(dxc, s['proj'], w['lru_conv_w'], t_real, tag + "lru_conv_bwd")
    tok = emit('mid', g)
    dproj = jnp.concatenate([dlru_g, dlru_x, dg_mla, dg_lru, dcq, dckv, _after(dkrp, tok)], axis=1)
    tok = emit('in', {'w_in': _matmul(s['h'], dproj, tag + "proj_dw", ta=True, tk_cap=1408)})
    dh_mm = _matmul(dproj, w['w_in'], tag + "proj_dx", tb=True, tk_cap=1536)
    return [(DN_ALPHA, du1), (1.0, dh_mm)], tok


def _swap_halves(a, axis=-1):
    h1, h2 = jnp.split(a, 2, axis=axis)
    return jnp.concatenate([h2, h1], axis=axis)


def _w_in_kernel(w_in):
    cq, ckv, kr, lg, lx, gm, gl = jnp.split(w_in, [256, 384, 448, 1472, 2496, 3520], axis=1)
    return jnp.concatenate([lg, lx, gm, gl, cq, ckv, kr, _swap_halves(kr)], axis=1)


def _layer_weights(fl):
    w = {}
    uq = fl['w_uq']
    w['w_q'] = jnp.concatenate([uq, _swap_halves(uq[..., QK_NOPE:])], axis=-1).reshape(Q_RANK, HEADS * 2 * LANE)
    w['w_kv'] = jnp.concatenate([fl['w_uk'].reshape(KV_RANK, -1), fl['w_uv'].reshape(KV_RANK, -1)], axis=1).astype(BF16)
    w['w_g'] = jnp.moveaxis(jnp.concatenate([fl['w_rg'], fl['w_ig']], axis=0), 0, 1).astype(BF16)
    w['b4'] = jnp.concatenate([fl['b_rg'], fl['b_ig']], axis=0)
    for n in ('q_norm', 'kv_norm', 'w_o_mla', 'lru_conv_w', 'lru_conv_b', 'lru_lambda', 'w_o_lru', 'w_out', 'ln1_g',
              'ln1_b', 'w_up', 'ffn_conv_w', 'ffn_conv_b', 'w_down', 'ln2_g', 'ln2_b'):
        w[n] = fl[n]
    return w


def _layer_grads(g):
    out = {}
    if 'w_in' in g:
        lg, lx, gm, gl, cq, ckv, kr, krs = jnp.split(g['w_in'], [1024, 2048, 3072, 4096, 4352, 4480, 4544], axis=1)
        out['w_in'] = jnp.concatenate([cq, ckv, kr + _swap_halves(krs), lg, lx, gm, gl], axis=1)
    if 'w_q' in g:
        gq = g['w_q'].reshape(Q_RANK, HEADS, 2 * LANE)
        out['w_uq'] = jnp.concatenate([gq[..., :QK_NOPE], gq[..., QK_NOPE:QK_NOPE + QK_ROPE] + _swap_halves(gq[..., QK_NOPE + QK_ROPE:])], axis=-1)
    if 'w_kv' in g:
        out['w_uk'] = g['w_kv'][:, :HEADS * QK_NOPE].reshape(KV_RANK, HEADS, QK_NOPE)
        out['w_uv'] = g['w_kv'][:, HEADS * QK_NOPE:].reshape(KV_RANK, HEADS, V_HEAD)
    if 'w_g' in g:
        gg = jnp.moveaxis(g['w_g'], 1, 0)
        out['w_rg'], out['w_ig'] = gg[:2], gg[2:]
    if 'b4' in g:
        out['b_rg'], out['b_ig'] = g['b4'][:2], g['b4'][2:]
    for n in ('q_norm', 'kv_norm', 'lru_conv_b', 'ln1_g', 'ln1_b', 'ffn_conv_b', 'ln2_g', 'ln2_b'):
        if n in g:
            out[n] = g[n].reshape(-1)
    for n in ('w_o_mla', 'lru_conv_w', 'lru_lambda', 'w_o_lru', 'w_out', 'w_up', 'ffn_conv_w', 'w_down'):
        if n in g:
            out[n] = g[n]
    return out


def _rope_table(tp):
    half = QK_ROPE // 2
    inv_freq = jnp.exp(-math.log(ROPE_THETA) * jnp.arange(half, dtype=F32) / half)
    ang = jnp.arange(tp, dtype=F32)[:, None] * inv_freq[None, :]
    c, s = jnp.cos(ang), jnp.sin(ang)
    return jnp.concatenate([c, c, -s, s], axis=1)


def _local_step(x, target, meta, ln0_g, ln0_b, layer_w, t_pad, emit):
    seq = x.shape[0]
    t_real = N_META + seq
    zpad = jnp.zeros((t_pad - t_real, D_MODEL), F32)
    xin = jnp.concatenate([meta, x, zpad], axis=0)
    tgt = jnp.concatenate([jnp.zeros((N_META, D_MODEL), F32), target, zpad], axis=0)
    cs = _rope_table(t_pad)
    h = _ln_fwd([(1.0, xin)], ln0_g, ln0_b, "ln0")
    saved = []
    for l in range(DEPTH):
        w_in, rest_of_weights = layer_w[l](h)
        h, s = _layer_fwd(h, w_in, rest_of_weights, cs, t_real, "l%d_" % l)
        saved.append(s)
    dy, lossvec = _loss_head(h, tgt, t_real, "loss_head")
    loss = jnp.sum(lossvec)
    terms, tok = [(1.0, dy)], None
    for l in reversed(range(DEPTH)):
        terms, tok = _layer_bwd(terms, saved[l], cs, t_real, "l%d_" % l,
                                functools.partial(lambda stage, g, l: emit(l, stage, _layer_grads(g)), l=l), tok)
    dxin, dg0, db0 = _ln_bwd(terms, [(1.0, xin)], _after(ln0_g, tok), "ln0_bwd")
    emit(None, 'head', {'meta_tokens': dxin[:N_META], 'ln0_g': dg0.reshape(-1), 'ln0_b': db0.reshape(-1)})
    return loss, dxin[N_META:t_real]


_HBM = pl.BlockSpec(memory_space=pltpu.HBM)
_SEM = pl.BlockSpec(memory_space=pltpu.SEMAPHORE)
_SIDE_EFFECT = pltpu.SideEffectType.DATAFLOW_SIDE_EFFECTING


def _peer_copies(src_refs, land_refs, scatters, send_sems, recv_sems):
    x, y, c = lax.axis_index("x"), lax.axis_index("y"), lax.axis_index("c")
    me = 4 * x + 2 * y + c
    copies = []
    for k in range(1, N_DEV):
        px = 1 - x if k & 4 else x
        py = 1 - y if k & 2 else y
        pc = 1 - c if k & 1 else c
        for t, (src, land) in enumerate(zip(src_refs, land_refs)):
            copies.append(pltpu.make_async_remote_copy(
                src_ref=src.at[4 * px + 2 * py + pc] if scatters[t] else src, dst_ref=land.at[me],
                send_sem=send_sems.at[7 * t + k - 1], recv_sem=recv_sems.at[7 * t + k - 1],
                device_id=(px, py, pc), device_id_type=pl.DeviceIdType.MESH))
    return me, copies


def _exchange_start(groups, name):
    flat = [it for grp in groups for it in grp]
    nt, ng = len(flat), len(groups)
    scatters = [sc for _, sc in flat]
    srcs = [pltpu.with_memory_space_constraint(a, pltpu.HBM) for a, _ in flat]
    land_shapes = [a.shape if sc else (N_DEV,) + a.shape for a, sc in flat]
    lands = [pltpu.with_memory_space_constraint(lax.empty(s, a.dtype), pltpu.HBM) for s, (a, _) in zip(land_shapes, flat)]
    bounds = [0]
    for grp in groups:
        bounds.append(bounds[-1] + len(grp))

    def body(*refs):
        src_refs, land_refs = refs[:nt], refs[nt:2 * nt]
        sem_refs = refs[2 * nt:2 * nt + 2 * ng]
        token_ref = refs[4 * nt + 2 * ng]
        for gi in range(ng):
            lo, hi = bounds[gi], bounds[gi + 1]
            _, copies = _peer_copies(src_refs[lo:hi], land_refs[lo:hi], scatters[lo:hi], sem_refs[2 * gi], sem_refs[2 * gi + 1])
            for cp in copies:
                cp.start()
        token_ref[...] = jnp.zeros_like(token_ref)

    out_shape = []
    for grp in groups:
        out_shape += [pltpu.SemaphoreType.DMA((7 * len(grp),)), pltpu.SemaphoreType.DMA((7 * len(grp),))]
    out_shape += [pltpu.HBM(a.shape, a.dtype) for a in srcs] + [pltpu.HBM(s, a.dtype) for s, a in zip(land_shapes, srcs)]
    out_shape += [jax.ShapeDtypeStruct((SUBLANE, LANE), F32)]
    res = pl.pallas_call(
        body, name=name, out_shape=out_shape,
        in_specs=[_HBM] * (2 * nt),
        out_specs=[_SEM] * (2 * ng) + [_HBM] * (2 * nt) + [pl.BlockSpec(memory_space=pltpu.VMEM)],
        input_output_aliases={t: 2 * ng + t for t in range(2 * nt)},
        compiler_params=pltpu.CompilerParams(has_side_effects=_SIDE_EFFECT),
    )(*srcs, *lands)
    sems, thru, token = res[:2 * ng], res[2 * ng:2 * ng + 2 * nt], res[-1]
    states = []
    for gi in range(ng):
        lo, hi = bounds[gi], bounds[gi + 1]
        states.append((sems[2 * gi], sems[2 * gi + 1], thru[lo:hi], thru[nt + lo:nt + hi], scatters[lo:hi]))
    return states, token[0, 0]


def _exchange_wait(state, after, name):
    send_sems, recv_sems, srcs, lands, scatters = state
    n = len(srcs)

    def body(*refs):
        src_refs, land_refs, local_sems = refs[:n], refs[n:2 * n], refs[-1]
        me, copies = _peer_copies(src_refs, land_refs, scatters, refs[2 * n], refs[2 * n + 1])
        mine = [pltpu.make_async_copy(src_refs[t].at[me] if scatters[t] else src_refs[t], land_refs[t].at[me], local_sems.at[t])
                for t in range(n)]
        for cp in mine:
            cp.start()
        for cp in copies:
            cp.wait_send()
        for cp in copies:
            cp.wait_recv()
        for cp in mine:
            cp.wait()

    res = pl.pallas_call(
        body, name=name,
        out_shape=[pltpu.HBM(a.shape, a.dtype) for a in srcs] + [pltpu.HBM(a.shape, a.dtype) for a in lands],
        in_specs=[_HBM] * (2 * n) + [_SEM, _SEM, _HBM],
        out_specs=[_HBM] * (2 * n),
        input_output_aliases={t: t for t in range(2 * n)},
        scratch_shapes=[pltpu.SemaphoreType.DMA((n,))],
        compiler_params=pltpu.CompilerParams(has_side_effects=_SIDE_EFFECT),
    )(*srcs, *lands, send_sems, recv_sems, pltpu.with_memory_space_constraint(after, pltpu.HBM))
    return res[n:]


def _as_rows(shape):
    return (1, shape[0]) if len(shape) == 1 else (math.prod(shape[:-1]), shape[-1])


def _sum_adamw(pieces, w, m, v, name):
    shape = w.shape
    nl = len(pieces)
    if nl > 1 and _as_rows(shape[1:])[0] % 16:
        pieces, nl = [jnp.stack(pieces, axis=1)], 1
    rows, cols = _as_rows(shape)
    rl = rows // nl
    cap = max(16, (1 << 18) // cols // 16 * 16)
    tr = _tile(rl, cap, 16)
    nb = rl // tr
    c1 = 1.0 / (1.0 - ADAM_B1 ** ADAM_STEP)
    c2 = 1.0 / (1.0 - ADAM_B2 ** ADAM_STEP)

    def body(*refs):
        p_refs = refs[:nl]
        w_ref, m_ref, v_ref, g_ref, d_ref, nm_ref, nv_ref = refs[nl:]
        li = pl.program_id(0)

        def total(p_ref):
            acc = p_ref[0].astype(F32)
            for k in range(1, N_DEV):
                acc = acc + p_ref[k].astype(F32)
            return acc

        gg = total(p_refs[0])
        for l in range(1, nl):
            gg = jnp.where(li == l, total(p_refs[l]), gg)
        nm = ADAM_B1 * m_ref[...] + (1.0 - ADAM_B1) * gg
        nv = ADAM_B2 * v_ref[...] + (1.0 - ADAM_B2) * (gg * gg)
        g_ref[...] = gg
        d_ref[...] = -ADAM_LR * ((nm * c1) / (jnp.sqrt(nv * c2) + ADAM_EPS) + ADAM_WD * w_ref[...])
        nm_ref[...] = nm
        nv_ref[...] = nv

    blk = pl.BlockSpec((tr, cols), lambda li, i: (li * nb + i, 0))
    p_specs = [pl.BlockSpec((N_DEV, tr, cols), functools.partial(lambda li, i, l: (0, jnp.where(li == l, i, 0), 0), l=l))
               for l in range(nl)]
    res = pl.pallas_call(
        body, name=name, grid=(nl, nb),
        in_specs=p_specs + [blk] * 3, out_specs=[blk] * 4,
        out_shape=[jax.ShapeDtypeStruct((rows, cols), F32)] * 4,
        compiler_params=_cparams("parallel", "parallel"),
    )(*[p.reshape(N_DEV, rl, cols) for p in pieces], *[a.reshape(rows, cols) for a in (w, m, v)])
    return [r.reshape(shape) for r in res]


def _to_shards(full, axis):
    shp = full.shape
    a = full.reshape(shp[:axis] + (N_DEV, shp[axis] // N_DEV) + shp[axis + 1:])
    return jnp.moveaxis(a, axis, 0)


def _from_shards(blocks, axis):
    a = jnp.moveaxis(blocks, 0, axis)
    shp = a.shape
    return a.reshape(shp[:axis] + (shp[axis] * shp[axis + 1],) + shp[axis + 2:])


def kernel(x, meta_tokens, ln0_g, ln0_b, w_in, q_norm, kv_norm, w_uq, w_uk, w_uv, w_o_mla, lru_conv_w, lru_conv_b, w_rg, b_rg, w_ig, b_ig, lru_lambda, w_o_lru, w_out, ln1_g, ln1_b, w_up, ffn_conv_w, ffn_conv_b, w_down, ln2_g, ln2_b, loss_target, m_meta_tokens, m_ln0_g, m_ln0_b, m_w_in, m_q_norm, m_kv_norm, m_w_uq, m_w_uk, m_w_uv, m_w_o_mla, m_lru_conv_w, m_lru_conv_b, m_w_rg, m_b_rg, m_w_ig, m_b_ig, m_lru_lambda, m_w_o_lru, m_w_out, m_ln1_g, m_ln1_b, m_w_up, m_ffn_conv_w, m_ffn_conv_b, m_w_down, m_ln2_g, m_ln2_b, v_meta_tokens, v_ln0_g, v_ln0_b, v_w_in, v_q_norm, v_kv_norm, v_w_uq, v_w_uk, v_w_uv, v_w_o_mla, v_lru_conv_w, v_lru_conv_b, v_w_rg, v_b_rg, v_w_ig, v_b_ig, v_lru_lambda, v_w_o_lru, v_w_out, v_ln1_g, v_ln1_b, v_w_up, v_ffn_conv_w, v_ffn_conv_b, v_w_down, v_ln2_g, v_ln2_b):
    args = (meta_tokens, ln0_g, ln0_b, w_in, q_norm, kv_norm, w_uq, w_uk, w_uv, w_o_mla, lru_conv_w, lru_conv_b, w_rg, b_rg, w_ig, b_ig, lru_lambda, w_o_lru, w_out, ln1_g, ln1_b, w_up, ffn_conv_w, ffn_conv_b, w_down, ln2_g, ln2_b)
    ms = (m_meta_tokens, m_ln0_g, m_ln0_b, m_w_in, m_q_norm, m_kv_norm, m_w_uq, m_w_uk, m_w_uv, m_w_o_mla, m_lru_conv_w, m_lru_conv_b, m_w_rg, m_b_rg, m_w_ig, m_b_ig, m_lru_lambda, m_w_o_lru, m_w_out, m_ln1_g, m_ln1_b, m_w_up, m_ffn_conv_w, m_ffn_conv_b, m_w_down, m_ln2_g, m_ln2_b)
    vs = (v_meta_tokens, v_ln0_g, v_ln0_b, v_w_in, v_q_norm, v_kv_norm, v_w_uq, v_w_uk, v_w_uv, v_w_o_mla, v_lru_conv_w, v_lru_conv_b, v_w_rg, v_b_rg, v_w_ig, v_b_ig, v_lru_lambda, v_w_o_lru, v_w_out, v_ln1_g, v_ln1_b, v_w_up, v_ffn_conv_w, v_ffn_conv_b, v_w_down, v_ln2_g, v_ln2_b)
    wd, md, vd = dict(zip(WEIGHTS, args)), dict(zip(WEIGHTS, ms)), dict(zip(WEIGHTS, vs))

    def shard_axis(n, l):
        return SHARD_AXIS[n] - (0 if l is None else 1)

    def shard(n, l):
        a = wd[n] if l is None else wd[n][l]
        return a.astype(BF16) if n in BIG else a

    layered = [n for n in SHARDED if n != 'meta_tokens']
    first = [('meta_tokens', None), ('w_in', 0)]
    rest0 = [(n, 0) for n in layered if n != 'w_in']
    later = [(n, 1) for n in layered]
    gather, token = _exchange_start([[(shard(*k), False) for k in keys] for keys in (first, rest0, later)], "gather_start")

    def arrive(gi, keys, after, name):
        return {k: _from_shards(b, shard_axis(*k)) for k, b in zip(keys, _exchange_wait(gather[gi], after, name))}

    def layer_weights(got, l):
        fl = {n: wd[n][l] for n in REPLICATED if n not in ('ln0_g', 'ln0_b')}
        fl.update({n: a for (n, _), a in got.items() if n not in ('meta_tokens', 'w_in')})
        return _layer_weights(fl)

    ln0_g = _after(wd['ln0_g'], token)
    got_first = arrive(0, first, ln0_g, "gather_wait_first")

    def first_layer(h):
        return _w_in_kernel(got_first['w_in', 0]), lambda proj: layer_weights(arrive(1, rest0, proj, "gather_wait_l0"), 0)

    def second_layer(h):
        got = arrive(2, later, h, "gather_wait_l1")
        return _w_in_kernel(got['w_in', 1]), lambda proj: layer_weights(got, 1)

    sent = []
    pending = []

    def send(l, stage, grads):
        for n, g in grads.items():
            if n in SHARD_AXIS:
                g = _to_shards(g, shard_axis(n, l))
                pending.append(((n, l), (g.astype(BF16) if n in BIG else g, True)))
            else:
                pending.append(((n, l), (g.astype(BF16) if n in LARGE_REPLICATED else g, False)))
        if l == DEPTH - 1 and stage != 'in':
            return None
        (state,), tok = _exchange_start([[it for _, it in pending]], "grads_start_%s_%s" % (l, stage))
        sent.append(([k for k, _ in pending], state))
        pending.clear()
        return tok

    seq = x.shape[1]
    t_pad = -(-(N_META + seq + MIN_PAD_ROWS) // LANE) * LANE
    loss, grad_x = _local_step(x[0], loss_target[0], got_first['meta_tokens', None], ln0_g, wd['ln0_b'],
                               [first_layer, second_layer], t_pad, send)
    loss = lax.psum(loss, ("x", "y", "c"))

    pieces = {}
    for gi, (keys, state) in enumerate(sent):
        pieces.update(zip(keys, _exchange_wait(state, grad_x, "grads_wait_%d" % gi)))
    outs = {}
    for n in WEIGHTS:
        ps = [pieces[n, None]] if (n, None) in pieces else [pieces[n, l] for l in range(DEPTH)]
        outs[n] = _sum_adamw(ps, wd[n], md[n], vd[n], "adamw_" + n)
    res = [loss, grad_x[None]]
    for k in range(4):
        res += [outs[n][k] for n in WEIGHTS]
    return tuple(res)
```

```python
import functools
import math

import jax
import jax.numpy as jnp
from jax import lax
from jax.experimental import pallas as pl
from jax.experimental.pallas import tpu as pltpu

F32 = jnp.float32
BF16 = jnp.bfloat16

N_DEV = 8
D_MODEL = 1024
N_META = 16
HEADS = 8
QK_NOPE = 128
QK_ROPE = 64
V_HEAD = 128
Q_RANK = 256
KV_RANK = 128
ROPE_THETA = 10000.0
LRU_BLOCKS = 8
LRU_C = 8.0
D_FF = 2816
DEPTH = 2
DN_ALPHA = (2.0 * DEPTH) ** 0.25
LN_EPS = 1e-5
RMS_EPS = 1e-6
LN2 = math.log(2.0)
ATT_SCALE = 1.0 / math.sqrt(QK_NOPE + QK_ROPE) / LN2
NEG_BIG = -1e30

ADAM_LR = 0.001
ADAM_B1 = 0.9
ADAM_B2 = 0.999
ADAM_EPS = 1e-08
ADAM_WD = 0.01
ADAM_STEP = 10

MIN_PAD_ROWS = 2
LANE = 128
SUBLANE = 8
VMEM_LIMIT = 56 * 1024 * 1024

PROJ_COLS = 4 * D_MODEL + Q_RANK + KV_RANK + 2 * QK_ROPE
C_LRU_G, C_LRU_X, C_G_MLA, C_G_LRU = 0, D_MODEL, 2 * D_MODEL, 3 * D_MODEL
C_CQ = 4 * D_MODEL
C_CKV = C_CQ + Q_RANK
C_KRP = C_CKV + KV_RANK

WEIGHTS = ['meta_tokens', 'ln0_g', 'ln0_b', 'w_in', 'q_norm', 'kv_norm', 'w_uq', 'w_uk', 'w_uv', 'w_o_mla',
           'lru_conv_w', 'lru_conv_b', 'w_rg', 'b_rg', 'w_ig', 'b_ig', 'lru_lambda', 'w_o_lru', 'w_out',
           'ln1_g', 'ln1_b', 'w_up', 'ffn_conv_w', 'ffn_conv_b', 'w_down', 'ln2_g', 'ln2_b']
SHARD_AXIS = {'meta_tokens': 1, 'w_in': 2, 'w_uq': 1, 'w_o_mla': 1, 'lru_conv_w': 2, 'b_rg': 2, 'b_ig': 2,
              'lru_lambda': 2, 'w_o_lru': 1, 'w_out': 1, 'w_up': 2, 'ffn_conv_w': 2, 'w_down': 1}
BIG = ['w_in', 'w_uq', 'w_o_mla', 'w_o_lru', 'w_out', 'w_up', 'w_down']
SHARDED = [n for n in WEIGHTS if n in SHARD_AXIS]
REPLICATED = [n for n in WEIGHTS if n not in SHARD_AXIS]
LARGE_REPLICATED = ['w_uk', 'w_uv', 'w_rg', 'w_ig']


def _cparams(*sem):
    return pltpu.CompilerParams(dimension_semantics=sem, vmem_limit_bytes=VMEM_LIMIT)


def _tile(n, cap, unit=LANE):
    best = None
    t = unit
    while t <= min(n, cap):
        if n % t == 0:
            best = t
        t += unit
    return n if best is None else best


def _sigmoid(x):
    return 1.0 / (1.0 + jnp.exp(-x))


_GELU_C = math.sqrt(2.0 / math.pi)


def _gelu(x):
    t = jnp.tanh(_GELU_C * (x + 0.044715 * x * x * x))
    return 0.5 * x * (1.0 + t)


def _gelu_and_grad(x):
    t = jnp.tanh(_GELU_C * (x + 0.044715 * x * x * x))
    g = 0.5 * x * (1.0 + t)
    dg = 0.5 * (1.0 + t) + 0.5 * x * (1.0 - t * t) * _GELU_C * (1.0 + 3.0 * 0.044715 * x * x)
    return g, dg


def _softplus_neg(lam):
    z = jnp.exp(-jnp.abs(lam))
    w = 1.0 + z
    log1p = jnp.where(w == 1.0, z, jnp.log(w) * z / (w - 1.0))
    return jnp.maximum(-lam, 0.0) + log1p


def _row_ids(shape, row0=0):
    return lax.broadcasted_iota(jnp.int32, shape, 0) + row0


def _matmul(a, b, name, ta=False, tb=False, out_dtype=F32, tm_cap=1408, tn_cap=1024, tk_cap=2048):
    if ta:
        kdim, m = a.shape
    else:
        m, kdim = a.shape
    if tb:
        n, k2 = b.shape
    else:
        k2, n = b.shape
    assert kdim == k2, (a.shape, b.shape, ta, tb)
    tm, tn, tk = _tile(m, tm_cap), _tile(n, tn_cap), _tile(kdim, tk_cap)
    nk = kdim // tk

    def body(a_ref, b_ref, o_ref, acc_ref):
        k = pl.program_id(2)

        @pl.when(k == 0)
        def _():
            acc_ref[...] = jnp.zeros_like(acc_ref)

        dn = (((0 if ta else 1,), (1 if tb else 0,)), ((), ()))
        acc_ref[...] += lax.dot_general(a_ref[...].astype(BF16), b_ref[...].astype(BF16), dn,
                                        preferred_element_type=F32)

        @pl.when(k == nk - 1)
        def _():
            o_ref[...] = acc_ref[...].astype(o_ref.dtype)

    a_spec = pl.BlockSpec((tk, tm), lambda i, j, k: (k, i)) if ta else pl.BlockSpec((tm, tk), lambda i, j, k: (i, k))
    b_spec = pl.BlockSpec((tn, tk), lambda i, j, k: (j, k)) if tb else pl.BlockSpec((tk, tn), lambda i, j, k: (k, j))
    return pl.pallas_call(
        body, name=name,
        grid=(m // tm, n // tn, nk),
        in_specs=[a_spec, b_spec],
        out_specs=pl.BlockSpec((tm, tn), lambda i, j, k: (i, j)),
        out_shape=jax.ShapeDtypeStruct((m, n), out_dtype),
        scratch_shapes=[pltpu.VMEM((tm, tn), F32)],
        compiler_params=_cparams("parallel", "parallel", "arbitrary"),
    )(a, b)


class Rw:
    def __init__(self, arr, width=None, cb=0):
        self.arr, self.width, self.cb = arr, (arr.shape[1] if width is None else width), cb


class Pm:
    def __init__(self, arr):
        self.arr = arr


def _rows(fn, name, ins, outs, accs=(), tm_cap=384):
    tp = next(o.arr.shape[0] for o in ins if isinstance(o, Rw))
    tm = _tile(tp, tm_cap)
    n_in, n_out, n_acc = len(ins), len(outs), len(accs)

    def body(*refs):
        i = pl.program_id(0)
        res = fn(i * tm, *[r[...] for r in refs[:n_in]])
        if not isinstance(res, (tuple, list)):
            res = (res,)
        assert len(res) == n_out + n_acc, (name, len(res))
        for k in range(n_out):
            refs[n_in + k][...] = res[k].astype(refs[n_in + k].dtype)
        for k in range(n_acc):
            ref = refs[n_in + n_out + k]

            @pl.when(i == 0)
            def _():
                ref[...] = jnp.zeros_like(ref)

            ref[...] += res[n_out + k]

    in_specs = []
    for o in ins:
        if isinstance(o, Rw):
            in_specs.append(pl.BlockSpec((tm, o.width), functools.partial(lambda i, cb: (i, cb), cb=o.cb)))
        else:
            in_specs.append(pl.BlockSpec(o.arr.shape, functools.partial(lambda i, nd: (0,) * nd, nd=o.arr.ndim)))
    out_specs = [pl.BlockSpec((tm, w), lambda i: (i, 0)) for (w, _) in outs]
    out_specs += [pl.BlockSpec(s, functools.partial(lambda i, nd: (0,) * nd, nd=len(s))) for s in accs]
    out_shape = [jax.ShapeDtypeStruct((tp, w), dt) for (w, dt) in outs]
    out_shape += [jax.ShapeDtypeStruct(s, F32) for s in accs]
    res = pl.pallas_call(
        body, name=name, grid=(tp // tm,), in_specs=in_specs, out_specs=out_specs, out_shape=out_shape,
        compiler_params=_cparams("arbitrary"),
    )(*[o.arr for o in ins])
    return res


class Cl:
    def __init__(self, arr, col0=0):
        self.arr, self.col0 = arr, col0


def _cols(fn, name, ins, outs, ncols, tc):
    assert ncols % tc == 0
    n_in, n_out = len(ins), len(outs)

    def body(*refs):
        res = fn(*[r[...] for r in refs[:n_in]])
        if not isinstance(res, (tuple, list)):
            res = (res,)
        assert len(res) == n_out, (name, len(res))
        for k in range(n_out):
            refs[n_in + k][...] = res[k].astype(refs[n_in + k].dtype)

    in_specs = []
    for o in ins:
        assert o.col0 % tc == 0, (name, o.col0, tc)
        in_specs.append(pl.BlockSpec((o.arr.shape[0], tc), functools.partial(lambda j, off: (0, j + off), off=o.col0 // tc)))
    out_specs = [pl.BlockSpec((r, tc), lambda j: (0, j)) for (r, _) in outs]
    out_shape = [jax.ShapeDtypeStruct((r, ncols), dt) for (r, dt) in outs]
    return pl.pallas_call(
        body, name=name, grid=(ncols // tc,), in_specs=in_specs, out_specs=out_specs, out_shape=out_shape,
        compiler_params=_cparams("parallel"),
    )(*[o.arr for o in ins])


def _ln_stats(u):
    mu = jnp.mean(u, axis=-1, keepdims=True)
    xc = u - mu
    var = jnp.mean(xc * xc, axis=-1, keepdims=True)
    rstd = lax.rsqrt(var + LN_EPS)
    return xc * rstd, rstd


def _ln_fwd(terms, g, b, name):
    coefs = [c for c, _ in terms]

    def fn(row0, *blk):
        xs, (gg, bb) = blk[:len(coefs)], blk[len(coefs):]
        u = sum(c * x for c, x in zip(coefs, xs))
        xhat, _ = _ln_stats(u)
        return xhat * gg + bb

    d = terms[0][1].shape[1]
    return _rows(fn, name, [Rw(x) for _, x in terms] + [Pm(g.reshape(1, d)), Pm(b.reshape(1, d))], [(d, F32)])[0]


def _ln_bwd(dy_terms, u_terms, g, name):
    dc = [c for c, _ in dy_terms]
    uc = [c for c, _ in u_terms]
    d = u_terms[0][1].shape[1]

    def fn(row0, *blk):
        dys = blk[:len(dc)]
        xs = blk[len(dc):len(dc) + len(uc)]
        gg = blk[-1]
        dy = sum(c * x for c, x in zip(dc, dys))
        u = sum(c * x for c, x in zip(uc, xs))
        xhat, rstd = _ln_stats(u)
        gdy = dy * gg
        m1 = jnp.mean(gdy, axis=-1, keepdims=True)
        m2 = jnp.mean(gdy * xhat, axis=-1, keepdims=True)
        du = rstd * (gdy - m1 - xhat * m2)
        return du, jnp.sum(dy * xhat, axis=0, keepdims=True), jnp.sum(dy, axis=0, keepdims=True)

    ins = [Rw(x) for _, x in dy_terms] + [Rw(x) for _, x in u_terms] + [Pm(g.reshape(1, d))]
    return _rows(fn, name, ins, [(d, F32)], accs=[(1, d), (1, d)])


def _loss_head(y, tgt, t_real, name):
    d = y.shape[1]

    def fn(row0, yb, tb):
        rows = _row_ids(yb.shape, row0)
        live = (rows >= N_META) & (rows < t_real)
        diff = jnp.where(live, yb - tb, 0.0)
        return diff * (1.0 / d), jnp.sum(diff * diff, axis=0, keepdims=True) * (0.5 / d)

    return _rows(fn, name, [Rw(y), Rw(tgt)], [(d, F32)], accs=[(1, d)])


def _rms(x, g):
    r = lax.rsqrt(jnp.mean(x * x, axis=-1, keepdims=True) + RMS_EPS)
    return x * r * g


def _rms_bwd(dy, x, g):
    r = lax.rsqrt(jnp.mean(x * x, axis=-1, keepdims=True) + RMS_EPS)
    gdy = dy * g
    dx = r * gdy - x * (r * r * r) * jnp.mean(gdy * x, axis=-1, keepdims=True)
    return dx, jnp.sum(dy * x * r, axis=0, keepdims=True)


def _mla_norms(proj, qn, kvn, name):
    def fn(row0, cq, ckv, g1, g2):
        return _rms(cq, g1), _rms(ckv, g2)

    return _rows(fn, name, [Rw(proj, Q_RANK, C_CQ // Q_RANK), Rw(proj, KV_RANK, C_CKV // KV_RANK),
                            Pm(qn.reshape(1, Q_RANK)), Pm(kvn.reshape(1, KV_RANK))],
                 [(Q_RANK, F32), (KV_RANK, F32)])


def _mla_norms_bwd(dcqn, dckvn, proj, qn, kvn, name):
    def fn(row0, d1, d2, cq, ckv, g1, g2):
        dx1, dg1 = _rms_bwd(d1, cq, g1)
        dx2, dg2 = _rms_bwd(d2, ckv, g2)
        return dx1, dx2, dg1, dg2

    return _rows(fn, name, [Rw(dcqn), Rw(dckvn), Rw(proj, Q_RANK, C_CQ // Q_RANK), Rw(proj, KV_RANK, C_CKV // KV_RANK),
                            Pm(qn.reshape(1, Q_RANK)), Pm(kvn.reshape(1, KV_RANK))],
                 [(Q_RANK, F32), (KV_RANK, F32)], accs=[(1, Q_RANK), (1, KV_RANK)])


def _fold_rope(z):
    return z + pltpu.roll(z, QK_ROPE, 1)


def _mla_pack(qext, kv, proj, cs, name):
    tp = qext.shape[0]
    tm = _tile(tp, 384)

    def body(q_ref, kn_ref, v_ref, kr_ref, cs_ref, qo_ref, ko_ref, vo_ref):
        cs_ = cs_ref[...]
        low = lax.broadcasted_iota(jnp.int32, cs_.shape, 1) < QK_ROPE
        q = q_ref[...]
        qr = jnp.where(low, _fold_rope(q[:, QK_NOPE:] * cs_), 0.0)
        qo_ref[...] = (jnp.concatenate([q[:, :QK_NOPE], qr], axis=1) * ATT_SCALE).astype(BF16)
        kr = _fold_rope(kr_ref[...] * cs_)
        ko_ref[...] = jnp.concatenate([kn_ref[...], kr], axis=1).astype(BF16)
        vo_ref[...] = v_ref[...].astype(BF16)

    return pl.pallas_call(
        body, name=name, grid=(tp // tm, HEADS),
        in_specs=[pl.BlockSpec((tm, 2 * LANE), lambda i, h: (i, h)),
                  pl.BlockSpec((tm, LANE), lambda i, h: (i, h)),
                  pl.BlockSpec((tm, LANE), lambda i, h: (i, HEADS + h)),
                  pl.BlockSpec((tm, LANE), lambda i, h: (i, C_KRP // LANE)),
                  pl.BlockSpec((tm, LANE), lambda i, h: (i, 0))],
        out_specs=[pl.BlockSpec((tm, 2 * LANE), lambda i, h: (i, h)),
                   pl.BlockSpec((tm, 2 * LANE), lambda i, h: (i, h)),
                   pl.BlockSpec((tm, LANE), lambda i, h: (i, h))],
        out_shape=[jax.ShapeDtypeStruct((tp, HEADS * 2 * LANE), BF16),
                   jax.ShapeDtypeStruct((tp, HEADS * 2 * LANE), BF16),
                   jax.ShapeDtypeStruct((tp, HEADS * LANE), BF16)],
        compiler_params=_cparams("parallel", "arbitrary"),
    )(qext, kv, kv, proj, cs)


def _mla_unpack(dq, dk, cs, name):
    tp = dq.shape[0]
    tm = _tile(tp, 384)

    def body(dq_ref, dk_ref, cs_ref, dqe_ref, dkn_ref, dkr_ref):
        h = pl.program_id(1)
        cs_ = cs_ref[...]
        low = lax.broadcasted_iota(jnp.int32, cs_.shape, 1) < QK_ROPE
        dq_ = dq_ref[...] * ATT_SCALE
        dqr = _fold_rope(jnp.where(low, dq_[:, QK_NOPE:], 0.0)) * cs_
        dqe_ref[...] = jnp.concatenate([dq_[:, :QK_NOPE], dqr], axis=1)
        dk_ = dk_ref[...]
        dkn_ref[...] = dk_[:, :QK_NOPE]

        @pl.when(h == 0)
        def _():
            dkr_ref[...] = jnp.zeros_like(dkr_ref)

        dkr_ref[...] += _fold_rope(jnp.where(low, dk_[:, QK_NOPE:], 0.0)) * cs_

    return pl.pallas_call(
        body, name=name, grid=(tp // tm, HEADS),
        in_specs=[pl.BlockSpec((tm, 2 * LANE), lambda i, h: (i, h)),
                  pl.BlockSpec((tm, 2 * LANE), lambda i, h: (i, h)),
                  pl.BlockSpec((tm, LANE), lambda i, h: (i, 0))],
        out_specs=[pl.BlockSpec((tm, 2 * LANE), lambda i, h: (i, h)),
                   pl.BlockSpec((tm, LANE), lambda i, h: (i, h)),
                   pl.BlockSpec((tm, LANE), lambda i, h: (i, 0))],
        out_shape=[jax.ShapeDtypeStruct((tp, HEADS * 2 * LANE), F32),
                   jax.ShapeDtypeStruct((tp, HEADS * LANE), F32),
                   jax.ShapeDtypeStruct((tp, LANE), F32)],
        compiler_params=_cparams("parallel", "arbitrary"),
    )(dq, dk, cs)


def _attn_fwd(q, k, v, t_real, name):
    tp = q.shape[0]
    tq = _tile(tp, 384)
    tkc = _tile(tp, 1408)
    nkc = -(-t_real // tkc)

    def body(q_ref, k_ref, v_ref, o_ref, lse_ref):
        qb = q_ref[...]
        m = l = acc = None
        for c in range(nkc):
            s = lax.dot_general(qb, k_ref[c * tkc:(c + 1) * tkc, :], (((1,), (1,)), ((), ())), preferred_element_type=F32)
            if (c + 1) * tkc > t_real:
                cols = lax.broadcasted_iota(jnp.int32, s.shape, 1) + c * tkc
                s = jnp.where(cols < t_real, s, NEG_BIG)
            mc = jnp.max(s, axis=-1, keepdims=True)
            m_new = mc if c == 0 else jnp.maximum(m, mc)
            p = jnp.exp2(s - m_new)
            lc = jnp.sum(p, axis=-1, keepdims=True)
            pv = jnp.dot(p.astype(BF16), v_ref[c * tkc:(c + 1) * tkc, :], preferred_element_type=F32)
            if c == 0:
                l, acc = lc, pv
            else:
                alpha = jnp.exp2(m - m_new)
                l, acc = alpha * l + lc, alpha * acc + pv
            m = m_new
        o_ref[...] = acc / l
        lse_ref[...] = m + jnp.log2(l)

    return pl.pallas_call(
        body, name=name, grid=(HEADS, tp // tq),
        in_specs=[pl.BlockSpec((tq, 2 * LANE), lambda h, i: (i, h)),
                  pl.BlockSpec((tp, 2 * LANE), lambda h, i: (0, h)),
                  pl.BlockSpec((tp, LANE), lambda h, i: (0, h))],
        out_specs=[pl.BlockSpec((tq, LANE), lambda h, i: (i, h)),
                   pl.BlockSpec((None, tq, 1), lambda h, i: (h, i, 0))],
        out_shape=[jax.ShapeDtypeStruct((tp, HEADS * LANE), F32),
                   jax.ShapeDtypeStruct((HEADS, tp, 1), F32)],
        compiler_params=_cparams("parallel", "parallel"),
    )(q, k, v)


def _attn_bwd(q, k, v, do, o, lse, t_real, name):
    tp = q.shape[0]
    tq = _tile(tp, 384)
    tkc = _tile(tp, 1408)
    nkc = -(-t_real // tkc)

    def body(q_ref, k_ref, v_ref, do_ref, o_ref, lse_ref, dq_ref, dk_ref, dv_ref):
        i = pl.program_id(1)

        @pl.when(i == 0)
        def _():
            dk_ref[...] = jnp.zeros_like(dk_ref)
            dv_ref[...] = jnp.zeros_like(dv_ref)

        qb = q_ref[...]
        dob = do_ref[...]
        dob16 = dob.astype(BF16)
        dol2 = (dob * LN2).astype(BF16)
        delta = jnp.sum(dob * o_ref[...], axis=-1, keepdims=True) * LN2
        lse = lse_ref[...]
        dq = None
        for c in range(nkc):
            ks = slice(c * tkc, (c + 1) * tkc)
            kb = k_ref[ks, :]
            s = lax.dot_general(qb, kb, (((1,), (1,)), ((), ())), preferred_element_type=F32)
            p = jnp.exp2(s - lse)
            if (c + 1) * tkc > t_real:
                cols = lax.broadcasted_iota(jnp.int32, s.shape, 1) + c * tkc
                p = jnp.where(cols < t_real, p, 0.0)
            dp = lax.dot_general(dol2, v_ref[ks, :], (((1,), (1,)), ((), ())), preferred_element_type=F32)
            ds = (p * (dp - delta)).astype(BF16)
            dqc = jnp.dot(ds, kb, preferred_element_type=F32)
            dq = dqc if c == 0 else dq + dqc
            dk_ref[ks, :] += lax.dot_general(ds, qb, (((0,), (0,)), ((), ())), preferred_element_type=F32)
            dv_ref[ks, :] += lax.dot_general(p.astype(BF16), dob16, (((0,), (0,)), ((), ())), preferred_element_type=F32)
        dq_ref[...] = dq

    return pl.pallas_call(
        body, name=name, grid=(HEADS, tp // tq),
        in_specs=[pl.BlockSpec((tq, 2 * LANE), lambda h, i: (i, h)),
                  pl.BlockSpec((tp, 2 * LANE), lambda h, i: (0, h)),
                  pl.BlockSpec((tp, LANE), lambda h, i: (0, h)),
                  pl.BlockSpec((tq, LANE), lambda h, i: (i, h)),
                  pl.BlockSpec((tq, LANE), lambda h, i: (i, h)),
                  pl.BlockSpec((None, tq, 1), lambda h, i: (h, i, 0))],
        out_specs=[pl.BlockSpec((tq, 2 * LANE), lambda h, i: (i, h)),
                   pl.BlockSpec((tp, 2 * LANE), lambda h, i: (0, h)),
                   pl.BlockSpec((tp, LANE), lambda h, i: (0, h))],
        out_shape=[jax.ShapeDtypeStruct((tp, HEADS * 2 * LANE), F32),
                   jax.ShapeDtypeStruct((tp, HEADS * 2 * LANE), F32),
                   jax.ShapeDtypeStruct((tp, HEADS * LANE), F32)],
        compiler_params=_cparams("parallel", "arbitrary"),
    )(q, k, v, do, o, lse)


def _shift_rows(x, s):
    tp = x.shape[0]
    return x if s % tp == 0 else pltpu.roll(x, s % tp, 0)


def _conv_fwd_val(xm, w, b, pad_left):
    acc = b + w[0:1, :] * _shift_rows(xm, pad_left)
    for k in range(1, w.shape[0]):
        acc = acc + w[k:k + 1, :] * _shift_rows(xm, pad_left - k)
    return acc


def _conv_bwd_val(dy, xm, w, pad_left, live):
    kk = w.shape[0]
    dx = w[0:1, :] * _shift_rows(dy, -pad_left)
    dws = [jnp.sum(dy * _shift_rows(xm, pad_left), axis=0, keepdims=True)]
    for k in range(1, kk):
        dx = dx + w[k:k + 1, :] * _shift_rows(dy, k - pad_left)
        dws.append(jnp.sum(dy * _shift_rows(xm, pad_left - k), axis=0, keepdims=True))
    return jnp.where(live, dx, 0.0), jnp.concatenate(dws, axis=0), jnp.sum(dy, axis=0, keepdims=True)


def _lru_conv_fwd(proj, w, b, t_real, name):
    def fn(x, ww, bb):
        xm = jnp.where(_row_ids(x.shape) < t_real, x, 0.0)
        return _conv_fwd_val(xm, ww, bb, 2)

    return _cols(fn, name, [Cl(proj, C_LRU_X), Cl(w), Cl(b.reshape(1, -1))], [(proj.shape[0], F32)], D_MODEL, 128)[0]


def _lru_conv_bwd(dxc, proj, w, t_real, name):
    def fn(dy, x, ww):
        live = _row_ids(x.shape) < t_real
        xm = jnp.where(live, x, 0.0)
        dym = jnp.where(live, dy, 0.0)
        return _conv_bwd_val(dym, xm, ww, 2, live)

    return _cols(fn, name, [Cl(dxc), Cl(proj, C_LRU_X), Cl(w)],
                 [(proj.shape[0], F32), (w.shape[0], F32), (1, F32)], D_MODEL, 128)


def _ffn_conv_act(up, w, b, t_real, name):
    def fn(g, v, wg, wv, bg, bv):
        live = _row_ids(g.shape) < t_real
        gc = _conv_fwd_val(jnp.where(live, g, 0.0), wg, bg, 1)
        vc = _conv_fwd_val(jnp.where(live, v, 0.0), wv, bv, 1)
        return _gelu(gc) * vc

    b2 = b.reshape(1, -1)
    return _cols(fn, name, [Cl(up), Cl(up, D_FF), Cl(w), Cl(w, D_FF), Cl(b2), Cl(b2, D_FF)],
                 [(up.shape[0], F32)], D_FF, 128)[0]


def _ffn_conv_act_bwd(dm, up, w, b, t_real, name):
    def fn(dmb, g, v, wg, wv, bg, bv):
        live = _row_ids(g.shape) < t_real
        gm, vm = jnp.where(live, g, 0.0), jnp.where(live, v, 0.0)
        gc = _conv_fwd_val(gm, wg, bg, 1)
        vc = _conv_fwd_val(vm, wv, bv, 1)
        act, dact = _gelu_and_grad(gc)
        dmm = jnp.where(live, dmb, 0.0)
        dgx, dwg, dbg = _conv_bwd_val(dmm * vc * dact, gm, wg, 1, live)
        dvx, dwv, dbv = _conv_bwd_val(dmm * act, vm, wv, 1, live)
        return dgx, dvx, dwg, dwv, dbg, dbv

    b2 = b.reshape(1, -1)
    tp, kk = up.shape[0], w.shape[0]
    return _cols(fn, name, [Cl(dm), Cl(up), Cl(up, D_FF), Cl(w), Cl(w, D_FF), Cl(b2), Cl(b2, D_FF)],
                 [(tp, F32), (tp, F32), (kk, F32), (kk, F32), (1, F32), (1, F32)], D_FF, 128)


def _lru_gates_fwd(xc, wg, b4, lam, t_real, name):
    tp = xc.shape[0]
    tm = _tile(tp, 1408)

    def body(x_ref, w_ref, b_ref, lam_ref, r0_ref, r1_ref, i0_ref, i1_ref, a0_ref, a1_ref, u0_ref, u1_ref):
        x = x_ref[...]
        xb = x.astype(BF16)
        live = _row_ids(x.shape, pl.program_id(1) * tm) < t_real
        bb = b_ref[...]
        sp = _softplus_neg(lam_ref[...])
        gate = [_sigmoid(jnp.dot(xb, w_ref[k], preferred_element_type=F32) + bb[k:k + 1, :]) for k in range(4)]
        for d, (r_ref, i_ref, a_ref, u_ref) in enumerate(((r0_ref, i0_ref, a0_ref, u0_ref), (r1_ref, i1_ref, a1_ref, u1_ref))):
            r, ig = gate[d], gate[2 + d]
            a = jnp.exp(-LRU_C * r * sp[d:d + 1, :])
            r_ref[...] = r
            i_ref[...] = ig
            a_ref[...] = a
            u_ref[...] = jnp.where(live, jnp.sqrt(1.0 - a * a) * (ig * x), 0.0)

    blk = pl.BlockSpec((tm, LANE), lambda g, i: (i, g))
    return pl.pallas_call(
        body, name=name, grid=(LRU_BLOCKS, tp // tm),
        in_specs=[blk, pl.BlockSpec((None, 4, LANE, LANE), lambda g, i: (g, 0, 0, 0)),
                  pl.BlockSpec((4, LANE), lambda g, i: (0, g)), pl.BlockSpec((2, LANE), lambda g, i: (0, g))],
        out_specs=[blk] * 8,
        out_shape=[jax.ShapeDtypeStruct((tp, D_MODEL), F32)] * 8,
        compiler_params=_cparams("parallel", "parallel"),
    )(xc, wg, b4, lam)


def _lru_gates_bwd(l0, l1, da0, da1, r0, r1, i0, i1, a0, a1, xc, wg, lam, t_real, name):
    tp = xc.shape[0]
    tm = _tile(tp, 1408)

    def body(l0_ref, l1_ref, da0_ref, da1_ref, r0_ref, r1_ref, i0_ref, i1_ref, a0_ref, a1_ref, x_ref, w_ref, lam_ref,
             dx_ref, dw_ref, db_ref, dlam_ref):
        i = pl.program_id(1)
        x = x_ref[...]
        xb = x.astype(BF16)
        live = _row_ids(x.shape, i * tm) < t_real
        lam_ = lam_ref[...]
        sp = _softplus_neg(lam_)
        dsp_dlam = -_sigmoid(-lam_)
        dx = jnp.zeros_like(x)
        dpre = [None] * 4
        dlam_rows = []
        for d, (l_ref, da_ref, r_ref, i_ref, a_ref) in enumerate(((l0_ref, da0_ref, r0_ref, i0_ref, a0_ref),
                                                                  (l1_ref, da1_ref, r1_ref, i1_ref, a1_ref))):
            r, ig, a = r_ref[...], i_ref[...], a_ref[...]
            du = jnp.where(live, l_ref[...], 0.0)
            s = jnp.sqrt(1.0 - a * a)
            dv = du * s
            ds = du * (ig * x)
            dla = jnp.where(live, da_ref[...], 0.0) * a - ds * (a * a) / s
            dla = jnp.where(live, dla, 0.0)
            dr = dla * (-LRU_C) * sp[d:d + 1, :]
            dlam_rows.append(jnp.sum(dla * (-LRU_C) * r, axis=0, keepdims=True) * dsp_dlam[d:d + 1, :])
            dpre[d] = dr * r * (1.0 - r)
            dpre[2 + d] = dv * x * ig * (1.0 - ig)
            dx = dx + dv * ig

        @pl.when(i == 0)
        def _():
            dw_ref[...] = jnp.zeros_like(dw_ref)
            db_ref[...] = jnp.zeros_like(db_ref)
            dlam_ref[...] = jnp.zeros_like(dlam_ref)

        for k in range(4):
            pk = dpre[k].astype(BF16)
            dx = dx + lax.dot_general(pk, w_ref[k], (((1,), (1,)), ((), ())), preferred_element_type=F32)
            dw_ref[k] += lax.dot_general(xb, pk, (((0,), (0,)), ((), ())), preferred_element_type=F32)
        db_ref[...] += jnp.concatenate([jnp.sum(p, axis=0, keepdims=True) for p in dpre], axis=0)
        dlam_ref[...] += jnp.concatenate(dlam_rows, axis=0)
        dx_ref[...] = dx

    blk = pl.BlockSpec((tm, LANE), lambda g, i: (i, g))
    return pl.pallas_call(
        body, name=name, grid=(LRU_BLOCKS, tp // tm),
        in_specs=[blk] * 11 + [pl.BlockSpec((None, 4, LANE, LANE), lambda g, i: (g, 0, 0, 0)),
                               pl.BlockSpec((2, LANE), lambda g, i: (0, g))],
        out_specs=[blk, pl.BlockSpec((None, 4, LANE, LANE), lambda g, i: (g, 0, 0, 0)),
                   pl.BlockSpec((4, LANE), lambda g, i: (0, g)), pl.BlockSpec((2, LANE), lambda g, i: (0, g))],
        out_shape=[jax.ShapeDtypeStruct((tp, D_MODEL), F32), jax.ShapeDtypeStruct((LRU_BLOCKS, 4, LANE, LANE), F32),
                   jax.ShapeDtypeStruct((4, D_MODEL), F32), jax.ShapeDtypeStruct((2, D_MODEL), F32)],
        compiler_params=_cparams("parallel", "arbitrary"),
    )(l0, l1, da0, da1, r0, r1, i0, i1, a0, a1, xc, wg, lam)


def _tile_scan(a, u, reverse):
    rows = lax.broadcasted_iota(jnp.int32, a.shape, 0)
    for s in (1, 2, 4):
        if reverse:
            keep = rows < SUBLANE - s
            a_sh, u_sh = pltpu.roll(a, SUBLANE - s, 0), pltpu.roll(u, SUBLANE - s, 0)
        else:
            keep = rows >= s
            a_sh, u_sh = pltpu.roll(a, s, 0), pltpu.roll(u, s, 0)
        u = u + a * jnp.where(keep, u_sh, 0.0)
        a = a * jnp.where(keep, a_sh, 1.0)
    return a, u


def _scan_fwd(a0, u0, a1, u1, name):
    tp, d = a0.shape
    tc = 128
    nt = tp // SUBLANE

    def body(a0_ref, u0_ref, a1_ref, u1_ref, h0_ref, h1_ref):
        def step(t, carry):
            c0, c1 = carry
            f = pl.multiple_of(t * SUBLANE, SUBLANE)
            b = pl.multiple_of((nt - 1 - t) * SUBLANE, SUBLANE)
            pa, pu = _tile_scan(a0_ref[pl.ds(f, SUBLANE), :], u0_ref[pl.ds(f, SUBLANE), :], False)
            h = pu + pa * c0
            h0_ref[pl.ds(f, SUBLANE), :] = h
            c0 = h[SUBLANE - 1:SUBLANE, :]
            pa, pu = _tile_scan(a1_ref[pl.ds(b, SUBLANE), :], u1_ref[pl.ds(b, SUBLANE), :], True)
            h = pu + pa * c1
            h1_ref[pl.ds(b, SUBLANE), :] = h
            c1 = h[0:1, :]
            return c0, c1

        z = jnp.zeros((1, tc), F32)
        lax.fori_loop(0, nt, step, (z, z))

    blk = pl.BlockSpec((tp, tc), lambda j: (0, j))
    return pl.pallas_call(
        body, name=name, grid=(d // tc,), in_specs=[blk] * 4, out_specs=[blk] * 2,
        out_shape=[jax.ShapeDtypeStruct((tp, d), F32)] * 2,
        compiler_params=_cparams("parallel"),
    )(a0, u0, a1, u1)


def _scan_bwd(dh, a0, a1, h0, h1, name):
    tp, d = dh.shape
    tc = 128
    nt = tp // SUBLANE

    def body(dh_ref, a0_ref, a1_ref, h0_ref, h1_ref, l0_ref, l1_ref, da0_ref, da1_ref):
        rows8 = lax.broadcasted_iota(jnp.int32, (SUBLANE, tc), 0)

        def step(t, carry):
            c0, c1 = carry
            b = pl.multiple_of((nt - 1 - t) * SUBLANE, SUBLANE)
            f = pl.multiple_of(t * SUBLANE, SUBLANE)
            a = a0_ref[pl.ds(b, SUBLANE), :]
            a_next = jnp.where(rows8 < SUBLANE - 1, pltpu.roll(a, SUBLANE - 1, 0), 1.0)
            pa, pu = _tile_scan(a_next, dh_ref[pl.ds(b, SUBLANE), :], True)
            lam = pu + pa * c0
            l0_ref[pl.ds(b, SUBLANE), :] = lam
            c0 = a[0:1, :] * lam[0:1, :]
            a = a1_ref[pl.ds(f, SUBLANE), :]
            a_prev = jnp.where(rows8 >= 1, pltpu.roll(a, 1, 0), 1.0)
            pa, pu = _tile_scan(a_prev, dh_ref[pl.ds(f, SUBLANE), :], False)
            lam = pu + pa * c1
            l1_ref[pl.ds(f, SUBLANE), :] = lam
            c1 = a[SUBLANE - 1:SUBLANE, :] * lam[SUBLANE - 1:SUBLANE, :]
            return c0, c1

        z = jnp.zeros((1, tc), F32)
        lax.fori_loop(0, nt, step, (z, z))
        rows = lax.broadcasted_iota(jnp.int32, (tp, tc), 0)
        da0_ref[...] = l0_ref[...] * jnp.where(rows >= 1, pltpu.roll(h0_ref[...], 1, 0), 0.0)
        da1_ref[...] = l1_ref[...] * jnp.where(rows < tp - 1, pltpu.roll(h1_ref[...], tp - 1, 0), 0.0)

    blk = pl.BlockSpec((tp, tc), lambda j: (0, j))
    return pl.pallas_call(
        body, name=name, grid=(d // tc,), in_specs=[blk] * 5, out_specs=[blk] * 4,
        out_shape=[jax.ShapeDtypeStruct((tp, d), F32)] * 4,
        compiler_params=_cparams("parallel"),
    )(dh, a0, a1, h0, h1)


def _gated_h(proj, h0, h1, name):
    def fn(row0, lg, x0, x1):
        return _gelu(lg) * (x0 + x1)

    return _rows(fn, name, [Rw(proj, D_MODEL, C_LRU_G // D_MODEL), Rw(h0), Rw(h1)], [(D_MODEL, F32)])[0]


def _gated_h_bwd(dgh, proj, h0, h1, name):
    def fn(row0, dg, lg, x0, x1):
        act, dact = _gelu_and_grad(lg)
        return dg * (x0 + x1) * dact, dg * act

    return _rows(fn, name, [Rw(dgh), Rw(proj, D_MODEL, C_LRU_G // D_MODEL), Rw(h0), Rw(h1)], [(D_MODEL, F32)] * 2)


def _mix(proj, y_mla, y_lru, name):
    def fn(row0, gm, gl, ym, yl):
        return _sigmoid(gm) * ym + _sigmoid(gl) * yl

    return _rows(fn, name, [Rw(proj, D_MODEL, C_G_MLA // D_MODEL), Rw(proj, D_MODEL, C_G_LRU // D_MODEL), Rw(y_mla), Rw(y_lru)],
                 [(D_MODEL, F32)])[0]


def _mix_bwd(dz, proj, y_mla, y_lru, name):
    def fn(row0, dzb, gm, gl, ym, yl):
        sm, sl = _sigmoid(gm), _sigmoid(gl)
        return dzb * sm, dzb * sl, dzb * ym * sm * (1.0 - sm), dzb * yl * sl * (1.0 - sl)

    return _rows(fn, name, [Rw(dz), Rw(proj, D_MODEL, C_G_MLA // D_MODEL), Rw(proj, D_MODEL, C_G_LRU // D_MODEL),
                            Rw(y_mla), Rw(y_lru)], [(D_MODEL, F32)] * 4)


def _layer_fwd(h, w_in, rest_of_weights, cs, t_real, tag):
    proj = _matmul(h, w_in, tag + "proj")
    w = dict(rest_of_weights(proj), w_in=w_in)
    cqn, ckvn = _mla_norms(proj, w['q_norm'], w['kv_norm'], tag + "mla_norms")
    qext = _matmul(cqn, w['w_q'], tag + "q_up")
    kv = _matmul(ckvn, w['w_kv'], tag + "kv_up")
    qc, kc, vb = _mla_pack(qext, kv, proj, cs, tag + "mla_pack")
    o, lse = _attn_fwd(qc, kc, vb, t_real, tag + "attn_fwd")
    y_mla = _matmul(o, w['w_o_mla'], tag + "o_mla")
    xc = _lru_conv_fwd(proj, w['lru_conv_w'], w['lru_conv_b'], t_real, tag + "lru_conv")
    r0, r1, i0, i1, a0, a1, u0, u1 = _lru_gates_fwd(xc, w['w_g'], w['b4'], w['lru_lambda'], t_real, tag + "lru_gates")
    h0, h1 = _scan_fwd(a0, u0, a1, u1, tag + "lru_scan")
    gh = _gated_h(proj, h0, h1, tag + "lru_gate_out")
    y_lru = _matmul(gh, w['w_o_lru'], tag + "o_lru")
    z = _mix(proj, y_mla, y_lru, tag + "mix")
    zo = _matmul(z, w['w_out'], tag + "w_out")
    hm = _ln_fwd([(DN_ALPHA, h), (1.0, zo)], w['ln1_g'], w['ln1_b'], tag + "ln1")
    up = _matmul(hm, w['w_up'], tag + "w_up")
    m = _ffn_conv_act(up, w['ffn_conv_w'], w['ffn_conv_b'], t_real, tag + "ffn_conv")
    f = _matmul(m, w['w_down'], tag + "w_down", tk_cap=1408)
    out = _ln_fwd([(DN_ALPHA, hm), (1.0, f)], w['ln2_g'], w['ln2_b'], tag + "ln2")
    saved = dict(w=w, h=h, proj=proj, cqn=cqn, ckvn=ckvn, qc=qc, kc=kc, vb=vb, o=o, lse=lse, y_mla=y_mla, xc=xc,
                 r0=r0, r1=r1, i0=i0, i1=i1, a0=a0, a1=a1, h0=h0, h1=h1, gh=gh, y_lru=y_lru, z=z, zo=zo, hm=hm,
                 up=up, m=m, f=f)
    return out, saved


def _after(a, tok):
    return a if tok is None else a + tok.astype(a.dtype)


def _layer_bwd(dout_terms, s, cs, t_real, tag, emit, tok):
    w = s['w']
    g = {}
    du2, dg2, db2 = _ln_bwd(dout_terms, [(DN_ALPHA, s['hm']), (1.0, s['f'])], _after(w['ln2_g'], tok), tag + "ln2_bwd")
    g['ln2_g'], g['ln2_b'] = dg2, db2
    dm = _matmul(du2, w['w_down'], tag + "w_down_dx", tb=True)
    g['w_down'] = _matmul(s['m'], du2, tag + "w_down_dw", ta=True, tk_cap=1408)
    dgp, dvp, dwg_, dwv_, dbg_, dbv_ = _ffn_conv_act_bwd(dm, s['up'], w['ffn_conv_w'], w['ffn_conv_b'], t_real, tag + "ffn_conv_bwd")
    g['ffn_conv_w'] = jnp.concatenate([dwg_, dwv_], axis=1)
    g['ffn_conv_b'] = jnp.concatenate([dbg_, dbv_], axis=1)
    dup = jnp.concatenate([dgp, dvp], axis=1)
    dhm_mm = _matmul(dup, w['w_up'], tag + "w_up_dx", tb=True, tk_cap=1408)
    g['w_up'] = _matmul(s['hm'], dup, tag + "w_up_dw", ta=True, tk_cap=1408)
    tok = emit('ffn', g)
    g = {}
    du1, dg1, db1 = _ln_bwd([(DN_ALPHA, du2), (1.0, dhm_mm)], [(DN_ALPHA, s['h']), (1.0, s['zo'])],
                            _after(w['ln1_g'], tok), tag + "ln1_bwd")
    g['ln1_g'], g['ln1_b'] = dg1, db1
    dz = _matmul(du1, w['w_out'], tag + "w_out_dx", tb=True)
    g['w_out'] = _matmul(s['z'], du1, tag + "w_out_dw", ta=True, tk_cap=1408)
    dy_mla, dy_lru, dg_mla, dg_lru = _mix_bwd(dz, s['proj'], s['y_mla'], s['y_lru'], tag + "mix_bwd")
    do = _matmul(dy_mla, w['w_o_mla'], tag + "o_mla_dx", tb=True)
    g['w_o_mla'] = _matmul(s['o'], dy_mla, tag + "o_mla_dw", ta=True, tk_cap=1408)
    dqc, dkc, dv = _attn_bwd(s['qc'], s['kc'], s['vb'], do, s['o'], s['lse'], t_real, tag + "attn_bwd")
    dqext, dkn, dkrp = _mla_unpack(dqc, dkc, cs, tag + "mla_unpack")
    dkv = jnp.concatenate([dkn, dv], axis=1)
    dcqn = _matmul(dqext, w['w_q'], tag + "q_up_dx", tb=True)
    g['w_q'] = _matmul(s['cqn'], dqext, tag + "q_up_dw", ta=True, tk_cap=1408)
    dckvn = _matmul(dkv, w['w_kv'], tag + "kv_up_dx", tb=True)
    g['w_kv'] = _matmul(s['ckvn'], dkv, tag + "kv_up_dw", ta=True, tk_cap=1408)
    dcq, dckv, g['q_norm'], g['kv_norm'] = _mla_norms_bwd(dcqn, dckvn, s['proj'], w['q_norm'], w['kv_norm'], tag + "mla_norms_bwd")
    dgh = _matmul(dy_lru, w['w_o_lru'], tag + "o_lru_dx", tb=True)
    g['w_o_lru'] = _matmul(s['gh'], dy_lru, tag + "o_lru_dw", ta=True, tk_cap=1408)
    dlru_g, dhs = _gated_h_bwd(dgh, s['proj'], s['h0'], s['h1'], tag + "lru_gate_out_bwd")
    l0, l1, da0, da1 = _scan_bwd(dhs, s['a0'], s['a1'], s['h0'], s['h1'], tag + "lru_scan_bwd")
    dxc, g['w_g'], g['b4'], g['lru_lambda'] = _lru_gates_bwd(
        l0, l1, da0, da1, s['r0'], s['r1'], s['i0'], s['i1'], s['a0'], s['a1'], s['xc'], w['w_g'], w['lru_lambda'],
        t_real, tag + "lru_gates_bwd")
    dlru_x, g['lru_conv_w'], g['lru_conv_b'] = _lru_conv_bwd(dxc, s['proj'], w['lru_conv_w'], t_real, tag + "lru_conv_bwd")
    tok = emit('mid', g)
    dproj = jnp.concatenate([dlru_g, dlru_x, dg_mla, dg_lru, dcq, dckv, _after(dkrp, tok)], axis=1)
    tok = emit('in', {'w_in': _matmul(s['h'], dproj, tag + "proj_dw", ta=True, tk_cap=1408)})
    dh_mm = _matmul(dproj, w['w_in'], tag + "proj_dx", tb=True, tk_cap=1536)
    return [(DN_ALPHA, du1), (1.0, dh_mm)], tok


def _swap_halves(a, axis=-1):
    h1, h2 = jnp.split(a, 2, axis=axis)
    return jnp.concatenate([h2, h1], axis=axis)


def _w_in_kernel(w_in):
    cq, ckv, kr, lg, lx, gm, gl = jnp.split(w_in, [256, 384, 448, 1472, 2496, 3520], axis=1)
    return jnp.concatenate([lg, lx, gm, gl, cq, ckv, kr, _swap_halves(kr)], axis=1)


def _layer_weights(fl):
    w = {}
    uq = fl['w_uq']
    w['w_q'] = jnp.concatenate([uq, _swap_halves(uq[..., QK_NOPE:])], axis=-1).reshape(Q_RANK, HEADS * 2 * LANE)
    w['w_kv'] = jnp.concatenate([fl['w_uk'].reshape(KV_RANK, -1), fl['w_uv'].reshape(KV_RANK, -1)], axis=1).astype(BF16)
    w['w_g'] = jnp.moveaxis(jnp.concatenate([fl['w_rg'], fl['w_ig']], axis=0), 0, 1).astype(BF16)
    w['b4'] = jnp.concatenate([fl['b_rg'], fl['b_ig']], axis=0)
    for n in ('q_norm', 'kv_norm', 'w_o_mla', 'lru_conv_w', 'lru_conv_b', 'lru_lambda', 'w_o_lru', 'w_out', 'ln1_g',
              'ln1_b', 'w_up', 'ffn_conv_w', 'ffn_conv_b', 'w_down', 'ln2_g', 'ln2_b'):
        w[n] = fl[n]
    return w


def _layer_grads(g):
    out = {}
    if 'w_in' in g:
        lg, lx, gm, gl, cq, ckv, kr, krs = jnp.split(g['w_in'], [1024, 2048, 3072, 4096, 4352, 4480, 4544], axis=1)
        out['w_in'] = jnp.concatenate([cq, ckv, kr + _swap_halves(krs), lg, lx, gm, gl], axis=1)
    if 'w_q' in g:
        gq = g['w_q'].reshape(Q_RANK, HEADS, 2 * LANE)
        out['w_uq'] = jnp.concatenate([gq[..., :QK_NOPE], gq[..., QK_NOPE:QK_NOPE + QK_ROPE] + _swap_halves(gq[..., QK_NOPE + QK_ROPE:])], axis=-1)
    if 'w_kv' in g:
        out['w_uk'] = g['w_kv'][:, :HEADS * QK_NOPE].reshape(KV_RANK, HEADS, QK_NOPE)
        out['w_uv'] = g['w_kv'][:, HEADS * QK_NOPE:].reshape(KV_RANK, HEADS, V_HEAD)
    if 'w_g' in g:
        gg = jnp.moveaxis(g['w_g'], 1, 0)
        out['w_rg'], out['w_ig'] = gg[:2], gg[2:]
    if 'b4' in g:
        out['b_rg'], out['b_ig'] = g['b4'][:2], g['b4'][2:]
    for n in ('q_norm', 'kv_norm', 'lru_conv_b', 'ln1_g', 'ln1_b', 'ffn_conv_b', 'ln2_g', 'ln2_b'):
        if n in g:
            out[n] = g[n].reshape(-1)
    for n in ('w_o_mla', 'lru_conv_w', 'lru_lambda', 'w_o_lru', 'w_out', 'w_up', 'ffn_conv_w', 'w_down'):
        if n in g:
            out[n] = g[n]
    return out


def _rope_table(tp):
    half = QK_ROPE // 2
    inv_freq = jnp.exp(-math.log(ROPE_THETA) * jnp.arange(half, dtype=F32) / half)
    ang = jnp.arange(tp, dtype=F32)[:, None] * inv_freq[None, :]
    c, s = jnp.cos(ang), jnp.sin(ang)
    return jnp.concatenate([c, c, -s, s], axis=1)


def _local_step(x, target, meta, ln0_g, ln0_b, layer_w, t_pad, emit):
    seq = x.shape[0]
    t_real = N_META + seq
    zpad = jnp.zeros((t_pad - t_real, D_MODEL), F32)
    xin = jnp.concatenate([meta, x, zpad], axis=0)
    tgt = jnp.concatenate([jnp.zeros((N_META, D_MODEL), F32), target, zpad], axis=0)
    cs = _rope_table(t_pad)
    h = _ln_fwd([(1.0, xin)], ln0_g, ln0_b, "ln0")
    saved = []
    for l in range(DEPTH):
        w_in, rest_of_weights = layer_w[l](h)
        h, s = _layer_fwd(h, w_in, rest_of_weights, cs, t_real, "l%d_" % l)
        saved.append(s)
    dy, lossvec = _loss_head(h, tgt, t_real, "loss_head")
    loss = jnp.sum(lossvec)
    terms, tok = [(1.0, dy)], None
    for l in reversed(range(DEPTH)):
        terms, tok = _layer_bwd(terms, saved[l], cs, t_real, "l%d_" % l,
                                functools.partial(lambda stage, g, l: emit(l, stage, _layer_grads(g)), l=l), tok)
    dxin, dg0, db0 = _ln_bwd(terms, [(1.0, xin)], _after(ln0_g, tok), "ln0_bwd")
    emit(None, 'head', {'meta_tokens': dxin[:N_META], 'ln0_g': dg0.reshape(-1), 'ln0_b': db0.reshape(-1)})
    return loss, dxin[N_META:t_real]


_HBM = pl.BlockSpec(memory_space=pltpu.HBM)
_SEM = pl.BlockSpec(memory_space=pltpu.SEMAPHORE)
_SIDE_EFFECT = pltpu.SideEffectType.DATAFLOW_SIDE_EFFECTING


def _peer_copies(src_refs, land_refs, scatters, send_sems, recv_sems):
    x, y, c = lax.axis_index("x"), lax.axis_index("y"), lax.axis_index("c")
    me = 4 * x + 2 * y + c
    copies = []
    for k in range(1, N_DEV):
        px = 1 - x if k & 4 else x
        py = 1 - y if k & 2 else y
        pc = 1 - c if k & 1 else c
        for t, (src, land) in enumerate(zip(src_refs, land_refs)):
            copies.append(pltpu.make_async_remote_copy(
                src_ref=src.at[4 * px + 2 * py + pc] if scatters[t] else src, dst_ref=land.at[me],
                send_sem=send_sems.at[7 * t + k - 1], recv_sem=recv_sems.at[7 * t + k - 1],
                device_id=(px, py, pc), device_id_type=pl.DeviceIdType.MESH))
    return me, copies


def _exchange_start(groups, name):
    flat = [it for grp in groups for it in grp]
    nt, ng = len(flat), len(groups)
    scatters = [sc for _, sc in flat]
    srcs = [pltpu.with_memory_space_constraint(a, pltpu.HBM) for a, _ in flat]
    land_shapes = [a.shape if sc else (N_DEV,) + a.shape for a, sc in flat]
    lands = [pltpu.with_memory_space_constraint(lax.empty(s, a.dtype), pltpu.HBM) for s, (a, _) in zip(land_shapes, flat)]
    bounds = [0]
    for grp in groups:
        bounds.append(bounds[-1] + len(grp))

    def body(*refs):
        src_refs, land_refs = refs[:nt], refs[nt:2 * nt]
        sem_refs = refs[2 * nt:2 * nt + 2 * ng]
        token_ref = refs[4 * nt + 2 * ng]
        for gi in range(ng):
            lo, hi = bounds[gi], bounds[gi + 1]
            _, copies = _peer_copies(src_refs[lo:hi], land_refs[lo:hi], scatters[lo:hi], sem_refs[2 * gi], sem_refs[2 * gi + 1])
            for cp in copies:
                cp.start()
        token_ref[...] = jnp.zeros_like(token_ref)

    out_shape = []
    for grp in groups:
        out_shape += [pltpu.SemaphoreType.DMA((7 * len(grp),)), pltpu.SemaphoreType.DMA((7 * len(grp),))]
    out_shape += [pltpu.HBM(a.shape, a.dtype) for a in srcs] + [pltpu.HBM(s, a.dtype) for s, a in zip(land_shapes, srcs)]
    out_shape += [jax.ShapeDtypeStruct((SUBLANE, LANE), F32)]
    res = pl.pallas_call(
        body, name=name, out_shape=out_shape,
        in_specs=[_HBM] * (2 * nt),
        out_specs=[_SEM] * (2 * ng) + [_HBM] * (2 * nt) + [pl.BlockSpec(memory_space=pltpu.VMEM)],
        input_output_aliases={t: 2 * ng + t for t in range(2 * nt)},
        compiler_params=pltpu.CompilerParams(has_side_effects=_SIDE_EFFECT),
    )(*srcs, *lands)
    sems, thru, token = res[:2 * ng], res[2 * ng:2 * ng + 2 * nt], res[-1]
    states = []
    for gi in range(ng):
        lo, hi = bounds[gi], bounds[gi + 1]
        states.append((sems[2 * gi], sems[2 * gi + 1], thru[lo:hi], thru[nt + lo:nt + hi], scatters[lo:hi]))
    return states, token[0, 0]


def _exchange_wait(state, after, name):
    send_sems, recv_sems, srcs, lands, scatters = state
    n = len(srcs)

    def body(*refs):
        _, copies = _peer_copies(refs[:n], refs[n:2 * n], scatters, refs[2 * n], refs[2 * n + 1])
        for cp in copies:
            cp.wait_send()
        for cp in copies:
            cp.wait_recv()

    res = pl.pallas_call(
        body, name=name,
        out_shape=[pltpu.HBM(a.shape, a.dtype) for a in srcs] + [pltpu.HBM(a.shape, a.dtype) for a in lands],
        in_specs=[_HBM] * (2 * n) + [_SEM, _SEM, _HBM],
        out_specs=[_HBM] * (2 * n),
        input_output_aliases={t: t for t in range(2 * n)},
        compiler_params=pltpu.CompilerParams(has_side_effects=_SIDE_EFFECT),
    )(*srcs, *lands, send_sems, recv_sems, pltpu.with_memory_space_constraint(after, pltpu.HBM))
    me = 4 * lax.axis_index("x") + 2 * lax.axis_index("y") + lax.axis_index("c")
    out = []
    for src, land, sc in zip(res[:n], res[n:], scatters):
        own = lax.dynamic_index_in_dim(src, me, 0, keepdims=True) if sc else src[None]
        out.append(lax.dynamic_update_slice_in_dim(land, own, me, 0))
    return out


def _as_rows(shape):
    return (1, shape[0]) if len(shape) == 1 else (math.prod(shape[:-1]), shape[-1])


def _sum_adamw(pieces, w, m, v, name):
    shape = w.shape
    nl = len(pieces)
    if nl > 1 and _as_rows(shape[1:])[0] % 16:
        pieces, nl = [jnp.stack(pieces, axis=1)], 1
    rows, cols = _as_rows(shape)
    rl = rows // nl
    cap = max(16, (1 << 18) // cols // 16 * 16)
    tr = _tile(rl, cap, 16)
    nb = rl // tr
    c1 = 1.0 / (1.0 - ADAM_B1 ** ADAM_STEP)
    c2 = 1.0 / (1.0 - ADAM_B2 ** ADAM_STEP)

    def body(*refs):
        p_refs = refs[:nl]
        w_ref, m_ref, v_ref, g_ref, d_ref, nm_ref, nv_ref = refs[nl:]
        li = pl.program_id(0)

        def total(p_ref):
            acc = p_ref[0].astype(F32)
            for k in range(1, N_DEV):
                acc = acc + p_ref[k].astype(F32)
            return acc

        gg = total(p_refs[0])
        for l in range(1, nl):
            gg = jnp.where(li == l, total(p_refs[l]), gg)
        nm = ADAM_B1 * m_ref[...] + (1.0 - ADAM_B1) * gg
        nv = ADAM_B2 * v_ref[...] + (1.0 - ADAM_B2) * (gg * gg)
        g_ref[...] = gg
        d_ref[...] = -ADAM_LR * ((nm * c1) / (jnp.sqrt(nv * c2) + ADAM_EPS) + ADAM_WD * w_ref[...])
        nm_ref[...] = nm
        nv_ref[...] = nv

    blk = pl.BlockSpec((tr, cols), lambda li, i: (li * nb + i, 0))
    p_specs = [pl.BlockSpec((N_DEV, tr, cols), functools.partial(lambda li, i, l: (0, jnp.where(li == l, i, 0), 0), l=l))
               for l in range(nl)]
    res = pl.pallas_call(
        body, name=name, grid=(nl, nb),
        in_specs=p_specs + [blk] * 3, out_specs=[blk] * 4,
        out_shape=[jax.ShapeDtypeStruct((rows, cols), F32)] * 4,
        compiler_params=_cparams("parallel", "parallel"),
    )(*[p.reshape(N_DEV, rl, cols) for p in pieces], *[a.reshape(rows, cols) for a in (w, m, v)])
    return [r.reshape(shape) for r in res]


def _to_shards(full, axis):
    shp = full.shape
    a = full.reshape(shp[:axis] + (N_DEV, shp[axis] // N_DEV) + shp[axis + 1:])
    return jnp.moveaxis(a, axis, 0)


def _from_shards(blocks, axis):
    a = jnp.moveaxis(blocks, 0, axis)
    shp = a.shape
    return a.reshape(shp[:axis] + (shp[axis] * shp[axis + 1],) + shp[axis + 2:])


def kernel(x, meta_tokens, ln0_g, ln0_b, w_in, q_norm, kv_norm, w_uq, w_uk, w_uv, w_o_mla, lru_conv_w, lru_conv_b, w_rg, b_rg, w_ig, b_ig, lru_lambda, w_o_lru, w_out, ln1_g, ln1_b, w_up, ffn_conv_w, ffn_conv_b, w_down, ln2_g, ln2_b, loss_target, m_meta_tokens, m_ln0_g, m_ln0_b, m_w_in, m_q_norm, m_kv_norm, m_w_uq, m_w_uk, m_w_uv, m_w_o_mla, m_lru_conv_w, m_lru_conv_b, m_w_rg, m_b_rg, m_w_ig, m_b_ig, m_lru_lambda, m_w_o_lru, m_w_out, m_ln1_g, m_ln1_b, m_w_up, m_ffn_conv_w, m_ffn_conv_b, m_w_down, m_ln2_g, m_ln2_b, v_meta_tokens, v_ln0_g, v_ln0_b, v_w_in, v_q_norm, v_kv_norm, v_w_uq, v_w_uk, v_w_uv, v_w_o_mla, v_lru_conv_w, v_lru_conv_b, v_w_rg, v_b_rg, v_w_ig, v_b_ig, v_lru_lambda, v_w_o_lru, v_w_out, v_ln1_g, v_ln1_b, v_w_up, v_ffn_conv_w, v_ffn_conv_b, v_w_down, v_ln2_g, v_ln2_b):
    args = (meta_tokens, ln0_g, ln0_b, w_in, q_norm, kv_norm, w_uq, w_uk, w_uv, w_o_mla, lru_conv_w, lru_conv_b, w_rg, b_rg, w_ig, b_ig, lru_lambda, w_o_lru, w_out, ln1_g, ln1_b, w_up, ffn_conv_w, ffn_conv_b, w_down, ln2_g, ln2_b)
    ms = (m_meta_tokens, m_ln0_g, m_ln0_b, m_w_in, m_q_norm, m_kv_norm, m_w_uq, m_w_uk, m_w_uv, m_w_o_mla, m_lru_conv_w, m_lru_conv_b, m_w_rg, m_b_rg, m_w_ig, m_b_ig, m_lru_lambda, m_w_o_lru, m_w_out, m_ln1_g, m_ln1_b, m_w_up, m_ffn_conv_w, m_ffn_conv_b, m_w_down, m_ln2_g, m_ln2_b)
    vs = (v_meta_tokens, v_ln0_g, v_ln0_b, v_w_in, v_q_norm, v_kv_norm, v_w_uq, v_w_uk, v_w_uv, v_w_o_mla, v_lru_conv_w, v_lru_conv_b, v_w_rg, v_b_rg, v_w_ig, v_b_ig, v_lru_lambda, v_w_o_lru, v_w_out, v_ln1_g, v_ln1_b, v_w_up, v_ffn_conv_w, v_ffn_conv_b, v_w_down, v_ln2_g, v_ln2_b)
    wd, md, vd = dict(zip(WEIGHTS, args)), dict(zip(WEIGHTS, ms)), dict(zip(WEIGHTS, vs))

    def shard_axis(n, l):
        return SHARD_AXIS[n] - (0 if l is None else 1)

    def shard(n, l):
        a = wd[n] if l is None else wd[n][l]
        return a.astype(BF16) if n in BIG else a

    layered = [n for n in SHARDED if n != 'meta_tokens']
    first = [('meta_tokens', None), ('w_in', 0)]
    rest0 = [(n, 0) for n in layered if n != 'w_in']
    later = [(n, 1) for n in layered]
    gather, token = _exchange_start([[(shard(*k), False) for k in keys] for keys in (first, rest0, later)], "gather_start")

    def arrive(gi, keys, after, name):
        return {k: _from_shards(b, shard_axis(*k)) for k, b in zip(keys, _exchange_wait(gather[gi], after, name))}

    def layer_weights(got, l):
        fl = {n: wd[n][l] for n in REPLICATED if n not in ('ln0_g', 'ln0_b')}
        fl.update({n: a for (n, _), a in got.items() if n not in ('meta_tokens', 'w_in')})
        return _layer_weights(fl)

    ln0_g = _after(wd['ln0_g'], token)
    got_first = arrive(0, first, ln0_g, "gather_wait_first")

    def first_layer(h):
        return _w_in_kernel(got_first['w_in', 0]), lambda proj: layer_weights(arrive(1, rest0, proj, "gather_wait_l0"), 0)

    def second_layer(h):
        got = arrive(2, later, h, "gather_wait_l1")
        return _w_in_kernel(got['w_in', 1]), lambda proj: layer_weights(got, 1)

    sent = []
    pending = []

    def send(l, stage, grads):
        for n, g in grads.items():
            if n in SHARD_AXIS:
                g = _to_shards(g, shard_axis(n, l))
                pending.append(((n, l), (g.astype(BF16) if n in BIG else g, True)))
            else:
                pending.append(((n, l), (g.astype(BF16) if n in LARGE_REPLICATED else g, False)))
        if l == DEPTH - 1 and stage != 'in':
            return None
        (state,), tok = _exchange_start([[it for _, it in pending]], "grads_start_%s_%s" % (l, stage))
        sent.append(([k for k, _ in pending], state))
        pending.clear()
        return tok

    seq = x.shape[1]
    t_pad = -(-(N_META + seq + MIN_PAD_ROWS) // LANE) * LANE
    loss, grad_x = _local_step(x[0], loss_target[0], got_first['meta_tokens', None], ln0_g, wd['ln0_b'],
                               [first_layer, second_layer], t_pad, send)
    loss = lax.psum(loss, ("x", "y", "c"))

    pieces = {}
    for gi, (keys, state) in enumerate(sent):
        pieces.update(zip(keys, _exchange_wait(state, grad_x, "grads_wait_%d" % gi)))
    outs = {}
    for n in WEIGHTS:
        ps = [pieces[n, None]] if (n, None) in pieces else [pieces[n, l] for l in range(DEPTH)]
        outs[n] = _sum_adamw(ps, wd[n], md[n], vd[n], "adamw_" + n)
    res = [loss, grad_x[None]]
    for k in range(4):
        res += [outs[n][k] for n in WEIGHTS]
    return tuple(res)
```

```python
import functools
import math

import jax
import jax.numpy as jnp
from jax import lax
from jax.experimental import pallas as pl
from jax.experimental.pallas import tpu as pltpu

F32 = jnp.float32
BF16 = jnp.bfloat16

N_DEV = 8
D_MODEL = 1024
N_META = 16
HEADS = 8
QK_NOPE = 128
QK_ROPE = 64
V_HEAD = 128
Q_RANK = 256
KV_RANK = 128
ROPE_THETA = 10000.0
LRU_BLOCKS = 8
LRU_C = 8.0
D_FF = 2816
DEPTH = 2
DN_ALPHA = (2.0 * DEPTH) ** 0.25
LN_EPS = 1e-5
RMS_EPS = 1e-6
LN2 = math.log(2.0)
ATT_SCALE = 1.0 / math.sqrt(QK_NOPE + QK_ROPE) / LN2
NEG_BIG = -1e30

ADAM_LR = 0.001
ADAM_B1 = 0.9
ADAM_B2 = 0.999
ADAM_EPS = 1e-08
ADAM_WD = 0.01
ADAM_STEP = 10

MIN_PAD_ROWS = 2
LANE = 128
SUBLANE = 8
VMEM_LIMIT = 56 * 1024 * 1024

PROJ_COLS = 4 * D_MODEL + Q_RANK + KV_RANK + 2 * QK_ROPE
C_LRU_G, C_LRU_X, C_G_MLA, C_G_LRU = 0, D_MODEL, 2 * D_MODEL, 3 * D_MODEL
C_CQ = 4 * D_MODEL
C_CKV = C_CQ + Q_RANK
C_KRP = C_CKV + KV_RANK

WEIGHTS = ['meta_tokens', 'ln0_g', 'ln0_b', 'w_in', 'q_norm', 'kv_norm', 'w_uq', 'w_uk', 'w_uv', 'w_o_mla',
           'lru_conv_w', 'lru_conv_b', 'w_rg', 'b_rg', 'w_ig', 'b_ig', 'lru_lambda', 'w_o_lru', 'w_out',
           'ln1_g', 'ln1_b', 'w_up', 'ffn_conv_w', 'ffn_conv_b', 'w_down', 'ln2_g', 'ln2_b']
SHARD_AXIS = {'meta_tokens': 1, 'w_in': 2, 'w_uq': 1, 'w_o_mla': 1, 'lru_conv_w': 2, 'b_rg': 2, 'b_ig': 2,
              'lru_lambda': 2, 'w_o_lru': 1, 'w_out': 1, 'w_up': 2, 'ffn_conv_w': 2, 'w_down': 1}
BIG = ['w_in', 'w_uq', 'w_o_mla', 'w_o_lru', 'w_out', 'w_up', 'w_down']
SHARDED = [n for n in WEIGHTS if n in SHARD_AXIS]
REPLICATED = [n for n in WEIGHTS if n not in SHARD_AXIS]
LARGE_REPLICATED = ['w_uk', 'w_uv', 'w_rg', 'w_ig']
STAGE_WEIGHTS = [['w_uq', 'lru_conv_w', 'b_rg', 'b_ig', 'lru_lambda'], ['w_o_mla', 'w_o_lru', 'w_out'], ['w_up', 'ffn_conv_w', 'w_down']]
STAGE_REPLICATED = [['q_norm', 'kv_norm', 'w_uk', 'w_uv', 'lru_conv_b', 'w_rg', 'w_ig'], ['ln1_g', 'ln1_b'], ['ffn_conv_b', 'ln2_g', 'ln2_b']]


def _cparams(*sem):
    return pltpu.CompilerParams(dimension_semantics=sem, vmem_limit_bytes=VMEM_LIMIT)


def _tile(n, cap, unit=LANE):
    best = None
    t = unit
    while t <= min(n, cap):
        if n % t == 0:
            best = t
        t += unit
    return n if best is None else best


def _sigmoid(x):
    return 1.0 / (1.0 + jnp.exp(-x))


_GELU_C = math.sqrt(2.0 / math.pi)


def _gelu(x):
    t = jnp.tanh(_GELU_C * (x + 0.044715 * x * x * x))
    return 0.5 * x * (1.0 + t)


def _gelu_and_grad(x):
    t = jnp.tanh(_GELU_C * (x + 0.044715 * x * x * x))
    g = 0.5 * x * (1.0 + t)
    dg = 0.5 * (1.0 + t) + 0.5 * x * (1.0 - t * t) * _GELU_C * (1.0 + 3.0 * 0.044715 * x * x)
    return g, dg


def _softplus_neg(lam):
    z = jnp.exp(-jnp.abs(lam))
    w = 1.0 + z
    log1p = jnp.where(w == 1.0, z, jnp.log(w) * z / (w - 1.0))
    return jnp.maximum(-lam, 0.0) + log1p


def _row_ids(shape, row0=0):
    return lax.broadcasted_iota(jnp.int32, shape, 0) + row0


def _matmul(a, b, name, ta=False, tb=False, out_dtype=F32, tm_cap=1408, tn_cap=1024, tk_cap=2048):
    if ta:
        kdim, m = a.shape
    else:
        m, kdim = a.shape
    if tb:
        n, k2 = b.shape
    else:
        k2, n = b.shape
    assert kdim == k2, (a.shape, b.shape, ta, tb)
    tm, tn, tk = _tile(m, tm_cap), _tile(n, tn_cap), _tile(kdim, tk_cap)
    nk = kdim // tk

    def body(a_ref, b_ref, o_ref, *acc):
        dn = (((0 if ta else 1,), (1 if tb else 0,)), ((), ()))
        part = lax.dot_general(a_ref[...].astype(BF16), b_ref[...].astype(BF16), dn, preferred_element_type=F32)
        if nk == 1:
            o_ref[...] = part.astype(o_ref.dtype)
            return
        acc_ref, k = acc[0], pl.program_id(2)

        @pl.when(k == 0)
        def _():
            acc_ref[...] = part

        @pl.when(k > 0)
        def _():
            acc_ref[...] += part

        @pl.when(k == nk - 1)
        def _():
            o_ref[...] = acc_ref[...].astype(o_ref.dtype)

    a_spec = pl.BlockSpec((tk, tm), lambda i, j, k: (k, i)) if ta else pl.BlockSpec((tm, tk), lambda i, j, k: (i, k))
    b_spec = pl.BlockSpec((tn, tk), lambda i, j, k: (j, k)) if tb else pl.BlockSpec((tk, tn), lambda i, j, k: (k, j))
    return pl.pallas_call(
        body, name=name,
        grid=(m // tm, n // tn, nk),
        in_specs=[a_spec, b_spec],
        out_specs=pl.BlockSpec((tm, tn), lambda i, j, k: (i, j)),
        out_shape=jax.ShapeDtypeStruct((m, n), out_dtype),
        scratch_shapes=[pltpu.VMEM((tm, tn), F32)] if nk > 1 else [],
        compiler_params=_cparams("parallel", "parallel", "arbitrary"),
    )(a, b)


class Rw:
    def __init__(self, arr, width=None, cb=0):
        self.arr, self.width, self.cb = arr, (arr.shape[1] if width is None else width), cb


class Pm:
    def __init__(self, arr):
        self.arr = arr


def _rows(fn, name, ins, outs, accs=(), tm_cap=384):
    tp = next(o.arr.shape[0] for o in ins if isinstance(o, Rw))
    tm = _tile(tp, tm_cap)
    n_in, n_out, n_acc = len(ins), len(outs), len(accs)

    def body(*refs):
        i = pl.program_id(0)
        res = fn(i * tm, *[r[...] for r in refs[:n_in]])
        if not isinstance(res, (tuple, list)):
            res = (res,)
        assert len(res) == n_out + n_acc, (name, len(res))
        for k in range(n_out):
            refs[n_in + k][...] = res[k].astype(refs[n_in + k].dtype)
        for k in range(n_acc):
            ref = refs[n_in + n_out + k]

            @pl.when(i == 0)
            def _():
                ref[...] = jnp.zeros_like(ref)

            ref[...] += res[n_out + k]

    in_specs = []
    for o in ins:
        if isinstance(o, Rw):
            in_specs.append(pl.BlockSpec((tm, o.width), functools.partial(lambda i, cb: (i, cb), cb=o.cb)))
        else:
            in_specs.append(pl.BlockSpec(o.arr.shape, functools.partial(lambda i, nd: (0,) * nd, nd=o.arr.ndim)))
    out_specs = [pl.BlockSpec((tm, w), lambda i: (i, 0)) for (w, _) in outs]
    out_specs += [pl.BlockSpec(s, functools.partial(lambda i, nd: (0,) * nd, nd=len(s))) for s in accs]
    out_shape = [jax.ShapeDtypeStruct((tp, w), dt) for (w, dt) in outs]
    out_shape += [jax.ShapeDtypeStruct(s, F32) for s in accs]
    res = pl.pallas_call(
        body, name=name, grid=(tp // tm,), in_specs=in_specs, out_specs=out_specs, out_shape=out_shape,
        compiler_params=_cparams("arbitrary"),
    )(*[o.arr for o in ins])
    return res


class Cl:
    def __init__(self, arr, col0=0):
        self.arr, self.col0 = arr, col0


def _cols(fn, name, ins, outs, ncols, tc):
    assert ncols % tc == 0
    n_in, n_out = len(ins), len(outs)

    def body(*refs):
        res = fn(*[r[...] for r in refs[:n_in]])
        if not isinstance(res, (tuple, list)):
            res = (res,)
        assert len(res) == n_out, (name, len(res))
        for k in range(n_out):
            refs[n_in + k][...] = res[k].astype(refs[n_in + k].dtype)

    in_specs = []
    for o in ins:
        assert o.col0 % tc == 0, (name, o.col0, tc)
        in_specs.append(pl.BlockSpec((o.arr.shape[0], tc), functools.partial(lambda j, off: (0, j + off), off=o.col0 // tc)))
    out_specs = [pl.BlockSpec((r, tc), lambda j: (0, j)) for (r, _) in outs]
    out_shape = [jax.ShapeDtypeStruct((r, ncols), dt) for (r, dt) in outs]
    return pl.pallas_call(
        body, name=name, grid=(ncols // tc,), in_specs=in_specs, out_specs=out_specs, out_shape=out_shape,
        compiler_params=_cparams("parallel"),
    )(*[o.arr for o in ins])


def _ln_stats(u):
    mu = jnp.mean(u, axis=-1, keepdims=True)
    xc = u - mu
    var = jnp.mean(xc * xc, axis=-1, keepdims=True)
    rstd = lax.rsqrt(var + LN_EPS)
    return xc * rstd, rstd


def _ln_fwd(terms, g, b, name):
    coefs = [c for c, _ in terms]

    def fn(row0, *blk):
        xs, (gg, bb) = blk[:len(coefs)], blk[len(coefs):]
        u = sum(c * x for c, x in zip(coefs, xs))
        xhat, _ = _ln_stats(u)
        return xhat * gg + bb

    d = terms[0][1].shape[1]
    return _rows(fn, name, [Rw(x) for _, x in terms] + [Pm(g.reshape(1, d)), Pm(b.reshape(1, d))], [(d, F32)])[0]


def _ln_bwd(dy_terms, u_terms, g, name):
    dc = [c for c, _ in dy_terms]
    uc = [c for c, _ in u_terms]
    d = u_terms[0][1].shape[1]

    def fn(row0, *blk):
        dys = blk[:len(dc)]
        xs = blk[len(dc):len(dc) + len(uc)]
        gg = blk[-1]
        dy = sum(c * x for c, x in zip(dc, dys))
        u = sum(c * x for c, x in zip(uc, xs))
        xhat, rstd = _ln_stats(u)
        gdy = dy * gg
        m1 = jnp.mean(gdy, axis=-1, keepdims=True)
        m2 = jnp.mean(gdy * xhat, axis=-1, keepdims=True)
        du = rstd * (gdy - m1 - xhat * m2)
        return du, jnp.sum(dy * xhat, axis=0, keepdims=True), jnp.sum(dy, axis=0, keepdims=True)

    ins = [Rw(x) for _, x in dy_terms] + [Rw(x) for _, x in u_terms] + [Pm(g.reshape(1, d))]
    return _rows(fn, name, ins, [(d, F32)], accs=[(1, d), (1, d)])


def _loss_head(y, tgt, t_real, name):
    d = y.shape[1]

    def fn(row0, yb, tb):
        rows = _row_ids(yb.shape, row0)
        live = (rows >= N_META) & (rows < t_real)
        diff = jnp.where(live, yb - tb, 0.0)
        return diff * (1.0 / d), jnp.sum(diff * diff, axis=0, keepdims=True) * (0.5 / d)

    return _rows(fn, name, [Rw(y), Rw(tgt)], [(d, F32)], accs=[(1, d)])


def _rms(x, g):
    r = lax.rsqrt(jnp.mean(x * x, axis=-1, keepdims=True) + RMS_EPS)
    return x * r * g


def _rms_bwd(dy, x, g):
    r = lax.rsqrt(jnp.mean(x * x, axis=-1, keepdims=True) + RMS_EPS)
    gdy = dy * g
    dx = r * gdy - x * (r * r * r) * jnp.mean(gdy * x, axis=-1, keepdims=True)
    return dx, jnp.sum(dy * x * r, axis=0, keepdims=True)


def _mla_norms(proj, qn, kvn, name):
    def fn(row0, cq, ckv, g1, g2):
        return _rms(cq, g1), _rms(ckv, g2)

    return _rows(fn, name, [Rw(proj, Q_RANK, C_CQ // Q_RANK), Rw(proj, KV_RANK, C_CKV // KV_RANK),
                            Pm(qn.reshape(1, Q_RANK)), Pm(kvn.reshape(1, KV_RANK))],
                 [(Q_RANK, F32), (KV_RANK, F32)])


def _mla_norms_bwd(dcqn, dckvn, proj, qn, kvn, name):
    def fn(row0, d1, d2, cq, ckv, g1, g2):
        dx1, dg1 = _rms_bwd(d1, cq, g1)
        dx2, dg2 = _rms_bwd(d2, ckv, g2)
        return dx1, dx2, dg1, dg2

    return _rows(fn, name, [Rw(dcqn), Rw(dckvn), Rw(proj, Q_RANK, C_CQ // Q_RANK), Rw(proj, KV_RANK, C_CKV // KV_RANK),
                            Pm(qn.reshape(1, Q_RANK)), Pm(kvn.reshape(1, KV_RANK))],
                 [(Q_RANK, F32), (KV_RANK, F32)], accs=[(1, Q_RANK), (1, KV_RANK)])


def _fold_rope(z):
    return z + pltpu.roll(z, QK_ROPE, 1)


def _mla_pack(qext, kv, proj, cs, name):
    tp = qext.shape[0]
    tm = _tile(tp, 384)

    def body(q_ref, kn_ref, v_ref, kr_ref, cs_ref, qo_ref, ko_ref, vo_ref):
        cs_ = cs_ref[...]
        low = lax.broadcasted_iota(jnp.int32, cs_.shape, 1) < QK_ROPE
        q = q_ref[...]
        qr = jnp.where(low, _fold_rope(q[:, QK_NOPE:] * cs_), 0.0)
        qo_ref[...] = (jnp.concatenate([q[:, :QK_NOPE], qr], axis=1) * ATT_SCALE).astype(BF16)
        kr = _fold_rope(kr_ref[...] * cs_)
        ko_ref[...] = jnp.concatenate([kn_ref[...], kr], axis=1).astype(BF16)
        vo_ref[...] = v_ref[...].astype(BF16)

    return pl.pallas_call(
        body, name=name, grid=(tp // tm, HEADS),
        in_specs=[pl.BlockSpec((tm, 2 * LANE), lambda i, h: (i, h)),
                  pl.BlockSpec((tm, LANE), lambda i, h: (i, h)),
                  pl.BlockSpec((tm, LANE), lambda i, h: (i, HEADS + h)),
                  pl.BlockSpec((tm, LANE), lambda i, h: (i, C_KRP // LANE)),
                  pl.BlockSpec((tm, LANE), lambda i, h: (i, 0))],
        out_specs=[pl.BlockSpec((tm, 2 * LANE), lambda i, h: (i, h)),
                   pl.BlockSpec((tm, 2 * LANE), lambda i, h: (i, h)),
                   pl.BlockSpec((tm, LANE), lambda i, h: (i, h))],
        out_shape=[jax.ShapeDtypeStruct((tp, HEADS * 2 * LANE), BF16),
                   jax.ShapeDtypeStruct((tp, HEADS * 2 * LANE), BF16),
                   jax.ShapeDtypeStruct((tp, HEADS * LANE), BF16)],
        compiler_params=_cparams("parallel", "arbitrary"),
    )(qext, kv, kv, proj, cs)


def _mla_unpack(dq, dk, cs, name):
    tp = dq.shape[0]
    tm = _tile(tp, 384)

    def body(dq_ref, dk_ref, cs_ref, dqe_ref, dkn_ref, dkr_ref):
        h = pl.program_id(1)
        cs_ = cs_ref[...]
        low = lax.broadcasted_iota(jnp.int32, cs_.shape, 1) < QK_ROPE
        dq_ = dq_ref[...] * ATT_SCALE
        dqr = _fold_rope(jnp.where(low, dq_[:, QK_NOPE:], 0.0)) * cs_
        dqe_ref[...] = jnp.concatenate([dq_[:, :QK_NOPE], dqr], axis=1)
        dk_ = dk_ref[...]
        dkn_ref[...] = dk_[:, :QK_NOPE]

        @pl.when(h == 0)
        def _():
            dkr_ref[...] = jnp.zeros_like(dkr_ref)

        dkr_ref[...] += _fold_rope(jnp.where(low, dk_[:, QK_NOPE:], 0.0)) * cs_

    return pl.pallas_call(
        body, name=name, grid=(tp // tm, HEADS),
        in_specs=[pl.BlockSpec((tm, 2 * LANE), lambda i, h: (i, h)),
                  pl.BlockSpec((tm, 2 * LANE), lambda i, h: (i, h)),
                  pl.BlockSpec((tm, LANE), lambda i, h: (i, 0))],
        out_specs=[pl.BlockSpec((tm, 2 * LANE), lambda i, h: (i, h)),
                   pl.BlockSpec((tm, LANE), lambda i, h: (i, h)),
                   pl.BlockSpec((tm, LANE), lambda i, h: (i, 0))],
        out_shape=[jax.ShapeDtypeStruct((tp, HEADS * 2 * LANE), F32),
                   jax.ShapeDtypeStruct((tp, HEADS * LANE), F32),
                   jax.ShapeDtypeStruct((tp, LANE), F32)],
        compiler_params=_cparams("parallel", "arbitrary"),
    )(dq, dk, cs)


def _attn_fwd(q, k, v, t_real, name):
    tp = q.shape[0]
    tq = _tile(tp, 384)
    tkc = _tile(tp, 1408)
    nkc = -(-t_real // tkc)

    def body(q_ref, k_ref, v_ref, o_ref, lse_ref):
        qb = q_ref[...]
        m = l = acc = None
        for c in range(nkc):
            s = lax.dot_general(qb, k_ref[c * tkc:(c + 1) * tkc, :], (((1,), (1,)), ((), ())), preferred_element_type=F32)
            if (c + 1) * tkc > t_real:
                cols = lax.broadcasted_iota(jnp.int32, s.shape, 1) + c * tkc
                s = jnp.where(cols < t_real, s, NEG_BIG)
            mc = jnp.max(s, axis=-1, keepdims=True)
            m_new = mc if c == 0 else jnp.maximum(m, mc)
            p = jnp.exp2(s - m_new)
            lc = jnp.sum(p, axis=-1, keepdims=True)
            pv = jnp.dot(p.astype(BF16), v_ref[c * tkc:(c + 1) * tkc, :], preferred_element_type=F32)
            if c == 0:
                l, acc = lc, pv
            else:
                alpha = jnp.exp2(m - m_new)
                l, acc = alpha * l + lc, alpha * acc + pv
            m = m_new
        o_ref[...] = acc / l
        lse_ref[...] = m + jnp.log2(l)

    return pl.pallas_call(
        body, name=name, grid=(HEADS, tp // tq),
        in_specs=[pl.BlockSpec((tq, 2 * LANE), lambda h, i: (i, h)),
                  pl.BlockSpec((tp, 2 * LANE), lambda h, i: (0, h)),
                  pl.BlockSpec((tp, LANE), lambda h, i: (0, h))],
        out_specs=[pl.BlockSpec((tq, LANE), lambda h, i: (i, h)),
                   pl.BlockSpec((None, tq, 1), lambda h, i: (h, i, 0))],
        out_shape=[jax.ShapeDtypeStruct((tp, HEADS * LANE), F32),
                   jax.ShapeDtypeStruct((HEADS, tp, 1), F32)],
        compiler_params=_cparams("parallel", "parallel"),
    )(q, k, v)


def _attn_bwd(q, k, v, do, o, lse, t_real, name):
    tp = q.shape[0]
    tq = _tile(tp, 384)
    tkc = _tile(tp, 1408)
    nkc = -(-t_real // tkc)

    def body(q_ref, k_ref, v_ref, do_ref, o_ref, lse_ref, dq_ref, dk_ref, dv_ref):
        i = pl.program_id(1)

        @pl.when(i == 0)
        def _():
            dk_ref[...] = jnp.zeros_like(dk_ref)
            dv_ref[...] = jnp.zeros_like(dv_ref)

        qb = q_ref[...]
        dob = do_ref[...]
        dob16 = dob.astype(BF16)
        dol2 = (dob * LN2).astype(BF16)
        delta = jnp.sum(dob * o_ref[...], axis=-1, keepdims=True) * LN2
        lse = lse_ref[...]
        dq = None
        for c in range(nkc):
            ks = slice(c * tkc, (c + 1) * tkc)
            kb = k_ref[ks, :]
            s = lax.dot_general(qb, kb, (((1,), (1,)), ((), ())), preferred_element_type=F32)
            p = jnp.exp2(s - lse)
            if (c + 1) * tkc > t_real:
                cols = lax.broadcasted_iota(jnp.int32, s.shape, 1) + c * tkc
                p = jnp.where(cols < t_real, p, 0.0)
            dp = lax.dot_general(dol2, v_ref[ks, :], (((1,), (1,)), ((), ())), preferred_element_type=F32)
            ds = (p * (dp - delta)).astype(BF16)
            dqc = jnp.dot(ds, kb, preferred_element_type=F32)
            dq = dqc if c == 0 else dq + dqc
            dk_ref[ks, :] += lax.dot_general(ds, qb, (((0,), (0,)), ((), ())), preferred_element_type=F32)
            dv_ref[ks, :] += lax.dot_general(p.astype(BF16), dob16, (((0,), (0,)), ((), ())), preferred_element_type=F32)
        dq_ref[...] = dq

    return pl.pallas_call(
        body, name=name, grid=(HEADS, tp // tq),
        in_specs=[pl.BlockSpec((tq, 2 * LANE), lambda h, i: (i, h)),
                  pl.BlockSpec((tp, 2 * LANE), lambda h, i: (0, h)),
                  pl.BlockSpec((tp, LANE), lambda h, i: (0, h)),
                  pl.BlockSpec((tq, LANE), lambda h, i: (i, h)),
                  pl.BlockSpec((tq, LANE), lambda h, i: (i, h)),
                  pl.BlockSpec((None, tq, 1), lambda h, i: (h, i, 0))],
        out_specs=[pl.BlockSpec((tq, 2 * LANE), lambda h, i: (i, h)),
                   pl.BlockSpec((tp, 2 * LANE), lambda h, i: (0, h)),
                   pl.BlockSpec((tp, LANE), lambda h, i: (0, h))],
        out_shape=[jax.ShapeDtypeStruct((tp, HEADS * 2 * LANE), F32),
                   jax.ShapeDtypeStruct((tp, HEADS * 2 * LANE), F32),
                   jax.ShapeDtypeStruct((tp, HEADS * LANE), F32)],
        compiler_params=_cparams("parallel", "arbitrary"),
    )(q, k, v, do, o, lse)


def _shift_rows(x, s):
    tp = x.shape[0]
    return x if s % tp == 0 else pltpu.roll(x, s % tp, 0)


def _conv_fwd_val(xm, w, b, pad_left):
    acc = b + w[0:1, :] * _shift_rows(xm, pad_left)
    for k in range(1, w.shape[0]):
        acc = acc + w[k:k + 1, :] * _shift_rows(xm, pad_left - k)
    return acc


def _conv_bwd_val(dy, xm, w, pad_left, live):
    kk = w.shape[0]
    dx = w[0:1, :] * _shift_rows(dy, -pad_left)
    dws = [jnp.sum(dy * _shift_rows(xm, pad_left), axis=0, keepdims=True)]
    for k in range(1, kk):
        dx = dx + w[k:k + 1, :] * _shift_rows(dy, k - pad_left)
        dws.append(jnp.sum(dy * _shift_rows(xm, pad_left - k), axis=0, keepdims=True))
    return jnp.where(live, dx, 0.0), jnp.concatenate(dws, axis=0), jnp.sum(dy, axis=0, keepdims=True)


def _lru_conv_fwd(proj, w, b, t_real, name):
    def fn(x, ww, bb):
        xm = jnp.where(_row_ids(x.shape) < t_real, x, 0.0)
        return _conv_fwd_val(xm, ww, bb, 2)

    return _cols(fn, name, [Cl(proj, C_LRU_X), Cl(w), Cl(b.reshape(1, -1))], [(proj.shape[0], F32)], D_MODEL, 128)[0]


def _lru_conv_bwd(dxc, proj, w, t_real, name):
    def fn(dy, x, ww):
        live = _row_ids(x.shape) < t_real
        xm = jnp.where(live, x, 0.0)
        dym = jnp.where(live, dy, 0.0)
        return _conv_bwd_val(dym, xm, ww, 2, live)

    return _cols(fn, name, [Cl(dxc), Cl(proj, C_LRU_X), Cl(w)],
                 [(proj.shape[0], F32), (w.shape[0], F32), (1, F32)], D_MODEL, 128)


def _ffn_conv_act(up, w, b, t_real, name):
    def fn(g, v, wg, wv, bg, bv):
        live = _row_ids(g.shape) < t_real
        gc = _conv_fwd_val(jnp.where(live, g, 0.0), wg, bg, 1)
        vc = _conv_fwd_val(jnp.where(live, v, 0.0), wv, bv, 1)
        return _gelu(gc) * vc

    b2 = b.reshape(1, -1)
    return _cols(fn, name, [Cl(up), Cl(up, D_FF), Cl(w), Cl(w, D_FF), Cl(b2), Cl(b2, D_FF)],
                 [(up.shape[0], F32)], D_FF, 128)[0]


def _ffn_conv_act_bwd(dm, up, w, b, t_real, name):
    def fn(dmb, g, v, wg, wv, bg, bv):
        live = _row_ids(g.shape) < t_real
        gm, vm = jnp.where(live, g, 0.0), jnp.where(live, v, 0.0)
        gc = _conv_fwd_val(gm, wg, bg, 1)
        vc = _conv_fwd_val(vm, wv, bv, 1)
        act, dact = _gelu_and_grad(gc)
        dmm = jnp.where(live, dmb, 0.0)
        dgx, dwg, dbg = _conv_bwd_val(dmm * vc * dact, gm, wg, 1, live)
        dvx, dwv, dbv = _conv_bwd_val(dmm * act, vm, wv, 1, live)
        return dgx, dvx, dwg, dwv, dbg, dbv

    b2 = b.reshape(1, -1)
    tp, kk = up.shape[0], w.shape[0]
    return _cols(fn, name, [Cl(dm), Cl(up), Cl(up, D_FF), Cl(w), Cl(w, D_FF), Cl(b2), Cl(b2, D_FF)],
                 [(tp, F32), (tp, F32), (kk, F32), (kk, F32), (1, F32), (1, F32)], D_FF, 128)


def _lru_gates_fwd(xc, wg, b4, lam, t_real, name):
    tp = xc.shape[0]
    tm = _tile(tp, 1408)

    def body(x_ref, w_ref, b_ref, lam_ref, r0_ref, r1_ref, i0_ref, i1_ref, a0_ref, a1_ref, u0_ref, u1_ref):
        x = x_ref[...]
        xb = x.astype(BF16)
        live = _row_ids(x.shape, pl.program_id(1) * tm) < t_real
        bb = b_ref[...]
        sp = _softplus_neg(lam_ref[...])
        gate = [_sigmoid(jnp.dot(xb, w_ref[k], preferred_element_type=F32) + bb[k:k + 1, :]) for k in range(4)]
        for d, (r_ref, i_ref, a_ref, u_ref) in enumerate(((r0_ref, i0_ref, a0_ref, u0_ref), (r1_ref, i1_ref, a1_ref, u1_ref))):
            r, ig = gate[d], gate[2 + d]
            a = jnp.exp(-LRU_C * r * sp[d:d + 1, :])
            r_ref[...] = r
            i_ref[...] = ig
            a_ref[...] = a
            u_ref[...] = jnp.where(live, jnp.sqrt(1.0 - a * a) * (ig * x), 0.0)

    blk = pl.BlockSpec((tm, LANE), lambda g, i: (i, g))
    return pl.pallas_call(
        body, name=name, grid=(LRU_BLOCKS, tp // tm),
        in_specs=[blk, pl.BlockSpec((None, 4, LANE, LANE), lambda g, i: (g, 0, 0, 0)),
                  pl.BlockSpec((4, LANE), lambda g, i: (0, g)), pl.BlockSpec((2, LANE), lambda g, i: (0, g))],
        out_specs=[blk] * 8,
        out_shape=[jax.ShapeDtypeStruct((tp, D_MODEL), F32)] * 8,
        compiler_params=_cparams("parallel", "parallel"),
    )(xc, wg, b4, lam)


def _lru_gates_bwd(l0, l1, da0, da1, r0, r1, i0, i1, a0, a1, xc, wg, lam, t_real, name):
    tp = xc.shape[0]
    tm = _tile(tp, 1408)

    def body(l0_ref, l1_ref, da0_ref, da1_ref, r0_ref, r1_ref, i0_ref, i1_ref, a0_ref, a1_ref, x_ref, w_ref, lam_ref,
             dx_ref, dw_ref, db_ref, dlam_ref):
        i = pl.program_id(1)
        x = x_ref[...]
        xb = x.astype(BF16)
        live = _row_ids(x.shape, i * tm) < t_real
        lam_ = lam_ref[...]
        sp = _softplus_neg(lam_)
        dsp_dlam = -_sigmoid(-lam_)
        dx = jnp.zeros_like(x)
        dpre = [None] * 4
        dlam_rows = []
        for d, (l_ref, da_ref, r_ref, i_ref, a_ref) in enumerate(((l0_ref, da0_ref, r0_ref, i0_ref, a0_ref),
                                                                  (l1_ref, da1_ref, r1_ref, i1_ref, a1_ref))):
            r, ig, a = r_ref[...], i_ref[...], a_ref[...]
            du = jnp.where(live, l_ref[...], 0.0)
            s = jnp.sqrt(1.0 - a * a)
            dv = du * s
            ds = du * (ig * x)
            dla = jnp.where(live, da_ref[...], 0.0) * a - ds * (a * a) / s
            dla = jnp.where(live, dla, 0.0)
            dr = dla * (-LRU_C) * sp[d:d + 1, :]
            dlam_rows.append(jnp.sum(dla * (-LRU_C) * r, axis=0, keepdims=True) * dsp_dlam[d:d + 1, :])
            dpre[d] = dr * r * (1.0 - r)
            dpre[2 + d] = dv * x * ig * (1.0 - ig)
            dx = dx + dv * ig

        @pl.when(i == 0)
        def _():
            dw_ref[...] = jnp.zeros_like(dw_ref)
            db_ref[...] = jnp.zeros_like(db_ref)
            dlam_ref[...] = jnp.zeros_like(dlam_ref)

        for k in range(4):
            pk = dpre[k].astype(BF16)
            dx = dx + lax.dot_general(pk, w_ref[k], (((1,), (1,)), ((), ())), preferred_element_type=F32)
            dw_ref[k] += lax.dot_general(xb, pk, (((0,), (0,)), ((), ())), preferred_element_type=F32)
        db_ref[...] += jnp.concatenate([jnp.sum(p, axis=0, keepdims=True) for p in dpre], axis=0)
        dlam_ref[...] += jnp.concatenate(dlam_rows, axis=0)
        dx_ref[...] = dx

    blk = pl.BlockSpec((tm, LANE), lambda g, i: (i, g))
    return pl.pallas_call(
        body, name=name, grid=(LRU_BLOCKS, tp // tm),
        in_specs=[blk] * 11 + [pl.BlockSpec((None, 4, LANE, LANE), lambda g, i: (g, 0, 0, 0)),
                               pl.BlockSpec((2, LANE), lambda g, i: (0, g))],
        out_specs=[blk, pl.BlockSpec((None, 4, LANE, LANE), lambda g, i: (g, 0, 0, 0)),
                   pl.BlockSpec((4, LANE), lambda g, i: (0, g)), pl.BlockSpec((2, LANE), lambda g, i: (0, g))],
        out_shape=[jax.ShapeDtypeStruct((tp, D_MODEL), F32), jax.ShapeDtypeStruct((LRU_BLOCKS, 4, LANE, LANE), F32),
                   jax.ShapeDtypeStruct((4, D_MODEL), F32), jax.ShapeDtypeStruct((2, D_MODEL), F32)],
        compiler_params=_cparams("parallel", "arbitrary"),
    )(l0, l1, da0, da1, r0, r1, i0, i1, a0, a1, xc, wg, lam)


def _tile_scan(a, u, reverse):
    rows = lax.broadcasted_iota(jnp.int32, a.shape, 0)
    for s in (1, 2, 4):
        if reverse:
            keep = rows < SUBLANE - s
            a_sh, u_sh = pltpu.roll(a, SUBLANE - s, 0), pltpu.roll(u, SUBLANE - s, 0)
        else:
            keep = rows >= s
            a_sh, u_sh = pltpu.roll(a, s, 0), pltpu.roll(u, s, 0)
        u = u + a * jnp.where(keep, u_sh, 0.0)
        a = a * jnp.where(keep, a_sh, 1.0)
    return a, u


def _scan_fwd(a0, u0, a1, u1, name):
    tp, d = a0.shape
    tc = 128
    nt = tp // SUBLANE

    def body(a0_ref, u0_ref, a1_ref, u1_ref, h0_ref, h1_ref):
        def step(t, carry):
            c0, c1 = carry
            f = pl.multiple_of(t * SUBLANE, SUBLANE)
            b = pl.multiple_of((nt - 1 - t) * SUBLANE, SUBLANE)
            pa, pu = _tile_scan(a0_ref[pl.ds(f, SUBLANE), :], u0_ref[pl.ds(f, SUBLANE), :], False)
            h = pu + pa * c0
            h0_ref[pl.ds(f, SUBLANE), :] = h
            c0 = h[SUBLANE - 1:SUBLANE, :]
            pa, pu = _tile_scan(a1_ref[pl.ds(b, SUBLANE), :], u1_ref[pl.ds(b, SUBLANE), :], True)
            h = pu + pa * c1
            h1_ref[pl.ds(b, SUBLANE), :] = h
            c1 = h[0:1, :]
            return c0, c1

        z = jnp.zeros((1, tc), F32)
        lax.fori_loop(0, nt, step, (z, z))

    blk = pl.BlockSpec((tp, tc), lambda j: (0, j))
    return pl.pallas_call(
        body, name=name, grid=(d // tc,), in_specs=[blk] * 4, out_specs=[blk] * 2,
        out_shape=[jax.ShapeDtypeStruct((tp, d), F32)] * 2,
        compiler_params=_cparams("parallel"),
    )(a0, u0, a1, u1)


def _scan_bwd(dh, a0, a1, h0, h1, name):
    tp, d = dh.shape
    tc = 128
    nt = tp // SUBLANE

    def body(dh_ref, a0_ref, a1_ref, h0_ref, h1_ref, l0_ref, l1_ref, da0_ref, da1_ref):
        rows8 = lax.broadcasted_iota(jnp.int32, (SUBLANE, tc), 0)

        def step(t, carry):
            c0, c1 = carry
            b = pl.multiple_of((nt - 1 - t) * SUBLANE, SUBLANE)
            f = pl.multiple_of(t * SUBLANE, SUBLANE)
            a = a0_ref[pl.ds(b, SUBLANE), :]
            a_next = jnp.where(rows8 < SUBLANE - 1, pltpu.roll(a, SUBLANE - 1, 0), 1.0)
            pa, pu = _tile_scan(a_next, dh_ref[pl.ds(b, SUBLANE), :], True)
            lam = pu + pa * c0
            l0_ref[pl.ds(b, SUBLANE), :] = lam
            c0 = a[0:1, :] * lam[0:1, :]
            a = a1_ref[pl.ds(f, SUBLANE), :]
            a_prev = jnp.where(rows8 >= 1, pltpu.roll(a, 1, 0), 1.0)
            pa, pu = _tile_scan(a_prev, dh_ref[pl.ds(f, SUBLANE), :], False)
            lam = pu + pa * c1
            l1_ref[pl.ds(f, SUBLANE), :] = lam
            c1 = a[SUBLANE - 1:SUBLANE, :] * lam[SUBLANE - 1:SUBLANE, :]
            return c0, c1

        z = jnp.zeros((1, tc), F32)
        lax.fori_loop(0, nt, step, (z, z))
        rows = lax.broadcasted_iota(jnp.int32, (tp, tc), 0)
        da0_ref[...] = l0_ref[...] * jnp.where(rows >= 1, pltpu.roll(h0_ref[...], 1, 0), 0.0)
        da1_ref[...] = l1_ref[...] * jnp.where(rows < tp - 1, pltpu.roll(h1_ref[...], tp - 1, 0), 0.0)

    blk = pl.BlockSpec((tp, tc), lambda j: (0, j))
    return pl.pallas_call(
        body, name=name, grid=(d // tc,), in_specs=[blk] * 5, out_specs=[blk] * 4,
        out_shape=[jax.ShapeDtypeStruct((tp, d), F32)] * 4,
        compiler_params=_cparams("parallel"),
    )(dh, a0, a1, h0, h1)


def _gated_h(proj, h0, h1, name):
    def fn(row0, lg, x0, x1):
        return _gelu(lg) * (x0 + x1)

    return _rows(fn, name, [Rw(proj, D_MODEL, C_LRU_G // D_MODEL), Rw(h0), Rw(h1)], [(D_MODEL, F32)])[0]


def _gated_h_bwd(dgh, proj, h0, h1, name):
    def fn(row0, dg, lg, x0, x1):
        act, dact = _gelu_and_grad(lg)
        return dg * (x0 + x1) * dact, dg * act

    return _rows(fn, name, [Rw(dgh), Rw(proj, D_MODEL, C_LRU_G // D_MODEL), Rw(h0), Rw(h1)], [(D_MODEL, F32)] * 2)


def _mix(proj, y_mla, y_lru, name):
    def fn(row0, gm, gl, ym, yl):
        return _sigmoid(gm) * ym + _sigmoid(gl) * yl

    return _rows(fn, name, [Rw(proj, D_MODEL, C_G_MLA // D_MODEL), Rw(proj, D_MODEL, C_G_LRU // D_MODEL), Rw(y_mla), Rw(y_lru)],
                 [(D_MODEL, F32)])[0]


def _mix_bwd(dz, proj, y_mla, y_lru, name):
    def fn(row0, dzb, gm, gl, ym, yl):
        sm, sl = _sigmoid(gm), _sigmoid(gl)
        return dzb * sm, dzb * sl, dzb * ym * sm * (1.0 - sm), dzb * yl * sl * (1.0 - sl)

    return _rows(fn, name, [Rw(dz), Rw(proj, D_MODEL, C_G_MLA // D_MODEL), Rw(proj, D_MODEL, C_G_LRU // D_MODEL),
                            Rw(y_mla), Rw(y_lru)], [(D_MODEL, F32)] * 4)


def _layer_fwd(h, w_in, more_weights, cs, t_real, tag):
    proj = _matmul(h, w_in, tag + "proj")
    w = dict(more_weights(0, proj), w_in=w_in)
    cqn, ckvn = _mla_norms(proj, w['q_norm'], w['kv_norm'], tag + "mla_norms")
    qext = _matmul(cqn, w['w_q'], tag + "q_up")
    kv = _matmul(ckvn, w['w_kv'], tag + "kv_up")
    qc, kc, vb = _mla_pack(qext, kv, proj, cs, tag + "mla_pack")
    o, lse = _attn_fwd(qc, kc, vb, t_real, tag + "attn_fwd")
    w.update(more_weights(1, o))
    y_mla = _matmul(o, w['w_o_mla'], tag + "o_mla")
    xc = _lru_conv_fwd(proj, w['lru_conv_w'], w['lru_conv_b'], t_real, tag + "lru_conv")
    r0, r1, i0, i1, a0, a1, u0, u1 = _lru_gates_fwd(xc, w['w_g'], w['b4'], w['lru_lambda'], t_real, tag + "lru_gates")
    h0, h1 = _scan_fwd(a0, u0, a1, u1, tag + "lru_scan")
    gh = _gated_h(proj, h0, h1, tag + "lru_gate_out")
    y_lru = _matmul(gh, w['w_o_lru'], tag + "o_lru")
    z = _mix(proj, y_mla, y_lru, tag + "mix")
    zo = _matmul(z, w['w_out'], tag + "w_out")
    hm = _ln_fwd([(DN_ALPHA, h), (1.0, zo)], w['ln1_g'], w['ln1_b'], tag + "ln1")
    w.update(more_weights(2, hm))
    up = _matmul(hm, w['w_up'], tag + "w_up")
    m = _ffn_conv_act(up, w['ffn_conv_w'], w['ffn_conv_b'], t_real, tag + "ffn_conv")
    f = _matmul(m, w['w_down'], tag + "w_down", tk_cap=1408)
    out = _ln_fwd([(DN_ALPHA, hm), (1.0, f)], w['ln2_g'], w['ln2_b'], tag + "ln2")
    saved = dict(w=w, h=h, proj=proj, cqn=cqn, ckvn=ckvn, qc=qc, kc=kc, vb=vb, o=o, lse=lse, y_mla=y_mla, xc=xc,
                 r0=r0, r1=r1, i0=i0, i1=i1, a0=a0, a1=a1, h0=h0, h1=h1, gh=gh, y_lru=y_lru, z=z, zo=zo, hm=hm,
                 up=up, m=m, f=f)
    return out, saved


def _after(a, tok):
    return a if tok is None else a + tok.astype(a.dtype)


def _layer_bwd(dout_terms, s, cs, t_real, tag, emit, tok):
    w = s['w']
    g = {}
    du2, dg2, db2 = _ln_bwd(dout_terms, [(DN_ALPHA, s['hm']), (1.0, s['f'])], _after(w['ln2_g'], tok), tag + "ln2_bwd")
    g['ln2_g'], g['ln2_b'] = dg2, db2
    dm = _matmul(du2, w['w_down'], tag + "w_down_dx", tb=True)
    g['w_down'] = _matmul(s['m'], du2, tag + "w_down_dw", ta=True, tk_cap=1408)
    dgp, dvp, dwg_, dwv_, dbg_, dbv_ = _ffn_conv_act_bwd(dm, s['up'], w['ffn_conv_w'], w['ffn_conv_b'], t_real, tag + "ffn_conv_bwd")
    g['ffn_conv_w'] = jnp.concatenate([dwg_, dwv_], axis=1)
    g['ffn_conv_b'] = jnp.concatenate([dbg_, dbv_], axis=1)
    dup = jnp.concatenate([dgp, dvp], axis=1)
    dhm_mm = _matmul(dup, w['w_up'], tag + "w_up_dx", tb=True, tk_cap=1408)
    g['w_up'] = _matmul(s['hm'], dup, tag + "w_up_dw", ta=True, tk_cap=1408)
    tok = emit('ffn', g)
    g = {}
    du1, dg1, db1 = _ln_bwd([(DN_ALPHA, du2), (1.0, dhm_mm)], [(DN_ALPHA, s['h']), (1.0, s['zo'])],
                            _after(w['ln1_g'], tok), tag + "ln1_bwd")
    g['ln1_g'], g['ln1_b'] = dg1, db1
    dz = _matmul(du1, w['w_out'], tag + "w_out_dx", tb=True)
    g['w_out'] = _matmul(s['z'], du1, tag + "w_out_dw", ta=True, tk_cap=1408)
    dy_mla, dy_lru, dg_mla, dg_lru = _mix_bwd(dz, s['proj'], s['y_mla'], s['y_lru'], tag + "mix_bwd")
    do = _matmul(dy_mla, w['w_o_mla'], tag + "o_mla_dx", tb=True)
    g['w_o_mla'] = _matmul(s['o'], dy_mla, tag + "o_mla_dw", ta=True, tk_cap=1408)
    dqc, dkc, dv = _attn_bwd(s['qc'], s['kc'], s['vb'], do, s['o'], s['lse'], t_real, tag + "attn_bwd")
    dqext, dkn, dkrp = _mla_unpack(dqc, dkc, cs, tag + "mla_unpack")
    dkv = jnp.concatenate([dkn, dv], axis=1)
    dcqn = _matmul(dqext, w['w_q'], tag + "q_up_dx", tb=True)
    g['w_q'] = _matmul(s['cqn'], dqext, tag + "q_up_dw", ta=True, tk_cap=1408)
    dckvn = _matmul(dkv, w['w_kv'], tag + "kv_up_dx", tb=True)
    g['w_kv'] = _matmul(s['ckvn'], dkv, tag + "kv_up_dw", ta=True, tk_cap=1408)
    dcq, dckv, g['q_norm'], g['kv_norm'] = _mla_norms_bwd(dcqn, dckvn, s['proj'], w['q_norm'], w['kv_norm'], tag + "mla_norms_bwd")
    dgh = _matmul(dy_lru, w['w_o_lru'], tag + "o_lru_dx", tb=True)
    g['w_o_lru'] = _matmul(s['gh'], dy_lru, tag + "o_lru_dw", ta=True, tk_cap=1408)
    dlru_g, dhs = _gated_h_bwd(dgh, s['proj'], s['h0'], s['h1'], tag + "lru_gate_out_bwd")
    l0, l1, da0, da1 = _scan_bwd(dhs, s['a0'], s['a1'], s['h0'], s['h1'], tag + "lru_scan_bwd")
    dxc, g['w_g'], g['b4'], g['lru_lambda'] = _lru_gates_bwd(
        l0, l1, da0, da1, s['r0'], s['r1'], s['i0'], s['i1'], s['a0'], s['a1'], s['xc'], w['w_g'], w['lru_lambda'],
        t_real, tag + "lru_gates_bwd")
    dlru_x, g['lru_conv_w'], g['lru_conv_b'] = _lru_conv_bwd(dxc, s['proj'], w['lru_conv_w'], t_real, tag + "lru_conv_bwd")
    tok = emit('mid', g)
    dproj = jnp.concatenate([dlru_g, dlru_x, dg_mla, dg_lru, dcq, dckv, _after(dkrp, tok)], axis=1)
    tok = emit('in', {'w_in': _matmul(s['h'], dproj, tag + "proj_dw", ta=True, tk_cap=1408)})
    dh_mm = _matmul(dproj, w['w_in'], tag + "proj_dx", tb=True, tk_cap=1536)
    return [(DN_ALPHA, du1), (1.0, dh_mm)], tok


def _swap_halves(a, axis=-1):
    h1, h2 = jnp.split(a, 2, axis=axis)
    return jnp.concatenate([h2, h1], axis=axis)


def _w_in_kernel(w_in):
    cq, ckv, kr, lg, lx, gm, gl = jnp.split(w_in, [256, 384, 448, 1472, 2496, 3520], axis=1)
    return jnp.concatenate([lg, lx, gm, gl, cq, ckv, kr, _swap_halves(kr)], axis=1)


def _layer_weights(fl):
    w = {}
    if 'w_uq' in fl:
        uq = fl['w_uq']
        w['w_q'] = jnp.concatenate([uq, _swap_halves(uq[..., QK_NOPE:])], axis=-1).reshape(Q_RANK, HEADS * 2 * LANE)
        w['w_kv'] = jnp.concatenate([fl['w_uk'].reshape(KV_RANK, -1), fl['w_uv'].reshape(KV_RANK, -1)], axis=1).astype(BF16)
        w['w_g'] = jnp.moveaxis(jnp.concatenate([fl['w_rg'], fl['w_ig']], axis=0), 0, 1).astype(BF16)
        w['b4'] = jnp.concatenate([fl['b_rg'], fl['b_ig']], axis=0)
    for n in ('q_norm', 'kv_norm', 'w_o_mla', 'lru_conv_w', 'lru_conv_b', 'lru_lambda', 'w_o_lru', 'w_out', 'ln1_g',
              'ln1_b', 'w_up', 'ffn_conv_w', 'ffn_conv_b', 'w_down', 'ln2_g', 'ln2_b'):
        if n in fl:
            w[n] = fl[n]
    return w


def _layer_grads(g):
    out = {}
    if 'w_in' in g:
        lg, lx, gm, gl, cq, ckv, kr, krs = jnp.split(g['w_in'], [1024, 2048, 3072, 4096, 4352, 4480, 4544], axis=1)
        out['w_in'] = jnp.concatenate([cq, ckv, kr + _swap_halves(krs), lg, lx, gm, gl], axis=1)
    if 'w_q' in g:
        gq = g['w_q'].reshape(Q_RANK, HEADS, 2 * LANE)
        out['w_uq'] = jnp.concatenate([gq[..., :QK_NOPE], gq[..., QK_NOPE:QK_NOPE + QK_ROPE] + _swap_halves(gq[..., QK_NOPE + QK_ROPE:])], axis=-1)
    if 'w_kv' in g:
        out['w_uk'] = g['w_kv'][:, :HEADS * QK_NOPE].reshape(KV_RANK, HEADS, QK_NOPE)
        out['w_uv'] = g['w_kv'][:, HEADS * QK_NOPE:].reshape(KV_RANK, HEADS, V_HEAD)
    if 'w_g' in g:
        gg = jnp.moveaxis(g['w_g'], 1, 0)
        out['w_rg'], out['w_ig'] = gg[:2], gg[2:]
    if 'b4' in g:
        out['b_rg'], out['b_ig'] = g['b4'][:2], g['b4'][2:]
    for n in ('q_norm', 'kv_norm', 'lru_conv_b', 'ln1_g', 'ln1_b', 'ffn_conv_b', 'ln2_g', 'ln2_b'):
        if n in g:
            out[n] = g[n].reshape(-1)
    for n in ('w_o_mla', 'lru_conv_w', 'lru_lambda', 'w_o_lru', 'w_out', 'w_up', 'ffn_conv_w', 'w_down'):
        if n in g:
            out[n] = g[n]
    return out


def _rope_table(tp):
    half = QK_ROPE // 2
    inv_freq = jnp.exp(-math.log(ROPE_THETA) * jnp.arange(half, dtype=F32) / half)
    ang = jnp.arange(tp, dtype=F32)[:, None] * inv_freq[None, :]
    c, s = jnp.cos(ang), jnp.sin(ang)
    return jnp.concatenate([c, c, -s, s], axis=1)


def _local_step(x, target, meta, ln0_g, ln0_b, layer_w, t_pad, emit):
    seq = x.shape[0]
    t_real = N_META + seq
    zpad = jnp.zeros((t_pad - t_real, D_MODEL), F32)
    xin = jnp.concatenate([meta, x, zpad], axis=0)
    tgt = jnp.concatenate([jnp.zeros((N_META, D_MODEL), F32), target, zpad], axis=0)
    cs = _rope_table(t_pad)
    h = _ln_fwd([(1.0, xin)], ln0_g, ln0_b, "ln0")
    saved = []
    for l in range(DEPTH):
        w_in, rest_of_weights = layer_w[l](h)
        h, s = _layer_fwd(h, w_in, rest_of_weights, cs, t_real, "l%d_" % l)
        saved.append(s)
    dy, lossvec = _loss_head(h, tgt, t_real, "loss_head")
    loss = jnp.sum(lossvec)
    terms, tok = [(1.0, dy)], None
    for l in reversed(range(DEPTH)):
        terms, tok = _layer_bwd(terms, saved[l], cs, t_real, "l%d_" % l,
                                functools.partial(lambda stage, g, l: emit(l, stage, _layer_grads(g)), l=l), tok)
    dxin, dg0, db0 = _ln_bwd(terms, [(1.0, xin)], _after(ln0_g, tok), "ln0_bwd")
    emit(None, 'head', {'meta_tokens': dxin[:N_META], 'ln0_g': dg0.reshape(-1), 'ln0_b': db0.reshape(-1)})
    return loss, dxin[N_META:t_real]


_HBM = pl.BlockSpec(memory_space=pltpu.HBM)
_SEM = pl.BlockSpec(memory_space=pltpu.SEMAPHORE)
_SIDE_EFFECT = pltpu.SideEffectType.DATAFLOW_SIDE_EFFECTING


def _peer_copies(src_refs, land_refs, scatters, send_sems, recv_sems):
    x, y, c = lax.axis_index("x"), lax.axis_index("y"), lax.axis_index("c")
    me = 4 * x + 2 * y + c
    copies = []
    for k in range(1, N_DEV):
        px = 1 - x if k & 4 else x
        py = 1 - y if k & 2 else y
        pc = 1 - c if k & 1 else c
        for t, (src, land) in enumerate(zip(src_refs, land_refs)):
            copies.append(pltpu.make_async_remote_copy(
                src_ref=src.at[4 * px + 2 * py + pc] if scatters[t] else src, dst_ref=land.at[me],
                send_sem=send_sems.at[7 * t + k - 1], recv_sem=recv_sems.at[7 * t + k - 1],
                device_id=(px, py, pc), device_id_type=pl.DeviceIdType.MESH))
    return me, copies


def _exchange_start(groups, name):
    flat = [it for grp in groups for it in grp]
    nt, ng = len(flat), len(groups)
    scatters = [sc for _, sc in flat]
    srcs = [pltpu.with_memory_space_constraint(a, pltpu.HBM) for a, _ in flat]
    land_shapes = [a.shape if sc else (N_DEV,) + a.shape for a, sc in flat]
    lands = [pltpu.with_memory_space_constraint(lax.empty(s, a.dtype), pltpu.HBM) for s, (a, _) in zip(land_shapes, flat)]
    bounds = [0]
    for grp in groups:
        bounds.append(bounds[-1] + len(grp))

    def body(*refs):
        src_refs, land_refs = refs[:nt], refs[nt:2 * nt]
        sem_refs = refs[2 * nt:2 * nt + 2 * ng]
        token_ref = refs[4 * nt + 2 * ng]
        for gi in range(ng):
            lo, hi = bounds[gi], bounds[gi + 1]
            _, copies = _peer_copies(src_refs[lo:hi], land_refs[lo:hi], scatters[lo:hi], sem_refs[2 * gi], sem_refs[2 * gi + 1])
            for cp in copies:
                cp.start()
        token_ref[...] = jnp.zeros_like(token_ref)

    out_shape = []
    for grp in groups:
        out_shape += [pltpu.SemaphoreType.DMA((7 * len(grp),)), pltpu.SemaphoreType.DMA((7 * len(grp),))]
    out_shape += [pltpu.HBM(a.shape, a.dtype) for a in srcs] + [pltpu.HBM(s, a.dtype) for s, a in zip(land_shapes, srcs)]
    out_shape += [jax.ShapeDtypeStruct((SUBLANE, LANE), F32)]
    res = pl.pallas_call(
        body, name=name, out_shape=out_shape,
        in_specs=[_HBM] * (2 * nt),
        out_specs=[_SEM] * (2 * ng) + [_HBM] * (2 * nt) + [pl.BlockSpec(memory_space=pltpu.VMEM)],
        input_output_aliases={t: 2 * ng + t for t in range(2 * nt)},
        compiler_params=pltpu.CompilerParams(has_side_effects=_SIDE_EFFECT),
    )(*srcs, *lands)
    sems, thru, token = res[:2 * ng], res[2 * ng:2 * ng + 2 * nt], res[-1]
    states = []
    for gi in range(ng):
        lo, hi = bounds[gi], bounds[gi + 1]
        states.append((sems[2 * gi], sems[2 * gi + 1], thru[lo:hi], thru[nt + lo:nt + hi], scatters[lo:hi]))
    return states, token[0, 0]


def _exchange_wait(state, after, name):
    send_sems, recv_sems, srcs, lands, scatters = state
    n = len(srcs)

    def body(*refs):
        _, copies = _peer_copies(refs[:n], refs[n:2 * n], scatters, refs[2 * n], refs[2 * n + 1])
        for cp in copies:
            cp.wait_send()
        for cp in copies:
            cp.wait_recv()

    res = pl.pallas_call(
        body, name=name,
        out_shape=[pltpu.HBM(a.shape, a.dtype) for a in srcs] + [pltpu.HBM(a.shape, a.dtype) for a in lands],
        in_specs=[_HBM] * (2 * n) + [_SEM, _SEM, _HBM],
        out_specs=[_HBM] * (2 * n),
        input_output_aliases={t: t for t in range(2 * n)},
        compiler_params=pltpu.CompilerParams(has_side_effects=_SIDE_EFFECT),
    )(*srcs, *lands, send_sems, recv_sems, pltpu.with_memory_space_constraint(after, pltpu.HBM))
    me = 4 * lax.axis_index("x") + 2 * lax.axis_index("y") + lax.axis_index("c")
    out = []
    for src, land, sc in zip(res[:n], res[n:], scatters):
        own = lax.dynamic_index_in_dim(src, me, 0, keepdims=True) if sc else src[None]
        out.append(lax.dynamic_update_slice_in_dim(land, own, me, 0))
    return out


def _as_rows(shape):
    return (1, shape[0]) if len(shape) == 1 else (math.prod(shape[:-1]), shape[-1])


def _sum_adamw(pieces, w, m, v, name):
    shape = w.shape
    nl = len(pieces)
    if nl > 1 and _as_rows(shape[1:])[0] % 16:
        pieces, nl = [jnp.stack(pieces, axis=1)], 1
    rows, cols = _as_rows(shape)
    rl = rows // nl
    cap = max(16, (1 << 18) // cols // 16 * 16)
    tr = _tile(rl, cap, 16)
    nb = rl // tr
    c1 = 1.0 / (1.0 - ADAM_B1 ** ADAM_STEP)
    c2 = 1.0 / (1.0 - ADAM_B2 ** ADAM_STEP)

    def body(*refs):
        p_refs = refs[:nl]
        w_ref, m_ref, v_ref, g_ref, d_ref, nm_ref, nv_ref = refs[nl:]
        li = pl.program_id(0)

        def total(p_ref):
            acc = p_ref[0].astype(F32)
            for k in range(1, N_DEV):
                acc = acc + p_ref[k].astype(F32)
            return acc

        gg = total(p_refs[0])
        for l in range(1, nl):
            gg = jnp.where(li == l, total(p_refs[l]), gg)
        nm = ADAM_B1 * m_ref[...] + (1.0 - ADAM_B1) * gg
        nv = ADAM_B2 * v_ref[...] + (1.0 - ADAM_B2) * (gg * gg)
        g_ref[...] = gg
        d_ref[...] = -ADAM_LR * ((nm * c1) / (jnp.sqrt(nv * c2) + ADAM_EPS) + ADAM_WD * w_ref[...])
        nm_ref[...] = nm
        nv_ref[...] = nv

    blk = pl.BlockSpec((tr, cols), lambda li, i: (li * nb + i, 0))
    p_specs = [pl.BlockSpec((N_DEV, tr, cols), functools.partial(lambda li, i, l: (0, jnp.where(li == l, i, 0), 0), l=l))
               for l in range(nl)]
    res = pl.pallas_call(
        body, name=name, grid=(nl, nb),
        in_specs=p_specs + [blk] * 3, out_specs=[blk] * 4,
        out_shape=[jax.ShapeDtypeStruct((rows, cols), F32)] * 4,
        compiler_params=_cparams("parallel", "parallel"),
    )(*[p.reshape(N_DEV, rl, cols) for p in pieces], *[a.reshape(rows, cols) for a in (w, m, v)])
    return [r.reshape(shape) for r in res]


def _to_shards(full, axis):
    shp = full.shape
    a = full.reshape(shp[:axis] + (N_DEV, shp[axis] // N_DEV) + shp[axis + 1:])
    return jnp.moveaxis(a, axis, 0)


def _from_shards(blocks, axis):
    a = jnp.moveaxis(blocks, 0, axis)
    shp = a.shape
    return a.reshape(shp[:axis] + (shp[axis] * shp[axis + 1],) + shp[axis + 2:])


def kernel(x, meta_tokens, ln0_g, ln0_b, w_in, q_norm, kv_norm, w_uq, w_uk, w_uv, w_o_mla, lru_conv_w, lru_conv_b, w_rg, b_rg, w_ig, b_ig, lru_lambda, w_o_lru, w_out, ln1_g, ln1_b, w_up, ffn_conv_w, ffn_conv_b, w_down, ln2_g, ln2_b, loss_target, m_meta_tokens, m_ln0_g, m_ln0_b, m_w_in, m_q_norm, m_kv_norm, m_w_uq, m_w_uk, m_w_uv, m_w_o_mla, m_lru_conv_w, m_lru_conv_b, m_w_rg, m_b_rg, m_w_ig, m_b_ig, m_lru_lambda, m_w_o_lru, m_w_out, m_ln1_g, m_ln1_b, m_w_up, m_ffn_conv_w, m_ffn_conv_b, m_w_down, m_ln2_g, m_ln2_b, v_meta_tokens, v_ln0_g, v_ln0_b, v_w_in, v_q_norm, v_kv_norm, v_w_uq, v_w_uk, v_w_uv, v_w_o_mla, v_lru_conv_w, v_lru_conv_b, v_w_rg, v_b_rg, v_w_ig, v_b_ig, v_lru_lambda, v_w_o_lru, v_w_out, v_ln1_g, v_ln1_b, v_w_up, v_ffn_conv_w, v_ffn_conv_b, v_w_down, v_ln2_g, v_ln2_b):
    args = (meta_tokens, ln0_g, ln0_b, w_in, q_norm, kv_norm, w_uq, w_uk, w_uv, w_o_mla, lru_conv_w, lru_conv_b, w_rg, b_rg, w_ig, b_ig, lru_lambda, w_o_lru, w_out, ln1_g, ln1_b, w_up, ffn_conv_w, ffn_conv_b, w_down, ln2_g, ln2_b)
    ms = (m_meta_tokens, m_ln0_g, m_ln0_b, m_w_in, m_q_norm, m_kv_norm, m_w_uq, m_w_uk, m_w_uv, m_w_o_mla, m_lru_conv_w, m_lru_conv_b, m_w_rg, m_b_rg, m_w_ig, m_b_ig, m_lru_lambda, m_w_o_lru, m_w_out, m_ln1_g, m_ln1_b, m_w_up, m_ffn_conv_w, m_ffn_conv_b, m_w_down, m_ln2_g, m_ln2_b)
    vs = (v_meta_tokens, v_ln0_g, v_ln0_b, v_w_in, v_q_norm, v_kv_norm, v_w_uq, v_w_uk, v_w_uv, v_w_o_mla, v_lru_conv_w, v_lru_conv_b, v_w_rg, v_b_rg, v_w_ig, v_b_ig, v_lru_lambda, v_w_o_lru, v_w_out, v_ln1_g, v_ln1_b, v_w_up, v_ffn_conv_w, v_ffn_conv_b, v_w_down, v_ln2_g, v_ln2_b)
    wd, md, vd = dict(zip(WEIGHTS, args)), dict(zip(WEIGHTS, ms)), dict(zip(WEIGHTS, vs))

    def shard_axis(n, l):
        return SHARD_AXIS[n] - (0 if l is None else 1)

    def shard(n, l):
        a = wd[n] if l is None else wd[n][l]
        return a.astype(BF16) if n in BIG else a

    first = [('meta_tokens', None), ('w_in', 0)]
    staged = [[(n, 0) for n in names] for names in STAGE_WEIGHTS]
    later = [(n, 1) for n in SHARDED if n != 'meta_tokens']
    gather, token = _exchange_start([[(shard(*k), False) for k in keys] for keys in [first] + staged + [later]], "gather_start")

    def arrive(gi, keys, after, name):
        return {k: _from_shards(b, shard_axis(*k)) for k, b in zip(keys, _exchange_wait(gather[gi], after, name))}

    def layer_weights(got, l, names):
        fl = {n: wd[n][l] for n in names if n in REPLICATED}
        fl.update({n: a for (n, _), a in got.items() if n in names})
        return _layer_weights(fl)

    ln0_g = _after(wd['ln0_g'], token)
    got_first = arrive(0, first, ln0_g, "gather_wait_first")

    def first_layer(h):
        def more(stage, after):
            got = arrive(1 + stage, staged[stage], after, "gather_wait_l0_%d" % stage)
            return layer_weights(got, 0, STAGE_WEIGHTS[stage] + STAGE_REPLICATED[stage])
        return _w_in_kernel(got_first['w_in', 0]), more

    def second_layer(h):
        got = arrive(1 + len(staged), later, h, "gather_wait_l1")
        return _w_in_kernel(got['w_in', 1]), lambda stage, after: layer_weights(got, 1, STAGE_WEIGHTS[stage] + STAGE_REPLICATED[stage])

    sent = []
    pending = []

    def send(l, stage, grads):
        for n, g in grads.items():
            if n in SHARD_AXIS:
                g = _to_shards(g, shard_axis(n, l))
                pending.append(((n, l), (g.astype(BF16) if n in BIG else g, True)))
            else:
                pending.append(((n, l), (g.astype(BF16) if n in LARGE_REPLICATED else g, False)))
        if l == DEPTH - 1 and stage != 'in':
            return None
        (state,), tok = _exchange_start([[it for _, it in pending]], "grads_start_%s_%s" % (l, stage))
        sent.append(([k for k, _ in pending], state))
        pending.clear()
        return tok

    seq = x.shape[1]
    t_pad = -(-(N_META + seq + MIN_PAD_ROWS) // LANE) * LANE
    loss, grad_x = _local_step(x[0], loss_target[0], got_first['meta_tokens', None], ln0_g, wd['ln0_b'],
                               [first_layer, second_layer], t_pad, send)
    loss = lax.psum(loss, ("x", "y", "c"))

    pieces = {}
    for gi, (keys, state) in enumerate(sent):
        pieces.update(zip(keys, _exchange_wait(state, grad_x, "grads_wait_%d" % gi)))
    outs = {}
    for n in WEIGHTS:
        ps = [pieces[n, None]] if (n, None) in pieces else [pieces[n, l] for l in range(DEPTH)]
        outs[n] = _sum_adamw(ps, wd[n], md[n], vd[n], "adamw_" + n)
    res = [loss, grad_x[None]]
    for k in range(4):
        res += [outs[n][k] for n in WEIGHTS]
    return tuple(res)
```

```python
import functools
import math

import jax
import jax.numpy as jnp
from jax import lax
from jax.experimental import pallas as pl
from jax.experimental.pallas import tpu as pltpu

F32 = jnp.float32
BF16 = jnp.bfloat16

N_DEV = 8
D_MODEL = 1024
N_META = 16
HEADS = 8
QK_NOPE = 128
QK_ROPE = 64
V_HEAD = 128
Q_RANK = 256
KV_RANK = 128
ROPE_THETA = 10000.0
LRU_BLOCKS = 8
LRU_C = 8.0
D_FF = 2816
DEPTH = 2
DN_ALPHA = (2.0 * DEPTH) ** 0.25
LN_EPS = 1e-5
RMS_EPS = 1e-6
LN2 = math.log(2.0)
ATT_SCALE = 1.0 / math.sqrt(QK_NOPE + QK_ROPE) / LN2
NEG_BIG = -1e30

ADAM_LR = 0.001
ADAM_B1 = 0.9
ADAM_B2 = 0.999
ADAM_EPS = 1e-08
ADAM_WD = 0.01
ADAM_STEP = 10

MIN_PAD_ROWS = 2
LANE = 128
SUBLANE = 8
VMEM_LIMIT = 56 * 1024 * 1024

PROJ_COLS = 4 * D_MODEL + Q_RANK + KV_RANK + 2 * QK_ROPE
C_LRU_G, C_LRU_X, C_G_MLA, C_G_LRU = 0, D_MODEL, 2 * D_MODEL, 3 * D_MODEL
C_CQ = 4 * D_MODEL
C_CKV = C_CQ + Q_RANK
C_KRP = C_CKV + KV_RANK

WEIGHTS = ['meta_tokens', 'ln0_g', 'ln0_b', 'w_in', 'q_norm', 'kv_norm', 'w_uq', 'w_uk', 'w_uv', 'w_o_mla',
           'lru_conv_w', 'lru_conv_b', 'w_rg', 'b_rg', 'w_ig', 'b_ig', 'lru_lambda', 'w_o_lru', 'w_out',
           'ln1_g', 'ln1_b', 'w_up', 'ffn_conv_w', 'ffn_conv_b', 'w_down', 'ln2_g', 'ln2_b']
SHARD_AXIS = {'meta_tokens': 1, 'w_in': 2, 'w_uq': 1, 'w_o_mla': 1, 'lru_conv_w': 2, 'b_rg': 2, 'b_ig': 2,
              'lru_lambda': 2, 'w_o_lru': 1, 'w_out': 1, 'w_up': 2, 'ffn_conv_w': 2, 'w_down': 1}
BIG = ['w_in', 'w_uq', 'w_o_mla', 'w_o_lru', 'w_out', 'w_up', 'w_down']
SHARDED = [n for n in WEIGHTS if n in SHARD_AXIS]
REPLICATED = [n for n in WEIGHTS if n not in SHARD_AXIS]
LARGE_REPLICATED = ['w_uk', 'w_uv', 'w_rg', 'w_ig']
STAGE_WEIGHTS = [['w_uq', 'lru_conv_w', 'b_rg', 'b_ig', 'lru_lambda'], ['w_o_mla', 'w_o_lru', 'w_out'], ['w_up', 'ffn_conv_w', 'w_down']]
STAGE_REPLICATED = [['q_norm', 'kv_norm', 'w_uk', 'w_uv', 'lru_conv_b', 'w_rg', 'w_ig'], ['ln1_g', 'ln1_b'], ['ffn_conv_b', 'ln2_g', 'ln2_b']]


def _cparams(*sem):
    return pltpu.CompilerParams(dimension_semantics=sem, vmem_limit_bytes=VMEM_LIMIT)


def _tile(n, cap, unit=LANE):
    best = None
    t = unit
    while t <= min(n, cap):
        if n % t == 0:
            best = t
        t += unit
    return n if best is None else best


def _sigmoid(x):
    return 1.0 / (1.0 + jnp.exp(-x))


_GELU_C = math.sqrt(2.0 / math.pi)


def _gelu(x):
    t = jnp.tanh(_GELU_C * (x + 0.044715 * x * x * x))
    return 0.5 * x * (1.0 + t)


def _gelu_and_grad(x):
    t = jnp.tanh(_GELU_C * (x + 0.044715 * x * x * x))
    g = 0.5 * x * (1.0 + t)
    dg = 0.5 * (1.0 + t) + 0.5 * x * (1.0 - t * t) * _GELU_C * (1.0 + 3.0 * 0.044715 * x * x)
    return g, dg


def _softplus_neg(lam):
    z = jnp.exp(-jnp.abs(lam))
    w = 1.0 + z
    log1p = jnp.where(w == 1.0, z, jnp.log(w) * z / (w - 1.0))
    return jnp.maximum(-lam, 0.0) + log1p


def _row_ids(shape, row0=0):
    return lax.broadcasted_iota(jnp.int32, shape, 0) + row0


def _matmul(a, b, name, ta=False, tb=False, out_dtype=F32, tm_cap=1408, tn_cap=1024, tk_cap=2048):
    if ta:
        kdim, m = a.shape
    else:
        m, kdim = a.shape
    if tb:
        n, k2 = b.shape
    else:
        k2, n = b.shape
    assert kdim == k2, (a.shape, b.shape, ta, tb)
    tm, tn, tk = _tile(m, tm_cap), _tile(n, tn_cap), _tile(kdim, tk_cap)
    nk = kdim // tk

    def body(a_ref, b_ref, o_ref, *acc):
        dn = (((0 if ta else 1,), (1 if tb else 0,)), ((), ()))
        part = lax.dot_general(a_ref[...].astype(BF16), b_ref[...].astype(BF16), dn, preferred_element_type=F32)
        if nk == 1:
            o_ref[...] = part.astype(o_ref.dtype)
            return
        acc_ref, k = acc[0], pl.program_id(2)

        @pl.when(k == 0)
        def _():
            acc_ref[...] = part

        @pl.when(k > 0)
        def _():
            acc_ref[...] += part

        @pl.when(k == nk - 1)
        def _():
            o_ref[...] = acc_ref[...].astype(o_ref.dtype)

    a_spec = pl.BlockSpec((tk, tm), lambda i, j, k: (k, i)) if ta else pl.BlockSpec((tm, tk), lambda i, j, k: (i, k))
    b_spec = pl.BlockSpec((tn, tk), lambda i, j, k: (j, k)) if tb else pl.BlockSpec((tk, tn), lambda i, j, k: (k, j))
    return pl.pallas_call(
        body, name=name,
        grid=(m // tm, n // tn, nk),
        in_specs=[a_spec, b_spec],
        out_specs=pl.BlockSpec((tm, tn), lambda i, j, k: (i, j)),
        out_shape=jax.ShapeDtypeStruct((m, n), out_dtype),
        scratch_shapes=[pltpu.VMEM((tm, tn), F32)] if nk > 1 else [],
        compiler_params=_cparams("parallel", "parallel", "arbitrary"),
    )(a, b)


class Rw:
    def __init__(self, arr, width=None, cb=0):
        self.arr, self.width, self.cb = arr, (arr.shape[1] if width is None else width), cb


class Pm:
    def __init__(self, arr):
        self.arr = arr


def _rows(fn, name, ins, outs, accs=(), tm_cap=384):
    tp = next(o.arr.shape[0] for o in ins if isinstance(o, Rw))
    tm = _tile(tp, tm_cap)
    n_in, n_out, n_acc = len(ins), len(outs), len(accs)

    def body(*refs):
        i = pl.program_id(0)
        res = fn(i * tm, *[r[...] for r in refs[:n_in]])
        if not isinstance(res, (tuple, list)):
            res = (res,)
        assert len(res) == n_out + n_acc, (name, len(res))
        for k in range(n_out):
            refs[n_in + k][...] = res[k].astype(refs[n_in + k].dtype)
        for k in range(n_acc):
            ref = refs[n_in + n_out + k]

            @pl.when(i == 0)
            def _():
                ref[...] = jnp.zeros_like(ref)

            ref[...] += res[n_out + k]

    in_specs = []
    for o in ins:
        if isinstance(o, Rw):
            in_specs.append(pl.BlockSpec((tm, o.width), functools.partial(lambda i, cb: (i, cb), cb=o.cb)))
        else:
            in_specs.append(pl.BlockSpec(o.arr.shape, functools.partial(lambda i, nd: (0,) * nd, nd=o.arr.ndim)))
    out_specs = [pl.BlockSpec((tm, w), lambda i: (i, 0)) for (w, _) in outs]
    out_specs += [pl.BlockSpec(s, functools.partial(lambda i, nd: (0,) * nd, nd=len(s))) for s in accs]
    out_shape = [jax.ShapeDtypeStruct((tp, w), dt) for (w, dt) in outs]
    out_shape += [jax.ShapeDtypeStruct(s, F32) for s in accs]
    res = pl.pallas_call(
        body, name=name, grid=(tp // tm,), in_specs=in_specs, out_specs=out_specs, out_shape=out_shape,
        compiler_params=_cparams("arbitrary"),
    )(*[o.arr for o in ins])
    return res


class Cl:
    def __init__(self, arr, col0=0, width=None):
        self.arr, self.col0, self.width = arr, col0, width


def _cols(fn, name, ins, outs, ncols, tc):
    assert ncols % tc == 0
    n_in, n_out = len(ins), len(outs)
    outs = [(o[0], o[1], o[2] if len(o) > 2 else tc) for o in outs]

    def body(*refs):
        res = fn(*[r[...] for r in refs[:n_in]])
        if not isinstance(res, (tuple, list)):
            res = (res,)
        assert len(res) == n_out, (name, len(res))
        for k in range(n_out):
            refs[n_in + k][...] = res[k].astype(refs[n_in + k].dtype)

    in_specs = []
    for o in ins:
        wd = tc if o.width is None else o.width
        assert o.col0 % wd == 0, (name, o.col0, wd)
        in_specs.append(pl.BlockSpec((o.arr.shape[0], wd), functools.partial(lambda j, off: (0, j + off), off=o.col0 // wd)))
    out_specs = [pl.BlockSpec((r, wd), lambda j: (0, j)) for (r, _, wd) in outs]
    out_shape = [jax.ShapeDtypeStruct((r, ncols // tc * wd), dt) for (r, dt, wd) in outs]
    return pl.pallas_call(
        body, name=name, grid=(ncols // tc,), in_specs=in_specs, out_specs=out_specs, out_shape=out_shape,
        compiler_params=_cparams("parallel"),
    )(*[o.arr for o in ins])


def _ln_stats(u):
    mu = jnp.mean(u, axis=-1, keepdims=True)
    xc = u - mu
    var = jnp.mean(xc * xc, axis=-1, keepdims=True)
    rstd = lax.rsqrt(var + LN_EPS)
    return xc * rstd, rstd


def _ln_fwd(terms, g, b, name):
    coefs = [c for c, _ in terms]

    def fn(row0, *blk):
        xs, (gg, bb) = blk[:len(coefs)], blk[len(coefs):]
        u = sum(c * x for c, x in zip(coefs, xs))
        xhat, _ = _ln_stats(u)
        return xhat * gg + bb

    d = terms[0][1].shape[1]
    return _rows(fn, name, [Rw(x) for _, x in terms] + [Pm(g.reshape(1, d)), Pm(b.reshape(1, d))], [(d, F32)])[0]


def _ln_bwd(dy_terms, u_terms, g, name):
    dc = [c for c, _ in dy_terms]
    uc = [c for c, _ in u_terms]
    d = u_terms[0][1].shape[1]

    def fn(row0, *blk):
        dys = blk[:len(dc)]
        xs = blk[len(dc):len(dc) + len(uc)]
        gg = blk[-1]
        dy = sum(c * x for c, x in zip(dc, dys))
        u = sum(c * x for c, x in zip(uc, xs))
        xhat, rstd = _ln_stats(u)
        gdy = dy * gg
        m1 = jnp.mean(gdy, axis=-1, keepdims=True)
        m2 = jnp.mean(gdy * xhat, axis=-1, keepdims=True)
        du = rstd * (gdy - m1 - xhat * m2)
        return du, jnp.sum(dy * xhat, axis=0, keepdims=True), jnp.sum(dy, axis=0, keepdims=True)

    ins = [Rw(x) for _, x in dy_terms] + [Rw(x) for _, x in u_terms] + [Pm(g.reshape(1, d))]
    return _rows(fn, name, ins, [(d, F32)], accs=[(1, d), (1, d)])


def _loss_head(y, tgt, t_real, name):
    d = y.shape[1]

    def fn(row0, yb, tb):
        rows = _row_ids(yb.shape, row0)
        live = (rows >= N_META) & (rows < t_real)
        diff = jnp.where(live, yb - tb, 0.0)
        return diff * (1.0 / d), jnp.sum(diff * diff, axis=0, keepdims=True) * (0.5 / d)

    return _rows(fn, name, [Rw(y), Rw(tgt)], [(d, F32)], accs=[(1, d)])


def _rms(x, g):
    r = lax.rsqrt(jnp.mean(x * x, axis=-1, keepdims=True) + RMS_EPS)
    return x * r * g


def _rms_bwd(dy, x, g):
    r = lax.rsqrt(jnp.mean(x * x, axis=-1, keepdims=True) + RMS_EPS)
    gdy = dy * g
    dx = r * gdy - x * (r * r * r) * jnp.mean(gdy * x, axis=-1, keepdims=True)
    return dx, jnp.sum(dy * x * r, axis=0, keepdims=True)


def _mla_norms(proj, qn, kvn, name):
    def fn(row0, cq, ckv, g1, g2):
        return _rms(cq, g1), _rms(ckv, g2)

    return _rows(fn, name, [Rw(proj, Q_RANK, C_CQ // Q_RANK), Rw(proj, KV_RANK, C_CKV // KV_RANK),
                            Pm(qn.reshape(1, Q_RANK)), Pm(kvn.reshape(1, KV_RANK))],
                 [(Q_RANK, F32), (KV_RANK, F32)])


def _mla_norms_bwd(dcqn, dckvn, proj, qn, kvn, name):
    def fn(row0, d1, d2, cq, ckv, g1, g2):
        dx1, dg1 = _rms_bwd(d1, cq, g1)
        dx2, dg2 = _rms_bwd(d2, ckv, g2)
        return dx1, dx2, dg1, dg2

    return _rows(fn, name, [Rw(dcqn), Rw(dckvn), Rw(proj, Q_RANK, C_CQ // Q_RANK), Rw(proj, KV_RANK, C_CKV // KV_RANK),
                            Pm(qn.reshape(1, Q_RANK)), Pm(kvn.reshape(1, KV_RANK))],
                 [(Q_RANK, F32), (KV_RANK, F32)], accs=[(1, Q_RANK), (1, KV_RANK)])


def _fold_rope(z):
    return z + pltpu.roll(z, QK_ROPE, 1)


def _mla_pack(qext, kv, proj, cs, name):
    tp = qext.shape[0]
    tm = _tile(tp, 384)

    def body(q_ref, kn_ref, v_ref, kr_ref, cs_ref, qo_ref, ko_ref, vo_ref):
        cs_ = cs_ref[...]
        low = lax.broadcasted_iota(jnp.int32, cs_.shape, 1) < QK_ROPE
        q = q_ref[...]
        qr = jnp.where(low, _fold_rope(q[:, QK_NOPE:] * cs_), 0.0)
        qo_ref[...] = (jnp.concatenate([q[:, :QK_NOPE], qr], axis=1) * ATT_SCALE).astype(BF16)
        kr = _fold_rope(kr_ref[...] * cs_)
        ko_ref[...] = jnp.concatenate([kn_ref[...], kr], axis=1).astype(BF16)
        vo_ref[...] = v_ref[...].astype(BF16)

    return pl.pallas_call(
        body, name=name, grid=(tp // tm, HEADS),
        in_specs=[pl.BlockSpec((tm, 2 * LANE), lambda i, h: (i, h)),
                  pl.BlockSpec((tm, LANE), lambda i, h: (i, h)),
                  pl.BlockSpec((tm, LANE), lambda i, h: (i, HEADS + h)),
                  pl.BlockSpec((tm, LANE), lambda i, h: (i, C_KRP // LANE)),
                  pl.BlockSpec((tm, LANE), lambda i, h: (i, 0))],
        out_specs=[pl.BlockSpec((tm, 2 * LANE), lambda i, h: (i, h)),
                   pl.BlockSpec((tm, 2 * LANE), lambda i, h: (i, h)),
                   pl.BlockSpec((tm, LANE), lambda i, h: (i, h))],
        out_shape=[jax.ShapeDtypeStruct((tp, HEADS * 2 * LANE), BF16),
                   jax.ShapeDtypeStruct((tp, HEADS * 2 * LANE), BF16),
                   jax.ShapeDtypeStruct((tp, HEADS * LANE), BF16)],
        compiler_params=_cparams("parallel", "arbitrary"),
    )(qext, kv, kv, proj, cs)


def _mla_unpack(dq, dk, cs, name):
    tp = dq.shape[0]
    tm = _tile(tp, 384)

    def body(dq_ref, dk_ref, cs_ref, dqe_ref, dkn_ref, dkr_ref):
        h = pl.program_id(1)
        cs_ = cs_ref[...]
        low = lax.broadcasted_iota(jnp.int32, cs_.shape, 1) < QK_ROPE
        dq_ = dq_ref[...] * ATT_SCALE
        dqr = _fold_rope(jnp.where(low, dq_[:, QK_NOPE:], 0.0)) * cs_
        dqe_ref[...] = jnp.concatenate([dq_[:, :QK_NOPE], dqr], axis=1)
        dk_ = dk_ref[...]
        dkn_ref[...] = dk_[:, :QK_NOPE]

        @pl.when(h == 0)
        def _():
            dkr_ref[...] = jnp.zeros_like(dkr_ref)

        dkr_ref[...] += _fold_rope(jnp.where(low, dk_[:, QK_NOPE:], 0.0)) * cs_

    return pl.pallas_call(
        body, name=name, grid=(tp // tm, HEADS),
        in_specs=[pl.BlockSpec((tm, 2 * LANE), lambda i, h: (i, h)),
                  pl.BlockSpec((tm, 2 * LANE), lambda i, h: (i, h)),
                  pl.BlockSpec((tm, LANE), lambda i, h: (i, 0))],
        out_specs=[pl.BlockSpec((tm, 2 * LANE), lambda i, h: (i, h)),
                   pl.BlockSpec((tm, LANE), lambda i, h: (i, h)),
                   pl.BlockSpec((tm, LANE), lambda i, h: (i, 0))],
        out_shape=[jax.ShapeDtypeStruct((tp, HEADS * 2 * LANE), F32),
                   jax.ShapeDtypeStruct((tp, HEADS * LANE), F32),
                   jax.ShapeDtypeStruct((tp, LANE), F32)],
        compiler_params=_cparams("parallel", "arbitrary"),
    )(dq, dk, cs)


def _attn_fwd(q, k, v, t_real, name):
    tp = q.shape[0]
    tq = _tile(tp, 384)
    tkc = _tile(tp, 1408)
    nkc = -(-t_real // tkc)

    def body(q_ref, k_ref, v_ref, o_ref, lse_ref):
        qb = q_ref[...]
        m = l = acc = None
        for c in range(nkc):
            s = lax.dot_general(qb, k_ref[c * tkc:(c + 1) * tkc, :], (((1,), (1,)), ((), ())), preferred_element_type=F32)
            if (c + 1) * tkc > t_real:
                cols = lax.broadcasted_iota(jnp.int32, s.shape, 1) + c * tkc
                s = jnp.where(cols < t_real, s, NEG_BIG)
            mc = jnp.max(s, axis=-1, keepdims=True)
            m_new = mc if c == 0 else jnp.maximum(m, mc)
            p = jnp.exp2(s - m_new)
            lc = jnp.sum(p, axis=-1, keepdims=True)
            pv = jnp.dot(p.astype(BF16), v_ref[c * tkc:(c + 1) * tkc, :], preferred_element_type=F32)
            if c == 0:
                l, acc = lc, pv
            else:
                alpha = jnp.exp2(m - m_new)
                l, acc = alpha * l + lc, alpha * acc + pv
            m = m_new
        o_ref[...] = acc / l
        lse_ref[...] = m + jnp.log2(l)

    return pl.pallas_call(
        body, name=name, grid=(HEADS, tp // tq),
        in_specs=[pl.BlockSpec((tq, 2 * LANE), lambda h, i: (i, h)),
                  pl.BlockSpec((tp, 2 * LANE), lambda h, i: (0, h)),
                  pl.BlockSpec((tp, LANE), lambda h, i: (0, h))],
        out_specs=[pl.BlockSpec((tq, LANE), lambda h, i: (i, h)),
                   pl.BlockSpec((None, tq, 1), lambda h, i: (h, i, 0))],
        out_shape=[jax.ShapeDtypeStruct((tp, HEADS * LANE), F32),
                   jax.ShapeDtypeStruct((HEADS, tp, 1), F32)],
        compiler_params=_cparams("parallel", "parallel"),
    )(q, k, v)


def _attn_bwd(q, k, v, do, o, lse, t_real, name):
    tp = q.shape[0]
    tq = _tile(tp, 384)
    tkc = _tile(tp, 1408)
    nkc = -(-t_real // tkc)

    def body(q_ref, k_ref, v_ref, do_ref, o_ref, lse_ref, dq_ref, dk_ref, dv_ref):
        i = pl.program_id(1)

        @pl.when(i == 0)
        def _():
            dk_ref[...] = jnp.zeros_like(dk_ref)
            dv_ref[...] = jnp.zeros_like(dv_ref)

        qb = q_ref[...]
        dob = do_ref[...]
        dob16 = dob.astype(BF16)
        dol2 = (dob * LN2).astype(BF16)
        delta = jnp.sum(dob * o_ref[...], axis=-1, keepdims=True) * LN2
        lse = lse_ref[...]
        dq = None
        for c in range(nkc):
            ks = slice(c * tkc, (c + 1) * tkc)
            kb = k_ref[ks, :]
            s = lax.dot_general(qb, kb, (((1,), (1,)), ((), ())), preferred_element_type=F32)
            p = jnp.exp2(s - lse)
            if (c + 1) * tkc > t_real:
                cols = lax.broadcasted_iota(jnp.int32, s.shape, 1) + c * tkc
                p = jnp.where(cols < t_real, p, 0.0)
            dp = lax.dot_general(dol2, v_ref[ks, :], (((1,), (1,)), ((), ())), preferred_element_type=F32)
            ds = (p * (dp - delta)).astype(BF16)
            dqc = jnp.dot(ds, kb, preferred_element_type=F32)
            dq = dqc if c == 0 else dq + dqc
            dk_ref[ks, :] += lax.dot_general(ds, qb, (((0,), (0,)), ((), ())), preferred_element_type=F32)
            dv_ref[ks, :] += lax.dot_general(p.astype(BF16), dob16, (((0,), (0,)), ((), ())), preferred_element_type=F32)
        dq_ref[...] = dq

    return pl.pallas_call(
        body, name=name, grid=(HEADS, tp // tq),
        in_specs=[pl.BlockSpec((tq, 2 * LANE), lambda h, i: (i, h)),
                  pl.BlockSpec((tp, 2 * LANE), lambda h, i: (0, h)),
                  pl.BlockSpec((tp, LANE), lambda h, i: (0, h)),
                  pl.BlockSpec((tq, LANE), lambda h, i: (i, h)),
                  pl.BlockSpec((tq, LANE), lambda h, i: (i, h)),
                  pl.BlockSpec((None, tq, 1), lambda h, i: (h, i, 0))],
        out_specs=[pl.BlockSpec((tq, 2 * LANE), lambda h, i: (i, h)),
                   pl.BlockSpec((tp, 2 * LANE), lambda h, i: (0, h)),
                   pl.BlockSpec((tp, LANE), lambda h, i: (0, h))],
        out_shape=[jax.ShapeDtypeStruct((tp, HEADS * 2 * LANE), F32),
                   jax.ShapeDtypeStruct((tp, HEADS * 2 * LANE), F32),
                   jax.ShapeDtypeStruct((tp, HEADS * LANE), F32)],
        compiler_params=_cparams("parallel", "arbitrary"),
    )(q, k, v, do, o, lse)


def _shift_rows(x, s):
    tp = x.shape[0]
    return x if s % tp == 0 else pltpu.roll(x, s % tp, 0)


def _conv_fwd_val(xm, w, b, pad_left):
    acc = b + w[0:1, :] * _shift_rows(xm, pad_left)
    for k in range(1, w.shape[0]):
        acc = acc + w[k:k + 1, :] * _shift_rows(xm, pad_left - k)
    return acc


def _conv_bwd_val(dy, xm, w, pad_left, live):
    kk = w.shape[0]
    dx = w[0:1, :] * _shift_rows(dy, -pad_left)
    dws = [jnp.sum(dy * _shift_rows(xm, pad_left), axis=0, keepdims=True)]
    for k in range(1, kk):
        dx = dx + w[k:k + 1, :] * _shift_rows(dy, k - pad_left)
        dws.append(jnp.sum(dy * _shift_rows(xm, pad_left - k), axis=0, keepdims=True))
    return jnp.where(live, dx, 0.0), jnp.concatenate(dws, axis=0), jnp.sum(dy, axis=0, keepdims=True)


def _lru_conv_fwd(proj, w, b, t_real, name):
    def fn(x, ww, bb):
        xm = jnp.where(_row_ids(x.shape) < t_real, x, 0.0)
        return _conv_fwd_val(xm, ww, bb, 2)

    return _cols(fn, name, [Cl(proj, C_LRU_X), Cl(w), Cl(b.reshape(1, -1))], [(proj.shape[0], F32)], D_MODEL, 128)[0]


def _lru_conv_bwd(dxc, proj, w, t_real, name):
    def fn(dy, x, ww):
        live = _row_ids(x.shape) < t_real
        xm = jnp.where(live, x, 0.0)
        dym = jnp.where(live, dy, 0.0)
        return _conv_bwd_val(dym, xm, ww, 2, live)

    return _cols(fn, name, [Cl(dxc), Cl(proj, C_LRU_X), Cl(w)],
                 [(proj.shape[0], F32), (w.shape[0], F32), (1, F32)], D_MODEL, 128)


def _pair_columns(a):
    lead = a.shape[:-1]
    return jnp.swapaxes(a.reshape(lead + (2, D_FF // LANE, LANE)), -3, -2).reshape(lead + (2 * D_FF,))


def _unpair_columns(a):
    lead = a.shape[:-1]
    return jnp.swapaxes(a.reshape(lead + (D_FF // LANE, 2, LANE)), -3, -2).reshape(lead + (2 * D_FF,))


def _ffn_conv_act(up, w, b, t_real, name):
    def fn(u, ww, bb):
        c = _conv_fwd_val(jnp.where(_row_ids(u.shape) < t_real, u, 0.0), ww, bb, 1)
        return _gelu(c[:, :LANE]) * c[:, LANE:]

    return _cols(fn, name, [Cl(up, 0, 2 * LANE), Cl(w, 0, 2 * LANE), Cl(b.reshape(1, -1), 0, 2 * LANE)],
                 [(up.shape[0], F32)], D_FF, LANE)[0]


def _ffn_conv_act_bwd(dm, up, w, b, t_real, name):
    def fn(dmb, u, ww, bb):
        live = _row_ids(u.shape) < t_real
        um = jnp.where(live, u, 0.0)
        c = _conv_fwd_val(um, ww, bb, 1)
        act, dact = _gelu_and_grad(c[:, :LANE])
        dmm = jnp.where(live[:, :LANE], dmb, 0.0)
        dc = jnp.concatenate([dmm * c[:, LANE:] * dact, dmm * act], axis=1)
        return _conv_bwd_val(dc, um, ww, 1, live)

    tp, kk = up.shape[0], w.shape[0]
    return _cols(fn, name, [Cl(dm), Cl(up, 0, 2 * LANE), Cl(w, 0, 2 * LANE), Cl(b.reshape(1, -1), 0, 2 * LANE)],
                 [(tp, F32, 2 * LANE), (kk, F32, 2 * LANE), (1, F32, 2 * LANE)], D_FF, LANE)


def _lru_gates_fwd(xc, wg, b4, lam, t_real, name):
    tp = xc.shape[0]
    tm = _tile(tp, 1408)

    def body(x_ref, w_ref, b_ref, lam_ref, r0_ref, r1_ref, i0_ref, i1_ref, a0_ref, a1_ref, u0_ref, u1_ref):
        x = x_ref[...]
        xb = x.astype(BF16)
        live = _row_ids(x.shape, pl.program_id(1) * tm) < t_real
        bb = b_ref[...]
        sp = _softplus_neg(lam_ref[...])
        gate = [_sigmoid(jnp.dot(xb, w_ref[k], preferred_element_type=F32) + bb[k:k + 1, :]) for k in range(4)]
        for d, (r_ref, i_ref, a_ref, u_ref) in enumerate(((r0_ref, i0_ref, a0_ref, u0_ref), (r1_ref, i1_ref, a1_ref, u1_ref))):
            r, ig = gate[d], gate[2 + d]
            a = jnp.exp(-LRU_C * r * sp[d:d + 1, :])
            r_ref[...] = r
            i_ref[...] = ig
            a_ref[...] = a
            u_ref[...] = jnp.where(live, jnp.sqrt(1.0 - a * a) * (ig * x), 0.0)

    blk = pl.BlockSpec((tm, LANE), lambda g, i: (i, g))
    return pl.pallas_call(
        body, name=name, grid=(LRU_BLOCKS, tp // tm),
        in_specs=[blk, pl.BlockSpec((None, 4, LANE, LANE), lambda g, i: (g, 0, 0, 0)),
                  pl.BlockSpec((4, LANE), lambda g, i: (0, g)), pl.BlockSpec((2, LANE), lambda g, i: (0, g))],
        out_specs=[blk] * 8,
        out_shape=[jax.ShapeDtypeStruct((tp, D_MODEL), F32)] * 8,
        compiler_params=_cparams("parallel", "parallel"),
    )(xc, wg, b4, lam)


def _lru_gates_bwd(l0, l1, da0, da1, r0, r1, i0, i1, a0, a1, xc, wg, lam, t_real, name):
    tp = xc.shape[0]
    tm = _tile(tp, 1408)

    def body(l0_ref, l1_ref, da0_ref, da1_ref, r0_ref, r1_ref, i0_ref, i1_ref, a0_ref, a1_ref, x_ref, w_ref, lam_ref,
             dx_ref, dw_ref, db_ref, dlam_ref):
        i = pl.program_id(1)
        x = x_ref[...]
        xb = x.astype(BF16)
        live = _row_ids(x.shape, i * tm) < t_real
        lam_ = lam_ref[...]
        sp = _softplus_neg(lam_)
        dsp_dlam = -_sigmoid(-lam_)
        dx = jnp.zeros_like(x)
        dpre = [None] * 4
        dlam_rows = []
        for d, (l_ref, da_ref, r_ref, i_ref, a_ref) in enumerate(((l0_ref, da0_ref, r0_ref, i0_ref, a0_ref),
                                                                  (l1_ref, da1_ref, r1_ref, i1_ref, a1_ref))):
            r, ig, a = r_ref[...], i_ref[...], a_ref[...]
            du = jnp.where(live, l_ref[...], 0.0)
            s = jnp.sqrt(1.0 - a * a)
            dv = du * s
            ds = du * (ig * x)
            dla = jnp.where(live, da_ref[...], 0.0) * a - ds * (a * a) / s
            dla = jnp.where(live, dla, 0.0)
            dr = dla * (-LRU_C) * sp[d:d + 1, :]
            dlam_rows.append(jnp.sum(dla * (-LRU_C) * r, axis=0, keepdims=True) * dsp_dlam[d:d + 1, :])
            dpre[d] = dr * r * (1.0 - r)
            dpre[2 + d] = dv * x * ig * (1.0 - ig)
            dx = dx + dv * ig

        @pl.when(i == 0)
        def _():
            dw_ref[...] = jnp.zeros_like(dw_ref)
            db_ref[...] = jnp.zeros_like(db_ref)
            dlam_ref[...] = jnp.zeros_like(dlam_ref)

        for k in range(4):
            pk = dpre[k].astype(BF16)
            dx = dx + lax.dot_general(pk, w_ref[k], (((1,), (1,)), ((), ())), preferred_element_type=F32)
            dw_ref[k] += lax.dot_general(xb, pk, (((0,), (0,)), ((), ())), preferred_element_type=F32)
        db_ref[...] += jnp.concatenate([jnp.sum(p, axis=0, keepdims=True) for p in dpre], axis=0)
        dlam_ref[...] += jnp.concatenate(dlam_rows, axis=0)
        dx_ref[...] = dx

    blk = pl.BlockSpec((tm, LANE), lambda g, i: (i, g))
    return pl.pallas_call(
        body, name=name, grid=(LRU_BLOCKS, tp // tm),
        in_specs=[blk] * 11 + [pl.BlockSpec((None, 4, LANE, LANE), lambda g, i: (g, 0, 0, 0)),
                               pl.BlockSpec((2, LANE), lambda g, i: (0, g))],
        out_specs=[blk, pl.BlockSpec((None, 4, LANE, LANE), lambda g, i: (g, 0, 0, 0)),
                   pl.BlockSpec((4, LANE), lambda g, i: (0, g)), pl.BlockSpec((2, LANE), lambda g, i: (0, g))],
        out_shape=[jax.ShapeDtypeStruct((tp, D_MODEL), F32), jax.ShapeDtypeStruct((LRU_BLOCKS, 4, LANE, LANE), F32),
                   jax.ShapeDtypeStruct((4, D_MODEL), F32), jax.ShapeDtypeStruct((2, D_MODEL), F32)],
        compiler_params=_cparams("parallel", "arbitrary"),
    )(l0, l1, da0, da1, r0, r1, i0, i1, a0, a1, xc, wg, lam)


def _tile_scan(a, u, reverse):
    rows = lax.broadcasted_iota(jnp.int32, a.shape, 0)
    for s in (1, 2, 4):
        if reverse:
            keep = rows < SUBLANE - s
            a_sh, u_sh = pltpu.roll(a, SUBLANE - s, 0), pltpu.roll(u, SUBLANE - s, 0)
        else:
            keep = rows >= s
            a_sh, u_sh = pltpu.roll(a, s, 0), pltpu.roll(u, s, 0)
        u = u + a * jnp.where(keep, u_sh, 0.0)
        a = a * jnp.where(keep, a_sh, 1.0)
    return a, u


def _scan_fwd(a0, u0, a1, u1, name):
    tp, d = a0.shape
    tc = 128
    nt = tp // SUBLANE

    def body(a0_ref, u0_ref, a1_ref, u1_ref, h0_ref, h1_ref):
        def step(t, carry):
            c0, c1 = carry
            f = pl.multiple_of(t * SUBLANE, SUBLANE)
            b = pl.multiple_of((nt - 1 - t) * SUBLANE, SUBLANE)
            pa, pu = _tile_scan(a0_ref[pl.ds(f, SUBLANE), :], u0_ref[pl.ds(f, SUBLANE), :], False)
            h = pu + pa * c0
            h0_ref[pl.ds(f, SUBLANE), :] = h
            c0 = h[SUBLANE - 1:SUBLANE, :]
            pa, pu = _tile_scan(a1_ref[pl.ds(b, SUBLANE), :], u1_ref[pl.ds(b, SUBLANE), :], True)
            h = pu + pa * c1
            h1_ref[pl.ds(b, SUBLANE), :] = h
            c1 = h[0:1, :]
            return c0, c1

        z = jnp.zeros((1, tc), F32)
        lax.fori_loop(0, nt, step, (z, z))

    blk = pl.BlockSpec((tp, tc), lambda j: (0, j))
    return pl.pallas_call(
        body, name=name, grid=(d // tc,), in_specs=[blk] * 4, out_specs=[blk] * 2,
        out_shape=[jax.ShapeDtypeStruct((tp, d), F32)] * 2,
        compiler_params=_cparams("parallel"),
    )(a0, u0, a1, u1)


def _scan_bwd(dh, a0, a1, h0, h1, name):
    tp, d = dh.shape
    tc = 128
    nt = tp // SUBLANE

    def body(dh_ref, a0_ref, a1_ref, h0_ref, h1_ref, l0_ref, l1_ref, da0_ref, da1_ref):
        rows8 = lax.broadcasted_iota(jnp.int32, (SUBLANE, tc), 0)

        def step(t, carry):
            c0, c1 = carry
            b = pl.multiple_of((nt - 1 - t) * SUBLANE, SUBLANE)
            f = pl.multiple_of(t * SUBLANE, SUBLANE)
            a = a0_ref[pl.ds(b, SUBLANE), :]
            a_next = jnp.where(rows8 < SUBLANE - 1, pltpu.roll(a, SUBLANE - 1, 0), 1.0)
            pa, pu = _tile_scan(a_next, dh_ref[pl.ds(b, SUBLANE), :], True)
            lam = pu + pa * c0
            l0_ref[pl.ds(b, SUBLANE), :] = lam
            c0 = a[0:1, :] * lam[0:1, :]
            a = a1_ref[pl.ds(f, SUBLANE), :]
            a_prev = jnp.where(rows8 >= 1, pltpu.roll(a, 1, 0), 1.0)
            pa, pu = _tile_scan(a_prev, dh_ref[pl.ds(f, SUBLANE), :], False)
            lam = pu + pa * c1
            l1_ref[pl.ds(f, SUBLANE), :] = lam
            c1 = a[SUBLANE - 1:SUBLANE, :] * lam[SUBLANE - 1:SUBLANE, :]
            return c0, c1

        z = jnp.zeros((1, tc), F32)
        lax.fori_loop(0, nt, step, (z, z))
        rows = lax.broadcasted_iota(jnp.int32, (tp, tc), 0)
        da0_ref[...] = l0_ref[...] * jnp.where(rows >= 1, pltpu.roll(h0_ref[...], 1, 0), 0.0)
        da1_ref[...] = l1_ref[...] * jnp.where(rows < tp - 1, pltpu.roll(h1_ref[...], tp - 1, 0), 0.0)

    blk = pl.BlockSpec((tp, tc), lambda j: (0, j))
    return pl.pallas_call(
        body, name=name, grid=(d // tc,), in_specs=[blk] * 5, out_specs=[blk] * 4,
        out_shape=[jax.ShapeDtypeStruct((tp, d), F32)] * 4,
        compiler_params=_cparams("parallel"),
    )(dh, a0, a1, h0, h1)


def _gated_h(proj, h0, h1, name):
    def fn(row0, lg, x0, x1):
        return _gelu(lg) * (x0 + x1)

    return _rows(fn, name, [Rw(proj, D_MODEL, C_LRU_G // D_MODEL), Rw(h0), Rw(h1)], [(D_MODEL, F32)])[0]


def _gated_h_bwd(dgh, proj, h0, h1, name):
    def fn(row0, dg, lg, x0, x1):
        act, dact = _gelu_and_grad(lg)
        return dg * (x0 + x1) * dact, dg * act

    return _rows(fn, name, [Rw(dgh), Rw(proj, D_MODEL, C_LRU_G // D_MODEL), Rw(h0), Rw(h1)], [(D_MODEL, F32)] * 2)


def _mix(proj, y_mla, y_lru, name):
    def fn(row0, gm, gl, ym, yl):
        return _sigmoid(gm) * ym + _sigmoid(gl) * yl

    return _rows(fn, name, [Rw(proj, D_MODEL, C_G_MLA // D_MODEL), Rw(proj, D_MODEL, C_G_LRU // D_MODEL), Rw(y_mla), Rw(y_lru)],
                 [(D_MODEL, F32)])[0]


def _mix_bwd(dz, proj, y_mla, y_lru, name):
    def fn(row0, dzb, gm, gl, ym, yl):
        sm, sl = _sigmoid(gm), _sigmoid(gl)
        return dzb * sm, dzb * sl, dzb * ym * sm * (1.0 - sm), dzb * yl * sl * (1.0 - sl)

    return _rows(fn, name, [Rw(dz), Rw(proj, D_MODEL, C_G_MLA // D_MODEL), Rw(proj, D_MODEL, C_G_LRU // D_MODEL),
                            Rw(y_mla), Rw(y_lru)], [(D_MODEL, F32)] * 4)


def _layer_fwd(h, w_in, more_weights, cs, t_real, tag):
    proj = _matmul(h, w_in, tag + "proj")
    w = dict(more_weights(0, proj), w_in=w_in)
    cqn, ckvn = _mla_norms(proj, w['q_norm'], w['kv_norm'], tag + "mla_norms")
    qext = _matmul(cqn, w['w_q'], tag + "q_up")
    kv = _matmul(ckvn, w['w_kv'], tag + "kv_up")
    qc, kc, vb = _mla_pack(qext, kv, proj, cs, tag + "mla_pack")
    o, lse = _attn_fwd(qc, kc, vb, t_real, tag + "attn_fwd")
    w.update(more_weights(1, o))
    y_mla = _matmul(o, w['w_o_mla'], tag + "o_mla")
    xc = _lru_conv_fwd(proj, w['lru_conv_w'], w['lru_conv_b'], t_real, tag + "lru_conv")
    r0, r1, i0, i1, a0, a1, u0, u1 = _lru_gates_fwd(xc, w['w_g'], w['b4'], w['lru_lambda'], t_real, tag + "lru_gates")
    h0, h1 = _scan_fwd(a0, u0, a1, u1, tag + "lru_scan")
    gh = _gated_h(proj, h0, h1, tag + "lru_gate_out")
    y_lru = _matmul(gh, w['w_o_lru'], tag + "o_lru")
    z = _mix(proj, y_mla, y_lru, tag + "mix")
    zo = _matmul(z, w['w_out'], tag + "w_out")
    hm = _ln_fwd([(DN_ALPHA, h), (1.0, zo)], w['ln1_g'], w['ln1_b'], tag + "ln1")
    w.update(more_weights(2, hm))
    up = _matmul(hm, w['w_up'], tag + "w_up")
    m = _ffn_conv_act(up, w['ffn_conv_w'], w['ffn_conv_b'], t_real, tag + "ffn_conv")
    f = _matmul(m, w['w_down'], tag + "w_down", tk_cap=1408)
    out = _ln_fwd([(DN_ALPHA, hm), (1.0, f)], w['ln2_g'], w['ln2_b'], tag + "ln2")
    saved = dict(w=w, h=h, proj=proj, cqn=cqn, ckvn=ckvn, qc=qc, kc=kc, vb=vb, o=o, lse=lse, y_mla=y_mla, xc=xc,
                 r0=r0, r1=r1, i0=i0, i1=i1, a0=a0, a1=a1, h0=h0, h1=h1, gh=gh, y_lru=y_lru, z=z, zo=zo, hm=hm,
                 up=up, m=m, f=f)
    return out, saved


DW_MATMUL = dict(ta=True, out_dtype=BF16, tm_cap=512, tn_cap=8192, tk_cap=384)


def _after(a, tok):
    return a if tok is None else a + tok.astype(a.dtype)


def _layer_bwd(dout_terms, s, cs, t_real, tag, emit, tok):
    w = s['w']
    g = {}
    du2, dg2, db2 = _ln_bwd(dout_terms, [(DN_ALPHA, s['hm']), (1.0, s['f'])], _after(w['ln2_g'], tok), tag + "ln2_bwd")
    g['ln2_g'], g['ln2_b'] = dg2, db2
    dm = _matmul(du2, w['w_down'], tag + "w_down_dx", tb=True)
    g['w_down'] = _matmul(s['m'], du2, tag + "w_down_dw", **DW_MATMUL)
    dup, g['ffn_conv_w'], g['ffn_conv_b'] = _ffn_conv_act_bwd(dm, s['up'], w['ffn_conv_w'], w['ffn_conv_b'], t_real, tag + "ffn_conv_bwd")
    dhm_mm = _matmul(dup, w['w_up'], tag + "w_up_dx", tb=True, tk_cap=1408)
    g['w_up'] = _matmul(s['hm'], dup, tag + "w_up_dw", **DW_MATMUL)
    tok = emit('ffn', g)
    g = {}
    du1, dg1, db1 = _ln_bwd([(DN_ALPHA, du2), (1.0, dhm_mm)], [(DN_ALPHA, s['h']), (1.0, s['zo'])],
                            _after(w['ln1_g'], tok), tag + "ln1_bwd")
    g['ln1_g'], g['ln1_b'] = dg1, db1
    dz = _matmul(du1, w['w_out'], tag + "w_out_dx", tb=True)
    g['w_out'] = _matmul(s['z'], du1, tag + "w_out_dw", **DW_MATMUL)
    dy_mla, dy_lru, dg_mla, dg_lru = _mix_bwd(dz, s['proj'], s['y_mla'], s['y_lru'], tag + "mix_bwd")
    do = _matmul(dy_mla, w['w_o_mla'], tag + "o_mla_dx", tb=True)
    g['w_o_mla'] = _matmul(s['o'], dy_mla, tag + "o_mla_dw", **DW_MATMUL)
    dqc, dkc, dv = _attn_bwd(s['qc'], s['kc'], s['vb'], do, s['o'], s['lse'], t_real, tag + "attn_bwd")
    dqext, dkn, dkrp = _mla_unpack(dqc, dkc, cs, tag + "mla_unpack")
    dkv = jnp.concatenate([dkn, dv], axis=1)
    dcqn = _matmul(dqext, w['w_q'], tag + "q_up_dx", tb=True)
    g['w_q'] = _matmul(s['cqn'], dqext, tag + "q_up_dw", **DW_MATMUL)
    dckvn = _matmul(dkv, w['w_kv'], tag + "kv_up_dx", tb=True)
    g['w_kv'] = _matmul(s['ckvn'], dkv, tag + "kv_up_dw", **DW_MATMUL)
    dcq, dckv, g['q_norm'], g['kv_norm'] = _mla_norms_bwd(dcqn, dckvn, s['proj'], w['q_norm'], w['kv_norm'], tag + "mla_norms_bwd")
    dgh = _matmul(dy_lru, w['w_o_lru'], tag + "o_lru_dx", tb=True)
    g['w_o_lru'] = _matmul(s['gh'], dy_lru, tag + "o_lru_dw", **DW_MATMUL)
    dlru_g, dhs = _gated_h_bwd(dgh, s['proj'], s['h0'], s['h1'], tag + "lru_gate_out_bwd")
    l0, l1, da0, da1 = _scan_bwd(dhs, s['a0'], s['a1'], s['h0'], s['h1'], tag + "lru_scan_bwd")
    dxc, g['w_g'], g['b4'], g['lru_lambda'] = _lru_gates_bwd(
        l0, l1, da0, da1, s['r0'], s['r1'], s['i0'], s['i1'], s['a0'], s['a1'], s['xc'], w['w_g'], w['lru_lambda'],
        t_real, tag + "lru_gates_bwd")
    dlru_x, g['lru_conv_w'], g['lru_conv_b'] = _lru_conv_bwd(dxc, s['proj'], w['lru_conv_w'], t_real, tag + "lru_conv_bwd")
    tok = emit('mid', g)
    dproj = jnp.concatenate([dlru_g, dlru_x, dg_mla, dg_lru, dcq, dckv, _after(dkrp, tok)], axis=1)
    tok = emit('in', {'w_in': _matmul(s['h'], dproj, tag + "proj_dw", **DW_MATMUL)})
    dh_mm = _matmul(dproj, w['w_in'], tag + "proj_dx", tb=True, tk_cap=1536)
    return [(DN_ALPHA, du1), (1.0, dh_mm)], tok


def _swap_halves(a, axis=-1):
    h1, h2 = jnp.split(a, 2, axis=axis)
    return jnp.concatenate([h2, h1], axis=axis)


def _w_in_kernel(w_in):
    cq, ckv, kr, lg, lx, gm, gl = jnp.split(w_in, [256, 384, 448, 1472, 2496, 3520], axis=1)
    return jnp.concatenate([lg, lx, gm, gl, cq, ckv, kr, _swap_halves(kr)], axis=1)


def _layer_weights(fl):
    w = {}
    if 'w_uq' in fl:
        uq = fl['w_uq']
        w['w_q'] = jnp.concatenate([uq, _swap_halves(uq[..., QK_NOPE:])], axis=-1).reshape(Q_RANK, HEADS * 2 * LANE)
        w['w_kv'] = jnp.concatenate([fl['w_uk'].reshape(KV_RANK, -1), fl['w_uv'].reshape(KV_RANK, -1)], axis=1).astype(BF16)
        w['w_g'] = jnp.moveaxis(jnp.concatenate([fl['w_rg'], fl['w_ig']], axis=0), 0, 1).astype(BF16)
        w['b4'] = jnp.concatenate([fl['b_rg'], fl['b_ig']], axis=0)
    for n in ('q_norm', 'kv_norm', 'w_o_mla', 'lru_conv_w', 'lru_conv_b', 'lru_lambda', 'w_o_lru', 'w_out', 'ln1_g',
              'ln1_b', 'w_down', 'ln2_g', 'ln2_b'):
        if n in fl:
            w[n] = fl[n]
    for n in ('w_up', 'ffn_conv_w', 'ffn_conv_b'):
        if n in fl:
            w[n] = _pair_columns(fl[n])
    return w


def _layer_grads(g):
    out = {}
    if 'w_in' in g:
        lg, lx, gm, gl, cq, ckv, kr, krs = jnp.split(g['w_in'], [1024, 2048, 3072, 4096, 4352, 4480, 4544], axis=1)
        out['w_in'] = jnp.concatenate([cq, ckv, kr + _swap_halves(krs), lg, lx, gm, gl], axis=1)
    if 'w_q' in g:
        gq = g['w_q'].reshape(Q_RANK, HEADS, 2 * LANE)
        out['w_uq'] = jnp.concatenate([gq[..., :QK_NOPE], gq[..., QK_NOPE:QK_NOPE + QK_ROPE] + _swap_halves(gq[..., QK_NOPE + QK_ROPE:])], axis=-1)
    if 'w_kv' in g:
        out['w_uk'] = g['w_kv'][:, :HEADS * QK_NOPE].reshape(KV_RANK, HEADS, QK_NOPE)
        out['w_uv'] = g['w_kv'][:, HEADS * QK_NOPE:].reshape(KV_RANK, HEADS, V_HEAD)
    if 'w_g' in g:
        gg = jnp.moveaxis(g['w_g'], 1, 0)
        out['w_rg'], out['w_ig'] = gg[:2], gg[2:]
    if 'b4' in g:
        out['b_rg'], out['b_ig'] = g['b4'][:2], g['b4'][2:]
    for n in ('w_up', 'ffn_conv_w', 'ffn_conv_b'):
        if n in g:
            out[n] = _unpair_columns(g[n])
    for n in ('q_norm', 'kv_norm', 'lru_conv_b', 'ln1_g', 'ln1_b', 'ffn_conv_b', 'ln2_g', 'ln2_b'):
        if n in g:
            out[n] = out.get(n, g[n]).reshape(-1)
    for n in ('w_o_mla', 'lru_conv_w', 'lru_lambda', 'w_o_lru', 'w_out', 'w_down'):
        if n in g:
            out[n] = g[n]
    return out


def _rope_table(tp):
    half = QK_ROPE // 2
    inv_freq = jnp.exp(-math.log(ROPE_THETA) * jnp.arange(half, dtype=F32) / half)
    ang = jnp.arange(tp, dtype=F32)[:, None] * inv_freq[None, :]
    c, s = jnp.cos(ang), jnp.sin(ang)
    return jnp.concatenate([c, c, -s, s], axis=1)


def _local_step(x, target, meta, ln0_g, ln0_b, layer_w, t_pad, emit):
    seq = x.shape[0]
    t_real = N_META + seq
    zpad = jnp.zeros((t_pad - t_real, D_MODEL), F32)
    xin = jnp.concatenate([meta, x, zpad], axis=0)
    tgt = jnp.concatenate([jnp.zeros((N_META, D_MODEL), F32), target, zpad], axis=0)
    cs = _rope_table(t_pad)
    h = _ln_fwd([(1.0, xin)], ln0_g, ln0_b, "ln0")
    saved = []
    for l in range(DEPTH):
        w_in, rest_of_weights = layer_w[l](h)
        h, s = _layer_fwd(h, w_in, rest_of_weights, cs, t_real, "l%d_" % l)
        saved.append(s)
    dy, lossvec = _loss_head(h, tgt, t_real, "loss_head")
    loss = jnp.sum(lossvec)
    terms, tok = [(1.0, dy)], None
    for l in reversed(range(DEPTH)):
        terms, tok = _layer_bwd(terms, saved[l], cs, t_real, "l%d_" % l,
                                functools.partial(lambda stage, g, l: emit(l, stage, _layer_grads(g)), l=l), tok)
    dxin, dg0, db0 = _ln_bwd(terms, [(1.0, xin)], _after(ln0_g, tok), "ln0_bwd")
    emit(None, 'head', {'meta_tokens': dxin[:N_META], 'ln0_g': dg0.reshape(-1), 'ln0_b': db0.reshape(-1)})
    return loss, dxin[N_META:t_real]


_HBM = pl.BlockSpec(memory_space=pltpu.HBM)
_SEM = pl.BlockSpec(memory_space=pltpu.SEMAPHORE)
_SIDE_EFFECT = pltpu.SideEffectType.DATAFLOW_SIDE_EFFECTING


def _peer_copies(src_refs, land_refs, scatters, send_sems, recv_sems):
    x, y, c = lax.axis_index("x"), lax.axis_index("y"), lax.axis_index("c")
    me = 4 * x + 2 * y + c
    copies = []
    for k in range(1, N_DEV):
        px = 1 - x if k & 4 else x
        py = 1 - y if k & 2 else y
        pc = 1 - c if k & 1 else c
        for t, (src, land) in enumerate(zip(src_refs, land_refs)):
            copies.append(pltpu.make_async_remote_copy(
                src_ref=src.at[4 * px + 2 * py + pc] if scatters[t] else src, dst_ref=land.at[me],
                send_sem=send_sems.at[7 * t + k - 1], recv_sem=recv_sems.at[7 * t + k - 1],
                device_id=(px, py, pc), device_id_type=pl.DeviceIdType.MESH))
    return me, copies


def _exchange_start(groups, name):
    flat = [it for grp in groups for it in grp]
    nt, ng = len(flat), len(groups)
    scatters = [sc for _, sc in flat]
    srcs = [pltpu.with_memory_space_constraint(a, pltpu.HBM) for a, _ in flat]
    land_shapes = [a.shape if sc else (N_DEV,) + a.shape for a, sc in flat]
    lands = [pltpu.with_memory_space_constraint(lax.empty(s, a.dtype), pltpu.HBM) for s, (a, _) in zip(land_shapes, flat)]
    bounds = [0]
    for grp in groups:
        bounds.append(bounds[-1] + len(grp))

    def body(*refs):
        src_refs, land_refs = refs[:nt], refs[nt:2 * nt]
        sem_refs = refs[2 * nt:2 * nt + 2 * ng]
        token_ref = refs[4 * nt + 2 * ng]
        for gi in range(ng):
            lo, hi = bounds[gi], bounds[gi + 1]
            _, copies = _peer_copies(src_refs[lo:hi], land_refs[lo:hi], scatters[lo:hi], sem_refs[2 * gi], sem_refs[2 * gi + 1])
            for cp in copies:
                cp.start()
        token_ref[...] = jnp.zeros_like(token_ref)

    out_shape = []
    for grp in groups:
        out_shape += [pltpu.SemaphoreType.DMA((7 * len(grp),)), pltpu.SemaphoreType.DMA((7 * len(grp),))]
    out_shape += [pltpu.HBM(a.shape, a.dtype) for a in srcs] + [pltpu.HBM(s, a.dtype) for s, a in zip(land_shapes, srcs)]
    out_shape += [jax.ShapeDtypeStruct((SUBLANE, LANE), F32)]
    res = pl.pallas_call(
        body, name=name, out_shape=out_shape,
        in_specs=[_HBM] * (2 * nt),
        out_specs=[_SEM] * (2 * ng) + [_HBM] * (2 * nt) + [pl.BlockSpec(memory_space=pltpu.VMEM)],
        input_output_aliases={t: 2 * ng + t for t in range(2 * nt)},
        compiler_params=pltpu.CompilerParams(has_side_effects=_SIDE_EFFECT),
    )(*srcs, *lands)
    sems, thru, token = res[:2 * ng], res[2 * ng:2 * ng + 2 * nt], res[-1]
    states = []
    for gi in range(ng):
        lo, hi = bounds[gi], bounds[gi + 1]
        states.append((sems[2 * gi], sems[2 * gi + 1], thru[lo:hi], thru[nt + lo:nt + hi], scatters[lo:hi]))
    return states, token[0, 0]


def _exchange_wait(state, after, name):
    send_sems, recv_sems, srcs, lands, scatters = state
    n = len(srcs)

    def body(*refs):
        _, copies = _peer_copies(refs[:n], refs[n:2 * n], scatters, refs[2 * n], refs[2 * n + 1])
        for cp in copies:
            cp.wait_send()
        for cp in copies:
            cp.wait_recv()

    res = pl.pallas_call(
        body, name=name,
        out_shape=[pltpu.HBM(a.shape, a.dtype) for a in srcs] + [pltpu.HBM(a.shape, a.dtype) for a in lands],
        in_specs=[_HBM] * (2 * n) + [_SEM, _SEM, _HBM],
        out_specs=[_HBM] * (2 * n),
        input_output_aliases={t: t for t in range(2 * n)},
        compiler_params=pltpu.CompilerParams(has_side_effects=_SIDE_EFFECT),
    )(*srcs, *lands, send_sems, recv_sems, pltpu.with_memory_space_constraint(after, pltpu.HBM))
    me = 4 * lax.axis_index("x") + 2 * lax.axis_index("y") + lax.axis_index("c")
    out = []
    for src, land, sc in zip(res[:n], res[n:], scatters):
        own = lax.dynamic_index_in_dim(src, me, 0, keepdims=True) if sc else src[None]
        out.append(lax.dynamic_update_slice_in_dim(land, own, me, 0))
    return out


def _as_rows(shape):
    return (1, shape[0]) if len(shape) == 1 else (math.prod(shape[:-1]), shape[-1])


def _sum_adamw(pieces, w, m, v, name):
    shape = w.shape
    nl = len(pieces)
    if nl > 1 and _as_rows(shape[1:])[0] % 16:
        pieces, nl = [jnp.stack(pieces, axis=1)], 1
    rows, cols = _as_rows(shape)
    rl = rows // nl
    cap = max(16, (1 << 18) // cols // 16 * 16)
    tr = _tile(rl, cap, 16)
    nb = rl // tr
    c1 = 1.0 / (1.0 - ADAM_B1 ** ADAM_STEP)
    c2 = 1.0 / (1.0 - ADAM_B2 ** ADAM_STEP)

    def body(*refs):
        p_refs = refs[:nl]
        w_ref, m_ref, v_ref, g_ref, d_ref, nm_ref, nv_ref = refs[nl:]
        li = pl.program_id(0)

        def total(p_ref):
            acc = p_ref[0].astype(F32)
            for k in range(1, N_DEV):
                acc = acc + p_ref[k].astype(F32)
            return acc

        gg = total(p_refs[0])
        for l in range(1, nl):
            gg = jnp.where(li == l, total(p_refs[l]), gg)
        nm = ADAM_B1 * m_ref[...] + (1.0 - ADAM_B1) * gg
        nv = ADAM_B2 * v_ref[...] + (1.0 - ADAM_B2) * (gg * gg)
        g_ref[...] = gg
        d_ref[...] = -ADAM_LR * ((nm * c1) / (jnp.sqrt(nv * c2) + ADAM_EPS) + ADAM_WD * w_ref[...])
        nm_ref[...] = nm
        nv_ref[...] = nv

    blk = pl.BlockSpec((tr, cols), lambda li, i: (li * nb + i, 0))
    p_specs = [pl.BlockSpec((N_DEV, tr, cols), functools.partial(lambda li, i, l: (0, jnp.where(li == l, i, 0), 0), l=l))
               for l in range(nl)]
    res = pl.pallas_call(
        body, name=name, grid=(nl, nb),
        in_specs=p_specs + [blk] * 3, out_specs=[blk] * 4,
        out_shape=[jax.ShapeDtypeStruct((rows, cols), F32)] * 4,
        compiler_params=_cparams("parallel", "parallel"),
    )(*[p.reshape(N_DEV, rl, cols) for p in pieces], *[a.reshape(rows, cols) for a in (w, m, v)])
    return [r.reshape(shape) for r in res]


def _to_shards(full, axis):
    shp = full.shape
    a = full.reshape(shp[:axis] + (N_DEV, shp[axis] // N_DEV) + shp[axis + 1:])
    return jnp.moveaxis(a, axis, 0)


def _from_shards(blocks, axis):
    a = jnp.moveaxis(blocks, 0, axis)
    shp = a.shape
    return a.reshape(shp[:axis] + (shp[axis] * shp[axis + 1],) + shp[axis + 2:])


def kernel(x, meta_tokens, ln0_g, ln0_b, w_in, q_norm, kv_norm, w_uq, w_uk, w_uv, w_o_mla, lru_conv_w, lru_conv_b, w_rg, b_rg, w_ig, b_ig, lru_lambda, w_o_lru, w_out, ln1_g, ln1_b, w_up, ffn_conv_w, ffn_conv_b, w_down, ln2_g, ln2_b, loss_target, m_meta_tokens, m_ln0_g, m_ln0_b, m_w_in, m_q_norm, m_kv_norm, m_w_uq, m_w_uk, m_w_uv, m_w_o_mla, m_lru_conv_w, m_lru_conv_b, m_w_rg, m_b_rg, m_w_ig, m_b_ig, m_lru_lambda, m_w_o_lru, m_w_out, m_ln1_g, m_ln1_b, m_w_up, m_ffn_conv_w, m_ffn_conv_b, m_w_down, m_ln2_g, m_ln2_b, v_meta_tokens, v_ln0_g, v_ln0_b, v_w_in, v_q_norm, v_kv_norm, v_w_uq, v_w_uk, v_w_uv, v_w_o_mla, v_lru_conv_w, v_lru_conv_b, v_w_rg, v_b_rg, v_w_ig, v_b_ig, v_lru_lambda, v_w_o_lru, v_w_out, v_ln1_g, v_ln1_b, v_w_up, v_ffn_conv_w, v_ffn_conv_b, v_w_down, v_ln2_g, v_ln2_b):
    args = (meta_tokens, ln0_g, ln0_b, w_in, q_norm, kv_norm, w_uq, w_uk, w_uv, w_o_mla, lru_conv_w, lru_conv_b, w_rg, b_rg, w_ig, b_ig, lru_lambda, w_o_lru, w_out, ln1_g, ln1_b, w_up, ffn_conv_w, ffn_conv_b, w_down, ln2_g, ln2_b)
    ms = (m_meta_tokens, m_ln0_g, m_ln0_b, m_w_in, m_q_norm, m_kv_norm, m_w_uq, m_w_uk, m_w_uv, m_w_o_mla, m_lru_conv_w, m_lru_conv_b, m_w_rg, m_b_rg, m_w_ig, m_b_ig, m_lru_lambda, m_w_o_lru, m_w_out, m_ln1_g, m_ln1_b, m_w_up, m_ffn_conv_w, m_ffn_conv_b, m_w_down, m_ln2_g, m_ln2_b)
    vs = (v_meta_tokens, v_ln0_g, v_ln0_b, v_w_in, v_q_norm, v_kv_norm, v_w_uq, v_w_uk, v_w_uv, v_w_o_mla, v_lru_conv_w, v_lru_conv_b, v_w_rg, v_b_rg, v_w_ig, v_b_ig, v_lru_lambda, v_w_o_lru, v_w_out, v_ln1_g, v_ln1_b, v_w_up, v_ffn_conv_w, v_ffn_conv_b, v_w_down, v_ln2_g, v_ln2_b)
    wd, md, vd = dict(zip(WEIGHTS, args)), dict(zip(WEIGHTS, ms)), dict(zip(WEIGHTS, vs))

    def shard_axis(n, l):
        return SHARD_AXIS[n] - (0 if l is None else 1)

    def shard(n, l):
        a = wd[n] if l is None else wd[n][l]
        return a.astype(BF16) if n in BIG else a

    first = [('meta_tokens', None), ('w_in', 0)]
    staged = [[(n, 0) for n in names] for names in STAGE_WEIGHTS]
    later = [(n, 1) for n in SHARDED if n != 'meta_tokens']
    gather, token = _exchange_start([[(shard(*k), False) for k in keys] for keys in [first] + staged + [later]], "gather_start")

    def arrive(gi, keys, after, name):
        return {k: _from_shards(b, shard_axis(*k)) for k, b in zip(keys, _exchange_wait(gather[gi], after, name))}

    def layer_weights(got, l, names):
        fl = {n: wd[n][l] for n in names if n in REPLICATED}
        fl.update({n: a for (n, _), a in got.items() if n in names})
        return _layer_weights(fl)

    ln0_g = _after(wd['ln0_g'], token)
    got_first = arrive(0, first, ln0_g, "gather_wait_first")

    def first_layer(h):
        def more(stage, after):
            got = arrive(1 + stage, staged[stage], after, "gather_wait_l0_%d" % stage)
            return layer_weights(got, 0, STAGE_WEIGHTS[stage] + STAGE_REPLICATED[stage])
        return _w_in_kernel(got_first['w_in', 0]), more

    def second_layer(h):
        got = arrive(1 + len(staged), later, h, "gather_wait_l1")
        return _w_in_kernel(got['w_in', 1]), lambda stage, after: layer_weights(got, 1, STAGE_WEIGHTS[stage] + STAGE_REPLICATED[stage])

    sent = []
    pending = []

    def send(l, stage, grads):
        for n, g in grads.items():
            if n in SHARD_AXIS:
                g = _to_shards(g, shard_axis(n, l))
                pending.append(((n, l), (g.astype(BF16) if n in BIG else g, True)))
            else:
                pending.append(((n, l), (g.astype(BF16) if n in LARGE_REPLICATED else g, False)))
        if l == DEPTH - 1 and stage != 'in':
            return None
        (state,), tok = _exchange_start([[it for _, it in pending]], "grads_start_%s_%s" % (l, stage))
        sent.append(([k for k, _ in pending], state))
        pending.clear()
        return tok

    seq = x.shape[1]
    t_pad = -(-(N_META + seq + MIN_PAD_ROWS) // LANE) * LANE
    loss, grad_x = _local_step(x[0], loss_target[0], got_first['meta_tokens', None], ln0_g, wd['ln0_b'],
                               [first_layer, second_layer], t_pad, send)
    loss = lax.psum(loss, ("x", "y", "c"))

    pieces = {}
    for gi, (keys, state) in enumerate(sent):
        pieces.update(zip(keys, _exchange_wait(state, grad_x, "grads_wait_%d" % gi)))
    outs = {}
    for n in WEIGHTS:
        ps = [pieces[n, None]] if (n, None) in pieces else [pieces[n, l] for l in range(DEPTH)]
        outs[n] = _sum_adamw(ps, wd[n], md[n], vd[n], "adamw_" + n)
    res = [loss, grad_x[None]]
    for k in range(4):
        res += [outs[n][k] for n in WEIGHTS]
    return tuple(res)
```

```python
import functools
import math

import jax
import jax.numpy as jnp
from jax import lax
from jax.experimental import pallas as pl
from jax.experimental.pallas import tpu as pltpu

F32 = jnp.float32
BF16 = jnp.bfloat16

N_DEV = 8
D_MODEL = 1024
N_META = 16
HEADS = 8
QK_NOPE = 128
QK_ROPE = 64
V_HEAD = 128
Q_RANK = 256
KV_RANK = 128
ROPE_THETA = 10000.0
LRU_BLOCKS = 8
LRU_C = 8.0
D_FF = 2816
DEPTH = 2
DN_ALPHA = (2.0 * DEPTH) ** 0.25
LN_EPS = 1e-5
RMS_EPS = 1e-6
LN2 = math.log(2.0)
ATT_SCALE = 1.0 / math.sqrt(QK_NOPE + QK_ROPE) / LN2
NEG_BIG = -1e30

ADAM_LR = 0.001
ADAM_B1 = 0.9
ADAM_B2 = 0.999
ADAM_EPS = 1e-08
ADAM_WD = 0.01
ADAM_STEP = 10

MIN_PAD_ROWS = 2
LANE = 128
SUBLANE = 8
VMEM_LIMIT = 56 * 1024 * 1024

PROJ_COLS = 4 * D_MODEL + Q_RANK + KV_RANK + 2 * QK_ROPE
C_LRU_G, C_LRU_X, C_G_MLA, C_G_LRU = 0, D_MODEL, 2 * D_MODEL, 3 * D_MODEL
C_CQ = 4 * D_MODEL
C_CKV = C_CQ + Q_RANK
C_KRP = C_CKV + KV_RANK

WEIGHTS = ['meta_tokens', 'ln0_g', 'ln0_b', 'w_in', 'q_norm', 'kv_norm', 'w_uq', 'w_uk', 'w_uv', 'w_o_mla',
           'lru_conv_w', 'lru_conv_b', 'w_rg', 'b_rg', 'w_ig', 'b_ig', 'lru_lambda', 'w_o_lru', 'w_out',
           'ln1_g', 'ln1_b', 'w_up', 'ffn_conv_w', 'ffn_conv_b', 'w_down', 'ln2_g', 'ln2_b']
SHARD_AXIS = {'meta_tokens': 1, 'w_in': 2, 'w_uq': 1, 'w_o_mla': 1, 'lru_conv_w': 2, 'b_rg': 2, 'b_ig': 2,
              'lru_lambda': 2, 'w_o_lru': 1, 'w_out': 1, 'w_up': 2, 'ffn_conv_w': 2, 'w_down': 1}
BIG = ['w_in', 'w_uq', 'w_o_mla', 'w_o_lru', 'w_out', 'w_up', 'w_down']
SHARDED = [n for n in WEIGHTS if n in SHARD_AXIS]
REPLICATED = [n for n in WEIGHTS if n not in SHARD_AXIS]
LARGE_REPLICATED = ['w_uk', 'w_uv', 'w_rg', 'w_ig']
STAGE_WEIGHTS = [['w_uq', 'lru_conv_w', 'b_rg', 'b_ig', 'lru_lambda'], ['w_o_mla', 'w_o_lru', 'w_out'], ['w_up', 'ffn_conv_w', 'w_down']]
STAGE_REPLICATED = [['q_norm', 'kv_norm', 'w_uk', 'w_uv', 'lru_conv_b', 'w_rg', 'w_ig'], ['ln1_g', 'ln1_b'], ['ffn_conv_b', 'ln2_g', 'ln2_b']]


def _cparams(*sem):
    return pltpu.CompilerParams(dimension_semantics=sem, vmem_limit_bytes=VMEM_LIMIT)


def _tile(n, cap, unit=LANE):
    best = None
    t = unit
    while t <= min(n, cap):
        if n % t == 0:
            best = t
        t += unit
    return n if best is None else best


def _sigmoid(x):
    return 1.0 / (1.0 + jnp.exp(-x))


_GELU_C = math.sqrt(2.0 / math.pi)


def _gelu(x):
    t = jnp.tanh(_GELU_C * (x + 0.044715 * x * x * x))
    return 0.5 * x * (1.0 + t)


def _gelu_and_grad(x):
    t = jnp.tanh(_GELU_C * (x + 0.044715 * x * x * x))
    g = 0.5 * x * (1.0 + t)
    dg = 0.5 * (1.0 + t) + 0.5 * x * (1.0 - t * t) * _GELU_C * (1.0 + 3.0 * 0.044715 * x * x)
    return g, dg


def _softplus_neg(lam):
    z = jnp.exp(-jnp.abs(lam))
    w = 1.0 + z
    log1p = jnp.where(w == 1.0, z, jnp.log(w) * z / (w - 1.0))
    return jnp.maximum(-lam, 0.0) + log1p


def _row_ids(shape, row0=0):
    return lax.broadcasted_iota(jnp.int32, shape, 0) + row0


def _matmul(a, b, name, ta=False, tb=False, out_dtype=F32, tm_cap=1408, tn_cap=1024, tk_cap=2048):
    if ta:
        kdim, m = a.shape
    else:
        m, kdim = a.shape
    if tb:
        n, k2 = b.shape
    else:
        k2, n = b.shape
    assert kdim == k2, (a.shape, b.shape, ta, tb)
    tm, tn, tk = _tile(m, tm_cap), _tile(n, tn_cap), _tile(kdim, tk_cap)
    nk = kdim // tk

    def body(a_ref, b_ref, o_ref, *acc):
        dn = (((0 if ta else 1,), (1 if tb else 0,)), ((), ()))
        part = lax.dot_general(a_ref[...].astype(BF16), b_ref[...].astype(BF16), dn, preferred_element_type=F32)
        if nk == 1:
            o_ref[...] = part.astype(o_ref.dtype)
            return
        acc_ref, k = acc[0], pl.program_id(2)

        @pl.when(k == 0)
        def _():
            acc_ref[...] = part

        @pl.when(k > 0)
        def _():
            acc_ref[...] += part

        @pl.when(k == nk - 1)
        def _():
            o_ref[...] = acc_ref[...].astype(o_ref.dtype)

    a_spec = pl.BlockSpec((tk, tm), lambda i, j, k: (k, i)) if ta else pl.BlockSpec((tm, tk), lambda i, j, k: (i, k))
    b_spec = pl.BlockSpec((tn, tk), lambda i, j, k: (j, k)) if tb else pl.BlockSpec((tk, tn), lambda i, j, k: (k, j))
    return pl.pallas_call(
        body, name=name,
        grid=(m // tm, n // tn, nk),
        in_specs=[a_spec, b_spec],
        out_specs=pl.BlockSpec((tm, tn), lambda i, j, k: (i, j)),
        out_shape=jax.ShapeDtypeStruct((m, n), out_dtype),
        scratch_shapes=[pltpu.VMEM((tm, tn), F32)] if nk > 1 else [],
        compiler_params=_cparams("parallel", "parallel", "arbitrary"),
    )(a, b)


class Rw:
    def __init__(self, arr, width=None, cb=0):
        self.arr, self.width, self.cb = arr, (arr.shape[1] if width is None else width), cb


class Pm:
    def __init__(self, arr):
        self.arr = arr


def _rows(fn, name, ins, outs, accs=(), tm_cap=384):
    tp = next(o.arr.shape[0] for o in ins if isinstance(o, Rw))
    tm = _tile(tp, tm_cap)
    n_in, n_out, n_acc = len(ins), len(outs), len(accs)

    def body(*refs):
        i = pl.program_id(0)
        res = fn(i * tm, *[r[...] for r in refs[:n_in]])
        if not isinstance(res, (tuple, list)):
            res = (res,)
        assert len(res) == n_out + n_acc, (name, len(res))
        for k in range(n_out):
            refs[n_in + k][...] = res[k].astype(refs[n_in + k].dtype)
        for k in range(n_acc):
            ref = refs[n_in + n_out + k]

            @pl.when(i == 0)
            def _():
                ref[...] = jnp.zeros_like(ref)

            ref[...] += res[n_out + k]

    in_specs = []
    for o in ins:
        if isinstance(o, Rw):
            in_specs.append(pl.BlockSpec((tm, o.width), functools.partial(lambda i, cb: (i, cb), cb=o.cb)))
        else:
            in_specs.append(pl.BlockSpec(o.arr.shape, functools.partial(lambda i, nd: (0,) * nd, nd=o.arr.ndim)))
    out_specs = [pl.BlockSpec((tm, w), lambda i: (i, 0)) for (w, _) in outs]
    out_specs += [pl.BlockSpec(s, functools.partial(lambda i, nd: (0,) * nd, nd=len(s))) for s in accs]
    out_shape = [jax.ShapeDtypeStruct((tp, w), dt) for (w, dt) in outs]
    out_shape += [jax.ShapeDtypeStruct(s, F32) for s in accs]
    res = pl.pallas_call(
        body, name=name, grid=(tp // tm,), in_specs=in_specs, out_specs=out_specs, out_shape=out_shape,
        compiler_params=_cparams("arbitrary"),
    )(*[o.arr for o in ins])
    return res


class Cl:
    def __init__(self, arr, col0=0, width=None, stride=1):
        self.arr, self.col0, self.width, self.stride = arr, col0, width, stride


def _cols(fn, name, ins, outs, ncols, tc):
    assert ncols % tc == 0
    n_in, n_out = len(ins), len(outs)
    outs = [(o[0], o[1], o[2] if len(o) > 2 else tc) for o in outs]

    def body(*refs):
        res = fn(*[r[...] for r in refs[:n_in]])
        if not isinstance(res, (tuple, list)):
            res = (res,)
        assert len(res) == n_out, (name, len(res))
        for k in range(n_out):
            refs[n_in + k][...] = res[k].astype(refs[n_in + k].dtype)

    in_specs = []
    for o in ins:
        wd = tc if o.width is None else o.width
        assert o.col0 % wd == 0, (name, o.col0, wd)
        in_specs.append(pl.BlockSpec((o.arr.shape[0], wd),
                                     functools.partial(lambda j, off, st: (0, st * j + off), off=o.col0 // wd, st=o.stride)))
    out_specs = [pl.BlockSpec((r, wd), lambda j: (0, j)) for (r, _, wd) in outs]
    out_shape = [jax.ShapeDtypeStruct((r, ncols // tc * wd), dt) for (r, dt, wd) in outs]
    return pl.pallas_call(
        body, name=name, grid=(ncols // tc,), in_specs=in_specs, out_specs=out_specs, out_shape=out_shape,
        compiler_params=_cparams("parallel"),
    )(*[o.arr for o in ins])


def _ln_stats(u):
    mu = jnp.mean(u, axis=-1, keepdims=True)
    xc = u - mu
    var = jnp.mean(xc * xc, axis=-1, keepdims=True)
    rstd = lax.rsqrt(var + LN_EPS)
    return xc * rstd, rstd


def _ln_fwd(terms, g, b, name):
    coefs = [c for c, _ in terms]

    def fn(row0, *blk):
        xs, (gg, bb) = blk[:len(coefs)], blk[len(coefs):]
        u = sum(c * x for c, x in zip(coefs, xs))
        xhat, _ = _ln_stats(u)
        return xhat * gg + bb

    d = terms[0][1].shape[1]
    return _rows(fn, name, [Rw(x) for _, x in terms] + [Pm(g.reshape(1, d)), Pm(b.reshape(1, d))], [(d, F32)])[0]


def _ln_bwd(dy_terms, u_terms, g, name):
    dc = [c for c, _ in dy_terms]
    uc = [c for c, _ in u_terms]
    d = u_terms[0][1].shape[1]

    def fn(row0, *blk):
        dys = blk[:len(dc)]
        xs = blk[len(dc):len(dc) + len(uc)]
        gg = blk[-1]
        dy = sum(c * x for c, x in zip(dc, dys))
        u = sum(c * x for c, x in zip(uc, xs))
        xhat, rstd = _ln_stats(u)
        gdy = dy * gg
        m1 = jnp.mean(gdy, axis=-1, keepdims=True)
        m2 = jnp.mean(gdy * xhat, axis=-1, keepdims=True)
        du = rstd * (gdy - m1 - xhat * m2)
        return du, jnp.sum(dy * xhat, axis=0, keepdims=True), jnp.sum(dy, axis=0, keepdims=True)

    ins = [Rw(x) for _, x in dy_terms] + [Rw(x) for _, x in u_terms] + [Pm(g.reshape(1, d))]
    return _rows(fn, name, ins, [(d, F32)], accs=[(1, d), (1, d)])


def _loss_head(y, tgt, t_real, name):
    d = y.shape[1]

    def fn(row0, yb, tb):
        rows = _row_ids(yb.shape, row0)
        live = (rows >= N_META) & (rows < t_real)
        diff = jnp.where(live, yb - tb, 0.0)
        return diff * (1.0 / d), jnp.sum(diff * diff, axis=0, keepdims=True) * (0.5 / d)

    return _rows(fn, name, [Rw(y), Rw(tgt)], [(d, F32)], accs=[(1, d)])


def _rms(x, g):
    r = lax.rsqrt(jnp.mean(x * x, axis=-1, keepdims=True) + RMS_EPS)
    return x * r * g


def _rms_bwd(dy, x, g):
    r = lax.rsqrt(jnp.mean(x * x, axis=-1, keepdims=True) + RMS_EPS)
    gdy = dy * g
    dx = r * gdy - x * (r * r * r) * jnp.mean(gdy * x, axis=-1, keepdims=True)
    return dx, jnp.sum(dy * x * r, axis=0, keepdims=True)


def _mla_norms(proj, qn, kvn, name):
    def fn(row0, cq, ckv, g1, g2):
        return _rms(cq, g1), _rms(ckv, g2)

    return _rows(fn, name, [Rw(proj, Q_RANK, C_CQ // Q_RANK), Rw(proj, KV_RANK, C_CKV // KV_RANK),
                            Pm(qn.reshape(1, Q_RANK)), Pm(kvn.reshape(1, KV_RANK))],
                 [(Q_RANK, F32), (KV_RANK, F32)])


def _mla_norms_bwd(dcqn, dckvn, proj, qn, kvn, name):
    def fn(row0, d1, d2, cq, ckv, g1, g2):
        dx1, dg1 = _rms_bwd(d1, cq, g1)
        dx2, dg2 = _rms_bwd(d2, ckv, g2)
        return dx1, dx2, dg1, dg2

    return _rows(fn, name, [Rw(dcqn), Rw(dckvn), Rw(proj, Q_RANK, C_CQ // Q_RANK), Rw(proj, KV_RANK, C_CKV // KV_RANK),
                            Pm(qn.reshape(1, Q_RANK)), Pm(kvn.reshape(1, KV_RANK))],
                 [(Q_RANK, F32), (KV_RANK, F32)], accs=[(1, Q_RANK), (1, KV_RANK)])


def _fold_rope(z):
    return z + pltpu.roll(z, QK_ROPE, 1)


def _mla_pack(qext, kv, proj, cs, name):
    tp = qext.shape[0]
    tm = _tile(tp, 384)

    def body(q_ref, kn_ref, v_ref, kr_ref, cs_ref, qo_ref, ko_ref, vo_ref):
        cs_ = cs_ref[...]
        low = lax.broadcasted_iota(jnp.int32, cs_.shape, 1) < QK_ROPE
        q = q_ref[...]
        qr = jnp.where(low, _fold_rope(q[:, QK_NOPE:] * cs_), 0.0)
        qo_ref[...] = (jnp.concatenate([q[:, :QK_NOPE], qr], axis=1) * ATT_SCALE).astype(BF16)
        kr = _fold_rope(kr_ref[...] * cs_)
        ko_ref[...] = jnp.concatenate([kn_ref[...], kr], axis=1).astype(BF16)
        vo_ref[...] = v_ref[...].astype(BF16)

    return pl.pallas_call(
        body, name=name, grid=(tp // tm, HEADS),
        in_specs=[pl.BlockSpec((tm, 2 * LANE), lambda i, h: (i, h)),
                  pl.BlockSpec((tm, LANE), lambda i, h: (i, h)),
                  pl.BlockSpec((tm, LANE), lambda i, h: (i, HEADS + h)),
                  pl.BlockSpec((tm, LANE), lambda i, h: (i, C_KRP // LANE)),
                  pl.BlockSpec((tm, LANE), lambda i, h: (i, 0))],
        out_specs=[pl.BlockSpec((tm, 2 * LANE), lambda i, h: (i, h)),
                   pl.BlockSpec((tm, 2 * LANE), lambda i, h: (i, h)),
                   pl.BlockSpec((tm, LANE), lambda i, h: (i, h))],
        out_shape=[jax.ShapeDtypeStruct((tp, HEADS * 2 * LANE), BF16),
                   jax.ShapeDtypeStruct((tp, HEADS * 2 * LANE), BF16),
                   jax.ShapeDtypeStruct((tp, HEADS * LANE), BF16)],
        compiler_params=_cparams("parallel", "arbitrary"),
    )(qext, kv, kv, proj, cs)


def _mla_unpack(dq, dk, cs, name):
    tp = dq.shape[0]
    tm = _tile(tp, 384)

    def body(dq_ref, dk_ref, cs_ref, dqe_ref, dkn_ref, dkr_ref):
        h = pl.program_id(1)
        cs_ = cs_ref[...]
        low = lax.broadcasted_iota(jnp.int32, cs_.shape, 1) < QK_ROPE
        dq_ = dq_ref[...] * ATT_SCALE
        dqr = _fold_rope(jnp.where(low, dq_[:, QK_NOPE:], 0.0)) * cs_
        dqe_ref[...] = jnp.concatenate([dq_[:, :QK_NOPE], dqr], axis=1)
        dk_ = dk_ref[...]
        dkn_ref[...] = dk_[:, :QK_NOPE]

        @pl.when(h == 0)
        def _():
            dkr_ref[...] = jnp.zeros_like(dkr_ref)

        dkr_ref[...] += _fold_rope(jnp.where(low, dk_[:, QK_NOPE:], 0.0)) * cs_

    return pl.pallas_call(
        body, name=name, grid=(tp // tm, HEADS),
        in_specs=[pl.BlockSpec((tm, 2 * LANE), lambda i, h: (i, h)),
                  pl.BlockSpec((tm, 2 * LANE), lambda i, h: (i, h)),
                  pl.BlockSpec((tm, LANE), lambda i, h: (i, 0))],
        out_specs=[pl.BlockSpec((tm, 2 * LANE), lambda i, h: (i, h)),
                   pl.BlockSpec((tm, LANE), lambda i, h: (i, h)),
                   pl.BlockSpec((tm, LANE), lambda i, h: (i, 0))],
        out_shape=[jax.ShapeDtypeStruct((tp, HEADS * 2 * LANE), F32),
                   jax.ShapeDtypeStruct((tp, HEADS * LANE), F32),
                   jax.ShapeDtypeStruct((tp, LANE), F32)],
        compiler_params=_cparams("parallel", "arbitrary"),
    )(dq, dk, cs)


def _attn_fwd(q, k, v, t_real, name):
    tp = q.shape[0]
    tq = _tile(tp, 384)
    tkc = _tile(tp, 1408)
    nkc = -(-t_real // tkc)

    def body(q_ref, k_ref, v_ref, o_ref, lse_ref):
        qb = q_ref[...]
        m = l = acc = None
        for c in range(nkc):
            s = lax.dot_general(qb, k_ref[c * tkc:(c + 1) * tkc, :], (((1,), (1,)), ((), ())), preferred_element_type=F32)
            if (c + 1) * tkc > t_real:
                cols = lax.broadcasted_iota(jnp.int32, s.shape, 1) + c * tkc
                s = jnp.where(cols < t_real, s, NEG_BIG)
            mc = jnp.max(s, axis=-1, keepdims=True)
            m_new = mc if c == 0 else jnp.maximum(m, mc)
            p = jnp.exp2(s - m_new)
            lc = jnp.sum(p, axis=-1, keepdims=True)
            pv = jnp.dot(p.astype(BF16), v_ref[c * tkc:(c + 1) * tkc, :], preferred_element_type=F32)
            if c == 0:
                l, acc = lc, pv
            else:
                alpha = jnp.exp2(m - m_new)
                l, acc = alpha * l + lc, alpha * acc + pv
            m = m_new
        o_ref[...] = acc / l
        lse_ref[...] = m + jnp.log2(l)

    return pl.pallas_call(
        body, name=name, grid=(HEADS, tp // tq),
        in_specs=[pl.BlockSpec((tq, 2 * LANE), lambda h, i: (i, h)),
                  pl.BlockSpec((tp, 2 * LANE), lambda h, i: (0, h)),
                  pl.BlockSpec((tp, LANE), lambda h, i: (0, h))],
        out_specs=[pl.BlockSpec((tq, LANE), lambda h, i: (i, h)),
                   pl.BlockSpec((None, tq, 1), lambda h, i: (h, i, 0))],
        out_shape=[jax.ShapeDtypeStruct((tp, HEADS * LANE), F32),
                   jax.ShapeDtypeStruct((HEADS, tp, 1), F32)],
        compiler_params=_cparams("parallel", "parallel"),
    )(q, k, v)


def _attn_bwd(q, k, v, do, o, lse, t_real, name):
    tp = q.shape[0]
    tq = _tile(tp, 384)
    tkc = _tile(tp, 1408)
    nkc = -(-t_real // tkc)

    def body(q_ref, k_ref, v_ref, do_ref, o_ref, lse_ref, dq_ref, dk_ref, dv_ref):
        i = pl.program_id(1)

        @pl.when(i == 0)
        def _():
            dk_ref[...] = jnp.zeros_like(dk_ref)
            dv_ref[...] = jnp.zeros_like(dv_ref)

        qb = q_ref[...]
        dob = do_ref[...]
        dob16 = dob.astype(BF16)
        dol2 = (dob * LN2).astype(BF16)
        delta = jnp.sum(dob * o_ref[...], axis=-1, keepdims=True) * LN2
        lse = lse_ref[...]
        dq = None
        for c in range(nkc):
            ks = slice(c * tkc, (c + 1) * tkc)
            kb = k_ref[ks, :]
            s = lax.dot_general(qb, kb, (((1,), (1,)), ((), ())), preferred_element_type=F32)
            p = jnp.exp2(s - lse)
            if (c + 1) * tkc > t_real:
                cols = lax.broadcasted_iota(jnp.int32, s.shape, 1) + c * tkc
                p = jnp.where(cols < t_real, p, 0.0)
            dp = lax.dot_general(dol2, v_ref[ks, :], (((1,), (1,)), ((), ())), preferred_element_type=F32)
            ds = (p * (dp - delta)).astype(BF16)
            dqc = jnp.dot(ds, kb, preferred_element_type=F32)
            dq = dqc if c == 0 else dq + dqc
            dk_ref[ks, :] += lax.dot_general(ds, qb, (((0,), (0,)), ((), ())), preferred_element_type=F32)
            dv_ref[ks, :] += lax.dot_general(p.astype(BF16), dob16, (((0,), (0,)), ((), ())), preferred_element_type=F32)
        dq_ref[...] = dq

    return pl.pallas_call(
        body, name=name, grid=(HEADS, tp // tq),
        in_specs=[pl.BlockSpec((tq, 2 * LANE), lambda h, i: (i, h)),
                  pl.BlockSpec((tp, 2 * LANE), lambda h, i: (0, h)),
                  pl.BlockSpec((tp, LANE), lambda h, i: (0, h)),
                  pl.BlockSpec((tq, LANE), lambda h, i: (i, h)),
                  pl.BlockSpec((tq, LANE), lambda h, i: (i, h)),
                  pl.BlockSpec((None, tq, 1), lambda h, i: (h, i, 0))],
        out_specs=[pl.BlockSpec((tq, 2 * LANE), lambda h, i: (i, h)),
                   pl.BlockSpec((tp, 2 * LANE), lambda h, i: (0, h)),
                   pl.BlockSpec((tp, LANE), lambda h, i: (0, h))],
        out_shape=[jax.ShapeDtypeStruct((tp, HEADS * 2 * LANE), F32),
                   jax.ShapeDtypeStruct((tp, HEADS * 2 * LANE), F32),
                   jax.ShapeDtypeStruct((tp, HEADS * LANE), F32)],
        compiler_params=_cparams("parallel", "arbitrary"),
    )(q, k, v, do, o, lse)


def _shift_rows(x, s):
    tp = x.shape[0]
    return x if s % tp == 0 else pltpu.roll(x, s % tp, 0)


def _conv_fwd_val(xm, w, b, pad_left):
    acc = b + w[0:1, :] * _shift_rows(xm, pad_left)
    for k in range(1, w.shape[0]):
        acc = acc + w[k:k + 1, :] * _shift_rows(xm, pad_left - k)
    return acc


def _conv_bwd_val(dy, xm, w, pad_left, live):
    kk = w.shape[0]
    dx = w[0:1, :] * _shift_rows(dy, -pad_left)
    dws = [jnp.sum(dy * _shift_rows(xm, pad_left), axis=0, keepdims=True)]
    for k in range(1, kk):
        dx = dx + w[k:k + 1, :] * _shift_rows(dy, k - pad_left)
        dws.append(jnp.sum(dy * _shift_rows(xm, pad_left - k), axis=0, keepdims=True))
    return jnp.where(live, dx, 0.0), jnp.concatenate(dws, axis=0), jnp.sum(dy, axis=0, keepdims=True)


def _lru_conv_fwd(proj, w, b, t_real, name):
    def fn(x, ww, bb):
        xm = jnp.where(_row_ids(x.shape) < t_real, x, 0.0)
        return _conv_fwd_val(xm, ww, bb, 2)

    return _cols(fn, name, [Cl(proj, C_LRU_X), Cl(w), Cl(b.reshape(1, -1))], [(proj.shape[0], F32)], D_MODEL, 128)[0]


def _lru_conv_bwd(dxc, proj, w, t_real, name):
    def fn(dy, x, ww):
        live = _row_ids(x.shape) < t_real
        xm = jnp.where(live, x, 0.0)
        dym = jnp.where(live, dy, 0.0)
        return _conv_bwd_val(dym, xm, ww, 2, live)

    return _cols(fn, name, [Cl(dxc), Cl(proj, C_LRU_X), Cl(w)],
                 [(proj.shape[0], F32), (w.shape[0], F32), (1, F32)], D_MODEL, 128)


def _pair_columns(a):
    lead = a.shape[:-1]
    return jnp.swapaxes(a.reshape(lead + (2, D_FF // LANE, LANE)), -3, -2).reshape(lead + (2 * D_FF,))


def _unpair_columns(a):
    lead = a.shape[:-1]
    return jnp.swapaxes(a.reshape(lead + (D_FF // LANE, 2, LANE)), -3, -2).reshape(lead + (2 * D_FF,))


def _ffn_conv_act(up, w, b, t_real, name):
    def fn(u, ww, bb):
        c = _conv_fwd_val(jnp.where(_row_ids(u.shape) < t_real, u, 0.0), ww, bb, 1)
        return _gelu(c[:, :LANE]) * c[:, LANE:]

    return _cols(fn, name, [Cl(up, 0, 2 * LANE), Cl(w, 0, 2 * LANE), Cl(b.reshape(1, -1), 0, 2 * LANE)],
                 [(up.shape[0], F32)], D_FF, LANE)[0]


def _ffn_conv_act_bwd(dm, up, w, b, t_real, name):
    def fn(dmb, g, v, wg, wv, bg, bv):
        live = _row_ids(g.shape) < t_real
        gm, vm = jnp.where(live, g, 0.0), jnp.where(live, v, 0.0)
        gc = _conv_fwd_val(gm, wg, bg, 1)
        vc = _conv_fwd_val(vm, wv, bv, 1)
        act, dact = _gelu_and_grad(gc)
        dmm = jnp.where(live, dmb, 0.0)
        dgx, dwg, dbg = _conv_bwd_val(dmm * vc * dact, gm, wg, 1, live)
        dvx, dwv, dbv = _conv_bwd_val(dmm * act, vm, wv, 1, live)
        return tuple(jnp.concatenate(pair, axis=1) for pair in ((dgx, dvx), (dwg, dwv), (dbg, dbv)))

    tp, kk = up.shape[0], w.shape[0]
    b2 = b.reshape(1, -1)
    return _cols(fn, name, [Cl(dm), Cl(up, 0, LANE, 2), Cl(up, LANE, LANE, 2), Cl(w, 0, LANE, 2), Cl(w, LANE, LANE, 2),
                            Cl(b2, 0, LANE, 2), Cl(b2, LANE, LANE, 2)],
                 [(tp, F32, 2 * LANE), (kk, F32, 2 * LANE), (1, F32, 2 * LANE)], D_FF, LANE)


def _lru_gates_fwd(xc, wg, b4, lam, t_real, name):
    tp = xc.shape[0]
    tm = _tile(tp, 1408)

    def body(x_ref, w_ref, b_ref, lam_ref, r0_ref, r1_ref, i0_ref, i1_ref, a0_ref, a1_ref, u0_ref, u1_ref):
        x = x_ref[...]
        xb = x.astype(BF16)
        live = _row_ids(x.shape, pl.program_id(1) * tm) < t_real
        bb = b_ref[...]
        sp = _softplus_neg(lam_ref[...])
        gate = [_sigmoid(jnp.dot(xb, w_ref[k], preferred_element_type=F32) + bb[k:k + 1, :]) for k in range(4)]
        for d, (r_ref, i_ref, a_ref, u_ref) in enumerate(((r0_ref, i0_ref, a0_ref, u0_ref), (r1_ref, i1_ref, a1_ref, u1_ref))):
            r, ig = gate[d], gate[2 + d]
            a = jnp.exp(-LRU_C * r * sp[d:d + 1, :])
            r_ref[...] = r
            i_ref[...] = ig
            a_ref[...] = a
            u_ref[...] = jnp.where(live, jnp.sqrt(1.0 - a * a) * (ig * x), 0.0)

    blk = pl.BlockSpec((tm, LANE), lambda g, i: (i, g))
    return pl.pallas_call(
        body, name=name, grid=(LRU_BLOCKS, tp // tm),
        in_specs=[blk, pl.BlockSpec((None, 4, LANE, LANE), lambda g, i: (g, 0, 0, 0)),
                  pl.BlockSpec((4, LANE), lambda g, i: (0, g)), pl.BlockSpec((2, LANE), lambda g, i: (0, g))],
        out_specs=[blk] * 8,
        out_shape=[jax.ShapeDtypeStruct((tp, D_MODEL), F32)] * 8,
        compiler_params=_cparams("parallel", "parallel"),
    )(xc, wg, b4, lam)


def _lru_gates_bwd(l0, l1, da0, da1, r0, r1, i0, i1, a0, a1, xc, wg, lam, t_real, name):
    tp = xc.shape[0]
    tm = _tile(tp, 1408)

    def body(l0_ref, l1_ref, da0_ref, da1_ref, r0_ref, r1_ref, i0_ref, i1_ref, a0_ref, a1_ref, x_ref, w_ref, lam_ref,
             dx_ref, dw_ref, db_ref, dlam_ref):
        i = pl.program_id(1)
        x = x_ref[...]
        xb = x.astype(BF16)
        live = _row_ids(x.shape, i * tm) < t_real
        lam_ = lam_ref[...]
        sp = _softplus_neg(lam_)
        dsp_dlam = -_sigmoid(-lam_)
        dx = jnp.zeros_like(x)
        dpre = [None] * 4
        dlam_rows = []
        for d, (l_ref, da_ref, r_ref, i_ref, a_ref) in enumerate(((l0_ref, da0_ref, r0_ref, i0_ref, a0_ref),
                                                                  (l1_ref, da1_ref, r1_ref, i1_ref, a1_ref))):
            r, ig, a = r_ref[...], i_ref[...], a_ref[...]
            du = jnp.where(live, l_ref[...], 0.0)
            s = jnp.sqrt(1.0 - a * a)
            dv = du * s
            ds = du * (ig * x)
            dla = jnp.where(live, da_ref[...], 0.0) * a - ds * (a * a) / s
            dla = jnp.where(live, dla, 0.0)
            dr = dla * (-LRU_C) * sp[d:d + 1, :]
            dlam_rows.append(jnp.sum(dla * (-LRU_C) * r, axis=0, keepdims=True) * dsp_dlam[d:d + 1, :])
            dpre[d] = dr * r * (1.0 - r)
            dpre[2 + d] = dv * x * ig * (1.0 - ig)
            dx = dx + dv * ig

        @pl.when(i == 0)
        def _():
            dw_ref[...] = jnp.zeros_like(dw_ref)
            db_ref[...] = jnp.zeros_like(db_ref)
            dlam_ref[...] = jnp.zeros_like(dlam_ref)

        for k in range(4):
            pk = dpre[k].astype(BF16)
            dx = dx + lax.dot_general(pk, w_ref[k], (((1,), (1,)), ((), ())), preferred_element_type=F32)
            dw_ref[k] += lax.dot_general(xb, pk, (((0,), (0,)), ((), ())), preferred_element_type=F32)
        db_ref[...] += jnp.concatenate([jnp.sum(p, axis=0, keepdims=True) for p in dpre], axis=0)
        dlam_ref[...] += jnp.concatenate(dlam_rows, axis=0)
        dx_ref[...] = dx

    blk = pl.BlockSpec((tm, LANE), lambda g, i: (i, g))
    return pl.pallas_call(
        body, name=name, grid=(LRU_BLOCKS, tp // tm),
        in_specs=[blk] * 11 + [pl.BlockSpec((None, 4, LANE, LANE), lambda g, i: (g, 0, 0, 0)),
                               pl.BlockSpec((2, LANE), lambda g, i: (0, g))],
        out_specs=[blk, pl.BlockSpec((None, 4, LANE, LANE), lambda g, i: (g, 0, 0, 0)),
                   pl.BlockSpec((4, LANE), lambda g, i: (0, g)), pl.BlockSpec((2, LANE), lambda g, i: (0, g))],
        out_shape=[jax.ShapeDtypeStruct((tp, D_MODEL), F32), jax.ShapeDtypeStruct((LRU_BLOCKS, 4, LANE, LANE), F32),
                   jax.ShapeDtypeStruct((4, D_MODEL), F32), jax.ShapeDtypeStruct((2, D_MODEL), F32)],
        compiler_params=_cparams("parallel", "arbitrary"),
    )(l0, l1, da0, da1, r0, r1, i0, i1, a0, a1, xc, wg, lam)


def _tile_scan(a, u, reverse):
    rows = lax.broadcasted_iota(jnp.int32, a.shape, 0)
    for s in (1, 2, 4):
        if reverse:
            keep = rows < SUBLANE - s
            a_sh, u_sh = pltpu.roll(a, SUBLANE - s, 0), pltpu.roll(u, SUBLANE - s, 0)
        else:
            keep = rows >= s
            a_sh, u_sh = pltpu.roll(a, s, 0), pltpu.roll(u, s, 0)
        u = u + a * jnp.where(keep, u_sh, 0.0)
        a = a * jnp.where(keep, a_sh, 1.0)
    return a, u


def _scan_fwd(a0, u0, a1, u1, name):
    tp, d = a0.shape
    tc = 128
    nt = tp // SUBLANE

    def body(a0_ref, u0_ref, a1_ref, u1_ref, h0_ref, h1_ref):
        def step(t, carry):
            c0, c1 = carry
            f = pl.multiple_of(t * SUBLANE, SUBLANE)
            b = pl.multiple_of((nt - 1 - t) * SUBLANE, SUBLANE)
            pa, pu = _tile_scan(a0_ref[pl.ds(f, SUBLANE), :], u0_ref[pl.ds(f, SUBLANE), :], False)
            h = pu + pa * c0
            h0_ref[pl.ds(f, SUBLANE), :] = h
            c0 = h[SUBLANE - 1:SUBLANE, :]
            pa, pu = _tile_scan(a1_ref[pl.ds(b, SUBLANE), :], u1_ref[pl.ds(b, SUBLANE), :], True)
            h = pu + pa * c1
            h1_ref[pl.ds(b, SUBLANE), :] = h
            c1 = h[0:1, :]
            return c0, c1

        z = jnp.zeros((1, tc), F32)
        lax.fori_loop(0, nt, step, (z, z))

    blk = pl.BlockSpec((tp, tc), lambda j: (0, j))
    return pl.pallas_call(
        body, name=name, grid=(d // tc,), in_specs=[blk] * 4, out_specs=[blk] * 2,
        out_shape=[jax.ShapeDtypeStruct((tp, d), F32)] * 2,
        compiler_params=_cparams("parallel"),
    )(a0, u0, a1, u1)


def _scan_bwd(dh, a0, a1, h0, h1, name):
    tp, d = dh.shape
    tc = 128
    nt = tp // SUBLANE

    def body(dh_ref, a0_ref, a1_ref, h0_ref, h1_ref, l0_ref, l1_ref, da0_ref, da1_ref):
        rows8 = lax.broadcasted_iota(jnp.int32, (SUBLANE, tc), 0)

        def step(t, carry):
            c0, c1 = carry
            b = pl.multiple_of((nt - 1 - t) * SUBLANE, SUBLANE)
            f = pl.multiple_of(t * SUBLANE, SUBLANE)
            a = a0_ref[pl.ds(b, SUBLANE), :]
            a_next = jnp.where(rows8 < SUBLANE - 1, pltpu.roll(a, SUBLANE - 1, 0), 1.0)
            pa, pu = _tile_scan(a_next, dh_ref[pl.ds(b, SUBLANE), :], True)
            lam = pu + pa * c0
            l0_ref[pl.ds(b, SUBLANE), :] = lam
            c0 = a[0:1, :] * lam[0:1, :]
            a = a1_ref[pl.ds(f, SUBLANE), :]
            a_prev = jnp.where(rows8 >= 1, pltpu.roll(a, 1, 0), 1.0)
            pa, pu = _tile_scan(a_prev, dh_ref[pl.ds(f, SUBLANE), :], False)
            lam = pu + pa * c1
            l1_ref[pl.ds(f, SUBLANE), :] = lam
            c1 = a[SUBLANE - 1:SUBLANE, :] * lam[SUBLANE - 1:SUBLANE, :]
            return c0, c1

        z = jnp.zeros((1, tc), F32)
        lax.fori_loop(0, nt, step, (z, z))
        rows = lax.broadcasted_iota(jnp.int32, (tp, tc), 0)
        da0_ref[...] = l0_ref[...] * jnp.where(rows >= 1, pltpu.roll(h0_ref[...], 1, 0), 0.0)
        da1_ref[...] = l1_ref[...] * jnp.where(rows < tp - 1, pltpu.roll(h1_ref[...], tp - 1, 0), 0.0)

    blk = pl.BlockSpec((tp, tc), lambda j: (0, j))
    return pl.pallas_call(
        body, name=name, grid=(d // tc,), in_specs=[blk] * 5, out_specs=[blk] * 4,
        out_shape=[jax.ShapeDtypeStruct((tp, d), F32)] * 4,
        compiler_params=_cparams("parallel"),
    )(dh, a0, a1, h0, h1)


def _gated_h(proj, h0, h1, name):
    def fn(row0, lg, x0, x1):
        return _gelu(lg) * (x0 + x1)

    return _rows(fn, name, [Rw(proj, D_MODEL, C_LRU_G // D_MODEL), Rw(h0), Rw(h1)], [(D_MODEL, F32)])[0]


def _gated_h_bwd(dgh, proj, h0, h1, name):
    def fn(row0, dg, lg, x0, x1):
        act, dact = _gelu_and_grad(lg)
        return dg * (x0 + x1) * dact, dg * act

    return _rows(fn, name, [Rw(dgh), Rw(proj, D_MODEL, C_LRU_G // D_MODEL), Rw(h0), Rw(h1)], [(D_MODEL, F32)] * 2)


def _mix(proj, y_mla, y_lru, name):
    def fn(row0, gm, gl, ym, yl):
        return _sigmoid(gm) * ym + _sigmoid(gl) * yl

    return _rows(fn, name, [Rw(proj, D_MODEL, C_G_MLA // D_MODEL), Rw(proj, D_MODEL, C_G_LRU // D_MODEL), Rw(y_mla), Rw(y_lru)],
                 [(D_MODEL, F32)])[0]


def _mix_bwd(dz, proj, y_mla, y_lru, name):
    def fn(row0, dzb, gm, gl, ym, yl):
        sm, sl = _sigmoid(gm), _sigmoid(gl)
        return dzb * sm, dzb * sl, dzb * ym * sm * (1.0 - sm), dzb * yl * sl * (1.0 - sl)

    return _rows(fn, name, [Rw(dz), Rw(proj, D_MODEL, C_G_MLA // D_MODEL), Rw(proj, D_MODEL, C_G_LRU // D_MODEL),
                            Rw(y_mla), Rw(y_lru)], [(D_MODEL, F32)] * 4)


def _layer_fwd(h, w_in, more_weights, cs, t_real, tag):
    proj = _matmul(h, w_in, tag + "proj")
    w = dict(more_weights(0, proj), w_in=w_in)
    cqn, ckvn = _mla_norms(proj, w['q_norm'], w['kv_norm'], tag + "mla_norms")
    qext = _matmul(cqn, w['w_q'], tag + "q_up")
    kv = _matmul(ckvn, w['w_kv'], tag + "kv_up")
    qc, kc, vb = _mla_pack(qext, kv, proj, cs, tag + "mla_pack")
    o, lse = _attn_fwd(qc, kc, vb, t_real, tag + "attn_fwd")
    w.update(more_weights(1, o))
    y_mla = _matmul(o, w['w_o_mla'], tag + "o_mla")
    xc = _lru_conv_fwd(proj, w['lru_conv_w'], w['lru_conv_b'], t_real, tag + "lru_conv")
    r0, r1, i0, i1, a0, a1, u0, u1 = _lru_gates_fwd(xc, w['w_g'], w['b4'], w['lru_lambda'], t_real, tag + "lru_gates")
    h0, h1 = _scan_fwd(a0, u0, a1, u1, tag + "lru_scan")
    gh = _gated_h(proj, h0, h1, tag + "lru_gate_out")
    y_lru = _matmul(gh, w['w_o_lru'], tag + "o_lru")
    z = _mix(proj, y_mla, y_lru, tag + "mix")
    zo = _matmul(z, w['w_out'], tag + "w_out")
    hm = _ln_fwd([(DN_ALPHA, h), (1.0, zo)], w['ln1_g'], w['ln1_b'], tag + "ln1")
    w.update(more_weights(2, hm))
    up = _matmul(hm, w['w_up'], tag + "w_up")
    m = _ffn_conv_act(up, w['ffn_conv_w'], w['ffn_conv_b'], t_real, tag + "ffn_conv")
    f = _matmul(m, w['w_down'], tag + "w_down", tk_cap=1408)
    out = _ln_fwd([(DN_ALPHA, hm), (1.0, f)], w['ln2_g'], w['ln2_b'], tag + "ln2")
    saved = dict(w=w, h=h, proj=proj, cqn=cqn, ckvn=ckvn, qc=qc, kc=kc, vb=vb, o=o, lse=lse, y_mla=y_mla, xc=xc,
                 r0=r0, r1=r1, i0=i0, i1=i1, a0=a0, a1=a1, h0=h0, h1=h1, gh=gh, y_lru=y_lru, z=z, zo=zo, hm=hm,
                 up=up, m=m, f=f)
    return out, saved


DW_MATMUL = dict(ta=True, out_dtype=BF16, tk_cap=1408)


def _after(a, tok):
    return a if tok is None else a + tok.astype(a.dtype)


def _layer_bwd(dout_terms, s, cs, t_real, tag, emit, tok):
    w = s['w']
    g = {}
    du2, dg2, db2 = _ln_bwd(dout_terms, [(DN_ALPHA, s['hm']), (1.0, s['f'])], _after(w['ln2_g'], tok), tag + "ln2_bwd")
    g['ln2_g'], g['ln2_b'] = dg2, db2
    dm = _matmul(du2, w['w_down'], tag + "w_down_dx", tb=True)
    g['w_down'] = _matmul(s['m'], du2, tag + "w_down_dw", **DW_MATMUL)
    dup, g['ffn_conv_w'], g['ffn_conv_b'] = _ffn_conv_act_bwd(dm, s['up'], w['ffn_conv_w'], w['ffn_conv_b'], t_real, tag + "ffn_conv_bwd")
    dhm_mm = _matmul(dup, w['w_up'], tag + "w_up_dx", tb=True, tk_cap=1408)
    g['w_up'] = _matmul(s['hm'], dup, tag + "w_up_dw", **DW_MATMUL)
    tok = emit('ffn', g)
    g = {}
    du1, dg1, db1 = _ln_bwd([(DN_ALPHA, du2), (1.0, dhm_mm)], [(DN_ALPHA, s['h']), (1.0, s['zo'])],
                            _after(w['ln1_g'], tok), tag + "ln1_bwd")
    g['ln1_g'], g['ln1_b'] = dg1, db1
    dz = _matmul(du1, w['w_out'], tag + "w_out_dx", tb=True)
    g['w_out'] = _matmul(s['z'], du1, tag + "w_out_dw", **DW_MATMUL)
    dy_mla, dy_lru, dg_mla, dg_lru = _mix_bwd(dz, s['proj'], s['y_mla'], s['y_lru'], tag + "mix_bwd")
    do = _matmul(dy_mla, w['w_o_mla'], tag + "o_mla_dx", tb=True)
    g['w_o_mla'] = _matmul(s['o'], dy_mla, tag + "o_mla_dw", **DW_MATMUL)
    dqc, dkc, dv = _attn_bwd(s['qc'], s['kc'], s['vb'], do, s['o'], s['lse'], t_real, tag + "attn_bwd")
    dqext, dkn, dkrp = _mla_unpack(dqc, dkc, cs, tag + "mla_unpack")
    dkv = jnp.concatenate([dkn, dv], axis=1)
    dcqn = _matmul(dqext, w['w_q'], tag + "q_up_dx", tb=True)
    g['w_q'] = _matmul(s['cqn'], dqext, tag + "q_up_dw", **DW_MATMUL)
    dckvn = _matmul(dkv, w['w_kv'], tag + "kv_up_dx", tb=True)
    g['w_kv'] = _matmul(s['ckvn'], dkv, tag + "kv_up_dw", **DW_MATMUL)
    dcq, dckv, g['q_norm'], g['kv_norm'] = _mla_norms_bwd(dcqn, dckvn, s['proj'], w['q_norm'], w['kv_norm'], tag + "mla_norms_bwd")
    dgh = _matmul(dy_lru, w['w_o_lru'], tag + "o_lru_dx", tb=True)
    g['w_o_lru'] = _matmul(s['gh'], dy_lru, tag + "o_lru_dw", **DW_MATMUL)
    dlru_g, dhs = _gated_h_bwd(dgh, s['proj'], s['h0'], s['h1'], tag + "lru_gate_out_bwd")
    l0, l1, da0, da1 = _scan_bwd(dhs, s['a0'], s['a1'], s['h0'], s['h1'], tag + "lru_scan_bwd")
    dxc, g['w_g'], g['b4'], g['lru_lambda'] = _lru_gates_bwd(
        l0, l1, da0, da1, s['r0'], s['r1'], s['i0'], s['i1'], s['a0'], s['a1'], s['xc'], w['w_g'], w['lru_lambda'],
        t_real, tag + "lru_gates_bwd")
    dlru_x, g['lru_conv_w'], g['lru_conv_b'] = _lru_conv_bwd(dxc, s['proj'], w['lru_conv_w'], t_real, tag + "lru_conv_bwd")
    tok = emit('mid', g)
    dproj = jnp.concatenate([dlru_g, dlru_x, dg_mla, dg_lru, dcq, dckv, _after(dkrp, tok)], axis=1)
    tok = emit('in', {'w_in': _matmul(s['h'], dproj, tag + "proj_dw", **DW_MATMUL)})
    dh_mm = _matmul(dproj, w['w_in'], tag + "proj_dx", tb=True, tk_cap=1536)
    return [(DN_ALPHA, du1), (1.0, dh_mm)], tok


def _swap_halves(a, axis=-1):
    h1, h2 = jnp.split(a, 2, axis=axis)
    return jnp.concatenate([h2, h1], axis=axis)


def _w_in_kernel(w_in):
    cq, ckv, kr, lg, lx, gm, gl = jnp.split(w_in, [256, 384, 448, 1472, 2496, 3520], axis=1)
    return jnp.concatenate([lg, lx, gm, gl, cq, ckv, kr, _swap_halves(kr)], axis=1)


def _layer_weights(fl):
    w = {}
    if 'w_uq' in fl:
        uq = fl['w_uq']
        w['w_q'] = jnp.concatenate([uq, _swap_halves(uq[..., QK_NOPE:])], axis=-1).reshape(Q_RANK, HEADS * 2 * LANE)
        w['w_kv'] = jnp.concatenate([fl['w_uk'].reshape(KV_RANK, -1), fl['w_uv'].reshape(KV_RANK, -1)], axis=1).astype(BF16)
        w['w_g'] = jnp.moveaxis(jnp.concatenate([fl['w_rg'], fl['w_ig']], axis=0), 0, 1).astype(BF16)
        w['b4'] = jnp.concatenate([fl['b_rg'], fl['b_ig']], axis=0)
    for n in ('q_norm', 'kv_norm', 'w_o_mla', 'lru_conv_w', 'lru_conv_b', 'lru_lambda', 'w_o_lru', 'w_out', 'ln1_g',
              'ln1_b', 'w_down', 'ln2_g', 'ln2_b'):
        if n in fl:
            w[n] = fl[n]
    for n in ('w_up', 'ffn_conv_w', 'ffn_conv_b'):
        if n in fl:
            w[n] = _pair_columns(fl[n])
    return w


def _layer_grads(g):
    out = {}
    if 'w_in' in g:
        lg, lx, gm, gl, cq, ckv, kr, krs = jnp.split(g['w_in'], [1024, 2048, 3072, 4096, 4352, 4480, 4544], axis=1)
        out['w_in'] = jnp.concatenate([cq, ckv, kr + _swap_halves(krs), lg, lx, gm, gl], axis=1)
    if 'w_q' in g:
        gq = g['w_q'].reshape(Q_RANK, HEADS, 2 * LANE)
        out['w_uq'] = jnp.concatenate([gq[..., :QK_NOPE], gq[..., QK_NOPE:QK_NOPE + QK_ROPE] + _swap_halves(gq[..., QK_NOPE + QK_ROPE:])], axis=-1)
    if 'w_kv' in g:
        out['w_uk'] = g['w_kv'][:, :HEADS * QK_NOPE].reshape(KV_RANK, HEADS, QK_NOPE)
        out['w_uv'] = g['w_kv'][:, HEADS * QK_NOPE:].reshape(KV_RANK, HEADS, V_HEAD)
    if 'w_g' in g:
        gg = jnp.moveaxis(g['w_g'], 1, 0)
        out['w_rg'], out['w_ig'] = gg[:2], gg[2:]
    if 'b4' in g:
        out['b_rg'], out['b_ig'] = g['b4'][:2], g['b4'][2:]
    for n in ('w_up', 'ffn_conv_w', 'ffn_conv_b'):
        if n in g:
            out[n] = _unpair_columns(g[n])
    for n in ('q_norm', 'kv_norm', 'lru_conv_b', 'ln1_g', 'ln1_b', 'ffn_conv_b', 'ln2_g', 'ln2_b'):
        if n in g:
            out[n] = out.get(n, g[n]).reshape(-1)
    for n in ('w_o_mla', 'lru_conv_w', 'lru_lambda', 'w_o_lru', 'w_out', 'w_down'):
        if n in g:
            out[n] = g[n]
    return out


def _rope_table(tp):
    half = QK_ROPE // 2
    inv_freq = jnp.exp(-math.log(ROPE_THETA) * jnp.arange(half, dtype=F32) / half)
    ang = jnp.arange(tp, dtype=F32)[:, None] * inv_freq[None, :]
    c, s = jnp.cos(ang), jnp.sin(ang)
    return jnp.concatenate([c, c, -s, s], axis=1)


def _local_step(x, target, meta, ln0_g, ln0_b, layer_w, t_pad, emit):
    seq = x.shape[0]
    t_real = N_META + seq
    zpad = jnp.zeros((t_pad - t_real, D_MODEL), F32)
    xin = jnp.concatenate([meta, x, zpad], axis=0)
    tgt = jnp.concatenate([jnp.zeros((N_META, D_MODEL), F32), target, zpad], axis=0)
    cs = _rope_table(t_pad)
    h = _ln_fwd([(1.0, xin)], ln0_g, ln0_b, "ln0")
    saved = []
    for l in range(DEPTH):
        w_in, rest_of_weights = layer_w[l](h)
        h, s = _layer_fwd(h, w_in, rest_of_weights, cs, t_real, "l%d_" % l)
        saved.append(s)
    dy, lossvec = _loss_head(h, tgt, t_real, "loss_head")
    loss = jnp.sum(lossvec)
    terms, tok = [(1.0, dy)], None
    for l in reversed(range(DEPTH)):
        terms, tok = _layer_bwd(terms, saved[l], cs, t_real, "l%d_" % l,
                                functools.partial(lambda stage, g, l: emit(l, stage, _layer_grads(g)), l=l), tok)
    dxin, dg0, db0 = _ln_bwd(terms, [(1.0, xin)], _after(ln0_g, tok), "ln0_bwd")
    emit(None, 'head', {'meta_tokens': dxin[:N_META], 'ln0_g': dg0.reshape(-1), 'ln0_b': db0.reshape(-1)})
    return loss, dxin[N_META:t_real]


_HBM = pl.BlockSpec(memory_space=pltpu.HBM)
_SEM = pl.BlockSpec(memory_space=pltpu.SEMAPHORE)
_SIDE_EFFECT = pltpu.SideEffectType.DATAFLOW_SIDE_EFFECTING


def _peer_copies(src_refs, land_refs, scatters, send_sems, recv_sems):
    x, y, c = lax.axis_index("x"), lax.axis_index("y"), lax.axis_index("c")
    me = 4 * x + 2 * y + c
    copies = []
    for k in range(1, N_DEV):
        px = 1 - x if k & 4 else x
        py = 1 - y if k & 2 else y
        pc = 1 - c if k & 1 else c
        for t, (src, land) in enumerate(zip(src_refs, land_refs)):
            copies.append(pltpu.make_async_remote_copy(
                src_ref=src.at[4 * px + 2 * py + pc] if scatters[t] else src, dst_ref=land.at[me],
                send_sem=send_sems.at[7 * t + k - 1], recv_sem=recv_sems.at[7 * t + k - 1],
                device_id=(px, py, pc), device_id_type=pl.DeviceIdType.MESH))
    return me, copies


def _exchange_start(groups, name):
    flat = [it for grp in groups for it in grp]
    nt, ng = len(flat), len(groups)
    scatters = [sc for _, sc in flat]
    srcs = [pltpu.with_memory_space_constraint(a, pltpu.HBM) for a, _ in flat]
    land_shapes = [a.shape if sc else (N_DEV,) + a.shape for a, sc in flat]
    lands = [pltpu.with_memory_space_constraint(lax.empty(s, a.dtype), pltpu.HBM) for s, (a, _) in zip(land_shapes, flat)]
    bounds = [0]
    for grp in groups:
        bounds.append(bounds[-1] + len(grp))

    def body(*refs):
        src_refs, land_refs = refs[:nt], refs[nt:2 * nt]
        sem_refs = refs[2 * nt:2 * nt + 2 * ng]
        token_ref = refs[4 * nt + 2 * ng]
        for gi in range(ng):
            lo, hi = bounds[gi], bounds[gi + 1]
            _, copies = _peer_copies(src_refs[lo:hi], land_refs[lo:hi], scatters[lo:hi], sem_refs[2 * gi], sem_refs[2 * gi + 1])
            for cp in copies:
                cp.start()
        token_ref[...] = jnp.zeros_like(token_ref)

    out_shape = []
    for grp in groups:
        out_shape += [pltpu.SemaphoreType.DMA((7 * len(grp),)), pltpu.SemaphoreType.DMA((7 * len(grp),))]
    out_shape += [pltpu.HBM(a.shape, a.dtype) for a in srcs] + [pltpu.HBM(s, a.dtype) for s, a in zip(land_shapes, srcs)]
    out_shape += [jax.ShapeDtypeStruct((SUBLANE, LANE), F32)]
    res = pl.pallas_call(
        body, name=name, out_shape=out_shape,
        in_specs=[_HBM] * (2 * nt),
        out_specs=[_SEM] * (2 * ng) + [_HBM] * (2 * nt) + [pl.BlockSpec(memory_space=pltpu.VMEM)],
        input_output_aliases={t: 2 * ng + t for t in range(2 * nt)},
        compiler_params=pltpu.CompilerParams(has_side_effects=_SIDE_EFFECT),
    )(*srcs, *lands)
    sems, thru, token = res[:2 * ng], res[2 * ng:2 * ng + 2 * nt], res[-1]
    states = []
    for gi in range(ng):
        lo, hi = bounds[gi], bounds[gi + 1]
        states.append((sems[2 * gi], sems[2 * gi + 1], thru[lo:hi], thru[nt + lo:nt + hi], scatters[lo:hi]))
    return states, token[0, 0]


def _exchange_wait(state, after, name):
    send_sems, recv_sems, srcs, lands, scatters = state
    n = len(srcs)

    def body(*refs):
        _, copies = _peer_copies(refs[:n], refs[n:2 * n], scatters, refs[2 * n], refs[2 * n + 1])
        for cp in copies:
            cp.wait_send()
        for cp in copies:
            cp.wait_recv()

    res = pl.pallas_call(
        body, name=name,
        out_shape=[pltpu.HBM(a.shape, a.dtype) for a in srcs] + [pltpu.HBM(a.shape, a.dtype) for a in lands],
        in_specs=[_HBM] * (2 * n) + [_SEM, _SEM, _HBM],
        out_specs=[_HBM] * (2 * n),
        input_output_aliases={t: t for t in range(2 * n)},
        compiler_params=pltpu.CompilerParams(has_side_effects=_SIDE_EFFECT),
    )(*srcs, *lands, send_sems, recv_sems, pltpu.with_memory_space_constraint(after, pltpu.HBM))
    me = 4 * lax.axis_index("x") + 2 * lax.axis_index("y") + lax.axis_index("c")
    out = []
    for src, land, sc in zip(res[:n], res[n:], scatters):
        own = lax.dynamic_index_in_dim(src, me, 0, keepdims=True) if sc else src[None]
        out.append(lax.dynamic_update_slice_in_dim(land, own, me, 0))
    return out


def _as_rows(shape):
    return (1, shape[0]) if len(shape) == 1 else (math.prod(shape[:-1]), shape[-1])


def _sum_adamw(pieces, w, m, v, name):
    shape = w.shape
    nl = len(pieces)
    if nl > 1 and _as_rows(shape[1:])[0] % 16:
        pieces, nl = [jnp.stack(pieces, axis=1)], 1
    rows, cols = _as_rows(shape)
    rl = rows // nl
    cap = max(16, (1 << 18) // cols // 16 * 16)
    tr = _tile(rl, cap, 16)
    nb = rl // tr
    c1 = 1.0 / (1.0 - ADAM_B1 ** ADAM_STEP)
    c2 = 1.0 / (1.0 - ADAM_B2 ** ADAM_STEP)

    def body(*refs):
        p_refs = refs[:nl]
        w_ref, m_ref, v_ref, g_ref, d_ref, nm_ref, nv_ref = refs[nl:]
        li = pl.program_id(0)

        def total(p_ref):
            acc = p_ref[0].astype(F32)
            for k in range(1, N_DEV):
                acc = acc + p_ref[k].astype(F32)
            return acc

        gg = total(p_refs[0])
        for l in range(1, nl):
            gg = jnp.where(li == l, total(p_refs[l]), gg)
        nm = ADAM_B1 * m_ref[...] + (1.0 - ADAM_B1) * gg
        nv = ADAM_B2 * v_ref[...] + (1.0 - ADAM_B2) * (gg * gg)
        g_ref[...] = gg
        d_ref[...] = -ADAM_LR * ((nm * c1) / (jnp.sqrt(nv * c2) + ADAM_EPS) + ADAM_WD * w_ref[...])
        nm_ref[...] = nm
        nv_ref[...] = nv

    blk = pl.BlockSpec((tr, cols), lambda li, i: (li * nb + i, 0))
    p_specs = [pl.BlockSpec((N_DEV, tr, cols), functools.partial(lambda li, i, l: (0, jnp.where(li == l, i, 0), 0), l=l))
               for l in range(nl)]
    res = pl.pallas_call(
        body, name=name, grid=(nl, nb),
        in_specs=p_specs + [blk] * 3, out_specs=[blk] * 4,
        out_shape=[jax.ShapeDtypeStruct((rows, cols), F32)] * 4,
        compiler_params=_cparams("parallel", "parallel"),
    )(*[p.reshape(N_DEV, rl, cols) for p in pieces], *[a.reshape(rows, cols) for a in (w, m, v)])
    return [r.reshape(shape) for r in res]


def _to_shards(full, axis):
    shp = full.shape
    a = full.reshape(shp[:axis] + (N_DEV, shp[axis] // N_DEV) + shp[axis + 1:])
    return jnp.moveaxis(a, axis, 0)


def _from_shards(blocks, axis):
    a = jnp.moveaxis(blocks, 0, axis)
    shp = a.shape
    return a.reshape(shp[:axis] + (shp[axis] * shp[axis + 1],) + shp[axis + 2:])


def kernel(x, meta_tokens, ln0_g, ln0_b, w_in, q_norm, kv_norm, w_uq, w_uk, w_uv, w_o_mla, lru_conv_w, lru_conv_b, w_rg, b_rg, w_ig, b_ig, lru_lambda, w_o_lru, w_out, ln1_g, ln1_b, w_up, ffn_conv_w, ffn_conv_b, w_down, ln2_g, ln2_b, loss_target, m_meta_tokens, m_ln0_g, m_ln0_b, m_w_in, m_q_norm, m_kv_norm, m_w_uq, m_w_uk, m_w_uv, m_w_o_mla, m_lru_conv_w, m_lru_conv_b, m_w_rg, m_b_rg, m_w_ig, m_b_ig, m_lru_lambda, m_w_o_lru, m_w_out, m_ln1_g, m_ln1_b, m_w_up, m_ffn_conv_w, m_ffn_conv_b, m_w_down, m_ln2_g, m_ln2_b, v_meta_tokens, v_ln0_g, v_ln0_b, v_w_in, v_q_norm, v_kv_norm, v_w_uq, v_w_uk, v_w_uv, v_w_o_mla, v_lru_conv_w, v_lru_conv_b, v_w_rg, v_b_rg, v_w_ig, v_b_ig, v_lru_lambda, v_w_o_lru, v_w_out, v_ln1_g, v_ln1_b, v_w_up, v_ffn_conv_w, v_ffn_conv_b, v_w_down, v_ln2_g, v_ln2_b):
    args = (meta_tokens, ln0_g, ln0_b, w_in, q_norm, kv_norm, w_uq, w_uk, w_uv, w_o_mla, lru_conv_w, lru_conv_b, w_rg, b_rg, w_ig, b_ig, lru_lambda, w_o_lru, w_out, ln1_g, ln1_b, w_up, ffn_conv_w, ffn_conv_b, w_down, ln2_g, ln2_b)
    ms = (m_meta_tokens, m_ln0_g, m_ln0_b, m_w_in, m_q_norm, m_kv_norm, m_w_uq, m_w_uk, m_w_uv, m_w_o_mla, m_lru_conv_w, m_lru_conv_b, m_w_rg, m_b_rg, m_w_ig, m_b_ig, m_lru_lambda, m_w_o_lru, m_w_out, m_ln1_g, m_ln1_b, m_w_up, m_ffn_conv_w, m_ffn_conv_b, m_w_down, m_ln2_g, m_ln2_b)
    vs = (v_meta_tokens, v_ln0_g, v_ln0_b, v_w_in, v_q_norm, v_kv_norm, v_w_uq, v_w_uk, v_w_uv, v_w_o_mla, v_lru_conv_w, v_lru_conv_b, v_w_rg, v_b_rg, v_w_ig, v_b_ig, v_lru_lambda, v_w_o_lru, v_w_out, v_ln1_g, v_ln1_b, v_w_up, v_ffn_conv_w, v_ffn_conv_b, v_w_down, v_ln2_g, v_ln2_b)
    wd, md, vd = dict(zip(WEIGHTS, args)), dict(zip(WEIGHTS, ms)), dict(zip(WEIGHTS, vs))

    def shard_axis(n, l):
        return SHARD_AXIS[n] - (0 if l is None else 1)

    def shard(n, l):
        a = wd[n] if l is None else wd[n][l]
        return a.astype(BF16) if n in BIG else a

    first = [('meta_tokens', None), ('w_in', 0)]
    staged = [[(n, 0) for n in names] for names in STAGE_WEIGHTS]
    later = [(n, 1) for n in SHARDED if n != 'meta_tokens']
    gather, token = _exchange_start([[(shard(*k), False) for k in keys] for keys in [first] + staged + [later]], "gather_start")

    def arrive(gi, keys, after, name):
        return {k: _from_shards(b, shard_axis(*k)) for k, b in zip(keys, _exchange_wait(gather[gi], after, name))}

    def layer_weights(got, l, names):
        fl = {n: wd[n][l] for n in names if n in REPLICATED}
        fl.update({n: a for (n, _), a in got.items() if n in names})
        return _layer_weights(fl)

    ln0_g = _after(wd['ln0_g'], token)
    got_first = arrive(0, first, ln0_g, "gather_wait_first")

    def first_layer(h):
        def more(stage, after):
            got = arrive(1 + stage, staged[stage], after, "gather_wait_l0_%d" % stage)
            return layer_weights(got, 0, STAGE_WEIGHTS[stage] + STAGE_REPLICATED[stage])
        return _w_in_kernel(got_first['w_in', 0]), more

    def second_layer(h):
        got = arrive(1 + len(staged), later, h, "gather_wait_l1")
        return _w_in_kernel(got['w_in', 1]), lambda stage, after: layer_weights(got, 1, STAGE_WEIGHTS[stage] + STAGE_REPLICATED[stage])

    sent = []
    pending = []

    def send(l, stage, grads):
        for n, g in grads.items():
            if n in SHARD_AXIS:
                g = _to_shards(g, shard_axis(n, l))
                pending.append(((n, l), (g.astype(BF16) if n in BIG else g, True)))
            else:
                pending.append(((n, l), (g.astype(BF16) if n in LARGE_REPLICATED else g, False)))
        if l == DEPTH - 1 and stage != 'in':
            return None
        (state,), tok = _exchange_start([[it for _, it in pending]], "grads_start_%s_%s" % (l, stage))
        sent.append(([k for k, _ in pending], state))
        pending.clear()
        return tok

    seq = x.shape[1]
    t_pad = -(-(N_META + seq + MIN_PAD_ROWS) // LANE) * LANE
    loss, grad_x = _local_step(x[0], loss_target[0], got_first['meta_tokens', None], ln0_g, wd['ln0_b'],
                               [first_layer, second_layer], t_pad, send)
    loss = lax.psum(loss, ("x", "y", "c"))

    pieces = {}
    for gi, (keys, state) in enumerate(sent):
        pieces.update(zip(keys, _exchange_wait(state, grad_x, "grads_wait_%d" % gi)))
    outs = {}
    for n in WEIGHTS:
        ps = [pieces[n, None]] if (n, None) in pieces else [pieces[n, l] for l in range(DEPTH)]
        outs[n] = _sum_adamw(ps, wd[n], md[n], vd[n], "adamw_" + n)
    res = [loss, grad_x[None]]
    for k in range(4):
        res += [outs[n][k] for n in WEIGHTS]
    return tuple(res)
```

```python
import functools
import math

import jax
import jax.numpy as jnp
from jax import lax
from jax.experimental import pallas as pl
from jax.experimental.pallas import tpu as pltpu

F32 = jnp.float32
BF16 = jnp.bfloat16

N_DEV = 8
D_MODEL = 1024
N_META = 16
HEADS = 8
QK_NOPE = 128
QK_ROPE = 64
V_HEAD = 128
Q_RANK = 256
KV_RANK = 128
ROPE_THETA = 10000.0
LRU_BLOCKS = 8
LRU_C = 8.0
D_FF = 2816
DEPTH = 2
DN_ALPHA = (2.0 * DEPTH) ** 0.25
LN_EPS = 1e-5
RMS_EPS = 1e-6
LN2 = math.log(2.0)
ATT_SCALE = 1.0 / math.sqrt(QK_NOPE + QK_ROPE) / LN2
NEG_BIG = -1e30

ADAM_LR = 0.001
ADAM_B1 = 0.9
ADAM_B2 = 0.999
ADAM_EPS = 1e-08
ADAM_WD = 0.01
ADAM_STEP = 10

MIN_PAD_ROWS = 2
LANE = 128
SUBLANE = 8
VMEM_LIMIT = 56 * 1024 * 1024

PROJ_COLS = 4 * D_MODEL + Q_RANK + KV_RANK + 2 * QK_ROPE
C_LRU_G, C_LRU_X, C_G_MLA, C_G_LRU = 0, D_MODEL, 2 * D_MODEL, 3 * D_MODEL
C_CQ = 4 * D_MODEL
C_CKV = C_CQ + Q_RANK
C_KRP = C_CKV + KV_RANK

WEIGHTS = ['meta_tokens', 'ln0_g', 'ln0_b', 'w_in', 'q_norm', 'kv_norm', 'w_uq', 'w_uk', 'w_uv', 'w_o_mla',
           'lru_conv_w', 'lru_conv_b', 'w_rg', 'b_rg', 'w_ig', 'b_ig', 'lru_lambda', 'w_o_lru', 'w_out',
           'ln1_g', 'ln1_b', 'w_up', 'ffn_conv_w', 'ffn_conv_b', 'w_down', 'ln2_g', 'ln2_b']
SHARD_AXIS = {'meta_tokens': 1, 'w_in': 2, 'w_uq': 1, 'w_o_mla': 1, 'lru_conv_w': 2, 'b_rg': 2, 'b_ig': 2,
              'lru_lambda': 2, 'w_o_lru': 1, 'w_out': 1, 'w_up': 2, 'ffn_conv_w': 2, 'w_down': 1}
BIG = ['w_in', 'w_uq', 'w_o_mla', 'w_o_lru', 'w_out', 'w_up', 'w_down']
SHARDED = [n for n in WEIGHTS if n in SHARD_AXIS]
REPLICATED = [n for n in WEIGHTS if n not in SHARD_AXIS]
LARGE_REPLICATED = ['w_uk', 'w_uv', 'w_rg', 'w_ig']
STAGE_WEIGHTS = [['w_uq', 'lru_conv_w', 'b_rg', 'b_ig', 'lru_lambda'], ['w_o_mla', 'w_o_lru', 'w_out'], ['w_up', 'ffn_conv_w', 'w_down']]
STAGE_REPLICATED = [['q_norm', 'kv_norm', 'w_uk', 'w_uv', 'lru_conv_b', 'w_rg', 'w_ig'], ['ln1_g', 'ln1_b'], ['ffn_conv_b', 'ln2_g', 'ln2_b']]


def _cparams(*sem):
    return pltpu.CompilerParams(dimension_semantics=sem, vmem_limit_bytes=VMEM_LIMIT)


def _tile(n, cap, unit=LANE):
    best = None
    t = unit
    while t <= min(n, cap):
        if n % t == 0:
            best = t
        t += unit
    return n if best is None else best


def _sigmoid(x):
    return 1.0 / (1.0 + jnp.exp(-x))


_GELU_C = math.sqrt(2.0 / math.pi)


def _gelu(x):
    t = jnp.tanh(_GELU_C * (x + 0.044715 * x * x * x))
    return 0.5 * x * (1.0 + t)


def _gelu_and_grad(x):
    t = jnp.tanh(_GELU_C * (x + 0.044715 * x * x * x))
    g = 0.5 * x * (1.0 + t)
    dg = 0.5 * (1.0 + t) + 0.5 * x * (1.0 - t * t) * _GELU_C * (1.0 + 3.0 * 0.044715 * x * x)
    return g, dg


def _softplus_neg(lam):
    z = jnp.exp(-jnp.abs(lam))
    w = 1.0 + z
    log1p = jnp.where(w == 1.0, z, jnp.log(w) * z / (w - 1.0))
    return jnp.maximum(-lam, 0.0) + log1p


def _row_ids(shape, row0=0):
    return lax.broadcasted_iota(jnp.int32, shape, 0) + row0


def _matmul(a, b, name, ta=False, tb=False, out_dtype=F32, tm_cap=1408, tn_cap=1024, tk_cap=2048):
    if ta:
        kdim, m = a.shape
    else:
        m, kdim = a.shape
    if tb:
        n, k2 = b.shape
    else:
        k2, n = b.shape
    assert kdim == k2, (a.shape, b.shape, ta, tb)
    tm, tn, tk = _tile(m, tm_cap), _tile(n, tn_cap), _tile(kdim, tk_cap)
    nk = kdim // tk

    def body(a_ref, b_ref, o_ref, *acc):
        dn = (((0 if ta else 1,), (1 if tb else 0,)), ((), ()))
        part = lax.dot_general(a_ref[...].astype(BF16), b_ref[...].astype(BF16), dn, preferred_element_type=F32)
        if nk == 1:
            o_ref[...] = part.astype(o_ref.dtype)
            return
        acc_ref, k = acc[0], pl.program_id(2)

        @pl.when(k == 0)
        def _():
            acc_ref[...] = part

        @pl.when(k > 0)
        def _():
            acc_ref[...] += part

        @pl.when(k == nk - 1)
        def _():
            o_ref[...] = acc_ref[...].astype(o_ref.dtype)

    a_spec = pl.BlockSpec((tk, tm), lambda i, j, k: (k, i)) if ta else pl.BlockSpec((tm, tk), lambda i, j, k: (i, k))
    b_spec = pl.BlockSpec((tn, tk), lambda i, j, k: (j, k)) if tb else pl.BlockSpec((tk, tn), lambda i, j, k: (k, j))
    return pl.pallas_call(
        body, name=name,
        grid=(m // tm, n // tn, nk),
        in_specs=[a_spec, b_spec],
        out_specs=pl.BlockSpec((tm, tn), lambda i, j, k: (i, j)),
        out_shape=jax.ShapeDtypeStruct((m, n), out_dtype),
        scratch_shapes=[pltpu.VMEM((tm, tn), F32)] if nk > 1 else [],
        compiler_params=_cparams("parallel", "parallel", "arbitrary"),
    )(a, b)


class Rw:
    def __init__(self, arr, width=None, cb=0):
        self.arr, self.width, self.cb = arr, (arr.shape[1] if width is None else width), cb


class Pm:
    def __init__(self, arr):
        self.arr = arr


def _rows(fn, name, ins, outs, accs=(), tm_cap=384):
    tp = next(o.arr.shape[0] for o in ins if isinstance(o, Rw))
    tm = _tile(tp, tm_cap)
    n_in, n_out, n_acc = len(ins), len(outs), len(accs)

    def body(*refs):
        i = pl.program_id(0)
        res = fn(i * tm, *[r[...] for r in refs[:n_in]])
        if not isinstance(res, (tuple, list)):
            res = (res,)
        assert len(res) == n_out + n_acc, (name, len(res))
        for k in range(n_out):
            refs[n_in + k][...] = res[k].astype(refs[n_in + k].dtype)
        for k in range(n_acc):
            ref = refs[n_in + n_out + k]

            @pl.when(i == 0)
            def _():
                ref[...] = jnp.zeros_like(ref)

            ref[...] += res[n_out + k]

    in_specs = []
    for o in ins:
        if isinstance(o, Rw):
            in_specs.append(pl.BlockSpec((tm, o.width), functools.partial(lambda i, cb: (i, cb), cb=o.cb)))
        else:
            in_specs.append(pl.BlockSpec(o.arr.shape, functools.partial(lambda i, nd: (0,) * nd, nd=o.arr.ndim)))
    out_specs = [pl.BlockSpec((tm, w), lambda i: (i, 0)) for (w, _) in outs]
    out_specs += [pl.BlockSpec(s, functools.partial(lambda i, nd: (0,) * nd, nd=len(s))) for s in accs]
    out_shape = [jax.ShapeDtypeStruct((tp, w), dt) for (w, dt) in outs]
    out_shape += [jax.ShapeDtypeStruct(s, F32) for s in accs]
    res = pl.pallas_call(
        body, name=name, grid=(tp // tm,), in_specs=in_specs, out_specs=out_specs, out_shape=out_shape,
        compiler_params=_cparams("arbitrary"),
    )(*[o.arr for o in ins])
    return res


class Cl:
    def __init__(self, arr, col0=0, width=None, stride=1):
        self.arr, self.col0, self.width, self.stride = arr, col0, width, stride


def _cols(fn, name, ins, outs, ncols, tc):
    assert ncols % tc == 0
    n_in, n_out = len(ins), len(outs)
    outs = [(o[0], o[1], o[2] if len(o) > 2 else tc) for o in outs]

    def body(*refs):
        res = fn(*[r[...] for r in refs[:n_in]])
        if not isinstance(res, (tuple, list)):
            res = (res,)
        assert len(res) == n_out, (name, len(res))
        for k in range(n_out):
            refs[n_in + k][...] = res[k].astype(refs[n_in + k].dtype)

    in_specs = []
    for o in ins:
        wd = tc if o.width is None else o.width
        assert o.col0 % wd == 0, (name, o.col0, wd)
        in_specs.append(pl.BlockSpec((o.arr.shape[0], wd),
                                     functools.partial(lambda j, off, st: (0, st * j + off), off=o.col0 // wd, st=o.stride)))
    out_specs = [pl.BlockSpec((r, wd), lambda j: (0, j)) for (r, _, wd) in outs]
    out_shape = [jax.ShapeDtypeStruct((r, ncols // tc * wd), dt) for (r, dt, wd) in outs]
    return pl.pallas_call(
        body, name=name, grid=(ncols // tc,), in_specs=in_specs, out_specs=out_specs, out_shape=out_shape,
        compiler_params=_cparams("parallel"),
    )(*[o.arr for o in ins])


def _ln_stats(u):
    mu = jnp.mean(u, axis=-1, keepdims=True)
    xc = u - mu
    var = jnp.mean(xc * xc, axis=-1, keepdims=True)
    rstd = lax.rsqrt(var + LN_EPS)
    return xc * rstd, rstd


def _ln_fwd(terms, g, b, name):
    coefs = [c for c, _ in terms]

    def fn(row0, *blk):
        xs, (gg, bb) = blk[:len(coefs)], blk[len(coefs):]
        u = sum(c * x for c, x in zip(coefs, xs))
        xhat, _ = _ln_stats(u)
        return xhat * gg + bb

    d = terms[0][1].shape[1]
    return _rows(fn, name, [Rw(x) for _, x in terms] + [Pm(g.reshape(1, d)), Pm(b.reshape(1, d))], [(d, F32)])[0]


def _ln_bwd(dy_terms, u_terms, g, name):
    dc = [c for c, _ in dy_terms]
    uc = [c for c, _ in u_terms]
    d = u_terms[0][1].shape[1]

    def fn(row0, *blk):
        dys = blk[:len(dc)]
        xs = blk[len(dc):len(dc) + len(uc)]
        gg = blk[-1]
        dy = sum(c * x for c, x in zip(dc, dys))
        u = sum(c * x for c, x in zip(uc, xs))
        xhat, rstd = _ln_stats(u)
        gdy = dy * gg
        m1 = jnp.mean(gdy, axis=-1, keepdims=True)
        m2 = jnp.mean(gdy * xhat, axis=-1, keepdims=True)
        du = rstd * (gdy - m1 - xhat * m2)
        return du, jnp.sum(dy * xhat, axis=0, keepdims=True), jnp.sum(dy, axis=0, keepdims=True)

    ins = [Rw(x) for _, x in dy_terms] + [Rw(x) for _, x in u_terms] + [Pm(g.reshape(1, d))]
    return _rows(fn, name, ins, [(d, F32)], accs=[(1, d), (1, d)])


def _loss_head(y, tgt, t_real, name):
    d = y.shape[1]

    def fn(row0, yb, tb):
        rows = _row_ids(yb.shape, row0)
        live = (rows >= N_META) & (rows < t_real)
        diff = jnp.where(live, yb - tb, 0.0)
        return diff * (1.0 / d), jnp.sum(diff * diff, axis=0, keepdims=True) * (0.5 / d)

    return _rows(fn, name, [Rw(y), Rw(tgt)], [(d, F32)], accs=[(1, d)])


def _rms(x, g):
    r = lax.rsqrt(jnp.mean(x * x, axis=-1, keepdims=True) + RMS_EPS)
    return x * r * g


def _rms_bwd(dy, x, g):
    r = lax.rsqrt(jnp.mean(x * x, axis=-1, keepdims=True) + RMS_EPS)
    gdy = dy * g
    dx = r * gdy - x * (r * r * r) * jnp.mean(gdy * x, axis=-1, keepdims=True)
    return dx, jnp.sum(dy * x * r, axis=0, keepdims=True)


def _mla_norms(proj, qn, kvn, name):
    def fn(row0, cq, ckv, g1, g2):
        return _rms(cq, g1), _rms(ckv, g2)

    return _rows(fn, name, [Rw(proj, Q_RANK, C_CQ // Q_RANK), Rw(proj, KV_RANK, C_CKV // KV_RANK),
                            Pm(qn.reshape(1, Q_RANK)), Pm(kvn.reshape(1, KV_RANK))],
                 [(Q_RANK, F32), (KV_RANK, F32)])


def _mla_norms_bwd(dcqn, dckvn, proj, qn, kvn, name):
    def fn(row0, d1, d2, cq, ckv, g1, g2):
        dx1, dg1 = _rms_bwd(d1, cq, g1)
        dx2, dg2 = _rms_bwd(d2, ckv, g2)
        return dx1, dx2, dg1, dg2

    return _rows(fn, name, [Rw(dcqn), Rw(dckvn), Rw(proj, Q_RANK, C_CQ // Q_RANK), Rw(proj, KV_RANK, C_CKV // KV_RANK),
                            Pm(qn.reshape(1, Q_RANK)), Pm(kvn.reshape(1, KV_RANK))],
                 [(Q_RANK, F32), (KV_RANK, F32)], accs=[(1, Q_RANK), (1, KV_RANK)])


def _fold_rope(z):
    return z + pltpu.roll(z, QK_ROPE, 1)


def _mla_pack(qext, kv, proj, cs, name):
    tp = qext.shape[0]
    tm = _tile(tp, 384)
    hw, nope_all = 2 * LANE, HEADS * QK_NOPE

    def body(q_ref, kv_ref, kr_ref, cs_ref, qo_ref, ko_ref, vo_ref):
        cs_ = cs_ref[...]
        low = lax.broadcasted_iota(jnp.int32, cs_.shape, 1) < QK_ROPE
        kr = _fold_rope(kr_ref[...] * cs_).astype(BF16)
        for h in range(HEADS):
            qr = jnp.where(low, _fold_rope(q_ref[:, h * hw + QK_NOPE:(h + 1) * hw] * cs_), 0.0)
            qo_ref[:, h * hw:h * hw + QK_NOPE] = (q_ref[:, h * hw:h * hw + QK_NOPE] * ATT_SCALE).astype(BF16)
            qo_ref[:, h * hw + QK_NOPE:(h + 1) * hw] = (qr * ATT_SCALE).astype(BF16)
            ko_ref[:, h * hw:h * hw + QK_NOPE] = kv_ref[:, h * QK_NOPE:(h + 1) * QK_NOPE].astype(BF16)
            ko_ref[:, h * hw + QK_NOPE:(h + 1) * hw] = kr
        vo_ref[...] = kv_ref[:, nope_all:].astype(BF16)

    row = lambda w: pl.BlockSpec((tm, w), lambda i: (i, 0))
    return pl.pallas_call(
        body, name=name, grid=(tp // tm,),
        in_specs=[row(HEADS * hw), row(2 * nope_all), pl.BlockSpec((tm, LANE), lambda i: (i, C_KRP // LANE)), row(LANE)],
        out_specs=[row(HEADS * hw), row(HEADS * hw), row(nope_all)],
        out_shape=[jax.ShapeDtypeStruct((tp, HEADS * hw), BF16),
                   jax.ShapeDtypeStruct((tp, HEADS * hw), BF16),
                   jax.ShapeDtypeStruct((tp, nope_all), BF16)],
        compiler_params=_cparams("parallel"),
    )(qext, kv, proj, cs)


def _mla_unpack(dq, dk, dv, cs, name):
    tp = dq.shape[0]
    tm = _tile(tp, 384)
    hw, nope_all = 2 * LANE, HEADS * QK_NOPE

    def body(dq_ref, dk_ref, dv_ref, cs_ref, dqe_ref, dkv_ref, dkr_ref):
        cs_ = cs_ref[...]
        low = lax.broadcasted_iota(jnp.int32, cs_.shape, 1) < QK_ROPE
        dkr = None
        for h in range(HEADS):
            dqe_ref[:, h * hw:h * hw + QK_NOPE] = dq_ref[:, h * hw:h * hw + QK_NOPE] * ATT_SCALE
            dqr = jnp.where(low, dq_ref[:, h * hw + QK_NOPE:(h + 1) * hw], 0.0) * ATT_SCALE
            dqe_ref[:, h * hw + QK_NOPE:(h + 1) * hw] = _fold_rope(dqr) * cs_
            dkv_ref[:, h * QK_NOPE:(h + 1) * QK_NOPE] = dk_ref[:, h * hw:h * hw + QK_NOPE]
            part = jnp.where(low, dk_ref[:, h * hw + QK_NOPE:(h + 1) * hw], 0.0)
            dkr = part if h == 0 else dkr + part
        dkv_ref[:, nope_all:] = dv_ref[...]
        dkr_ref[...] = _fold_rope(dkr) * cs_

    row = lambda w: pl.BlockSpec((tm, w), lambda i: (i, 0))
    return pl.pallas_call(
        body, name=name, grid=(tp // tm,),
        in_specs=[row(HEADS * hw), row(HEADS * hw), row(nope_all), row(LANE)],
        out_specs=[row(HEADS * hw), row(2 * nope_all), row(LANE)],
        out_shape=[jax.ShapeDtypeStruct((tp, HEADS * hw), F32),
                   jax.ShapeDtypeStruct((tp, 2 * nope_all), F32),
                   jax.ShapeDtypeStruct((tp, LANE), F32)],
        compiler_params=_cparams("parallel"),
    )(dq, dk, dv, cs)


def _attn_fwd(q, k, v, t_real, name):
    tp = q.shape[0]
    tq = _tile(tp, 384)
    tkc = _tile(tp, 1408)
    nkc = -(-t_real // tkc)

    def body(q_ref, k_ref, v_ref, o_ref, lse_ref):
        qb = q_ref[...]
        m = l = acc = None
        for c in range(nkc):
            s = lax.dot_general(qb, k_ref[c * tkc:(c + 1) * tkc, :], (((1,), (1,)), ((), ())), preferred_element_type=F32)
            if (c + 1) * tkc > t_real:
                cols = lax.broadcasted_iota(jnp.int32, s.shape, 1) + c * tkc
                s = jnp.where(cols < t_real, s, NEG_BIG)
            mc = jnp.max(s, axis=-1, keepdims=True)
            m_new = mc if c == 0 else jnp.maximum(m, mc)
            p = jnp.exp2(s - m_new)
            lc = jnp.sum(p, axis=-1, keepdims=True)
            pv = jnp.dot(p.astype(BF16), v_ref[c * tkc:(c + 1) * tkc, :], preferred_element_type=F32)
            if c == 0:
                l, acc = lc, pv
            else:
                alpha = jnp.exp2(m - m_new)
                l, acc = alpha * l + lc, alpha * acc + pv
            m = m_new
        o_ref[...] = acc / l
        lse_ref[...] = m + jnp.log2(l)

    return pl.pallas_call(
        body, name=name, grid=(HEADS, tp // tq),
        in_specs=[pl.BlockSpec((tq, 2 * LANE), lambda h, i: (i, h)),
                  pl.BlockSpec((tp, 2 * LANE), lambda h, i: (0, h)),
                  pl.BlockSpec((tp, LANE), lambda h, i: (0, h))],
        out_specs=[pl.BlockSpec((tq, LANE), lambda h, i: (i, h)),
                   pl.BlockSpec((None, tq, 1), lambda h, i: (h, i, 0))],
        out_shape=[jax.ShapeDtypeStruct((tp, HEADS * LANE), F32),
                   jax.ShapeDtypeStruct((HEADS, tp, 1), F32)],
        compiler_params=_cparams("parallel", "parallel"),
    )(q, k, v)


def _attn_bwd(q, k, v, do, o, lse, t_real, name):
    tp = q.shape[0]
    tq = _tile(tp, 384)
    tkc = _tile(tp, 1408)
    nkc = -(-t_real // tkc)

    def body(q_ref, k_ref, v_ref, do_ref, o_ref, lse_ref, dq_ref, dk_ref, dv_ref):
        i = pl.program_id(1)

        @pl.when(i == 0)
        def _():
            dk_ref[...] = jnp.zeros_like(dk_ref)
            dv_ref[...] = jnp.zeros_like(dv_ref)

        qb = q_ref[...]
        dob = do_ref[...]
        dob16 = dob.astype(BF16)
        dol2 = (dob * LN2).astype(BF16)
        delta = jnp.sum(dob * o_ref[...], axis=-1, keepdims=True) * LN2
        lse = lse_ref[...]
        dq = None
        for c in range(nkc):
            ks = slice(c * tkc, (c + 1) * tkc)
            kb = k_ref[ks, :]
            s = lax.dot_general(qb, kb, (((1,), (1,)), ((), ())), preferred_element_type=F32)
            p = jnp.exp2(s - lse)
            if (c + 1) * tkc > t_real:
                cols = lax.broadcasted_iota(jnp.int32, s.shape, 1) + c * tkc
                p = jnp.where(cols < t_real, p, 0.0)
            dp = lax.dot_general(dol2, v_ref[ks, :], (((1,), (1,)), ((), ())), preferred_element_type=F32)
            ds = (p * (dp - delta)).astype(BF16)
            dqc = jnp.dot(ds, kb, preferred_element_type=F32)
            dq = dqc if c == 0 else dq + dqc
            dk_ref[ks, :] += lax.dot_general(ds, qb, (((0,), (0,)), ((), ())), preferred_element_type=F32)
            dv_ref[ks, :] += lax.dot_general(p.astype(BF16), dob16, (((0,), (0,)), ((), ())), preferred_element_type=F32)
        dq_ref[...] = dq

    return pl.pallas_call(
        body, name=name, grid=(HEADS, tp // tq),
        in_specs=[pl.BlockSpec((tq, 2 * LANE), lambda h, i: (i, h)),
                  pl.BlockSpec((tp, 2 * LANE), lambda h, i: (0, h)),
                  pl.BlockSpec((tp, LANE), lambda h, i: (0, h)),
                  pl.BlockSpec((tq, LANE), lambda h, i: (i, h)),
                  pl.BlockSpec((tq, LANE), lambda h, i: (i, h)),
                  pl.BlockSpec((None, tq, 1), lambda h, i: (h, i, 0))],
        out_specs=[pl.BlockSpec((tq, 2 * LANE), lambda h, i: (i, h)),
                   pl.BlockSpec((tp, 2 * LANE), lambda h, i: (0, h)),
                   pl.BlockSpec((tp, LANE), lambda h, i: (0, h))],
        out_shape=[jax.ShapeDtypeStruct((tp, HEADS * 2 * LANE), F32),
                   jax.ShapeDtypeStruct((tp, HEADS * 2 * LANE), F32),
                   jax.ShapeDtypeStruct((tp, HEADS * LANE), F32)],
        compiler_params=_cparams("parallel", "arbitrary"),
    )(q, k, v, do, o, lse)


def _shift_rows(x, s):
    tp = x.shape[0]
    return x if s % tp == 0 else pltpu.roll(x, s % tp, 0)


def _conv_fwd_val(xm, w, b, pad_left):
    acc = b + w[0:1, :] * _shift_rows(xm, pad_left)
    for k in range(1, w.shape[0]):
        acc = acc + w[k:k + 1, :] * _shift_rows(xm, pad_left - k)
    return acc


def _conv_bwd_val(dy, xm, w, pad_left, live):
    kk = w.shape[0]
    dx = w[0:1, :] * _shift_rows(dy, -pad_left)
    dws = [jnp.sum(dy * _shift_rows(xm, pad_left), axis=0, keepdims=True)]
    for k in range(1, kk):
        dx = dx + w[k:k + 1, :] * _shift_rows(dy, k - pad_left)
        dws.append(jnp.sum(dy * _shift_rows(xm, pad_left - k), axis=0, keepdims=True))
    return jnp.where(live, dx, 0.0), jnp.concatenate(dws, axis=0), jnp.sum(dy, axis=0, keepdims=True)


def _lru_conv_fwd(proj, w, b, t_real, name):
    def fn(x, ww, bb):
        xm = jnp.where(_row_ids(x.shape) < t_real, x, 0.0)
        return _conv_fwd_val(xm, ww, bb, 2)

    return _cols(fn, name, [Cl(proj, C_LRU_X), Cl(w), Cl(b.reshape(1, -1))], [(proj.shape[0], F32)], D_MODEL, 128)[0]


def _lru_conv_bwd(dxc, proj, w, t_real, name):
    def fn(dy, x, ww):
        live = _row_ids(x.shape) < t_real
        xm = jnp.where(live, x, 0.0)
        dym = jnp.where(live, dy, 0.0)
        return _conv_bwd_val(dym, xm, ww, 2, live)

    return _cols(fn, name, [Cl(dxc), Cl(proj, C_LRU_X), Cl(w)],
                 [(proj.shape[0], F32), (w.shape[0], F32), (1, F32)], D_MODEL, 128)


def _ffn_conv_act(up, w, b, t_real, name):
    def fn(g, v, wg, wv, bg, bv):
        live = _row_ids(g.shape) < t_real
        gc = _conv_fwd_val(jnp.where(live, g, 0.0), wg, bg, 1)
        vc = _conv_fwd_val(jnp.where(live, v, 0.0), wv, bv, 1)
        return _gelu(gc) * vc

    b2 = b.reshape(1, -1)
    return _cols(fn, name, [Cl(up), Cl(up, D_FF), Cl(w), Cl(w, D_FF), Cl(b2), Cl(b2, D_FF)],
                 [(up.shape[0], F32)], D_FF, 128)[0]


def _ffn_conv_act_bwd(dm, up, w, b, t_real, name):
    def fn(dmb, g, v, wg, wv, bg, bv):
        live = _row_ids(g.shape) < t_real
        gm, vm = jnp.where(live, g, 0.0), jnp.where(live, v, 0.0)
        gc = _conv_fwd_val(gm, wg, bg, 1)
        vc = _conv_fwd_val(vm, wv, bv, 1)
        act, dact = _gelu_and_grad(gc)
        dmm = jnp.where(live, dmb, 0.0)
        dgx, dwg, dbg = _conv_bwd_val(dmm * vc * dact, gm, wg, 1, live)
        dvx, dwv, dbv = _conv_bwd_val(dmm * act, vm, wv, 1, live)
        return dgx, dvx, dwg, dwv, dbg, dbv

    b2 = b.reshape(1, -1)
    tp, kk = up.shape[0], w.shape[0]
    return _cols(fn, name, [Cl(dm), Cl(up), Cl(up, D_FF), Cl(w), Cl(w, D_FF), Cl(b2), Cl(b2, D_FF)],
                 [(tp, F32), (tp, F32), (kk, F32), (kk, F32), (1, F32), (1, F32)], D_FF, 128)


def _lru_gates_fwd(xc, wg, b4, lam, t_real, name):
    tp = xc.shape[0]
    tm = _tile(tp, 1408)

    def body(x_ref, w_ref, b_ref, lam_ref, r0_ref, r1_ref, i0_ref, i1_ref, a0_ref, a1_ref, u0_ref, u1_ref):
        x = x_ref[...]
        xb = x.astype(BF16)
        live = _row_ids(x.shape, pl.program_id(1) * tm) < t_real
        bb = b_ref[...]
        sp = _softplus_neg(lam_ref[...])
        gate = [_sigmoid(jnp.dot(xb, w_ref[k], preferred_element_type=F32) + bb[k:k + 1, :]) for k in range(4)]
        for d, (r_ref, i_ref, a_ref, u_ref) in enumerate(((r0_ref, i0_ref, a0_ref, u0_ref), (r1_ref, i1_ref, a1_ref, u1_ref))):
            r, ig = gate[d], gate[2 + d]
            a = jnp.exp(-LRU_C * r * sp[d:d + 1, :])
            r_ref[...] = r
            i_ref[...] = ig
            a_ref[...] = a
            u_ref[...] = jnp.where(live, jnp.sqrt(1.0 - a * a) * (ig * x), 0.0)

    blk = pl.BlockSpec((tm, LANE), lambda g, i: (i, g))
    return pl.pallas_call(
        body, name=name, grid=(LRU_BLOCKS, tp // tm),
        in_specs=[blk, pl.BlockSpec((None, 4, LANE, LANE), lambda g, i: (g, 0, 0, 0)),
                  pl.BlockSpec((4, LANE), lambda g, i: (0, g)), pl.BlockSpec((2, LANE), lambda g, i: (0, g))],
        out_specs=[blk] * 8,
        out_shape=[jax.ShapeDtypeStruct((tp, D_MODEL), F32)] * 8,
        compiler_params=_cparams("parallel", "parallel"),
    )(xc, wg, b4, lam)


def _lru_gates_bwd(l0, l1, da0, da1, r0, r1, i0, i1, a0, a1, xc, wg, lam, t_real, name):
    tp = xc.shape[0]
    tm = _tile(tp, 1408)

    def body(l0_ref, l1_ref, da0_ref, da1_ref, r0_ref, r1_ref, i0_ref, i1_ref, a0_ref, a1_ref, x_ref, w_ref, lam_ref,
             dx_ref, dw_ref, db_ref, dlam_ref):
        i = pl.program_id(1)
        x = x_ref[...]
        xb = x.astype(BF16)
        live = _row_ids(x.shape, i * tm) < t_real
        lam_ = lam_ref[...]
        sp = _softplus_neg(lam_)
        dsp_dlam = -_sigmoid(-lam_)
        dx = jnp.zeros_like(x)
        dpre = [None] * 4
        dlam_rows = []
        for d, (l_ref, da_ref, r_ref, i_ref, a_ref) in enumerate(((l0_ref, da0_ref, r0_ref, i0_ref, a0_ref),
                                                                  (l1_ref, da1_ref, r1_ref, i1_ref, a1_ref))):
            r, ig, a = r_ref[...], i_ref[...], a_ref[...]
            du = jnp.where(live, l_ref[...], 0.0)
            s = jnp.sqrt(1.0 - a * a)
            dv = du * s
            ds = du * (ig * x)
            dla = jnp.where(live, da_ref[...], 0.0) * a - ds * (a * a) / s
            dla = jnp.where(live, dla, 0.0)
            dr = dla * (-LRU_C) * sp[d:d + 1, :]
            dlam_rows.append(jnp.sum(dla * (-LRU_C) * r, axis=0, keepdims=True) * dsp_dlam[d:d + 1, :])
            dpre[d] = dr * r * (1.0 - r)
            dpre[2 + d] = dv * x * ig * (1.0 - ig)
            dx = dx + dv * ig

        @pl.when(i == 0)
        def _():
            dw_ref[...] = jnp.zeros_like(dw_ref)
            db_ref[...] = jnp.zeros_like(db_ref)
            dlam_ref[...] = jnp.zeros_like(dlam_ref)

        for k in range(4):
            pk = dpre[k].astype(BF16)
            dx = dx + lax.dot_general(pk, w_ref[k], (((1,), (1,)), ((), ())), preferred_element_type=F32)
            dw_ref[k] += lax.dot_general(xb, pk, (((0,), (0,)), ((), ())), preferred_element_type=F32)
        db_ref[...] += jnp.concatenate([jnp.sum(p, axis=0, keepdims=True) for p in dpre], axis=0)
        dlam_ref[...] += jnp.concatenate(dlam_rows, axis=0)
        dx_ref[...] = dx

    blk = pl.BlockSpec((tm, LANE), lambda g, i: (i, g))
    return pl.pallas_call(
        body, name=name, grid=(LRU_BLOCKS, tp // tm),
        in_specs=[blk] * 11 + [pl.BlockSpec((None, 4, LANE, LANE), lambda g, i: (g, 0, 0, 0)),
                               pl.BlockSpec((2, LANE), lambda g, i: (0, g))],
        out_specs=[blk, pl.BlockSpec((None, 4, LANE, LANE), lambda g, i: (g, 0, 0, 0)),
                   pl.BlockSpec((4, LANE), lambda g, i: (0, g)), pl.BlockSpec((2, LANE), lambda g, i: (0, g))],
        out_shape=[jax.ShapeDtypeStruct((tp, D_MODEL), F32), jax.ShapeDtypeStruct((LRU_BLOCKS, 4, LANE, LANE), F32),
                   jax.ShapeDtypeStruct((4, D_MODEL), F32), jax.ShapeDtypeStruct((2, D_MODEL), F32)],
        compiler_params=_cparams("parallel", "arbitrary"),
    )(l0, l1, da0, da1, r0, r1, i0, i1, a0, a1, xc, wg, lam)


def _tile_scan(a, u, reverse):
    rows = lax.broadcasted_iota(jnp.int32, a.shape, 0)
    for s in (1, 2, 4):
        if reverse:
            keep = rows < SUBLANE - s
            a_sh, u_sh = pltpu.roll(a, SUBLANE - s, 0), pltpu.roll(u, SUBLANE - s, 0)
        else:
            keep = rows >= s
            a_sh, u_sh = pltpu.roll(a, s, 0), pltpu.roll(u, s, 0)
        u = u + a * jnp.where(keep, u_sh, 0.0)
        a = a * jnp.where(keep, a_sh, 1.0)
    return a, u


def _scan_fwd(a0, u0, a1, u1, name):
    tp, d = a0.shape
    tc = 128
    nt = tp // SUBLANE

    def body(a0_ref, u0_ref, a1_ref, u1_ref, h0_ref, h1_ref):
        def step(t, carry):
            c0, c1 = carry
            f = pl.multiple_of(t * SUBLANE, SUBLANE)
            b = pl.multiple_of((nt - 1 - t) * SUBLANE, SUBLANE)
            pa, pu = _tile_scan(a0_ref[pl.ds(f, SUBLANE), :], u0_ref[pl.ds(f, SUBLANE), :], False)
            h = pu + pa * c0
            h0_ref[pl.ds(f, SUBLANE), :] = h
            c0 = h[SUBLANE - 1:SUBLANE, :]
            pa, pu = _tile_scan(a1_ref[pl.ds(b, SUBLANE), :], u1_ref[pl.ds(b, SUBLANE), :], True)
            h = pu + pa * c1
            h1_ref[pl.ds(b, SUBLANE), :] = h
            c1 = h[0:1, :]
            return c0, c1

        z = jnp.zeros((1, tc), F32)
        lax.fori_loop(0, nt, step, (z, z))

    blk = pl.BlockSpec((tp, tc), lambda j: (0, j))
    return pl.pallas_call(
        body, name=name, grid=(d // tc,), in_specs=[blk] * 4, out_specs=[blk] * 2,
        out_shape=[jax.ShapeDtypeStruct((tp, d), F32)] * 2,
        compiler_params=_cparams("parallel"),
    )(a0, u0, a1, u1)


def _scan_bwd(dh, a0, a1, h0, h1, name):
    tp, d = dh.shape
    tc = 128
    nt = tp // SUBLANE

    def body(dh_ref, a0_ref, a1_ref, h0_ref, h1_ref, l0_ref, l1_ref, da0_ref, da1_ref):
        rows8 = lax.broadcasted_iota(jnp.int32, (SUBLANE, tc), 0)

        def step(t, carry):
            c0, c1 = carry
            b = pl.multiple_of((nt - 1 - t) * SUBLANE, SUBLANE)
            f = pl.multiple_of(t * SUBLANE, SUBLANE)
            a = a0_ref[pl.ds(b, SUBLANE), :]
            a_next = jnp.where(rows8 < SUBLANE - 1, pltpu.roll(a, SUBLANE - 1, 0), 1.0)
            pa, pu = _tile_scan(a_next, dh_ref[pl.ds(b, SUBLANE), :], True)
            lam = pu + pa * c0
            l0_ref[pl.ds(b, SUBLANE), :] = lam
            c0 = a[0:1, :] * lam[0:1, :]
            a = a1_ref[pl.ds(f, SUBLANE), :]
            a_prev = jnp.where(rows8 >= 1, pltpu.roll(a, 1, 0), 1.0)
            pa, pu = _tile_scan(a_prev, dh_ref[pl.ds(f, SUBLANE), :], False)
            lam = pu + pa * c1
            l1_ref[pl.ds(f, SUBLANE), :] = lam
            c1 = a[SUBLANE - 1:SUBLANE, :] * lam[SUBLANE - 1:SUBLANE, :]
            return c0, c1

        z = jnp.zeros((1, tc), F32)
        lax.fori_loop(0, nt, step, (z, z))
        rows = lax.broadcasted_iota(jnp.int32, (tp, tc), 0)
        da0_ref[...] = l0_ref[...] * jnp.where(rows >= 1, pltpu.roll(h0_ref[...], 1, 0), 0.0)
        da1_ref[...] = l1_ref[...] * jnp.where(rows < tp - 1, pltpu.roll(h1_ref[...], tp - 1, 0), 0.0)

    blk = pl.BlockSpec((tp, tc), lambda j: (0, j))
    return pl.pallas_call(
        body, name=name, grid=(d // tc,), in_specs=[blk] * 5, out_specs=[blk] * 4,
        out_shape=[jax.ShapeDtypeStruct((tp, d), F32)] * 4,
        compiler_params=_cparams("parallel"),
    )(dh, a0, a1, h0, h1)


def _gated_h(proj, h0, h1, name):
    def fn(row0, lg, x0, x1):
        return _gelu(lg) * (x0 + x1)

    return _rows(fn, name, [Rw(proj, D_MODEL, C_LRU_G // D_MODEL), Rw(h0), Rw(h1)], [(D_MODEL, F32)])[0]


def _gated_h_bwd(dgh, proj, h0, h1, name):
    def fn(row0, dg, lg, x0, x1):
        act, dact = _gelu_and_grad(lg)
        return dg * (x0 + x1) * dact, dg * act

    return _rows(fn, name, [Rw(dgh), Rw(proj, D_MODEL, C_LRU_G // D_MODEL), Rw(h0), Rw(h1)], [(D_MODEL, F32)] * 2)


def _mix(proj, y_mla, y_lru, name):
    def fn(row0, gm, gl, ym, yl):
        return _sigmoid(gm) * ym + _sigmoid(gl) * yl

    return _rows(fn, name, [Rw(proj, D_MODEL, C_G_MLA // D_MODEL), Rw(proj, D_MODEL, C_G_LRU // D_MODEL), Rw(y_mla), Rw(y_lru)],
                 [(D_MODEL, F32)])[0]


def _mix_bwd(dz, proj, y_mla, y_lru, name):
    def fn(row0, dzb, gm, gl, ym, yl):
        sm, sl = _sigmoid(gm), _sigmoid(gl)
        return dzb * sm, dzb * sl, dzb * ym * sm * (1.0 - sm), dzb * yl * sl * (1.0 - sl)

    return _rows(fn, name, [Rw(dz), Rw(proj, D_MODEL, C_G_MLA // D_MODEL), Rw(proj, D_MODEL, C_G_LRU // D_MODEL),
                            Rw(y_mla), Rw(y_lru)], [(D_MODEL, F32)] * 4)


def _layer_fwd(h, w_in, more_weights, cs, t_real, tag):
    proj = _matmul(h, w_in, tag + "proj")
    w = dict(more_weights(0, proj), w_in=w_in)
    cqn, ckvn = _mla_norms(proj, w['q_norm'], w['kv_norm'], tag + "mla_norms")
    qext = _matmul(cqn, w['w_q'], tag + "q_up")
    kv = _matmul(ckvn, w['w_kv'], tag + "kv_up")
    qc, kc, vb = _mla_pack(qext, kv, proj, cs, tag + "mla_pack")
    o, lse = _attn_fwd(qc, kc, vb, t_real, tag + "attn_fwd")
    w.update(more_weights(1, o))
    y_mla = _matmul(o, w['w_o_mla'], tag + "o_mla")
    xc = _lru_conv_fwd(proj, w['lru_conv_w'], w['lru_conv_b'], t_real, tag + "lru_conv")
    r0, r1, i0, i1, a0, a1, u0, u1 = _lru_gates_fwd(xc, w['w_g'], w['b4'], w['lru_lambda'], t_real, tag + "lru_gates")
    h0, h1 = _scan_fwd(a0, u0, a1, u1, tag + "lru_scan")
    gh = _gated_h(proj, h0, h1, tag + "lru_gate_out")
    y_lru = _matmul(gh, w['w_o_lru'], tag + "o_lru")
    z = _mix(proj, y_mla, y_lru, tag + "mix")
    zo = _matmul(z, w['w_out'], tag + "w_out")
    hm = _ln_fwd([(DN_ALPHA, h), (1.0, zo)], w['ln1_g'], w['ln1_b'], tag + "ln1")
    w.update(more_weights(2, hm))
    up = _matmul(hm, w['w_up'], tag + "w_up")
    m = _ffn_conv_act(up, w['ffn_conv_w'], w['ffn_conv_b'], t_real, tag + "ffn_conv")
    f = _matmul(m, w['w_down'], tag + "w_down", tk_cap=1408)
    out = _ln_fwd([(DN_ALPHA, hm), (1.0, f)], w['ln2_g'], w['ln2_b'], tag + "ln2")
    saved = dict(w=w, h=h, proj=proj, cqn=cqn, ckvn=ckvn, qc=qc, kc=kc, vb=vb, o=o, lse=lse, y_mla=y_mla, xc=xc,
                 r0=r0, r1=r1, i0=i0, i1=i1, a0=a0, a1=a1, h0=h0, h1=h1, gh=gh, y_lru=y_lru, z=z, zo=zo, hm=hm,
                 up=up, m=m, f=f)
    return out, saved


DW_MATMUL = dict(ta=True, out_dtype=BF16, tn_cap=1408, tk_cap=1408)


def _after(a, tok):
    return a if tok is None else a + tok.astype(a.dtype)


def _layer_bwd(dout_terms, s, cs, t_real, tag, emit, tok):
    w = s['w']
    g = {}
    du2, dg2, db2 = _ln_bwd(dout_terms, [(DN_ALPHA, s['hm']), (1.0, s['f'])], _after(w['ln2_g'], tok), tag + "ln2_bwd")
    g['ln2_g'], g['ln2_b'] = dg2, db2
    dm = _matmul(du2, w['w_down'], tag + "w_down_dx", tb=True)
    g['w_down'] = _matmul(s['m'], du2, tag + "w_down_dw", **DW_MATMUL)
    dgp, dvp, dwg_, dwv_, dbg_, dbv_ = _ffn_conv_act_bwd(dm, s['up'], w['ffn_conv_w'], w['ffn_conv_b'], t_real, tag + "ffn_conv_bwd")
    g['ffn_conv_w'] = jnp.concatenate([dwg_, dwv_], axis=1)
    g['ffn_conv_b'] = jnp.concatenate([dbg_, dbv_], axis=1)
    dup = jnp.concatenate([dgp, dvp], axis=1)
    dhm_mm = _matmul(dup, w['w_up'], tag + "w_up_dx", tb=True, tk_cap=1408)
    g['w_up'] = _matmul(s['hm'], dup, tag + "w_up_dw", **DW_MATMUL)
    tok = emit('ffn', g)
    g = {}
    du1, dg1, db1 = _ln_bwd([(DN_ALPHA, du2), (1.0, dhm_mm)], [(DN_ALPHA, s['h']), (1.0, s['zo'])],
                            _after(w['ln1_g'], tok), tag + "ln1_bwd")
    g['ln1_g'], g['ln1_b'] = dg1, db1
    dz = _matmul(du1, w['w_out'], tag + "w_out_dx", tb=True)
    g['w_out'] = _matmul(s['z'], du1, tag + "w_out_dw", **DW_MATMUL)
    dy_mla, dy_lru, dg_mla, dg_lru = _mix_bwd(dz, s['proj'], s['y_mla'], s['y_lru'], tag + "mix_bwd")
    do = _matmul(dy_mla, w['w_o_mla'], tag + "o_mla_dx", tb=True)
    g['w_o_mla'] = _matmul(s['o'], dy_mla, tag + "o_mla_dw", **DW_MATMUL)
    dqc, dkc, dv = _attn_bwd(s['qc'], s['kc'], s['vb'], do, s['o'], s['lse'], t_real, tag + "attn_bwd")
    dqext, dkv, dkrp = _mla_unpack(dqc, dkc, dv, cs, tag + "mla_unpack")
    dcqn = _matmul(dqext, w['w_q'], tag + "q_up_dx", tb=True)
    g['w_q'] = _matmul(s['cqn'], dqext, tag + "q_up_dw", **DW_MATMUL)
    dckvn = _matmul(dkv, w['w_kv'], tag + "kv_up_dx", tb=True)
    g['w_kv'] = _matmul(s['ckvn'], dkv, tag + "kv_up_dw", **DW_MATMUL)
    dcq, dckv, g['q_norm'], g['kv_norm'] = _mla_norms_bwd(dcqn, dckvn, s['proj'], w['q_norm'], w['kv_norm'], tag + "mla_norms_bwd")
    dgh = _matmul(dy_lru, w['w_o_lru'], tag + "o_lru_dx", tb=True)
    g['w_o_lru'] = _matmul(s['gh'], dy_lru, tag + "o_lru_dw", **DW_MATMUL)
    dlru_g, dhs = _gated_h_bwd(dgh, s['proj'], s['h0'], s['h1'], tag + "lru_gate_out_bwd")
    l0, l1, da0, da1 = _scan_bwd(dhs, s['a0'], s['a1'], s['h0'], s['h1'], tag + "lru_scan_bwd")
    dxc, g['w_g'], g['b4'], g['lru_lambda'] = _lru_gates_bwd(
        l0, l1, da0, da1, s['r0'], s['r1'], s['i0'], s['i1'], s['a0'], s['a1'], s['xc'], w['w_g'], w['lru_lambda'],
        t_real, tag + "lru_gates_bwd")
    dlru_x, g['lru_conv_w'], g['lru_conv_b'] = _lru_conv_bwd(dxc, s['proj'], w['lru_conv_w'], t_real, tag + "lru_conv_bwd")
    tok = emit('mid', g)
    dproj = jnp.concatenate([dlru_g, dlru_x, dg_mla, dg_lru, dcq, dckv, _after(dkrp, tok)], axis=1)
    tok = emit('in', {'w_in': _matmul(s['h'], dproj, tag + "proj_dw", **DW_MATMUL)})
    dh_mm = _matmul(dproj, w['w_in'], tag + "proj_dx", tb=True, tk_cap=1536)
    return [(DN_ALPHA, du1), (1.0, dh_mm)], tok


def _swap_halves(a, axis=-1):
    h1, h2 = jnp.split(a, 2, axis=axis)
    return jnp.concatenate([h2, h1], axis=axis)


def _w_in_kernel(w_in):
    cq, ckv, kr, lg, lx, gm, gl = jnp.split(w_in, [256, 384, 448, 1472, 2496, 3520], axis=1)
    return jnp.concatenate([lg, lx, gm, gl, cq, ckv, kr, _swap_halves(kr)], axis=1)


def _layer_weights(fl):
    w = {}
    if 'w_uq' in fl:
        uq = fl['w_uq']
        w['w_q'] = jnp.concatenate([uq, _swap_halves(uq[..., QK_NOPE:])], axis=-1).reshape(Q_RANK, HEADS * 2 * LANE)
        w['w_kv'] = jnp.concatenate([fl['w_uk'].reshape(KV_RANK, -1), fl['w_uv'].reshape(KV_RANK, -1)], axis=1).astype(BF16)
        w['w_g'] = jnp.moveaxis(jnp.concatenate([fl['w_rg'], fl['w_ig']], axis=0), 0, 1).astype(BF16)
        w['b4'] = jnp.concatenate([fl['b_rg'], fl['b_ig']], axis=0)
    for n in ('q_norm', 'kv_norm', 'w_o_mla', 'lru_conv_w', 'lru_conv_b', 'lru_lambda', 'w_o_lru', 'w_out', 'ln1_g',
              'ln1_b', 'w_up', 'ffn_conv_w', 'ffn_conv_b', 'w_down', 'ln2_g', 'ln2_b'):
        if n in fl:
            w[n] = fl[n]
    return w


def _layer_grads(g):
    out = {}
    if 'w_in' in g:
        lg, lx, gm, gl, cq, ckv, kr, krs = jnp.split(g['w_in'], [1024, 2048, 3072, 4096, 4352, 4480, 4544], axis=1)
        out['w_in'] = jnp.concatenate([cq, ckv, kr + _swap_halves(krs), lg, lx, gm, gl], axis=1)
    if 'w_q' in g:
        gq = g['w_q'].reshape(Q_RANK, HEADS, 2 * LANE)
        out['w_uq'] = jnp.concatenate([gq[..., :QK_NOPE], gq[..., QK_NOPE:QK_NOPE + QK_ROPE] + _swap_halves(gq[..., QK_NOPE + QK_ROPE:])], axis=-1)
    if 'w_kv' in g:
        out['w_uk'] = g['w_kv'][:, :HEADS * QK_NOPE].reshape(KV_RANK, HEADS, QK_NOPE)
        out['w_uv'] = g['w_kv'][:, HEADS * QK_NOPE:].reshape(KV_RANK, HEADS, V_HEAD)
    if 'w_g' in g:
        gg = jnp.moveaxis(g['w_g'], 1, 0)
        out['w_rg'], out['w_ig'] = gg[:2], gg[2:]
    if 'b4' in g:
        out['b_rg'], out['b_ig'] = g['b4'][:2], g['b4'][2:]
    for n in ('q_norm', 'kv_norm', 'lru_conv_b', 'ln1_g', 'ln1_b', 'ffn_conv_b', 'ln2_g', 'ln2_b'):
        if n in g:
            out[n] = g[n].reshape(-1)
    for n in ('w_o_mla', 'lru_conv_w', 'lru_lambda', 'w_o_lru', 'w_out', 'w_up', 'ffn_conv_w', 'w_down'):
        if n in g:
            out[n] = g[n]
    return out


def _rope_table(tp):
    half = QK_ROPE // 2
    inv_freq = jnp.exp(-math.log(ROPE_THETA) * jnp.arange(half, dtype=F32) / half)
    ang = jnp.arange(tp, dtype=F32)[:, None] * inv_freq[None, :]
    c, s = jnp.cos(ang), jnp.sin(ang)
    return jnp.concatenate([c, c, -s, s], axis=1)


def _local_step(x, target, meta, ln0_g, ln0_b, layer_w, t_pad, emit):
    seq = x.shape[0]
    t_real = N_META + seq
    zpad = jnp.zeros((t_pad - t_real, D_MODEL), F32)
    xin = jnp.concatenate([meta, x, zpad], axis=0)
    tgt = jnp.concatenate([jnp.zeros((N_META, D_MODEL), F32), target, zpad], axis=0)
    cs = _rope_table(t_pad)
    h = _ln_fwd([(1.0, xin)], ln0_g, ln0_b, "ln0")
    saved = []
    for l in range(DEPTH):
        w_in, rest_of_weights = layer_w[l](h)
        h, s = _layer_fwd(h, w_in, rest_of_weights, cs, t_real, "l%d_" % l)
        saved.append(s)
    dy, lossvec = _loss_head(h, tgt, t_real, "loss_head")
    loss = jnp.sum(lossvec)
    terms, tok = [(1.0, dy)], None
    for l in reversed(range(DEPTH)):
        terms, tok = _layer_bwd(terms, saved[l], cs, t_real, "l%d_" % l,
                                functools.partial(lambda stage, g, l: emit(l, stage, _layer_grads(g)), l=l), tok)
    dxin, dg0, db0 = _ln_bwd(terms, [(1.0, xin)], _after(ln0_g, tok), "ln0_bwd")
    emit(None, 'head', {'meta_tokens': dxin[:N_META], 'ln0_g': dg0.reshape(-1), 'ln0_b': db0.reshape(-1)})
    return loss, dxin[N_META:t_real]


_HBM = pl.BlockSpec(memory_space=pltpu.HBM)
_SEM = pl.BlockSpec(memory_space=pltpu.SEMAPHORE)
_SIDE_EFFECT = pltpu.SideEffectType.DATAFLOW_SIDE_EFFECTING


def _peer_copies(src_refs, land_refs, scatters, send_sems, recv_sems):
    x, y, c = lax.axis_index("x"), lax.axis_index("y"), lax.axis_index("c")
    me = 4 * x + 2 * y + c
    copies = []
    for k in range(1, N_DEV):
        px = 1 - x if k & 4 else x
        py = 1 - y if k & 2 else y
        pc = 1 - c if k & 1 else c
        for t, (src, land) in enumerate(zip(src_refs, land_refs)):
            copies.append(pltpu.make_async_remote_copy(
                src_ref=src.at[4 * px + 2 * py + pc] if scatters[t] else src, dst_ref=land.at[me],
                send_sem=send_sems.at[7 * t + k - 1], recv_sem=recv_sems.at[7 * t + k - 1],
                device_id=(px, py, pc), device_id_type=pl.DeviceIdType.MESH))
    return me, copies


def _exchange_start(groups, name):
    flat = [it for grp in groups for it in grp]
    nt, ng = len(flat), len(groups)
    scatters = [sc for _, sc in flat]
    srcs = [pltpu.with_memory_space_constraint(a, pltpu.HBM) for a, _ in flat]
    land_shapes = [a.shape if sc else (N_DEV,) + a.shape for a, sc in flat]
    lands = [pltpu.with_memory_space_constraint(lax.empty(s, a.dtype), pltpu.HBM) for s, (a, _) in zip(land_shapes, flat)]
    bounds = [0]
    for grp in groups:
        bounds.append(bounds[-1] + len(grp))

    def body(*refs):
        src_refs, land_refs = refs[:nt], refs[nt:2 * nt]
        sem_refs = refs[2 * nt:2 * nt + 2 * ng]
        token_ref = refs[4 * nt + 2 * ng]
        for gi in range(ng):
            lo, hi = bounds[gi], bounds[gi + 1]
            _, copies = _peer_copies(src_refs[lo:hi], land_refs[lo:hi], scatters[lo:hi], sem_refs[2 * gi], sem_refs[2 * gi + 1])
            for cp in copies:
                cp.start()
        token_ref[...] = jnp.zeros_like(token_ref)

    out_shape = []
    for grp in groups:
        out_shape += [pltpu.SemaphoreType.DMA((7 * len(grp),)), pltpu.SemaphoreType.DMA((7 * len(grp),))]
    out_shape += [pltpu.HBM(a.shape, a.dtype) for a in srcs] + [pltpu.HBM(s, a.dtype) for s, a in zip(land_shapes, srcs)]
    out_shape += [jax.ShapeDtypeStruct((SUBLANE, LANE), F32)]
    res = pl.pallas_call(
        body, name=name, out_shape=out_shape,
        in_specs=[_HBM] * (2 * nt),
        out_specs=[_SEM] * (2 * ng) + [_HBM] * (2 * nt) + [pl.BlockSpec(memory_space=pltpu.VMEM)],
        input_output_aliases={t: 2 * ng + t for t in range(2 * nt)},
        compiler_params=pltpu.CompilerParams(has_side_effects=_SIDE_EFFECT),
    )(*srcs, *lands)
    sems, thru, token = res[:2 * ng], res[2 * ng:2 * ng + 2 * nt], res[-1]
    states = []
    for gi in range(ng):
        lo, hi = bounds[gi], bounds[gi + 1]
        states.append((sems[2 * gi], sems[2 * gi + 1], thru[lo:hi], thru[nt + lo:nt + hi], scatters[lo:hi]))
    return states, token[0, 0]


def _exchange_wait(state, after, name):
    send_sems, recv_sems, srcs, lands, scatters = state
    n = len(srcs)

    def body(*refs):
        _, copies = _peer_copies(refs[:n], refs[n:2 * n], scatters, refs[2 * n], refs[2 * n + 1])
        for cp in copies:
            cp.wait_send()
        for cp in copies:
            cp.wait_recv()

    res = pl.pallas_call(
        body, name=name,
        out_shape=[pltpu.HBM(a.shape, a.dtype) for a in srcs] + [pltpu.HBM(a.shape, a.dtype) for a in lands],
        in_specs=[_HBM] * (2 * n) + [_SEM, _SEM, _HBM],
        out_specs=[_HBM] * (2 * n),
        input_output_aliases={t: t for t in range(2 * n)},
        compiler_params=pltpu.CompilerParams(has_side_effects=_SIDE_EFFECT),
    )(*srcs, *lands, send_sems, recv_sems, pltpu.with_memory_space_constraint(after, pltpu.HBM))
    me = 4 * lax.axis_index("x") + 2 * lax.axis_index("y") + lax.axis_index("c")
    out = []
    for src, land, sc in zip(res[:n], res[n:], scatters):
        own = lax.dynamic_index_in_dim(src, me, 0, keepdims=True) if sc else src[None]
        out.append(lax.dynamic_update_slice_in_dim(land, own, me, 0))
    return out


def _as_rows(shape):
    return (1, shape[0]) if len(shape) == 1 else (math.prod(shape[:-1]), shape[-1])


def _sum_adamw(pieces, w, m, v, name):
    shape = w.shape
    nl = len(pieces)
    if nl > 1 and _as_rows(shape[1:])[0] % 16:
        pieces, nl = [jnp.stack(pieces, axis=1)], 1
    rows, cols = _as_rows(shape)
    rl = rows // nl
    cap = max(16, (1 << 18) // cols // 16 * 16)
    tr = _tile(rl, cap, 16)
    nb = rl // tr
    c1 = 1.0 / (1.0 - ADAM_B1 ** ADAM_STEP)
    c2 = 1.0 / (1.0 - ADAM_B2 ** ADAM_STEP)

    def body(*refs):
        p_refs = refs[:nl]
        w_ref, m_ref, v_ref, g_ref, d_ref, nm_ref, nv_ref = refs[nl:]
        li = pl.program_id(0)

        def total(p_ref):
            acc = p_ref[0].astype(F32)
            for k in range(1, N_DEV):
                acc = acc + p_ref[k].astype(F32)
            return acc

        gg = total(p_refs[0])
        for l in range(1, nl):
            gg = jnp.where(li == l, total(p_refs[l]), gg)
        nm = ADAM_B1 * m_ref[...] + (1.0 - ADAM_B1) * gg
        nv = ADAM_B2 * v_ref[...] + (1.0 - ADAM_B2) * (gg * gg)
        g_ref[...] = gg
        d_ref[...] = -ADAM_LR * ((nm * c1) / (jnp.sqrt(nv * c2) + ADAM_EPS) + ADAM_WD * w_ref[...])
        nm_ref[...] = nm
        nv_ref[...] = nv

    blk = pl.BlockSpec((tr, cols), lambda li, i: (li * nb + i, 0))
    p_specs = [pl.BlockSpec((N_DEV, tr, cols), functools.partial(lambda li, i, l: (0, jnp.where(li == l, i, 0), 0), l=l))
               for l in range(nl)]
    res = pl.pallas_call(
        body, name=name, grid=(nl, nb),
        in_specs=p_specs + [blk] * 3, out_specs=[blk] * 4,
        out_shape=[jax.ShapeDtypeStruct((rows, cols), F32)] * 4,
        compiler_params=_cparams("parallel", "parallel"),
    )(*[p.reshape(N_DEV, rl, cols) for p in pieces], *[a.reshape(rows, cols) for a in (w, m, v)])
    return [r.reshape(shape) for r in res]


def _to_shards(full, axis):
    shp = full.shape
    a = full.reshape(shp[:axis] + (N_DEV, shp[axis] // N_DEV) + shp[axis + 1:])
    return jnp.moveaxis(a, axis, 0)


def _from_shards(blocks, axis):
    a = jnp.moveaxis(blocks, 0, axis)
    shp = a.shape
    return a.reshape(shp[:axis] + (shp[axis] * shp[axis + 1],) + shp[axis + 2:])


def kernel(x, meta_tokens, ln0_g, ln0_b, w_in, q_norm, kv_norm, w_uq, w_uk, w_uv, w_o_mla, lru_conv_w, lru_conv_b, w_rg, b_rg, w_ig, b_ig, lru_lambda, w_o_lru, w_out, ln1_g, ln1_b, w_up, ffn_conv_w, ffn_conv_b, w_down, ln2_g, ln2_b, loss_target, m_meta_tokens, m_ln0_g, m_ln0_b, m_w_in, m_q_norm, m_kv_norm, m_w_uq, m_w_uk, m_w_uv, m_w_o_mla, m_lru_conv_w, m_lru_conv_b, m_w_rg, m_b_rg, m_w_ig, m_b_ig, m_lru_lambda, m_w_o_lru, m_w_out, m_ln1_g, m_ln1_b, m_w_up, m_ffn_conv_w, m_ffn_conv_b, m_w_down, m_ln2_g, m_ln2_b, v_meta_tokens, v_ln0_g, v_ln0_b, v_w_in, v_q_norm, v_kv_norm, v_w_uq, v_w_uk, v_w_uv, v_w_o_mla, v_lru_conv_w, v_lru_conv_b, v_w_rg, v_b_rg, v_w_ig, v_b_ig, v_lru_lambda, v_w_o_lru, v_w_out, v_ln1_g, v_ln1_b, v_w_up, v_ffn_conv_w, v_ffn_conv_b, v_w_down, v_ln2_g, v_ln2_b):
    args = (meta_tokens, ln0_g, ln0_b, w_in, q_norm, kv_norm, w_uq, w_uk, w_uv, w_o_mla, lru_conv_w, lru_conv_b, w_rg, b_rg, w_ig, b_ig, lru_lambda, w_o_lru, w_out, ln1_g, ln1_b, w_up, ffn_conv_w, ffn_conv_b, w_down, ln2_g, ln2_b)
    ms = (m_meta_tokens, m_ln0_g, m_ln0_b, m_w_in, m_q_norm, m_kv_norm, m_w_uq, m_w_uk, m_w_uv, m_w_o_mla, m_lru_conv_w, m_lru_conv_b, m_w_rg, m_b_rg, m_w_ig, m_b_ig, m_lru_lambda, m_w_o_lru, m_w_out, m_ln1_g, m_ln1_b, m_w_up, m_ffn_conv_w, m_ffn_conv_b, m_w_down, m_ln2_g, m_ln2_b)
    vs = (v_meta_tokens, v_ln0_g, v_ln0_b, v_w_in, v_q_norm, v_kv_norm, v_w_uq, v_w_uk, v_w_uv, v_w_o_mla, v_lru_conv_w, v_lru_conv_b, v_w_rg, v_b_rg, v_w_ig, v_b_ig, v_lru_lambda, v_w_o_lru, v_w_out, v_ln1_g, v_ln1_b, v_w_up, v_ffn_conv_w, v_ffn_conv_b, v_w_down, v_ln2_g, v_ln2_b)
    wd, md, vd = dict(zip(WEIGHTS, args)), dict(zip(WEIGHTS, ms)), dict(zip(WEIGHTS, vs))

    def shard_axis(n, l):
        return SHARD_AXIS[n] - (0 if l is None else 1)

    def shard(n, l):
        a = wd[n] if l is None else wd[n][l]
        return a.astype(BF16) if n in BIG else a

    first = [('meta_tokens', None), ('w_in', 0)]
    staged = [[(n, 0) for n in names] for names in STAGE_WEIGHTS]
    later = [(n, 1) for n in SHARDED if n != 'meta_tokens']
    gather, token = _exchange_start([[(shard(*k), False) for k in keys] for keys in [first] + staged + [later]], "gather_start")

    def arrive(gi, keys, after, name):
        return {k: _from_shards(b, shard_axis(*k)) for k, b in zip(keys, _exchange_wait(gather[gi], after, name))}

    def layer_weights(got, l, names):
        fl = {n: wd[n][l] for n in names if n in REPLICATED}
        fl.update({n: a for (n, _), a in got.items() if n in names})
        return _layer_weights(fl)

    ln0_g = _after(wd['ln0_g'], token)
    got_first = arrive(0, first, ln0_g, "gather_wait_first")

    def first_layer(h):
        def more(stage, after):
            got = arrive(1 + stage, staged[stage], after, "gather_wait_l0_%d" % stage)
            return layer_weights(got, 0, STAGE_WEIGHTS[stage] + STAGE_REPLICATED[stage])
        return _w_in_kernel(got_first['w_in', 0]), more

    def second_layer(h):
        got = arrive(1 + len(staged), later, h, "gather_wait_l1")
        return _w_in_kernel(got['w_in', 1]), lambda stage, after: layer_weights(got, 1, STAGE_WEIGHTS[stage] + STAGE_REPLICATED[stage])

    sent = []
    pending = []

    def send(l, stage, grads):
        for n, g in grads.items():
            if n in SHARD_AXIS:
                g = _to_shards(g, shard_axis(n, l))
                pending.append(((n, l), (g.astype(BF16) if n in BIG else g, True)))
            else:
                pending.append(((n, l), (g.astype(BF16) if n in LARGE_REPLICATED else g, False)))
        if l == DEPTH - 1 and stage != 'in':
            return None
        (state,), tok = _exchange_start([[it for _, it in pending]], "grads_start_%s_%s" % (l, stage))
        sent.append(([k for k, _ in pending], state))
        pending.clear()
        return tok

    seq = x.shape[1]
    t_pad = -(-(N_META + seq + MIN_PAD_ROWS) // LANE) * LANE
    loss, grad_x = _local_step(x[0], loss_target[0], got_first['meta_tokens', None], ln0_g, wd['ln0_b'],
                               [first_layer, second_layer], t_pad, send)
    loss = lax.psum(loss, ("x", "y", "c"))

    pieces = {}
    for gi, (keys, state) in enumerate(sent):
        pieces.update(zip(keys, _exchange_wait(state, grad_x, "grads_wait_%d" % gi)))
    outs = {}
    for n in WEIGHTS:
        ps = [pieces[n, None]] if (n, None) in pieces else [pieces[n, l] for l in range(DEPTH)]
        outs[n] = _sum_adamw(ps, wd[n], md[n], vd[n], "adamw_" + n)
    res = [loss, grad_x[None]]
    for k in range(4):
        res += [outs[n][k] for n in WEIGHTS]
    return tuple(res)
```

```python
import functools
import math

import jax
import jax.numpy as jnp
from jax import lax
from jax.experimental import pallas as pl
from jax.experimental.pallas import tpu as pltpu

F32 = jnp.float32
BF16 = jnp.bfloat16

N_DEV = 8
D_MODEL = 1024
N_META = 16
HEADS = 8
QK_NOPE = 128
QK_ROPE = 64
V_HEAD = 128
Q_RANK = 256
KV_RANK = 128
ROPE_THETA = 10000.0
LRU_BLOCKS = 8
LRU_C = 8.0
D_FF = 2816
DEPTH = 2
DN_ALPHA = (2.0 * DEPTH) ** 0.25
LN_EPS = 1e-5
RMS_EPS = 1e-6
LN2 = math.log(2.0)
ATT_SCALE = 1.0 / math.sqrt(QK_NOPE + QK_ROPE) / LN2
NEG_BIG = -1e30

ADAM_LR = 0.001
ADAM_B1 = 0.9
ADAM_B2 = 0.999
ADAM_EPS = 1e-08
ADAM_WD = 0.01
ADAM_STEP = 10

MIN_PAD_ROWS = 2
LANE = 128
SUBLANE = 8
VMEM_LIMIT = 56 * 1024 * 1024

PROJ_COLS = 4 * D_MODEL + Q_RANK + KV_RANK + 2 * QK_ROPE
C_LRU_G, C_LRU_X, C_G_MLA, C_G_LRU = 0, D_MODEL, 2 * D_MODEL, 3 * D_MODEL
C_CQ = 4 * D_MODEL
C_CKV = C_CQ + Q_RANK
C_KRP = C_CKV + KV_RANK

WEIGHTS = ['meta_tokens', 'ln0_g', 'ln0_b', 'w_in', 'q_norm', 'kv_norm', 'w_uq', 'w_uk', 'w_uv', 'w_o_mla',
           'lru_conv_w', 'lru_conv_b', 'w_rg', 'b_rg', 'w_ig', 'b_ig', 'lru_lambda', 'w_o_lru', 'w_out',
           'ln1_g', 'ln1_b', 'w_up', 'ffn_conv_w', 'ffn_conv_b', 'w_down', 'ln2_g', 'ln2_b']
SHARD_AXIS = {'meta_tokens': 1, 'w_in': 2, 'w_uq': 1, 'w_o_mla': 1, 'lru_conv_w': 2, 'b_rg': 2, 'b_ig': 2,
              'lru_lambda': 2, 'w_o_lru': 1, 'w_out': 1, 'w_up': 2, 'ffn_conv_w': 2, 'w_down': 1}
BIG = ['w_in', 'w_uq', 'w_o_mla', 'w_o_lru', 'w_out', 'w_up', 'w_down']
SHARDED = [n for n in WEIGHTS if n in SHARD_AXIS]
REPLICATED = [n for n in WEIGHTS if n not in SHARD_AXIS]
LARGE_REPLICATED = ['w_uk', 'w_uv', 'w_rg', 'w_ig']
STAGE_WEIGHTS = [['w_uq', 'lru_conv_w', 'b_rg', 'b_ig', 'lru_lambda'], ['w_o_mla', 'w_o_lru', 'w_out'], ['w_up', 'ffn_conv_w', 'w_down']]
STAGE_REPLICATED = [['q_norm', 'kv_norm', 'w_uk', 'w_uv', 'lru_conv_b', 'w_rg', 'w_ig'], ['ln1_g', 'ln1_b'], ['ffn_conv_b', 'ln2_g', 'ln2_b']]


def _cparams(*sem):
    return pltpu.CompilerParams(dimension_semantics=sem, vmem_limit_bytes=VMEM_LIMIT)


def _tile(n, cap, unit=LANE):
    best = None
    t = unit
    while t <= min(n, cap):
        if n % t == 0:
            best = t
        t += unit
    return n if best is None else best


def _sigmoid(x):
    return 1.0 / (1.0 + jnp.exp(-x))


_GELU_C = math.sqrt(2.0 / math.pi)


def _gelu(x):
    t = jnp.tanh(_GELU_C * (x + 0.044715 * x * x * x))
    return 0.5 * x * (1.0 + t)


def _gelu_and_grad(x):
    t = jnp.tanh(_GELU_C * (x + 0.044715 * x * x * x))
    g = 0.5 * x * (1.0 + t)
    dg = 0.5 * (1.0 + t) + 0.5 * x * (1.0 - t * t) * _GELU_C * (1.0 + 3.0 * 0.044715 * x * x)
    return g, dg


def _softplus_neg(lam):
    z = jnp.exp(-jnp.abs(lam))
    w = 1.0 + z
    log1p = jnp.where(w == 1.0, z, jnp.log(w) * z / (w - 1.0))
    return jnp.maximum(-lam, 0.0) + log1p


def _row_ids(shape, row0=0):
    return lax.broadcasted_iota(jnp.int32, shape, 0) + row0


def _matmul(a, b, name, ta=False, tb=False, out_dtype=F32, tm_cap=1408, tn_cap=1024, tk_cap=2048):
    if ta:
        kdim, m = a.shape
    else:
        m, kdim = a.shape
    if tb:
        n, k2 = b.shape
    else:
        k2, n = b.shape
    assert kdim == k2, (a.shape, b.shape, ta, tb)
    tm, tn, tk = _tile(m, tm_cap), _tile(n, tn_cap), _tile(kdim, tk_cap)
    nk = kdim // tk

    def body(a_ref, b_ref, o_ref, *acc):
        dn = (((0 if ta else 1,), (1 if tb else 0,)), ((), ()))
        part = lax.dot_general(a_ref[...].astype(BF16), b_ref[...].astype(BF16), dn, preferred_element_type=F32)
        if nk == 1:
            o_ref[...] = part.astype(o_ref.dtype)
            return
        acc_ref, k = acc[0], pl.program_id(2)

        @pl.when(k == 0)
        def _():
            acc_ref[...] = part

        @pl.when(k > 0)
        def _():
            acc_ref[...] += part

        @pl.when(k == nk - 1)
        def _():
            o_ref[...] = acc_ref[...].astype(o_ref.dtype)

    a_spec = pl.BlockSpec((tk, tm), lambda i, j, k: (k, i)) if ta else pl.BlockSpec((tm, tk), lambda i, j, k: (i, k))
    b_spec = pl.BlockSpec((tn, tk), lambda i, j, k: (j, k)) if tb else pl.BlockSpec((tk, tn), lambda i, j, k: (k, j))
    return pl.pallas_call(
        body, name=name,
        grid=(m // tm, n // tn, nk),
        in_specs=[a_spec, b_spec],
        out_specs=pl.BlockSpec((tm, tn), lambda i, j, k: (i, j)),
        out_shape=jax.ShapeDtypeStruct((m, n), out_dtype),
        scratch_shapes=[pltpu.VMEM((tm, tn), F32)] if nk > 1 else [],
        compiler_params=_cparams("parallel", "parallel", "arbitrary"),
    )(a, b)


class Rw:
    def __init__(self, arr, width=None, cb=0):
        self.arr, self.width, self.cb = arr, (arr.shape[1] if width is None else width), cb


class Pm:
    def __init__(self, arr):
        self.arr = arr


def _rows(fn, name, ins, outs, accs=(), tm_cap=384):
    tp = next(o.arr.shape[0] for o in ins if isinstance(o, Rw))
    tm = _tile(tp, tm_cap)
    n_in, n_out, n_acc = len(ins), len(outs), len(accs)

    def body(*refs):
        i = pl.program_id(0)
        res = fn(i * tm, *[r[...] for r in refs[:n_in]])
        if not isinstance(res, (tuple, list)):
            res = (res,)
        assert len(res) == n_out + n_acc, (name, len(res))
        for k in range(n_out):
            refs[n_in + k][...] = res[k].astype(refs[n_in + k].dtype)
        for k in range(n_acc):
            ref = refs[n_in + n_out + k]

            @pl.when(i == 0)
            def _():
                ref[...] = jnp.zeros_like(ref)

            ref[...] += res[n_out + k]

    in_specs = []
    for o in ins:
        if isinstance(o, Rw):
            in_specs.append(pl.BlockSpec((tm, o.width), functools.partial(lambda i, cb: (i, cb), cb=o.cb)))
        else:
            in_specs.append(pl.BlockSpec(o.arr.shape, functools.partial(lambda i, nd: (0,) * nd, nd=o.arr.ndim)))
    out_specs = [pl.BlockSpec((tm, w), lambda i: (i, 0)) for (w, _) in outs]
    out_specs += [pl.BlockSpec(s, functools.partial(lambda i, nd: (0,) * nd, nd=len(s))) for s in accs]
    out_shape = [jax.ShapeDtypeStruct((tp, w), dt) for (w, dt) in outs]
    out_shape += [jax.ShapeDtypeStruct(s, F32) for s in accs]
    res = pl.pallas_call(
        body, name=name, grid=(tp // tm,), in_specs=in_specs, out_specs=out_specs, out_shape=out_shape,
        compiler_params=_cparams("arbitrary"),
    )(*[o.arr for o in ins])
    return res


class Cl:
    def __init__(self, arr, col0=0, width=None, stride=1):
        self.arr, self.col0, self.width, self.stride = arr, col0, width, stride


def _cols(fn, name, ins, outs, ncols, tc):
    assert ncols % tc == 0
    n_in, n_out = len(ins), len(outs)
    outs = [(o[0], o[1], o[2] if len(o) > 2 else tc) for o in outs]

    def body(*refs):
        res = fn(*[r[...] for r in refs[:n_in]])
        if not isinstance(res, (tuple, list)):
            res = (res,)
        assert len(res) == n_out, (name, len(res))
        for k in range(n_out):
            refs[n_in + k][...] = res[k].astype(refs[n_in + k].dtype)

    in_specs = []
    for o in ins:
        wd = tc if o.width is None else o.width
        assert o.col0 % wd == 0, (name, o.col0, wd)
        in_specs.append(pl.BlockSpec((o.arr.shape[0], wd),
                                     functools.partial(lambda j, off, st: (0, st * j + off), off=o.col0 // wd, st=o.stride)))
    out_specs = [pl.BlockSpec((r, wd), lambda j: (0, j)) for (r, _, wd) in outs]
    out_shape = [jax.ShapeDtypeStruct((r, ncols // tc * wd), dt) for (r, dt, wd) in outs]
    return pl.pallas_call(
        body, name=name, grid=(ncols // tc,), in_specs=in_specs, out_specs=out_specs, out_shape=out_shape,
        compiler_params=_cparams("parallel"),
    )(*[o.arr for o in ins])


def _ln_stats(u):
    mu = jnp.mean(u, axis=-1, keepdims=True)
    xc = u - mu
    var = jnp.mean(xc * xc, axis=-1, keepdims=True)
    rstd = lax.rsqrt(var + LN_EPS)
    return xc * rstd, rstd


def _ln_fwd(terms, g, b, name):
    coefs = [c for c, _ in terms]

    def fn(row0, *blk):
        xs, (gg, bb) = blk[:len(coefs)], blk[len(coefs):]
        u = sum(c * x for c, x in zip(coefs, xs))
        xhat, _ = _ln_stats(u)
        return xhat * gg + bb

    d = terms[0][1].shape[1]
    return _rows(fn, name, [Rw(x) for _, x in terms] + [Pm(g.reshape(1, d)), Pm(b.reshape(1, d))], [(d, F32)])[0]


def _ln_bwd(dy_terms, u_terms, g, name):
    dc = [c for c, _ in dy_terms]
    uc = [c for c, _ in u_terms]
    d = u_terms[0][1].shape[1]

    def fn(row0, *blk):
        dys = blk[:len(dc)]
        xs = blk[len(dc):len(dc) + len(uc)]
        gg = blk[-1]
        dy = sum(c * x for c, x in zip(dc, dys))
        u = sum(c * x for c, x in zip(uc, xs))
        xhat, rstd = _ln_stats(u)
        gdy = dy * gg
        m1 = jnp.mean(gdy, axis=-1, keepdims=True)
        m2 = jnp.mean(gdy * xhat, axis=-1, keepdims=True)
        du = rstd * (gdy - m1 - xhat * m2)
        return du, jnp.sum(dy * xhat, axis=0, keepdims=True), jnp.sum(dy, axis=0, keepdims=True)

    ins = [Rw(x) for _, x in dy_terms] + [Rw(x) for _, x in u_terms] + [Pm(g.reshape(1, d))]
    return _rows(fn, name, ins, [(d, F32)], accs=[(1, d), (1, d)])


def _loss_head(y, tgt, t_real, name):
    d = y.shape[1]

    def fn(row0, yb, tb):
        rows = _row_ids(yb.shape, row0)
        live = (rows >= N_META) & (rows < t_real)
        diff = jnp.where(live, yb - tb, 0.0)
        return diff * (1.0 / d), jnp.sum(diff * diff, axis=0, keepdims=True) * (0.5 / d)

    return _rows(fn, name, [Rw(y), Rw(tgt)], [(d, F32)], accs=[(1, d)])


def _rms(x, g):
    r = lax.rsqrt(jnp.mean(x * x, axis=-1, keepdims=True) + RMS_EPS)
    return x * r * g


def _rms_bwd(dy, x, g):
    r = lax.rsqrt(jnp.mean(x * x, axis=-1, keepdims=True) + RMS_EPS)
    gdy = dy * g
    dx = r * gdy - x * (r * r * r) * jnp.mean(gdy * x, axis=-1, keepdims=True)
    return dx, jnp.sum(dy * x * r, axis=0, keepdims=True)


def _mla_norms(proj, qn, kvn, name):
    def fn(row0, cq, ckv, g1, g2):
        return _rms(cq, g1), _rms(ckv, g2)

    return _rows(fn, name, [Rw(proj, Q_RANK, C_CQ // Q_RANK), Rw(proj, KV_RANK, C_CKV // KV_RANK),
                            Pm(qn.reshape(1, Q_RANK)), Pm(kvn.reshape(1, KV_RANK))],
                 [(Q_RANK, F32), (KV_RANK, F32)])


def _mla_norms_bwd(dcqn, dckvn, proj, qn, kvn, name):
    def fn(row0, d1, d2, cq, ckv, g1, g2):
        dx1, dg1 = _rms_bwd(d1, cq, g1)
        dx2, dg2 = _rms_bwd(d2, ckv, g2)
        return dx1, dx2, dg1, dg2

    return _rows(fn, name, [Rw(dcqn), Rw(dckvn), Rw(proj, Q_RANK, C_CQ // Q_RANK), Rw(proj, KV_RANK, C_CKV // KV_RANK),
                            Pm(qn.reshape(1, Q_RANK)), Pm(kvn.reshape(1, KV_RANK))],
                 [(Q_RANK, F32), (KV_RANK, F32)], accs=[(1, Q_RANK), (1, KV_RANK)])


def _fold_rope(z):
    return z + pltpu.roll(z, QK_ROPE, 1)


def _mla_pack(qext, kv, proj, cs, name):
    tp = qext.shape[0]
    tm = _tile(tp, 384)
    hw, nope_all = 2 * LANE, HEADS * QK_NOPE

    def body(q_ref, kv_ref, kr_ref, cs_ref, qo_ref, ko_ref, vo_ref):
        cs_ = cs_ref[...]
        low = lax.broadcasted_iota(jnp.int32, cs_.shape, 1) < QK_ROPE
        kr = _fold_rope(kr_ref[...] * cs_).astype(BF16)
        for h in range(HEADS):
            qr = jnp.where(low, _fold_rope(q_ref[:, h * hw + QK_NOPE:(h + 1) * hw] * cs_), 0.0)
            qo_ref[:, h * hw:h * hw + QK_NOPE] = (q_ref[:, h * hw:h * hw + QK_NOPE] * ATT_SCALE).astype(BF16)
            qo_ref[:, h * hw + QK_NOPE:(h + 1) * hw] = (qr * ATT_SCALE).astype(BF16)
            ko_ref[:, h * hw:h * hw + QK_NOPE] = kv_ref[:, h * QK_NOPE:(h + 1) * QK_NOPE].astype(BF16)
            ko_ref[:, h * hw + QK_NOPE:(h + 1) * hw] = kr
        vo_ref[...] = kv_ref[:, nope_all:].astype(BF16)

    row = lambda w: pl.BlockSpec((tm, w), lambda i: (i, 0))
    return pl.pallas_call(
        body, name=name, grid=(tp // tm,),
        in_specs=[row(HEADS * hw), row(2 * nope_all), pl.BlockSpec((tm, LANE), lambda i: (i, C_KRP // LANE)), row(LANE)],
        out_specs=[row(HEADS * hw), row(HEADS * hw), row(nope_all)],
        out_shape=[jax.ShapeDtypeStruct((tp, HEADS * hw), BF16),
                   jax.ShapeDtypeStruct((tp, HEADS * hw), BF16),
                   jax.ShapeDtypeStruct((tp, nope_all), BF16)],
        compiler_params=_cparams("parallel"),
    )(qext, kv, proj, cs)


def _mla_unpack(dq, dk, dv, cs, name):
    tp = dq.shape[0]
    tm = _tile(tp, 384)
    hw, nope_all = 2 * LANE, HEADS * QK_NOPE

    def body(dq_ref, dk_ref, dv_ref, cs_ref, dqe_ref, dkv_ref, dkr_ref):
        cs_ = cs_ref[...]
        low = lax.broadcasted_iota(jnp.int32, cs_.shape, 1) < QK_ROPE
        dkr = None
        for h in range(HEADS):
            dqe_ref[:, h * hw:h * hw + QK_NOPE] = dq_ref[:, h * hw:h * hw + QK_NOPE] * ATT_SCALE
            dqr = jnp.where(low, dq_ref[:, h * hw + QK_NOPE:(h + 1) * hw], 0.0) * ATT_SCALE
            dqe_ref[:, h * hw + QK_NOPE:(h + 1) * hw] = _fold_rope(dqr) * cs_
            dkv_ref[:, h * QK_NOPE:(h + 1) * QK_NOPE] = dk_ref[:, h * hw:h * hw + QK_NOPE]
            part = jnp.where(low, dk_ref[:, h * hw + QK_NOPE:(h + 1) * hw], 0.0)
            dkr = part if h == 0 else dkr + part
        dkv_ref[:, nope_all:] = dv_ref[...]
        dkr_ref[...] = _fold_rope(dkr) * cs_

    row = lambda w: pl.BlockSpec((tm, w), lambda i: (i, 0))
    return pl.pallas_call(
        body, name=name, grid=(tp // tm,),
        in_specs=[row(HEADS * hw), row(HEADS * hw), row(nope_all), row(LANE)],
        out_specs=[row(HEADS * hw), row(2 * nope_all), row(LANE)],
        out_shape=[jax.ShapeDtypeStruct((tp, HEADS * hw), F32),
                   jax.ShapeDtypeStruct((tp, 2 * nope_all), F32),
                   jax.ShapeDtypeStruct((tp, LANE), F32)],
        compiler_params=_cparams("parallel"),
    )(dq, dk, dv, cs)


def _attn_fwd(q, k, v, t_real, name):
    tp = q.shape[0]
    tq = _tile(tp, 1408)
    tkc = _tile(tp, 1408)
    nkc = -(-t_real // tkc)

    def body(q_ref, k_ref, v_ref, o_ref, lse_ref):
        qb = q_ref[...]
        m = l = acc = None
        for c in range(nkc):
            s = lax.dot_general(qb, k_ref[c * tkc:(c + 1) * tkc, :], (((1,), (1,)), ((), ())), preferred_element_type=F32)
            if (c + 1) * tkc > t_real:
                cols = lax.broadcasted_iota(jnp.int32, s.shape, 1) + c * tkc
                s = jnp.where(cols < t_real, s, NEG_BIG)
            mc = jnp.max(s, axis=-1, keepdims=True)
            m_new = mc if c == 0 else jnp.maximum(m, mc)
            p = jnp.exp2(s - m_new)
            lc = jnp.sum(p, axis=-1, keepdims=True)
            pv = jnp.dot(p.astype(BF16), v_ref[c * tkc:(c + 1) * tkc, :], preferred_element_type=F32)
            if c == 0:
                l, acc = lc, pv
            else:
                alpha = jnp.exp2(m - m_new)
                l, acc = alpha * l + lc, alpha * acc + pv
            m = m_new
        o_ref[...] = acc / l
        lse_ref[...] = m + jnp.log2(l)

    return pl.pallas_call(
        body, name=name, grid=(HEADS, tp // tq),
        in_specs=[pl.BlockSpec((tq, 2 * LANE), lambda h, i: (i, h)),
                  pl.BlockSpec((tp, 2 * LANE), lambda h, i: (0, h)),
                  pl.BlockSpec((tp, LANE), lambda h, i: (0, h))],
        out_specs=[pl.BlockSpec((tq, LANE), lambda h, i: (i, h)),
                   pl.BlockSpec((None, tq, 1), lambda h, i: (h, i, 0))],
        out_shape=[jax.ShapeDtypeStruct((tp, HEADS * LANE), F32),
                   jax.ShapeDtypeStruct((HEADS, tp, 1), F32)],
        compiler_params=_cparams("parallel", "parallel"),
    )(q, k, v)


def _attn_bwd(q, k, v, do, o, lse, t_real, name):
    tp = q.shape[0]
    tq = _tile(tp, 704, 64)
    tkc = _tile(tp, 1408)
    nkc = -(-t_real // tkc)

    def body(q_ref, k_ref, v_ref, do_ref, o_ref, lse_ref, dq_ref, dk_ref, dv_ref):
        i = pl.program_id(1)

        @pl.when(i == 0)
        def _():
            dk_ref[...] = jnp.zeros_like(dk_ref)
            dv_ref[...] = jnp.zeros_like(dv_ref)

        qb = q_ref[...]
        dob = do_ref[...]
        dob16 = dob.astype(BF16)
        dol2 = (dob * LN2).astype(BF16)
        delta = jnp.sum(dob * o_ref[...], axis=-1, keepdims=True) * LN2
        lse = lse_ref[...]
        dq = None
        for c in range(nkc):
            ks = slice(c * tkc, (c + 1) * tkc)
            kb = k_ref[ks, :]
            s = lax.dot_general(qb, kb, (((1,), (1,)), ((), ())), preferred_element_type=F32)
            p = jnp.exp2(s - lse)
            if (c + 1) * tkc > t_real:
                cols = lax.broadcasted_iota(jnp.int32, s.shape, 1) + c * tkc
                p = jnp.where(cols < t_real, p, 0.0)
            dp = lax.dot_general(dol2, v_ref[ks, :], (((1,), (1,)), ((), ())), preferred_element_type=F32)
            ds = (p * (dp - delta)).astype(BF16)
            dqc = jnp.dot(ds, kb, preferred_element_type=F32)
            dq = dqc if c == 0 else dq + dqc
            dk_ref[ks, :] += lax.dot_general(ds, qb, (((0,), (0,)), ((), ())), preferred_element_type=F32)
            dv_ref[ks, :] += lax.dot_general(p.astype(BF16), dob16, (((0,), (0,)), ((), ())), preferred_element_type=F32)
        dq_ref[...] = dq

    return pl.pallas_call(
        body, name=name, grid=(HEADS, tp // tq),
        in_specs=[pl.BlockSpec((tq, 2 * LANE), lambda h, i: (i, h)),
                  pl.BlockSpec((tp, 2 * LANE), lambda h, i: (0, h)),
                  pl.BlockSpec((tp, LANE), lambda h, i: (0, h)),
                  pl.BlockSpec((tq, LANE), lambda h, i: (i, h)),
                  pl.BlockSpec((tq, LANE), lambda h, i: (i, h)),
                  pl.BlockSpec((None, tq, 1), lambda h, i: (h, i, 0))],
        out_specs=[pl.BlockSpec((tq, 2 * LANE), lambda h, i: (i, h)),
                   pl.BlockSpec((tp, 2 * LANE), lambda h, i: (0, h)),
                   pl.BlockSpec((tp, LANE), lambda h, i: (0, h))],
        out_shape=[jax.ShapeDtypeStruct((tp, HEADS * 2 * LANE), F32),
                   jax.ShapeDtypeStruct((tp, HEADS * 2 * LANE), F32),
                   jax.ShapeDtypeStruct((tp, HEADS * LANE), F32)],
        compiler_params=_cparams("parallel", "arbitrary"),
    )(q, k, v, do, o, lse)


def _shift_rows(x, s):
    tp = x.shape[0]
    return x if s % tp == 0 else pltpu.roll(x, s % tp, 0)


def _conv_fwd_val(xm, w, b, pad_left):
    acc = b + w[0:1, :] * _shift_rows(xm, pad_left)
    for k in range(1, w.shape[0]):
        acc = acc + w[k:k + 1, :] * _shift_rows(xm, pad_left - k)
    return acc


def _conv_bwd_val(dy, xm, w, pad_left, live):
    kk = w.shape[0]
    dx = w[0:1, :] * _shift_rows(dy, -pad_left)
    dws = [jnp.sum(dy * _shift_rows(xm, pad_left), axis=0, keepdims=True)]
    for k in range(1, kk):
        dx = dx + w[k:k + 1, :] * _shift_rows(dy, k - pad_left)
        dws.append(jnp.sum(dy * _shift_rows(xm, pad_left - k), axis=0, keepdims=True))
    return jnp.where(live, dx, 0.0), jnp.concatenate(dws, axis=0), jnp.sum(dy, axis=0, keepdims=True)


def _lru_conv_fwd(proj, w, b, t_real, name):
    def fn(x, ww, bb):
        xm = jnp.where(_row_ids(x.shape) < t_real, x, 0.0)
        return _conv_fwd_val(xm, ww, bb, 2)

    return _cols(fn, name, [Cl(proj, C_LRU_X), Cl(w), Cl(b.reshape(1, -1))], [(proj.shape[0], F32)], D_MODEL, 128)[0]


def _lru_conv_bwd(dxc, proj, w, t_real, name):
    def fn(dy, x, ww):
        live = _row_ids(x.shape) < t_real
        xm = jnp.where(live, x, 0.0)
        dym = jnp.where(live, dy, 0.0)
        return _conv_bwd_val(dym, xm, ww, 2, live)

    return _cols(fn, name, [Cl(dxc), Cl(proj, C_LRU_X), Cl(w)],
                 [(proj.shape[0], F32), (w.shape[0], F32), (1, F32)], D_MODEL, 128)


def _ffn_conv_act(up, w, b, t_real, name):
    def fn(g, v, wg, wv, bg, bv):
        live = _row_ids(g.shape) < t_real
        gc = _conv_fwd_val(jnp.where(live, g, 0.0), wg, bg, 1)
        vc = _conv_fwd_val(jnp.where(live, v, 0.0), wv, bv, 1)
        return _gelu(gc) * vc

    b2 = b.reshape(1, -1)
    return _cols(fn, name, [Cl(up), Cl(up, D_FF), Cl(w), Cl(w, D_FF), Cl(b2), Cl(b2, D_FF)],
                 [(up.shape[0], F32)], D_FF, 128)[0]


def _ffn_conv_act_bwd(dm, up, w, b, t_real, name):
    def fn(dmb, g, v, wg, wv, bg, bv):
        live = _row_ids(g.shape) < t_real
        gm, vm = jnp.where(live, g, 0.0), jnp.where(live, v, 0.0)
        gc = _conv_fwd_val(gm, wg, bg, 1)
        vc = _conv_fwd_val(vm, wv, bv, 1)
        act, dact = _gelu_and_grad(gc)
        dmm = jnp.where(live, dmb, 0.0)
        dgx, dwg, dbg = _conv_bwd_val(dmm * vc * dact, gm, wg, 1, live)
        dvx, dwv, dbv = _conv_bwd_val(dmm * act, vm, wv, 1, live)
        return dgx, dvx, dwg, dwv, dbg, dbv

    b2 = b.reshape(1, -1)
    tp, kk = up.shape[0], w.shape[0]
    return _cols(fn, name, [Cl(dm), Cl(up), Cl(up, D_FF), Cl(w), Cl(w, D_FF), Cl(b2), Cl(b2, D_FF)],
                 [(tp, F32), (tp, F32), (kk, F32), (kk, F32), (1, F32), (1, F32)], D_FF, 128)


def _lru_gates_fwd(xc, wg, b4, lam, t_real, name):
    tp = xc.shape[0]
    tm = _tile(tp, 1408)

    def body(x_ref, w_ref, b_ref, lam_ref, r0_ref, r1_ref, i0_ref, i1_ref, a0_ref, a1_ref, u0_ref, u1_ref):
        x = x_ref[...]
        xb = x.astype(BF16)
        live = _row_ids(x.shape, pl.program_id(1) * tm) < t_real
        bb = b_ref[...]
        sp = _softplus_neg(lam_ref[...])
        gate = [_sigmoid(jnp.dot(xb, w_ref[k], preferred_element_type=F32) + bb[k:k + 1, :]) for k in range(4)]
        for d, (r_ref, i_ref, a_ref, u_ref) in enumerate(((r0_ref, i0_ref, a0_ref, u0_ref), (r1_ref, i1_ref, a1_ref, u1_ref))):
            r, ig = gate[d], gate[2 + d]
            a = jnp.exp(-LRU_C * r * sp[d:d + 1, :])
            r_ref[...] = r
            i_ref[...] = ig
            a_ref[...] = a
            u_ref[...] = jnp.where(live, jnp.sqrt(1.0 - a * a) * (ig * x), 0.0)

    blk = pl.BlockSpec((tm, LANE), lambda g, i: (i, g))
    return pl.pallas_call(
        body, name=name, grid=(LRU_BLOCKS, tp // tm),
        in_specs=[blk, pl.BlockSpec((None, 4, LANE, LANE), lambda g, i: (g, 0, 0, 0)),
                  pl.BlockSpec((4, LANE), lambda g, i: (0, g)), pl.BlockSpec((2, LANE), lambda g, i: (0, g))],
        out_specs=[blk] * 8,
        out_shape=[jax.ShapeDtypeStruct((tp, D_MODEL), F32)] * 8,
        compiler_params=_cparams("parallel", "parallel"),
    )(xc, wg, b4, lam)


def _lru_gates_bwd(l0, l1, da0, da1, r0, r1, i0, i1, a0, a1, xc, wg, lam, t_real, name):
    tp = xc.shape[0]
    tm = _tile(tp, 1408)

    def body(l0_ref, l1_ref, da0_ref, da1_ref, r0_ref, r1_ref, i0_ref, i1_ref, a0_ref, a1_ref, x_ref, w_ref, lam_ref,
             dx_ref, dw_ref, db_ref, dlam_ref):
        i = pl.program_id(1)
        x = x_ref[...]
        xb = x.astype(BF16)
        live = _row_ids(x.shape, i * tm) < t_real
        lam_ = lam_ref[...]
        sp = _softplus_neg(lam_)
        dsp_dlam = -_sigmoid(-lam_)
        dx = jnp.zeros_like(x)
        dpre = [None] * 4
        dlam_rows = []
        for d, (l_ref, da_ref, r_ref, i_ref, a_ref) in enumerate(((l0_ref, da0_ref, r0_ref, i0_ref, a0_ref),
                                                                  (l1_ref, da1_ref, r1_ref, i1_ref, a1_ref))):
            r, ig, a = r_ref[...], i_ref[...], a_ref[...]
            du = jnp.where(live, l_ref[...], 0.0)
            s = jnp.sqrt(1.0 - a * a)
            dv = du * s
            ds = du * (ig * x)
            dla = jnp.where(live, da_ref[...], 0.0) * a - ds * (a * a) / s
            dla = jnp.where(live, dla, 0.0)
            dr = dla * (-LRU_C) * sp[d:d + 1, :]
            dlam_rows.append(jnp.sum(dla * (-LRU_C) * r, axis=0, keepdims=True) * dsp_dlam[d:d + 1, :])
            dpre[d] = dr * r * (1.0 - r)
            dpre[2 + d] = dv * x * ig * (1.0 - ig)
            dx = dx + dv * ig

        @pl.when(i == 0)
        def _():
            dw_ref[...] = jnp.zeros_like(dw_ref)
            db_ref[...] = jnp.zeros_like(db_ref)
            dlam_ref[...] = jnp.zeros_like(dlam_ref)

        for k in range(4):
            pk = dpre[k].astype(BF16)
            dx = dx + lax.dot_general(pk, w_ref[k], (((1,), (1,)), ((), ())), preferred_element_type=F32)
            dw_ref[k] += lax.dot_general(xb, pk, (((0,), (0,)), ((), ())), preferred_element_type=F32)
        db_ref[...] += jnp.concatenate([jnp.sum(p, axis=0, keepdims=True) for p in dpre], axis=0)
        dlam_ref[...] += jnp.concatenate(dlam_rows, axis=0)
        dx_ref[...] = dx

    blk = pl.BlockSpec((tm, LANE), lambda g, i: (i, g))
    return pl.pallas_call(
        body, name=name, grid=(LRU_BLOCKS, tp // tm),
        in_specs=[blk] * 11 + [pl.BlockSpec((None, 4, LANE, LANE), lambda g, i: (g, 0, 0, 0)),
                               pl.BlockSpec((2, LANE), lambda g, i: (0, g))],
        out_specs=[blk, pl.BlockSpec((None, 4, LANE, LANE), lambda g, i: (g, 0, 0, 0)),
                   pl.BlockSpec((4, LANE), lambda g, i: (0, g)), pl.BlockSpec((2, LANE), lambda g, i: (0, g))],
        out_shape=[jax.ShapeDtypeStruct((tp, D_MODEL), F32), jax.ShapeDtypeStruct((LRU_BLOCKS, 4, LANE, LANE), F32),
                   jax.ShapeDtypeStruct((4, D_MODEL), F32), jax.ShapeDtypeStruct((2, D_MODEL), F32)],
        compiler_params=_cparams("parallel", "arbitrary"),
    )(l0, l1, da0, da1, r0, r1, i0, i1, a0, a1, xc, wg, lam)


def _tile_scan(a, u, reverse):
    rows = lax.broadcasted_iota(jnp.int32, a.shape, 0)
    for s in (1, 2, 4):
        if reverse:
            keep = rows < SUBLANE - s
            a_sh, u_sh = pltpu.roll(a, SUBLANE - s, 0), pltpu.roll(u, SUBLANE - s, 0)
        else:
            keep = rows >= s
            a_sh, u_sh = pltpu.roll(a, s, 0), pltpu.roll(u, s, 0)
        u = u + a * jnp.where(keep, u_sh, 0.0)
        a = a * jnp.where(keep, a_sh, 1.0)
    return a, u


def _scan_fwd(a0, u0, a1, u1, name):
    tp, d = a0.shape
    tc = 128
    nt = tp // SUBLANE

    def body(a0_ref, u0_ref, a1_ref, u1_ref, h0_ref, h1_ref):
        def step(t, carry):
            c0, c1 = carry
            f = pl.multiple_of(t * SUBLANE, SUBLANE)
            b = pl.multiple_of((nt - 1 - t) * SUBLANE, SUBLANE)
            pa, pu = _tile_scan(a0_ref[pl.ds(f, SUBLANE), :], u0_ref[pl.ds(f, SUBLANE), :], False)
            h = pu + pa * c0
            h0_ref[pl.ds(f, SUBLANE), :] = h
            c0 = h[SUBLANE - 1:SUBLANE, :]
            pa, pu = _tile_scan(a1_ref[pl.ds(b, SUBLANE), :], u1_ref[pl.ds(b, SUBLANE), :], True)
            h = pu + pa * c1
            h1_ref[pl.ds(b, SUBLANE), :] = h
            c1 = h[0:1, :]
            return c0, c1

        z = jnp.zeros((1, tc), F32)
        lax.fori_loop(0, nt, step, (z, z))

    blk = pl.BlockSpec((tp, tc), lambda j: (0, j))
    return pl.pallas_call(
        body, name=name, grid=(d // tc,), in_specs=[blk] * 4, out_specs=[blk] * 2,
        out_shape=[jax.ShapeDtypeStruct((tp, d), F32)] * 2,
        compiler_params=_cparams("parallel"),
    )(a0, u0, a1, u1)


def _scan_bwd(dh, a0, a1, h0, h1, name):
    tp, d = dh.shape
    tc = 128
    nt = tp // SUBLANE

    def body(dh_ref, a0_ref, a1_ref, h0_ref, h1_ref, l0_ref, l1_ref, da0_ref, da1_ref):
        rows8 = lax.broadcasted_iota(jnp.int32, (SUBLANE, tc), 0)

        def step(t, carry):
            c0, c1 = carry
            b = pl.multiple_of((nt - 1 - t) * SUBLANE, SUBLANE)
            f = pl.multiple_of(t * SUBLANE, SUBLANE)
            a = a0_ref[pl.ds(b, SUBLANE), :]
            a_next = jnp.where(rows8 < SUBLANE - 1, pltpu.roll(a, SUBLANE - 1, 0), 1.0)
            pa, pu = _tile_scan(a_next, dh_ref[pl.ds(b, SUBLANE), :], True)
            lam = pu + pa * c0
            l0_ref[pl.ds(b, SUBLANE), :] = lam
            c0 = a[0:1, :] * lam[0:1, :]
            a = a1_ref[pl.ds(f, SUBLANE), :]
            a_prev = jnp.where(rows8 >= 1, pltpu.roll(a, 1, 0), 1.0)
            pa, pu = _tile_scan(a_prev, dh_ref[pl.ds(f, SUBLANE), :], False)
            lam = pu + pa * c1
            l1_ref[pl.ds(f, SUBLANE), :] = lam
            c1 = a[SUBLANE - 1:SUBLANE, :] * lam[SUBLANE - 1:SUBLANE, :]
            return c0, c1

        z = jnp.zeros((1, tc), F32)
        lax.fori_loop(0, nt, step, (z, z))
        rows = lax.broadcasted_iota(jnp.int32, (tp, tc), 0)
        da0_ref[...] = l0_ref[...] * jnp.where(rows >= 1, pltpu.roll(h0_ref[...], 1, 0), 0.0)
        da1_ref[...] = l1_ref[...] * jnp.where(rows < tp - 1, pltpu.roll(h1_ref[...], tp - 1, 0), 0.0)

    blk = pl.BlockSpec((tp, tc), lambda j: (0, j))
    return pl.pallas_call(
        body, name=name, grid=(d // tc,), in_specs=[blk] * 5, out_specs=[blk] * 4,
        out_shape=[jax.ShapeDtypeStruct((tp, d), F32)] * 4,
        compiler_params=_cparams("parallel"),
    )(dh, a0, a1, h0, h1)


def _gated_h(proj, h0, h1, name):
    def fn(row0, lg, x0, x1):
        return _gelu(lg) * (x0 + x1)

    return _rows(fn, name, [Rw(proj, D_MODEL, C_LRU_G // D_MODEL), Rw(h0), Rw(h1)], [(D_MODEL, F32)])[0]


def _gated_h_bwd(dgh, proj, h0, h1, name):
    def fn(row0, dg, lg, x0, x1):
        act, dact = _gelu_and_grad(lg)
        return dg * (x0 + x1) * dact, dg * act

    return _rows(fn, name, [Rw(dgh), Rw(proj, D_MODEL, C_LRU_G // D_MODEL), Rw(h0), Rw(h1)], [(D_MODEL, F32)] * 2)


def _mix(proj, y_mla, y_lru, name):
    def fn(row0, gm, gl, ym, yl):
        return _sigmoid(gm) * ym + _sigmoid(gl) * yl

    return _rows(fn, name, [Rw(proj, D_MODEL, C_G_MLA // D_MODEL), Rw(proj, D_MODEL, C_G_LRU // D_MODEL), Rw(y_mla), Rw(y_lru)],
                 [(D_MODEL, F32)])[0]


def _mix_bwd(dz, proj, y_mla, y_lru, name):
    def fn(row0, dzb, gm, gl, ym, yl):
        sm, sl = _sigmoid(gm), _sigmoid(gl)
        return dzb * sm, dzb * sl, dzb * ym * sm * (1.0 - sm), dzb * yl * sl * (1.0 - sl)

    return _rows(fn, name, [Rw(dz), Rw(proj, D_MODEL, C_G_MLA // D_MODEL), Rw(proj, D_MODEL, C_G_LRU // D_MODEL),
                            Rw(y_mla), Rw(y_lru)], [(D_MODEL, F32)] * 4)


def _layer_fwd(h, w_in, more_weights, cs, t_real, tag):
    proj = _matmul(h, w_in, tag + "proj")
    w = dict(more_weights(0, proj), w_in=w_in)
    cqn, ckvn = _mla_norms(proj, w['q_norm'], w['kv_norm'], tag + "mla_norms")
    qext = _matmul(cqn, w['w_q'], tag + "q_up")
    kv = _matmul(ckvn, w['w_kv'], tag + "kv_up")
    qc, kc, vb = _mla_pack(qext, kv, proj, cs, tag + "mla_pack")
    o, lse = _attn_fwd(qc, kc, vb, t_real, tag + "attn_fwd")
    w.update(more_weights(1, o))
    y_mla = _matmul(o, w['w_o_mla'], tag + "o_mla")
    xc = _lru_conv_fwd(proj, w['lru_conv_w'], w['lru_conv_b'], t_real, tag + "lru_conv")
    r0, r1, i0, i1, a0, a1, u0, u1 = _lru_gates_fwd(xc, w['w_g'], w['b4'], w['lru_lambda'], t_real, tag + "lru_gates")
    h0, h1 = _scan_fwd(a0, u0, a1, u1, tag + "lru_scan")
    gh = _gated_h(proj, h0, h1, tag + "lru_gate_out")
    y_lru = _matmul(gh, w['w_o_lru'], tag + "o_lru")
    z = _mix(proj, y_mla, y_lru, tag + "mix")
    zo = _matmul(z, w['w_out'], tag + "w_out")
    hm = _ln_fwd([(DN_ALPHA, h), (1.0, zo)], w['ln1_g'], w['ln1_b'], tag + "ln1")
    w.update(more_weights(2, hm))
    up = _matmul(hm, w['w_up'], tag + "w_up")
    m = _ffn_conv_act(up, w['ffn_conv_w'], w['ffn_conv_b'], t_real, tag + "ffn_conv")
    f = _matmul(m, w['w_down'], tag + "w_down", tk_cap=1408)
    out = _ln_fwd([(DN_ALPHA, hm), (1.0, f)], w['ln2_g'], w['ln2_b'], tag + "ln2")
    saved = dict(w=w, h=h, proj=proj, cqn=cqn, ckvn=ckvn, qc=qc, kc=kc, vb=vb, o=o, lse=lse, y_mla=y_mla, xc=xc,
                 r0=r0, r1=r1, i0=i0, i1=i1, a0=a0, a1=a1, h0=h0, h1=h1, gh=gh, y_lru=y_lru, z=z, zo=zo, hm=hm,
                 up=up, m=m, f=f)
    return out, saved


DW_MATMUL = dict(ta=True, out_dtype=BF16, tn_cap=1408, tk_cap=1408)


def _after(a, tok):
    return a if tok is None else a + tok.astype(a.dtype)


def _layer_bwd(dout_terms, s, cs, t_real, tag, emit, tok):
    w = s['w']
    g = {}
    du2, dg2, db2 = _ln_bwd(dout_terms, [(DN_ALPHA, s['hm']), (1.0, s['f'])], _after(w['ln2_g'], tok), tag + "ln2_bwd")
    g['ln2_g'], g['ln2_b'] = dg2, db2
    dm = _matmul(du2, w['w_down'], tag + "w_down_dx", tb=True)
    g['w_down'] = _matmul(s['m'], du2, tag + "w_down_dw", **DW_MATMUL)
    dgp, dvp, dwg_, dwv_, dbg_, dbv_ = _ffn_conv_act_bwd(dm, s['up'], w['ffn_conv_w'], w['ffn_conv_b'], t_real, tag + "ffn_conv_bwd")
    g['ffn_conv_w'] = jnp.concatenate([dwg_, dwv_], axis=1)
    g['ffn_conv_b'] = jnp.concatenate([dbg_, dbv_], axis=1)
    dup = jnp.concatenate([dgp, dvp], axis=1)
    dhm_mm = _matmul(dup, w['w_up'], tag + "w_up_dx", tb=True, tk_cap=1408)
    g['w_up'] = _matmul(s['hm'], dup, tag + "w_up_dw", **DW_MATMUL)
    tok = emit('ffn', g)
    g = {}
    du1, dg1, db1 = _ln_bwd([(DN_ALPHA, du2), (1.0, dhm_mm)], [(DN_ALPHA, s['h']), (1.0, s['zo'])],
                            _after(w['ln1_g'], tok), tag + "ln1_bwd")
    g['ln1_g'], g['ln1_b'] = dg1, db1
    dz = _matmul(du1, w['w_out'], tag + "w_out_dx", tb=True)
    g['w_out'] = _matmul(s['z'], du1, tag + "w_out_dw", **DW_MATMUL)
    dy_mla, dy_lru, dg_mla, dg_lru = _mix_bwd(dz, s['proj'], s['y_mla'], s['y_lru'], tag + "mix_bwd")
    do = _matmul(dy_mla, w['w_o_mla'], tag + "o_mla_dx", tb=True)
    g['w_o_mla'] = _matmul(s['o'], dy_mla, tag + "o_mla_dw", **DW_MATMUL)
    dqc, dkc, dv = _attn_bwd(s['qc'], s['kc'], s['vb'], do, s['o'], s['lse'], t_real, tag + "attn_bwd")
    dqext, dkv, dkrp = _mla_unpack(dqc, dkc, dv, cs, tag + "mla_unpack")
    dcqn = _matmul(dqext, w['w_q'], tag + "q_up_dx", tb=True)
    g['w_q'] = _matmul(s['cqn'], dqext, tag + "q_up_dw", **DW_MATMUL)
    dckvn = _matmul(dkv, w['w_kv'], tag + "kv_up_dx", tb=True)
    g['w_kv'] = _matmul(s['ckvn'], dkv, tag + "kv_up_dw", **DW_MATMUL)
    dcq, dckv, g['q_norm'], g['kv_norm'] = _mla_norms_bwd(dcqn, dckvn, s['proj'], w['q_norm'], w['kv_norm'], tag + "mla_norms_bwd")
    dgh = _matmul(dy_lru, w['w_o_lru'], tag + "o_lru_dx", tb=True)
    g['w_o_lru'] = _matmul(s['gh'], dy_lru, tag + "o_lru_dw", **DW_MATMUL)
    dlru_g, dhs = _gated_h_bwd(dgh, s['proj'], s['h0'], s['h1'], tag + "lru_gate_out_bwd")
    l0, l1, da0, da1 = _scan_bwd(dhs, s['a0'], s['a1'], s['h0'], s['h1'], tag + "lru_scan_bwd")
    dxc, g['w_g'], g['b4'], g['lru_lambda'] = _lru_gates_bwd(
        l0, l1, da0, da1, s['r0'], s['r1'], s['i0'], s['i1'], s['a0'], s['a1'], s['xc'], w['w_g'], w['lru_lambda'],
        t_real, tag + "lru_gates_bwd")
    dlru_x, g['lru_conv_w'], g['lru_conv_b'] = _lru_conv_bwd(dxc, s['proj'], w['lru_conv_w'], t_real, tag + "lru_conv_bwd")
    tok = emit('mid', g)
    dproj = jnp.concatenate([dlru_g, dlru_x, dg_mla, dg_lru, dcq, dckv, _after(dkrp, tok)], axis=1)
    tok = emit('in', {'w_in': _matmul(s['h'], dproj, tag + "proj_dw", **DW_MATMUL)})
    dh_mm = _matmul(dproj, w['w_in'], tag + "proj_dx", tb=True, tk_cap=1536)
    return [(DN_ALPHA, du1), (1.0, dh_mm)], tok


def _swap_halves(a, axis=-1):
    h1, h2 = jnp.split(a, 2, axis=axis)
    return jnp.concatenate([h2, h1], axis=axis)


def _w_in_kernel(w_in):
    cq, ckv, kr, lg, lx, gm, gl = jnp.split(w_in, [256, 384, 448, 1472, 2496, 3520], axis=1)
    return jnp.concatenate([lg, lx, gm, gl, cq, ckv, kr, _swap_halves(kr)], axis=1)


def _layer_weights(fl):
    w = {}
    if 'w_uq' in fl:
        uq = fl['w_uq']
        w['w_q'] = jnp.concatenate([uq, _swap_halves(uq[..., QK_NOPE:])], axis=-1).reshape(Q_RANK, HEADS * 2 * LANE)
        w['w_kv'] = jnp.concatenate([fl['w_uk'].reshape(KV_RANK, -1), fl['w_uv'].reshape(KV_RANK, -1)], axis=1).astype(BF16)
        w['w_g'] = jnp.moveaxis(jnp.concatenate([fl['w_rg'], fl['w_ig']], axis=0), 0, 1).astype(BF16)
        w['b4'] = jnp.concatenate([fl['b_rg'], fl['b_ig']], axis=0)
    for n in ('q_norm', 'kv_norm', 'w_o_mla', 'lru_conv_w', 'lru_conv_b', 'lru_lambda', 'w_o_lru', 'w_out', 'ln1_g',
              'ln1_b', 'w_up', 'ffn_conv_w', 'ffn_conv_b', 'w_down', 'ln2_g', 'ln2_b'):
        if n in fl:
            w[n] = fl[n]
    return w


def _layer_grads(g):
    out = {}
    if 'w_in' in g:
        lg, lx, gm, gl, cq, ckv, kr, krs = jnp.split(g['w_in'], [1024, 2048, 3072, 4096, 4352, 4480, 4544], axis=1)
        out['w_in'] = jnp.concatenate([cq, ckv, kr + _swap_halves(krs), lg, lx, gm, gl], axis=1)
    if 'w_q' in g:
        gq = g['w_q'].reshape(Q_RANK, HEADS, 2 * LANE)
        out['w_uq'] = jnp.concatenate([gq[..., :QK_NOPE], gq[..., QK_NOPE:QK_NOPE + QK_ROPE] + _swap_halves(gq[..., QK_NOPE + QK_ROPE:])], axis=-1)
    if 'w_kv' in g:
        out['w_uk'] = g['w_kv'][:, :HEADS * QK_NOPE].reshape(KV_RANK, HEADS, QK_NOPE)
        out['w_uv'] = g['w_kv'][:, HEADS * QK_NOPE:].reshape(KV_RANK, HEADS, V_HEAD)
    if 'w_g' in g:
        gg = jnp.moveaxis(g['w_g'], 1, 0)
        out['w_rg'], out['w_ig'] = gg[:2], gg[2:]
    if 'b4' in g:
        out['b_rg'], out['b_ig'] = g['b4'][:2], g['b4'][2:]
    for n in ('q_norm', 'kv_norm', 'lru_conv_b', 'ln1_g', 'ln1_b', 'ffn_conv_b', 'ln2_g', 'ln2_b'):
        if n in g:
            out[n] = g[n].reshape(-1)
    for n in ('w_o_mla', 'lru_conv_w', 'lru_lambda', 'w_o_lru', 'w_out', 'w_up', 'ffn_conv_w', 'w_down'):
        if n in g:
            out[n] = g[n]
    return out


def _rope_table(tp):
    half = QK_ROPE // 2
    inv_freq = jnp.exp(-math.log(ROPE_THETA) * jnp.arange(half, dtype=F32) / half)
    ang = jnp.arange(tp, dtype=F32)[:, None] * inv_freq[None, :]
    c, s = jnp.cos(ang), jnp.sin(ang)
    return jnp.concatenate([c, c, -s, s], axis=1)


def _local_step(x, target, meta, ln0_g, ln0_b, layer_w, t_pad, emit):
    seq = x.shape[0]
    t_real = N_META + seq
    zpad = jnp.zeros((t_pad - t_real, D_MODEL), F32)
    xin = jnp.concatenate([meta, x, zpad], axis=0)
    tgt = jnp.concatenate([jnp.zeros((N_META, D_MODEL), F32), target, zpad], axis=0)
    cs = _rope_table(t_pad)
    h = _ln_fwd([(1.0, xin)], ln0_g, ln0_b, "ln0")
    saved = []
    for l in range(DEPTH):
        w_in, rest_of_weights = layer_w[l](h)
        h, s = _layer_fwd(h, w_in, rest_of_weights, cs, t_real, "l%d_" % l)
        saved.append(s)
    dy, lossvec = _loss_head(h, tgt, t_real, "loss_head")
    terms, tok = [(1.0, dy)], None
    for l in reversed(range(DEPTH)):
        terms, tok = _layer_bwd(terms, saved[l], cs, t_real, "l%d_" % l,
                                functools.partial(lambda stage, g, l: emit(l, stage, _layer_grads(g)), l=l), tok)
    dxin, dg0, db0 = _ln_bwd(terms, [(1.0, xin)], _after(ln0_g, tok), "ln0_bwd")
    emit(None, 'head', {'meta_tokens': dxin[:N_META], 'ln0_g': dg0.reshape(-1), 'ln0_b': db0.reshape(-1), 'loss': lossvec})
    return dxin[N_META:t_real]


_HBM = pl.BlockSpec(memory_space=pltpu.HBM)
_SEM = pl.BlockSpec(memory_space=pltpu.SEMAPHORE)
_SIDE_EFFECT = pltpu.SideEffectType.DATAFLOW_SIDE_EFFECTING


def _peer_copies(src_refs, land_refs, scatters, send_sems, recv_sems):
    x, y, c = lax.axis_index("x"), lax.axis_index("y"), lax.axis_index("c")
    me = 4 * x + 2 * y + c
    copies = []
    for k in range(1, N_DEV):
        px = 1 - x if k & 4 else x
        py = 1 - y if k & 2 else y
        pc = 1 - c if k & 1 else c
        for t, (src, land) in enumerate(zip(src_refs, land_refs)):
            copies.append(pltpu.make_async_remote_copy(
                src_ref=src.at[4 * px + 2 * py + pc] if scatters[t] else src, dst_ref=land.at[me],
                send_sem=send_sems.at[7 * t + k - 1], recv_sem=recv_sems.at[7 * t + k - 1],
                device_id=(px, py, pc), device_id_type=pl.DeviceIdType.MESH))
    return me, copies


def _exchange_start(groups, name):
    flat = [it for grp in groups for it in grp]
    nt, ng = len(flat), len(groups)
    scatters = [sc for _, sc in flat]
    srcs = [pltpu.with_memory_space_constraint(a, pltpu.HBM) for a, _ in flat]
    land_shapes = [a.shape if sc else (N_DEV,) + a.shape for a, sc in flat]
    lands = [pltpu.with_memory_space_constraint(lax.empty(s, a.dtype), pltpu.HBM) for s, (a, _) in zip(land_shapes, flat)]
    bounds = [0]
    for grp in groups:
        bounds.append(bounds[-1] + len(grp))

    def body(*refs):
        src_refs, land_refs = refs[:nt], refs[nt:2 * nt]
        sem_refs = refs[2 * nt:2 * nt + 2 * ng]
        token_ref = refs[4 * nt + 2 * ng]
        for gi in range(ng):
            lo, hi = bounds[gi], bounds[gi + 1]
            _, copies = _peer_copies(src_refs[lo:hi], land_refs[lo:hi], scatters[lo:hi], sem_refs[2 * gi], sem_refs[2 * gi + 1])
            for cp in copies:
                cp.start()
        token_ref[...] = jnp.zeros_like(token_ref)

    out_shape = []
    for grp in groups:
        out_shape += [pltpu.SemaphoreType.DMA((7 * len(grp),)), pltpu.SemaphoreType.DMA((7 * len(grp),))]
    out_shape += [pltpu.HBM(a.shape, a.dtype) for a in srcs] + [pltpu.HBM(s, a.dtype) for s, a in zip(land_shapes, srcs)]
    out_shape += [jax.ShapeDtypeStruct((SUBLANE, LANE), F32)]
    res = pl.pallas_call(
        body, name=name, out_shape=out_shape,
        in_specs=[_HBM] * (2 * nt),
        out_specs=[_SEM] * (2 * ng) + [_HBM] * (2 * nt) + [pl.BlockSpec(memory_space=pltpu.VMEM)],
        input_output_aliases={t: 2 * ng + t for t in range(2 * nt)},
        compiler_params=pltpu.CompilerParams(has_side_effects=_SIDE_EFFECT),
    )(*srcs, *lands)
    sems, thru, token = res[:2 * ng], res[2 * ng:2 * ng + 2 * nt], res[-1]
    states = []
    for gi in range(ng):
        lo, hi = bounds[gi], bounds[gi + 1]
        states.append((sems[2 * gi], sems[2 * gi + 1], thru[lo:hi], thru[nt + lo:nt + hi], scatters[lo:hi]))
    return states, token[0, 0]


def _exchange_wait(state, after, name):
    send_sems, recv_sems, srcs, lands, scatters = state
    n = len(srcs)

    def body(*refs):
        _, copies = _peer_copies(refs[:n], refs[n:2 * n], scatters, refs[2 * n], refs[2 * n + 1])
        for cp in copies:
            cp.wait_send()
        for cp in copies:
            cp.wait_recv()

    res = pl.pallas_call(
        body, name=name,
        out_shape=[pltpu.HBM(a.shape, a.dtype) for a in srcs] + [pltpu.HBM(a.shape, a.dtype) for a in lands],
        in_specs=[_HBM] * (2 * n) + [_SEM, _SEM, _HBM],
        out_specs=[_HBM] * (2 * n),
        input_output_aliases={t: t for t in range(2 * n)},
        compiler_params=pltpu.CompilerParams(has_side_effects=_SIDE_EFFECT),
    )(*srcs, *lands, send_sems, recv_sems, pltpu.with_memory_space_constraint(after, pltpu.HBM))
    me = 4 * lax.axis_index("x") + 2 * lax.axis_index("y") + lax.axis_index("c")
    out = []
    for src, land, sc in zip(res[:n], res[n:], scatters):
        own = lax.dynamic_index_in_dim(src, me, 0, keepdims=True) if sc else src[None]
        out.append(lax.dynamic_update_slice_in_dim(land, own, me, 0))
    return out


def _as_rows(shape):
    return (1, shape[0]) if len(shape) == 1 else (math.prod(shape[:-1]), shape[-1])


def _sum_adamw(pieces, w, m, v, name):
    shape = w.shape
    nl = len(pieces)
    if nl > 1 and _as_rows(shape[1:])[0] % 16:
        pieces, nl = [jnp.stack(pieces, axis=1)], 1
    rows, cols = _as_rows(shape)
    rl = rows // nl
    cap = max(16, (1 << 18) // cols // 16 * 16)
    tr = _tile(rl, cap, 16)
    nb = rl // tr
    c1 = 1.0 / (1.0 - ADAM_B1 ** ADAM_STEP)
    c2 = 1.0 / (1.0 - ADAM_B2 ** ADAM_STEP)

    def body(*refs):
        p_refs = refs[:nl]
        w_ref, m_ref, v_ref, g_ref, d_ref, nm_ref, nv_ref = refs[nl:]
        li = pl.program_id(0)

        def total(p_ref):
            acc = p_ref[0].astype(F32)
            for k in range(1, N_DEV):
                acc = acc + p_ref[k].astype(F32)
            return acc

        gg = total(p_refs[0])
        for l in range(1, nl):
            gg = jnp.where(li == l, total(p_refs[l]), gg)
        nm = ADAM_B1 * m_ref[...] + (1.0 - ADAM_B1) * gg
        nv = ADAM_B2 * v_ref[...] + (1.0 - ADAM_B2) * (gg * gg)
        g_ref[...] = gg
        d_ref[...] = -ADAM_LR * ((nm * c1) / (jnp.sqrt(nv * c2) + ADAM_EPS) + ADAM_WD * w_ref[...])
        nm_ref[...] = nm
        nv_ref[...] = nv

    blk = pl.BlockSpec((tr, cols), lambda li, i: (li * nb + i, 0))
    p_specs = [pl.BlockSpec((N_DEV, tr, cols), functools.partial(lambda li, i, l: (0, jnp.where(li == l, i, 0), 0), l=l))
               for l in range(nl)]
    res = pl.pallas_call(
        body, name=name, grid=(nl, nb),
        in_specs=p_specs + [blk] * 3, out_specs=[blk] * 4,
        out_shape=[jax.ShapeDtypeStruct((rows, cols), F32)] * 4,
        compiler_params=_cparams("parallel", "parallel"),
    )(*[p.reshape(N_DEV, rl, cols) for p in pieces], *[a.reshape(rows, cols) for a in (w, m, v)])
    return [r.reshape(shape) for r in res]


def _to_shards(full, axis):
    shp = full.shape
    a = full.reshape(shp[:axis] + (N_DEV, shp[axis] // N_DEV) + shp[axis + 1:])
    return jnp.moveaxis(a, axis, 0)


def _from_shards(blocks, axis):
    a = jnp.moveaxis(blocks, 0, axis)
    shp = a.shape
    return a.reshape(shp[:axis] + (shp[axis] * shp[axis + 1],) + shp[axis + 2:])


def kernel(x, meta_tokens, ln0_g, ln0_b, w_in, q_norm, kv_norm, w_uq, w_uk, w_uv, w_o_mla, lru_conv_w, lru_conv_b, w_rg, b_rg, w_ig, b_ig, lru_lambda, w_o_lru, w_out, ln1_g, ln1_b, w_up, ffn_conv_w, ffn_conv_b, w_down, ln2_g, ln2_b, loss_target, m_meta_tokens, m_ln0_g, m_ln0_b, m_w_in, m_q_norm, m_kv_norm, m_w_uq, m_w_uk, m_w_uv, m_w_o_mla, m_lru_conv_w, m_lru_conv_b, m_w_rg, m_b_rg, m_w_ig, m_b_ig, m_lru_lambda, m_w_o_lru, m_w_out, m_ln1_g, m_ln1_b, m_w_up, m_ffn_conv_w, m_ffn_conv_b, m_w_down, m_ln2_g, m_ln2_b, v_meta_tokens, v_ln0_g, v_ln0_b, v_w_in, v_q_norm, v_kv_norm, v_w_uq, v_w_uk, v_w_uv, v_w_o_mla, v_lru_conv_w, v_lru_conv_b, v_w_rg, v_b_rg, v_w_ig, v_b_ig, v_lru_lambda, v_w_o_lru, v_w_out, v_ln1_g, v_ln1_b, v_w_up, v_ffn_conv_w, v_ffn_conv_b, v_w_down, v_ln2_g, v_ln2_b):
    args = (meta_tokens, ln0_g, ln0_b, w_in, q_norm, kv_norm, w_uq, w_uk, w_uv, w_o_mla, lru_conv_w, lru_conv_b, w_rg, b_rg, w_ig, b_ig, lru_lambda, w_o_lru, w_out, ln1_g, ln1_b, w_up, ffn_conv_w, ffn_conv_b, w_down, ln2_g, ln2_b)
    ms = (m_meta_tokens, m_ln0_g, m_ln0_b, m_w_in, m_q_norm, m_kv_norm, m_w_uq, m_w_uk, m_w_uv, m_w_o_mla, m_lru_conv_w, m_lru_conv_b, m_w_rg, m_b_rg, m_w_ig, m_b_ig, m_lru_lambda, m_w_o_lru, m_w_out, m_ln1_g, m_ln1_b, m_w_up, m_ffn_conv_w, m_ffn_conv_b, m_w_down, m_ln2_g, m_ln2_b)
    vs = (v_meta_tokens, v_ln0_g, v_ln0_b, v_w_in, v_q_norm, v_kv_norm, v_w_uq, v_w_uk, v_w_uv, v_w_o_mla, v_lru_conv_w, v_lru_conv_b, v_w_rg, v_b_rg, v_w_ig, v_b_ig, v_lru_lambda, v_w_o_lru, v_w_out, v_ln1_g, v_ln1_b, v_w_up, v_ffn_conv_w, v_ffn_conv_b, v_w_down, v_ln2_g, v_ln2_b)
    wd, md, vd = dict(zip(WEIGHTS, args)), dict(zip(WEIGHTS, ms)), dict(zip(WEIGHTS, vs))

    def shard_axis(n, l):
        return SHARD_AXIS[n] - (0 if l is None else 1)

    def shard(n, l):
        a = wd[n] if l is None else wd[n][l]
        return a.astype(BF16) if n in BIG else a

    first = [('meta_tokens', None), ('w_in', 0)]
    staged = [[(n, 0) for n in names] for names in STAGE_WEIGHTS]
    later = [(n, 1) for n in SHARDED if n != 'meta_tokens']
    gather, token = _exchange_start([[(shard(*k), False) for k in keys] for keys in [first] + staged + [later]], "gather_start")

    def arrive(gi, keys, after, name):
        return {k: _from_shards(b, shard_axis(*k)) for k, b in zip(keys, _exchange_wait(gather[gi], after, name))}

    def layer_weights(got, l, names):
        fl = {n: wd[n][l] for n in names if n in REPLICATED}
        fl.update({n: a for (n, _), a in got.items() if n in names})
        return _layer_weights(fl)

    ln0_g = _after(wd['ln0_g'], token)
    got_first = arrive(0, first, ln0_g, "gather_wait_first")

    def first_layer(h):
        def more(stage, after):
            got = arrive(1 + stage, staged[stage], after, "gather_wait_l0_%d" % stage)
            return layer_weights(got, 0, STAGE_WEIGHTS[stage] + STAGE_REPLICATED[stage])
        return _w_in_kernel(got_first['w_in', 0]), more

    def second_layer(h):
        got = arrive(1 + len(staged), later, h, "gather_wait_l1")
        return _w_in_kernel(got['w_in', 1]), lambda stage, after: layer_weights(got, 1, STAGE_WEIGHTS[stage] + STAGE_REPLICATED[stage])

    sent = []
    pending = []

    def send(l, stage, grads):
        for n, g in grads.items():
            if n in SHARD_AXIS:
                g = _to_shards(g, shard_axis(n, l))
                pending.append(((n, l), (g.astype(BF16) if n in BIG else g, True)))
            else:
                pending.append(((n, l), (g.astype(BF16) if n in LARGE_REPLICATED else g, False)))
        if l == DEPTH - 1 and stage != 'in':
            return None
        (state,), tok = _exchange_start([[it for _, it in pending]], "grads_start_%s_%s" % (l, stage))
        sent.append(([k for k, _ in pending], state))
        pending.clear()
        return tok

    seq = x.shape[1]
    t_pad = -(-(N_META + seq + MIN_PAD_ROWS) // LANE) * LANE
    grad_x = _local_step(x[0], loss_target[0], got_first['meta_tokens', None], ln0_g, wd['ln0_b'],
                         [first_layer, second_layer], t_pad, send)

    pieces = {}
    for gi, (keys, state) in enumerate(sent):
        pieces.update(zip(keys, _exchange_wait(state, grad_x, "grads_wait_%d" % gi)))
    loss = jnp.sum(pieces['loss', None])
    outs = {}
    for n in WEIGHTS:
        ps = [pieces[n, None]] if (n, None) in pieces else [pieces[n, l] for l in range(DEPTH)]
        outs[n] = _sum_adamw(ps, wd[n], md[n], vd[n], "adamw_" + n)
    res = [loss, grad_x[None]]
    for k in range(4):
        res += [outs[n][k] for n in WEIGHTS]
    return tuple(res)
```

```python
import functools
import math

import jax
import jax.numpy as jnp
from jax import lax
from jax.experimental import pallas as pl
from jax.experimental.pallas import tpu as pltpu

F32 = jnp.float32
BF16 = jnp.bfloat16

N_DEV = 8
D_MODEL = 1024
N_META = 16
HEADS = 8
QK_NOPE = 128
QK_ROPE = 64
V_HEAD = 128
Q_RANK = 256
KV_RANK = 128
ROPE_THETA = 10000.0
LRU_BLOCKS = 8
LRU_C = 8.0
D_FF = 2816
DEPTH = 2
DN_ALPHA = (2.0 * DEPTH) ** 0.25
LN_EPS = 1e-5
RMS_EPS = 1e-6
LN2 = math.log(2.0)
ATT_SCALE = 1.0 / math.sqrt(QK_NOPE + QK_ROPE) / LN2
NEG_BIG = -1e30

ADAM_LR = 0.001
ADAM_B1 = 0.9
ADAM_B2 = 0.999
ADAM_EPS = 1e-08
ADAM_WD = 0.01
ADAM_STEP = 10

MIN_PAD_ROWS = 2
LANE = 128
SUBLANE = 8
VMEM_LIMIT = 56 * 1024 * 1024

PROJ_COLS = 4 * D_MODEL + Q_RANK + KV_RANK + 2 * QK_ROPE
C_LRU_G, C_LRU_X, C_G_MLA, C_G_LRU = 0, D_MODEL, 2 * D_MODEL, 3 * D_MODEL
C_CQ = 4 * D_MODEL
C_CKV = C_CQ + Q_RANK
C_KRP = C_CKV + KV_RANK

WEIGHTS = ['meta_tokens', 'ln0_g', 'ln0_b', 'w_in', 'q_norm', 'kv_norm', 'w_uq', 'w_uk', 'w_uv', 'w_o_mla',
           'lru_conv_w', 'lru_conv_b', 'w_rg', 'b_rg', 'w_ig', 'b_ig', 'lru_lambda', 'w_o_lru', 'w_out',
           'ln1_g', 'ln1_b', 'w_up', 'ffn_conv_w', 'ffn_conv_b', 'w_down', 'ln2_g', 'ln2_b']
SHARD_AXIS = {'meta_tokens': 1, 'w_in': 2, 'w_uq': 1, 'w_o_mla': 1, 'lru_conv_w': 2, 'b_rg': 2, 'b_ig': 2,
              'lru_lambda': 2, 'w_o_lru': 1, 'w_out': 1, 'w_up': 2, 'ffn_conv_w': 2, 'w_down': 1}
BIG = ['w_in', 'w_uq', 'w_o_mla', 'w_o_lru', 'w_out', 'w_up', 'w_down']
SHARDED = [n for n in WEIGHTS if n in SHARD_AXIS]
REPLICATED = [n for n in WEIGHTS if n not in SHARD_AXIS]
LARGE_REPLICATED = ['w_uk', 'w_uv', 'w_rg', 'w_ig']
STAGE_WEIGHTS = [['w_uq', 'lru_conv_w', 'b_rg', 'b_ig', 'lru_lambda'], ['w_o_mla', 'w_o_lru', 'w_out'], ['w_up', 'ffn_conv_w', 'w_down']]
STAGE_REPLICATED = [['q_norm', 'kv_norm', 'w_uk', 'w_uv', 'lru_conv_b', 'w_rg', 'w_ig'], ['ln1_g', 'ln1_b'], ['ffn_conv_b', 'ln2_g', 'ln2_b']]


def _cparams(*sem):
    return pltpu.CompilerParams(dimension_semantics=sem, vmem_limit_bytes=VMEM_LIMIT)


def _tile(n, cap, unit=LANE):
    best = None
    t = unit
    while t <= min(n, cap):
        if n % t == 0:
            best = t
        t += unit
    return n if best is None else best


def _sigmoid(x):
    return 1.0 / (1.0 + jnp.exp(-x))


_GELU_C = math.sqrt(2.0 / math.pi)


_GELU_A = 0.044715


def _gelu(x):
    t = jnp.tanh(x * (_GELU_C + (_GELU_C * _GELU_A) * (x * x)))
    hx = 0.5 * x
    return hx + hx * t


def _gelu_and_grad(x):
    x2 = x * x
    t = jnp.tanh(x * (_GELU_C + (_GELU_C * _GELU_A) * x2))
    hx = 0.5 * x
    dg = 0.5 + 0.5 * t + (hx * (1.0 - t * t)) * (_GELU_C + (3.0 * _GELU_C * _GELU_A) * x2)
    return hx + hx * t, dg


def _softplus_neg(lam):
    z = jnp.exp(-jnp.abs(lam))
    w = 1.0 + z
    log1p = jnp.where(w == 1.0, z, jnp.log(w) * z / (w - 1.0))
    return jnp.maximum(-lam, 0.0) + log1p


def _row_ids(shape, row0=0):
    return lax.broadcasted_iota(jnp.int32, shape, 0) + row0


def _matmul(a, b, name, ta=False, tb=False, out_dtype=F32, tm_cap=1408, tn_cap=1024, tk_cap=2048):
    if ta:
        kdim, m = a.shape
    else:
        m, kdim = a.shape
    if tb:
        n, k2 = b.shape
    else:
        k2, n = b.shape
    assert kdim == k2, (a.shape, b.shape, ta, tb)
    tm, tn, tk = _tile(m, tm_cap), _tile(n, tn_cap), _tile(kdim, tk_cap)
    nk = kdim // tk

    def body(a_ref, b_ref, o_ref, *acc):
        dn = (((0 if ta else 1,), (1 if tb else 0,)), ((), ()))
        part = lax.dot_general(a_ref[...].astype(BF16), b_ref[...].astype(BF16), dn, preferred_element_type=F32)
        if nk == 1:
            o_ref[...] = part.astype(o_ref.dtype)
            return
        acc_ref, k = acc[0], pl.program_id(2)

        @pl.when(k == 0)
        def _():
            acc_ref[...] = part

        @pl.when(k > 0)
        def _():
            acc_ref[...] += part

        @pl.when(k == nk - 1)
        def _():
            o_ref[...] = acc_ref[...].astype(o_ref.dtype)

    a_spec = pl.BlockSpec((tk, tm), lambda i, j, k: (k, i)) if ta else pl.BlockSpec((tm, tk), lambda i, j, k: (i, k))
    b_spec = pl.BlockSpec((tn, tk), lambda i, j, k: (j, k)) if tb else pl.BlockSpec((tk, tn), lambda i, j, k: (k, j))
    return pl.pallas_call(
        body, name=name,
        grid=(m // tm, n // tn, nk),
        in_specs=[a_spec, b_spec],
        out_specs=pl.BlockSpec((tm, tn), lambda i, j, k: (i, j)),
        out_shape=jax.ShapeDtypeStruct((m, n), out_dtype),
        scratch_shapes=[pltpu.VMEM((tm, tn), F32)] if nk > 1 else [],
        compiler_params=_cparams("parallel", "parallel", "arbitrary"),
    )(a, b)


class Rw:
    def __init__(self, arr, width=None, cb=0):
        self.arr, self.width, self.cb = arr, (arr.shape[1] if width is None else width), cb


class Pm:
    def __init__(self, arr):
        self.arr = arr


def _rows(fn, name, ins, outs, accs=(), tm_cap=384):
    tp = next(o.arr.shape[0] for o in ins if isinstance(o, Rw))
    tm = _tile(tp, tm_cap)
    n_in, n_out, n_acc = len(ins), len(outs), len(accs)

    def body(*refs):
        i = pl.program_id(0)
        res = fn(i * tm, *[r[...] for r in refs[:n_in]])
        if not isinstance(res, (tuple, list)):
            res = (res,)
        assert len(res) == n_out + n_acc, (name, len(res))
        for k in range(n_out):
            refs[n_in + k][...] = res[k].astype(refs[n_in + k].dtype)
        for k in range(n_acc):
            ref = refs[n_in + n_out + k]

            @pl.when(i == 0)
            def _():
                ref[...] = jnp.zeros_like(ref)

            ref[...] += res[n_out + k]

    in_specs = []
    for o in ins:
        if isinstance(o, Rw):
            in_specs.append(pl.BlockSpec((tm, o.width), functools.partial(lambda i, cb: (i, cb), cb=o.cb)))
        else:
            in_specs.append(pl.BlockSpec(o.arr.shape, functools.partial(lambda i, nd: (0,) * nd, nd=o.arr.ndim)))
    out_specs = [pl.BlockSpec((tm, w), lambda i: (i, 0)) for (w, _) in outs]
    out_specs += [pl.BlockSpec(s, functools.partial(lambda i, nd: (0,) * nd, nd=len(s))) for s in accs]
    out_shape = [jax.ShapeDtypeStruct((tp, w), dt) for (w, dt) in outs]
    out_shape += [jax.ShapeDtypeStruct(s, F32) for s in accs]
    res = pl.pallas_call(
        body, name=name, grid=(tp // tm,), in_specs=in_specs, out_specs=out_specs, out_shape=out_shape,
        compiler_params=_cparams("arbitrary"),
    )(*[o.arr for o in ins])
    return res


class Cl:
    def __init__(self, arr, col0=0, width=None, stride=1):
        self.arr, self.col0, self.width, self.stride = arr, col0, width, stride


def _cols(fn, name, ins, outs, ncols, tc):
    assert ncols % tc == 0
    n_in, n_out = len(ins), len(outs)
    outs = [(o[0], o[1], o[2] if len(o) > 2 else tc) for o in outs]

    def body(*refs):
        res = fn(*[r[...] for r in refs[:n_in]])
        if not isinstance(res, (tuple, list)):
            res = (res,)
        assert len(res) == n_out, (name, len(res))
        for k in range(n_out):
            refs[n_in + k][...] = res[k].astype(refs[n_in + k].dtype)

    in_specs = []
    for o in ins:
        wd = tc if o.width is None else o.width
        assert o.col0 % wd == 0, (name, o.col0, wd)
        in_specs.append(pl.BlockSpec((o.arr.shape[0], wd),
                                     functools.partial(lambda j, off, st: (0, st * j + off), off=o.col0 // wd, st=o.stride)))
    out_specs = [pl.BlockSpec((r, wd), lambda j: (0, j)) for (r, _, wd) in outs]
    out_shape = [jax.ShapeDtypeStruct((r, ncols // tc * wd), dt) for (r, dt, wd) in outs]
    return pl.pallas_call(
        body, name=name, grid=(ncols // tc,), in_specs=in_specs, out_specs=out_specs, out_shape=out_shape,
        compiler_params=_cparams("parallel"),
    )(*[o.arr for o in ins])


def _ln_stats(u):
    mu = jnp.mean(u, axis=-1, keepdims=True)
    xc = u - mu
    var = jnp.mean(xc * xc, axis=-1, keepdims=True)
    rstd = lax.rsqrt(var + LN_EPS)
    return xc * rstd, rstd


def _ln_fwd(terms, g, b, name):
    coefs = [c for c, _ in terms]

    def fn(row0, *blk):
        xs, (gg, bb) = blk[:len(coefs)], blk[len(coefs):]
        u = sum(c * x for c, x in zip(coefs, xs))
        xhat, _ = _ln_stats(u)
        return xhat * gg + bb

    d = terms[0][1].shape[1]
    return _rows(fn, name, [Rw(x) for _, x in terms] + [Pm(g.reshape(1, d)), Pm(b.reshape(1, d))], [(d, F32)])[0]


def _ln_bwd(dy_terms, u_terms, g, name):
    dc = [c for c, _ in dy_terms]
    uc = [c for c, _ in u_terms]
    d = u_terms[0][1].shape[1]

    def fn(row0, *blk):
        dys = blk[:len(dc)]
        xs = blk[len(dc):len(dc) + len(uc)]
        gg = blk[-1]
        dy = sum(c * x for c, x in zip(dc, dys))
        u = sum(c * x for c, x in zip(uc, xs))
        xhat, rstd = _ln_stats(u)
        gdy = dy * gg
        m1 = jnp.mean(gdy, axis=-1, keepdims=True)
        m2 = jnp.mean(gdy * xhat, axis=-1, keepdims=True)
        du = rstd * (gdy - m1 - xhat * m2)
        return du, jnp.sum(dy * xhat, axis=0, keepdims=True), jnp.sum(dy, axis=0, keepdims=True)

    ins = [Rw(x) for _, x in dy_terms] + [Rw(x) for _, x in u_terms] + [Pm(g.reshape(1, d))]
    return _rows(fn, name, ins, [(d, F32)], accs=[(1, d), (1, d)])


def _loss_head(y, tgt, t_real, name):
    d = y.shape[1]

    def fn(row0, yb, tb):
        rows = _row_ids(yb.shape, row0)
        live = (rows >= N_META) & (rows < t_real)
        diff = jnp.where(live, yb - tb, 0.0)
        return diff * (1.0 / d), jnp.sum(diff * diff, axis=0, keepdims=True) * (0.5 / d)

    return _rows(fn, name, [Rw(y), Rw(tgt)], [(d, F32)], accs=[(1, d)])


def _rms(x, g):
    r = lax.rsqrt(jnp.mean(x * x, axis=-1, keepdims=True) + RMS_EPS)
    return x * r * g


def _rms_bwd(dy, x, g):
    r = lax.rsqrt(jnp.mean(x * x, axis=-1, keepdims=True) + RMS_EPS)
    gdy = dy * g
    dx = r * gdy - x * (r * r * r) * jnp.mean(gdy * x, axis=-1, keepdims=True)
    return dx, jnp.sum(dy * x * r, axis=0, keepdims=True)


def _mla_norms(proj, qn, kvn, name):
    def fn(row0, cq, ckv, g1, g2):
        return _rms(cq, g1), _rms(ckv, g2)

    return _rows(fn, name, [Rw(proj, Q_RANK, C_CQ // Q_RANK), Rw(proj, KV_RANK, C_CKV // KV_RANK),
                            Pm(qn.reshape(1, Q_RANK)), Pm(kvn.reshape(1, KV_RANK))],
                 [(Q_RANK, F32), (KV_RANK, F32)])


def _mla_norms_bwd(dcqn, dckvn, proj, qn, kvn, name):
    def fn(row0, d1, d2, cq, ckv, g1, g2):
        dx1, dg1 = _rms_bwd(d1, cq, g1)
        dx2, dg2 = _rms_bwd(d2, ckv, g2)
        return dx1, dx2, dg1, dg2

    return _rows(fn, name, [Rw(dcqn), Rw(dckvn), Rw(proj, Q_RANK, C_CQ // Q_RANK), Rw(proj, KV_RANK, C_CKV // KV_RANK),
                            Pm(qn.reshape(1, Q_RANK)), Pm(kvn.reshape(1, KV_RANK))],
                 [(Q_RANK, F32), (KV_RANK, F32)], accs=[(1, Q_RANK), (1, KV_RANK)])


def _fold_rope(z):
    return z + pltpu.roll(z, QK_ROPE, 1)


def _mla_pack(qext, kv, proj, cs, name):
    tp = qext.shape[0]
    tm = _tile(tp, 384)
    hw, nope_all = 2 * LANE, HEADS * QK_NOPE

    def body(q_ref, kv_ref, kr_ref, cs_ref, qo_ref, ko_ref, vo_ref):
        cs_ = cs_ref[...]
        low = lax.broadcasted_iota(jnp.int32, cs_.shape, 1) < QK_ROPE
        kr = _fold_rope(kr_ref[...] * cs_).astype(BF16)
        for h in range(HEADS):
            qr = jnp.where(low, _fold_rope(q_ref[:, h * hw + QK_NOPE:(h + 1) * hw] * cs_), 0.0)
            qo_ref[:, h * hw:h * hw + QK_NOPE] = (q_ref[:, h * hw:h * hw + QK_NOPE] * ATT_SCALE).astype(BF16)
            qo_ref[:, h * hw + QK_NOPE:(h + 1) * hw] = (qr * ATT_SCALE).astype(BF16)
            ko_ref[:, h * hw:h * hw + QK_NOPE] = kv_ref[:, h * QK_NOPE:(h + 1) * QK_NOPE].astype(BF16)
            ko_ref[:, h * hw + QK_NOPE:(h + 1) * hw] = kr
        vo_ref[...] = kv_ref[:, nope_all:].astype(BF16)

    row = lambda w: pl.BlockSpec((tm, w), lambda i: (i, 0))
    return pl.pallas_call(
        body, name=name, grid=(tp // tm,),
        in_specs=[row(HEADS * hw), row(2 * nope_all), pl.BlockSpec((tm, LANE), lambda i: (i, C_KRP // LANE)), row(LANE)],
        out_specs=[row(HEADS * hw), row(HEADS * hw), row(nope_all)],
        out_shape=[jax.ShapeDtypeStruct((tp, HEADS * hw), BF16),
                   jax.ShapeDtypeStruct((tp, HEADS * hw), BF16),
                   jax.ShapeDtypeStruct((tp, nope_all), BF16)],
        compiler_params=_cparams("parallel"),
    )(qext, kv, proj, cs)


def _mla_unpack(dq, dk, dv, cs, name):
    tp = dq.shape[0]
    tm = _tile(tp, 384)
    hw, nope_all = 2 * LANE, HEADS * QK_NOPE

    def body(dq_ref, dk_ref, dv_ref, cs_ref, dqe_ref, dkv_ref, dkr_ref):
        cs_ = cs_ref[...]
        low = lax.broadcasted_iota(jnp.int32, cs_.shape, 1) < QK_ROPE
        dkr = None
        for h in range(HEADS):
            dqe_ref[:, h * hw:h * hw + QK_NOPE] = dq_ref[:, h * hw:h * hw + QK_NOPE] * ATT_SCALE
            dqr = jnp.where(low, dq_ref[:, h * hw + QK_NOPE:(h + 1) * hw], 0.0) * ATT_SCALE
            dqe_ref[:, h * hw + QK_NOPE:(h + 1) * hw] = _fold_rope(dqr) * cs_
            dkv_ref[:, h * QK_NOPE:(h + 1) * QK_NOPE] = dk_ref[:, h * hw:h * hw + QK_NOPE]
            part = jnp.where(low, dk_ref[:, h * hw + QK_NOPE:(h + 1) * hw], 0.0)
            dkr = part if h == 0 else dkr + part
        dkv_ref[:, nope_all:] = dv_ref[...]
        dkr_ref[...] = _fold_rope(dkr) * cs_

    row = lambda w: pl.BlockSpec((tm, w), lambda i: (i, 0))
    return pl.pallas_call(
        body, name=name, grid=(tp // tm,),
        in_specs=[row(HEADS * hw), row(HEADS * hw), row(nope_all), row(LANE)],
        out_specs=[row(HEADS * hw), row(2 * nope_all), row(LANE)],
        out_shape=[jax.ShapeDtypeStruct((tp, HEADS * hw), F32),
                   jax.ShapeDtypeStruct((tp, 2 * nope_all), F32),
                   jax.ShapeDtypeStruct((tp, LANE), F32)],
        compiler_params=_cparams("parallel"),
    )(dq, dk, dv, cs)


def _attn_fwd(q, k, v, t_real, name):
    tp = q.shape[0]
    tq = _tile(tp, 1408)
    tkc = _tile(tp, 1408)
    nkc = -(-t_real // tkc)

    def body(q_ref, k_ref, v_ref, o_ref, lse_ref):
        qb = q_ref[...]
        m = l = acc = None
        for c in range(nkc):
            s = lax.dot_general(qb, k_ref[c * tkc:(c + 1) * tkc, :], (((1,), (1,)), ((), ())), preferred_element_type=F32)
            if (c + 1) * tkc > t_real:
                cols = lax.broadcasted_iota(jnp.int32, s.shape, 1) + c * tkc
                s = jnp.where(cols < t_real, s, NEG_BIG)
            mc = jnp.max(s, axis=-1, keepdims=True)
            m_new = mc if c == 0 else jnp.maximum(m, mc)
            p = jnp.exp2(s - m_new)
            lc = jnp.sum(p, axis=-1, keepdims=True)
            pv = jnp.dot(p.astype(BF16), v_ref[c * tkc:(c + 1) * tkc, :], preferred_element_type=F32)
            if c == 0:
                l, acc = lc, pv
            else:
                alpha = jnp.exp2(m - m_new)
                l, acc = alpha * l + lc, alpha * acc + pv
            m = m_new
        o_ref[...] = acc / l
        lse_ref[...] = m + jnp.log2(l)

    return pl.pallas_call(
        body, name=name, grid=(HEADS, tp // tq),
        in_specs=[pl.BlockSpec((tq, 2 * LANE), lambda h, i: (i, h)),
                  pl.BlockSpec((tp, 2 * LANE), lambda h, i: (0, h)),
                  pl.BlockSpec((tp, LANE), lambda h, i: (0, h))],
        out_specs=[pl.BlockSpec((tq, LANE), lambda h, i: (i, h)),
                   pl.BlockSpec((None, tq, 1), lambda h, i: (h, i, 0))],
        out_shape=[jax.ShapeDtypeStruct((tp, HEADS * LANE), F32),
                   jax.ShapeDtypeStruct((HEADS, tp, 1), F32)],
        compiler_params=_cparams("parallel", "parallel"),
    )(q, k, v)


def _attn_bwd(q, k, v, do, o, lse, t_real, name):
    tp = q.shape[0]
    tq = _tile(tp, 704, 64)
    tkc = _tile(tp, 1408)
    nkc = -(-t_real // tkc)

    def body(q_ref, k_ref, v_ref, do_ref, o_ref, lse_ref, dq_ref, dk_ref, dv_ref):
        i = pl.program_id(1)

        @pl.when(i == 0)
        def _():
            dk_ref[...] = jnp.zeros_like(dk_ref)
            dv_ref[...] = jnp.zeros_like(dv_ref)

        qb = q_ref[...]
        dob = do_ref[...]
        dob16 = dob.astype(BF16)
        dol2 = (dob * LN2).astype(BF16)
        delta = jnp.sum(dob * o_ref[...], axis=-1, keepdims=True) * LN2
        lse = lse_ref[...]
        dq = None
        for c in range(nkc):
            ks = slice(c * tkc, (c + 1) * tkc)
            kb = k_ref[ks, :]
            s = lax.dot_general(qb, kb, (((1,), (1,)), ((), ())), preferred_element_type=F32)
            p = jnp.exp2(s - lse)
            if (c + 1) * tkc > t_real:
                cols = lax.broadcasted_iota(jnp.int32, s.shape, 1) + c * tkc
                p = jnp.where(cols < t_real, p, 0.0)
            dp = lax.dot_general(dol2, v_ref[ks, :], (((1,), (1,)), ((), ())), preferred_element_type=F32)
            ds = (p * (dp - delta)).astype(BF16)
            dqc = jnp.dot(ds, kb, preferred_element_type=F32)
            dq = dqc if c == 0 else dq + dqc
            dk_ref[ks, :] += lax.dot_general(ds, qb, (((0,), (0,)), ((), ())), preferred_element_type=F32)
            dv_ref[ks, :] += lax.dot_general(p.astype(BF16), dob16, (((0,), (0,)), ((), ())), preferred_element_type=F32)
        dq_ref[...] = dq

    return pl.pallas_call(
        body, name=name, grid=(HEADS, tp // tq),
        in_specs=[pl.BlockSpec((tq, 2 * LANE), lambda h, i: (i, h)),
                  pl.BlockSpec((tp, 2 * LANE), lambda h, i: (0, h)),
                  pl.BlockSpec((tp, LANE), lambda h, i: (0, h)),
                  pl.BlockSpec((tq, LANE), lambda h, i: (i, h)),
                  pl.BlockSpec((tq, LANE), lambda h, i: (i, h)),
                  pl.BlockSpec((None, tq, 1), lambda h, i: (h, i, 0))],
        out_specs=[pl.BlockSpec((tq, 2 * LANE), lambda h, i: (i, h)),
                   pl.BlockSpec((tp, 2 * LANE), lambda h, i: (0, h)),
                   pl.BlockSpec((tp, LANE), lambda h, i: (0, h))],
        out_shape=[jax.ShapeDtypeStruct((tp, HEADS * 2 * LANE), F32),
                   jax.ShapeDtypeStruct((tp, HEADS * 2 * LANE), F32),
                   jax.ShapeDtypeStruct((tp, HEADS * LANE), F32)],
        compiler_params=_cparams("parallel", "arbitrary"),
    )(q, k, v, do, o, lse)


def _shift_rows(x, s):
    tp = x.shape[0]
    return x if s % tp == 0 else pltpu.roll(x, s % tp, 0)


def _conv_fwd_val(xm, w, b, pad_left):
    acc = b + w[0:1, :] * _shift_rows(xm, pad_left)
    for k in range(1, w.shape[0]):
        acc = acc + w[k:k + 1, :] * _shift_rows(xm, pad_left - k)
    return acc


def _conv_bwd_val(dy, xm, w, pad_left, live):
    kk = w.shape[0]
    dx = w[0:1, :] * _shift_rows(dy, -pad_left)
    dws = [jnp.sum(dy * _shift_rows(xm, pad_left), axis=0, keepdims=True)]
    for k in range(1, kk):
        dx = dx + w[k:k + 1, :] * _shift_rows(dy, k - pad_left)
        dws.append(jnp.sum(dy * _shift_rows(xm, pad_left - k), axis=0, keepdims=True))
    return jnp.where(live, dx, 0.0), jnp.concatenate(dws, axis=0), jnp.sum(dy, axis=0, keepdims=True)


def _lru_conv_fwd(proj, w, b, t_real, name):
    def fn(x, ww, bb):
        xm = jnp.where(_row_ids(x.shape) < t_real, x, 0.0)
        return _conv_fwd_val(xm, ww, bb, 2)

    return _cols(fn, name, [Cl(proj, C_LRU_X), Cl(w), Cl(b.reshape(1, -1))], [(proj.shape[0], F32)], D_MODEL, 128)[0]


def _lru_conv_bwd(dxc, proj, w, t_real, name):
    def fn(dy, x, ww):
        live = _row_ids(x.shape) < t_real
        xm = jnp.where(live, x, 0.0)
        dym = jnp.where(live, dy, 0.0)
        return _conv_bwd_val(dym, xm, ww, 2, live)

    return _cols(fn, name, [Cl(dxc), Cl(proj, C_LRU_X), Cl(w)],
                 [(proj.shape[0], F32), (w.shape[0], F32), (1, F32)], D_MODEL, 128)


def _ffn_conv_act(up, w, b, t_real, name):
    def fn(g, v, wg, wv, bg, bv):
        live = _row_ids(g.shape) < t_real
        gc = _conv_fwd_val(jnp.where(live, g, 0.0), wg, bg, 1)
        vc = _conv_fwd_val(jnp.where(live, v, 0.0), wv, bv, 1)
        return _gelu(gc) * vc

    b2 = b.reshape(1, -1)
    return _cols(fn, name, [Cl(up), Cl(up, D_FF), Cl(w), Cl(w, D_FF), Cl(b2), Cl(b2, D_FF)],
                 [(up.shape[0], F32)], D_FF, 128)[0]


def _ffn_conv_act_bwd(dm, up, w, b, t_real, name):
    def fn(dmb, g, v, wg, wv, bg, bv):
        live = _row_ids(g.shape) < t_real
        gm, vm = jnp.where(live, g, 0.0), jnp.where(live, v, 0.0)
        gc = _conv_fwd_val(gm, wg, bg, 1)
        vc = _conv_fwd_val(vm, wv, bv, 1)
        act, dact = _gelu_and_grad(gc)
        dmm = jnp.where(live, dmb, 0.0)
        dgx, dwg, dbg = _conv_bwd_val(dmm * vc * dact, gm, wg, 1, live)
        dvx, dwv, dbv = _conv_bwd_val(dmm * act, vm, wv, 1, live)
        return dgx, dvx, dwg, dwv, dbg, dbv

    b2 = b.reshape(1, -1)
    tp, kk = up.shape[0], w.shape[0]
    return _cols(fn, name, [Cl(dm), Cl(up), Cl(up, D_FF), Cl(w), Cl(w, D_FF), Cl(b2), Cl(b2, D_FF)],
                 [(tp, F32), (tp, F32), (kk, F32), (kk, F32), (1, F32), (1, F32)], D_FF, 128)


def _lru_gates_fwd(xc, wg, b4, lam, t_real, name):
    tp = xc.shape[0]
    tm = _tile(tp, 1408)

    def body(x_ref, w_ref, b_ref, lam_ref, r0_ref, r1_ref, i0_ref, i1_ref, a0_ref, a1_ref, u0_ref, u1_ref):
        x = x_ref[...]
        xb = x.astype(BF16)
        live = _row_ids(x.shape, pl.program_id(1) * tm) < t_real
        bb = b_ref[...]
        sp = _softplus_neg(lam_ref[...])
        gate = [_sigmoid(jnp.dot(xb, w_ref[k], preferred_element_type=F32) + bb[k:k + 1, :]) for k in range(4)]
        for d, (r_ref, i_ref, a_ref, u_ref) in enumerate(((r0_ref, i0_ref, a0_ref, u0_ref), (r1_ref, i1_ref, a1_ref, u1_ref))):
            r, ig = gate[d], gate[2 + d]
            a = jnp.exp(-LRU_C * r * sp[d:d + 1, :])
            r_ref[...] = r
            i_ref[...] = ig
            a_ref[...] = a
            u_ref[...] = jnp.where(live, jnp.sqrt(1.0 - a * a) * (ig * x), 0.0)

    blk = pl.BlockSpec((tm, LANE), lambda g, i: (i, g))
    return pl.pallas_call(
        body, name=name, grid=(LRU_BLOCKS, tp // tm),
        in_specs=[blk, pl.BlockSpec((None, 4, LANE, LANE), lambda g, i: (g, 0, 0, 0)),
                  pl.BlockSpec((4, LANE), lambda g, i: (0, g)), pl.BlockSpec((2, LANE), lambda g, i: (0, g))],
        out_specs=[blk] * 8,
        out_shape=[jax.ShapeDtypeStruct((tp, D_MODEL), F32)] * 8,
        compiler_params=_cparams("parallel", "parallel"),
    )(xc, wg, b4, lam)


def _lru_gates_bwd(l0, l1, da0, da1, r0, r1, i0, i1, a0, a1, xc, wg, lam, t_real, name):
    tp = xc.shape[0]
    tm = _tile(tp, 1408)

    def body(l0_ref, l1_ref, da0_ref, da1_ref, r0_ref, r1_ref, i0_ref, i1_ref, a0_ref, a1_ref, x_ref, w_ref, lam_ref,
             dx_ref, dw_ref, db_ref, dlam_ref):
        i = pl.program_id(1)
        x = x_ref[...]
        xb = x.astype(BF16)
        live = _row_ids(x.shape, i * tm) < t_real
        lam_ = lam_ref[...]
        sp = _softplus_neg(lam_)
        dsp_dlam = -_sigmoid(-lam_)
        dx = jnp.zeros_like(x)
        dpre = [None] * 4
        dlam_rows = []
        for d, (l_ref, da_ref, r_ref, i_ref, a_ref) in enumerate(((l0_ref, da0_ref, r0_ref, i0_ref, a0_ref),
                                                                  (l1_ref, da1_ref, r1_ref, i1_ref, a1_ref))):
            r, ig, a = r_ref[...], i_ref[...], a_ref[...]
            du = jnp.where(live, l_ref[...], 0.0)
            s = jnp.sqrt(1.0 - a * a)
            dv = du * s
            ds = du * (ig * x)
            dla = jnp.where(live, da_ref[...], 0.0) * a - ds * (a * a) / s
            dla = jnp.where(live, dla, 0.0)
            dr = dla * (-LRU_C) * sp[d:d + 1, :]
            dlam_rows.append(jnp.sum(dla * (-LRU_C) * r, axis=0, keepdims=True) * dsp_dlam[d:d + 1, :])
            dpre[d] = dr * r * (1.0 - r)
            dpre[2 + d] = dv * x * ig * (1.0 - ig)
            dx = dx + dv * ig

        @pl.when(i == 0)
        def _():
            dw_ref[...] = jnp.zeros_like(dw_ref)
            db_ref[...] = jnp.zeros_like(db_ref)
            dlam_ref[...] = jnp.zeros_like(dlam_ref)

        for k in range(4):
            pk = dpre[k].astype(BF16)
            dx = dx + lax.dot_general(pk, w_ref[k], (((1,), (1,)), ((), ())), preferred_element_type=F32)
            dw_ref[k] += lax.dot_general(xb, pk, (((0,), (0,)), ((), ())), preferred_element_type=F32)
        db_ref[...] += jnp.concatenate([jnp.sum(p, axis=0, keepdims=True) for p in dpre], axis=0)
        dlam_ref[...] += jnp.concatenate(dlam_rows, axis=0)
        dx_ref[...] = dx

    blk = pl.BlockSpec((tm, LANE), lambda g, i: (i, g))
    return pl.pallas_call(
        body, name=name, grid=(LRU_BLOCKS, tp // tm),
        in_specs=[blk] * 11 + [pl.BlockSpec((None, 4, LANE, LANE), lambda g, i: (g, 0, 0, 0)),
                               pl.BlockSpec((2, LANE), lambda g, i: (0, g))],
        out_specs=[blk, pl.BlockSpec((None, 4, LANE, LANE), lambda g, i: (g, 0, 0, 0)),
                   pl.BlockSpec((4, LANE), lambda g, i: (0, g)), pl.BlockSpec((2, LANE), lambda g, i: (0, g))],
        out_shape=[jax.ShapeDtypeStruct((tp, D_MODEL), F32), jax.ShapeDtypeStruct((LRU_BLOCKS, 4, LANE, LANE), F32),
                   jax.ShapeDtypeStruct((4, D_MODEL), F32), jax.ShapeDtypeStruct((2, D_MODEL), F32)],
        compiler_params=_cparams("parallel", "arbitrary"),
    )(l0, l1, da0, da1, r0, r1, i0, i1, a0, a1, xc, wg, lam)


SCAN_UNROLL = 4


def _loop_tiles(nt, step, carry):
    assert nt % SCAN_UNROLL == 0

    def trip(tt, c):
        for u in range(SCAN_UNROLL):
            c = step(tt * SCAN_UNROLL + u, c)
        return c

    return lax.fori_loop(0, nt // SCAN_UNROLL, trip, carry)


def _tile_scan(a, u, reverse):
    rows = lax.broadcasted_iota(jnp.int32, a.shape, 0)
    for s in (1, 2, 4):
        if reverse:
            keep = rows < SUBLANE - s
            a_sh, u_sh = pltpu.roll(a, SUBLANE - s, 0), pltpu.roll(u, SUBLANE - s, 0)
        else:
            keep = rows >= s
            a_sh, u_sh = pltpu.roll(a, s, 0), pltpu.roll(u, s, 0)
        u = u + a * jnp.where(keep, u_sh, 0.0)
        a = a * jnp.where(keep, a_sh, 1.0)
    return a, u


def _scan_fwd(a0, u0, a1, u1, name):
    tp, d = a0.shape
    tc = 128
    nt = tp // SUBLANE

    def body(a0_ref, u0_ref, a1_ref, u1_ref, h0_ref, h1_ref):
        def step(t, carry):
            c0, c1 = carry
            f = pl.multiple_of(t * SUBLANE, SUBLANE)
            b = pl.multiple_of((nt - 1 - t) * SUBLANE, SUBLANE)
            pa, pu = _tile_scan(a0_ref[pl.ds(f, SUBLANE), :], u0_ref[pl.ds(f, SUBLANE), :], False)
            h = pu + pa * c0
            h0_ref[pl.ds(f, SUBLANE), :] = h
            c0 = h[SUBLANE - 1:SUBLANE, :]
            pa, pu = _tile_scan(a1_ref[pl.ds(b, SUBLANE), :], u1_ref[pl.ds(b, SUBLANE), :], True)
            h = pu + pa * c1
            h1_ref[pl.ds(b, SUBLANE), :] = h
            c1 = h[0:1, :]
            return c0, c1

        z = jnp.zeros((1, tc), F32)
        _loop_tiles(nt, step, (z, z))

    blk = pl.BlockSpec((tp, tc), lambda j: (0, j))
    return pl.pallas_call(
        body, name=name, grid=(d // tc,), in_specs=[blk] * 4, out_specs=[blk] * 2,
        out_shape=[jax.ShapeDtypeStruct((tp, d), F32)] * 2,
        compiler_params=_cparams("parallel"),
    )(a0, u0, a1, u1)


def _scan_bwd(dh, a0, a1, h0, h1, name):
    tp, d = dh.shape
    tc = 128
    nt = tp // SUBLANE

    def body(dh_ref, a0_ref, a1_ref, h0_ref, h1_ref, l0_ref, l1_ref, da0_ref, da1_ref):
        rows8 = lax.broadcasted_iota(jnp.int32, (SUBLANE, tc), 0)

        def step(t, carry):
            c0, c1 = carry
            b = pl.multiple_of((nt - 1 - t) * SUBLANE, SUBLANE)
            f = pl.multiple_of(t * SUBLANE, SUBLANE)
            a = a0_ref[pl.ds(b, SUBLANE), :]
            a_next = jnp.where(rows8 < SUBLANE - 1, pltpu.roll(a, SUBLANE - 1, 0), 1.0)
            pa, pu = _tile_scan(a_next, dh_ref[pl.ds(b, SUBLANE), :], True)
            lam = pu + pa * c0
            l0_ref[pl.ds(b, SUBLANE), :] = lam
            c0 = a[0:1, :] * lam[0:1, :]
            a = a1_ref[pl.ds(f, SUBLANE), :]
            a_prev = jnp.where(rows8 >= 1, pltpu.roll(a, 1, 0), 1.0)
            pa, pu = _tile_scan(a_prev, dh_ref[pl.ds(f, SUBLANE), :], False)
            lam = pu + pa * c1
            l1_ref[pl.ds(f, SUBLANE), :] = lam
            c1 = a[SUBLANE - 1:SUBLANE, :] * lam[SUBLANE - 1:SUBLANE, :]
            return c0, c1

        z = jnp.zeros((1, tc), F32)
        _loop_tiles(nt, step, (z, z))
        rows = lax.broadcasted_iota(jnp.int32, (tp, tc), 0)
        da0_ref[...] = l0_ref[...] * jnp.where(rows >= 1, pltpu.roll(h0_ref[...], 1, 0), 0.0)
        da1_ref[...] = l1_ref[...] * jnp.where(rows < tp - 1, pltpu.roll(h1_ref[...], tp - 1, 0), 0.0)

    blk = pl.BlockSpec((tp, tc), lambda j: (0, j))
    return pl.pallas_call(
        body, name=name, grid=(d // tc,), in_specs=[blk] * 5, out_specs=[blk] * 4,
        out_shape=[jax.ShapeDtypeStruct((tp, d), F32)] * 4,
        compiler_params=_cparams("parallel"),
    )(dh, a0, a1, h0, h1)


def _gated_h(proj, h0, h1, name):
    def fn(row0, lg, x0, x1):
        return _gelu(lg) * (x0 + x1)

    return _rows(fn, name, [Rw(proj, D_MODEL, C_LRU_G // D_MODEL), Rw(h0), Rw(h1)], [(D_MODEL, F32)])[0]


def _gated_h_bwd(dgh, proj, h0, h1, name):
    def fn(row0, dg, lg, x0, x1):
        act, dact = _gelu_and_grad(lg)
        return dg * (x0 + x1) * dact, dg * act

    return _rows(fn, name, [Rw(dgh), Rw(proj, D_MODEL, C_LRU_G // D_MODEL), Rw(h0), Rw(h1)], [(D_MODEL, F32)] * 2)


def _mix(proj, y_mla, y_lru, name):
    def fn(row0, gm, gl, ym, yl):
        return _sigmoid(gm) * ym + _sigmoid(gl) * yl

    return _rows(fn, name, [Rw(proj, D_MODEL, C_G_MLA // D_MODEL), Rw(proj, D_MODEL, C_G_LRU // D_MODEL), Rw(y_mla), Rw(y_lru)],
                 [(D_MODEL, F32)])[0]


def _mix_bwd(dz, proj, y_mla, y_lru, name):
    def fn(row0, dzb, gm, gl, ym, yl):
        sm, sl = _sigmoid(gm), _sigmoid(gl)
        return dzb * sm, dzb * sl, dzb * ym * sm * (1.0 - sm), dzb * yl * sl * (1.0 - sl)

    return _rows(fn, name, [Rw(dz), Rw(proj, D_MODEL, C_G_MLA // D_MODEL), Rw(proj, D_MODEL, C_G_LRU // D_MODEL),
                            Rw(y_mla), Rw(y_lru)], [(D_MODEL, F32)] * 4)


def _layer_fwd(h, w_in, more_weights, cs, t_real, tag):
    proj = _matmul(h, w_in, tag + "proj")
    w = dict(more_weights(0, proj), w_in=w_in)
    cqn, ckvn = _mla_norms(proj, w['q_norm'], w['kv_norm'], tag + "mla_norms")
    qext = _matmul(cqn, w['w_q'], tag + "q_up")
    kv = _matmul(ckvn, w['w_kv'], tag + "kv_up")
    qc, kc, vb = _mla_pack(qext, kv, proj, cs, tag + "mla_pack")
    o, lse = _attn_fwd(qc, kc, vb, t_real, tag + "attn_fwd")
    w.update(more_weights(1, o))
    y_mla = _matmul(o, w['w_o_mla'], tag + "o_mla")
    xc = _lru_conv_fwd(proj, w['lru_conv_w'], w['lru_conv_b'], t_real, tag + "lru_conv")
    r0, r1, i0, i1, a0, a1, u0, u1 = _lru_gates_fwd(xc, w['w_g'], w['b4'], w['lru_lambda'], t_real, tag + "lru_gates")
    h0, h1 = _scan_fwd(a0, u0, a1, u1, tag + "lru_scan")
    gh = _gated_h(proj, h0, h1, tag + "lru_gate_out")
    y_lru = _matmul(gh, w['w_o_lru'], tag + "o_lru")
    z = _mix(proj, y_mla, y_lru, tag + "mix")
    zo = _matmul(z, w['w_out'], tag + "w_out")
    hm = _ln_fwd([(DN_ALPHA, h), (1.0, zo)], w['ln1_g'], w['ln1_b'], tag + "ln1")
    w.update(more_weights(2, hm))
    up = _matmul(hm, w['w_up'], tag + "w_up")
    m = _ffn_conv_act(up, w['ffn_conv_w'], w['ffn_conv_b'], t_real, tag + "ffn_conv")
    f = _matmul(m, w['w_down'], tag + "w_down", tk_cap=1408)
    out = _ln_fwd([(DN_ALPHA, hm), (1.0, f)], w['ln2_g'], w['ln2_b'], tag + "ln2")
    saved = dict(w=w, h=h, proj=proj, cqn=cqn, ckvn=ckvn, qc=qc, kc=kc, vb=vb, o=o, lse=lse, y_mla=y_mla, xc=xc,
                 r0=r0, r1=r1, i0=i0, i1=i1, a0=a0, a1=a1, h0=h0, h1=h1, gh=gh, y_lru=y_lru, z=z, zo=zo, hm=hm,
                 up=up, m=m, f=f)
    return out, saved


DW_MATMUL = dict(ta=True, out_dtype=BF16, tn_cap=1408, tk_cap=1408)


def _after(a, tok):
    return a if tok is None else a + tok.astype(a.dtype)


def _layer_bwd(dout_terms, s, cs, t_real, tag, emit, tok):
    w = s['w']
    g = {}
    du2, dg2, db2 = _ln_bwd(dout_terms, [(DN_ALPHA, s['hm']), (1.0, s['f'])], _after(w['ln2_g'], tok), tag + "ln2_bwd")
    g['ln2_g'], g['ln2_b'] = dg2, db2
    dm = _matmul(du2, w['w_down'], tag + "w_down_dx", tb=True)
    g['w_down'] = _matmul(s['m'], du2, tag + "w_down_dw", **DW_MATMUL)
    dgp, dvp, dwg_, dwv_, dbg_, dbv_ = _ffn_conv_act_bwd(dm, s['up'], w['ffn_conv_w'], w['ffn_conv_b'], t_real, tag + "ffn_conv_bwd")
    g['ffn_conv_w'] = jnp.concatenate([dwg_, dwv_], axis=1)
    g['ffn_conv_b'] = jnp.concatenate([dbg_, dbv_], axis=1)
    dup = jnp.concatenate([dgp, dvp], axis=1)
    dhm_mm = _matmul(dup, w['w_up'], tag + "w_up_dx", tb=True, tk_cap=1408)
    g['w_up'] = _matmul(s['hm'], dup, tag + "w_up_dw", **DW_MATMUL)
    tok = emit('ffn', g)
    g = {}
    du1, dg1, db1 = _ln_bwd([(DN_ALPHA, du2), (1.0, dhm_mm)], [(DN_ALPHA, s['h']), (1.0, s['zo'])],
                            _after(w['ln1_g'], tok), tag + "ln1_bwd")
    g['ln1_g'], g['ln1_b'] = dg1, db1
    dz = _matmul(du1, w['w_out'], tag + "w_out_dx", tb=True)
    g['w_out'] = _matmul(s['z'], du1, tag + "w_out_dw", **DW_MATMUL)
    dy_mla, dy_lru, dg_mla, dg_lru = _mix_bwd(dz, s['proj'], s['y_mla'], s['y_lru'], tag + "mix_bwd")
    do = _matmul(dy_mla, w['w_o_mla'], tag + "o_mla_dx", tb=True)
    g['w_o_mla'] = _matmul(s['o'], dy_mla, tag + "o_mla_dw", **DW_MATMUL)
    dqc, dkc, dv = _attn_bwd(s['qc'], s['kc'], s['vb'], do, s['o'], s['lse'], t_real, tag + "attn_bwd")
    dqext, dkv, dkrp = _mla_unpack(dqc, dkc, dv, cs, tag + "mla_unpack")
    dcqn = _matmul(dqext, w['w_q'], tag + "q_up_dx", tb=True)
    g['w_q'] = _matmul(s['cqn'], dqext, tag + "q_up_dw", **DW_MATMUL)
    dckvn = _matmul(dkv, w['w_kv'], tag + "kv_up_dx", tb=True)
    g['w_kv'] = _matmul(s['ckvn'], dkv, tag + "kv_up_dw", **DW_MATMUL)
    dcq, dckv, g['q_norm'], g['kv_norm'] = _mla_norms_bwd(dcqn, dckvn, s['proj'], w['q_norm'], w['kv_norm'], tag + "mla_norms_bwd")
    dgh = _matmul(dy_lru, w['w_o_lru'], tag + "o_lru_dx", tb=True)
    g['w_o_lru'] = _matmul(s['gh'], dy_lru, tag + "o_lru_dw", **DW_MATMUL)
    dlru_g, dhs = _gated_h_bwd(dgh, s['proj'], s['h0'], s['h1'], tag + "lru_gate_out_bwd")
    l0, l1, da0, da1 = _scan_bwd(dhs, s['a0'], s['a1'], s['h0'], s['h1'], tag + "lru_scan_bwd")
    dxc, g['w_g'], g['b4'], g['lru_lambda'] = _lru_gates_bwd(
        l0, l1, da0, da1, s['r0'], s['r1'], s['i0'], s['i1'], s['a0'], s['a1'], s['xc'], w['w_g'], w['lru_lambda'],
        t_real, tag + "lru_gates_bwd")
    dlru_x, g['lru_conv_w'], g['lru_conv_b'] = _lru_conv_bwd(dxc, s['proj'], w['lru_conv_w'], t_real, tag + "lru_conv_bwd")
    tok = emit('mid', g)
    dproj = jnp.concatenate([dlru_g, dlru_x, dg_mla, dg_lru, dcq, dckv, _after(dkrp, tok)], axis=1)
    tok = emit('in', {'w_in': _matmul(s['h'], dproj, tag + "proj_dw", **DW_MATMUL)})
    dh_mm = _matmul(dproj, w['w_in'], tag + "proj_dx", tb=True, tk_cap=1536)
    return [(DN_ALPHA, du1), (1.0, dh_mm)], tok


def _swap_halves(a, axis=-1):
    h1, h2 = jnp.split(a, 2, axis=axis)
    return jnp.concatenate([h2, h1], axis=axis)


def _w_in_kernel(w_in):
    cq, ckv, kr, lg, lx, gm, gl = jnp.split(w_in, [256, 384, 448, 1472, 2496, 3520], axis=1)
    return jnp.concatenate([lg, lx, gm, gl, cq, ckv, kr, _swap_halves(kr)], axis=1)


def _layer_weights(fl):
    w = {}
    if 'w_uq' in fl:
        uq = fl['w_uq']
        w['w_q'] = jnp.concatenate([uq, _swap_halves(uq[..., QK_NOPE:])], axis=-1).reshape(Q_RANK, HEADS * 2 * LANE)
        w['w_kv'] = jnp.concatenate([fl['w_uk'].reshape(KV_RANK, -1), fl['w_uv'].reshape(KV_RANK, -1)], axis=1).astype(BF16)
        w['w_g'] = jnp.moveaxis(jnp.concatenate([fl['w_rg'], fl['w_ig']], axis=0), 0, 1).astype(BF16)
        w['b4'] = jnp.concatenate([fl['b_rg'], fl['b_ig']], axis=0)
    for n in ('q_norm', 'kv_norm', 'w_o_mla', 'lru_conv_w', 'lru_conv_b', 'lru_lambda', 'w_o_lru', 'w_out', 'ln1_g',
              'ln1_b', 'w_up', 'ffn_conv_w', 'ffn_conv_b', 'w_down', 'ln2_g', 'ln2_b'):
        if n in fl:
            w[n] = fl[n]
    return w


def _layer_grads(g):
    out = {}
    if 'w_in' in g:
        lg, lx, gm, gl, cq, ckv, kr, krs = jnp.split(g['w_in'], [1024, 2048, 3072, 4096, 4352, 4480, 4544], axis=1)
        out['w_in'] = jnp.concatenate([cq, ckv, kr + _swap_halves(krs), lg, lx, gm, gl], axis=1)
    if 'w_q' in g:
        gq = g['w_q'].reshape(Q_RANK, HEADS, 2 * LANE)
        out['w_uq'] = jnp.concatenate([gq[..., :QK_NOPE], gq[..., QK_NOPE:QK_NOPE + QK_ROPE] + _swap_halves(gq[..., QK_NOPE + QK_ROPE:])], axis=-1)
    if 'w_kv' in g:
        out['w_uk'] = g['w_kv'][:, :HEADS * QK_NOPE].reshape(KV_RANK, HEADS, QK_NOPE)
        out['w_uv'] = g['w_kv'][:, HEADS * QK_NOPE:].reshape(KV_RANK, HEADS, V_HEAD)
    if 'w_g' in g:
        gg = jnp.moveaxis(g['w_g'], 1, 0)
        out['w_rg'], out['w_ig'] = gg[:2], gg[2:]
    if 'b4' in g:
        out['b_rg'], out['b_ig'] = g['b4'][:2], g['b4'][2:]
    for n in ('q_norm', 'kv_norm', 'lru_conv_b', 'ln1_g', 'ln1_b', 'ffn_conv_b', 'ln2_g', 'ln2_b'):
        if n in g:
            out[n] = g[n].reshape(-1)
    for n in ('w_o_mla', 'lru_conv_w', 'lru_lambda', 'w_o_lru', 'w_out', 'w_up', 'ffn_conv_w', 'w_down'):
        if n in g:
            out[n] = g[n]
    return out


def _rope_table(tp):
    half = QK_ROPE // 2
    inv_freq = jnp.exp(-math.log(ROPE_THETA) * jnp.arange(half, dtype=F32) / half)
    ang = jnp.arange(tp, dtype=F32)[:, None] * inv_freq[None, :]
    c, s = jnp.cos(ang), jnp.sin(ang)
    return jnp.concatenate([c, c, -s, s], axis=1)


def _local_step(x, target, meta, ln0_g, ln0_b, layer_w, t_pad, emit):
    seq = x.shape[0]
    t_real = N_META + seq
    zpad = jnp.zeros((t_pad - t_real, D_MODEL), F32)
    xin = jnp.concatenate([meta, x, zpad], axis=0)
    tgt = jnp.concatenate([jnp.zeros((N_META, D_MODEL), F32), target, zpad], axis=0)
    cs = _rope_table(t_pad)
    h = _ln_fwd([(1.0, xin)], ln0_g, ln0_b, "ln0")
    saved = []
    for l in range(DEPTH):
        w_in, rest_of_weights = layer_w[l](h)
        h, s = _layer_fwd(h, w_in, rest_of_weights, cs, t_real, "l%d_" % l)
        saved.append(s)
    dy, lossvec = _loss_head(h, tgt, t_real, "loss_head")
    terms, tok = [(1.0, dy)], None
    for l in reversed(range(DEPTH)):
        terms, tok = _layer_bwd(terms, saved[l], cs, t_real, "l%d_" % l,
                                functools.partial(lambda stage, g, l: emit(l, stage, _layer_grads(g)), l=l), tok)
    dxin, dg0, db0 = _ln_bwd(terms, [(1.0, xin)], _after(ln0_g, tok), "ln0_bwd")
    emit(None, 'head', {'meta_tokens': dxin[:N_META], 'ln0_g': dg0.reshape(-1), 'ln0_b': db0.reshape(-1), 'loss': lossvec})
    return dxin[N_META:t_real]


_HBM = pl.BlockSpec(memory_space=pltpu.HBM)
_SEM = pl.BlockSpec(memory_space=pltpu.SEMAPHORE)
_SIDE_EFFECT = pltpu.SideEffectType.DATAFLOW_SIDE_EFFECTING


def _peer_copies(src_refs, land_refs, scatters, send_sems, recv_sems):
    x, y, c = lax.axis_index("x"), lax.axis_index("y"), lax.axis_index("c")
    me = 4 * x + 2 * y + c
    copies = []
    for k in range(1, N_DEV):
        px = 1 - x if k & 4 else x
        py = 1 - y if k & 2 else y
        pc = 1 - c if k & 1 else c
        for t, (src, land) in enumerate(zip(src_refs, land_refs)):
            copies.append(pltpu.make_async_remote_copy(
                src_ref=src.at[4 * px + 2 * py + pc] if scatters[t] else src, dst_ref=land.at[me],
                send_sem=send_sems.at[7 * t + k - 1], recv_sem=recv_sems.at[7 * t + k - 1],
                device_id=(px, py, pc), device_id_type=pl.DeviceIdType.MESH))
    return me, copies


def _exchange_start(groups, name):
    flat = [it for grp in groups for it in grp]
    nt, ng = len(flat), len(groups)
    scatters = [sc for _, sc in flat]
    srcs = [pltpu.with_memory_space_constraint(a, pltpu.HBM) for a, _ in flat]
    land_shapes = [a.shape if sc else (N_DEV,) + a.shape for a, sc in flat]
    lands = [pltpu.with_memory_space_constraint(lax.empty(s, a.dtype), pltpu.HBM) for s, (a, _) in zip(land_shapes, flat)]
    bounds = [0]
    for grp in groups:
        bounds.append(bounds[-1] + len(grp))

    def body(*refs):
        src_refs, land_refs = refs[:nt], refs[nt:2 * nt]
        sem_refs = refs[2 * nt:2 * nt + 2 * ng]
        token_ref = refs[4 * nt + 2 * ng]
        for gi in range(ng):
            lo, hi = bounds[gi], bounds[gi + 1]
            _, copies = _peer_copies(src_refs[lo:hi], land_refs[lo:hi], scatters[lo:hi], sem_refs[2 * gi], sem_refs[2 * gi + 1])
            for cp in copies:
                cp.start()
        token_ref[...] = jnp.zeros_like(token_ref)

    out_shape = []
    for grp in groups:
        out_shape += [pltpu.SemaphoreType.DMA((7 * len(grp),)), pltpu.SemaphoreType.DMA((7 * len(grp),))]
    out_shape += [pltpu.HBM(a.shape, a.dtype) for a in srcs] + [pltpu.HBM(s, a.dtype) for s, a in zip(land_shapes, srcs)]
    out_shape += [jax.ShapeDtypeStruct((SUBLANE, LANE), F32)]
    res = pl.pallas_call(
        body, name=name, out_shape=out_shape,
        in_specs=[_HBM] * (2 * nt),
        out_specs=[_SEM] * (2 * ng) + [_HBM] * (2 * nt) + [pl.BlockSpec(memory_space=pltpu.VMEM)],
        input_output_aliases={t: 2 * ng + t for t in range(2 * nt)},
        compiler_params=pltpu.CompilerParams(has_side_effects=_SIDE_EFFECT),
    )(*srcs, *lands)
    sems, thru, token = res[:2 * ng], res[2 * ng:2 * ng + 2 * nt], res[-1]
    states = []
    for gi in range(ng):
        lo, hi = bounds[gi], bounds[gi + 1]
        states.append((sems[2 * gi], sems[2 * gi + 1], thru[lo:hi], thru[nt + lo:nt + hi], scatters[lo:hi]))
    return states, token[0, 0]


def _exchange_wait(state, after, name):
    send_sems, recv_sems, srcs, lands, scatters = state
    n = len(srcs)

    def body(*refs):
        _, copies = _peer_copies(refs[:n], refs[n:2 * n], scatters, refs[2 * n], refs[2 * n + 1])
        for cp in copies:
            cp.wait_send()
        for cp in copies:
            cp.wait_recv()

    res = pl.pallas_call(
        body, name=name,
        out_shape=[pltpu.HBM(a.shape, a.dtype) for a in srcs] + [pltpu.HBM(a.shape, a.dtype) for a in lands],
        in_specs=[_HBM] * (2 * n) + [_SEM, _SEM, _HBM],
        out_specs=[_HBM] * (2 * n),
        input_output_aliases={t: t for t in range(2 * n)},
        compiler_params=pltpu.CompilerParams(has_side_effects=_SIDE_EFFECT),
    )(*srcs, *lands, send_sems, recv_sems, pltpu.with_memory_space_constraint(after, pltpu.HBM))
    me = 4 * lax.axis_index("x") + 2 * lax.axis_index("y") + lax.axis_index("c")
    out = []
    for src, land, sc in zip(res[:n], res[n:], scatters):
        own = lax.dynamic_index_in_dim(src, me, 0, keepdims=True) if sc else src[None]
        out.append(lax.dynamic_update_slice_in_dim(land, own, me, 0))
    return out


def _as_rows(shape):
    return (1, shape[0]) if len(shape) == 1 else (math.prod(shape[:-1]), shape[-1])


def _sum_adamw(pieces, w, m, v, name):
    shape = w.shape
    nl = len(pieces)
    if nl > 1 and _as_rows(shape[1:])[0] % 16:
        pieces, nl = [jnp.stack(pieces, axis=1)], 1
    rows, cols = _as_rows(shape)
    rl = rows // nl
    cap = max(16, (1 << 18) // cols // 16 * 16)
    tr = _tile(rl, cap, 16)
    nb = rl // tr
    c1 = 1.0 / (1.0 - ADAM_B1 ** ADAM_STEP)
    c2 = 1.0 / (1.0 - ADAM_B2 ** ADAM_STEP)

    def body(*refs):
        p_refs = refs[:nl]
        w_ref, m_ref, v_ref, g_ref, d_ref, nm_ref, nv_ref = refs[nl:]
        li = pl.program_id(0)

        def total(p_ref):
            acc = p_ref[0].astype(F32)
            for k in range(1, N_DEV):
                acc = acc + p_ref[k].astype(F32)
            return acc

        gg = total(p_refs[0])
        for l in range(1, nl):
            gg = jnp.where(li == l, total(p_refs[l]), gg)
        nm = ADAM_B1 * m_ref[...] + (1.0 - ADAM_B1) * gg
        nv = ADAM_B2 * v_ref[...] + (1.0 - ADAM_B2) * (gg * gg)
        g_ref[...] = gg
        d_ref[...] = -ADAM_LR * ((nm * c1) / (jnp.sqrt(nv * c2) + ADAM_EPS) + ADAM_WD * w_ref[...])
        nm_ref[...] = nm
        nv_ref[...] = nv

    blk = pl.BlockSpec((tr, cols), lambda li, i: (li * nb + i, 0))
    p_specs = [pl.BlockSpec((N_DEV, tr, cols), functools.partial(lambda li, i, l: (0, jnp.where(li == l, i, 0), 0), l=l))
               for l in range(nl)]
    res = pl.pallas_call(
        body, name=name, grid=(nl, nb),
        in_specs=p_specs + [blk] * 3, out_specs=[blk] * 4,
        out_shape=[jax.ShapeDtypeStruct((rows, cols), F32)] * 4,
        compiler_params=_cparams("parallel", "parallel"),
    )(*[p.reshape(N_DEV, rl, cols) for p in pieces], *[a.reshape(rows, cols) for a in (w, m, v)])
    return [r.reshape(shape) for r in res]


def _to_shards(full, axis):
    shp = full.shape
    a = full.reshape(shp[:axis] + (N_DEV, shp[axis] // N_DEV) + shp[axis + 1:])
    return jnp.moveaxis(a, axis, 0)


def _from_shards(blocks, axis):
    a = jnp.moveaxis(blocks, 0, axis)
    shp = a.shape
    return a.reshape(shp[:axis] + (shp[axis] * shp[axis + 1],) + shp[axis + 2:])


def kernel(x, meta_tokens, ln0_g, ln0_b, w_in, q_norm, kv_norm, w_uq, w_uk, w_uv, w_o_mla, lru_conv_w, lru_conv_b, w_rg, b_rg, w_ig, b_ig, lru_lambda, w_o_lru, w_out, ln1_g, ln1_b, w_up, ffn_conv_w, ffn_conv_b, w_down, ln2_g, ln2_b, loss_target, m_meta_tokens, m_ln0_g, m_ln0_b, m_w_in, m_q_norm, m_kv_norm, m_w_uq, m_w_uk, m_w_uv, m_w_o_mla, m_lru_conv_w, m_lru_conv_b, m_w_rg, m_b_rg, m_w_ig, m_b_ig, m_lru_lambda, m_w_o_lru, m_w_out, m_ln1_g, m_ln1_b, m_w_up, m_ffn_conv_w, m_ffn_conv_b, m_w_down, m_ln2_g, m_ln2_b, v_meta_tokens, v_ln0_g, v_ln0_b, v_w_in, v_q_norm, v_kv_norm, v_w_uq, v_w_uk, v_w_uv, v_w_o_mla, v_lru_conv_w, v_lru_conv_b, v_w_rg, v_b_rg, v_w_ig, v_b_ig, v_lru_lambda, v_w_o_lru, v_w_out, v_ln1_g, v_ln1_b, v_w_up, v_ffn_conv_w, v_ffn_conv_b, v_w_down, v_ln2_g, v_ln2_b):
    args = (meta_tokens, ln0_g, ln0_b, w_in, q_norm, kv_norm, w_uq, w_uk, w_uv, w_o_mla, lru_conv_w, lru_conv_b, w_rg, b_rg, w_ig, b_ig, lru_lambda, w_o_lru, w_out, ln1_g, ln1_b, w_up, ffn_conv_w, ffn_conv_b, w_down, ln2_g, ln2_b)
    ms = (m_meta_tokens, m_ln0_g, m_ln0_b, m_w_in, m_q_norm, m_kv_norm, m_w_uq, m_w_uk, m_w_uv, m_w_o_mla, m_lru_conv_w, m_lru_conv_b, m_w_rg, m_b_rg, m_w_ig, m_b_ig, m_lru_lambda, m_w_o_lru, m_w_out, m_ln1_g, m_ln1_b, m_w_up, m_ffn_conv_w, m_ffn_conv_b, m_w_down, m_ln2_g, m_ln2_b)
    vs = (v_meta_tokens, v_ln0_g, v_ln0_b, v_w_in, v_q_norm, v_kv_norm, v_w_uq, v_w_uk, v_w_uv, v_w_o_mla, v_lru_conv_w, v_lru_conv_b, v_w_rg, v_b_rg, v_w_ig, v_b_ig, v_lru_lambda, v_w_o_lru, v_w_out, v_ln1_g, v_ln1_b, v_w_up, v_ffn_conv_w, v_ffn_conv_b, v_w_down, v_ln2_g, v_ln2_b)
    wd, md, vd = dict(zip(WEIGHTS, args)), dict(zip(WEIGHTS, ms)), dict(zip(WEIGHTS, vs))

    def shard_axis(n, l):
        return SHARD_AXIS[n] - (0 if l is None else 1)

    def shard(n, l):
        a = wd[n] if l is None else wd[n][l]
        return a.astype(BF16) if n in BIG else a

    first = [('meta_tokens', None), ('w_in', 0)]
    staged = [[(n, 0) for n in names] for names in STAGE_WEIGHTS]
    later = [(n, 1) for n in SHARDED if n != 'meta_tokens']
    gather, token = _exchange_start([[(shard(*k), False) for k in keys] for keys in [first] + staged + [later]], "gather_start")

    def arrive(gi, keys, after, name):
        return {k: _from_shards(b, shard_axis(*k)) for k, b in zip(keys, _exchange_wait(gather[gi], after, name))}

    def layer_weights(got, l, names):
        fl = {n: wd[n][l] for n in names if n in REPLICATED}
        fl.update({n: a for (n, _), a in got.items() if n in names})
        return _layer_weights(fl)

    ln0_g = _after(wd['ln0_g'], token)
    got_first = arrive(0, first, ln0_g, "gather_wait_first")

    def first_layer(h):
        def more(stage, after):
            got = arrive(1 + stage, staged[stage], after, "gather_wait_l0_%d" % stage)
            return layer_weights(got, 0, STAGE_WEIGHTS[stage] + STAGE_REPLICATED[stage])
        return _w_in_kernel(got_first['w_in', 0]), more

    def second_layer(h):
        got = arrive(1 + len(staged), later, h, "gather_wait_l1")
        return _w_in_kernel(got['w_in', 1]), lambda stage, after: layer_weights(got, 1, STAGE_WEIGHTS[stage] + STAGE_REPLICATED[stage])

    sent = []
    pending = []

    def send(l, stage, grads):
        for n, g in grads.items():
            if n in SHARD_AXIS:
                g = _to_shards(g, shard_axis(n, l))
                pending.append(((n, l), (g.astype(BF16) if n in BIG else g, True)))
            else:
                pending.append(((n, l), (g.astype(BF16) if n in LARGE_REPLICATED else g, False)))
        if l == DEPTH - 1 and stage != 'in':
            return None
        (state,), tok = _exchange_start([[it for _, it in pending]], "grads_start_%s_%s" % (l, stage))
        sent.append(([k for k, _ in pending], state))
        pending.clear()
        return tok

    seq = x.shape[1]
    t_pad = -(-(N_META + seq + MIN_PAD_ROWS) // LANE) * LANE
    grad_x = _local_step(x[0], loss_target[0], got_first['meta_tokens', None], ln0_g, wd['ln0_b'],
                         [first_layer, second_layer], t_pad, send)

    pieces = {}
    for gi, (keys, state) in enumerate(sent):
        pieces.update(zip(keys, _exchange_wait(state, grad_x, "grads_wait_%d" % gi)))
    loss = jnp.sum(pieces['loss', None])
    outs = {}
    for n in WEIGHTS:
        ps = [pieces[n, None]] if (n, None) in pieces else [pieces[n, l] for l in range(DEPTH)]
        outs[n] = _sum_adamw(ps, wd[n], md[n], vd[n], "adamw_" + n)
    res = [loss, grad_x[None]]
    for k in range(4):
        res += [outs[n][k] for n in WEIGHTS]
    return tuple(res)
```

```python
import functools
import math

import jax
import jax.numpy as jnp
from jax import lax
from jax.experimental import pallas as pl
from jax.experimental.pallas import tpu as pltpu

F32 = jnp.float32
BF16 = jnp.bfloat16

N_DEV = 8
D_MODEL = 1024
N_META = 16
HEADS = 8
QK_NOPE = 128
QK_ROPE = 64
V_HEAD = 128
Q_RANK = 256
KV_RANK = 128
ROPE_THETA = 10000.0
LRU_BLOCKS = 8
LRU_C = 8.0
D_FF = 2816
DEPTH = 2
DN_ALPHA = (2.0 * DEPTH) ** 0.25
LN_EPS = 1e-5
RMS_EPS = 1e-6
LN2 = math.log(2.0)
ATT_SCALE = 1.0 / math.sqrt(QK_NOPE + QK_ROPE) / LN2
NEG_BIG = -1e30

ADAM_LR = 0.001
ADAM_B1 = 0.9
ADAM_B2 = 0.999
ADAM_EPS = 1e-08
ADAM_WD = 0.01
ADAM_STEP = 10

MIN_PAD_ROWS = 2
LANE = 128
SUBLANE = 8
VMEM_LIMIT = 56 * 1024 * 1024

PROJ_COLS = 4 * D_MODEL + Q_RANK + KV_RANK + 2 * QK_ROPE
C_LRU_G, C_LRU_X, C_G_MLA, C_G_LRU = 0, D_MODEL, 2 * D_MODEL, 3 * D_MODEL
C_CQ = 4 * D_MODEL
C_CKV = C_CQ + Q_RANK
C_KRP = C_CKV + KV_RANK

WEIGHTS = ['meta_tokens', 'ln0_g', 'ln0_b', 'w_in', 'q_norm', 'kv_norm', 'w_uq', 'w_uk', 'w_uv', 'w_o_mla',
           'lru_conv_w', 'lru_conv_b', 'w_rg', 'b_rg', 'w_ig', 'b_ig', 'lru_lambda', 'w_o_lru', 'w_out',
           'ln1_g', 'ln1_b', 'w_up', 'ffn_conv_w', 'ffn_conv_b', 'w_down', 'ln2_g', 'ln2_b']
SHARD_AXIS = {'meta_tokens': 1, 'w_in': 2, 'w_uq': 1, 'w_o_mla': 1, 'lru_conv_w': 2, 'b_rg': 2, 'b_ig': 2,
              'lru_lambda': 2, 'w_o_lru': 1, 'w_out': 1, 'w_up': 2, 'ffn_conv_w': 2, 'w_down': 1}
BIG = ['w_in', 'w_uq', 'w_o_mla', 'w_o_lru', 'w_out', 'w_up', 'w_down']
SHARDED = [n for n in WEIGHTS if n in SHARD_AXIS]
REPLICATED = [n for n in WEIGHTS if n not in SHARD_AXIS]
LARGE_REPLICATED = ['w_uk', 'w_uv', 'w_rg', 'w_ig']
STAGE_WEIGHTS = [['w_uq', 'lru_conv_w', 'b_rg', 'b_ig', 'lru_lambda'], ['w_o_mla', 'w_o_lru', 'w_out'], ['w_up', 'ffn_conv_w', 'w_down']]
STAGE_REPLICATED = [['q_norm', 'kv_norm', 'w_uk', 'w_uv', 'lru_conv_b', 'w_rg', 'w_ig'], ['ln1_g', 'ln1_b'], ['ffn_conv_b', 'ln2_g', 'ln2_b']]


def _cparams(*sem):
    return pltpu.CompilerParams(dimension_semantics=sem, vmem_limit_bytes=VMEM_LIMIT)


def _tile(n, cap, unit=LANE):
    best = None
    t = unit
    while t <= min(n, cap):
        if n % t == 0:
            best = t
        t += unit
    return n if best is None else best


def _sigmoid(x):
    return 1.0 / (1.0 + jnp.exp(-x))


_GELU_C = math.sqrt(2.0 / math.pi)


_GELU_A = 0.044715


def _gelu(x):
    t = jnp.tanh(x * (_GELU_C + (_GELU_C * _GELU_A) * (x * x)))
    hx = 0.5 * x
    return hx + hx * t


def _gelu_and_grad(x):
    x2 = x * x
    t = jnp.tanh(x * (_GELU_C + (_GELU_C * _GELU_A) * x2))
    hx = 0.5 * x
    dg = 0.5 + 0.5 * t + (hx * (1.0 - t * t)) * (_GELU_C + (3.0 * _GELU_C * _GELU_A) * x2)
    return hx + hx * t, dg


def _softplus_neg(lam):
    z = jnp.exp(-jnp.abs(lam))
    w = 1.0 + z
    log1p = jnp.where(w == 1.0, z, jnp.log(w) * z / (w - 1.0))
    return jnp.maximum(-lam, 0.0) + log1p


def _row_ids(shape, row0=0):
    return lax.broadcasted_iota(jnp.int32, shape, 0) + row0


def _matmul(a, b, name, ta=False, tb=False, out_dtype=F32, tm_cap=1408, tn_cap=1024, tk_cap=2048):
    if ta:
        kdim, m = a.shape
    else:
        m, kdim = a.shape
    if tb:
        n, k2 = b.shape
    else:
        k2, n = b.shape
    assert kdim == k2, (a.shape, b.shape, ta, tb)
    tm, tn, tk = _tile(m, tm_cap), _tile(n, tn_cap), _tile(kdim, tk_cap)
    nk = kdim // tk

    def body(a_ref, b_ref, o_ref, *acc):
        dn = (((0 if ta else 1,), (1 if tb else 0,)), ((), ()))
        part = lax.dot_general(a_ref[...].astype(BF16), b_ref[...].astype(BF16), dn, preferred_element_type=F32)
        if nk == 1:
            o_ref[...] = part.astype(o_ref.dtype)
            return
        acc_ref, k = acc[0], pl.program_id(2)

        @pl.when(k == 0)
        def _():
            acc_ref[...] = part

        @pl.when(k > 0)
        def _():
            acc_ref[...] += part

        @pl.when(k == nk - 1)
        def _():
            o_ref[...] = acc_ref[...].astype(o_ref.dtype)

    a_spec = pl.BlockSpec((tk, tm), lambda i, j, k: (k, i)) if ta else pl.BlockSpec((tm, tk), lambda i, j, k: (i, k))
    b_spec = pl.BlockSpec((tn, tk), lambda i, j, k: (j, k)) if tb else pl.BlockSpec((tk, tn), lambda i, j, k: (k, j))
    return pl.pallas_call(
        body, name=name,
        grid=(m // tm, n // tn, nk),
        in_specs=[a_spec, b_spec],
        out_specs=pl.BlockSpec((tm, tn), lambda i, j, k: (i, j)),
        out_shape=jax.ShapeDtypeStruct((m, n), out_dtype),
        scratch_shapes=[pltpu.VMEM((tm, tn), F32)] if nk > 1 else [],
        compiler_params=_cparams("parallel", "parallel", "arbitrary"),
    )(a, b)


class Rw:
    def __init__(self, arr, width=None, cb=0):
        self.arr, self.width, self.cb = arr, (arr.shape[1] if width is None else width), cb


class Pm:
    def __init__(self, arr):
        self.arr = arr


class Into:
    def __init__(self, arr, col0, width):
        self.arr, self.col0, self.width = arr, col0, width


def _call_with_into(body, name, grid, in_specs, operands, outs, spec_of, shape_of, extra_out_specs, extra_out_shape, sem):
    intos = [(k, o) for k, o in enumerate(outs) if isinstance(o, Into)]
    aliases = {len(operands) + n: k for n, (k, _) in enumerate(intos)}
    return pl.pallas_call(
        body, name=name, grid=grid,
        in_specs=in_specs + [pl.BlockSpec(memory_space=pl.ANY)] * len(intos),
        out_specs=[spec_of(o) for o in outs] + extra_out_specs,
        out_shape=[jax.ShapeDtypeStruct(o.arr.shape, o.arr.dtype) if isinstance(o, Into) else shape_of(o) for o in outs]
        + extra_out_shape,
        input_output_aliases=aliases,
        compiler_params=_cparams(sem),
    )(*operands, *[o.arr for _, o in intos])


def _rows(fn, name, ins, outs, accs=(), tm_cap=384):
    tp = next(o.arr.shape[0] for o in ins if isinstance(o, Rw))
    tm = _tile(tp, tm_cap)
    n_in, n_out, n_acc = len(ins), len(outs), len(accs)
    n_into = sum(isinstance(o, Into) for o in outs)

    def body(*refs):
        i = pl.program_id(0)
        res = fn(i * tm, *[r[...] for r in refs[:n_in]])
        if not isinstance(res, (tuple, list)):
            res = (res,)
        assert len(res) == n_out + n_acc, (name, len(res))
        out_refs = refs[n_in + n_into:]
        for k in range(n_out):
            out_refs[k][...] = res[k].astype(out_refs[k].dtype)
        for k in range(n_acc):
            ref = out_refs[n_out + k]

            @pl.when(i == 0)
            def _():
                ref[...] = jnp.zeros_like(ref)

            ref[...] += res[n_out + k]

    in_specs = []
    for o in ins:
        if isinstance(o, Rw):
            in_specs.append(pl.BlockSpec((tm, o.width), functools.partial(lambda i, cb: (i, cb), cb=o.cb)))
        else:
            in_specs.append(pl.BlockSpec(o.arr.shape, functools.partial(lambda i, nd: (0,) * nd, nd=o.arr.ndim)))

    def spec_of(o):
        if isinstance(o, Into):
            assert o.col0 % o.width == 0, (name, o.col0, o.width)
            return pl.BlockSpec((tm, o.width), functools.partial(lambda i, cb: (i, cb), cb=o.col0 // o.width))
        return pl.BlockSpec((tm, o[0]), lambda i: (i, 0))

    return _call_with_into(
        body, name, (tp // tm,), in_specs, [o.arr for o in ins], list(outs), spec_of,
        lambda o: jax.ShapeDtypeStruct((tp, o[0]), o[1]),
        [pl.BlockSpec(s, functools.partial(lambda i, nd: (0,) * nd, nd=len(s))) for s in accs],
        [jax.ShapeDtypeStruct(s, F32) for s in accs], "arbitrary")


class Cl:
    def __init__(self, arr, col0=0):
        self.arr, self.col0 = arr, col0


def _cols(fn, name, ins, outs, ncols, tc):
    assert ncols % tc == 0
    n_in, n_out = len(ins), len(outs)
    n_into = sum(isinstance(o, Into) for o in outs)

    def body(*refs):
        res = fn(*[r[...] for r in refs[:n_in]])
        if not isinstance(res, (tuple, list)):
            res = (res,)
        assert len(res) == n_out, (name, len(res))
        out_refs = refs[n_in + n_into:]
        for k in range(n_out):
            out_refs[k][...] = res[k].astype(out_refs[k].dtype)

    in_specs = []
    for o in ins:
        assert o.col0 % tc == 0, (name, o.col0, tc)
        in_specs.append(pl.BlockSpec((o.arr.shape[0], tc), functools.partial(lambda j, off: (0, j + off), off=o.col0 // tc)))

    def spec_of(o):
        if isinstance(o, Into):
            assert o.col0 % tc == 0 and o.width == ncols, (name, o.col0, o.width)
            return pl.BlockSpec((o.arr.shape[0], tc), functools.partial(lambda j, off: (0, j + off), off=o.col0 // tc))
        return pl.BlockSpec((o[0], tc), lambda j: (0, j))

    return _call_with_into(body, name, (ncols // tc,), in_specs, [o.arr for o in ins], list(outs), spec_of,
                           lambda o: jax.ShapeDtypeStruct((o[0], ncols), o[1]), [], [], "parallel")


def _ln_stats(u):
    mu = jnp.mean(u, axis=-1, keepdims=True)
    xc = u - mu
    var = jnp.mean(xc * xc, axis=-1, keepdims=True)
    rstd = lax.rsqrt(var + LN_EPS)
    return xc * rstd, rstd


def _ln_fwd(terms, g, b, name):
    coefs = [c for c, _ in terms]

    def fn(row0, *blk):
        xs, (gg, bb) = blk[:len(coefs)], blk[len(coefs):]
        u = sum(c * x for c, x in zip(coefs, xs))
        xhat, _ = _ln_stats(u)
        return xhat * gg + bb

    d = terms[0][1].shape[1]
    return _rows(fn, name, [Rw(x) for _, x in terms] + [Pm(g.reshape(1, d)), Pm(b.reshape(1, d))], [(d, F32)])[0]


def _ln_bwd(dy_terms, u_terms, g, name):
    dc = [c for c, _ in dy_terms]
    uc = [c for c, _ in u_terms]
    d = u_terms[0][1].shape[1]

    def fn(row0, *blk):
        dys = blk[:len(dc)]
        xs = blk[len(dc):len(dc) + len(uc)]
        gg = blk[-1]
        dy = sum(c * x for c, x in zip(dc, dys))
        u = sum(c * x for c, x in zip(uc, xs))
        xhat, rstd = _ln_stats(u)
        gdy = dy * gg
        m1 = jnp.mean(gdy, axis=-1, keepdims=True)
        m2 = jnp.mean(gdy * xhat, axis=-1, keepdims=True)
        du = rstd * (gdy - m1 - xhat * m2)
        return du, jnp.sum(dy * xhat, axis=0, keepdims=True), jnp.sum(dy, axis=0, keepdims=True)

    ins = [Rw(x) for _, x in dy_terms] + [Rw(x) for _, x in u_terms] + [Pm(g.reshape(1, d))]
    return _rows(fn, name, ins, [(d, F32)], accs=[(1, d), (1, d)])


def _loss_head(y, tgt, t_real, name):
    d = y.shape[1]

    def fn(row0, yb, tb):
        rows = _row_ids(yb.shape, row0)
        live = (rows >= N_META) & (rows < t_real)
        diff = jnp.where(live, yb - tb, 0.0)
        return diff * (1.0 / d), jnp.sum(diff * diff, axis=0, keepdims=True) * (0.5 / d)

    return _rows(fn, name, [Rw(y), Rw(tgt)], [(d, F32)], accs=[(1, d)])


def _rms(x, g):
    r = lax.rsqrt(jnp.mean(x * x, axis=-1, keepdims=True) + RMS_EPS)
    return x * r * g


def _rms_bwd(dy, x, g):
    r = lax.rsqrt(jnp.mean(x * x, axis=-1, keepdims=True) + RMS_EPS)
    gdy = dy * g
    dx = r * gdy - x * (r * r * r) * jnp.mean(gdy * x, axis=-1, keepdims=True)
    return dx, jnp.sum(dy * x * r, axis=0, keepdims=True)


def _mla_norms(proj, qn, kvn, name):
    def fn(row0, cq, ckv, g1, g2):
        return _rms(cq, g1), _rms(ckv, g2)

    return _rows(fn, name, [Rw(proj, Q_RANK, C_CQ // Q_RANK), Rw(proj, KV_RANK, C_CKV // KV_RANK),
                            Pm(qn.reshape(1, Q_RANK)), Pm(kvn.reshape(1, KV_RANK))],
                 [(Q_RANK, F32), (KV_RANK, F32)])


def _mla_norms_bwd(dcqn, dckvn, dkrp, proj, qn, kvn, dproj, name):
    def fn(row0, d1, d2, dkr, cq, ckv, g1, g2):
        dx1, dg1 = _rms_bwd(d1, cq, g1)
        dx2, dg2 = _rms_bwd(d2, ckv, g2)
        return jnp.concatenate([dx1, dx2, dkr], axis=1), dg1, dg2

    return _rows(fn, name, [Rw(dcqn), Rw(dckvn), Rw(dkrp), Rw(proj, Q_RANK, C_CQ // Q_RANK), Rw(proj, KV_RANK, C_CKV // KV_RANK),
                            Pm(qn.reshape(1, Q_RANK)), Pm(kvn.reshape(1, KV_RANK))],
                 [Into(dproj, C_CQ, PROJ_COLS - C_CQ)], accs=[(1, Q_RANK), (1, KV_RANK)])


def _fold_rope(z):
    return z + pltpu.roll(z, QK_ROPE, 1)


def _mla_pack(qext, kv, proj, cs, name):
    tp = qext.shape[0]
    tm = _tile(tp, 384)
    hw, nope_all = 2 * LANE, HEADS * QK_NOPE

    def body(q_ref, kv_ref, kr_ref, cs_ref, qo_ref, ko_ref, vo_ref):
        cs_ = cs_ref[...]
        low = lax.broadcasted_iota(jnp.int32, cs_.shape, 1) < QK_ROPE
        kr = _fold_rope(kr_ref[...] * cs_).astype(BF16)
        for h in range(HEADS):
            qr = jnp.where(low, _fold_rope(q_ref[:, h * hw + QK_NOPE:(h + 1) * hw] * cs_), 0.0)
            qo_ref[:, h * hw:h * hw + QK_NOPE] = (q_ref[:, h * hw:h * hw + QK_NOPE] * ATT_SCALE).astype(BF16)
            qo_ref[:, h * hw + QK_NOPE:(h + 1) * hw] = (qr * ATT_SCALE).astype(BF16)
            ko_ref[:, h * hw:h * hw + QK_NOPE] = kv_ref[:, h * QK_NOPE:(h + 1) * QK_NOPE].astype(BF16)
            ko_ref[:, h * hw + QK_NOPE:(h + 1) * hw] = kr
        vo_ref[...] = kv_ref[:, nope_all:].astype(BF16)

    row = lambda w: pl.BlockSpec((tm, w), lambda i: (i, 0))
    return pl.pallas_call(
        body, name=name, grid=(tp // tm,),
        in_specs=[row(HEADS * hw), row(2 * nope_all), pl.BlockSpec((tm, LANE), lambda i: (i, C_KRP // LANE)), row(LANE)],
        out_specs=[row(HEADS * hw), row(HEADS * hw), row(nope_all)],
        out_shape=[jax.ShapeDtypeStruct((tp, HEADS * hw), BF16),
                   jax.ShapeDtypeStruct((tp, HEADS * hw), BF16),
                   jax.ShapeDtypeStruct((tp, nope_all), BF16)],
        compiler_params=_cparams("parallel"),
    )(qext, kv, proj, cs)


def _mla_unpack(dq, dk, dv, cs, name):
    tp = dq.shape[0]
    tm = _tile(tp, 384)
    hw, nope_all = 2 * LANE, HEADS * QK_NOPE

    def body(dq_ref, dk_ref, dv_ref, cs_ref, dqe_ref, dkv_ref, dkr_ref):
        cs_ = cs_ref[...]
        low = lax.broadcasted_iota(jnp.int32, cs_.shape, 1) < QK_ROPE
        dkr = None
        for h in range(HEADS):
            dqe_ref[:, h * hw:h * hw + QK_NOPE] = dq_ref[:, h * hw:h * hw + QK_NOPE] * ATT_SCALE
            dqr = jnp.where(low, dq_ref[:, h * hw + QK_NOPE:(h + 1) * hw], 0.0) * ATT_SCALE
            dqe_ref[:, h * hw + QK_NOPE:(h + 1) * hw] = _fold_rope(dqr) * cs_
            dkv_ref[:, h * QK_NOPE:(h + 1) * QK_NOPE] = dk_ref[:, h * hw:h * hw + QK_NOPE]
            part = jnp.where(low, dk_ref[:, h * hw + QK_NOPE:(h + 1) * hw], 0.0)
            dkr = part if h == 0 else dkr + part
        dkv_ref[:, nope_all:] = dv_ref[...]
        dkr_ref[...] = _fold_rope(dkr) * cs_

    row = lambda w: pl.BlockSpec((tm, w), lambda i: (i, 0))
    return pl.pallas_call(
        body, name=name, grid=(tp // tm,),
        in_specs=[row(HEADS * hw), row(HEADS * hw), row(nope_all), row(LANE)],
        out_specs=[row(HEADS * hw), row(2 * nope_all), row(LANE)],
        out_shape=[jax.ShapeDtypeStruct((tp, HEADS * hw), F32),
                   jax.ShapeDtypeStruct((tp, 2 * nope_all), F32),
                   jax.ShapeDtypeStruct((tp, LANE), F32)],
        compiler_params=_cparams("parallel"),
    )(dq, dk, dv, cs)


def _attn_fwd(q, k, v, t_real, name):
    tp = q.shape[0]
    tq = _tile(tp, 1408)
    tkc = _tile(tp, 1408)
    nkc = -(-t_real // tkc)

    def body(q_ref, k_ref, v_ref, o_ref, lse_ref):
        qb = q_ref[...]
        m = l = acc = None
        for c in range(nkc):
            s = lax.dot_general(qb, k_ref[c * tkc:(c + 1) * tkc, :], (((1,), (1,)), ((), ())), preferred_element_type=F32)
            if (c + 1) * tkc > t_real:
                cols = lax.broadcasted_iota(jnp.int32, s.shape, 1) + c * tkc
                s = jnp.where(cols < t_real, s, NEG_BIG)
            mc = jnp.max(s, axis=-1, keepdims=True)
            m_new = mc if c == 0 else jnp.maximum(m, mc)
            p = jnp.exp2(s - m_new)
            lc = jnp.sum(p, axis=-1, keepdims=True)
            pv = jnp.dot(p.astype(BF16), v_ref[c * tkc:(c + 1) * tkc, :], preferred_element_type=F32)
            if c == 0:
                l, acc = lc, pv
            else:
                alpha = jnp.exp2(m - m_new)
                l, acc = alpha * l + lc, alpha * acc + pv
            m = m_new
        o_ref[...] = acc / l
        lse_ref[...] = m + jnp.log2(l)

    return pl.pallas_call(
        body, name=name, grid=(HEADS, tp // tq),
        in_specs=[pl.BlockSpec((tq, 2 * LANE), lambda h, i: (i, h)),
                  pl.BlockSpec((tp, 2 * LANE), lambda h, i: (0, h)),
                  pl.BlockSpec((tp, LANE), lambda h, i: (0, h))],
        out_specs=[pl.BlockSpec((tq, LANE), lambda h, i: (i, h)),
                   pl.BlockSpec((None, tq, 1), lambda h, i: (h, i, 0))],
        out_shape=[jax.ShapeDtypeStruct((tp, HEADS * LANE), F32),
                   jax.ShapeDtypeStruct((HEADS, tp, 1), F32)],
        compiler_params=_cparams("parallel", "parallel"),
    )(q, k, v)


def _attn_bwd(q, k, v, do, o, lse, t_real, name):
    tp = q.shape[0]
    tq = _tile(tp, 704, 64)
    tkc = _tile(tp, 1408)
    nkc = -(-t_real // tkc)

    def body(q_ref, k_ref, v_ref, do_ref, o_ref, lse_ref, dq_ref, dk_ref, dv_ref):
        i = pl.program_id(1)

        @pl.when(i == 0)
        def _():
            dk_ref[...] = jnp.zeros_like(dk_ref)
            dv_ref[...] = jnp.zeros_like(dv_ref)

        qb = q_ref[...]
        dob = do_ref[...]
        dob16 = dob.astype(BF16)
        dol2 = (dob * LN2).astype(BF16)
        delta = jnp.sum(dob * o_ref[...], axis=-1, keepdims=True) * LN2
        lse = lse_ref[...]
        dq = None
        for c in range(nkc):
            ks = slice(c * tkc, (c + 1) * tkc)
            kb = k_ref[ks, :]
            s = lax.dot_general(qb, kb, (((1,), (1,)), ((), ())), preferred_element_type=F32)
            p = jnp.exp2(s - lse)
            if (c + 1) * tkc > t_real:
                cols = lax.broadcasted_iota(jnp.int32, s.shape, 1) + c * tkc
                p = jnp.where(cols < t_real, p, 0.0)
            dp = lax.dot_general(dol2, v_ref[ks, :], (((1,), (1,)), ((), ())), preferred_element_type=F32)
            ds = (p * (dp - delta)).astype(BF16)
            dqc = jnp.dot(ds, kb, preferred_element_type=F32)
            dq = dqc if c == 0 else dq + dqc
            dk_ref[ks, :] += lax.dot_general(ds, qb, (((0,), (0,)), ((), ())), preferred_element_type=F32)
            dv_ref[ks, :] += lax.dot_general(p.astype(BF16), dob16, (((0,), (0,)), ((), ())), preferred_element_type=F32)
        dq_ref[...] = dq

    return pl.pallas_call(
        body, name=name, grid=(HEADS, tp // tq),
        in_specs=[pl.BlockSpec((tq, 2 * LANE), lambda h, i: (i, h)),
                  pl.BlockSpec((tp, 2 * LANE), lambda h, i: (0, h)),
                  pl.BlockSpec((tp, LANE), lambda h, i: (0, h)),
                  pl.BlockSpec((tq, LANE), lambda h, i: (i, h)),
                  pl.BlockSpec((tq, LANE), lambda h, i: (i, h)),
                  pl.BlockSpec((None, tq, 1), lambda h, i: (h, i, 0))],
        out_specs=[pl.BlockSpec((tq, 2 * LANE), lambda h, i: (i, h)),
                   pl.BlockSpec((tp, 2 * LANE), lambda h, i: (0, h)),
                   pl.BlockSpec((tp, LANE), lambda h, i: (0, h))],
        out_shape=[jax.ShapeDtypeStruct((tp, HEADS * 2 * LANE), F32),
                   jax.ShapeDtypeStruct((tp, HEADS * 2 * LANE), F32),
                   jax.ShapeDtypeStruct((tp, HEADS * LANE), F32)],
        compiler_params=_cparams("parallel", "arbitrary"),
    )(q, k, v, do, o, lse)


def _shift_rows(x, s):
    tp = x.shape[0]
    return x if s % tp == 0 else pltpu.roll(x, s % tp, 0)


def _conv_fwd_val(xm, w, b, pad_left):
    acc = b + w[0:1, :] * _shift_rows(xm, pad_left)
    for k in range(1, w.shape[0]):
        acc = acc + w[k:k + 1, :] * _shift_rows(xm, pad_left - k)
    return acc


def _conv_bwd_val(dy, xm, w, pad_left, live):
    kk = w.shape[0]
    dx = w[0:1, :] * _shift_rows(dy, -pad_left)
    dws = [jnp.sum(dy * _shift_rows(xm, pad_left), axis=0, keepdims=True)]
    for k in range(1, kk):
        dx = dx + w[k:k + 1, :] * _shift_rows(dy, k - pad_left)
        dws.append(jnp.sum(dy * _shift_rows(xm, pad_left - k), axis=0, keepdims=True))
    return jnp.where(live, dx, 0.0), jnp.concatenate(dws, axis=0), jnp.sum(dy, axis=0, keepdims=True)


def _lru_conv_fwd(proj, w, b, t_real, name):
    def fn(x, ww, bb):
        xm = jnp.where(_row_ids(x.shape) < t_real, x, 0.0)
        return _conv_fwd_val(xm, ww, bb, 2)

    return _cols(fn, name, [Cl(proj, C_LRU_X), Cl(w), Cl(b.reshape(1, -1))], [(proj.shape[0], F32)], D_MODEL, 128)[0]


def _lru_conv_bwd(dxc, proj, w, dproj, t_real, name):
    def fn(dy, x, ww):
        live = _row_ids(x.shape) < t_real
        xm = jnp.where(live, x, 0.0)
        dym = jnp.where(live, dy, 0.0)
        return _conv_bwd_val(dym, xm, ww, 2, live)

    return _cols(fn, name, [Cl(dxc), Cl(proj, C_LRU_X), Cl(w)],
                 [Into(dproj, C_LRU_X, D_MODEL), (w.shape[0], F32), (1, F32)], D_MODEL, 128)


def _ffn_conv_act(up, w, b, t_real, name):
    def fn(g, v, wg, wv, bg, bv):
        live = _row_ids(g.shape) < t_real
        gc = _conv_fwd_val(jnp.where(live, g, 0.0), wg, bg, 1)
        vc = _conv_fwd_val(jnp.where(live, v, 0.0), wv, bv, 1)
        return _gelu(gc) * vc

    b2 = b.reshape(1, -1)
    return _cols(fn, name, [Cl(up), Cl(up, D_FF), Cl(w), Cl(w, D_FF), Cl(b2), Cl(b2, D_FF)],
                 [(up.shape[0], F32)], D_FF, 128)[0]


def _ffn_conv_act_bwd(dm, up, w, b, t_real, name):
    tp, kk = up.shape[0], w.shape[0]
    nb = D_FF // LANE
    assert nb >= 2

    def body(dm_ref, g_ref, v_ref, wg_ref, wv_ref, bg_ref, bv_ref, dup_ref, dwg_ref, dwv_ref, dbg_ref, dbv_ref, stage, sems):
        j = pl.program_id(0)
        slot = j % 2

        def copies(step, sl):
            return [pltpu.make_async_copy(stage.at[sl, half],
                                          dup_ref.at[:, pl.ds(pl.multiple_of(half * D_FF + step * LANE, LANE), LANE)],
                                          sems.at[sl, half]) for half in range(2)]

        @pl.when(j >= 2)
        def _():
            for cp in copies(j - 2, slot):
                cp.wait()

        live = _row_ids((tp, LANE)) < t_real
        gm, vm = jnp.where(live, g_ref[...], 0.0), jnp.where(live, v_ref[...], 0.0)
        gc = _conv_fwd_val(gm, wg_ref[...], bg_ref[...], 1)
        vc = _conv_fwd_val(vm, wv_ref[...], bv_ref[...], 1)
        act, dact = _gelu_and_grad(gc)
        dmm = jnp.where(live, dm_ref[...], 0.0)
        stage[slot, 0], dwg_ref[...], dbg_ref[...] = _conv_bwd_val(dmm * vc * dact, gm, wg_ref[...], 1, live)
        stage[slot, 1], dwv_ref[...], dbv_ref[...] = _conv_bwd_val(dmm * act, vm, wv_ref[...], 1, live)
        for cp in copies(j, slot):
            cp.start()

        @pl.when(j == nb - 1)
        def _():
            for cp in copies(j - 1, 1 - slot) + copies(j, slot):
                cp.wait()

    b2 = b.reshape(1, -1)
    col = lambda rows, off: pl.BlockSpec((rows, LANE), functools.partial(lambda j, o: (0, j + o), o=off))
    return pl.pallas_call(
        body, name=name, grid=(nb,),
        in_specs=[col(tp, 0), col(tp, 0), col(tp, nb), col(kk, 0), col(kk, nb), col(1, 0), col(1, nb)],
        out_specs=[pl.BlockSpec(memory_space=pl.ANY), col(kk, 0), col(kk, 0), col(1, 0), col(1, 0)],
        out_shape=[jax.ShapeDtypeStruct((tp, 2 * D_FF), F32), jax.ShapeDtypeStruct((kk, D_FF), F32),
                   jax.ShapeDtypeStruct((kk, D_FF), F32), jax.ShapeDtypeStruct((1, D_FF), F32),
                   jax.ShapeDtypeStruct((1, D_FF), F32)],
        scratch_shapes=[pltpu.VMEM((2, 2, tp, LANE), F32), pltpu.SemaphoreType.DMA((2, 2))],
        compiler_params=_cparams("arbitrary"),
    )(dm, up, up, w, w, b2, b2)


def _lru_gates_fwd(xc, wg, b4, lam, t_real, name):
    tp = xc.shape[0]
    tm = _tile(tp, 1408)

    def body(x_ref, w_ref, b_ref, lam_ref, r0_ref, r1_ref, i0_ref, i1_ref, a0_ref, a1_ref, u0_ref, u1_ref):
        x = x_ref[...]
        xb = x.astype(BF16)
        live = _row_ids(x.shape, pl.program_id(1) * tm) < t_real
        bb = b_ref[...]
        sp = _softplus_neg(lam_ref[...])
        gate = [_sigmoid(jnp.dot(xb, w_ref[k], preferred_element_type=F32) + bb[k:k + 1, :]) for k in range(4)]
        for d, (r_ref, i_ref, a_ref, u_ref) in enumerate(((r0_ref, i0_ref, a0_ref, u0_ref), (r1_ref, i1_ref, a1_ref, u1_ref))):
            r, ig = gate[d], gate[2 + d]
            a = jnp.exp(-LRU_C * r * sp[d:d + 1, :])
            r_ref[...] = r
            i_ref[...] = ig
            a_ref[...] = a
            u_ref[...] = jnp.where(live, jnp.sqrt(1.0 - a * a) * (ig * x), 0.0)

    blk = pl.BlockSpec((tm, LANE), lambda g, i: (i, g))
    return pl.pallas_call(
        body, name=name, grid=(LRU_BLOCKS, tp // tm),
        in_specs=[blk, pl.BlockSpec((None, 4, LANE, LANE), lambda g, i: (g, 0, 0, 0)),
                  pl.BlockSpec((4, LANE), lambda g, i: (0, g)), pl.BlockSpec((2, LANE), lambda g, i: (0, g))],
        out_specs=[blk] * 8,
        out_shape=[jax.ShapeDtypeStruct((tp, D_MODEL), F32)] * 8,
        compiler_params=_cparams("parallel", "parallel"),
    )(xc, wg, b4, lam)


def _lru_gates_bwd(l0, l1, da0, da1, r0, r1, i0, i1, a0, a1, xc, wg, lam, t_real, name):
    tp = xc.shape[0]
    tm = _tile(tp, 1408)

    def body(l0_ref, l1_ref, da0_ref, da1_ref, r0_ref, r1_ref, i0_ref, i1_ref, a0_ref, a1_ref, x_ref, w_ref, lam_ref,
             dx_ref, dw_ref, db_ref, dlam_ref):
        i = pl.program_id(1)
        x = x_ref[...]
        xb = x.astype(BF16)
        live = _row_ids(x.shape, i * tm) < t_real
        lam_ = lam_ref[...]
        sp = _softplus_neg(lam_)
        dsp_dlam = -_sigmoid(-lam_)
        dx = jnp.zeros_like(x)
        dpre = [None] * 4
        dlam_rows = []
        for d, (l_ref, da_ref, r_ref, i_ref, a_ref) in enumerate(((l0_ref, da0_ref, r0_ref, i0_ref, a0_ref),
                                                                  (l1_ref, da1_ref, r1_ref, i1_ref, a1_ref))):
            r, ig, a = r_ref[...], i_ref[...], a_ref[...]
            du = jnp.where(live, l_ref[...], 0.0)
            s = jnp.sqrt(1.0 - a * a)
            dv = du * s
            ds = du * (ig * x)
            dla = jnp.where(live, da_ref[...], 0.0) * a - ds * (a * a) / s
            dla = jnp.where(live, dla, 0.0)
            dr = dla * (-LRU_C) * sp[d:d + 1, :]
            dlam_rows.append(jnp.sum(dla * (-LRU_C) * r, axis=0, keepdims=True) * dsp_dlam[d:d + 1, :])
            dpre[d] = dr * r * (1.0 - r)
            dpre[2 + d] = dv * x * ig * (1.0 - ig)
            dx = dx + dv * ig

        @pl.when(i == 0)
        def _():
            dw_ref[...] = jnp.zeros_like(dw_ref)
            db_ref[...] = jnp.zeros_like(db_ref)
            dlam_ref[...] = jnp.zeros_like(dlam_ref)

        for k in range(4):
            pk = dpre[k].astype(BF16)
            dx = dx + lax.dot_general(pk, w_ref[k], (((1,), (1,)), ((), ())), preferred_element_type=F32)
            dw_ref[k] += lax.dot_general(xb, pk, (((0,), (0,)), ((), ())), preferred_element_type=F32)
        db_ref[...] += jnp.concatenate([jnp.sum(p, axis=0, keepdims=True) for p in dpre], axis=0)
        dlam_ref[...] += jnp.concatenate(dlam_rows, axis=0)
        dx_ref[...] = dx

    blk = pl.BlockSpec((tm, LANE), lambda g, i: (i, g))
    return pl.pallas_call(
        body, name=name, grid=(LRU_BLOCKS, tp // tm),
        in_specs=[blk] * 11 + [pl.BlockSpec((None, 4, LANE, LANE), lambda g, i: (g, 0, 0, 0)),
                               pl.BlockSpec((2, LANE), lambda g, i: (0, g))],
        out_specs=[blk, pl.BlockSpec((None, 4, LANE, LANE), lambda g, i: (g, 0, 0, 0)),
                   pl.BlockSpec((4, LANE), lambda g, i: (0, g)), pl.BlockSpec((2, LANE), lambda g, i: (0, g))],
        out_shape=[jax.ShapeDtypeStruct((tp, D_MODEL), F32), jax.ShapeDtypeStruct((LRU_BLOCKS, 4, LANE, LANE), F32),
                   jax.ShapeDtypeStruct((4, D_MODEL), F32), jax.ShapeDtypeStruct((2, D_MODEL), F32)],
        compiler_params=_cparams("parallel", "arbitrary"),
    )(l0, l1, da0, da1, r0, r1, i0, i1, a0, a1, xc, wg, lam)


SCAN_UNROLL = 4


def _loop_tiles(nt, step, carry):
    assert nt % SCAN_UNROLL == 0

    def trip(tt, c):
        for u in range(SCAN_UNROLL):
            c = step(tt * SCAN_UNROLL + u, c)
        return c

    return lax.fori_loop(0, nt // SCAN_UNROLL, trip, carry)


def _tile_scan(a, u, reverse):
    rows = lax.broadcasted_iota(jnp.int32, a.shape, 0)
    for s in (1, 2, 4):
        if reverse:
            keep = rows < SUBLANE - s
            a_sh, u_sh = pltpu.roll(a, SUBLANE - s, 0), pltpu.roll(u, SUBLANE - s, 0)
        else:
            keep = rows >= s
            a_sh, u_sh = pltpu.roll(a, s, 0), pltpu.roll(u, s, 0)
        u = u + a * jnp.where(keep, u_sh, 0.0)
        a = a * jnp.where(keep, a_sh, 1.0)
    return a, u


def _scan_fwd(a0, u0, a1, u1, name):
    tp, d = a0.shape
    tc = 128
    nt = tp // SUBLANE

    def body(a0_ref, u0_ref, a1_ref, u1_ref, h0_ref, h1_ref):
        def step(t, carry):
            c0, c1 = carry
            f = pl.multiple_of(t * SUBLANE, SUBLANE)
            b = pl.multiple_of((nt - 1 - t) * SUBLANE, SUBLANE)
            pa, pu = _tile_scan(a0_ref[pl.ds(f, SUBLANE), :], u0_ref[pl.ds(f, SUBLANE), :], False)
            h = pu + pa * c0
            h0_ref[pl.ds(f, SUBLANE), :] = h
            c0 = h[SUBLANE - 1:SUBLANE, :]
            pa, pu = _tile_scan(a1_ref[pl.ds(b, SUBLANE), :], u1_ref[pl.ds(b, SUBLANE), :], True)
            h = pu + pa * c1
            h1_ref[pl.ds(b, SUBLANE), :] = h
            c1 = h[0:1, :]
            return c0, c1

        z = jnp.zeros((1, tc), F32)
        _loop_tiles(nt, step, (z, z))

    blk = pl.BlockSpec((tp, tc), lambda j: (0, j))
    return pl.pallas_call(
        body, name=name, grid=(d // tc,), in_specs=[blk] * 4, out_specs=[blk] * 2,
        out_shape=[jax.ShapeDtypeStruct((tp, d), F32)] * 2,
        compiler_params=_cparams("parallel"),
    )(a0, u0, a1, u1)


def _scan_bwd(dh, a0, a1, h0, h1, name):
    tp, d = dh.shape
    tc = 128
    nt = tp // SUBLANE

    def body(dh_ref, a0_ref, a1_ref, h0_ref, h1_ref, l0_ref, l1_ref, da0_ref, da1_ref):
        rows8 = lax.broadcasted_iota(jnp.int32, (SUBLANE, tc), 0)

        def step(t, carry):
            c0, c1 = carry
            b = pl.multiple_of((nt - 1 - t) * SUBLANE, SUBLANE)
            f = pl.multiple_of(t * SUBLANE, SUBLANE)
            a = a0_ref[pl.ds(b, SUBLANE), :]
            a_next = jnp.where(rows8 < SUBLANE - 1, pltpu.roll(a, SUBLANE - 1, 0), 1.0)
            pa, pu = _tile_scan(a_next, dh_ref[pl.ds(b, SUBLANE), :], True)
            lam = pu + pa * c0
            l0_ref[pl.ds(b, SUBLANE), :] = lam
            c0 = a[0:1, :] * lam[0:1, :]
            a = a1_ref[pl.ds(f, SUBLANE), :]
            a_prev = jnp.where(rows8 >= 1, pltpu.roll(a, 1, 0), 1.0)
            pa, pu = _tile_scan(a_prev, dh_ref[pl.ds(f, SUBLANE), :], False)
            lam = pu + pa * c1
            l1_ref[pl.ds(f, SUBLANE), :] = lam
            c1 = a[SUBLANE - 1:SUBLANE, :] * lam[SUBLANE - 1:SUBLANE, :]
            return c0, c1

        z = jnp.zeros((1, tc), F32)
        _loop_tiles(nt, step, (z, z))
        rows = lax.broadcasted_iota(jnp.int32, (tp, tc), 0)
        da0_ref[...] = l0_ref[...] * jnp.where(rows >= 1, pltpu.roll(h0_ref[...], 1, 0), 0.0)
        da1_ref[...] = l1_ref[...] * jnp.where(rows < tp - 1, pltpu.roll(h1_ref[...], tp - 1, 0), 0.0)

    blk = pl.BlockSpec((tp, tc), lambda j: (0, j))
    return pl.pallas_call(
        body, name=name, grid=(d // tc,), in_specs=[blk] * 5, out_specs=[blk] * 4,
        out_shape=[jax.ShapeDtypeStruct((tp, d), F32)] * 4,
        compiler_params=_cparams("parallel"),
    )(dh, a0, a1, h0, h1)


def _gated_h(proj, h0, h1, name):
    def fn(row0, lg, x0, x1):
        return _gelu(lg) * (x0 + x1)

    return _rows(fn, name, [Rw(proj, D_MODEL, C_LRU_G // D_MODEL), Rw(h0), Rw(h1)], [(D_MODEL, F32)])[0]


def _gated_h_bwd(dgh, proj, h0, h1, dproj, name):
    def fn(row0, dg, lg, x0, x1):
        act, dact = _gelu_and_grad(lg)
        return dg * (x0 + x1) * dact, dg * act

    return _rows(fn, name, [Rw(dgh), Rw(proj, D_MODEL, C_LRU_G // D_MODEL), Rw(h0), Rw(h1)],
                 [Into(dproj, C_LRU_G, D_MODEL), (D_MODEL, F32)])


def _mix(proj, y_mla, y_lru, name):
    def fn(row0, gm, gl, ym, yl):
        return _sigmoid(gm) * ym + _sigmoid(gl) * yl

    return _rows(fn, name, [Rw(proj, D_MODEL, C_G_MLA // D_MODEL), Rw(proj, D_MODEL, C_G_LRU // D_MODEL), Rw(y_mla), Rw(y_lru)],
                 [(D_MODEL, F32)])[0]


def _mix_bwd(dz, proj, y_mla, y_lru, dproj, name):
    def fn(row0, dzb, gm, gl, ym, yl):
        sm, sl = _sigmoid(gm), _sigmoid(gl)
        dg = jnp.concatenate([dzb * ym * sm * (1.0 - sm), dzb * yl * sl * (1.0 - sl)], axis=1)
        return dzb * sm, dzb * sl, dg

    return _rows(fn, name, [Rw(dz), Rw(proj, D_MODEL, C_G_MLA // D_MODEL), Rw(proj, D_MODEL, C_G_LRU // D_MODEL),
                            Rw(y_mla), Rw(y_lru)], [(D_MODEL, F32), (D_MODEL, F32), Into(dproj, C_G_MLA, 2 * D_MODEL)])


def _layer_fwd(h, w_in, more_weights, cs, t_real, tag):
    proj = _matmul(h, w_in, tag + "proj")
    w = dict(more_weights(0, proj), w_in=w_in)
    cqn, ckvn = _mla_norms(proj, w['q_norm'], w['kv_norm'], tag + "mla_norms")
    qext = _matmul(cqn, w['w_q'], tag + "q_up")
    kv = _matmul(ckvn, w['w_kv'], tag + "kv_up")
    qc, kc, vb = _mla_pack(qext, kv, proj, cs, tag + "mla_pack")
    o, lse = _attn_fwd(qc, kc, vb, t_real, tag + "attn_fwd")
    w.update(more_weights(1, o))
    y_mla = _matmul(o, w['w_o_mla'], tag + "o_mla")
    xc = _lru_conv_fwd(proj, w['lru_conv_w'], w['lru_conv_b'], t_real, tag + "lru_conv")
    r0, r1, i0, i1, a0, a1, u0, u1 = _lru_gates_fwd(xc, w['w_g'], w['b4'], w['lru_lambda'], t_real, tag + "lru_gates")
    h0, h1 = _scan_fwd(a0, u0, a1, u1, tag + "lru_scan")
    gh = _gated_h(proj, h0, h1, tag + "lru_gate_out")
    y_lru = _matmul(gh, w['w_o_lru'], tag + "o_lru")
    z = _mix(proj, y_mla, y_lru, tag + "mix")
    zo = _matmul(z, w['w_out'], tag + "w_out")
    hm = _ln_fwd([(DN_ALPHA, h), (1.0, zo)], w['ln1_g'], w['ln1_b'], tag + "ln1")
    w.update(more_weights(2, hm))
    up = _matmul(hm, w['w_up'], tag + "w_up")
    m = _ffn_conv_act(up, w['ffn_conv_w'], w['ffn_conv_b'], t_real, tag + "ffn_conv")
    f = _matmul(m, w['w_down'], tag + "w_down", tk_cap=1408)
    out = _ln_fwd([(DN_ALPHA, hm), (1.0, f)], w['ln2_g'], w['ln2_b'], tag + "ln2")
    saved = dict(w=w, h=h, proj=proj, cqn=cqn, ckvn=ckvn, qc=qc, kc=kc, vb=vb, o=o, lse=lse, y_mla=y_mla, xc=xc,
                 r0=r0, r1=r1, i0=i0, i1=i1, a0=a0, a1=a1, h0=h0, h1=h1, gh=gh, y_lru=y_lru, z=z, zo=zo, hm=hm,
                 up=up, m=m, f=f)
    return out, saved


DW_MATMUL = dict(ta=True, out_dtype=BF16, tn_cap=1408, tk_cap=1408)


def _after(a, tok):
    return a if tok is None else a + tok.astype(a.dtype)


def _layer_bwd(dout_terms, s, cs, t_real, tag, emit, tok):
    w = s['w']
    g = {}
    du2, dg2, db2 = _ln_bwd(dout_terms, [(DN_ALPHA, s['hm']), (1.0, s['f'])], _after(w['ln2_g'], tok), tag + "ln2_bwd")
    g['ln2_g'], g['ln2_b'] = dg2, db2
    dm = _matmul(du2, w['w_down'], tag + "w_down_dx", tb=True)
    g['w_down'] = _matmul(s['m'], du2, tag + "w_down_dw", **DW_MATMUL)
    dup, dwg_, dwv_, dbg_, dbv_ = _ffn_conv_act_bwd(dm, s['up'], w['ffn_conv_w'], w['ffn_conv_b'], t_real, tag + "ffn_conv_bwd")
    g['ffn_conv_w'] = jnp.concatenate([dwg_, dwv_], axis=1)
    g['ffn_conv_b'] = jnp.concatenate([dbg_, dbv_], axis=1)
    dhm_mm = _matmul(dup, w['w_up'], tag + "w_up_dx", tb=True, tk_cap=1408)
    g['w_up'] = _matmul(s['hm'], dup, tag + "w_up_dw", **DW_MATMUL)
    tok = emit('ffn', g)
    g = {}
    du1, dg1, db1 = _ln_bwd([(DN_ALPHA, du2), (1.0, dhm_mm)], [(DN_ALPHA, s['h']), (1.0, s['zo'])],
                            _after(w['ln1_g'], tok), tag + "ln1_bwd")
    g['ln1_g'], g['ln1_b'] = dg1, db1
    dz = _matmul(du1, w['w_out'], tag + "w_out_dx", tb=True)
    g['w_out'] = _matmul(s['z'], du1, tag + "w_out_dw", **DW_MATMUL)
    dproj = lax.empty(s['proj'].shape, F32)
    dy_mla, dy_lru, dproj = _mix_bwd(dz, s['proj'], s['y_mla'], s['y_lru'], dproj, tag + "mix_bwd")
    do = _matmul(dy_mla, w['w_o_mla'], tag + "o_mla_dx", tb=True)
    g['w_o_mla'] = _matmul(s['o'], dy_mla, tag + "o_mla_dw", **DW_MATMUL)
    dqc, dkc, dv = _attn_bwd(s['qc'], s['kc'], s['vb'], do, s['o'], s['lse'], t_real, tag + "attn_bwd")
    dqext, dkv, dkrp = _mla_unpack(dqc, dkc, dv, cs, tag + "mla_unpack")
    dcqn = _matmul(dqext, w['w_q'], tag + "q_up_dx", tb=True)
    g['w_q'] = _matmul(s['cqn'], dqext, tag + "q_up_dw", **DW_MATMUL)
    dckvn = _matmul(dkv, w['w_kv'], tag + "kv_up_dx", tb=True)
    g['w_kv'] = _matmul(s['ckvn'], dkv, tag + "kv_up_dw", **DW_MATMUL)
    dgh = _matmul(dy_lru, w['w_o_lru'], tag + "o_lru_dx", tb=True)
    g['w_o_lru'] = _matmul(s['gh'], dy_lru, tag + "o_lru_dw", **DW_MATMUL)
    dproj, dhs = _gated_h_bwd(dgh, s['proj'], s['h0'], s['h1'], dproj, tag + "lru_gate_out_bwd")
    l0, l1, da0, da1 = _scan_bwd(dhs, s['a0'], s['a1'], s['h0'], s['h1'], tag + "lru_scan_bwd")
    dxc, g['w_g'], g['b4'], g['lru_lambda'] = _lru_gates_bwd(
        l0, l1, da0, da1, s['r0'], s['r1'], s['i0'], s['i1'], s['a0'], s['a1'], s['xc'], w['w_g'], w['lru_lambda'],
        t_real, tag + "lru_gates_bwd")
    dproj, g['lru_conv_w'], g['lru_conv_b'] = _lru_conv_bwd(dxc, s['proj'], w['lru_conv_w'], dproj, t_real, tag + "lru_conv_bwd")
    tok = emit('mid', g)
    dproj, dqn, dkvn = _mla_norms_bwd(dcqn, dckvn, _after(dkrp, tok), s['proj'], w['q_norm'], w['kv_norm'], dproj,
                                      tag + "mla_norms_bwd")
    tok = emit('in', {'w_in': _matmul(s['h'], dproj, tag + "proj_dw", **DW_MATMUL), 'q_norm': dqn, 'kv_norm': dkvn})
    dh_mm = _matmul(dproj, w['w_in'], tag + "proj_dx", tb=True, tk_cap=1536)
    return [(DN_ALPHA, du1), (1.0, dh_mm)], tok


def _swap_halves(a, axis=-1):
    h1, h2 = jnp.split(a, 2, axis=axis)
    return jnp.concatenate([h2, h1], axis=axis)


def _w_in_kernel(w_in):
    cq, ckv, kr, lg, lx, gm, gl = jnp.split(w_in, [256, 384, 448, 1472, 2496, 3520], axis=1)
    return jnp.concatenate([lg, lx, gm, gl, cq, ckv, kr, _swap_halves(kr)], axis=1)


def _layer_weights(fl):
    w = {}
    if 'w_uq' in fl:
        uq = fl['w_uq']
        w['w_q'] = jnp.concatenate([uq, _swap_halves(uq[..., QK_NOPE:])], axis=-1).reshape(Q_RANK, HEADS * 2 * LANE)
        w['w_kv'] = jnp.concatenate([fl['w_uk'].reshape(KV_RANK, -1), fl['w_uv'].reshape(KV_RANK, -1)], axis=1).astype(BF16)
        w['w_g'] = jnp.moveaxis(jnp.concatenate([fl['w_rg'], fl['w_ig']], axis=0), 0, 1).astype(BF16)
        w['b4'] = jnp.concatenate([fl['b_rg'], fl['b_ig']], axis=0)
    for n in ('q_norm', 'kv_norm', 'w_o_mla', 'lru_conv_w', 'lru_conv_b', 'lru_lambda', 'w_o_lru', 'w_out', 'ln1_g',
              'ln1_b', 'w_up', 'ffn_conv_w', 'ffn_conv_b', 'w_down', 'ln2_g', 'ln2_b'):
        if n in fl:
            w[n] = fl[n]
    return w


def _layer_grads(g):
    out = {}
    if 'w_in' in g:
        lg, lx, gm, gl, cq, ckv, kr, krs = jnp.split(g['w_in'], [1024, 2048, 3072, 4096, 4352, 4480, 4544], axis=1)
        out['w_in'] = jnp.concatenate([cq, ckv, kr + _swap_halves(krs), lg, lx, gm, gl], axis=1)
    if 'w_q' in g:
        gq = g['w_q'].reshape(Q_RANK, HEADS, 2 * LANE)
        out['w_uq'] = jnp.concatenate([gq[..., :QK_NOPE], gq[..., QK_NOPE:QK_NOPE + QK_ROPE] + _swap_halves(gq[..., QK_NOPE + QK_ROPE:])], axis=-1)
    if 'w_kv' in g:
        out['w_uk'] = g['w_kv'][:, :HEADS * QK_NOPE].reshape(KV_RANK, HEADS, QK_NOPE)
        out['w_uv'] = g['w_kv'][:, HEADS * QK_NOPE:].reshape(KV_RANK, HEADS, V_HEAD)
    if 'w_g' in g:
        gg = jnp.moveaxis(g['w_g'], 1, 0)
        out['w_rg'], out['w_ig'] = gg[:2], gg[2:]
    if 'b4' in g:
        out['b_rg'], out['b_ig'] = g['b4'][:2], g['b4'][2:]
    for n in ('q_norm', 'kv_norm', 'lru_conv_b', 'ln1_g', 'ln1_b', 'ffn_conv_b', 'ln2_g', 'ln2_b'):
        if n in g:
            out[n] = g[n].reshape(-1)
    for n in ('w_o_mla', 'lru_conv_w', 'lru_lambda', 'w_o_lru', 'w_out', 'w_up', 'ffn_conv_w', 'w_down'):
        if n in g:
            out[n] = g[n]
    return out


def _rope_table(tp):
    half = QK_ROPE // 2
    inv_freq = jnp.exp(-math.log(ROPE_THETA) * jnp.arange(half, dtype=F32) / half)
    ang = jnp.arange(tp, dtype=F32)[:, None] * inv_freq[None, :]
    c, s = jnp.cos(ang), jnp.sin(ang)
    return jnp.concatenate([c, c, -s, s], axis=1)


def _local_step(x, target, meta, ln0_g, ln0_b, layer_w, t_pad, emit):
    seq = x.shape[0]
    t_real = N_META + seq
    zpad = jnp.zeros((t_pad - t_real, D_MODEL), F32)
    xin = jnp.concatenate([meta, x, zpad], axis=0)
    tgt = jnp.concatenate([jnp.zeros((N_META, D_MODEL), F32), target, zpad], axis=0)
    cs = _rope_table(t_pad)
    h = _ln_fwd([(1.0, xin)], ln0_g, ln0_b, "ln0")
    saved = []
    for l in range(DEPTH):
        w_in, rest_of_weights = layer_w[l](h)
        h, s = _layer_fwd(h, w_in, rest_of_weights, cs, t_real, "l%d_" % l)
        saved.append(s)
    dy, lossvec = _loss_head(h, tgt, t_real, "loss_head")
    terms, tok = [(1.0, dy)], None
    for l in reversed(range(DEPTH)):
        terms, tok = _layer_bwd(terms, saved[l], cs, t_real, "l%d_" % l,
                                functools.partial(lambda stage, g, l: emit(l, stage, _layer_grads(g)), l=l), tok)
    dxin, dg0, db0 = _ln_bwd(terms, [(1.0, xin)], _after(ln0_g, tok), "ln0_bwd")
    emit(None, 'head', {'meta_tokens': dxin[:N_META], 'ln0_g': dg0.reshape(-1), 'ln0_b': db0.reshape(-1), 'loss': lossvec})
    return dxin[N_META:t_real]


_HBM = pl.BlockSpec(memory_space=pltpu.HBM)
_SEM = pl.BlockSpec(memory_space=pltpu.SEMAPHORE)
_SIDE_EFFECT = pltpu.SideEffectType.DATAFLOW_SIDE_EFFECTING


def _peer_copies(src_refs, land_refs, scatters, send_sems, recv_sems):
    x, y, c = lax.axis_index("x"), lax.axis_index("y"), lax.axis_index("c")
    me = 4 * x + 2 * y + c
    copies = []
    for k in range(1, N_DEV):
        px = 1 - x if k & 4 else x
        py = 1 - y if k & 2 else y
        pc = 1 - c if k & 1 else c
        for t, (src, land) in enumerate(zip(src_refs, land_refs)):
            copies.append(pltpu.make_async_remote_copy(
                src_ref=src.at[4 * px + 2 * py + pc] if scatters[t] else src, dst_ref=land.at[me],
                send_sem=send_sems.at[7 * t + k - 1], recv_sem=recv_sems.at[7 * t + k - 1],
                device_id=(px, py, pc), device_id_type=pl.DeviceIdType.MESH))
    return me, copies


def _exchange_start(groups, name):
    flat = [it for grp in groups for it in grp]
    nt, ng = len(flat), len(groups)
    scatters = [sc for _, sc in flat]
    srcs = [pltpu.with_memory_space_constraint(a, pltpu.HBM) for a, _ in flat]
    land_shapes = [a.shape if sc else (N_DEV,) + a.shape for a, sc in flat]
    lands = [pltpu.with_memory_space_constraint(lax.empty(s, a.dtype), pltpu.HBM) for s, (a, _) in zip(land_shapes, flat)]
    bounds = [0]
    for grp in groups:
        bounds.append(bounds[-1] + len(grp))

    def body(*refs):
        src_refs, land_refs = refs[:nt], refs[nt:2 * nt]
        sem_refs = refs[2 * nt:2 * nt + 2 * ng]
        token_ref = refs[4 * nt + 2 * ng]
        for gi in range(ng):
            lo, hi = bounds[gi], bounds[gi + 1]
            _, copies = _peer_copies(src_refs[lo:hi], land_refs[lo:hi], scatters[lo:hi], sem_refs[2 * gi], sem_refs[2 * gi + 1])
            for cp in copies:
                cp.start()
        token_ref[...] = jnp.zeros_like(token_ref)

    out_shape = []
    for grp in groups:
        out_shape += [pltpu.SemaphoreType.DMA((7 * len(grp),)), pltpu.SemaphoreType.DMA((7 * len(grp),))]
    out_shape += [pltpu.HBM(a.shape, a.dtype) for a in srcs] + [pltpu.HBM(s, a.dtype) for s, a in zip(land_shapes, srcs)]
    out_shape += [jax.ShapeDtypeStruct((SUBLANE, LANE), F32)]
    res = pl.pallas_call(
        body, name=name, out_shape=out_shape,
        in_specs=[_HBM] * (2 * nt),
        out_specs=[_SEM] * (2 * ng) + [_HBM] * (2 * nt) + [pl.BlockSpec(memory_space=pltpu.VMEM)],
        input_output_aliases={t: 2 * ng + t for t in range(2 * nt)},
        compiler_params=pltpu.CompilerParams(has_side_effects=_SIDE_EFFECT),
    )(*srcs, *lands)
    sems, thru, token = res[:2 * ng], res[2 * ng:2 * ng + 2 * nt], res[-1]
    states = []
    for gi in range(ng):
        lo, hi = bounds[gi], bounds[gi + 1]
        states.append((sems[2 * gi], sems[2 * gi + 1], thru[lo:hi], thru[nt + lo:nt + hi], scatters[lo:hi]))
    return states, token[0, 0]


def _exchange_wait(state, after, name):
    send_sems, recv_sems, srcs, lands, scatters = state
    n = len(srcs)

    def body(*refs):
        _, copies = _peer_copies(refs[:n], refs[n:2 * n], scatters, refs[2 * n], refs[2 * n + 1])
        for cp in copies:
            cp.wait_send()
        for cp in copies:
            cp.wait_recv()

    res = pl.pallas_call(
        body, name=name,
        out_shape=[pltpu.HBM(a.shape, a.dtype) for a in srcs] + [pltpu.HBM(a.shape, a.dtype) for a in lands],
        in_specs=[_HBM] * (2 * n) + [_SEM, _SEM, _HBM],
        out_specs=[_HBM] * (2 * n),
        input_output_aliases={t: t for t in range(2 * n)},
        compiler_params=pltpu.CompilerParams(has_side_effects=_SIDE_EFFECT),
    )(*srcs, *lands, send_sems, recv_sems, pltpu.with_memory_space_constraint(after, pltpu.HBM))
    me = 4 * lax.axis_index("x") + 2 * lax.axis_index("y") + lax.axis_index("c")
    out = []
    for src, land, sc in zip(res[:n], res[n:], scatters):
        own = lax.dynamic_index_in_dim(src, me, 0, keepdims=True) if sc else src[None]
        out.append(lax.dynamic_update_slice_in_dim(land, own, me, 0))
    return out


def _as_rows(shape):
    return (1, shape[0]) if len(shape) == 1 else (math.prod(shape[:-1]), shape[-1])


def _sum_adamw(pieces, w, m, v, name):
    shape = w.shape
    nl = len(pieces)
    if nl > 1 and _as_rows(shape[1:])[0] % 16:
        pieces, nl = [jnp.stack(pieces, axis=1)], 1
    rows, cols = _as_rows(shape)
    rl = rows // nl
    cap = max(16, (1 << 18) // cols // 16 * 16)
    tr = _tile(rl, cap, 16)
    nb = rl // tr
    c1 = 1.0 / (1.0 - ADAM_B1 ** ADAM_STEP)
    c2 = 1.0 / (1.0 - ADAM_B2 ** ADAM_STEP)

    def body(*refs):
        p_refs = refs[:nl]
        w_ref, m_ref, v_ref, g_ref, d_ref, nm_ref, nv_ref = refs[nl:]
        li = pl.program_id(0)

        def total(p_ref):
            acc = p_ref[0].astype(F32)
            for k in range(1, N_DEV):
                acc = acc + p_ref[k].astype(F32)
            return acc

        gg = total(p_refs[0])
        for l in range(1, nl):
            gg = jnp.where(li == l, total(p_refs[l]), gg)
        nm = ADAM_B1 * m_ref[...] + (1.0 - ADAM_B1) * gg
        nv = ADAM_B2 * v_ref[...] + (1.0 - ADAM_B2) * (gg * gg)
        g_ref[...] = gg
        d_ref[...] = -ADAM_LR * ((nm * c1) / (jnp.sqrt(nv * c2) + ADAM_EPS) + ADAM_WD * w_ref[...])
        nm_ref[...] = nm
        nv_ref[...] = nv

    blk = pl.BlockSpec((tr, cols), lambda li, i: (li * nb + i, 0))
    p_specs = [pl.BlockSpec((N_DEV, tr, cols), functools.partial(lambda li, i, l: (0, jnp.where(li == l, i, 0), 0), l=l))
               for l in range(nl)]
    res = pl.pallas_call(
        body, name=name, grid=(nl, nb),
        in_specs=p_specs + [blk] * 3, out_specs=[blk] * 4,
        out_shape=[jax.ShapeDtypeStruct((rows, cols), F32)] * 4,
        compiler_params=_cparams("parallel", "parallel"),
    )(*[p.reshape(N_DEV, rl, cols) for p in pieces], *[a.reshape(rows, cols) for a in (w, m, v)])
    return [r.reshape(shape) for r in res]


def _to_shards(full, axis):
    shp = full.shape
    a = full.reshape(shp[:axis] + (N_DEV, shp[axis] // N_DEV) + shp[axis + 1:])
    return jnp.moveaxis(a, axis, 0)


def _from_shards(blocks, axis):
    a = jnp.moveaxis(blocks, 0, axis)
    shp = a.shape
    return a.reshape(shp[:axis] + (shp[axis] * shp[axis + 1],) + shp[axis + 2:])


def kernel(x, meta_tokens, ln0_g, ln0_b, w_in, q_norm, kv_norm, w_uq, w_uk, w_uv, w_o_mla, lru_conv_w, lru_conv_b, w_rg, b_rg, w_ig, b_ig, lru_lambda, w_o_lru, w_out, ln1_g, ln1_b, w_up, ffn_conv_w, ffn_conv_b, w_down, ln2_g, ln2_b, loss_target, m_meta_tokens, m_ln0_g, m_ln0_b, m_w_in, m_q_norm, m_kv_norm, m_w_uq, m_w_uk, m_w_uv, m_w_o_mla, m_lru_conv_w, m_lru_conv_b, m_w_rg, m_b_rg, m_w_ig, m_b_ig, m_lru_lambda, m_w_o_lru, m_w_out, m_ln1_g, m_ln1_b, m_w_up, m_ffn_conv_w, m_ffn_conv_b, m_w_down, m_ln2_g, m_ln2_b, v_meta_tokens, v_ln0_g, v_ln0_b, v_w_in, v_q_norm, v_kv_norm, v_w_uq, v_w_uk, v_w_uv, v_w_o_mla, v_lru_conv_w, v_lru_conv_b, v_w_rg, v_b_rg, v_w_ig, v_b_ig, v_lru_lambda, v_w_o_lru, v_w_out, v_ln1_g, v_ln1_b, v_w_up, v_ffn_conv_w, v_ffn_conv_b, v_w_down, v_ln2_g, v_ln2_b):
    args = (meta_tokens, ln0_g, ln0_b, w_in, q_norm, kv_norm, w_uq, w_uk, w_uv, w_o_mla, lru_conv_w, lru_conv_b, w_rg, b_rg, w_ig, b_ig, lru_lambda, w_o_lru, w_out, ln1_g, ln1_b, w_up, ffn_conv_w, ffn_conv_b, w_down, ln2_g, ln2_b)
    ms = (m_meta_tokens, m_ln0_g, m_ln0_b, m_w_in, m_q_norm, m_kv_norm, m_w_uq, m_w_uk, m_w_uv, m_w_o_mla, m_lru_conv_w, m_lru_conv_b, m_w_rg, m_b_rg, m_w_ig, m_b_ig, m_lru_lambda, m_w_o_lru, m_w_out, m_ln1_g, m_ln1_b, m_w_up, m_ffn_conv_w, m_ffn_conv_b, m_w_down, m_ln2_g, m_ln2_b)
    vs = (v_meta_tokens, v_ln0_g, v_ln0_b, v_w_in, v_q_norm, v_kv_norm, v_w_uq, v_w_uk, v_w_uv, v_w_o_mla, v_lru_conv_w, v_lru_conv_b, v_w_rg, v_b_rg, v_w_ig, v_b_ig, v_lru_lambda, v_w_o_lru, v_w_out, v_ln1_g, v_ln1_b, v_w_up, v_ffn_conv_w, v_ffn_conv_b, v_w_down, v_ln2_g, v_ln2_b)
    wd, md, vd = dict(zip(WEIGHTS, args)), dict(zip(WEIGHTS, ms)), dict(zip(WEIGHTS, vs))

    def shard_axis(n, l):
        return SHARD_AXIS[n] - (0 if l is None else 1)

    def shard(n, l):
        a = wd[n] if l is None else wd[n][l]
        return a.astype(BF16) if n in BIG else a

    first = [('meta_tokens', None), ('w_in', 0)]
    staged = [[(n, 0) for n in names] for names in STAGE_WEIGHTS]
    later = [(n, 1) for n in SHARDED if n != 'meta_tokens']
    gather, token = _exchange_start([[(shard(*k), False) for k in keys] for keys in [first] + staged + [later]], "gather_start")

    def arrive(gi, keys, after, name):
        return {k: _from_shards(b, shard_axis(*k)) for k, b in zip(keys, _exchange_wait(gather[gi], after, name))}

    def layer_weights(got, l, names):
        fl = {n: wd[n][l] for n in names if n in REPLICATED}
        fl.update({n: a for (n, _), a in got.items() if n in names})
        return _layer_weights(fl)

    ln0_g = _after(wd['ln0_g'], token)
    got_first = arrive(0, first, ln0_g, "gather_wait_first")

    def first_layer(h):
        def more(stage, after):
            got = arrive(1 + stage, staged[stage], after, "gather_wait_l0_%d" % stage)
            return layer_weights(got, 0, STAGE_WEIGHTS[stage] + STAGE_REPLICATED[stage])
        return _w_in_kernel(got_first['w_in', 0]), more

    def second_layer(h):
        got = arrive(1 + len(staged), later, h, "gather_wait_l1")
        return _w_in_kernel(got['w_in', 1]), lambda stage, after: layer_weights(got, 1, STAGE_WEIGHTS[stage] + STAGE_REPLICATED[stage])

    sent = []
    pending = []

    def send(l, stage, grads):
        for n, g in grads.items():
            if n in SHARD_AXIS:
                g = _to_shards(g, shard_axis(n, l))
                pending.append(((n, l), (g.astype(BF16) if n in BIG else g, True)))
            else:
                pending.append(((n, l), (g.astype(BF16) if n in LARGE_REPLICATED else g, False)))
        if l == DEPTH - 1 and stage != 'in':
            return None
        (state,), tok = _exchange_start([[it for _, it in pending]], "grads_start_%s_%s" % (l, stage))
        sent.append(([k for k, _ in pending], state))
        pending.clear()
        return tok

    seq = x.shape[1]
    t_pad = -(-(N_META + seq + MIN_PAD_ROWS) // LANE) * LANE
    grad_x = _local_step(x[0], loss_target[0], got_first['meta_tokens', None], ln0_g, wd['ln0_b'],
                         [first_layer, second_layer], t_pad, send)

    pieces = {}
    for gi, (keys, state) in enumerate(sent):
        pieces.update(zip(keys, _exchange_wait(state, grad_x, "grads_wait_%d" % gi)))
    loss = jnp.sum(pieces['loss', None])
    outs = {}
    for n in WEIGHTS:
        ps = [pieces[n, None]] if (n, None) in pieces else [pieces[n, l] for l in range(DEPTH)]
        outs[n] = _sum_adamw(ps, wd[n], md[n], vd[n], "adamw_" + n)
    res = [loss, grad_x[None]]
    for k in range(4):
        res += [outs[n][k] for n in WEIGHTS]
    return tuple(res)
```

```python
import functools
import math

import jax
import jax.numpy as jnp
from jax import lax
from jax.experimental import pallas as pl
from jax.experimental.pallas import tpu as pltpu

F32 = jnp.float32
BF16 = jnp.bfloat16

N_DEV = 8
D_MODEL = 1024
N_META = 16
HEADS = 8
QK_NOPE = 128
QK_ROPE = 64
V_HEAD = 128
Q_RANK = 256
KV_RANK = 128
ROPE_THETA = 10000.0
LRU_BLOCKS = 8
LRU_C = 8.0
D_FF = 2816
DEPTH = 2
DN_ALPHA = (2.0 * DEPTH) ** 0.25
LN_EPS = 1e-5
RMS_EPS = 1e-6
LN2 = math.log(2.0)
ATT_SCALE = 1.0 / math.sqrt(QK_NOPE + QK_ROPE) / LN2
NEG_BIG = -1e30

ADAM_LR = 0.001
ADAM_B1 = 0.9
ADAM_B2 = 0.999
ADAM_EPS = 1e-08
ADAM_WD = 0.01
ADAM_STEP = 10

MIN_PAD_ROWS = 2
LANE = 128
SUBLANE = 8
VMEM_LIMIT = 56 * 1024 * 1024

PROJ_COLS = 4 * D_MODEL + Q_RANK + KV_RANK + 2 * QK_ROPE
C_LRU_G, C_LRU_X, C_G_MLA, C_G_LRU = 0, D_MODEL, 2 * D_MODEL, 3 * D_MODEL
C_CQ = 4 * D_MODEL
C_CKV = C_CQ + Q_RANK
C_KRP = C_CKV + KV_RANK

WEIGHTS = ['meta_tokens', 'ln0_g', 'ln0_b', 'w_in', 'q_norm', 'kv_norm', 'w_uq', 'w_uk', 'w_uv', 'w_o_mla',
           'lru_conv_w', 'lru_conv_b', 'w_rg', 'b_rg', 'w_ig', 'b_ig', 'lru_lambda', 'w_o_lru', 'w_out',
           'ln1_g', 'ln1_b', 'w_up', 'ffn_conv_w', 'ffn_conv_b', 'w_down', 'ln2_g', 'ln2_b']
SHARD_AXIS = {'meta_tokens': 1, 'w_in': 2, 'w_uq': 1, 'w_o_mla': 1, 'lru_conv_w': 2, 'b_rg': 2, 'b_ig': 2,
              'lru_lambda': 2, 'w_o_lru': 1, 'w_out': 1, 'w_up': 2, 'ffn_conv_w': 2, 'w_down': 1}
BIG = ['w_in', 'w_uq', 'w_o_mla', 'w_o_lru', 'w_out', 'w_up', 'w_down']
SHARDED = [n for n in WEIGHTS if n in SHARD_AXIS]
REPLICATED = [n for n in WEIGHTS if n not in SHARD_AXIS]
LARGE_REPLICATED = ['w_uk', 'w_uv', 'w_rg', 'w_ig']
SENT_TRANSPOSED = ['w_in', 'w_up']
STAGE_WEIGHTS = [['w_uq', 'lru_conv_w', 'b_rg', 'b_ig', 'lru_lambda'], ['w_o_mla', 'w_o_lru', 'w_out'], ['w_up', 'ffn_conv_w', 'w_down']]
STAGE_REPLICATED = [['q_norm', 'kv_norm', 'w_uk', 'w_uv', 'lru_conv_b', 'w_rg', 'w_ig'], ['ln1_g', 'ln1_b'], ['ffn_conv_b', 'ln2_g', 'ln2_b']]


def _cparams(*sem):
    return pltpu.CompilerParams(dimension_semantics=sem, vmem_limit_bytes=VMEM_LIMIT)


def _tile(n, cap, unit=LANE):
    best = None
    t = unit
    while t <= min(n, cap):
        if n % t == 0:
            best = t
        t += unit
    return n if best is None else best


def _sigmoid(x):
    return 1.0 / (1.0 + jnp.exp(-x))


_GELU_C = math.sqrt(2.0 / math.pi)


_GELU_A = 0.044715


def _gelu(x):
    t = jnp.tanh(x * (_GELU_C + (_GELU_C * _GELU_A) * (x * x)))
    hx = 0.5 * x
    return hx + hx * t


def _gelu_and_grad(x):
    x2 = x * x
    t = jnp.tanh(x * (_GELU_C + (_GELU_C * _GELU_A) * x2))
    hx = 0.5 * x
    dg = 0.5 + 0.5 * t + (hx * (1.0 - t * t)) * (_GELU_C + (3.0 * _GELU_C * _GELU_A) * x2)
    return hx + hx * t, dg


def _softplus_neg(lam):
    z = jnp.exp(-jnp.abs(lam))
    w = 1.0 + z
    log1p = jnp.where(w == 1.0, z, jnp.log(w) * z / (w - 1.0))
    return jnp.maximum(-lam, 0.0) + log1p


def _row_ids(shape, row0=0):
    return lax.broadcasted_iota(jnp.int32, shape, 0) + row0


def _matmul(a, b, name, ta=False, tb=False, out_dtype=F32, tm_cap=1408, tn_cap=1024, tk_cap=2048):
    if ta:
        kdim, m = a.shape
    else:
        m, kdim = a.shape
    if tb:
        n, k2 = b.shape
    else:
        k2, n = b.shape
    assert kdim == k2, (a.shape, b.shape, ta, tb)
    tm, tn, tk = _tile(m, tm_cap), _tile(n, tn_cap), _tile(kdim, tk_cap)
    nk = kdim // tk

    def body(a_ref, b_ref, o_ref, *acc):
        dn = (((0 if ta else 1,), (1 if tb else 0,)), ((), ()))
        part = lax.dot_general(a_ref[...].astype(BF16), b_ref[...].astype(BF16), dn, preferred_element_type=F32)
        if nk == 1:
            o_ref[...] = part.astype(o_ref.dtype)
            return
        acc_ref, k = acc[0], pl.program_id(2)

        @pl.when(k == 0)
        def _():
            acc_ref[...] = part

        @pl.when(k > 0)
        def _():
            acc_ref[...] += part

        @pl.when(k == nk - 1)
        def _():
            o_ref[...] = acc_ref[...].astype(o_ref.dtype)

    a_spec = pl.BlockSpec((tk, tm), lambda i, j, k: (k, i)) if ta else pl.BlockSpec((tm, tk), lambda i, j, k: (i, k))
    b_spec = pl.BlockSpec((tn, tk), lambda i, j, k: (j, k)) if tb else pl.BlockSpec((tk, tn), lambda i, j, k: (k, j))
    return pl.pallas_call(
        body, name=name,
        grid=(m // tm, n // tn, nk),
        in_specs=[a_spec, b_spec],
        out_specs=pl.BlockSpec((tm, tn), lambda i, j, k: (i, j)),
        out_shape=jax.ShapeDtypeStruct((m, n), out_dtype),
        scratch_shapes=[pltpu.VMEM((tm, tn), F32)] if nk > 1 else [],
        compiler_params=_cparams("parallel", "parallel", "arbitrary"),
    )(a, b)


class Rw:
    def __init__(self, arr, width=None, cb=0):
        self.arr, self.width, self.cb = arr, (arr.shape[1] if width is None else width), cb


class Pm:
    def __init__(self, arr):
        self.arr = arr


class Into:
    def __init__(self, arr, col0, width):
        self.arr, self.col0, self.width = arr, col0, width


def _call_with_into(body, name, grid, in_specs, operands, outs, spec_of, shape_of, extra_out_specs, extra_out_shape, sem):
    intos = [(k, o) for k, o in enumerate(outs) if isinstance(o, Into)]
    aliases = {len(operands) + n: k for n, (k, _) in enumerate(intos)}
    return pl.pallas_call(
        body, name=name, grid=grid,
        in_specs=in_specs + [pl.BlockSpec(memory_space=pl.ANY)] * len(intos),
        out_specs=[spec_of(o) for o in outs] + extra_out_specs,
        out_shape=[jax.ShapeDtypeStruct(o.arr.shape, o.arr.dtype) if isinstance(o, Into) else shape_of(o) for o in outs]
        + extra_out_shape,
        input_output_aliases=aliases,
        compiler_params=_cparams(sem),
    )(*operands, *[o.arr for _, o in intos])


def _rows(fn, name, ins, outs, accs=(), tm_cap=384):
    tp = next(o.arr.shape[0] for o in ins if isinstance(o, Rw))
    tm = _tile(tp, tm_cap)
    n_in, n_out, n_acc = len(ins), len(outs), len(accs)
    n_into = sum(isinstance(o, Into) for o in outs)

    def body(*refs):
        i = pl.program_id(0)
        res = fn(i * tm, *[r[...] for r in refs[:n_in]])
        if not isinstance(res, (tuple, list)):
            res = (res,)
        assert len(res) == n_out + n_acc, (name, len(res))
        out_refs = refs[n_in + n_into:]
        for k in range(n_out):
            out_refs[k][...] = res[k].astype(out_refs[k].dtype)
        for k in range(n_acc):
            ref = out_refs[n_out + k]

            @pl.when(i == 0)
            def _():
                ref[...] = jnp.zeros_like(ref)

            ref[...] += res[n_out + k]

    in_specs = []
    for o in ins:
        if isinstance(o, Rw):
            in_specs.append(pl.BlockSpec((tm, o.width), functools.partial(lambda i, cb: (i, cb), cb=o.cb)))
        else:
            in_specs.append(pl.BlockSpec(o.arr.shape, functools.partial(lambda i, nd: (0,) * nd, nd=o.arr.ndim)))

    def spec_of(o):
        if isinstance(o, Into):
            assert o.col0 % o.width == 0, (name, o.col0, o.width)
            return pl.BlockSpec((tm, o.width), functools.partial(lambda i, cb: (i, cb), cb=o.col0 // o.width))
        return pl.BlockSpec((tm, o[0]), lambda i: (i, 0))

    return _call_with_into(
        body, name, (tp // tm,), in_specs, [o.arr for o in ins], list(outs), spec_of,
        lambda o: jax.ShapeDtypeStruct((tp, o[0]), o[1]),
        [pl.BlockSpec(s, functools.partial(lambda i, nd: (0,) * nd, nd=len(s))) for s in accs],
        [jax.ShapeDtypeStruct(s, F32) for s in accs], "arbitrary")


class Cl:
    def __init__(self, arr, col0=0):
        self.arr, self.col0 = arr, col0


def _cols(fn, name, ins, outs, ncols, tc):
    assert ncols % tc == 0
    n_in, n_out = len(ins), len(outs)
    n_into = sum(isinstance(o, Into) for o in outs)

    def body(*refs):
        res = fn(*[r[...] for r in refs[:n_in]])
        if not isinstance(res, (tuple, list)):
            res = (res,)
        assert len(res) == n_out, (name, len(res))
        out_refs = refs[n_in + n_into:]
        for k in range(n_out):
            out_refs[k][...] = res[k].astype(out_refs[k].dtype)

    in_specs = []
    for o in ins:
        assert o.col0 % tc == 0, (name, o.col0, tc)
        in_specs.append(pl.BlockSpec((o.arr.shape[0], tc), functools.partial(lambda j, off: (0, j + off), off=o.col0 // tc)))

    def spec_of(o):
        if isinstance(o, Into):
            assert o.col0 % tc == 0 and o.width == ncols, (name, o.col0, o.width)
            return pl.BlockSpec((o.arr.shape[0], tc), functools.partial(lambda j, off: (0, j + off), off=o.col0 // tc))
        return pl.BlockSpec((o[0], tc), lambda j: (0, j))

    return _call_with_into(body, name, (ncols // tc,), in_specs, [o.arr for o in ins], list(outs), spec_of,
                           lambda o: jax.ShapeDtypeStruct((o[0], ncols), o[1]), [], [], "parallel")


def _ln_stats(u):
    mu = jnp.mean(u, axis=-1, keepdims=True)
    xc = u - mu
    var = jnp.mean(xc * xc, axis=-1, keepdims=True)
    rstd = lax.rsqrt(var + LN_EPS)
    return xc * rstd, rstd


def _ln_fwd(terms, g, b, name):
    coefs = [c for c, _ in terms]

    def fn(row0, *blk):
        xs, (gg, bb) = blk[:len(coefs)], blk[len(coefs):]
        u = sum(c * x for c, x in zip(coefs, xs))
        xhat, _ = _ln_stats(u)
        return xhat * gg + bb

    d = terms[0][1].shape[1]
    return _rows(fn, name, [Rw(x) for _, x in terms] + [Pm(g.reshape(1, d)), Pm(b.reshape(1, d))], [(d, F32)])[0]


def _ln_bwd(dy_terms, u_terms, g, name):
    dc = [c for c, _ in dy_terms]
    uc = [c for c, _ in u_terms]
    d = u_terms[0][1].shape[1]

    def fn(row0, *blk):
        dys = blk[:len(dc)]
        xs = blk[len(dc):len(dc) + len(uc)]
        gg = blk[-1]
        dy = sum(c * x for c, x in zip(dc, dys))
        u = sum(c * x for c, x in zip(uc, xs))
        xhat, rstd = _ln_stats(u)
        gdy = dy * gg
        m1 = jnp.mean(gdy, axis=-1, keepdims=True)
        m2 = jnp.mean(gdy * xhat, axis=-1, keepdims=True)
        du = rstd * (gdy - m1 - xhat * m2)
        return du, jnp.sum(dy * xhat, axis=0, keepdims=True), jnp.sum(dy, axis=0, keepdims=True)

    ins = [Rw(x) for _, x in dy_terms] + [Rw(x) for _, x in u_terms] + [Pm(g.reshape(1, d))]
    return _rows(fn, name, ins, [(d, F32)], accs=[(1, d), (1, d)])


def _loss_head(y, tgt, t_real, name):
    d = y.shape[1]

    def fn(row0, yb, tb):
        rows = _row_ids(yb.shape, row0)
        live = (rows >= N_META) & (rows < t_real)
        diff = jnp.where(live, yb - tb, 0.0)
        return diff * (1.0 / d), jnp.sum(diff * diff, axis=0, keepdims=True) * (0.5 / d)

    return _rows(fn, name, [Rw(y), Rw(tgt)], [(d, F32)], accs=[(1, d)])


def _rms(x, g):
    r = lax.rsqrt(jnp.mean(x * x, axis=-1, keepdims=True) + RMS_EPS)
    return x * r * g


def _rms_bwd(dy, x, g):
    r = lax.rsqrt(jnp.mean(x * x, axis=-1, keepdims=True) + RMS_EPS)
    gdy = dy * g
    dx = r * gdy - x * (r * r * r) * jnp.mean(gdy * x, axis=-1, keepdims=True)
    return dx, jnp.sum(dy * x * r, axis=0, keepdims=True)


def _mla_norms(proj, qn, kvn, name):
    def fn(row0, cq, ckv, g1, g2):
        return _rms(cq, g1), _rms(ckv, g2)

    return _rows(fn, name, [Rw(proj, Q_RANK, C_CQ // Q_RANK), Rw(proj, KV_RANK, C_CKV // KV_RANK),
                            Pm(qn.reshape(1, Q_RANK)), Pm(kvn.reshape(1, KV_RANK))],
                 [(Q_RANK, F32), (KV_RANK, F32)])


def _mla_norms_bwd(dcqn, dckvn, dkrp, proj, qn, kvn, dproj, name):
    def fn(row0, d1, d2, dkr, cq, ckv, g1, g2):
        dx1, dg1 = _rms_bwd(d1, cq, g1)
        dx2, dg2 = _rms_bwd(d2, ckv, g2)
        return jnp.concatenate([dx1, dx2, dkr], axis=1), dg1, dg2

    return _rows(fn, name, [Rw(dcqn), Rw(dckvn), Rw(dkrp), Rw(proj, Q_RANK, C_CQ // Q_RANK), Rw(proj, KV_RANK, C_CKV // KV_RANK),
                            Pm(qn.reshape(1, Q_RANK)), Pm(kvn.reshape(1, KV_RANK))],
                 [Into(dproj, C_CQ, PROJ_COLS - C_CQ)], accs=[(1, Q_RANK), (1, KV_RANK)])


def _fold_rope(z):
    return z + pltpu.roll(z, QK_ROPE, 1)


def _mla_pack(qext, kv, proj, cs, name):
    tp = qext.shape[0]
    tm = _tile(tp, 384)
    hw, nope_all = 2 * LANE, HEADS * QK_NOPE

    def body(q_ref, kv_ref, kr_ref, cs_ref, qo_ref, ko_ref, vo_ref):
        cs_ = cs_ref[...]
        low = lax.broadcasted_iota(jnp.int32, cs_.shape, 1) < QK_ROPE
        kr = _fold_rope(kr_ref[...] * cs_).astype(BF16)
        for h in range(HEADS):
            qr = jnp.where(low, _fold_rope(q_ref[:, h * hw + QK_NOPE:(h + 1) * hw] * cs_), 0.0)
            qo_ref[:, h * hw:h * hw + QK_NOPE] = (q_ref[:, h * hw:h * hw + QK_NOPE] * ATT_SCALE).astype(BF16)
            qo_ref[:, h * hw + QK_NOPE:(h + 1) * hw] = (qr * ATT_SCALE).astype(BF16)
            ko_ref[:, h * hw:h * hw + QK_NOPE] = kv_ref[:, h * QK_NOPE:(h + 1) * QK_NOPE].astype(BF16)
            ko_ref[:, h * hw + QK_NOPE:(h + 1) * hw] = kr
        vo_ref[...] = kv_ref[:, nope_all:].astype(BF16)

    row = lambda w: pl.BlockSpec((tm, w), lambda i: (i, 0))
    return pl.pallas_call(
        body, name=name, grid=(tp // tm,),
        in_specs=[row(HEADS * hw), row(2 * nope_all), pl.BlockSpec((tm, LANE), lambda i: (i, C_KRP // LANE)), row(LANE)],
        out_specs=[row(HEADS * hw), row(HEADS * hw), row(nope_all)],
        out_shape=[jax.ShapeDtypeStruct((tp, HEADS * hw), BF16),
                   jax.ShapeDtypeStruct((tp, HEADS * hw), BF16),
                   jax.ShapeDtypeStruct((tp, nope_all), BF16)],
        compiler_params=_cparams("parallel"),
    )(qext, kv, proj, cs)


def _mla_unpack(dq, dk, dv, cs, name):
    tp = dq.shape[0]
    tm = _tile(tp, 384)
    hw, nope_all = 2 * LANE, HEADS * QK_NOPE

    def body(dq_ref, dk_ref, dv_ref, cs_ref, dqe_ref, dkv_ref, dkr_ref):
        cs_ = cs_ref[...]
        low = lax.broadcasted_iota(jnp.int32, cs_.shape, 1) < QK_ROPE
        dkr = None
        for h in range(HEADS):
            dqe_ref[:, h * hw:h * hw + QK_NOPE] = dq_ref[:, h * hw:h * hw + QK_NOPE] * ATT_SCALE
            dqr = jnp.where(low, dq_ref[:, h * hw + QK_NOPE:(h + 1) * hw], 0.0) * ATT_SCALE
            dqe_ref[:, h * hw + QK_NOPE:(h + 1) * hw] = _fold_rope(dqr) * cs_
            dkv_ref[:, h * QK_NOPE:(h + 1) * QK_NOPE] = dk_ref[:, h * hw:h * hw + QK_NOPE]
            part = jnp.where(low, dk_ref[:, h * hw + QK_NOPE:(h + 1) * hw], 0.0)
            dkr = part if h == 0 else dkr + part
        dkv_ref[:, nope_all:] = dv_ref[...]
        dkr_ref[...] = _fold_rope(dkr) * cs_

    row = lambda w: pl.BlockSpec((tm, w), lambda i: (i, 0))
    return pl.pallas_call(
        body, name=name, grid=(tp // tm,),
        in_specs=[row(HEADS * hw), row(HEADS * hw), row(nope_all), row(LANE)],
        out_specs=[row(HEADS * hw), row(2 * nope_all), row(LANE)],
        out_shape=[jax.ShapeDtypeStruct((tp, HEADS * hw), F32),
                   jax.ShapeDtypeStruct((tp, 2 * nope_all), F32),
                   jax.ShapeDtypeStruct((tp, LANE), F32)],
        compiler_params=_cparams("parallel"),
    )(dq, dk, dv, cs)


def _attn_fwd(q, k, v, t_real, name):
    tp = q.shape[0]
    tq = _tile(tp, 1408)
    tkc = _tile(tp, 1408)
    nkc = -(-t_real // tkc)

    def body(q_ref, k_ref, v_ref, o_ref, lse_ref):
        qb = q_ref[...]
        m = l = acc = None
        for c in range(nkc):
            s = lax.dot_general(qb, k_ref[c * tkc:(c + 1) * tkc, :], (((1,), (1,)), ((), ())), preferred_element_type=F32)
            if (c + 1) * tkc > t_real:
                cols = lax.broadcasted_iota(jnp.int32, s.shape, 1) + c * tkc
                s = jnp.where(cols < t_real, s, NEG_BIG)
            mc = jnp.max(s, axis=-1, keepdims=True)
            m_new = mc if c == 0 else jnp.maximum(m, mc)
            p = jnp.exp2(s - m_new)
            lc = jnp.sum(p, axis=-1, keepdims=True)
            pv = jnp.dot(p.astype(BF16), v_ref[c * tkc:(c + 1) * tkc, :], preferred_element_type=F32)
            if c == 0:
                l, acc = lc, pv
            else:
                alpha = jnp.exp2(m - m_new)
                l, acc = alpha * l + lc, alpha * acc + pv
            m = m_new
        o_ref[...] = acc / l
        lse_ref[...] = m + jnp.log2(l)

    return pl.pallas_call(
        body, name=name, grid=(HEADS, tp // tq),
        in_specs=[pl.BlockSpec((tq, 2 * LANE), lambda h, i: (i, h)),
                  pl.BlockSpec((tp, 2 * LANE), lambda h, i: (0, h)),
                  pl.BlockSpec((tp, LANE), lambda h, i: (0, h))],
        out_specs=[pl.BlockSpec((tq, LANE), lambda h, i: (i, h)),
                   pl.BlockSpec((None, tq, 1), lambda h, i: (h, i, 0))],
        out_shape=[jax.ShapeDtypeStruct((tp, HEADS * LANE), F32),
                   jax.ShapeDtypeStruct((HEADS, tp, 1), F32)],
        compiler_params=_cparams("parallel", "parallel"),
    )(q, k, v)


def _attn_bwd(q, k, v, do, o, lse, t_real, name):
    tp = q.shape[0]
    tq = _tile(tp, 704, 64)
    tkc = _tile(tp, 1408)
    nkc = -(-t_real // tkc)

    def body(q_ref, k_ref, v_ref, do_ref, o_ref, lse_ref, dq_ref, dk_ref, dv_ref):
        i = pl.program_id(1)

        @pl.when(i == 0)
        def _():
            dk_ref[...] = jnp.zeros_like(dk_ref)
            dv_ref[...] = jnp.zeros_like(dv_ref)

        qb = q_ref[...]
        dob = do_ref[...]
        dob16 = dob.astype(BF16)
        dol2 = (dob * LN2).astype(BF16)
        delta = jnp.sum(dob * o_ref[...], axis=-1, keepdims=True) * LN2
        lse = lse_ref[...]
        dq = None
        for c in range(nkc):
            ks = slice(c * tkc, (c + 1) * tkc)
            kb = k_ref[ks, :]
            s = lax.dot_general(qb, kb, (((1,), (1,)), ((), ())), preferred_element_type=F32)
            p = jnp.exp2(s - lse)
            if (c + 1) * tkc > t_real:
                cols = lax.broadcasted_iota(jnp.int32, s.shape, 1) + c * tkc
                p = jnp.where(cols < t_real, p, 0.0)
            dp = lax.dot_general(dol2, v_ref[ks, :], (((1,), (1,)), ((), ())), preferred_element_type=F32)
            ds = (p * (dp - delta)).astype(BF16)
            dqc = jnp.dot(ds, kb, preferred_element_type=F32)
            dq = dqc if c == 0 else dq + dqc
            dk_ref[ks, :] += lax.dot_general(ds, qb, (((0,), (0,)), ((), ())), preferred_element_type=F32)
            dv_ref[ks, :] += lax.dot_general(p.astype(BF16), dob16, (((0,), (0,)), ((), ())), preferred_element_type=F32)
        dq_ref[...] = dq

    return pl.pallas_call(
        body, name=name, grid=(HEADS, tp // tq),
        in_specs=[pl.BlockSpec((tq, 2 * LANE), lambda h, i: (i, h)),
                  pl.BlockSpec((tp, 2 * LANE), lambda h, i: (0, h)),
                  pl.BlockSpec((tp, LANE), lambda h, i: (0, h)),
                  pl.BlockSpec((tq, LANE), lambda h, i: (i, h)),
                  pl.BlockSpec((tq, LANE), lambda h, i: (i, h)),
                  pl.BlockSpec((None, tq, 1), lambda h, i: (h, i, 0))],
        out_specs=[pl.BlockSpec((tq, 2 * LANE), lambda h, i: (i, h)),
                   pl.BlockSpec((tp, 2 * LANE), lambda h, i: (0, h)),
                   pl.BlockSpec((tp, LANE), lambda h, i: (0, h))],
        out_shape=[jax.ShapeDtypeStruct((tp, HEADS * 2 * LANE), F32),
                   jax.ShapeDtypeStruct((tp, HEADS * 2 * LANE), F32),
                   jax.ShapeDtypeStruct((tp, HEADS * LANE), F32)],
        compiler_params=_cparams("parallel", "arbitrary"),
    )(q, k, v, do, o, lse)


def _shift_rows(x, s):
    tp = x.shape[0]
    return x if s % tp == 0 else pltpu.roll(x, s % tp, 0)


def _conv_fwd_val(xm, w, b, pad_left):
    acc = b + w[0:1, :] * _shift_rows(xm, pad_left)
    for k in range(1, w.shape[0]):
        acc = acc + w[k:k + 1, :] * _shift_rows(xm, pad_left - k)
    return acc


def _conv_bwd_val(dy, xm, w, pad_left, live):
    kk = w.shape[0]
    dx = w[0:1, :] * _shift_rows(dy, -pad_left)
    dws = [jnp.sum(dy * _shift_rows(xm, pad_left), axis=0, keepdims=True)]
    for k in range(1, kk):
        dx = dx + w[k:k + 1, :] * _shift_rows(dy, k - pad_left)
        dws.append(jnp.sum(dy * _shift_rows(xm, pad_left - k), axis=0, keepdims=True))
    return jnp.where(live, dx, 0.0), jnp.concatenate(dws, axis=0), jnp.sum(dy, axis=0, keepdims=True)


def _lru_conv_fwd(proj, w, b, t_real, name):
    def fn(x, ww, bb):
        xm = jnp.where(_row_ids(x.shape) < t_real, x, 0.0)
        return _conv_fwd_val(xm, ww, bb, 2)

    return _cols(fn, name, [Cl(proj, C_LRU_X), Cl(w), Cl(b.reshape(1, -1))], [(proj.shape[0], F32)], D_MODEL, 128)[0]


def _lru_conv_bwd(dxc, proj, w, dproj, t_real, name):
    def fn(dy, x, ww):
        live = _row_ids(x.shape) < t_real
        xm = jnp.where(live, x, 0.0)
        dym = jnp.where(live, dy, 0.0)
        return _conv_bwd_val(dym, xm, ww, 2, live)

    return _cols(fn, name, [Cl(dxc), Cl(proj, C_LRU_X), Cl(w)],
                 [Into(dproj, C_LRU_X, D_MODEL), (w.shape[0], F32), (1, F32)], D_MODEL, 128)


def _ffn_conv_act(up, w, b, t_real, name):
    def fn(g, v, wg, wv, bg, bv):
        live = _row_ids(g.shape) < t_real
        gc = _conv_fwd_val(jnp.where(live, g, 0.0), wg, bg, 1)
        vc = _conv_fwd_val(jnp.where(live, v, 0.0), wv, bv, 1)
        return _gelu(gc) * vc

    b2 = b.reshape(1, -1)
    return _cols(fn, name, [Cl(up), Cl(up, D_FF), Cl(w), Cl(w, D_FF), Cl(b2), Cl(b2, D_FF)],
                 [(up.shape[0], F32)], D_FF, 128)[0]


def _ffn_conv_act_bwd(dm, up, w, b, t_real, name):
    tp, kk = up.shape[0], w.shape[0]
    nb = D_FF // LANE
    assert nb >= 2

    def body(dm_ref, g_ref, v_ref, wg_ref, wv_ref, bg_ref, bv_ref, dup_ref, dwg_ref, dwv_ref, dbg_ref, dbv_ref, stage, sems):
        j = pl.program_id(0)
        slot = j % 2

        def copies(step, sl):
            return [pltpu.make_async_copy(stage.at[sl, half],
                                          dup_ref.at[:, pl.ds(pl.multiple_of(half * D_FF + step * LANE, LANE), LANE)],
                                          sems.at[sl, half]) for half in range(2)]

        @pl.when(j >= 2)
        def _():
            for cp in copies(j - 2, slot):
                cp.wait()

        live = _row_ids((tp, LANE)) < t_real
        gm, vm = jnp.where(live, g_ref[...], 0.0), jnp.where(live, v_ref[...], 0.0)
        gc = _conv_fwd_val(gm, wg_ref[...], bg_ref[...], 1)
        vc = _conv_fwd_val(vm, wv_ref[...], bv_ref[...], 1)
        act, dact = _gelu_and_grad(gc)
        dmm = jnp.where(live, dm_ref[...], 0.0)
        stage[slot, 0], dwg_ref[...], dbg_ref[...] = _conv_bwd_val(dmm * vc * dact, gm, wg_ref[...], 1, live)
        stage[slot, 1], dwv_ref[...], dbv_ref[...] = _conv_bwd_val(dmm * act, vm, wv_ref[...], 1, live)
        for cp in copies(j, slot):
            cp.start()

        @pl.when(j == nb - 1)
        def _():
            for cp in copies(j - 1, 1 - slot) + copies(j, slot):
                cp.wait()

    b2 = b.reshape(1, -1)
    col = lambda rows, off: pl.BlockSpec((rows, LANE), functools.partial(lambda j, o: (0, j + o), o=off))
    return pl.pallas_call(
        body, name=name, grid=(nb,),
        in_specs=[col(tp, 0), col(tp, 0), col(tp, nb), col(kk, 0), col(kk, nb), col(1, 0), col(1, nb)],
        out_specs=[pl.BlockSpec(memory_space=pl.ANY), col(kk, 0), col(kk, 0), col(1, 0), col(1, 0)],
        out_shape=[jax.ShapeDtypeStruct((tp, 2 * D_FF), F32), jax.ShapeDtypeStruct((kk, D_FF), F32),
                   jax.ShapeDtypeStruct((kk, D_FF), F32), jax.ShapeDtypeStruct((1, D_FF), F32),
                   jax.ShapeDtypeStruct((1, D_FF), F32)],
        scratch_shapes=[pltpu.VMEM((2, 2, tp, LANE), F32), pltpu.SemaphoreType.DMA((2, 2))],
        compiler_params=_cparams("arbitrary"),
    )(dm, up, up, w, w, b2, b2)


def _lru_gates_fwd(xc, wg, b4, lam, t_real, name):
    tp = xc.shape[0]
    tm = _tile(tp, 1408)

    def body(x_ref, w_ref, b_ref, lam_ref, r0_ref, r1_ref, i0_ref, i1_ref, a0_ref, a1_ref, u0_ref, u1_ref):
        x = x_ref[...]
        xb = x.astype(BF16)
        live = _row_ids(x.shape, pl.program_id(1) * tm) < t_real
        bb = b_ref[...]
        sp = _softplus_neg(lam_ref[...])
        gate = [_sigmoid(jnp.dot(xb, w_ref[k], preferred_element_type=F32) + bb[k:k + 1, :]) for k in range(4)]
        for d, (r_ref, i_ref, a_ref, u_ref) in enumerate(((r0_ref, i0_ref, a0_ref, u0_ref), (r1_ref, i1_ref, a1_ref, u1_ref))):
            r, ig = gate[d], gate[2 + d]
            a = jnp.exp(-LRU_C * r * sp[d:d + 1, :])
            r_ref[...] = r
            i_ref[...] = ig
            a_ref[...] = a
            u_ref[...] = jnp.where(live, jnp.sqrt(1.0 - a * a) * (ig * x), 0.0)

    blk = pl.BlockSpec((tm, LANE), lambda g, i: (i, g))
    return pl.pallas_call(
        body, name=name, grid=(LRU_BLOCKS, tp // tm),
        in_specs=[blk, pl.BlockSpec((None, 4, LANE, LANE), lambda g, i: (g, 0, 0, 0)),
                  pl.BlockSpec((4, LANE), lambda g, i: (0, g)), pl.BlockSpec((2, LANE), lambda g, i: (0, g))],
        out_specs=[blk] * 8,
        out_shape=[jax.ShapeDtypeStruct((tp, D_MODEL), F32)] * 8,
        compiler_params=_cparams("parallel", "parallel"),
    )(xc, wg, b4, lam)


def _lru_gates_bwd(l0, l1, da0, da1, r0, r1, i0, i1, a0, a1, xc, wg, lam, t_real, name):
    tp = xc.shape[0]
    tm = _tile(tp, 1408)

    def body(l0_ref, l1_ref, da0_ref, da1_ref, r0_ref, r1_ref, i0_ref, i1_ref, a0_ref, a1_ref, x_ref, w_ref, lam_ref,
             dx_ref, dw_ref, db_ref, dlam_ref):
        i = pl.program_id(1)
        x = x_ref[...]
        xb = x.astype(BF16)
        live = _row_ids(x.shape, i * tm) < t_real
        lam_ = lam_ref[...]
        sp = _softplus_neg(lam_)
        dsp_dlam = -_sigmoid(-lam_)
        dx = jnp.zeros_like(x)
        dpre = [None] * 4
        dlam_rows = []
        for d, (l_ref, da_ref, r_ref, i_ref, a_ref) in enumerate(((l0_ref, da0_ref, r0_ref, i0_ref, a0_ref),
                                                                  (l1_ref, da1_ref, r1_ref, i1_ref, a1_ref))):
            r, ig, a = r_ref[...], i_ref[...], a_ref[...]
            du = jnp.where(live, l_ref[...], 0.0)
            s = jnp.sqrt(1.0 - a * a)
            dv = du * s
            ds = du * (ig * x)
            dla = jnp.where(live, da_ref[...], 0.0) * a - ds * (a * a) / s
            dla = jnp.where(live, dla, 0.0)
            dr = dla * (-LRU_C) * sp[d:d + 1, :]
            dlam_rows.append(jnp.sum(dla * (-LRU_C) * r, axis=0, keepdims=True) * dsp_dlam[d:d + 1, :])
            dpre[d] = dr * r * (1.0 - r)
            dpre[2 + d] = dv * x * ig * (1.0 - ig)
            dx = dx + dv * ig

        @pl.when(i == 0)
        def _():
            dw_ref[...] = jnp.zeros_like(dw_ref)
            db_ref[...] = jnp.zeros_like(db_ref)
            dlam_ref[...] = jnp.zeros_like(dlam_ref)

        for k in range(4):
            pk = dpre[k].astype(BF16)
            dx = dx + lax.dot_general(pk, w_ref[k], (((1,), (1,)), ((), ())), preferred_element_type=F32)
            dw_ref[k] += lax.dot_general(xb, pk, (((0,), (0,)), ((), ())), preferred_element_type=F32)
        db_ref[...] += jnp.concatenate([jnp.sum(p, axis=0, keepdims=True) for p in dpre], axis=0)
        dlam_ref[...] += jnp.concatenate(dlam_rows, axis=0)
        dx_ref[...] = dx

    blk = pl.BlockSpec((tm, LANE), lambda g, i: (i, g))
    return pl.pallas_call(
        body, name=name, grid=(LRU_BLOCKS, tp // tm),
        in_specs=[blk] * 11 + [pl.BlockSpec((None, 4, LANE, LANE), lambda g, i: (g, 0, 0, 0)),
                               pl.BlockSpec((2, LANE), lambda g, i: (0, g))],
        out_specs=[blk, pl.BlockSpec((None, 4, LANE, LANE), lambda g, i: (g, 0, 0, 0)),
                   pl.BlockSpec((4, LANE), lambda g, i: (0, g)), pl.BlockSpec((2, LANE), lambda g, i: (0, g))],
        out_shape=[jax.ShapeDtypeStruct((tp, D_MODEL), F32), jax.ShapeDtypeStruct((LRU_BLOCKS, 4, LANE, LANE), F32),
                   jax.ShapeDtypeStruct((4, D_MODEL), F32), jax.ShapeDtypeStruct((2, D_MODEL), F32)],
        compiler_params=_cparams("parallel", "arbitrary"),
    )(l0, l1, da0, da1, r0, r1, i0, i1, a0, a1, xc, wg, lam)


SCAN_UNROLL = 4


def _loop_tiles(nt, step, carry):
    assert nt % SCAN_UNROLL == 0

    def trip(tt, c):
        for u in range(SCAN_UNROLL):
            c = step(tt * SCAN_UNROLL + u, c)
        return c

    return lax.fori_loop(0, nt // SCAN_UNROLL, trip, carry)


def _tile_scan(a, u, reverse):
    rows = lax.broadcasted_iota(jnp.int32, a.shape, 0)
    for s in (1, 2, 4):
        if reverse:
            keep = rows < SUBLANE - s
            a_sh, u_sh = pltpu.roll(a, SUBLANE - s, 0), pltpu.roll(u, SUBLANE - s, 0)
        else:
            keep = rows >= s
            a_sh, u_sh = pltpu.roll(a, s, 0), pltpu.roll(u, s, 0)
        u = u + a * jnp.where(keep, u_sh, 0.0)
        a = a * jnp.where(keep, a_sh, 1.0)
    return a, u


def _scan_fwd(a0, u0, a1, u1, name):
    tp, d = a0.shape
    tc = 128
    nt = tp // SUBLANE

    def body(a0_ref, u0_ref, a1_ref, u1_ref, h0_ref, h1_ref):
        def step(t, carry):
            c0, c1 = carry
            f = pl.multiple_of(t * SUBLANE, SUBLANE)
            b = pl.multiple_of((nt - 1 - t) * SUBLANE, SUBLANE)
            pa, pu = _tile_scan(a0_ref[pl.ds(f, SUBLANE), :], u0_ref[pl.ds(f, SUBLANE), :], False)
            h = pu + pa * c0
            h0_ref[pl.ds(f, SUBLANE), :] = h
            c0 = h[SUBLANE - 1:SUBLANE, :]
            pa, pu = _tile_scan(a1_ref[pl.ds(b, SUBLANE), :], u1_ref[pl.ds(b, SUBLANE), :], True)
            h = pu + pa * c1
            h1_ref[pl.ds(b, SUBLANE), :] = h
            c1 = h[0:1, :]
            return c0, c1

        z = jnp.zeros((1, tc), F32)
        _loop_tiles(nt, step, (z, z))

    blk = pl.BlockSpec((tp, tc), lambda j: (0, j))
    return pl.pallas_call(
        body, name=name, grid=(d // tc,), in_specs=[blk] * 4, out_specs=[blk] * 2,
        out_shape=[jax.ShapeDtypeStruct((tp, d), F32)] * 2,
        compiler_params=_cparams("parallel"),
    )(a0, u0, a1, u1)


def _scan_bwd(dh, a0, a1, h0, h1, name):
    tp, d = dh.shape
    tc = 128
    nt = tp // SUBLANE

    def body(dh_ref, a0_ref, a1_ref, h0_ref, h1_ref, l0_ref, l1_ref, da0_ref, da1_ref):
        rows8 = lax.broadcasted_iota(jnp.int32, (SUBLANE, tc), 0)

        def step(t, carry):
            c0, c1 = carry
            b = pl.multiple_of((nt - 1 - t) * SUBLANE, SUBLANE)
            f = pl.multiple_of(t * SUBLANE, SUBLANE)
            a = a0_ref[pl.ds(b, SUBLANE), :]
            a_next = jnp.where(rows8 < SUBLANE - 1, pltpu.roll(a, SUBLANE - 1, 0), 1.0)
            pa, pu = _tile_scan(a_next, dh_ref[pl.ds(b, SUBLANE), :], True)
            lam = pu + pa * c0
            l0_ref[pl.ds(b, SUBLANE), :] = lam
            c0 = a[0:1, :] * lam[0:1, :]
            a = a1_ref[pl.ds(f, SUBLANE), :]
            a_prev = jnp.where(rows8 >= 1, pltpu.roll(a, 1, 0), 1.0)
            pa, pu = _tile_scan(a_prev, dh_ref[pl.ds(f, SUBLANE), :], False)
            lam = pu + pa * c1
            l1_ref[pl.ds(f, SUBLANE), :] = lam
            c1 = a[SUBLANE - 1:SUBLANE, :] * lam[SUBLANE - 1:SUBLANE, :]
            return c0, c1

        z = jnp.zeros((1, tc), F32)
        _loop_tiles(nt, step, (z, z))
        rows = lax.broadcasted_iota(jnp.int32, (tp, tc), 0)
        da0_ref[...] = l0_ref[...] * jnp.where(rows >= 1, pltpu.roll(h0_ref[...], 1, 0), 0.0)
        da1_ref[...] = l1_ref[...] * jnp.where(rows < tp - 1, pltpu.roll(h1_ref[...], tp - 1, 0), 0.0)

    blk = pl.BlockSpec((tp, tc), lambda j: (0, j))
    return pl.pallas_call(
        body, name=name, grid=(d // tc,), in_specs=[blk] * 5, out_specs=[blk] * 4,
        out_shape=[jax.ShapeDtypeStruct((tp, d), F32)] * 4,
        compiler_params=_cparams("parallel"),
    )(dh, a0, a1, h0, h1)


def _gated_h(proj, h0, h1, name):
    def fn(row0, lg, x0, x1):
        return _gelu(lg) * (x0 + x1)

    return _rows(fn, name, [Rw(proj, D_MODEL, C_LRU_G // D_MODEL), Rw(h0), Rw(h1)], [(D_MODEL, F32)])[0]


def _gated_h_bwd(dgh, proj, h0, h1, dproj, name):
    def fn(row0, dg, lg, x0, x1):
        act, dact = _gelu_and_grad(lg)
        return dg * (x0 + x1) * dact, dg * act

    return _rows(fn, name, [Rw(dgh), Rw(proj, D_MODEL, C_LRU_G // D_MODEL), Rw(h0), Rw(h1)],
                 [Into(dproj, C_LRU_G, D_MODEL), (D_MODEL, F32)])


def _mix(proj, y_mla, y_lru, name):
    def fn(row0, gm, gl, ym, yl):
        return _sigmoid(gm) * ym + _sigmoid(gl) * yl

    return _rows(fn, name, [Rw(proj, D_MODEL, C_G_MLA // D_MODEL), Rw(proj, D_MODEL, C_G_LRU // D_MODEL), Rw(y_mla), Rw(y_lru)],
                 [(D_MODEL, F32)])[0]


def _mix_bwd(dz, proj, y_mla, y_lru, dproj, name):
    def fn(row0, dzb, gm, gl, ym, yl):
        sm, sl = _sigmoid(gm), _sigmoid(gl)
        dg = jnp.concatenate([dzb * ym * sm * (1.0 - sm), dzb * yl * sl * (1.0 - sl)], axis=1)
        return dzb * sm, dzb * sl, dg

    return _rows(fn, name, [Rw(dz), Rw(proj, D_MODEL, C_G_MLA // D_MODEL), Rw(proj, D_MODEL, C_G_LRU // D_MODEL),
                            Rw(y_mla), Rw(y_lru)], [(D_MODEL, F32), (D_MODEL, F32), Into(dproj, C_G_MLA, 2 * D_MODEL)])


def _layer_fwd(h, w_in, more_weights, cs, t_real, tag):
    proj = _matmul(h, w_in, tag + "proj", tb=True)
    w = dict(more_weights(0, proj), w_in=w_in)
    cqn, ckvn = _mla_norms(proj, w['q_norm'], w['kv_norm'], tag + "mla_norms")
    qext = _matmul(cqn, w['w_q'], tag + "q_up")
    kv = _matmul(ckvn, w['w_kv'], tag + "kv_up")
    qc, kc, vb = _mla_pack(qext, kv, proj, cs, tag + "mla_pack")
    o, lse = _attn_fwd(qc, kc, vb, t_real, tag + "attn_fwd")
    w.update(more_weights(1, o))
    y_mla = _matmul(o, w['w_o_mla'], tag + "o_mla")
    xc = _lru_conv_fwd(proj, w['lru_conv_w'], w['lru_conv_b'], t_real, tag + "lru_conv")
    r0, r1, i0, i1, a0, a1, u0, u1 = _lru_gates_fwd(xc, w['w_g'], w['b4'], w['lru_lambda'], t_real, tag + "lru_gates")
    h0, h1 = _scan_fwd(a0, u0, a1, u1, tag + "lru_scan")
    gh = _gated_h(proj, h0, h1, tag + "lru_gate_out")
    y_lru = _matmul(gh, w['w_o_lru'], tag + "o_lru")
    z = _mix(proj, y_mla, y_lru, tag + "mix")
    zo = _matmul(z, w['w_out'], tag + "w_out")
    hm = _ln_fwd([(DN_ALPHA, h), (1.0, zo)], w['ln1_g'], w['ln1_b'], tag + "ln1")
    w.update(more_weights(2, hm))
    up = _matmul(hm, w['w_up'], tag + "w_up", tb=True)
    m = _ffn_conv_act(up, w['ffn_conv_w'], w['ffn_conv_b'], t_real, tag + "ffn_conv")
    f = _matmul(m, w['w_down'], tag + "w_down", tk_cap=1408)
    out = _ln_fwd([(DN_ALPHA, hm), (1.0, f)], w['ln2_g'], w['ln2_b'], tag + "ln2")
    saved = dict(w=w, h=h, proj=proj, cqn=cqn, ckvn=ckvn, qc=qc, kc=kc, vb=vb, o=o, lse=lse, y_mla=y_mla, xc=xc,
                 r0=r0, r1=r1, i0=i0, i1=i1, a0=a0, a1=a1, h0=h0, h1=h1, gh=gh, y_lru=y_lru, z=z, zo=zo, hm=hm,
                 up=up, m=m, f=f)
    return out, saved


DW_MATMUL = dict(ta=True, out_dtype=BF16, tn_cap=1408, tk_cap=1408)


def _after(a, tok):
    return a if tok is None else a + tok.astype(a.dtype)


def _layer_bwd(dout_terms, s, cs, t_real, tag, emit, tok):
    w = s['w']
    g = {}
    du2, dg2, db2 = _ln_bwd(dout_terms, [(DN_ALPHA, s['hm']), (1.0, s['f'])], _after(w['ln2_g'], tok), tag + "ln2_bwd")
    g['ln2_g'], g['ln2_b'] = dg2, db2
    dm = _matmul(du2, w['w_down'], tag + "w_down_dx", tb=True)
    g['w_down'] = _matmul(s['m'], du2, tag + "w_down_dw", **DW_MATMUL)
    dup, dwg_, dwv_, dbg_, dbv_ = _ffn_conv_act_bwd(dm, s['up'], w['ffn_conv_w'], w['ffn_conv_b'], t_real, tag + "ffn_conv_bwd")
    g['ffn_conv_w'] = jnp.concatenate([dwg_, dwv_], axis=1)
    g['ffn_conv_b'] = jnp.concatenate([dbg_, dbv_], axis=1)
    dhm_mm = _matmul(dup, w['w_up'], tag + "w_up_dx", tk_cap=1408)
    g['w_up'] = _matmul(s['hm'], dup, tag + "w_up_dw", **DW_MATMUL)
    tok = emit('ffn', g)
    g = {}
    du1, dg1, db1 = _ln_bwd([(DN_ALPHA, du2), (1.0, dhm_mm)], [(DN_ALPHA, s['h']), (1.0, s['zo'])],
                            _after(w['ln1_g'], tok), tag + "ln1_bwd")
    g['ln1_g'], g['ln1_b'] = dg1, db1
    dz = _matmul(du1, w['w_out'], tag + "w_out_dx", tb=True)
    g['w_out'] = _matmul(s['z'], du1, tag + "w_out_dw", **DW_MATMUL)
    dproj = lax.empty(s['proj'].shape, F32)
    dy_mla, dy_lru, dproj = _mix_bwd(dz, s['proj'], s['y_mla'], s['y_lru'], dproj, tag + "mix_bwd")
    do = _matmul(dy_mla, w['w_o_mla'], tag + "o_mla_dx", tb=True)
    g['w_o_mla'] = _matmul(s['o'], dy_mla, tag + "o_mla_dw", **DW_MATMUL)
    dqc, dkc, dv = _attn_bwd(s['qc'], s['kc'], s['vb'], do, s['o'], s['lse'], t_real, tag + "attn_bwd")
    dqext, dkv, dkrp = _mla_unpack(dqc, dkc, dv, cs, tag + "mla_unpack")
    dcqn = _matmul(dqext, w['w_q'], tag + "q_up_dx", tb=True)
    g['w_q'] = _matmul(s['cqn'], dqext, tag + "q_up_dw", **DW_MATMUL)
    dckvn = _matmul(dkv, w['w_kv'], tag + "kv_up_dx", tb=True)
    g['w_kv'] = _matmul(s['ckvn'], dkv, tag + "kv_up_dw", **DW_MATMUL)
    dgh = _matmul(dy_lru, w['w_o_lru'], tag + "o_lru_dx", tb=True)
    g['w_o_lru'] = _matmul(s['gh'], dy_lru, tag + "o_lru_dw", **DW_MATMUL)
    dproj, dhs = _gated_h_bwd(dgh, s['proj'], s['h0'], s['h1'], dproj, tag + "lru_gate_out_bwd")
    l0, l1, da0, da1 = _scan_bwd(dhs, s['a0'], s['a1'], s['h0'], s['h1'], tag + "lru_scan_bwd")
    dxc, g['w_g'], g['b4'], g['lru_lambda'] = _lru_gates_bwd(
        l0, l1, da0, da1, s['r0'], s['r1'], s['i0'], s['i1'], s['a0'], s['a1'], s['xc'], w['w_g'], w['lru_lambda'],
        t_real, tag + "lru_gates_bwd")
    dproj, g['lru_conv_w'], g['lru_conv_b'] = _lru_conv_bwd(dxc, s['proj'], w['lru_conv_w'], dproj, t_real, tag + "lru_conv_bwd")
    tok = emit('mid', g)
    dproj, dqn, dkvn = _mla_norms_bwd(dcqn, dckvn, _after(dkrp, tok), s['proj'], w['q_norm'], w['kv_norm'], dproj,
                                      tag + "mla_norms_bwd")
    tok = emit('in', {'w_in': _matmul(s['h'], dproj, tag + "proj_dw", **DW_MATMUL), 'q_norm': dqn, 'kv_norm': dkvn})
    dh_mm = _matmul(dproj, w['w_in'], tag + "proj_dx", tk_cap=1536)
    return [(DN_ALPHA, du1), (1.0, dh_mm)], tok


def _swap_halves(a, axis=-1):
    h1, h2 = jnp.split(a, 2, axis=axis)
    return jnp.concatenate([h2, h1], axis=axis)


def _w_in_kernel(w_in_t):
    cq, ckv, kr, lg, lx, gm, gl = jnp.split(w_in_t, [256, 384, 448, 1472, 2496, 3520], axis=0)
    return jnp.concatenate([lg, lx, gm, gl, cq, ckv, kr, _swap_halves(kr, axis=0)], axis=0)


def _layer_weights(fl):
    w = {}
    if 'w_uq' in fl:
        uq = fl['w_uq']
        w['w_q'] = jnp.concatenate([uq, _swap_halves(uq[..., QK_NOPE:])], axis=-1).reshape(Q_RANK, HEADS * 2 * LANE)
        w['w_kv'] = jnp.concatenate([fl['w_uk'].reshape(KV_RANK, -1), fl['w_uv'].reshape(KV_RANK, -1)], axis=1).astype(BF16)
        w['w_g'] = jnp.moveaxis(jnp.concatenate([fl['w_rg'], fl['w_ig']], axis=0), 0, 1).astype(BF16)
        w['b4'] = jnp.concatenate([fl['b_rg'], fl['b_ig']], axis=0)
    for n in ('q_norm', 'kv_norm', 'w_o_mla', 'lru_conv_w', 'lru_conv_b', 'lru_lambda', 'w_o_lru', 'w_out', 'ln1_g',
              'ln1_b', 'w_up', 'ffn_conv_w', 'ffn_conv_b', 'w_down', 'ln2_g', 'ln2_b'):
        if n in fl:
            w[n] = fl[n]
    return w


def _layer_grads(g):
    out = {}
    if 'w_in' in g:
        lg, lx, gm, gl, cq, ckv, kr, krs = jnp.split(g['w_in'], [1024, 2048, 3072, 4096, 4352, 4480, 4544], axis=1)
        out['w_in'] = jnp.concatenate([cq, ckv, kr + _swap_halves(krs), lg, lx, gm, gl], axis=1)
    if 'w_q' in g:
        gq = g['w_q'].reshape(Q_RANK, HEADS, 2 * LANE)
        out['w_uq'] = jnp.concatenate([gq[..., :QK_NOPE], gq[..., QK_NOPE:QK_NOPE + QK_ROPE] + _swap_halves(gq[..., QK_NOPE + QK_ROPE:])], axis=-1)
    if 'w_kv' in g:
        out['w_uk'] = g['w_kv'][:, :HEADS * QK_NOPE].reshape(KV_RANK, HEADS, QK_NOPE)
        out['w_uv'] = g['w_kv'][:, HEADS * QK_NOPE:].reshape(KV_RANK, HEADS, V_HEAD)
    if 'w_g' in g:
        gg = jnp.moveaxis(g['w_g'], 1, 0)
        out['w_rg'], out['w_ig'] = gg[:2], gg[2:]
    if 'b4' in g:
        out['b_rg'], out['b_ig'] = g['b4'][:2], g['b4'][2:]
    for n in ('q_norm', 'kv_norm', 'lru_conv_b', 'ln1_g', 'ln1_b', 'ffn_conv_b', 'ln2_g', 'ln2_b'):
        if n in g:
            out[n] = g[n].reshape(-1)
    for n in ('w_o_mla', 'lru_conv_w', 'lru_lambda', 'w_o_lru', 'w_out', 'w_up', 'ffn_conv_w', 'w_down'):
        if n in g:
            out[n] = g[n]
    return out


def _rope_table(tp):
    half = QK_ROPE // 2
    inv_freq = jnp.exp(-math.log(ROPE_THETA) * jnp.arange(half, dtype=F32) / half)
    ang = jnp.arange(tp, dtype=F32)[:, None] * inv_freq[None, :]
    c, s = jnp.cos(ang), jnp.sin(ang)
    return jnp.concatenate([c, c, -s, s], axis=1)


def _local_step(x, target, meta, ln0_g, ln0_b, layer_w, t_pad, emit):
    seq = x.shape[0]
    t_real = N_META + seq
    zpad = jnp.zeros((t_pad - t_real, D_MODEL), F32)
    xin = jnp.concatenate([meta, x, zpad], axis=0)
    tgt = jnp.concatenate([jnp.zeros((N_META, D_MODEL), F32), target, zpad], axis=0)
    cs = _rope_table(t_pad)
    h = _ln_fwd([(1.0, xin)], ln0_g, ln0_b, "ln0")
    saved = []
    for l in range(DEPTH):
        w_in, rest_of_weights = layer_w[l](h)
        h, s = _layer_fwd(h, w_in, rest_of_weights, cs, t_real, "l%d_" % l)
        saved.append(s)
    dy, lossvec = _loss_head(h, tgt, t_real, "loss_head")
    terms, tok = [(1.0, dy)], None
    for l in reversed(range(DEPTH)):
        terms, tok = _layer_bwd(terms, saved[l], cs, t_real, "l%d_" % l,
                                functools.partial(lambda stage, g, l: emit(l, stage, _layer_grads(g)), l=l), tok)
    dxin, dg0, db0 = _ln_bwd(terms, [(1.0, xin)], _after(ln0_g, tok), "ln0_bwd")
    emit(None, 'head', {'meta_tokens': dxin[:N_META], 'ln0_g': dg0.reshape(-1), 'ln0_b': db0.reshape(-1), 'loss': lossvec})
    return dxin[N_META:t_real]


_HBM = pl.BlockSpec(memory_space=pltpu.HBM)
_SEM = pl.BlockSpec(memory_space=pltpu.SEMAPHORE)
_SIDE_EFFECT = pltpu.SideEffectType.DATAFLOW_SIDE_EFFECTING


def _peer_copies(src_refs, land_refs, scatters, send_sems, recv_sems):
    x, y, c = lax.axis_index("x"), lax.axis_index("y"), lax.axis_index("c")
    me = 4 * x + 2 * y + c
    copies = []
    for k in range(1, N_DEV):
        px = 1 - x if k & 4 else x
        py = 1 - y if k & 2 else y
        pc = 1 - c if k & 1 else c
        for t, (src, land) in enumerate(zip(src_refs, land_refs)):
            copies.append(pltpu.make_async_remote_copy(
                src_ref=src.at[4 * px + 2 * py + pc] if scatters[t] else src, dst_ref=land.at[me],
                send_sem=send_sems.at[7 * t + k - 1], recv_sem=recv_sems.at[7 * t + k - 1],
                device_id=(px, py, pc), device_id_type=pl.DeviceIdType.MESH))
    return me, copies


def _exchange_start(groups, name):
    flat = [it for grp in groups for it in grp]
    nt, ng = len(flat), len(groups)
    scatters = [sc for _, sc in flat]
    srcs = [pltpu.with_memory_space_constraint(a, pltpu.HBM) for a, _ in flat]
    land_shapes = [a.shape if sc else (N_DEV,) + a.shape for a, sc in flat]
    lands = [pltpu.with_memory_space_constraint(lax.empty(s, a.dtype), pltpu.HBM) for s, (a, _) in zip(land_shapes, flat)]
    bounds = [0]
    for grp in groups:
        bounds.append(bounds[-1] + len(grp))

    def body(*refs):
        src_refs, land_refs = refs[:nt], refs[nt:2 * nt]
        sem_refs = refs[2 * nt:2 * nt + 2 * ng]
        token_ref = refs[4 * nt + 2 * ng]
        for gi in range(ng):
            lo, hi = bounds[gi], bounds[gi + 1]
            _, copies = _peer_copies(src_refs[lo:hi], land_refs[lo:hi], scatters[lo:hi], sem_refs[2 * gi], sem_refs[2 * gi + 1])
            for cp in copies:
                cp.start()
        token_ref[...] = jnp.zeros_like(token_ref)

    out_shape = []
    for grp in groups:
        out_shape += [pltpu.SemaphoreType.DMA((7 * len(grp),)), pltpu.SemaphoreType.DMA((7 * len(grp),))]
    out_shape += [pltpu.HBM(a.shape, a.dtype) for a in srcs] + [pltpu.HBM(s, a.dtype) for s, a in zip(land_shapes, srcs)]
    out_shape += [jax.ShapeDtypeStruct((SUBLANE, LANE), F32)]
    res = pl.pallas_call(
        body, name=name, out_shape=out_shape,
        in_specs=[_HBM] * (2 * nt),
        out_specs=[_SEM] * (2 * ng) + [_HBM] * (2 * nt) + [pl.BlockSpec(memory_space=pltpu.VMEM)],
        input_output_aliases={t: 2 * ng + t for t in range(2 * nt)},
        compiler_params=pltpu.CompilerParams(has_side_effects=_SIDE_EFFECT),
    )(*srcs, *lands)
    sems, thru, token = res[:2 * ng], res[2 * ng:2 * ng + 2 * nt], res[-1]
    states = []
    for gi in range(ng):
        lo, hi = bounds[gi], bounds[gi + 1]
        states.append((sems[2 * gi], sems[2 * gi + 1], thru[lo:hi], thru[nt + lo:nt + hi], scatters[lo:hi]))
    return states, token[0, 0]


def _exchange_wait(state, after, name):
    send_sems, recv_sems, srcs, lands, scatters = state
    n = len(srcs)

    def body(*refs):
        _, copies = _peer_copies(refs[:n], refs[n:2 * n], scatters, refs[2 * n], refs[2 * n + 1])
        for cp in copies:
            cp.wait_send()
        for cp in copies:
            cp.wait_recv()

    res = pl.pallas_call(
        body, name=name,
        out_shape=[pltpu.HBM(a.shape, a.dtype) for a in srcs] + [pltpu.HBM(a.shape, a.dtype) for a in lands],
        in_specs=[_HBM] * (2 * n) + [_SEM, _SEM, _HBM],
        out_specs=[_HBM] * (2 * n),
        input_output_aliases={t: t for t in range(2 * n)},
        compiler_params=pltpu.CompilerParams(has_side_effects=_SIDE_EFFECT),
    )(*srcs, *lands, send_sems, recv_sems, pltpu.with_memory_space_constraint(after, pltpu.HBM))
    me = 4 * lax.axis_index("x") + 2 * lax.axis_index("y") + lax.axis_index("c")
    out = []
    for src, land, sc in zip(res[:n], res[n:], scatters):
        own = lax.dynamic_index_in_dim(src, me, 0, keepdims=True) if sc else src[None]
        out.append(lax.dynamic_update_slice_in_dim(land, own, me, 0))
    return out


def _as_rows(shape):
    return (1, shape[0]) if len(shape) == 1 else (math.prod(shape[:-1]), shape[-1])


def _sum_adamw(pieces, w, m, v, name):
    shape = w.shape
    nl = len(pieces)
    if nl > 1 and _as_rows(shape[1:])[0] % 16:
        pieces, nl = [jnp.stack(pieces, axis=1)], 1
    rows, cols = _as_rows(shape)
    rl = rows // nl
    cap = max(16, (1 << 18) // cols // 16 * 16)
    tr = _tile(rl, cap, 16)
    nb = rl // tr
    c1 = 1.0 / (1.0 - ADAM_B1 ** ADAM_STEP)
    c2 = 1.0 / (1.0 - ADAM_B2 ** ADAM_STEP)

    def body(*refs):
        p_refs = refs[:nl]
        w_ref, m_ref, v_ref, g_ref, d_ref, nm_ref, nv_ref = refs[nl:]
        li = pl.program_id(0)

        def total(p_ref):
            acc = p_ref[0].astype(F32)
            for k in range(1, N_DEV):
                acc = acc + p_ref[k].astype(F32)
            return acc

        gg = total(p_refs[0])
        for l in range(1, nl):
            gg = jnp.where(li == l, total(p_refs[l]), gg)
        nm = ADAM_B1 * m_ref[...] + (1.0 - ADAM_B1) * gg
        nv = ADAM_B2 * v_ref[...] + (1.0 - ADAM_B2) * (gg * gg)
        g_ref[...] = gg
        d_ref[...] = -ADAM_LR * ((nm * c1) / (jnp.sqrt(nv * c2) + ADAM_EPS) + ADAM_WD * w_ref[...])
        nm_ref[...] = nm
        nv_ref[...] = nv

    blk = pl.BlockSpec((tr, cols), lambda li, i: (li * nb + i, 0))
    p_specs = [pl.BlockSpec((N_DEV, tr, cols), functools.partial(lambda li, i, l: (0, jnp.where(li == l, i, 0), 0), l=l))
               for l in range(nl)]
    res = pl.pallas_call(
        body, name=name, grid=(nl, nb),
        in_specs=p_specs + [blk] * 3, out_specs=[blk] * 4,
        out_shape=[jax.ShapeDtypeStruct((rows, cols), F32)] * 4,
        compiler_params=_cparams("parallel", "parallel"),
    )(*[p.reshape(N_DEV, rl, cols) for p in pieces], *[a.reshape(rows, cols) for a in (w, m, v)])
    return [r.reshape(shape) for r in res]


def _to_shards(full, axis):
    shp = full.shape
    a = full.reshape(shp[:axis] + (N_DEV, shp[axis] // N_DEV) + shp[axis + 1:])
    return jnp.moveaxis(a, axis, 0)


def _from_shards(blocks, axis):
    a = jnp.moveaxis(blocks, 0, axis)
    shp = a.shape
    return a.reshape(shp[:axis] + (shp[axis] * shp[axis + 1],) + shp[axis + 2:])


def kernel(x, meta_tokens, ln0_g, ln0_b, w_in, q_norm, kv_norm, w_uq, w_uk, w_uv, w_o_mla, lru_conv_w, lru_conv_b, w_rg, b_rg, w_ig, b_ig, lru_lambda, w_o_lru, w_out, ln1_g, ln1_b, w_up, ffn_conv_w, ffn_conv_b, w_down, ln2_g, ln2_b, loss_target, m_meta_tokens, m_ln0_g, m_ln0_b, m_w_in, m_q_norm, m_kv_norm, m_w_uq, m_w_uk, m_w_uv, m_w_o_mla, m_lru_conv_w, m_lru_conv_b, m_w_rg, m_b_rg, m_w_ig, m_b_ig, m_lru_lambda, m_w_o_lru, m_w_out, m_ln1_g, m_ln1_b, m_w_up, m_ffn_conv_w, m_ffn_conv_b, m_w_down, m_ln2_g, m_ln2_b, v_meta_tokens, v_ln0_g, v_ln0_b, v_w_in, v_q_norm, v_kv_norm, v_w_uq, v_w_uk, v_w_uv, v_w_o_mla, v_lru_conv_w, v_lru_conv_b, v_w_rg, v_b_rg, v_w_ig, v_b_ig, v_lru_lambda, v_w_o_lru, v_w_out, v_ln1_g, v_ln1_b, v_w_up, v_ffn_conv_w, v_ffn_conv_b, v_w_down, v_ln2_g, v_ln2_b):
    args = (meta_tokens, ln0_g, ln0_b, w_in, q_norm, kv_norm, w_uq, w_uk, w_uv, w_o_mla, lru_conv_w, lru_conv_b, w_rg, b_rg, w_ig, b_ig, lru_lambda, w_o_lru, w_out, ln1_g, ln1_b, w_up, ffn_conv_w, ffn_conv_b, w_down, ln2_g, ln2_b)
    ms = (m_meta_tokens, m_ln0_g, m_ln0_b, m_w_in, m_q_norm, m_kv_norm, m_w_uq, m_w_uk, m_w_uv, m_w_o_mla, m_lru_conv_w, m_lru_conv_b, m_w_rg, m_b_rg, m_w_ig, m_b_ig, m_lru_lambda, m_w_o_lru, m_w_out, m_ln1_g, m_ln1_b, m_w_up, m_ffn_conv_w, m_ffn_conv_b, m_w_down, m_ln2_g, m_ln2_b)
    vs = (v_meta_tokens, v_ln0_g, v_ln0_b, v_w_in, v_q_norm, v_kv_norm, v_w_uq, v_w_uk, v_w_uv, v_w_o_mla, v_lru_conv_w, v_lru_conv_b, v_w_rg, v_b_rg, v_w_ig, v_b_ig, v_lru_lambda, v_w_o_lru, v_w_out, v_ln1_g, v_ln1_b, v_w_up, v_ffn_conv_w, v_ffn_conv_b, v_w_down, v_ln2_g, v_ln2_b)
    wd, md, vd = dict(zip(WEIGHTS, args)), dict(zip(WEIGHTS, ms)), dict(zip(WEIGHTS, vs))

    def shard_axis(n, l):
        return SHARD_AXIS[n] - (0 if l is None else 1)

    def shard(n, l):
        a = wd[n] if l is None else wd[n][l]
        if n in SENT_TRANSPOSED:
            a = a.T
        return a.astype(BF16) if n in BIG else a

    first = [('meta_tokens', None), ('w_in', 0)]
    staged = [[(n, 0) for n in names] for names in STAGE_WEIGHTS]
    later = [(n, 1) for n in SHARDED if n != 'meta_tokens']
    gather, token = _exchange_start([[(shard(*k), False) for k in keys] for keys in [first] + staged + [later]], "gather_start")

    def arrive(gi, keys, after, name):
        return {k: b.reshape(-1, b.shape[-1]) if k[0] in SENT_TRANSPOSED else _from_shards(b, shard_axis(*k))
                for k, b in zip(keys, _exchange_wait(gather[gi], after, name))}

    def layer_weights(got, l, names):
        fl = {n: wd[n][l] for n in names if n in REPLICATED}
        fl.update({n: a for (n, _), a in got.items() if n in names})
        return _layer_weights(fl)

    ln0_g = _after(wd['ln0_g'], token)
    got_first = arrive(0, first, ln0_g, "gather_wait_first")

    def first_layer(h):
        def more(stage, after):
            got = arrive(1 + stage, staged[stage], after, "gather_wait_l0_%d" % stage)
            return layer_weights(got, 0, STAGE_WEIGHTS[stage] + STAGE_REPLICATED[stage])
        return _w_in_kernel(got_first['w_in', 0]), more

    def second_layer(h):
        got = arrive(1 + len(staged), later, h, "gather_wait_l1")
        return _w_in_kernel(got['w_in', 1]), lambda stage, after: layer_weights(got, 1, STAGE_WEIGHTS[stage] + STAGE_REPLICATED[stage])

    sent = []
    pending = []

    def send(l, stage, grads):
        for n, g in grads.items():
            if n in SHARD_AXIS:
                g = _to_shards(g, shard_axis(n, l))
                pending.append(((n, l), (g.astype(BF16) if n in BIG else g, True)))
            else:
                pending.append(((n, l), (g.astype(BF16) if n in LARGE_REPLICATED else g, False)))
        if l == DEPTH - 1 and stage != 'in':
            return None
        (state,), tok = _exchange_start([[it for _, it in pending]], "grads_start_%s_%s" % (l, stage))
        sent.append(([k for k, _ in pending], state))
        pending.clear()
        return tok

    seq = x.shape[1]
    t_pad = -(-(N_META + seq + MIN_PAD_ROWS) // LANE) * LANE
    grad_x = _local_step(x[0], loss_target[0], got_first['meta_tokens', None], ln0_g, wd['ln0_b'],
                         [first_layer, second_layer], t_pad, send)

    pieces = {}
    for gi, (keys, state) in enumerate(sent):
        pieces.update(zip(keys, _exchange_wait(state, grad_x, "grads_wait_%d" % gi)))
    loss = jnp.sum(pieces['loss', None])
    outs = {}
    for n in WEIGHTS:
        ps = [pieces[n, None]] if (n, None) in pieces else [pieces[n, l] for l in range(DEPTH)]
        outs[n] = _sum_adamw(ps, wd[n], md[n], vd[n], "adamw_" + n)
    res = [loss, grad_x[None]]
    for k in range(4):
        res += [outs[n][k] for n in WEIGHTS]
    return tuple(res)
```

```python
import functools
import math

import jax
import jax.numpy as jnp
from jax import lax
from jax.experimental import pallas as pl
from jax.experimental.pallas import tpu as pltpu

F32 = jnp.float32
BF16 = jnp.bfloat16

N_DEV = 8
D_MODEL = 1024
N_META = 16
HEADS = 8
QK_NOPE = 128
QK_ROPE = 64
V_HEAD = 128
Q_RANK = 256
KV_RANK = 128
ROPE_THETA = 10000.0
LRU_BLOCKS = 8
LRU_C = 8.0
D_FF = 2816
DEPTH = 2
DN_ALPHA = (2.0 * DEPTH) ** 0.25
LN_EPS = 1e-5
RMS_EPS = 1e-6
LN2 = math.log(2.0)
ATT_SCALE = 1.0 / math.sqrt(QK_NOPE + QK_ROPE) / LN2
NEG_BIG = -1e30

ADAM_LR = 0.001
ADAM_B1 = 0.9
ADAM_B2 = 0.999
ADAM_EPS = 1e-08
ADAM_WD = 0.01
ADAM_STEP = 10

MIN_PAD_ROWS = 2
LANE = 128
SUBLANE = 8
VMEM_LIMIT = 56 * 1024 * 1024

PROJ_COLS = 4 * D_MODEL + Q_RANK + KV_RANK + 2 * QK_ROPE
C_LRU_G, C_LRU_X, C_G_MLA, C_G_LRU = 0, D_MODEL, 2 * D_MODEL, 3 * D_MODEL
C_CQ = 4 * D_MODEL
C_CKV = C_CQ + Q_RANK
C_KRP = C_CKV + KV_RANK

WEIGHTS = ['meta_tokens', 'ln0_g', 'ln0_b', 'w_in', 'q_norm', 'kv_norm', 'w_uq', 'w_uk', 'w_uv', 'w_o_mla',
           'lru_conv_w', 'lru_conv_b', 'w_rg', 'b_rg', 'w_ig', 'b_ig', 'lru_lambda', 'w_o_lru', 'w_out',
           'ln1_g', 'ln1_b', 'w_up', 'ffn_conv_w', 'ffn_conv_b', 'w_down', 'ln2_g', 'ln2_b']
SHARD_AXIS = {'meta_tokens': 1, 'w_in': 2, 'w_uq': 1, 'w_o_mla': 1, 'lru_conv_w': 2, 'b_rg': 2, 'b_ig': 2,
              'lru_lambda': 2, 'w_o_lru': 1, 'w_out': 1, 'w_up': 2, 'ffn_conv_w': 2, 'w_down': 1}
BIG = ['w_in', 'w_uq', 'w_o_mla', 'w_o_lru', 'w_out', 'w_up', 'w_down']
SHARDED = [n for n in WEIGHTS if n in SHARD_AXIS]
REPLICATED = [n for n in WEIGHTS if n not in SHARD_AXIS]
LARGE_REPLICATED = ['w_uk', 'w_uv', 'w_rg', 'w_ig']
SENT_TRANSPOSED = ['w_in', 'w_up']
STAGE_WEIGHTS = [['w_uq', 'lru_conv_w', 'b_rg', 'b_ig', 'lru_lambda'], ['w_o_mla', 'w_o_lru', 'w_out'], ['w_up', 'ffn_conv_w', 'w_down']]
STAGE_REPLICATED = [['q_norm', 'kv_norm', 'w_uk', 'w_uv', 'lru_conv_b', 'w_rg', 'w_ig'], ['ln1_g', 'ln1_b'], ['ffn_conv_b', 'ln2_g', 'ln2_b']]


def _cparams(*sem):
    return pltpu.CompilerParams(dimension_semantics=sem, vmem_limit_bytes=VMEM_LIMIT)


def _tile(n, cap, unit=LANE):
    best = None
    t = unit
    while t <= min(n, cap):
        if n % t == 0:
            best = t
        t += unit
    return n if best is None else best


def _sigmoid(x):
    return 1.0 / (1.0 + jnp.exp(-x))


_GELU_C = math.sqrt(2.0 / math.pi)


_GELU_A = 0.044715


def _gelu(x):
    t = jnp.tanh(x * (_GELU_C + (_GELU_C * _GELU_A) * (x * x)))
    hx = 0.5 * x
    return hx + hx * t


def _gelu_and_grad(x):
    x2 = x * x
    t = jnp.tanh(x * (_GELU_C + (_GELU_C * _GELU_A) * x2))
    hx = 0.5 * x
    dg = 0.5 + 0.5 * t + (hx * (1.0 - t * t)) * (_GELU_C + (3.0 * _GELU_C * _GELU_A) * x2)
    return hx + hx * t, dg


def _softplus_neg(lam):
    z = jnp.exp(-jnp.abs(lam))
    w = 1.0 + z
    log1p = jnp.where(w == 1.0, z, jnp.log(w) * z / (w - 1.0))
    return jnp.maximum(-lam, 0.0) + log1p


def _row_ids(shape, row0=0):
    return lax.broadcasted_iota(jnp.int32, shape, 0) + row0


def _matmul(a, b, name, ta=False, tb=False, out_dtype=F32, tm_cap=1408, tn_cap=1024, tk_cap=2048):
    if ta:
        kdim, m = a.shape
    else:
        m, kdim = a.shape
    if tb:
        n, k2 = b.shape
    else:
        k2, n = b.shape
    assert kdim == k2, (a.shape, b.shape, ta, tb)
    tm, tn, tk = _tile(m, tm_cap), _tile(n, tn_cap), _tile(kdim, tk_cap)
    nk = kdim // tk

    def body(a_ref, b_ref, o_ref, *acc):
        dn = (((0 if ta else 1,), (1 if tb else 0,)), ((), ()))
        part = lax.dot_general(a_ref[...].astype(BF16), b_ref[...].astype(BF16), dn, preferred_element_type=F32)
        if nk == 1:
            o_ref[...] = part.astype(o_ref.dtype)
            return
        acc_ref, k = acc[0], pl.program_id(2)

        @pl.when(k == 0)
        def _():
            acc_ref[...] = part

        @pl.when(k > 0)
        def _():
            acc_ref[...] += part

        @pl.when(k == nk - 1)
        def _():
            o_ref[...] = acc_ref[...].astype(o_ref.dtype)

    a_spec = pl.BlockSpec((tk, tm), lambda i, j, k: (k, i)) if ta else pl.BlockSpec((tm, tk), lambda i, j, k: (i, k))
    b_spec = pl.BlockSpec((tn, tk), lambda i, j, k: (j, k)) if tb else pl.BlockSpec((tk, tn), lambda i, j, k: (k, j))
    return pl.pallas_call(
        body, name=name,
        grid=(m // tm, n // tn, nk),
        in_specs=[a_spec, b_spec],
        out_specs=pl.BlockSpec((tm, tn), lambda i, j, k: (i, j)),
        out_shape=jax.ShapeDtypeStruct((m, n), out_dtype),
        scratch_shapes=[pltpu.VMEM((tm, tn), F32)] if nk > 1 else [],
        compiler_params=_cparams("parallel", "parallel", "arbitrary"),
    )(a, b)


class Rw:
    def __init__(self, arr, width=None, cb=0):
        self.arr, self.width, self.cb = arr, (arr.shape[1] if width is None else width), cb


class Pm:
    def __init__(self, arr):
        self.arr = arr


class Into:
    def __init__(self, arr, col0, width):
        self.arr, self.col0, self.width = arr, col0, width


def _call_with_into(body, name, grid, in_specs, operands, outs, spec_of, shape_of, extra_out_specs, extra_out_shape, sem):
    intos = [(k, o) for k, o in enumerate(outs) if isinstance(o, Into)]
    aliases = {len(operands) + n: k for n, (k, _) in enumerate(intos)}
    return pl.pallas_call(
        body, name=name, grid=grid,
        in_specs=in_specs + [pl.BlockSpec(memory_space=pl.ANY)] * len(intos),
        out_specs=[spec_of(o) for o in outs] + extra_out_specs,
        out_shape=[jax.ShapeDtypeStruct(o.arr.shape, o.arr.dtype) if isinstance(o, Into) else shape_of(o) for o in outs]
        + extra_out_shape,
        input_output_aliases=aliases,
        compiler_params=_cparams(sem),
    )(*operands, *[o.arr for _, o in intos])


def _rows(fn, name, ins, outs, accs=(), tm_cap=384):
    tp = next(o.arr.shape[0] for o in ins if isinstance(o, Rw))
    tm = _tile(tp, tm_cap)
    n_in, n_out, n_acc = len(ins), len(outs), len(accs)
    n_into = sum(isinstance(o, Into) for o in outs)

    def body(*refs):
        i = pl.program_id(0)
        res = fn(i * tm, *[r[...] for r in refs[:n_in]])
        if not isinstance(res, (tuple, list)):
            res = (res,)
        assert len(res) == n_out + n_acc, (name, len(res))
        out_refs = refs[n_in + n_into:]
        for k in range(n_out):
            out_refs[k][...] = res[k].astype(out_refs[k].dtype)
        for k in range(n_acc):
            ref = out_refs[n_out + k]

            @pl.when(i == 0)
            def _():
                ref[...] = jnp.zeros_like(ref)

            ref[...] += res[n_out + k]

    in_specs = []
    for o in ins:
        if isinstance(o, Rw):
            in_specs.append(pl.BlockSpec((tm, o.width), functools.partial(lambda i, cb: (i, cb), cb=o.cb)))
        else:
            in_specs.append(pl.BlockSpec(o.arr.shape, functools.partial(lambda i, nd: (0,) * nd, nd=o.arr.ndim)))

    def spec_of(o):
        if isinstance(o, Into):
            assert o.col0 % o.width == 0, (name, o.col0, o.width)
            return pl.BlockSpec((tm, o.width), functools.partial(lambda i, cb: (i, cb), cb=o.col0 // o.width))
        return pl.BlockSpec((tm, o[0]), lambda i: (i, 0))

    return _call_with_into(
        body, name, (tp // tm,), in_specs, [o.arr for o in ins], list(outs), spec_of,
        lambda o: jax.ShapeDtypeStruct((tp, o[0]), o[1]),
        [pl.BlockSpec(s, functools.partial(lambda i, nd: (0,) * nd, nd=len(s))) for s in accs],
        [jax.ShapeDtypeStruct(s, F32) for s in accs], "arbitrary")


class Cl:
    def __init__(self, arr, col0=0):
        self.arr, self.col0 = arr, col0


def _cols(fn, name, ins, outs, ncols, tc):
    assert ncols % tc == 0
    n_in, n_out = len(ins), len(outs)
    n_into = sum(isinstance(o, Into) for o in outs)

    def body(*refs):
        res = fn(*[r[...] for r in refs[:n_in]])
        if not isinstance(res, (tuple, list)):
            res = (res,)
        assert len(res) == n_out, (name, len(res))
        out_refs = refs[n_in + n_into:]
        for k in range(n_out):
            out_refs[k][...] = res[k].astype(out_refs[k].dtype)

    in_specs = []
    for o in ins:
        assert o.col0 % tc == 0, (name, o.col0, tc)
        in_specs.append(pl.BlockSpec((o.arr.shape[0], tc), functools.partial(lambda j, off: (0, j + off), off=o.col0 // tc)))

    def spec_of(o):
        if isinstance(o, Into):
            assert o.col0 % tc == 0 and o.width == ncols, (name, o.col0, o.width)
            return pl.BlockSpec((o.arr.shape[0], tc), functools.partial(lambda j, off: (0, j + off), off=o.col0 // tc))
        return pl.BlockSpec((o[0], tc), lambda j: (0, j))

    return _call_with_into(body, name, (ncols // tc,), in_specs, [o.arr for o in ins], list(outs), spec_of,
                           lambda o: jax.ShapeDtypeStruct((o[0], ncols), o[1]), [], [], "parallel")


def _ln_stats(u):
    mu = jnp.mean(u, axis=-1, keepdims=True)
    xc = u - mu
    var = jnp.mean(xc * xc, axis=-1, keepdims=True)
    rstd = lax.rsqrt(var + LN_EPS)
    return xc * rstd, rstd


def _ln_fwd(terms, g, b, name):
    coefs = [c for c, _ in terms]

    def fn(row0, *blk):
        xs, (gg, bb) = blk[:len(coefs)], blk[len(coefs):]
        u = sum(c * x for c, x in zip(coefs, xs))
        xhat, _ = _ln_stats(u)
        return xhat * gg + bb

    d = terms[0][1].shape[1]
    return _rows(fn, name, [Rw(x) for _, x in terms] + [Pm(g.reshape(1, d)), Pm(b.reshape(1, d))], [(d, F32)])[0]


def _ln_bwd(dy_terms, u_terms, g, name):
    dc = [c for c, _ in dy_terms]
    uc = [c for c, _ in u_terms]
    d = u_terms[0][1].shape[1]

    def fn(row0, *blk):
        dys = blk[:len(dc)]
        xs = blk[len(dc):len(dc) + len(uc)]
        gg = blk[-1]
        dy = sum(c * x for c, x in zip(dc, dys))
        u = sum(c * x for c, x in zip(uc, xs))
        xhat, rstd = _ln_stats(u)
        gdy = dy * gg
        m1 = jnp.mean(gdy, axis=-1, keepdims=True)
        m2 = jnp.mean(gdy * xhat, axis=-1, keepdims=True)
        du = rstd * (gdy - m1 - xhat * m2)
        return du, jnp.sum(dy * xhat, axis=0, keepdims=True), jnp.sum(dy, axis=0, keepdims=True)

    ins = [Rw(x) for _, x in dy_terms] + [Rw(x) for _, x in u_terms] + [Pm(g.reshape(1, d))]
    return _rows(fn, name, ins, [(d, F32)], accs=[(1, d), (1, d)])


def _loss_head(y, tgt, t_real, name):
    d = y.shape[1]

    def fn(row0, yb, tb):
        rows = _row_ids(yb.shape, row0)
        live = (rows >= N_META) & (rows < t_real)
        diff = jnp.where(live, yb - tb, 0.0)
        return diff * (1.0 / d), jnp.sum(diff * diff, axis=0, keepdims=True) * (0.5 / d)

    return _rows(fn, name, [Rw(y), Rw(tgt)], [(d, F32)], accs=[(1, d)])


def _rms(x, g):
    r = lax.rsqrt(jnp.mean(x * x, axis=-1, keepdims=True) + RMS_EPS)
    return x * r * g


def _rms_bwd(dy, x, g):
    r = lax.rsqrt(jnp.mean(x * x, axis=-1, keepdims=True) + RMS_EPS)
    gdy = dy * g
    dx = r * gdy - x * (r * r * r) * jnp.mean(gdy * x, axis=-1, keepdims=True)
    return dx, jnp.sum(dy * x * r, axis=0, keepdims=True)


def _mla_norms(proj, qn, kvn, name):
    def fn(row0, cq, ckv, g1, g2):
        return _rms(cq, g1), _rms(ckv, g2)

    return _rows(fn, name, [Rw(proj, Q_RANK, C_CQ // Q_RANK), Rw(proj, KV_RANK, C_CKV // KV_RANK),
                            Pm(qn.reshape(1, Q_RANK)), Pm(kvn.reshape(1, KV_RANK))],
                 [(Q_RANK, F32), (KV_RANK, F32)])


def _mla_norms_bwd(dcqn, dckvn, dkrp, proj, qn, kvn, dproj, name):
    def fn(row0, d1, d2, dkr, cq, ckv, g1, g2):
        dx1, dg1 = _rms_bwd(d1, cq, g1)
        dx2, dg2 = _rms_bwd(d2, ckv, g2)
        return jnp.concatenate([dx1, dx2, dkr], axis=1), dg1, dg2

    return _rows(fn, name, [Rw(dcqn), Rw(dckvn), Rw(dkrp), Rw(proj, Q_RANK, C_CQ // Q_RANK), Rw(proj, KV_RANK, C_CKV // KV_RANK),
                            Pm(qn.reshape(1, Q_RANK)), Pm(kvn.reshape(1, KV_RANK))],
                 [Into(dproj, C_CQ, PROJ_COLS - C_CQ)], accs=[(1, Q_RANK), (1, KV_RANK)])


def _fold_rope(z):
    return z + pltpu.roll(z, QK_ROPE, 1)


def _mla_pack(qext, kv, proj, cs, name):
    tp = qext.shape[0]
    tm = _tile(tp, 384)
    hw, nope_all = 2 * LANE, HEADS * QK_NOPE

    def body(q_ref, kv_ref, kr_ref, cs_ref, qo_ref, ko_ref, vo_ref):
        cs_ = cs_ref[...]
        low = lax.broadcasted_iota(jnp.int32, cs_.shape, 1) < QK_ROPE
        kr = _fold_rope(kr_ref[...] * cs_).astype(BF16)
        for h in range(HEADS):
            qr = jnp.where(low, _fold_rope(q_ref[:, h * hw + QK_NOPE:(h + 1) * hw] * cs_), 0.0)
            qo_ref[:, h * hw:h * hw + QK_NOPE] = (q_ref[:, h * hw:h * hw + QK_NOPE] * ATT_SCALE).astype(BF16)
            qo_ref[:, h * hw + QK_NOPE:(h + 1) * hw] = (qr * ATT_SCALE).astype(BF16)
            ko_ref[:, h * hw:h * hw + QK_NOPE] = kv_ref[:, h * QK_NOPE:(h + 1) * QK_NOPE].astype(BF16)
            ko_ref[:, h * hw + QK_NOPE:(h + 1) * hw] = kr
        vo_ref[...] = kv_ref[:, nope_all:].astype(BF16)

    row = lambda w: pl.BlockSpec((tm, w), lambda i: (i, 0))
    return pl.pallas_call(
        body, name=name, grid=(tp // tm,),
        in_specs=[row(HEADS * hw), row(2 * nope_all), pl.BlockSpec((tm, LANE), lambda i: (i, C_KRP // LANE)), row(LANE)],
        out_specs=[row(HEADS * hw), row(HEADS * hw), row(nope_all)],
        out_shape=[jax.ShapeDtypeStruct((tp, HEADS * hw), BF16),
                   jax.ShapeDtypeStruct((tp, HEADS * hw), BF16),
                   jax.ShapeDtypeStruct((tp, nope_all), BF16)],
        compiler_params=_cparams("parallel"),
    )(qext, kv, proj, cs)


def _mla_unpack(dq, dk, dv, cs, name):
    tp = dq.shape[0]
    tm = _tile(tp, 384)
    hw, nope_all = 2 * LANE, HEADS * QK_NOPE

    def body(dq_ref, dk_ref, dv_ref, cs_ref, dqe_ref, dkv_ref, dkr_ref):
        cs_ = cs_ref[...]
        low = lax.broadcasted_iota(jnp.int32, cs_.shape, 1) < QK_ROPE
        dkr = None
        for h in range(HEADS):
            dqe_ref[:, h * hw:h * hw + QK_NOPE] = dq_ref[:, h * hw:h * hw + QK_NOPE] * ATT_SCALE
            dqr = jnp.where(low, dq_ref[:, h * hw + QK_NOPE:(h + 1) * hw], 0.0) * ATT_SCALE
            dqe_ref[:, h * hw + QK_NOPE:(h + 1) * hw] = _fold_rope(dqr) * cs_
            dkv_ref[:, h * QK_NOPE:(h + 1) * QK_NOPE] = dk_ref[:, h * hw:h * hw + QK_NOPE]
            part = jnp.where(low, dk_ref[:, h * hw + QK_NOPE:(h + 1) * hw], 0.0)
            dkr = part if h == 0 else dkr + part
        dkv_ref[:, nope_all:] = dv_ref[...]
        dkr_ref[...] = _fold_rope(dkr) * cs_

    row = lambda w: pl.BlockSpec((tm, w), lambda i: (i, 0))
    return pl.pallas_call(
        body, name=name, grid=(tp // tm,),
        in_specs=[row(HEADS * hw), row(HEADS * hw), row(nope_all), row(LANE)],
        out_specs=[row(HEADS * hw), row(2 * nope_all), row(LANE)],
        out_shape=[jax.ShapeDtypeStruct((tp, HEADS * hw), F32),
                   jax.ShapeDtypeStruct((tp, 2 * nope_all), F32),
                   jax.ShapeDtypeStruct((tp, LANE), F32)],
        compiler_params=_cparams("parallel"),
    )(dq, dk, dv, cs)


def _attn_fwd(q, k, v, t_real, name):
    tp = q.shape[0]
    tq = _tile(tp, 1408)
    tkc = _tile(tp, 1408)
    nkc = -(-t_real // tkc)

    def body(q_ref, k_ref, v_ref, o_ref, lse_ref):
        qb = q_ref[...]
        m = l = acc = None
        for c in range(nkc):
            s = lax.dot_general(qb, k_ref[c * tkc:(c + 1) * tkc, :], (((1,), (1,)), ((), ())), preferred_element_type=F32)
            if (c + 1) * tkc > t_real:
                cols = lax.broadcasted_iota(jnp.int32, s.shape, 1) + c * tkc
                s = jnp.where(cols < t_real, s, NEG_BIG)
            mc = jnp.max(s, axis=-1, keepdims=True)
            m_new = mc if c == 0 else jnp.maximum(m, mc)
            p = jnp.exp2(s - m_new)
            lc = jnp.sum(p, axis=-1, keepdims=True)
            pv = jnp.dot(p.astype(BF16), v_ref[c * tkc:(c + 1) * tkc, :], preferred_element_type=F32)
            if c == 0:
                l, acc = lc, pv
            else:
                alpha = jnp.exp2(m - m_new)
                l, acc = alpha * l + lc, alpha * acc + pv
            m = m_new
        o_ref[...] = acc / l
        lse_ref[...] = m + jnp.log2(l)

    return pl.pallas_call(
        body, name=name, grid=(HEADS, tp // tq),
        in_specs=[pl.BlockSpec((tq, 2 * LANE), lambda h, i: (i, h)),
                  pl.BlockSpec((tp, 2 * LANE), lambda h, i: (0, h)),
                  pl.BlockSpec((tp, LANE), lambda h, i: (0, h))],
        out_specs=[pl.BlockSpec((tq, LANE), lambda h, i: (i, h)),
                   pl.BlockSpec((None, tq, 1), lambda h, i: (h, i, 0))],
        out_shape=[jax.ShapeDtypeStruct((tp, HEADS * LANE), F32),
                   jax.ShapeDtypeStruct((HEADS, tp, 1), F32)],
        compiler_params=_cparams("parallel", "parallel"),
    )(q, k, v)


def _attn_bwd(q, k, v, do, o, lse, t_real, name):
    tp = q.shape[0]
    tq = _tile(tp, 704, 64)
    tkc = _tile(tp, 1408)
    nkc = -(-t_real // tkc)

    def body(q_ref, k_ref, v_ref, do_ref, o_ref, lse_ref, dq_ref, dk_ref, dv_ref):
        i = pl.program_id(1)

        @pl.when(i == 0)
        def _():
            dk_ref[...] = jnp.zeros_like(dk_ref)
            dv_ref[...] = jnp.zeros_like(dv_ref)

        qb = q_ref[...]
        dob = do_ref[...]
        dob16 = dob.astype(BF16)
        dol2 = (dob * LN2).astype(BF16)
        delta = jnp.sum(dob * o_ref[...], axis=-1, keepdims=True) * LN2
        lse = lse_ref[...]
        dq = None
        for c in range(nkc):
            ks = slice(c * tkc, (c + 1) * tkc)
            kb = k_ref[ks, :]
            s = lax.dot_general(qb, kb, (((1,), (1,)), ((), ())), preferred_element_type=F32)
            p = jnp.exp2(s - lse)
            if (c + 1) * tkc > t_real:
                cols = lax.broadcasted_iota(jnp.int32, s.shape, 1) + c * tkc
                p = jnp.where(cols < t_real, p, 0.0)
            dp = lax.dot_general(dol2, v_ref[ks, :], (((1,), (1,)), ((), ())), preferred_element_type=F32)
            ds = (p * (dp - delta)).astype(BF16)
            dqc = jnp.dot(ds, kb, preferred_element_type=F32)
            dq = dqc if c == 0 else dq + dqc
            dk_ref[ks, :] += lax.dot_general(ds, qb, (((0,), (0,)), ((), ())), preferred_element_type=F32)
            dv_ref[ks, :] += lax.dot_general(p.astype(BF16), dob16, (((0,), (0,)), ((), ())), preferred_element_type=F32)
        dq_ref[...] = dq

    return pl.pallas_call(
        body, name=name, grid=(HEADS, tp // tq),
        in_specs=[pl.BlockSpec((tq, 2 * LANE), lambda h, i: (i, h)),
                  pl.BlockSpec((tp, 2 * LANE), lambda h, i: (0, h)),
                  pl.BlockSpec((tp, LANE), lambda h, i: (0, h)),
                  pl.BlockSpec((tq, LANE), lambda h, i: (i, h)),
                  pl.BlockSpec((tq, LANE), lambda h, i: (i, h)),
                  pl.BlockSpec((None, tq, 1), lambda h, i: (h, i, 0))],
        out_specs=[pl.BlockSpec((tq, 2 * LANE), lambda h, i: (i, h)),
                   pl.BlockSpec((tp, 2 * LANE), lambda h, i: (0, h)),
                   pl.BlockSpec((tp, LANE), lambda h, i: (0, h))],
        out_shape=[jax.ShapeDtypeStruct((tp, HEADS * 2 * LANE), F32),
                   jax.ShapeDtypeStruct((tp, HEADS * 2 * LANE), F32),
                   jax.ShapeDtypeStruct((tp, HEADS * LANE), F32)],
        compiler_params=_cparams("parallel", "arbitrary"),
    )(q, k, v, do, o, lse)


def _shift_rows(x, s):
    tp = x.shape[0]
    return x if s % tp == 0 else pltpu.roll(x, s % tp, 0)


def _conv_fwd_val(xm, w, b, pad_left):
    acc = b + w[0:1, :] * _shift_rows(xm, pad_left)
    for k in range(1, w.shape[0]):
        acc = acc + w[k:k + 1, :] * _shift_rows(xm, pad_left - k)
    return acc


def _conv_bwd_val(dy, xm, w, pad_left, live):
    kk = w.shape[0]
    dx = w[0:1, :] * _shift_rows(dy, -pad_left)
    dws = [jnp.sum(dy * _shift_rows(xm, pad_left), axis=0, keepdims=True)]
    for k in range(1, kk):
        dx = dx + w[k:k + 1, :] * _shift_rows(dy, k - pad_left)
        dws.append(jnp.sum(dy * _shift_rows(xm, pad_left - k), axis=0, keepdims=True))
    return jnp.where(live, dx, 0.0), jnp.concatenate(dws, axis=0), jnp.sum(dy, axis=0, keepdims=True)


def _lru_conv_fwd(proj, w, b, t_real, name):
    def fn(x, ww, bb):
        xm = jnp.where(_row_ids(x.shape) < t_real, x, 0.0)
        return _conv_fwd_val(xm, ww, bb, 2)

    return _cols(fn, name, [Cl(proj, C_LRU_X), Cl(w), Cl(b.reshape(1, -1))], [(proj.shape[0], F32)], D_MODEL, 128)[0]


def _lru_conv_bwd(dxc, proj, w, dproj, t_real, name):
    def fn(dy, x, ww):
        live = _row_ids(x.shape) < t_real
        xm = jnp.where(live, x, 0.0)
        dym = jnp.where(live, dy, 0.0)
        return _conv_bwd_val(dym, xm, ww, 2, live)

    return _cols(fn, name, [Cl(dxc), Cl(proj, C_LRU_X), Cl(w)],
                 [Into(dproj, C_LRU_X, D_MODEL), (w.shape[0], F32), (1, F32)], D_MODEL, 128)


def _ffn_conv_act(up, w, b, t_real, name):
    def fn(g, v, wg, wv, bg, bv):
        live = _row_ids(g.shape) < t_real
        gc = _conv_fwd_val(jnp.where(live, g, 0.0), wg, bg, 1)
        vc = _conv_fwd_val(jnp.where(live, v, 0.0), wv, bv, 1)
        return _gelu(gc) * vc

    b2 = b.reshape(1, -1)
    return _cols(fn, name, [Cl(up), Cl(up, D_FF), Cl(w), Cl(w, D_FF), Cl(b2), Cl(b2, D_FF)],
                 [(up.shape[0], F32)], D_FF, 128)[0]


def _ffn_conv_act_bwd(dm, up, w, b, t_real, name):
    tp, kk = up.shape[0], w.shape[0]
    nb = D_FF // LANE
    assert nb >= 2

    def body(dm_ref, g_ref, v_ref, wg_ref, wv_ref, bg_ref, bv_ref, dup_ref, dwg_ref, dwv_ref, dbg_ref, dbv_ref, stage, sems):
        j = pl.program_id(0)
        slot = j % 2

        def copies(step, sl):
            return [pltpu.make_async_copy(stage.at[sl, half],
                                          dup_ref.at[:, pl.ds(pl.multiple_of(half * D_FF + step * LANE, LANE), LANE)],
                                          sems.at[sl, half]) for half in range(2)]

        @pl.when(j >= 2)
        def _():
            for cp in copies(j - 2, slot):
                cp.wait()

        live = _row_ids((tp, LANE)) < t_real
        gm, vm = jnp.where(live, g_ref[...], 0.0), jnp.where(live, v_ref[...], 0.0)
        gc = _conv_fwd_val(gm, wg_ref[...], bg_ref[...], 1)
        vc = _conv_fwd_val(vm, wv_ref[...], bv_ref[...], 1)
        act, dact = _gelu_and_grad(gc)
        dmm = jnp.where(live, dm_ref[...], 0.0)
        stage[slot, 0], dwg_ref[...], dbg_ref[...] = _conv_bwd_val(dmm * vc * dact, gm, wg_ref[...], 1, live)
        stage[slot, 1], dwv_ref[...], dbv_ref[...] = _conv_bwd_val(dmm * act, vm, wv_ref[...], 1, live)
        for cp in copies(j, slot):
            cp.start()

        @pl.when(j == nb - 1)
        def _():
            for cp in copies(j - 1, 1 - slot) + copies(j, slot):
                cp.wait()

    b2 = b.reshape(1, -1)
    col = lambda rows, off: pl.BlockSpec((rows, LANE), functools.partial(lambda j, o: (0, j + o), o=off))
    return pl.pallas_call(
        body, name=name, grid=(nb,),
        in_specs=[col(tp, 0), col(tp, 0), col(tp, nb), col(kk, 0), col(kk, nb), col(1, 0), col(1, nb)],
        out_specs=[pl.BlockSpec(memory_space=pl.ANY), col(kk, 0), col(kk, 0), col(1, 0), col(1, 0)],
        out_shape=[jax.ShapeDtypeStruct((tp, 2 * D_FF), F32), jax.ShapeDtypeStruct((kk, D_FF), F32),
                   jax.ShapeDtypeStruct((kk, D_FF), F32), jax.ShapeDtypeStruct((1, D_FF), F32),
                   jax.ShapeDtypeStruct((1, D_FF), F32)],
        scratch_shapes=[pltpu.VMEM((2, 2, tp, LANE), F32), pltpu.SemaphoreType.DMA((2, 2))],
        compiler_params=_cparams("arbitrary"),
    )(dm, up, up, w, w, b2, b2)


def _lru_gates_fwd(xc, wg, b4, lam, t_real, name):
    tp = xc.shape[0]
    tm = _tile(tp, 1408)

    def body(x_ref, w_ref, b_ref, lam_ref, r0_ref, r1_ref, i0_ref, i1_ref, a0_ref, a1_ref, u0_ref, u1_ref):
        x = x_ref[...]
        xb = x.astype(BF16)
        live = _row_ids(x.shape, pl.program_id(1) * tm) < t_real
        bb = b_ref[...]
        sp = _softplus_neg(lam_ref[...])
        gate = [_sigmoid(jnp.dot(xb, w_ref[k], preferred_element_type=F32) + bb[k:k + 1, :]) for k in range(4)]
        for d, (r_ref, i_ref, a_ref, u_ref) in enumerate(((r0_ref, i0_ref, a0_ref, u0_ref), (r1_ref, i1_ref, a1_ref, u1_ref))):
            r, ig = gate[d], gate[2 + d]
            a = jnp.exp(-LRU_C * r * sp[d:d + 1, :])
            r_ref[...] = r
            i_ref[...] = ig
            a_ref[...] = a
            u_ref[...] = jnp.where(live, jnp.sqrt(1.0 - a * a) * (ig * x), 0.0)

    blk = pl.BlockSpec((tm, LANE), lambda g, i: (i, g))
    return pl.pallas_call(
        body, name=name, grid=(LRU_BLOCKS, tp // tm),
        in_specs=[blk, pl.BlockSpec((None, 4, LANE, LANE), lambda g, i: (g, 0, 0, 0)),
                  pl.BlockSpec((4, LANE), lambda g, i: (0, g)), pl.BlockSpec((2, LANE), lambda g, i: (0, g))],
        out_specs=[blk] * 8,
        out_shape=[jax.ShapeDtypeStruct((tp, D_MODEL), F32)] * 8,
        compiler_params=_cparams("parallel", "parallel"),
    )(xc, wg, b4, lam)


def _lru_gates_bwd(l0, l1, da0, da1, r0, r1, i0, i1, a0, a1, xc, wg, lam, t_real, name):
    tp = xc.shape[0]
    tm = _tile(tp, 1408)

    rc = 32
    assert tm % rc == 0

    def fold(v):
        out = v[0:SUBLANE]
        for t in range(1, rc // SUBLANE):
            out = out + v[t * SUBLANE:(t + 1) * SUBLANE]
        return out

    def body(l0_ref, l1_ref, da0_ref, da1_ref, r0_ref, r1_ref, i0_ref, i1_ref, a0_ref, a1_ref, x_ref, w_ref, lam_ref,
             dx_ref, dw_ref, db_ref, dlam_ref, pre_s, dxp_s):
        i = pl.program_id(1)
        lam_ = lam_ref[...]
        sp = _softplus_neg(lam_)
        dsp_dlam = -_sigmoid(-lam_)

        def chunk(c, sums):
            r0 = pl.multiple_of(c * rc, rc)
            rows = pl.ds(r0, rc)
            x = x_ref[rows, :]
            live = _row_ids((rc, LANE), i * tm + r0) < t_real
            dxp = jnp.zeros_like(x)
            sums = list(sums)
            for d, (l_ref, da_ref, r_ref, i_ref, a_ref) in enumerate(((l0_ref, da0_ref, r0_ref, i0_ref, a0_ref),
                                                                      (l1_ref, da1_ref, r1_ref, i1_ref, a1_ref))):
                r, ig, a = r_ref[rows, :], i_ref[rows, :], a_ref[rows, :]
                du = jnp.where(live, l_ref[rows, :], 0.0)
                a2 = a * a
                rs = lax.rsqrt(1.0 - a2)
                dv = du * ((1.0 - a2) * rs)
                ds = du * (ig * x)
                dla = jnp.where(live, da_ref[rows, :] * a - ds * (a2 * rs), 0.0)
                dr = dla * (-LRU_C) * sp[d:d + 1, :]
                p_r = dr * r * (1.0 - r)
                p_i = dv * x * ig * (1.0 - ig)
                pre_s[d, rows, :] = p_r.astype(BF16)
                pre_s[2 + d, rows, :] = p_i.astype(BF16)
                dxp = dxp + dv * ig
                sums[d] = sums[d] + fold(p_r)
                sums[2 + d] = sums[2 + d] + fold(p_i)
                sums[4 + d] = sums[4 + d] + fold(dla * (-LRU_C) * r)
            dxp_s[rows, :] = dxp
            return tuple(sums)

        zero = jnp.zeros((SUBLANE, LANE), F32)
        sums = lax.fori_loop(0, tm // rc, chunk, (zero,) * 6)

        @pl.when(i == 0)
        def _():
            dw_ref[...] = jnp.zeros_like(dw_ref)
            db_ref[...] = jnp.zeros_like(db_ref)
            dlam_ref[...] = jnp.zeros_like(dlam_ref)

        xb = x_ref[...].astype(BF16)
        dx = dxp_s[...]
        for k in range(4):
            pk = pre_s[k]
            dx = dx + lax.dot_general(pk, w_ref[k], (((1,), (1,)), ((), ())), preferred_element_type=F32)
            dw_ref[k] += lax.dot_general(xb, pk, (((0,), (0,)), ((), ())), preferred_element_type=F32)
        db_ref[...] += jnp.concatenate([jnp.sum(sums[k], axis=0, keepdims=True) for k in range(4)], axis=0)
        dlam_ref[...] += jnp.concatenate([jnp.sum(sums[4 + d], axis=0, keepdims=True) * dsp_dlam[d:d + 1, :] for d in range(2)], axis=0)
        dx_ref[...] = dx

    blk = pl.BlockSpec((tm, LANE), lambda g, i: (i, g))
    return pl.pallas_call(
        body, name=name, grid=(LRU_BLOCKS, tp // tm),
        in_specs=[blk] * 11 + [pl.BlockSpec((None, 4, LANE, LANE), lambda g, i: (g, 0, 0, 0)),
                               pl.BlockSpec((2, LANE), lambda g, i: (0, g))],
        out_specs=[blk, pl.BlockSpec((None, 4, LANE, LANE), lambda g, i: (g, 0, 0, 0)),
                   pl.BlockSpec((4, LANE), lambda g, i: (0, g)), pl.BlockSpec((2, LANE), lambda g, i: (0, g))],
        out_shape=[jax.ShapeDtypeStruct((tp, D_MODEL), F32), jax.ShapeDtypeStruct((LRU_BLOCKS, 4, LANE, LANE), F32),
                   jax.ShapeDtypeStruct((4, D_MODEL), F32), jax.ShapeDtypeStruct((2, D_MODEL), F32)],
        scratch_shapes=[pltpu.VMEM((4, tm, LANE), BF16), pltpu.VMEM((tm, LANE), F32)],
        compiler_params=_cparams("parallel", "arbitrary"),
    )(l0, l1, da0, da1, r0, r1, i0, i1, a0, a1, xc, wg, lam)


SCAN_UNROLL = 4


def _loop_tiles(nt, step, carry):
    assert nt % SCAN_UNROLL == 0

    def trip(tt, c):
        for u in range(SCAN_UNROLL):
            c = step(tt * SCAN_UNROLL + u, c)
        return c

    return lax.fori_loop(0, nt // SCAN_UNROLL, trip, carry)


def _tile_scan(a, u, reverse):
    rows = lax.broadcasted_iota(jnp.int32, a.shape, 0)
    for s in (1, 2, 4):
        if reverse:
            keep = rows < SUBLANE - s
            a_sh, u_sh = pltpu.roll(a, SUBLANE - s, 0), pltpu.roll(u, SUBLANE - s, 0)
        else:
            keep = rows >= s
            a_sh, u_sh = pltpu.roll(a, s, 0), pltpu.roll(u, s, 0)
        u = u + a * jnp.where(keep, u_sh, 0.0)
        a = a * jnp.where(keep, a_sh, 1.0)
    return a, u


def _scan_fwd(a0, u0, a1, u1, name):
    tp, d = a0.shape
    tc = 128
    nt = tp // SUBLANE

    def body(a0_ref, u0_ref, a1_ref, u1_ref, h0_ref, h1_ref):
        def step(t, carry):
            c0, c1 = carry
            f = pl.multiple_of(t * SUBLANE, SUBLANE)
            b = pl.multiple_of((nt - 1 - t) * SUBLANE, SUBLANE)
            pa, pu = _tile_scan(a0_ref[pl.ds(f, SUBLANE), :], u0_ref[pl.ds(f, SUBLANE), :], False)
            h = pu + pa * c0
            h0_ref[pl.ds(f, SUBLANE), :] = h
            c0 = h[SUBLANE - 1:SUBLANE, :]
            pa, pu = _tile_scan(a1_ref[pl.ds(b, SUBLANE), :], u1_ref[pl.ds(b, SUBLANE), :], True)
            h = pu + pa * c1
            h1_ref[pl.ds(b, SUBLANE), :] = h
            c1 = h[0:1, :]
            return c0, c1

        z = jnp.zeros((1, tc), F32)
        _loop_tiles(nt, step, (z, z))

    blk = pl.BlockSpec((tp, tc), lambda j: (0, j))
    return pl.pallas_call(
        body, name=name, grid=(d // tc,), in_specs=[blk] * 4, out_specs=[blk] * 2,
        out_shape=[jax.ShapeDtypeStruct((tp, d), F32)] * 2,
        compiler_params=_cparams("parallel"),
    )(a0, u0, a1, u1)


def _scan_bwd(dh, a0, a1, h0, h1, name):
    tp, d = dh.shape
    tc = 128
    nt = tp // SUBLANE

    def body(dh_ref, a0_ref, a1_ref, h0_ref, h1_ref, l0_ref, l1_ref, da0_ref, da1_ref):
        rows8 = lax.broadcasted_iota(jnp.int32, (SUBLANE, tc), 0)

        def step(t, carry):
            c0, c1 = carry
            b = pl.multiple_of((nt - 1 - t) * SUBLANE, SUBLANE)
            f = pl.multiple_of(t * SUBLANE, SUBLANE)
            a = a0_ref[pl.ds(b, SUBLANE), :]
            a_next = jnp.where(rows8 < SUBLANE - 1, pltpu.roll(a, SUBLANE - 1, 0), 1.0)
            pa, pu = _tile_scan(a_next, dh_ref[pl.ds(b, SUBLANE), :], True)
            lam = pu + pa * c0
            l0_ref[pl.ds(b, SUBLANE), :] = lam
            c0 = a[0:1, :] * lam[0:1, :]
            a = a1_ref[pl.ds(f, SUBLANE), :]
            a_prev = jnp.where(rows8 >= 1, pltpu.roll(a, 1, 0), 1.0)
            pa, pu = _tile_scan(a_prev, dh_ref[pl.ds(f, SUBLANE), :], False)
            lam = pu + pa * c1
            l1_ref[pl.ds(f, SUBLANE), :] = lam
            c1 = a[SUBLANE - 1:SUBLANE, :] * lam[SUBLANE - 1:SUBLANE, :]
            return c0, c1

        z = jnp.zeros((1, tc), F32)
        _loop_tiles(nt, step, (z, z))
        rows = lax.broadcasted_iota(jnp.int32, (tp, tc), 0)
        da0_ref[...] = l0_ref[...] * jnp.where(rows >= 1, pltpu.roll(h0_ref[...], 1, 0), 0.0)
        da1_ref[...] = l1_ref[...] * jnp.where(rows < tp - 1, pltpu.roll(h1_ref[...], tp - 1, 0), 0.0)

    blk = pl.BlockSpec((tp, tc), lambda j: (0, j))
    return pl.pallas_call(
        body, name=name, grid=(d // tc,), in_specs=[blk] * 5, out_specs=[blk] * 4,
        out_shape=[jax.ShapeDtypeStruct((tp, d), F32)] * 4,
        compiler_params=_cparams("parallel"),
    )(dh, a0, a1, h0, h1)


def _gated_h(proj, h0, h1, name):
    def fn(row0, lg, x0, x1):
        return _gelu(lg) * (x0 + x1)

    return _rows(fn, name, [Rw(proj, D_MODEL, C_LRU_G // D_MODEL), Rw(h0), Rw(h1)], [(D_MODEL, F32)])[0]


def _gated_h_bwd(dgh, proj, h0, h1, dproj, name):
    def fn(row0, dg, lg, x0, x1):
        act, dact = _gelu_and_grad(lg)
        return dg * (x0 + x1) * dact, dg * act

    return _rows(fn, name, [Rw(dgh), Rw(proj, D_MODEL, C_LRU_G // D_MODEL), Rw(h0), Rw(h1)],
                 [Into(dproj, C_LRU_G, D_MODEL), (D_MODEL, F32)])


def _mix(proj, y_mla, y_lru, name):
    def fn(row0, gm, gl, ym, yl):
        return _sigmoid(gm) * ym + _sigmoid(gl) * yl

    return _rows(fn, name, [Rw(proj, D_MODEL, C_G_MLA // D_MODEL), Rw(proj, D_MODEL, C_G_LRU // D_MODEL), Rw(y_mla), Rw(y_lru)],
                 [(D_MODEL, F32)])[0]


def _mix_bwd(dz, proj, y_mla, y_lru, dproj, name):
    def fn(row0, dzb, gm, gl, ym, yl):
        sm, sl = _sigmoid(gm), _sigmoid(gl)
        dg = jnp.concatenate([dzb * ym * sm * (1.0 - sm), dzb * yl * sl * (1.0 - sl)], axis=1)
        return dzb * sm, dzb * sl, dg

    return _rows(fn, name, [Rw(dz), Rw(proj, D_MODEL, C_G_MLA // D_MODEL), Rw(proj, D_MODEL, C_G_LRU // D_MODEL),
                            Rw(y_mla), Rw(y_lru)], [(D_MODEL, F32), (D_MODEL, F32), Into(dproj, C_G_MLA, 2 * D_MODEL)])


def _layer_fwd(h, w_in, more_weights, cs, t_real, tag):
    proj = _matmul(h, w_in, tag + "proj", tb=True)
    w = dict(more_weights(0, proj), w_in=w_in)
    cqn, ckvn = _mla_norms(proj, w['q_norm'], w['kv_norm'], tag + "mla_norms")
    qext = _matmul(cqn, w['w_q'], tag + "q_up")
    kv = _matmul(ckvn, w['w_kv'], tag + "kv_up")
    qc, kc, vb = _mla_pack(qext, kv, proj, cs, tag + "mla_pack")
    o, lse = _attn_fwd(qc, kc, vb, t_real, tag + "attn_fwd")
    w.update(more_weights(1, o))
    y_mla = _matmul(o, w['w_o_mla'], tag + "o_mla")
    xc = _lru_conv_fwd(proj, w['lru_conv_w'], w['lru_conv_b'], t_real, tag + "lru_conv")
    r0, r1, i0, i1, a0, a1, u0, u1 = _lru_gates_fwd(xc, w['w_g'], w['b4'], w['lru_lambda'], t_real, tag + "lru_gates")
    h0, h1 = _scan_fwd(a0, u0, a1, u1, tag + "lru_scan")
    gh = _gated_h(proj, h0, h1, tag + "lru_gate_out")
    y_lru = _matmul(gh, w['w_o_lru'], tag + "o_lru")
    z = _mix(proj, y_mla, y_lru, tag + "mix")
    zo = _matmul(z, w['w_out'], tag + "w_out")
    hm = _ln_fwd([(DN_ALPHA, h), (1.0, zo)], w['ln1_g'], w['ln1_b'], tag + "ln1")
    w.update(more_weights(2, hm))
    up = _matmul(hm, w['w_up'], tag + "w_up", tb=True)
    m = _ffn_conv_act(up, w['ffn_conv_w'], w['ffn_conv_b'], t_real, tag + "ffn_conv")
    f = _matmul(m, w['w_down'], tag + "w_down", tk_cap=1408)
    out = _ln_fwd([(DN_ALPHA, hm), (1.0, f)], w['ln2_g'], w['ln2_b'], tag + "ln2")
    saved = dict(w=w, h=h, proj=proj, cqn=cqn, ckvn=ckvn, qc=qc, kc=kc, vb=vb, o=o, lse=lse, y_mla=y_mla, xc=xc,
                 r0=r0, r1=r1, i0=i0, i1=i1, a0=a0, a1=a1, h0=h0, h1=h1, gh=gh, y_lru=y_lru, z=z, zo=zo, hm=hm,
                 up=up, m=m, f=f)
    return out, saved


DW_MATMUL = dict(ta=True, out_dtype=BF16, tn_cap=1408, tk_cap=1408)


def _after(a, tok):
    return a if tok is None else a + tok.astype(a.dtype)


def _layer_bwd(dout_terms, s, cs, t_real, tag, emit, tok):
    w = s['w']
    g = {}
    du2, dg2, db2 = _ln_bwd(dout_terms, [(DN_ALPHA, s['hm']), (1.0, s['f'])], _after(w['ln2_g'], tok), tag + "ln2_bwd")
    g['ln2_g'], g['ln2_b'] = dg2, db2
    dm = _matmul(du2, w['w_down'], tag + "w_down_dx", tb=True)
    g['w_down'] = _matmul(s['m'], du2, tag + "w_down_dw", **DW_MATMUL)
    dup, dwg_, dwv_, dbg_, dbv_ = _ffn_conv_act_bwd(dm, s['up'], w['ffn_conv_w'], w['ffn_conv_b'], t_real, tag + "ffn_conv_bwd")
    g['ffn_conv_w'] = jnp.concatenate([dwg_, dwv_], axis=1)
    g['ffn_conv_b'] = jnp.concatenate([dbg_, dbv_], axis=1)
    dhm_mm = _matmul(dup, w['w_up'], tag + "w_up_dx", tk_cap=1408)
    g['w_up'] = _matmul(s['hm'], dup, tag + "w_up_dw", **DW_MATMUL)
    tok = emit('ffn', g)
    g = {}
    du1, dg1, db1 = _ln_bwd([(DN_ALPHA, du2), (1.0, dhm_mm)], [(DN_ALPHA, s['h']), (1.0, s['zo'])],
                            _after(w['ln1_g'], tok), tag + "ln1_bwd")
    g['ln1_g'], g['ln1_b'] = dg1, db1
    dz = _matmul(du1, w['w_out'], tag + "w_out_dx", tb=True)
    g['w_out'] = _matmul(s['z'], du1, tag + "w_out_dw", **DW_MATMUL)
    dproj = lax.empty(s['proj'].shape, F32)
    dy_mla, dy_lru, dproj = _mix_bwd(dz, s['proj'], s['y_mla'], s['y_lru'], dproj, tag + "mix_bwd")
    do = _matmul(dy_mla, w['w_o_mla'], tag + "o_mla_dx", tb=True)
    g['w_o_mla'] = _matmul(s['o'], dy_mla, tag + "o_mla_dw", **DW_MATMUL)
    dqc, dkc, dv = _attn_bwd(s['qc'], s['kc'], s['vb'], do, s['o'], s['lse'], t_real, tag + "attn_bwd")
    dqext, dkv, dkrp = _mla_unpack(dqc, dkc, dv, cs, tag + "mla_unpack")
    dcqn = _matmul(dqext, w['w_q'], tag + "q_up_dx", tb=True)
    g['w_q'] = _matmul(s['cqn'], dqext, tag + "q_up_dw", **DW_MATMUL)
    dckvn = _matmul(dkv, w['w_kv'], tag + "kv_up_dx", tb=True)
    g['w_kv'] = _matmul(s['ckvn'], dkv, tag + "kv_up_dw", **DW_MATMUL)
    dgh = _matmul(dy_lru, w['w_o_lru'], tag + "o_lru_dx", tb=True)
    g['w_o_lru'] = _matmul(s['gh'], dy_lru, tag + "o_lru_dw", **DW_MATMUL)
    dproj, dhs = _gated_h_bwd(dgh, s['proj'], s['h0'], s['h1'], dproj, tag + "lru_gate_out_bwd")
    l0, l1, da0, da1 = _scan_bwd(dhs, s['a0'], s['a1'], s['h0'], s['h1'], tag + "lru_scan_bwd")
    dxc, g['w_g'], g['b4'], g['lru_lambda'] = _lru_gates_bwd(
        l0, l1, da0, da1, s['r0'], s['r1'], s['i0'], s['i1'], s['a0'], s['a1'], s['xc'], w['w_g'], w['lru_lambda'],
        t_real, tag + "lru_gates_bwd")
    dproj, g['lru_conv_w'], g['lru_conv_b'] = _lru_conv_bwd(dxc, s['proj'], w['lru_conv_w'], dproj, t_real, tag + "lru_conv_bwd")
    tok = emit('mid', g)
    dproj, dqn, dkvn = _mla_norms_bwd(dcqn, dckvn, _after(dkrp, tok), s['proj'], w['q_norm'], w['kv_norm'], dproj,
                                      tag + "mla_norms_bwd")
    tok = emit('in', {'w_in': _matmul(s['h'], dproj, tag + "proj_dw", **DW_MATMUL), 'q_norm': dqn, 'kv_norm': dkvn})
    dh_mm = _matmul(dproj, w['w_in'], tag + "proj_dx", tk_cap=1536)
    return [(DN_ALPHA, du1), (1.0, dh_mm)], tok


def _swap_halves(a, axis=-1):
    h1, h2 = jnp.split(a, 2, axis=axis)
    return jnp.concatenate([h2, h1], axis=axis)


def _w_in_kernel(w_in_t):
    cq, ckv, kr, lg, lx, gm, gl = jnp.split(w_in_t, [256, 384, 448, 1472, 2496, 3520], axis=0)
    return jnp.concatenate([lg, lx, gm, gl, cq, ckv, kr, _swap_halves(kr, axis=0)], axis=0)


def _layer_weights(fl):
    w = {}
    if 'w_uq' in fl:
        uq = fl['w_uq']
        w['w_q'] = jnp.concatenate([uq, _swap_halves(uq[..., QK_NOPE:])], axis=-1).reshape(Q_RANK, HEADS * 2 * LANE)
        w['w_kv'] = jnp.concatenate([fl['w_uk'].reshape(KV_RANK, -1), fl['w_uv'].reshape(KV_RANK, -1)], axis=1).astype(BF16)
        w['w_g'] = jnp.moveaxis(jnp.concatenate([fl['w_rg'], fl['w_ig']], axis=0), 0, 1).astype(BF16)
        w['b4'] = jnp.concatenate([fl['b_rg'], fl['b_ig']], axis=0)
    for n in ('q_norm', 'kv_norm', 'w_o_mla', 'lru_conv_w', 'lru_conv_b', 'lru_lambda', 'w_o_lru', 'w_out', 'ln1_g',
              'ln1_b', 'w_up', 'ffn_conv_w', 'ffn_conv_b', 'w_down', 'ln2_g', 'ln2_b'):
        if n in fl:
            w[n] = fl[n]
    return w


def _layer_grads(g):
    out = {}
    if 'w_in' in g:
        lg, lx, gm, gl, cq, ckv, kr, krs = jnp.split(g['w_in'], [1024, 2048, 3072, 4096, 4352, 4480, 4544], axis=1)
        out['w_in'] = jnp.concatenate([cq, ckv, kr + _swap_halves(krs), lg, lx, gm, gl], axis=1)
    if 'w_q' in g:
        gq = g['w_q'].reshape(Q_RANK, HEADS, 2 * LANE)
        out['w_uq'] = jnp.concatenate([gq[..., :QK_NOPE], gq[..., QK_NOPE:QK_NOPE + QK_ROPE] + _swap_halves(gq[..., QK_NOPE + QK_ROPE:])], axis=-1)
    if 'w_kv' in g:
        out['w_uk'] = g['w_kv'][:, :HEADS * QK_NOPE].reshape(KV_RANK, HEADS, QK_NOPE)
        out['w_uv'] = g['w_kv'][:, HEADS * QK_NOPE:].reshape(KV_RANK, HEADS, V_HEAD)
    if 'w_g' in g:
        gg = jnp.moveaxis(g['w_g'], 1, 0)
        out['w_rg'], out['w_ig'] = gg[:2], gg[2:]
    if 'b4' in g:
        out['b_rg'], out['b_ig'] = g['b4'][:2], g['b4'][2:]
    for n in ('q_norm', 'kv_norm', 'lru_conv_b', 'ln1_g', 'ln1_b', 'ffn_conv_b', 'ln2_g', 'ln2_b'):
        if n in g:
            out[n] = g[n].reshape(-1)
    for n in ('w_o_mla', 'lru_conv_w', 'lru_lambda', 'w_o_lru', 'w_out', 'w_up', 'ffn_conv_w', 'w_down'):
        if n in g:
            out[n] = g[n]
    return out


def _rope_table(tp):
    half = QK_ROPE // 2
    inv_freq = jnp.exp(-math.log(ROPE_THETA) * jnp.arange(half, dtype=F32) / half)
    ang = jnp.arange(tp, dtype=F32)[:, None] * inv_freq[None, :]
    c, s = jnp.cos(ang), jnp.sin(ang)
    return jnp.concatenate([c, c, -s, s], axis=1)


def _local_step(x, target, meta, ln0_g, ln0_b, layer_w, t_pad, emit):
    seq = x.shape[0]
    t_real = N_META + seq
    zpad = jnp.zeros((t_pad - t_real, D_MODEL), F32)
    xin = jnp.concatenate([meta, x, zpad], axis=0)
    tgt = jnp.concatenate([jnp.zeros((N_META, D_MODEL), F32), target, zpad], axis=0)
    cs = _rope_table(t_pad)
    h = _ln_fwd([(1.0, xin)], ln0_g, ln0_b, "ln0")
    saved = []
    for l in range(DEPTH):
        w_in, rest_of_weights = layer_w[l](h)
        h, s = _layer_fwd(h, w_in, rest_of_weights, cs, t_real, "l%d_" % l)
        saved.append(s)
    dy, lossvec = _loss_head(h, tgt, t_real, "loss_head")
    terms, tok = [(1.0, dy)], None
    for l in reversed(range(DEPTH)):
        terms, tok = _layer_bwd(terms, saved[l], cs, t_real, "l%d_" % l,
                                functools.partial(lambda stage, g, l: emit(l, stage, _layer_grads(g)), l=l), tok)
    dxin, dg0, db0 = _ln_bwd(terms, [(1.0, xin)], _after(ln0_g, tok), "ln0_bwd")
    emit(None, 'head', {'meta_tokens': dxin[:N_META], 'ln0_g': dg0.reshape(-1), 'ln0_b': db0.reshape(-1), 'loss': lossvec})
    return dxin[N_META:t_real]


_HBM = pl.BlockSpec(memory_space=pltpu.HBM)
_SEM = pl.BlockSpec(memory_space=pltpu.SEMAPHORE)
_SIDE_EFFECT = pltpu.SideEffectType.DATAFLOW_SIDE_EFFECTING


def _peer_copies(src_refs, land_refs, scatters, send_sems, recv_sems):
    x, y, c = lax.axis_index("x"), lax.axis_index("y"), lax.axis_index("c")
    me = 4 * x + 2 * y + c
    copies = []
    for k in range(1, N_DEV):
        px = 1 - x if k & 4 else x
        py = 1 - y if k & 2 else y
        pc = 1 - c if k & 1 else c
        for t, (src, land) in enumerate(zip(src_refs, land_refs)):
            copies.append(pltpu.make_async_remote_copy(
                src_ref=src.at[4 * px + 2 * py + pc] if scatters[t] else src, dst_ref=land.at[me],
                send_sem=send_sems.at[7 * t + k - 1], recv_sem=recv_sems.at[7 * t + k - 1],
                device_id=(px, py, pc), device_id_type=pl.DeviceIdType.MESH))
    return me, copies


def _exchange_start(groups, name):
    flat = [it for grp in groups for it in grp]
    nt, ng = len(flat), len(groups)
    scatters = [sc for _, sc in flat]
    srcs = [pltpu.with_memory_space_constraint(a, pltpu.HBM) for a, _ in flat]
    land_shapes = [a.shape if sc else (N_DEV,) + a.shape for a, sc in flat]
    lands = [pltpu.with_memory_space_constraint(lax.empty(s, a.dtype), pltpu.HBM) for s, (a, _) in zip(land_shapes, flat)]
    bounds = [0]
    for grp in groups:
        bounds.append(bounds[-1] + len(grp))

    def body(*refs):
        src_refs, land_refs = refs[:nt], refs[nt:2 * nt]
        sem_refs = refs[2 * nt:2 * nt + 2 * ng]
        token_ref = refs[4 * nt + 2 * ng]
        for gi in range(ng):
            lo, hi = bounds[gi], bounds[gi + 1]
            _, copies = _peer_copies(src_refs[lo:hi], land_refs[lo:hi], scatters[lo:hi], sem_refs[2 * gi], sem_refs[2 * gi + 1])
            for cp in copies:
                cp.start()
        token_ref[...] = jnp.zeros_like(token_ref)

    out_shape = []
    for grp in groups:
        out_shape += [pltpu.SemaphoreType.DMA((7 * len(grp),)), pltpu.SemaphoreType.DMA((7 * len(grp),))]
    out_shape += [pltpu.HBM(a.shape, a.dtype) for a in srcs] + [pltpu.HBM(s, a.dtype) for s, a in zip(land_shapes, srcs)]
    out_shape += [jax.ShapeDtypeStruct((SUBLANE, LANE), F32)]
    res = pl.pallas_call(
        body, name=name, out_shape=out_shape,
        in_specs=[_HBM] * (2 * nt),
        out_specs=[_SEM] * (2 * ng) + [_HBM] * (2 * nt) + [pl.BlockSpec(memory_space=pltpu.VMEM)],
        input_output_aliases={t: 2 * ng + t for t in range(2 * nt)},
        compiler_params=pltpu.CompilerParams(has_side_effects=_SIDE_EFFECT),
    )(*srcs, *lands)
    sems, thru, token = res[:2 * ng], res[2 * ng:2 * ng + 2 * nt], res[-1]
    states = []
    for gi in range(ng):
        lo, hi = bounds[gi], bounds[gi + 1]
        states.append((sems[2 * gi], sems[2 * gi + 1], thru[lo:hi], thru[nt + lo:nt + hi], scatters[lo:hi]))
    return states, token[0, 0]


def _exchange_wait(state, after, name):
    send_sems, recv_sems, srcs, lands, scatters = state
    n = len(srcs)

    def body(*refs):
        _, copies = _peer_copies(refs[:n], refs[n:2 * n], scatters, refs[2 * n], refs[2 * n + 1])
        for cp in copies:
            cp.wait_send()
        for cp in copies:
            cp.wait_recv()

    res = pl.pallas_call(
        body, name=name,
        out_shape=[pltpu.HBM(a.shape, a.dtype) for a in srcs] + [pltpu.HBM(a.shape, a.dtype) for a in lands],
        in_specs=[_HBM] * (2 * n) + [_SEM, _SEM, _HBM],
        out_specs=[_HBM] * (2 * n),
        input_output_aliases={t: t for t in range(2 * n)},
        compiler_params=pltpu.CompilerParams(has_side_effects=_SIDE_EFFECT),
    )(*srcs, *lands, send_sems, recv_sems, pltpu.with_memory_space_constraint(after, pltpu.HBM))
    me = 4 * lax.axis_index("x") + 2 * lax.axis_index("y") + lax.axis_index("c")
    out = []
    for src, land, sc in zip(res[:n], res[n:], scatters):
        own = lax.dynamic_index_in_dim(src, me, 0, keepdims=True) if sc else src[None]
        out.append(lax.dynamic_update_slice_in_dim(land, own, me, 0))
    return out


def _as_rows(shape):
    return (1, shape[0]) if len(shape) == 1 else (math.prod(shape[:-1]), shape[-1])


def _sum_adamw(pieces, w, m, v, name):
    shape = w.shape
    nl = len(pieces)
    if nl > 1 and _as_rows(shape[1:])[0] % 16:
        pieces, nl = [jnp.stack(pieces, axis=1)], 1
    rows, cols = _as_rows(shape)
    rl = rows // nl
    cap = max(16, (1 << 18) // cols // 16 * 16)
    tr = _tile(rl, cap, 16)
    nb = rl // tr
    c1 = 1.0 / (1.0 - ADAM_B1 ** ADAM_STEP)
    c2 = 1.0 / (1.0 - ADAM_B2 ** ADAM_STEP)

    def body(*refs):
        p_refs = refs[:nl]
        w_ref, m_ref, v_ref, g_ref, d_ref, nm_ref, nv_ref = refs[nl:]
        li = pl.program_id(0)

        def total(p_ref):
            acc = p_ref[0].astype(F32)
            for k in range(1, N_DEV):
                acc = acc + p_ref[k].astype(F32)
            return acc

        gg = total(p_refs[0])
        for l in range(1, nl):
            gg = jnp.where(li == l, total(p_refs[l]), gg)
        nm = ADAM_B1 * m_ref[...] + (1.0 - ADAM_B1) * gg
        nv = ADAM_B2 * v_ref[...] + (1.0 - ADAM_B2) * (gg * gg)
        g_ref[...] = gg
        d_ref[...] = -ADAM_LR * ((nm * c1) / (jnp.sqrt(nv * c2) + ADAM_EPS) + ADAM_WD * w_ref[...])
        nm_ref[...] = nm
        nv_ref[...] = nv

    blk = pl.BlockSpec((tr, cols), lambda li, i: (li * nb + i, 0))
    p_specs = [pl.BlockSpec((N_DEV, tr, cols), functools.partial(lambda li, i, l: (0, jnp.where(li == l, i, 0), 0), l=l))
               for l in range(nl)]
    res = pl.pallas_call(
        body, name=name, grid=(nl, nb),
        in_specs=p_specs + [blk] * 3, out_specs=[blk] * 4,
        out_shape=[jax.ShapeDtypeStruct((rows, cols), F32)] * 4,
        compiler_params=_cparams("parallel", "parallel"),
    )(*[p.reshape(N_DEV, rl, cols) for p in pieces], *[a.reshape(rows, cols) for a in (w, m, v)])
    return [r.reshape(shape) for r in res]


def _to_shards(full, axis):
    shp = full.shape
    a = full.reshape(shp[:axis] + (N_DEV, shp[axis] // N_DEV) + shp[axis + 1:])
    return jnp.moveaxis(a, axis, 0)


def _from_shards(blocks, axis):
    a = jnp.moveaxis(blocks, 0, axis)
    shp = a.shape
    return a.reshape(shp[:axis] + (shp[axis] * shp[axis + 1],) + shp[axis + 2:])


def kernel(x, meta_tokens, ln0_g, ln0_b, w_in, q_norm, kv_norm, w_uq, w_uk, w_uv, w_o_mla, lru_conv_w, lru_conv_b, w_rg, b_rg, w_ig, b_ig, lru_lambda, w_o_lru, w_out, ln1_g, ln1_b, w_up, ffn_conv_w, ffn_conv_b, w_down, ln2_g, ln2_b, loss_target, m_meta_tokens, m_ln0_g, m_ln0_b, m_w_in, m_q_norm, m_kv_norm, m_w_uq, m_w_uk, m_w_uv, m_w_o_mla, m_lru_conv_w, m_lru_conv_b, m_w_rg, m_b_rg, m_w_ig, m_b_ig, m_lru_lambda, m_w_o_lru, m_w_out, m_ln1_g, m_ln1_b, m_w_up, m_ffn_conv_w, m_ffn_conv_b, m_w_down, m_ln2_g, m_ln2_b, v_meta_tokens, v_ln0_g, v_ln0_b, v_w_in, v_q_norm, v_kv_norm, v_w_uq, v_w_uk, v_w_uv, v_w_o_mla, v_lru_conv_w, v_lru_conv_b, v_w_rg, v_b_rg, v_w_ig, v_b_ig, v_lru_lambda, v_w_o_lru, v_w_out, v_ln1_g, v_ln1_b, v_w_up, v_ffn_conv_w, v_ffn_conv_b, v_w_down, v_ln2_g, v_ln2_b):
    args = (meta_tokens, ln0_g, ln0_b, w_in, q_norm, kv_norm, w_uq, w_uk, w_uv, w_o_mla, lru_conv_w, lru_conv_b, w_rg, b_rg, w_ig, b_ig, lru_lambda, w_o_lru, w_out, ln1_g, ln1_b, w_up, ffn_conv_w, ffn_conv_b, w_down, ln2_g, ln2_b)
    ms = (m_meta_tokens, m_ln0_g, m_ln0_b, m_w_in, m_q_norm, m_kv_norm, m_w_uq, m_w_uk, m_w_uv, m_w_o_mla, m_lru_conv_w, m_lru_conv_b, m_w_rg, m_b_rg, m_w_ig, m_b_ig, m_lru_lambda, m_w_o_lru, m_w_out, m_ln1_g, m_ln1_b, m_w_up, m_ffn_conv_w, m_ffn_conv_b, m_w_down, m_ln2_g, m_ln2_b)
    vs = (v_meta_tokens, v_ln0_g, v_ln0_b, v_w_in, v_q_norm, v_kv_norm, v_w_uq, v_w_uk, v_w_uv, v_w_o_mla, v_lru_conv_w, v_lru_conv_b, v_w_rg, v_b_rg, v_w_ig, v_b_ig, v_lru_lambda, v_w_o_lru, v_w_out, v_ln1_g, v_ln1_b, v_w_up, v_ffn_conv_w, v_ffn_conv_b, v_w_down, v_ln2_g, v_ln2_b)
    wd, md, vd = dict(zip(WEIGHTS, args)), dict(zip(WEIGHTS, ms)), dict(zip(WEIGHTS, vs))

    def shard_axis(n, l):
        return SHARD_AXIS[n] - (0 if l is None else 1)

    def shard(n, l):
        a = wd[n] if l is None else wd[n][l]
        if n in SENT_TRANSPOSED:
            a = a.T
        return a.astype(BF16) if n in BIG else a

    first = [('meta_tokens', None), ('w_in', 0)]
    staged = [[(n, 0) for n in names] for names in STAGE_WEIGHTS]
    later = [(n, 1) for n in SHARDED if n != 'meta_tokens']
    gather, token = _exchange_start([[(shard(*k), False) for k in keys] for keys in [first] + staged + [later]], "gather_start")

    def arrive(gi, keys, after, name):
        return {k: b.reshape(-1, b.shape[-1]) if k[0] in SENT_TRANSPOSED else _from_shards(b, shard_axis(*k))
                for k, b in zip(keys, _exchange_wait(gather[gi], after, name))}

    def layer_weights(got, l, names):
        fl = {n: wd[n][l] for n in names if n in REPLICATED}
        fl.update({n: a for (n, _), a in got.items() if n in names})
        return _layer_weights(fl)

    ln0_g = _after(wd['ln0_g'], token)
    got_first = arrive(0, first, ln0_g, "gather_wait_first")

    def first_layer(h):
        def more(stage, after):
            got = arrive(1 + stage, staged[stage], after, "gather_wait_l0_%d" % stage)
            return layer_weights(got, 0, STAGE_WEIGHTS[stage] + STAGE_REPLICATED[stage])
        return _w_in_kernel(got_first['w_in', 0]), more

    def second_layer(h):
        got = arrive(1 + len(staged), later, h, "gather_wait_l1")
        return _w_in_kernel(got['w_in', 1]), lambda stage, after: layer_weights(got, 1, STAGE_WEIGHTS[stage] + STAGE_REPLICATED[stage])

    sent = []
    pending = []

    def send(l, stage, grads):
        for n, g in grads.items():
            if n in SHARD_AXIS:
                g = _to_shards(g, shard_axis(n, l))
                pending.append(((n, l), (g.astype(BF16) if n in BIG else g, True)))
            else:
                pending.append(((n, l), (g.astype(BF16) if n in LARGE_REPLICATED else g, False)))
        if l == DEPTH - 1 and stage != 'in':
            return None
        (state,), tok = _exchange_start([[it for _, it in pending]], "grads_start_%s_%s" % (l, stage))
        sent.append(([k for k, _ in pending], state))
        pending.clear()
        return tok

    seq = x.shape[1]
    t_pad = -(-(N_META + seq + MIN_PAD_ROWS) // LANE) * LANE
    grad_x = _local_step(x[0], loss_target[0], got_first['meta_tokens', None], ln0_g, wd['ln0_b'],
                         [first_layer, second_layer], t_pad, send)

    pieces = {}
    for gi, (keys, state) in enumerate(sent):
        pieces.update(zip(keys, _exchange_wait(state, grad_x, "grads_wait_%d" % gi)))
    loss = jnp.sum(pieces['loss', None])
    outs = {}
    for n in WEIGHTS:
        ps = [pieces[n, None]] if (n, None) in pieces else [pieces[n, l] for l in range(DEPTH)]
        outs[n] = _sum_adamw(ps, wd[n], md[n], vd[n], "adamw_" + n)
    res = [loss, grad_x[None]]
    for k in range(4):
        res += [outs[n][k] for n in WEIGHTS]
    return tuple(res)
```

```python
import functools
import math

import jax
import jax.numpy as jnp
from jax import lax
from jax.experimental import pallas as pl
from jax.experimental.pallas import tpu as pltpu

F32 = jnp.float32
BF16 = jnp.bfloat16

N_DEV = 8
D_MODEL = 1024
N_META = 16
HEADS = 8
QK_NOPE = 128
QK_ROPE = 64
V_HEAD = 128
Q_RANK = 256
KV_RANK = 128
ROPE_THETA = 10000.0
LRU_BLOCKS = 8
LRU_C = 8.0
D_FF = 2816
DEPTH = 2
DN_ALPHA = (2.0 * DEPTH) ** 0.25
LN_EPS = 1e-5
RMS_EPS = 1e-6
LN2 = math.log(2.0)
ATT_SCALE = 1.0 / math.sqrt(QK_NOPE + QK_ROPE) / LN2
NEG_BIG = -1e30

ADAM_LR = 0.001
ADAM_B1 = 0.9
ADAM_B2 = 0.999
ADAM_EPS = 1e-08
ADAM_WD = 0.01
ADAM_STEP = 10

MIN_PAD_ROWS = 2
LANE = 128
SUBLANE = 8
VMEM_LIMIT = 56 * 1024 * 1024

PROJ_COLS = 4 * D_MODEL + Q_RANK + KV_RANK + 2 * QK_ROPE
C_LRU_G, C_LRU_X, C_G_MLA, C_G_LRU = 0, D_MODEL, 2 * D_MODEL, 3 * D_MODEL
C_CQ = 4 * D_MODEL
C_CKV = C_CQ + Q_RANK
C_KRP = C_CKV + KV_RANK

WEIGHTS = ['meta_tokens', 'ln0_g', 'ln0_b', 'w_in', 'q_norm', 'kv_norm', 'w_uq', 'w_uk', 'w_uv', 'w_o_mla',
           'lru_conv_w', 'lru_conv_b', 'w_rg', 'b_rg', 'w_ig', 'b_ig', 'lru_lambda', 'w_o_lru', 'w_out',
           'ln1_g', 'ln1_b', 'w_up', 'ffn_conv_w', 'ffn_conv_b', 'w_down', 'ln2_g', 'ln2_b']
SHARD_AXIS = {'meta_tokens': 1, 'w_in': 2, 'w_uq': 1, 'w_o_mla': 1, 'lru_conv_w': 2, 'b_rg': 2, 'b_ig': 2,
              'lru_lambda': 2, 'w_o_lru': 1, 'w_out': 1, 'w_up': 2, 'ffn_conv_w': 2, 'w_down': 1}
BIG = ['w_in', 'w_uq', 'w_o_mla', 'w_o_lru', 'w_out', 'w_up', 'w_down']
SHARDED = [n for n in WEIGHTS if n in SHARD_AXIS]
REPLICATED = [n for n in WEIGHTS if n not in SHARD_AXIS]
LARGE_REPLICATED = ['w_uk', 'w_uv', 'w_rg', 'w_ig']
SENT_TRANSPOSED = ['w_in', 'w_up']
STAGE_WEIGHTS = [['w_uq', 'lru_conv_w', 'b_rg', 'b_ig', 'lru_lambda'], ['w_o_mla', 'w_o_lru', 'w_out'], ['w_up', 'ffn_conv_w', 'w_down']]
STAGE_REPLICATED = [['q_norm', 'kv_norm', 'w_uk', 'w_uv', 'lru_conv_b', 'w_rg', 'w_ig'], ['ln1_g', 'ln1_b'], ['ffn_conv_b', 'ln2_g', 'ln2_b']]


def _cparams(*sem):
    return pltpu.CompilerParams(dimension_semantics=sem, vmem_limit_bytes=VMEM_LIMIT)


def _tile(n, cap, unit=LANE):
    best = None
    t = unit
    while t <= min(n, cap):
        if n % t == 0:
            best = t
        t += unit
    return n if best is None else best


def _sigmoid(x):
    return 1.0 / (1.0 + jnp.exp(-x))


_GELU_C = math.sqrt(2.0 / math.pi)


_GELU_A = 0.044715


def _gelu(x):
    t = jnp.tanh(x * (_GELU_C + (_GELU_C * _GELU_A) * (x * x)))
    hx = 0.5 * x
    return hx + hx * t


def _gelu_and_grad(x):
    x2 = x * x
    t = jnp.tanh(x * (_GELU_C + (_GELU_C * _GELU_A) * x2))
    hx = 0.5 * x
    dg = 0.5 + 0.5 * t + (hx * (1.0 - t * t)) * (_GELU_C + (3.0 * _GELU_C * _GELU_A) * x2)
    return hx + hx * t, dg


def _softplus_neg(lam):
    z = jnp.exp(-jnp.abs(lam))
    w = 1.0 + z
    log1p = jnp.where(w == 1.0, z, jnp.log(w) * z / (w - 1.0))
    return jnp.maximum(-lam, 0.0) + log1p


def _row_ids(shape, row0=0):
    return lax.broadcasted_iota(jnp.int32, shape, 0) + row0


def _matmul(a, b, name, ta=False, tb=False, out_dtype=F32, tm_cap=1408, tn_cap=1024, tk_cap=2048):
    if ta:
        kdim, m = a.shape
    else:
        m, kdim = a.shape
    if tb:
        n, k2 = b.shape
    else:
        k2, n = b.shape
    assert kdim == k2, (a.shape, b.shape, ta, tb)
    tm, tn, tk = _tile(m, tm_cap), _tile(n, tn_cap), _tile(kdim, tk_cap)
    nk = kdim // tk

    def body(a_ref, b_ref, o_ref, *acc):
        dn = (((0 if ta else 1,), (1 if tb else 0,)), ((), ()))
        part = lax.dot_general(a_ref[...].astype(BF16), b_ref[...].astype(BF16), dn, preferred_element_type=F32)
        if nk == 1:
            o_ref[...] = part.astype(o_ref.dtype)
            return
        acc_ref, k = acc[0], pl.program_id(2)

        @pl.when(k == 0)
        def _():
            acc_ref[...] = part

        @pl.when(k > 0)
        def _():
            acc_ref[...] += part

        @pl.when(k == nk - 1)
        def _():
            o_ref[...] = acc_ref[...].astype(o_ref.dtype)

    a_spec = pl.BlockSpec((tk, tm), lambda i, j, k: (k, i)) if ta else pl.BlockSpec((tm, tk), lambda i, j, k: (i, k))
    b_spec = pl.BlockSpec((tn, tk), lambda i, j, k: (j, k)) if tb else pl.BlockSpec((tk, tn), lambda i, j, k: (k, j))
    return pl.pallas_call(
        body, name=name,
        grid=(m // tm, n // tn, nk),
        in_specs=[a_spec, b_spec],
        out_specs=pl.BlockSpec((tm, tn), lambda i, j, k: (i, j)),
        out_shape=jax.ShapeDtypeStruct((m, n), out_dtype),
        scratch_shapes=[pltpu.VMEM((tm, tn), F32)] if nk > 1 else [],
        compiler_params=_cparams("parallel", "parallel", "arbitrary"),
    )(a, b)


class Rw:
    def __init__(self, arr, width=None, cb=0):
        self.arr, self.width, self.cb = arr, (arr.shape[1] if width is None else width), cb


class Pm:
    def __init__(self, arr):
        self.arr = arr


class Into:
    def __init__(self, arr, col0, width):
        self.arr, self.col0, self.width = arr, col0, width


def _call_with_into(body, name, grid, in_specs, operands, outs, spec_of, shape_of, extra_out_specs, extra_out_shape, sem):
    intos = [(k, o) for k, o in enumerate(outs) if isinstance(o, Into)]
    aliases = {len(operands) + n: k for n, (k, _) in enumerate(intos)}
    return pl.pallas_call(
        body, name=name, grid=grid,
        in_specs=in_specs + [pl.BlockSpec(memory_space=pl.ANY)] * len(intos),
        out_specs=[spec_of(o) for o in outs] + extra_out_specs,
        out_shape=[jax.ShapeDtypeStruct(o.arr.shape, o.arr.dtype) if isinstance(o, Into) else shape_of(o) for o in outs]
        + extra_out_shape,
        input_output_aliases=aliases,
        compiler_params=_cparams(sem),
    )(*operands, *[o.arr for _, o in intos])


def _rows(fn, name, ins, outs, accs=(), tm_cap=384):
    tp = next(o.arr.shape[0] for o in ins if isinstance(o, Rw))
    tm = _tile(tp, tm_cap)
    n_in, n_out, n_acc = len(ins), len(outs), len(accs)
    n_into = sum(isinstance(o, Into) for o in outs)

    def body(*refs):
        i = pl.program_id(0)
        res = fn(i * tm, *[r[...] for r in refs[:n_in]])
        if not isinstance(res, (tuple, list)):
            res = (res,)
        assert len(res) == n_out + n_acc, (name, len(res))
        out_refs = refs[n_in + n_into:]
        for k in range(n_out):
            out_refs[k][...] = res[k].astype(out_refs[k].dtype)
        for k in range(n_acc):
            ref = out_refs[n_out + k]

            @pl.when(i == 0)
            def _():
                ref[...] = jnp.zeros_like(ref)

            ref[...] += res[n_out + k]

    in_specs = []
    for o in ins:
        if isinstance(o, Rw):
            in_specs.append(pl.BlockSpec((tm, o.width), functools.partial(lambda i, cb: (i, cb), cb=o.cb)))
        else:
            in_specs.append(pl.BlockSpec(o.arr.shape, functools.partial(lambda i, nd: (0,) * nd, nd=o.arr.ndim)))

    def spec_of(o):
        if isinstance(o, Into):
            assert o.col0 % o.width == 0, (name, o.col0, o.width)
            return pl.BlockSpec((tm, o.width), functools.partial(lambda i, cb: (i, cb), cb=o.col0 // o.width))
        return pl.BlockSpec((tm, o[0]), lambda i: (i, 0))

    return _call_with_into(
        body, name, (tp // tm,), in_specs, [o.arr for o in ins], list(outs), spec_of,
        lambda o: jax.ShapeDtypeStruct((tp, o[0]), o[1]),
        [pl.BlockSpec(s, functools.partial(lambda i, nd: (0,) * nd, nd=len(s))) for s in accs],
        [jax.ShapeDtypeStruct(s, F32) for s in accs], "arbitrary")


class Cl:
    def __init__(self, arr, col0=0):
        self.arr, self.col0 = arr, col0


def _cols(fn, name, ins, outs, ncols, tc):
    assert ncols % tc == 0
    n_in, n_out = len(ins), len(outs)
    n_into = sum(isinstance(o, Into) for o in outs)

    def body(*refs):
        res = fn(*[r[...] for r in refs[:n_in]])
        if not isinstance(res, (tuple, list)):
            res = (res,)
        assert len(res) == n_out, (name, len(res))
        out_refs = refs[n_in + n_into:]
        for k in range(n_out):
            out_refs[k][...] = res[k].astype(out_refs[k].dtype)

    in_specs = []
    for o in ins:
        assert o.col0 % tc == 0, (name, o.col0, tc)
        in_specs.append(pl.BlockSpec((o.arr.shape[0], tc), functools.partial(lambda j, off: (0, j + off), off=o.col0 // tc)))

    def spec_of(o):
        if isinstance(o, Into):
            assert o.col0 % tc == 0 and o.width == ncols, (name, o.col0, o.width)
            return pl.BlockSpec((o.arr.shape[0], tc), functools.partial(lambda j, off: (0, j + off), off=o.col0 // tc))
        return pl.BlockSpec((o[0], tc), lambda j: (0, j))

    return _call_with_into(body, name, (ncols // tc,), in_specs, [o.arr for o in ins], list(outs), spec_of,
                           lambda o: jax.ShapeDtypeStruct((o[0], ncols), o[1]), [], [], "parallel")


def _ln_stats(u):
    mu = jnp.mean(u, axis=-1, keepdims=True)
    xc = u - mu
    var = jnp.mean(xc * xc, axis=-1, keepdims=True)
    rstd = lax.rsqrt(var + LN_EPS)
    return xc * rstd, rstd


def _ln_fwd(terms, g, b, name):
    coefs = [c for c, _ in terms]

    def fn(row0, *blk):
        xs, (gg, bb) = blk[:len(coefs)], blk[len(coefs):]
        u = sum(c * x for c, x in zip(coefs, xs))
        xhat, _ = _ln_stats(u)
        return xhat * gg + bb

    d = terms[0][1].shape[1]
    return _rows(fn, name, [Rw(x) for _, x in terms] + [Pm(g.reshape(1, d)), Pm(b.reshape(1, d))], [(d, F32)])[0]


def _ln_bwd(dy_terms, u_terms, g, name):
    dc = [c for c, _ in dy_terms]
    uc = [c for c, _ in u_terms]
    d = u_terms[0][1].shape[1]

    def fn(row0, *blk):
        dys = blk[:len(dc)]
        xs = blk[len(dc):len(dc) + len(uc)]
        gg = blk[-1]
        dy = sum(c * x for c, x in zip(dc, dys))
        u = sum(c * x for c, x in zip(uc, xs))
        xhat, rstd = _ln_stats(u)
        gdy = dy * gg
        m1 = jnp.mean(gdy, axis=-1, keepdims=True)
        m2 = jnp.mean(gdy * xhat, axis=-1, keepdims=True)
        du = rstd * (gdy - m1 - xhat * m2)
        return du, jnp.sum(dy * xhat, axis=0, keepdims=True), jnp.sum(dy, axis=0, keepdims=True)

    ins = [Rw(x) for _, x in dy_terms] + [Rw(x) for _, x in u_terms] + [Pm(g.reshape(1, d))]
    return _rows(fn, name, ins, [(d, F32)], accs=[(1, d), (1, d)])


def _loss_head(y, tgt, t_real, name):
    d = y.shape[1]

    def fn(row0, yb, tb):
        rows = _row_ids(yb.shape, row0)
        live = (rows >= N_META) & (rows < t_real)
        diff = jnp.where(live, yb - tb, 0.0)
        return diff * (1.0 / d), jnp.sum(diff * diff, axis=0, keepdims=True) * (0.5 / d)

    return _rows(fn, name, [Rw(y), Rw(tgt)], [(d, F32)], accs=[(1, d)])


def _rms(x, g):
    r = lax.rsqrt(jnp.mean(x * x, axis=-1, keepdims=True) + RMS_EPS)
    return x * r * g


def _rms_bwd(dy, x, g):
    r = lax.rsqrt(jnp.mean(x * x, axis=-1, keepdims=True) + RMS_EPS)
    gdy = dy * g
    dx = r * gdy - x * (r * r * r) * jnp.mean(gdy * x, axis=-1, keepdims=True)
    return dx, jnp.sum(dy * x * r, axis=0, keepdims=True)


def _mla_norms(proj, qn, kvn, name):
    def fn(row0, cq, ckv, g1, g2):
        return _rms(cq, g1), _rms(ckv, g2)

    return _rows(fn, name, [Rw(proj, Q_RANK, C_CQ // Q_RANK), Rw(proj, KV_RANK, C_CKV // KV_RANK),
                            Pm(qn.reshape(1, Q_RANK)), Pm(kvn.reshape(1, KV_RANK))],
                 [(Q_RANK, F32), (KV_RANK, F32)])


def _mla_norms_bwd(dcqn, dckvn, dkrp, proj, qn, kvn, dproj, name):
    def fn(row0, d1, d2, dkr, cq, ckv, g1, g2):
        dx1, dg1 = _rms_bwd(d1, cq, g1)
        dx2, dg2 = _rms_bwd(d2, ckv, g2)
        return jnp.concatenate([dx1, dx2, dkr], axis=1), dg1, dg2

    return _rows(fn, name, [Rw(dcqn), Rw(dckvn), Rw(dkrp), Rw(proj, Q_RANK, C_CQ // Q_RANK), Rw(proj, KV_RANK, C_CKV // KV_RANK),
                            Pm(qn.reshape(1, Q_RANK)), Pm(kvn.reshape(1, KV_RANK))],
                 [Into(dproj, C_CQ, PROJ_COLS - C_CQ)], accs=[(1, Q_RANK), (1, KV_RANK)])


def _fold_rope(z):
    return z + pltpu.roll(z, QK_ROPE, 1)


def _mla_pack(qext, kv, proj, cs, name):
    tp = qext.shape[0]
    tm = _tile(tp, 384)
    hw, nope_all = 2 * LANE, HEADS * QK_NOPE

    def body(q_ref, kv_ref, kr_ref, cs_ref, qo_ref, ko_ref, vo_ref):
        cs_ = cs_ref[...]
        low = lax.broadcasted_iota(jnp.int32, cs_.shape, 1) < QK_ROPE
        kr = _fold_rope(kr_ref[...] * cs_).astype(BF16)
        for h in range(HEADS):
            qr = jnp.where(low, _fold_rope(q_ref[:, h * hw + QK_NOPE:(h + 1) * hw] * cs_), 0.0)
            qo_ref[:, h * hw:h * hw + QK_NOPE] = (q_ref[:, h * hw:h * hw + QK_NOPE] * ATT_SCALE).astype(BF16)
            qo_ref[:, h * hw + QK_NOPE:(h + 1) * hw] = (qr * ATT_SCALE).astype(BF16)
            ko_ref[:, h * hw:h * hw + QK_NOPE] = kv_ref[:, h * QK_NOPE:(h + 1) * QK_NOPE].astype(BF16)
            ko_ref[:, h * hw + QK_NOPE:(h + 1) * hw] = kr
        vo_ref[...] = kv_ref[:, nope_all:].astype(BF16)

    row = lambda w: pl.BlockSpec((tm, w), lambda i: (i, 0))
    return pl.pallas_call(
        body, name=name, grid=(tp // tm,),
        in_specs=[row(HEADS * hw), row(2 * nope_all), pl.BlockSpec((tm, LANE), lambda i: (i, C_KRP // LANE)), row(LANE)],
        out_specs=[row(HEADS * hw), row(HEADS * hw), row(nope_all)],
        out_shape=[jax.ShapeDtypeStruct((tp, HEADS * hw), BF16),
                   jax.ShapeDtypeStruct((tp, HEADS * hw), BF16),
                   jax.ShapeDtypeStruct((tp, nope_all), BF16)],
        compiler_params=_cparams("parallel"),
    )(qext, kv, proj, cs)


def _mla_unpack(dq, dk, dv, cs, name):
    tp = dq.shape[0]
    tm = _tile(tp, 384)
    hw, nope_all = 2 * LANE, HEADS * QK_NOPE

    def body(dq_ref, dk_ref, dv_ref, cs_ref, dqe_ref, dkv_ref, dkr_ref):
        cs_ = cs_ref[...]
        low = lax.broadcasted_iota(jnp.int32, cs_.shape, 1) < QK_ROPE
        dkr = None
        for h in range(HEADS):
            dqe_ref[:, h * hw:h * hw + QK_NOPE] = dq_ref[:, h * hw:h * hw + QK_NOPE] * ATT_SCALE
            dqr = jnp.where(low, dq_ref[:, h * hw + QK_NOPE:(h + 1) * hw], 0.0) * ATT_SCALE
            dqe_ref[:, h * hw + QK_NOPE:(h + 1) * hw] = _fold_rope(dqr) * cs_
            dkv_ref[:, h * QK_NOPE:(h + 1) * QK_NOPE] = dk_ref[:, h * hw:h * hw + QK_NOPE]
            part = jnp.where(low, dk_ref[:, h * hw + QK_NOPE:(h + 1) * hw], 0.0)
            dkr = part if h == 0 else dkr + part
        dkv_ref[:, nope_all:] = dv_ref[...]
        dkr_ref[...] = _fold_rope(dkr) * cs_

    row = lambda w: pl.BlockSpec((tm, w), lambda i: (i, 0))
    return pl.pallas_call(
        body, name=name, grid=(tp // tm,),
        in_specs=[row(HEADS * hw), row(HEADS * hw), row(nope_all), row(LANE)],
        out_specs=[row(HEADS * hw), row(2 * nope_all), row(LANE)],
        out_shape=[jax.ShapeDtypeStruct((tp, HEADS * hw), F32),
                   jax.ShapeDtypeStruct((tp, 2 * nope_all), F32),
                   jax.ShapeDtypeStruct((tp, LANE), F32)],
        compiler_params=_cparams("parallel"),
    )(dq, dk, dv, cs)


def _attn_fwd(q, k, v, t_real, name):
    tp = q.shape[0]
    tq = _tile(tp, 1408)
    tkc = _tile(tp, 1408)
    nkc = -(-t_real // tkc)

    def body(q_ref, k_ref, v_ref, o_ref, lse_ref):
        qb = q_ref[...]
        m = l = acc = None
        for c in range(nkc):
            s = lax.dot_general(qb, k_ref[c * tkc:(c + 1) * tkc, :], (((1,), (1,)), ((), ())), preferred_element_type=F32)
            if (c + 1) * tkc > t_real:
                cols = lax.broadcasted_iota(jnp.int32, s.shape, 1) + c * tkc
                s = jnp.where(cols < t_real, s, NEG_BIG)
            mc = jnp.max(s, axis=-1, keepdims=True)
            m_new = mc if c == 0 else jnp.maximum(m, mc)
            p = jnp.exp2(s - m_new)
            lc = jnp.sum(p, axis=-1, keepdims=True)
            pv = jnp.dot(p.astype(BF16), v_ref[c * tkc:(c + 1) * tkc, :], preferred_element_type=F32)
            if c == 0:
                l, acc = lc, pv
            else:
                alpha = jnp.exp2(m - m_new)
                l, acc = alpha * l + lc, alpha * acc + pv
            m = m_new
        o_ref[...] = acc / l
        lse_ref[...] = m + jnp.log2(l)

    return pl.pallas_call(
        body, name=name, grid=(HEADS, tp // tq),
        in_specs=[pl.BlockSpec((tq, 2 * LANE), lambda h, i: (i, h)),
                  pl.BlockSpec((tp, 2 * LANE), lambda h, i: (0, h)),
                  pl.BlockSpec((tp, LANE), lambda h, i: (0, h))],
        out_specs=[pl.BlockSpec((tq, LANE), lambda h, i: (i, h)),
                   pl.BlockSpec((None, tq, 1), lambda h, i: (h, i, 0))],
        out_shape=[jax.ShapeDtypeStruct((tp, HEADS * LANE), F32),
                   jax.ShapeDtypeStruct((HEADS, tp, 1), F32)],
        compiler_params=_cparams("parallel", "parallel"),
    )(q, k, v)


def _attn_bwd(q, k, v, do, o, lse, t_real, name):
    tp = q.shape[0]
    tq = _tile(tp, 704, 64)
    tkc = _tile(tp, 1408)
    nkc = -(-t_real // tkc)

    def body(q_ref, k_ref, v_ref, do_ref, o_ref, lse_ref, dq_ref, dk_ref, dv_ref):
        i = pl.program_id(1)

        @pl.when(i == 0)
        def _():
            dk_ref[...] = jnp.zeros_like(dk_ref)
            dv_ref[...] = jnp.zeros_like(dv_ref)

        qb = q_ref[...]
        dob = do_ref[...]
        dob16 = dob.astype(BF16)
        dol2 = (dob * LN2).astype(BF16)
        delta = jnp.sum(dob * o_ref[...], axis=-1, keepdims=True) * LN2
        lse = lse_ref[...]
        dq = None
        for c in range(nkc):
            ks = slice(c * tkc, (c + 1) * tkc)
            kb = k_ref[ks, :]
            s = lax.dot_general(qb, kb, (((1,), (1,)), ((), ())), preferred_element_type=F32)
            p = jnp.exp2(s - lse)
            if (c + 1) * tkc > t_real:
                cols = lax.broadcasted_iota(jnp.int32, s.shape, 1) + c * tkc
                p = jnp.where(cols < t_real, p, 0.0)
            dp = lax.dot_general(dol2, v_ref[ks, :], (((1,), (1,)), ((), ())), preferred_element_type=F32)
            ds = (p * (dp - delta)).astype(BF16)
            dqc = jnp.dot(ds, kb, preferred_element_type=F32)
            dq = dqc if c == 0 else dq + dqc
            dk_ref[ks, :] += lax.dot_general(ds, qb, (((0,), (0,)), ((), ())), preferred_element_type=F32)
            dv_ref[ks, :] += lax.dot_general(p.astype(BF16), dob16, (((0,), (0,)), ((), ())), preferred_element_type=F32)
        dq_ref[...] = dq

    return pl.pallas_call(
        body, name=name, grid=(HEADS, tp // tq),
        in_specs=[pl.BlockSpec((tq, 2 * LANE), lambda h, i: (i, h)),
                  pl.BlockSpec((tp, 2 * LANE), lambda h, i: (0, h)),
                  pl.BlockSpec((tp, LANE), lambda h, i: (0, h)),
                  pl.BlockSpec((tq, LANE), lambda h, i: (i, h)),
                  pl.BlockSpec((tq, LANE), lambda h, i: (i, h)),
                  pl.BlockSpec((None, tq, 1), lambda h, i: (h, i, 0))],
        out_specs=[pl.BlockSpec((tq, 2 * LANE), lambda h, i: (i, h)),
                   pl.BlockSpec((tp, 2 * LANE), lambda h, i: (0, h)),
                   pl.BlockSpec((tp, LANE), lambda h, i: (0, h))],
        out_shape=[jax.ShapeDtypeStruct((tp, HEADS * 2 * LANE), F32),
                   jax.ShapeDtypeStruct((tp, HEADS * 2 * LANE), F32),
                   jax.ShapeDtypeStruct((tp, HEADS * LANE), F32)],
        compiler_params=_cparams("parallel", "arbitrary"),
    )(q, k, v, do, o, lse)


def _shift_rows(x, s):
    tp = x.shape[0]
    return x if s % tp == 0 else pltpu.roll(x, s % tp, 0)


def _conv_fwd_val(xm, w, b, pad_left):
    acc = b + w[0:1, :] * _shift_rows(xm, pad_left)
    for k in range(1, w.shape[0]):
        acc = acc + w[k:k + 1, :] * _shift_rows(xm, pad_left - k)
    return acc


def _conv_bwd_val(dy, xm, w, pad_left, live):
    kk = w.shape[0]
    dx = w[0:1, :] * _shift_rows(dy, -pad_left)
    dws = [jnp.sum(dy * _shift_rows(xm, pad_left), axis=0, keepdims=True)]
    for k in range(1, kk):
        dx = dx + w[k:k + 1, :] * _shift_rows(dy, k - pad_left)
        dws.append(jnp.sum(dy * _shift_rows(xm, pad_left - k), axis=0, keepdims=True))
    return jnp.where(live, dx, 0.0), jnp.concatenate(dws, axis=0), jnp.sum(dy, axis=0, keepdims=True)


def _lru_conv_fwd(proj, w, b, t_real, name):
    def fn(x, ww, bb):
        xm = jnp.where(_row_ids(x.shape) < t_real, x, 0.0)
        return _conv_fwd_val(xm, ww, bb, 2)

    return _cols(fn, name, [Cl(proj, C_LRU_X), Cl(w), Cl(b.reshape(1, -1))], [(proj.shape[0], F32)], D_MODEL, 128)[0]


def _lru_conv_bwd(dxc, proj, w, dproj, t_real, name):
    def fn(dy, x, ww):
        live = _row_ids(x.shape) < t_real
        xm = jnp.where(live, x, 0.0)
        dym = jnp.where(live, dy, 0.0)
        return _conv_bwd_val(dym, xm, ww, 2, live)

    return _cols(fn, name, [Cl(dxc), Cl(proj, C_LRU_X), Cl(w)],
                 [Into(dproj, C_LRU_X, D_MODEL), (w.shape[0], F32), (1, F32)], D_MODEL, 128)


def _ffn_conv_act(up, w, b, t_real, name):
    def fn(g, v, wg, wv, bg, bv):
        live = _row_ids(g.shape) < t_real
        gc = _conv_fwd_val(jnp.where(live, g, 0.0), wg, bg, 1)
        vc = _conv_fwd_val(jnp.where(live, v, 0.0), wv, bv, 1)
        return _gelu(gc) * vc

    b2 = b.reshape(1, -1)
    return _cols(fn, name, [Cl(up), Cl(up, D_FF), Cl(w), Cl(w, D_FF), Cl(b2), Cl(b2, D_FF)],
                 [(up.shape[0], F32)], D_FF, 128)[0]


def _ffn_conv_act_bwd(dm, up, w, b, t_real, name):
    tp, kk = up.shape[0], w.shape[0]
    nb = D_FF // LANE
    assert nb >= 2

    def body(dm_ref, g_ref, v_ref, wg_ref, wv_ref, bg_ref, bv_ref, dup_ref, dwg_ref, dwv_ref, dbg_ref, dbv_ref, stage, sems):
        j = pl.program_id(0)
        slot = j % 2

        def copies(step, sl):
            return [pltpu.make_async_copy(stage.at[sl, half],
                                          dup_ref.at[:, pl.ds(pl.multiple_of(half * D_FF + step * LANE, LANE), LANE)],
                                          sems.at[sl, half]) for half in range(2)]

        @pl.when(j >= 2)
        def _():
            for cp in copies(j - 2, slot):
                cp.wait()

        live = _row_ids((tp, LANE)) < t_real
        gm, vm = jnp.where(live, g_ref[...], 0.0), jnp.where(live, v_ref[...], 0.0)
        gc = _conv_fwd_val(gm, wg_ref[...], bg_ref[...], 1)
        vc = _conv_fwd_val(vm, wv_ref[...], bv_ref[...], 1)
        act, dact = _gelu_and_grad(gc)
        dmm = jnp.where(live, dm_ref[...], 0.0)
        stage[slot, 0], dwg_ref[...], dbg_ref[...] = _conv_bwd_val(dmm * vc * dact, gm, wg_ref[...], 1, live)
        stage[slot, 1], dwv_ref[...], dbv_ref[...] = _conv_bwd_val(dmm * act, vm, wv_ref[...], 1, live)
        for cp in copies(j, slot):
            cp.start()

        @pl.when(j == nb - 1)
        def _():
            for cp in copies(j - 1, 1 - slot) + copies(j, slot):
                cp.wait()

    b2 = b.reshape(1, -1)
    col = lambda rows, off: pl.BlockSpec((rows, LANE), functools.partial(lambda j, o: (0, j + o), o=off))
    return pl.pallas_call(
        body, name=name, grid=(nb,),
        in_specs=[col(tp, 0), col(tp, 0), col(tp, nb), col(kk, 0), col(kk, nb), col(1, 0), col(1, nb)],
        out_specs=[pl.BlockSpec(memory_space=pl.ANY), col(kk, 0), col(kk, 0), col(1, 0), col(1, 0)],
        out_shape=[jax.ShapeDtypeStruct((tp, 2 * D_FF), F32), jax.ShapeDtypeStruct((kk, D_FF), F32),
                   jax.ShapeDtypeStruct((kk, D_FF), F32), jax.ShapeDtypeStruct((1, D_FF), F32),
                   jax.ShapeDtypeStruct((1, D_FF), F32)],
        scratch_shapes=[pltpu.VMEM((2, 2, tp, LANE), F32), pltpu.SemaphoreType.DMA((2, 2))],
        compiler_params=_cparams("arbitrary"),
    )(dm, up, up, w, w, b2, b2)


def _lru_gates_fwd(xc, wg, b4, lam, t_real, name):
    tp = xc.shape[0]
    tm = _tile(tp, 1408)

    def body(x_ref, w_ref, b_ref, lam_ref, r0_ref, r1_ref, i0_ref, i1_ref, a0_ref, a1_ref, u0_ref, u1_ref):
        x = x_ref[...]
        xb = x.astype(BF16)
        live = _row_ids(x.shape, pl.program_id(1) * tm) < t_real
        bb = b_ref[...]
        sp = _softplus_neg(lam_ref[...])
        gate = [_sigmoid(jnp.dot(xb, w_ref[k], preferred_element_type=F32) + bb[k:k + 1, :]) for k in range(4)]
        for d, (r_ref, i_ref, a_ref, u_ref) in enumerate(((r0_ref, i0_ref, a0_ref, u0_ref), (r1_ref, i1_ref, a1_ref, u1_ref))):
            r, ig = gate[d], gate[2 + d]
            a = jnp.exp(-LRU_C * r * sp[d:d + 1, :])
            r_ref[...] = r
            i_ref[...] = ig
            a_ref[...] = a
            u_ref[...] = jnp.where(live, jnp.sqrt(1.0 - a * a) * (ig * x), 0.0)

    blk = pl.BlockSpec((tm, LANE), lambda g, i: (i, g))
    return pl.pallas_call(
        body, name=name, grid=(LRU_BLOCKS, tp // tm),
        in_specs=[blk, pl.BlockSpec((None, 4, LANE, LANE), lambda g, i: (g, 0, 0, 0)),
                  pl.BlockSpec((4, LANE), lambda g, i: (0, g)), pl.BlockSpec((2, LANE), lambda g, i: (0, g))],
        out_specs=[blk] * 8,
        out_shape=[jax.ShapeDtypeStruct((tp, D_MODEL), F32)] * 8,
        compiler_params=_cparams("parallel", "parallel"),
    )(xc, wg, b4, lam)


def _lru_gates_bwd(l0, l1, da0, da1, r0, r1, i0, i1, a0, a1, xc, wg, lam, t_real, name):
    tp = xc.shape[0]
    tm = _tile(tp, 1408)

    rc = 32
    assert tm % rc == 0

    def fold(v):
        out = v[0:SUBLANE]
        for t in range(1, rc // SUBLANE):
            out = out + v[t * SUBLANE:(t + 1) * SUBLANE]
        return out

    def body(l0_ref, l1_ref, da0_ref, da1_ref, r0_ref, r1_ref, i0_ref, i1_ref, a0_ref, a1_ref, x_ref, w_ref, lam_ref,
             dx_ref, dw_ref, db_ref, dlam_ref, pre_s, dxp_s):
        i = pl.program_id(1)
        lam_ = lam_ref[...]
        sp = _softplus_neg(lam_)
        dsp_dlam = -_sigmoid(-lam_)

        def chunk(c, sums):
            r0 = pl.multiple_of(c * rc, rc)
            rows = pl.ds(r0, rc)
            x = x_ref[rows, :]
            live = _row_ids((rc, LANE), i * tm + r0) < t_real
            dxp = jnp.zeros_like(x)
            sums = list(sums)
            for d, (l_ref, da_ref, r_ref, i_ref, a_ref) in enumerate(((l0_ref, da0_ref, r0_ref, i0_ref, a0_ref),
                                                                      (l1_ref, da1_ref, r1_ref, i1_ref, a1_ref))):
                r, ig, a = r_ref[rows, :], i_ref[rows, :], a_ref[rows, :]
                du = jnp.where(live, l_ref[rows, :], 0.0)
                a2 = a * a
                rs = lax.rsqrt(1.0 - a2)
                dv = du * ((1.0 - a2) * rs)
                ds = du * (ig * x)
                dla = jnp.where(live, da_ref[rows, :] * a - ds * (a2 * rs), 0.0)
                dr = dla * (-LRU_C) * sp[d:d + 1, :]
                p_r = dr * r * (1.0 - r)
                p_i = dv * x * ig * (1.0 - ig)
                pre_s[d, rows, :] = p_r.astype(BF16)
                pre_s[2 + d, rows, :] = p_i.astype(BF16)
                dxp = dxp + dv * ig
                sums[d] = sums[d] + fold(p_r)
                sums[2 + d] = sums[2 + d] + fold(p_i)
                sums[4 + d] = sums[4 + d] + fold(dla * (-LRU_C) * r)
            dxp_s[rows, :] = dxp
            return tuple(sums)

        zero = jnp.zeros((SUBLANE, LANE), F32)
        sums = lax.fori_loop(0, tm // rc, chunk, (zero,) * 6)

        @pl.when(i == 0)
        def _():
            dw_ref[...] = jnp.zeros_like(dw_ref)
            db_ref[...] = jnp.zeros_like(db_ref)
            dlam_ref[...] = jnp.zeros_like(dlam_ref)

        xb = x_ref[...].astype(BF16)
        dx = dxp_s[...]
        for k in range(4):
            pk = pre_s[k]
            dx = dx + lax.dot_general(pk, w_ref[k], (((1,), (1,)), ((), ())), preferred_element_type=F32)
            dw_ref[k] += lax.dot_general(xb, pk, (((0,), (0,)), ((), ())), preferred_element_type=F32)
        db_ref[...] += jnp.concatenate([jnp.sum(sums[k], axis=0, keepdims=True) for k in range(4)], axis=0)
        dlam_ref[...] += jnp.concatenate([jnp.sum(sums[4 + d], axis=0, keepdims=True) * dsp_dlam[d:d + 1, :] for d in range(2)], axis=0)
        dx_ref[...] = dx

    blk = pl.BlockSpec((tm, LANE), lambda g, i: (i, g))
    return pl.pallas_call(
        body, name=name, grid=(LRU_BLOCKS, tp // tm),
        in_specs=[blk] * 11 + [pl.BlockSpec((None, 4, LANE, LANE), lambda g, i: (g, 0, 0, 0)),
                               pl.BlockSpec((2, LANE), lambda g, i: (0, g))],
        out_specs=[blk, pl.BlockSpec((None, 4, LANE, LANE), lambda g, i: (g, 0, 0, 0)),
                   pl.BlockSpec((4, LANE), lambda g, i: (0, g)), pl.BlockSpec((2, LANE), lambda g, i: (0, g))],
        out_shape=[jax.ShapeDtypeStruct((tp, D_MODEL), F32), jax.ShapeDtypeStruct((LRU_BLOCKS, 4, LANE, LANE), F32),
                   jax.ShapeDtypeStruct((4, D_MODEL), F32), jax.ShapeDtypeStruct((2, D_MODEL), F32)],
        scratch_shapes=[pltpu.VMEM((4, tm, LANE), BF16), pltpu.VMEM((tm, LANE), F32)],
        compiler_params=_cparams("parallel", "arbitrary"),
    )(l0, l1, da0, da1, r0, r1, i0, i1, a0, a1, xc, wg, lam)


SCAN_UNROLL = 4


def _loop_tiles(nt, step, carry):
    assert nt % SCAN_UNROLL == 0

    def trip(tt, c):
        for u in range(SCAN_UNROLL):
            c = step(tt * SCAN_UNROLL + u, c)
        return c

    return lax.fori_loop(0, nt // SCAN_UNROLL, trip, carry)


def _tile_scan(a, u, reverse):
    rows = lax.broadcasted_iota(jnp.int32, a.shape, 0)
    for s in (1, 2, 4):
        if reverse:
            keep = rows < SUBLANE - s
            a_sh, u_sh = pltpu.roll(a, SUBLANE - s, 0), pltpu.roll(u, SUBLANE - s, 0)
        else:
            keep = rows >= s
            a_sh, u_sh = pltpu.roll(a, s, 0), pltpu.roll(u, s, 0)
        u = u + a * jnp.where(keep, u_sh, 0.0)
        a = a * jnp.where(keep, a_sh, 1.0)
    return a, u


def _scan_fwd(a0, u0, a1, u1, name):
    tp, d = a0.shape
    tc = 128
    nt = tp // SUBLANE

    def body(a0_ref, u0_ref, a1_ref, u1_ref, h0_ref, h1_ref):
        def step(t, carry):
            c0, c1 = carry
            f = pl.multiple_of(t * SUBLANE, SUBLANE)
            b = pl.multiple_of((nt - 1 - t) * SUBLANE, SUBLANE)
            pa, pu = _tile_scan(a0_ref[pl.ds(f, SUBLANE), :], u0_ref[pl.ds(f, SUBLANE), :], False)
            h = pu + pa * c0
            h0_ref[pl.ds(f, SUBLANE), :] = h
            c0 = h[SUBLANE - 1:SUBLANE, :]
            pa, pu = _tile_scan(a1_ref[pl.ds(b, SUBLANE), :], u1_ref[pl.ds(b, SUBLANE), :], True)
            h = pu + pa * c1
            h1_ref[pl.ds(b, SUBLANE), :] = h
            c1 = h[0:1, :]
            return c0, c1

        z = jnp.zeros((1, tc), F32)
        _loop_tiles(nt, step, (z, z))

    blk = pl.BlockSpec((tp, tc), lambda j: (0, j))
    return pl.pallas_call(
        body, name=name, grid=(d // tc,), in_specs=[blk] * 4, out_specs=[blk] * 2,
        out_shape=[jax.ShapeDtypeStruct((tp, d), F32)] * 2,
        compiler_params=_cparams("parallel"),
    )(a0, u0, a1, u1)


def _scan_bwd(dh, a0, a1, h0, h1, name):
    tp, d = dh.shape
    tc = 128
    nt = tp // SUBLANE

    def body(dh_ref, a0_ref, a1_ref, h0_ref, h1_ref, l0_ref, l1_ref, da0_ref, da1_ref):
        rows8 = lax.broadcasted_iota(jnp.int32, (SUBLANE, tc), 0)

        def step(t, carry):
            c0, c1 = carry
            b = pl.multiple_of((nt - 1 - t) * SUBLANE, SUBLANE)
            f = pl.multiple_of(t * SUBLANE, SUBLANE)
            a = a0_ref[pl.ds(b, SUBLANE), :]
            a_next = jnp.where(rows8 < SUBLANE - 1, pltpu.roll(a, SUBLANE - 1, 0), 1.0)
            pa, pu = _tile_scan(a_next, dh_ref[pl.ds(b, SUBLANE), :], True)
            lam = pu + pa * c0
            l0_ref[pl.ds(b, SUBLANE), :] = lam
            c0 = a[0:1, :] * lam[0:1, :]
            a = a1_ref[pl.ds(f, SUBLANE), :]
            a_prev = jnp.where(rows8 >= 1, pltpu.roll(a, 1, 0), 1.0)
            pa, pu = _tile_scan(a_prev, dh_ref[pl.ds(f, SUBLANE), :], False)
            lam = pu + pa * c1
            l1_ref[pl.ds(f, SUBLANE), :] = lam
            c1 = a[SUBLANE - 1:SUBLANE, :] * lam[SUBLANE - 1:SUBLANE, :]
            return c0, c1

        z = jnp.zeros((1, tc), F32)
        _loop_tiles(nt, step, (z, z))
        rows = lax.broadcasted_iota(jnp.int32, (tp, tc), 0)
        da0_ref[...] = l0_ref[...] * jnp.where(rows >= 1, pltpu.roll(h0_ref[...], 1, 0), 0.0)
        da1_ref[...] = l1_ref[...] * jnp.where(rows < tp - 1, pltpu.roll(h1_ref[...], tp - 1, 0), 0.0)

    blk = pl.BlockSpec((tp, tc), lambda j: (0, j))
    return pl.pallas_call(
        body, name=name, grid=(d // tc,), in_specs=[blk] * 5, out_specs=[blk] * 4,
        out_shape=[jax.ShapeDtypeStruct((tp, d), F32)] * 4,
        compiler_params=_cparams("parallel"),
    )(dh, a0, a1, h0, h1)


def _gated_h(proj, h0, h1, name):
    def fn(row0, lg, x0, x1):
        return _gelu(lg) * (x0 + x1)

    return _rows(fn, name, [Rw(proj, D_MODEL, C_LRU_G // D_MODEL), Rw(h0), Rw(h1)], [(D_MODEL, F32)])[0]


def _gated_h_bwd(dgh, proj, h0, h1, dproj, name):
    def fn(row0, dg, lg, x0, x1):
        act, dact = _gelu_and_grad(lg)
        return dg * (x0 + x1) * dact, dg * act

    return _rows(fn, name, [Rw(dgh), Rw(proj, D_MODEL, C_LRU_G // D_MODEL), Rw(h0), Rw(h1)],
                 [Into(dproj, C_LRU_G, D_MODEL), (D_MODEL, F32)])


def _mix(proj, y_mla, y_lru, name):
    def fn(row0, gm, gl, ym, yl):
        return _sigmoid(gm) * ym + _sigmoid(gl) * yl

    return _rows(fn, name, [Rw(proj, D_MODEL, C_G_MLA // D_MODEL), Rw(proj, D_MODEL, C_G_LRU // D_MODEL), Rw(y_mla), Rw(y_lru)],
                 [(D_MODEL, F32)])[0]


def _mix_bwd(dz, proj, y_mla, y_lru, dproj, name):
    def fn(row0, dzb, gm, gl, ym, yl):
        sm, sl = _sigmoid(gm), _sigmoid(gl)
        dg = jnp.concatenate([dzb * ym * sm * (1.0 - sm), dzb * yl * sl * (1.0 - sl)], axis=1)
        return dzb * sm, dzb * sl, dg

    return _rows(fn, name, [Rw(dz), Rw(proj, D_MODEL, C_G_MLA // D_MODEL), Rw(proj, D_MODEL, C_G_LRU // D_MODEL),
                            Rw(y_mla), Rw(y_lru)], [(D_MODEL, F32), (D_MODEL, F32), Into(dproj, C_G_MLA, 2 * D_MODEL)])


def _layer_fwd(h, w_in, more_weights, cs, t_real, tag):
    proj = _matmul(h, w_in, tag + "proj", tb=True)
    w = dict(more_weights(0, proj), w_in=w_in)
    cqn, ckvn = _mla_norms(proj, w['q_norm'], w['kv_norm'], tag + "mla_norms")
    qext = _matmul(cqn, w['w_q'], tag + "q_up")
    kv = _matmul(ckvn, w['w_kv'], tag + "kv_up")
    qc, kc, vb = _mla_pack(qext, kv, proj, cs, tag + "mla_pack")
    o, lse = _attn_fwd(qc, kc, vb, t_real, tag + "attn_fwd")
    w.update(more_weights(1, o))
    y_mla = _matmul(o, w['w_o_mla'], tag + "o_mla")
    xc = _lru_conv_fwd(proj, w['lru_conv_w'], w['lru_conv_b'], t_real, tag + "lru_conv")
    r0, r1, i0, i1, a0, a1, u0, u1 = _lru_gates_fwd(xc, w['w_g'], w['b4'], w['lru_lambda'], t_real, tag + "lru_gates")
    h0, h1 = _scan_fwd(a0, u0, a1, u1, tag + "lru_scan")
    gh = _gated_h(proj, h0, h1, tag + "lru_gate_out")
    y_lru = _matmul(gh, w['w_o_lru'], tag + "o_lru")
    z = _mix(proj, y_mla, y_lru, tag + "mix")
    zo = _matmul(z, w['w_out'], tag + "w_out")
    hm = _ln_fwd([(DN_ALPHA, h), (1.0, zo)], w['ln1_g'], w['ln1_b'], tag + "ln1")
    w.update(more_weights(2, hm))
    up = _matmul(hm, w['w_up'], tag + "w_up", tb=True)
    m = _ffn_conv_act(up, w['ffn_conv_w'], w['ffn_conv_b'], t_real, tag + "ffn_conv")
    f = _matmul(m, w['w_down'], tag + "w_down", tk_cap=1408)
    out = _ln_fwd([(DN_ALPHA, hm), (1.0, f)], w['ln2_g'], w['ln2_b'], tag + "ln2")
    saved = dict(w=w, h=h, proj=proj, cqn=cqn, ckvn=ckvn, qc=qc, kc=kc, vb=vb, o=o, lse=lse, y_mla=y_mla, xc=xc,
                 r0=r0, r1=r1, i0=i0, i1=i1, a0=a0, a1=a1, h0=h0, h1=h1, gh=gh, y_lru=y_lru, z=z, zo=zo, hm=hm,
                 up=up, m=m, f=f)
    return out, saved


DW_MATMUL = dict(ta=True, out_dtype=BF16, tn_cap=1408, tk_cap=1408)


def _after(a, tok):
    return a if tok is None else a + tok.astype(a.dtype)


def _layer_bwd(dout_terms, s, cs, t_real, tag, emit, tok):
    w = s['w']
    g = {}
    du2, dg2, db2 = _ln_bwd(dout_terms, [(DN_ALPHA, s['hm']), (1.0, s['f'])], _after(w['ln2_g'], tok), tag + "ln2_bwd")
    g['ln2_g'], g['ln2_b'] = dg2, db2
    dm = _matmul(du2, w['w_down'], tag + "w_down_dx", tb=True)
    g['w_down'] = _matmul(s['m'], du2, tag + "w_down_dw", **DW_MATMUL)
    dup, dwg_, dwv_, dbg_, dbv_ = _ffn_conv_act_bwd(dm, s['up'], w['ffn_conv_w'], w['ffn_conv_b'], t_real, tag + "ffn_conv_bwd")
    g['ffn_conv_w'] = jnp.concatenate([dwg_, dwv_], axis=1)
    g['ffn_conv_b'] = jnp.concatenate([dbg_, dbv_], axis=1)
    dhm_mm = _matmul(dup, w['w_up'], tag + "w_up_dx", tk_cap=1408)
    g['w_up'] = _matmul(s['hm'], dup, tag + "w_up_dw", **DW_MATMUL)
    tok = emit('ffn', g)
    g = {}
    du1, dg1, db1 = _ln_bwd([(DN_ALPHA, du2), (1.0, dhm_mm)], [(DN_ALPHA, s['h']), (1.0, s['zo'])],
                            _after(w['ln1_g'], tok), tag + "ln1_bwd")
    g['ln1_g'], g['ln1_b'] = dg1, db1
    dz = _matmul(du1, w['w_out'], tag + "w_out_dx", tb=True)
    g['w_out'] = _matmul(s['z'], du1, tag + "w_out_dw", **DW_MATMUL)
    dproj = lax.empty(s['proj'].shape, F32)
    dy_mla, dy_lru, dproj = _mix_bwd(dz, s['proj'], s['y_mla'], s['y_lru'], dproj, tag + "mix_bwd")
    do = _matmul(dy_mla, w['w_o_mla'], tag + "o_mla_dx", tb=True)
    g['w_o_mla'] = _matmul(s['o'], dy_mla, tag + "o_mla_dw", **DW_MATMUL)
    dqc, dkc, dv = _attn_bwd(s['qc'], s['kc'], s['vb'], do, s['o'], s['lse'], t_real, tag + "attn_bwd")
    dqext, dkv, dkrp = _mla_unpack(dqc, dkc, dv, cs, tag + "mla_unpack")
    dcqn = _matmul(dqext, w['w_q'], tag + "q_up_dx", tb=True)
    g['w_q'] = _matmul(s['cqn'], dqext, tag + "q_up_dw", **DW_MATMUL)
    dckvn = _matmul(dkv, w['w_kv'], tag + "kv_up_dx", tb=True)
    g['w_kv'] = _matmul(s['ckvn'], dkv, tag + "kv_up_dw", **DW_MATMUL)
    dgh = _matmul(dy_lru, w['w_o_lru'], tag + "o_lru_dx", tb=True)
    g['w_o_lru'] = _matmul(s['gh'], dy_lru, tag + "o_lru_dw", **DW_MATMUL)
    dproj, dhs = _gated_h_bwd(dgh, s['proj'], s['h0'], s['h1'], dproj, tag + "lru_gate_out_bwd")
    l0, l1, da0, da1 = _scan_bwd(dhs, s['a0'], s['a1'], s['h0'], s['h1'], tag + "lru_scan_bwd")
    dxc, g['w_g'], g['b4'], g['lru_lambda'] = _lru_gates_bwd(
        l0, l1, da0, da1, s['r0'], s['r1'], s['i0'], s['i1'], s['a0'], s['a1'], s['xc'], w['w_g'], w['lru_lambda'],
        t_real, tag + "lru_gates_bwd")
    dproj, g['lru_conv_w'], g['lru_conv_b'] = _lru_conv_bwd(dxc, s['proj'], w['lru_conv_w'], dproj, t_real, tag + "lru_conv_bwd")
    tok = emit('mid', g)
    dproj, dqn, dkvn = _mla_norms_bwd(dcqn, dckvn, _after(dkrp, tok), s['proj'], w['q_norm'], w['kv_norm'], dproj,
                                      tag + "mla_norms_bwd")
    tok = emit('in', {'w_in': _matmul(s['h'], dproj, tag + "proj_dw", **DW_MATMUL), 'q_norm': dqn, 'kv_norm': dkvn})
    dh_mm = _matmul(dproj, w['w_in'], tag + "proj_dx", tk_cap=1536)
    return [(DN_ALPHA, du1), (1.0, dh_mm)], tok


def _swap_halves(a, axis=-1):
    h1, h2 = jnp.split(a, 2, axis=axis)
    return jnp.concatenate([h2, h1], axis=axis)


def _w_in_kernel(w_in_t):
    cq, ckv, kr, lg, lx, gm, gl = jnp.split(w_in_t, [256, 384, 448, 1472, 2496, 3520], axis=0)
    return jnp.concatenate([lg, lx, gm, gl, cq, ckv, kr, _swap_halves(kr, axis=0)], axis=0)


def _layer_weights(fl):
    w = {}
    if 'w_uq' in fl:
        uq = fl['w_uq']
        w['w_q'] = jnp.concatenate([uq, _swap_halves(uq[..., QK_NOPE:])], axis=-1).reshape(Q_RANK, HEADS * 2 * LANE)
        w['w_kv'] = jnp.concatenate([fl['w_uk'].reshape(KV_RANK, -1), fl['w_uv'].reshape(KV_RANK, -1)], axis=1).astype(BF16)
        w['w_g'] = jnp.moveaxis(jnp.concatenate([fl['w_rg'], fl['w_ig']], axis=0), 0, 1).astype(BF16)
        w['b4'] = jnp.concatenate([fl['b_rg'], fl['b_ig']], axis=0)
    for n in ('q_norm', 'kv_norm', 'w_o_mla', 'lru_conv_w', 'lru_conv_b', 'lru_lambda', 'w_o_lru', 'w_out', 'ln1_g',
              'ln1_b', 'w_up', 'ffn_conv_w', 'ffn_conv_b', 'w_down', 'ln2_g', 'ln2_b'):
        if n in fl:
            w[n] = fl[n]
    return w


def _layer_grads(g):
    out = {}
    if 'w_in' in g:
        lg, lx, gm, gl, cq, ckv, kr, krs = jnp.split(g['w_in'], [1024, 2048, 3072, 4096, 4352, 4480, 4544], axis=1)
        out['w_in'] = jnp.concatenate([cq, ckv, kr + _swap_halves(krs), lg, lx, gm, gl], axis=1)
    if 'w_q' in g:
        gq = g['w_q'].reshape(Q_RANK, HEADS, 2 * LANE)
        out['w_uq'] = jnp.concatenate([gq[..., :QK_NOPE], gq[..., QK_NOPE:QK_NOPE + QK_ROPE] + _swap_halves(gq[..., QK_NOPE + QK_ROPE:])], axis=-1)
    if 'w_kv' in g:
        out['w_uk'] = g['w_kv'][:, :HEADS * QK_NOPE].reshape(KV_RANK, HEADS, QK_NOPE)
        out['w_uv'] = g['w_kv'][:, HEADS * QK_NOPE:].reshape(KV_RANK, HEADS, V_HEAD)
    if 'w_g' in g:
        gg = jnp.moveaxis(g['w_g'], 1, 0)
        out['w_rg'], out['w_ig'] = gg[:2], gg[2:]
    if 'b4' in g:
        out['b_rg'], out['b_ig'] = g['b4'][:2], g['b4'][2:]
    for n in ('q_norm', 'kv_norm', 'lru_conv_b', 'ln1_g', 'ln1_b', 'ffn_conv_b', 'ln2_g', 'ln2_b'):
        if n in g:
            out[n] = g[n].reshape(-1)
    for n in ('w_o_mla', 'lru_conv_w', 'lru_lambda', 'w_o_lru', 'w_out', 'w_up', 'ffn_conv_w', 'w_down'):
        if n in g:
            out[n] = g[n]
    return out


def _rope_table(tp):
    half = QK_ROPE // 2
    inv_freq = jnp.exp(-math.log(ROPE_THETA) * jnp.arange(half, dtype=F32) / half)
    ang = jnp.arange(tp, dtype=F32)[:, None] * inv_freq[None, :]
    c, s = jnp.cos(ang), jnp.sin(ang)
    return jnp.concatenate([c, c, -s, s], axis=1)


def _local_step(x, target, meta, ln0_g, ln0_b, layer_w, t_pad, emit):
    seq = x.shape[0]
    t_real = N_META + seq
    zpad = jnp.zeros((t_pad - t_real, D_MODEL), F32)
    xin = jnp.concatenate([meta, x, zpad], axis=0)
    tgt = jnp.concatenate([jnp.zeros((N_META, D_MODEL), F32), target, zpad], axis=0)
    cs = _rope_table(t_pad)
    h = _ln_fwd([(1.0, xin)], ln0_g, ln0_b, "ln0")
    saved = []
    for l in range(DEPTH):
        w_in, rest_of_weights = layer_w[l](h)
        h, s = _layer_fwd(h, w_in, rest_of_weights, cs, t_real, "l%d_" % l)
        saved.append(s)
    dy, lossvec = _loss_head(h, tgt, t_real, "loss_head")
    terms, tok = [(1.0, dy)], None
    for l in reversed(range(DEPTH)):
        terms, tok = _layer_bwd(terms, saved[l], cs, t_real, "l%d_" % l,
                                functools.partial(lambda stage, g, l: emit(l, stage, _layer_grads(g)), l=l), tok)
    dxin, dg0, db0 = _ln_bwd(terms, [(1.0, xin)], _after(ln0_g, tok), "ln0_bwd")
    emit(None, 'head', {'meta_tokens': dxin[:N_META], 'ln0_g': dg0.reshape(-1), 'ln0_b': db0.reshape(-1), 'loss': lossvec})
    return dxin[N_META:t_real]


_HBM = pl.BlockSpec(memory_space=pltpu.HBM)
_SEM = pl.BlockSpec(memory_space=pltpu.SEMAPHORE)
_SIDE_EFFECT = pltpu.SideEffectType.DATAFLOW_SIDE_EFFECTING


def _peer_copies(src_refs, land_refs, scatters, send_sems, recv_sems):
    x, y, c = lax.axis_index("x"), lax.axis_index("y"), lax.axis_index("c")
    me = 4 * x + 2 * y + c
    copies = []
    for k in range(1, N_DEV):
        px = 1 - x if k & 4 else x
        py = 1 - y if k & 2 else y
        pc = 1 - c if k & 1 else c
        for t, (src, land) in enumerate(zip(src_refs, land_refs)):
            copies.append(pltpu.make_async_remote_copy(
                src_ref=src.at[4 * px + 2 * py + pc] if scatters[t] else src, dst_ref=land.at[me],
                send_sem=send_sems.at[7 * t + k - 1], recv_sem=recv_sems.at[7 * t + k - 1],
                device_id=(px, py, pc), device_id_type=pl.DeviceIdType.MESH))
    return me, copies


def _own_block_in_place(land, own):
    me = 4 * lax.axis_index("x") + 2 * lax.axis_index("y") + lax.axis_index("c")
    return lax.dynamic_update_slice_in_dim(land, own, me, 0)


def _gather_two_level(shards, name):
    nt = len(shards)

    def body(*refs):
        x_refs, out_refs = refs[:nt], refs[nt:2 * nt]
        token_ref, send_sems, recv_sems = refs[2 * nt:]
        x, y, c = lax.axis_index("x"), lax.axis_index("y"), lax.axis_index("c")
        me, sibling = (x, y, c), (x, y, 1 - c)
        chips = [(1 - x, y), (x, 1 - y), (1 - x, 1 - y)]

        def copy(t, k, block, to, own=False):
            px, py, pc = block
            slot = out_refs[t].at[4 * px + 2 * py + pc]
            return pltpu.make_async_remote_copy(
                src_ref=x_refs[t] if own else slot, dst_ref=slot,
                send_sem=send_sems.at[7 * t + k], recv_sem=recv_sems.at[7 * t + k],
                device_id=to, device_id_type=pl.DeviceIdType.MESH)

        sent = []
        for t in range(nt):
            first = [copy(t, 1 + j, me, (*chip, c), own=True) for j, chip in enumerate(chips)]
            first.append(copy(t, 0, me, sibling, own=True))
            for cp in first:
                cp.start()
            sent += first
        token_ref[...] = jnp.zeros_like(token_ref)
        for j, chip in enumerate(chips):
            for t in range(nt):
                copy(t, 1 + j, (*chip, c), me).wait_recv()
                passed = copy(t, 4 + j, (*chip, c), sibling)
                passed.start()
                sent.append(passed)
        for t in range(nt):
            copy(t, 0, sibling, me).wait_recv()
            for j, chip in enumerate(chips):
                copy(t, 4 + j, (*chip, 1 - c), me).wait_recv()
        for cp in sent:
            cp.wait_send()

    any_space = pl.BlockSpec(memory_space=pl.ANY)
    res = pl.pallas_call(
        body, name=name,
        out_shape=[jax.ShapeDtypeStruct((N_DEV,) + a.shape, a.dtype) for a in shards] + [jax.ShapeDtypeStruct((SUBLANE, LANE), F32)],
        in_specs=[any_space] * nt, out_specs=[any_space] * nt + [pl.BlockSpec(memory_space=pltpu.VMEM)],
        scratch_shapes=[pltpu.SemaphoreType.DMA((7 * nt,)), pltpu.SemaphoreType.DMA((7 * nt,))],
    )(*shards)
    return [_own_block_in_place(land, a[None]) for land, a in zip(res[:nt], shards)], res[nt][0, 0]


def _exchange_start(groups, name):
    flat = [it for grp in groups for it in grp]
    nt, ng = len(flat), len(groups)
    scatters = [sc for _, sc in flat]
    srcs = [pltpu.with_memory_space_constraint(a, pltpu.HBM) for a, _ in flat]
    land_shapes = [a.shape if sc else (N_DEV,) + a.shape for a, sc in flat]
    lands = [pltpu.with_memory_space_constraint(lax.empty(s, a.dtype), pltpu.HBM) for s, (a, _) in zip(land_shapes, flat)]
    bounds = [0]
    for grp in groups:
        bounds.append(bounds[-1] + len(grp))

    def body(*refs):
        src_refs, land_refs = refs[:nt], refs[nt:2 * nt]
        sem_refs = refs[2 * nt:2 * nt + 2 * ng]
        token_ref = refs[4 * nt + 2 * ng]
        for gi in range(ng):
            lo, hi = bounds[gi], bounds[gi + 1]
            _, copies = _peer_copies(src_refs[lo:hi], land_refs[lo:hi], scatters[lo:hi], sem_refs[2 * gi], sem_refs[2 * gi + 1])
            for cp in copies:
                cp.start()
        token_ref[...] = jnp.zeros_like(token_ref)

    out_shape = []
    for grp in groups:
        out_shape += [pltpu.SemaphoreType.DMA((7 * len(grp),)), pltpu.SemaphoreType.DMA((7 * len(grp),))]
    out_shape += [pltpu.HBM(a.shape, a.dtype) for a in srcs] + [pltpu.HBM(s, a.dtype) for s, a in zip(land_shapes, srcs)]
    out_shape += [jax.ShapeDtypeStruct((SUBLANE, LANE), F32)]
    res = pl.pallas_call(
        body, name=name, out_shape=out_shape,
        in_specs=[_HBM] * (2 * nt),
        out_specs=[_SEM] * (2 * ng) + [_HBM] * (2 * nt) + [pl.BlockSpec(memory_space=pltpu.VMEM)],
        input_output_aliases={t: 2 * ng + t for t in range(2 * nt)},
        compiler_params=pltpu.CompilerParams(has_side_effects=_SIDE_EFFECT),
    )(*srcs, *lands)
    sems, thru, token = res[:2 * ng], res[2 * ng:2 * ng + 2 * nt], res[-1]
    states = []
    for gi in range(ng):
        lo, hi = bounds[gi], bounds[gi + 1]
        states.append((sems[2 * gi], sems[2 * gi + 1], thru[lo:hi], thru[nt + lo:nt + hi], scatters[lo:hi]))
    return states, token[0, 0]


def _exchange_wait(state, after, name):
    send_sems, recv_sems, srcs, lands, scatters = state
    n = len(srcs)

    def body(*refs):
        _, copies = _peer_copies(refs[:n], refs[n:2 * n], scatters, refs[2 * n], refs[2 * n + 1])
        for cp in copies:
            cp.wait_send()
        for cp in copies:
            cp.wait_recv()

    res = pl.pallas_call(
        body, name=name,
        out_shape=[pltpu.HBM(a.shape, a.dtype) for a in srcs] + [pltpu.HBM(a.shape, a.dtype) for a in lands],
        in_specs=[_HBM] * (2 * n) + [_SEM, _SEM, _HBM],
        out_specs=[_HBM] * (2 * n),
        input_output_aliases={t: t for t in range(2 * n)},
        compiler_params=pltpu.CompilerParams(has_side_effects=_SIDE_EFFECT),
    )(*srcs, *lands, send_sems, recv_sems, pltpu.with_memory_space_constraint(after, pltpu.HBM))
    me = 4 * lax.axis_index("x") + 2 * lax.axis_index("y") + lax.axis_index("c")
    out = []
    for src, land, sc in zip(res[:n], res[n:], scatters):
        own = lax.dynamic_index_in_dim(src, me, 0, keepdims=True) if sc else src[None]
        out.append(lax.dynamic_update_slice_in_dim(land, own, me, 0))
    return out


def _as_rows(shape):
    return (1, shape[0]) if len(shape) == 1 else (math.prod(shape[:-1]), shape[-1])


def _sum_adamw(pieces, w, m, v, name):
    shape = w.shape
    nl = len(pieces)
    if nl > 1 and _as_rows(shape[1:])[0] % 16:
        pieces, nl = [jnp.stack(pieces, axis=1)], 1
    rows, cols = _as_rows(shape)
    rl = rows // nl
    cap = max(16, (1 << 18) // cols // 16 * 16)
    tr = _tile(rl, cap, 16)
    nb = rl // tr
    c1 = 1.0 / (1.0 - ADAM_B1 ** ADAM_STEP)
    c2 = 1.0 / (1.0 - ADAM_B2 ** ADAM_STEP)

    def body(*refs):
        p_refs = refs[:nl]
        w_ref, m_ref, v_ref, g_ref, d_ref, nm_ref, nv_ref = refs[nl:]
        li = pl.program_id(0)

        def total(p_ref):
            acc = p_ref[0].astype(F32)
            for k in range(1, N_DEV):
                acc = acc + p_ref[k].astype(F32)
            return acc

        gg = total(p_refs[0])
        for l in range(1, nl):
            gg = jnp.where(li == l, total(p_refs[l]), gg)
        nm = ADAM_B1 * m_ref[...] + (1.0 - ADAM_B1) * gg
        nv = ADAM_B2 * v_ref[...] + (1.0 - ADAM_B2) * (gg * gg)
        g_ref[...] = gg
        d_ref[...] = -ADAM_LR * ((nm * c1) / (jnp.sqrt(nv * c2) + ADAM_EPS) + ADAM_WD * w_ref[...])
        nm_ref[...] = nm
        nv_ref[...] = nv

    blk = pl.BlockSpec((tr, cols), lambda li, i: (li * nb + i, 0))
    p_specs = [pl.BlockSpec((N_DEV, tr, cols), functools.partial(lambda li, i, l: (0, jnp.where(li == l, i, 0), 0), l=l))
               for l in range(nl)]
    res = pl.pallas_call(
        body, name=name, grid=(nl, nb),
        in_specs=p_specs + [blk] * 3, out_specs=[blk] * 4,
        out_shape=[jax.ShapeDtypeStruct((rows, cols), F32)] * 4,
        compiler_params=_cparams("parallel", "parallel"),
    )(*[p.reshape(N_DEV, rl, cols) for p in pieces], *[a.reshape(rows, cols) for a in (w, m, v)])
    return [r.reshape(shape) for r in res]


def _to_shards(full, axis):
    shp = full.shape
    a = full.reshape(shp[:axis] + (N_DEV, shp[axis] // N_DEV) + shp[axis + 1:])
    return jnp.moveaxis(a, axis, 0)


def _from_shards(blocks, axis):
    a = jnp.moveaxis(blocks, 0, axis)
    shp = a.shape
    return a.reshape(shp[:axis] + (shp[axis] * shp[axis + 1],) + shp[axis + 2:])


def kernel(x, meta_tokens, ln0_g, ln0_b, w_in, q_norm, kv_norm, w_uq, w_uk, w_uv, w_o_mla, lru_conv_w, lru_conv_b, w_rg, b_rg, w_ig, b_ig, lru_lambda, w_o_lru, w_out, ln1_g, ln1_b, w_up, ffn_conv_w, ffn_conv_b, w_down, ln2_g, ln2_b, loss_target, m_meta_tokens, m_ln0_g, m_ln0_b, m_w_in, m_q_norm, m_kv_norm, m_w_uq, m_w_uk, m_w_uv, m_w_o_mla, m_lru_conv_w, m_lru_conv_b, m_w_rg, m_b_rg, m_w_ig, m_b_ig, m_lru_lambda, m_w_o_lru, m_w_out, m_ln1_g, m_ln1_b, m_w_up, m_ffn_conv_w, m_ffn_conv_b, m_w_down, m_ln2_g, m_ln2_b, v_meta_tokens, v_ln0_g, v_ln0_b, v_w_in, v_q_norm, v_kv_norm, v_w_uq, v_w_uk, v_w_uv, v_w_o_mla, v_lru_conv_w, v_lru_conv_b, v_w_rg, v_b_rg, v_w_ig, v_b_ig, v_lru_lambda, v_w_o_lru, v_w_out, v_ln1_g, v_ln1_b, v_w_up, v_ffn_conv_w, v_ffn_conv_b, v_w_down, v_ln2_g, v_ln2_b):
    args = (meta_tokens, ln0_g, ln0_b, w_in, q_norm, kv_norm, w_uq, w_uk, w_uv, w_o_mla, lru_conv_w, lru_conv_b, w_rg, b_rg, w_ig, b_ig, lru_lambda, w_o_lru, w_out, ln1_g, ln1_b, w_up, ffn_conv_w, ffn_conv_b, w_down, ln2_g, ln2_b)
    ms = (m_meta_tokens, m_ln0_g, m_ln0_b, m_w_in, m_q_norm, m_kv_norm, m_w_uq, m_w_uk, m_w_uv, m_w_o_mla, m_lru_conv_w, m_lru_conv_b, m_w_rg, m_b_rg, m_w_ig, m_b_ig, m_lru_lambda, m_w_o_lru, m_w_out, m_ln1_g, m_ln1_b, m_w_up, m_ffn_conv_w, m_ffn_conv_b, m_w_down, m_ln2_g, m_ln2_b)
    vs = (v_meta_tokens, v_ln0_g, v_ln0_b, v_w_in, v_q_norm, v_kv_norm, v_w_uq, v_w_uk, v_w_uv, v_w_o_mla, v_lru_conv_w, v_lru_conv_b, v_w_rg, v_b_rg, v_w_ig, v_b_ig, v_lru_lambda, v_w_o_lru, v_w_out, v_ln1_g, v_ln1_b, v_w_up, v_ffn_conv_w, v_ffn_conv_b, v_w_down, v_ln2_g, v_ln2_b)
    wd, md, vd = dict(zip(WEIGHTS, args)), dict(zip(WEIGHTS, ms)), dict(zip(WEIGHTS, vs))

    def shard_axis(n, l):
        return SHARD_AXIS[n] - (0 if l is None else 1)

    def shard(n, l):
        a = wd[n] if l is None else wd[n][l]
        if n in SENT_TRANSPOSED:
            a = a.T
        return a.astype(BF16) if n in BIG else a

    def whole(keys, landed):
        return {k: b.reshape(-1, b.shape[-1]) if k[0] in SENT_TRANSPOSED else _from_shards(b, shard_axis(*k))
                for k, b in zip(keys, landed)}

    first = [('meta_tokens', None), ('w_in', 0)]
    landed, token = _gather_two_level([shard(*k) for k in first], "gather_first")
    got_first = whole(first, landed)
    staged = [[(n, 0) for n in names] for names in STAGE_WEIGHTS]
    later = [(n, 1) for n in SHARDED if n != 'meta_tokens']
    gather, token = _exchange_start([[(_after(shard(*k), token), False) for k in keys] for keys in staged + [later]], "gather_start")

    def arrive(gi, keys, after, name):
        return whole(keys, _exchange_wait(gather[gi], after, name))

    def layer_weights(got, l, names):
        fl = {n: wd[n][l] for n in names if n in REPLICATED}
        fl.update({n: a for (n, _), a in got.items() if n in names})
        return _layer_weights(fl)

    ln0_g = _after(wd['ln0_g'], token)

    def first_layer(h):
        def more(stage, after):
            got = arrive(stage, staged[stage], after, "gather_wait_l0_%d" % stage)
            return layer_weights(got, 0, STAGE_WEIGHTS[stage] + STAGE_REPLICATED[stage])
        return _w_in_kernel(got_first['w_in', 0]), more

    def second_layer(h):
        got = arrive(len(staged), later, h, "gather_wait_l1")
        return _w_in_kernel(got['w_in', 1]), lambda stage, after: layer_weights(got, 1, STAGE_WEIGHTS[stage] + STAGE_REPLICATED[stage])

    sent = []
    pending = []

    def send(l, stage, grads):
        for n, g in grads.items():
            if n in SHARD_AXIS:
                g = _to_shards(g, shard_axis(n, l))
                pending.append(((n, l), (g.astype(BF16) if n in BIG else g, True)))
            else:
                pending.append(((n, l), (g.astype(BF16) if n in LARGE_REPLICATED else g, False)))
        if l == DEPTH - 1 and stage != 'in':
            return None
        (state,), tok = _exchange_start([[it for _, it in pending]], "grads_start_%s_%s" % (l, stage))
        sent.append(([k for k, _ in pending], state))
        pending.clear()
        return tok

    seq = x.shape[1]
    t_pad = -(-(N_META + seq + MIN_PAD_ROWS) // LANE) * LANE
    grad_x = _local_step(x[0], loss_target[0], got_first['meta_tokens', None], ln0_g, wd['ln0_b'],
                         [first_layer, second_layer], t_pad, send)

    pieces = {}
    for gi, (keys, state) in enumerate(sent):
        pieces.update(zip(keys, _exchange_wait(state, grad_x, "grads_wait_%d" % gi)))
    loss = jnp.sum(pieces['loss', None])
    outs = {}
    for n in WEIGHTS:
        ps = [pieces[n, None]] if (n, None) in pieces else [pieces[n, l] for l in range(DEPTH)]
        outs[n] = _sum_adamw(ps, wd[n], md[n], vd[n], "adamw_" + n)
    res = [loss, grad_x[None]]
    for k in range(4):
        res += [outs[n][k] for n in WEIGHTS]
    return tuple(res)
```

```python
import functools
import math

import jax
import jax.numpy as jnp
from jax import lax
from jax.experimental import pallas as pl
from jax.experimental.pallas import tpu as pltpu

F32 = jnp.float32
BF16 = jnp.bfloat16

N_DEV = 8
D_MODEL = 1024
N_META = 16
HEADS = 8
QK_NOPE = 128
QK_ROPE = 64
V_HEAD = 128
Q_RANK = 256
KV_RANK = 128
ROPE_THETA = 10000.0
LRU_BLOCKS = 8
LRU_C = 8.0
D_FF = 2816
DEPTH = 2
DN_ALPHA = (2.0 * DEPTH) ** 0.25
LN_EPS = 1e-5
RMS_EPS = 1e-6
LN2 = math.log(2.0)
ATT_SCALE = 1.0 / math.sqrt(QK_NOPE + QK_ROPE) / LN2
NEG_BIG = -1e30

ADAM_LR = 0.001
ADAM_B1 = 0.9
ADAM_B2 = 0.999
ADAM_EPS = 1e-08
ADAM_WD = 0.01
ADAM_STEP = 10

MIN_PAD_ROWS = 2
LANE = 128
SUBLANE = 8
VMEM_LIMIT = 56 * 1024 * 1024

PROJ_COLS = 4 * D_MODEL + Q_RANK + KV_RANK + 2 * QK_ROPE
C_LRU_G, C_LRU_X, C_G_MLA, C_G_LRU = 0, D_MODEL, 2 * D_MODEL, 3 * D_MODEL
C_CQ = 4 * D_MODEL
C_CKV = C_CQ + Q_RANK
C_KRP = C_CKV + KV_RANK

WEIGHTS = ['meta_tokens', 'ln0_g', 'ln0_b', 'w_in', 'q_norm', 'kv_norm', 'w_uq', 'w_uk', 'w_uv', 'w_o_mla',
           'lru_conv_w', 'lru_conv_b', 'w_rg', 'b_rg', 'w_ig', 'b_ig', 'lru_lambda', 'w_o_lru', 'w_out',
           'ln1_g', 'ln1_b', 'w_up', 'ffn_conv_w', 'ffn_conv_b', 'w_down', 'ln2_g', 'ln2_b']
SHARD_AXIS = {'meta_tokens': 1, 'w_in': 2, 'w_uq': 1, 'w_o_mla': 1, 'lru_conv_w': 2, 'b_rg': 2, 'b_ig': 2,
              'lru_lambda': 2, 'w_o_lru': 1, 'w_out': 1, 'w_up': 2, 'ffn_conv_w': 2, 'w_down': 1}
BIG = ['w_in', 'w_uq', 'w_o_mla', 'w_o_lru', 'w_out', 'w_up', 'w_down']
SHARDED = [n for n in WEIGHTS if n in SHARD_AXIS]
REPLICATED = [n for n in WEIGHTS if n not in SHARD_AXIS]
LARGE_REPLICATED = ['w_uk', 'w_uv', 'w_rg', 'w_ig']
SENT_TRANSPOSED = ['w_in', 'w_up']
STAGE_WEIGHTS = [['w_uq', 'lru_conv_w', 'b_rg', 'b_ig', 'lru_lambda'], ['w_o_mla', 'w_o_lru', 'w_out'], ['w_up', 'ffn_conv_w', 'w_down']]
STAGE_REPLICATED = [['q_norm', 'kv_norm', 'w_uk', 'w_uv', 'lru_conv_b', 'w_rg', 'w_ig'], ['ln1_g', 'ln1_b'], ['ffn_conv_b', 'ln2_g', 'ln2_b']]


def _cparams(*sem):
    return pltpu.CompilerParams(dimension_semantics=sem, vmem_limit_bytes=VMEM_LIMIT)


def _tile(n, cap, unit=LANE):
    best = None
    t = unit
    while t <= min(n, cap):
        if n % t == 0:
            best = t
        t += unit
    return n if best is None else best


def _sigmoid(x):
    return 1.0 / (1.0 + jnp.exp(-x))


_GELU_C = math.sqrt(2.0 / math.pi)


_GELU_A = 0.044715


def _gelu(x):
    t = jnp.tanh(x * (_GELU_C + (_GELU_C * _GELU_A) * (x * x)))
    hx = 0.5 * x
    return hx + hx * t


def _gelu_and_grad(x):
    x2 = x * x
    t = jnp.tanh(x * (_GELU_C + (_GELU_C * _GELU_A) * x2))
    hx = 0.5 * x
    dg = 0.5 + 0.5 * t + (hx * (1.0 - t * t)) * (_GELU_C + (3.0 * _GELU_C * _GELU_A) * x2)
    return hx + hx * t, dg


def _softplus_neg(lam):
    z = jnp.exp(-jnp.abs(lam))
    w = 1.0 + z
    log1p = jnp.where(w == 1.0, z, jnp.log(w) * z / (w - 1.0))
    return jnp.maximum(-lam, 0.0) + log1p


def _row_ids(shape, row0=0):
    return lax.broadcasted_iota(jnp.int32, shape, 0) + row0


def _matmul(a, b, name, ta=False, tb=False, out_dtype=F32, tm_cap=1408, tn_cap=1024, tk_cap=2048):
    if ta:
        kdim, m = a.shape
    else:
        m, kdim = a.shape
    if tb:
        n, k2 = b.shape
    else:
        k2, n = b.shape
    assert kdim == k2, (a.shape, b.shape, ta, tb)
    tm, tn, tk = _tile(m, tm_cap), _tile(n, tn_cap), _tile(kdim, tk_cap)
    nk = kdim // tk

    def body(a_ref, b_ref, o_ref, *acc):
        dn = (((0 if ta else 1,), (1 if tb else 0,)), ((), ()))
        part = lax.dot_general(a_ref[...].astype(BF16), b_ref[...].astype(BF16), dn, preferred_element_type=F32)
        if nk == 1:
            o_ref[...] = part.astype(o_ref.dtype)
            return
        acc_ref, k = acc[0], pl.program_id(2)

        @pl.when(k == 0)
        def _():
            acc_ref[...] = part

        @pl.when(k > 0)
        def _():
            acc_ref[...] += part

        @pl.when(k == nk - 1)
        def _():
            o_ref[...] = acc_ref[...].astype(o_ref.dtype)

    a_spec = pl.BlockSpec((tk, tm), lambda i, j, k: (k, i)) if ta else pl.BlockSpec((tm, tk), lambda i, j, k: (i, k))
    b_spec = pl.BlockSpec((tn, tk), lambda i, j, k: (j, k)) if tb else pl.BlockSpec((tk, tn), lambda i, j, k: (k, j))
    return pl.pallas_call(
        body, name=name,
        grid=(m // tm, n // tn, nk),
        in_specs=[a_spec, b_spec],
        out_specs=pl.BlockSpec((tm, tn), lambda i, j, k: (i, j)),
        out_shape=jax.ShapeDtypeStruct((m, n), out_dtype),
        scratch_shapes=[pltpu.VMEM((tm, tn), F32)] if nk > 1 else [],
        compiler_params=_cparams("parallel", "parallel", "arbitrary"),
    )(a, b)


class Rw:
    def __init__(self, arr, width=None, cb=0):
        self.arr, self.width, self.cb = arr, (arr.shape[1] if width is None else width), cb


class Pm:
    def __init__(self, arr):
        self.arr = arr


class Into:
    def __init__(self, arr, col0, width):
        self.arr, self.col0, self.width = arr, col0, width


def _call_with_into(body, name, grid, in_specs, operands, outs, spec_of, shape_of, extra_out_specs, extra_out_shape, sem):
    intos = [(k, o) for k, o in enumerate(outs) if isinstance(o, Into)]
    aliases = {len(operands) + n: k for n, (k, _) in enumerate(intos)}
    return pl.pallas_call(
        body, name=name, grid=grid,
        in_specs=in_specs + [pl.BlockSpec(memory_space=pl.ANY)] * len(intos),
        out_specs=[spec_of(o) for o in outs] + extra_out_specs,
        out_shape=[jax.ShapeDtypeStruct(o.arr.shape, o.arr.dtype) if isinstance(o, Into) else shape_of(o) for o in outs]
        + extra_out_shape,
        input_output_aliases=aliases,
        compiler_params=_cparams(sem),
    )(*operands, *[o.arr for _, o in intos])


def _rows(fn, name, ins, outs, accs=(), tm_cap=384):
    tp = next(o.arr.shape[0] for o in ins if isinstance(o, Rw))
    tm = _tile(tp, tm_cap)
    n_in, n_out, n_acc = len(ins), len(outs), len(accs)
    n_into = sum(isinstance(o, Into) for o in outs)

    def body(*refs):
        i = pl.program_id(0)
        res = fn(i * tm, *[r[...] for r in refs[:n_in]])
        if not isinstance(res, (tuple, list)):
            res = (res,)
        assert len(res) == n_out + n_acc, (name, len(res))
        out_refs = refs[n_in + n_into:]
        for k in range(n_out):
            out_refs[k][...] = res[k].astype(out_refs[k].dtype)
        for k in range(n_acc):
            ref = out_refs[n_out + k]

            @pl.when(i == 0)
            def _():
                ref[...] = jnp.zeros_like(ref)

            ref[...] += res[n_out + k]

    in_specs = []
    for o in ins:
        if isinstance(o, Rw):
            in_specs.append(pl.BlockSpec((tm, o.width), functools.partial(lambda i, cb: (i, cb), cb=o.cb)))
        else:
            in_specs.append(pl.BlockSpec(o.arr.shape, functools.partial(lambda i, nd: (0,) * nd, nd=o.arr.ndim)))

    def spec_of(o):
        if isinstance(o, Into):
            assert o.col0 % o.width == 0, (name, o.col0, o.width)
            return pl.BlockSpec((tm, o.width), functools.partial(lambda i, cb: (i, cb), cb=o.col0 // o.width))
        return pl.BlockSpec((tm, o[0]), lambda i: (i, 0))

    return _call_with_into(
        body, name, (tp // tm,), in_specs, [o.arr for o in ins], list(outs), spec_of,
        lambda o: jax.ShapeDtypeStruct((tp, o[0]), o[1]),
        [pl.BlockSpec(s, functools.partial(lambda i, nd: (0,) * nd, nd=len(s))) for s in accs],
        [jax.ShapeDtypeStruct(s, F32) for s in accs], "arbitrary")


class Cl:
    def __init__(self, arr, col0=0):
        self.arr, self.col0 = arr, col0


def _cols(fn, name, ins, outs, ncols, tc):
    assert ncols % tc == 0
    n_in, n_out = len(ins), len(outs)
    n_into = sum(isinstance(o, Into) for o in outs)

    def body(*refs):
        res = fn(*[r[...] for r in refs[:n_in]])
        if not isinstance(res, (tuple, list)):
            res = (res,)
        assert len(res) == n_out, (name, len(res))
        out_refs = refs[n_in + n_into:]
        for k in range(n_out):
            out_refs[k][...] = res[k].astype(out_refs[k].dtype)

    in_specs = []
    for o in ins:
        assert o.col0 % tc == 0, (name, o.col0, tc)
        in_specs.append(pl.BlockSpec((o.arr.shape[0], tc), functools.partial(lambda j, off: (0, j + off), off=o.col0 // tc)))

    def spec_of(o):
        if isinstance(o, Into):
            assert o.col0 % tc == 0 and o.width == ncols, (name, o.col0, o.width)
            return pl.BlockSpec((o.arr.shape[0], tc), functools.partial(lambda j, off: (0, j + off), off=o.col0 // tc))
        return pl.BlockSpec((o[0], tc), lambda j: (0, j))

    return _call_with_into(body, name, (ncols // tc,), in_specs, [o.arr for o in ins], list(outs), spec_of,
                           lambda o: jax.ShapeDtypeStruct((o[0], ncols), o[1]), [], [], "parallel")


def _ln_stats(u):
    mu = jnp.mean(u, axis=-1, keepdims=True)
    xc = u - mu
    var = jnp.mean(xc * xc, axis=-1, keepdims=True)
    rstd = lax.rsqrt(var + LN_EPS)
    return xc * rstd, rstd


def _ln_fwd(terms, g, b, name):
    coefs = [c for c, _ in terms]

    def fn(row0, *blk):
        xs, (gg, bb) = blk[:len(coefs)], blk[len(coefs):]
        u = sum(c * x for c, x in zip(coefs, xs))
        xhat, _ = _ln_stats(u)
        return xhat * gg + bb

    d = terms[0][1].shape[1]
    return _rows(fn, name, [Rw(x) for _, x in terms] + [Pm(g.reshape(1, d)), Pm(b.reshape(1, d))], [(d, F32)])[0]


def _ln_bwd(dy_terms, u_terms, g, name):
    dc = [c for c, _ in dy_terms]
    uc = [c for c, _ in u_terms]
    d = u_terms[0][1].shape[1]

    def fn(row0, *blk):
        dys = blk[:len(dc)]
        xs = blk[len(dc):len(dc) + len(uc)]
        gg = blk[-1]
        dy = sum(c * x for c, x in zip(dc, dys))
        u = sum(c * x for c, x in zip(uc, xs))
        xhat, rstd = _ln_stats(u)
        gdy = dy * gg
        m1 = jnp.mean(gdy, axis=-1, keepdims=True)
        m2 = jnp.mean(gdy * xhat, axis=-1, keepdims=True)
        du = rstd * (gdy - m1 - xhat * m2)
        return du, jnp.sum(dy * xhat, axis=0, keepdims=True), jnp.sum(dy, axis=0, keepdims=True)

    ins = [Rw(x) for _, x in dy_terms] + [Rw(x) for _, x in u_terms] + [Pm(g.reshape(1, d))]
    return _rows(fn, name, ins, [(d, F32)], accs=[(1, d), (1, d)])


def _loss_head(y, tgt, t_real, name):
    d = y.shape[1]

    def fn(row0, yb, tb):
        rows = _row_ids(yb.shape, row0)
        live = (rows >= N_META) & (rows < t_real)
        diff = jnp.where(live, yb - tb, 0.0)
        return diff * (1.0 / d), jnp.sum(diff * diff, axis=0, keepdims=True) * (0.5 / d)

    return _rows(fn, name, [Rw(y), Rw(tgt)], [(d, F32)], accs=[(1, d)])


def _rms(x, g):
    r = lax.rsqrt(jnp.mean(x * x, axis=-1, keepdims=True) + RMS_EPS)
    return x * r * g


def _rms_bwd(dy, x, g):
    r = lax.rsqrt(jnp.mean(x * x, axis=-1, keepdims=True) + RMS_EPS)
    gdy = dy * g
    dx = r * gdy - x * (r * r * r) * jnp.mean(gdy * x, axis=-1, keepdims=True)
    return dx, jnp.sum(dy * x * r, axis=0, keepdims=True)


def _mla_norms(proj, qn, kvn, name):
    def fn(row0, cq, ckv, g1, g2):
        return _rms(cq, g1), _rms(ckv, g2)

    return _rows(fn, name, [Rw(proj, Q_RANK, C_CQ // Q_RANK), Rw(proj, KV_RANK, C_CKV // KV_RANK),
                            Pm(qn.reshape(1, Q_RANK)), Pm(kvn.reshape(1, KV_RANK))],
                 [(Q_RANK, F32), (KV_RANK, F32)])


def _mla_norms_bwd(dcqn, dckvn, dkrp, proj, qn, kvn, dproj, name):
    def fn(row0, d1, d2, dkr, cq, ckv, g1, g2):
        dx1, dg1 = _rms_bwd(d1, cq, g1)
        dx2, dg2 = _rms_bwd(d2, ckv, g2)
        return jnp.concatenate([dx1, dx2, dkr], axis=1), dg1, dg2

    return _rows(fn, name, [Rw(dcqn), Rw(dckvn), Rw(dkrp), Rw(proj, Q_RANK, C_CQ // Q_RANK), Rw(proj, KV_RANK, C_CKV // KV_RANK),
                            Pm(qn.reshape(1, Q_RANK)), Pm(kvn.reshape(1, KV_RANK))],
                 [Into(dproj, C_CQ, PROJ_COLS - C_CQ)], accs=[(1, Q_RANK), (1, KV_RANK)])


def _fold_rope(z):
    return z + pltpu.roll(z, QK_ROPE, 1)


def _mla_pack(qext, kv, proj, cs, name):
    tp = qext.shape[0]
    tm = _tile(tp, 384)
    hw, nope_all = 2 * LANE, HEADS * QK_NOPE

    def body(q_ref, kv_ref, kr_ref, cs_ref, qo_ref, ko_ref, vo_ref):
        cs_ = cs_ref[...]
        low = lax.broadcasted_iota(jnp.int32, cs_.shape, 1) < QK_ROPE
        kr = _fold_rope(kr_ref[...] * cs_).astype(BF16)
        for h in range(HEADS):
            qr = jnp.where(low, _fold_rope(q_ref[:, h * hw + QK_NOPE:(h + 1) * hw] * cs_), 0.0)
            qo_ref[:, h * hw:h * hw + QK_NOPE] = (q_ref[:, h * hw:h * hw + QK_NOPE] * ATT_SCALE).astype(BF16)
            qo_ref[:, h * hw + QK_NOPE:(h + 1) * hw] = (qr * ATT_SCALE).astype(BF16)
            ko_ref[:, h * hw:h * hw + QK_NOPE] = kv_ref[:, h * QK_NOPE:(h + 1) * QK_NOPE].astype(BF16)
            ko_ref[:, h * hw + QK_NOPE:(h + 1) * hw] = kr
        vo_ref[...] = kv_ref[:, nope_all:].astype(BF16)

    row = lambda w: pl.BlockSpec((tm, w), lambda i: (i, 0))
    return pl.pallas_call(
        body, name=name, grid=(tp // tm,),
        in_specs=[row(HEADS * hw), row(2 * nope_all), pl.BlockSpec((tm, LANE), lambda i: (i, C_KRP // LANE)), row(LANE)],
        out_specs=[row(HEADS * hw), row(HEADS * hw), row(nope_all)],
        out_shape=[jax.ShapeDtypeStruct((tp, HEADS * hw), BF16),
                   jax.ShapeDtypeStruct((tp, HEADS * hw), BF16),
                   jax.ShapeDtypeStruct((tp, nope_all), BF16)],
        compiler_params=_cparams("parallel"),
    )(qext, kv, proj, cs)


def _mla_unpack(dq, dk, dv, cs, name):
    tp = dq.shape[0]
    tm = _tile(tp, 384)
    hw, nope_all = 2 * LANE, HEADS * QK_NOPE

    def body(dq_ref, dk_ref, dv_ref, cs_ref, dqe_ref, dkv_ref, dkr_ref):
        cs_ = cs_ref[...]
        low = lax.broadcasted_iota(jnp.int32, cs_.shape, 1) < QK_ROPE
        dkr = None
        for h in range(HEADS):
            dqe_ref[:, h * hw:h * hw + QK_NOPE] = dq_ref[:, h * hw:h * hw + QK_NOPE] * ATT_SCALE
            dqr = jnp.where(low, dq_ref[:, h * hw + QK_NOPE:(h + 1) * hw], 0.0) * ATT_SCALE
            dqe_ref[:, h * hw + QK_NOPE:(h + 1) * hw] = _fold_rope(dqr) * cs_
            dkv_ref[:, h * QK_NOPE:(h + 1) * QK_NOPE] = dk_ref[:, h * hw:h * hw + QK_NOPE]
            part = jnp.where(low, dk_ref[:, h * hw + QK_NOPE:(h + 1) * hw], 0.0)
            dkr = part if h == 0 else dkr + part
        dkv_ref[:, nope_all:] = dv_ref[...]
        dkr_ref[...] = _fold_rope(dkr) * cs_

    row = lambda w: pl.BlockSpec((tm, w), lambda i: (i, 0))
    return pl.pallas_call(
        body, name=name, grid=(tp // tm,),
        in_specs=[row(HEADS * hw), row(HEADS * hw), row(nope_all), row(LANE)],
        out_specs=[row(HEADS * hw), row(2 * nope_all), row(LANE)],
        out_shape=[jax.ShapeDtypeStruct((tp, HEADS * hw), F32),
                   jax.ShapeDtypeStruct((tp, 2 * nope_all), F32),
                   jax.ShapeDtypeStruct((tp, LANE), F32)],
        compiler_params=_cparams("parallel"),
    )(dq, dk, dv, cs)


def _attn_fwd(q, k, v, t_real, name):
    tp = q.shape[0]
    tq = _tile(tp, 1408)
    tkc = _tile(tp, 1408)
    nkc = -(-t_real // tkc)

    def body(q_ref, k_ref, v_ref, o_ref, lse_ref):
        qb = q_ref[...]
        m = l = acc = None
        for c in range(nkc):
            s = lax.dot_general(qb, k_ref[c * tkc:(c + 1) * tkc, :], (((1,), (1,)), ((), ())), preferred_element_type=F32)
            if (c + 1) * tkc > t_real:
                cols = lax.broadcasted_iota(jnp.int32, s.shape, 1) + c * tkc
                s = jnp.where(cols < t_real, s, NEG_BIG)
            mc = jnp.max(s, axis=-1, keepdims=True)
            m_new = mc if c == 0 else jnp.maximum(m, mc)
            p = jnp.exp2(s - m_new)
            lc = jnp.sum(p, axis=-1, keepdims=True)
            pv = jnp.dot(p.astype(BF16), v_ref[c * tkc:(c + 1) * tkc, :], preferred_element_type=F32)
            if c == 0:
                l, acc = lc, pv
            else:
                alpha = jnp.exp2(m - m_new)
                l, acc = alpha * l + lc, alpha * acc + pv
            m = m_new
        o_ref[...] = acc / l
        lse_ref[...] = m + jnp.log2(l)

    return pl.pallas_call(
        body, name=name, grid=(HEADS, tp // tq),
        in_specs=[pl.BlockSpec((tq, 2 * LANE), lambda h, i: (i, h)),
                  pl.BlockSpec((tp, 2 * LANE), lambda h, i: (0, h)),
                  pl.BlockSpec((tp, LANE), lambda h, i: (0, h))],
        out_specs=[pl.BlockSpec((tq, LANE), lambda h, i: (i, h)),
                   pl.BlockSpec((None, tq, 1), lambda h, i: (h, i, 0))],
        out_shape=[jax.ShapeDtypeStruct((tp, HEADS * LANE), F32),
                   jax.ShapeDtypeStruct((HEADS, tp, 1), F32)],
        compiler_params=_cparams("parallel", "parallel"),
    )(q, k, v)


def _attn_bwd(q, k, v, do, o, lse, t_real, name):
    tp = q.shape[0]
    tq = _tile(tp, 1408)
    tkc = _tile(tp, 704, 64)
    nkc = -(-t_real // tkc)

    def body(q_ref, k_ref, v_ref, do_ref, o_ref, lse_ref, dq_ref, dk_ref, dv_ref):
        i = pl.program_id(1)

        @pl.when(i == 0)
        def _():
            dk_ref[...] = jnp.zeros_like(dk_ref)
            dv_ref[...] = jnp.zeros_like(dv_ref)

        qb = q_ref[...]
        dob = do_ref[...]
        dob16 = dob.astype(BF16)
        dol2 = (dob * LN2).astype(BF16)
        delta = jnp.sum(dob * o_ref[...], axis=-1, keepdims=True) * LN2
        lse = lse_ref[...]
        dq = None
        for c in range(nkc):
            ks = slice(c * tkc, (c + 1) * tkc)
            kb = k_ref[ks, :]
            s = lax.dot_general(qb, kb, (((1,), (1,)), ((), ())), preferred_element_type=F32)
            p = jnp.exp2(s - lse)
            if (c + 1) * tkc > t_real:
                cols = lax.broadcasted_iota(jnp.int32, s.shape, 1) + c * tkc
                p = jnp.where(cols < t_real, p, 0.0)
            dp = lax.dot_general(dol2, v_ref[ks, :], (((1,), (1,)), ((), ())), preferred_element_type=F32)
            ds = (p * (dp - delta)).astype(BF16)
            dqc = jnp.dot(ds, kb, preferred_element_type=F32)
            dq = dqc if c == 0 else dq + dqc
            dk_ref[ks, :] += lax.dot_general(ds, qb, (((0,), (0,)), ((), ())), preferred_element_type=F32)
            dv_ref[ks, :] += lax.dot_general(p.astype(BF16), dob16, (((0,), (0,)), ((), ())), preferred_element_type=F32)
        dq_ref[...] = dq

    return pl.pallas_call(
        body, name=name, grid=(HEADS, tp // tq),
        in_specs=[pl.BlockSpec((tq, 2 * LANE), lambda h, i: (i, h)),
                  pl.BlockSpec((tp, 2 * LANE), lambda h, i: (0, h)),
                  pl.BlockSpec((tp, LANE), lambda h, i: (0, h)),
                  pl.BlockSpec((tq, LANE), lambda h, i: (i, h)),
                  pl.BlockSpec((tq, LANE), lambda h, i: (i, h)),
                  pl.BlockSpec((None, tq, 1), lambda h, i: (h, i, 0))],
        out_specs=[pl.BlockSpec((tq, 2 * LANE), lambda h, i: (i, h)),
                   pl.BlockSpec((tp, 2 * LANE), lambda h, i: (0, h)),
                   pl.BlockSpec((tp, LANE), lambda h, i: (0, h))],
        out_shape=[jax.ShapeDtypeStruct((tp, HEADS * 2 * LANE), F32),
                   jax.ShapeDtypeStruct((tp, HEADS * 2 * LANE), F32),
                   jax.ShapeDtypeStruct((tp, HEADS * LANE), F32)],
        compiler_params=_cparams("parallel", "arbitrary"),
    )(q, k, v, do, o, lse)


def _shift_rows(x, s):
    tp = x.shape[0]
    return x if s % tp == 0 else pltpu.roll(x, s % tp, 0)


def _conv_fwd_val(xm, w, b, pad_left):
    acc = b + w[0:1, :] * _shift_rows(xm, pad_left)
    for k in range(1, w.shape[0]):
        acc = acc + w[k:k + 1, :] * _shift_rows(xm, pad_left - k)
    return acc


def _conv_bwd_val(dy, xm, w, pad_left, live):
    kk = w.shape[0]
    dx = w[0:1, :] * _shift_rows(dy, -pad_left)
    dws = [jnp.sum(dy * _shift_rows(xm, pad_left), axis=0, keepdims=True)]
    for k in range(1, kk):
        dx = dx + w[k:k + 1, :] * _shift_rows(dy, k - pad_left)
        dws.append(jnp.sum(dy * _shift_rows(xm, pad_left - k), axis=0, keepdims=True))
    return jnp.where(live, dx, 0.0), jnp.concatenate(dws, axis=0), jnp.sum(dy, axis=0, keepdims=True)


def _lru_conv_fwd(proj, w, b, t_real, name):
    def fn(x, ww, bb):
        xm = jnp.where(_row_ids(x.shape) < t_real, x, 0.0)
        return _conv_fwd_val(xm, ww, bb, 2)

    return _cols(fn, name, [Cl(proj, C_LRU_X), Cl(w), Cl(b.reshape(1, -1))], [(proj.shape[0], F32)], D_MODEL, 128)[0]


def _lru_conv_bwd(dxc, proj, w, dproj, t_real, name):
    def fn(dy, x, ww):
        live = _row_ids(x.shape) < t_real
        xm = jnp.where(live, x, 0.0)
        dym = jnp.where(live, dy, 0.0)
        return _conv_bwd_val(dym, xm, ww, 2, live)

    return _cols(fn, name, [Cl(dxc), Cl(proj, C_LRU_X), Cl(w)],
                 [Into(dproj, C_LRU_X, D_MODEL), (w.shape[0], F32), (1, F32)], D_MODEL, 128)


def _ffn_conv_act(up, w, b, t_real, name):
    def fn(g, v, wg, wv, bg, bv):
        live = _row_ids(g.shape) < t_real
        gc = _conv_fwd_val(jnp.where(live, g, 0.0), wg, bg, 1)
        vc = _conv_fwd_val(jnp.where(live, v, 0.0), wv, bv, 1)
        return _gelu(gc) * vc

    b2 = b.reshape(1, -1)
    return _cols(fn, name, [Cl(up), Cl(up, D_FF), Cl(w), Cl(w, D_FF), Cl(b2), Cl(b2, D_FF)],
                 [(up.shape[0], F32)], D_FF, 128)[0]


def _ffn_conv_act_bwd(dm, up, w, b, t_real, name):
    tp, kk = up.shape[0], w.shape[0]
    nb = D_FF // LANE
    assert nb >= 2

    def body(dm_ref, g_ref, v_ref, wg_ref, wv_ref, bg_ref, bv_ref, dup_ref, dwg_ref, dwv_ref, dbg_ref, dbv_ref, stage, sems):
        j = pl.program_id(0)
        slot = j % 2

        def copies(step, sl):
            return [pltpu.make_async_copy(stage.at[sl, half],
                                          dup_ref.at[:, pl.ds(pl.multiple_of(half * D_FF + step * LANE, LANE), LANE)],
                                          sems.at[sl, half]) for half in range(2)]

        @pl.when(j >= 2)
        def _():
            for cp in copies(j - 2, slot):
                cp.wait()

        live = _row_ids((tp, LANE)) < t_real
        gm, vm = jnp.where(live, g_ref[...], 0.0), jnp.where(live, v_ref[...], 0.0)
        gc = _conv_fwd_val(gm, wg_ref[...], bg_ref[...], 1)
        vc = _conv_fwd_val(vm, wv_ref[...], bv_ref[...], 1)
        act, dact = _gelu_and_grad(gc)
        dmm = jnp.where(live, dm_ref[...], 0.0)
        stage[slot, 0], dwg_ref[...], dbg_ref[...] = _conv_bwd_val(dmm * vc * dact, gm, wg_ref[...], 1, live)
        stage[slot, 1], dwv_ref[...], dbv_ref[...] = _conv_bwd_val(dmm * act, vm, wv_ref[...], 1, live)
        for cp in copies(j, slot):
            cp.start()

        @pl.when(j == nb - 1)
        def _():
            for cp in copies(j - 1, 1 - slot) + copies(j, slot):
                cp.wait()

    b2 = b.reshape(1, -1)
    col = lambda rows, off: pl.BlockSpec((rows, LANE), functools.partial(lambda j, o: (0, j + o), o=off))
    return pl.pallas_call(
        body, name=name, grid=(nb,),
        in_specs=[col(tp, 0), col(tp, 0), col(tp, nb), col(kk, 0), col(kk, nb), col(1, 0), col(1, nb)],
        out_specs=[pl.BlockSpec(memory_space=pl.ANY), col(kk, 0), col(kk, 0), col(1, 0), col(1, 0)],
        out_shape=[jax.ShapeDtypeStruct((tp, 2 * D_FF), F32), jax.ShapeDtypeStruct((kk, D_FF), F32),
                   jax.ShapeDtypeStruct((kk, D_FF), F32), jax.ShapeDtypeStruct((1, D_FF), F32),
                   jax.ShapeDtypeStruct((1, D_FF), F32)],
        scratch_shapes=[pltpu.VMEM((2, 2, tp, LANE), F32), pltpu.SemaphoreType.DMA((2, 2))],
        compiler_params=_cparams("arbitrary"),
    )(dm, up, up, w, w, b2, b2)


def _lru_gates_fwd(xc, wg, b4, lam, t_real, name):
    tp = xc.shape[0]
    tm = _tile(tp, 1408)

    def body(x_ref, w_ref, b_ref, lam_ref, r0_ref, r1_ref, i0_ref, i1_ref, a0_ref, a1_ref, u0_ref, u1_ref):
        x = x_ref[...]
        xb = x.astype(BF16)
        live = _row_ids(x.shape, pl.program_id(1) * tm) < t_real
        bb = b_ref[...]
        sp = _softplus_neg(lam_ref[...])
        gate = [_sigmoid(jnp.dot(xb, w_ref[k], preferred_element_type=F32) + bb[k:k + 1, :]) for k in range(4)]
        for d, (r_ref, i_ref, a_ref, u_ref) in enumerate(((r0_ref, i0_ref, a0_ref, u0_ref), (r1_ref, i1_ref, a1_ref, u1_ref))):
            r, ig = gate[d], gate[2 + d]
            a = jnp.exp(-LRU_C * r * sp[d:d + 1, :])
            r_ref[...] = r
            i_ref[...] = ig
            a_ref[...] = a
            u_ref[...] = jnp.where(live, jnp.sqrt(1.0 - a * a) * (ig * x), 0.0)

    blk = pl.BlockSpec((tm, LANE), lambda g, i: (i, g))
    return pl.pallas_call(
        body, name=name, grid=(LRU_BLOCKS, tp // tm),
        in_specs=[blk, pl.BlockSpec((None, 4, LANE, LANE), lambda g, i: (g, 0, 0, 0)),
                  pl.BlockSpec((4, LANE), lambda g, i: (0, g)), pl.BlockSpec((2, LANE), lambda g, i: (0, g))],
        out_specs=[blk] * 8,
        out_shape=[jax.ShapeDtypeStruct((tp, D_MODEL), F32)] * 8,
        compiler_params=_cparams("parallel", "parallel"),
    )(xc, wg, b4, lam)


def _lru_gates_bwd(l0, l1, da0, da1, r0, r1, i0, i1, a0, a1, xc, wg, lam, t_real, name):
    tp = xc.shape[0]
    tm = _tile(tp, 1408)

    rc = 32
    assert tm % rc == 0

    def fold(v):
        out = v[0:SUBLANE]
        for t in range(1, rc // SUBLANE):
            out = out + v[t * SUBLANE:(t + 1) * SUBLANE]
        return out

    def body(l0_ref, l1_ref, da0_ref, da1_ref, r0_ref, r1_ref, i0_ref, i1_ref, a0_ref, a1_ref, x_ref, w_ref, lam_ref,
             dx_ref, dw_ref, db_ref, dlam_ref, pre_s, dxp_s):
        i = pl.program_id(1)
        lam_ = lam_ref[...]
        sp = _softplus_neg(lam_)
        dsp_dlam = -_sigmoid(-lam_)

        def chunk(c, sums):
            r0 = pl.multiple_of(c * rc, rc)
            rows = pl.ds(r0, rc)
            x = x_ref[rows, :]
            live = _row_ids((rc, LANE), i * tm + r0) < t_real
            dxp = jnp.zeros_like(x)
            sums = list(sums)
            for d, (l_ref, da_ref, r_ref, i_ref, a_ref) in enumerate(((l0_ref, da0_ref, r0_ref, i0_ref, a0_ref),
                                                                      (l1_ref, da1_ref, r1_ref, i1_ref, a1_ref))):
                r, ig, a = r_ref[rows, :], i_ref[rows, :], a_ref[rows, :]
                du = jnp.where(live, l_ref[rows, :], 0.0)
                a2 = a * a
                rs = lax.rsqrt(1.0 - a2)
                dv = du * ((1.0 - a2) * rs)
                ds = du * (ig * x)
                dla = jnp.where(live, da_ref[rows, :] * a - ds * (a2 * rs), 0.0)
                dr = dla * (-LRU_C) * sp[d:d + 1, :]
                p_r = dr * r * (1.0 - r)
                p_i = dv * x * ig * (1.0 - ig)
                pre_s[d, rows, :] = p_r.astype(BF16)
                pre_s[2 + d, rows, :] = p_i.astype(BF16)
                dxp = dxp + dv * ig
                sums[d] = sums[d] + fold(p_r)
                sums[2 + d] = sums[2 + d] + fold(p_i)
                sums[4 + d] = sums[4 + d] + fold(dla * (-LRU_C) * r)
            dxp_s[rows, :] = dxp
            return tuple(sums)

        zero = jnp.zeros((SUBLANE, LANE), F32)
        sums = lax.fori_loop(0, tm // rc, chunk, (zero,) * 6)

        @pl.when(i == 0)
        def _():
            dw_ref[...] = jnp.zeros_like(dw_ref)
            db_ref[...] = jnp.zeros_like(db_ref)
            dlam_ref[...] = jnp.zeros_like(dlam_ref)

        xb = x_ref[...].astype(BF16)
        dx = dxp_s[...]
        for k in range(4):
            pk = pre_s[k]
            dx = dx + lax.dot_general(pk, w_ref[k], (((1,), (1,)), ((), ())), preferred_element_type=F32)
            dw_ref[k] += lax.dot_general(xb, pk, (((0,), (0,)), ((), ())), preferred_element_type=F32)
        db_ref[...] += jnp.concatenate([jnp.sum(sums[k], axis=0, keepdims=True) for k in range(4)], axis=0)
        dlam_ref[...] += jnp.concatenate([jnp.sum(sums[4 + d], axis=0, keepdims=True) * dsp_dlam[d:d + 1, :] for d in range(2)], axis=0)
        dx_ref[...] = dx

    blk = pl.BlockSpec((tm, LANE), lambda g, i: (i, g))
    return pl.pallas_call(
        body, name=name, grid=(LRU_BLOCKS, tp // tm),
        in_specs=[blk] * 11 + [pl.BlockSpec((None, 4, LANE, LANE), lambda g, i: (g, 0, 0, 0)),
                               pl.BlockSpec((2, LANE), lambda g, i: (0, g))],
        out_specs=[blk, pl.BlockSpec((None, 4, LANE, LANE), lambda g, i: (g, 0, 0, 0)),
                   pl.BlockSpec((4, LANE), lambda g, i: (0, g)), pl.BlockSpec((2, LANE), lambda g, i: (0, g))],
        out_shape=[jax.ShapeDtypeStruct((tp, D_MODEL), F32), jax.ShapeDtypeStruct((LRU_BLOCKS, 4, LANE, LANE), F32),
                   jax.ShapeDtypeStruct((4, D_MODEL), F32), jax.ShapeDtypeStruct((2, D_MODEL), F32)],
        scratch_shapes=[pltpu.VMEM((4, tm, LANE), BF16), pltpu.VMEM((tm, LANE), F32)],
        compiler_params=_cparams("parallel", "arbitrary"),
    )(l0, l1, da0, da1, r0, r1, i0, i1, a0, a1, xc, wg, lam)


SCAN_UNROLL = 4


def _loop_tiles(nt, step, carry):
    assert nt % SCAN_UNROLL == 0

    def trip(tt, c):
        for u in range(SCAN_UNROLL):
            c = step(tt * SCAN_UNROLL + u, c)
        return c

    return lax.fori_loop(0, nt // SCAN_UNROLL, trip, carry)


def _tile_scan(a, u, reverse):
    rows = lax.broadcasted_iota(jnp.int32, a.shape, 0)
    for s in (1, 2, 4):
        if reverse:
            keep = rows < SUBLANE - s
            a_sh, u_sh = pltpu.roll(a, SUBLANE - s, 0), pltpu.roll(u, SUBLANE - s, 0)
        else:
            keep = rows >= s
            a_sh, u_sh = pltpu.roll(a, s, 0), pltpu.roll(u, s, 0)
        u = u + a * jnp.where(keep, u_sh, 0.0)
        a = a * jnp.where(keep, a_sh, 1.0)
    return a, u


def _scan_fwd(a0, u0, a1, u1, name):
    tp, d = a0.shape
    tc = 128
    nt = tp // SUBLANE

    def body(a0_ref, u0_ref, a1_ref, u1_ref, h0_ref, h1_ref):
        def step(t, carry):
            c0, c1 = carry
            f = pl.multiple_of(t * SUBLANE, SUBLANE)
            b = pl.multiple_of((nt - 1 - t) * SUBLANE, SUBLANE)
            pa, pu = _tile_scan(a0_ref[pl.ds(f, SUBLANE), :], u0_ref[pl.ds(f, SUBLANE), :], False)
            h = pu + pa * c0
            h0_ref[pl.ds(f, SUBLANE), :] = h
            c0 = h[SUBLANE - 1:SUBLANE, :]
            pa, pu = _tile_scan(a1_ref[pl.ds(b, SUBLANE), :], u1_ref[pl.ds(b, SUBLANE), :], True)
            h = pu + pa * c1
            h1_ref[pl.ds(b, SUBLANE), :] = h
            c1 = h[0:1, :]
            return c0, c1

        z = jnp.zeros((1, tc), F32)
        _loop_tiles(nt, step, (z, z))

    blk = pl.BlockSpec((tp, tc), lambda j: (0, j))
    return pl.pallas_call(
        body, name=name, grid=(d // tc,), in_specs=[blk] * 4, out_specs=[blk] * 2,
        out_shape=[jax.ShapeDtypeStruct((tp, d), F32)] * 2,
        compiler_params=_cparams("parallel"),
    )(a0, u0, a1, u1)


def _scan_bwd(dh, a0, a1, h0, h1, name):
    tp, d = dh.shape
    tc = 128
    nt = tp // SUBLANE

    def body(dh_ref, a0_ref, a1_ref, h0_ref, h1_ref, l0_ref, l1_ref, da0_ref, da1_ref):
        rows8 = lax.broadcasted_iota(jnp.int32, (SUBLANE, tc), 0)

        def step(t, carry):
            c0, c1 = carry
            b = pl.multiple_of((nt - 1 - t) * SUBLANE, SUBLANE)
            f = pl.multiple_of(t * SUBLANE, SUBLANE)
            a = a0_ref[pl.ds(b, SUBLANE), :]
            a_next = jnp.where(rows8 < SUBLANE - 1, pltpu.roll(a, SUBLANE - 1, 0), 1.0)
            pa, pu = _tile_scan(a_next, dh_ref[pl.ds(b, SUBLANE), :], True)
            lam = pu + pa * c0
            l0_ref[pl.ds(b, SUBLANE), :] = lam
            c0 = a[0:1, :] * lam[0:1, :]
            a = a1_ref[pl.ds(f, SUBLANE), :]
            a_prev = jnp.where(rows8 >= 1, pltpu.roll(a, 1, 0), 1.0)
            pa, pu = _tile_scan(a_prev, dh_ref[pl.ds(f, SUBLANE), :], False)
            lam = pu + pa * c1
            l1_ref[pl.ds(f, SUBLANE), :] = lam
            c1 = a[SUBLANE - 1:SUBLANE, :] * lam[SUBLANE - 1:SUBLANE, :]
            return c0, c1

        z = jnp.zeros((1, tc), F32)
        _loop_tiles(nt, step, (z, z))
        rows = lax.broadcasted_iota(jnp.int32, (tp, tc), 0)
        da0_ref[...] = l0_ref[...] * jnp.where(rows >= 1, pltpu.roll(h0_ref[...], 1, 0), 0.0)
        da1_ref[...] = l1_ref[...] * jnp.where(rows < tp - 1, pltpu.roll(h1_ref[...], tp - 1, 0), 0.0)

    blk = pl.BlockSpec((tp, tc), lambda j: (0, j))
    return pl.pallas_call(
        body, name=name, grid=(d // tc,), in_specs=[blk] * 5, out_specs=[blk] * 4,
        out_shape=[jax.ShapeDtypeStruct((tp, d), F32)] * 4,
        compiler_params=_cparams("parallel"),
    )(dh, a0, a1, h0, h1)


def _gated_h(proj, h0, h1, name):
    def fn(row0, lg, x0, x1):
        return _gelu(lg) * (x0 + x1)

    return _rows(fn, name, [Rw(proj, D_MODEL, C_LRU_G // D_MODEL), Rw(h0), Rw(h1)], [(D_MODEL, F32)])[0]


def _gated_h_bwd(dgh, proj, h0, h1, dproj, name):
    def fn(row0, dg, lg, x0, x1):
        act, dact = _gelu_and_grad(lg)
        return dg * (x0 + x1) * dact, dg * act

    return _rows(fn, name, [Rw(dgh), Rw(proj, D_MODEL, C_LRU_G // D_MODEL), Rw(h0), Rw(h1)],
                 [Into(dproj, C_LRU_G, D_MODEL), (D_MODEL, F32)])


def _mix(proj, y_mla, y_lru, name):
    def fn(row0, gm, gl, ym, yl):
        return _sigmoid(gm) * ym + _sigmoid(gl) * yl

    return _rows(fn, name, [Rw(proj, D_MODEL, C_G_MLA // D_MODEL), Rw(proj, D_MODEL, C_G_LRU // D_MODEL), Rw(y_mla), Rw(y_lru)],
                 [(D_MODEL, F32)])[0]


def _mix_bwd(dz, proj, y_mla, y_lru, dproj, name):
    def fn(row0, dzb, gm, gl, ym, yl):
        sm, sl = _sigmoid(gm), _sigmoid(gl)
        dg = jnp.concatenate([dzb * ym * sm * (1.0 - sm), dzb * yl * sl * (1.0 - sl)], axis=1)
        return dzb * sm, dzb * sl, dg

    return _rows(fn, name, [Rw(dz), Rw(proj, D_MODEL, C_G_MLA // D_MODEL), Rw(proj, D_MODEL, C_G_LRU // D_MODEL),
                            Rw(y_mla), Rw(y_lru)], [(D_MODEL, F32), (D_MODEL, F32), Into(dproj, C_G_MLA, 2 * D_MODEL)])


def _layer_fwd(h, w_in, more_weights, cs, t_real, tag):
    proj = _matmul(h, w_in, tag + "proj", tb=True)
    w = dict(more_weights(0, proj), w_in=w_in)
    cqn, ckvn = _mla_norms(proj, w['q_norm'], w['kv_norm'], tag + "mla_norms")
    qext = _matmul(cqn, w['w_q'], tag + "q_up")
    kv = _matmul(ckvn, w['w_kv'], tag + "kv_up")
    qc, kc, vb = _mla_pack(qext, kv, proj, cs, tag + "mla_pack")
    o, lse = _attn_fwd(qc, kc, vb, t_real, tag + "attn_fwd")
    w.update(more_weights(1, o))
    y_mla = _matmul(o, w['w_o_mla'], tag + "o_mla")
    xc = _lru_conv_fwd(proj, w['lru_conv_w'], w['lru_conv_b'], t_real, tag + "lru_conv")
    r0, r1, i0, i1, a0, a1, u0, u1 = _lru_gates_fwd(xc, w['w_g'], w['b4'], w['lru_lambda'], t_real, tag + "lru_gates")
    h0, h1 = _scan_fwd(a0, u0, a1, u1, tag + "lru_scan")
    gh = _gated_h(proj, h0, h1, tag + "lru_gate_out")
    y_lru = _matmul(gh, w['w_o_lru'], tag + "o_lru")
    z = _mix(proj, y_mla, y_lru, tag + "mix")
    zo = _matmul(z, w['w_out'], tag + "w_out")
    hm = _ln_fwd([(DN_ALPHA, h), (1.0, zo)], w['ln1_g'], w['ln1_b'], tag + "ln1")
    w.update(more_weights(2, hm))
    up = _matmul(hm, w['w_up'], tag + "w_up", tb=True)
    m = _ffn_conv_act(up, w['ffn_conv_w'], w['ffn_conv_b'], t_real, tag + "ffn_conv")
    f = _matmul(m, w['w_down'], tag + "w_down", tk_cap=1408)
    out = _ln_fwd([(DN_ALPHA, hm), (1.0, f)], w['ln2_g'], w['ln2_b'], tag + "ln2")
    saved = dict(w=w, h=h, proj=proj, cqn=cqn, ckvn=ckvn, qc=qc, kc=kc, vb=vb, o=o, lse=lse, y_mla=y_mla, xc=xc,
                 r0=r0, r1=r1, i0=i0, i1=i1, a0=a0, a1=a1, h0=h0, h1=h1, gh=gh, y_lru=y_lru, z=z, zo=zo, hm=hm,
                 up=up, m=m, f=f)
    return out, saved


DW_MATMUL = dict(ta=True, out_dtype=BF16, tn_cap=1408, tk_cap=1408)


def _after(a, tok):
    return a if tok is None else a + tok.astype(a.dtype)


def _layer_bwd(dout_terms, s, cs, t_real, tag, emit, tok):
    w = s['w']
    g = {}
    du2, dg2, db2 = _ln_bwd(dout_terms, [(DN_ALPHA, s['hm']), (1.0, s['f'])], _after(w['ln2_g'], tok), tag + "ln2_bwd")
    g['ln2_g'], g['ln2_b'] = dg2, db2
    dm = _matmul(du2, w['w_down'], tag + "w_down_dx", tb=True)
    g['w_down'] = _matmul(s['m'], du2, tag + "w_down_dw", **DW_MATMUL)
    dup, dwg_, dwv_, dbg_, dbv_ = _ffn_conv_act_bwd(dm, s['up'], w['ffn_conv_w'], w['ffn_conv_b'], t_real, tag + "ffn_conv_bwd")
    g['ffn_conv_w'] = jnp.concatenate([dwg_, dwv_], axis=1)
    g['ffn_conv_b'] = jnp.concatenate([dbg_, dbv_], axis=1)
    dhm_mm = _matmul(dup, w['w_up'], tag + "w_up_dx", tk_cap=1408)
    g['w_up'] = _matmul(s['hm'], dup, tag + "w_up_dw", **DW_MATMUL)
    tok = emit('ffn', g)
    g = {}
    du1, dg1, db1 = _ln_bwd([(DN_ALPHA, du2), (1.0, dhm_mm)], [(DN_ALPHA, s['h']), (1.0, s['zo'])],
                            _after(w['ln1_g'], tok), tag + "ln1_bwd")
    g['ln1_g'], g['ln1_b'] = dg1, db1
    dz = _matmul(du1, w['w_out'], tag + "w_out_dx", tb=True)
    g['w_out'] = _matmul(s['z'], du1, tag + "w_out_dw", **DW_MATMUL)
    dproj = lax.empty(s['proj'].shape, F32)
    dy_mla, dy_lru, dproj = _mix_bwd(dz, s['proj'], s['y_mla'], s['y_lru'], dproj, tag + "mix_bwd")
    do = _matmul(dy_mla, w['w_o_mla'], tag + "o_mla_dx", tb=True)
    g['w_o_mla'] = _matmul(s['o'], dy_mla, tag + "o_mla_dw", **DW_MATMUL)
    dqc, dkc, dv = _attn_bwd(s['qc'], s['kc'], s['vb'], do, s['o'], s['lse'], t_real, tag + "attn_bwd")
    dqext, dkv, dkrp = _mla_unpack(dqc, dkc, dv, cs, tag + "mla_unpack")
    dcqn = _matmul(dqext, w['w_q'], tag + "q_up_dx", tb=True)
    g['w_q'] = _matmul(s['cqn'], dqext, tag + "q_up_dw", **DW_MATMUL)
    dckvn = _matmul(dkv, w['w_kv'], tag + "kv_up_dx", tb=True)
    g['w_kv'] = _matmul(s['ckvn'], dkv, tag + "kv_up_dw", **DW_MATMUL)
    dgh = _matmul(dy_lru, w['w_o_lru'], tag + "o_lru_dx", tb=True)
    g['w_o_lru'] = _matmul(s['gh'], dy_lru, tag + "o_lru_dw", **DW_MATMUL)
    dproj, dhs = _gated_h_bwd(dgh, s['proj'], s['h0'], s['h1'], dproj, tag + "lru_gate_out_bwd")
    l0, l1, da0, da1 = _scan_bwd(dhs, s['a0'], s['a1'], s['h0'], s['h1'], tag + "lru_scan_bwd")
    dxc, g['w_g'], g['b4'], g['lru_lambda'] = _lru_gates_bwd(
        l0, l1, da0, da1, s['r0'], s['r1'], s['i0'], s['i1'], s['a0'], s['a1'], s['xc'], w['w_g'], w['lru_lambda'],
        t_real, tag + "lru_gates_bwd")
    dproj, g['lru_conv_w'], g['lru_conv_b'] = _lru_conv_bwd(dxc, s['proj'], w['lru_conv_w'], dproj, t_real, tag + "lru_conv_bwd")
    tok = emit('mid', g)
    dproj, dqn, dkvn = _mla_norms_bwd(dcqn, dckvn, _after(dkrp, tok), s['proj'], w['q_norm'], w['kv_norm'], dproj,
                                      tag + "mla_norms_bwd")
    tok = emit('in', {'w_in': _matmul(s['h'], dproj, tag + "proj_dw", **DW_MATMUL), 'q_norm': dqn, 'kv_norm': dkvn})
    dh_mm = _matmul(dproj, w['w_in'], tag + "proj_dx", tk_cap=1536)
    return [(DN_ALPHA, du1), (1.0, dh_mm)], tok


def _swap_halves(a, axis=-1):
    h1, h2 = jnp.split(a, 2, axis=axis)
    return jnp.concatenate([h2, h1], axis=axis)


def _w_in_kernel(w_in_t):
    cq, ckv, kr, lg, lx, gm, gl = jnp.split(w_in_t, [256, 384, 448, 1472, 2496, 3520], axis=0)
    return jnp.concatenate([lg, lx, gm, gl, cq, ckv, kr, _swap_halves(kr, axis=0)], axis=0)


def _layer_weights(fl):
    w = {}
    if 'w_uq' in fl:
        uq = fl['w_uq']
        w['w_q'] = jnp.concatenate([uq, _swap_halves(uq[..., QK_NOPE:])], axis=-1).reshape(Q_RANK, HEADS * 2 * LANE)
        w['w_kv'] = jnp.concatenate([fl['w_uk'].reshape(KV_RANK, -1), fl['w_uv'].reshape(KV_RANK, -1)], axis=1).astype(BF16)
        w['w_g'] = jnp.moveaxis(jnp.concatenate([fl['w_rg'], fl['w_ig']], axis=0), 0, 1).astype(BF16)
        w['b4'] = jnp.concatenate([fl['b_rg'], fl['b_ig']], axis=0)
    for n in ('q_norm', 'kv_norm', 'w_o_mla', 'lru_conv_w', 'lru_conv_b', 'lru_lambda', 'w_o_lru', 'w_out', 'ln1_g',
              'ln1_b', 'w_up', 'ffn_conv_w', 'ffn_conv_b', 'w_down', 'ln2_g', 'ln2_b'):
        if n in fl:
            w[n] = fl[n]
    return w


def _layer_grads(g):
    out = {}
    if 'w_in' in g:
        lg, lx, gm, gl, cq, ckv, kr, krs = jnp.split(g['w_in'], [1024, 2048, 3072, 4096, 4352, 4480, 4544], axis=1)
        out['w_in'] = jnp.concatenate([cq, ckv, kr + _swap_halves(krs), lg, lx, gm, gl], axis=1)
    if 'w_q' in g:
        gq = g['w_q'].reshape(Q_RANK, HEADS, 2 * LANE)
        out['w_uq'] = jnp.concatenate([gq[..., :QK_NOPE], gq[..., QK_NOPE:QK_NOPE + QK_ROPE] + _swap_halves(gq[..., QK_NOPE + QK_ROPE:])], axis=-1)
    if 'w_kv' in g:
        out['w_uk'] = g['w_kv'][:, :HEADS * QK_NOPE].reshape(KV_RANK, HEADS, QK_NOPE)
        out['w_uv'] = g['w_kv'][:, HEADS * QK_NOPE:].reshape(KV_RANK, HEADS, V_HEAD)
    if 'w_g' in g:
        gg = jnp.moveaxis(g['w_g'], 1, 0)
        out['w_rg'], out['w_ig'] = gg[:2], gg[2:]
    if 'b4' in g:
        out['b_rg'], out['b_ig'] = g['b4'][:2], g['b4'][2:]
    for n in ('q_norm', 'kv_norm', 'lru_conv_b', 'ln1_g', 'ln1_b', 'ffn_conv_b', 'ln2_g', 'ln2_b'):
        if n in g:
            out[n] = g[n].reshape(-1)
    for n in ('w_o_mla', 'lru_conv_w', 'lru_lambda', 'w_o_lru', 'w_out', 'w_up', 'ffn_conv_w', 'w_down'):
        if n in g:
            out[n] = g[n]
    return out


def _rope_table(tp):
    half = QK_ROPE // 2
    inv_freq = jnp.exp(-math.log(ROPE_THETA) * jnp.arange(half, dtype=F32) / half)
    ang = jnp.arange(tp, dtype=F32)[:, None] * inv_freq[None, :]
    c, s = jnp.cos(ang), jnp.sin(ang)
    return jnp.concatenate([c, c, -s, s], axis=1)


def _local_step(x, target, meta, ln0_g, ln0_b, layer_w, t_pad, emit):
    seq = x.shape[0]
    t_real = N_META + seq
    zpad = jnp.zeros((t_pad - t_real, D_MODEL), F32)
    xin = jnp.concatenate([meta, x, zpad], axis=0)
    tgt = jnp.concatenate([jnp.zeros((N_META, D_MODEL), F32), target, zpad], axis=0)
    cs = _rope_table(t_pad)
    h = _ln_fwd([(1.0, xin)], ln0_g, ln0_b, "ln0")
    saved = []
    for l in range(DEPTH):
        w_in, rest_of_weights = layer_w[l](h)
        h, s = _layer_fwd(h, w_in, rest_of_weights, cs, t_real, "l%d_" % l)
        saved.append(s)
    dy, lossvec = _loss_head(h, tgt, t_real, "loss_head")
    terms, tok = [(1.0, dy)], None
    for l in reversed(range(DEPTH)):
        terms, tok = _layer_bwd(terms, saved[l], cs, t_real, "l%d_" % l,
                                functools.partial(lambda stage, g, l: emit(l, stage, _layer_grads(g)), l=l), tok)
    dxin, dg0, db0 = _ln_bwd(terms, [(1.0, xin)], _after(ln0_g, tok), "ln0_bwd")
    emit(None, 'head', {'meta_tokens': dxin[:N_META], 'ln0_g': dg0.reshape(-1), 'ln0_b': db0.reshape(-1), 'loss': lossvec})
    return dxin[N_META:t_real]


_HBM = pl.BlockSpec(memory_space=pltpu.HBM)
_SEM = pl.BlockSpec(memory_space=pltpu.SEMAPHORE)
_SIDE_EFFECT = pltpu.SideEffectType.DATAFLOW_SIDE_EFFECTING


def _peer_copies(src_refs, land_refs, scatters, send_sems, recv_sems):
    x, y, c = lax.axis_index("x"), lax.axis_index("y"), lax.axis_index("c")
    me = 4 * x + 2 * y + c
    copies = []
    for k in range(1, N_DEV):
        px = 1 - x if k & 4 else x
        py = 1 - y if k & 2 else y
        pc = 1 - c if k & 1 else c
        for t, (src, land) in enumerate(zip(src_refs, land_refs)):
            copies.append(pltpu.make_async_remote_copy(
                src_ref=src.at[4 * px + 2 * py + pc] if scatters[t] else src, dst_ref=land.at[me],
                send_sem=send_sems.at[7 * t + k - 1], recv_sem=recv_sems.at[7 * t + k - 1],
                device_id=(px, py, pc), device_id_type=pl.DeviceIdType.MESH))
    return me, copies


def _own_block_in_place(land, own):
    me = 4 * lax.axis_index("x") + 2 * lax.axis_index("y") + lax.axis_index("c")
    return lax.dynamic_update_slice_in_dim(land, own, me, 0)


def _gather_two_level(shards, name):
    nt = len(shards)

    def body(*refs):
        x_refs, out_refs = refs[:nt], refs[nt:2 * nt]
        token_ref, send_sems, recv_sems = refs[2 * nt:]
        x, y, c = lax.axis_index("x"), lax.axis_index("y"), lax.axis_index("c")
        me, sibling = (x, y, c), (x, y, 1 - c)
        chips = [(1 - x, y), (x, 1 - y), (1 - x, 1 - y)]

        def copy(t, k, block, to, own=False):
            px, py, pc = block
            slot = out_refs[t].at[4 * px + 2 * py + pc]
            return pltpu.make_async_remote_copy(
                src_ref=x_refs[t] if own else slot, dst_ref=slot,
                send_sem=send_sems.at[7 * t + k], recv_sem=recv_sems.at[7 * t + k],
                device_id=to, device_id_type=pl.DeviceIdType.MESH)

        sent = []
        for t in range(nt):
            first = [copy(t, 1 + j, me, (*chip, c), own=True) for j, chip in enumerate(chips)]
            first.append(copy(t, 0, me, sibling, own=True))
            for cp in first:
                cp.start()
            sent += first
        token_ref[...] = jnp.zeros_like(token_ref)
        for j, chip in enumerate(chips):
            for t in range(nt):
                copy(t, 1 + j, (*chip, c), me).wait_recv()
                passed = copy(t, 4 + j, (*chip, c), sibling)
                passed.start()
                sent.append(passed)
        for t in range(nt):
            copy(t, 0, sibling, me).wait_recv()
            for j, chip in enumerate(chips):
                copy(t, 4 + j, (*chip, 1 - c), me).wait_recv()
        for cp in sent:
            cp.wait_send()

    any_space = pl.BlockSpec(memory_space=pl.ANY)
    res = pl.pallas_call(
        body, name=name,
        out_shape=[jax.ShapeDtypeStruct((N_DEV,) + a.shape, a.dtype) for a in shards] + [jax.ShapeDtypeStruct((SUBLANE, LANE), F32)],
        in_specs=[any_space] * nt, out_specs=[any_space] * nt + [pl.BlockSpec(memory_space=pltpu.VMEM)],
        scratch_shapes=[pltpu.SemaphoreType.DMA((7 * nt,)), pltpu.SemaphoreType.DMA((7 * nt,))],
    )(*shards)
    return [_own_block_in_place(land, a[None]) for land, a in zip(res[:nt], shards)], res[nt][0, 0]


def _exchange_start(groups, name):
    flat = [it for grp in groups for it in grp]
    nt, ng = len(flat), len(groups)
    scatters = [sc for _, sc in flat]
    srcs = [pltpu.with_memory_space_constraint(a, pltpu.HBM) for a, _ in flat]
    land_shapes = [a.shape if sc else (N_DEV,) + a.shape for a, sc in flat]
    lands = [pltpu.with_memory_space_constraint(lax.empty(s, a.dtype), pltpu.HBM) for s, (a, _) in zip(land_shapes, flat)]
    bounds = [0]
    for grp in groups:
        bounds.append(bounds[-1] + len(grp))

    def body(*refs):
        src_refs, land_refs = refs[:nt], refs[nt:2 * nt]
        sem_refs = refs[2 * nt:2 * nt + 2 * ng]
        token_ref = refs[4 * nt + 2 * ng]
        for gi in range(ng):
            lo, hi = bounds[gi], bounds[gi + 1]
            _, copies = _peer_copies(src_refs[lo:hi], land_refs[lo:hi], scatters[lo:hi], sem_refs[2 * gi], sem_refs[2 * gi + 1])
            for cp in copies:
                cp.start()
        token_ref[...] = jnp.zeros_like(token_ref)

    out_shape = []
    for grp in groups:
        out_shape += [pltpu.SemaphoreType.DMA((7 * len(grp),)), pltpu.SemaphoreType.DMA((7 * len(grp),))]
    out_shape += [pltpu.HBM(a.shape, a.dtype) for a in srcs] + [pltpu.HBM(s, a.dtype) for s, a in zip(land_shapes, srcs)]
    out_shape += [jax.ShapeDtypeStruct((SUBLANE, LANE), F32)]
    res = pl.pallas_call(
        body, name=name, out_shape=out_shape,
        in_specs=[_HBM] * (2 * nt),
        out_specs=[_SEM] * (2 * ng) + [_HBM] * (2 * nt) + [pl.BlockSpec(memory_space=pltpu.VMEM)],
        input_output_aliases={t: 2 * ng + t for t in range(2 * nt)},
        compiler_params=pltpu.CompilerParams(has_side_effects=_SIDE_EFFECT),
    )(*srcs, *lands)
    sems, thru, token = res[:2 * ng], res[2 * ng:2 * ng + 2 * nt], res[-1]
    states = []
    for gi in range(ng):
        lo, hi = bounds[gi], bounds[gi + 1]
        states.append((sems[2 * gi], sems[2 * gi + 1], thru[lo:hi], thru[nt + lo:nt + hi], scatters[lo:hi]))
    return states, token[0, 0]


def _exchange_wait(state, after, name):
    send_sems, recv_sems, srcs, lands, scatters = state
    n = len(srcs)

    def body(*refs):
        _, copies = _peer_copies(refs[:n], refs[n:2 * n], scatters, refs[2 * n], refs[2 * n + 1])
        for cp in copies:
            cp.wait_send()
        for cp in copies:
            cp.wait_recv()

    res = pl.pallas_call(
        body, name=name,
        out_shape=[pltpu.HBM(a.shape, a.dtype) for a in srcs] + [pltpu.HBM(a.shape, a.dtype) for a in lands],
        in_specs=[_HBM] * (2 * n) + [_SEM, _SEM, _HBM],
        out_specs=[_HBM] * (2 * n),
        input_output_aliases={t: t for t in range(2 * n)},
        compiler_params=pltpu.CompilerParams(has_side_effects=_SIDE_EFFECT),
    )(*srcs, *lands, send_sems, recv_sems, pltpu.with_memory_space_constraint(after, pltpu.HBM))
    me = 4 * lax.axis_index("x") + 2 * lax.axis_index("y") + lax.axis_index("c")
    out = []
    for src, land, sc in zip(res[:n], res[n:], scatters):
        own = lax.dynamic_index_in_dim(src, me, 0, keepdims=True) if sc else src[None]
        out.append(lax.dynamic_update_slice_in_dim(land, own, me, 0))
    return out


def _as_rows(shape):
    return (1, shape[0]) if len(shape) == 1 else (math.prod(shape[:-1]), shape[-1])


def _sum_adamw(pieces, w, m, v, name):
    shape = w.shape
    nl = len(pieces)
    if nl > 1 and _as_rows(shape[1:])[0] % 16:
        pieces, nl = [jnp.stack(pieces, axis=1)], 1
    rows, cols = _as_rows(shape)
    rl = rows // nl
    cap = max(16, (1 << 18) // cols // 16 * 16)
    tr = _tile(rl, cap, 16)
    nb = rl // tr
    c1 = 1.0 / (1.0 - ADAM_B1 ** ADAM_STEP)
    c2 = 1.0 / (1.0 - ADAM_B2 ** ADAM_STEP)

    def body(*refs):
        p_refs = refs[:nl]
        w_ref, m_ref, v_ref, g_ref, d_ref, nm_ref, nv_ref = refs[nl:]
        li = pl.program_id(0)

        def total(p_ref):
            acc = p_ref[0].astype(F32)
            for k in range(1, N_DEV):
                acc = acc + p_ref[k].astype(F32)
            return acc

        gg = total(p_refs[0])
        for l in range(1, nl):
            gg = jnp.where(li == l, total(p_refs[l]), gg)
        nm = ADAM_B1 * m_ref[...] + (1.0 - ADAM_B1) * gg
        nv = ADAM_B2 * v_ref[...] + (1.0 - ADAM_B2) * (gg * gg)
        g_ref[...] = gg
        d_ref[...] = -ADAM_LR * ((nm * c1) / (jnp.sqrt(nv * c2) + ADAM_EPS) + ADAM_WD * w_ref[...])
        nm_ref[...] = nm
        nv_ref[...] = nv

    blk = pl.BlockSpec((tr, cols), lambda li, i: (li * nb + i, 0))
    p_specs = [pl.BlockSpec((N_DEV, tr, cols), functools.partial(lambda li, i, l: (0, jnp.where(li == l, i, 0), 0), l=l))
               for l in range(nl)]
    res = pl.pallas_call(
        body, name=name, grid=(nl, nb),
        in_specs=p_specs + [blk] * 3, out_specs=[blk] * 4,
        out_shape=[jax.ShapeDtypeStruct((rows, cols), F32)] * 4,
        compiler_params=_cparams("parallel", "parallel"),
    )(*[p.reshape(N_DEV, rl, cols) for p in pieces], *[a.reshape(rows, cols) for a in (w, m, v)])
    return [r.reshape(shape) for r in res]


def _to_shards(full, axis):
    shp = full.shape
    a = full.reshape(shp[:axis] + (N_DEV, shp[axis] // N_DEV) + shp[axis + 1:])
    return jnp.moveaxis(a, axis, 0)


def _from_shards(blocks, axis):
    a = jnp.moveaxis(blocks, 0, axis)
    shp = a.shape
    return a.reshape(shp[:axis] + (shp[axis] * shp[axis + 1],) + shp[axis + 2:])


def kernel(x, meta_tokens, ln0_g, ln0_b, w_in, q_norm, kv_norm, w_uq, w_uk, w_uv, w_o_mla, lru_conv_w, lru_conv_b, w_rg, b_rg, w_ig, b_ig, lru_lambda, w_o_lru, w_out, ln1_g, ln1_b, w_up, ffn_conv_w, ffn_conv_b, w_down, ln2_g, ln2_b, loss_target, m_meta_tokens, m_ln0_g, m_ln0_b, m_w_in, m_q_norm, m_kv_norm, m_w_uq, m_w_uk, m_w_uv, m_w_o_mla, m_lru_conv_w, m_lru_conv_b, m_w_rg, m_b_rg, m_w_ig, m_b_ig, m_lru_lambda, m_w_o_lru, m_w_out, m_ln1_g, m_ln1_b, m_w_up, m_ffn_conv_w, m_ffn_conv_b, m_w_down, m_ln2_g, m_ln2_b, v_meta_tokens, v_ln0_g, v_ln0_b, v_w_in, v_q_norm, v_kv_norm, v_w_uq, v_w_uk, v_w_uv, v_w_o_mla, v_lru_conv_w, v_lru_conv_b, v_w_rg, v_b_rg, v_w_ig, v_b_ig, v_lru_lambda, v_w_o_lru, v_w_out, v_ln1_g, v_ln1_b, v_w_up, v_ffn_conv_w, v_ffn_conv_b, v_w_down, v_ln2_g, v_ln2_b):
    args = (meta_tokens, ln0_g, ln0_b, w_in, q_norm, kv_norm, w_uq, w_uk, w_uv, w_o_mla, lru_conv_w, lru_conv_b, w_rg, b_rg, w_ig, b_ig, lru_lambda, w_o_lru, w_out, ln1_g, ln1_b, w_up, ffn_conv_w, ffn_conv_b, w_down, ln2_g, ln2_b)
    ms = (m_meta_tokens, m_ln0_g, m_ln0_b, m_w_in, m_q_norm, m_kv_norm, m_w_uq, m_w_uk, m_w_uv, m_w_o_mla, m_lru_conv_w, m_lru_conv_b, m_w_rg, m_b_rg, m_w_ig, m_b_ig, m_lru_lambda, m_w_o_lru, m_w_out, m_ln1_g, m_ln1_b, m_w_up, m_ffn_conv_w, m_ffn_conv_b, m_w_down, m_ln2_g, m_ln2_b)
    vs = (v_meta_tokens, v_ln0_g, v_ln0_b, v_w_in, v_q_norm, v_kv_norm, v_w_uq, v_w_uk, v_w_uv, v_w_o_mla, v_lru_conv_w, v_lru_conv_b, v_w_rg, v_b_rg, v_w_ig, v_b_ig, v_lru_lambda, v_w_o_lru, v_w_out, v_ln1_g, v_ln1_b, v_w_up, v_ffn_conv_w, v_ffn_conv_b, v_w_down, v_ln2_g, v_ln2_b)
    wd, md, vd = dict(zip(WEIGHTS, args)), dict(zip(WEIGHTS, ms)), dict(zip(WEIGHTS, vs))

    def shard_axis(n, l):
        return SHARD_AXIS[n] - (0 if l is None else 1)

    def shard(n, l):
        a = wd[n] if l is None else wd[n][l]
        if n in SENT_TRANSPOSED:
            a = a.T
        return a.astype(BF16) if n in BIG else a

    def whole(keys, landed):
        return {k: b.reshape(-1, b.shape[-1]) if k[0] in SENT_TRANSPOSED else _from_shards(b, shard_axis(*k))
                for k, b in zip(keys, landed)}

    first = [('meta_tokens', None), ('w_in', 0)]
    landed, token = _gather_two_level([shard(*k) for k in first], "gather_first")
    got_first = whole(first, landed)
    staged = [[(n, 0) for n in names] for names in STAGE_WEIGHTS]
    later = [(n, 1) for n in SHARDED if n != 'meta_tokens']
    gather, token = _exchange_start([[(_after(shard(*k), token), False) for k in keys] for keys in staged + [later]], "gather_start")

    def arrive(gi, keys, after, name):
        return whole(keys, _exchange_wait(gather[gi], after, name))

    def layer_weights(got, l, names):
        fl = {n: wd[n][l] for n in names if n in REPLICATED}
        fl.update({n: a for (n, _), a in got.items() if n in names})
        return _layer_weights(fl)

    ln0_g = _after(wd['ln0_g'], token)

    def first_layer(h):
        def more(stage, after):
            got = arrive(stage, staged[stage], after, "gather_wait_l0_%d" % stage)
            return layer_weights(got, 0, STAGE_WEIGHTS[stage] + STAGE_REPLICATED[stage])
        return _w_in_kernel(got_first['w_in', 0]), more

    def second_layer(h):
        got = arrive(len(staged), later, h, "gather_wait_l1")
        return _w_in_kernel(got['w_in', 1]), lambda stage, after: layer_weights(got, 1, STAGE_WEIGHTS[stage] + STAGE_REPLICATED[stage])

    sent = []
    pending = []

    def send(l, stage, grads):
        for n, g in grads.items():
            if n in SHARD_AXIS:
                g = _to_shards(g, shard_axis(n, l))
                pending.append(((n, l), (g.astype(BF16) if n in BIG else g, True)))
            else:
                pending.append(((n, l), (g.astype(BF16) if n in LARGE_REPLICATED else g, False)))
        if l == DEPTH - 1 and stage != 'in':
            return None
        (state,), tok = _exchange_start([[it for _, it in pending]], "grads_start_%s_%s" % (l, stage))
        sent.append(([k for k, _ in pending], state))
        pending.clear()
        return tok

    seq = x.shape[1]
    t_pad = -(-(N_META + seq + MIN_PAD_ROWS) // LANE) * LANE
    grad_x = _local_step(x[0], loss_target[0], got_first['meta_tokens', None], ln0_g, wd['ln0_b'],
                         [first_layer, second_layer], t_pad, send)

    pieces = {}
    for gi, (keys, state) in enumerate(sent):
        pieces.update(zip(keys, _exchange_wait(state, grad_x, "grads_wait_%d" % gi)))
    loss = jnp.sum(pieces['loss', None])
    outs = {}
    for n in WEIGHTS:
        ps = [pieces[n, None]] if (n, None) in pieces else [pieces[n, l] for l in range(DEPTH)]
        outs[n] = _sum_adamw(ps, wd[n], md[n], vd[n], "adamw_" + n)
    res = [loss, grad_x[None]]
    for k in range(4):
        res += [outs[n][k] for n in WEIGHTS]
    return tuple(res)
```

```python
import functools
import math

import jax
import jax.numpy as jnp
from jax import lax
from jax.experimental import pallas as pl
from jax.experimental.pallas import tpu as pltpu

F32 = jnp.float32
BF16 = jnp.bfloat16

N_DEV = 8
D_MODEL = 1024
N_META = 16
HEADS = 8
QK_NOPE = 128
QK_ROPE = 64
V_HEAD = 128
Q_RANK = 256
KV_RANK = 128
ROPE_THETA = 10000.0
LRU_BLOCKS = 8
LRU_C = 8.0
D_FF = 2816
DEPTH = 2
DN_ALPHA = (2.0 * DEPTH) ** 0.25
LN_EPS = 1e-5
RMS_EPS = 1e-6
LN2 = math.log(2.0)
ATT_SCALE = 1.0 / math.sqrt(QK_NOPE + QK_ROPE) / LN2
NEG_BIG = -1e30

ADAM_LR = 0.001
ADAM_B1 = 0.9
ADAM_B2 = 0.999
ADAM_EPS = 1e-08
ADAM_WD = 0.01
ADAM_STEP = 10

MIN_PAD_ROWS = 2
LANE = 128
SUBLANE = 8
VMEM_LIMIT = 56 * 1024 * 1024

PROJ_COLS = 4 * D_MODEL + Q_RANK + KV_RANK + 2 * QK_ROPE
C_LRU_G, C_LRU_X, C_G_MLA, C_G_LRU = 0, D_MODEL, 2 * D_MODEL, 3 * D_MODEL
C_CQ = 4 * D_MODEL
C_CKV = C_CQ + Q_RANK
C_KRP = C_CKV + KV_RANK

WEIGHTS = ['meta_tokens', 'ln0_g', 'ln0_b', 'w_in', 'q_norm', 'kv_norm', 'w_uq', 'w_uk', 'w_uv', 'w_o_mla',
           'lru_conv_w', 'lru_conv_b', 'w_rg', 'b_rg', 'w_ig', 'b_ig', 'lru_lambda', 'w_o_lru', 'w_out',
           'ln1_g', 'ln1_b', 'w_up', 'ffn_conv_w', 'ffn_conv_b', 'w_down', 'ln2_g', 'ln2_b']
SHARD_AXIS = {'meta_tokens': 1, 'w_in': 2, 'w_uq': 1, 'w_o_mla': 1, 'lru_conv_w': 2, 'b_rg': 2, 'b_ig': 2,
              'lru_lambda': 2, 'w_o_lru': 1, 'w_out': 1, 'w_up': 2, 'ffn_conv_w': 2, 'w_down': 1}
BIG = ['w_in', 'w_uq', 'w_o_mla', 'w_o_lru', 'w_out', 'w_up', 'w_down']
SHARDED = [n for n in WEIGHTS if n in SHARD_AXIS]
REPLICATED = [n for n in WEIGHTS if n not in SHARD_AXIS]
LARGE_REPLICATED = ['w_uk', 'w_uv', 'w_rg', 'w_ig']
SENT_TRANSPOSED = ['w_in', 'w_up']
STAGE_WEIGHTS = [['w_uq', 'lru_conv_w', 'b_rg', 'b_ig', 'lru_lambda'], ['w_o_mla', 'w_o_lru', 'w_out'], ['w_up', 'ffn_conv_w', 'w_down']]
STAGE_REPLICATED = [['q_norm', 'kv_norm', 'w_uk', 'w_uv', 'lru_conv_b', 'w_rg', 'w_ig'], ['ln1_g', 'ln1_b'], ['ffn_conv_b', 'ln2_g', 'ln2_b']]


def _cparams(*sem):
    return pltpu.CompilerParams(dimension_semantics=sem, vmem_limit_bytes=VMEM_LIMIT)


def _tile(n, cap, unit=LANE):
    best = None
    t = unit
    while t <= min(n, cap):
        if n % t == 0:
            best = t
        t += unit
    return n if best is None else best


def _sigmoid(x):
    return 1.0 / (1.0 + jnp.exp(-x))


_GELU_C = math.sqrt(2.0 / math.pi)


_GELU_A = 0.044715


def _gelu(x):
    t = jnp.tanh(x * (_GELU_C + (_GELU_C * _GELU_A) * (x * x)))
    hx = 0.5 * x
    return hx + hx * t


def _gelu_and_grad(x):
    x2 = x * x
    t = jnp.tanh(x * (_GELU_C + (_GELU_C * _GELU_A) * x2))
    hx = 0.5 * x
    dg = 0.5 + 0.5 * t + (hx * (1.0 - t * t)) * (_GELU_C + (3.0 * _GELU_C * _GELU_A) * x2)
    return hx + hx * t, dg


def _softplus_neg(lam):
    z = jnp.exp(-jnp.abs(lam))
    w = 1.0 + z
    log1p = jnp.where(w == 1.0, z, jnp.log(w) * z / (w - 1.0))
    return jnp.maximum(-lam, 0.0) + log1p


def _row_ids(shape, row0=0):
    return lax.broadcasted_iota(jnp.int32, shape, 0) + row0


def _matmul(a, b, name, ta=False, tb=False, out_dtype=F32, tm_cap=1408, tn_cap=1024, tk_cap=2048):
    if ta:
        kdim, m = a.shape
    else:
        m, kdim = a.shape
    if tb:
        n, k2 = b.shape
    else:
        k2, n = b.shape
    assert kdim == k2, (a.shape, b.shape, ta, tb)
    tm, tn, tk = _tile(m, tm_cap), _tile(n, tn_cap), _tile(kdim, tk_cap)
    nk = kdim // tk

    def body(a_ref, b_ref, o_ref, *acc):
        dn = (((0 if ta else 1,), (1 if tb else 0,)), ((), ()))
        part = lax.dot_general(a_ref[...].astype(BF16), b_ref[...].astype(BF16), dn, preferred_element_type=F32)
        if nk == 1:
            o_ref[...] = part.astype(o_ref.dtype)
            return
        acc_ref, k = acc[0], pl.program_id(2)

        @pl.when(k == 0)
        def _():
            acc_ref[...] = part

        @pl.when(k > 0)
        def _():
            acc_ref[...] += part

        @pl.when(k == nk - 1)
        def _():
            o_ref[...] = acc_ref[...].astype(o_ref.dtype)

    a_spec = pl.BlockSpec((tk, tm), lambda i, j, k: (k, i)) if ta else pl.BlockSpec((tm, tk), lambda i, j, k: (i, k))
    b_spec = pl.BlockSpec((tn, tk), lambda i, j, k: (j, k)) if tb else pl.BlockSpec((tk, tn), lambda i, j, k: (k, j))
    return pl.pallas_call(
        body, name=name,
        grid=(m // tm, n // tn, nk),
        in_specs=[a_spec, b_spec],
        out_specs=pl.BlockSpec((tm, tn), lambda i, j, k: (i, j)),
        out_shape=jax.ShapeDtypeStruct((m, n), out_dtype),
        scratch_shapes=[pltpu.VMEM((tm, tn), F32)] if nk > 1 else [],
        compiler_params=_cparams("parallel", "parallel", "arbitrary"),
    )(a, b)


class Rw:
    def __init__(self, arr, width=None, cb=0):
        self.arr, self.width, self.cb = arr, (arr.shape[1] if width is None else width), cb


class Pm:
    def __init__(self, arr):
        self.arr = arr


class Into:
    def __init__(self, arr, col0, width):
        self.arr, self.col0, self.width = arr, col0, width


def _call_with_into(body, name, grid, in_specs, operands, outs, spec_of, shape_of, extra_out_specs, extra_out_shape, sem):
    intos = [(k, o) for k, o in enumerate(outs) if isinstance(o, Into)]
    aliases = {len(operands) + n: k for n, (k, _) in enumerate(intos)}
    return pl.pallas_call(
        body, name=name, grid=grid,
        in_specs=in_specs + [pl.BlockSpec(memory_space=pl.ANY)] * len(intos),
        out_specs=[spec_of(o) for o in outs] + extra_out_specs,
        out_shape=[jax.ShapeDtypeStruct(o.arr.shape, o.arr.dtype) if isinstance(o, Into) else shape_of(o) for o in outs]
        + extra_out_shape,
        input_output_aliases=aliases,
        compiler_params=_cparams(sem),
    )(*operands, *[o.arr for _, o in intos])


def _rows(fn, name, ins, outs, accs=(), tm_cap=384):
    tp = next(o.arr.shape[0] for o in ins if isinstance(o, Rw))
    tm = _tile(tp, tm_cap)
    n_in, n_out, n_acc = len(ins), len(outs), len(accs)
    n_into = sum(isinstance(o, Into) for o in outs)

    def body(*refs):
        i = pl.program_id(0)
        res = fn(i * tm, *[r[...] for r in refs[:n_in]])
        if not isinstance(res, (tuple, list)):
            res = (res,)
        assert len(res) == n_out + n_acc, (name, len(res))
        out_refs = refs[n_in + n_into:]
        for k in range(n_out):
            out_refs[k][...] = res[k].astype(out_refs[k].dtype)
        for k in range(n_acc):
            ref = out_refs[n_out + k]

            @pl.when(i == 0)
            def _():
                ref[...] = jnp.zeros_like(ref)

            ref[...] += res[n_out + k]

    in_specs = []
    for o in ins:
        if isinstance(o, Rw):
            in_specs.append(pl.BlockSpec((tm, o.width), functools.partial(lambda i, cb: (i, cb), cb=o.cb)))
        else:
            in_specs.append(pl.BlockSpec(o.arr.shape, functools.partial(lambda i, nd: (0,) * nd, nd=o.arr.ndim)))

    def spec_of(o):
        if isinstance(o, Into):
            assert o.col0 % o.width == 0, (name, o.col0, o.width)
            return pl.BlockSpec((tm, o.width), functools.partial(lambda i, cb: (i, cb), cb=o.col0 // o.width))
        return pl.BlockSpec((tm, o[0]), lambda i: (i, 0))

    return _call_with_into(
        body, name, (tp // tm,), in_specs, [o.arr for o in ins], list(outs), spec_of,
        lambda o: jax.ShapeDtypeStruct((tp, o[0]), o[1]),
        [pl.BlockSpec(s, functools.partial(lambda i, nd: (0,) * nd, nd=len(s))) for s in accs],
        [jax.ShapeDtypeStruct(s, F32) for s in accs], "arbitrary")


class Cl:
    def __init__(self, arr, col0=0):
        self.arr, self.col0 = arr, col0


def _cols(fn, name, ins, outs, ncols, tc):
    assert ncols % tc == 0
    n_in, n_out = len(ins), len(outs)
    n_into = sum(isinstance(o, Into) for o in outs)

    def body(*refs):
        res = fn(*[r[...] for r in refs[:n_in]])
        if not isinstance(res, (tuple, list)):
            res = (res,)
        assert len(res) == n_out, (name, len(res))
        out_refs = refs[n_in + n_into:]
        for k in range(n_out):
            out_refs[k][...] = res[k].astype(out_refs[k].dtype)

    in_specs = []
    for o in ins:
        assert o.col0 % tc == 0, (name, o.col0, tc)
        in_specs.append(pl.BlockSpec((o.arr.shape[0], tc), functools.partial(lambda j, off: (0, j + off), off=o.col0 // tc)))

    def spec_of(o):
        if isinstance(o, Into):
            assert o.col0 % tc == 0 and o.width == ncols, (name, o.col0, o.width)
            return pl.BlockSpec((o.arr.shape[0], tc), functools.partial(lambda j, off: (0, j + off), off=o.col0 // tc))
        return pl.BlockSpec((o[0], tc), lambda j: (0, j))

    return _call_with_into(body, name, (ncols // tc,), in_specs, [o.arr for o in ins], list(outs), spec_of,
                           lambda o: jax.ShapeDtypeStruct((o[0], ncols), o[1]), [], [], "parallel")


def _ln_stats(u):
    mu = jnp.mean(u, axis=-1, keepdims=True)
    xc = u - mu
    var = jnp.mean(xc * xc, axis=-1, keepdims=True)
    rstd = lax.rsqrt(var + LN_EPS)
    return xc * rstd, rstd


def _ln_fwd(terms, g, b, name):
    coefs = [c for c, _ in terms]

    def fn(row0, *blk):
        xs, (gg, bb) = blk[:len(coefs)], blk[len(coefs):]
        u = sum(c * x for c, x in zip(coefs, xs))
        xhat, _ = _ln_stats(u)
        return xhat * gg + bb

    d = terms[0][1].shape[1]
    return _rows(fn, name, [Rw(x) for _, x in terms] + [Pm(g.reshape(1, d)), Pm(b.reshape(1, d))], [(d, F32)])[0]


def _ln_bwd(dy_terms, u_terms, g, name):
    dc = [c for c, _ in dy_terms]
    uc = [c for c, _ in u_terms]
    d = u_terms[0][1].shape[1]

    def fn(row0, *blk):
        dys = blk[:len(dc)]
        xs = blk[len(dc):len(dc) + len(uc)]
        gg = blk[-1]
        dy = sum(c * x for c, x in zip(dc, dys))
        u = sum(c * x for c, x in zip(uc, xs))
        xhat, rstd = _ln_stats(u)
        gdy = dy * gg
        m1 = jnp.mean(gdy, axis=-1, keepdims=True)
        m2 = jnp.mean(gdy * xhat, axis=-1, keepdims=True)
        du = rstd * (gdy - m1 - xhat * m2)
        return du, jnp.sum(dy * xhat, axis=0, keepdims=True), jnp.sum(dy, axis=0, keepdims=True)

    ins = [Rw(x) for _, x in dy_terms] + [Rw(x) for _, x in u_terms] + [Pm(g.reshape(1, d))]
    return _rows(fn, name, ins, [(d, F32)], accs=[(1, d), (1, d)])


def _loss_head(y, tgt, t_real, name):
    d = y.shape[1]

    def fn(row0, yb, tb):
        rows = _row_ids(yb.shape, row0)
        live = (rows >= N_META) & (rows < t_real)
        diff = jnp.where(live, yb - tb, 0.0)
        return diff * (1.0 / d), jnp.sum(diff * diff, axis=0, keepdims=True) * (0.5 / d)

    return _rows(fn, name, [Rw(y), Rw(tgt)], [(d, F32)], accs=[(1, d)])


def _rms(x, g):
    r = lax.rsqrt(jnp.mean(x * x, axis=-1, keepdims=True) + RMS_EPS)
    return x * r * g


def _rms_bwd(dy, x, g):
    r = lax.rsqrt(jnp.mean(x * x, axis=-1, keepdims=True) + RMS_EPS)
    gdy = dy * g
    dx = r * gdy - x * (r * r * r) * jnp.mean(gdy * x, axis=-1, keepdims=True)
    return dx, jnp.sum(dy * x * r, axis=0, keepdims=True)


def _mla_norms(proj, qn, kvn, name):
    def fn(row0, cq, ckv, g1, g2):
        return _rms(cq, g1), _rms(ckv, g2)

    return _rows(fn, name, [Rw(proj, Q_RANK, C_CQ // Q_RANK), Rw(proj, KV_RANK, C_CKV // KV_RANK),
                            Pm(qn.reshape(1, Q_RANK)), Pm(kvn.reshape(1, KV_RANK))],
                 [(Q_RANK, F32), (KV_RANK, F32)])


def _mla_norms_bwd(dcqn, dckvn, dkrp, proj, qn, kvn, dproj, name):
    def fn(row0, d1, d2, dkr, cq, ckv, g1, g2):
        dx1, dg1 = _rms_bwd(d1, cq, g1)
        dx2, dg2 = _rms_bwd(d2, ckv, g2)
        return jnp.concatenate([dx1, dx2, dkr], axis=1), dg1, dg2

    return _rows(fn, name, [Rw(dcqn), Rw(dckvn), Rw(dkrp), Rw(proj, Q_RANK, C_CQ // Q_RANK), Rw(proj, KV_RANK, C_CKV // KV_RANK),
                            Pm(qn.reshape(1, Q_RANK)), Pm(kvn.reshape(1, KV_RANK))],
                 [Into(dproj, C_CQ, PROJ_COLS - C_CQ)], accs=[(1, Q_RANK), (1, KV_RANK)])


def _fold_rope(z):
    return z + pltpu.roll(z, QK_ROPE, 1)


def _mla_pack(qext, kv, proj, cs, name):
    tp = qext.shape[0]
    tm = _tile(tp, 384)
    hw, nope_all = 2 * LANE, HEADS * QK_NOPE

    def body(q_ref, kv_ref, kr_ref, cs_ref, qo_ref, ko_ref, vo_ref):
        cs_ = cs_ref[...]
        low = lax.broadcasted_iota(jnp.int32, cs_.shape, 1) < QK_ROPE
        kr = _fold_rope(kr_ref[...] * cs_).astype(BF16)
        for h in range(HEADS):
            qr = jnp.where(low, _fold_rope(q_ref[:, h * hw + QK_NOPE:(h + 1) * hw] * cs_), 0.0)
            qo_ref[:, h * hw:h * hw + QK_NOPE] = (q_ref[:, h * hw:h * hw + QK_NOPE] * ATT_SCALE).astype(BF16)
            qo_ref[:, h * hw + QK_NOPE:(h + 1) * hw] = (qr * ATT_SCALE).astype(BF16)
            ko_ref[:, h * hw:h * hw + QK_NOPE] = kv_ref[:, h * QK_NOPE:(h + 1) * QK_NOPE].astype(BF16)
            ko_ref[:, h * hw + QK_NOPE:(h + 1) * hw] = kr
        vo_ref[...] = kv_ref[:, nope_all:].astype(BF16)

    row = lambda w: pl.BlockSpec((tm, w), lambda i: (i, 0))
    return pl.pallas_call(
        body, name=name, grid=(tp // tm,),
        in_specs=[row(HEADS * hw), row(2 * nope_all), pl.BlockSpec((tm, LANE), lambda i: (i, C_KRP // LANE)), row(LANE)],
        out_specs=[row(HEADS * hw), row(HEADS * hw), row(nope_all)],
        out_shape=[jax.ShapeDtypeStruct((tp, HEADS * hw), BF16),
                   jax.ShapeDtypeStruct((tp, HEADS * hw), BF16),
                   jax.ShapeDtypeStruct((tp, nope_all), BF16)],
        compiler_params=_cparams("parallel"),
    )(qext, kv, proj, cs)


def _mla_unpack(dk, dv, cs, name):
    tp = dk.shape[0]
    tm = _tile(tp, 384)
    hw, nope_all = 2 * LANE, HEADS * QK_NOPE

    def body(dk_ref, dv_ref, cs_ref, dkv_ref, dkr_ref):
        cs_ = cs_ref[...]
        low = lax.broadcasted_iota(jnp.int32, cs_.shape, 1) < QK_ROPE
        dkr = None
        for h in range(HEADS):
            dkv_ref[:, h * QK_NOPE:(h + 1) * QK_NOPE] = dk_ref[:, h * hw:h * hw + QK_NOPE]
            part = jnp.where(low, dk_ref[:, h * hw + QK_NOPE:(h + 1) * hw], 0.0)
            dkr = part if h == 0 else dkr + part
        dkv_ref[:, nope_all:] = dv_ref[...]
        dkr_ref[...] = _fold_rope(dkr) * cs_

    row = lambda w: pl.BlockSpec((tm, w), lambda i: (i, 0))
    return pl.pallas_call(
        body, name=name, grid=(tp // tm,),
        in_specs=[row(HEADS * hw), row(nope_all), row(LANE)],
        out_specs=[row(2 * nope_all), row(LANE)],
        out_shape=[jax.ShapeDtypeStruct((tp, 2 * nope_all), F32), jax.ShapeDtypeStruct((tp, LANE), F32)],
        compiler_params=_cparams("parallel"),
    )(dk, dv, cs)


def _attn_fwd(q, k, v, t_real, name):
    tp = q.shape[0]
    tq = _tile(tp, 1408)
    tkc = _tile(tp, 1408)
    nkc = -(-t_real // tkc)

    def body(q_ref, k_ref, v_ref, o_ref, lse_ref):
        qb = q_ref[...]
        m = l = acc = None
        for c in range(nkc):
            s = lax.dot_general(qb, k_ref[c * tkc:(c + 1) * tkc, :], (((1,), (1,)), ((), ())), preferred_element_type=F32)
            if (c + 1) * tkc > t_real:
                cols = lax.broadcasted_iota(jnp.int32, s.shape, 1) + c * tkc
                s = jnp.where(cols < t_real, s, NEG_BIG)
            mc = jnp.max(s, axis=-1, keepdims=True)
            m_new = mc if c == 0 else jnp.maximum(m, mc)
            p = jnp.exp2(s - m_new)
            lc = jnp.sum(p, axis=-1, keepdims=True)
            pv = jnp.dot(p.astype(BF16), v_ref[c * tkc:(c + 1) * tkc, :], preferred_element_type=F32)
            if c == 0:
                l, acc = lc, pv
            else:
                alpha = jnp.exp2(m - m_new)
                l, acc = alpha * l + lc, alpha * acc + pv
            m = m_new
        o_ref[...] = acc / l
        lse_ref[...] = m + jnp.log2(l)

    return pl.pallas_call(
        body, name=name, grid=(HEADS, tp // tq),
        in_specs=[pl.BlockSpec((tq, 2 * LANE), lambda h, i: (i, h)),
                  pl.BlockSpec((tp, 2 * LANE), lambda h, i: (0, h)),
                  pl.BlockSpec((tp, LANE), lambda h, i: (0, h))],
        out_specs=[pl.BlockSpec((tq, LANE), lambda h, i: (i, h)),
                   pl.BlockSpec((None, tq, 1), lambda h, i: (h, i, 0))],
        out_shape=[jax.ShapeDtypeStruct((tp, HEADS * LANE), F32),
                   jax.ShapeDtypeStruct((HEADS, tp, 1), F32)],
        compiler_params=_cparams("parallel", "parallel"),
    )(q, k, v)


def _attn_bwd(q, k, v, do, o, lse, cs, t_real, name):
    tp = q.shape[0]
    tq = _tile(tp, 1408)
    tkc = _tile(tp, 704, 64)
    nkc = -(-t_real // tkc)

    def body(q_ref, k_ref, v_ref, do_ref, o_ref, lse_ref, cs_ref, dq_ref, dk_ref, dv_ref):
        i = pl.program_id(1)

        @pl.when(i == 0)
        def _():
            dk_ref[...] = jnp.zeros_like(dk_ref)
            dv_ref[...] = jnp.zeros_like(dv_ref)

        qb = q_ref[...]
        dob = do_ref[...]
        dob16 = dob.astype(BF16)
        dol2 = (dob * LN2).astype(BF16)
        delta = jnp.sum(dob * o_ref[...], axis=-1, keepdims=True) * LN2
        lse = lse_ref[...]
        dq = None
        for c in range(nkc):
            ks = slice(c * tkc, (c + 1) * tkc)
            kb = k_ref[ks, :]
            s = lax.dot_general(qb, kb, (((1,), (1,)), ((), ())), preferred_element_type=F32)
            p = jnp.exp2(s - lse)
            if (c + 1) * tkc > t_real:
                cols = lax.broadcasted_iota(jnp.int32, s.shape, 1) + c * tkc
                p = jnp.where(cols < t_real, p, 0.0)
            dp = lax.dot_general(dol2, v_ref[ks, :], (((1,), (1,)), ((), ())), preferred_element_type=F32)
            ds = (p * (dp - delta)).astype(BF16)
            dqc = jnp.dot(ds, kb, preferred_element_type=F32)
            dq = dqc if c == 0 else dq + dqc
            dk_ref[ks, :] += lax.dot_general(ds, qb, (((0,), (0,)), ((), ())), preferred_element_type=F32)
            dv_ref[ks, :] += lax.dot_general(p.astype(BF16), dob16, (((0,), (0,)), ((), ())), preferred_element_type=F32)
        cs_ = cs_ref[...]
        low = lax.broadcasted_iota(jnp.int32, cs_.shape, 1) < QK_ROPE
        dq = dq * ATT_SCALE
        dq_ref[:, :QK_NOPE] = dq[:, :QK_NOPE]
        dq_ref[:, QK_NOPE:] = _fold_rope(jnp.where(low, dq[:, QK_NOPE:], 0.0)) * cs_

    return pl.pallas_call(
        body, name=name, grid=(HEADS, tp // tq),
        in_specs=[pl.BlockSpec((tq, 2 * LANE), lambda h, i: (i, h)),
                  pl.BlockSpec((tp, 2 * LANE), lambda h, i: (0, h)),
                  pl.BlockSpec((tp, LANE), lambda h, i: (0, h)),
                  pl.BlockSpec((tq, LANE), lambda h, i: (i, h)),
                  pl.BlockSpec((tq, LANE), lambda h, i: (i, h)),
                  pl.BlockSpec((None, tq, 1), lambda h, i: (h, i, 0)),
                  pl.BlockSpec((tq, LANE), lambda h, i: (i, 0))],
        out_specs=[pl.BlockSpec((tq, 2 * LANE), lambda h, i: (i, h)),
                   pl.BlockSpec((tp, 2 * LANE), lambda h, i: (0, h)),
                   pl.BlockSpec((tp, LANE), lambda h, i: (0, h))],
        out_shape=[jax.ShapeDtypeStruct((tp, HEADS * 2 * LANE), F32),
                   jax.ShapeDtypeStruct((tp, HEADS * 2 * LANE), F32),
                   jax.ShapeDtypeStruct((tp, HEADS * LANE), F32)],
        compiler_params=_cparams("parallel", "arbitrary"),
    )(q, k, v, do, o, lse, cs)


def _shift_rows(x, s):
    tp = x.shape[0]
    return x if s % tp == 0 else pltpu.roll(x, s % tp, 0)


def _conv_fwd_val(xm, w, b, pad_left):
    acc = b + w[0:1, :] * _shift_rows(xm, pad_left)
    for k in range(1, w.shape[0]):
        acc = acc + w[k:k + 1, :] * _shift_rows(xm, pad_left - k)
    return acc


def _conv_bwd_val(dy, xm, w, pad_left, live):
    kk = w.shape[0]
    dx = w[0:1, :] * _shift_rows(dy, -pad_left)
    dws = [jnp.sum(dy * _shift_rows(xm, pad_left), axis=0, keepdims=True)]
    for k in range(1, kk):
        dx = dx + w[k:k + 1, :] * _shift_rows(dy, k - pad_left)
        dws.append(jnp.sum(dy * _shift_rows(xm, pad_left - k), axis=0, keepdims=True))
    return jnp.where(live, dx, 0.0), jnp.concatenate(dws, axis=0), jnp.sum(dy, axis=0, keepdims=True)


def _lru_conv_fwd(proj, w, b, t_real, name):
    def fn(x, ww, bb):
        xm = jnp.where(_row_ids(x.shape) < t_real, x, 0.0)
        return _conv_fwd_val(xm, ww, bb, 2)

    return _cols(fn, name, [Cl(proj, C_LRU_X), Cl(w), Cl(b.reshape(1, -1))], [(proj.shape[0], F32)], D_MODEL, 128)[0]


def _lru_conv_bwd(dxc, proj, w, dproj, t_real, name):
    def fn(dy, x, ww):
        live = _row_ids(x.shape) < t_real
        xm = jnp.where(live, x, 0.0)
        dym = jnp.where(live, dy, 0.0)
        return _conv_bwd_val(dym, xm, ww, 2, live)

    return _cols(fn, name, [Cl(dxc), Cl(proj, C_LRU_X), Cl(w)],
                 [Into(dproj, C_LRU_X, D_MODEL), (w.shape[0], F32), (1, F32)], D_MODEL, 128)


def _ffn_conv_act(up, w, b, t_real, name):
    def fn(g, v, wg, wv, bg, bv):
        live = _row_ids(g.shape) < t_real
        gc = _conv_fwd_val(jnp.where(live, g, 0.0), wg, bg, 1)
        vc = _conv_fwd_val(jnp.where(live, v, 0.0), wv, bv, 1)
        return _gelu(gc) * vc

    b2 = b.reshape(1, -1)
    return _cols(fn, name, [Cl(up), Cl(up, D_FF), Cl(w), Cl(w, D_FF), Cl(b2), Cl(b2, D_FF)],
                 [(up.shape[0], F32)], D_FF, 128)[0]


def _ffn_conv_act_bwd(dm, up, w, b, t_real, name):
    tp, kk = up.shape[0], w.shape[0]
    nb = D_FF // LANE
    assert nb >= 2

    def body(dm_ref, g_ref, v_ref, wg_ref, wv_ref, bg_ref, bv_ref, dup_ref, dwg_ref, dwv_ref, dbg_ref, dbv_ref, stage, sems):
        j = pl.program_id(0)
        slot = j % 2

        def copies(step, sl):
            return [pltpu.make_async_copy(stage.at[sl, half],
                                          dup_ref.at[:, pl.ds(pl.multiple_of(half * D_FF + step * LANE, LANE), LANE)],
                                          sems.at[sl, half]) for half in range(2)]

        @pl.when(j >= 2)
        def _():
            for cp in copies(j - 2, slot):
                cp.wait()

        live = _row_ids((tp, LANE)) < t_real
        gm, vm = jnp.where(live, g_ref[...], 0.0), jnp.where(live, v_ref[...], 0.0)
        gc = _conv_fwd_val(gm, wg_ref[...], bg_ref[...], 1)
        vc = _conv_fwd_val(vm, wv_ref[...], bv_ref[...], 1)
        act, dact = _gelu_and_grad(gc)
        dmm = jnp.where(live, dm_ref[...], 0.0)
        stage[slot, 0], dwg_ref[...], dbg_ref[...] = _conv_bwd_val(dmm * vc * dact, gm, wg_ref[...], 1, live)
        stage[slot, 1], dwv_ref[...], dbv_ref[...] = _conv_bwd_val(dmm * act, vm, wv_ref[...], 1, live)
        for cp in copies(j, slot):
            cp.start()

        @pl.when(j == nb - 1)
        def _():
            for cp in copies(j - 1, 1 - slot) + copies(j, slot):
                cp.wait()

    b2 = b.reshape(1, -1)
    col = lambda rows, off: pl.BlockSpec((rows, LANE), functools.partial(lambda j, o: (0, j + o), o=off))
    return pl.pallas_call(
        body, name=name, grid=(nb,),
        in_specs=[col(tp, 0), col(tp, 0), col(tp, nb), col(kk, 0), col(kk, nb), col(1, 0), col(1, nb)],
        out_specs=[pl.BlockSpec(memory_space=pl.ANY), col(kk, 0), col(kk, 0), col(1, 0), col(1, 0)],
        out_shape=[jax.ShapeDtypeStruct((tp, 2 * D_FF), F32), jax.ShapeDtypeStruct((kk, D_FF), F32),
                   jax.ShapeDtypeStruct((kk, D_FF), F32), jax.ShapeDtypeStruct((1, D_FF), F32),
                   jax.ShapeDtypeStruct((1, D_FF), F32)],
        scratch_shapes=[pltpu.VMEM((2, 2, tp, LANE), F32), pltpu.SemaphoreType.DMA((2, 2))],
        compiler_params=_cparams("arbitrary"),
    )(dm, up, up, w, w, b2, b2)


def _lru_gates_fwd(xc, wg, b4, lam, t_real, name):
    tp = xc.shape[0]
    tm = _tile(tp, 1408)

    def body(x_ref, w_ref, b_ref, lam_ref, r0_ref, r1_ref, i0_ref, i1_ref, a0_ref, a1_ref, u0_ref, u1_ref):
        x = x_ref[...]
        xb = x.astype(BF16)
        live = _row_ids(x.shape, pl.program_id(1) * tm) < t_real
        bb = b_ref[...]
        sp = _softplus_neg(lam_ref[...])
        gate = [_sigmoid(jnp.dot(xb, w_ref[k], preferred_element_type=F32) + bb[k:k + 1, :]) for k in range(4)]
        for d, (r_ref, i_ref, a_ref, u_ref) in enumerate(((r0_ref, i0_ref, a0_ref, u0_ref), (r1_ref, i1_ref, a1_ref, u1_ref))):
            r, ig = gate[d], gate[2 + d]
            a = jnp.exp(-LRU_C * r * sp[d:d + 1, :])
            r_ref[...] = r
            i_ref[...] = ig
            a_ref[...] = a
            u_ref[...] = jnp.where(live, jnp.sqrt(1.0 - a * a) * (ig * x), 0.0)

    blk = pl.BlockSpec((tm, LANE), lambda g, i: (i, g))
    return pl.pallas_call(
        body, name=name, grid=(LRU_BLOCKS, tp // tm),
        in_specs=[blk, pl.BlockSpec((None, 4, LANE, LANE), lambda g, i: (g, 0, 0, 0)),
                  pl.BlockSpec((4, LANE), lambda g, i: (0, g)), pl.BlockSpec((2, LANE), lambda g, i: (0, g))],
        out_specs=[blk] * 8,
        out_shape=[jax.ShapeDtypeStruct((tp, D_MODEL), F32)] * 8,
        compiler_params=_cparams("parallel", "parallel"),
    )(xc, wg, b4, lam)


def _lru_gates_bwd(l0, l1, da0, da1, r0, r1, i0, i1, a0, a1, xc, wg, lam, t_real, name):
    tp = xc.shape[0]
    tm = _tile(tp, 1408)

    rc = 32
    assert tm % rc == 0

    def fold(v):
        out = v[0:SUBLANE]
        for t in range(1, rc // SUBLANE):
            out = out + v[t * SUBLANE:(t + 1) * SUBLANE]
        return out

    def body(l0_ref, l1_ref, da0_ref, da1_ref, r0_ref, r1_ref, i0_ref, i1_ref, a0_ref, a1_ref, x_ref, w_ref, lam_ref,
             dx_ref, dw_ref, db_ref, dlam_ref, pre_s, dxp_s):
        i = pl.program_id(1)
        lam_ = lam_ref[...]
        sp = _softplus_neg(lam_)
        dsp_dlam = -_sigmoid(-lam_)

        def chunk(c, sums):
            r0 = pl.multiple_of(c * rc, rc)
            rows = pl.ds(r0, rc)
            x = x_ref[rows, :]
            live = _row_ids((rc, LANE), i * tm + r0) < t_real
            dxp = jnp.zeros_like(x)
            sums = list(sums)
            for d, (l_ref, da_ref, r_ref, i_ref, a_ref) in enumerate(((l0_ref, da0_ref, r0_ref, i0_ref, a0_ref),
                                                                      (l1_ref, da1_ref, r1_ref, i1_ref, a1_ref))):
                r, ig, a = r_ref[rows, :], i_ref[rows, :], a_ref[rows, :]
                du = jnp.where(live, l_ref[rows, :], 0.0)
                a2 = a * a
                rs = lax.rsqrt(1.0 - a2)
                dv = du * ((1.0 - a2) * rs)
                ds = du * (ig * x)
                dla = jnp.where(live, da_ref[rows, :] * a - ds * (a2 * rs), 0.0)
                dr = dla * (-LRU_C) * sp[d:d + 1, :]
                p_r = dr * r * (1.0 - r)
                p_i = dv * x * ig * (1.0 - ig)
                pre_s[d, rows, :] = p_r.astype(BF16)
                pre_s[2 + d, rows, :] = p_i.astype(BF16)
                dxp = dxp + dv * ig
                sums[d] = sums[d] + fold(p_r)
                sums[2 + d] = sums[2 + d] + fold(p_i)
                sums[4 + d] = sums[4 + d] + fold(dla * (-LRU_C) * r)
            dxp_s[rows, :] = dxp
            return tuple(sums)

        zero = jnp.zeros((SUBLANE, LANE), F32)
        sums = lax.fori_loop(0, tm // rc, chunk, (zero,) * 6)

        @pl.when(i == 0)
        def _():
            dw_ref[...] = jnp.zeros_like(dw_ref)
            db_ref[...] = jnp.zeros_like(db_ref)
            dlam_ref[...] = jnp.zeros_like(dlam_ref)

        xb = x_ref[...].astype(BF16)
        dx = dxp_s[...]
        for k in range(4):
            pk = pre_s[k]
            dx = dx + lax.dot_general(pk, w_ref[k], (((1,), (1,)), ((), ())), preferred_element_type=F32)
            dw_ref[k] += lax.dot_general(xb, pk, (((0,), (0,)), ((), ())), preferred_element_type=F32)
        db_ref[...] += jnp.concatenate([jnp.sum(sums[k], axis=0, keepdims=True) for k in range(4)], axis=0)
        dlam_ref[...] += jnp.concatenate([jnp.sum(sums[4 + d], axis=0, keepdims=True) * dsp_dlam[d:d + 1, :] for d in range(2)], axis=0)
        dx_ref[...] = dx

    blk = pl.BlockSpec((tm, LANE), lambda g, i: (i, g))
    return pl.pallas_call(
        body, name=name, grid=(LRU_BLOCKS, tp // tm),
        in_specs=[blk] * 11 + [pl.BlockSpec((None, 4, LANE, LANE), lambda g, i: (g, 0, 0, 0)),
                               pl.BlockSpec((2, LANE), lambda g, i: (0, g))],
        out_specs=[blk, pl.BlockSpec((None, 4, LANE, LANE), lambda g, i: (g, 0, 0, 0)),
                   pl.BlockSpec((4, LANE), lambda g, i: (0, g)), pl.BlockSpec((2, LANE), lambda g, i: (0, g))],
        out_shape=[jax.ShapeDtypeStruct((tp, D_MODEL), F32), jax.ShapeDtypeStruct((LRU_BLOCKS, 4, LANE, LANE), F32),
                   jax.ShapeDtypeStruct((4, D_MODEL), F32), jax.ShapeDtypeStruct((2, D_MODEL), F32)],
        scratch_shapes=[pltpu.VMEM((4, tm, LANE), BF16), pltpu.VMEM((tm, LANE), F32)],
        compiler_params=_cparams("parallel", "arbitrary"),
    )(l0, l1, da0, da1, r0, r1, i0, i1, a0, a1, xc, wg, lam)


SCAN_UNROLL = 4


def _loop_tiles(nt, step, carry):
    assert nt % SCAN_UNROLL == 0

    def trip(tt, c):
        for u in range(SCAN_UNROLL):
            c = step(tt * SCAN_UNROLL + u, c)
        return c

    return lax.fori_loop(0, nt // SCAN_UNROLL, trip, carry)


def _tile_scan(a, u, reverse):
    rows = lax.broadcasted_iota(jnp.int32, a.shape, 0)
    for s in (1, 2, 4):
        if reverse:
            keep = rows < SUBLANE - s
            a_sh, u_sh = pltpu.roll(a, SUBLANE - s, 0), pltpu.roll(u, SUBLANE - s, 0)
        else:
            keep = rows >= s
            a_sh, u_sh = pltpu.roll(a, s, 0), pltpu.roll(u, s, 0)
        u = u + a * jnp.where(keep, u_sh, 0.0)
        a = a * jnp.where(keep, a_sh, 1.0)
    return a, u


def _scan_fwd(a0, u0, a1, u1, name):
    tp, d = a0.shape
    tc = 128
    nt = tp // SUBLANE

    def body(a0_ref, u0_ref, a1_ref, u1_ref, h0_ref, h1_ref):
        def step(t, carry):
            c0, c1 = carry
            f = pl.multiple_of(t * SUBLANE, SUBLANE)
            b = pl.multiple_of((nt - 1 - t) * SUBLANE, SUBLANE)
            pa, pu = _tile_scan(a0_ref[pl.ds(f, SUBLANE), :], u0_ref[pl.ds(f, SUBLANE), :], False)
            h = pu + pa * c0
            h0_ref[pl.ds(f, SUBLANE), :] = h
            c0 = h[SUBLANE - 1:SUBLANE, :]
            pa, pu = _tile_scan(a1_ref[pl.ds(b, SUBLANE), :], u1_ref[pl.ds(b, SUBLANE), :], True)
            h = pu + pa * c1
            h1_ref[pl.ds(b, SUBLANE), :] = h
            c1 = h[0:1, :]
            return c0, c1

        z = jnp.zeros((1, tc), F32)
        _loop_tiles(nt, step, (z, z))

    blk = pl.BlockSpec((tp, tc), lambda j: (0, j))
    return pl.pallas_call(
        body, name=name, grid=(d // tc,), in_specs=[blk] * 4, out_specs=[blk] * 2,
        out_shape=[jax.ShapeDtypeStruct((tp, d), F32)] * 2,
        compiler_params=_cparams("parallel"),
    )(a0, u0, a1, u1)


def _scan_bwd(dh, a0, a1, h0, h1, name):
    tp, d = dh.shape
    tc = 128
    nt = tp // SUBLANE

    def body(dh_ref, a0_ref, a1_ref, h0_ref, h1_ref, l0_ref, l1_ref, da0_ref, da1_ref):
        rows8 = lax.broadcasted_iota(jnp.int32, (SUBLANE, tc), 0)

        def step(t, carry):
            c0, c1 = carry
            b = pl.multiple_of((nt - 1 - t) * SUBLANE, SUBLANE)
            f = pl.multiple_of(t * SUBLANE, SUBLANE)
            a = a0_ref[pl.ds(b, SUBLANE), :]
            a_next = jnp.where(rows8 < SUBLANE - 1, pltpu.roll(a, SUBLANE - 1, 0), 1.0)
            pa, pu = _tile_scan(a_next, dh_ref[pl.ds(b, SUBLANE), :], True)
            lam = pu + pa * c0
            l0_ref[pl.ds(b, SUBLANE), :] = lam
            c0 = a[0:1, :] * lam[0:1, :]
            a = a1_ref[pl.ds(f, SUBLANE), :]
            a_prev = jnp.where(rows8 >= 1, pltpu.roll(a, 1, 0), 1.0)
            pa, pu = _tile_scan(a_prev, dh_ref[pl.ds(f, SUBLANE), :], False)
            lam = pu + pa * c1
            l1_ref[pl.ds(f, SUBLANE), :] = lam
            c1 = a[SUBLANE - 1:SUBLANE, :] * lam[SUBLANE - 1:SUBLANE, :]
            return c0, c1

        z = jnp.zeros((1, tc), F32)
        _loop_tiles(nt, step, (z, z))
        rows = lax.broadcasted_iota(jnp.int32, (tp, tc), 0)
        da0_ref[...] = l0_ref[...] * jnp.where(rows >= 1, pltpu.roll(h0_ref[...], 1, 0), 0.0)
        da1_ref[...] = l1_ref[...] * jnp.where(rows < tp - 1, pltpu.roll(h1_ref[...], tp - 1, 0), 0.0)

    blk = pl.BlockSpec((tp, tc), lambda j: (0, j))
    return pl.pallas_call(
        body, name=name, grid=(d // tc,), in_specs=[blk] * 5, out_specs=[blk] * 4,
        out_shape=[jax.ShapeDtypeStruct((tp, d), F32)] * 4,
        compiler_params=_cparams("parallel"),
    )(dh, a0, a1, h0, h1)


def _gated_h(proj, h0, h1, name):
    def fn(row0, lg, x0, x1):
        return _gelu(lg) * (x0 + x1)

    return _rows(fn, name, [Rw(proj, D_MODEL, C_LRU_G // D_MODEL), Rw(h0), Rw(h1)], [(D_MODEL, F32)])[0]


def _gated_h_bwd(dgh, proj, h0, h1, dproj, name):
    def fn(row0, dg, lg, x0, x1):
        act, dact = _gelu_and_grad(lg)
        return dg * (x0 + x1) * dact, dg * act

    return _rows(fn, name, [Rw(dgh), Rw(proj, D_MODEL, C_LRU_G // D_MODEL), Rw(h0), Rw(h1)],
                 [Into(dproj, C_LRU_G, D_MODEL), (D_MODEL, F32)])


def _mix(proj, y_mla, y_lru, name):
    def fn(row0, gm, gl, ym, yl):
        return _sigmoid(gm) * ym + _sigmoid(gl) * yl

    return _rows(fn, name, [Rw(proj, D_MODEL, C_G_MLA // D_MODEL), Rw(proj, D_MODEL, C_G_LRU // D_MODEL), Rw(y_mla), Rw(y_lru)],
                 [(D_MODEL, F32)])[0]


def _mix_bwd(dz, proj, y_mla, y_lru, dproj, name):
    def fn(row0, dzb, gm, gl, ym, yl):
        sm, sl = _sigmoid(gm), _sigmoid(gl)
        dg = jnp.concatenate([dzb * ym * sm * (1.0 - sm), dzb * yl * sl * (1.0 - sl)], axis=1)
        return dzb * sm, dzb * sl, dg

    return _rows(fn, name, [Rw(dz), Rw(proj, D_MODEL, C_G_MLA // D_MODEL), Rw(proj, D_MODEL, C_G_LRU // D_MODEL),
                            Rw(y_mla), Rw(y_lru)], [(D_MODEL, F32), (D_MODEL, F32), Into(dproj, C_G_MLA, 2 * D_MODEL)])


def _layer_fwd(h, w_in, more_weights, cs, t_real, tag):
    proj = _matmul(h, w_in, tag + "proj", tb=True)
    w = dict(more_weights(0, proj), w_in=w_in)
    cqn, ckvn = _mla_norms(proj, w['q_norm'], w['kv_norm'], tag + "mla_norms")
    qext = _matmul(cqn, w['w_q'], tag + "q_up")
    kv = _matmul(ckvn, w['w_kv'], tag + "kv_up")
    qc, kc, vb = _mla_pack(qext, kv, proj, cs, tag + "mla_pack")
    o, lse = _attn_fwd(qc, kc, vb, t_real, tag + "attn_fwd")
    w.update(more_weights(1, o))
    y_mla = _matmul(o, w['w_o_mla'], tag + "o_mla")
    xc = _lru_conv_fwd(proj, w['lru_conv_w'], w['lru_conv_b'], t_real, tag + "lru_conv")
    r0, r1, i0, i1, a0, a1, u0, u1 = _lru_gates_fwd(xc, w['w_g'], w['b4'], w['lru_lambda'], t_real, tag + "lru_gates")
    h0, h1 = _scan_fwd(a0, u0, a1, u1, tag + "lru_scan")
    gh = _gated_h(proj, h0, h1, tag + "lru_gate_out")
    y_lru = _matmul(gh, w['w_o_lru'], tag + "o_lru")
    z = _mix(proj, y_mla, y_lru, tag + "mix")
    zo = _matmul(z, w['w_out'], tag + "w_out")
    hm = _ln_fwd([(DN_ALPHA, h), (1.0, zo)], w['ln1_g'], w['ln1_b'], tag + "ln1")
    w.update(more_weights(2, hm))
    up = _matmul(hm, w['w_up'], tag + "w_up", tb=True)
    m = _ffn_conv_act(up, w['ffn_conv_w'], w['ffn_conv_b'], t_real, tag + "ffn_conv")
    f = _matmul(m, w['w_down'], tag + "w_down", tk_cap=1408)
    out = _ln_fwd([(DN_ALPHA, hm), (1.0, f)], w['ln2_g'], w['ln2_b'], tag + "ln2")
    saved = dict(w=w, h=h, proj=proj, cqn=cqn, ckvn=ckvn, qc=qc, kc=kc, vb=vb, o=o, lse=lse, y_mla=y_mla, xc=xc,
                 r0=r0, r1=r1, i0=i0, i1=i1, a0=a0, a1=a1, h0=h0, h1=h1, gh=gh, y_lru=y_lru, z=z, zo=zo, hm=hm,
                 up=up, m=m, f=f)
    return out, saved


DW_MATMUL = dict(ta=True, out_dtype=BF16, tn_cap=1408, tk_cap=1408)


def _after(a, tok):
    return a if tok is None else a + tok.astype(a.dtype)


def _layer_bwd(dout_terms, s, cs, t_real, tag, emit, tok):
    w = s['w']
    g = {}
    du2, dg2, db2 = _ln_bwd(dout_terms, [(DN_ALPHA, s['hm']), (1.0, s['f'])], _after(w['ln2_g'], tok), tag + "ln2_bwd")
    g['ln2_g'], g['ln2_b'] = dg2, db2
    dm = _matmul(du2, w['w_down'], tag + "w_down_dx", tb=True)
    g['w_down'] = _matmul(s['m'], du2, tag + "w_down_dw", **DW_MATMUL)
    dup, dwg_, dwv_, dbg_, dbv_ = _ffn_conv_act_bwd(dm, s['up'], w['ffn_conv_w'], w['ffn_conv_b'], t_real, tag + "ffn_conv_bwd")
    g['ffn_conv_w'] = jnp.concatenate([dwg_, dwv_], axis=1)
    g['ffn_conv_b'] = jnp.concatenate([dbg_, dbv_], axis=1)
    dhm_mm = _matmul(dup, w['w_up'], tag + "w_up_dx", tk_cap=1408)
    g['w_up'] = _matmul(s['hm'], dup, tag + "w_up_dw", **DW_MATMUL)
    tok = emit('ffn', g)
    g = {}
    du1, dg1, db1 = _ln_bwd([(DN_ALPHA, du2), (1.0, dhm_mm)], [(DN_ALPHA, s['h']), (1.0, s['zo'])],
                            _after(w['ln1_g'], tok), tag + "ln1_bwd")
    g['ln1_g'], g['ln1_b'] = dg1, db1
    dz = _matmul(du1, w['w_out'], tag + "w_out_dx", tb=True)
    g['w_out'] = _matmul(s['z'], du1, tag + "w_out_dw", **DW_MATMUL)
    dproj = lax.empty(s['proj'].shape, F32)
    dy_mla, dy_lru, dproj = _mix_bwd(dz, s['proj'], s['y_mla'], s['y_lru'], dproj, tag + "mix_bwd")
    do = _matmul(dy_mla, w['w_o_mla'], tag + "o_mla_dx", tb=True)
    g['w_o_mla'] = _matmul(s['o'], dy_mla, tag + "o_mla_dw", **DW_MATMUL)
    dqext, dkc, dv = _attn_bwd(s['qc'], s['kc'], s['vb'], do, s['o'], s['lse'], cs, t_real, tag + "attn_bwd")
    dkv, dkrp = _mla_unpack(dkc, dv, cs, tag + "mla_unpack")
    dcqn = _matmul(dqext, w['w_q'], tag + "q_up_dx", tb=True)
    g['w_q'] = _matmul(s['cqn'], dqext, tag + "q_up_dw", **DW_MATMUL)
    dckvn = _matmul(dkv, w['w_kv'], tag + "kv_up_dx", tb=True)
    g['w_kv'] = _matmul(s['ckvn'], dkv, tag + "kv_up_dw", **DW_MATMUL)
    dgh = _matmul(dy_lru, w['w_o_lru'], tag + "o_lru_dx", tb=True)
    g['w_o_lru'] = _matmul(s['gh'], dy_lru, tag + "o_lru_dw", **DW_MATMUL)
    dproj, dhs = _gated_h_bwd(dgh, s['proj'], s['h0'], s['h1'], dproj, tag + "lru_gate_out_bwd")
    l0, l1, da0, da1 = _scan_bwd(dhs, s['a0'], s['a1'], s['h0'], s['h1'], tag + "lru_scan_bwd")
    dxc, g['w_g'], g['b4'], g['lru_lambda'] = _lru_gates_bwd(
        l0, l1, da0, da1, s['r0'], s['r1'], s['i0'], s['i1'], s['a0'], s['a1'], s['xc'], w['w_g'], w['lru_lambda'],
        t_real, tag + "lru_gates_bwd")
    dproj, g['lru_conv_w'], g['lru_conv_b'] = _lru_conv_bwd(dxc, s['proj'], w['lru_conv_w'], dproj, t_real, tag + "lru_conv_bwd")
    tok = emit('mid', g)
    dproj, dqn, dkvn = _mla_norms_bwd(dcqn, dckvn, _after(dkrp, tok), s['proj'], w['q_norm'], w['kv_norm'], dproj,
                                      tag + "mla_norms_bwd")
    tok = emit('in', {'w_in': _matmul(s['h'], dproj, tag + "proj_dw", **DW_MATMUL), 'q_norm': dqn, 'kv_norm': dkvn})
    dh_mm = _matmul(dproj, w['w_in'], tag + "proj_dx", tk_cap=1536)
    return [(DN_ALPHA, du1), (1.0, dh_mm)], tok


def _swap_halves(a, axis=-1):
    h1, h2 = jnp.split(a, 2, axis=axis)
    return jnp.concatenate([h2, h1], axis=axis)


def _w_in_kernel(w_in_t):
    cq, ckv, kr, lg, lx, gm, gl = jnp.split(w_in_t, [256, 384, 448, 1472, 2496, 3520], axis=0)
    return jnp.concatenate([lg, lx, gm, gl, cq, ckv, kr, _swap_halves(kr, axis=0)], axis=0)


def _layer_weights(fl):
    w = {}
    if 'w_uq' in fl:
        uq = fl['w_uq']
        w['w_q'] = jnp.concatenate([uq, _swap_halves(uq[..., QK_NOPE:])], axis=-1).reshape(Q_RANK, HEADS * 2 * LANE)
        w['w_kv'] = jnp.concatenate([fl['w_uk'].reshape(KV_RANK, -1), fl['w_uv'].reshape(KV_RANK, -1)], axis=1).astype(BF16)
        w['w_g'] = jnp.moveaxis(jnp.concatenate([fl['w_rg'], fl['w_ig']], axis=0), 0, 1).astype(BF16)
        w['b4'] = jnp.concatenate([fl['b_rg'], fl['b_ig']], axis=0)
    for n in ('q_norm', 'kv_norm', 'w_o_mla', 'lru_conv_w', 'lru_conv_b', 'lru_lambda', 'w_o_lru', 'w_out', 'ln1_g',
              'ln1_b', 'w_up', 'ffn_conv_w', 'ffn_conv_b', 'w_down', 'ln2_g', 'ln2_b'):
        if n in fl:
            w[n] = fl[n]
    return w


def _layer_grads(g):
    out = {}
    if 'w_in' in g:
        lg, lx, gm, gl, cq, ckv, kr, krs = jnp.split(g['w_in'], [1024, 2048, 3072, 4096, 4352, 4480, 4544], axis=1)
        out['w_in'] = jnp.concatenate([cq, ckv, kr + _swap_halves(krs), lg, lx, gm, gl], axis=1)
    if 'w_q' in g:
        gq = g['w_q'].reshape(Q_RANK, HEADS, 2 * LANE)
        out['w_uq'] = jnp.concatenate([gq[..., :QK_NOPE], gq[..., QK_NOPE:QK_NOPE + QK_ROPE] + _swap_halves(gq[..., QK_NOPE + QK_ROPE:])], axis=-1)
    if 'w_kv' in g:
        out['w_uk'] = g['w_kv'][:, :HEADS * QK_NOPE].reshape(KV_RANK, HEADS, QK_NOPE)
        out['w_uv'] = g['w_kv'][:, HEADS * QK_NOPE:].reshape(KV_RANK, HEADS, V_HEAD)
    if 'w_g' in g:
        gg = jnp.moveaxis(g['w_g'], 1, 0)
        out['w_rg'], out['w_ig'] = gg[:2], gg[2:]
    if 'b4' in g:
        out['b_rg'], out['b_ig'] = g['b4'][:2], g['b4'][2:]
    for n in ('q_norm', 'kv_norm', 'lru_conv_b', 'ln1_g', 'ln1_b', 'ffn_conv_b', 'ln2_g', 'ln2_b'):
        if n in g:
            out[n] = g[n].reshape(-1)
    for n in ('w_o_mla', 'lru_conv_w', 'lru_lambda', 'w_o_lru', 'w_out', 'w_up', 'ffn_conv_w', 'w_down'):
        if n in g:
            out[n] = g[n]
    return out


def _rope_table(tp):
    half = QK_ROPE // 2
    inv_freq = jnp.exp(-math.log(ROPE_THETA) * jnp.arange(half, dtype=F32) / half)
    ang = jnp.arange(tp, dtype=F32)[:, None] * inv_freq[None, :]
    c, s = jnp.cos(ang), jnp.sin(ang)
    return jnp.concatenate([c, c, -s, s], axis=1)


def _local_step(x, target, meta, ln0_g, ln0_b, layer_w, t_pad, emit):
    seq = x.shape[0]
    t_real = N_META + seq
    zpad = jnp.zeros((t_pad - t_real, D_MODEL), F32)
    xin = jnp.concatenate([meta, x, zpad], axis=0)
    tgt = jnp.concatenate([jnp.zeros((N_META, D_MODEL), F32), target, zpad], axis=0)
    cs = _rope_table(t_pad)
    h = _ln_fwd([(1.0, xin)], ln0_g, ln0_b, "ln0")
    saved = []
    for l in range(DEPTH):
        w_in, rest_of_weights = layer_w[l](h)
        h, s = _layer_fwd(h, w_in, rest_of_weights, cs, t_real, "l%d_" % l)
        saved.append(s)
    dy, lossvec = _loss_head(h, tgt, t_real, "loss_head")
    terms, tok = [(1.0, dy)], None
    for l in reversed(range(DEPTH)):
        terms, tok = _layer_bwd(terms, saved[l], cs, t_real, "l%d_" % l,
                                functools.partial(lambda stage, g, l: emit(l, stage, _layer_grads(g)), l=l), tok)
    dxin, dg0, db0 = _ln_bwd(terms, [(1.0, xin)], _after(ln0_g, tok), "ln0_bwd")
    emit(None, 'head', {'meta_tokens': dxin[:N_META], 'ln0_g': dg0.reshape(-1), 'ln0_b': db0.reshape(-1), 'loss': lossvec})
    return dxin[N_META:t_real]


_HBM = pl.BlockSpec(memory_space=pltpu.HBM)
_SEM = pl.BlockSpec(memory_space=pltpu.SEMAPHORE)
_SIDE_EFFECT = pltpu.SideEffectType.DATAFLOW_SIDE_EFFECTING


def _peer_copies(src_refs, land_refs, scatters, send_sems, recv_sems):
    x, y, c = lax.axis_index("x"), lax.axis_index("y"), lax.axis_index("c")
    me = 4 * x + 2 * y + c
    copies = []
    for k in range(1, N_DEV):
        px = 1 - x if k & 4 else x
        py = 1 - y if k & 2 else y
        pc = 1 - c if k & 1 else c
        for t, (src, land) in enumerate(zip(src_refs, land_refs)):
            copies.append(pltpu.make_async_remote_copy(
                src_ref=src.at[4 * px + 2 * py + pc] if scatters[t] else src, dst_ref=land.at[me],
                send_sem=send_sems.at[7 * t + k - 1], recv_sem=recv_sems.at[7 * t + k - 1],
                device_id=(px, py, pc), device_id_type=pl.DeviceIdType.MESH))
    return me, copies


def _own_block_in_place(land, own):
    me = 4 * lax.axis_index("x") + 2 * lax.axis_index("y") + lax.axis_index("c")
    return lax.dynamic_update_slice_in_dim(land, own, me, 0)


def _gather_two_level(shards, name):
    nt = len(shards)

    def body(*refs):
        x_refs, out_refs = refs[:nt], refs[nt:2 * nt]
        token_ref, send_sems, recv_sems = refs[2 * nt:]
        x, y, c = lax.axis_index("x"), lax.axis_index("y"), lax.axis_index("c")
        me, sibling = (x, y, c), (x, y, 1 - c)
        chips = [(1 - x, y), (x, 1 - y), (1 - x, 1 - y)]

        def copy(t, k, block, to, own=False):
            px, py, pc = block
            slot = out_refs[t].at[4 * px + 2 * py + pc]
            return pltpu.make_async_remote_copy(
                src_ref=x_refs[t] if own else slot, dst_ref=slot,
                send_sem=send_sems.at[7 * t + k], recv_sem=recv_sems.at[7 * t + k],
                device_id=to, device_id_type=pl.DeviceIdType.MESH)

        sent = []
        for t in range(nt):
            first = [copy(t, 1 + j, me, (*chip, c), own=True) for j, chip in enumerate(chips)]
            first.append(copy(t, 0, me, sibling, own=True))
            for cp in first:
                cp.start()
            sent += first
        token_ref[...] = jnp.zeros_like(token_ref)
        for j, chip in enumerate(chips):
            for t in range(nt):
                copy(t, 1 + j, (*chip, c), me).wait_recv()
                passed = copy(t, 4 + j, (*chip, c), sibling)
                passed.start()
                sent.append(passed)
        for t in range(nt):
            copy(t, 0, sibling, me).wait_recv()
            for j, chip in enumerate(chips):
                copy(t, 4 + j, (*chip, 1 - c), me).wait_recv()
        for cp in sent:
            cp.wait_send()

    any_space = pl.BlockSpec(memory_space=pl.ANY)
    res = pl.pallas_call(
        body, name=name,
        out_shape=[jax.ShapeDtypeStruct((N_DEV,) + a.shape, a.dtype) for a in shards] + [jax.ShapeDtypeStruct((SUBLANE, LANE), F32)],
        in_specs=[any_space] * nt, out_specs=[any_space] * nt + [pl.BlockSpec(memory_space=pltpu.VMEM)],
        scratch_shapes=[pltpu.SemaphoreType.DMA((7 * nt,)), pltpu.SemaphoreType.DMA((7 * nt,))],
    )(*shards)
    return [_own_block_in_place(land, a[None]) for land, a in zip(res[:nt], shards)], res[nt][0, 0]


def _exchange_start(groups, name):
    flat = [it for grp in groups for it in grp]
    nt, ng = len(flat), len(groups)
    scatters = [sc for _, sc in flat]
    srcs = [pltpu.with_memory_space_constraint(a, pltpu.HBM) for a, _ in flat]
    land_shapes = [a.shape if sc else (N_DEV,) + a.shape for a, sc in flat]
    lands = [pltpu.with_memory_space_constraint(lax.empty(s, a.dtype), pltpu.HBM) for s, (a, _) in zip(land_shapes, flat)]
    bounds = [0]
    for grp in groups:
        bounds.append(bounds[-1] + len(grp))

    def body(*refs):
        src_refs, land_refs = refs[:nt], refs[nt:2 * nt]
        sem_refs = refs[2 * nt:2 * nt + 2 * ng]
        token_ref = refs[4 * nt + 2 * ng]
        for gi in range(ng):
            lo, hi = bounds[gi], bounds[gi + 1]
            _, copies = _peer_copies(src_refs[lo:hi], land_refs[lo:hi], scatters[lo:hi], sem_refs[2 * gi], sem_refs[2 * gi + 1])
            for cp in copies:
                cp.start()
        token_ref[...] = jnp.zeros_like(token_ref)

    out_shape = []
    for grp in groups:
        out_shape += [pltpu.SemaphoreType.DMA((7 * len(grp),)), pltpu.SemaphoreType.DMA((7 * len(grp),))]
    out_shape += [pltpu.HBM(a.shape, a.dtype) for a in srcs] + [pltpu.HBM(s, a.dtype) for s, a in zip(land_shapes, srcs)]
    out_shape += [jax.ShapeDtypeStruct((SUBLANE, LANE), F32)]
    res = pl.pallas_call(
        body, name=name, out_shape=out_shape,
        in_specs=[_HBM] * (2 * nt),
        out_specs=[_SEM] * (2 * ng) + [_HBM] * (2 * nt) + [pl.BlockSpec(memory_space=pltpu.VMEM)],
        input_output_aliases={t: 2 * ng + t for t in range(2 * nt)},
        compiler_params=pltpu.CompilerParams(has_side_effects=_SIDE_EFFECT),
    )(*srcs, *lands)
    sems, thru, token = res[:2 * ng], res[2 * ng:2 * ng + 2 * nt], res[-1]
    states = []
    for gi in range(ng):
        lo, hi = bounds[gi], bounds[gi + 1]
        states.append((sems[2 * gi], sems[2 * gi + 1], thru[lo:hi], thru[nt + lo:nt + hi], scatters[lo:hi]))
    return states, token[0, 0]


def _exchange_wait(state, after, name):
    send_sems, recv_sems, srcs, lands, scatters = state
    n = len(srcs)

    def body(*refs):
        _, copies = _peer_copies(refs[:n], refs[n:2 * n], scatters, refs[2 * n], refs[2 * n + 1])
        for cp in copies:
            cp.wait_send()
        for cp in copies:
            cp.wait_recv()

    res = pl.pallas_call(
        body, name=name,
        out_shape=[pltpu.HBM(a.shape, a.dtype) for a in srcs] + [pltpu.HBM(a.shape, a.dtype) for a in lands],
        in_specs=[_HBM] * (2 * n) + [_SEM, _SEM, _HBM],
        out_specs=[_HBM] * (2 * n),
        input_output_aliases={t: t for t in range(2 * n)},
        compiler_params=pltpu.CompilerParams(has_side_effects=_SIDE_EFFECT),
    )(*srcs, *lands, send_sems, recv_sems, pltpu.with_memory_space_constraint(after, pltpu.HBM))
    me = 4 * lax.axis_index("x") + 2 * lax.axis_index("y") + lax.axis_index("c")
    out = []
    for src, land, sc in zip(res[:n], res[n:], scatters):
        own = lax.dynamic_index_in_dim(src, me, 0, keepdims=True) if sc else src[None]
        out.append(lax.dynamic_update_slice_in_dim(land, own, me, 0))
    return out


def _as_rows(shape):
    return (1, shape[0]) if len(shape) == 1 else (math.prod(shape[:-1]), shape[-1])


def _sum_adamw(pieces, w, m, v, name):
    shape = w.shape
    nl = len(pieces)
    if nl > 1 and _as_rows(shape[1:])[0] % 16:
        pieces, nl = [jnp.stack(pieces, axis=1)], 1
    rows, cols = _as_rows(shape)
    rl = rows // nl
    cap = max(16, (1 << 18) // cols // 16 * 16)
    tr = _tile(rl, cap, 16)
    nb = rl // tr
    c1 = 1.0 / (1.0 - ADAM_B1 ** ADAM_STEP)
    c2 = 1.0 / (1.0 - ADAM_B2 ** ADAM_STEP)

    def body(*refs):
        p_refs = refs[:nl]
        w_ref, m_ref, v_ref, g_ref, d_ref, nm_ref, nv_ref = refs[nl:]
        li = pl.program_id(0)

        def total(p_ref):
            acc = p_ref[0].astype(F32)
            for k in range(1, N_DEV):
                acc = acc + p_ref[k].astype(F32)
            return acc

        gg = total(p_refs[0])
        for l in range(1, nl):
            gg = jnp.where(li == l, total(p_refs[l]), gg)
        nm = ADAM_B1 * m_ref[...] + (1.0 - ADAM_B1) * gg
        nv = ADAM_B2 * v_ref[...] + (1.0 - ADAM_B2) * (gg * gg)
        g_ref[...] = gg
        d_ref[...] = -ADAM_LR * ((nm * c1) / (jnp.sqrt(nv * c2) + ADAM_EPS) + ADAM_WD * w_ref[...])
        nm_ref[...] = nm
        nv_ref[...] = nv

    blk = pl.BlockSpec((tr, cols), lambda li, i: (li * nb + i, 0))
    p_specs = [pl.BlockSpec((N_DEV, tr, cols), functools.partial(lambda li, i, l: (0, jnp.where(li == l, i, 0), 0), l=l))
               for l in range(nl)]
    res = pl.pallas_call(
        body, name=name, grid=(nl, nb),
        in_specs=p_specs + [blk] * 3, out_specs=[blk] * 4,
        out_shape=[jax.ShapeDtypeStruct((rows, cols), F32)] * 4,
        compiler_params=_cparams("parallel", "parallel"),
    )(*[p.reshape(N_DEV, rl, cols) for p in pieces], *[a.reshape(rows, cols) for a in (w, m, v)])
    return [r.reshape(shape) for r in res]


def _to_shards(full, axis):
    shp = full.shape
    a = full.reshape(shp[:axis] + (N_DEV, shp[axis] // N_DEV) + shp[axis + 1:])
    return jnp.moveaxis(a, axis, 0)


def _from_shards(blocks, axis):
    a = jnp.moveaxis(blocks, 0, axis)
    shp = a.shape
    return a.reshape(shp[:axis] + (shp[axis] * shp[axis + 1],) + shp[axis + 2:])


def kernel(x, meta_tokens, ln0_g, ln0_b, w_in, q_norm, kv_norm, w_uq, w_uk, w_uv, w_o_mla, lru_conv_w, lru_conv_b, w_rg, b_rg, w_ig, b_ig, lru_lambda, w_o_lru, w_out, ln1_g, ln1_b, w_up, ffn_conv_w, ffn_conv_b, w_down, ln2_g, ln2_b, loss_target, m_meta_tokens, m_ln0_g, m_ln0_b, m_w_in, m_q_norm, m_kv_norm, m_w_uq, m_w_uk, m_w_uv, m_w_o_mla, m_lru_conv_w, m_lru_conv_b, m_w_rg, m_b_rg, m_w_ig, m_b_ig, m_lru_lambda, m_w_o_lru, m_w_out, m_ln1_g, m_ln1_b, m_w_up, m_ffn_conv_w, m_ffn_conv_b, m_w_down, m_ln2_g, m_ln2_b, v_meta_tokens, v_ln0_g, v_ln0_b, v_w_in, v_q_norm, v_kv_norm, v_w_uq, v_w_uk, v_w_uv, v_w_o_mla, v_lru_conv_w, v_lru_conv_b, v_w_rg, v_b_rg, v_w_ig, v_b_ig, v_lru_lambda, v_w_o_lru, v_w_out, v_ln1_g, v_ln1_b, v_w_up, v_ffn_conv_w, v_ffn_conv_b, v_w_down, v_ln2_g, v_ln2_b):
    args = (meta_tokens, ln0_g, ln0_b, w_in, q_norm, kv_norm, w_uq, w_uk, w_uv, w_o_mla, lru_conv_w, lru_conv_b, w_rg, b_rg, w_ig, b_ig, lru_lambda, w_o_lru, w_out, ln1_g, ln1_b, w_up, ffn_conv_w, ffn_conv_b, w_down, ln2_g, ln2_b)
    ms = (m_meta_tokens, m_ln0_g, m_ln0_b, m_w_in, m_q_norm, m_kv_norm, m_w_uq, m_w_uk, m_w_uv, m_w_o_mla, m_lru_conv_w, m_lru_conv_b, m_w_rg, m_b_rg, m_w_ig, m_b_ig, m_lru_lambda, m_w_o_lru, m_w_out, m_ln1_g, m_ln1_b, m_w_up, m_ffn_conv_w, m_ffn_conv_b, m_w_down, m_ln2_g, m_ln2_b)
    vs = (v_meta_tokens, v_ln0_g, v_ln0_b, v_w_in, v_q_norm, v_kv_norm, v_w_uq, v_w_uk, v_w_uv, v_w_o_mla, v_lru_conv_w, v_lru_conv_b, v_w_rg, v_b_rg, v_w_ig, v_b_ig, v_lru_lambda, v_w_o_lru, v_w_out, v_ln1_g, v_ln1_b, v_w_up, v_ffn_conv_w, v_ffn_conv_b, v_w_down, v_ln2_g, v_ln2_b)
    wd, md, vd = dict(zip(WEIGHTS, args)), dict(zip(WEIGHTS, ms)), dict(zip(WEIGHTS, vs))

    def shard_axis(n, l):
        return SHARD_AXIS[n] - (0 if l is None else 1)

    def shard(n, l):
        a = wd[n] if l is None else wd[n][l]
        if n in SENT_TRANSPOSED:
            a = a.T
        return a.astype(BF16) if n in BIG else a

    def whole(keys, landed):
        return {k: b.reshape(-1, b.shape[-1]) if k[0] in SENT_TRANSPOSED else _from_shards(b, shard_axis(*k))
                for k, b in zip(keys, landed)}

    first = [('meta_tokens', None), ('w_in', 0)]
    landed, token = _gather_two_level([shard(*k) for k in first], "gather_first")
    got_first = whole(first, landed)
    staged = [[(n, 0) for n in names] for names in STAGE_WEIGHTS]
    later = [(n, 1) for n in SHARDED if n != 'meta_tokens']
    gather, token = _exchange_start([[(_after(shard(*k), token), False) for k in keys] for keys in staged + [later]], "gather_start")

    def arrive(gi, keys, after, name):
        return whole(keys, _exchange_wait(gather[gi], after, name))

    def layer_weights(got, l, names):
        fl = {n: wd[n][l] for n in names if n in REPLICATED}
        fl.update({n: a for (n, _), a in got.items() if n in names})
        return _layer_weights(fl)

    ln0_g = _after(wd['ln0_g'], token)

    def first_layer(h):
        def more(stage, after):
            got = arrive(stage, staged[stage], after, "gather_wait_l0_%d" % stage)
            return layer_weights(got, 0, STAGE_WEIGHTS[stage] + STAGE_REPLICATED[stage])
        return _w_in_kernel(got_first['w_in', 0]), more

    def second_layer(h):
        got = arrive(len(staged), later, h, "gather_wait_l1")
        return _w_in_kernel(got['w_in', 1]), lambda stage, after: layer_weights(got, 1, STAGE_WEIGHTS[stage] + STAGE_REPLICATED[stage])

    sent = []
    pending = []

    def send(l, stage, grads):
        for n, g in grads.items():
            if n in SHARD_AXIS:
                g = _to_shards(g, shard_axis(n, l))
                pending.append(((n, l), (g.astype(BF16) if n in BIG else g, True)))
            else:
                pending.append(((n, l), (g.astype(BF16) if n in LARGE_REPLICATED else g, False)))
        if l == DEPTH - 1 and stage != 'in':
            return None
        (state,), tok = _exchange_start([[it for _, it in pending]], "grads_start_%s_%s" % (l, stage))
        sent.append(([k for k, _ in pending], state))
        pending.clear()
        return tok

    seq = x.shape[1]
    t_pad = -(-(N_META + seq + MIN_PAD_ROWS) // LANE) * LANE
    grad_x = _local_step(x[0], loss_target[0], got_first['meta_tokens', None], ln0_g, wd['ln0_b'],
                         [first_layer, second_layer], t_pad, send)

    pieces = {}
    for gi, (keys, state) in enumerate(sent):
        pieces.update(zip(keys, _exchange_wait(state, grad_x, "grads_wait_%d" % gi)))
    loss = jnp.sum(pieces['loss', None])
    outs = {}
    for n in WEIGHTS:
        ps = [pieces[n, None]] if (n, None) in pieces else [pieces[n, l] for l in range(DEPTH)]
        outs[n] = _sum_adamw(ps, wd[n], md[n], vd[n], "adamw_" + n)
    res = [loss, grad_x[None]]
    for k in range(4):
        res += [outs[n][k] for n in WEIGHTS]
    return tuple(res)
```

```python
import functools
import math

import jax
import jax.numpy as jnp
from jax import lax
from jax.experimental import pallas as pl
from jax.experimental.pallas import tpu as pltpu

F32 = jnp.float32
BF16 = jnp.bfloat16

N_DEV = 8
D_MODEL = 1024
N_META = 16
HEADS = 8
QK_NOPE = 128
QK_ROPE = 64
V_HEAD = 128
Q_RANK = 256
KV_RANK = 128
ROPE_THETA = 10000.0
LRU_BLOCKS = 8
LRU_C = 8.0
D_FF = 2816
DEPTH = 2
DN_ALPHA = (2.0 * DEPTH) ** 0.25
LN_EPS = 1e-5
RMS_EPS = 1e-6
LN2 = math.log(2.0)
ATT_SCALE = 1.0 / math.sqrt(QK_NOPE + QK_ROPE) / LN2
NEG_BIG = -1e30

ADAM_LR = 0.001
ADAM_B1 = 0.9
ADAM_B2 = 0.999
ADAM_EPS = 1e-08
ADAM_WD = 0.01
ADAM_STEP = 10

MIN_PAD_ROWS = 2
LANE = 128
SUBLANE = 8
VMEM_LIMIT = 56 * 1024 * 1024

PROJ_COLS = 4 * D_MODEL + Q_RANK + KV_RANK + 2 * QK_ROPE
C_LRU_G, C_LRU_X, C_G_MLA, C_G_LRU = 0, D_MODEL, 2 * D_MODEL, 3 * D_MODEL
C_CQ = 4 * D_MODEL
C_CKV = C_CQ + Q_RANK
C_KRP = C_CKV + KV_RANK

WEIGHTS = ['meta_tokens', 'ln0_g', 'ln0_b', 'w_in', 'q_norm', 'kv_norm', 'w_uq', 'w_uk', 'w_uv', 'w_o_mla',
           'lru_conv_w', 'lru_conv_b', 'w_rg', 'b_rg', 'w_ig', 'b_ig', 'lru_lambda', 'w_o_lru', 'w_out',
           'ln1_g', 'ln1_b', 'w_up', 'ffn_conv_w', 'ffn_conv_b', 'w_down', 'ln2_g', 'ln2_b']
SHARD_AXIS = {'meta_tokens': 1, 'w_in': 2, 'w_uq': 1, 'w_o_mla': 1, 'lru_conv_w': 2, 'b_rg': 2, 'b_ig': 2,
              'lru_lambda': 2, 'w_o_lru': 1, 'w_out': 1, 'w_up': 2, 'ffn_conv_w': 2, 'w_down': 1}
BIG = ['w_in', 'w_uq', 'w_o_mla', 'w_o_lru', 'w_out', 'w_up', 'w_down']
SHARDED = [n for n in WEIGHTS if n in SHARD_AXIS]
REPLICATED = [n for n in WEIGHTS if n not in SHARD_AXIS]
LARGE_REPLICATED = ['w_uk', 'w_uv', 'w_rg', 'w_ig']
SENT_TRANSPOSED = ['w_in', 'w_up']
STAGE_WEIGHTS = [['w_uq', 'lru_conv_w', 'b_rg', 'b_ig', 'lru_lambda'], ['w_o_mla', 'w_o_lru', 'w_out'], ['w_up', 'ffn_conv_w', 'w_down']]
STAGE_REPLICATED = [['q_norm', 'kv_norm', 'w_uk', 'w_uv', 'lru_conv_b', 'w_rg', 'w_ig'], ['ln1_g', 'ln1_b'], ['ffn_conv_b', 'ln2_g', 'ln2_b']]


def _cparams(*sem):
    return pltpu.CompilerParams(dimension_semantics=sem, vmem_limit_bytes=VMEM_LIMIT)


def _tile(n, cap, unit=LANE):
    best = None
    t = unit
    while t <= min(n, cap):
        if n % t == 0:
            best = t
        t += unit
    return n if best is None else best


def _sigmoid(x):
    return 1.0 / (1.0 + jnp.exp(-x))


_GELU_C = math.sqrt(2.0 / math.pi)


_GELU_A = 0.044715


def _gelu(x):
    t = jnp.tanh(x * (_GELU_C + (_GELU_C * _GELU_A) * (x * x)))
    hx = 0.5 * x
    return hx + hx * t


def _gelu_and_grad(x):
    x2 = x * x
    t = jnp.tanh(x * (_GELU_C + (_GELU_C * _GELU_A) * x2))
    hx = 0.5 * x
    dg = 0.5 + 0.5 * t + (hx * (1.0 - t * t)) * (_GELU_C + (3.0 * _GELU_C * _GELU_A) * x2)
    return hx + hx * t, dg


def _softplus_neg(lam):
    z = jnp.exp(-jnp.abs(lam))
    w = 1.0 + z
    log1p = jnp.where(w == 1.0, z, jnp.log(w) * z / (w - 1.0))
    return jnp.maximum(-lam, 0.0) + log1p


def _row_ids(shape, row0=0):
    return lax.broadcasted_iota(jnp.int32, shape, 0) + row0


def _matmul(a, b, name, ta=False, tb=False, out_dtype=F32, tm_cap=1408, tn_cap=1024, tk_cap=2048):
    if ta:
        kdim, m = a.shape
    else:
        m, kdim = a.shape
    if tb:
        n, k2 = b.shape
    else:
        k2, n = b.shape
    assert kdim == k2, (a.shape, b.shape, ta, tb)
    tm, tn, tk = _tile(m, tm_cap), _tile(n, tn_cap), _tile(kdim, tk_cap)
    nk = kdim // tk

    def body(a_ref, b_ref, o_ref, *acc):
        dn = (((0 if ta else 1,), (1 if tb else 0,)), ((), ()))
        part = lax.dot_general(a_ref[...].astype(BF16), b_ref[...].astype(BF16), dn, preferred_element_type=F32)
        if nk == 1:
            o_ref[...] = part.astype(o_ref.dtype)
            return
        acc_ref, k = acc[0], pl.program_id(2)

        @pl.when(k == 0)
        def _():
            acc_ref[...] = part

        @pl.when(k > 0)
        def _():
            acc_ref[...] += part

        @pl.when(k == nk - 1)
        def _():
            o_ref[...] = acc_ref[...].astype(o_ref.dtype)

    a_spec = pl.BlockSpec((tk, tm), lambda i, j, k: (k, i)) if ta else pl.BlockSpec((tm, tk), lambda i, j, k: (i, k))
    b_spec = pl.BlockSpec((tn, tk), lambda i, j, k: (j, k)) if tb else pl.BlockSpec((tk, tn), lambda i, j, k: (k, j))
    return pl.pallas_call(
        body, name=name,
        grid=(m // tm, n // tn, nk),
        in_specs=[a_spec, b_spec],
        out_specs=pl.BlockSpec((tm, tn), lambda i, j, k: (i, j)),
        out_shape=jax.ShapeDtypeStruct((m, n), out_dtype),
        scratch_shapes=[pltpu.VMEM((tm, tn), F32)] if nk > 1 else [],
        compiler_params=_cparams("parallel", "parallel", "arbitrary"),
    )(a, b)


class Rw:
    def __init__(self, arr, width=None, cb=0):
        self.arr, self.width, self.cb = arr, (arr.shape[1] if width is None else width), cb


class Pm:
    def __init__(self, arr):
        self.arr = arr


class Into:
    def __init__(self, arr, col0, width):
        self.arr, self.col0, self.width = arr, col0, width


def _call_with_into(body, name, grid, in_specs, operands, outs, spec_of, shape_of, extra_out_specs, extra_out_shape, sem):
    intos = [(k, o) for k, o in enumerate(outs) if isinstance(o, Into)]
    aliases = {len(operands) + n: k for n, (k, _) in enumerate(intos)}
    return pl.pallas_call(
        body, name=name, grid=grid,
        in_specs=in_specs + [pl.BlockSpec(memory_space=pl.ANY)] * len(intos),
        out_specs=[spec_of(o) for o in outs] + extra_out_specs,
        out_shape=[jax.ShapeDtypeStruct(o.arr.shape, o.arr.dtype) if isinstance(o, Into) else shape_of(o) for o in outs]
        + extra_out_shape,
        input_output_aliases=aliases,
        compiler_params=_cparams(sem),
    )(*operands, *[o.arr for _, o in intos])


def _rows(fn, name, ins, outs, accs=(), tm_cap=384):
    tp = next(o.arr.shape[0] for o in ins if isinstance(o, Rw))
    tm = _tile(tp, tm_cap)
    n_in, n_out, n_acc = len(ins), len(outs), len(accs)
    n_into = sum(isinstance(o, Into) for o in outs)

    def body(*refs):
        i = pl.program_id(0)
        res = fn(i * tm, *[r[...] for r in refs[:n_in]])
        if not isinstance(res, (tuple, list)):
            res = (res,)
        assert len(res) == n_out + n_acc, (name, len(res))
        out_refs = refs[n_in + n_into:]
        for k in range(n_out):
            out_refs[k][...] = res[k].astype(out_refs[k].dtype)
        for k in range(n_acc):
            ref = out_refs[n_out + k]

            @pl.when(i == 0)
            def _():
                ref[...] = jnp.zeros_like(ref)

            ref[...] += res[n_out + k]

    in_specs = []
    for o in ins:
        if isinstance(o, Rw):
            in_specs.append(pl.BlockSpec((tm, o.width), functools.partial(lambda i, cb: (i, cb), cb=o.cb)))
        else:
            in_specs.append(pl.BlockSpec(o.arr.shape, functools.partial(lambda i, nd: (0,) * nd, nd=o.arr.ndim)))

    def spec_of(o):
        if isinstance(o, Into):
            assert o.col0 % o.width == 0, (name, o.col0, o.width)
            return pl.BlockSpec((tm, o.width), functools.partial(lambda i, cb: (i, cb), cb=o.col0 // o.width))
        return pl.BlockSpec((tm, o[0]), lambda i: (i, 0))

    return _call_with_into(
        body, name, (tp // tm,), in_specs, [o.arr for o in ins], list(outs), spec_of,
        lambda o: jax.ShapeDtypeStruct((tp, o[0]), o[1]),
        [pl.BlockSpec(s, functools.partial(lambda i, nd: (0,) * nd, nd=len(s))) for s in accs],
        [jax.ShapeDtypeStruct(s, F32) for s in accs], "arbitrary")


class Cl:
    def __init__(self, arr, col0=0):
        self.arr, self.col0 = arr, col0


def _cols(fn, name, ins, outs, ncols, tc):
    assert ncols % tc == 0
    n_in, n_out = len(ins), len(outs)
    n_into = sum(isinstance(o, Into) for o in outs)

    def body(*refs):
        res = fn(*[r[...] for r in refs[:n_in]])
        if not isinstance(res, (tuple, list)):
            res = (res,)
        assert len(res) == n_out, (name, len(res))
        out_refs = refs[n_in + n_into:]
        for k in range(n_out):
            out_refs[k][...] = res[k].astype(out_refs[k].dtype)

    in_specs = []
    for o in ins:
        assert o.col0 % tc == 0, (name, o.col0, tc)
        in_specs.append(pl.BlockSpec((o.arr.shape[0], tc), functools.partial(lambda j, off: (0, j + off), off=o.col0 // tc)))

    def spec_of(o):
        if isinstance(o, Into):
            assert o.col0 % tc == 0 and o.width == ncols, (name, o.col0, o.width)
            return pl.BlockSpec((o.arr.shape[0], tc), functools.partial(lambda j, off: (0, j + off), off=o.col0 // tc))
        return pl.BlockSpec((o[0], tc), lambda j: (0, j))

    return _call_with_into(body, name, (ncols // tc,), in_specs, [o.arr for o in ins], list(outs), spec_of,
                           lambda o: jax.ShapeDtypeStruct((o[0], ncols), o[1]), [], [], "parallel")


def _ln_stats(u):
    mu = jnp.mean(u, axis=-1, keepdims=True)
    xc = u - mu
    var = jnp.mean(xc * xc, axis=-1, keepdims=True)
    rstd = lax.rsqrt(var + LN_EPS)
    return xc * rstd, rstd


def _ln_fwd(terms, g, b, name):
    coefs = [c for c, _ in terms]

    def fn(row0, *blk):
        xs, (gg, bb) = blk[:len(coefs)], blk[len(coefs):]
        u = sum(c * x for c, x in zip(coefs, xs))
        xhat, _ = _ln_stats(u)
        return xhat * gg + bb

    d = terms[0][1].shape[1]
    return _rows(fn, name, [Rw(x) for _, x in terms] + [Pm(g.reshape(1, d)), Pm(b.reshape(1, d))], [(d, F32)])[0]


def _ln_bwd(dy_terms, u_terms, g, name):
    dc = [c for c, _ in dy_terms]
    uc = [c for c, _ in u_terms]
    d = u_terms[0][1].shape[1]

    def fn(row0, *blk):
        dys = blk[:len(dc)]
        xs = blk[len(dc):len(dc) + len(uc)]
        gg = blk[-1]
        dy = sum(c * x for c, x in zip(dc, dys))
        u = sum(c * x for c, x in zip(uc, xs))
        xhat, rstd = _ln_stats(u)
        gdy = dy * gg
        m1 = jnp.mean(gdy, axis=-1, keepdims=True)
        m2 = jnp.mean(gdy * xhat, axis=-1, keepdims=True)
        du = rstd * (gdy - m1 - xhat * m2)
        return du, jnp.sum(dy * xhat, axis=0, keepdims=True), jnp.sum(dy, axis=0, keepdims=True)

    ins = [Rw(x) for _, x in dy_terms] + [Rw(x) for _, x in u_terms] + [Pm(g.reshape(1, d))]
    return _rows(fn, name, ins, [(d, F32)], accs=[(1, d), (1, d)])


def _loss_head(y, tgt, t_real, name):
    d = y.shape[1]

    def fn(row0, yb, tb):
        rows = _row_ids(yb.shape, row0)
        live = (rows >= N_META) & (rows < t_real)
        diff = jnp.where(live, yb - tb, 0.0)
        return diff * (1.0 / d), jnp.sum(diff * diff, axis=0, keepdims=True) * (0.5 / d)

    return _rows(fn, name, [Rw(y), Rw(tgt)], [(d, F32)], accs=[(1, d)])


def _rms(x, g):
    r = lax.rsqrt(jnp.mean(x * x, axis=-1, keepdims=True) + RMS_EPS)
    return x * r * g


def _rms_bwd(dy, x, g):
    r = lax.rsqrt(jnp.mean(x * x, axis=-1, keepdims=True) + RMS_EPS)
    gdy = dy * g
    dx = r * gdy - x * (r * r * r) * jnp.mean(gdy * x, axis=-1, keepdims=True)
    return dx, jnp.sum(dy * x * r, axis=0, keepdims=True)


def _mla_norms(proj, qn, kvn, name):
    def fn(row0, cq, ckv, g1, g2):
        return _rms(cq, g1), _rms(ckv, g2)

    return _rows(fn, name, [Rw(proj, Q_RANK, C_CQ // Q_RANK), Rw(proj, KV_RANK, C_CKV // KV_RANK),
                            Pm(qn.reshape(1, Q_RANK)), Pm(kvn.reshape(1, KV_RANK))],
                 [(Q_RANK, BF16), (KV_RANK, BF16)])


def _mla_norms_bwd(dcqn, dckvn, dkrp, proj, qn, kvn, dproj, name):
    def fn(row0, d1, d2, dkr, cq, ckv, g1, g2):
        dx1, dg1 = _rms_bwd(d1, cq, g1)
        dx2, dg2 = _rms_bwd(d2, ckv, g2)
        return jnp.concatenate([dx1, dx2, dkr], axis=1), dg1, dg2

    return _rows(fn, name, [Rw(dcqn), Rw(dckvn), Rw(dkrp), Rw(proj, Q_RANK, C_CQ // Q_RANK), Rw(proj, KV_RANK, C_CKV // KV_RANK),
                            Pm(qn.reshape(1, Q_RANK)), Pm(kvn.reshape(1, KV_RANK))],
                 [Into(dproj, C_CQ, PROJ_COLS - C_CQ)], accs=[(1, Q_RANK), (1, KV_RANK)])


def _fold_rope(z):
    return z + pltpu.roll(z, QK_ROPE, 1)


def _mla_pack(qext, kv, proj, cs, name):
    tp = qext.shape[0]
    tm = _tile(tp, 384)
    hw, nope_all = 2 * LANE, HEADS * QK_NOPE

    def body(q_ref, kv_ref, kr_ref, cs_ref, qo_ref, ko_ref, vo_ref):
        cs_ = cs_ref[...]
        low = lax.broadcasted_iota(jnp.int32, cs_.shape, 1) < QK_ROPE
        kr = _fold_rope(kr_ref[...] * cs_).astype(BF16)
        for h in range(HEADS):
            qr = jnp.where(low, _fold_rope(q_ref[:, h * hw + QK_NOPE:(h + 1) * hw] * cs_), 0.0)
            qo_ref[:, h * hw:h * hw + QK_NOPE] = (q_ref[:, h * hw:h * hw + QK_NOPE] * ATT_SCALE).astype(BF16)
            qo_ref[:, h * hw + QK_NOPE:(h + 1) * hw] = (qr * ATT_SCALE).astype(BF16)
            ko_ref[:, h * hw:h * hw + QK_NOPE] = kv_ref[:, h * QK_NOPE:(h + 1) * QK_NOPE].astype(BF16)
            ko_ref[:, h * hw + QK_NOPE:(h + 1) * hw] = kr
        vo_ref[...] = kv_ref[:, nope_all:].astype(BF16)

    row = lambda w: pl.BlockSpec((tm, w), lambda i: (i, 0))
    return pl.pallas_call(
        body, name=name, grid=(tp // tm,),
        in_specs=[row(HEADS * hw), row(2 * nope_all), pl.BlockSpec((tm, LANE), lambda i: (i, C_KRP // LANE)), row(LANE)],
        out_specs=[row(HEADS * hw), row(HEADS * hw), row(nope_all)],
        out_shape=[jax.ShapeDtypeStruct((tp, HEADS * hw), BF16),
                   jax.ShapeDtypeStruct((tp, HEADS * hw), BF16),
                   jax.ShapeDtypeStruct((tp, nope_all), BF16)],
        compiler_params=_cparams("parallel"),
    )(qext, kv, proj, cs)


def _mla_unpack(dk, dv, cs, name):
    tp = dk.shape[0]
    tm = _tile(tp, 384)
    hw, nope_all = 2 * LANE, HEADS * QK_NOPE

    def body(dk_ref, dv_ref, cs_ref, dkv_ref, dkr_ref):
        cs_ = cs_ref[...]
        low = lax.broadcasted_iota(jnp.int32, cs_.shape, 1) < QK_ROPE
        dkr = None
        for h in range(HEADS):
            dkv_ref[:, h * QK_NOPE:(h + 1) * QK_NOPE] = dk_ref[:, h * hw:h * hw + QK_NOPE].astype(BF16)
            part = jnp.where(low, dk_ref[:, h * hw + QK_NOPE:(h + 1) * hw], 0.0)
            dkr = part if h == 0 else dkr + part
        dkv_ref[:, nope_all:] = dv_ref[...].astype(BF16)
        dkr_ref[...] = _fold_rope(dkr) * cs_

    row = lambda w: pl.BlockSpec((tm, w), lambda i: (i, 0))
    return pl.pallas_call(
        body, name=name, grid=(tp // tm,),
        in_specs=[row(HEADS * hw), row(nope_all), row(LANE)],
        out_specs=[row(2 * nope_all), row(LANE)],
        out_shape=[jax.ShapeDtypeStruct((tp, 2 * nope_all), BF16), jax.ShapeDtypeStruct((tp, LANE), F32)],
        compiler_params=_cparams("parallel"),
    )(dk, dv, cs)


def _attn_fwd(q, k, v, t_real, name):
    tp = q.shape[0]
    tq = _tile(tp, 1408)
    tkc = _tile(tp, 1408)
    nkc = -(-t_real // tkc)

    def body(q_ref, k_ref, v_ref, o_ref, lse_ref):
        qb = q_ref[...]
        m = l = acc = None
        for c in range(nkc):
            s = lax.dot_general(qb, k_ref[c * tkc:(c + 1) * tkc, :], (((1,), (1,)), ((), ())), preferred_element_type=F32)
            if (c + 1) * tkc > t_real:
                cols = lax.broadcasted_iota(jnp.int32, s.shape, 1) + c * tkc
                s = jnp.where(cols < t_real, s, NEG_BIG)
            mc = jnp.max(s, axis=-1, keepdims=True)
            m_new = mc if c == 0 else jnp.maximum(m, mc)
            p = jnp.exp2(s - m_new)
            lc = jnp.sum(p, axis=-1, keepdims=True)
            pv = jnp.dot(p.astype(BF16), v_ref[c * tkc:(c + 1) * tkc, :], preferred_element_type=F32)
            if c == 0:
                l, acc = lc, pv
            else:
                alpha = jnp.exp2(m - m_new)
                l, acc = alpha * l + lc, alpha * acc + pv
            m = m_new
        o_ref[...] = acc / l
        lse_ref[...] = m + jnp.log2(l)

    return pl.pallas_call(
        body, name=name, grid=(HEADS, tp // tq),
        in_specs=[pl.BlockSpec((tq, 2 * LANE), lambda h, i: (i, h)),
                  pl.BlockSpec((tp, 2 * LANE), lambda h, i: (0, h)),
                  pl.BlockSpec((tp, LANE), lambda h, i: (0, h))],
        out_specs=[pl.BlockSpec((tq, LANE), lambda h, i: (i, h)),
                   pl.BlockSpec((None, tq, 1), lambda h, i: (h, i, 0))],
        out_shape=[jax.ShapeDtypeStruct((tp, HEADS * LANE), F32),
                   jax.ShapeDtypeStruct((HEADS, tp, 1), F32)],
        compiler_params=_cparams("parallel", "parallel"),
    )(q, k, v)


def _attn_bwd(q, k, v, do, o, lse, cs, t_real, name):
    tp = q.shape[0]
    tq = _tile(tp, 1408)
    tkc = _tile(tp, 704, 64)
    nkc = -(-t_real // tkc)

    def body(q_ref, k_ref, v_ref, do_ref, o_ref, lse_ref, cs_ref, dq_ref, dk_ref, dv_ref):
        i = pl.program_id(1)

        @pl.when(i == 0)
        def _():
            dk_ref[...] = jnp.zeros_like(dk_ref)
            dv_ref[...] = jnp.zeros_like(dv_ref)

        qb = q_ref[...]
        dob = do_ref[...]
        dob16 = dob.astype(BF16)
        dol2 = (dob * LN2).astype(BF16)
        delta = jnp.sum(dob * o_ref[...], axis=-1, keepdims=True) * LN2
        lse = lse_ref[...]
        dq = None
        for c in range(nkc):
            ks = slice(c * tkc, (c + 1) * tkc)
            kb = k_ref[ks, :]
            s = lax.dot_general(qb, kb, (((1,), (1,)), ((), ())), preferred_element_type=F32)
            p = jnp.exp2(s - lse)
            if (c + 1) * tkc > t_real:
                cols = lax.broadcasted_iota(jnp.int32, s.shape, 1) + c * tkc
                p = jnp.where(cols < t_real, p, 0.0)
            dp = lax.dot_general(dol2, v_ref[ks, :], (((1,), (1,)), ((), ())), preferred_element_type=F32)
            ds = (p * (dp - delta)).astype(BF16)
            dqc = jnp.dot(ds, kb, preferred_element_type=F32)
            dq = dqc if c == 0 else dq + dqc
            dk_ref[ks, :] += lax.dot_general(ds, qb, (((0,), (0,)), ((), ())), preferred_element_type=F32)
            dv_ref[ks, :] += lax.dot_general(p.astype(BF16), dob16, (((0,), (0,)), ((), ())), preferred_element_type=F32)
        cs_ = cs_ref[...]
        low = lax.broadcasted_iota(jnp.int32, cs_.shape, 1) < QK_ROPE
        dq = dq * ATT_SCALE
        dq_ref[:, :QK_NOPE] = dq[:, :QK_NOPE].astype(BF16)
        dq_ref[:, QK_NOPE:] = (_fold_rope(jnp.where(low, dq[:, QK_NOPE:], 0.0)) * cs_).astype(BF16)

    return pl.pallas_call(
        body, name=name, grid=(HEADS, tp // tq),
        in_specs=[pl.BlockSpec((tq, 2 * LANE), lambda h, i: (i, h)),
                  pl.BlockSpec((tp, 2 * LANE), lambda h, i: (0, h)),
                  pl.BlockSpec((tp, LANE), lambda h, i: (0, h)),
                  pl.BlockSpec((tq, LANE), lambda h, i: (i, h)),
                  pl.BlockSpec((tq, LANE), lambda h, i: (i, h)),
                  pl.BlockSpec((None, tq, 1), lambda h, i: (h, i, 0)),
                  pl.BlockSpec((tq, LANE), lambda h, i: (i, 0))],
        out_specs=[pl.BlockSpec((tq, 2 * LANE), lambda h, i: (i, h)),
                   pl.BlockSpec((tp, 2 * LANE), lambda h, i: (0, h)),
                   pl.BlockSpec((tp, LANE), lambda h, i: (0, h))],
        out_shape=[jax.ShapeDtypeStruct((tp, HEADS * 2 * LANE), BF16),
                   jax.ShapeDtypeStruct((tp, HEADS * 2 * LANE), F32),
                   jax.ShapeDtypeStruct((tp, HEADS * LANE), F32)],
        compiler_params=_cparams("parallel", "arbitrary"),
    )(q, k, v, do, o, lse, cs)


def _shift_rows(x, s):
    tp = x.shape[0]
    return x if s % tp == 0 else pltpu.roll(x, s % tp, 0)


def _conv_fwd_val(xm, w, b, pad_left):
    acc = b + w[0:1, :] * _shift_rows(xm, pad_left)
    for k in range(1, w.shape[0]):
        acc = acc + w[k:k + 1, :] * _shift_rows(xm, pad_left - k)
    return acc


def _conv_bwd_val(dy, xm, w, pad_left, live):
    kk = w.shape[0]
    dx = w[0:1, :] * _shift_rows(dy, -pad_left)
    dws = [jnp.sum(dy * _shift_rows(xm, pad_left), axis=0, keepdims=True)]
    for k in range(1, kk):
        dx = dx + w[k:k + 1, :] * _shift_rows(dy, k - pad_left)
        dws.append(jnp.sum(dy * _shift_rows(xm, pad_left - k), axis=0, keepdims=True))
    return jnp.where(live, dx, 0.0), jnp.concatenate(dws, axis=0), jnp.sum(dy, axis=0, keepdims=True)


def _lru_conv_fwd(proj, w, b, t_real, name):
    def fn(x, ww, bb):
        xm = jnp.where(_row_ids(x.shape) < t_real, x, 0.0)
        return _conv_fwd_val(xm, ww, bb, 2)

    return _cols(fn, name, [Cl(proj, C_LRU_X), Cl(w), Cl(b.reshape(1, -1))], [(proj.shape[0], F32)], D_MODEL, 128)[0]


def _lru_conv_bwd(dxc, proj, w, dproj, t_real, name):
    def fn(dy, x, ww):
        live = _row_ids(x.shape) < t_real
        xm = jnp.where(live, x, 0.0)
        dym = jnp.where(live, dy, 0.0)
        return _conv_bwd_val(dym, xm, ww, 2, live)

    return _cols(fn, name, [Cl(dxc), Cl(proj, C_LRU_X), Cl(w)],
                 [Into(dproj, C_LRU_X, D_MODEL), (w.shape[0], F32), (1, F32)], D_MODEL, 128)


def _ffn_conv_act(up, w, b, t_real, name):
    def fn(g, v, wg, wv, bg, bv):
        live = _row_ids(g.shape) < t_real
        gc = _conv_fwd_val(jnp.where(live, g, 0.0), wg, bg, 1)
        vc = _conv_fwd_val(jnp.where(live, v, 0.0), wv, bv, 1)
        return _gelu(gc) * vc

    b2 = b.reshape(1, -1)
    return _cols(fn, name, [Cl(up), Cl(up, D_FF), Cl(w), Cl(w, D_FF), Cl(b2), Cl(b2, D_FF)],
                 [(up.shape[0], BF16)], D_FF, 128)[0]


def _ffn_conv_act_bwd(dm, up, w, b, t_real, name):
    tp, kk = up.shape[0], w.shape[0]
    nb = D_FF // LANE
    assert nb >= 2

    def body(dm_ref, g_ref, v_ref, wg_ref, wv_ref, bg_ref, bv_ref, dup_ref, dwg_ref, dwv_ref, dbg_ref, dbv_ref, stage, sems):
        j = pl.program_id(0)
        slot = j % 2

        def copies(step, sl):
            return [pltpu.make_async_copy(stage.at[sl, half],
                                          dup_ref.at[:, pl.ds(pl.multiple_of(half * D_FF + step * LANE, LANE), LANE)],
                                          sems.at[sl, half]) for half in range(2)]

        @pl.when(j >= 2)
        def _():
            for cp in copies(j - 2, slot):
                cp.wait()

        live = _row_ids((tp, LANE)) < t_real
        gm, vm = jnp.where(live, g_ref[...], 0.0), jnp.where(live, v_ref[...], 0.0)
        gc = _conv_fwd_val(gm, wg_ref[...], bg_ref[...], 1)
        vc = _conv_fwd_val(vm, wv_ref[...], bv_ref[...], 1)
        act, dact = _gelu_and_grad(gc)
        dmm = jnp.where(live, dm_ref[...], 0.0)
        dgx, dwg_ref[...], dbg_ref[...] = _conv_bwd_val(dmm * vc * dact, gm, wg_ref[...], 1, live)
        dvx, dwv_ref[...], dbv_ref[...] = _conv_bwd_val(dmm * act, vm, wv_ref[...], 1, live)
        stage[slot, 0] = dgx.astype(BF16)
        stage[slot, 1] = dvx.astype(BF16)
        for cp in copies(j, slot):
            cp.start()

        @pl.when(j == nb - 1)
        def _():
            for cp in copies(j - 1, 1 - slot) + copies(j, slot):
                cp.wait()

    b2 = b.reshape(1, -1)
    col = lambda rows, off: pl.BlockSpec((rows, LANE), functools.partial(lambda j, o: (0, j + o), o=off))
    return pl.pallas_call(
        body, name=name, grid=(nb,),
        in_specs=[col(tp, 0), col(tp, 0), col(tp, nb), col(kk, 0), col(kk, nb), col(1, 0), col(1, nb)],
        out_specs=[pl.BlockSpec(memory_space=pl.ANY), col(kk, 0), col(kk, 0), col(1, 0), col(1, 0)],
        out_shape=[jax.ShapeDtypeStruct((tp, 2 * D_FF), BF16), jax.ShapeDtypeStruct((kk, D_FF), F32),
                   jax.ShapeDtypeStruct((kk, D_FF), F32), jax.ShapeDtypeStruct((1, D_FF), F32),
                   jax.ShapeDtypeStruct((1, D_FF), F32)],
        scratch_shapes=[pltpu.VMEM((2, 2, tp, LANE), BF16), pltpu.SemaphoreType.DMA((2, 2))],
        compiler_params=_cparams("arbitrary"),
    )(dm, up, up, w, w, b2, b2)


def _lru_gates_fwd(xc, wg, b4, lam, t_real, name):
    tp = xc.shape[0]
    tm = _tile(tp, 1408)

    def body(x_ref, w_ref, b_ref, lam_ref, r0_ref, r1_ref, i0_ref, i1_ref, a0_ref, a1_ref, u0_ref, u1_ref):
        x = x_ref[...]
        xb = x.astype(BF16)
        live = _row_ids(x.shape, pl.program_id(1) * tm) < t_real
        bb = b_ref[...]
        sp = _softplus_neg(lam_ref[...])
        gate = [_sigmoid(jnp.dot(xb, w_ref[k], preferred_element_type=F32) + bb[k:k + 1, :]) for k in range(4)]
        for d, (r_ref, i_ref, a_ref, u_ref) in enumerate(((r0_ref, i0_ref, a0_ref, u0_ref), (r1_ref, i1_ref, a1_ref, u1_ref))):
            r, ig = gate[d], gate[2 + d]
            a = jnp.exp(-LRU_C * r * sp[d:d + 1, :])
            r_ref[...] = r
            i_ref[...] = ig
            a_ref[...] = a
            u_ref[...] = jnp.where(live, jnp.sqrt(1.0 - a * a) * (ig * x), 0.0)

    blk = pl.BlockSpec((tm, LANE), lambda g, i: (i, g))
    return pl.pallas_call(
        body, name=name, grid=(LRU_BLOCKS, tp // tm),
        in_specs=[blk, pl.BlockSpec((None, 4, LANE, LANE), lambda g, i: (g, 0, 0, 0)),
                  pl.BlockSpec((4, LANE), lambda g, i: (0, g)), pl.BlockSpec((2, LANE), lambda g, i: (0, g))],
        out_specs=[blk] * 8,
        out_shape=[jax.ShapeDtypeStruct((tp, D_MODEL), F32)] * 8,
        compiler_params=_cparams("parallel", "parallel"),
    )(xc, wg, b4, lam)


def _lru_gates_bwd(l0, l1, da0, da1, r0, r1, i0, i1, a0, a1, xc, wg, lam, t_real, name):
    tp = xc.shape[0]
    tm = _tile(tp, 1408)

    rc = 32
    assert tm % rc == 0

    def fold(v):
        out = v[0:SUBLANE]
        for t in range(1, rc // SUBLANE):
            out = out + v[t * SUBLANE:(t + 1) * SUBLANE]
        return out

    def body(l0_ref, l1_ref, da0_ref, da1_ref, r0_ref, r1_ref, i0_ref, i1_ref, a0_ref, a1_ref, x_ref, w_ref, lam_ref,
             dx_ref, dw_ref, db_ref, dlam_ref, pre_s, dxp_s):
        i = pl.program_id(1)
        lam_ = lam_ref[...]
        sp = _softplus_neg(lam_)
        dsp_dlam = -_sigmoid(-lam_)

        def chunk(c, sums):
            r0 = pl.multiple_of(c * rc, rc)
            rows = pl.ds(r0, rc)
            x = x_ref[rows, :]
            live = _row_ids((rc, LANE), i * tm + r0) < t_real
            dxp = jnp.zeros_like(x)
            sums = list(sums)
            for d, (l_ref, da_ref, r_ref, i_ref, a_ref) in enumerate(((l0_ref, da0_ref, r0_ref, i0_ref, a0_ref),
                                                                      (l1_ref, da1_ref, r1_ref, i1_ref, a1_ref))):
                r, ig, a = r_ref[rows, :], i_ref[rows, :], a_ref[rows, :]
                du = jnp.where(live, l_ref[rows, :], 0.0)
                a2 = a * a
                rs = lax.rsqrt(1.0 - a2)
                dv = du * ((1.0 - a2) * rs)
                ds = du * (ig * x)
                dla = jnp.where(live, da_ref[rows, :] * a - ds * (a2 * rs), 0.0)
                dr = dla * (-LRU_C) * sp[d:d + 1, :]
                p_r = dr * r * (1.0 - r)
                p_i = dv * x * ig * (1.0 - ig)
                pre_s[d, rows, :] = p_r.astype(BF16)
                pre_s[2 + d, rows, :] = p_i.astype(BF16)
                dxp = dxp + dv * ig
                sums[d] = sums[d] + fold(p_r)
                sums[2 + d] = sums[2 + d] + fold(p_i)
                sums[4 + d] = sums[4 + d] + fold(dla * (-LRU_C) * r)
            dxp_s[rows, :] = dxp
            return tuple(sums)

        zero = jnp.zeros((SUBLANE, LANE), F32)
        sums = lax.fori_loop(0, tm // rc, chunk, (zero,) * 6)

        @pl.when(i == 0)
        def _():
            dw_ref[...] = jnp.zeros_like(dw_ref)
            db_ref[...] = jnp.zeros_like(db_ref)
            dlam_ref[...] = jnp.zeros_like(dlam_ref)

        xb = x_ref[...].astype(BF16)
        dx = dxp_s[...]
        for k in range(4):
            pk = pre_s[k]
            dx = dx + lax.dot_general(pk, w_ref[k], (((1,), (1,)), ((), ())), preferred_element_type=F32)
            dw_ref[k] += lax.dot_general(xb, pk, (((0,), (0,)), ((), ())), preferred_element_type=F32)
        db_ref[...] += jnp.concatenate([jnp.sum(sums[k], axis=0, keepdims=True) for k in range(4)], axis=0)
        dlam_ref[...] += jnp.concatenate([jnp.sum(sums[4 + d], axis=0, keepdims=True) * dsp_dlam[d:d + 1, :] for d in range(2)], axis=0)
        dx_ref[...] = dx

    blk = pl.BlockSpec((tm, LANE), lambda g, i: (i, g))
    return pl.pallas_call(
        body, name=name, grid=(LRU_BLOCKS, tp // tm),
        in_specs=[blk] * 11 + [pl.BlockSpec((None, 4, LANE, LANE), lambda g, i: (g, 0, 0, 0)),
                               pl.BlockSpec((2, LANE), lambda g, i: (0, g))],
        out_specs=[blk, pl.BlockSpec((None, 4, LANE, LANE), lambda g, i: (g, 0, 0, 0)),
                   pl.BlockSpec((4, LANE), lambda g, i: (0, g)), pl.BlockSpec((2, LANE), lambda g, i: (0, g))],
        out_shape=[jax.ShapeDtypeStruct((tp, D_MODEL), F32), jax.ShapeDtypeStruct((LRU_BLOCKS, 4, LANE, LANE), F32),
                   jax.ShapeDtypeStruct((4, D_MODEL), F32), jax.ShapeDtypeStruct((2, D_MODEL), F32)],
        scratch_shapes=[pltpu.VMEM((4, tm, LANE), BF16), pltpu.VMEM((tm, LANE), F32)],
        compiler_params=_cparams("parallel", "arbitrary"),
    )(l0, l1, da0, da1, r0, r1, i0, i1, a0, a1, xc, wg, lam)


SCAN_UNROLL = 4


def _loop_tiles(nt, step, carry):
    assert nt % SCAN_UNROLL == 0

    def trip(tt, c):
        for u in range(SCAN_UNROLL):
            c = step(tt * SCAN_UNROLL + u, c)
        return c

    return lax.fori_loop(0, nt // SCAN_UNROLL, trip, carry)


def _tile_scan(a, u, reverse):
    rows = lax.broadcasted_iota(jnp.int32, a.shape, 0)
    for s in (1, 2, 4):
        if reverse:
            keep = rows < SUBLANE - s
            a_sh, u_sh = pltpu.roll(a, SUBLANE - s, 0), pltpu.roll(u, SUBLANE - s, 0)
        else:
            keep = rows >= s
            a_sh, u_sh = pltpu.roll(a, s, 0), pltpu.roll(u, s, 0)
        u = u + a * jnp.where(keep, u_sh, 0.0)
        a = a * jnp.where(keep, a_sh, 1.0)
    return a, u


def _scan_fwd(a0, u0, a1, u1, name):
    tp, d = a0.shape
    tc = 128
    nt = tp // SUBLANE

    def body(a0_ref, u0_ref, a1_ref, u1_ref, h0_ref, h1_ref):
        def step(t, carry):
            c0, c1 = carry
            f = pl.multiple_of(t * SUBLANE, SUBLANE)
            b = pl.multiple_of((nt - 1 - t) * SUBLANE, SUBLANE)
            pa, pu = _tile_scan(a0_ref[pl.ds(f, SUBLANE), :], u0_ref[pl.ds(f, SUBLANE), :], False)
            h = pu + pa * c0
            h0_ref[pl.ds(f, SUBLANE), :] = h
            c0 = h[SUBLANE - 1:SUBLANE, :]
            pa, pu = _tile_scan(a1_ref[pl.ds(b, SUBLANE), :], u1_ref[pl.ds(b, SUBLANE), :], True)
            h = pu + pa * c1
            h1_ref[pl.ds(b, SUBLANE), :] = h
            c1 = h[0:1, :]
            return c0, c1

        z = jnp.zeros((1, tc), F32)
        _loop_tiles(nt, step, (z, z))

    blk = pl.BlockSpec((tp, tc), lambda j: (0, j))
    return pl.pallas_call(
        body, name=name, grid=(d // tc,), in_specs=[blk] * 4, out_specs=[blk] * 2,
        out_shape=[jax.ShapeDtypeStruct((tp, d), F32)] * 2,
        compiler_params=_cparams("parallel"),
    )(a0, u0, a1, u1)


def _scan_bwd(dh, a0, a1, h0, h1, name):
    tp, d = dh.shape
    tc = 128
    nt = tp // SUBLANE

    def body(dh_ref, a0_ref, a1_ref, h0_ref, h1_ref, l0_ref, l1_ref, da0_ref, da1_ref):
        rows8 = lax.broadcasted_iota(jnp.int32, (SUBLANE, tc), 0)

        def step(t, carry):
            c0, c1 = carry
            b = pl.multiple_of((nt - 1 - t) * SUBLANE, SUBLANE)
            f = pl.multiple_of(t * SUBLANE, SUBLANE)
            a = a0_ref[pl.ds(b, SUBLANE), :]
            a_next = jnp.where(rows8 < SUBLANE - 1, pltpu.roll(a, SUBLANE - 1, 0), 1.0)
            pa, pu = _tile_scan(a_next, dh_ref[pl.ds(b, SUBLANE), :], True)
            lam = pu + pa * c0
            l0_ref[pl.ds(b, SUBLANE), :] = lam
            c0 = a[0:1, :] * lam[0:1, :]
            a = a1_ref[pl.ds(f, SUBLANE), :]
            a_prev = jnp.where(rows8 >= 1, pltpu.roll(a, 1, 0), 1.0)
            pa, pu = _tile_scan(a_prev, dh_ref[pl.ds(f, SUBLANE), :], False)
            lam = pu + pa * c1
            l1_ref[pl.ds(f, SUBLANE), :] = lam
            c1 = a[SUBLANE - 1:SUBLANE, :] * lam[SUBLANE - 1:SUBLANE, :]
            return c0, c1

        z = jnp.zeros((1, tc), F32)
        _loop_tiles(nt, step, (z, z))
        rows = lax.broadcasted_iota(jnp.int32, (tp, tc), 0)
        da0_ref[...] = l0_ref[...] * jnp.where(rows >= 1, pltpu.roll(h0_ref[...], 1, 0), 0.0)
        da1_ref[...] = l1_ref[...] * jnp.where(rows < tp - 1, pltpu.roll(h1_ref[...], tp - 1, 0), 0.0)

    blk = pl.BlockSpec((tp, tc), lambda j: (0, j))
    return pl.pallas_call(
        body, name=name, grid=(d // tc,), in_specs=[blk] * 5, out_specs=[blk] * 4,
        out_shape=[jax.ShapeDtypeStruct((tp, d), F32)] * 4,
        compiler_params=_cparams("parallel"),
    )(dh, a0, a1, h0, h1)


def _gated_h(proj, h0, h1, name):
    def fn(row0, lg, x0, x1):
        return _gelu(lg) * (x0 + x1)

    return _rows(fn, name, [Rw(proj, D_MODEL, C_LRU_G // D_MODEL), Rw(h0), Rw(h1)], [(D_MODEL, BF16)])[0]


def _gated_h_bwd(dgh, proj, h0, h1, dproj, name):
    def fn(row0, dg, lg, x0, x1):
        act, dact = _gelu_and_grad(lg)
        return dg * (x0 + x1) * dact, dg * act

    return _rows(fn, name, [Rw(dgh), Rw(proj, D_MODEL, C_LRU_G // D_MODEL), Rw(h0), Rw(h1)],
                 [Into(dproj, C_LRU_G, D_MODEL), (D_MODEL, F32)])


def _mix(proj, y_mla, y_lru, name):
    def fn(row0, gm, gl, ym, yl):
        return _sigmoid(gm) * ym + _sigmoid(gl) * yl

    return _rows(fn, name, [Rw(proj, D_MODEL, C_G_MLA // D_MODEL), Rw(proj, D_MODEL, C_G_LRU // D_MODEL), Rw(y_mla), Rw(y_lru)],
                 [(D_MODEL, BF16)])[0]


def _mix_bwd(dz, proj, y_mla, y_lru, dproj, name):
    def fn(row0, dzb, gm, gl, ym, yl):
        sm, sl = _sigmoid(gm), _sigmoid(gl)
        dg = jnp.concatenate([dzb * ym * sm * (1.0 - sm), dzb * yl * sl * (1.0 - sl)], axis=1)
        return dzb * sm, dzb * sl, dg

    return _rows(fn, name, [Rw(dz), Rw(proj, D_MODEL, C_G_MLA // D_MODEL), Rw(proj, D_MODEL, C_G_LRU // D_MODEL),
                            Rw(y_mla), Rw(y_lru)], [(D_MODEL, BF16), (D_MODEL, BF16), Into(dproj, C_G_MLA, 2 * D_MODEL)])


def _layer_fwd(h, w_in, more_weights, cs, t_real, tag):
    proj = _matmul(h, w_in, tag + "proj", tb=True)
    w = dict(more_weights(0, proj), w_in=w_in)
    cqn, ckvn = _mla_norms(proj, w['q_norm'], w['kv_norm'], tag + "mla_norms")
    qext = _matmul(cqn, w['w_q'], tag + "q_up")
    kv = _matmul(ckvn, w['w_kv'], tag + "kv_up")
    qc, kc, vb = _mla_pack(qext, kv, proj, cs, tag + "mla_pack")
    o, lse = _attn_fwd(qc, kc, vb, t_real, tag + "attn_fwd")
    w.update(more_weights(1, o))
    y_mla = _matmul(o, w['w_o_mla'], tag + "o_mla")
    xc = _lru_conv_fwd(proj, w['lru_conv_w'], w['lru_conv_b'], t_real, tag + "lru_conv")
    r0, r1, i0, i1, a0, a1, u0, u1 = _lru_gates_fwd(xc, w['w_g'], w['b4'], w['lru_lambda'], t_real, tag + "lru_gates")
    h0, h1 = _scan_fwd(a0, u0, a1, u1, tag + "lru_scan")
    gh = _gated_h(proj, h0, h1, tag + "lru_gate_out")
    y_lru = _matmul(gh, w['w_o_lru'], tag + "o_lru")
    z = _mix(proj, y_mla, y_lru, tag + "mix")
    zo = _matmul(z, w['w_out'], tag + "w_out")
    hm = _ln_fwd([(DN_ALPHA, h), (1.0, zo)], w['ln1_g'], w['ln1_b'], tag + "ln1")
    w.update(more_weights(2, hm))
    up = _matmul(hm, w['w_up'], tag + "w_up", tb=True)
    m = _ffn_conv_act(up, w['ffn_conv_w'], w['ffn_conv_b'], t_real, tag + "ffn_conv")
    f = _matmul(m, w['w_down'], tag + "w_down", tk_cap=1408)
    out = _ln_fwd([(DN_ALPHA, hm), (1.0, f)], w['ln2_g'], w['ln2_b'], tag + "ln2")
    saved = dict(w=w, h=h, proj=proj, cqn=cqn, ckvn=ckvn, qc=qc, kc=kc, vb=vb, o=o, lse=lse, y_mla=y_mla, xc=xc,
                 r0=r0, r1=r1, i0=i0, i1=i1, a0=a0, a1=a1, h0=h0, h1=h1, gh=gh, y_lru=y_lru, z=z, zo=zo, hm=hm,
                 up=up, m=m, f=f)
    return out, saved


DW_MATMUL = dict(ta=True, out_dtype=BF16, tn_cap=1408, tk_cap=1408)


def _after(a, tok):
    return a if tok is None else a + tok.astype(a.dtype)


def _layer_bwd(dout_terms, s, cs, t_real, tag, emit, tok):
    w = s['w']
    g = {}
    du2, dg2, db2 = _ln_bwd(dout_terms, [(DN_ALPHA, s['hm']), (1.0, s['f'])], _after(w['ln2_g'], tok), tag + "ln2_bwd")
    g['ln2_g'], g['ln2_b'] = dg2, db2
    dm = _matmul(du2, w['w_down'], tag + "w_down_dx", tb=True)
    g['w_down'] = _matmul(s['m'], du2, tag + "w_down_dw", **DW_MATMUL)
    dup, dwg_, dwv_, dbg_, dbv_ = _ffn_conv_act_bwd(dm, s['up'], w['ffn_conv_w'], w['ffn_conv_b'], t_real, tag + "ffn_conv_bwd")
    g['ffn_conv_w'] = jnp.concatenate([dwg_, dwv_], axis=1)
    g['ffn_conv_b'] = jnp.concatenate([dbg_, dbv_], axis=1)
    dhm_mm = _matmul(dup, w['w_up'], tag + "w_up_dx", tk_cap=1408)
    g['w_up'] = _matmul(s['hm'], dup, tag + "w_up_dw", **DW_MATMUL)
    tok = emit('ffn', g)
    g = {}
    du1, dg1, db1 = _ln_bwd([(DN_ALPHA, du2), (1.0, dhm_mm)], [(DN_ALPHA, s['h']), (1.0, s['zo'])],
                            _after(w['ln1_g'], tok), tag + "ln1_bwd")
    g['ln1_g'], g['ln1_b'] = dg1, db1
    dz = _matmul(du1, w['w_out'], tag + "w_out_dx", tb=True)
    g['w_out'] = _matmul(s['z'], du1, tag + "w_out_dw", **DW_MATMUL)
    dproj = lax.empty(s['proj'].shape, BF16)
    dy_mla, dy_lru, dproj = _mix_bwd(dz, s['proj'], s['y_mla'], s['y_lru'], dproj, tag + "mix_bwd")
    do = _matmul(dy_mla, w['w_o_mla'], tag + "o_mla_dx", tb=True)
    g['w_o_mla'] = _matmul(s['o'], dy_mla, tag + "o_mla_dw", **DW_MATMUL)
    dqext, dkc, dv = _attn_bwd(s['qc'], s['kc'], s['vb'], do, s['o'], s['lse'], cs, t_real, tag + "attn_bwd")
    dkv, dkrp = _mla_unpack(dkc, dv, cs, tag + "mla_unpack")
    dcqn = _matmul(dqext, w['w_q'], tag + "q_up_dx", tb=True)
    g['w_q'] = _matmul(s['cqn'], dqext, tag + "q_up_dw", **DW_MATMUL)
    dckvn = _matmul(dkv, w['w_kv'], tag + "kv_up_dx", tb=True)
    g['w_kv'] = _matmul(s['ckvn'], dkv, tag + "kv_up_dw", **DW_MATMUL)
    dgh = _matmul(dy_lru, w['w_o_lru'], tag + "o_lru_dx", tb=True)
    g['w_o_lru'] = _matmul(s['gh'], dy_lru, tag + "o_lru_dw", **DW_MATMUL)
    dproj, dhs = _gated_h_bwd(dgh, s['proj'], s['h0'], s['h1'], dproj, tag + "lru_gate_out_bwd")
    l0, l1, da0, da1 = _scan_bwd(dhs, s['a0'], s['a1'], s['h0'], s['h1'], tag + "lru_scan_bwd")
    dxc, g['w_g'], g['b4'], g['lru_lambda'] = _lru_gates_bwd(
        l0, l1, da0, da1, s['r0'], s['r1'], s['i0'], s['i1'], s['a0'], s['a1'], s['xc'], w['w_g'], w['lru_lambda'],
        t_real, tag + "lru_gates_bwd")
    dproj, g['lru_conv_w'], g['lru_conv_b'] = _lru_conv_bwd(dxc, s['proj'], w['lru_conv_w'], dproj, t_real, tag + "lru_conv_bwd")
    tok = emit('mid', g)
    dproj, dqn, dkvn = _mla_norms_bwd(dcqn, dckvn, _after(dkrp, tok), s['proj'], w['q_norm'], w['kv_norm'], dproj,
                                      tag + "mla_norms_bwd")
    tok = emit('in', {'w_in': _matmul(s['h'], dproj, tag + "proj_dw", **DW_MATMUL), 'q_norm': dqn, 'kv_norm': dkvn})
    dh_mm = _matmul(dproj, w['w_in'], tag + "proj_dx", tk_cap=1536)
    return [(DN_ALPHA, du1), (1.0, dh_mm)], tok


def _swap_halves(a, axis=-1):
    h1, h2 = jnp.split(a, 2, axis=axis)
    return jnp.concatenate([h2, h1], axis=axis)


def _w_in_kernel(w_in_t):
    cq, ckv, kr, lg, lx, gm, gl = jnp.split(w_in_t, [256, 384, 448, 1472, 2496, 3520], axis=0)
    return jnp.concatenate([lg, lx, gm, gl, cq, ckv, kr, _swap_halves(kr, axis=0)], axis=0)


def _layer_weights(fl):
    w = {}
    if 'w_uq' in fl:
        uq = fl['w_uq']
        w['w_q'] = jnp.concatenate([uq, _swap_halves(uq[..., QK_NOPE:])], axis=-1).reshape(Q_RANK, HEADS * 2 * LANE)
        w['w_kv'] = jnp.concatenate([fl['w_uk'].reshape(KV_RANK, -1), fl['w_uv'].reshape(KV_RANK, -1)], axis=1).astype(BF16)
        w['w_g'] = jnp.moveaxis(jnp.concatenate([fl['w_rg'], fl['w_ig']], axis=0), 0, 1).astype(BF16)
        w['b4'] = jnp.concatenate([fl['b_rg'], fl['b_ig']], axis=0)
    for n in ('q_norm', 'kv_norm', 'w_o_mla', 'lru_conv_w', 'lru_conv_b', 'lru_lambda', 'w_o_lru', 'w_out', 'ln1_g',
              'ln1_b', 'w_up', 'ffn_conv_w', 'ffn_conv_b', 'w_down', 'ln2_g', 'ln2_b'):
        if n in fl:
            w[n] = fl[n]
    return w


def _layer_grads(g):
    out = {}
    if 'w_in' in g:
        lg, lx, gm, gl, cq, ckv, kr, krs = jnp.split(g['w_in'], [1024, 2048, 3072, 4096, 4352, 4480, 4544], axis=1)
        out['w_in'] = jnp.concatenate([cq, ckv, kr + _swap_halves(krs), lg, lx, gm, gl], axis=1)
    if 'w_q' in g:
        gq = g['w_q'].reshape(Q_RANK, HEADS, 2 * LANE)
        out['w_uq'] = jnp.concatenate([gq[..., :QK_NOPE], gq[..., QK_NOPE:QK_NOPE + QK_ROPE] + _swap_halves(gq[..., QK_NOPE + QK_ROPE:])], axis=-1)
    if 'w_kv' in g:
        out['w_uk'] = g['w_kv'][:, :HEADS * QK_NOPE].reshape(KV_RANK, HEADS, QK_NOPE)
        out['w_uv'] = g['w_kv'][:, HEADS * QK_NOPE:].reshape(KV_RANK, HEADS, V_HEAD)
    if 'w_g' in g:
        gg = jnp.moveaxis(g['w_g'], 1, 0)
        out['w_rg'], out['w_ig'] = gg[:2], gg[2:]
    if 'b4' in g:
        out['b_rg'], out['b_ig'] = g['b4'][:2], g['b4'][2:]
    for n in ('q_norm', 'kv_norm', 'lru_conv_b', 'ln1_g', 'ln1_b', 'ffn_conv_b', 'ln2_g', 'ln2_b'):
        if n in g:
            out[n] = g[n].reshape(-1)
    for n in ('w_o_mla', 'lru_conv_w', 'lru_lambda', 'w_o_lru', 'w_out', 'w_up', 'ffn_conv_w', 'w_down'):
        if n in g:
            out[n] = g[n]
    return out


def _rope_table(tp):
    half = QK_ROPE // 2
    inv_freq = jnp.exp(-math.log(ROPE_THETA) * jnp.arange(half, dtype=F32) / half)
    ang = jnp.arange(tp, dtype=F32)[:, None] * inv_freq[None, :]
    c, s = jnp.cos(ang), jnp.sin(ang)
    return jnp.concatenate([c, c, -s, s], axis=1)


def _local_step(x, target, meta, ln0_g, ln0_b, layer_w, t_pad, emit):
    seq = x.shape[0]
    t_real = N_META + seq
    zpad = jnp.zeros((t_pad - t_real, D_MODEL), F32)
    xin = jnp.concatenate([meta, x, zpad], axis=0)
    tgt = jnp.concatenate([jnp.zeros((N_META, D_MODEL), F32), target, zpad], axis=0)
    cs = _rope_table(t_pad)
    h = _ln_fwd([(1.0, xin)], ln0_g, ln0_b, "ln0")
    saved = []
    for l in range(DEPTH):
        w_in, rest_of_weights = layer_w[l](h)
        h, s = _layer_fwd(h, w_in, rest_of_weights, cs, t_real, "l%d_" % l)
        saved.append(s)
    dy, lossvec = _loss_head(h, tgt, t_real, "loss_head")
    terms, tok = [(1.0, dy)], None
    for l in reversed(range(DEPTH)):
        terms, tok = _layer_bwd(terms, saved[l], cs, t_real, "l%d_" % l,
                                functools.partial(lambda stage, g, l: emit(l, stage, _layer_grads(g)), l=l), tok)
    dxin, dg0, db0 = _ln_bwd(terms, [(1.0, xin)], _after(ln0_g, tok), "ln0_bwd")
    emit(None, 'head', {'meta_tokens': dxin[:N_META], 'ln0_g': dg0.reshape(-1), 'ln0_b': db0.reshape(-1), 'loss': lossvec})
    return dxin[N_META:t_real]


_HBM = pl.BlockSpec(memory_space=pltpu.HBM)
_SEM = pl.BlockSpec(memory_space=pltpu.SEMAPHORE)
_SIDE_EFFECT = pltpu.SideEffectType.DATAFLOW_SIDE_EFFECTING


def _peer_copies(src_refs, land_refs, scatters, send_sems, recv_sems):
    x, y, c = lax.axis_index("x"), lax.axis_index("y"), lax.axis_index("c")
    me = 4 * x + 2 * y + c
    copies = []
    for k in range(1, N_DEV):
        px = 1 - x if k & 4 else x
        py = 1 - y if k & 2 else y
        pc = 1 - c if k & 1 else c
        for t, (src, land) in enumerate(zip(src_refs, land_refs)):
            copies.append(pltpu.make_async_remote_copy(
                src_ref=src.at[4 * px + 2 * py + pc] if scatters[t] else src, dst_ref=land.at[me],
                send_sem=send_sems.at[7 * t + k - 1], recv_sem=recv_sems.at[7 * t + k - 1],
                device_id=(px, py, pc), device_id_type=pl.DeviceIdType.MESH))
    return me, copies


def _own_block_in_place(land, own):
    me = 4 * lax.axis_index("x") + 2 * lax.axis_index("y") + lax.axis_index("c")
    return lax.dynamic_update_slice_in_dim(land, own, me, 0)


def _gather_two_level(shards, name):
    nt = len(shards)

    def body(*refs):
        x_refs, out_refs = refs[:nt], refs[nt:2 * nt]
        token_ref, send_sems, recv_sems = refs[2 * nt:]
        x, y, c = lax.axis_index("x"), lax.axis_index("y"), lax.axis_index("c")
        me, sibling = (x, y, c), (x, y, 1 - c)
        chips = [(1 - x, y), (x, 1 - y), (1 - x, 1 - y)]

        def copy(t, k, block, to, own=False):
            px, py, pc = block
            slot = out_refs[t].at[4 * px + 2 * py + pc]
            return pltpu.make_async_remote_copy(
                src_ref=x_refs[t] if own else slot, dst_ref=slot,
                send_sem=send_sems.at[7 * t + k], recv_sem=recv_sems.at[7 * t + k],
                device_id=to, device_id_type=pl.DeviceIdType.MESH)

        sent = []
        for t in range(nt):
            first = [copy(t, 1 + j, me, (*chip, c), own=True) for j, chip in enumerate(chips)]
            first.append(copy(t, 0, me, sibling, own=True))
            for cp in first:
                cp.start()
            sent += first
        token_ref[...] = jnp.zeros_like(token_ref)
        for j, chip in enumerate(chips):
            for t in range(nt):
                copy(t, 1 + j, (*chip, c), me).wait_recv()
                passed = copy(t, 4 + j, (*chip, c), sibling)
                passed.start()
                sent.append(passed)
        for t in range(nt):
            copy(t, 0, sibling, me).wait_recv()
            for j, chip in enumerate(chips):
                copy(t, 4 + j, (*chip, 1 - c), me).wait_recv()
        for cp in sent:
            cp.wait_send()

    any_space = pl.BlockSpec(memory_space=pl.ANY)
    res = pl.pallas_call(
        body, name=name,
        out_shape=[jax.ShapeDtypeStruct((N_DEV,) + a.shape, a.dtype) for a in shards] + [jax.ShapeDtypeStruct((SUBLANE, LANE), F32)],
        in_specs=[any_space] * nt, out_specs=[any_space] * nt + [pl.BlockSpec(memory_space=pltpu.VMEM)],
        scratch_shapes=[pltpu.SemaphoreType.DMA((7 * nt,)), pltpu.SemaphoreType.DMA((7 * nt,))],
    )(*shards)
    return [_own_block_in_place(land, a[None]) for land, a in zip(res[:nt], shards)], res[nt][0, 0]


def _exchange_start(groups, name):
    flat = [it for grp in groups for it in grp]
    nt, ng = len(flat), len(groups)
    scatters = [sc for _, sc in flat]
    srcs = [pltpu.with_memory_space_constraint(a, pltpu.HBM) for a, _ in flat]
    land_shapes = [a.shape if sc else (N_DEV,) + a.shape for a, sc in flat]
    lands = [pltpu.with_memory_space_constraint(lax.empty(s, a.dtype), pltpu.HBM) for s, (a, _) in zip(land_shapes, flat)]
    bounds = [0]
    for grp in groups:
        bounds.append(bounds[-1] + len(grp))

    def body(*refs):
        src_refs, land_refs = refs[:nt], refs[nt:2 * nt]
        sem_refs = refs[2 * nt:2 * nt + 2 * ng]
        token_ref = refs[4 * nt + 2 * ng]
        for gi in range(ng):
            lo, hi = bounds[gi], bounds[gi + 1]
            _, copies = _peer_copies(src_refs[lo:hi], land_refs[lo:hi], scatters[lo:hi], sem_refs[2 * gi], sem_refs[2 * gi + 1])
            for cp in copies:
                cp.start()
        token_ref[...] = jnp.zeros_like(token_ref)

    out_shape = []
    for grp in groups:
        out_shape += [pltpu.SemaphoreType.DMA((7 * len(grp),)), pltpu.SemaphoreType.DMA((7 * len(grp),))]
    out_shape += [pltpu.HBM(a.shape, a.dtype) for a in srcs] + [pltpu.HBM(s, a.dtype) for s, a in zip(land_shapes, srcs)]
    out_shape += [jax.ShapeDtypeStruct((SUBLANE, LANE), F32)]
    res = pl.pallas_call(
        body, name=name, out_shape=out_shape,
        in_specs=[_HBM] * (2 * nt),
        out_specs=[_SEM] * (2 * ng) + [_HBM] * (2 * nt) + [pl.BlockSpec(memory_space=pltpu.VMEM)],
        input_output_aliases={t: 2 * ng + t for t in range(2 * nt)},
        compiler_params=pltpu.CompilerParams(has_side_effects=_SIDE_EFFECT),
    )(*srcs, *lands)
    sems, thru, token = res[:2 * ng], res[2 * ng:2 * ng + 2 * nt], res[-1]
    states = []
    for gi in range(ng):
        lo, hi = bounds[gi], bounds[gi + 1]
        states.append((sems[2 * gi], sems[2 * gi + 1], thru[lo:hi], thru[nt + lo:nt + hi], scatters[lo:hi]))
    return states, token[0, 0]


def _exchange_wait(state, after, name):
    send_sems, recv_sems, srcs, lands, scatters = state
    n = len(srcs)

    def body(*refs):
        _, copies = _peer_copies(refs[:n], refs[n:2 * n], scatters, refs[2 * n], refs[2 * n + 1])
        for cp in copies:
            cp.wait_send()
        for cp in copies:
            cp.wait_recv()

    res = pl.pallas_call(
        body, name=name,
        out_shape=[pltpu.HBM(a.shape, a.dtype) for a in srcs] + [pltpu.HBM(a.shape, a.dtype) for a in lands],
        in_specs=[_HBM] * (2 * n) + [_SEM, _SEM, _HBM],
        out_specs=[_HBM] * (2 * n),
        input_output_aliases={t: t for t in range(2 * n)},
        compiler_params=pltpu.CompilerParams(has_side_effects=_SIDE_EFFECT),
    )(*srcs, *lands, send_sems, recv_sems, pltpu.with_memory_space_constraint(after, pltpu.HBM))
    me = 4 * lax.axis_index("x") + 2 * lax.axis_index("y") + lax.axis_index("c")
    out = []
    for src, land, sc in zip(res[:n], res[n:], scatters):
        own = lax.dynamic_index_in_dim(src, me, 0, keepdims=True) if sc else src[None]
        out.append(lax.dynamic_update_slice_in_dim(land, own, me, 0))
    return out


def _as_rows(shape):
    return (1, shape[0]) if len(shape) == 1 else (math.prod(shape[:-1]), shape[-1])


def _sum_adamw(pieces, w, m, v, name):
    shape = w.shape
    nl = len(pieces)
    if nl > 1 and _as_rows(shape[1:])[0] % 16:
        pieces, nl = [jnp.stack(pieces, axis=1)], 1
    rows, cols = _as_rows(shape)
    rl = rows // nl
    cap = max(16, (1 << 18) // cols // 16 * 16)
    tr = _tile(rl, cap, 16)
    nb = rl // tr
    c1 = 1.0 / (1.0 - ADAM_B1 ** ADAM_STEP)
    c2 = 1.0 / (1.0 - ADAM_B2 ** ADAM_STEP)

    def body(*refs):
        p_refs = refs[:nl]
        w_ref, m_ref, v_ref, g_ref, d_ref, nm_ref, nv_ref = refs[nl:]
        li = pl.program_id(0)

        def total(p_ref):
            acc = p_ref[0].astype(F32)
            for k in range(1, N_DEV):
                acc = acc + p_ref[k].astype(F32)
            return acc

        gg = total(p_refs[0])
        for l in range(1, nl):
            gg = jnp.where(li == l, total(p_refs[l]), gg)
        nm = ADAM_B1 * m_ref[...] + (1.0 - ADAM_B1) * gg
        nv = ADAM_B2 * v_ref[...] + (1.0 - ADAM_B2) * (gg * gg)
        g_ref[...] = gg
        d_ref[...] = -ADAM_LR * ((nm * c1) / (jnp.sqrt(nv * c2) + ADAM_EPS) + ADAM_WD * w_ref[...])
        nm_ref[...] = nm
        nv_ref[...] = nv

    blk = pl.BlockSpec((tr, cols), lambda li, i: (li * nb + i, 0))
    p_specs = [pl.BlockSpec((N_DEV, tr, cols), functools.partial(lambda li, i, l: (0, jnp.where(li == l, i, 0), 0), l=l))
               for l in range(nl)]
    res = pl.pallas_call(
        body, name=name, grid=(nl, nb),
        in_specs=p_specs + [blk] * 3, out_specs=[blk] * 4,
        out_shape=[jax.ShapeDtypeStruct((rows, cols), F32)] * 4,
        compiler_params=_cparams("parallel", "parallel"),
    )(*[p.reshape(N_DEV, rl, cols) for p in pieces], *[a.reshape(rows, cols) for a in (w, m, v)])
    return [r.reshape(shape) for r in res]


def _to_shards(full, axis):
    shp = full.shape
    a = full.reshape(shp[:axis] + (N_DEV, shp[axis] // N_DEV) + shp[axis + 1:])
    return jnp.moveaxis(a, axis, 0)


def _from_shards(blocks, axis):
    a = jnp.moveaxis(blocks, 0, axis)
    shp = a.shape
    return a.reshape(shp[:axis] + (shp[axis] * shp[axis + 1],) + shp[axis + 2:])


def kernel(x, meta_tokens, ln0_g, ln0_b, w_in, q_norm, kv_norm, w_uq, w_uk, w_uv, w_o_mla, lru_conv_w, lru_conv_b, w_rg, b_rg, w_ig, b_ig, lru_lambda, w_o_lru, w_out, ln1_g, ln1_b, w_up, ffn_conv_w, ffn_conv_b, w_down, ln2_g, ln2_b, loss_target, m_meta_tokens, m_ln0_g, m_ln0_b, m_w_in, m_q_norm, m_kv_norm, m_w_uq, m_w_uk, m_w_uv, m_w_o_mla, m_lru_conv_w, m_lru_conv_b, m_w_rg, m_b_rg, m_w_ig, m_b_ig, m_lru_lambda, m_w_o_lru, m_w_out, m_ln1_g, m_ln1_b, m_w_up, m_ffn_conv_w, m_ffn_conv_b, m_w_down, m_ln2_g, m_ln2_b, v_meta_tokens, v_ln0_g, v_ln0_b, v_w_in, v_q_norm, v_kv_norm, v_w_uq, v_w_uk, v_w_uv, v_w_o_mla, v_lru_conv_w, v_lru_conv_b, v_w_rg, v_b_rg, v_w_ig, v_b_ig, v_lru_lambda, v_w_o_lru, v_w_out, v_ln1_g, v_ln1_b, v_w_up, v_ffn_conv_w, v_ffn_conv_b, v_w_down, v_ln2_g, v_ln2_b):
    args = (meta_tokens, ln0_g, ln0_b, w_in, q_norm, kv_norm, w_uq, w_uk, w_uv, w_o_mla, lru_conv_w, lru_conv_b, w_rg, b_rg, w_ig, b_ig, lru_lambda, w_o_lru, w_out, ln1_g, ln1_b, w_up, ffn_conv_w, ffn_conv_b, w_down, ln2_g, ln2_b)
    ms = (m_meta_tokens, m_ln0_g, m_ln0_b, m_w_in, m_q_norm, m_kv_norm, m_w_uq, m_w_uk, m_w_uv, m_w_o_mla, m_lru_conv_w, m_lru_conv_b, m_w_rg, m_b_rg, m_w_ig, m_b_ig, m_lru_lambda, m_w_o_lru, m_w_out, m_ln1_g, m_ln1_b, m_w_up, m_ffn_conv_w, m_ffn_conv_b, m_w_down, m_ln2_g, m_ln2_b)
    vs = (v_meta_tokens, v_ln0_g, v_ln0_b, v_w_in, v_q_norm, v_kv_norm, v_w_uq, v_w_uk, v_w_uv, v_w_o_mla, v_lru_conv_w, v_lru_conv_b, v_w_rg, v_b_rg, v_w_ig, v_b_ig, v_lru_lambda, v_w_o_lru, v_w_out, v_ln1_g, v_ln1_b, v_w_up, v_ffn_conv_w, v_ffn_conv_b, v_w_down, v_ln2_g, v_ln2_b)
    wd, md, vd = dict(zip(WEIGHTS, args)), dict(zip(WEIGHTS, ms)), dict(zip(WEIGHTS, vs))

    def shard_axis(n, l):
        return SHARD_AXIS[n] - (0 if l is None else 1)

    def shard(n, l):
        a = wd[n] if l is None else wd[n][l]
        if n in SENT_TRANSPOSED:
            a = a.T
        return a.astype(BF16) if n in BIG else a

    def whole(keys, landed):
        return {k: b.reshape(-1, b.shape[-1]) if k[0] in SENT_TRANSPOSED else _from_shards(b, shard_axis(*k))
                for k, b in zip(keys, landed)}

    first = [('meta_tokens', None), ('w_in', 0)]
    landed, token = _gather_two_level([shard(*k) for k in first], "gather_first")
    got_first = whole(first, landed)
    staged = [[(n, 0) for n in names] for names in STAGE_WEIGHTS]
    later = [(n, 1) for n in SHARDED if n != 'meta_tokens']
    gather, token = _exchange_start([[(_after(shard(*k), token), False) for k in keys] for keys in staged + [later]], "gather_start")

    def arrive(gi, keys, after, name):
        return whole(keys, _exchange_wait(gather[gi], after, name))

    def layer_weights(got, l, names):
        fl = {n: wd[n][l] for n in names if n in REPLICATED}
        fl.update({n: a for (n, _), a in got.items() if n in names})
        return _layer_weights(fl)

    ln0_g = _after(wd['ln0_g'], token)

    def first_layer(h):
        def more(stage, after):
            got = arrive(stage, staged[stage], after, "gather_wait_l0_%d" % stage)
            return layer_weights(got, 0, STAGE_WEIGHTS[stage] + STAGE_REPLICATED[stage])
        return _w_in_kernel(got_first['w_in', 0]), more

    def second_layer(h):
        got = arrive(len(staged), later, h, "gather_wait_l1")
        return _w_in_kernel(got['w_in', 1]), lambda stage, after: layer_weights(got, 1, STAGE_WEIGHTS[stage] + STAGE_REPLICATED[stage])

    sent = []
    pending = []

    def send(l, stage, grads):
        for n, g in grads.items():
            if n in SHARD_AXIS:
                g = _to_shards(g, shard_axis(n, l))
                pending.append(((n, l), (g.astype(BF16) if n in BIG else g, True)))
            else:
                pending.append(((n, l), (g.astype(BF16) if n in LARGE_REPLICATED else g, False)))
        if l == DEPTH - 1 and stage != 'in':
            return None
        (state,), tok = _exchange_start([[it for _, it in pending]], "grads_start_%s_%s" % (l, stage))
        sent.append(([k for k, _ in pending], state))
        pending.clear()
        return tok

    seq = x.shape[1]
    t_pad = -(-(N_META + seq + MIN_PAD_ROWS) // LANE) * LANE
    grad_x = _local_step(x[0], loss_target[0], got_first['meta_tokens', None], ln0_g, wd['ln0_b'],
                         [first_layer, second_layer], t_pad, send)

    pieces = {}
    for gi, (keys, state) in enumerate(sent):
        pieces.update(zip(keys, _exchange_wait(state, grad_x, "grads_wait_%d" % gi)))
    loss = jnp.sum(pieces['loss', None])
    outs = {}
    for n in WEIGHTS:
        ps = [pieces[n, None]] if (n, None) in pieces else [pieces[n, l] for l in range(DEPTH)]
        outs[n] = _sum_adamw(ps, wd[n], md[n], vd[n], "adamw_" + n)
    res = [loss, grad_x[None]]
    for k in range(4):
        res += [outs[n][k] for n in WEIGHTS]
    return tuple(res)
```

```python
import functools
import math

import jax
import jax.numpy as jnp
from jax import lax
from jax.experimental import pallas as pl
from jax.experimental.pallas import tpu as pltpu

F32 = jnp.float32
BF16 = jnp.bfloat16

N_DEV = 8
D_MODEL = 1024
N_META = 16
HEADS = 8
QK_NOPE = 128
QK_ROPE = 64
V_HEAD = 128
Q_RANK = 256
KV_RANK = 128
ROPE_THETA = 10000.0
LRU_BLOCKS = 8
LRU_C = 8.0
D_FF = 2816
DEPTH = 2
DN_ALPHA = (2.0 * DEPTH) ** 0.25
LN_EPS = 1e-5
RMS_EPS = 1e-6
LN2 = math.log(2.0)
ATT_SCALE = 1.0 / math.sqrt(QK_NOPE + QK_ROPE) / LN2
NEG_BIG = -1e30

ADAM_LR = 0.001
ADAM_B1 = 0.9
ADAM_B2 = 0.999
ADAM_EPS = 1e-08
ADAM_WD = 0.01
ADAM_STEP = 10

MIN_PAD_ROWS = 2
LANE = 128
SUBLANE = 8
VMEM_LIMIT = 56 * 1024 * 1024

PROJ_COLS = 4 * D_MODEL + Q_RANK + KV_RANK + 2 * QK_ROPE
C_LRU_G, C_LRU_X, C_G_MLA, C_G_LRU = 0, D_MODEL, 2 * D_MODEL, 3 * D_MODEL
C_CQ = 4 * D_MODEL
C_CKV = C_CQ + Q_RANK
C_KRP = C_CKV + KV_RANK

WEIGHTS = ['meta_tokens', 'ln0_g', 'ln0_b', 'w_in', 'q_norm', 'kv_norm', 'w_uq', 'w_uk', 'w_uv', 'w_o_mla',
           'lru_conv_w', 'lru_conv_b', 'w_rg', 'b_rg', 'w_ig', 'b_ig', 'lru_lambda', 'w_o_lru', 'w_out',
           'ln1_g', 'ln1_b', 'w_up', 'ffn_conv_w', 'ffn_conv_b', 'w_down', 'ln2_g', 'ln2_b']
SHARD_AXIS = {'meta_tokens': 1, 'w_in': 2, 'w_uq': 1, 'w_o_mla': 1, 'lru_conv_w': 2, 'b_rg': 2, 'b_ig': 2,
              'lru_lambda': 2, 'w_o_lru': 1, 'w_out': 1, 'w_up': 2, 'ffn_conv_w': 2, 'w_down': 1}
BIG = ['w_in', 'w_uq', 'w_o_mla', 'w_o_lru', 'w_out', 'w_up', 'w_down']
SHARDED = [n for n in WEIGHTS if n in SHARD_AXIS]
REPLICATED = [n for n in WEIGHTS if n not in SHARD_AXIS]
LARGE_REPLICATED = ['w_uk', 'w_uv', 'w_rg', 'w_ig']
SENT_TRANSPOSED = ['w_in', 'w_up']
STAGE_WEIGHTS = [['w_uq', 'lru_conv_w', 'b_rg', 'b_ig', 'lru_lambda'], ['w_o_mla', 'w_o_lru', 'w_out'], ['w_up', 'ffn_conv_w', 'w_down']]
STAGE_REPLICATED = [['q_norm', 'kv_norm', 'w_uk', 'w_uv', 'lru_conv_b', 'w_rg', 'w_ig'], ['ln1_g', 'ln1_b'], ['ffn_conv_b', 'ln2_g', 'ln2_b']]


def _cparams(*sem):
    return pltpu.CompilerParams(dimension_semantics=sem, vmem_limit_bytes=VMEM_LIMIT)


def _tile(n, cap, unit=LANE):
    best = None
    t = unit
    while t <= min(n, cap):
        if n % t == 0:
            best = t
        t += unit
    return n if best is None else best


def _sigmoid(x):
    return 1.0 / (1.0 + jnp.exp(-x))


_GELU_C = math.sqrt(2.0 / math.pi)


_GELU_A = 0.044715


def _gelu(x):
    t = jnp.tanh(x * (_GELU_C + (_GELU_C * _GELU_A) * (x * x)))
    hx = 0.5 * x
    return hx + hx * t


def _gelu_and_grad(x):
    x2 = x * x
    t = jnp.tanh(x * (_GELU_C + (_GELU_C * _GELU_A) * x2))
    hx = 0.5 * x
    dg = 0.5 + 0.5 * t + (hx * (1.0 - t * t)) * (_GELU_C + (3.0 * _GELU_C * _GELU_A) * x2)
    return hx + hx * t, dg


def _softplus_neg(lam):
    z = jnp.exp(-jnp.abs(lam))
    w = 1.0 + z
    log1p = jnp.where(w == 1.0, z, jnp.log(w) * z / (w - 1.0))
    return jnp.maximum(-lam, 0.0) + log1p


def _row_ids(shape, row0=0):
    return lax.broadcasted_iota(jnp.int32, shape, 0) + row0


def _matmul(a, b, name, ta=False, tb=False, out_dtype=F32, tm_cap=1408, tn_cap=1024, tk_cap=2048):
    if ta:
        kdim, m = a.shape
    else:
        m, kdim = a.shape
    if tb:
        n, k2 = b.shape
    else:
        k2, n = b.shape
    assert kdim == k2, (a.shape, b.shape, ta, tb)
    tm, tn, tk = _tile(m, tm_cap), _tile(n, tn_cap), _tile(kdim, tk_cap)
    nk = kdim // tk

    def body(a_ref, b_ref, o_ref, *acc):
        dn = (((0 if ta else 1,), (1 if tb else 0,)), ((), ()))
        part = lax.dot_general(a_ref[...].astype(BF16), b_ref[...].astype(BF16), dn, preferred_element_type=F32)
        if nk == 1:
            o_ref[...] = part.astype(o_ref.dtype)
            return
        acc_ref, k = acc[0], pl.program_id(2)

        @pl.when(k == 0)
        def _():
            acc_ref[...] = part

        @pl.when(k > 0)
        def _():
            acc_ref[...] += part

        @pl.when(k == nk - 1)
        def _():
            o_ref[...] = acc_ref[...].astype(o_ref.dtype)

    a_spec = pl.BlockSpec((tk, tm), lambda i, j, k: (k, i)) if ta else pl.BlockSpec((tm, tk), lambda i, j, k: (i, k))
    b_spec = pl.BlockSpec((tn, tk), lambda i, j, k: (j, k)) if tb else pl.BlockSpec((tk, tn), lambda i, j, k: (k, j))
    return pl.pallas_call(
        body, name=name,
        grid=(m // tm, n // tn, nk),
        in_specs=[a_spec, b_spec],
        out_specs=pl.BlockSpec((tm, tn), lambda i, j, k: (i, j)),
        out_shape=jax.ShapeDtypeStruct((m, n), out_dtype),
        scratch_shapes=[pltpu.VMEM((tm, tn), F32)] if nk > 1 else [],
        compiler_params=_cparams("parallel", "parallel", "arbitrary"),
    )(a, b)


class Rw:
    def __init__(self, arr, width=None, cb=0):
        self.arr, self.width, self.cb = arr, (arr.shape[1] if width is None else width), cb


class Pm:
    def __init__(self, arr):
        self.arr = arr


class Into:
    def __init__(self, arr, col0, width):
        self.arr, self.col0, self.width = arr, col0, width


def _call_with_into(body, name, grid, in_specs, operands, outs, spec_of, shape_of, extra_out_specs, extra_out_shape, sem):
    intos = [(k, o) for k, o in enumerate(outs) if isinstance(o, Into)]
    aliases = {len(operands) + n: k for n, (k, _) in enumerate(intos)}
    return pl.pallas_call(
        body, name=name, grid=grid,
        in_specs=in_specs + [pl.BlockSpec(memory_space=pl.ANY)] * len(intos),
        out_specs=[spec_of(o) for o in outs] + extra_out_specs,
        out_shape=[jax.ShapeDtypeStruct(o.arr.shape, o.arr.dtype) if isinstance(o, Into) else shape_of(o) for o in outs]
        + extra_out_shape,
        input_output_aliases=aliases,
        compiler_params=_cparams(sem),
    )(*operands, *[o.arr for _, o in intos])


def _rows(fn, name, ins, outs, accs=(), tm_cap=384):
    tp = next(o.arr.shape[0] for o in ins if isinstance(o, Rw))
    tm = _tile(tp, tm_cap)
    n_in, n_out, n_acc = len(ins), len(outs), len(accs)
    n_into = sum(isinstance(o, Into) for o in outs)

    def body(*refs):
        i = pl.program_id(0)
        res = fn(i * tm, *[r[...] for r in refs[:n_in]])
        if not isinstance(res, (tuple, list)):
            res = (res,)
        assert len(res) == n_out + n_acc, (name, len(res))
        out_refs = refs[n_in + n_into:]
        for k in range(n_out):
            out_refs[k][...] = res[k].astype(out_refs[k].dtype)
        for k in range(n_acc):
            ref = out_refs[n_out + k]

            @pl.when(i == 0)
            def _():
                ref[...] = jnp.zeros_like(ref)

            ref[...] += res[n_out + k]

    in_specs = []
    for o in ins:
        if isinstance(o, Rw):
            in_specs.append(pl.BlockSpec((tm, o.width), functools.partial(lambda i, cb: (i, cb), cb=o.cb)))
        else:
            in_specs.append(pl.BlockSpec(o.arr.shape, functools.partial(lambda i, nd: (0,) * nd, nd=o.arr.ndim)))

    def spec_of(o):
        if isinstance(o, Into):
            assert o.col0 % o.width == 0, (name, o.col0, o.width)
            return pl.BlockSpec((tm, o.width), functools.partial(lambda i, cb: (i, cb), cb=o.col0 // o.width))
        return pl.BlockSpec((tm, o[0]), lambda i: (i, 0))

    return _call_with_into(
        body, name, (tp // tm,), in_specs, [o.arr for o in ins], list(outs), spec_of,
        lambda o: jax.ShapeDtypeStruct((tp, o[0]), o[1]),
        [pl.BlockSpec(s, functools.partial(lambda i, nd: (0,) * nd, nd=len(s))) for s in accs],
        [jax.ShapeDtypeStruct(s, F32) for s in accs], "arbitrary")


class Cl:
    def __init__(self, arr, col0=0):
        self.arr, self.col0 = arr, col0


def _cols(fn, name, ins, outs, ncols, tc):
    assert ncols % tc == 0
    n_in, n_out = len(ins), len(outs)
    n_into = sum(isinstance(o, Into) for o in outs)

    def body(*refs):
        res = fn(*[r[...] for r in refs[:n_in]])
        if not isinstance(res, (tuple, list)):
            res = (res,)
        assert len(res) == n_out, (name, len(res))
        out_refs = refs[n_in + n_into:]
        for k in range(n_out):
            out_refs[k][...] = res[k].astype(out_refs[k].dtype)

    in_specs = []
    for o in ins:
        assert o.col0 % tc == 0, (name, o.col0, tc)
        in_specs.append(pl.BlockSpec((o.arr.shape[0], tc), functools.partial(lambda j, off: (0, j + off), off=o.col0 // tc)))

    def spec_of(o):
        if isinstance(o, Into):
            assert o.col0 % tc == 0 and o.width == ncols, (name, o.col0, o.width)
            return pl.BlockSpec((o.arr.shape[0], tc), functools.partial(lambda j, off: (0, j + off), off=o.col0 // tc))
        return pl.BlockSpec((o[0], tc), lambda j: (0, j))

    return _call_with_into(body, name, (ncols // tc,), in_specs, [o.arr for o in ins], list(outs), spec_of,
                           lambda o: jax.ShapeDtypeStruct((o[0], ncols), o[1]), [], [], "parallel")


def _ln_stats(u):
    mu = jnp.mean(u, axis=-1, keepdims=True)
    xc = u - mu
    var = jnp.mean(xc * xc, axis=-1, keepdims=True)
    rstd = lax.rsqrt(var + LN_EPS)
    return xc * rstd, rstd


def _ln_fwd(terms, g, b, name):
    coefs = [c for c, _ in terms]

    def fn(row0, *blk):
        xs, (gg, bb) = blk[:len(coefs)], blk[len(coefs):]
        u = sum(c * x for c, x in zip(coefs, xs))
        xhat, _ = _ln_stats(u)
        return xhat * gg + bb

    d = terms[0][1].shape[1]
    return _rows(fn, name, [Rw(x) for _, x in terms] + [Pm(g.reshape(1, d)), Pm(b.reshape(1, d))], [(d, F32)])[0]


def _ln_bwd(dy_terms, u_terms, g, name):
    dc = [c for c, _ in dy_terms]
    uc = [c for c, _ in u_terms]
    d = u_terms[0][1].shape[1]

    def fn(row0, *blk):
        dys = blk[:len(dc)]
        xs = blk[len(dc):len(dc) + len(uc)]
        gg = blk[-1]
        dy = sum(c * x for c, x in zip(dc, dys))
        u = sum(c * x for c, x in zip(uc, xs))
        xhat, rstd = _ln_stats(u)
        gdy = dy * gg
        m1 = jnp.mean(gdy, axis=-1, keepdims=True)
        m2 = jnp.mean(gdy * xhat, axis=-1, keepdims=True)
        du = rstd * (gdy - m1 - xhat * m2)
        return du, jnp.sum(dy * xhat, axis=0, keepdims=True), jnp.sum(dy, axis=0, keepdims=True)

    ins = [Rw(x) for _, x in dy_terms] + [Rw(x) for _, x in u_terms] + [Pm(g.reshape(1, d))]
    return _rows(fn, name, ins, [(d, F32)], accs=[(1, d), (1, d)])


def _loss_head(y, tgt, t_real, name):
    d = y.shape[1]

    def fn(row0, yb, tb):
        rows = _row_ids(yb.shape, row0)
        live = (rows >= N_META) & (rows < t_real)
        diff = jnp.where(live, yb - tb, 0.0)
        return diff * (1.0 / d), jnp.sum(diff * diff, axis=0, keepdims=True) * (0.5 / d)

    return _rows(fn, name, [Rw(y), Rw(tgt)], [(d, F32)], accs=[(1, d)])


def _rms(x, g):
    r = lax.rsqrt(jnp.mean(x * x, axis=-1, keepdims=True) + RMS_EPS)
    return x * r * g


def _rms_bwd(dy, x, g):
    r = lax.rsqrt(jnp.mean(x * x, axis=-1, keepdims=True) + RMS_EPS)
    gdy = dy * g
    dx = r * gdy - x * (r * r * r) * jnp.mean(gdy * x, axis=-1, keepdims=True)
    return dx, jnp.sum(dy * x * r, axis=0, keepdims=True)


def _mla_norms(proj, qn, kvn, name):
    def fn(row0, cq, ckv, g1, g2):
        return _rms(cq, g1), _rms(ckv, g2)

    return _rows(fn, name, [Rw(proj, Q_RANK, C_CQ // Q_RANK), Rw(proj, KV_RANK, C_CKV // KV_RANK),
                            Pm(qn.reshape(1, Q_RANK)), Pm(kvn.reshape(1, KV_RANK))],
                 [(Q_RANK, BF16), (KV_RANK, BF16)])


def _mla_norms_bwd(dcqn, dckvn, dkrp, proj, qn, kvn, dproj, name):
    def fn(row0, d1, d2, dkr, cq, ckv, g1, g2):
        dx1, dg1 = _rms_bwd(d1, cq, g1)
        dx2, dg2 = _rms_bwd(d2, ckv, g2)
        return jnp.concatenate([dx1, dx2, dkr], axis=1), dg1, dg2

    return _rows(fn, name, [Rw(dcqn), Rw(dckvn), Rw(dkrp), Rw(proj, Q_RANK, C_CQ // Q_RANK), Rw(proj, KV_RANK, C_CKV // KV_RANK),
                            Pm(qn.reshape(1, Q_RANK)), Pm(kvn.reshape(1, KV_RANK))],
                 [Into(dproj, C_CQ, PROJ_COLS - C_CQ)], accs=[(1, Q_RANK), (1, KV_RANK)])


def _fold_rope(z):
    return z + pltpu.roll(z, QK_ROPE, 1)


def _mla_pack(qext, kv, proj, cs, name):
    tp = qext.shape[0]
    tm = _tile(tp, 384)
    hw, nope_all = 2 * LANE, HEADS * QK_NOPE

    def body(q_ref, kv_ref, kr_ref, cs_ref, qo_ref, ko_ref, vo_ref):
        cs_ = cs_ref[...]
        low = lax.broadcasted_iota(jnp.int32, cs_.shape, 1) < QK_ROPE
        kr = _fold_rope(kr_ref[...] * cs_).astype(BF16)
        for h in range(HEADS):
            qr = jnp.where(low, _fold_rope(q_ref[:, h * hw + QK_NOPE:(h + 1) * hw] * cs_), 0.0)
            qo_ref[:, h * hw:h * hw + QK_NOPE] = (q_ref[:, h * hw:h * hw + QK_NOPE] * ATT_SCALE).astype(BF16)
            qo_ref[:, h * hw + QK_NOPE:(h + 1) * hw] = (qr * ATT_SCALE).astype(BF16)
            ko_ref[:, h * hw:h * hw + QK_NOPE] = kv_ref[:, h * QK_NOPE:(h + 1) * QK_NOPE].astype(BF16)
            ko_ref[:, h * hw + QK_NOPE:(h + 1) * hw] = kr
        vo_ref[...] = kv_ref[:, nope_all:].astype(BF16)

    row = lambda w: pl.BlockSpec((tm, w), lambda i: (i, 0))
    return pl.pallas_call(
        body, name=name, grid=(tp // tm,),
        in_specs=[row(HEADS * hw), row(2 * nope_all), pl.BlockSpec((tm, LANE), lambda i: (i, C_KRP // LANE)), row(LANE)],
        out_specs=[row(HEADS * hw), row(HEADS * hw), row(nope_all)],
        out_shape=[jax.ShapeDtypeStruct((tp, HEADS * hw), BF16),
                   jax.ShapeDtypeStruct((tp, HEADS * hw), BF16),
                   jax.ShapeDtypeStruct((tp, nope_all), BF16)],
        compiler_params=_cparams("parallel"),
    )(qext, kv, proj, cs)


def _mla_unpack(dk, dv, cs, name):
    tp = dk.shape[0]
    tm = _tile(tp, 384)
    hw, nope_all = 2 * LANE, HEADS * QK_NOPE

    def body(dk_ref, dv_ref, cs_ref, dkv_ref, dkr_ref):
        cs_ = cs_ref[...]
        low = lax.broadcasted_iota(jnp.int32, cs_.shape, 1) < QK_ROPE
        dkr = None
        for h in range(HEADS):
            dkv_ref[:, h * QK_NOPE:(h + 1) * QK_NOPE] = dk_ref[:, h * hw:h * hw + QK_NOPE].astype(BF16)
            part = jnp.where(low, dk_ref[:, h * hw + QK_NOPE:(h + 1) * hw], 0.0)
            dkr = part if h == 0 else dkr + part
        dkv_ref[:, nope_all:] = dv_ref[...].astype(BF16)
        dkr_ref[...] = _fold_rope(dkr) * cs_

    row = lambda w: pl.BlockSpec((tm, w), lambda i: (i, 0))
    return pl.pallas_call(
        body, name=name, grid=(tp // tm,),
        in_specs=[row(HEADS * hw), row(nope_all), row(LANE)],
        out_specs=[row(2 * nope_all), row(LANE)],
        out_shape=[jax.ShapeDtypeStruct((tp, 2 * nope_all), BF16), jax.ShapeDtypeStruct((tp, LANE), F32)],
        compiler_params=_cparams("parallel"),
    )(dk, dv, cs)


def _attn_fwd(q, k, v, t_real, name):
    tp = q.shape[0]
    tq = _tile(tp, 1408)
    tkc = _tile(tp, 1408)
    nkc = -(-t_real // tkc)

    def body(q_ref, k_ref, v_ref, o_ref, lse_ref):
        qb = q_ref[...]
        m = l = acc = None
        for c in range(nkc):
            s = lax.dot_general(qb, k_ref[c * tkc:(c + 1) * tkc, :], (((1,), (1,)), ((), ())), preferred_element_type=F32)
            if (c + 1) * tkc > t_real:
                cols = lax.broadcasted_iota(jnp.int32, s.shape, 1) + c * tkc
                s = jnp.where(cols < t_real, s, NEG_BIG)
            mc = jnp.max(s, axis=-1, keepdims=True)
            m_new = mc if c == 0 else jnp.maximum(m, mc)
            p = jnp.exp2(s - m_new)
            lc = jnp.sum(p, axis=-1, keepdims=True)
            pv = jnp.dot(p.astype(BF16), v_ref[c * tkc:(c + 1) * tkc, :], preferred_element_type=F32)
            if c == 0:
                l, acc = lc, pv
            else:
                alpha = jnp.exp2(m - m_new)
                l, acc = alpha * l + lc, alpha * acc + pv
            m = m_new
        o_ref[...] = acc / l
        lse_ref[...] = m + jnp.log2(l)

    return pl.pallas_call(
        body, name=name, grid=(HEADS, tp // tq),
        in_specs=[pl.BlockSpec((tq, 2 * LANE), lambda h, i: (i, h)),
                  pl.BlockSpec((tp, 2 * LANE), lambda h, i: (0, h)),
                  pl.BlockSpec((tp, LANE), lambda h, i: (0, h))],
        out_specs=[pl.BlockSpec((tq, LANE), lambda h, i: (i, h)),
                   pl.BlockSpec((None, tq, 1), lambda h, i: (h, i, 0))],
        out_shape=[jax.ShapeDtypeStruct((tp, HEADS * LANE), F32),
                   jax.ShapeDtypeStruct((HEADS, tp, 1), F32)],
        compiler_params=_cparams("parallel", "parallel"),
    )(q, k, v)


def _attn_bwd(q, k, v, do, o, lse, cs, t_real, name):
    tp = q.shape[0]
    tq = _tile(tp, 1408)
    tkc = _tile(tp, 704, 64)
    nkc = -(-t_real // tkc)

    def body(q_ref, k_ref, v_ref, do_ref, o_ref, lse_ref, cs_ref, dq_ref, dk_ref, dv_ref):
        i = pl.program_id(1)

        @pl.when(i == 0)
        def _():
            dk_ref[...] = jnp.zeros_like(dk_ref)
            dv_ref[...] = jnp.zeros_like(dv_ref)

        qb = q_ref[...]
        dob = do_ref[...]
        dob16 = dob.astype(BF16)
        dol2 = (dob * LN2).astype(BF16)
        delta = jnp.sum(dob * o_ref[...], axis=-1, keepdims=True) * LN2
        lse = lse_ref[...]
        dq = None
        for c in range(nkc):
            ks = slice(c * tkc, (c + 1) * tkc)
            kb = k_ref[ks, :]
            s = lax.dot_general(qb, kb, (((1,), (1,)), ((), ())), preferred_element_type=F32)
            p = jnp.exp2(s - lse)
            if (c + 1) * tkc > t_real:
                cols = lax.broadcasted_iota(jnp.int32, s.shape, 1) + c * tkc
                p = jnp.where(cols < t_real, p, 0.0)
            dp = lax.dot_general(dol2, v_ref[ks, :], (((1,), (1,)), ((), ())), preferred_element_type=F32)
            ds = (p * (dp - delta)).astype(BF16)
            dqc = jnp.dot(ds, kb, preferred_element_type=F32)
            dq = dqc if c == 0 else dq + dqc
            dk_ref[ks, :] += lax.dot_general(ds, qb, (((0,), (0,)), ((), ())), preferred_element_type=F32)
            dv_ref[ks, :] += lax.dot_general(p.astype(BF16), dob16, (((0,), (0,)), ((), ())), preferred_element_type=F32)
        cs_ = cs_ref[...]
        low = lax.broadcasted_iota(jnp.int32, cs_.shape, 1) < QK_ROPE
        dq = dq * ATT_SCALE
        dq_ref[:, :QK_NOPE] = dq[:, :QK_NOPE].astype(BF16)
        dq_ref[:, QK_NOPE:] = (_fold_rope(jnp.where(low, dq[:, QK_NOPE:], 0.0)) * cs_).astype(BF16)

    return pl.pallas_call(
        body, name=name, grid=(HEADS, tp // tq),
        in_specs=[pl.BlockSpec((tq, 2 * LANE), lambda h, i: (i, h)),
                  pl.BlockSpec((tp, 2 * LANE), lambda h, i: (0, h)),
                  pl.BlockSpec((tp, LANE), lambda h, i: (0, h)),
                  pl.BlockSpec((tq, LANE), lambda h, i: (i, h)),
                  pl.BlockSpec((tq, LANE), lambda h, i: (i, h)),
                  pl.BlockSpec((None, tq, 1), lambda h, i: (h, i, 0)),
                  pl.BlockSpec((tq, LANE), lambda h, i: (i, 0))],
        out_specs=[pl.BlockSpec((tq, 2 * LANE), lambda h, i: (i, h)),
                   pl.BlockSpec((tp, 2 * LANE), lambda h, i: (0, h)),
                   pl.BlockSpec((tp, LANE), lambda h, i: (0, h))],
        out_shape=[jax.ShapeDtypeStruct((tp, HEADS * 2 * LANE), BF16),
                   jax.ShapeDtypeStruct((tp, HEADS * 2 * LANE), F32),
                   jax.ShapeDtypeStruct((tp, HEADS * LANE), F32)],
        compiler_params=_cparams("parallel", "arbitrary"),
    )(q, k, v, do, o, lse, cs)


def _shift_rows(x, s):
    tp = x.shape[0]
    return x if s % tp == 0 else pltpu.roll(x, s % tp, 0)


def _conv_fwd_val(xm, w, b, pad_left):
    acc = b + w[0:1, :] * _shift_rows(xm, pad_left)
    for k in range(1, w.shape[0]):
        acc = acc + w[k:k + 1, :] * _shift_rows(xm, pad_left - k)
    return acc


def _conv_bwd_val(dy, xm, w, pad_left, live):
    kk = w.shape[0]
    dx = w[0:1, :] * _shift_rows(dy, -pad_left)
    dws = [jnp.sum(dy * _shift_rows(xm, pad_left), axis=0, keepdims=True)]
    for k in range(1, kk):
        dx = dx + w[k:k + 1, :] * _shift_rows(dy, k - pad_left)
        dws.append(jnp.sum(dy * _shift_rows(xm, pad_left - k), axis=0, keepdims=True))
    return jnp.where(live, dx, 0.0), jnp.concatenate(dws, axis=0), jnp.sum(dy, axis=0, keepdims=True)


def _lru_conv_fwd(proj, w, b, t_real, name):
    def fn(x, ww, bb):
        xm = jnp.where(_row_ids(x.shape) < t_real, x, 0.0)
        return _conv_fwd_val(xm, ww, bb, 2)

    return _cols(fn, name, [Cl(proj, C_LRU_X), Cl(w), Cl(b.reshape(1, -1))], [(proj.shape[0], F32)], D_MODEL, 128)[0]


def _lru_conv_bwd(dxc, proj, w, dproj, t_real, name):
    def fn(dy, x, ww):
        live = _row_ids(x.shape) < t_real
        xm = jnp.where(live, x, 0.0)
        dym = jnp.where(live, dy, 0.0)
        return _conv_bwd_val(dym, xm, ww, 2, live)

    return _cols(fn, name, [Cl(dxc), Cl(proj, C_LRU_X), Cl(w)],
                 [Into(dproj, C_LRU_X, D_MODEL), (w.shape[0], F32), (1, F32)], D_MODEL, 128)


def _ffn_conv_act(up, w, b, t_real, name):
    def fn(g, v, wg, wv, bg, bv):
        live = _row_ids(g.shape) < t_real
        gc = _conv_fwd_val(jnp.where(live, g, 0.0), wg, bg, 1)
        vc = _conv_fwd_val(jnp.where(live, v, 0.0), wv, bv, 1)
        return _gelu(gc) * vc

    b2 = b.reshape(1, -1)
    return _cols(fn, name, [Cl(up), Cl(up, D_FF), Cl(w), Cl(w, D_FF), Cl(b2), Cl(b2, D_FF)],
                 [(up.shape[0], BF16)], D_FF, 128)[0]


def _ffn_conv_act_bwd(dm, up, w, b, t_real, name):
    tp, kk = up.shape[0], w.shape[0]
    nb = D_FF // LANE
    assert nb >= 2

    def body(dm_ref, g_ref, v_ref, wg_ref, wv_ref, bg_ref, bv_ref, dup_ref, dwg_ref, dwv_ref, dbg_ref, dbv_ref, stage, sems):
        j = pl.program_id(0)
        slot = j % 2

        def copies(step, sl):
            return [pltpu.make_async_copy(stage.at[sl, half],
                                          dup_ref.at[:, pl.ds(pl.multiple_of(half * D_FF + step * LANE, LANE), LANE)],
                                          sems.at[sl, half]) for half in range(2)]

        @pl.when(j >= 2)
        def _():
            for cp in copies(j - 2, slot):
                cp.wait()

        live = _row_ids((tp, LANE)) < t_real
        gm, vm = jnp.where(live, g_ref[...], 0.0), jnp.where(live, v_ref[...], 0.0)
        gc = _conv_fwd_val(gm, wg_ref[...], bg_ref[...], 1)
        vc = _conv_fwd_val(vm, wv_ref[...], bv_ref[...], 1)
        act, dact = _gelu_and_grad(gc)
        dmm = jnp.where(live, dm_ref[...], 0.0)
        dgx, dwg_ref[...], dbg_ref[...] = _conv_bwd_val(dmm * vc * dact, gm, wg_ref[...], 1, live)
        dvx, dwv_ref[...], dbv_ref[...] = _conv_bwd_val(dmm * act, vm, wv_ref[...], 1, live)
        stage[slot, 0] = dgx.astype(BF16)
        stage[slot, 1] = dvx.astype(BF16)
        for cp in copies(j, slot):
            cp.start()

        @pl.when(j == nb - 1)
        def _():
            for cp in copies(j - 1, 1 - slot) + copies(j, slot):
                cp.wait()

    b2 = b.reshape(1, -1)
    col = lambda rows, off: pl.BlockSpec((rows, LANE), functools.partial(lambda j, o: (0, j + o), o=off))
    return pl.pallas_call(
        body, name=name, grid=(nb,),
        in_specs=[col(tp, 0), col(tp, 0), col(tp, nb), col(kk, 0), col(kk, nb), col(1, 0), col(1, nb)],
        out_specs=[pl.BlockSpec(memory_space=pl.ANY), col(kk, 0), col(kk, 0), col(1, 0), col(1, 0)],
        out_shape=[jax.ShapeDtypeStruct((tp, 2 * D_FF), BF16), jax.ShapeDtypeStruct((kk, D_FF), F32),
                   jax.ShapeDtypeStruct((kk, D_FF), F32), jax.ShapeDtypeStruct((1, D_FF), F32),
                   jax.ShapeDtypeStruct((1, D_FF), F32)],
        scratch_shapes=[pltpu.VMEM((2, 2, tp, LANE), BF16), pltpu.SemaphoreType.DMA((2, 2))],
        compiler_params=_cparams("arbitrary"),
    )(dm, up, up, w, w, b2, b2)


def _lru_gates_fwd(xc, wg, b4, lam, t_real, name):
    tp = xc.shape[0]
    tm = _tile(tp, 1408)

    def body(x_ref, w_ref, b_ref, lam_ref, r0_ref, r1_ref, i0_ref, i1_ref, a0_ref, a1_ref, u0_ref, u1_ref):
        x = x_ref[...]
        xb = x.astype(BF16)
        live = _row_ids(x.shape, pl.program_id(1) * tm) < t_real
        bb = b_ref[...]
        sp = _softplus_neg(lam_ref[...])
        gate = [_sigmoid(jnp.dot(xb, w_ref[k], preferred_element_type=F32) + bb[k:k + 1, :]) for k in range(4)]
        for d, (r_ref, i_ref, a_ref, u_ref) in enumerate(((r0_ref, i0_ref, a0_ref, u0_ref), (r1_ref, i1_ref, a1_ref, u1_ref))):
            r, ig = gate[d], gate[2 + d]
            a = jnp.exp(-LRU_C * r * sp[d:d + 1, :])
            r_ref[...] = r
            i_ref[...] = ig
            a_ref[...] = a
            u_ref[...] = jnp.where(live, jnp.sqrt(1.0 - a * a) * (ig * x), 0.0)

    blk = pl.BlockSpec((tm, LANE), lambda g, i: (i, g))
    return pl.pallas_call(
        body, name=name, grid=(LRU_BLOCKS, tp // tm),
        in_specs=[blk, pl.BlockSpec((None, 4, LANE, LANE), lambda g, i: (g, 0, 0, 0)),
                  pl.BlockSpec((4, LANE), lambda g, i: (0, g)), pl.BlockSpec((2, LANE), lambda g, i: (0, g))],
        out_specs=[blk] * 8,
        out_shape=[jax.ShapeDtypeStruct((tp, D_MODEL), F32)] * 8,
        compiler_params=_cparams("parallel", "parallel"),
    )(xc, wg, b4, lam)


def _lru_gates_bwd(l0, l1, da0, da1, r0, r1, i0, i1, a0, a1, xc, wg, lam, t_real, name):
    tp = xc.shape[0]
    tm = _tile(tp, 1408)

    rc = 32
    assert tm % rc == 0

    def fold(v):
        out = v[0:SUBLANE]
        for t in range(1, rc // SUBLANE):
            out = out + v[t * SUBLANE:(t + 1) * SUBLANE]
        return out

    def body(l0_ref, l1_ref, da0_ref, da1_ref, r0_ref, r1_ref, i0_ref, i1_ref, a0_ref, a1_ref, x_ref, w_ref, lam_ref,
             dx_ref, dw_ref, db_ref, dlam_ref, pre_s, dxp_s):
        i = pl.program_id(1)
        lam_ = lam_ref[...]
        sp = _softplus_neg(lam_)
        dsp_dlam = -_sigmoid(-lam_)

        def chunk(c, sums):
            r0 = pl.multiple_of(c * rc, rc)
            rows = pl.ds(r0, rc)
            x = x_ref[rows, :]
            live = _row_ids((rc, LANE), i * tm + r0) < t_real
            dxp = jnp.zeros_like(x)
            sums = list(sums)
            for d, (l_ref, da_ref, r_ref, i_ref, a_ref) in enumerate(((l0_ref, da0_ref, r0_ref, i0_ref, a0_ref),
                                                                      (l1_ref, da1_ref, r1_ref, i1_ref, a1_ref))):
                r, ig, a = r_ref[rows, :], i_ref[rows, :], a_ref[rows, :]
                du = jnp.where(live, l_ref[rows, :], 0.0)
                a2 = a * a
                rs = lax.rsqrt(1.0 - a2)
                dv = du * ((1.0 - a2) * rs)
                ds = du * (ig * x)
                dla = jnp.where(live, da_ref[rows, :] * a - ds * (a2 * rs), 0.0)
                dr = dla * (-LRU_C) * sp[d:d + 1, :]
                p_r = dr * r * (1.0 - r)
                p_i = dv * x * ig * (1.0 - ig)
                pre_s[d, rows, :] = p_r.astype(BF16)
                pre_s[2 + d, rows, :] = p_i.astype(BF16)
                dxp = dxp + dv * ig
                sums[d] = sums[d] + fold(p_r)
                sums[2 + d] = sums[2 + d] + fold(p_i)
                sums[4 + d] = sums[4 + d] + fold(dla * (-LRU_C) * r)
            dxp_s[rows, :] = dxp
            return tuple(sums)

        zero = jnp.zeros((SUBLANE, LANE), F32)
        sums = lax.fori_loop(0, tm // rc, chunk, (zero,) * 6)

        @pl.when(i == 0)
        def _():
            dw_ref[...] = jnp.zeros_like(dw_ref)
            db_ref[...] = jnp.zeros_like(db_ref)
            dlam_ref[...] = jnp.zeros_like(dlam_ref)

        xb = x_ref[...].astype(BF16)
        dx = dxp_s[...]
        for k in range(4):
            pk = pre_s[k]
            dx = dx + lax.dot_general(pk, w_ref[k], (((1,), (1,)), ((), ())), preferred_element_type=F32)
            dw_ref[k] += lax.dot_general(xb, pk, (((0,), (0,)), ((), ())), preferred_element_type=F32)
        db_ref[...] += jnp.concatenate([jnp.sum(sums[k], axis=0, keepdims=True) for k in range(4)], axis=0)
        dlam_ref[...] += jnp.concatenate([jnp.sum(sums[4 + d], axis=0, keepdims=True) * dsp_dlam[d:d + 1, :] for d in range(2)], axis=0)
        dx_ref[...] = dx

    blk = pl.BlockSpec((tm, LANE), lambda g, i: (i, g))
    return pl.pallas_call(
        body, name=name, grid=(LRU_BLOCKS, tp // tm),
        in_specs=[blk] * 11 + [pl.BlockSpec((None, 4, LANE, LANE), lambda g, i: (g, 0, 0, 0)),
                               pl.BlockSpec((2, LANE), lambda g, i: (0, g))],
        out_specs=[blk, pl.BlockSpec((None, 4, LANE, LANE), lambda g, i: (g, 0, 0, 0)),
                   pl.BlockSpec((4, LANE), lambda g, i: (0, g)), pl.BlockSpec((2, LANE), lambda g, i: (0, g))],
        out_shape=[jax.ShapeDtypeStruct((tp, D_MODEL), F32), jax.ShapeDtypeStruct((LRU_BLOCKS, 4, LANE, LANE), F32),
                   jax.ShapeDtypeStruct((4, D_MODEL), F32), jax.ShapeDtypeStruct((2, D_MODEL), F32)],
        scratch_shapes=[pltpu.VMEM((4, tm, LANE), BF16), pltpu.VMEM((tm, LANE), F32)],
        compiler_params=_cparams("parallel", "arbitrary"),
    )(l0, l1, da0, da1, r0, r1, i0, i1, a0, a1, xc, wg, lam)


SCAN_UNROLL = 4


def _loop_tiles(nt, step, carry):
    assert nt % SCAN_UNROLL == 0

    def trip(tt, c):
        for u in range(SCAN_UNROLL):
            c = step(tt * SCAN_UNROLL + u, c)
        return c

    return lax.fori_loop(0, nt // SCAN_UNROLL, trip, carry)


def _tile_scan(a, u, reverse):
    rows = lax.broadcasted_iota(jnp.int32, a.shape, 0)
    for s in (1, 2, 4):
        if reverse:
            keep = rows < SUBLANE - s
            a_sh, u_sh = pltpu.roll(a, SUBLANE - s, 0), pltpu.roll(u, SUBLANE - s, 0)
        else:
            keep = rows >= s
            a_sh, u_sh = pltpu.roll(a, s, 0), pltpu.roll(u, s, 0)
        u = u + a * jnp.where(keep, u_sh, 0.0)
        a = a * jnp.where(keep, a_sh, 1.0)
    return a, u


def _scan_fwd(a0, u0, a1, u1, proj, name):
    tp, d = a0.shape
    tc = 128
    nt = tp // SUBLANE

    def body(a0_ref, u0_ref, a1_ref, u1_ref, lg_ref, h0_ref, h1_ref, gh_ref):
        def step(t, carry):
            c0, c1 = carry
            f = pl.multiple_of(t * SUBLANE, SUBLANE)
            b = pl.multiple_of((nt - 1 - t) * SUBLANE, SUBLANE)
            pa, pu = _tile_scan(a0_ref[pl.ds(f, SUBLANE), :], u0_ref[pl.ds(f, SUBLANE), :], False)
            h = pu + pa * c0
            h0_ref[pl.ds(f, SUBLANE), :] = h
            c0 = h[SUBLANE - 1:SUBLANE, :]
            pa, pu = _tile_scan(a1_ref[pl.ds(b, SUBLANE), :], u1_ref[pl.ds(b, SUBLANE), :], True)
            h = pu + pa * c1
            h1_ref[pl.ds(b, SUBLANE), :] = h
            c1 = h[0:1, :]
            return c0, c1

        z = jnp.zeros((1, tc), F32)
        _loop_tiles(nt, step, (z, z))
        gh_ref[...] = (_gelu(lg_ref[...]) * (h0_ref[...] + h1_ref[...])).astype(BF16)

    blk = pl.BlockSpec((tp, tc), lambda j: (0, j))
    return pl.pallas_call(
        body, name=name, grid=(d // tc,),
        in_specs=[blk] * 4 + [pl.BlockSpec((tp, tc), lambda j: (0, j + C_LRU_G // tc))], out_specs=[blk] * 3,
        out_shape=[jax.ShapeDtypeStruct((tp, d), F32)] * 2 + [jax.ShapeDtypeStruct((tp, d), BF16)],
        compiler_params=_cparams("parallel"),
    )(a0, u0, a1, u1, proj)


def _scan_bwd(dh, a0, a1, h0, h1, name):
    tp, d = dh.shape
    tc = 128
    nt = tp // SUBLANE

    def body(dh_ref, a0_ref, a1_ref, h0_ref, h1_ref, l0_ref, l1_ref, da0_ref, da1_ref):
        rows8 = lax.broadcasted_iota(jnp.int32, (SUBLANE, tc), 0)

        def step(t, carry):
            c0, c1 = carry
            b = pl.multiple_of((nt - 1 - t) * SUBLANE, SUBLANE)
            f = pl.multiple_of(t * SUBLANE, SUBLANE)
            a = a0_ref[pl.ds(b, SUBLANE), :]
            a_next = jnp.where(rows8 < SUBLANE - 1, pltpu.roll(a, SUBLANE - 1, 0), 1.0)
            pa, pu = _tile_scan(a_next, dh_ref[pl.ds(b, SUBLANE), :], True)
            lam = pu + pa * c0
            l0_ref[pl.ds(b, SUBLANE), :] = lam
            c0 = a[0:1, :] * lam[0:1, :]
            a = a1_ref[pl.ds(f, SUBLANE), :]
            a_prev = jnp.where(rows8 >= 1, pltpu.roll(a, 1, 0), 1.0)
            pa, pu = _tile_scan(a_prev, dh_ref[pl.ds(f, SUBLANE), :], False)
            lam = pu + pa * c1
            l1_ref[pl.ds(f, SUBLANE), :] = lam
            c1 = a[SUBLANE - 1:SUBLANE, :] * lam[SUBLANE - 1:SUBLANE, :]
            return c0, c1

        z = jnp.zeros((1, tc), F32)
        _loop_tiles(nt, step, (z, z))
        rows = lax.broadcasted_iota(jnp.int32, (tp, tc), 0)
        da0_ref[...] = l0_ref[...] * jnp.where(rows >= 1, pltpu.roll(h0_ref[...], 1, 0), 0.0)
        da1_ref[...] = l1_ref[...] * jnp.where(rows < tp - 1, pltpu.roll(h1_ref[...], tp - 1, 0), 0.0)

    blk = pl.BlockSpec((tp, tc), lambda j: (0, j))
    return pl.pallas_call(
        body, name=name, grid=(d // tc,), in_specs=[blk] * 5, out_specs=[blk] * 4,
        out_shape=[jax.ShapeDtypeStruct((tp, d), F32)] * 4,
        compiler_params=_cparams("parallel"),
    )(dh, a0, a1, h0, h1)


def _gated_h_bwd(dgh, proj, h0, h1, dproj, name):
    def fn(row0, dg, lg, x0, x1):
        act, dact = _gelu_and_grad(lg)
        return dg * (x0 + x1) * dact, dg * act

    return _rows(fn, name, [Rw(dgh), Rw(proj, D_MODEL, C_LRU_G // D_MODEL), Rw(h0), Rw(h1)],
                 [Into(dproj, C_LRU_G, D_MODEL), (D_MODEL, F32)])


def _mix(proj, y_mla, y_lru, name):
    def fn(row0, gm, gl, ym, yl):
        return _sigmoid(gm) * ym + _sigmoid(gl) * yl

    return _rows(fn, name, [Rw(proj, D_MODEL, C_G_MLA // D_MODEL), Rw(proj, D_MODEL, C_G_LRU // D_MODEL), Rw(y_mla), Rw(y_lru)],
                 [(D_MODEL, BF16)])[0]


def _mix_bwd(dz, proj, y_mla, y_lru, dproj, name):
    def fn(row0, dzb, gm, gl, ym, yl):
        sm, sl = _sigmoid(gm), _sigmoid(gl)
        dg = jnp.concatenate([dzb * ym * sm * (1.0 - sm), dzb * yl * sl * (1.0 - sl)], axis=1)
        return dzb * sm, dzb * sl, dg

    return _rows(fn, name, [Rw(dz), Rw(proj, D_MODEL, C_G_MLA // D_MODEL), Rw(proj, D_MODEL, C_G_LRU // D_MODEL),
                            Rw(y_mla), Rw(y_lru)], [(D_MODEL, BF16), (D_MODEL, BF16), Into(dproj, C_G_MLA, 2 * D_MODEL)])


def _layer_fwd(h, w_in, more_weights, cs, t_real, tag):
    proj = _matmul(h, w_in, tag + "proj", tb=True)
    w = dict(more_weights(0, proj), w_in=w_in)
    cqn, ckvn = _mla_norms(proj, w['q_norm'], w['kv_norm'], tag + "mla_norms")
    qext = _matmul(cqn, w['w_q'], tag + "q_up")
    kv = _matmul(ckvn, w['w_kv'], tag + "kv_up")
    qc, kc, vb = _mla_pack(qext, kv, proj, cs, tag + "mla_pack")
    o, lse = _attn_fwd(qc, kc, vb, t_real, tag + "attn_fwd")
    w.update(more_weights(1, o))
    y_mla = _matmul(o, w['w_o_mla'], tag + "o_mla")
    xc = _lru_conv_fwd(proj, w['lru_conv_w'], w['lru_conv_b'], t_real, tag + "lru_conv")
    r0, r1, i0, i1, a0, a1, u0, u1 = _lru_gates_fwd(xc, w['w_g'], w['b4'], w['lru_lambda'], t_real, tag + "lru_gates")
    h0, h1, gh = _scan_fwd(a0, u0, a1, u1, proj, tag + "lru_scan")
    y_lru = _matmul(gh, w['w_o_lru'], tag + "o_lru")
    z = _mix(proj, y_mla, y_lru, tag + "mix")
    zo = _matmul(z, w['w_out'], tag + "w_out")
    hm = _ln_fwd([(DN_ALPHA, h), (1.0, zo)], w['ln1_g'], w['ln1_b'], tag + "ln1")
    w.update(more_weights(2, hm))
    up = _matmul(hm, w['w_up'], tag + "w_up", tb=True)
    m = _ffn_conv_act(up, w['ffn_conv_w'], w['ffn_conv_b'], t_real, tag + "ffn_conv")
    f = _matmul(m, w['w_down'], tag + "w_down", tk_cap=1408)
    out = _ln_fwd([(DN_ALPHA, hm), (1.0, f)], w['ln2_g'], w['ln2_b'], tag + "ln2")
    saved = dict(w=w, h=h, proj=proj, cqn=cqn, ckvn=ckvn, qc=qc, kc=kc, vb=vb, o=o, lse=lse, y_mla=y_mla, xc=xc,
                 r0=r0, r1=r1, i0=i0, i1=i1, a0=a0, a1=a1, h0=h0, h1=h1, gh=gh, y_lru=y_lru, z=z, zo=zo, hm=hm,
                 up=up, m=m, f=f)
    return out, saved


DW_MATMUL = dict(ta=True, out_dtype=BF16, tn_cap=1408, tk_cap=1408)


def _after(a, tok):
    return a if tok is None else a + tok.astype(a.dtype)


def _layer_bwd(dout_terms, s, cs, t_real, tag, emit, tok):
    w = s['w']
    g = {}
    du2, dg2, db2 = _ln_bwd(dout_terms, [(DN_ALPHA, s['hm']), (1.0, s['f'])], _after(w['ln2_g'], tok), tag + "ln2_bwd")
    g['ln2_g'], g['ln2_b'] = dg2, db2
    dm = _matmul(du2, w['w_down'], tag + "w_down_dx", tb=True)
    g['w_down'] = _matmul(s['m'], du2, tag + "w_down_dw", **DW_MATMUL)
    dup, dwg_, dwv_, dbg_, dbv_ = _ffn_conv_act_bwd(dm, s['up'], w['ffn_conv_w'], w['ffn_conv_b'], t_real, tag + "ffn_conv_bwd")
    g['ffn_conv_w'] = jnp.concatenate([dwg_, dwv_], axis=1)
    g['ffn_conv_b'] = jnp.concatenate([dbg_, dbv_], axis=1)
    dhm_mm = _matmul(dup, w['w_up'], tag + "w_up_dx", tk_cap=1408)
    g['w_up'] = _matmul(s['hm'], dup, tag + "w_up_dw", **DW_MATMUL)
    tok = emit('ffn', g)
    g = {}
    du1, dg1, db1 = _ln_bwd([(DN_ALPHA, du2), (1.0, dhm_mm)], [(DN_ALPHA, s['h']), (1.0, s['zo'])],
                            _after(w['ln1_g'], tok), tag + "ln1_bwd")
    g['ln1_g'], g['ln1_b'] = dg1, db1
    dz = _matmul(du1, w['w_out'], tag + "w_out_dx", tb=True)
    g['w_out'] = _matmul(s['z'], du1, tag + "w_out_dw", **DW_MATMUL)
    dproj = lax.empty(s['proj'].shape, BF16)
    dy_mla, dy_lru, dproj = _mix_bwd(dz, s['proj'], s['y_mla'], s['y_lru'], dproj, tag + "mix_bwd")
    do = _matmul(dy_mla, w['w_o_mla'], tag + "o_mla_dx", tb=True)
    g['w_o_mla'] = _matmul(s['o'], dy_mla, tag + "o_mla_dw", **DW_MATMUL)
    dqext, dkc, dv = _attn_bwd(s['qc'], s['kc'], s['vb'], do, s['o'], s['lse'], cs, t_real, tag + "attn_bwd")
    dkv, dkrp = _mla_unpack(dkc, dv, cs, tag + "mla_unpack")
    dcqn = _matmul(dqext, w['w_q'], tag + "q_up_dx", tb=True)
    g['w_q'] = _matmul(s['cqn'], dqext, tag + "q_up_dw", **DW_MATMUL)
    dckvn = _matmul(dkv, w['w_kv'], tag + "kv_up_dx", tb=True)
    g['w_kv'] = _matmul(s['ckvn'], dkv, tag + "kv_up_dw", **DW_MATMUL)
    dgh = _matmul(dy_lru, w['w_o_lru'], tag + "o_lru_dx", tb=True)
    g['w_o_lru'] = _matmul(s['gh'], dy_lru, tag + "o_lru_dw", **DW_MATMUL)
    dproj, dhs = _gated_h_bwd(dgh, s['proj'], s['h0'], s['h1'], dproj, tag + "lru_gate_out_bwd")
    l0, l1, da0, da1 = _scan_bwd(dhs, s['a0'], s['a1'], s['h0'], s['h1'], tag + "lru_scan_bwd")
    dxc, g['w_g'], g['b4'], g['lru_lambda'] = _lru_gates_bwd(
        l0, l1, da0, da1, s['r0'], s['r1'], s['i0'], s['i1'], s['a0'], s['a1'], s['xc'], w['w_g'], w['lru_lambda'],
        t_real, tag + "lru_gates_bwd")
    dproj, g['lru_conv_w'], g['lru_conv_b'] = _lru_conv_bwd(dxc, s['proj'], w['lru_conv_w'], dproj, t_real, tag + "lru_conv_bwd")
    tok = emit('mid', g)
    dproj, dqn, dkvn = _mla_norms_bwd(dcqn, dckvn, _after(dkrp, tok), s['proj'], w['q_norm'], w['kv_norm'], dproj,
                                      tag + "mla_norms_bwd")
    tok = emit('in', {'w_in': _matmul(s['h'], dproj, tag + "proj_dw", **DW_MATMUL), 'q_norm': dqn, 'kv_norm': dkvn})
    dh_mm = _matmul(dproj, w['w_in'], tag + "proj_dx", tk_cap=1536)
    return [(DN_ALPHA, du1), (1.0, dh_mm)], tok


def _swap_halves(a, axis=-1):
    h1, h2 = jnp.split(a, 2, axis=axis)
    return jnp.concatenate([h2, h1], axis=axis)


def _w_in_kernel(w_in_t):
    cq, ckv, kr, lg, lx, gm, gl = jnp.split(w_in_t, [256, 384, 448, 1472, 2496, 3520], axis=0)
    return jnp.concatenate([lg, lx, gm, gl, cq, ckv, kr, _swap_halves(kr, axis=0)], axis=0)


def _layer_weights(fl):
    w = {}
    if 'w_uq' in fl:
        uq = fl['w_uq']
        w['w_q'] = jnp.concatenate([uq, _swap_halves(uq[..., QK_NOPE:])], axis=-1).reshape(Q_RANK, HEADS * 2 * LANE)
        w['w_kv'] = jnp.concatenate([fl['w_uk'].reshape(KV_RANK, -1), fl['w_uv'].reshape(KV_RANK, -1)], axis=1).astype(BF16)
        w['w_g'] = jnp.moveaxis(jnp.concatenate([fl['w_rg'], fl['w_ig']], axis=0), 0, 1).astype(BF16)
        w['b4'] = jnp.concatenate([fl['b_rg'], fl['b_ig']], axis=0)
    for n in ('q_norm', 'kv_norm', 'w_o_mla', 'lru_conv_w', 'lru_conv_b', 'lru_lambda', 'w_o_lru', 'w_out', 'ln1_g',
              'ln1_b', 'w_up', 'ffn_conv_w', 'ffn_conv_b', 'w_down', 'ln2_g', 'ln2_b'):
        if n in fl:
            w[n] = fl[n]
    return w


def _layer_grads(g):
    out = {}
    if 'w_in' in g:
        lg, lx, gm, gl, cq, ckv, kr, krs = jnp.split(g['w_in'], [1024, 2048, 3072, 4096, 4352, 4480, 4544], axis=1)
        out['w_in'] = jnp.concatenate([cq, ckv, kr + _swap_halves(krs), lg, lx, gm, gl], axis=1)
    if 'w_q' in g:
        gq = g['w_q'].reshape(Q_RANK, HEADS, 2 * LANE)
        out['w_uq'] = jnp.concatenate([gq[..., :QK_NOPE], gq[..., QK_NOPE:QK_NOPE + QK_ROPE] + _swap_halves(gq[..., QK_NOPE + QK_ROPE:])], axis=-1)
    if 'w_kv' in g:
        out['w_uk'] = g['w_kv'][:, :HEADS * QK_NOPE].reshape(KV_RANK, HEADS, QK_NOPE)
        out['w_uv'] = g['w_kv'][:, HEADS * QK_NOPE:].reshape(KV_RANK, HEADS, V_HEAD)
    if 'w_g' in g:
        gg = jnp.moveaxis(g['w_g'], 1, 0)
        out['w_rg'], out['w_ig'] = gg[:2], gg[2:]
    if 'b4' in g:
        out['b_rg'], out['b_ig'] = g['b4'][:2], g['b4'][2:]
    for n in ('q_norm', 'kv_norm', 'lru_conv_b', 'ln1_g', 'ln1_b', 'ffn_conv_b', 'ln2_g', 'ln2_b'):
        if n in g:
            out[n] = g[n].reshape(-1)
    for n in ('w_o_mla', 'lru_conv_w', 'lru_lambda', 'w_o_lru', 'w_out', 'w_up', 'ffn_conv_w', 'w_down'):
        if n in g:
            out[n] = g[n]
    return out


def _rope_table(tp):
    half = QK_ROPE // 2
    inv_freq = jnp.exp(-math.log(ROPE_THETA) * jnp.arange(half, dtype=F32) / half)
    ang = jnp.arange(tp, dtype=F32)[:, None] * inv_freq[None, :]
    c, s = jnp.cos(ang), jnp.sin(ang)
    return jnp.concatenate([c, c, -s, s], axis=1)


def _local_step(x, target, meta, ln0_g, ln0_b, layer_w, t_pad, emit):
    seq = x.shape[0]
    t_real = N_META + seq
    zpad = jnp.zeros((t_pad - t_real, D_MODEL), F32)
    xin = jnp.concatenate([meta, x, zpad], axis=0)
    tgt = jnp.concatenate([jnp.zeros((N_META, D_MODEL), F32), target, zpad], axis=0)
    cs = _rope_table(t_pad)
    h = _ln_fwd([(1.0, xin)], ln0_g, ln0_b, "ln0")
    saved = []
    for l in range(DEPTH):
        w_in, rest_of_weights = layer_w[l](h)
        h, s = _layer_fwd(h, w_in, rest_of_weights, cs, t_real, "l%d_" % l)
        saved.append(s)
    dy, lossvec = _loss_head(h, tgt, t_real, "loss_head")
    terms, tok = [(1.0, dy)], None
    for l in reversed(range(DEPTH)):
        terms, tok = _layer_bwd(terms, saved[l], cs, t_real, "l%d_" % l,
                                functools.partial(lambda stage, g, l: emit(l, stage, _layer_grads(g)), l=l), tok)
    dxin, dg0, db0 = _ln_bwd(terms, [(1.0, xin)], _after(ln0_g, tok), "ln0_bwd")
    emit(None, 'head', {'meta_tokens': dxin[:N_META], 'ln0_g': dg0.reshape(-1), 'ln0_b': db0.reshape(-1), 'loss': lossvec})
    return dxin[N_META:t_real]


_HBM = pl.BlockSpec(memory_space=pltpu.HBM)
_SEM = pl.BlockSpec(memory_space=pltpu.SEMAPHORE)
_SIDE_EFFECT = pltpu.SideEffectType.DATAFLOW_SIDE_EFFECTING


def _peer_copies(src_refs, land_refs, scatters, send_sems, recv_sems):
    x, y, c = lax.axis_index("x"), lax.axis_index("y"), lax.axis_index("c")
    me = 4 * x + 2 * y + c
    copies = []
    for k in range(1, N_DEV):
        px = 1 - x if k & 4 else x
        py = 1 - y if k & 2 else y
        pc = 1 - c if k & 1 else c
        for t, (src, land) in enumerate(zip(src_refs, land_refs)):
            copies.append(pltpu.make_async_remote_copy(
                src_ref=src.at[4 * px + 2 * py + pc] if scatters[t] else src, dst_ref=land.at[me],
                send_sem=send_sems.at[7 * t + k - 1], recv_sem=recv_sems.at[7 * t + k - 1],
                device_id=(px, py, pc), device_id_type=pl.DeviceIdType.MESH))
    return me, copies


def _own_block_in_place(land, own):
    me = 4 * lax.axis_index("x") + 2 * lax.axis_index("y") + lax.axis_index("c")
    return lax.dynamic_update_slice_in_dim(land, own, me, 0)


def _gather_two_level(shards, name):
    nt = len(shards)

    def body(*refs):
        x_refs, out_refs = refs[:nt], refs[nt:2 * nt]
        token_ref, send_sems, recv_sems = refs[2 * nt:]
        x, y, c = lax.axis_index("x"), lax.axis_index("y"), lax.axis_index("c")
        me, sibling = (x, y, c), (x, y, 1 - c)
        chips = [(1 - x, y), (x, 1 - y), (1 - x, 1 - y)]

        def copy(t, k, block, to, own=False):
            px, py, pc = block
            slot = out_refs[t].at[4 * px + 2 * py + pc]
            return pltpu.make_async_remote_copy(
                src_ref=x_refs[t] if own else slot, dst_ref=slot,
                send_sem=send_sems.at[7 * t + k], recv_sem=recv_sems.at[7 * t + k],
                device_id=to, device_id_type=pl.DeviceIdType.MESH)

        sent = []
        for t in range(nt):
            first = [copy(t, 1 + j, me, (*chip, c), own=True) for j, chip in enumerate(chips)]
            first.append(copy(t, 0, me, sibling, own=True))
            for cp in first:
                cp.start()
            sent += first
        token_ref[...] = jnp.zeros_like(token_ref)
        for j, chip in enumerate(chips):
            for t in range(nt):
                copy(t, 1 + j, (*chip, c), me).wait_recv()
                passed = copy(t, 4 + j, (*chip, c), sibling)
                passed.start()
                sent.append(passed)
        for t in range(nt):
            copy(t, 0, sibling, me).wait_recv()
            for j, chip in enumerate(chips):
                copy(t, 4 + j, (*chip, 1 - c), me).wait_recv()
        for cp in sent:
            cp.wait_send()

    any_space = pl.BlockSpec(memory_space=pl.ANY)
    res = pl.pallas_call(
        body, name=name,
        out_shape=[jax.ShapeDtypeStruct((N_DEV,) + a.shape, a.dtype) for a in shards] + [jax.ShapeDtypeStruct((SUBLANE, LANE), F32)],
        in_specs=[any_space] * nt, out_specs=[any_space] * nt + [pl.BlockSpec(memory_space=pltpu.VMEM)],
        scratch_shapes=[pltpu.SemaphoreType.DMA((7 * nt,)), pltpu.SemaphoreType.DMA((7 * nt,))],
    )(*shards)
    return [_own_block_in_place(land, a[None]) for land, a in zip(res[:nt], shards)], res[nt][0, 0]


def _exchange_start(groups, name):
    flat = [it for grp in groups for it in grp]
    nt, ng = len(flat), len(groups)
    scatters = [sc for _, sc in flat]
    srcs = [pltpu.with_memory_space_constraint(a, pltpu.HBM) for a, _ in flat]
    land_shapes = [a.shape if sc else (N_DEV,) + a.shape for a, sc in flat]
    lands = [pltpu.with_memory_space_constraint(lax.empty(s, a.dtype), pltpu.HBM) for s, (a, _) in zip(land_shapes, flat)]
    bounds = [0]
    for grp in groups:
        bounds.append(bounds[-1] + len(grp))

    def body(*refs):
        src_refs, land_refs = refs[:nt], refs[nt:2 * nt]
        sem_refs = refs[2 * nt:2 * nt + 2 * ng]
        token_ref = refs[4 * nt + 2 * ng]
        for gi in range(ng):
            lo, hi = bounds[gi], bounds[gi + 1]
            _, copies = _peer_copies(src_refs[lo:hi], land_refs[lo:hi], scatters[lo:hi], sem_refs[2 * gi], sem_refs[2 * gi + 1])
            for cp in copies:
                cp.start()
        token_ref[...] = jnp.zeros_like(token_ref)

    out_shape = []
    for grp in groups:
        out_shape += [pltpu.SemaphoreType.DMA((7 * len(grp),)), pltpu.SemaphoreType.DMA((7 * len(grp),))]
    out_shape += [pltpu.HBM(a.shape, a.dtype) for a in srcs] + [pltpu.HBM(s, a.dtype) for s, a in zip(land_shapes, srcs)]
    out_shape += [jax.ShapeDtypeStruct((SUBLANE, LANE), F32)]
    res = pl.pallas_call(
        body, name=name, out_shape=out_shape,
        in_specs=[_HBM] * (2 * nt),
        out_specs=[_SEM] * (2 * ng) + [_HBM] * (2 * nt) + [pl.BlockSpec(memory_space=pltpu.VMEM)],
        input_output_aliases={t: 2 * ng + t for t in range(2 * nt)},
        compiler_params=pltpu.CompilerParams(has_side_effects=_SIDE_EFFECT),
    )(*srcs, *lands)
    sems, thru, token = res[:2 * ng], res[2 * ng:2 * ng + 2 * nt], res[-1]
    states = []
    for gi in range(ng):
        lo, hi = bounds[gi], bounds[gi + 1]
        states.append((sems[2 * gi], sems[2 * gi + 1], thru[lo:hi], thru[nt + lo:nt + hi], scatters[lo:hi]))
    return states, token[0, 0]


def _exchange_wait(state, after, name):
    send_sems, recv_sems, srcs, lands, scatters = state
    n = len(srcs)

    def body(*refs):
        _, copies = _peer_copies(refs[:n], refs[n:2 * n], scatters, refs[2 * n], refs[2 * n + 1])
        for cp in copies:
            cp.wait_send()
        for cp in copies:
            cp.wait_recv()

    res = pl.pallas_call(
        body, name=name,
        out_shape=[pltpu.HBM(a.shape, a.dtype) for a in srcs] + [pltpu.HBM(a.shape, a.dtype) for a in lands],
        in_specs=[_HBM] * (2 * n) + [_SEM, _SEM, _HBM],
        out_specs=[_HBM] * (2 * n),
        input_output_aliases={t: t for t in range(2 * n)},
        compiler_params=pltpu.CompilerParams(has_side_effects=_SIDE_EFFECT),
    )(*srcs, *lands, send_sems, recv_sems, pltpu.with_memory_space_constraint(after, pltpu.HBM))
    me = 4 * lax.axis_index("x") + 2 * lax.axis_index("y") + lax.axis_index("c")
    out = []
    for src, land, sc in zip(res[:n], res[n:], scatters):
        own = lax.dynamic_index_in_dim(src, me, 0, keepdims=True) if sc else src[None]
        out.append(lax.dynamic_update_slice_in_dim(land, own, me, 0))
    return out


def _as_rows(shape):
    return (1, shape[0]) if len(shape) == 1 else (math.prod(shape[:-1]), shape[-1])


def _sum_adamw(pieces, w, m, v, name):
    shape = w.shape
    nl = len(pieces)
    if nl > 1 and _as_rows(shape[1:])[0] % 16:
        pieces, nl = [jnp.stack(pieces, axis=1)], 1
    rows, cols = _as_rows(shape)
    rl = rows // nl
    cap = max(16, (1 << 18) // cols // 16 * 16)
    tr = _tile(rl, cap, 16)
    nb = rl // tr
    c1 = 1.0 / (1.0 - ADAM_B1 ** ADAM_STEP)
    c2 = 1.0 / (1.0 - ADAM_B2 ** ADAM_STEP)

    def body(*refs):
        p_refs = refs[:nl]
        w_ref, m_ref, v_ref, g_ref, d_ref, nm_ref, nv_ref = refs[nl:]
        li = pl.program_id(0)

        def total(p_ref):
            acc = p_ref[0].astype(F32)
            for k in range(1, N_DEV):
                acc = acc + p_ref[k].astype(F32)
            return acc

        gg = total(p_refs[0])
        for l in range(1, nl):
            gg = jnp.where(li == l, total(p_refs[l]), gg)
        nm = ADAM_B1 * m_ref[...] + (1.0 - ADAM_B1) * gg
        nv = ADAM_B2 * v_ref[...] + (1.0 - ADAM_B2) * (gg * gg)
        g_ref[...] = gg
        d_ref[...] = -ADAM_LR * ((nm * c1) / (jnp.sqrt(nv * c2) + ADAM_EPS) + ADAM_WD * w_ref[...])
        nm_ref[...] = nm
        nv_ref[...] = nv

    blk = pl.BlockSpec((tr, cols), lambda li, i: (li * nb + i, 0))
    p_specs = [pl.BlockSpec((N_DEV, tr, cols), functools.partial(lambda li, i, l: (0, jnp.where(li == l, i, 0), 0), l=l))
               for l in range(nl)]
    res = pl.pallas_call(
        body, name=name, grid=(nl, nb),
        in_specs=p_specs + [blk] * 3, out_specs=[blk] * 4,
        out_shape=[jax.ShapeDtypeStruct((rows, cols), F32)] * 4,
        compiler_params=_cparams("parallel", "parallel"),
    )(*[p.reshape(N_DEV, rl, cols) for p in pieces], *[a.reshape(rows, cols) for a in (w, m, v)])
    return [r.reshape(shape) for r in res]


def _to_shards(full, axis):
    shp = full.shape
    a = full.reshape(shp[:axis] + (N_DEV, shp[axis] // N_DEV) + shp[axis + 1:])
    return jnp.moveaxis(a, axis, 0)


def _from_shards(blocks, axis):
    a = jnp.moveaxis(blocks, 0, axis)
    shp = a.shape
    return a.reshape(shp[:axis] + (shp[axis] * shp[axis + 1],) + shp[axis + 2:])


def kernel(x, meta_tokens, ln0_g, ln0_b, w_in, q_norm, kv_norm, w_uq, w_uk, w_uv, w_o_mla, lru_conv_w, lru_conv_b, w_rg, b_rg, w_ig, b_ig, lru_lambda, w_o_lru, w_out, ln1_g, ln1_b, w_up, ffn_conv_w, ffn_conv_b, w_down, ln2_g, ln2_b, loss_target, m_meta_tokens, m_ln0_g, m_ln0_b, m_w_in, m_q_norm, m_kv_norm, m_w_uq, m_w_uk, m_w_uv, m_w_o_mla, m_lru_conv_w, m_lru_conv_b, m_w_rg, m_b_rg, m_w_ig, m_b_ig, m_lru_lambda, m_w_o_lru, m_w_out, m_ln1_g, m_ln1_b, m_w_up, m_ffn_conv_w, m_ffn_conv_b, m_w_down, m_ln2_g, m_ln2_b, v_meta_tokens, v_ln0_g, v_ln0_b, v_w_in, v_q_norm, v_kv_norm, v_w_uq, v_w_uk, v_w_uv, v_w_o_mla, v_lru_conv_w, v_lru_conv_b, v_w_rg, v_b_rg, v_w_ig, v_b_ig, v_lru_lambda, v_w_o_lru, v_w_out, v_ln1_g, v_ln1_b, v_w_up, v_ffn_conv_w, v_ffn_conv_b, v_w_down, v_ln2_g, v_ln2_b):
    args = (meta_tokens, ln0_g, ln0_b, w_in, q_norm, kv_norm, w_uq, w_uk, w_uv, w_o_mla, lru_conv_w, lru_conv_b, w_rg, b_rg, w_ig, b_ig, lru_lambda, w_o_lru, w_out, ln1_g, ln1_b, w_up, ffn_conv_w, ffn_conv_b, w_down, ln2_g, ln2_b)
    ms = (m_meta_tokens, m_ln0_g, m_ln0_b, m_w_in, m_q_norm, m_kv_norm, m_w_uq, m_w_uk, m_w_uv, m_w_o_mla, m_lru_conv_w, m_lru_conv_b, m_w_rg, m_b_rg, m_w_ig, m_b_ig, m_lru_lambda, m_w_o_lru, m_w_out, m_ln1_g, m_ln1_b, m_w_up, m_ffn_conv_w, m_ffn_conv_b, m_w_down, m_ln2_g, m_ln2_b)
    vs = (v_meta_tokens, v_ln0_g, v_ln0_b, v_w_in, v_q_norm, v_kv_norm, v_w_uq, v_w_uk, v_w_uv, v_w_o_mla, v_lru_conv_w, v_lru_conv_b, v_w_rg, v_b_rg, v_w_ig, v_b_ig, v_lru_lambda, v_w_o_lru, v_w_out, v_ln1_g, v_ln1_b, v_w_up, v_ffn_conv_w, v_ffn_conv_b, v_w_down, v_ln2_g, v_ln2_b)
    wd, md, vd = dict(zip(WEIGHTS, args)), dict(zip(WEIGHTS, ms)), dict(zip(WEIGHTS, vs))

    def shard_axis(n, l):
        return SHARD_AXIS[n] - (0 if l is None else 1)

    def shard(n, l):
        a = wd[n] if l is None else wd[n][l]
        if n in SENT_TRANSPOSED:
            a = a.T
        return a.astype(BF16) if n in BIG else a

    def whole(keys, landed):
        return {k: b.reshape(-1, b.shape[-1]) if k[0] in SENT_TRANSPOSED else _from_shards(b, shard_axis(*k))
                for k, b in zip(keys, landed)}

    first = [('meta_tokens', None), ('w_in', 0)]
    landed, token = _gather_two_level([shard(*k) for k in first], "gather_first")
    got_first = whole(first, landed)
    staged = [[(n, 0) for n in names] for names in STAGE_WEIGHTS]
    later = [(n, 1) for n in SHARDED if n != 'meta_tokens']
    gather, token = _exchange_start([[(_after(shard(*k), token), False) for k in keys] for keys in staged + [later]], "gather_start")

    def arrive(gi, keys, after, name):
        return whole(keys, _exchange_wait(gather[gi], after, name))

    def layer_weights(got, l, names):
        fl = {n: wd[n][l] for n in names if n in REPLICATED}
        fl.update({n: a for (n, _), a in got.items() if n in names})
        return _layer_weights(fl)

    ln0_g = _after(wd['ln0_g'], token)

    def first_layer(h):
        def more(stage, after):
            got = arrive(stage, staged[stage], after, "gather_wait_l0_%d" % stage)
            return layer_weights(got, 0, STAGE_WEIGHTS[stage] + STAGE_REPLICATED[stage])
        return _w_in_kernel(got_first['w_in', 0]), more

    def second_layer(h):
        got = arrive(len(staged), later, h, "gather_wait_l1")
        return _w_in_kernel(got['w_in', 1]), lambda stage, after: layer_weights(got, 1, STAGE_WEIGHTS[stage] + STAGE_REPLICATED[stage])

    sent = []
    pending = []

    def send(l, stage, grads):
        for n, g in grads.items():
            if n in SHARD_AXIS:
                g = _to_shards(g, shard_axis(n, l))
                pending.append(((n, l), (g.astype(BF16) if n in BIG else g, True)))
            else:
                pending.append(((n, l), (g.astype(BF16) if n in LARGE_REPLICATED else g, False)))
        if l == DEPTH - 1 and stage != 'in':
            return None
        (state,), tok = _exchange_start([[it for _, it in pending]], "grads_start_%s_%s" % (l, stage))
        sent.append(([k for k, _ in pending], state))
        pending.clear()
        return tok

    seq = x.shape[1]
    t_pad = -(-(N_META + seq + MIN_PAD_ROWS) // LANE) * LANE
    grad_x = _local_step(x[0], loss_target[0], got_first['meta_tokens', None], ln0_g, wd['ln0_b'],
                         [first_layer, second_layer], t_pad, send)

    pieces = {}
    for gi, (keys, state) in enumerate(sent):
        pieces.update(zip(keys, _exchange_wait(state, grad_x, "grads_wait_%d" % gi)))
    loss = jnp.sum(pieces['loss', None])
    outs = {}
    for n in WEIGHTS:
        ps = [pieces[n, None]] if (n, None) in pieces else [pieces[n, l] for l in range(DEPTH)]
        outs[n] = _sum_adamw(ps, wd[n], md[n], vd[n], "adamw_" + n)
    res = [loss, grad_x[None]]
    for k in range(4):
        res += [outs[n][k] for n in WEIGHTS]
    return tuple(res)
```

```python
import functools
import math

import jax
import jax.numpy as jnp
from jax import lax
from jax.experimental import pallas as pl
from jax.experimental.pallas import tpu as pltpu

F32 = jnp.float32
BF16 = jnp.bfloat16

N_DEV = 8
D_MODEL = 1024
N_META = 16
HEADS = 8
QK_NOPE = 128
QK_ROPE = 64
V_HEAD = 128
Q_RANK = 256
KV_RANK = 128
ROPE_THETA = 10000.0
LRU_BLOCKS = 8
LRU_C = 8.0
D_FF = 2816
DEPTH = 2
DN_ALPHA = (2.0 * DEPTH) ** 0.25
LN_EPS = 1e-5
RMS_EPS = 1e-6
LN2 = math.log(2.0)
ATT_SCALE = 1.0 / math.sqrt(QK_NOPE + QK_ROPE) / LN2
NEG_BIG = -1e30

ADAM_LR = 0.001
ADAM_B1 = 0.9
ADAM_B2 = 0.999
ADAM_EPS = 1e-08
ADAM_WD = 0.01
ADAM_STEP = 10

MIN_PAD_ROWS = 2
LANE = 128
SUBLANE = 8
VMEM_LIMIT = 56 * 1024 * 1024

PROJ_COLS = 4 * D_MODEL + Q_RANK + KV_RANK + 2 * QK_ROPE
C_LRU_G, C_LRU_X, C_G_MLA, C_G_LRU = 0, D_MODEL, 2 * D_MODEL, 3 * D_MODEL
C_CQ = 4 * D_MODEL
C_CKV = C_CQ + Q_RANK
C_KRP = C_CKV + KV_RANK

WEIGHTS = ['meta_tokens', 'ln0_g', 'ln0_b', 'w_in', 'q_norm', 'kv_norm', 'w_uq', 'w_uk', 'w_uv', 'w_o_mla',
           'lru_conv_w', 'lru_conv_b', 'w_rg', 'b_rg', 'w_ig', 'b_ig', 'lru_lambda', 'w_o_lru', 'w_out',
           'ln1_g', 'ln1_b', 'w_up', 'ffn_conv_w', 'ffn_conv_b', 'w_down', 'ln2_g', 'ln2_b']
SHARD_AXIS = {'meta_tokens': 1, 'w_in': 2, 'w_uq': 1, 'w_o_mla': 1, 'lru_conv_w': 2, 'b_rg': 2, 'b_ig': 2,
              'lru_lambda': 2, 'w_o_lru': 1, 'w_out': 1, 'w_up': 2, 'ffn_conv_w': 2, 'w_down': 1}
BIG = ['w_in', 'w_uq', 'w_o_mla', 'w_o_lru', 'w_out', 'w_up', 'w_down']
SHARDED = [n for n in WEIGHTS if n in SHARD_AXIS]
REPLICATED = [n for n in WEIGHTS if n not in SHARD_AXIS]
LARGE_REPLICATED = ['w_uk', 'w_uv', 'w_rg', 'w_ig']
SENT_TRANSPOSED = ['w_in', 'w_up']
STAGE_WEIGHTS = [['w_uq', 'lru_conv_w', 'b_rg', 'b_ig', 'lru_lambda'], ['w_o_mla', 'w_o_lru', 'w_out'], ['w_up', 'ffn_conv_w', 'w_down']]
STAGE_REPLICATED = [['q_norm', 'kv_norm', 'w_uk', 'w_uv', 'lru_conv_b', 'w_rg', 'w_ig'], ['ln1_g', 'ln1_b'], ['ffn_conv_b', 'ln2_g', 'ln2_b']]


def _cparams(*sem):
    return pltpu.CompilerParams(dimension_semantics=sem, vmem_limit_bytes=VMEM_LIMIT)


def _tile(n, cap, unit=LANE):
    best = None
    t = unit
    while t <= min(n, cap):
        if n % t == 0:
            best = t
        t += unit
    return n if best is None else best


def _sigmoid(x):
    return 1.0 / (1.0 + jnp.exp(-x))


_GELU_C = math.sqrt(2.0 / math.pi)


_GELU_A = 0.044715


def _gelu(x):
    t = jnp.tanh(x * (_GELU_C + (_GELU_C * _GELU_A) * (x * x)))
    hx = 0.5 * x
    return hx + hx * t


def _gelu_and_grad(x):
    x2 = x * x
    t = jnp.tanh(x * (_GELU_C + (_GELU_C * _GELU_A) * x2))
    hx = 0.5 * x
    dg = 0.5 + 0.5 * t + (hx * (1.0 - t * t)) * (_GELU_C + (3.0 * _GELU_C * _GELU_A) * x2)
    return hx + hx * t, dg


def _softplus_neg(lam):
    z = jnp.exp(-jnp.abs(lam))
    w = 1.0 + z
    log1p = jnp.where(w == 1.0, z, jnp.log(w) * z / (w - 1.0))
    return jnp.maximum(-lam, 0.0) + log1p


def _row_ids(shape, row0=0):
    return lax.broadcasted_iota(jnp.int32, shape, 0) + row0


def _matmul(a, b, name, ta=False, tb=False, out_dtype=F32, tm_cap=1408, tn_cap=1024, tk_cap=2048):
    if ta:
        kdim, m = a.shape
    else:
        m, kdim = a.shape
    if tb:
        n, k2 = b.shape
    else:
        k2, n = b.shape
    assert kdim == k2, (a.shape, b.shape, ta, tb)
    tm, tn, tk = _tile(m, tm_cap), _tile(n, tn_cap), _tile(kdim, tk_cap)
    nk = kdim // tk

    def body(a_ref, b_ref, o_ref, *acc):
        dn = (((0 if ta else 1,), (1 if tb else 0,)), ((), ()))
        part = lax.dot_general(a_ref[...].astype(BF16), b_ref[...].astype(BF16), dn, preferred_element_type=F32)
        if nk == 1:
            o_ref[...] = part.astype(o_ref.dtype)
            return
        acc_ref, k = acc[0], pl.program_id(2)

        @pl.when(k == 0)
        def _():
            acc_ref[...] = part

        @pl.when(k > 0)
        def _():
            acc_ref[...] += part

        @pl.when(k == nk - 1)
        def _():
            o_ref[...] = acc_ref[...].astype(o_ref.dtype)

    a_spec = pl.BlockSpec((tk, tm), lambda i, j, k: (k, i)) if ta else pl.BlockSpec((tm, tk), lambda i, j, k: (i, k))
    b_spec = pl.BlockSpec((tn, tk), lambda i, j, k: (j, k)) if tb else pl.BlockSpec((tk, tn), lambda i, j, k: (k, j))
    return pl.pallas_call(
        body, name=name,
        grid=(m // tm, n // tn, nk),
        in_specs=[a_spec, b_spec],
        out_specs=pl.BlockSpec((tm, tn), lambda i, j, k: (i, j)),
        out_shape=jax.ShapeDtypeStruct((m, n), out_dtype),
        scratch_shapes=[pltpu.VMEM((tm, tn), F32)] if nk > 1 else [],
        compiler_params=_cparams("parallel", "parallel", "arbitrary"),
    )(a, b)


class Rw:
    def __init__(self, arr, width=None, cb=0):
        self.arr, self.width, self.cb = arr, (arr.shape[1] if width is None else width), cb


class Pm:
    def __init__(self, arr):
        self.arr = arr


class Into:
    def __init__(self, arr, col0, width):
        self.arr, self.col0, self.width = arr, col0, width


def _call_with_into(body, name, grid, in_specs, operands, outs, spec_of, shape_of, extra_out_specs, extra_out_shape, sem):
    intos = [(k, o) for k, o in enumerate(outs) if isinstance(o, Into)]
    aliases = {len(operands) + n: k for n, (k, _) in enumerate(intos)}
    return pl.pallas_call(
        body, name=name, grid=grid,
        in_specs=in_specs + [pl.BlockSpec(memory_space=pl.ANY)] * len(intos),
        out_specs=[spec_of(o) for o in outs] + extra_out_specs,
        out_shape=[jax.ShapeDtypeStruct(o.arr.shape, o.arr.dtype) if isinstance(o, Into) else shape_of(o) for o in outs]
        + extra_out_shape,
        input_output_aliases=aliases,
        compiler_params=_cparams(sem),
    )(*operands, *[o.arr for _, o in intos])


def _rows(fn, name, ins, outs, accs=(), tm_cap=384):
    tp = next(o.arr.shape[0] for o in ins if isinstance(o, Rw))
    tm = _tile(tp, tm_cap)
    n_in, n_out, n_acc = len(ins), len(outs), len(accs)
    n_into = sum(isinstance(o, Into) for o in outs)

    def body(*refs):
        i = pl.program_id(0)
        res = fn(i * tm, *[r[...] for r in refs[:n_in]])
        if not isinstance(res, (tuple, list)):
            res = (res,)
        assert len(res) == n_out + n_acc, (name, len(res))
        out_refs = refs[n_in + n_into:]
        for k in range(n_out):
            out_refs[k][...] = res[k].astype(out_refs[k].dtype)
        for k in range(n_acc):
            ref = out_refs[n_out + k]

            @pl.when(i == 0)
            def _():
                ref[...] = jnp.zeros_like(ref)

            ref[...] += res[n_out + k]

    in_specs = []
    for o in ins:
        if isinstance(o, Rw):
            in_specs.append(pl.BlockSpec((tm, o.width), functools.partial(lambda i, cb: (i, cb), cb=o.cb)))
        else:
            in_specs.append(pl.BlockSpec(o.arr.shape, functools.partial(lambda i, nd: (0,) * nd, nd=o.arr.ndim)))

    def spec_of(o):
        if isinstance(o, Into):
            assert o.col0 % o.width == 0, (name, o.col0, o.width)
            return pl.BlockSpec((tm, o.width), functools.partial(lambda i, cb: (i, cb), cb=o.col0 // o.width))
        return pl.BlockSpec((tm, o[0]), lambda i: (i, 0))

    return _call_with_into(
        body, name, (tp // tm,), in_specs, [o.arr for o in ins], list(outs), spec_of,
        lambda o: jax.ShapeDtypeStruct((tp, o[0]), o[1]),
        [pl.BlockSpec(s, functools.partial(lambda i, nd: (0,) * nd, nd=len(s))) for s in accs],
        [jax.ShapeDtypeStruct(s, F32) for s in accs], "arbitrary")


class Cl:
    def __init__(self, arr, col0=0):
        self.arr, self.col0 = arr, col0


def _cols(fn, name, ins, outs, ncols, tc):
    assert ncols % tc == 0
    n_in, n_out = len(ins), len(outs)
    n_into = sum(isinstance(o, Into) for o in outs)

    def body(*refs):
        res = fn(*[r[...] for r in refs[:n_in]])
        if not isinstance(res, (tuple, list)):
            res = (res,)
        assert len(res) == n_out, (name, len(res))
        out_refs = refs[n_in + n_into:]
        for k in range(n_out):
            out_refs[k][...] = res[k].astype(out_refs[k].dtype)

    in_specs = []
    for o in ins:
        assert o.col0 % tc == 0, (name, o.col0, tc)
        in_specs.append(pl.BlockSpec((o.arr.shape[0], tc), functools.partial(lambda j, off: (0, j + off), off=o.col0 // tc)))

    def spec_of(o):
        if isinstance(o, Into):
            assert o.col0 % tc == 0 and o.width == ncols, (name, o.col0, o.width)
            return pl.BlockSpec((o.arr.shape[0], tc), functools.partial(lambda j, off: (0, j + off), off=o.col0 // tc))
        return pl.BlockSpec((o[0], tc), lambda j: (0, j))

    return _call_with_into(body, name, (ncols // tc,), in_specs, [o.arr for o in ins], list(outs), spec_of,
                           lambda o: jax.ShapeDtypeStruct((o[0], ncols), o[1]), [], [], "parallel")


def _ln_stats(u):
    mu = jnp.mean(u, axis=-1, keepdims=True)
    xc = u - mu
    var = jnp.mean(xc * xc, axis=-1, keepdims=True)
    rstd = lax.rsqrt(var + LN_EPS)
    return xc * rstd, rstd


def _ln_fwd(terms, g, b, name):
    coefs = [c for c, _ in terms]

    def fn(row0, *blk):
        xs, (gg, bb) = blk[:len(coefs)], blk[len(coefs):]
        u = sum(c * x for c, x in zip(coefs, xs))
        xhat, _ = _ln_stats(u)
        return xhat * gg + bb

    d = terms[0][1].shape[1]
    return _rows(fn, name, [Rw(x) for _, x in terms] + [Pm(g.reshape(1, d)), Pm(b.reshape(1, d))], [(d, F32)])[0]


def _ln_bwd(dy_terms, u_terms, g, name):
    dc = [c for c, _ in dy_terms]
    uc = [c for c, _ in u_terms]
    d = u_terms[0][1].shape[1]

    def fn(row0, *blk):
        dys = blk[:len(dc)]
        xs = blk[len(dc):len(dc) + len(uc)]
        gg = blk[-1]
        dy = sum(c * x for c, x in zip(dc, dys))
        u = sum(c * x for c, x in zip(uc, xs))
        xhat, rstd = _ln_stats(u)
        gdy = dy * gg
        m1 = jnp.mean(gdy, axis=-1, keepdims=True)
        m2 = jnp.mean(gdy * xhat, axis=-1, keepdims=True)
        du = rstd * (gdy - m1 - xhat * m2)
        return du, jnp.sum(dy * xhat, axis=0, keepdims=True), jnp.sum(dy, axis=0, keepdims=True)

    ins = [Rw(x) for _, x in dy_terms] + [Rw(x) for _, x in u_terms] + [Pm(g.reshape(1, d))]
    return _rows(fn, name, ins, [(d, F32)], accs=[(1, d), (1, d)])


def _loss_head(y, tgt, t_real, name):
    d = y.shape[1]

    def fn(row0, yb, tb):
        rows = _row_ids(yb.shape, row0)
        live = (rows >= N_META) & (rows < t_real)
        diff = jnp.where(live, yb - tb, 0.0)
        return diff * (1.0 / d), jnp.sum(diff * diff, axis=0, keepdims=True) * (0.5 / d)

    return _rows(fn, name, [Rw(y), Rw(tgt)], [(d, F32)], accs=[(1, d)])


def _rms(x, g):
    r = lax.rsqrt(jnp.mean(x * x, axis=-1, keepdims=True) + RMS_EPS)
    return x * r * g


def _rms_bwd(dy, x, g):
    r = lax.rsqrt(jnp.mean(x * x, axis=-1, keepdims=True) + RMS_EPS)
    gdy = dy * g
    dx = r * gdy - x * (r * r * r) * jnp.mean(gdy * x, axis=-1, keepdims=True)
    return dx, jnp.sum(dy * x * r, axis=0, keepdims=True)


def _mla_norms(proj, qn, kvn, name):
    def fn(row0, cq, ckv, g1, g2):
        return _rms(cq, g1), _rms(ckv, g2)

    return _rows(fn, name, [Rw(proj, Q_RANK, C_CQ // Q_RANK), Rw(proj, KV_RANK, C_CKV // KV_RANK),
                            Pm(qn.reshape(1, Q_RANK)), Pm(kvn.reshape(1, KV_RANK))],
                 [(Q_RANK, BF16), (KV_RANK, BF16)])


def _mla_norms_bwd(dcqn, dckvn, dkrp, proj, qn, kvn, dproj, name):
    def fn(row0, d1, d2, dkr, cq, ckv, g1, g2):
        dx1, dg1 = _rms_bwd(d1, cq, g1)
        dx2, dg2 = _rms_bwd(d2, ckv, g2)
        return jnp.concatenate([dx1, dx2, dkr], axis=1), dg1, dg2

    return _rows(fn, name, [Rw(dcqn), Rw(dckvn), Rw(dkrp), Rw(proj, Q_RANK, C_CQ // Q_RANK), Rw(proj, KV_RANK, C_CKV // KV_RANK),
                            Pm(qn.reshape(1, Q_RANK)), Pm(kvn.reshape(1, KV_RANK))],
                 [Into(dproj, C_CQ, PROJ_COLS - C_CQ)], accs=[(1, Q_RANK), (1, KV_RANK)])


def _fold_rope(z):
    return z + pltpu.roll(z, QK_ROPE, 1)


def _mla_pack(qext, kv, proj, cs, name):
    tp = qext.shape[0]
    tm = _tile(tp, 384)
    hw, nope_all = 2 * LANE, HEADS * QK_NOPE

    def body(q_ref, kv_ref, kr_ref, cs_ref, qo_ref, ko_ref, vo_ref):
        cs_ = cs_ref[...]
        low = lax.broadcasted_iota(jnp.int32, cs_.shape, 1) < QK_ROPE
        kr = _fold_rope(kr_ref[...] * cs_).astype(BF16)
        for h in range(HEADS):
            qr = jnp.where(low, _fold_rope(q_ref[:, h * hw + QK_NOPE:(h + 1) * hw] * cs_), 0.0)
            qo_ref[:, h * hw:h * hw + QK_NOPE] = (q_ref[:, h * hw:h * hw + QK_NOPE] * ATT_SCALE).astype(BF16)
            qo_ref[:, h * hw + QK_NOPE:(h + 1) * hw] = (qr * ATT_SCALE).astype(BF16)
            ko_ref[:, h * hw:h * hw + QK_NOPE] = kv_ref[:, h * QK_NOPE:(h + 1) * QK_NOPE].astype(BF16)
            ko_ref[:, h * hw + QK_NOPE:(h + 1) * hw] = kr
        vo_ref[...] = kv_ref[:, nope_all:].astype(BF16)

    row = lambda w: pl.BlockSpec((tm, w), lambda i: (i, 0))
    return pl.pallas_call(
        body, name=name, grid=(tp // tm,),
        in_specs=[row(HEADS * hw), row(2 * nope_all), pl.BlockSpec((tm, LANE), lambda i: (i, C_KRP // LANE)), row(LANE)],
        out_specs=[row(HEADS * hw), row(HEADS * hw), row(nope_all)],
        out_shape=[jax.ShapeDtypeStruct((tp, HEADS * hw), BF16),
                   jax.ShapeDtypeStruct((tp, HEADS * hw), BF16),
                   jax.ShapeDtypeStruct((tp, nope_all), BF16)],
        compiler_params=_cparams("parallel"),
    )(qext, kv, proj, cs)


def _mla_unpack(dk, dv, cs, name):
    tp = dk.shape[0]
    tm = _tile(tp, 384)
    hw, nope_all = 2 * LANE, HEADS * QK_NOPE

    def body(dk_ref, dv_ref, cs_ref, dkv_ref, dkr_ref):
        cs_ = cs_ref[...]
        low = lax.broadcasted_iota(jnp.int32, cs_.shape, 1) < QK_ROPE
        dkr = None
        for h in range(HEADS):
            dkv_ref[:, h * QK_NOPE:(h + 1) * QK_NOPE] = dk_ref[:, h * hw:h * hw + QK_NOPE].astype(BF16)
            part = jnp.where(low, dk_ref[:, h * hw + QK_NOPE:(h + 1) * hw], 0.0)
            dkr = part if h == 0 else dkr + part
        dkv_ref[:, nope_all:] = dv_ref[...].astype(BF16)
        dkr_ref[...] = _fold_rope(dkr) * cs_

    row = lambda w: pl.BlockSpec((tm, w), lambda i: (i, 0))
    return pl.pallas_call(
        body, name=name, grid=(tp // tm,),
        in_specs=[row(HEADS * hw), row(nope_all), row(LANE)],
        out_specs=[row(2 * nope_all), row(LANE)],
        out_shape=[jax.ShapeDtypeStruct((tp, 2 * nope_all), BF16), jax.ShapeDtypeStruct((tp, LANE), F32)],
        compiler_params=_cparams("parallel"),
    )(dk, dv, cs)


def _attn_fwd(q, k, v, t_real, name):
    tp = q.shape[0]
    tq = _tile(tp, 1408)
    tkc = _tile(tp, 1408)
    nkc = -(-t_real // tkc)

    def body(q_ref, k_ref, v_ref, o_ref, lse_ref):
        qb = q_ref[...]
        m = l = acc = None
        for c in range(nkc):
            s = lax.dot_general(qb, k_ref[c * tkc:(c + 1) * tkc, :], (((1,), (1,)), ((), ())), preferred_element_type=F32)
            if (c + 1) * tkc > t_real:
                cols = lax.broadcasted_iota(jnp.int32, s.shape, 1) + c * tkc
                s = jnp.where(cols < t_real, s, NEG_BIG)
            mc = jnp.max(s, axis=-1, keepdims=True)
            m_new = mc if c == 0 else jnp.maximum(m, mc)
            p = jnp.exp2(s - m_new)
            lc = jnp.sum(p, axis=-1, keepdims=True)
            pv = jnp.dot(p.astype(BF16), v_ref[c * tkc:(c + 1) * tkc, :], preferred_element_type=F32)
            if c == 0:
                l, acc = lc, pv
            else:
                alpha = jnp.exp2(m - m_new)
                l, acc = alpha * l + lc, alpha * acc + pv
            m = m_new
        o_ref[...] = acc / l
        lse_ref[...] = m + jnp.log2(l)

    return pl.pallas_call(
        body, name=name, grid=(HEADS, tp // tq),
        in_specs=[pl.BlockSpec((tq, 2 * LANE), lambda h, i: (i, h)),
                  pl.BlockSpec((tp, 2 * LANE), lambda h, i: (0, h)),
                  pl.BlockSpec((tp, LANE), lambda h, i: (0, h))],
        out_specs=[pl.BlockSpec((tq, LANE), lambda h, i: (i, h)),
                   pl.BlockSpec((None, tq, 1), lambda h, i: (h, i, 0))],
        out_shape=[jax.ShapeDtypeStruct((tp, HEADS * LANE), F32),
                   jax.ShapeDtypeStruct((HEADS, tp, 1), F32)],
        compiler_params=_cparams("parallel", "parallel"),
    )(q, k, v)


def _attn_bwd(q, k, v, do, o, lse, cs, t_real, name):
    tp = q.shape[0]
    tq = _tile(tp, 1408)
    tkc = _tile(tp, 704, 64)
    nkc = -(-t_real // tkc)

    def body(q_ref, k_ref, v_ref, do_ref, o_ref, lse_ref, cs_ref, dq_ref, dk_ref, dv_ref):
        i = pl.program_id(1)

        @pl.when(i == 0)
        def _():
            dk_ref[...] = jnp.zeros_like(dk_ref)
            dv_ref[...] = jnp.zeros_like(dv_ref)

        qb = q_ref[...]
        dob = do_ref[...]
        dob16 = dob.astype(BF16)
        dol2 = (dob * LN2).astype(BF16)
        delta = jnp.sum(dob * o_ref[...], axis=-1, keepdims=True) * LN2
        lse = lse_ref[...]
        dq = None
        for c in range(nkc):
            ks = slice(c * tkc, (c + 1) * tkc)
            kb = k_ref[ks, :]
            s = lax.dot_general(qb, kb, (((1,), (1,)), ((), ())), preferred_element_type=F32)
            p = jnp.exp2(s - lse)
            if (c + 1) * tkc > t_real:
                cols = lax.broadcasted_iota(jnp.int32, s.shape, 1) + c * tkc
                p = jnp.where(cols < t_real, p, 0.0)
            dp = lax.dot_general(dol2, v_ref[ks, :], (((1,), (1,)), ((), ())), preferred_element_type=F32)
            ds = (p * (dp - delta)).astype(BF16)
            dqc = jnp.dot(ds, kb, preferred_element_type=F32)
            dq = dqc if c == 0 else dq + dqc
            dk_ref[ks, :] += lax.dot_general(ds, qb, (((0,), (0,)), ((), ())), preferred_element_type=F32)
            dv_ref[ks, :] += lax.dot_general(p.astype(BF16), dob16, (((0,), (0,)), ((), ())), preferred_element_type=F32)
        cs_ = cs_ref[...]
        low = lax.broadcasted_iota(jnp.int32, cs_.shape, 1) < QK_ROPE
        dq = dq * ATT_SCALE
        dq_ref[:, :QK_NOPE] = dq[:, :QK_NOPE].astype(BF16)
        dq_ref[:, QK_NOPE:] = (_fold_rope(jnp.where(low, dq[:, QK_NOPE:], 0.0)) * cs_).astype(BF16)

    return pl.pallas_call(
        body, name=name, grid=(HEADS, tp // tq),
        in_specs=[pl.BlockSpec((tq, 2 * LANE), lambda h, i: (i, h)),
                  pl.BlockSpec((tp, 2 * LANE), lambda h, i: (0, h)),
                  pl.BlockSpec((tp, LANE), lambda h, i: (0, h)),
                  pl.BlockSpec((tq, LANE), lambda h, i: (i, h)),
                  pl.BlockSpec((tq, LANE), lambda h, i: (i, h)),
                  pl.BlockSpec((None, tq, 1), lambda h, i: (h, i, 0)),
                  pl.BlockSpec((tq, LANE), lambda h, i: (i, 0))],
        out_specs=[pl.BlockSpec((tq, 2 * LANE), lambda h, i: (i, h)),
                   pl.BlockSpec((tp, 2 * LANE), lambda h, i: (0, h)),
                   pl.BlockSpec((tp, LANE), lambda h, i: (0, h))],
        out_shape=[jax.ShapeDtypeStruct((tp, HEADS * 2 * LANE), BF16),
                   jax.ShapeDtypeStruct((tp, HEADS * 2 * LANE), F32),
                   jax.ShapeDtypeStruct((tp, HEADS * LANE), F32)],
        compiler_params=_cparams("parallel", "arbitrary"),
    )(q, k, v, do, o, lse, cs)


def _shift_rows(x, s):
    tp = x.shape[0]
    return x if s % tp == 0 else pltpu.roll(x, s % tp, 0)


def _conv_fwd_val(xm, w, b, pad_left):
    acc = b + w[0:1, :] * _shift_rows(xm, pad_left)
    for k in range(1, w.shape[0]):
        acc = acc + w[k:k + 1, :] * _shift_rows(xm, pad_left - k)
    return acc


def _conv_bwd_val(dy, xm, w, pad_left, live):
    kk = w.shape[0]
    dx = w[0:1, :] * _shift_rows(dy, -pad_left)
    dws = [jnp.sum(dy * _shift_rows(xm, pad_left), axis=0, keepdims=True)]
    for k in range(1, kk):
        dx = dx + w[k:k + 1, :] * _shift_rows(dy, k - pad_left)
        dws.append(jnp.sum(dy * _shift_rows(xm, pad_left - k), axis=0, keepdims=True))
    return jnp.where(live, dx, 0.0), jnp.concatenate(dws, axis=0), jnp.sum(dy, axis=0, keepdims=True)


def _lru_conv_fwd(proj, w, b, t_real, name):
    def fn(x, ww, bb):
        xm = jnp.where(_row_ids(x.shape) < t_real, x, 0.0)
        return _conv_fwd_val(xm, ww, bb, 2)

    return _cols(fn, name, [Cl(proj, C_LRU_X), Cl(w), Cl(b.reshape(1, -1))], [(proj.shape[0], F32)], D_MODEL, 128)[0]


def _lru_conv_bwd(dxc, proj, w, dproj, t_real, name):
    def fn(dy, x, ww):
        live = _row_ids(x.shape) < t_real
        xm = jnp.where(live, x, 0.0)
        dym = jnp.where(live, dy, 0.0)
        return _conv_bwd_val(dym, xm, ww, 2, live)

    return _cols(fn, name, [Cl(dxc), Cl(proj, C_LRU_X), Cl(w)],
                 [Into(dproj, C_LRU_X, D_MODEL), (w.shape[0], F32), (1, F32)], D_MODEL, 128)


def _ffn_conv_act(up, w, b, t_real, name):
    def fn(g, v, wg, wv, bg, bv):
        live = _row_ids(g.shape) < t_real
        gc = _conv_fwd_val(jnp.where(live, g, 0.0), wg, bg, 1)
        vc = _conv_fwd_val(jnp.where(live, v, 0.0), wv, bv, 1)
        return _gelu(gc) * vc

    b2 = b.reshape(1, -1)
    return _cols(fn, name, [Cl(up), Cl(up, D_FF), Cl(w), Cl(w, D_FF), Cl(b2), Cl(b2, D_FF)],
                 [(up.shape[0], BF16)], D_FF, 128)[0]


def _ffn_conv_act_bwd(dm, up, w, b, t_real, name):
    tp, kk = up.shape[0], w.shape[0]
    nb = D_FF // LANE
    assert nb >= 2

    def body(dm_ref, g_ref, v_ref, wg_ref, wv_ref, bg_ref, bv_ref, dup_ref, dwg_ref, dwv_ref, dbg_ref, dbv_ref, stage, sems):
        j = pl.program_id(0)
        slot = j % 2

        def copies(step, sl):
            return [pltpu.make_async_copy(stage.at[sl, half],
                                          dup_ref.at[:, pl.ds(pl.multiple_of(half * D_FF + step * LANE, LANE), LANE)],
                                          sems.at[sl, half]) for half in range(2)]

        @pl.when(j >= 2)
        def _():
            for cp in copies(j - 2, slot):
                cp.wait()

        live = _row_ids((tp, LANE)) < t_real
        gm, vm = jnp.where(live, g_ref[...], 0.0), jnp.where(live, v_ref[...], 0.0)
        gc = _conv_fwd_val(gm, wg_ref[...], bg_ref[...], 1)
        vc = _conv_fwd_val(vm, wv_ref[...], bv_ref[...], 1)
        act, dact = _gelu_and_grad(gc)
        dmm = jnp.where(live, dm_ref[...], 0.0)
        dgx, dwg_ref[...], dbg_ref[...] = _conv_bwd_val(dmm * vc * dact, gm, wg_ref[...], 1, live)
        dvx, dwv_ref[...], dbv_ref[...] = _conv_bwd_val(dmm * act, vm, wv_ref[...], 1, live)
        stage[slot, 0] = dgx.astype(BF16)
        stage[slot, 1] = dvx.astype(BF16)
        for cp in copies(j, slot):
            cp.start()

        @pl.when(j == nb - 1)
        def _():
            for cp in copies(j - 1, 1 - slot) + copies(j, slot):
                cp.wait()

    b2 = b.reshape(1, -1)
    col = lambda rows, off: pl.BlockSpec((rows, LANE), functools.partial(lambda j, o: (0, j + o), o=off))
    return pl.pallas_call(
        body, name=name, grid=(nb,),
        in_specs=[col(tp, 0), col(tp, 0), col(tp, nb), col(kk, 0), col(kk, nb), col(1, 0), col(1, nb)],
        out_specs=[pl.BlockSpec(memory_space=pl.ANY), col(kk, 0), col(kk, 0), col(1, 0), col(1, 0)],
        out_shape=[jax.ShapeDtypeStruct((tp, 2 * D_FF), BF16), jax.ShapeDtypeStruct((kk, D_FF), F32),
                   jax.ShapeDtypeStruct((kk, D_FF), F32), jax.ShapeDtypeStruct((1, D_FF), F32),
                   jax.ShapeDtypeStruct((1, D_FF), F32)],
        scratch_shapes=[pltpu.VMEM((2, 2, tp, LANE), BF16), pltpu.SemaphoreType.DMA((2, 2))],
        compiler_params=_cparams("arbitrary"),
    )(dm, up, up, w, w, b2, b2)


def _lru_gates_fwd(xc, wg, b4, lam, t_real, name):
    tp = xc.shape[0]
    tm = _tile(tp, 1408)

    def body(x_ref, w_ref, b_ref, lam_ref, r0_ref, r1_ref, i0_ref, i1_ref, a0_ref, a1_ref, u0_ref, u1_ref):
        x = x_ref[...]
        xb = x.astype(BF16)
        live = _row_ids(x.shape, pl.program_id(1) * tm) < t_real
        bb = b_ref[...]
        sp = _softplus_neg(lam_ref[...])
        gate = [_sigmoid(jnp.dot(xb, w_ref[k], preferred_element_type=F32) + bb[k:k + 1, :]) for k in range(4)]
        for d, (r_ref, i_ref, a_ref, u_ref) in enumerate(((r0_ref, i0_ref, a0_ref, u0_ref), (r1_ref, i1_ref, a1_ref, u1_ref))):
            r, ig = gate[d], gate[2 + d]
            a = jnp.exp(-LRU_C * r * sp[d:d + 1, :])
            r_ref[...] = r
            i_ref[...] = ig
            a_ref[...] = a
            u_ref[...] = jnp.where(live, jnp.sqrt(1.0 - a * a) * (ig * x), 0.0)

    blk = pl.BlockSpec((tm, LANE), lambda g, i: (i, g))
    return pl.pallas_call(
        body, name=name, grid=(LRU_BLOCKS, tp // tm),
        in_specs=[blk, pl.BlockSpec((None, 4, LANE, LANE), lambda g, i: (g, 0, 0, 0)),
                  pl.BlockSpec((4, LANE), lambda g, i: (0, g)), pl.BlockSpec((2, LANE), lambda g, i: (0, g))],
        out_specs=[blk] * 8,
        out_shape=[jax.ShapeDtypeStruct((tp, D_MODEL), F32)] * 8,
        compiler_params=_cparams("parallel", "parallel"),
    )(xc, wg, b4, lam)


def _lru_gates_bwd(l0, l1, da0, da1, r0, r1, i0, i1, a0, a1, xc, wg, lam, t_real, name):
    tp = xc.shape[0]
    tm = _tile(tp, 1408)

    rc = 32
    assert tm % rc == 0

    def fold(v):
        out = v[0:SUBLANE]
        for t in range(1, rc // SUBLANE):
            out = out + v[t * SUBLANE:(t + 1) * SUBLANE]
        return out

    def body(l0_ref, l1_ref, da0_ref, da1_ref, r0_ref, r1_ref, i0_ref, i1_ref, a0_ref, a1_ref, x_ref, w_ref, lam_ref,
             dx_ref, dw_ref, db_ref, dlam_ref, pre_s, dxp_s):
        i = pl.program_id(1)
        lam_ = lam_ref[...]
        sp = _softplus_neg(lam_)
        dsp_dlam = -_sigmoid(-lam_)

        def chunk(c, sums):
            r0 = pl.multiple_of(c * rc, rc)
            rows = pl.ds(r0, rc)
            x = x_ref[rows, :]
            live = _row_ids((rc, LANE), i * tm + r0) < t_real
            dxp = jnp.zeros_like(x)
            sums = list(sums)
            for d, (l_ref, da_ref, r_ref, i_ref, a_ref) in enumerate(((l0_ref, da0_ref, r0_ref, i0_ref, a0_ref),
                                                                      (l1_ref, da1_ref, r1_ref, i1_ref, a1_ref))):
                r, ig, a = r_ref[rows, :], i_ref[rows, :], a_ref[rows, :]
                du = jnp.where(live, l_ref[rows, :], 0.0)
                a2 = a * a
                rs = lax.rsqrt(1.0 - a2)
                dv = du * ((1.0 - a2) * rs)
                ds = du * (ig * x)
                dla = jnp.where(live, da_ref[rows, :] * a - ds * (a2 * rs), 0.0)
                dr = dla * (-LRU_C) * sp[d:d + 1, :]
                p_r = dr * r * (1.0 - r)
                p_i = dv * x * ig * (1.0 - ig)
                pre_s[d, rows, :] = p_r.astype(BF16)
                pre_s[2 + d, rows, :] = p_i.astype(BF16)
                dxp = dxp + dv * ig
                sums[d] = sums[d] + fold(p_r)
                sums[2 + d] = sums[2 + d] + fold(p_i)
                sums[4 + d] = sums[4 + d] + fold(dla * (-LRU_C) * r)
            dxp_s[rows, :] = dxp
            return tuple(sums)

        zero = jnp.zeros((SUBLANE, LANE), F32)
        sums = lax.fori_loop(0, tm // rc, chunk, (zero,) * 6)

        @pl.when(i == 0)
        def _():
            dw_ref[...] = jnp.zeros_like(dw_ref)
            db_ref[...] = jnp.zeros_like(db_ref)
            dlam_ref[...] = jnp.zeros_like(dlam_ref)

        xb = x_ref[...].astype(BF16)
        dx = dxp_s[...]
        for k in range(4):
            pk = pre_s[k]
            dx = dx + lax.dot_general(pk, w_ref[k], (((1,), (1,)), ((), ())), preferred_element_type=F32)
            dw_ref[k] += lax.dot_general(xb, pk, (((0,), (0,)), ((), ())), preferred_element_type=F32)
        db_ref[...] += jnp.concatenate([jnp.sum(sums[k], axis=0, keepdims=True) for k in range(4)], axis=0)
        dlam_ref[...] += jnp.concatenate([jnp.sum(sums[4 + d], axis=0, keepdims=True) * dsp_dlam[d:d + 1, :] for d in range(2)], axis=0)
        dx_ref[...] = dx

    blk = pl.BlockSpec((tm, LANE), lambda g, i: (i, g))
    return pl.pallas_call(
        body, name=name, grid=(LRU_BLOCKS, tp // tm),
        in_specs=[blk] * 11 + [pl.BlockSpec((None, 4, LANE, LANE), lambda g, i: (g, 0, 0, 0)),
                               pl.BlockSpec((2, LANE), lambda g, i: (0, g))],
        out_specs=[blk, pl.BlockSpec((None, 4, LANE, LANE), lambda g, i: (g, 0, 0, 0)),
                   pl.BlockSpec((4, LANE), lambda g, i: (0, g)), pl.BlockSpec((2, LANE), lambda g, i: (0, g))],
        out_shape=[jax.ShapeDtypeStruct((tp, D_MODEL), F32), jax.ShapeDtypeStruct((LRU_BLOCKS, 4, LANE, LANE), F32),
                   jax.ShapeDtypeStruct((4, D_MODEL), F32), jax.ShapeDtypeStruct((2, D_MODEL), F32)],
        scratch_shapes=[pltpu.VMEM((4, tm, LANE), BF16), pltpu.VMEM((tm, LANE), F32)],
        compiler_params=_cparams("parallel", "arbitrary"),
    )(l0, l1, da0, da1, r0, r1, i0, i1, a0, a1, xc, wg, lam)


SCAN_UNROLL = 4


def _loop_tiles(nt, step, carry):
    assert nt % SCAN_UNROLL == 0

    def trip(tt, c):
        for u in range(SCAN_UNROLL):
            c = step(tt * SCAN_UNROLL + u, c)
        return c

    return lax.fori_loop(0, nt // SCAN_UNROLL, trip, carry)


def _tile_scan(a, u, reverse):
    rows = lax.broadcasted_iota(jnp.int32, a.shape, 0)
    for s in (1, 2, 4):
        if reverse:
            keep = rows < SUBLANE - s
            a_sh, u_sh = pltpu.roll(a, SUBLANE - s, 0), pltpu.roll(u, SUBLANE - s, 0)
        else:
            keep = rows >= s
            a_sh, u_sh = pltpu.roll(a, s, 0), pltpu.roll(u, s, 0)
        u = u + a * jnp.where(keep, u_sh, 0.0)
        a = a * jnp.where(keep, a_sh, 1.0)
    return a, u


def _scan_fwd(a0, u0, a1, u1, proj, name):
    tp, d = a0.shape
    tc = 128
    nt = tp // SUBLANE

    def body(a0_ref, u0_ref, a1_ref, u1_ref, lg_ref, h0_ref, h1_ref, gh_ref):
        def step(t, carry):
            c0, c1 = carry
            f = pl.multiple_of(t * SUBLANE, SUBLANE)
            b = pl.multiple_of((nt - 1 - t) * SUBLANE, SUBLANE)
            pa, pu = _tile_scan(a0_ref[pl.ds(f, SUBLANE), :], u0_ref[pl.ds(f, SUBLANE), :], False)
            h = pu + pa * c0
            h0_ref[pl.ds(f, SUBLANE), :] = h
            c0 = h[SUBLANE - 1:SUBLANE, :]
            pa, pu = _tile_scan(a1_ref[pl.ds(b, SUBLANE), :], u1_ref[pl.ds(b, SUBLANE), :], True)
            h = pu + pa * c1
            h1_ref[pl.ds(b, SUBLANE), :] = h
            c1 = h[0:1, :]
            return c0, c1

        z = jnp.zeros((1, tc), F32)
        _loop_tiles(nt, step, (z, z))
        gh_ref[...] = (_gelu(lg_ref[...]) * (h0_ref[...] + h1_ref[...])).astype(BF16)

    blk = pl.BlockSpec((tp, tc), lambda j: (0, j))
    return pl.pallas_call(
        body, name=name, grid=(d // tc,),
        in_specs=[blk] * 4 + [pl.BlockSpec((tp, tc), lambda j: (0, j + C_LRU_G // tc))], out_specs=[blk] * 3,
        out_shape=[jax.ShapeDtypeStruct((tp, d), F32)] * 2 + [jax.ShapeDtypeStruct((tp, d), BF16)],
        compiler_params=_cparams("parallel"),
    )(a0, u0, a1, u1, proj)


def _scan_bwd(dh, a0, a1, h0, h1, name):
    tp, d = dh.shape
    tc = 128
    nt = tp // SUBLANE

    def body(dh_ref, a0_ref, a1_ref, h0_ref, h1_ref, l0_ref, l1_ref, da0_ref, da1_ref):
        rows8 = lax.broadcasted_iota(jnp.int32, (SUBLANE, tc), 0)

        def step(t, carry):
            c0, c1 = carry
            b = pl.multiple_of((nt - 1 - t) * SUBLANE, SUBLANE)
            f = pl.multiple_of(t * SUBLANE, SUBLANE)
            a = a0_ref[pl.ds(b, SUBLANE), :]
            a_next = jnp.where(rows8 < SUBLANE - 1, pltpu.roll(a, SUBLANE - 1, 0), 1.0)
            pa, pu = _tile_scan(a_next, dh_ref[pl.ds(b, SUBLANE), :], True)
            lam = pu + pa * c0
            l0_ref[pl.ds(b, SUBLANE), :] = lam
            c0 = a[0:1, :] * lam[0:1, :]
            a = a1_ref[pl.ds(f, SUBLANE), :]
            a_prev = jnp.where(rows8 >= 1, pltpu.roll(a, 1, 0), 1.0)
            pa, pu = _tile_scan(a_prev, dh_ref[pl.ds(f, SUBLANE), :], False)
            lam = pu + pa * c1
            l1_ref[pl.ds(f, SUBLANE), :] = lam
            c1 = a[SUBLANE - 1:SUBLANE, :] * lam[SUBLANE - 1:SUBLANE, :]
            return c0, c1

        z = jnp.zeros((1, tc), F32)
        _loop_tiles(nt, step, (z, z))
        rows = lax.broadcasted_iota(jnp.int32, (tp, tc), 0)
        da0_ref[...] = l0_ref[...] * jnp.where(rows >= 1, pltpu.roll(h0_ref[...], 1, 0), 0.0)
        da1_ref[...] = l1_ref[...] * jnp.where(rows < tp - 1, pltpu.roll(h1_ref[...], tp - 1, 0), 0.0)

    blk = pl.BlockSpec((tp, tc), lambda j: (0, j))
    return pl.pallas_call(
        body, name=name, grid=(d // tc,), in_specs=[blk] * 5, out_specs=[blk] * 4,
        out_shape=[jax.ShapeDtypeStruct((tp, d), F32)] * 4,
        compiler_params=_cparams("parallel"),
    )(dh, a0, a1, h0, h1)


def _gated_h_bwd(dgh, proj, h0, h1, dproj, name):
    def fn(row0, dg, lg, x0, x1):
        act, dact = _gelu_and_grad(lg)
        return dg * (x0 + x1) * dact, dg * act

    return _rows(fn, name, [Rw(dgh), Rw(proj, D_MODEL, C_LRU_G // D_MODEL), Rw(h0), Rw(h1)],
                 [Into(dproj, C_LRU_G, D_MODEL), (D_MODEL, F32)])


def _mix(proj, y_mla, y_lru, name):
    def fn(row0, gm, gl, ym, yl):
        return _sigmoid(gm) * ym + _sigmoid(gl) * yl

    return _rows(fn, name, [Rw(proj, D_MODEL, C_G_MLA // D_MODEL), Rw(proj, D_MODEL, C_G_LRU // D_MODEL), Rw(y_mla), Rw(y_lru)],
                 [(D_MODEL, BF16)])[0]


def _mix_bwd(dz, proj, y_mla, y_lru, dproj, name):
    def fn(row0, dzb, gm, gl, ym, yl):
        sm, sl = _sigmoid(gm), _sigmoid(gl)
        dg = jnp.concatenate([dzb * ym * sm * (1.0 - sm), dzb * yl * sl * (1.0 - sl)], axis=1)
        return dzb * sm, dzb * sl, dg

    return _rows(fn, name, [Rw(dz), Rw(proj, D_MODEL, C_G_MLA // D_MODEL), Rw(proj, D_MODEL, C_G_LRU // D_MODEL),
                            Rw(y_mla), Rw(y_lru)], [(D_MODEL, BF16), (D_MODEL, BF16), Into(dproj, C_G_MLA, 2 * D_MODEL)])


def _layer_fwd(h, w_in, more_weights, cs, t_real, tag):
    proj = _matmul(h, w_in, tag + "proj", tb=True)
    w = dict(more_weights(0, proj), w_in=w_in)
    cqn, ckvn = _mla_norms(proj, w['q_norm'], w['kv_norm'], tag + "mla_norms")
    qext = _matmul(cqn, w['w_q'], tag + "q_up")
    kv = _matmul(ckvn, w['w_kv'], tag + "kv_up")
    qc, kc, vb = _mla_pack(qext, kv, proj, cs, tag + "mla_pack")
    o, lse = _attn_fwd(qc, kc, vb, t_real, tag + "attn_fwd")
    w.update(more_weights(1, o))
    y_mla = _matmul(o, w['w_o_mla'], tag + "o_mla")
    xc = _lru_conv_fwd(proj, w['lru_conv_w'], w['lru_conv_b'], t_real, tag + "lru_conv")
    r0, r1, i0, i1, a0, a1, u0, u1 = _lru_gates_fwd(xc, w['w_g'], w['b4'], w['lru_lambda'], t_real, tag + "lru_gates")
    h0, h1, gh = _scan_fwd(a0, u0, a1, u1, proj, tag + "lru_scan")
    y_lru = _matmul(gh, w['w_o_lru'], tag + "o_lru")
    z = _mix(proj, y_mla, y_lru, tag + "mix")
    zo = _matmul(z, w['w_out'], tag + "w_out")
    hm = _ln_fwd([(DN_ALPHA, h), (1.0, zo)], w['ln1_g'], w['ln1_b'], tag + "ln1")
    w.update(more_weights(2, hm))
    up = _matmul(hm, w['w_up'], tag + "w_up", tb=True)
    m = _ffn_conv_act(up, w['ffn_conv_w'], w['ffn_conv_b'], t_real, tag + "ffn_conv")
    f = _matmul(m, w['w_down'], tag + "w_down", tk_cap=1408)
    out = _ln_fwd([(DN_ALPHA, hm), (1.0, f)], w['ln2_g'], w['ln2_b'], tag + "ln2")
    saved = dict(w=w, h=h, proj=proj, cqn=cqn, ckvn=ckvn, qc=qc, kc=kc, vb=vb, o=o, lse=lse, y_mla=y_mla, xc=xc,
                 r0=r0, r1=r1, i0=i0, i1=i1, a0=a0, a1=a1, h0=h0, h1=h1, gh=gh, y_lru=y_lru, z=z, zo=zo, hm=hm,
                 up=up, m=m, f=f)
    return out, saved


DW_MATMUL = dict(ta=True, out_dtype=BF16, tn_cap=1408, tk_cap=1408)


def _after(a, tok):
    return a if tok is None else a + tok.astype(a.dtype)


def _layer_bwd(dout_terms, s, cs, t_real, tag, emit, tok):
    w = s['w']
    g = {}
    du2, dg2, db2 = _ln_bwd(dout_terms, [(DN_ALPHA, s['hm']), (1.0, s['f'])], _after(w['ln2_g'], tok), tag + "ln2_bwd")
    g['ln2_g'], g['ln2_b'] = dg2, db2
    dm = _matmul(du2, w['w_down'], tag + "w_down_dx", tb=True)
    g['w_down'] = _matmul(s['m'], du2, tag + "w_down_dw", **DW_MATMUL)
    dup, dwg_, dwv_, dbg_, dbv_ = _ffn_conv_act_bwd(dm, s['up'], w['ffn_conv_w'], w['ffn_conv_b'], t_real, tag + "ffn_conv_bwd")
    g['ffn_conv_w'] = jnp.concatenate([dwg_, dwv_], axis=1)
    g['ffn_conv_b'] = jnp.concatenate([dbg_, dbv_], axis=1)
    dhm_mm = _matmul(dup, w['w_up'], tag + "w_up_dx", tk_cap=1408)
    g['w_up'] = _matmul(s['hm'], dup, tag + "w_up_dw", **DW_MATMUL)
    tok = emit('ffn', g)
    g = {}
    du1, dg1, db1 = _ln_bwd([(DN_ALPHA, du2), (1.0, dhm_mm)], [(DN_ALPHA, s['h']), (1.0, s['zo'])],
                            _after(w['ln1_g'], tok), tag + "ln1_bwd")
    g['ln1_g'], g['ln1_b'] = dg1, db1
    dz = _matmul(du1, w['w_out'], tag + "w_out_dx", tb=True)
    g['w_out'] = _matmul(s['z'], du1, tag + "w_out_dw", **DW_MATMUL)
    dproj = lax.empty(s['proj'].shape, BF16)
    dy_mla, dy_lru, dproj = _mix_bwd(dz, s['proj'], s['y_mla'], s['y_lru'], dproj, tag + "mix_bwd")
    do = _matmul(dy_mla, w['w_o_mla'], tag + "o_mla_dx", tb=True)
    g['w_o_mla'] = _matmul(s['o'], dy_mla, tag + "o_mla_dw", **DW_MATMUL)
    dqext, dkc, dv = _attn_bwd(s['qc'], s['kc'], s['vb'], do, s['o'], s['lse'], cs, t_real, tag + "attn_bwd")
    dkv, dkrp = _mla_unpack(dkc, dv, cs, tag + "mla_unpack")
    dcqn = _matmul(dqext, w['w_q'], tag + "q_up_dx", tb=True)
    g['w_q'] = _matmul(s['cqn'], dqext, tag + "q_up_dw", **DW_MATMUL)
    dckvn = _matmul(dkv, w['w_kv'], tag + "kv_up_dx", tb=True)
    g['w_kv'] = _matmul(s['ckvn'], dkv, tag + "kv_up_dw", **DW_MATMUL)
    dgh = _matmul(dy_lru, w['w_o_lru'], tag + "o_lru_dx", tb=True)
    g['w_o_lru'] = _matmul(s['gh'], dy_lru, tag + "o_lru_dw", **DW_MATMUL)
    dproj, dhs = _gated_h_bwd(dgh, s['proj'], s['h0'], s['h1'], dproj, tag + "lru_gate_out_bwd")
    l0, l1, da0, da1 = _scan_bwd(dhs, s['a0'], s['a1'], s['h0'], s['h1'], tag + "lru_scan_bwd")
    dxc, g['w_g'], g['b4'], g['lru_lambda'] = _lru_gates_bwd(
        l0, l1, da0, da1, s['r0'], s['r1'], s['i0'], s['i1'], s['a0'], s['a1'], s['xc'], w['w_g'], w['lru_lambda'],
        t_real, tag + "lru_gates_bwd")
    dproj, g['lru_conv_w'], g['lru_conv_b'] = _lru_conv_bwd(dxc, s['proj'], w['lru_conv_w'], dproj, t_real, tag + "lru_conv_bwd")
    tok = emit('mid', g)
    dproj, dqn, dkvn = _mla_norms_bwd(dcqn, dckvn, _after(dkrp, tok), s['proj'], w['q_norm'], w['kv_norm'], dproj,
                                      tag + "mla_norms_bwd")
    tok = emit('in', {'w_in': _matmul(s['h'], dproj, tag + "proj_dw", **DW_MATMUL), 'q_norm': dqn, 'kv_norm': dkvn})
    dh_mm = _matmul(dproj, w['w_in'], tag + "proj_dx", tk_cap=1536)
    return [(DN_ALPHA, du1), (1.0, dh_mm)], tok


def _swap_halves(a, axis=-1):
    h1, h2 = jnp.split(a, 2, axis=axis)
    return jnp.concatenate([h2, h1], axis=axis)


def _w_in_kernel(w_in_t):
    cq, ckv, kr, lg, lx, gm, gl = jnp.split(w_in_t, [256, 384, 448, 1472, 2496, 3520], axis=0)
    return jnp.concatenate([lg, lx, gm, gl, cq, ckv, kr, _swap_halves(kr, axis=0)], axis=0)


def _layer_weights(fl):
    w = {}
    if 'w_uq' in fl:
        uq = fl['w_uq']
        w['w_q'] = jnp.concatenate([uq, _swap_halves(uq[..., QK_NOPE:])], axis=-1).reshape(Q_RANK, HEADS * 2 * LANE)
        w['w_kv'] = jnp.concatenate([fl['w_uk'].reshape(KV_RANK, -1), fl['w_uv'].reshape(KV_RANK, -1)], axis=1).astype(BF16)
        w['w_g'] = jnp.moveaxis(jnp.concatenate([fl['w_rg'], fl['w_ig']], axis=0), 0, 1).astype(BF16)
        w['b4'] = jnp.concatenate([fl['b_rg'], fl['b_ig']], axis=0)
    for n in ('q_norm', 'kv_norm', 'w_o_mla', 'lru_conv_w', 'lru_conv_b', 'lru_lambda', 'w_o_lru', 'w_out', 'ln1_g',
              'ln1_b', 'w_up', 'ffn_conv_w', 'ffn_conv_b', 'w_down', 'ln2_g', 'ln2_b'):
        if n in fl:
            w[n] = fl[n]
    return w


def _layer_grads(g):
    out = {}
    if 'w_in' in g:
        lg, lx, gm, gl, cq, ckv, kr, krs = jnp.split(g['w_in'], [1024, 2048, 3072, 4096, 4352, 4480, 4544], axis=1)
        out['w_in'] = jnp.concatenate([cq, ckv, kr + _swap_halves(krs), lg, lx, gm, gl], axis=1)
    if 'w_q' in g:
        gq = g['w_q'].reshape(Q_RANK, HEADS, 2 * LANE)
        out['w_uq'] = jnp.concatenate([gq[..., :QK_NOPE], gq[..., QK_NOPE:QK_NOPE + QK_ROPE] + _swap_halves(gq[..., QK_NOPE + QK_ROPE:])], axis=-1)
    if 'w_kv' in g:
        out['w_uk'] = g['w_kv'][:, :HEADS * QK_NOPE].reshape(KV_RANK, HEADS, QK_NOPE)
        out['w_uv'] = g['w_kv'][:, HEADS * QK_NOPE:].reshape(KV_RANK, HEADS, V_HEAD)
    if 'w_g' in g:
        gg = jnp.moveaxis(g['w_g'], 1, 0)
        out['w_rg'], out['w_ig'] = gg[:2], gg[2:]
    if 'b4' in g:
        out['b_rg'], out['b_ig'] = g['b4'][:2], g['b4'][2:]
    for n in ('q_norm', 'kv_norm', 'lru_conv_b', 'ln1_g', 'ln1_b', 'ffn_conv_b', 'ln2_g', 'ln2_b'):
        if n in g:
            out[n] = g[n].reshape(-1)
    for n in ('w_o_mla', 'lru_conv_w', 'lru_lambda', 'w_o_lru', 'w_out', 'w_up', 'ffn_conv_w', 'w_down'):
        if n in g:
            out[n] = g[n]
    return out


def _rope_table(tp):
    half = QK_ROPE // 2
    inv_freq = jnp.exp(-math.log(ROPE_THETA) * jnp.arange(half, dtype=F32) / half)
    ang = jnp.arange(tp, dtype=F32)[:, None] * inv_freq[None, :]
    c, s = jnp.cos(ang), jnp.sin(ang)
    return jnp.concatenate([c, c, -s, s], axis=1)


def _local_step(x, target, meta, ln0_g, ln0_b, layer_w, t_pad, emit):
    seq = x.shape[0]
    t_real = N_META + seq
    zpad = jnp.zeros((t_pad - t_real, D_MODEL), F32)
    xin = jnp.concatenate([meta, x, zpad], axis=0)
    tgt = jnp.concatenate([jnp.zeros((N_META, D_MODEL), F32), target, zpad], axis=0)
    cs = _rope_table(t_pad)
    h = _ln_fwd([(1.0, xin)], ln0_g, ln0_b, "ln0")
    saved = []
    for l in range(DEPTH):
        w_in, rest_of_weights = layer_w[l](h)
        h, s = _layer_fwd(h, w_in, rest_of_weights, cs, t_real, "l%d_" % l)
        saved.append(s)
    dy, lossvec = _loss_head(h, tgt, t_real, "loss_head")
    terms, tok = [(1.0, dy)], None
    for l in reversed(range(DEPTH)):
        terms, tok = _layer_bwd(terms, saved[l], cs, t_real, "l%d_" % l,
                                functools.partial(lambda stage, g, l: emit(l, stage, _layer_grads(g)), l=l), tok)
    dxin, dg0, db0 = _ln_bwd(terms, [(1.0, xin)], _after(ln0_g, tok), "ln0_bwd")
    emit(None, 'head', {'meta_tokens': dxin[:N_META], 'ln0_g': dg0.reshape(-1), 'ln0_b': db0.reshape(-1), 'loss': lossvec})
    return dxin[N_META:t_real]


_HBM = pl.BlockSpec(memory_space=pltpu.HBM)
_SEM = pl.BlockSpec(memory_space=pltpu.SEMAPHORE)
_SIDE_EFFECT = pltpu.SideEffectType.DATAFLOW_SIDE_EFFECTING


def _peer_copies(src_refs, land_refs, scatters, send_sems, recv_sems):
    x, y, c = lax.axis_index("x"), lax.axis_index("y"), lax.axis_index("c")
    me = 4 * x + 2 * y + c
    copies = []
    for k in range(1, N_DEV):
        px = 1 - x if k & 4 else x
        py = 1 - y if k & 2 else y
        pc = 1 - c if k & 1 else c
        for t, (src, land) in enumerate(zip(src_refs, land_refs)):
            copies.append(pltpu.make_async_remote_copy(
                src_ref=src.at[4 * px + 2 * py + pc] if scatters[t] else src, dst_ref=land.at[me],
                send_sem=send_sems.at[7 * t + k - 1], recv_sem=recv_sems.at[7 * t + k - 1],
                device_id=(px, py, pc), device_id_type=pl.DeviceIdType.MESH))
    return me, copies


def _own_block_in_place(land, own):
    me = 4 * lax.axis_index("x") + 2 * lax.axis_index("y") + lax.axis_index("c")
    return lax.dynamic_update_slice_in_dim(land, own, me, 0)


def _gather_two_level(shards, name):
    nt = len(shards)

    def body(*refs):
        x_refs, out_refs = refs[:nt], refs[nt:2 * nt]
        token_ref, send_sems, recv_sems = refs[2 * nt:]
        x, y, c = lax.axis_index("x"), lax.axis_index("y"), lax.axis_index("c")
        me, sibling = (x, y, c), (x, y, 1 - c)
        chips = [(1 - x, y), (x, 1 - y), (1 - x, 1 - y)]

        def copy(t, k, block, to, own=False):
            px, py, pc = block
            slot = out_refs[t].at[4 * px + 2 * py + pc]
            return pltpu.make_async_remote_copy(
                src_ref=x_refs[t] if own else slot, dst_ref=slot,
                send_sem=send_sems.at[7 * t + k], recv_sem=recv_sems.at[7 * t + k],
                device_id=to, device_id_type=pl.DeviceIdType.MESH)

        sent = []
        for t in range(nt):
            first = [copy(t, 1 + j, me, (*chip, c), own=True) for j, chip in enumerate(chips)]
            first.append(copy(t, 0, me, sibling, own=True))
            for cp in first:
                cp.start()
            sent += first
        token_ref[...] = jnp.zeros_like(token_ref)
        for j, chip in enumerate(chips):
            for t in range(nt):
                copy(t, 1 + j, (*chip, c), me).wait_recv()
                passed = copy(t, 4 + j, (*chip, c), sibling)
                passed.start()
                sent.append(passed)
        for t in range(nt):
            copy(t, 0, sibling, me).wait_recv()
            for j, chip in enumerate(chips):
                copy(t, 4 + j, (*chip, 1 - c), me).wait_recv()
        for cp in sent:
            cp.wait_send()

    any_space = pl.BlockSpec(memory_space=pl.ANY)
    res = pl.pallas_call(
        body, name=name,
        out_shape=[jax.ShapeDtypeStruct((N_DEV,) + a.shape, a.dtype) for a in shards] + [jax.ShapeDtypeStruct((SUBLANE, LANE), F32)],
        in_specs=[any_space] * nt, out_specs=[any_space] * nt + [pl.BlockSpec(memory_space=pltpu.VMEM)],
        scratch_shapes=[pltpu.SemaphoreType.DMA((7 * nt,)), pltpu.SemaphoreType.DMA((7 * nt,))],
    )(*shards)
    return [_own_block_in_place(land, a[None]) for land, a in zip(res[:nt], shards)], res[nt][0, 0]


def _exchange_start(groups, name):
    flat = [it for grp in groups for it in grp]
    nt, ng = len(flat), len(groups)
    scatters = [sc for _, sc in flat]
    srcs = [pltpu.with_memory_space_constraint(a, pltpu.HBM) for a, _ in flat]
    land_shapes = [a.shape if sc else (N_DEV,) + a.shape for a, sc in flat]
    lands = [pltpu.with_memory_space_constraint(lax.empty(s, a.dtype), pltpu.HBM) for s, (a, _) in zip(land_shapes, flat)]
    bounds = [0]
    for grp in groups:
        bounds.append(bounds[-1] + len(grp))

    def body(*refs):
        src_refs, land_refs = refs[:nt], refs[nt:2 * nt]
        sem_refs = refs[2 * nt:2 * nt + 2 * ng]
        token_ref = refs[4 * nt + 2 * ng]
        for gi in range(ng):
            lo, hi = bounds[gi], bounds[gi + 1]
            _, copies = _peer_copies(src_refs[lo:hi], land_refs[lo:hi], scatters[lo:hi], sem_refs[2 * gi], sem_refs[2 * gi + 1])
            for cp in copies:
                cp.start()
        token_ref[...] = jnp.zeros_like(token_ref)

    out_shape = []
    for grp in groups:
        out_shape += [pltpu.SemaphoreType.DMA((7 * len(grp),)), pltpu.SemaphoreType.DMA((7 * len(grp),))]
    out_shape += [pltpu.HBM(a.shape, a.dtype) for a in srcs] + [pltpu.HBM(s, a.dtype) for s, a in zip(land_shapes, srcs)]
    out_shape += [jax.ShapeDtypeStruct((SUBLANE, LANE), F32)]
    res = pl.pallas_call(
        body, name=name, out_shape=out_shape,
        in_specs=[_HBM] * (2 * nt),
        out_specs=[_SEM] * (2 * ng) + [_HBM] * (2 * nt) + [pl.BlockSpec(memory_space=pltpu.VMEM)],
        input_output_aliases={t: 2 * ng + t for t in range(2 * nt)},
        compiler_params=pltpu.CompilerParams(has_side_effects=_SIDE_EFFECT),
    )(*srcs, *lands)
    sems, thru, token = res[:2 * ng], res[2 * ng:2 * ng + 2 * nt], res[-1]
    states = []
    for gi in range(ng):
        lo, hi = bounds[gi], bounds[gi + 1]
        states.append((sems[2 * gi], sems[2 * gi + 1], thru[lo:hi], thru[nt + lo:nt + hi], scatters[lo:hi]))
    return states, token[0, 0]


def _exchange_wait(state, after, name):
    send_sems, recv_sems, srcs, lands, scatters = state
    n = len(srcs)

    def body(*refs):
        _, copies = _peer_copies(refs[:n], refs[n:2 * n], scatters, refs[2 * n], refs[2 * n + 1])
        for cp in copies:
            cp.wait_send()
        for cp in copies:
            cp.wait_recv()

    res = pl.pallas_call(
        body, name=name,
        out_shape=[pltpu.HBM(a.shape, a.dtype) for a in srcs] + [pltpu.HBM(a.shape, a.dtype) for a in lands],
        in_specs=[_HBM] * (2 * n) + [_SEM, _SEM, _HBM],
        out_specs=[_HBM] * (2 * n),
        input_output_aliases={t: t for t in range(2 * n)},
        compiler_params=pltpu.CompilerParams(has_side_effects=_SIDE_EFFECT),
    )(*srcs, *lands, send_sems, recv_sems, pltpu.with_memory_space_constraint(after, pltpu.HBM))
    me = 4 * lax.axis_index("x") + 2 * lax.axis_index("y") + lax.axis_index("c")
    out = []
    for src, land, sc in zip(res[:n], res[n:], scatters):
        own = lax.dynamic_index_in_dim(src, me, 0, keepdims=True) if sc else src[None]
        out.append(lax.dynamic_update_slice_in_dim(land, own, me, 0))
    return out


def _as_rows(shape):
    return (1, shape[0]) if len(shape) == 1 else (math.prod(shape[:-1]), shape[-1])


def _sum_adamw(pieces, w, m, v, name):
    shape = w.shape
    nl = len(pieces)
    if nl > 1 and _as_rows(shape[1:])[0] % 16:
        pieces, nl = [jnp.stack(pieces, axis=1)], 1
    rows, cols = _as_rows(shape)
    rl = rows // nl
    cap = max(16, (1 << 18) // cols // 16 * 16)
    tr = _tile(rl, cap, 16)
    nb = rl // tr
    c1 = 1.0 / (1.0 - ADAM_B1 ** ADAM_STEP)
    c2 = 1.0 / (1.0 - ADAM_B2 ** ADAM_STEP)

    def body(*refs):
        p_refs = refs[:nl]
        w_ref, m_ref, v_ref, g_ref, d_ref, nm_ref, nv_ref = refs[nl:]
        li = pl.program_id(0)

        def total(p_ref):
            acc = p_ref[0].astype(F32)
            for k in range(1, N_DEV):
                acc = acc + p_ref[k].astype(F32)
            return acc

        gg = total(p_refs[0])
        for l in range(1, nl):
            gg = jnp.where(li == l, total(p_refs[l]), gg)
        nm = ADAM_B1 * m_ref[...] + (1.0 - ADAM_B1) * gg
        nv = ADAM_B2 * v_ref[...] + (1.0 - ADAM_B2) * (gg * gg)
        g_ref[...] = gg
        d_ref[...] = -ADAM_LR * ((nm * c1) / (jnp.sqrt(nv * c2) + ADAM_EPS) + ADAM_WD * w_ref[...])
        nm_ref[...] = nm
        nv_ref[...] = nv

    blk = pl.BlockSpec((tr, cols), lambda li, i: (li * nb + i, 0))
    p_specs = [pl.BlockSpec((N_DEV, tr, cols), functools.partial(lambda li, i, l: (0, jnp.where(li == l, i, 0), 0), l=l))
               for l in range(nl)]
    res = pl.pallas_call(
        body, name=name, grid=(nl, nb),
        in_specs=p_specs + [blk] * 3, out_specs=[blk] * 4,
        out_shape=[jax.ShapeDtypeStruct((rows, cols), F32)] * 4,
        compiler_params=_cparams("parallel", "parallel"),
    )(*[p.reshape(N_DEV, rl, cols) for p in pieces], *[a.reshape(rows, cols) for a in (w, m, v)])
    return [r.reshape(shape) for r in res]


def _to_shards(full, axis):
    shp = full.shape
    a = full.reshape(shp[:axis] + (N_DEV, shp[axis] // N_DEV) + shp[axis + 1:])
    return jnp.moveaxis(a, axis, 0)


def _from_shards(blocks, axis):
    a = jnp.moveaxis(blocks, 0, axis)
    shp = a.shape
    return a.reshape(shp[:axis] + (shp[axis] * shp[axis + 1],) + shp[axis + 2:])


def kernel(x, meta_tokens, ln0_g, ln0_b, w_in, q_norm, kv_norm, w_uq, w_uk, w_uv, w_o_mla, lru_conv_w, lru_conv_b, w_rg, b_rg, w_ig, b_ig, lru_lambda, w_o_lru, w_out, ln1_g, ln1_b, w_up, ffn_conv_w, ffn_conv_b, w_down, ln2_g, ln2_b, loss_target, m_meta_tokens, m_ln0_g, m_ln0_b, m_w_in, m_q_norm, m_kv_norm, m_w_uq, m_w_uk, m_w_uv, m_w_o_mla, m_lru_conv_w, m_lru_conv_b, m_w_rg, m_b_rg, m_w_ig, m_b_ig, m_lru_lambda, m_w_o_lru, m_w_out, m_ln1_g, m_ln1_b, m_w_up, m_ffn_conv_w, m_ffn_conv_b, m_w_down, m_ln2_g, m_ln2_b, v_meta_tokens, v_ln0_g, v_ln0_b, v_w_in, v_q_norm, v_kv_norm, v_w_uq, v_w_uk, v_w_uv, v_w_o_mla, v_lru_conv_w, v_lru_conv_b, v_w_rg, v_b_rg, v_w_ig, v_b_ig, v_lru_lambda, v_w_o_lru, v_w_out, v_ln1_g, v_ln1_b, v_w_up, v_ffn_conv_w, v_ffn_conv_b, v_w_down, v_ln2_g, v_ln2_b):
    args = (meta_tokens, ln0_g, ln0_b, w_in, q_norm, kv_norm, w_uq, w_uk, w_uv, w_o_mla, lru_conv_w, lru_conv_b, w_rg, b_rg, w_ig, b_ig, lru_lambda, w_o_lru, w_out, ln1_g, ln1_b, w_up, ffn_conv_w, ffn_conv_b, w_down, ln2_g, ln2_b)
    ms = (m_meta_tokens, m_ln0_g, m_ln0_b, m_w_in, m_q_norm, m_kv_norm, m_w_uq, m_w_uk, m_w_uv, m_w_o_mla, m_lru_conv_w, m_lru_conv_b, m_w_rg, m_b_rg, m_w_ig, m_b_ig, m_lru_lambda, m_w_o_lru, m_w_out, m_ln1_g, m_ln1_b, m_w_up, m_ffn_conv_w, m_ffn_conv_b, m_w_down, m_ln2_g, m_ln2_b)
    vs = (v_meta_tokens, v_ln0_g, v_ln0_b, v_w_in, v_q_norm, v_kv_norm, v_w_uq, v_w_uk, v_w_uv, v_w_o_mla, v_lru_conv_w, v_lru_conv_b, v_w_rg, v_b_rg, v_w_ig, v_b_ig, v_lru_lambda, v_w_o_lru, v_w_out, v_ln1_g, v_ln1_b, v_w_up, v_ffn_conv_w, v_ffn_conv_b, v_w_down, v_ln2_g, v_ln2_b)
    wd, md, vd = dict(zip(WEIGHTS, args)), dict(zip(WEIGHTS, ms)), dict(zip(WEIGHTS, vs))

    def shard_axis(n, l):
        return SHARD_AXIS[n] - (0 if l is None else 1)

    def shard(n, l):
        a = wd[n] if l is None else wd[n][l]
        if n in SENT_TRANSPOSED:
            a = a.T
        return a.astype(BF16) if n in BIG else a

    def whole(keys, landed):
        return {k: b.reshape(-1, b.shape[-1]) if k[0] in SENT_TRANSPOSED else _from_shards(b, shard_axis(*k))
                for k, b in zip(keys, landed)}

    first = [('meta_tokens', None), ('w_in', 0)]
    landed, token = _gather_two_level([shard(*k) for k in first], "gather_first")
    got_first = whole(first, landed)
    staged = [[(n, 0) for n in names] for names in STAGE_WEIGHTS]
    later = [(n, 1) for n in SHARDED if n != 'meta_tokens']
    groups = [[(shard(*k), False) for k in keys] for keys in staged + [later]]
    groups[0][0] = (_after(groups[0][0][0], token), False)
    gather, token = _exchange_start(groups, "gather_start")

    def arrive(gi, keys, after, name):
        return whole(keys, _exchange_wait(gather[gi], after, name))

    def layer_weights(got, l, names):
        fl = {n: wd[n][l] for n in names if n in REPLICATED}
        fl.update({n: a for (n, _), a in got.items() if n in names})
        return _layer_weights(fl)

    ln0_g = _after(wd['ln0_g'], token)

    def first_layer(h):
        def more(stage, after):
            got = arrive(stage, staged[stage], after, "gather_wait_l0_%d" % stage)
            return layer_weights(got, 0, STAGE_WEIGHTS[stage] + STAGE_REPLICATED[stage])
        return _w_in_kernel(got_first['w_in', 0]), more

    def second_layer(h):
        got = arrive(len(staged), later, h, "gather_wait_l1")
        return _w_in_kernel(got['w_in', 1]), lambda stage, after: layer_weights(got, 1, STAGE_WEIGHTS[stage] + STAGE_REPLICATED[stage])

    sent = []
    pending = []

    def send(l, stage, grads):
        for n, g in grads.items():
            if n in SHARD_AXIS:
                g = _to_shards(g, shard_axis(n, l))
                pending.append(((n, l), (g.astype(BF16) if n in BIG else g, True)))
            else:
                pending.append(((n, l), (g.astype(BF16) if n in LARGE_REPLICATED else g, False)))
        if l == DEPTH - 1 and stage != 'in':
            return None
        (state,), tok = _exchange_start([[it for _, it in pending]], "grads_start_%s_%s" % (l, stage))
        sent.append(([k for k, _ in pending], state))
        pending.clear()
        return tok

    seq = x.shape[1]
    t_pad = -(-(N_META + seq + MIN_PAD_ROWS) // LANE) * LANE
    grad_x = _local_step(x[0], loss_target[0], got_first['meta_tokens', None], ln0_g, wd['ln0_b'],
                         [first_layer, second_layer], t_pad, send)

    pieces, outs, after = {}, {}, grad_x
    for gi, (keys, state) in enumerate(sent):
        pieces.update(zip(keys, _exchange_wait(state, after, "grads_wait_%d" % gi)))
        if gi == len(sent) - 2:
            for n in WEIGHTS:
                if (n, 0) in pieces:
                    outs[n] = _sum_adamw([pieces[n, l] for l in range(DEPTH)], wd[n], md[n], vd[n], "adamw_" + n)
                    after = outs[n][1]
    loss = jnp.sum(pieces['loss', None])
    for n in WEIGHTS:
        if (n, None) in pieces:
            outs[n] = _sum_adamw([pieces[n, None]], wd[n], md[n], vd[n], "adamw_" + n)
    res = [loss, grad_x[None]]
    for k in range(4):
        res += [outs[n][k] for n in WEIGHTS]
    return tuple(res)
```

```python
import functools
import math

import jax
import jax.numpy as jnp
from jax import lax
from jax.experimental import pallas as pl
from jax.experimental.pallas import tpu as pltpu

F32 = jnp.float32
BF16 = jnp.bfloat16

N_DEV = 8
D_MODEL = 1024
N_META = 16
HEADS = 8
QK_NOPE = 128
QK_ROPE = 64
V_HEAD = 128
Q_RANK = 256
KV_RANK = 128
ROPE_THETA = 10000.0
LRU_BLOCKS = 8
LRU_C = 8.0
D_FF = 2816
DEPTH = 2
DN_ALPHA = (2.0 * DEPTH) ** 0.25
LN_EPS = 1e-5
RMS_EPS = 1e-6
LN2 = math.log(2.0)
ATT_SCALE = 1.0 / math.sqrt(QK_NOPE + QK_ROPE) / LN2
NEG_BIG = -1e30

ADAM_LR = 0.001
ADAM_B1 = 0.9
ADAM_B2 = 0.999
ADAM_EPS = 1e-08
ADAM_WD = 0.01
ADAM_STEP = 10

MIN_PAD_ROWS = 2
LANE = 128
SUBLANE = 8
VMEM_LIMIT = 56 * 1024 * 1024

PROJ_COLS = 4 * D_MODEL + Q_RANK + KV_RANK + 2 * QK_ROPE
C_LRU_G, C_LRU_X, C_G_MLA, C_G_LRU = 0, D_MODEL, 2 * D_MODEL, 3 * D_MODEL
C_CQ = 4 * D_MODEL
C_CKV = C_CQ + Q_RANK
C_KRP = C_CKV + KV_RANK

WEIGHTS = ['meta_tokens', 'ln0_g', 'ln0_b', 'w_in', 'q_norm', 'kv_norm', 'w_uq', 'w_uk', 'w_uv', 'w_o_mla',
           'lru_conv_w', 'lru_conv_b', 'w_rg', 'b_rg', 'w_ig', 'b_ig', 'lru_lambda', 'w_o_lru', 'w_out',
           'ln1_g', 'ln1_b', 'w_up', 'ffn_conv_w', 'ffn_conv_b', 'w_down', 'ln2_g', 'ln2_b']
SHARD_AXIS = {'meta_tokens': 1, 'w_in': 2, 'w_uq': 1, 'w_o_mla': 1, 'lru_conv_w': 2, 'b_rg': 2, 'b_ig': 2,
              'lru_lambda': 2, 'w_o_lru': 1, 'w_out': 1, 'w_up': 2, 'ffn_conv_w': 2, 'w_down': 1}
BIG = ['w_in', 'w_uq', 'w_o_mla', 'w_o_lru', 'w_out', 'w_up', 'w_down']
SHARDED = [n for n in WEIGHTS if n in SHARD_AXIS]
REPLICATED = [n for n in WEIGHTS if n not in SHARD_AXIS]
LARGE_REPLICATED = ['w_uk', 'w_uv', 'w_rg', 'w_ig']
SENT_TRANSPOSED = ['w_in', 'w_up']
STAGE_WEIGHTS = [['w_uq', 'lru_conv_w', 'b_rg', 'b_ig', 'lru_lambda'], ['w_o_mla', 'w_o_lru', 'w_out'], ['w_up', 'ffn_conv_w', 'w_down']]
STAGE_REPLICATED = [['q_norm', 'kv_norm', 'w_uk', 'w_uv', 'lru_conv_b', 'w_rg', 'w_ig'], ['ln1_g', 'ln1_b'], ['ffn_conv_b', 'ln2_g', 'ln2_b']]


def _cparams(*sem):
    return pltpu.CompilerParams(dimension_semantics=sem, vmem_limit_bytes=VMEM_LIMIT)


def _tile(n, cap, unit=LANE):
    best = None
    t = unit
    while t <= min(n, cap):
        if n % t == 0:
            best = t
        t += unit
    return n if best is None else best


def _sigmoid(x):
    return 1.0 / (1.0 + jnp.exp(-x))


_GELU_C = math.sqrt(2.0 / math.pi)


_GELU_A = 0.044715


def _gelu(x):
    t = jnp.tanh(x * (_GELU_C + (_GELU_C * _GELU_A) * (x * x)))
    hx = 0.5 * x
    return hx + hx * t


def _gelu_and_grad(x):
    x2 = x * x
    t = jnp.tanh(x * (_GELU_C + (_GELU_C * _GELU_A) * x2))
    hx = 0.5 * x
    dg = 0.5 + 0.5 * t + (hx * (1.0 - t * t)) * (_GELU_C + (3.0 * _GELU_C * _GELU_A) * x2)
    return hx + hx * t, dg


def _softplus_neg(lam):
    z = jnp.exp(-jnp.abs(lam))
    w = 1.0 + z
    log1p = jnp.where(w == 1.0, z, jnp.log(w) * z / (w - 1.0))
    return jnp.maximum(-lam, 0.0) + log1p


def _row_ids(shape, row0=0):
    return lax.broadcasted_iota(jnp.int32, shape, 0) + row0


def _matmul(a, b, name, ta=False, tb=False, out_dtype=F32, tm_cap=1408, tn_cap=1024, tk_cap=2048):
    if ta:
        kdim, m = a.shape
    else:
        m, kdim = a.shape
    if tb:
        n, k2 = b.shape
    else:
        k2, n = b.shape
    assert kdim == k2, (a.shape, b.shape, ta, tb)
    tm, tn, tk = _tile(m, tm_cap), _tile(n, tn_cap), _tile(kdim, tk_cap)
    nk = kdim // tk

    def body(a_ref, b_ref, o_ref, *acc):
        dn = (((0 if ta else 1,), (1 if tb else 0,)), ((), ()))
        part = lax.dot_general(a_ref[...].astype(BF16), b_ref[...].astype(BF16), dn, preferred_element_type=F32)
        if nk == 1:
            o_ref[...] = part.astype(o_ref.dtype)
            return
        acc_ref, k = acc[0], pl.program_id(2)

        @pl.when(k == 0)
        def _():
            acc_ref[...] = part

        @pl.when(k > 0)
        def _():
            acc_ref[...] += part

        @pl.when(k == nk - 1)
        def _():
            o_ref[...] = acc_ref[...].astype(o_ref.dtype)

    a_spec = pl.BlockSpec((tk, tm), lambda i, j, k: (k, i)) if ta else pl.BlockSpec((tm, tk), lambda i, j, k: (i, k))
    b_spec = pl.BlockSpec((tn, tk), lambda i, j, k: (j, k)) if tb else pl.BlockSpec((tk, tn), lambda i, j, k: (k, j))
    return pl.pallas_call(
        body, name=name,
        grid=(m // tm, n // tn, nk),
        in_specs=[a_spec, b_spec],
        out_specs=pl.BlockSpec((tm, tn), lambda i, j, k: (i, j)),
        out_shape=jax.ShapeDtypeStruct((m, n), out_dtype),
        scratch_shapes=[pltpu.VMEM((tm, tn), F32)] if nk > 1 else [],
        compiler_params=_cparams("parallel", "parallel", "arbitrary"),
    )(a, b)


class Rw:
    def __init__(self, arr, width=None, cb=0):
        self.arr, self.width, self.cb = arr, (arr.shape[1] if width is None else width), cb


class Pm:
    def __init__(self, arr):
        self.arr = arr


class Into:
    def __init__(self, arr, col0, width):
        self.arr, self.col0, self.width = arr, col0, width


def _call_with_into(body, name, grid, in_specs, operands, outs, spec_of, shape_of, extra_out_specs, extra_out_shape, sem):
    intos = [(k, o) for k, o in enumerate(outs) if isinstance(o, Into)]
    aliases = {len(operands) + n: k for n, (k, _) in enumerate(intos)}
    return pl.pallas_call(
        body, name=name, grid=grid,
        in_specs=in_specs + [pl.BlockSpec(memory_space=pl.ANY)] * len(intos),
        out_specs=[spec_of(o) for o in outs] + extra_out_specs,
        out_shape=[jax.ShapeDtypeStruct(o.arr.shape, o.arr.dtype) if isinstance(o, Into) else shape_of(o) for o in outs]
        + extra_out_shape,
        input_output_aliases=aliases,
        compiler_params=_cparams(sem),
    )(*operands, *[o.arr for _, o in intos])


def _rows(fn, name, ins, outs, accs=(), tm_cap=384):
    tp = next(o.arr.shape[0] for o in ins if isinstance(o, Rw))
    tm = _tile(tp, tm_cap)
    n_in, n_out, n_acc = len(ins), len(outs), len(accs)
    n_into = sum(isinstance(o, Into) for o in outs)

    def body(*refs):
        i = pl.program_id(0)
        res = fn(i * tm, *[r[...] for r in refs[:n_in]])
        if not isinstance(res, (tuple, list)):
            res = (res,)
        assert len(res) == n_out + n_acc, (name, len(res))
        out_refs = refs[n_in + n_into:]
        for k in range(n_out):
            out_refs[k][...] = res[k].astype(out_refs[k].dtype)
        for k in range(n_acc):
            ref = out_refs[n_out + k]

            @pl.when(i == 0)
            def _():
                ref[...] = jnp.zeros_like(ref)

            ref[...] += res[n_out + k]

    in_specs = []
    for o in ins:
        if isinstance(o, Rw):
            in_specs.append(pl.BlockSpec((tm, o.width), functools.partial(lambda i, cb: (i, cb), cb=o.cb)))
        else:
            in_specs.append(pl.BlockSpec(o.arr.shape, functools.partial(lambda i, nd: (0,) * nd, nd=o.arr.ndim)))

    def spec_of(o):
        if isinstance(o, Into):
            assert o.col0 % o.width == 0, (name, o.col0, o.width)
            return pl.BlockSpec((tm, o.width), functools.partial(lambda i, cb: (i, cb), cb=o.col0 // o.width))
        return pl.BlockSpec((tm, o[0]), lambda i: (i, 0))

    return _call_with_into(
        body, name, (tp // tm,), in_specs, [o.arr for o in ins], list(outs), spec_of,
        lambda o: jax.ShapeDtypeStruct((tp, o[0]), o[1]),
        [pl.BlockSpec(s, functools.partial(lambda i, nd: (0,) * nd, nd=len(s))) for s in accs],
        [jax.ShapeDtypeStruct(s, F32) for s in accs], "arbitrary")


class Cl:
    def __init__(self, arr, col0=0):
        self.arr, self.col0 = arr, col0


def _cols(fn, name, ins, outs, ncols, tc):
    assert ncols % tc == 0
    n_in, n_out = len(ins), len(outs)
    n_into = sum(isinstance(o, Into) for o in outs)

    def body(*refs):
        res = fn(*[r[...] for r in refs[:n_in]])
        if not isinstance(res, (tuple, list)):
            res = (res,)
        assert len(res) == n_out, (name, len(res))
        out_refs = refs[n_in + n_into:]
        for k in range(n_out):
            out_refs[k][...] = res[k].astype(out_refs[k].dtype)

    in_specs = []
    for o in ins:
        assert o.col0 % tc == 0, (name, o.col0, tc)
        in_specs.append(pl.BlockSpec((o.arr.shape[0], tc), functools.partial(lambda j, off: (0, j + off), off=o.col0 // tc)))

    def spec_of(o):
        if isinstance(o, Into):
            assert o.col0 % tc == 0 and o.width == ncols, (name, o.col0, o.width)
            return pl.BlockSpec((o.arr.shape[0], tc), functools.partial(lambda j, off: (0, j + off), off=o.col0 // tc))
        return pl.BlockSpec((o[0], tc), lambda j: (0, j))

    return _call_with_into(body, name, (ncols // tc,), in_specs, [o.arr for o in ins], list(outs), spec_of,
                           lambda o: jax.ShapeDtypeStruct((o[0], ncols), o[1]), [], [], "parallel")


def _ln_stats(u):
    mu = jnp.mean(u, axis=-1, keepdims=True)
    xc = u - mu
    var = jnp.mean(xc * xc, axis=-1, keepdims=True)
    rstd = lax.rsqrt(var + LN_EPS)
    return xc * rstd, rstd


def _ln_fwd(terms, g, b, name):
    coefs = [c for c, _ in terms]

    def fn(row0, *blk):
        xs, (gg, bb) = blk[:len(coefs)], blk[len(coefs):]
        u = sum(c * x for c, x in zip(coefs, xs))
        xhat, _ = _ln_stats(u)
        return xhat * gg + bb

    d = terms[0][1].shape[1]
    return _rows(fn, name, [Rw(x) for _, x in terms] + [Pm(g.reshape(1, d)), Pm(b.reshape(1, d))], [(d, F32)])[0]


def _ln_bwd(dy_terms, u_terms, g, name):
    dc = [c for c, _ in dy_terms]
    uc = [c for c, _ in u_terms]
    d = u_terms[0][1].shape[1]

    def fn(row0, *blk):
        dys = blk[:len(dc)]
        xs = blk[len(dc):len(dc) + len(uc)]
        gg = blk[-1]
        dy = sum(c * x for c, x in zip(dc, dys))
        u = sum(c * x for c, x in zip(uc, xs))
        xhat, rstd = _ln_stats(u)
        gdy = dy * gg
        m1 = jnp.mean(gdy, axis=-1, keepdims=True)
        m2 = jnp.mean(gdy * xhat, axis=-1, keepdims=True)
        du = rstd * (gdy - m1 - xhat * m2)
        return du, jnp.sum(dy * xhat, axis=0, keepdims=True), jnp.sum(dy, axis=0, keepdims=True)

    ins = [Rw(x) for _, x in dy_terms] + [Rw(x) for _, x in u_terms] + [Pm(g.reshape(1, d))]
    return _rows(fn, name, ins, [(d, F32)], accs=[(1, d), (1, d)])


def _loss_head(y, tgt, t_real, name):
    d = y.shape[1]

    def fn(row0, yb, tb):
        rows = _row_ids(yb.shape, row0)
        live = (rows >= N_META) & (rows < t_real)
        diff = jnp.where(live, yb - tb, 0.0)
        return diff * (1.0 / d), jnp.sum(diff * diff, axis=0, keepdims=True) * (0.5 / d)

    return _rows(fn, name, [Rw(y), Rw(tgt)], [(d, F32)], accs=[(1, d)])


def _rms(x, g):
    r = lax.rsqrt(jnp.mean(x * x, axis=-1, keepdims=True) + RMS_EPS)
    return x * r * g


def _rms_bwd(dy, x, g):
    r = lax.rsqrt(jnp.mean(x * x, axis=-1, keepdims=True) + RMS_EPS)
    gdy = dy * g
    dx = r * gdy - x * (r * r * r) * jnp.mean(gdy * x, axis=-1, keepdims=True)
    return dx, jnp.sum(dy * x * r, axis=0, keepdims=True)


def _mla_norms(proj, qn, kvn, name):
    def fn(row0, cq, ckv, g1, g2):
        return _rms(cq, g1), _rms(ckv, g2)

    return _rows(fn, name, [Rw(proj, Q_RANK, C_CQ // Q_RANK), Rw(proj, KV_RANK, C_CKV // KV_RANK),
                            Pm(qn.reshape(1, Q_RANK)), Pm(kvn.reshape(1, KV_RANK))],
                 [(Q_RANK, BF16), (KV_RANK, BF16)])


def _mla_norms_bwd(dcqn, dckvn, dkrp, proj, qn, kvn, dproj, name):
    def fn(row0, d1, d2, dkr, cq, ckv, g1, g2):
        dx1, dg1 = _rms_bwd(d1, cq, g1)
        dx2, dg2 = _rms_bwd(d2, ckv, g2)
        return jnp.concatenate([dx1, dx2, dkr], axis=1), dg1, dg2

    return _rows(fn, name, [Rw(dcqn), Rw(dckvn), Rw(dkrp), Rw(proj, Q_RANK, C_CQ // Q_RANK), Rw(proj, KV_RANK, C_CKV // KV_RANK),
                            Pm(qn.reshape(1, Q_RANK)), Pm(kvn.reshape(1, KV_RANK))],
                 [Into(dproj, C_CQ, PROJ_COLS - C_CQ)], accs=[(1, Q_RANK), (1, KV_RANK)])


def _fold_rope(z):
    return z + pltpu.roll(z, QK_ROPE, 1)


def _mla_pack(qext, kv, proj, cs, name):
    tp = qext.shape[0]
    tm = _tile(tp, 384)
    hw, nope_all = 2 * LANE, HEADS * QK_NOPE

    def body(q_ref, kv_ref, kr_ref, cs_ref, qo_ref, ko_ref, vo_ref):
        cs_ = cs_ref[...]
        low = lax.broadcasted_iota(jnp.int32, cs_.shape, 1) < QK_ROPE
        kr = _fold_rope(kr_ref[...] * cs_).astype(BF16)
        for h in range(HEADS):
            qr = jnp.where(low, _fold_rope(q_ref[:, h * hw + QK_NOPE:(h + 1) * hw] * cs_), 0.0)
            qo_ref[:, h * hw:h * hw + QK_NOPE] = (q_ref[:, h * hw:h * hw + QK_NOPE] * ATT_SCALE).astype(BF16)
            qo_ref[:, h * hw + QK_NOPE:(h + 1) * hw] = (qr * ATT_SCALE).astype(BF16)
            ko_ref[:, h * hw:h * hw + QK_NOPE] = kv_ref[:, h * QK_NOPE:(h + 1) * QK_NOPE].astype(BF16)
            ko_ref[:, h * hw + QK_NOPE:(h + 1) * hw] = kr
        vo_ref[...] = kv_ref[:, nope_all:].astype(BF16)

    row = lambda w: pl.BlockSpec((tm, w), lambda i: (i, 0))
    return pl.pallas_call(
        body, name=name, grid=(tp // tm,),
        in_specs=[row(HEADS * hw), row(2 * nope_all), pl.BlockSpec((tm, LANE), lambda i: (i, C_KRP // LANE)), row(LANE)],
        out_specs=[row(HEADS * hw), row(HEADS * hw), row(nope_all)],
        out_shape=[jax.ShapeDtypeStruct((tp, HEADS * hw), BF16),
                   jax.ShapeDtypeStruct((tp, HEADS * hw), BF16),
                   jax.ShapeDtypeStruct((tp, nope_all), BF16)],
        compiler_params=_cparams("parallel"),
    )(qext, kv, proj, cs)


def _mla_unpack(dk, dv, cs, name):
    tp = dk.shape[0]
    tm = _tile(tp, 384)
    hw, nope_all = 2 * LANE, HEADS * QK_NOPE

    def body(dk_ref, dv_ref, cs_ref, dkv_ref, dkr_ref):
        cs_ = cs_ref[...]
        low = lax.broadcasted_iota(jnp.int32, cs_.shape, 1) < QK_ROPE
        dkr = None
        for h in range(HEADS):
            dkv_ref[:, h * QK_NOPE:(h + 1) * QK_NOPE] = dk_ref[:, h * hw:h * hw + QK_NOPE].astype(BF16)
            part = jnp.where(low, dk_ref[:, h * hw + QK_NOPE:(h + 1) * hw], 0.0)
            dkr = part if h == 0 else dkr + part
        dkv_ref[:, nope_all:] = dv_ref[...].astype(BF16)
        dkr_ref[...] = _fold_rope(dkr) * cs_

    row = lambda w: pl.BlockSpec((tm, w), lambda i: (i, 0))
    return pl.pallas_call(
        body, name=name, grid=(tp // tm,),
        in_specs=[row(HEADS * hw), row(nope_all), row(LANE)],
        out_specs=[row(2 * nope_all), row(LANE)],
        out_shape=[jax.ShapeDtypeStruct((tp, 2 * nope_all), BF16), jax.ShapeDtypeStruct((tp, LANE), F32)],
        compiler_params=_cparams("parallel"),
    )(dk, dv, cs)


def _attn_fwd(q, k, v, t_real, name):
    tp = q.shape[0]
    tq = _tile(tp, 1408)
    tkc = _tile(tp, 1408)
    nkc = -(-t_real // tkc)

    def body(q_ref, k_ref, v_ref, o_ref, lse_ref):
        qb = q_ref[...]
        m = l = acc = None
        for c in range(nkc):
            s = lax.dot_general(qb, k_ref[c * tkc:(c + 1) * tkc, :], (((1,), (1,)), ((), ())), preferred_element_type=F32)
            if (c + 1) * tkc > t_real:
                cols = lax.broadcasted_iota(jnp.int32, s.shape, 1) + c * tkc
                s = jnp.where(cols < t_real, s, NEG_BIG)
            mc = jnp.max(s, axis=-1, keepdims=True)
            m_new = mc if c == 0 else jnp.maximum(m, mc)
            p = jnp.exp2(s - m_new)
            lc = jnp.sum(p, axis=-1, keepdims=True)
            pv = jnp.dot(p.astype(BF16), v_ref[c * tkc:(c + 1) * tkc, :], preferred_element_type=F32)
            if c == 0:
                l, acc = lc, pv
            else:
                alpha = jnp.exp2(m - m_new)
                l, acc = alpha * l + lc, alpha * acc + pv
            m = m_new
        o_ref[...] = acc / l
        lse_ref[...] = m + jnp.log2(l)

    return pl.pallas_call(
        body, name=name, grid=(HEADS, tp // tq),
        in_specs=[pl.BlockSpec((tq, 2 * LANE), lambda h, i: (i, h)),
                  pl.BlockSpec((tp, 2 * LANE), lambda h, i: (0, h)),
                  pl.BlockSpec((tp, LANE), lambda h, i: (0, h))],
        out_specs=[pl.BlockSpec((tq, LANE), lambda h, i: (i, h)),
                   pl.BlockSpec((None, tq, 1), lambda h, i: (h, i, 0))],
        out_shape=[jax.ShapeDtypeStruct((tp, HEADS * LANE), F32),
                   jax.ShapeDtypeStruct((HEADS, tp, 1), F32)],
        compiler_params=_cparams("parallel", "parallel"),
    )(q, k, v)


def _attn_bwd(q, k, v, do, o, lse, cs, t_real, name):
    tp = q.shape[0]
    tq = _tile(tp, 1408)
    tkc = _tile(tp, 704, 64)
    nkc = -(-t_real // tkc)

    def body(q_ref, k_ref, v_ref, do_ref, o_ref, lse_ref, cs_ref, dq_ref, dk_ref, dv_ref):
        i = pl.program_id(1)

        @pl.when(i == 0)
        def _():
            dk_ref[...] = jnp.zeros_like(dk_ref)
            dv_ref[...] = jnp.zeros_like(dv_ref)

        qb = q_ref[...]
        dob = do_ref[...]
        dob16 = dob.astype(BF16)
        dol2 = (dob * LN2).astype(BF16)
        delta = jnp.sum(dob * o_ref[...], axis=-1, keepdims=True) * LN2
        lse = lse_ref[...]
        dq = None
        for c in range(nkc):
            ks = slice(c * tkc, (c + 1) * tkc)
            kb = k_ref[ks, :]
            s = lax.dot_general(qb, kb, (((1,), (1,)), ((), ())), preferred_element_type=F32)
            p = jnp.exp2(s - lse)
            if (c + 1) * tkc > t_real:
                cols = lax.broadcasted_iota(jnp.int32, s.shape, 1) + c * tkc
                p = jnp.where(cols < t_real, p, 0.0)
            dp = lax.dot_general(dol2, v_ref[ks, :], (((1,), (1,)), ((), ())), preferred_element_type=F32)
            ds = (p * (dp - delta)).astype(BF16)
            dqc = jnp.dot(ds, kb, preferred_element_type=F32)
            dq = dqc if c == 0 else dq + dqc
            dk_ref[ks, :] += lax.dot_general(ds, qb, (((0,), (0,)), ((), ())), preferred_element_type=F32)
            dv_ref[ks, :] += lax.dot_general(p.astype(BF16), dob16, (((0,), (0,)), ((), ())), preferred_element_type=F32)
        cs_ = cs_ref[...]
        low = lax.broadcasted_iota(jnp.int32, cs_.shape, 1) < QK_ROPE
        dq = dq * ATT_SCALE
        dq_ref[:, :QK_NOPE] = dq[:, :QK_NOPE].astype(BF16)
        dq_ref[:, QK_NOPE:] = (_fold_rope(jnp.where(low, dq[:, QK_NOPE:], 0.0)) * cs_).astype(BF16)

    return pl.pallas_call(
        body, name=name, grid=(HEADS, tp // tq),
        in_specs=[pl.BlockSpec((tq, 2 * LANE), lambda h, i: (i, h)),
                  pl.BlockSpec((tp, 2 * LANE), lambda h, i: (0, h)),
                  pl.BlockSpec((tp, LANE), lambda h, i: (0, h)),
                  pl.BlockSpec((tq, LANE), lambda h, i: (i, h)),
                  pl.BlockSpec((tq, LANE), lambda h, i: (i, h)),
                  pl.BlockSpec((None, tq, 1), lambda h, i: (h, i, 0)),
                  pl.BlockSpec((tq, LANE), lambda h, i: (i, 0))],
        out_specs=[pl.BlockSpec((tq, 2 * LANE), lambda h, i: (i, h)),
                   pl.BlockSpec((tp, 2 * LANE), lambda h, i: (0, h)),
                   pl.BlockSpec((tp, LANE), lambda h, i: (0, h))],
        out_shape=[jax.ShapeDtypeStruct((tp, HEADS * 2 * LANE), BF16),
                   jax.ShapeDtypeStruct((tp, HEADS * 2 * LANE), F32),
                   jax.ShapeDtypeStruct((tp, HEADS * LANE), F32)],
        compiler_params=_cparams("parallel", "arbitrary"),
    )(q, k, v, do, o, lse, cs)


def _shift_rows(x, s):
    tp = x.shape[0]
    return x if s % tp == 0 else pltpu.roll(x, s % tp, 0)


def _taps(xm, kk, pad_left):
    return [_shift_rows(xm, pad_left - k) for k in range(kk)]


def _conv_fwd_val(taps, w, b):
    acc = b + w[0:1, :] * taps[0]
    for k in range(1, len(taps)):
        acc = acc + w[k:k + 1, :] * taps[k]
    return acc


def _conv_bwd_val(dy, taps, w, pad_left, live):
    dx = w[0:1, :] * _shift_rows(dy, -pad_left)
    for k in range(1, len(taps)):
        dx = dx + w[k:k + 1, :] * _shift_rows(dy, k - pad_left)
    dw = jnp.concatenate([jnp.sum(dy * t, axis=0, keepdims=True) for t in taps], axis=0)
    return jnp.where(live, dx, 0.0), dw, jnp.sum(dy, axis=0, keepdims=True)


def _lru_conv_fwd(proj, w, b, t_real, name):
    def fn(x, ww, bb):
        xm = jnp.where(_row_ids(x.shape) < t_real, x, 0.0)
        return _conv_fwd_val(_taps(xm, ww.shape[0], 2), ww, bb)

    return _cols(fn, name, [Cl(proj, C_LRU_X), Cl(w), Cl(b.reshape(1, -1))], [(proj.shape[0], F32)], D_MODEL, 128)[0]


def _lru_conv_bwd(dxc, proj, w, dproj, t_real, name):
    def fn(dy, x, ww):
        live = _row_ids(x.shape) < t_real
        xm = jnp.where(live, x, 0.0)
        dym = jnp.where(live, dy, 0.0)
        return _conv_bwd_val(dym, _taps(xm, ww.shape[0], 2), ww, 2, live)

    return _cols(fn, name, [Cl(dxc), Cl(proj, C_LRU_X), Cl(w)],
                 [Into(dproj, C_LRU_X, D_MODEL), (w.shape[0], F32), (1, F32)], D_MODEL, 128)


def _ffn_conv_act(up, w, b, t_real, name):
    def fn(g, v, wg, wv, bg, bv):
        live = _row_ids(g.shape) < t_real
        gc = _conv_fwd_val(_taps(jnp.where(live, g, 0.0), wg.shape[0], 1), wg, bg)
        vc = _conv_fwd_val(_taps(jnp.where(live, v, 0.0), wv.shape[0], 1), wv, bv)
        return _gelu(gc) * vc

    b2 = b.reshape(1, -1)
    return _cols(fn, name, [Cl(up), Cl(up, D_FF), Cl(w), Cl(w, D_FF), Cl(b2), Cl(b2, D_FF)],
                 [(up.shape[0], BF16)], D_FF, 128)[0]


def _ffn_conv_act_bwd(dm, up, w, b, t_real, name):
    tp, kk = up.shape[0], w.shape[0]
    nb = D_FF // LANE
    assert nb >= 2

    def body(dm_ref, g_ref, v_ref, wg_ref, wv_ref, bg_ref, bv_ref, dup_ref, dwg_ref, dwv_ref, dbg_ref, dbv_ref, stage, sems):
        j = pl.program_id(0)
        slot = j % 2

        def copies(step, sl):
            return [pltpu.make_async_copy(stage.at[sl, half],
                                          dup_ref.at[:, pl.ds(pl.multiple_of(half * D_FF + step * LANE, LANE), LANE)],
                                          sems.at[sl, half]) for half in range(2)]

        @pl.when(j >= 2)
        def _():
            for cp in copies(j - 2, slot):
                cp.wait()

        live = _row_ids((tp, LANE)) < t_real
        gm, vm = jnp.where(live, g_ref[...], 0.0), jnp.where(live, v_ref[...], 0.0)
        gt, vt = _taps(gm, kk, 1), _taps(vm, kk, 1)
        gc = _conv_fwd_val(gt, wg_ref[...], bg_ref[...])
        vc = _conv_fwd_val(vt, wv_ref[...], bv_ref[...])
        act, dact = _gelu_and_grad(gc)
        dmm = jnp.where(live, dm_ref[...], 0.0)
        dgx, dwg_ref[...], dbg_ref[...] = _conv_bwd_val(dmm * vc * dact, gt, wg_ref[...], 1, live)
        dvx, dwv_ref[...], dbv_ref[...] = _conv_bwd_val(dmm * act, vt, wv_ref[...], 1, live)
        stage[slot, 0] = dgx.astype(BF16)
        stage[slot, 1] = dvx.astype(BF16)
        for cp in copies(j, slot):
            cp.start()

        @pl.when(j == nb - 1)
        def _():
            for cp in copies(j - 1, 1 - slot) + copies(j, slot):
                cp.wait()

    b2 = b.reshape(1, -1)
    col = lambda rows, off: pl.BlockSpec((rows, LANE), functools.partial(lambda j, o: (0, j + o), o=off))
    return pl.pallas_call(
        body, name=name, grid=(nb,),
        in_specs=[col(tp, 0), col(tp, 0), col(tp, nb), col(kk, 0), col(kk, nb), col(1, 0), col(1, nb)],
        out_specs=[pl.BlockSpec(memory_space=pl.ANY), col(kk, 0), col(kk, 0), col(1, 0), col(1, 0)],
        out_shape=[jax.ShapeDtypeStruct((tp, 2 * D_FF), BF16), jax.ShapeDtypeStruct((kk, D_FF), F32),
                   jax.ShapeDtypeStruct((kk, D_FF), F32), jax.ShapeDtypeStruct((1, D_FF), F32),
                   jax.ShapeDtypeStruct((1, D_FF), F32)],
        scratch_shapes=[pltpu.VMEM((2, 2, tp, LANE), BF16), pltpu.SemaphoreType.DMA((2, 2))],
        compiler_params=_cparams("arbitrary"),
    )(dm, up, up, w, w, b2, b2)


def _lru_gates_fwd(xc, wg, b4, lam, t_real, name):
    tp = xc.shape[0]
    tm = _tile(tp, 1408)

    def body(x_ref, w_ref, b_ref, lam_ref, r0_ref, r1_ref, i0_ref, i1_ref, a0_ref, a1_ref, u0_ref, u1_ref):
        x = x_ref[...]
        xb = x.astype(BF16)
        live = _row_ids(x.shape, pl.program_id(1) * tm) < t_real
        bb = b_ref[...]
        sp = _softplus_neg(lam_ref[...])
        gate = [_sigmoid(jnp.dot(xb, w_ref[k], preferred_element_type=F32) + bb[k:k + 1, :]) for k in range(4)]
        for d, (r_ref, i_ref, a_ref, u_ref) in enumerate(((r0_ref, i0_ref, a0_ref, u0_ref), (r1_ref, i1_ref, a1_ref, u1_ref))):
            r, ig = gate[d], gate[2 + d]
            a = jnp.exp(-LRU_C * r * sp[d:d + 1, :])
            r_ref[...] = r
            i_ref[...] = ig
            a_ref[...] = a
            u_ref[...] = jnp.where(live, jnp.sqrt(1.0 - a * a) * (ig * x), 0.0)

    blk = pl.BlockSpec((tm, LANE), lambda g, i: (i, g))
    return pl.pallas_call(
        body, name=name, grid=(LRU_BLOCKS, tp // tm),
        in_specs=[blk, pl.BlockSpec((None, 4, LANE, LANE), lambda g, i: (g, 0, 0, 0)),
                  pl.BlockSpec((4, LANE), lambda g, i: (0, g)), pl.BlockSpec((2, LANE), lambda g, i: (0, g))],
        out_specs=[blk] * 8,
        out_shape=[jax.ShapeDtypeStruct((tp, D_MODEL), F32)] * 8,
        compiler_params=_cparams("parallel", "parallel"),
    )(xc, wg, b4, lam)


def _lru_gates_bwd(l0, l1, da0, da1, r0, r1, i0, i1, a0, a1, xc, wg, lam, t_real, name):
    tp = xc.shape[0]
    tm = _tile(tp, 1408)

    rc = 32
    assert tm % rc == 0

    def fold(v):
        out = v[0:SUBLANE]
        for t in range(1, rc // SUBLANE):
            out = out + v[t * SUBLANE:(t + 1) * SUBLANE]
        return out

    def body(l0_ref, l1_ref, da0_ref, da1_ref, r0_ref, r1_ref, i0_ref, i1_ref, a0_ref, a1_ref, x_ref, w_ref, lam_ref,
             dx_ref, dw_ref, db_ref, dlam_ref, pre_s, dxp_s):
        i = pl.program_id(1)
        lam_ = lam_ref[...]
        sp = _softplus_neg(lam_)
        dsp_dlam = -_sigmoid(-lam_)

        def chunk(c, sums):
            r0 = pl.multiple_of(c * rc, rc)
            rows = pl.ds(r0, rc)
            x = x_ref[rows, :]
            live = _row_ids((rc, LANE), i * tm + r0) < t_real
            dxp = jnp.zeros_like(x)
            sums = list(sums)
            for d, (l_ref, da_ref, r_ref, i_ref, a_ref) in enumerate(((l0_ref, da0_ref, r0_ref, i0_ref, a0_ref),
                                                                      (l1_ref, da1_ref, r1_ref, i1_ref, a1_ref))):
                r, ig, a = r_ref[rows, :], i_ref[rows, :], a_ref[rows, :]
                du = jnp.where(live, l_ref[rows, :], 0.0)
                a2 = a * a
                rs = lax.rsqrt(1.0 - a2)
                dv = du * ((1.0 - a2) * rs)
                ds = du * (ig * x)
                dla = jnp.where(live, da_ref[rows, :] * a - ds * (a2 * rs), 0.0)
                dr = dla * (-LRU_C) * sp[d:d + 1, :]
                p_r = dr * r * (1.0 - r)
                p_i = dv * x * ig * (1.0 - ig)
                pre_s[d, rows, :] = p_r.astype(BF16)
                pre_s[2 + d, rows, :] = p_i.astype(BF16)
                dxp = dxp + dv * ig
                sums[d] = sums[d] + fold(p_r)
                sums[2 + d] = sums[2 + d] + fold(p_i)
                sums[4 + d] = sums[4 + d] + fold(dla * (-LRU_C) * r)
            dxp_s[rows, :] = dxp
            return tuple(sums)

        zero = jnp.zeros((SUBLANE, LANE), F32)
        sums = lax.fori_loop(0, tm // rc, chunk, (zero,) * 6)

        @pl.when(i == 0)
        def _():
            dw_ref[...] = jnp.zeros_like(dw_ref)
            db_ref[...] = jnp.zeros_like(db_ref)
            dlam_ref[...] = jnp.zeros_like(dlam_ref)

        xb = x_ref[...].astype(BF16)
        dx = dxp_s[...]
        for k in range(4):
            pk = pre_s[k]
            dx = dx + lax.dot_general(pk, w_ref[k], (((1,), (1,)), ((), ())), preferred_element_type=F32)
            dw_ref[k] += lax.dot_general(xb, pk, (((0,), (0,)), ((), ())), preferred_element_type=F32)
        db_ref[...] += jnp.concatenate([jnp.sum(sums[k], axis=0, keepdims=True) for k in range(4)], axis=0)
        dlam_ref[...] += jnp.concatenate([jnp.sum(sums[4 + d], axis=0, keepdims=True) * dsp_dlam[d:d + 1, :] for d in range(2)], axis=0)
        dx_ref[...] = dx

    blk = pl.BlockSpec((tm, LANE), lambda g, i: (i, g))
    return pl.pallas_call(
        body, name=name, grid=(LRU_BLOCKS, tp // tm),
        in_specs=[blk] * 11 + [pl.BlockSpec((None, 4, LANE, LANE), lambda g, i: (g, 0, 0, 0)),
                               pl.BlockSpec((2, LANE), lambda g, i: (0, g))],
        out_specs=[blk, pl.BlockSpec((None, 4, LANE, LANE), lambda g, i: (g, 0, 0, 0)),
                   pl.BlockSpec((4, LANE), lambda g, i: (0, g)), pl.BlockSpec((2, LANE), lambda g, i: (0, g))],
        out_shape=[jax.ShapeDtypeStruct((tp, D_MODEL), F32), jax.ShapeDtypeStruct((LRU_BLOCKS, 4, LANE, LANE), F32),
                   jax.ShapeDtypeStruct((4, D_MODEL), F32), jax.ShapeDtypeStruct((2, D_MODEL), F32)],
        scratch_shapes=[pltpu.VMEM((4, tm, LANE), BF16), pltpu.VMEM((tm, LANE), F32)],
        compiler_params=_cparams("parallel", "arbitrary"),
    )(l0, l1, da0, da1, r0, r1, i0, i1, a0, a1, xc, wg, lam)


SCAN_UNROLL = 4


def _loop_tiles(nt, step, carry):
    assert nt % SCAN_UNROLL == 0

    def trip(tt, c):
        for u in range(SCAN_UNROLL):
            c = step(tt * SCAN_UNROLL + u, c)
        return c

    return lax.fori_loop(0, nt // SCAN_UNROLL, trip, carry)


def _tile_scan(a, u, reverse):
    rows = lax.broadcasted_iota(jnp.int32, a.shape, 0)
    for s in (1, 2, 4):
        if reverse:
            keep = rows < SUBLANE - s
            a_sh, u_sh = pltpu.roll(a, SUBLANE - s, 0), pltpu.roll(u, SUBLANE - s, 0)
        else:
            keep = rows >= s
            a_sh, u_sh = pltpu.roll(a, s, 0), pltpu.roll(u, s, 0)
        u = u + a * jnp.where(keep, u_sh, 0.0)
        a = a * jnp.where(keep, a_sh, 1.0)
    return a, u


def _scan_fwd(a0, u0, a1, u1, proj, name):
    tp, d = a0.shape
    tc = 128
    nt = tp // SUBLANE

    def body(a0_ref, u0_ref, a1_ref, u1_ref, lg_ref, h0_ref, h1_ref, gh_ref):
        def step(t, carry):
            c0, c1 = carry
            f = pl.multiple_of(t * SUBLANE, SUBLANE)
            b = pl.multiple_of((nt - 1 - t) * SUBLANE, SUBLANE)
            pa, pu = _tile_scan(a0_ref[pl.ds(f, SUBLANE), :], u0_ref[pl.ds(f, SUBLANE), :], False)
            h = pu + pa * c0
            h0_ref[pl.ds(f, SUBLANE), :] = h
            c0 = h[SUBLANE - 1:SUBLANE, :]
            pa, pu = _tile_scan(a1_ref[pl.ds(b, SUBLANE), :], u1_ref[pl.ds(b, SUBLANE), :], True)
            h = pu + pa * c1
            h1_ref[pl.ds(b, SUBLANE), :] = h
            c1 = h[0:1, :]
            return c0, c1

        z = jnp.zeros((1, tc), F32)
        _loop_tiles(nt, step, (z, z))
        gh_ref[...] = (_gelu(lg_ref[...]) * (h0_ref[...] + h1_ref[...])).astype(BF16)

    blk = pl.BlockSpec((tp, tc), lambda j: (0, j))
    return pl.pallas_call(
        body, name=name, grid=(d // tc,),
        in_specs=[blk] * 4 + [pl.BlockSpec((tp, tc), lambda j: (0, j + C_LRU_G // tc))], out_specs=[blk] * 3,
        out_shape=[jax.ShapeDtypeStruct((tp, d), F32)] * 2 + [jax.ShapeDtypeStruct((tp, d), BF16)],
        compiler_params=_cparams("parallel"),
    )(a0, u0, a1, u1, proj)


def _scan_bwd(dh, a0, a1, h0, h1, name):
    tp, d = dh.shape
    tc = 128
    nt = tp // SUBLANE

    def body(dh_ref, a0_ref, a1_ref, h0_ref, h1_ref, l0_ref, l1_ref, da0_ref, da1_ref):
        rows8 = lax.broadcasted_iota(jnp.int32, (SUBLANE, tc), 0)

        def step(t, carry):
            c0, c1 = carry
            b = pl.multiple_of((nt - 1 - t) * SUBLANE, SUBLANE)
            f = pl.multiple_of(t * SUBLANE, SUBLANE)
            a = a0_ref[pl.ds(b, SUBLANE), :]
            a_next = jnp.where(rows8 < SUBLANE - 1, pltpu.roll(a, SUBLANE - 1, 0), 1.0)
            pa, pu = _tile_scan(a_next, dh_ref[pl.ds(b, SUBLANE), :], True)
            lam = pu + pa * c0
            l0_ref[pl.ds(b, SUBLANE), :] = lam
            c0 = a[0:1, :] * lam[0:1, :]
            a = a1_ref[pl.ds(f, SUBLANE), :]
            a_prev = jnp.where(rows8 >= 1, pltpu.roll(a, 1, 0), 1.0)
            pa, pu = _tile_scan(a_prev, dh_ref[pl.ds(f, SUBLANE), :], False)
            lam = pu + pa * c1
            l1_ref[pl.ds(f, SUBLANE), :] = lam
            c1 = a[SUBLANE - 1:SUBLANE, :] * lam[SUBLANE - 1:SUBLANE, :]
            return c0, c1

        z = jnp.zeros((1, tc), F32)
        _loop_tiles(nt, step, (z, z))
        rows = lax.broadcasted_iota(jnp.int32, (tp, tc), 0)
        da0_ref[...] = l0_ref[...] * jnp.where(rows >= 1, pltpu.roll(h0_ref[...], 1, 0), 0.0)
        da1_ref[...] = l1_ref[...] * jnp.where(rows < tp - 1, pltpu.roll(h1_ref[...], tp - 1, 0), 0.0)

    blk = pl.BlockSpec((tp, tc), lambda j: (0, j))
    return pl.pallas_call(
        body, name=name, grid=(d // tc,), in_specs=[blk] * 5, out_specs=[blk] * 4,
        out_shape=[jax.ShapeDtypeStruct((tp, d), F32)] * 4,
        compiler_params=_cparams("parallel"),
    )(dh, a0, a1, h0, h1)


def _gated_h_bwd(dgh, proj, h0, h1, dproj, name):
    def fn(row0, dg, lg, x0, x1):
        act, dact = _gelu_and_grad(lg)
        return dg * (x0 + x1) * dact, dg * act

    return _rows(fn, name, [Rw(dgh), Rw(proj, D_MODEL, C_LRU_G // D_MODEL), Rw(h0), Rw(h1)],
                 [Into(dproj, C_LRU_G, D_MODEL), (D_MODEL, F32)])


def _mix(proj, y_mla, y_lru, name):
    def fn(row0, gm, gl, ym, yl):
        return _sigmoid(gm) * ym + _sigmoid(gl) * yl

    return _rows(fn, name, [Rw(proj, D_MODEL, C_G_MLA // D_MODEL), Rw(proj, D_MODEL, C_G_LRU // D_MODEL), Rw(y_mla), Rw(y_lru)],
                 [(D_MODEL, BF16)])[0]


def _mix_bwd(dz, proj, y_mla, y_lru, dproj, name):
    def fn(row0, dzb, gm, gl, ym, yl):
        sm, sl = _sigmoid(gm), _sigmoid(gl)
        dg = jnp.concatenate([dzb * ym * sm * (1.0 - sm), dzb * yl * sl * (1.0 - sl)], axis=1)
        return dzb * sm, dzb * sl, dg

    return _rows(fn, name, [Rw(dz), Rw(proj, D_MODEL, C_G_MLA // D_MODEL), Rw(proj, D_MODEL, C_G_LRU // D_MODEL),
                            Rw(y_mla), Rw(y_lru)], [(D_MODEL, BF16), (D_MODEL, BF16), Into(dproj, C_G_MLA, 2 * D_MODEL)])


def _layer_fwd(h, w_in, more_weights, cs, t_real, tag):
    proj = _matmul(h, w_in, tag + "proj", tb=True)
    w = dict(more_weights(0, proj), w_in=w_in)
    cqn, ckvn = _mla_norms(proj, w['q_norm'], w['kv_norm'], tag + "mla_norms")
    qext = _matmul(cqn, w['w_q'], tag + "q_up")
    kv = _matmul(ckvn, w['w_kv'], tag + "kv_up")
    qc, kc, vb = _mla_pack(qext, kv, proj, cs, tag + "mla_pack")
    o, lse = _attn_fwd(qc, kc, vb, t_real, tag + "attn_fwd")
    w.update(more_weights(1, o))
    y_mla = _matmul(o, w['w_o_mla'], tag + "o_mla")
    xc = _lru_conv_fwd(proj, w['lru_conv_w'], w['lru_conv_b'], t_real, tag + "lru_conv")
    r0, r1, i0, i1, a0, a1, u0, u1 = _lru_gates_fwd(xc, w['w_g'], w['b4'], w['lru_lambda'], t_real, tag + "lru_gates")
    h0, h1, gh = _scan_fwd(a0, u0, a1, u1, proj, tag + "lru_scan")
    y_lru = _matmul(gh, w['w_o_lru'], tag + "o_lru")
    z = _mix(proj, y_mla, y_lru, tag + "mix")
    zo = _matmul(z, w['w_out'], tag + "w_out")
    hm = _ln_fwd([(DN_ALPHA, h), (1.0, zo)], w['ln1_g'], w['ln1_b'], tag + "ln1")
    w.update(more_weights(2, hm))
    up = _matmul(hm, w['w_up'], tag + "w_up", tb=True)
    m = _ffn_conv_act(up, w['ffn_conv_w'], w['ffn_conv_b'], t_real, tag + "ffn_conv")
    f = _matmul(m, w['w_down'], tag + "w_down", tk_cap=1408)
    out = _ln_fwd([(DN_ALPHA, hm), (1.0, f)], w['ln2_g'], w['ln2_b'], tag + "ln2")
    saved = dict(w=w, h=h, proj=proj, cqn=cqn, ckvn=ckvn, qc=qc, kc=kc, vb=vb, o=o, lse=lse, y_mla=y_mla, xc=xc,
                 r0=r0, r1=r1, i0=i0, i1=i1, a0=a0, a1=a1, h0=h0, h1=h1, gh=gh, y_lru=y_lru, z=z, zo=zo, hm=hm,
                 up=up, m=m, f=f)
    return out, saved


DW_MATMUL = dict(ta=True, out_dtype=BF16, tn_cap=1408, tk_cap=1408)


def _after(a, tok):
    return a if tok is None else a + tok.astype(a.dtype)


def _layer_bwd(dout_terms, s, cs, t_real, tag, emit, tok):
    w = s['w']
    g = {}
    du2, dg2, db2 = _ln_bwd(dout_terms, [(DN_ALPHA, s['hm']), (1.0, s['f'])], _after(w['ln2_g'], tok), tag + "ln2_bwd")
    g['ln2_g'], g['ln2_b'] = dg2, db2
    dm = _matmul(du2, w['w_down'], tag + "w_down_dx", tb=True)
    g['w_down'] = _matmul(s['m'], du2, tag + "w_down_dw", **DW_MATMUL)
    dup, dwg_, dwv_, dbg_, dbv_ = _ffn_conv_act_bwd(dm, s['up'], w['ffn_conv_w'], w['ffn_conv_b'], t_real, tag + "ffn_conv_bwd")
    g['ffn_conv_w'] = jnp.concatenate([dwg_, dwv_], axis=1)
    g['ffn_conv_b'] = jnp.concatenate([dbg_, dbv_], axis=1)
    dhm_mm = _matmul(dup, w['w_up'], tag + "w_up_dx", tk_cap=1408)
    g['w_up'] = _matmul(s['hm'], dup, tag + "w_up_dw", **DW_MATMUL)
    tok = emit('ffn', g)
    g = {}
    du1, dg1, db1 = _ln_bwd([(DN_ALPHA, du2), (1.0, dhm_mm)], [(DN_ALPHA, s['h']), (1.0, s['zo'])],
                            _after(w['ln1_g'], tok), tag + "ln1_bwd")
    g['ln1_g'], g['ln1_b'] = dg1, db1
    dz = _matmul(du1, w['w_out'], tag + "w_out_dx", tb=True)
    g['w_out'] = _matmul(s['z'], du1, tag + "w_out_dw", **DW_MATMUL)
    dproj = lax.empty(s['proj'].shape, BF16)
    dy_mla, dy_lru, dproj = _mix_bwd(dz, s['proj'], s['y_mla'], s['y_lru'], dproj, tag + "mix_bwd")
    do = _matmul(dy_mla, w['w_o_mla'], tag + "o_mla_dx", tb=True)
    g['w_o_mla'] = _matmul(s['o'], dy_mla, tag + "o_mla_dw", **DW_MATMUL)
    dqext, dkc, dv = _attn_bwd(s['qc'], s['kc'], s['vb'], do, s['o'], s['lse'], cs, t_real, tag + "attn_bwd")
    dkv, dkrp = _mla_unpack(dkc, dv, cs, tag + "mla_unpack")
    dcqn = _matmul(dqext, w['w_q'], tag + "q_up_dx", tb=True)
    g['w_q'] = _matmul(s['cqn'], dqext, tag + "q_up_dw", **DW_MATMUL)
    dckvn = _matmul(dkv, w['w_kv'], tag + "kv_up_dx", tb=True)
    g['w_kv'] = _matmul(s['ckvn'], dkv, tag + "kv_up_dw", **DW_MATMUL)
    dgh = _matmul(dy_lru, w['w_o_lru'], tag + "o_lru_dx", tb=True)
    g['w_o_lru'] = _matmul(s['gh'], dy_lru, tag + "o_lru_dw", **DW_MATMUL)
    dproj, dhs = _gated_h_bwd(dgh, s['proj'], s['h0'], s['h1'], dproj, tag + "lru_gate_out_bwd")
    l0, l1, da0, da1 = _scan_bwd(dhs, s['a0'], s['a1'], s['h0'], s['h1'], tag + "lru_scan_bwd")
    dxc, g['w_g'], g['b4'], g['lru_lambda'] = _lru_gates_bwd(
        l0, l1, da0, da1, s['r0'], s['r1'], s['i0'], s['i1'], s['a0'], s['a1'], s['xc'], w['w_g'], w['lru_lambda'],
        t_real, tag + "lru_gates_bwd")
    dproj, g['lru_conv_w'], g['lru_conv_b'] = _lru_conv_bwd(dxc, s['proj'], w['lru_conv_w'], dproj, t_real, tag + "lru_conv_bwd")
    tok = emit('mid', g)
    dproj, dqn, dkvn = _mla_norms_bwd(dcqn, dckvn, _after(dkrp, tok), s['proj'], w['q_norm'], w['kv_norm'], dproj,
                                      tag + "mla_norms_bwd")
    tok = emit('in', {'w_in': _matmul(s['h'], dproj, tag + "proj_dw", **DW_MATMUL), 'q_norm': dqn, 'kv_norm': dkvn})
    dh_mm = _matmul(dproj, w['w_in'], tag + "proj_dx", tk_cap=1536)
    return [(DN_ALPHA, du1), (1.0, dh_mm)], tok


def _swap_halves(a, axis=-1):
    h1, h2 = jnp.split(a, 2, axis=axis)
    return jnp.concatenate([h2, h1], axis=axis)


def _w_in_kernel(w_in_t):
    cq, ckv, kr, lg, lx, gm, gl = jnp.split(w_in_t, [256, 384, 448, 1472, 2496, 3520], axis=0)
    return jnp.concatenate([lg, lx, gm, gl, cq, ckv, kr, _swap_halves(kr, axis=0)], axis=0)


def _layer_weights(fl):
    w = {}
    if 'w_uq' in fl:
        uq = fl['w_uq']
        w['w_q'] = jnp.concatenate([uq, _swap_halves(uq[..., QK_NOPE:])], axis=-1).reshape(Q_RANK, HEADS * 2 * LANE)
        w['w_kv'] = jnp.concatenate([fl['w_uk'].reshape(KV_RANK, -1), fl['w_uv'].reshape(KV_RANK, -1)], axis=1).astype(BF16)
        w['w_g'] = jnp.moveaxis(jnp.concatenate([fl['w_rg'], fl['w_ig']], axis=0), 0, 1).astype(BF16)
        w['b4'] = jnp.concatenate([fl['b_rg'], fl['b_ig']], axis=0)
    for n in ('q_norm', 'kv_norm', 'w_o_mla', 'lru_conv_w', 'lru_conv_b', 'lru_lambda', 'w_o_lru', 'w_out', 'ln1_g',
              'ln1_b', 'w_up', 'ffn_conv_w', 'ffn_conv_b', 'w_down', 'ln2_g', 'ln2_b'):
        if n in fl:
            w[n] = fl[n]
    return w


def _layer_grads(g):
    out = {}
    if 'w_in' in g:
        lg, lx, gm, gl, cq, ckv, kr, krs = jnp.split(g['w_in'], [1024, 2048, 3072, 4096, 4352, 4480, 4544], axis=1)
        out['w_in'] = jnp.concatenate([cq, ckv, kr + _swap_halves(krs), lg, lx, gm, gl], axis=1)
    if 'w_q' in g:
        gq = g['w_q'].reshape(Q_RANK, HEADS, 2 * LANE)
        out['w_uq'] = jnp.concatenate([gq[..., :QK_NOPE], gq[..., QK_NOPE:QK_NOPE + QK_ROPE] + _swap_halves(gq[..., QK_NOPE + QK_ROPE:])], axis=-1)
    if 'w_kv' in g:
        out['w_uk'] = g['w_kv'][:, :HEADS * QK_NOPE].reshape(KV_RANK, HEADS, QK_NOPE)
        out['w_uv'] = g['w_kv'][:, HEADS * QK_NOPE:].reshape(KV_RANK, HEADS, V_HEAD)
    if 'w_g' in g:
        gg = jnp.moveaxis(g['w_g'], 1, 0)
        out['w_rg'], out['w_ig'] = gg[:2], gg[2:]
    if 'b4' in g:
        out['b_rg'], out['b_ig'] = g['b4'][:2], g['b4'][2:]
    for n in ('q_norm', 'kv_norm', 'lru_conv_b', 'ln1_g', 'ln1_b', 'ffn_conv_b', 'ln2_g', 'ln2_b'):
        if n in g:
            out[n] = g[n].reshape(-1)
    for n in ('w_o_mla', 'lru_conv_w', 'lru_lambda', 'w_o_lru', 'w_out', 'w_up', 'ffn_conv_w', 'w_down'):
        if n in g:
            out[n] = g[n]
    return out


def _rope_table(tp):
    half = QK_ROPE // 2
    inv_freq = jnp.exp(-math.log(ROPE_THETA) * jnp.arange(half, dtype=F32) / half)
    ang = jnp.arange(tp, dtype=F32)[:, None] * inv_freq[None, :]
    c, s = jnp.cos(ang), jnp.sin(ang)
    return jnp.concatenate([c, c, -s, s], axis=1)


def _local_step(x, target, meta, ln0_g, ln0_b, layer_w, t_pad, emit):
    seq = x.shape[0]
    t_real = N_META + seq
    zpad = jnp.zeros((t_pad - t_real, D_MODEL), F32)
    xin = jnp.concatenate([meta, x, zpad], axis=0)
    tgt = jnp.concatenate([jnp.zeros((N_META, D_MODEL), F32), target, zpad], axis=0)
    cs = _rope_table(t_pad)
    h = _ln_fwd([(1.0, xin)], ln0_g, ln0_b, "ln0")
    saved = []
    for l in range(DEPTH):
        w_in, rest_of_weights = layer_w[l](h)
        h, s = _layer_fwd(h, w_in, rest_of_weights, cs, t_real, "l%d_" % l)
        saved.append(s)
    dy, lossvec = _loss_head(h, tgt, t_real, "loss_head")
    terms, tok = [(1.0, dy)], None
    for l in reversed(range(DEPTH)):
        terms, tok = _layer_bwd(terms, saved[l], cs, t_real, "l%d_" % l,
                                functools.partial(lambda stage, g, l: emit(l, stage, _layer_grads(g)), l=l), tok)
    dxin, dg0, db0 = _ln_bwd(terms, [(1.0, xin)], _after(ln0_g, tok), "ln0_bwd")
    emit(None, 'head', {'meta_tokens': dxin[:N_META], 'ln0_g': dg0.reshape(-1), 'ln0_b': db0.reshape(-1), 'loss': lossvec})
    return dxin[N_META:t_real]


_HBM = pl.BlockSpec(memory_space=pltpu.HBM)
_SEM = pl.BlockSpec(memory_space=pltpu.SEMAPHORE)
_SIDE_EFFECT = pltpu.SideEffectType.DATAFLOW_SIDE_EFFECTING


def _peer_copies(src_refs, land_refs, scatters, send_sems, recv_sems):
    x, y, c = lax.axis_index("x"), lax.axis_index("y"), lax.axis_index("c")
    me = 4 * x + 2 * y + c
    copies = []
    for k in range(1, N_DEV):
        px = 1 - x if k & 4 else x
        py = 1 - y if k & 2 else y
        pc = 1 - c if k & 1 else c
        for t, (src, land) in enumerate(zip(src_refs, land_refs)):
            copies.append(pltpu.make_async_remote_copy(
                src_ref=src.at[4 * px + 2 * py + pc] if scatters[t] else src, dst_ref=land.at[me],
                send_sem=send_sems.at[7 * t + k - 1], recv_sem=recv_sems.at[7 * t + k - 1],
                device_id=(px, py, pc), device_id_type=pl.DeviceIdType.MESH))
    return me, copies


def _own_block_in_place(land, own):
    me = 4 * lax.axis_index("x") + 2 * lax.axis_index("y") + lax.axis_index("c")
    return lax.dynamic_update_slice_in_dim(land, own, me, 0)


def _gather_two_level(shards, name):
    nt = len(shards)

    def body(*refs):
        x_refs, out_refs = refs[:nt], refs[nt:2 * nt]
        token_ref, send_sems, recv_sems = refs[2 * nt:]
        x, y, c = lax.axis_index("x"), lax.axis_index("y"), lax.axis_index("c")
        me, sibling = (x, y, c), (x, y, 1 - c)
        chips = [(1 - x, y), (x, 1 - y), (1 - x, 1 - y)]

        def copy(t, k, block, to, own=False):
            px, py, pc = block
            slot = out_refs[t].at[4 * px + 2 * py + pc]
            return pltpu.make_async_remote_copy(
                src_ref=x_refs[t] if own else slot, dst_ref=slot,
                send_sem=send_sems.at[7 * t + k], recv_sem=recv_sems.at[7 * t + k],
                device_id=to, device_id_type=pl.DeviceIdType.MESH)

        sent = []
        for t in range(nt):
            first = [copy(t, 1 + j, me, (*chip, c), own=True) for j, chip in enumerate(chips)]
            first.append(copy(t, 0, me, sibling, own=True))
            for cp in first:
                cp.start()
            sent += first
        token_ref[...] = jnp.zeros_like(token_ref)
        for j, chip in enumerate(chips):
            for t in range(nt):
                copy(t, 1 + j, (*chip, c), me).wait_recv()
                passed = copy(t, 4 + j, (*chip, c), sibling)
                passed.start()
                sent.append(passed)
        for t in range(nt):
            copy(t, 0, sibling, me).wait_recv()
            for j, chip in enumerate(chips):
                copy(t, 4 + j, (*chip, 1 - c), me).wait_recv()
        for cp in sent:
            cp.wait_send()

    any_space = pl.BlockSpec(memory_space=pl.ANY)
    res = pl.pallas_call(
        body, name=name,
        out_shape=[jax.ShapeDtypeStruct((N_DEV,) + a.shape, a.dtype) for a in shards] + [jax.ShapeDtypeStruct((SUBLANE, LANE), F32)],
        in_specs=[any_space] * nt, out_specs=[any_space] * nt + [pl.BlockSpec(memory_space=pltpu.VMEM)],
        scratch_shapes=[pltpu.SemaphoreType.DMA((7 * nt,)), pltpu.SemaphoreType.DMA((7 * nt,))],
    )(*shards)
    return [_own_block_in_place(land, a[None]) for land, a in zip(res[:nt], shards)], res[nt][0, 0]


def _exchange_start(groups, name):
    flat = [it for grp in groups for it in grp]
    nt, ng = len(flat), len(groups)
    scatters = [sc for _, sc in flat]
    srcs = [pltpu.with_memory_space_constraint(a, pltpu.HBM) for a, _ in flat]
    land_shapes = [a.shape if sc else (N_DEV,) + a.shape for a, sc in flat]
    lands = [pltpu.with_memory_space_constraint(lax.empty(s, a.dtype), pltpu.HBM) for s, (a, _) in zip(land_shapes, flat)]
    bounds = [0]
    for grp in groups:
        bounds.append(bounds[-1] + len(grp))

    def body(*refs):
        src_refs, land_refs = refs[:nt], refs[nt:2 * nt]
        sem_refs = refs[2 * nt:2 * nt + 2 * ng]
        token_ref = refs[4 * nt + 2 * ng]
        for gi in range(ng):
            lo, hi = bounds[gi], bounds[gi + 1]
            _, copies = _peer_copies(src_refs[lo:hi], land_refs[lo:hi], scatters[lo:hi], sem_refs[2 * gi], sem_refs[2 * gi + 1])
            for cp in copies:
                cp.start()
        token_ref[...] = jnp.zeros_like(token_ref)

    out_shape = []
    for grp in groups:
        out_shape += [pltpu.SemaphoreType.DMA((7 * len(grp),)), pltpu.SemaphoreType.DMA((7 * len(grp),))]
    out_shape += [pltpu.HBM(a.shape, a.dtype) for a in srcs] + [pltpu.HBM(s, a.dtype) for s, a in zip(land_shapes, srcs)]
    out_shape += [jax.ShapeDtypeStruct((SUBLANE, LANE), F32)]
    res = pl.pallas_call(
        body, name=name, out_shape=out_shape,
        in_specs=[_HBM] * (2 * nt),
        out_specs=[_SEM] * (2 * ng) + [_HBM] * (2 * nt) + [pl.BlockSpec(memory_space=pltpu.VMEM)],
        input_output_aliases={t: 2 * ng + t for t in range(2 * nt)},
        compiler_params=pltpu.CompilerParams(has_side_effects=_SIDE_EFFECT),
    )(*srcs, *lands)
    sems, thru, token = res[:2 * ng], res[2 * ng:2 * ng + 2 * nt], res[-1]
    states = []
    for gi in range(ng):
        lo, hi = bounds[gi], bounds[gi + 1]
        states.append((sems[2 * gi], sems[2 * gi + 1], thru[lo:hi], thru[nt + lo:nt + hi], scatters[lo:hi]))
    return states, token[0, 0]


def _exchange_wait(state, after, name):
    send_sems, recv_sems, srcs, lands, scatters = state
    n = len(srcs)

    def body(*refs):
        _, copies = _peer_copies(refs[:n], refs[n:2 * n], scatters, refs[2 * n], refs[2 * n + 1])
        for cp in copies:
            cp.wait_send()
        for cp in copies:
            cp.wait_recv()

    res = pl.pallas_call(
        body, name=name,
        out_shape=[pltpu.HBM(a.shape, a.dtype) for a in srcs] + [pltpu.HBM(a.shape, a.dtype) for a in lands],
        in_specs=[_HBM] * (2 * n) + [_SEM, _SEM, _HBM],
        out_specs=[_HBM] * (2 * n),
        input_output_aliases={t: t for t in range(2 * n)},
        compiler_params=pltpu.CompilerParams(has_side_effects=_SIDE_EFFECT),
    )(*srcs, *lands, send_sems, recv_sems, pltpu.with_memory_space_constraint(after, pltpu.HBM))
    me = 4 * lax.axis_index("x") + 2 * lax.axis_index("y") + lax.axis_index("c")
    out = []
    for src, land, sc in zip(res[:n], res[n:], scatters):
        own = lax.dynamic_index_in_dim(src, me, 0, keepdims=True) if sc else src[None]
        out.append(lax.dynamic_update_slice_in_dim(land, own, me, 0))
    return out


def _as_rows(shape):
    return (1, shape[0]) if len(shape) == 1 else (math.prod(shape[:-1]), shape[-1])


def _sum_adamw(pieces, w, m, v, name):
    shape = w.shape
    nl = len(pieces)
    if nl > 1 and _as_rows(shape[1:])[0] % 16:
        pieces, nl = [jnp.stack(pieces, axis=1)], 1
    rows, cols = _as_rows(shape)
    rl = rows // nl
    cap = max(16, (1 << 18) // cols // 16 * 16)
    tr = _tile(rl, cap, 16)
    nb = rl // tr
    c1 = 1.0 / (1.0 - ADAM_B1 ** ADAM_STEP)
    c2 = 1.0 / (1.0 - ADAM_B2 ** ADAM_STEP)

    def body(*refs):
        p_refs = refs[:nl]
        w_ref, m_ref, v_ref, g_ref, d_ref, nm_ref, nv_ref = refs[nl:]
        li = pl.program_id(0)

        def total(p_ref):
            acc = p_ref[0].astype(F32)
            for k in range(1, N_DEV):
                acc = acc + p_ref[k].astype(F32)
            return acc

        gg = total(p_refs[0])
        for l in range(1, nl):
            gg = jnp.where(li == l, total(p_refs[l]), gg)
        nm = ADAM_B1 * m_ref[...] + (1.0 - ADAM_B1) * gg
        nv = ADAM_B2 * v_ref[...] + (1.0 - ADAM_B2) * (gg * gg)
        g_ref[...] = gg
        d_ref[...] = -ADAM_LR * ((nm * c1) / (jnp.sqrt(nv * c2) + ADAM_EPS) + ADAM_WD * w_ref[...])
        nm_ref[...] = nm
        nv_ref[...] = nv

    blk = pl.BlockSpec((tr, cols), lambda li, i: (li * nb + i, 0))
    p_specs = [pl.BlockSpec((N_DEV, tr, cols), functools.partial(lambda li, i, l: (0, jnp.where(li == l, i, 0), 0), l=l))
               for l in range(nl)]
    res = pl.pallas_call(
        body, name=name, grid=(nl, nb),
        in_specs=p_specs + [blk] * 3, out_specs=[blk] * 4,
        out_shape=[jax.ShapeDtypeStruct((rows, cols), F32)] * 4,
        compiler_params=_cparams("parallel", "parallel"),
    )(*[p.reshape(N_DEV, rl, cols) for p in pieces], *[a.reshape(rows, cols) for a in (w, m, v)])
    return [r.reshape(shape) for r in res]


def _to_shards(full, axis):
    shp = full.shape
    a = full.reshape(shp[:axis] + (N_DEV, shp[axis] // N_DEV) + shp[axis + 1:])
    return jnp.moveaxis(a, axis, 0)


def _from_shards(blocks, axis):
    a = jnp.moveaxis(blocks, 0, axis)
    shp = a.shape
    return a.reshape(shp[:axis] + (shp[axis] * shp[axis + 1],) + shp[axis + 2:])


def kernel(x, meta_tokens, ln0_g, ln0_b, w_in, q_norm, kv_norm, w_uq, w_uk, w_uv, w_o_mla, lru_conv_w, lru_conv_b, w_rg, b_rg, w_ig, b_ig, lru_lambda, w_o_lru, w_out, ln1_g, ln1_b, w_up, ffn_conv_w, ffn_conv_b, w_down, ln2_g, ln2_b, loss_target, m_meta_tokens, m_ln0_g, m_ln0_b, m_w_in, m_q_norm, m_kv_norm, m_w_uq, m_w_uk, m_w_uv, m_w_o_mla, m_lru_conv_w, m_lru_conv_b, m_w_rg, m_b_rg, m_w_ig, m_b_ig, m_lru_lambda, m_w_o_lru, m_w_out, m_ln1_g, m_ln1_b, m_w_up, m_ffn_conv_w, m_ffn_conv_b, m_w_down, m_ln2_g, m_ln2_b, v_meta_tokens, v_ln0_g, v_ln0_b, v_w_in, v_q_norm, v_kv_norm, v_w_uq, v_w_uk, v_w_uv, v_w_o_mla, v_lru_conv_w, v_lru_conv_b, v_w_rg, v_b_rg, v_w_ig, v_b_ig, v_lru_lambda, v_w_o_lru, v_w_out, v_ln1_g, v_ln1_b, v_w_up, v_ffn_conv_w, v_ffn_conv_b, v_w_down, v_ln2_g, v_ln2_b):
    args = (meta_tokens, ln0_g, ln0_b, w_in, q_norm, kv_norm, w_uq, w_uk, w_uv, w_o_mla, lru_conv_w, lru_conv_b, w_rg, b_rg, w_ig, b_ig, lru_lambda, w_o_lru, w_out, ln1_g, ln1_b, w_up, ffn_conv_w, ffn_conv_b, w_down, ln2_g, ln2_b)
    ms = (m_meta_tokens, m_ln0_g, m_ln0_b, m_w_in, m_q_norm, m_kv_norm, m_w_uq, m_w_uk, m_w_uv, m_w_o_mla, m_lru_conv_w, m_lru_conv_b, m_w_rg, m_b_rg, m_w_ig, m_b_ig, m_lru_lambda, m_w_o_lru, m_w_out, m_ln1_g, m_ln1_b, m_w_up, m_ffn_conv_w, m_ffn_conv_b, m_w_down, m_ln2_g, m_ln2_b)
    vs = (v_meta_tokens, v_ln0_g, v_ln0_b, v_w_in, v_q_norm, v_kv_norm, v_w_uq, v_w_uk, v_w_uv, v_w_o_mla, v_lru_conv_w, v_lru_conv_b, v_w_rg, v_b_rg, v_w_ig, v_b_ig, v_lru_lambda, v_w_o_lru, v_w_out, v_ln1_g, v_ln1_b, v_w_up, v_ffn_conv_w, v_ffn_conv_b, v_w_down, v_ln2_g, v_ln2_b)
    wd, md, vd = dict(zip(WEIGHTS, args)), dict(zip(WEIGHTS, ms)), dict(zip(WEIGHTS, vs))

    def shard_axis(n, l):
        return SHARD_AXIS[n] - (0 if l is None else 1)

    def shard(n, l):
        a = wd[n] if l is None else wd[n][l]
        if n in SENT_TRANSPOSED:
            a = a.T
        return a.astype(BF16) if n in BIG else a

    def whole(keys, landed):
        return {k: b.reshape(-1, b.shape[-1]) if k[0] in SENT_TRANSPOSED else _from_shards(b, shard_axis(*k))
                for k, b in zip(keys, landed)}

    first = [('meta_tokens', None), ('w_in', 0)]
    landed, token = _gather_two_level([shard(*k) for k in first], "gather_first")
    got_first = whole(first, landed)
    staged = [[(n, 0) for n in names] for names in STAGE_WEIGHTS]
    later = [(n, 1) for n in SHARDED if n != 'meta_tokens']
    groups = [[(shard(*k), False) for k in keys] for keys in staged + [later]]
    groups[0][0] = (_after(groups[0][0][0], token), False)
    gather, token = _exchange_start(groups, "gather_start")

    def arrive(gi, keys, after, name):
        return whole(keys, _exchange_wait(gather[gi], after, name))

    def layer_weights(got, l, names):
        fl = {n: wd[n][l] for n in names if n in REPLICATED}
        fl.update({n: a for (n, _), a in got.items() if n in names})
        return _layer_weights(fl)

    ln0_g = _after(wd['ln0_g'], token)

    def first_layer(h):
        def more(stage, after):
            got = arrive(stage, staged[stage], after, "gather_wait_l0_%d" % stage)
            return layer_weights(got, 0, STAGE_WEIGHTS[stage] + STAGE_REPLICATED[stage])
        return _w_in_kernel(got_first['w_in', 0]), more

    def second_layer(h):
        got = arrive(len(staged), later, h, "gather_wait_l1")
        return _w_in_kernel(got['w_in', 1]), lambda stage, after: layer_weights(got, 1, STAGE_WEIGHTS[stage] + STAGE_REPLICATED[stage])

    sent = []
    pending = []

    def send(l, stage, grads):
        for n, g in grads.items():
            if n in SHARD_AXIS:
                g = _to_shards(g, shard_axis(n, l))
                pending.append(((n, l), (g.astype(BF16) if n in BIG else g, True)))
            else:
                pending.append(((n, l), (g.astype(BF16) if n in LARGE_REPLICATED else g, False)))
        if l == DEPTH - 1 and stage != 'in':
            return None
        (state,), tok = _exchange_start([[it for _, it in pending]], "grads_start_%s_%s" % (l, stage))
        sent.append(([k for k, _ in pending], state))
        pending.clear()
        return tok

    seq = x.shape[1]
    t_pad = -(-(N_META + seq + MIN_PAD_ROWS) // LANE) * LANE
    grad_x = _local_step(x[0], loss_target[0], got_first['meta_tokens', None], ln0_g, wd['ln0_b'],
                         [first_layer, second_layer], t_pad, send)

    pieces, outs, after = {}, {}, grad_x
    for gi, (keys, state) in enumerate(sent):
        pieces.update(zip(keys, _exchange_wait(state, after, "grads_wait_%d" % gi)))
        if gi == len(sent) - 2:
            for n in WEIGHTS:
                if (n, 0) in pieces:
                    outs[n] = _sum_adamw([pieces[n, l] for l in range(DEPTH)], wd[n], md[n], vd[n], "adamw_" + n)
                    after = outs[n][1]
    loss = jnp.sum(pieces['loss', None])
    for n in WEIGHTS:
        if (n, None) in pieces:
            outs[n] = _sum_adamw([pieces[n, None]], wd[n], md[n], vd[n], "adamw_" + n)
    res = [loss, grad_x[None]]
    for k in range(4):
        res += [outs[n][k] for n in WEIGHTS]
    return tuple(res)
```

```python
import functools
import math

import jax
import jax.numpy as jnp
from jax import lax
from jax.experimental import pallas as pl
from jax.experimental.pallas import tpu as pltpu

F32 = jnp.float32
BF16 = jnp.bfloat16

N_DEV = 8
D_MODEL = 1024
N_META = 16
HEADS = 8
QK_NOPE = 128
QK_ROPE = 64
V_HEAD = 128
Q_RANK = 256
KV_RANK = 128
ROPE_THETA = 10000.0
LRU_BLOCKS = 8
LRU_C = 8.0
D_FF = 2816
DEPTH = 2
DN_ALPHA = (2.0 * DEPTH) ** 0.25
LN_EPS = 1e-5
RMS_EPS = 1e-6
LN2 = math.log(2.0)
ATT_SCALE = 1.0 / math.sqrt(QK_NOPE + QK_ROPE) / LN2
NEG_BIG = -1e30

ADAM_LR = 0.001
ADAM_B1 = 0.9
ADAM_B2 = 0.999
ADAM_EPS = 1e-08
ADAM_WD = 0.01
ADAM_STEP = 10

MIN_PAD_ROWS = 2
LANE = 128
SUBLANE = 8
VMEM_LIMIT = 56 * 1024 * 1024

PROJ_COLS = 4 * D_MODEL + Q_RANK + KV_RANK + 2 * QK_ROPE
C_LRU_G, C_LRU_X, C_G_MLA, C_G_LRU = 0, D_MODEL, 2 * D_MODEL, 3 * D_MODEL
C_CQ = 4 * D_MODEL
C_CKV = C_CQ + Q_RANK
C_KRP = C_CKV + KV_RANK

WEIGHTS = ['meta_tokens', 'ln0_g', 'ln0_b', 'w_in', 'q_norm', 'kv_norm', 'w_uq', 'w_uk', 'w_uv', 'w_o_mla',
           'lru_conv_w', 'lru_conv_b', 'w_rg', 'b_rg', 'w_ig', 'b_ig', 'lru_lambda', 'w_o_lru', 'w_out',
           'ln1_g', 'ln1_b', 'w_up', 'ffn_conv_w', 'ffn_conv_b', 'w_down', 'ln2_g', 'ln2_b']
SHARD_AXIS = {'meta_tokens': 1, 'w_in': 2, 'w_uq': 1, 'w_o_mla': 1, 'lru_conv_w': 2, 'b_rg': 2, 'b_ig': 2,
              'lru_lambda': 2, 'w_o_lru': 1, 'w_out': 1, 'w_up': 2, 'ffn_conv_w': 2, 'w_down': 1}
BIG = ['w_in', 'w_uq', 'w_o_mla', 'w_o_lru', 'w_out', 'w_up', 'w_down']
SHARDED = [n for n in WEIGHTS if n in SHARD_AXIS]
REPLICATED = [n for n in WEIGHTS if n not in SHARD_AXIS]
LARGE_REPLICATED = ['w_uk', 'w_uv', 'w_rg', 'w_ig']
SENT_TRANSPOSED = ['w_in', 'w_up']
STAGE_WEIGHTS = [['w_uq', 'lru_conv_w', 'b_rg', 'b_ig', 'lru_lambda'], ['w_o_mla', 'w_o_lru', 'w_out'], ['w_up', 'ffn_conv_w', 'w_down']]
STAGE_REPLICATED = [['q_norm', 'kv_norm', 'w_uk', 'w_uv', 'lru_conv_b', 'w_rg', 'w_ig'], ['ln1_g', 'ln1_b'], ['ffn_conv_b', 'ln2_g', 'ln2_b']]


def _cparams(*sem):
    return pltpu.CompilerParams(dimension_semantics=sem, vmem_limit_bytes=VMEM_LIMIT)


def _tile(n, cap, unit=LANE):
    best = None
    t = unit
    while t <= min(n, cap):
        if n % t == 0:
            best = t
        t += unit
    return n if best is None else best


def _sigmoid(x):
    return 1.0 / (1.0 + jnp.exp(-x))


_GELU_C = math.sqrt(2.0 / math.pi)


_GELU_A = 0.044715


def _gelu(x):
    t = jnp.tanh(x * (_GELU_C + (_GELU_C * _GELU_A) * (x * x)))
    hx = 0.5 * x
    return hx + hx * t


def _gelu_and_grad(x):
    x2 = x * x
    t = jnp.tanh(x * (_GELU_C + (_GELU_C * _GELU_A) * x2))
    hx = 0.5 * x
    dg = 0.5 + 0.5 * t + (hx * (1.0 - t * t)) * (_GELU_C + (3.0 * _GELU_C * _GELU_A) * x2)
    return hx + hx * t, dg


def _softplus_neg(lam):
    z = jnp.exp(-jnp.abs(lam))
    w = 1.0 + z
    log1p = jnp.where(w == 1.0, z, jnp.log(w) * z / (w - 1.0))
    return jnp.maximum(-lam, 0.0) + log1p


def _row_ids(shape, row0=0):
    return lax.broadcasted_iota(jnp.int32, shape, 0) + row0


def _matmul(a, b, name, ta=False, tb=False, out_dtype=F32, tm_cap=1408, tn_cap=1024, tk_cap=2048):
    if ta:
        kdim, m = a.shape
    else:
        m, kdim = a.shape
    if tb:
        n, k2 = b.shape
    else:
        k2, n = b.shape
    assert kdim == k2, (a.shape, b.shape, ta, tb)
    tm, tn, tk = _tile(m, tm_cap), _tile(n, tn_cap), _tile(kdim, tk_cap)
    nk = kdim // tk

    def body(a_ref, b_ref, o_ref, *acc):
        dn = (((0 if ta else 1,), (1 if tb else 0,)), ((), ()))
        part = lax.dot_general(a_ref[...].astype(BF16), b_ref[...].astype(BF16), dn, preferred_element_type=F32)
        if nk == 1:
            o_ref[...] = part.astype(o_ref.dtype)
            return
        acc_ref, k = acc[0], pl.program_id(2)

        @pl.when(k == 0)
        def _():
            acc_ref[...] = part

        @pl.when(k > 0)
        def _():
            acc_ref[...] += part

        @pl.when(k == nk - 1)
        def _():
            o_ref[...] = acc_ref[...].astype(o_ref.dtype)

    a_spec = pl.BlockSpec((tk, tm), lambda i, j, k: (k, i)) if ta else pl.BlockSpec((tm, tk), lambda i, j, k: (i, k))
    b_spec = pl.BlockSpec((tn, tk), lambda i, j, k: (j, k)) if tb else pl.BlockSpec((tk, tn), lambda i, j, k: (k, j))
    return pl.pallas_call(
        body, name=name,
        grid=(m // tm, n // tn, nk),
        in_specs=[a_spec, b_spec],
        out_specs=pl.BlockSpec((tm, tn), lambda i, j, k: (i, j)),
        out_shape=jax.ShapeDtypeStruct((m, n), out_dtype),
        scratch_shapes=[pltpu.VMEM((tm, tn), F32)] if nk > 1 else [],
        compiler_params=_cparams("parallel", "parallel", "arbitrary"),
    )(a, b)


class Rw:
    def __init__(self, arr, width=None, cb=0):
        self.arr, self.width, self.cb = arr, (arr.shape[1] if width is None else width), cb


class Pm:
    def __init__(self, arr):
        self.arr = arr


class Into:
    def __init__(self, arr, col0, width):
        self.arr, self.col0, self.width = arr, col0, width


def _call_with_into(body, name, grid, in_specs, operands, outs, spec_of, shape_of, extra_out_specs, extra_out_shape, sem):
    intos = [(k, o) for k, o in enumerate(outs) if isinstance(o, Into)]
    aliases = {len(operands) + n: k for n, (k, _) in enumerate(intos)}
    return pl.pallas_call(
        body, name=name, grid=grid,
        in_specs=in_specs + [pl.BlockSpec(memory_space=pl.ANY)] * len(intos),
        out_specs=[spec_of(o) for o in outs] + extra_out_specs,
        out_shape=[jax.ShapeDtypeStruct(o.arr.shape, o.arr.dtype) if isinstance(o, Into) else shape_of(o) for o in outs]
        + extra_out_shape,
        input_output_aliases=aliases,
        compiler_params=_cparams(sem),
    )(*operands, *[o.arr for _, o in intos])


def _rows(fn, name, ins, outs, accs=(), tm_cap=384):
    tp = next(o.arr.shape[0] for o in ins if isinstance(o, Rw))
    tm = _tile(tp, tm_cap)
    n_in, n_out, n_acc = len(ins), len(outs), len(accs)
    n_into = sum(isinstance(o, Into) for o in outs)

    def body(*refs):
        i = pl.program_id(0)
        res = fn(i * tm, *[r[...] for r in refs[:n_in]])
        if not isinstance(res, (tuple, list)):
            res = (res,)
        assert len(res) == n_out + n_acc, (name, len(res))
        out_refs = refs[n_in + n_into:]
        for k in range(n_out):
            out_refs[k][...] = res[k].astype(out_refs[k].dtype)
        for k in range(n_acc):
            ref = out_refs[n_out + k]

            @pl.when(i == 0)
            def _():
                ref[...] = jnp.zeros_like(ref)

            ref[...] += res[n_out + k]

    in_specs = []
    for o in ins:
        if isinstance(o, Rw):
            in_specs.append(pl.BlockSpec((tm, o.width), functools.partial(lambda i, cb: (i, cb), cb=o.cb)))
        else:
            in_specs.append(pl.BlockSpec(o.arr.shape, functools.partial(lambda i, nd: (0,) * nd, nd=o.arr.ndim)))

    def spec_of(o):
        if isinstance(o, Into):
            assert o.col0 % o.width == 0, (name, o.col0, o.width)
            return pl.BlockSpec((tm, o.width), functools.partial(lambda i, cb: (i, cb), cb=o.col0 // o.width))
        return pl.BlockSpec((tm, o[0]), lambda i: (i, 0))

    return _call_with_into(
        body, name, (tp // tm,), in_specs, [o.arr for o in ins], list(outs), spec_of,
        lambda o: jax.ShapeDtypeStruct((tp, o[0]), o[1]),
        [pl.BlockSpec(s, functools.partial(lambda i, nd: (0,) * nd, nd=len(s))) for s in accs],
        [jax.ShapeDtypeStruct(s, F32) for s in accs], "arbitrary")


class Cl:
    def __init__(self, arr, col0=0):
        self.arr, self.col0 = arr, col0


def _cols(fn, name, ins, outs, ncols, tc):
    assert ncols % tc == 0
    n_in, n_out = len(ins), len(outs)
    n_into = sum(isinstance(o, Into) for o in outs)

    def body(*refs):
        res = fn(*[r[...] for r in refs[:n_in]])
        if not isinstance(res, (tuple, list)):
            res = (res,)
        assert len(res) == n_out, (name, len(res))
        out_refs = refs[n_in + n_into:]
        for k in range(n_out):
            out_refs[k][...] = res[k].astype(out_refs[k].dtype)

    in_specs = []
    for o in ins:
        assert o.col0 % tc == 0, (name, o.col0, tc)
        in_specs.append(pl.BlockSpec((o.arr.shape[0], tc), functools.partial(lambda j, off: (0, j + off), off=o.col0 // tc)))

    def spec_of(o):
        if isinstance(o, Into):
            assert o.col0 % tc == 0 and o.width == ncols, (name, o.col0, o.width)
            return pl.BlockSpec((o.arr.shape[0], tc), functools.partial(lambda j, off: (0, j + off), off=o.col0 // tc))
        return pl.BlockSpec((o[0], tc), lambda j: (0, j))

    return _call_with_into(body, name, (ncols // tc,), in_specs, [o.arr for o in ins], list(outs), spec_of,
                           lambda o: jax.ShapeDtypeStruct((o[0], ncols), o[1]), [], [], "parallel")


def _ln_stats(u):
    mu = jnp.mean(u, axis=-1, keepdims=True)
    xc = u - mu
    var = jnp.mean(xc * xc, axis=-1, keepdims=True)
    rstd = lax.rsqrt(var + LN_EPS)
    return xc * rstd, rstd


def _ln_fwd(terms, g, b, name):
    coefs = [c for c, _ in terms]

    def fn(row0, *blk):
        xs, (gg, bb) = blk[:len(coefs)], blk[len(coefs):]
        u = sum(c * x for c, x in zip(coefs, xs))
        xhat, _ = _ln_stats(u)
        return xhat * gg + bb

    d = terms[0][1].shape[1]
    return _rows(fn, name, [Rw(x) for _, x in terms] + [Pm(g.reshape(1, d)), Pm(b.reshape(1, d))], [(d, F32)])[0]


def _ln_bwd(dy_terms, u_terms, g, name):
    dc = [c for c, _ in dy_terms]
    uc = [c for c, _ in u_terms]
    d = u_terms[0][1].shape[1]

    def fn(row0, *blk):
        dys = blk[:len(dc)]
        xs = blk[len(dc):len(dc) + len(uc)]
        gg = blk[-1]
        dy = sum(c * x for c, x in zip(dc, dys))
        u = sum(c * x for c, x in zip(uc, xs))
        xhat, rstd = _ln_stats(u)
        gdy = dy * gg
        m1 = jnp.mean(gdy, axis=-1, keepdims=True)
        m2 = jnp.mean(gdy * xhat, axis=-1, keepdims=True)
        du = rstd * (gdy - m1 - xhat * m2)
        return du, jnp.sum(dy * xhat, axis=0, keepdims=True), jnp.sum(dy, axis=0, keepdims=True)

    ins = [Rw(x) for _, x in dy_terms] + [Rw(x) for _, x in u_terms] + [Pm(g.reshape(1, d))]
    return _rows(fn, name, ins, [(d, F32)], accs=[(1, d), (1, d)])


def _loss_head(y, tgt, t_real, name):
    d = y.shape[1]

    def fn(row0, yb, tb):
        rows = _row_ids(yb.shape, row0)
        live = (rows >= N_META) & (rows < t_real)
        diff = jnp.where(live, yb - tb, 0.0)
        return diff * (1.0 / d), jnp.sum(diff * diff, axis=0, keepdims=True) * (0.5 / d)

    return _rows(fn, name, [Rw(y), Rw(tgt)], [(d, F32)], accs=[(1, d)])


def _rms(x, g):
    r = lax.rsqrt(jnp.mean(x * x, axis=-1, keepdims=True) + RMS_EPS)
    return x * r * g


def _rms_bwd(dy, x, g):
    r = lax.rsqrt(jnp.mean(x * x, axis=-1, keepdims=True) + RMS_EPS)
    gdy = dy * g
    dx = r * gdy - x * (r * r * r) * jnp.mean(gdy * x, axis=-1, keepdims=True)
    return dx, jnp.sum(dy * x * r, axis=0, keepdims=True)


def _mla_norms(proj, qn, kvn, name):
    def fn(row0, cq, ckv, g1, g2):
        return _rms(cq, g1), _rms(ckv, g2)

    return _rows(fn, name, [Rw(proj, Q_RANK, C_CQ // Q_RANK), Rw(proj, KV_RANK, C_CKV // KV_RANK),
                            Pm(qn.reshape(1, Q_RANK)), Pm(kvn.reshape(1, KV_RANK))],
                 [(Q_RANK, BF16), (KV_RANK, BF16)])


def _mla_norms_bwd(dcqn, dckvn, dkrp, proj, qn, kvn, dproj, name):
    def fn(row0, d1, d2, dkr, cq, ckv, g1, g2):
        dx1, dg1 = _rms_bwd(d1, cq, g1)
        dx2, dg2 = _rms_bwd(d2, ckv, g2)
        return jnp.concatenate([dx1, dx2, dkr], axis=1), dg1, dg2

    return _rows(fn, name, [Rw(dcqn), Rw(dckvn), Rw(dkrp), Rw(proj, Q_RANK, C_CQ // Q_RANK), Rw(proj, KV_RANK, C_CKV // KV_RANK),
                            Pm(qn.reshape(1, Q_RANK)), Pm(kvn.reshape(1, KV_RANK))],
                 [Into(dproj, C_CQ, PROJ_COLS - C_CQ)], accs=[(1, Q_RANK), (1, KV_RANK)])


def _fold_rope(z):
    return z + pltpu.roll(z, QK_ROPE, 1)


def _mla_pack(qext, kv, proj, cs, name):
    tp = qext.shape[0]
    tm = _tile(tp, 384)
    hw, nope_all = 2 * LANE, HEADS * QK_NOPE

    def body(q_ref, kv_ref, kr_ref, cs_ref, qo_ref, ko_ref, vo_ref):
        cs_ = cs_ref[...]
        low = lax.broadcasted_iota(jnp.int32, cs_.shape, 1) < QK_ROPE
        kr = _fold_rope(kr_ref[...] * cs_).astype(BF16)
        for h in range(HEADS):
            qr = jnp.where(low, _fold_rope(q_ref[:, h * hw + QK_NOPE:(h + 1) * hw] * cs_), 0.0)
            qo_ref[:, h * hw:h * hw + QK_NOPE] = (q_ref[:, h * hw:h * hw + QK_NOPE] * ATT_SCALE).astype(BF16)
            qo_ref[:, h * hw + QK_NOPE:(h + 1) * hw] = (qr * ATT_SCALE).astype(BF16)
            ko_ref[:, h * hw:h * hw + QK_NOPE] = kv_ref[:, h * QK_NOPE:(h + 1) * QK_NOPE].astype(BF16)
            ko_ref[:, h * hw + QK_NOPE:(h + 1) * hw] = kr
        vo_ref[...] = kv_ref[:, nope_all:].astype(BF16)

    row = lambda w: pl.BlockSpec((tm, w), lambda i: (i, 0))
    return pl.pallas_call(
        body, name=name, grid=(tp // tm,),
        in_specs=[row(HEADS * hw), row(2 * nope_all), pl.BlockSpec((tm, LANE), lambda i: (i, C_KRP // LANE)), row(LANE)],
        out_specs=[row(HEADS * hw), row(HEADS * hw), row(nope_all)],
        out_shape=[jax.ShapeDtypeStruct((tp, HEADS * hw), BF16),
                   jax.ShapeDtypeStruct((tp, HEADS * hw), BF16),
                   jax.ShapeDtypeStruct((tp, nope_all), BF16)],
        compiler_params=_cparams("parallel"),
    )(qext, kv, proj, cs)


def _mla_unpack(dk, dv, cs, name):
    tp = dk.shape[0]
    tm = _tile(tp, 384)
    hw, nope_all = 2 * LANE, HEADS * QK_NOPE

    def body(dk_ref, dv_ref, cs_ref, dkv_ref, dkr_ref):
        cs_ = cs_ref[...]
        low = lax.broadcasted_iota(jnp.int32, cs_.shape, 1) < QK_ROPE
        dkr = None
        for h in range(HEADS):
            dkv_ref[:, h * QK_NOPE:(h + 1) * QK_NOPE] = dk_ref[:, h * hw:h * hw + QK_NOPE].astype(BF16)
            part = jnp.where(low, dk_ref[:, h * hw + QK_NOPE:(h + 1) * hw], 0.0)
            dkr = part if h == 0 else dkr + part
        dkv_ref[:, nope_all:] = dv_ref[...].astype(BF16)
        dkr_ref[...] = _fold_rope(dkr) * cs_

    row = lambda w: pl.BlockSpec((tm, w), lambda i: (i, 0))
    return pl.pallas_call(
        body, name=name, grid=(tp // tm,),
        in_specs=[row(HEADS * hw), row(nope_all), row(LANE)],
        out_specs=[row(2 * nope_all), row(LANE)],
        out_shape=[jax.ShapeDtypeStruct((tp, 2 * nope_all), BF16), jax.ShapeDtypeStruct((tp, LANE), F32)],
        compiler_params=_cparams("parallel"),
    )(dk, dv, cs)


def _attn_fwd(q, k, v, t_real, name):
    tp = q.shape[0]
    tq = _tile(tp, 1408)
    tkc = _tile(tp, 704, 64)
    nkc = -(-t_real // tkc)

    def body(q_ref, k_ref, v_ref, o_ref, lse_ref):
        qb = q_ref[...]
        m = l = acc = None
        for c in range(nkc):
            s = lax.dot_general(qb, k_ref[c * tkc:(c + 1) * tkc, :], (((1,), (1,)), ((), ())), preferred_element_type=F32)
            if (c + 1) * tkc > t_real:
                cols = lax.broadcasted_iota(jnp.int32, s.shape, 1) + c * tkc
                s = jnp.where(cols < t_real, s, NEG_BIG)
            mc = jnp.max(s, axis=-1, keepdims=True)
            m_new = mc if c == 0 else jnp.maximum(m, mc)
            p = jnp.exp2(s - m_new)
            lc = jnp.sum(p, axis=-1, keepdims=True)
            pv = jnp.dot(p.astype(BF16), v_ref[c * tkc:(c + 1) * tkc, :], preferred_element_type=F32)
            if c == 0:
                l, acc = lc, pv
            else:
                alpha = jnp.exp2(m - m_new)
                l, acc = alpha * l + lc, alpha * acc + pv
            m = m_new
        o_ref[...] = acc / l
        lse_ref[...] = m + jnp.log2(l)

    return pl.pallas_call(
        body, name=name, grid=(HEADS, tp // tq),
        in_specs=[pl.BlockSpec((tq, 2 * LANE), lambda h, i: (i, h)),
                  pl.BlockSpec((tp, 2 * LANE), lambda h, i: (0, h)),
                  pl.BlockSpec((tp, LANE), lambda h, i: (0, h))],
        out_specs=[pl.BlockSpec((tq, LANE), lambda h, i: (i, h)),
                   pl.BlockSpec((None, tq, 1), lambda h, i: (h, i, 0))],
        out_shape=[jax.ShapeDtypeStruct((tp, HEADS * LANE), F32),
                   jax.ShapeDtypeStruct((HEADS, tp, 1), F32)],
        compiler_params=_cparams("parallel", "parallel"),
    )(q, k, v)


def _attn_bwd(q, k, v, do, o, lse, cs, t_real, name):
    tp = q.shape[0]
    tq = _tile(tp, 1408)
    tkc = _tile(tp, 704, 64)
    nkc = -(-t_real // tkc)

    def body(q_ref, k_ref, v_ref, do_ref, o_ref, lse_ref, cs_ref, dq_ref, dk_ref, dv_ref):
        i = pl.program_id(1)

        @pl.when(i == 0)
        def _():
            dk_ref[...] = jnp.zeros_like(dk_ref)
            dv_ref[...] = jnp.zeros_like(dv_ref)

        qb = q_ref[...]
        dob = do_ref[...]
        dob16 = dob.astype(BF16)
        dol2 = (dob * LN2).astype(BF16)
        delta = jnp.sum(dob * o_ref[...], axis=-1, keepdims=True) * LN2
        lse = lse_ref[...]
        dq = None
        for c in range(nkc):
            ks = slice(c * tkc, (c + 1) * tkc)
            kb = k_ref[ks, :]
            s = lax.dot_general(qb, kb, (((1,), (1,)), ((), ())), preferred_element_type=F32)
            p = jnp.exp2(s - lse)
            if (c + 1) * tkc > t_real:
                cols = lax.broadcasted_iota(jnp.int32, s.shape, 1) + c * tkc
                p = jnp.where(cols < t_real, p, 0.0)
            dp = lax.dot_general(dol2, v_ref[ks, :], (((1,), (1,)), ((), ())), preferred_element_type=F32)
            ds = (p * (dp - delta)).astype(BF16)
            dqc = jnp.dot(ds, kb, preferred_element_type=F32)
            dq = dqc if c == 0 else dq + dqc
            dk_ref[ks, :] += lax.dot_general(ds, qb, (((0,), (0,)), ((), ())), preferred_element_type=F32)
            dv_ref[ks, :] += lax.dot_general(p.astype(BF16), dob16, (((0,), (0,)), ((), ())), preferred_element_type=F32)
        cs_ = cs_ref[...]
        low = lax.broadcasted_iota(jnp.int32, cs_.shape, 1) < QK_ROPE
        dq = dq * ATT_SCALE
        dq_ref[:, :QK_NOPE] = dq[:, :QK_NOPE].astype(BF16)
        dq_ref[:, QK_NOPE:] = (_fold_rope(jnp.where(low, dq[:, QK_NOPE:], 0.0)) * cs_).astype(BF16)

    return pl.pallas_call(
        body, name=name, grid=(HEADS, tp // tq),
        in_specs=[pl.BlockSpec((tq, 2 * LANE), lambda h, i: (i, h)),
                  pl.BlockSpec((tp, 2 * LANE), lambda h, i: (0, h)),
                  pl.BlockSpec((tp, LANE), lambda h, i: (0, h)),
                  pl.BlockSpec((tq, LANE), lambda h, i: (i, h)),
                  pl.BlockSpec((tq, LANE), lambda h, i: (i, h)),
                  pl.BlockSpec((None, tq, 1), lambda h, i: (h, i, 0)),
                  pl.BlockSpec((tq, LANE), lambda h, i: (i, 0))],
        out_specs=[pl.BlockSpec((tq, 2 * LANE), lambda h, i: (i, h)),
                   pl.BlockSpec((tp, 2 * LANE), lambda h, i: (0, h)),
                   pl.BlockSpec((tp, LANE), lambda h, i: (0, h))],
        out_shape=[jax.ShapeDtypeStruct((tp, HEADS * 2 * LANE), BF16),
                   jax.ShapeDtypeStruct((tp, HEADS * 2 * LANE), F32),
                   jax.ShapeDtypeStruct((tp, HEADS * LANE), F32)],
        compiler_params=_cparams("parallel", "arbitrary"),
    )(q, k, v, do, o, lse, cs)


def _shift_rows(x, s):
    tp = x.shape[0]
    return x if s % tp == 0 else pltpu.roll(x, s % tp, 0)


def _conv_fwd_val(xm, w, b, pad_left):
    acc = b + w[0:1, :] * _shift_rows(xm, pad_left)
    for k in range(1, w.shape[0]):
        acc = acc + w[k:k + 1, :] * _shift_rows(xm, pad_left - k)
    return acc


def _conv_bwd_val(dy, xm, w, pad_left, live):
    kk = w.shape[0]
    dx = w[0:1, :] * _shift_rows(dy, -pad_left)
    dws = [jnp.sum(dy * _shift_rows(xm, pad_left), axis=0, keepdims=True)]
    for k in range(1, kk):
        dx = dx + w[k:k + 1, :] * _shift_rows(dy, k - pad_left)
        dws.append(jnp.sum(dy * _shift_rows(xm, pad_left - k), axis=0, keepdims=True))
    return jnp.where(live, dx, 0.0), jnp.concatenate(dws, axis=0), jnp.sum(dy, axis=0, keepdims=True)


def _lru_conv_fwd(proj, w, b, t_real, name):
    def fn(x, ww, bb):
        xm = jnp.where(_row_ids(x.shape) < t_real, x, 0.0)
        return _conv_fwd_val(xm, ww, bb, 2)

    return _cols(fn, name, [Cl(proj, C_LRU_X), Cl(w), Cl(b.reshape(1, -1))], [(proj.shape[0], F32)], D_MODEL, 128)[0]


def _lru_conv_bwd(dxc, proj, w, dproj, t_real, name):
    def fn(dy, x, ww):
        live = _row_ids(x.shape) < t_real
        xm = jnp.where(live, x, 0.0)
        dym = jnp.where(live, dy, 0.0)
        return _conv_bwd_val(dym, xm, ww, 2, live)

    return _cols(fn, name, [Cl(dxc), Cl(proj, C_LRU_X), Cl(w)],
                 [Into(dproj, C_LRU_X, D_MODEL), (w.shape[0], F32), (1, F32)], D_MODEL, 128)


def _ffn_conv_act(up, w, b, t_real, name):
    def fn(g, v, wg, wv, bg, bv):
        live = _row_ids(g.shape) < t_real
        gc = _conv_fwd_val(jnp.where(live, g, 0.0), wg, bg, 1)
        vc = _conv_fwd_val(jnp.where(live, v, 0.0), wv, bv, 1)
        return _gelu(gc) * vc

    b2 = b.reshape(1, -1)
    return _cols(fn, name, [Cl(up), Cl(up, D_FF), Cl(w), Cl(w, D_FF), Cl(b2), Cl(b2, D_FF)],
                 [(up.shape[0], BF16)], D_FF, 128)[0]


def _ffn_conv_act_bwd(dm, up, w, b, t_real, name):
    tp, kk = up.shape[0], w.shape[0]
    nb = D_FF // LANE
    assert nb >= 2

    def body(dm_ref, g_ref, v_ref, wg_ref, wv_ref, bg_ref, bv_ref, dup_ref, dwg_ref, dwv_ref, dbg_ref, dbv_ref, stage, sems):
        j = pl.program_id(0)
        slot = j % 2

        def copies(step, sl):
            return [pltpu.make_async_copy(stage.at[sl, half],
                                          dup_ref.at[:, pl.ds(pl.multiple_of(half * D_FF + step * LANE, LANE), LANE)],
                                          sems.at[sl, half]) for half in range(2)]

        @pl.when(j >= 2)
        def _():
            for cp in copies(j - 2, slot):
                cp.wait()

        live = _row_ids((tp, LANE)) < t_real
        gm, vm = jnp.where(live, g_ref[...], 0.0), jnp.where(live, v_ref[...], 0.0)
        gc = _conv_fwd_val(gm, wg_ref[...], bg_ref[...], 1)
        vc = _conv_fwd_val(vm, wv_ref[...], bv_ref[...], 1)
        act, dact = _gelu_and_grad(gc)
        dmm = jnp.where(live, dm_ref[...], 0.0)
        dgx, dwg_ref[...], dbg_ref[...] = _conv_bwd_val(dmm * vc * dact, gm, wg_ref[...], 1, live)
        dvx, dwv_ref[...], dbv_ref[...] = _conv_bwd_val(dmm * act, vm, wv_ref[...], 1, live)
        stage[slot, 0] = dgx.astype(BF16)
        stage[slot, 1] = dvx.astype(BF16)
        for cp in copies(j, slot):
            cp.start()

        @pl.when(j == nb - 1)
        def _():
            for cp in copies(j - 1, 1 - slot) + copies(j, slot):
                cp.wait()

    b2 = b.reshape(1, -1)
    col = lambda rows, off: pl.BlockSpec((rows, LANE), functools.partial(lambda j, o: (0, j + o), o=off))
    return pl.pallas_call(
        body, name=name, grid=(nb,),
        in_specs=[col(tp, 0), col(tp, 0), col(tp, nb), col(kk, 0), col(kk, nb), col(1, 0), col(1, nb)],
        out_specs=[pl.BlockSpec(memory_space=pl.ANY), col(kk, 0), col(kk, 0), col(1, 0), col(1, 0)],
        out_shape=[jax.ShapeDtypeStruct((tp, 2 * D_FF), BF16), jax.ShapeDtypeStruct((kk, D_FF), F32),
                   jax.ShapeDtypeStruct((kk, D_FF), F32), jax.ShapeDtypeStruct((1, D_FF), F32),
                   jax.ShapeDtypeStruct((1, D_FF), F32)],
        scratch_shapes=[pltpu.VMEM((2, 2, tp, LANE), BF16), pltpu.SemaphoreType.DMA((2, 2))],
        compiler_params=_cparams("arbitrary"),
    )(dm, up, up, w, w, b2, b2)


def _lru_gates_fwd(xc, wg, b4, lam, t_real, name):
    tp = xc.shape[0]
    tm = _tile(tp, 1408)

    def body(x_ref, w_ref, b_ref, lam_ref, r0_ref, r1_ref, i0_ref, i1_ref, a0_ref, a1_ref, u0_ref, u1_ref):
        x = x_ref[...]
        xb = x.astype(BF16)
        live = _row_ids(x.shape, pl.program_id(1) * tm) < t_real
        bb = b_ref[...]
        sp = _softplus_neg(lam_ref[...])
        gate = [_sigmoid(jnp.dot(xb, w_ref[k], preferred_element_type=F32) + bb[k:k + 1, :]) for k in range(4)]
        for d, (r_ref, i_ref, a_ref, u_ref) in enumerate(((r0_ref, i0_ref, a0_ref, u0_ref), (r1_ref, i1_ref, a1_ref, u1_ref))):
            r, ig = gate[d], gate[2 + d]
            a = jnp.exp(-LRU_C * r * sp[d:d + 1, :])
            r_ref[...] = r
            i_ref[...] = ig
            a_ref[...] = a
            u_ref[...] = jnp.where(live, jnp.sqrt(1.0 - a * a) * (ig * x), 0.0)

    blk = pl.BlockSpec((tm, LANE), lambda g, i: (i, g))
    return pl.pallas_call(
        body, name=name, grid=(LRU_BLOCKS, tp // tm),
        in_specs=[blk, pl.BlockSpec((None, 4, LANE, LANE), lambda g, i: (g, 0, 0, 0)),
                  pl.BlockSpec((4, LANE), lambda g, i: (0, g)), pl.BlockSpec((2, LANE), lambda g, i: (0, g))],
        out_specs=[blk] * 8,
        out_shape=[jax.ShapeDtypeStruct((tp, D_MODEL), F32)] * 8,
        compiler_params=_cparams("parallel", "parallel"),
    )(xc, wg, b4, lam)


def _lru_gates_bwd(l0, l1, da0, da1, r0, r1, i0, i1, a0, a1, xc, wg, lam, t_real, name):
    tp = xc.shape[0]
    tm = _tile(tp, 1408)

    rc = 32
    assert tm % rc == 0

    def fold(v):
        out = v[0:SUBLANE]
        for t in range(1, rc // SUBLANE):
            out = out + v[t * SUBLANE:(t + 1) * SUBLANE]
        return out

    def body(l0_ref, l1_ref, da0_ref, da1_ref, r0_ref, r1_ref, i0_ref, i1_ref, a0_ref, a1_ref, x_ref, w_ref, lam_ref,
             dx_ref, dw_ref, db_ref, dlam_ref, pre_s, dxp_s):
        i = pl.program_id(1)
        lam_ = lam_ref[...]
        sp = _softplus_neg(lam_)
        dsp_dlam = -_sigmoid(-lam_)

        def chunk(c, sums):
            r0 = pl.multiple_of(c * rc, rc)
            rows = pl.ds(r0, rc)
            x = x_ref[rows, :]
            live = _row_ids((rc, LANE), i * tm + r0) < t_real
            dxp = jnp.zeros_like(x)
            sums = list(sums)
            for d, (l_ref, da_ref, r_ref, i_ref, a_ref) in enumerate(((l0_ref, da0_ref, r0_ref, i0_ref, a0_ref),
                                                                      (l1_ref, da1_ref, r1_ref, i1_ref, a1_ref))):
                r, ig, a = r_ref[rows, :], i_ref[rows, :], a_ref[rows, :]
                du = jnp.where(live, l_ref[rows, :], 0.0)
                a2 = a * a
                rs = lax.rsqrt(1.0 - a2)
                dv = du * ((1.0 - a2) * rs)
                ds = du * (ig * x)
                dla = jnp.where(live, da_ref[rows, :] * a - ds * (a2 * rs), 0.0)
                dr = dla * (-LRU_C) * sp[d:d + 1, :]
                p_r = dr * r * (1.0 - r)
                p_i = dv * x * ig * (1.0 - ig)
                pre_s[d, rows, :] = p_r.astype(BF16)
                pre_s[2 + d, rows, :] = p_i.astype(BF16)
                dxp = dxp + dv * ig
                sums[d] = sums[d] + fold(p_r)
                sums[2 + d] = sums[2 + d] + fold(p_i)
                sums[4 + d] = sums[4 + d] + fold(dla * (-LRU_C) * r)
            dxp_s[rows, :] = dxp
            return tuple(sums)

        zero = jnp.zeros((SUBLANE, LANE), F32)
        sums = lax.fori_loop(0, tm // rc, chunk, (zero,) * 6)

        @pl.when(i == 0)
        def _():
            dw_ref[...] = jnp.zeros_like(dw_ref)
            db_ref[...] = jnp.zeros_like(db_ref)
            dlam_ref[...] = jnp.zeros_like(dlam_ref)

        xb = x_ref[...].astype(BF16)
        dx = dxp_s[...]
        for k in range(4):
            pk = pre_s[k]
            dx = dx + lax.dot_general(pk, w_ref[k], (((1,), (1,)), ((), ())), preferred_element_type=F32)
            dw_ref[k] += lax.dot_general(xb, pk, (((0,), (0,)), ((), ())), preferred_element_type=F32)
        db_ref[...] += jnp.concatenate([jnp.sum(sums[k], axis=0, keepdims=True) for k in range(4)], axis=0)
        dlam_ref[...] += jnp.concatenate([jnp.sum(sums[4 + d], axis=0, keepdims=True) * dsp_dlam[d:d + 1, :] for d in range(2)], axis=0)
        dx_ref[...] = dx

    blk = pl.BlockSpec((tm, LANE), lambda g, i: (i, g))
    return pl.pallas_call(
        body, name=name, grid=(LRU_BLOCKS, tp // tm),
        in_specs=[blk] * 11 + [pl.BlockSpec((None, 4, LANE, LANE), lambda g, i: (g, 0, 0, 0)),
                               pl.BlockSpec((2, LANE), lambda g, i: (0, g))],
        out_specs=[blk, pl.BlockSpec((None, 4, LANE, LANE), lambda g, i: (g, 0, 0, 0)),
                   pl.BlockSpec((4, LANE), lambda g, i: (0, g)), pl.BlockSpec((2, LANE), lambda g, i: (0, g))],
        out_shape=[jax.ShapeDtypeStruct((tp, D_MODEL), F32), jax.ShapeDtypeStruct((LRU_BLOCKS, 4, LANE, LANE), F32),
                   jax.ShapeDtypeStruct((4, D_MODEL), F32), jax.ShapeDtypeStruct((2, D_MODEL), F32)],
        scratch_shapes=[pltpu.VMEM((4, tm, LANE), BF16), pltpu.VMEM((tm, LANE), F32)],
        compiler_params=_cparams("parallel", "arbitrary"),
    )(l0, l1, da0, da1, r0, r1, i0, i1, a0, a1, xc, wg, lam)


SCAN_UNROLL = 4


def _loop_tiles(nt, step, carry):
    assert nt % SCAN_UNROLL == 0

    def trip(tt, c):
        for u in range(SCAN_UNROLL):
            c = step(tt * SCAN_UNROLL + u, c)
        return c

    return lax.fori_loop(0, nt // SCAN_UNROLL, trip, carry)


def _tile_scan(a, u, reverse):
    rows = lax.broadcasted_iota(jnp.int32, a.shape, 0)
    for s in (1, 2, 4):
        if reverse:
            keep = rows < SUBLANE - s
            a_sh, u_sh = pltpu.roll(a, SUBLANE - s, 0), pltpu.roll(u, SUBLANE - s, 0)
        else:
            keep = rows >= s
            a_sh, u_sh = pltpu.roll(a, s, 0), pltpu.roll(u, s, 0)
        u = u + a * jnp.where(keep, u_sh, 0.0)
        a = a * jnp.where(keep, a_sh, 1.0)
    return a, u


def _scan_fwd(a0, u0, a1, u1, proj, name):
    tp, d = a0.shape
    tc = 128
    nt = tp // SUBLANE

    def body(a0_ref, u0_ref, a1_ref, u1_ref, lg_ref, h0_ref, h1_ref, gh_ref):
        def step(t, carry):
            c0, c1 = carry
            f = pl.multiple_of(t * SUBLANE, SUBLANE)
            b = pl.multiple_of((nt - 1 - t) * SUBLANE, SUBLANE)
            pa, pu = _tile_scan(a0_ref[pl.ds(f, SUBLANE), :], u0_ref[pl.ds(f, SUBLANE), :], False)
            h = pu + pa * c0
            h0_ref[pl.ds(f, SUBLANE), :] = h
            c0 = h[SUBLANE - 1:SUBLANE, :]
            pa, pu = _tile_scan(a1_ref[pl.ds(b, SUBLANE), :], u1_ref[pl.ds(b, SUBLANE), :], True)
            h = pu + pa * c1
            h1_ref[pl.ds(b, SUBLANE), :] = h
            c1 = h[0:1, :]
            return c0, c1

        z = jnp.zeros((1, tc), F32)
        _loop_tiles(nt, step, (z, z))
        gh_ref[...] = (_gelu(lg_ref[...]) * (h0_ref[...] + h1_ref[...])).astype(BF16)

    blk = pl.BlockSpec((tp, tc), lambda j: (0, j))
    return pl.pallas_call(
        body, name=name, grid=(d // tc,),
        in_specs=[blk] * 4 + [pl.BlockSpec((tp, tc), lambda j: (0, j + C_LRU_G // tc))], out_specs=[blk] * 3,
        out_shape=[jax.ShapeDtypeStruct((tp, d), F32)] * 2 + [jax.ShapeDtypeStruct((tp, d), BF16)],
        compiler_params=_cparams("parallel"),
    )(a0, u0, a1, u1, proj)


def _scan_bwd(dh, a0, a1, h0, h1, name):
    tp, d = dh.shape
    tc = 128
    nt = tp // SUBLANE

    def body(dh_ref, a0_ref, a1_ref, h0_ref, h1_ref, l0_ref, l1_ref, da0_ref, da1_ref):
        rows8 = lax.broadcasted_iota(jnp.int32, (SUBLANE, tc), 0)

        def step(t, carry):
            c0, c1 = carry
            b = pl.multiple_of((nt - 1 - t) * SUBLANE, SUBLANE)
            f = pl.multiple_of(t * SUBLANE, SUBLANE)
            a = a0_ref[pl.ds(b, SUBLANE), :]
            a_next = jnp.where(rows8 < SUBLANE - 1, pltpu.roll(a, SUBLANE - 1, 0), 1.0)
            pa, pu = _tile_scan(a_next, dh_ref[pl.ds(b, SUBLANE), :], True)
            lam = pu + pa * c0
            l0_ref[pl.ds(b, SUBLANE), :] = lam
            c0 = a[0:1, :] * lam[0:1, :]
            a = a1_ref[pl.ds(f, SUBLANE), :]
            a_prev = jnp.where(rows8 >= 1, pltpu.roll(a, 1, 0), 1.0)
            pa, pu = _tile_scan(a_prev, dh_ref[pl.ds(f, SUBLANE), :], False)
            lam = pu + pa * c1
            l1_ref[pl.ds(f, SUBLANE), :] = lam
            c1 = a[SUBLANE - 1:SUBLANE, :] * lam[SUBLANE - 1:SUBLANE, :]
            return c0, c1

        z = jnp.zeros((1, tc), F32)
        _loop_tiles(nt, step, (z, z))
        rows = lax.broadcasted_iota(jnp.int32, (tp, tc), 0)
        da0_ref[...] = l0_ref[...] * jnp.where(rows >= 1, pltpu.roll(h0_ref[...], 1, 0), 0.0)
        da1_ref[...] = l1_ref[...] * jnp.where(rows < tp - 1, pltpu.roll(h1_ref[...], tp - 1, 0), 0.0)

    blk = pl.BlockSpec((tp, tc), lambda j: (0, j))
    return pl.pallas_call(
        body, name=name, grid=(d // tc,), in_specs=[blk] * 5, out_specs=[blk] * 4,
        out_shape=[jax.ShapeDtypeStruct((tp, d), F32)] * 4,
        compiler_params=_cparams("parallel"),
    )(dh, a0, a1, h0, h1)


def _gated_h_bwd(dgh, proj, h0, h1, dproj, name):
    def fn(row0, dg, lg, x0, x1):
        act, dact = _gelu_and_grad(lg)
        return dg * (x0 + x1) * dact, dg * act

    return _rows(fn, name, [Rw(dgh), Rw(proj, D_MODEL, C_LRU_G // D_MODEL), Rw(h0), Rw(h1)],
                 [Into(dproj, C_LRU_G, D_MODEL), (D_MODEL, F32)])


def _mix(proj, y_mla, y_lru, name):
    def fn(row0, gm, gl, ym, yl):
        return _sigmoid(gm) * ym + _sigmoid(gl) * yl

    return _rows(fn, name, [Rw(proj, D_MODEL, C_G_MLA // D_MODEL), Rw(proj, D_MODEL, C_G_LRU // D_MODEL), Rw(y_mla), Rw(y_lru)],
                 [(D_MODEL, BF16)])[0]


def _mix_bwd(dz, proj, y_mla, y_lru, dproj, name):
    def fn(row0, dzb, gm, gl, ym, yl):
        sm, sl = _sigmoid(gm), _sigmoid(gl)
        dg = jnp.concatenate([dzb * ym * sm * (1.0 - sm), dzb * yl * sl * (1.0 - sl)], axis=1)
        return dzb * sm, dzb * sl, dg

    return _rows(fn, name, [Rw(dz), Rw(proj, D_MODEL, C_G_MLA // D_MODEL), Rw(proj, D_MODEL, C_G_LRU // D_MODEL),
                            Rw(y_mla), Rw(y_lru)], [(D_MODEL, BF16), (D_MODEL, BF16), Into(dproj, C_G_MLA, 2 * D_MODEL)])


def _layer_fwd(h, w_in, more_weights, cs, t_real, tag):
    proj = _matmul(h, w_in, tag + "proj", tb=True)
    w = dict(more_weights(0, proj), w_in=w_in)
    cqn, ckvn = _mla_norms(proj, w['q_norm'], w['kv_norm'], tag + "mla_norms")
    qext = _matmul(cqn, w['w_q'], tag + "q_up")
    kv = _matmul(ckvn, w['w_kv'], tag + "kv_up")
    qc, kc, vb = _mla_pack(qext, kv, proj, cs, tag + "mla_pack")
    o, lse = _attn_fwd(qc, kc, vb, t_real, tag + "attn_fwd")
    w.update(more_weights(1, o))
    y_mla = _matmul(o, w['w_o_mla'], tag + "o_mla")
    xc = _lru_conv_fwd(proj, w['lru_conv_w'], w['lru_conv_b'], t_real, tag + "lru_conv")
    r0, r1, i0, i1, a0, a1, u0, u1 = _lru_gates_fwd(xc, w['w_g'], w['b4'], w['lru_lambda'], t_real, tag + "lru_gates")
    h0, h1, gh = _scan_fwd(a0, u0, a1, u1, proj, tag + "lru_scan")
    y_lru = _matmul(gh, w['w_o_lru'], tag + "o_lru")
    z = _mix(proj, y_mla, y_lru, tag + "mix")
    zo = _matmul(z, w['w_out'], tag + "w_out")
    hm = _ln_fwd([(DN_ALPHA, h), (1.0, zo)], w['ln1_g'], w['ln1_b'], tag + "ln1")
    w.update(more_weights(2, hm))
    up = _matmul(hm, w['w_up'], tag + "w_up", tb=True)
    m = _ffn_conv_act(up, w['ffn_conv_w'], w['ffn_conv_b'], t_real, tag + "ffn_conv")
    f = _matmul(m, w['w_down'], tag + "w_down", tk_cap=1408)
    out = _ln_fwd([(DN_ALPHA, hm), (1.0, f)], w['ln2_g'], w['ln2_b'], tag + "ln2")
    saved = dict(w=w, h=h, proj=proj, cqn=cqn, ckvn=ckvn, qc=qc, kc=kc, vb=vb, o=o, lse=lse, y_mla=y_mla, xc=xc,
                 r0=r0, r1=r1, i0=i0, i1=i1, a0=a0, a1=a1, h0=h0, h1=h1, gh=gh, y_lru=y_lru, z=z, zo=zo, hm=hm,
                 up=up, m=m, f=f)
    return out, saved


DW_MATMUL = dict(ta=True, out_dtype=BF16, tn_cap=1408, tk_cap=1408)


def _after(a, tok):
    return a if tok is None else a + tok.astype(a.dtype)


def _layer_bwd(dout_terms, s, cs, t_real, tag, emit, tok):
    w = s['w']
    g = {}
    du2, dg2, db2 = _ln_bwd(dout_terms, [(DN_ALPHA, s['hm']), (1.0, s['f'])], _after(w['ln2_g'], tok), tag + "ln2_bwd")
    g['ln2_g'], g['ln2_b'] = dg2, db2
    dm = _matmul(du2, w['w_down'], tag + "w_down_dx", tb=True)
    g['w_down'] = _matmul(s['m'], du2, tag + "w_down_dw", **DW_MATMUL)
    dup, dwg_, dwv_, dbg_, dbv_ = _ffn_conv_act_bwd(dm, s['up'], w['ffn_conv_w'], w['ffn_conv_b'], t_real, tag + "ffn_conv_bwd")
    g['ffn_conv_w'] = jnp.concatenate([dwg_, dwv_], axis=1)
    g['ffn_conv_b'] = jnp.concatenate([dbg_, dbv_], axis=1)
    dhm_mm = _matmul(dup, w['w_up'], tag + "w_up_dx", tk_cap=1408)
    g['w_up'] = _matmul(s['hm'], dup, tag + "w_up_dw", **DW_MATMUL)
    tok = emit('ffn', g)
    g = {}
    du1, dg1, db1 = _ln_bwd([(DN_ALPHA, du2), (1.0, dhm_mm)], [(DN_ALPHA, s['h']), (1.0, s['zo'])],
                            _after(w['ln1_g'], tok), tag + "ln1_bwd")
    g['ln1_g'], g['ln1_b'] = dg1, db1
    dz = _matmul(du1, w['w_out'], tag + "w_out_dx", tb=True)
    g['w_out'] = _matmul(s['z'], du1, tag + "w_out_dw", **DW_MATMUL)
    dproj = lax.empty(s['proj'].shape, BF16)
    dy_mla, dy_lru, dproj = _mix_bwd(dz, s['proj'], s['y_mla'], s['y_lru'], dproj, tag + "mix_bwd")
    do = _matmul(dy_mla, w['w_o_mla'], tag + "o_mla_dx", tb=True)
    g['w_o_mla'] = _matmul(s['o'], dy_mla, tag + "o_mla_dw", **DW_MATMUL)
    dqext, dkc, dv = _attn_bwd(s['qc'], s['kc'], s['vb'], do, s['o'], s['lse'], cs, t_real, tag + "attn_bwd")
    dkv, dkrp = _mla_unpack(dkc, dv, cs, tag + "mla_unpack")
    dcqn = _matmul(dqext, w['w_q'], tag + "q_up_dx", tb=True)
    g['w_q'] = _matmul(s['cqn'], dqext, tag + "q_up_dw", **DW_MATMUL)
    dckvn = _matmul(dkv, w['w_kv'], tag + "kv_up_dx", tb=True)
    g['w_kv'] = _matmul(s['ckvn'], dkv, tag + "kv_up_dw", **DW_MATMUL)
    dgh = _matmul(dy_lru, w['w_o_lru'], tag + "o_lru_dx", tb=True)
    g['w_o_lru'] = _matmul(s['gh'], dy_lru, tag + "o_lru_dw", **DW_MATMUL)
    dproj, dhs = _gated_h_bwd(dgh, s['proj'], s['h0'], s['h1'], dproj, tag + "lru_gate_out_bwd")
    l0, l1, da0, da1 = _scan_bwd(dhs, s['a0'], s['a1'], s['h0'], s['h1'], tag + "lru_scan_bwd")
    dxc, g['w_g'], g['b4'], g['lru_lambda'] = _lru_gates_bwd(
        l0, l1, da0, da1, s['r0'], s['r1'], s['i0'], s['i1'], s['a0'], s['a1'], s['xc'], w['w_g'], w['lru_lambda'],
        t_real, tag + "lru_gates_bwd")
    dproj, g['lru_conv_w'], g['lru_conv_b'] = _lru_conv_bwd(dxc, s['proj'], w['lru_conv_w'], dproj, t_real, tag + "lru_conv_bwd")
    tok = emit('mid', g)
    dproj, dqn, dkvn = _mla_norms_bwd(dcqn, dckvn, _after(dkrp, tok), s['proj'], w['q_norm'], w['kv_norm'], dproj,
                                      tag + "mla_norms_bwd")
    tok = emit('in', {'w_in': _matmul(s['h'], dproj, tag + "proj_dw", **DW_MATMUL), 'q_norm': dqn, 'kv_norm': dkvn})
    dh_mm = _matmul(dproj, w['w_in'], tag + "proj_dx", tk_cap=1536)
    return [(DN_ALPHA, du1), (1.0, dh_mm)], tok


def _swap_halves(a, axis=-1):
    h1, h2 = jnp.split(a, 2, axis=axis)
    return jnp.concatenate([h2, h1], axis=axis)


def _w_in_kernel(w_in_t):
    cq, ckv, kr, lg, lx, gm, gl = jnp.split(w_in_t, [256, 384, 448, 1472, 2496, 3520], axis=0)
    return jnp.concatenate([lg, lx, gm, gl, cq, ckv, kr, _swap_halves(kr, axis=0)], axis=0)


def _layer_weights(fl):
    w = {}
    if 'w_uq' in fl:
        uq = fl['w_uq']
        w['w_q'] = jnp.concatenate([uq, _swap_halves(uq[..., QK_NOPE:])], axis=-1).reshape(Q_RANK, HEADS * 2 * LANE)
        w['w_kv'] = jnp.concatenate([fl['w_uk'].reshape(KV_RANK, -1), fl['w_uv'].reshape(KV_RANK, -1)], axis=1).astype(BF16)
        w['w_g'] = jnp.moveaxis(jnp.concatenate([fl['w_rg'], fl['w_ig']], axis=0), 0, 1).astype(BF16)
        w['b4'] = jnp.concatenate([fl['b_rg'], fl['b_ig']], axis=0)
    for n in ('q_norm', 'kv_norm', 'w_o_mla', 'lru_conv_w', 'lru_conv_b', 'lru_lambda', 'w_o_lru', 'w_out', 'ln1_g',
              'ln1_b', 'w_up', 'ffn_conv_w', 'ffn_conv_b', 'w_down', 'ln2_g', 'ln2_b'):
        if n in fl:
            w[n] = fl[n]
    return w


def _layer_grads(g):
    out = {}
    if 'w_in' in g:
        lg, lx, gm, gl, cq, ckv, kr, krs = jnp.split(g['w_in'], [1024, 2048, 3072, 4096, 4352, 4480, 4544], axis=1)
        out['w_in'] = jnp.concatenate([cq, ckv, kr + _swap_halves(krs), lg, lx, gm, gl], axis=1)
    if 'w_q' in g:
        gq = g['w_q'].reshape(Q_RANK, HEADS, 2 * LANE)
        out['w_uq'] = jnp.concatenate([gq[..., :QK_NOPE], gq[..., QK_NOPE:QK_NOPE + QK_ROPE] + _swap_halves(gq[..., QK_NOPE + QK_ROPE:])], axis=-1)
    if 'w_kv' in g:
        out['w_uk'] = g['w_kv'][:, :HEADS * QK_NOPE].reshape(KV_RANK, HEADS, QK_NOPE)
        out['w_uv'] = g['w_kv'][:, HEADS * QK_NOPE:].reshape(KV_RANK, HEADS, V_HEAD)
    if 'w_g' in g:
        gg = jnp.moveaxis(g['w_g'], 1, 0)
        out['w_rg'], out['w_ig'] = gg[:2], gg[2:]
    if 'b4' in g:
        out['b_rg'], out['b_ig'] = g['b4'][:2], g['b4'][2:]
    for n in ('q_norm', 'kv_norm', 'lru_conv_b', 'ln1_g', 'ln1_b', 'ffn_conv_b', 'ln2_g', 'ln2_b'):
        if n in g:
            out[n] = g[n].reshape(-1)
    for n in ('w_o_mla', 'lru_conv_w', 'lru_lambda', 'w_o_lru', 'w_out', 'w_up', 'ffn_conv_w', 'w_down'):
        if n in g:
            out[n] = g[n]
    return out


def _rope_table(tp):
    half = QK_ROPE // 2
    inv_freq = jnp.exp(-math.log(ROPE_THETA) * jnp.arange(half, dtype=F32) / half)
    ang = jnp.arange(tp, dtype=F32)[:, None] * inv_freq[None, :]
    c, s = jnp.cos(ang), jnp.sin(ang)
    return jnp.concatenate([c, c, -s, s], axis=1)


def _local_step(x, target, meta, ln0_g, ln0_b, layer_w, t_pad, emit):
    seq = x.shape[0]
    t_real = N_META + seq
    zpad = jnp.zeros((t_pad - t_real, D_MODEL), F32)
    xin = jnp.concatenate([meta, x, zpad], axis=0)
    tgt = jnp.concatenate([jnp.zeros((N_META, D_MODEL), F32), target, zpad], axis=0)
    cs = _rope_table(t_pad)
    h = _ln_fwd([(1.0, xin)], ln0_g, ln0_b, "ln0")
    saved = []
    for l in range(DEPTH):
        w_in, rest_of_weights = layer_w[l](h)
        h, s = _layer_fwd(h, w_in, rest_of_weights, cs, t_real, "l%d_" % l)
        saved.append(s)
    dy, lossvec = _loss_head(h, tgt, t_real, "loss_head")
    terms, tok = [(1.0, dy)], None
    for l in reversed(range(DEPTH)):
        terms, tok = _layer_bwd(terms, saved[l], cs, t_real, "l%d_" % l,
                                functools.partial(lambda stage, g, l: emit(l, stage, _layer_grads(g)), l=l), tok)
    dxin, dg0, db0 = _ln_bwd(terms, [(1.0, xin)], _after(ln0_g, tok), "ln0_bwd")
    emit(None, 'head', {'meta_tokens': dxin[:N_META], 'ln0_g': dg0.reshape(-1), 'ln0_b': db0.reshape(-1), 'loss': lossvec})
    return dxin[N_META:t_real]


_HBM = pl.BlockSpec(memory_space=pltpu.HBM)
_SEM = pl.BlockSpec(memory_space=pltpu.SEMAPHORE)
_SIDE_EFFECT = pltpu.SideEffectType.DATAFLOW_SIDE_EFFECTING


def _peer_copies(src_refs, land_refs, scatters, send_sems, recv_sems):
    x, y, c = lax.axis_index("x"), lax.axis_index("y"), lax.axis_index("c")
    me = 4 * x + 2 * y + c
    copies = []
    for k in range(1, N_DEV):
        px = 1 - x if k & 4 else x
        py = 1 - y if k & 2 else y
        pc = 1 - c if k & 1 else c
        for t, (src, land) in enumerate(zip(src_refs, land_refs)):
            copies.append(pltpu.make_async_remote_copy(
                src_ref=src.at[4 * px + 2 * py + pc] if scatters[t] else src, dst_ref=land.at[me],
                send_sem=send_sems.at[7 * t + k - 1], recv_sem=recv_sems.at[7 * t + k - 1],
                device_id=(px, py, pc), device_id_type=pl.DeviceIdType.MESH))
    return me, copies


def _own_block_in_place(land, own):
    me = 4 * lax.axis_index("x") + 2 * lax.axis_index("y") + lax.axis_index("c")
    return lax.dynamic_update_slice_in_dim(land, own, me, 0)


def _gather_two_level(shards, name):
    nt = len(shards)

    def body(*refs):
        x_refs, out_refs = refs[:nt], refs[nt:2 * nt]
        token_ref, send_sems, recv_sems = refs[2 * nt:]
        x, y, c = lax.axis_index("x"), lax.axis_index("y"), lax.axis_index("c")
        me, sibling = (x, y, c), (x, y, 1 - c)
        chips = [(1 - x, y), (x, 1 - y), (1 - x, 1 - y)]

        def copy(t, k, block, to, own=False):
            px, py, pc = block
            slot = out_refs[t].at[4 * px + 2 * py + pc]
            return pltpu.make_async_remote_copy(
                src_ref=x_refs[t] if own else slot, dst_ref=slot,
                send_sem=send_sems.at[7 * t + k], recv_sem=recv_sems.at[7 * t + k],
                device_id=to, device_id_type=pl.DeviceIdType.MESH)

        sent = []
        for t in range(nt):
            first = [copy(t, 1 + j, me, (*chip, c), own=True) for j, chip in enumerate(chips)]
            first.append(copy(t, 0, me, sibling, own=True))
            for cp in first:
                cp.start()
            sent += first
        token_ref[...] = jnp.zeros_like(token_ref)
        for j, chip in enumerate(chips):
            for t in range(nt):
                copy(t, 1 + j, (*chip, c), me).wait_recv()
                passed = copy(t, 4 + j, (*chip, c), sibling)
                passed.start()
                sent.append(passed)
        for t in range(nt):
            copy(t, 0, sibling, me).wait_recv()
            for j, chip in enumerate(chips):
                copy(t, 4 + j, (*chip, 1 - c), me).wait_recv()
        for cp in sent:
            cp.wait_send()

    any_space = pl.BlockSpec(memory_space=pl.ANY)
    res = pl.pallas_call(
        body, name=name,
        out_shape=[jax.ShapeDtypeStruct((N_DEV,) + a.shape, a.dtype) for a in shards] + [jax.ShapeDtypeStruct((SUBLANE, LANE), F32)],
        in_specs=[any_space] * nt, out_specs=[any_space] * nt + [pl.BlockSpec(memory_space=pltpu.VMEM)],
        scratch_shapes=[pltpu.SemaphoreType.DMA((7 * nt,)), pltpu.SemaphoreType.DMA((7 * nt,))],
    )(*shards)
    return [_own_block_in_place(land, a[None]) for land, a in zip(res[:nt], shards)], res[nt][0, 0]


def _exchange_start(groups, name):
    flat = [it for grp in groups for it in grp]
    nt, ng = len(flat), len(groups)
    scatters = [sc for _, sc in flat]
    srcs = [pltpu.with_memory_space_constraint(a, pltpu.HBM) for a, _ in flat]
    land_shapes = [a.shape if sc else (N_DEV,) + a.shape for a, sc in flat]
    lands = [pltpu.with_memory_space_constraint(lax.empty(s, a.dtype), pltpu.HBM) for s, (a, _) in zip(land_shapes, flat)]
    bounds = [0]
    for grp in groups:
        bounds.append(bounds[-1] + len(grp))

    def body(*refs):
        src_refs, land_refs = refs[:nt], refs[nt:2 * nt]
        sem_refs = refs[2 * nt:2 * nt + 2 * ng]
        token_ref = refs[4 * nt + 2 * ng]
        for gi in range(ng):
            lo, hi = bounds[gi], bounds[gi + 1]
            _, copies = _peer_copies(src_refs[lo:hi], land_refs[lo:hi], scatters[lo:hi], sem_refs[2 * gi], sem_refs[2 * gi + 1])
            for cp in copies:
                cp.start()
        token_ref[...] = jnp.zeros_like(token_ref)

    out_shape = []
    for grp in groups:
        out_shape += [pltpu.SemaphoreType.DMA((7 * len(grp),)), pltpu.SemaphoreType.DMA((7 * len(grp),))]
    out_shape += [pltpu.HBM(a.shape, a.dtype) for a in srcs] + [pltpu.HBM(s, a.dtype) for s, a in zip(land_shapes, srcs)]
    out_shape += [jax.ShapeDtypeStruct((SUBLANE, LANE), F32)]
    res = pl.pallas_call(
        body, name=name, out_shape=out_shape,
        in_specs=[_HBM] * (2 * nt),
        out_specs=[_SEM] * (2 * ng) + [_HBM] * (2 * nt) + [pl.BlockSpec(memory_space=pltpu.VMEM)],
        input_output_aliases={t: 2 * ng + t for t in range(2 * nt)},
        compiler_params=pltpu.CompilerParams(has_side_effects=_SIDE_EFFECT),
    )(*srcs, *lands)
    sems, thru, token = res[:2 * ng], res[2 * ng:2 * ng + 2 * nt], res[-1]
    states = []
    for gi in range(ng):
        lo, hi = bounds[gi], bounds[gi + 1]
        states.append((sems[2 * gi], sems[2 * gi + 1], thru[lo:hi], thru[nt + lo:nt + hi], scatters[lo:hi]))
    return states, token[0, 0]


def _exchange_wait(state, after, name):
    send_sems, recv_sems, srcs, lands, scatters = state
    n = len(srcs)

    def body(*refs):
        _, copies = _peer_copies(refs[:n], refs[n:2 * n], scatters, refs[2 * n], refs[2 * n + 1])
        for cp in copies:
            cp.wait_send()
        for cp in copies:
            cp.wait_recv()

    res = pl.pallas_call(
        body, name=name,
        out_shape=[pltpu.HBM(a.shape, a.dtype) for a in srcs] + [pltpu.HBM(a.shape, a.dtype) for a in lands],
        in_specs=[_HBM] * (2 * n) + [_SEM, _SEM, _HBM],
        out_specs=[_HBM] * (2 * n),
        input_output_aliases={t: t for t in range(2 * n)},
        compiler_params=pltpu.CompilerParams(has_side_effects=_SIDE_EFFECT),
    )(*srcs, *lands, send_sems, recv_sems, pltpu.with_memory_space_constraint(after, pltpu.HBM))
    me = 4 * lax.axis_index("x") + 2 * lax.axis_index("y") + lax.axis_index("c")
    out = []
    for src, land, sc in zip(res[:n], res[n:], scatters):
        own = lax.dynamic_index_in_dim(src, me, 0, keepdims=True) if sc else src[None]
        out.append(lax.dynamic_update_slice_in_dim(land, own, me, 0))
    return out


def _as_rows(shape):
    return (1, shape[0]) if len(shape) == 1 else (math.prod(shape[:-1]), shape[-1])


def _sum_adamw(pieces, w, m, v, name):
    shape = w.shape
    nl = len(pieces)
    if nl > 1 and _as_rows(shape[1:])[0] % 16:
        pieces, nl = [jnp.stack(pieces, axis=1)], 1
    rows, cols = _as_rows(shape)
    rl = rows // nl
    cap = max(16, (1 << 18) // cols // 16 * 16)
    tr = _tile(rl, cap, 16)
    nb = rl // tr
    c1 = 1.0 / (1.0 - ADAM_B1 ** ADAM_STEP)
    c2 = 1.0 / (1.0 - ADAM_B2 ** ADAM_STEP)

    def body(*refs):
        p_refs = refs[:nl]
        w_ref, m_ref, v_ref, g_ref, d_ref, nm_ref, nv_ref = refs[nl:]
        li = pl.program_id(0)

        def total(p_ref):
            acc = p_ref[0].astype(F32)
            for k in range(1, N_DEV):
                acc = acc + p_ref[k].astype(F32)
            return acc

        gg = total(p_refs[0])
        for l in range(1, nl):
            gg = jnp.where(li == l, total(p_refs[l]), gg)
        nm = ADAM_B1 * m_ref[...] + (1.0 - ADAM_B1) * gg
        nv = ADAM_B2 * v_ref[...] + (1.0 - ADAM_B2) * (gg * gg)
        g_ref[...] = gg
        d_ref[...] = -ADAM_LR * ((nm * c1) / (jnp.sqrt(nv * c2) + ADAM_EPS) + ADAM_WD * w_ref[...])
        nm_ref[...] = nm
        nv_ref[...] = nv

    blk = pl.BlockSpec((tr, cols), lambda li, i: (li * nb + i, 0))
    p_specs = [pl.BlockSpec((N_DEV, tr, cols), functools.partial(lambda li, i, l: (0, jnp.where(li == l, i, 0), 0), l=l))
               for l in range(nl)]
    res = pl.pallas_call(
        body, name=name, grid=(nl, nb),
        in_specs=p_specs + [blk] * 3, out_specs=[blk] * 4,
        out_shape=[jax.ShapeDtypeStruct((rows, cols), F32)] * 4,
        compiler_params=_cparams("parallel", "parallel"),
    )(*[p.reshape(N_DEV, rl, cols) for p in pieces], *[a.reshape(rows, cols) for a in (w, m, v)])
    return [r.reshape(shape) for r in res]


def _to_shards(full, axis):
    shp = full.shape
    a = full.reshape(shp[:axis] + (N_DEV, shp[axis] // N_DEV) + shp[axis + 1:])
    return jnp.moveaxis(a, axis, 0)


def _from_shards(blocks, axis):
    a = jnp.moveaxis(blocks, 0, axis)
    shp = a.shape
    return a.reshape(shp[:axis] + (shp[axis] * shp[axis + 1],) + shp[axis + 2:])


def kernel(x, meta_tokens, ln0_g, ln0_b, w_in, q_norm, kv_norm, w_uq, w_uk, w_uv, w_o_mla, lru_conv_w, lru_conv_b, w_rg, b_rg, w_ig, b_ig, lru_lambda, w_o_lru, w_out, ln1_g, ln1_b, w_up, ffn_conv_w, ffn_conv_b, w_down, ln2_g, ln2_b, loss_target, m_meta_tokens, m_ln0_g, m_ln0_b, m_w_in, m_q_norm, m_kv_norm, m_w_uq, m_w_uk, m_w_uv, m_w_o_mla, m_lru_conv_w, m_lru_conv_b, m_w_rg, m_b_rg, m_w_ig, m_b_ig, m_lru_lambda, m_w_o_lru, m_w_out, m_ln1_g, m_ln1_b, m_w_up, m_ffn_conv_w, m_ffn_conv_b, m_w_down, m_ln2_g, m_ln2_b, v_meta_tokens, v_ln0_g, v_ln0_b, v_w_in, v_q_norm, v_kv_norm, v_w_uq, v_w_uk, v_w_uv, v_w_o_mla, v_lru_conv_w, v_lru_conv_b, v_w_rg, v_b_rg, v_w_ig, v_b_ig, v_lru_lambda, v_w_o_lru, v_w_out, v_ln1_g, v_ln1_b, v_w_up, v_ffn_conv_w, v_ffn_conv_b, v_w_down, v_ln2_g, v_ln2_b):
    args = (meta_tokens, ln0_g, ln0_b, w_in, q_norm, kv_norm, w_uq, w_uk, w_uv, w_o_mla, lru_conv_w, lru_conv_b, w_rg, b_rg, w_ig, b_ig, lru_lambda, w_o_lru, w_out, ln1_g, ln1_b, w_up, ffn_conv_w, ffn_conv_b, w_down, ln2_g, ln2_b)
    ms = (m_meta_tokens, m_ln0_g, m_ln0_b, m_w_in, m_q_norm, m_kv_norm, m_w_uq, m_w_uk, m_w_uv, m_w_o_mla, m_lru_conv_w, m_lru_conv_b, m_w_rg, m_b_rg, m_w_ig, m_b_ig, m_lru_lambda, m_w_o_lru, m_w_out, m_ln1_g, m_ln1_b, m_w_up, m_ffn_conv_w, m_ffn_conv_b, m_w_down, m_ln2_g, m_ln2_b)
    vs = (v_meta_tokens, v_ln0_g, v_ln0_b, v_w_in, v_q_norm, v_kv_norm, v_w_uq, v_w_uk, v_w_uv, v_w_o_mla, v_lru_conv_w, v_lru_conv_b, v_w_rg, v_b_rg, v_w_ig, v_b_ig, v_lru_lambda, v_w_o_lru, v_w_out, v_ln1_g, v_ln1_b, v_w_up, v_ffn_conv_w, v_ffn_conv_b, v_w_down, v_ln2_g, v_ln2_b)
    wd, md, vd = dict(zip(WEIGHTS, args)), dict(zip(WEIGHTS, ms)), dict(zip(WEIGHTS, vs))

    def shard_axis(n, l):
        return SHARD_AXIS[n] - (0 if l is None else 1)

    def shard(n, l):
        a = wd[n] if l is None else wd[n][l]
        if n in SENT_TRANSPOSED:
            a = a.T
        return a.astype(BF16) if n in BIG else a

    def whole(keys, landed):
        return {k: b.reshape(-1, b.shape[-1]) if k[0] in SENT_TRANSPOSED else _from_shards(b, shard_axis(*k))
                for k, b in zip(keys, landed)}

    first = [('meta_tokens', None), ('w_in', 0)]
    landed, token = _gather_two_level([shard(*k) for k in first], "gather_first")
    got_first = whole(first, landed)
    staged = [[(n, 0) for n in names] for names in STAGE_WEIGHTS]
    later = [(n, 1) for n in SHARDED if n != 'meta_tokens']
    groups = [[(shard(*k), False) for k in keys] for keys in staged + [later]]
    groups[0][0] = (_after(groups[0][0][0], token), False)
    gather, token = _exchange_start(groups, "gather_start")

    def arrive(gi, keys, after, name):
        return whole(keys, _exchange_wait(gather[gi], after, name))

    def layer_weights(got, l, names):
        fl = {n: wd[n][l] for n in names if n in REPLICATED}
        fl.update({n: a for (n, _), a in got.items() if n in names})
        return _layer_weights(fl)

    ln0_g = _after(wd['ln0_g'], token)

    def first_layer(h):
        def more(stage, after):
            got = arrive(stage, staged[stage], after, "gather_wait_l0_%d" % stage)
            return layer_weights(got, 0, STAGE_WEIGHTS[stage] + STAGE_REPLICATED[stage])
        return _w_in_kernel(got_first['w_in', 0]), more

    def second_layer(h):
        got = arrive(len(staged), later, h, "gather_wait_l1")
        return _w_in_kernel(got['w_in', 1]), lambda stage, after: layer_weights(got, 1, STAGE_WEIGHTS[stage] + STAGE_REPLICATED[stage])

    sent = []
    pending = []

    def send(l, stage, grads):
        for n, g in grads.items():
            if n in SHARD_AXIS:
                g = _to_shards(g, shard_axis(n, l))
                pending.append(((n, l), (g.astype(BF16) if n in BIG else g, True)))
            else:
                pending.append(((n, l), (g.astype(BF16) if n in LARGE_REPLICATED else g, False)))
        if l == DEPTH - 1 and stage != 'in':
            return None
        (state,), tok = _exchange_start([[it for _, it in pending]], "grads_start_%s_%s" % (l, stage))
        sent.append(([k for k, _ in pending], state))
        pending.clear()
        return tok

    seq = x.shape[1]
    t_pad = -(-(N_META + seq + MIN_PAD_ROWS) // LANE) * LANE
    grad_x = _local_step(x[0], loss_target[0], got_first['meta_tokens', None], ln0_g, wd['ln0_b'],
                         [first_layer, second_layer], t_pad, send)

    pieces, outs, after = {}, {}, grad_x
    for gi, (keys, state) in enumerate(sent):
        pieces.update(zip(keys, _exchange_wait(state, after, "grads_wait_%d" % gi)))
        if gi == len(sent) - 2:
            for n in WEIGHTS:
                if (n, 0) in pieces:
                    outs[n] = _sum_adamw([pieces[n, l] for l in range(DEPTH)], wd[n], md[n], vd[n], "adamw_" + n)
                    after = outs[n][1]
    loss = jnp.sum(pieces['loss', None])
    for n in WEIGHTS:
        if (n, None) in pieces:
            outs[n] = _sum_adamw([pieces[n, None]], wd[n], md[n], vd[n], "adamw_" + n)
    res = [loss, grad_x[None]]
    for k in range(4):
        res += [outs[n][k] for n in WEIGHTS]
    return tuple(res)
```

```python
import functools
import math

import jax
import jax.numpy as jnp
from jax import lax
from jax.experimental import pallas as pl
from jax.experimental.pallas import tpu as pltpu

F32 = jnp.float32
BF16 = jnp.bfloat16

N_DEV = 8
D_MODEL = 1024
N_META = 16
HEADS = 8
QK_NOPE = 128
QK_ROPE = 64
V_HEAD = 128
Q_RANK = 256
KV_RANK = 128
ROPE_THETA = 10000.0
LRU_BLOCKS = 8
LRU_C = 8.0
D_FF = 2816
DEPTH = 2
DN_ALPHA = (2.0 * DEPTH) ** 0.25
LN_EPS = 1e-5
RMS_EPS = 1e-6
LN2 = math.log(2.0)
ATT_SCALE = 1.0 / math.sqrt(QK_NOPE + QK_ROPE) / LN2
NEG_BIG = -1e30

ADAM_LR = 0.001
ADAM_B1 = 0.9
ADAM_B2 = 0.999
ADAM_EPS = 1e-08
ADAM_WD = 0.01
ADAM_STEP = 10

MIN_PAD_ROWS = 2
LANE = 128
SUBLANE = 8
VMEM_LIMIT = 56 * 1024 * 1024

PROJ_COLS = 4 * D_MODEL + Q_RANK + KV_RANK + 2 * QK_ROPE
C_LRU_G, C_LRU_X, C_G_MLA, C_G_LRU = 0, D_MODEL, 2 * D_MODEL, 3 * D_MODEL
C_CQ = 4 * D_MODEL
C_CKV = C_CQ + Q_RANK
C_KRP = C_CKV + KV_RANK

WEIGHTS = ['meta_tokens', 'ln0_g', 'ln0_b', 'w_in', 'q_norm', 'kv_norm', 'w_uq', 'w_uk', 'w_uv', 'w_o_mla',
           'lru_conv_w', 'lru_conv_b', 'w_rg', 'b_rg', 'w_ig', 'b_ig', 'lru_lambda', 'w_o_lru', 'w_out',
           'ln1_g', 'ln1_b', 'w_up', 'ffn_conv_w', 'ffn_conv_b', 'w_down', 'ln2_g', 'ln2_b']
SHARD_AXIS = {'meta_tokens': 1, 'w_in': 2, 'w_uq': 1, 'w_o_mla': 1, 'lru_conv_w': 2, 'b_rg': 2, 'b_ig': 2,
              'lru_lambda': 2, 'w_o_lru': 1, 'w_out': 1, 'w_up': 2, 'ffn_conv_w': 2, 'w_down': 1}
BIG = ['w_in', 'w_uq', 'w_o_mla', 'w_o_lru', 'w_out', 'w_up', 'w_down']
SHARDED = [n for n in WEIGHTS if n in SHARD_AXIS]
REPLICATED = [n for n in WEIGHTS if n not in SHARD_AXIS]
LARGE_REPLICATED = ['w_uk', 'w_uv', 'w_rg', 'w_ig']
SENT_TRANSPOSED = ['w_in', 'w_up']
STAGE_WEIGHTS = [['w_uq', 'lru_conv_w', 'b_rg', 'b_ig', 'lru_lambda'], ['w_o_mla', 'w_o_lru', 'w_out'], ['w_up', 'ffn_conv_w', 'w_down']]
STAGE_REPLICATED = [['q_norm', 'kv_norm', 'w_uk', 'w_uv', 'lru_conv_b', 'w_rg', 'w_ig'], ['ln1_g', 'ln1_b'], ['ffn_conv_b', 'ln2_g', 'ln2_b']]


def _cparams(*sem):
    return pltpu.CompilerParams(dimension_semantics=sem, vmem_limit_bytes=VMEM_LIMIT)


def _tile(n, cap, unit=LANE):
    best = None
    t = unit
    while t <= min(n, cap):
        if n % t == 0:
            best = t
        t += unit
    return n if best is None else best


def _sigmoid(x):
    return 1.0 / (1.0 + jnp.exp(-x))


_GELU_C = math.sqrt(2.0 / math.pi)


_GELU_A = 0.044715


def _gelu(x):
    t = jnp.tanh(x * (_GELU_C + (_GELU_C * _GELU_A) * (x * x)))
    hx = 0.5 * x
    return hx + hx * t


def _gelu_and_grad(x):
    x2 = x * x
    t = jnp.tanh(x * (_GELU_C + (_GELU_C * _GELU_A) * x2))
    hx = 0.5 * x
    dg = 0.5 + 0.5 * t + (hx * (1.0 - t * t)) * (_GELU_C + (3.0 * _GELU_C * _GELU_A) * x2)
    return hx + hx * t, dg


def _softplus_neg(lam):
    z = jnp.exp(-jnp.abs(lam))
    w = 1.0 + z
    log1p = jnp.where(w == 1.0, z, jnp.log(w) * z / (w - 1.0))
    return jnp.maximum(-lam, 0.0) + log1p


def _row_ids(shape, row0=0):
    return lax.broadcasted_iota(jnp.int32, shape, 0) + row0


def _matmul(a, b, name, ta=False, tb=False, out_dtype=F32, tm_cap=1408, tn_cap=1024, tk_cap=2048):
    if ta:
        kdim, m = a.shape
    else:
        m, kdim = a.shape
    if tb:
        n, k2 = b.shape
    else:
        k2, n = b.shape
    assert kdim == k2, (a.shape, b.shape, ta, tb)
    tm, tn, tk = _tile(m, tm_cap), _tile(n, tn_cap), _tile(kdim, tk_cap)
    nk = kdim // tk

    def body(a_ref, b_ref, o_ref, *acc):
        dn = (((0 if ta else 1,), (1 if tb else 0,)), ((), ()))
        part = lax.dot_general(a_ref[...].astype(BF16), b_ref[...].astype(BF16), dn, preferred_element_type=F32)
        if nk == 1:
            o_ref[...] = part.astype(o_ref.dtype)
            return
        acc_ref, k = acc[0], pl.program_id(2)

        @pl.when(k == 0)
        def _():
            acc_ref[...] = part

        @pl.when(k > 0)
        def _():
            acc_ref[...] += part

        @pl.when(k == nk - 1)
        def _():
            o_ref[...] = acc_ref[...].astype(o_ref.dtype)

    a_spec = pl.BlockSpec((tk, tm), lambda i, j, k: (k, i)) if ta else pl.BlockSpec((tm, tk), lambda i, j, k: (i, k))
    b_spec = pl.BlockSpec((tn, tk), lambda i, j, k: (j, k)) if tb else pl.BlockSpec((tk, tn), lambda i, j, k: (k, j))
    return pl.pallas_call(
        body, name=name,
        grid=(m // tm, n // tn, nk),
        in_specs=[a_spec, b_spec],
        out_specs=pl.BlockSpec((tm, tn), lambda i, j, k: (i, j)),
        out_shape=jax.ShapeDtypeStruct((m, n), out_dtype),
        scratch_shapes=[pltpu.VMEM((tm, tn), F32)] if nk > 1 else [],
        compiler_params=_cparams("parallel", "parallel", "arbitrary"),
    )(a, b)


class Rw:
    def __init__(self, arr, width=None, cb=0):
        self.arr, self.width, self.cb = arr, (arr.shape[1] if width is None else width), cb


class Pm:
    def __init__(self, arr):
        self.arr = arr


class Into:
    def __init__(self, arr, col0, width):
        self.arr, self.col0, self.width = arr, col0, width


def _call_with_into(body, name, grid, in_specs, operands, outs, spec_of, shape_of, extra_out_specs, extra_out_shape, sem):
    intos = [(k, o) for k, o in enumerate(outs) if isinstance(o, Into)]
    aliases = {len(operands) + n: k for n, (k, _) in enumerate(intos)}
    return pl.pallas_call(
        body, name=name, grid=grid,
        in_specs=in_specs + [pl.BlockSpec(memory_space=pl.ANY)] * len(intos),
        out_specs=[spec_of(o) for o in outs] + extra_out_specs,
        out_shape=[jax.ShapeDtypeStruct(o.arr.shape, o.arr.dtype) if isinstance(o, Into) else shape_of(o) for o in outs]
        + extra_out_shape,
        input_output_aliases=aliases,
        compiler_params=_cparams(sem),
    )(*operands, *[o.arr for _, o in intos])


def _rows(fn, name, ins, outs, accs=(), tm_cap=384):
    tp = next(o.arr.shape[0] for o in ins if isinstance(o, Rw))
    tm = _tile(tp, tm_cap)
    n_in, n_out, n_acc = len(ins), len(outs), len(accs)
    n_into = sum(isinstance(o, Into) for o in outs)

    def body(*refs):
        i = pl.program_id(0)
        res = fn(i * tm, *[r[...] for r in refs[:n_in]])
        if not isinstance(res, (tuple, list)):
            res = (res,)
        assert len(res) == n_out + n_acc, (name, len(res))
        out_refs = refs[n_in + n_into:]
        for k in range(n_out):
            out_refs[k][...] = res[k].astype(out_refs[k].dtype)
        for k in range(n_acc):
            ref = out_refs[n_out + k]

            @pl.when(i == 0)
            def _():
                ref[...] = jnp.zeros_like(ref)

            ref[...] += res[n_out + k]

    in_specs = []
    for o in ins:
        if isinstance(o, Rw):
            in_specs.append(pl.BlockSpec((tm, o.width), functools.partial(lambda i, cb: (i, cb), cb=o.cb)))
        else:
            in_specs.append(pl.BlockSpec(o.arr.shape, functools.partial(lambda i, nd: (0,) * nd, nd=o.arr.ndim)))

    def spec_of(o):
        if isinstance(o, Into):
            assert o.col0 % o.width == 0, (name, o.col0, o.width)
            return pl.BlockSpec((tm, o.width), functools.partial(lambda i, cb: (i, cb), cb=o.col0 // o.width))
        return pl.BlockSpec((tm, o[0]), lambda i: (i, 0))

    return _call_with_into(
        body, name, (tp // tm,), in_specs, [o.arr for o in ins], list(outs), spec_of,
        lambda o: jax.ShapeDtypeStruct((tp, o[0]), o[1]),
        [pl.BlockSpec(s, functools.partial(lambda i, nd: (0,) * nd, nd=len(s))) for s in accs],
        [jax.ShapeDtypeStruct(s, F32) for s in accs], "arbitrary")


class Cl:
    def __init__(self, arr, col0=0):
        self.arr, self.col0 = arr, col0


def _cols(fn, name, ins, outs, ncols, tc):
    assert ncols % tc == 0
    n_in, n_out = len(ins), len(outs)
    n_into = sum(isinstance(o, Into) for o in outs)

    def body(*refs):
        res = fn(*[r[...] for r in refs[:n_in]])
        if not isinstance(res, (tuple, list)):
            res = (res,)
        assert len(res) == n_out, (name, len(res))
        out_refs = refs[n_in + n_into:]
        for k in range(n_out):
            out_refs[k][...] = res[k].astype(out_refs[k].dtype)

    in_specs = []
    for o in ins:
        assert o.col0 % tc == 0, (name, o.col0, tc)
        in_specs.append(pl.BlockSpec((o.arr.shape[0], tc), functools.partial(lambda j, off: (0, j + off), off=o.col0 // tc)))

    def spec_of(o):
        if isinstance(o, Into):
            assert o.col0 % tc == 0 and o.width == ncols, (name, o.col0, o.width)
            return pl.BlockSpec((o.arr.shape[0], tc), functools.partial(lambda j, off: (0, j + off), off=o.col0 // tc))
        return pl.BlockSpec((o[0], tc), lambda j: (0, j))

    return _call_with_into(body, name, (ncols // tc,), in_specs, [o.arr for o in ins], list(outs), spec_of,
                           lambda o: jax.ShapeDtypeStruct((o[0], ncols), o[1]), [], [], "parallel")


def _ln_stats(u):
    mu = jnp.mean(u, axis=-1, keepdims=True)
    xc = u - mu
    var = jnp.mean(xc * xc, axis=-1, keepdims=True)
    rstd = lax.rsqrt(var + LN_EPS)
    return xc * rstd, rstd


def _ln_fwd(terms, g, b, name):
    coefs = [c for c, _ in terms]

    def fn(row0, *blk):
        xs, (gg, bb) = blk[:len(coefs)], blk[len(coefs):]
        u = sum(c * x for c, x in zip(coefs, xs))
        xhat, _ = _ln_stats(u)
        return xhat * gg + bb

    d = terms[0][1].shape[1]
    return _rows(fn, name, [Rw(x) for _, x in terms] + [Pm(g.reshape(1, d)), Pm(b.reshape(1, d))], [(d, F32)])[0]


def _ln_bwd(dy_terms, u_terms, g, name):
    dc = [c for c, _ in dy_terms]
    uc = [c for c, _ in u_terms]
    d = u_terms[0][1].shape[1]

    def fn(row0, *blk):
        dys = blk[:len(dc)]
        xs = blk[len(dc):len(dc) + len(uc)]
        gg = blk[-1]
        dy = sum(c * x for c, x in zip(dc, dys))
        u = sum(c * x for c, x in zip(uc, xs))
        xhat, rstd = _ln_stats(u)
        gdy = dy * gg
        m1 = jnp.mean(gdy, axis=-1, keepdims=True)
        m2 = jnp.mean(gdy * xhat, axis=-1, keepdims=True)
        du = rstd * (gdy - m1 - xhat * m2)
        return du, jnp.sum(dy * xhat, axis=0, keepdims=True), jnp.sum(dy, axis=0, keepdims=True)

    ins = [Rw(x) for _, x in dy_terms] + [Rw(x) for _, x in u_terms] + [Pm(g.reshape(1, d))]
    return _rows(fn, name, ins, [(d, F32)], accs=[(1, d), (1, d)])


def _loss_head(y, tgt, t_real, name):
    d = y.shape[1]

    def fn(row0, yb, tb):
        rows = _row_ids(yb.shape, row0)
        live = (rows >= N_META) & (rows < t_real)
        diff = jnp.where(live, yb - tb, 0.0)
        return diff * (1.0 / d), jnp.sum(diff * diff, axis=0, keepdims=True) * (0.5 / d)

    return _rows(fn, name, [Rw(y), Rw(tgt)], [(d, F32)], accs=[(1, d)])


def _rms(x, g):
    r = lax.rsqrt(jnp.mean(x * x, axis=-1, keepdims=True) + RMS_EPS)
    return x * r * g


def _rms_bwd(dy, x, g):
    r = lax.rsqrt(jnp.mean(x * x, axis=-1, keepdims=True) + RMS_EPS)
    gdy = dy * g
    dx = r * gdy - x * (r * r * r) * jnp.mean(gdy * x, axis=-1, keepdims=True)
    return dx, jnp.sum(dy * x * r, axis=0, keepdims=True)


def _mla_norms(proj, qn, kvn, name):
    def fn(row0, cq, ckv, g1, g2):
        return _rms(cq, g1), _rms(ckv, g2)

    return _rows(fn, name, [Rw(proj, Q_RANK, C_CQ // Q_RANK), Rw(proj, KV_RANK, C_CKV // KV_RANK),
                            Pm(qn.reshape(1, Q_RANK)), Pm(kvn.reshape(1, KV_RANK))],
                 [(Q_RANK, BF16), (KV_RANK, BF16)])


def _mla_norms_bwd(dcqn, dckvn, dkrp, proj, qn, kvn, dproj, name):
    def fn(row0, d1, d2, dkr, cq, ckv, g1, g2):
        dx1, dg1 = _rms_bwd(d1, cq, g1)
        dx2, dg2 = _rms_bwd(d2, ckv, g2)
        return jnp.concatenate([dx1, dx2, dkr], axis=1), dg1, dg2

    return _rows(fn, name, [Rw(dcqn), Rw(dckvn), Rw(dkrp), Rw(proj, Q_RANK, C_CQ // Q_RANK), Rw(proj, KV_RANK, C_CKV // KV_RANK),
                            Pm(qn.reshape(1, Q_RANK)), Pm(kvn.reshape(1, KV_RANK))],
                 [Into(dproj, C_CQ, PROJ_COLS - C_CQ)], accs=[(1, Q_RANK), (1, KV_RANK)])


def _fold_rope(z):
    return z + pltpu.roll(z, QK_ROPE, 1)


def _mla_pack(qext, kv, proj, cs, name):
    tp = qext.shape[0]
    tm = _tile(tp, 384)
    hw, nope_all = 2 * LANE, HEADS * QK_NOPE

    def body(q_ref, kv_ref, kr_ref, cs_ref, qo_ref, ko_ref, vo_ref):
        cs_ = cs_ref[...]
        low = lax.broadcasted_iota(jnp.int32, cs_.shape, 1) < QK_ROPE
        kr = _fold_rope(kr_ref[...] * cs_).astype(BF16)
        for h in range(HEADS):
            qr = jnp.where(low, _fold_rope(q_ref[:, h * hw + QK_NOPE:(h + 1) * hw] * cs_), 0.0)
            qo_ref[:, h * hw:h * hw + QK_NOPE] = (q_ref[:, h * hw:h * hw + QK_NOPE] * ATT_SCALE).astype(BF16)
            qo_ref[:, h * hw + QK_NOPE:(h + 1) * hw] = (qr * ATT_SCALE).astype(BF16)
            ko_ref[:, h * hw:h * hw + QK_NOPE] = kv_ref[:, h * QK_NOPE:(h + 1) * QK_NOPE].astype(BF16)
            ko_ref[:, h * hw + QK_NOPE:(h + 1) * hw] = kr
        vo_ref[...] = kv_ref[:, nope_all:].astype(BF16)

    row = lambda w: pl.BlockSpec((tm, w), lambda i: (i, 0))
    return pl.pallas_call(
        body, name=name, grid=(tp // tm,),
        in_specs=[row(HEADS * hw), row(2 * nope_all), pl.BlockSpec((tm, LANE), lambda i: (i, C_KRP // LANE)), row(LANE)],
        out_specs=[row(HEADS * hw), row(HEADS * hw), row(nope_all)],
        out_shape=[jax.ShapeDtypeStruct((tp, HEADS * hw), BF16),
                   jax.ShapeDtypeStruct((tp, HEADS * hw), BF16),
                   jax.ShapeDtypeStruct((tp, nope_all), BF16)],
        compiler_params=_cparams("parallel"),
    )(qext, kv, proj, cs)


def _mla_unpack(dk, dv, cs, name):
    tp = dk.shape[0]
    tm = _tile(tp, 384)
    hw, nope_all = 2 * LANE, HEADS * QK_NOPE

    def body(dk_ref, dv_ref, cs_ref, dkv_ref, dkr_ref):
        cs_ = cs_ref[...]
        low = lax.broadcasted_iota(jnp.int32, cs_.shape, 1) < QK_ROPE
        dkr = None
        for h in range(HEADS):
            dkv_ref[:, h * QK_NOPE:(h + 1) * QK_NOPE] = dk_ref[:, h * hw:h * hw + QK_NOPE].astype(BF16)
            part = jnp.where(low, dk_ref[:, h * hw + QK_NOPE:(h + 1) * hw], 0.0)
            dkr = part if h == 0 else dkr + part
        dkv_ref[:, nope_all:] = dv_ref[...].astype(BF16)
        dkr_ref[...] = _fold_rope(dkr) * cs_

    row = lambda w: pl.BlockSpec((tm, w), lambda i: (i, 0))
    return pl.pallas_call(
        body, name=name, grid=(tp // tm,),
        in_specs=[row(HEADS * hw), row(nope_all), row(LANE)],
        out_specs=[row(2 * nope_all), row(LANE)],
        out_shape=[jax.ShapeDtypeStruct((tp, 2 * nope_all), BF16), jax.ShapeDtypeStruct((tp, LANE), F32)],
        compiler_params=_cparams("parallel"),
    )(dk, dv, cs)


def _attn_fwd(q, k, v, t_real, name):
    tp = q.shape[0]
    tq = _tile(tp, 1408)
    tkc = _tile(tp, 704, 64)
    nkc = -(-t_real // tkc)

    def body(q_ref, k_ref, v_ref, o_ref, lse_ref):
        qb = q_ref[...]
        m = l = acc = None
        for c in range(nkc):
            s = lax.dot_general(qb, k_ref[c * tkc:(c + 1) * tkc, :], (((1,), (1,)), ((), ())), preferred_element_type=F32)
            if (c + 1) * tkc > t_real:
                cols = lax.broadcasted_iota(jnp.int32, s.shape, 1) + c * tkc
                s = jnp.where(cols < t_real, s, NEG_BIG)
            mc = jnp.max(s, axis=-1, keepdims=True)
            m_new = mc if c == 0 else jnp.maximum(m, mc)
            p = jnp.exp2(s - m_new)
            lc = jnp.sum(p, axis=-1, keepdims=True)
            pv = jnp.dot(p.astype(BF16), v_ref[c * tkc:(c + 1) * tkc, :], preferred_element_type=F32)
            if c == 0:
                l, acc = lc, pv
            else:
                alpha = jnp.exp2(m - m_new)
                l, acc = alpha * l + lc, alpha * acc + pv
            m = m_new
        o_ref[...] = acc / l
        lse_ref[...] = m + jnp.log2(l)

    return pl.pallas_call(
        body, name=name, grid=(HEADS, tp // tq),
        in_specs=[pl.BlockSpec((tq, 2 * LANE), lambda h, i: (i, h)),
                  pl.BlockSpec((tp, 2 * LANE), lambda h, i: (0, h)),
                  pl.BlockSpec((tp, LANE), lambda h, i: (0, h))],
        out_specs=[pl.BlockSpec((tq, LANE), lambda h, i: (i, h)),
                   pl.BlockSpec((None, tq, 1), lambda h, i: (h, i, 0))],
        out_shape=[jax.ShapeDtypeStruct((tp, HEADS * LANE), F32),
                   jax.ShapeDtypeStruct((HEADS, tp, 1), F32)],
        compiler_params=_cparams("parallel", "parallel"),
    )(q, k, v)


def _attn_bwd(q, k, v, do, o, lse, cs, t_real, name):
    tp = q.shape[0]
    tq = _tile(tp, 1408)
    tkc = _tile(tp, 704, 64)
    nkc = -(-t_real // tkc)

    def body(q_ref, k_ref, v_ref, do_ref, o_ref, lse_ref, cs_ref, dq_ref, dk_ref, dv_ref):
        i = pl.program_id(1)

        @pl.when(i == 0)
        def _():
            dk_ref[...] = jnp.zeros_like(dk_ref)
            dv_ref[...] = jnp.zeros_like(dv_ref)

        qb = q_ref[...]
        dob = do_ref[...]
        dob16 = dob.astype(BF16)
        dol2 = (dob * LN2).astype(BF16)
        delta = jnp.sum(dob * o_ref[...], axis=-1, keepdims=True) * LN2
        lse = lse_ref[...]
        dq = None
        for c in range(nkc):
            ks = slice(c * tkc, (c + 1) * tkc)
            kb = k_ref[ks, :]
            s = lax.dot_general(qb, kb, (((1,), (1,)), ((), ())), preferred_element_type=F32)
            p = jnp.exp2(s - lse)
            if (c + 1) * tkc > t_real:
                cols = lax.broadcasted_iota(jnp.int32, s.shape, 1) + c * tkc
                p = jnp.where(cols < t_real, p, 0.0)
            dp = lax.dot_general(dol2, v_ref[ks, :], (((1,), (1,)), ((), ())), preferred_element_type=F32)
            ds = (p * (dp - delta)).astype(BF16)
            dqc = jnp.dot(ds, kb, preferred_element_type=F32)
            dq = dqc if c == 0 else dq + dqc
            dk_ref[ks, :] += lax.dot_general(ds, qb, (((0,), (0,)), ((), ())), preferred_element_type=F32)
            dv_ref[ks, :] += lax.dot_general(p.astype(BF16), dob16, (((0,), (0,)), ((), ())), preferred_element_type=F32)
        cs_ = cs_ref[...]
        low = lax.broadcasted_iota(jnp.int32, cs_.shape, 1) < QK_ROPE
        dq = dq * ATT_SCALE
        dq_ref[:, :QK_NOPE] = dq[:, :QK_NOPE].astype(BF16)
        dq_ref[:, QK_NOPE:] = (_fold_rope(jnp.where(low, dq[:, QK_NOPE:], 0.0)) * cs_).astype(BF16)

    return pl.pallas_call(
        body, name=name, grid=(HEADS, tp // tq),
        in_specs=[pl.BlockSpec((tq, 2 * LANE), lambda h, i: (i, h)),
                  pl.BlockSpec((tp, 2 * LANE), lambda h, i: (0, h)),
                  pl.BlockSpec((tp, LANE), lambda h, i: (0, h)),
                  pl.BlockSpec((tq, LANE), lambda h, i: (i, h)),
                  pl.BlockSpec((tq, LANE), lambda h, i: (i, h)),
                  pl.BlockSpec((None, tq, 1), lambda h, i: (h, i, 0)),
                  pl.BlockSpec((tq, LANE), lambda h, i: (i, 0))],
        out_specs=[pl.BlockSpec((tq, 2 * LANE), lambda h, i: (i, h)),
                   pl.BlockSpec((tp, 2 * LANE), lambda h, i: (0, h)),
                   pl.BlockSpec((tp, LANE), lambda h, i: (0, h))],
        out_shape=[jax.ShapeDtypeStruct((tp, HEADS * 2 * LANE), BF16),
                   jax.ShapeDtypeStruct((tp, HEADS * 2 * LANE), F32),
                   jax.ShapeDtypeStruct((tp, HEADS * LANE), F32)],
        compiler_params=_cparams("parallel", "arbitrary"),
    )(q, k, v, do, o, lse, cs)


def _shift_rows(x, s):
    tp = x.shape[0]
    return x if s % tp == 0 else pltpu.roll(x, s % tp, 0)


def _conv_fwd_val(xm, w, b, pad_left):
    acc = b + w[0:1, :] * _shift_rows(xm, pad_left)
    for k in range(1, w.shape[0]):
        acc = acc + w[k:k + 1, :] * _shift_rows(xm, pad_left - k)
    return acc


def _conv_bwd_val(dy, xm, w, pad_left, live):
    kk = w.shape[0]
    dx = w[0:1, :] * _shift_rows(dy, -pad_left)
    dws = [jnp.sum(dy * _shift_rows(xm, pad_left), axis=0, keepdims=True)]
    for k in range(1, kk):
        dx = dx + w[k:k + 1, :] * _shift_rows(dy, k - pad_left)
        dws.append(jnp.sum(dy * _shift_rows(xm, pad_left - k), axis=0, keepdims=True))
    return jnp.where(live, dx, 0.0), jnp.concatenate(dws, axis=0), jnp.sum(dy, axis=0, keepdims=True)


def _lru_conv_fwd(proj, w, b, t_real, name):
    def fn(x, ww, bb):
        xm = jnp.where(_row_ids(x.shape) < t_real, x, 0.0)
        return _conv_fwd_val(xm, ww, bb, 2)

    return _cols(fn, name, [Cl(proj, C_LRU_X), Cl(w), Cl(b.reshape(1, -1))], [(proj.shape[0], F32)], D_MODEL, 128)[0]


def _lru_conv_bwd(dxc, proj, w, dproj, t_real, name):
    def fn(dy, x, ww):
        live = _row_ids(x.shape) < t_real
        xm = jnp.where(live, x, 0.0)
        dym = jnp.where(live, dy, 0.0)
        return _conv_bwd_val(dym, xm, ww, 2, live)

    return _cols(fn, name, [Cl(dxc), Cl(proj, C_LRU_X), Cl(w)],
                 [Into(dproj, C_LRU_X, D_MODEL), (w.shape[0], F32), (1, F32)], D_MODEL, 128)


def _ffn_conv_act(up, w, b, t_real, name):
    def fn(g, v, wg, wv, bg, bv):
        live = _row_ids(g.shape) < t_real
        gc = _conv_fwd_val(jnp.where(live, g, 0.0), wg, bg, 1)
        vc = _conv_fwd_val(jnp.where(live, v, 0.0), wv, bv, 1)
        return _gelu(gc) * vc

    b2 = b.reshape(1, -1)
    return _cols(fn, name, [Cl(up), Cl(up, D_FF), Cl(w), Cl(w, D_FF), Cl(b2), Cl(b2, D_FF)],
                 [(up.shape[0], BF16)], D_FF, 128)[0]


def _ffn_conv_act_bwd(dm, up, w, b, t_real, name):
    tp, kk = up.shape[0], w.shape[0]
    nb = D_FF // LANE
    assert nb >= 2

    def body(dm_ref, g_ref, v_ref, wg_ref, wv_ref, bg_ref, bv_ref, dup_ref, dwg_ref, dwv_ref, dbg_ref, dbv_ref, stage, sems):
        j = pl.program_id(0)
        slot = j % 2

        def copies(step, sl):
            return [pltpu.make_async_copy(stage.at[sl, half],
                                          dup_ref.at[:, pl.ds(pl.multiple_of(half * D_FF + step * LANE, LANE), LANE)],
                                          sems.at[sl, half]) for half in range(2)]

        @pl.when(j >= 2)
        def _():
            for cp in copies(j - 2, slot):
                cp.wait()

        live = _row_ids((tp, LANE)) < t_real
        gm, vm = jnp.where(live, g_ref[...], 0.0), jnp.where(live, v_ref[...], 0.0)
        gc = _conv_fwd_val(gm, wg_ref[...], bg_ref[...], 1)
        vc = _conv_fwd_val(vm, wv_ref[...], bv_ref[...], 1)
        act, dact = _gelu_and_grad(gc)
        dmm = jnp.where(live, dm_ref[...], 0.0)
        dgx, dwg_ref[...], dbg_ref[...] = _conv_bwd_val(dmm * vc * dact, gm, wg_ref[...], 1, live)
        dvx, dwv_ref[...], dbv_ref[...] = _conv_bwd_val(dmm * act, vm, wv_ref[...], 1, live)
        stage[slot, 0] = dgx.astype(BF16)
        stage[slot, 1] = dvx.astype(BF16)
        for cp in copies(j, slot):
            cp.start()

        @pl.when(j == nb - 1)
        def _():
            for cp in copies(j - 1, 1 - slot) + copies(j, slot):
                cp.wait()

    b2 = b.reshape(1, -1)
    col = lambda rows, off: pl.BlockSpec((rows, LANE), functools.partial(lambda j, o: (0, j + o), o=off))
    return pl.pallas_call(
        body, name=name, grid=(nb,),
        in_specs=[col(tp, 0), col(tp, 0), col(tp, nb), col(kk, 0), col(kk, nb), col(1, 0), col(1, nb)],
        out_specs=[pl.BlockSpec(memory_space=pl.ANY), col(kk, 0), col(kk, 0), col(1, 0), col(1, 0)],
        out_shape=[jax.ShapeDtypeStruct((tp, 2 * D_FF), BF16), jax.ShapeDtypeStruct((kk, D_FF), F32),
                   jax.ShapeDtypeStruct((kk, D_FF), F32), jax.ShapeDtypeStruct((1, D_FF), F32),
                   jax.ShapeDtypeStruct((1, D_FF), F32)],
        scratch_shapes=[pltpu.VMEM((2, 2, tp, LANE), BF16), pltpu.SemaphoreType.DMA((2, 2))],
        compiler_params=_cparams("arbitrary"),
    )(dm, up, up, w, w, b2, b2)


def _lru_gates_fwd(xc, wg, b4, lam, t_real, name):
    tp = xc.shape[0]
    tm = _tile(tp, 1408)

    def body(x_ref, w_ref, b_ref, lam_ref, r0_ref, r1_ref, i0_ref, i1_ref, a0_ref, a1_ref, u0_ref, u1_ref):
        x = x_ref[...]
        xb = x.astype(BF16)
        live = _row_ids(x.shape, pl.program_id(1) * tm) < t_real
        bb = b_ref[...]
        sp = _softplus_neg(lam_ref[...])
        gate = [_sigmoid(jnp.dot(xb, w_ref[k], preferred_element_type=F32) + bb[k:k + 1, :]) for k in range(4)]
        for d, (r_ref, i_ref, a_ref, u_ref) in enumerate(((r0_ref, i0_ref, a0_ref, u0_ref), (r1_ref, i1_ref, a1_ref, u1_ref))):
            r, ig = gate[d], gate[2 + d]
            a = jnp.exp(-LRU_C * r * sp[d:d + 1, :])
            r_ref[...] = r
            i_ref[...] = ig
            a_ref[...] = a
            u_ref[...] = jnp.where(live, jnp.sqrt(1.0 - a * a) * (ig * x), 0.0)

    blk = pl.BlockSpec((tm, LANE), lambda g, i: (i, g))
    return pl.pallas_call(
        body, name=name, grid=(LRU_BLOCKS, tp // tm),
        in_specs=[blk, pl.BlockSpec((None, 4, LANE, LANE), lambda g, i: (g, 0, 0, 0)),
                  pl.BlockSpec((4, LANE), lambda g, i: (0, g)), pl.BlockSpec((2, LANE), lambda g, i: (0, g))],
        out_specs=[blk] * 8,
        out_shape=[jax.ShapeDtypeStruct((tp, D_MODEL), F32)] * 8,
        compiler_params=_cparams("parallel", "parallel"),
    )(xc, wg, b4, lam)


def _lru_gates_bwd(l0, l1, da0, da1, r0, r1, i0, i1, a0, a1, xc, wg, lam, t_real, name):
    tp = xc.shape[0]
    tm = _tile(tp, 1408)

    rc = 32
    assert tm % rc == 0

    def fold(v):
        out = v[0:SUBLANE]
        for t in range(1, rc // SUBLANE):
            out = out + v[t * SUBLANE:(t + 1) * SUBLANE]
        return out

    def body(l0_ref, l1_ref, da0_ref, da1_ref, r0_ref, r1_ref, i0_ref, i1_ref, a0_ref, a1_ref, x_ref, w_ref, lam_ref,
             dx_ref, dw_ref, db_ref, dlam_ref, pre_s, dxp_s):
        i = pl.program_id(1)
        lam_ = lam_ref[...]
        sp = _softplus_neg(lam_)
        dsp_dlam = -_sigmoid(-lam_)

        def chunk(c, sums):
            r0 = pl.multiple_of(c * rc, rc)
            rows = pl.ds(r0, rc)
            x = x_ref[rows, :]
            live = _row_ids((rc, LANE), i * tm + r0) < t_real
            dxp = jnp.zeros_like(x)
            sums = list(sums)
            for d, (l_ref, da_ref, r_ref, i_ref, a_ref) in enumerate(((l0_ref, da0_ref, r0_ref, i0_ref, a0_ref),
                                                                      (l1_ref, da1_ref, r1_ref, i1_ref, a1_ref))):
                r, ig, a = r_ref[rows, :], i_ref[rows, :], a_ref[rows, :]
                du = jnp.where(live, l_ref[rows, :], 0.0)
                a2 = a * a
                rs = lax.rsqrt(1.0 - a2)
                dv = du * ((1.0 - a2) * rs)
                ds = du * (ig * x)
                dla = jnp.where(live, da_ref[rows, :] * a - ds * (a2 * rs), 0.0)
                dr = dla * (-LRU_C) * sp[d:d + 1, :]
                p_r = dr * r * (1.0 - r)
                p_i = dv * x * ig * (1.0 - ig)
                pre_s[d, rows, :] = p_r.astype(BF16)
                pre_s[2 + d, rows, :] = p_i.astype(BF16)
                dxp = dxp + dv * ig
                sums[d] = sums[d] + fold(p_r)
                sums[2 + d] = sums[2 + d] + fold(p_i)
                sums[4 + d] = sums[4 + d] + fold(dla * (-LRU_C) * r)
            dxp_s[rows, :] = dxp
            return tuple(sums)

        zero = jnp.zeros((SUBLANE, LANE), F32)
        sums = lax.fori_loop(0, tm // rc, chunk, (zero,) * 6)

        @pl.when(i == 0)
        def _():
            dw_ref[...] = jnp.zeros_like(dw_ref)
            db_ref[...] = jnp.zeros_like(db_ref)
            dlam_ref[...] = jnp.zeros_like(dlam_ref)

        xb = x_ref[...].astype(BF16)
        dx = dxp_s[...]
        for k in range(4):
            pk = pre_s[k]
            dx = dx + lax.dot_general(pk, w_ref[k], (((1,), (1,)), ((), ())), preferred_element_type=F32)
            dw_ref[k] += lax.dot_general(xb, pk, (((0,), (0,)), ((), ())), preferred_element_type=F32)
        db_ref[...] += jnp.concatenate([jnp.sum(sums[k], axis=0, keepdims=True) for k in range(4)], axis=0)
        dlam_ref[...] += jnp.concatenate([jnp.sum(sums[4 + d], axis=0, keepdims=True) * dsp_dlam[d:d + 1, :] for d in range(2)], axis=0)
        dx_ref[...] = dx

    blk = pl.BlockSpec((tm, LANE), lambda g, i: (i, g))
    return pl.pallas_call(
        body, name=name, grid=(LRU_BLOCKS, tp // tm),
        in_specs=[blk] * 11 + [pl.BlockSpec((None, 4, LANE, LANE), lambda g, i: (g, 0, 0, 0)),
                               pl.BlockSpec((2, LANE), lambda g, i: (0, g))],
        out_specs=[blk, pl.BlockSpec((None, 4, LANE, LANE), lambda g, i: (g, 0, 0, 0)),
                   pl.BlockSpec((4, LANE), lambda g, i: (0, g)), pl.BlockSpec((2, LANE), lambda g, i: (0, g))],
        out_shape=[jax.ShapeDtypeStruct((tp, D_MODEL), F32), jax.ShapeDtypeStruct((LRU_BLOCKS, 4, LANE, LANE), F32),
                   jax.ShapeDtypeStruct((4, D_MODEL), F32), jax.ShapeDtypeStruct((2, D_MODEL), F32)],
        scratch_shapes=[pltpu.VMEM((4, tm, LANE), BF16), pltpu.VMEM((tm, LANE), F32)],
        compiler_params=_cparams("parallel", "arbitrary"),
    )(l0, l1, da0, da1, r0, r1, i0, i1, a0, a1, xc, wg, lam)


SCAN_UNROLL = 4


def _loop_tiles(nt, step, carry):
    assert nt % SCAN_UNROLL == 0

    def trip(tt, c):
        for u in range(SCAN_UNROLL):
            c = step(tt * SCAN_UNROLL + u, c)
        return c

    return lax.fori_loop(0, nt // SCAN_UNROLL, trip, carry)


def _tile_scan(a, u, reverse):
    rows = lax.broadcasted_iota(jnp.int32, a.shape, 0)
    for s in (1, 2, 4):
        if reverse:
            keep = rows < SUBLANE - s
            a_sh, u_sh = pltpu.roll(a, SUBLANE - s, 0), pltpu.roll(u, SUBLANE - s, 0)
        else:
            keep = rows >= s
            a_sh, u_sh = pltpu.roll(a, s, 0), pltpu.roll(u, s, 0)
        u = u + a * jnp.where(keep, u_sh, 0.0)
        a = a * jnp.where(keep, a_sh, 1.0)
    return a, u


def _scan_fwd(a0, u0, a1, u1, proj, name):
    tp, d = a0.shape
    tc = 128
    nt = tp // SUBLANE

    def body(a0_ref, u0_ref, a1_ref, u1_ref, lg_ref, h0_ref, h1_ref, gh_ref):
        def step(t, carry):
            c0, c1 = carry
            f = pl.multiple_of(t * SUBLANE, SUBLANE)
            b = pl.multiple_of((nt - 1 - t) * SUBLANE, SUBLANE)
            pa, pu = _tile_scan(a0_ref[pl.ds(f, SUBLANE), :], u0_ref[pl.ds(f, SUBLANE), :], False)
            h = pu + pa * c0
            h0_ref[pl.ds(f, SUBLANE), :] = h
            c0 = h[SUBLANE - 1:SUBLANE, :]
            pa, pu = _tile_scan(a1_ref[pl.ds(b, SUBLANE), :], u1_ref[pl.ds(b, SUBLANE), :], True)
            h = pu + pa * c1
            h1_ref[pl.ds(b, SUBLANE), :] = h
            c1 = h[0:1, :]
            return c0, c1

        z = jnp.zeros((1, tc), F32)
        _loop_tiles(nt, step, (z, z))
        gh_ref[...] = (_gelu(lg_ref[...]) * (h0_ref[...] + h1_ref[...])).astype(BF16)

    blk = pl.BlockSpec((tp, tc), lambda j: (0, j))
    return pl.pallas_call(
        body, name=name, grid=(d // tc,),
        in_specs=[blk] * 4 + [pl.BlockSpec((tp, tc), lambda j: (0, j + C_LRU_G // tc))], out_specs=[blk] * 3,
        out_shape=[jax.ShapeDtypeStruct((tp, d), F32)] * 2 + [jax.ShapeDtypeStruct((tp, d), BF16)],
        compiler_params=_cparams("parallel"),
    )(a0, u0, a1, u1, proj)


def _scan_bwd(dh, a0, a1, h0, h1, name):
    tp, d = dh.shape
    tc = 128
    nt = tp // SUBLANE

    def body(dh_ref, a0_ref, a1_ref, h0_ref, h1_ref, l0_ref, l1_ref, da0_ref, da1_ref):
        rows8 = lax.broadcasted_iota(jnp.int32, (SUBLANE, tc), 0)

        def step(t, carry):
            c0, c1 = carry
            b = pl.multiple_of((nt - 1 - t) * SUBLANE, SUBLANE)
            f = pl.multiple_of(t * SUBLANE, SUBLANE)
            a = a0_ref[pl.ds(b, SUBLANE), :]
            a_next = jnp.where(rows8 < SUBLANE - 1, pltpu.roll(a, SUBLANE - 1, 0), 1.0)
            pa, pu = _tile_scan(a_next, dh_ref[pl.ds(b, SUBLANE), :], True)
            lam = pu + pa * c0
            l0_ref[pl.ds(b, SUBLANE), :] = lam
            c0 = a[0:1, :] * lam[0:1, :]
            a = a1_ref[pl.ds(f, SUBLANE), :]
            a_prev = jnp.where(rows8 >= 1, pltpu.roll(a, 1, 0), 1.0)
            pa, pu = _tile_scan(a_prev, dh_ref[pl.ds(f, SUBLANE), :], False)
            lam = pu + pa * c1
            l1_ref[pl.ds(f, SUBLANE), :] = lam
            c1 = a[SUBLANE - 1:SUBLANE, :] * lam[SUBLANE - 1:SUBLANE, :]
            return c0, c1

        z = jnp.zeros((1, tc), F32)
        _loop_tiles(nt, step, (z, z))
        rows = lax.broadcasted_iota(jnp.int32, (tp, tc), 0)
        da0_ref[...] = l0_ref[...] * jnp.where(rows >= 1, pltpu.roll(h0_ref[...], 1, 0), 0.0)
        da1_ref[...] = l1_ref[...] * jnp.where(rows < tp - 1, pltpu.roll(h1_ref[...], tp - 1, 0), 0.0)

    blk = pl.BlockSpec((tp, tc), lambda j: (0, j))
    return pl.pallas_call(
        body, name=name, grid=(d // tc,), in_specs=[blk] * 5, out_specs=[blk] * 4,
        out_shape=[jax.ShapeDtypeStruct((tp, d), F32)] * 4,
        compiler_params=_cparams("parallel"),
    )(dh, a0, a1, h0, h1)


def _gated_h_bwd(dgh, proj, h0, h1, dproj, name):
    def fn(row0, dg, lg, x0, x1):
        act, dact = _gelu_and_grad(lg)
        return dg * (x0 + x1) * dact, dg * act

    return _rows(fn, name, [Rw(dgh), Rw(proj, D_MODEL, C_LRU_G // D_MODEL), Rw(h0), Rw(h1)],
                 [Into(dproj, C_LRU_G, D_MODEL), (D_MODEL, F32)])


def _mix(proj, y_mla, y_lru, name):
    def fn(row0, gm, gl, ym, yl):
        return _sigmoid(gm) * ym + _sigmoid(gl) * yl

    return _rows(fn, name, [Rw(proj, D_MODEL, C_G_MLA // D_MODEL), Rw(proj, D_MODEL, C_G_LRU // D_MODEL), Rw(y_mla), Rw(y_lru)],
                 [(D_MODEL, BF16)])[0]


def _mix_bwd(dz, proj, y_mla, y_lru, dproj, name):
    def fn(row0, dzb, gm, gl, ym, yl):
        sm, sl = _sigmoid(gm), _sigmoid(gl)
        dg = jnp.concatenate([dzb * ym * sm * (1.0 - sm), dzb * yl * sl * (1.0 - sl)], axis=1)
        return dzb * sm, dzb * sl, dg

    return _rows(fn, name, [Rw(dz), Rw(proj, D_MODEL, C_G_MLA // D_MODEL), Rw(proj, D_MODEL, C_G_LRU // D_MODEL),
                            Rw(y_mla), Rw(y_lru)], [(D_MODEL, BF16), (D_MODEL, BF16), Into(dproj, C_G_MLA, 2 * D_MODEL)])


def _layer_fwd(h, w_in, more_weights, cs, t_real, tag):
    proj = _matmul(h, w_in, tag + "proj", tb=True)
    w = dict(more_weights(0, proj), w_in=w_in)
    cqn, ckvn = _mla_norms(proj, w['q_norm'], w['kv_norm'], tag + "mla_norms")
    qext = _matmul(cqn, w['w_q'], tag + "q_up")
    kv = _matmul(ckvn, w['w_kv'], tag + "kv_up")
    qc, kc, vb = _mla_pack(qext, kv, proj, cs, tag + "mla_pack")
    o, lse = _attn_fwd(qc, kc, vb, t_real, tag + "attn_fwd")
    w.update(more_weights(1, o))
    y_mla = _matmul(o, w['w_o_mla'], tag + "o_mla")
    xc = _lru_conv_fwd(proj, w['lru_conv_w'], w['lru_conv_b'], t_real, tag + "lru_conv")
    r0, r1, i0, i1, a0, a1, u0, u1 = _lru_gates_fwd(xc, w['w_g'], w['b4'], w['lru_lambda'], t_real, tag + "lru_gates")
    h0, h1, gh = _scan_fwd(a0, u0, a1, u1, proj, tag + "lru_scan")
    y_lru = _matmul(gh, w['w_o_lru'], tag + "o_lru")
    z = _mix(proj, y_mla, y_lru, tag + "mix")
    zo = _matmul(z, w['w_out'], tag + "w_out")
    hm = _ln_fwd([(DN_ALPHA, h), (1.0, zo)], w['ln1_g'], w['ln1_b'], tag + "ln1")
    w.update(more_weights(2, hm))
    up = _matmul(hm, w['w_up'], tag + "w_up", tb=True)
    m = _ffn_conv_act(up, w['ffn_conv_w'], w['ffn_conv_b'], t_real, tag + "ffn_conv")
    f = _matmul(m, w['w_down'], tag + "w_down", tk_cap=1408)
    out = _ln_fwd([(DN_ALPHA, hm), (1.0, f)], w['ln2_g'], w['ln2_b'], tag + "ln2")
    saved = dict(w=w, h=h, proj=proj, cqn=cqn, ckvn=ckvn, qc=qc, kc=kc, vb=vb, o=o, lse=lse, y_mla=y_mla, xc=xc,
                 r0=r0, r1=r1, i0=i0, i1=i1, a0=a0, a1=a1, h0=h0, h1=h1, gh=gh, y_lru=y_lru, z=z, zo=zo, hm=hm,
                 up=up, m=m, f=f)
    return out, saved


DW_MATMUL = dict(ta=True, out_dtype=BF16, tn_cap=2304, tk_cap=1408)


def _after(a, tok):
    return a if tok is None else a + tok.astype(a.dtype)


def _layer_bwd(dout_terms, s, cs, t_real, tag, emit, tok):
    w = s['w']
    g = {}
    du2, dg2, db2 = _ln_bwd(dout_terms, [(DN_ALPHA, s['hm']), (1.0, s['f'])], _after(w['ln2_g'], tok), tag + "ln2_bwd")
    g['ln2_g'], g['ln2_b'] = dg2, db2
    dm = _matmul(du2, w['w_down'], tag + "w_down_dx", tb=True)
    g['w_down'] = _matmul(s['m'], du2, tag + "w_down_dw", **DW_MATMUL)
    dup, dwg_, dwv_, dbg_, dbv_ = _ffn_conv_act_bwd(dm, s['up'], w['ffn_conv_w'], w['ffn_conv_b'], t_real, tag + "ffn_conv_bwd")
    g['ffn_conv_w'] = jnp.concatenate([dwg_, dwv_], axis=1)
    g['ffn_conv_b'] = jnp.concatenate([dbg_, dbv_], axis=1)
    dhm_mm = _matmul(dup, w['w_up'], tag + "w_up_dx", tk_cap=1408)
    g['w_up'] = _matmul(s['hm'], dup, tag + "w_up_dw", **DW_MATMUL)
    tok = emit('ffn', g)
    g = {}
    du1, dg1, db1 = _ln_bwd([(DN_ALPHA, du2), (1.0, dhm_mm)], [(DN_ALPHA, s['h']), (1.0, s['zo'])],
                            _after(w['ln1_g'], tok), tag + "ln1_bwd")
    g['ln1_g'], g['ln1_b'] = dg1, db1
    dz = _matmul(du1, w['w_out'], tag + "w_out_dx", tb=True)
    g['w_out'] = _matmul(s['z'], du1, tag + "w_out_dw", **DW_MATMUL)
    dproj = lax.empty(s['proj'].shape, BF16)
    dy_mla, dy_lru, dproj = _mix_bwd(dz, s['proj'], s['y_mla'], s['y_lru'], dproj, tag + "mix_bwd")
    do = _matmul(dy_mla, w['w_o_mla'], tag + "o_mla_dx", tb=True)
    g['w_o_mla'] = _matmul(s['o'], dy_mla, tag + "o_mla_dw", **DW_MATMUL)
    dqext, dkc, dv = _attn_bwd(s['qc'], s['kc'], s['vb'], do, s['o'], s['lse'], cs, t_real, tag + "attn_bwd")
    dkv, dkrp = _mla_unpack(dkc, dv, cs, tag + "mla_unpack")
    dcqn = _matmul(dqext, w['w_q'], tag + "q_up_dx", tb=True)
    g['w_q'] = _matmul(s['cqn'], dqext, tag + "q_up_dw", **DW_MATMUL)
    dckvn = _matmul(dkv, w['w_kv'], tag + "kv_up_dx", tb=True)
    g['w_kv'] = _matmul(s['ckvn'], dkv, tag + "kv_up_dw", **DW_MATMUL)
    dgh = _matmul(dy_lru, w['w_o_lru'], tag + "o_lru_dx", tb=True)
    g['w_o_lru'] = _matmul(s['gh'], dy_lru, tag + "o_lru_dw", **DW_MATMUL)
    dproj, dhs = _gated_h_bwd(dgh, s['proj'], s['h0'], s['h1'], dproj, tag + "lru_gate_out_bwd")
    l0, l1, da0, da1 = _scan_bwd(dhs, s['a0'], s['a1'], s['h0'], s['h1'], tag + "lru_scan_bwd")
    dxc, g['w_g'], g['b4'], g['lru_lambda'] = _lru_gates_bwd(
        l0, l1, da0, da1, s['r0'], s['r1'], s['i0'], s['i1'], s['a0'], s['a1'], s['xc'], w['w_g'], w['lru_lambda'],
        t_real, tag + "lru_gates_bwd")
    dproj, g['lru_conv_w'], g['lru_conv_b'] = _lru_conv_bwd(dxc, s['proj'], w['lru_conv_w'], dproj, t_real, tag + "lru_conv_bwd")
    tok = emit('mid', g)
    dproj, dqn, dkvn = _mla_norms_bwd(dcqn, dckvn, _after(dkrp, tok), s['proj'], w['q_norm'], w['kv_norm'], dproj,
                                      tag + "mla_norms_bwd")
    tok = emit('in', {'w_in': _matmul(s['h'], dproj, tag + "proj_dw", **DW_MATMUL), 'q_norm': dqn, 'kv_norm': dkvn})
    dh_mm = _matmul(dproj, w['w_in'], tag + "proj_dx", tk_cap=1536)
    return [(DN_ALPHA, du1), (1.0, dh_mm)], tok


def _swap_halves(a, axis=-1):
    h1, h2 = jnp.split(a, 2, axis=axis)
    return jnp.concatenate([h2, h1], axis=axis)


def _w_in_kernel(w_in_t):
    cq, ckv, kr, lg, lx, gm, gl = jnp.split(w_in_t, [256, 384, 448, 1472, 2496, 3520], axis=0)
    return jnp.concatenate([lg, lx, gm, gl, cq, ckv, kr, _swap_halves(kr, axis=0)], axis=0)


def _layer_weights(fl):
    w = {}
    if 'w_uq' in fl:
        uq = fl['w_uq']
        w['w_q'] = jnp.concatenate([uq, _swap_halves(uq[..., QK_NOPE:])], axis=-1).reshape(Q_RANK, HEADS * 2 * LANE)
        w['w_kv'] = jnp.concatenate([fl['w_uk'].reshape(KV_RANK, -1), fl['w_uv'].reshape(KV_RANK, -1)], axis=1).astype(BF16)
        w['w_g'] = jnp.moveaxis(jnp.concatenate([fl['w_rg'], fl['w_ig']], axis=0), 0, 1).astype(BF16)
        w['b4'] = jnp.concatenate([fl['b_rg'], fl['b_ig']], axis=0)
    for n in ('q_norm', 'kv_norm', 'w_o_mla', 'lru_conv_w', 'lru_conv_b', 'lru_lambda', 'w_o_lru', 'w_out', 'ln1_g',
              'ln1_b', 'w_up', 'ffn_conv_w', 'ffn_conv_b', 'w_down', 'ln2_g', 'ln2_b'):
        if n in fl:
            w[n] = fl[n]
    return w


def _layer_grads(g):
    out = {}
    if 'w_in' in g:
        lg, lx, gm, gl, cq, ckv, kr, krs = jnp.split(g['w_in'], [1024, 2048, 3072, 4096, 4352, 4480, 4544], axis=1)
        out['w_in'] = jnp.concatenate([cq, ckv, kr + _swap_halves(krs), lg, lx, gm, gl], axis=1)
    if 'w_q' in g:
        gq = g['w_q'].reshape(Q_RANK, HEADS, 2 * LANE)
        out['w_uq'] = jnp.concatenate([gq[..., :QK_NOPE], gq[..., QK_NOPE:QK_NOPE + QK_ROPE] + _swap_halves(gq[..., QK_NOPE + QK_ROPE:])], axis=-1)
    if 'w_kv' in g:
        out['w_uk'] = g['w_kv'][:, :HEADS * QK_NOPE].reshape(KV_RANK, HEADS, QK_NOPE)
        out['w_uv'] = g['w_kv'][:, HEADS * QK_NOPE:].reshape(KV_RANK, HEADS, V_HEAD)
    if 'w_g' in g:
        gg = jnp.moveaxis(g['w_g'], 1, 0)
        out['w_rg'], out['w_ig'] = gg[:2], gg[2:]
    if 'b4' in g:
        out['b_rg'], out['b_ig'] = g['b4'][:2], g['b4'][2:]
    for n in ('q_norm', 'kv_norm', 'lru_conv_b', 'ln1_g', 'ln1_b', 'ffn_conv_b', 'ln2_g', 'ln2_b'):
        if n in g:
            out[n] = g[n].reshape(-1)
    for n in ('w_o_mla', 'lru_conv_w', 'lru_lambda', 'w_o_lru', 'w_out', 'w_up', 'ffn_conv_w', 'w_down'):
        if n in g:
            out[n] = g[n]
    return out


def _rope_table(tp):
    half = QK_ROPE // 2
    inv_freq = jnp.exp(-math.log(ROPE_THETA) * jnp.arange(half, dtype=F32) / half)
    ang = jnp.arange(tp, dtype=F32)[:, None] * inv_freq[None, :]
    c, s = jnp.cos(ang), jnp.sin(ang)
    return jnp.concatenate([c, c, -s, s], axis=1)


def _local_step(x, target, meta, ln0_g, ln0_b, layer_w, t_pad, emit):
    seq = x.shape[0]
    t_real = N_META + seq
    zpad = jnp.zeros((t_pad - t_real, D_MODEL), F32)
    xin = jnp.concatenate([meta, x, zpad], axis=0)
    tgt = jnp.concatenate([jnp.zeros((N_META, D_MODEL), F32), target, zpad], axis=0)
    cs = _rope_table(t_pad)
    h = _ln_fwd([(1.0, xin)], ln0_g, ln0_b, "ln0")
    saved = []
    for l in range(DEPTH):
        w_in, rest_of_weights = layer_w[l](h)
        h, s = _layer_fwd(h, w_in, rest_of_weights, cs, t_real, "l%d_" % l)
        saved.append(s)
    dy, lossvec = _loss_head(h, tgt, t_real, "loss_head")
    terms, tok = [(1.0, dy)], None
    for l in reversed(range(DEPTH)):
        terms, tok = _layer_bwd(terms, saved[l], cs, t_real, "l%d_" % l,
                                functools.partial(lambda stage, g, l: emit(l, stage, _layer_grads(g)), l=l), tok)
    dxin, dg0, db0 = _ln_bwd(terms, [(1.0, xin)], _after(ln0_g, tok), "ln0_bwd")
    emit(None, 'head', {'meta_tokens': dxin[:N_META], 'ln0_g': dg0.reshape(-1), 'ln0_b': db0.reshape(-1), 'loss': lossvec})
    return dxin[N_META:t_real]


_HBM = pl.BlockSpec(memory_space=pltpu.HBM)
_SEM = pl.BlockSpec(memory_space=pltpu.SEMAPHORE)
_SIDE_EFFECT = pltpu.SideEffectType.DATAFLOW_SIDE_EFFECTING


def _peer_copies(src_refs, land_refs, scatters, send_sems, recv_sems):
    x, y, c = lax.axis_index("x"), lax.axis_index("y"), lax.axis_index("c")
    me = 4 * x + 2 * y + c
    copies = []
    for k in range(1, N_DEV):
        px = 1 - x if k & 4 else x
        py = 1 - y if k & 2 else y
        pc = 1 - c if k & 1 else c
        for t, (src, land) in enumerate(zip(src_refs, land_refs)):
            copies.append(pltpu.make_async_remote_copy(
                src_ref=src.at[4 * px + 2 * py + pc] if scatters[t] else src, dst_ref=land.at[me],
                send_sem=send_sems.at[7 * t + k - 1], recv_sem=recv_sems.at[7 * t + k - 1],
                device_id=(px, py, pc), device_id_type=pl.DeviceIdType.MESH))
    return me, copies


def _own_block_in_place(land, own):
    me = 4 * lax.axis_index("x") + 2 * lax.axis_index("y") + lax.axis_index("c")
    return lax.dynamic_update_slice_in_dim(land, own, me, 0)


def _gather_two_level(shards, name):
    nt = len(shards)

    def body(*refs):
        x_refs, out_refs = refs[:nt], refs[nt:2 * nt]
        token_ref, send_sems, recv_sems = refs[2 * nt:]
        x, y, c = lax.axis_index("x"), lax.axis_index("y"), lax.axis_index("c")
        me, sibling = (x, y, c), (x, y, 1 - c)
        chips = [(1 - x, y), (x, 1 - y), (1 - x, 1 - y)]

        def copy(t, k, block, to, own=False):
            px, py, pc = block
            slot = out_refs[t].at[4 * px + 2 * py + pc]
            return pltpu.make_async_remote_copy(
                src_ref=x_refs[t] if own else slot, dst_ref=slot,
                send_sem=send_sems.at[7 * t + k], recv_sem=recv_sems.at[7 * t + k],
                device_id=to, device_id_type=pl.DeviceIdType.MESH)

        sent = []
        for t in range(nt):
            first = [copy(t, 1 + j, me, (*chip, c), own=True) for j, chip in enumerate(chips)]
            first.append(copy(t, 0, me, sibling, own=True))
            for cp in first:
                cp.start()
            sent += first
        token_ref[...] = jnp.zeros_like(token_ref)
        for j, chip in enumerate(chips):
            for t in range(nt):
                copy(t, 1 + j, (*chip, c), me).wait_recv()
                passed = copy(t, 4 + j, (*chip, c), sibling)
                passed.start()
                sent.append(passed)
        for t in range(nt):
            copy(t, 0, sibling, me).wait_recv()
            for j, chip in enumerate(chips):
                copy(t, 4 + j, (*chip, 1 - c), me).wait_recv()
        for cp in sent:
            cp.wait_send()

    any_space = pl.BlockSpec(memory_space=pl.ANY)
    res = pl.pallas_call(
        body, name=name,
        out_shape=[jax.ShapeDtypeStruct((N_DEV,) + a.shape, a.dtype) for a in shards] + [jax.ShapeDtypeStruct((SUBLANE, LANE), F32)],
        in_specs=[any_space] * nt, out_specs=[any_space] * nt + [pl.BlockSpec(memory_space=pltpu.VMEM)],
        scratch_shapes=[pltpu.SemaphoreType.DMA((7 * nt,)), pltpu.SemaphoreType.DMA((7 * nt,))],
    )(*shards)
    return [_own_block_in_place(land, a[None]) for land, a in zip(res[:nt], shards)], res[nt][0, 0]


def _exchange_start(groups, name):
    flat = [it for grp in groups for it in grp]
    nt, ng = len(flat), len(groups)
    scatters = [sc for _, sc in flat]
    srcs = [pltpu.with_memory_space_constraint(a, pltpu.HBM) for a, _ in flat]
    land_shapes = [a.shape if sc else (N_DEV,) + a.shape for a, sc in flat]
    lands = [pltpu.with_memory_space_constraint(lax.empty(s, a.dtype), pltpu.HBM) for s, (a, _) in zip(land_shapes, flat)]
    bounds = [0]
    for grp in groups:
        bounds.append(bounds[-1] + len(grp))

    def body(*refs):
        src_refs, land_refs = refs[:nt], refs[nt:2 * nt]
        sem_refs = refs[2 * nt:2 * nt + 2 * ng]
        token_ref = refs[4 * nt + 2 * ng]
        for gi in range(ng):
            lo, hi = bounds[gi], bounds[gi + 1]
            _, copies = _peer_copies(src_refs[lo:hi], land_refs[lo:hi], scatters[lo:hi], sem_refs[2 * gi], sem_refs[2 * gi + 1])
            for cp in copies:
                cp.start()
        token_ref[...] = jnp.zeros_like(token_ref)

    out_shape = []
    for grp in groups:
        out_shape += [pltpu.SemaphoreType.DMA((7 * len(grp),)), pltpu.SemaphoreType.DMA((7 * len(grp),))]
    out_shape += [pltpu.HBM(a.shape, a.dtype) for a in srcs] + [pltpu.HBM(s, a.dtype) for s, a in zip(land_shapes, srcs)]
    out_shape += [jax.ShapeDtypeStruct((SUBLANE, LANE), F32)]
    res = pl.pallas_call(
        body, name=name, out_shape=out_shape,
        in_specs=[_HBM] * (2 * nt),
        out_specs=[_SEM] * (2 * ng) + [_HBM] * (2 * nt) + [pl.BlockSpec(memory_space=pltpu.VMEM)],
        input_output_aliases={t: 2 * ng + t for t in range(2 * nt)},
        compiler_params=pltpu.CompilerParams(has_side_effects=_SIDE_EFFECT),
    )(*srcs, *lands)
    sems, thru, token = res[:2 * ng], res[2 * ng:2 * ng + 2 * nt], res[-1]
    states = []
    for gi in range(ng):
        lo, hi = bounds[gi], bounds[gi + 1]
        states.append((sems[2 * gi], sems[2 * gi + 1], thru[lo:hi], thru[nt + lo:nt + hi], scatters[lo:hi]))
    return states, token[0, 0]


def _exchange_wait(state, after, name):
    send_sems, recv_sems, srcs, lands, scatters = state
    n = len(srcs)

    def body(*refs):
        _, copies = _peer_copies(refs[:n], refs[n:2 * n], scatters, refs[2 * n], refs[2 * n + 1])
        for cp in copies:
            cp.wait_send()
        for cp in copies:
            cp.wait_recv()

    res = pl.pallas_call(
        body, name=name,
        out_shape=[pltpu.HBM(a.shape, a.dtype) for a in srcs] + [pltpu.HBM(a.shape, a.dtype) for a in lands],
        in_specs=[_HBM] * (2 * n) + [_SEM, _SEM, _HBM],
        out_specs=[_HBM] * (2 * n),
        input_output_aliases={t: t for t in range(2 * n)},
        compiler_params=pltpu.CompilerParams(has_side_effects=_SIDE_EFFECT),
    )(*srcs, *lands, send_sems, recv_sems, pltpu.with_memory_space_constraint(after, pltpu.HBM))
    me = 4 * lax.axis_index("x") + 2 * lax.axis_index("y") + lax.axis_index("c")
    out = []
    for src, land, sc in zip(res[:n], res[n:], scatters):
        own = lax.dynamic_index_in_dim(src, me, 0, keepdims=True) if sc else src[None]
        out.append(lax.dynamic_update_slice_in_dim(land, own, me, 0))
    return out


def _as_rows(shape):
    return (1, shape[0]) if len(shape) == 1 else (math.prod(shape[:-1]), shape[-1])


def _sum_adamw(pieces, w, m, v, name):
    shape = w.shape
    nl = len(pieces)
    if nl > 1 and _as_rows(shape[1:])[0] % 16:
        pieces, nl = [jnp.stack(pieces, axis=1)], 1
    rows, cols = _as_rows(shape)
    rl = rows // nl
    cap = max(16, (1 << 18) // cols // 16 * 16)
    tr = _tile(rl, cap, 16)
    nb = rl // tr
    c1 = 1.0 / (1.0 - ADAM_B1 ** ADAM_STEP)
    c2 = 1.0 / (1.0 - ADAM_B2 ** ADAM_STEP)

    def body(*refs):
        p_refs = refs[:nl]
        w_ref, m_ref, v_ref, g_ref, d_ref, nm_ref, nv_ref = refs[nl:]
        li = pl.program_id(0)

        def total(p_ref):
            acc = p_ref[0].astype(F32)
            for k in range(1, N_DEV):
                acc = acc + p_ref[k].astype(F32)
            return acc

        gg = total(p_refs[0])
        for l in range(1, nl):
            gg = jnp.where(li == l, total(p_refs[l]), gg)
        nm = ADAM_B1 * m_ref[...] + (1.0 - ADAM_B1) * gg
        nv = ADAM_B2 * v_ref[...] + (1.0 - ADAM_B2) * (gg * gg)
        g_ref[...] = gg
        d_ref[...] = -ADAM_LR * ((nm * c1) / (jnp.sqrt(nv * c2) + ADAM_EPS) + ADAM_WD * w_ref[...])
        nm_ref[...] = nm
        nv_ref[...] = nv

    blk = pl.BlockSpec((tr, cols), lambda li, i: (li * nb + i, 0))
    p_specs = [pl.BlockSpec((N_DEV, tr, cols), functools.partial(lambda li, i, l: (0, jnp.where(li == l, i, 0), 0), l=l))
               for l in range(nl)]
    res = pl.pallas_call(
        body, name=name, grid=(nl, nb),
        in_specs=p_specs + [blk] * 3, out_specs=[blk] * 4,
        out_shape=[jax.ShapeDtypeStruct((rows, cols), F32)] * 4,
        compiler_params=_cparams("parallel", "parallel"),
    )(*[p.reshape(N_DEV, rl, cols) for p in pieces], *[a.reshape(rows, cols) for a in (w, m, v)])
    return [r.reshape(shape) for r in res]


def _to_shards(full, axis):
    shp = full.shape
    a = full.reshape(shp[:axis] + (N_DEV, shp[axis] // N_DEV) + shp[axis + 1:])
    return jnp.moveaxis(a, axis, 0)


def _from_shards(blocks, axis):
    a = jnp.moveaxis(blocks, 0, axis)
    shp = a.shape
    return a.reshape(shp[:axis] + (shp[axis] * shp[axis + 1],) + shp[axis + 2:])


def kernel(x, meta_tokens, ln0_g, ln0_b, w_in, q_norm, kv_norm, w_uq, w_uk, w_uv, w_o_mla, lru_conv_w, lru_conv_b, w_rg, b_rg, w_ig, b_ig, lru_lambda, w_o_lru, w_out, ln1_g, ln1_b, w_up, ffn_conv_w, ffn_conv_b, w_down, ln2_g, ln2_b, loss_target, m_meta_tokens, m_ln0_g, m_ln0_b, m_w_in, m_q_norm, m_kv_norm, m_w_uq, m_w_uk, m_w_uv, m_w_o_mla, m_lru_conv_w, m_lru_conv_b, m_w_rg, m_b_rg, m_w_ig, m_b_ig, m_lru_lambda, m_w_o_lru, m_w_out, m_ln1_g, m_ln1_b, m_w_up, m_ffn_conv_w, m_ffn_conv_b, m_w_down, m_ln2_g, m_ln2_b, v_meta_tokens, v_ln0_g, v_ln0_b, v_w_in, v_q_norm, v_kv_norm, v_w_uq, v_w_uk, v_w_uv, v_w_o_mla, v_lru_conv_w, v_lru_conv_b, v_w_rg, v_b_rg, v_w_ig, v_b_ig, v_lru_lambda, v_w_o_lru, v_w_out, v_ln1_g, v_ln1_b, v_w_up, v_ffn_conv_w, v_ffn_conv_b, v_w_down, v_ln2_g, v_ln2_b):
    args = (meta_tokens, ln0_g, ln0_b, w_in, q_norm, kv_norm, w_uq, w_uk, w_uv, w_o_mla, lru_conv_w, lru_conv_b, w_rg, b_rg, w_ig, b_ig, lru_lambda, w_o_lru, w_out, ln1_g, ln1_b, w_up, ffn_conv_w, ffn_conv_b, w_down, ln2_g, ln2_b)
    ms = (m_meta_tokens, m_ln0_g, m_ln0_b, m_w_in, m_q_norm, m_kv_norm, m_w_uq, m_w_uk, m_w_uv, m_w_o_mla, m_lru_conv_w, m_lru_conv_b, m_w_rg, m_b_rg, m_w_ig, m_b_ig, m_lru_lambda, m_w_o_lru, m_w_out, m_ln1_g, m_ln1_b, m_w_up, m_ffn_conv_w, m_ffn_conv_b, m_w_down, m_ln2_g, m_ln2_b)
    vs = (v_meta_tokens, v_ln0_g, v_ln0_b, v_w_in, v_q_norm, v_kv_norm, v_w_uq, v_w_uk, v_w_uv, v_w_o_mla, v_lru_conv_w, v_lru_conv_b, v_w_rg, v_b_rg, v_w_ig, v_b_ig, v_lru_lambda, v_w_o_lru, v_w_out, v_ln1_g, v_ln1_b, v_w_up, v_ffn_conv_w, v_ffn_conv_b, v_w_down, v_ln2_g, v_ln2_b)
    wd, md, vd = dict(zip(WEIGHTS, args)), dict(zip(WEIGHTS, ms)), dict(zip(WEIGHTS, vs))

    def shard_axis(n, l):
        return SHARD_AXIS[n] - (0 if l is None else 1)

    def shard(n, l):
        a = wd[n] if l is None else wd[n][l]
        if n in SENT_TRANSPOSED:
            a = a.T
        return a.astype(BF16) if n in BIG else a

    def whole(keys, landed):
        return {k: b.reshape(-1, b.shape[-1]) if k[0] in SENT_TRANSPOSED else _from_shards(b, shard_axis(*k))
                for k, b in zip(keys, landed)}

    first = [('meta_tokens', None), ('w_in', 0)]
    landed, token = _gather_two_level([shard(*k) for k in first], "gather_first")
    got_first = whole(first, landed)
    staged = [[(n, 0) for n in names] for names in STAGE_WEIGHTS]
    later = [(n, 1) for n in SHARDED if n != 'meta_tokens']
    groups = [[(shard(*k), False) for k in keys] for keys in staged + [later]]
    groups[0][0] = (_after(groups[0][0][0], token), False)
    gather, token = _exchange_start(groups, "gather_start")

    def arrive(gi, keys, after, name):
        return whole(keys, _exchange_wait(gather[gi], after, name))

    def layer_weights(got, l, names):
        fl = {n: wd[n][l] for n in names if n in REPLICATED}
        fl.update({n: a for (n, _), a in got.items() if n in names})
        return _layer_weights(fl)

    ln0_g = _after(wd['ln0_g'], token)

    def first_layer(h):
        def more(stage, after):
            got = arrive(stage, staged[stage], after, "gather_wait_l0_%d" % stage)
            return layer_weights(got, 0, STAGE_WEIGHTS[stage] + STAGE_REPLICATED[stage])
        return _w_in_kernel(got_first['w_in', 0]), more

    def second_layer(h):
        got = arrive(len(staged), later, h, "gather_wait_l1")
        return _w_in_kernel(got['w_in', 1]), lambda stage, after: layer_weights(got, 1, STAGE_WEIGHTS[stage] + STAGE_REPLICATED[stage])

    sent = []
    pending = []

    def send(l, stage, grads):
        for n, g in grads.items():
            if n in SHARD_AXIS:
                g = _to_shards(g, shard_axis(n, l))
                pending.append(((n, l), (g.astype(BF16) if n in BIG else g, True)))
            else:
                pending.append(((n, l), (g.astype(BF16) if n in LARGE_REPLICATED else g, False)))
        if l == DEPTH - 1 and stage != 'in':
            return None
        (state,), tok = _exchange_start([[it for _, it in pending]], "grads_start_%s_%s" % (l, stage))
        sent.append(([k for k, _ in pending], state))
        pending.clear()
        return tok

    seq = x.shape[1]
    t_pad = -(-(N_META + seq + MIN_PAD_ROWS) // LANE) * LANE
    grad_x = _local_step(x[0], loss_target[0], got_first['meta_tokens', None], ln0_g, wd['ln0_b'],
                         [first_layer, second_layer], t_pad, send)

    pieces, outs, after = {}, {}, grad_x
    for gi, (keys, state) in enumerate(sent):
        pieces.update(zip(keys, _exchange_wait(state, after, "grads_wait_%d" % gi)))
        if gi == len(sent) - 2:
            for n in WEIGHTS:
                if (n, 0) in pieces:
                    outs[n] = _sum_adamw([pieces[n, l] for l in range(DEPTH)], wd[n], md[n], vd[n], "adamw_" + n)
                    after = outs[n][1]
    loss = jnp.sum(pieces['loss', None])
    for n in WEIGHTS:
        if (n, None) in pieces:
            outs[n] = _sum_adamw([pieces[n, None]], wd[n], md[n], vd[n], "adamw_" + n)
    res = [loss, grad_x[None]]
    for k in range(4):
        res += [outs[n][k] for n in WEIGHTS]
    return tuple(res)
```

```python
import functools
import math

import jax
import jax.numpy as jnp
from jax import lax
from jax.experimental import pallas as pl
from jax.experimental.pallas import tpu as pltpu

F32 = jnp.float32
BF16 = jnp.bfloat16

N_DEV = 8
D_MODEL = 1024
N_META = 16
HEADS = 8
QK_NOPE = 128
QK_ROPE = 64
V_HEAD = 128
Q_RANK = 256
KV_RANK = 128
ROPE_THETA = 10000.0
LRU_BLOCKS = 8
LRU_C = 8.0
D_FF = 2816
DEPTH = 2
DN_ALPHA = (2.0 * DEPTH) ** 0.25
LN_EPS = 1e-5
RMS_EPS = 1e-6
LN2 = math.log(2.0)
ATT_SCALE = 1.0 / math.sqrt(QK_NOPE + QK_ROPE) / LN2
NEG_BIG = -1e30

ADAM_LR = 0.001
ADAM_B1 = 0.9
ADAM_B2 = 0.999
ADAM_EPS = 1e-08
ADAM_WD = 0.01
ADAM_STEP = 10

MIN_PAD_ROWS = 2
LANE = 128
SUBLANE = 8
VMEM_LIMIT = 56 * 1024 * 1024

PROJ_COLS = 4 * D_MODEL + Q_RANK + KV_RANK + 2 * QK_ROPE
C_LRU_G, C_LRU_X, C_G_MLA, C_G_LRU = 0, D_MODEL, 2 * D_MODEL, 3 * D_MODEL
C_CQ = 4 * D_MODEL
C_CKV = C_CQ + Q_RANK
C_KRP = C_CKV + KV_RANK

WEIGHTS = ['meta_tokens', 'ln0_g', 'ln0_b', 'w_in', 'q_norm', 'kv_norm', 'w_uq', 'w_uk', 'w_uv', 'w_o_mla',
           'lru_conv_w', 'lru_conv_b', 'w_rg', 'b_rg', 'w_ig', 'b_ig', 'lru_lambda', 'w_o_lru', 'w_out',
           'ln1_g', 'ln1_b', 'w_up', 'ffn_conv_w', 'ffn_conv_b', 'w_down', 'ln2_g', 'ln2_b']
SHARD_AXIS = {'meta_tokens': 1, 'w_in': 2, 'w_uq': 1, 'w_o_mla': 1, 'lru_conv_w': 2, 'b_rg': 2, 'b_ig': 2,
              'lru_lambda': 2, 'w_o_lru': 1, 'w_out': 1, 'w_up': 2, 'ffn_conv_w': 2, 'w_down': 1}
BIG = ['w_in', 'w_uq', 'w_o_mla', 'w_o_lru', 'w_out', 'w_up', 'w_down']
SHARDED = [n for n in WEIGHTS if n in SHARD_AXIS]
REPLICATED = [n for n in WEIGHTS if n not in SHARD_AXIS]
LARGE_REPLICATED = ['w_uk', 'w_uv', 'w_rg', 'w_ig']
SENT_TRANSPOSED = ['w_in', 'w_up']
STAGE_WEIGHTS = [['w_uq', 'lru_conv_w', 'b_rg', 'b_ig', 'lru_lambda'], ['w_o_mla', 'w_o_lru', 'w_out'], ['w_up', 'ffn_conv_w', 'w_down']]
STAGE_REPLICATED = [['q_norm', 'kv_norm', 'w_uk', 'w_uv', 'lru_conv_b', 'w_rg', 'w_ig'], ['ln1_g', 'ln1_b'], ['ffn_conv_b', 'ln2_g', 'ln2_b']]


def _cparams(*sem):
    return pltpu.CompilerParams(dimension_semantics=sem, vmem_limit_bytes=VMEM_LIMIT)


def _tile(n, cap, unit=LANE):
    best = None
    t = unit
    while t <= min(n, cap):
        if n % t == 0:
            best = t
        t += unit
    return n if best is None else best


def _sigmoid(x):
    return 1.0 / (1.0 + jnp.exp(-x))


_GELU_C = math.sqrt(2.0 / math.pi)


_GELU_A = 0.044715


def _gelu(x):
    t = jnp.tanh(x * (_GELU_C + (_GELU_C * _GELU_A) * (x * x)))
    hx = 0.5 * x
    return hx + hx * t


def _gelu_and_grad(x):
    x2 = x * x
    t = jnp.tanh(x * (_GELU_C + (_GELU_C * _GELU_A) * x2))
    hx = 0.5 * x
    dg = 0.5 + 0.5 * t + (hx * (1.0 - t * t)) * (_GELU_C + (3.0 * _GELU_C * _GELU_A) * x2)
    return hx + hx * t, dg


def _softplus_neg(lam):
    z = jnp.exp(-jnp.abs(lam))
    w = 1.0 + z
    log1p = jnp.where(w == 1.0, z, jnp.log(w) * z / (w - 1.0))
    return jnp.maximum(-lam, 0.0) + log1p


def _row_ids(shape, row0=0):
    return lax.broadcasted_iota(jnp.int32, shape, 0) + row0


def _matmul(a, b, name, ta=False, tb=False, out_dtype=F32, tm_cap=1408, tn_cap=1536, tk_cap=2048):
    if ta:
        kdim, m = a.shape
    else:
        m, kdim = a.shape
    if tb:
        n, k2 = b.shape
    else:
        k2, n = b.shape
    assert kdim == k2, (a.shape, b.shape, ta, tb)
    tm, tn, tk = _tile(m, tm_cap), _tile(n, tn_cap), _tile(kdim, tk_cap)
    nk = kdim // tk

    def body(a_ref, b_ref, o_ref, *acc):
        dn = (((0 if ta else 1,), (1 if tb else 0,)), ((), ()))
        part = lax.dot_general(a_ref[...].astype(BF16), b_ref[...].astype(BF16), dn, preferred_element_type=F32)
        if nk == 1:
            o_ref[...] = part.astype(o_ref.dtype)
            return
        acc_ref, k = acc[0], pl.program_id(2)

        @pl.when(k == 0)
        def _():
            acc_ref[...] = part

        @pl.when(k > 0)
        def _():
            acc_ref[...] += part

        @pl.when(k == nk - 1)
        def _():
            o_ref[...] = acc_ref[...].astype(o_ref.dtype)

    a_spec = pl.BlockSpec((tk, tm), lambda i, j, k: (k, i)) if ta else pl.BlockSpec((tm, tk), lambda i, j, k: (i, k))
    b_spec = pl.BlockSpec((tn, tk), lambda i, j, k: (j, k)) if tb else pl.BlockSpec((tk, tn), lambda i, j, k: (k, j))
    return pl.pallas_call(
        body, name=name,
        grid=(m // tm, n // tn, nk),
        in_specs=[a_spec, b_spec],
        out_specs=pl.BlockSpec((tm, tn), lambda i, j, k: (i, j)),
        out_shape=jax.ShapeDtypeStruct((m, n), out_dtype),
        scratch_shapes=[pltpu.VMEM((tm, tn), F32)] if nk > 1 else [],
        compiler_params=_cparams("parallel", "parallel", "arbitrary"),
    )(a, b)


class Rw:
    def __init__(self, arr, width=None, cb=0):
        self.arr, self.width, self.cb = arr, (arr.shape[1] if width is None else width), cb


class Pm:
    def __init__(self, arr):
        self.arr = arr


class Into:
    def __init__(self, arr, col0, width):
        self.arr, self.col0, self.width = arr, col0, width


def _call_with_into(body, name, grid, in_specs, operands, outs, spec_of, shape_of, extra_out_specs, extra_out_shape, sem):
    intos = [(k, o) for k, o in enumerate(outs) if isinstance(o, Into)]
    aliases = {len(operands) + n: k for n, (k, _) in enumerate(intos)}
    return pl.pallas_call(
        body, name=name, grid=grid,
        in_specs=in_specs + [pl.BlockSpec(memory_space=pl.ANY)] * len(intos),
        out_specs=[spec_of(o) for o in outs] + extra_out_specs,
        out_shape=[jax.ShapeDtypeStruct(o.arr.shape, o.arr.dtype) if isinstance(o, Into) else shape_of(o) for o in outs]
        + extra_out_shape,
        input_output_aliases=aliases,
        compiler_params=_cparams(sem),
    )(*operands, *[o.arr for _, o in intos])


def _rows(fn, name, ins, outs, accs=(), tm_cap=384):
    tp = next(o.arr.shape[0] for o in ins if isinstance(o, Rw))
    tm = _tile(tp, tm_cap)
    n_in, n_out, n_acc = len(ins), len(outs), len(accs)
    n_into = sum(isinstance(o, Into) for o in outs)

    def body(*refs):
        i = pl.program_id(0)
        res = fn(i * tm, *[r[...] for r in refs[:n_in]])
        if not isinstance(res, (tuple, list)):
            res = (res,)
        assert len(res) == n_out + n_acc, (name, len(res))
        out_refs = refs[n_in + n_into:]
        for k in range(n_out):
            out_refs[k][...] = res[k].astype(out_refs[k].dtype)
        for k in range(n_acc):
            ref = out_refs[n_out + k]

            @pl.when(i == 0)
            def _():
                ref[...] = jnp.zeros_like(ref)

            ref[...] += res[n_out + k]

    in_specs = []
    for o in ins:
        if isinstance(o, Rw):
            in_specs.append(pl.BlockSpec((tm, o.width), functools.partial(lambda i, cb: (i, cb), cb=o.cb)))
        else:
            in_specs.append(pl.BlockSpec(o.arr.shape, functools.partial(lambda i, nd: (0,) * nd, nd=o.arr.ndim)))

    def spec_of(o):
        if isinstance(o, Into):
            assert o.col0 % o.width == 0, (name, o.col0, o.width)
            return pl.BlockSpec((tm, o.width), functools.partial(lambda i, cb: (i, cb), cb=o.col0 // o.width))
        return pl.BlockSpec((tm, o[0]), lambda i: (i, 0))

    return _call_with_into(
        body, name, (tp // tm,), in_specs, [o.arr for o in ins], list(outs), spec_of,
        lambda o: jax.ShapeDtypeStruct((tp, o[0]), o[1]),
        [pl.BlockSpec(s, functools.partial(lambda i, nd: (0,) * nd, nd=len(s))) for s in accs],
        [jax.ShapeDtypeStruct(s, F32) for s in accs], "arbitrary")


class Cl:
    def __init__(self, arr, col0=0):
        self.arr, self.col0 = arr, col0


def _cols(fn, name, ins, outs, ncols, tc):
    assert ncols % tc == 0
    n_in, n_out = len(ins), len(outs)
    n_into = sum(isinstance(o, Into) for o in outs)

    def body(*refs):
        res = fn(*[r[...] for r in refs[:n_in]])
        if not isinstance(res, (tuple, list)):
            res = (res,)
        assert len(res) == n_out, (name, len(res))
        out_refs = refs[n_in + n_into:]
        for k in range(n_out):
            out_refs[k][...] = res[k].astype(out_refs[k].dtype)

    in_specs = []
    for o in ins:
        assert o.col0 % tc == 0, (name, o.col0, tc)
        in_specs.append(pl.BlockSpec((o.arr.shape[0], tc), functools.partial(lambda j, off: (0, j + off), off=o.col0 // tc)))

    def spec_of(o):
        if isinstance(o, Into):
            assert o.col0 % tc == 0 and o.width == ncols, (name, o.col0, o.width)
            return pl.BlockSpec((o.arr.shape[0], tc), functools.partial(lambda j, off: (0, j + off), off=o.col0 // tc))
        return pl.BlockSpec((o[0], tc), lambda j: (0, j))

    return _call_with_into(body, name, (ncols // tc,), in_specs, [o.arr for o in ins], list(outs), spec_of,
                           lambda o: jax.ShapeDtypeStruct((o[0], ncols), o[1]), [], [], "parallel")


def _ln_stats(u):
    mu = jnp.mean(u, axis=-1, keepdims=True)
    xc = u - mu
    var = jnp.mean(xc * xc, axis=-1, keepdims=True)
    rstd = lax.rsqrt(var + LN_EPS)
    return xc * rstd, rstd


def _ln_fwd(terms, g, b, name):
    coefs = [c for c, _ in terms]

    def fn(row0, *blk):
        xs, (gg, bb) = blk[:len(coefs)], blk[len(coefs):]
        u = sum(c * x for c, x in zip(coefs, xs))
        xhat, _ = _ln_stats(u)
        return xhat * gg + bb

    d = terms[0][1].shape[1]
    return _rows(fn, name, [Rw(x) for _, x in terms] + [Pm(g.reshape(1, d)), Pm(b.reshape(1, d))], [(d, F32)])[0]


def _ln_bwd(dy_terms, u_terms, g, name):
    dc = [c for c, _ in dy_terms]
    uc = [c for c, _ in u_terms]
    d = u_terms[0][1].shape[1]

    def fn(row0, *blk):
        dys = blk[:len(dc)]
        xs = blk[len(dc):len(dc) + len(uc)]
        gg = blk[-1]
        dy = sum(c * x for c, x in zip(dc, dys))
        u = sum(c * x for c, x in zip(uc, xs))
        xhat, rstd = _ln_stats(u)
        gdy = dy * gg
        m1 = jnp.mean(gdy, axis=-1, keepdims=True)
        m2 = jnp.mean(gdy * xhat, axis=-1, keepdims=True)
        du = rstd * (gdy - m1 - xhat * m2)
        return du, jnp.sum(dy * xhat, axis=0, keepdims=True), jnp.sum(dy, axis=0, keepdims=True)

    ins = [Rw(x) for _, x in dy_terms] + [Rw(x) for _, x in u_terms] + [Pm(g.reshape(1, d))]
    return _rows(fn, name, ins, [(d, F32)], accs=[(1, d), (1, d)])


def _loss_head(y, tgt, t_real, name):
    d = y.shape[1]

    def fn(row0, yb, tb):
        rows = _row_ids(yb.shape, row0)
        live = (rows >= N_META) & (rows < t_real)
        diff = jnp.where(live, yb - tb, 0.0)
        return diff * (1.0 / d), jnp.sum(diff * diff, axis=0, keepdims=True) * (0.5 / d)

    return _rows(fn, name, [Rw(y), Rw(tgt)], [(d, F32)], accs=[(1, d)])


def _rms(x, g):
    r = lax.rsqrt(jnp.mean(x * x, axis=-1, keepdims=True) + RMS_EPS)
    return x * r * g


def _rms_bwd(dy, x, g):
    r = lax.rsqrt(jnp.mean(x * x, axis=-1, keepdims=True) + RMS_EPS)
    gdy = dy * g
    dx = r * gdy - x * (r * r * r) * jnp.mean(gdy * x, axis=-1, keepdims=True)
    return dx, jnp.sum(dy * x * r, axis=0, keepdims=True)


def _mla_norms(proj, qn, kvn, name):
    def fn(row0, cq, ckv, g1, g2):
        return _rms(cq, g1), _rms(ckv, g2)

    return _rows(fn, name, [Rw(proj, Q_RANK, C_CQ // Q_RANK), Rw(proj, KV_RANK, C_CKV // KV_RANK),
                            Pm(qn.reshape(1, Q_RANK)), Pm(kvn.reshape(1, KV_RANK))],
                 [(Q_RANK, BF16), (KV_RANK, BF16)])


def _mla_norms_bwd(dcqn, dckvn, dkrp, proj, qn, kvn, dproj, name):
    def fn(row0, d1, d2, dkr, cq, ckv, g1, g2):
        dx1, dg1 = _rms_bwd(d1, cq, g1)
        dx2, dg2 = _rms_bwd(d2, ckv, g2)
        return jnp.concatenate([dx1, dx2, dkr], axis=1), dg1, dg2

    return _rows(fn, name, [Rw(dcqn), Rw(dckvn), Rw(dkrp), Rw(proj, Q_RANK, C_CQ // Q_RANK), Rw(proj, KV_RANK, C_CKV // KV_RANK),
                            Pm(qn.reshape(1, Q_RANK)), Pm(kvn.reshape(1, KV_RANK))],
                 [Into(dproj, C_CQ, PROJ_COLS - C_CQ)], accs=[(1, Q_RANK), (1, KV_RANK)])


def _fold_rope(z):
    return z + pltpu.roll(z, QK_ROPE, 1)


def _mla_pack(qext, kv, proj, cs, name):
    tp = qext.shape[0]
    tm = _tile(tp, 384)
    hw, nope_all = 2 * LANE, HEADS * QK_NOPE

    def body(q_ref, kv_ref, kr_ref, cs_ref, qo_ref, ko_ref, vo_ref):
        cs_ = cs_ref[...]
        low = lax.broadcasted_iota(jnp.int32, cs_.shape, 1) < QK_ROPE
        kr = _fold_rope(kr_ref[...] * cs_).astype(BF16)
        for h in range(HEADS):
            qr = jnp.where(low, _fold_rope(q_ref[:, h * hw + QK_NOPE:(h + 1) * hw] * cs_), 0.0)
            qo_ref[:, h * hw:h * hw + QK_NOPE] = (q_ref[:, h * hw:h * hw + QK_NOPE] * ATT_SCALE).astype(BF16)
            qo_ref[:, h * hw + QK_NOPE:(h + 1) * hw] = (qr * ATT_SCALE).astype(BF16)
            ko_ref[:, h * hw:h * hw + QK_NOPE] = kv_ref[:, h * QK_NOPE:(h + 1) * QK_NOPE].astype(BF16)
            ko_ref[:, h * hw + QK_NOPE:(h + 1) * hw] = kr
        vo_ref[...] = kv_ref[:, nope_all:].astype(BF16)

    row = lambda w: pl.BlockSpec((tm, w), lambda i: (i, 0))
    return pl.pallas_call(
        body, name=name, grid=(tp // tm,),
        in_specs=[row(HEADS * hw), row(2 * nope_all), pl.BlockSpec((tm, LANE), lambda i: (i, C_KRP // LANE)), row(LANE)],
        out_specs=[row(HEADS * hw), row(HEADS * hw), row(nope_all)],
        out_shape=[jax.ShapeDtypeStruct((tp, HEADS * hw), BF16),
                   jax.ShapeDtypeStruct((tp, HEADS * hw), BF16),
                   jax.ShapeDtypeStruct((tp, nope_all), BF16)],
        compiler_params=_cparams("parallel"),
    )(qext, kv, proj, cs)


def _mla_unpack(dk, dv, cs, name):
    tp = dk.shape[0]
    tm = _tile(tp, 384)
    hw, nope_all = 2 * LANE, HEADS * QK_NOPE

    def body(dk_ref, dv_ref, cs_ref, dkv_ref, dkr_ref):
        cs_ = cs_ref[...]
        low = lax.broadcasted_iota(jnp.int32, cs_.shape, 1) < QK_ROPE
        dkr = None
        for h in range(HEADS):
            dkv_ref[:, h * QK_NOPE:(h + 1) * QK_NOPE] = dk_ref[:, h * hw:h * hw + QK_NOPE].astype(BF16)
            part = jnp.where(low, dk_ref[:, h * hw + QK_NOPE:(h + 1) * hw], 0.0)
            dkr = part if h == 0 else dkr + part
        dkv_ref[:, nope_all:] = dv_ref[...].astype(BF16)
        dkr_ref[...] = _fold_rope(dkr) * cs_

    row = lambda w: pl.BlockSpec((tm, w), lambda i: (i, 0))
    return pl.pallas_call(
        body, name=name, grid=(tp // tm,),
        in_specs=[row(HEADS * hw), row(nope_all), row(LANE)],
        out_specs=[row(2 * nope_all), row(LANE)],
        out_shape=[jax.ShapeDtypeStruct((tp, 2 * nope_all), BF16), jax.ShapeDtypeStruct((tp, LANE), F32)],
        compiler_params=_cparams("parallel"),
    )(dk, dv, cs)


def _attn_fwd(q, k, v, t_real, name):
    tp = q.shape[0]
    tq = _tile(tp, 1408)
    tkc = _tile(tp, 704, 64)
    nkc = -(-t_real // tkc)

    def body(q_ref, k_ref, v_ref, o_ref, lse_ref):
        qb = q_ref[...]
        m = l = acc = None
        for c in range(nkc):
            s = lax.dot_general(qb, k_ref[c * tkc:(c + 1) * tkc, :], (((1,), (1,)), ((), ())), preferred_element_type=F32)
            if (c + 1) * tkc > t_real:
                cols = lax.broadcasted_iota(jnp.int32, s.shape, 1) + c * tkc
                s = jnp.where(cols < t_real, s, NEG_BIG)
            mc = jnp.max(s, axis=-1, keepdims=True)
            m_new = mc if c == 0 else jnp.maximum(m, mc)
            p = jnp.exp2(s - m_new)
            lc = jnp.sum(p, axis=-1, keepdims=True)
            pv = jnp.dot(p.astype(BF16), v_ref[c * tkc:(c + 1) * tkc, :], preferred_element_type=F32)
            if c == 0:
                l, acc = lc, pv
            else:
                alpha = jnp.exp2(m - m_new)
                l, acc = alpha * l + lc, alpha * acc + pv
            m = m_new
        o_ref[...] = acc / l
        lse_ref[...] = m + jnp.log2(l)

    return pl.pallas_call(
        body, name=name, grid=(HEADS, tp // tq),
        in_specs=[pl.BlockSpec((tq, 2 * LANE), lambda h, i: (i, h)),
                  pl.BlockSpec((tp, 2 * LANE), lambda h, i: (0, h)),
                  pl.BlockSpec((tp, LANE), lambda h, i: (0, h))],
        out_specs=[pl.BlockSpec((tq, LANE), lambda h, i: (i, h)),
                   pl.BlockSpec((None, tq, 1), lambda h, i: (h, i, 0))],
        out_shape=[jax.ShapeDtypeStruct((tp, HEADS * LANE), F32),
                   jax.ShapeDtypeStruct((HEADS, tp, 1), F32)],
        compiler_params=_cparams("parallel", "parallel"),
    )(q, k, v)


def _attn_bwd(q, k, v, do, o, lse, cs, t_real, name):
    tp = q.shape[0]
    tq = _tile(tp, 1408)
    tkc = _tile(tp, 704, 64)
    nkc = -(-t_real // tkc)

    def body(q_ref, k_ref, v_ref, do_ref, o_ref, lse_ref, cs_ref, dq_ref, dk_ref, dv_ref):
        i = pl.program_id(1)

        @pl.when(i == 0)
        def _():
            dk_ref[...] = jnp.zeros_like(dk_ref)
            dv_ref[...] = jnp.zeros_like(dv_ref)

        qb = q_ref[...]
        dob = do_ref[...]
        dob16 = dob.astype(BF16)
        dol2 = (dob * LN2).astype(BF16)
        delta = jnp.sum(dob * o_ref[...], axis=-1, keepdims=True) * LN2
        lse = lse_ref[...]
        dq = None
        for c in range(nkc):
            ks = slice(c * tkc, (c + 1) * tkc)
            kb = k_ref[ks, :]
            s = lax.dot_general(qb, kb, (((1,), (1,)), ((), ())), preferred_element_type=F32)
            p = jnp.exp2(s - lse)
            if (c + 1) * tkc > t_real:
                cols = lax.broadcasted_iota(jnp.int32, s.shape, 1) + c * tkc
                p = jnp.where(cols < t_real, p, 0.0)
            dp = lax.dot_general(dol2, v_ref[ks, :], (((1,), (1,)), ((), ())), preferred_element_type=F32)
            ds = (p * (dp - delta)).astype(BF16)
            dqc = jnp.dot(ds, kb, preferred_element_type=F32)
            dq = dqc if c == 0 else dq + dqc
            dk_ref[ks, :] += lax.dot_general(ds, qb, (((0,), (0,)), ((), ())), preferred_element_type=F32)
            dv_ref[ks, :] += lax.dot_general(p.astype(BF16), dob16, (((0,), (0,)), ((), ())), preferred_element_type=F32)
        cs_ = cs_ref[...]
        low = lax.broadcasted_iota(jnp.int32, cs_.shape, 1) < QK_ROPE
        dq = dq * ATT_SCALE
        dq_ref[:, :QK_NOPE] = dq[:, :QK_NOPE].astype(BF16)
        dq_ref[:, QK_NOPE:] = (_fold_rope(jnp.where(low, dq[:, QK_NOPE:], 0.0)) * cs_).astype(BF16)

    return pl.pallas_call(
        body, name=name, grid=(HEADS, tp // tq),
        in_specs=[pl.BlockSpec((tq, 2 * LANE), lambda h, i: (i, h)),
                  pl.BlockSpec((tp, 2 * LANE), lambda h, i: (0, h)),
                  pl.BlockSpec((tp, LANE), lambda h, i: (0, h)),
                  pl.BlockSpec((tq, LANE), lambda h, i: (i, h)),
                  pl.BlockSpec((tq, LANE), lambda h, i: (i, h)),
                  pl.BlockSpec((None, tq, 1), lambda h, i: (h, i, 0)),
                  pl.BlockSpec((tq, LANE), lambda h, i: (i, 0))],
        out_specs=[pl.BlockSpec((tq, 2 * LANE), lambda h, i: (i, h)),
                   pl.BlockSpec((tp, 2 * LANE), lambda h, i: (0, h)),
                   pl.BlockSpec((tp, LANE), lambda h, i: (0, h))],
        out_shape=[jax.ShapeDtypeStruct((tp, HEADS * 2 * LANE), BF16),
                   jax.ShapeDtypeStruct((tp, HEADS * 2 * LANE), F32),
                   jax.ShapeDtypeStruct((tp, HEADS * LANE), F32)],
        compiler_params=_cparams("parallel", "arbitrary"),
    )(q, k, v, do, o, lse, cs)


def _shift_rows(x, s):
    tp = x.shape[0]
    return x if s % tp == 0 else pltpu.roll(x, s % tp, 0)


def _conv_fwd_val(xm, w, b, pad_left):
    acc = b + w[0:1, :] * _shift_rows(xm, pad_left)
    for k in range(1, w.shape[0]):
        acc = acc + w[k:k + 1, :] * _shift_rows(xm, pad_left - k)
    return acc


def _conv_bwd_val(dy, xm, w, pad_left, live):
    kk = w.shape[0]
    dx = w[0:1, :] * _shift_rows(dy, -pad_left)
    dws = [jnp.sum(dy * _shift_rows(xm, pad_left), axis=0, keepdims=True)]
    for k in range(1, kk):
        dx = dx + w[k:k + 1, :] * _shift_rows(dy, k - pad_left)
        dws.append(jnp.sum(dy * _shift_rows(xm, pad_left - k), axis=0, keepdims=True))
    return jnp.where(live, dx, 0.0), jnp.concatenate(dws, axis=0), jnp.sum(dy, axis=0, keepdims=True)


def _lru_conv_fwd(proj, w, b, t_real, name):
    def fn(x, ww, bb):
        xm = jnp.where(_row_ids(x.shape) < t_real, x, 0.0)
        return _conv_fwd_val(xm, ww, bb, 2)

    return _cols(fn, name, [Cl(proj, C_LRU_X), Cl(w), Cl(b.reshape(1, -1))], [(proj.shape[0], F32)], D_MODEL, 128)[0]


def _lru_conv_bwd(dxc, proj, w, dproj, t_real, name):
    def fn(dy, x, ww):
        live = _row_ids(x.shape) < t_real
        xm = jnp.where(live, x, 0.0)
        dym = jnp.where(live, dy, 0.0)
        return _conv_bwd_val(dym, xm, ww, 2, live)

    return _cols(fn, name, [Cl(dxc), Cl(proj, C_LRU_X), Cl(w)],
                 [Into(dproj, C_LRU_X, D_MODEL), (w.shape[0], F32), (1, F32)], D_MODEL, 128)


def _ffn_conv_act(up, w, b, t_real, name):
    def fn(g, v, wg, wv, bg, bv):
        live = _row_ids(g.shape) < t_real
        gc = _conv_fwd_val(jnp.where(live, g, 0.0), wg, bg, 1)
        vc = _conv_fwd_val(jnp.where(live, v, 0.0), wv, bv, 1)
        return _gelu(gc) * vc

    b2 = b.reshape(1, -1)
    return _cols(fn, name, [Cl(up), Cl(up, D_FF), Cl(w), Cl(w, D_FF), Cl(b2), Cl(b2, D_FF)],
                 [(up.shape[0], BF16)], D_FF, 128)[0]


def _ffn_conv_act_bwd(dm, up, w, b, t_real, name):
    tp, kk = up.shape[0], w.shape[0]
    nb = D_FF // LANE
    assert nb >= 2

    def body(dm_ref, g_ref, v_ref, wg_ref, wv_ref, bg_ref, bv_ref, dup_ref, dwg_ref, dwv_ref, dbg_ref, dbv_ref, stage, sems):
        j = pl.program_id(0)
        slot = j % 2

        def copies(step, sl):
            return [pltpu.make_async_copy(stage.at[sl, half],
                                          dup_ref.at[:, pl.ds(pl.multiple_of(half * D_FF + step * LANE, LANE), LANE)],
                                          sems.at[sl, half]) for half in range(2)]

        @pl.when(j >= 2)
        def _():
            for cp in copies(j - 2, slot):
                cp.wait()

        live = _row_ids((tp, LANE)) < t_real
        gm, vm = jnp.where(live, g_ref[...], 0.0), jnp.where(live, v_ref[...], 0.0)
        gc = _conv_fwd_val(gm, wg_ref[...], bg_ref[...], 1)
        vc = _conv_fwd_val(vm, wv_ref[...], bv_ref[...], 1)
        act, dact = _gelu_and_grad(gc)
        dmm = jnp.where(live, dm_ref[...], 0.0)
        dgx, dwg_ref[...], dbg_ref[...] = _conv_bwd_val(dmm * vc * dact, gm, wg_ref[...], 1, live)
        dvx, dwv_ref[...], dbv_ref[...] = _conv_bwd_val(dmm * act, vm, wv_ref[...], 1, live)
        stage[slot, 0] = dgx.astype(BF16)
        stage[slot, 1] = dvx.astype(BF16)
        for cp in copies(j, slot):
            cp.start()

        @pl.when(j == nb - 1)
        def _():
            for cp in copies(j - 1, 1 - slot) + copies(j, slot):
                cp.wait()

    b2 = b.reshape(1, -1)
    col = lambda rows, off: pl.BlockSpec((rows, LANE), functools.partial(lambda j, o: (0, j + o), o=off))
    return pl.pallas_call(
        body, name=name, grid=(nb,),
        in_specs=[col(tp, 0), col(tp, 0), col(tp, nb), col(kk, 0), col(kk, nb), col(1, 0), col(1, nb)],
        out_specs=[pl.BlockSpec(memory_space=pl.ANY), col(kk, 0), col(kk, 0), col(1, 0), col(1, 0)],
        out_shape=[jax.ShapeDtypeStruct((tp, 2 * D_FF), BF16), jax.ShapeDtypeStruct((kk, D_FF), F32),
                   jax.ShapeDtypeStruct((kk, D_FF), F32), jax.ShapeDtypeStruct((1, D_FF), F32),
                   jax.ShapeDtypeStruct((1, D_FF), F32)],
        scratch_shapes=[pltpu.VMEM((2, 2, tp, LANE), BF16), pltpu.SemaphoreType.DMA((2, 2))],
        compiler_params=_cparams("arbitrary"),
    )(dm, up, up, w, w, b2, b2)


def _lru_gates_fwd(xc, wg, b4, lam, t_real, name):
    tp = xc.shape[0]
    tm = _tile(tp, 1408)

    def body(x_ref, w_ref, b_ref, lam_ref, r0_ref, r1_ref, i0_ref, i1_ref, a0_ref, a1_ref, u0_ref, u1_ref):
        x = x_ref[...]
        xb = x.astype(BF16)
        live = _row_ids(x.shape, pl.program_id(1) * tm) < t_real
        bb = b_ref[...]
        sp = _softplus_neg(lam_ref[...])
        gate = [_sigmoid(jnp.dot(xb, w_ref[k], preferred_element_type=F32) + bb[k:k + 1, :]) for k in range(4)]
        for d, (r_ref, i_ref, a_ref, u_ref) in enumerate(((r0_ref, i0_ref, a0_ref, u0_ref), (r1_ref, i1_ref, a1_ref, u1_ref))):
            r, ig = gate[d], gate[2 + d]
            a = jnp.exp(-LRU_C * r * sp[d:d + 1, :])
            r_ref[...] = r
            i_ref[...] = ig
            a_ref[...] = a
            u_ref[...] = jnp.where(live, jnp.sqrt(1.0 - a * a) * (ig * x), 0.0)

    blk = pl.BlockSpec((tm, LANE), lambda g, i: (i, g))
    return pl.pallas_call(
        body, name=name, grid=(LRU_BLOCKS, tp // tm),
        in_specs=[blk, pl.BlockSpec((None, 4, LANE, LANE), lambda g, i: (g, 0, 0, 0)),
                  pl.BlockSpec((4, LANE), lambda g, i: (0, g)), pl.BlockSpec((2, LANE), lambda g, i: (0, g))],
        out_specs=[blk] * 8,
        out_shape=[jax.ShapeDtypeStruct((tp, D_MODEL), F32)] * 8,
        compiler_params=_cparams("parallel", "parallel"),
    )(xc, wg, b4, lam)


def _lru_gates_bwd(l0, l1, da0, da1, r0, r1, i0, i1, a0, a1, xc, wg, lam, t_real, name):
    tp = xc.shape[0]
    tm = _tile(tp, 1408)

    rc = 32
    assert tm % rc == 0

    def fold(v):
        out = v[0:SUBLANE]
        for t in range(1, rc // SUBLANE):
            out = out + v[t * SUBLANE:(t + 1) * SUBLANE]
        return out

    def body(l0_ref, l1_ref, da0_ref, da1_ref, r0_ref, r1_ref, i0_ref, i1_ref, a0_ref, a1_ref, x_ref, w_ref, lam_ref,
             dx_ref, dw_ref, db_ref, dlam_ref, pre_s, dxp_s):
        i = pl.program_id(1)
        lam_ = lam_ref[...]
        sp = _softplus_neg(lam_)
        dsp_dlam = -_sigmoid(-lam_)

        def chunk(c, sums):
            r0 = pl.multiple_of(c * rc, rc)
            rows = pl.ds(r0, rc)
            x = x_ref[rows, :]
            live = _row_ids((rc, LANE), i * tm + r0) < t_real
            dxp = jnp.zeros_like(x)
            sums = list(sums)
            for d, (l_ref, da_ref, r_ref, i_ref, a_ref) in enumerate(((l0_ref, da0_ref, r0_ref, i0_ref, a0_ref),
                                                                      (l1_ref, da1_ref, r1_ref, i1_ref, a1_ref))):
                r, ig, a = r_ref[rows, :], i_ref[rows, :], a_ref[rows, :]
                du = jnp.where(live, l_ref[rows, :], 0.0)
                a2 = a * a
                rs = lax.rsqrt(1.0 - a2)
                dv = du * ((1.0 - a2) * rs)
                ds = du * (ig * x)
                dla = jnp.where(live, da_ref[rows, :] * a - ds * (a2 * rs), 0.0)
                dr = dla * (-LRU_C) * sp[d:d + 1, :]
                p_r = dr * r * (1.0 - r)
                p_i = dv * x * ig * (1.0 - ig)
                pre_s[d, rows, :] = p_r.astype(BF16)
                pre_s[2 + d, rows, :] = p_i.astype(BF16)
                dxp = dxp + dv * ig
                sums[d] = sums[d] + fold(p_r)
                sums[2 + d] = sums[2 + d] + fold(p_i)
                sums[4 + d] = sums[4 + d] + fold(dla * (-LRU_C) * r)
            dxp_s[rows, :] = dxp
            return tuple(sums)

        zero = jnp.zeros((SUBLANE, LANE), F32)
        sums = lax.fori_loop(0, tm // rc, chunk, (zero,) * 6)

        @pl.when(i == 0)
        def _():
            dw_ref[...] = jnp.zeros_like(dw_ref)
            db_ref[...] = jnp.zeros_like(db_ref)
            dlam_ref[...] = jnp.zeros_like(dlam_ref)

        xb = x_ref[...].astype(BF16)
        dx = dxp_s[...]
        for k in range(4):
            pk = pre_s[k]
            dx = dx + lax.dot_general(pk, w_ref[k], (((1,), (1,)), ((), ())), preferred_element_type=F32)
            dw_ref[k] += lax.dot_general(xb, pk, (((0,), (0,)), ((), ())), preferred_element_type=F32)
        db_ref[...] += jnp.concatenate([jnp.sum(sums[k], axis=0, keepdims=True) for k in range(4)], axis=0)
        dlam_ref[...] += jnp.concatenate([jnp.sum(sums[4 + d], axis=0, keepdims=True) * dsp_dlam[d:d + 1, :] for d in range(2)], axis=0)
        dx_ref[...] = dx

    blk = pl.BlockSpec((tm, LANE), lambda g, i: (i, g))
    return pl.pallas_call(
        body, name=name, grid=(LRU_BLOCKS, tp // tm),
        in_specs=[blk] * 11 + [pl.BlockSpec((None, 4, LANE, LANE), lambda g, i: (g, 0, 0, 0)),
                               pl.BlockSpec((2, LANE), lambda g, i: (0, g))],
        out_specs=[blk, pl.BlockSpec((None, 4, LANE, LANE), lambda g, i: (g, 0, 0, 0)),
                   pl.BlockSpec((4, LANE), lambda g, i: (0, g)), pl.BlockSpec((2, LANE), lambda g, i: (0, g))],
        out_shape=[jax.ShapeDtypeStruct((tp, D_MODEL), F32), jax.ShapeDtypeStruct((LRU_BLOCKS, 4, LANE, LANE), F32),
                   jax.ShapeDtypeStruct((4, D_MODEL), F32), jax.ShapeDtypeStruct((2, D_MODEL), F32)],
        scratch_shapes=[pltpu.VMEM((4, tm, LANE), BF16), pltpu.VMEM((tm, LANE), F32)],
        compiler_params=_cparams("parallel", "arbitrary"),
    )(l0, l1, da0, da1, r0, r1, i0, i1, a0, a1, xc, wg, lam)


SCAN_UNROLL = 4


def _loop_tiles(nt, step, carry):
    assert nt % SCAN_UNROLL == 0

    def trip(tt, c):
        for u in range(SCAN_UNROLL):
            c = step(tt * SCAN_UNROLL + u, c)
        return c

    return lax.fori_loop(0, nt // SCAN_UNROLL, trip, carry)


def _tile_scan(a, u, reverse):
    rows = lax.broadcasted_iota(jnp.int32, a.shape, 0)
    for s in (1, 2, 4):
        if reverse:
            keep = rows < SUBLANE - s
            a_sh, u_sh = pltpu.roll(a, SUBLANE - s, 0), pltpu.roll(u, SUBLANE - s, 0)
        else:
            keep = rows >= s
            a_sh, u_sh = pltpu.roll(a, s, 0), pltpu.roll(u, s, 0)
        u = u + a * jnp.where(keep, u_sh, 0.0)
        a = a * jnp.where(keep, a_sh, 1.0)
    return a, u


def _scan_fwd(a0, u0, a1, u1, proj, name):
    tp, d = a0.shape
    tc = 128
    nt = tp // SUBLANE

    def body(a0_ref, u0_ref, a1_ref, u1_ref, lg_ref, h0_ref, h1_ref, gh_ref):
        def step(t, carry):
            c0, c1 = carry
            f = pl.multiple_of(t * SUBLANE, SUBLANE)
            b = pl.multiple_of((nt - 1 - t) * SUBLANE, SUBLANE)
            pa, pu = _tile_scan(a0_ref[pl.ds(f, SUBLANE), :], u0_ref[pl.ds(f, SUBLANE), :], False)
            h = pu + pa * c0
            h0_ref[pl.ds(f, SUBLANE), :] = h
            c0 = h[SUBLANE - 1:SUBLANE, :]
            pa, pu = _tile_scan(a1_ref[pl.ds(b, SUBLANE), :], u1_ref[pl.ds(b, SUBLANE), :], True)
            h = pu + pa * c1
            h1_ref[pl.ds(b, SUBLANE), :] = h
            c1 = h[0:1, :]
            return c0, c1

        z = jnp.zeros((1, tc), F32)
        _loop_tiles(nt, step, (z, z))
        gh_ref[...] = (_gelu(lg_ref[...]) * (h0_ref[...] + h1_ref[...])).astype(BF16)

    blk = pl.BlockSpec((tp, tc), lambda j: (0, j))
    return pl.pallas_call(
        body, name=name, grid=(d // tc,),
        in_specs=[blk] * 4 + [pl.BlockSpec((tp, tc), lambda j: (0, j + C_LRU_G // tc))], out_specs=[blk] * 3,
        out_shape=[jax.ShapeDtypeStruct((tp, d), F32)] * 2 + [jax.ShapeDtypeStruct((tp, d), BF16)],
        compiler_params=_cparams("parallel"),
    )(a0, u0, a1, u1, proj)


def _scan_bwd(dh, a0, a1, h0, h1, name):
    tp, d = dh.shape
    tc = 128
    nt = tp // SUBLANE

    def body(dh_ref, a0_ref, a1_ref, h0_ref, h1_ref, l0_ref, l1_ref, da0_ref, da1_ref):
        rows8 = lax.broadcasted_iota(jnp.int32, (SUBLANE, tc), 0)

        def step(t, carry):
            c0, c1 = carry
            b = pl.multiple_of((nt - 1 - t) * SUBLANE, SUBLANE)
            f = pl.multiple_of(t * SUBLANE, SUBLANE)
            a = a0_ref[pl.ds(b, SUBLANE), :]
            a_next = jnp.where(rows8 < SUBLANE - 1, pltpu.roll(a, SUBLANE - 1, 0), 1.0)
            pa, pu = _tile_scan(a_next, dh_ref[pl.ds(b, SUBLANE), :], True)
            lam = pu + pa * c0
            l0_ref[pl.ds(b, SUBLANE), :] = lam
            c0 = a[0:1, :] * lam[0:1, :]
            a = a1_ref[pl.ds(f, SUBLANE), :]
            a_prev = jnp.where(rows8 >= 1, pltpu.roll(a, 1, 0), 1.0)
            pa, pu = _tile_scan(a_prev, dh_ref[pl.ds(f, SUBLANE), :], False)
            lam = pu + pa * c1
            l1_ref[pl.ds(f, SUBLANE), :] = lam
            c1 = a[SUBLANE - 1:SUBLANE, :] * lam[SUBLANE - 1:SUBLANE, :]
            return c0, c1

        z = jnp.zeros((1, tc), F32)
        _loop_tiles(nt, step, (z, z))
        rows = lax.broadcasted_iota(jnp.int32, (tp, tc), 0)
        da0_ref[...] = l0_ref[...] * jnp.where(rows >= 1, pltpu.roll(h0_ref[...], 1, 0), 0.0)
        da1_ref[...] = l1_ref[...] * jnp.where(rows < tp - 1, pltpu.roll(h1_ref[...], tp - 1, 0), 0.0)

    blk = pl.BlockSpec((tp, tc), lambda j: (0, j))
    return pl.pallas_call(
        body, name=name, grid=(d // tc,), in_specs=[blk] * 5, out_specs=[blk] * 4,
        out_shape=[jax.ShapeDtypeStruct((tp, d), F32)] * 4,
        compiler_params=_cparams("parallel"),
    )(dh, a0, a1, h0, h1)


def _gated_h_bwd(dgh, proj, h0, h1, dproj, name):
    def fn(row0, dg, lg, x0, x1):
        act, dact = _gelu_and_grad(lg)
        return dg * (x0 + x1) * dact, dg * act

    return _rows(fn, name, [Rw(dgh), Rw(proj, D_MODEL, C_LRU_G // D_MODEL), Rw(h0), Rw(h1)],
                 [Into(dproj, C_LRU_G, D_MODEL), (D_MODEL, F32)])


def _mix(proj, y_mla, y_lru, name):
    def fn(row0, gm, gl, ym, yl):
        return _sigmoid(gm) * ym + _sigmoid(gl) * yl

    return _rows(fn, name, [Rw(proj, D_MODEL, C_G_MLA // D_MODEL), Rw(proj, D_MODEL, C_G_LRU // D_MODEL), Rw(y_mla), Rw(y_lru)],
                 [(D_MODEL, BF16)])[0]


def _mix_bwd(dz, proj, y_mla, y_lru, dproj, name):
    def fn(row0, dzb, gm, gl, ym, yl):
        sm, sl = _sigmoid(gm), _sigmoid(gl)
        dg = jnp.concatenate([dzb * ym * sm * (1.0 - sm), dzb * yl * sl * (1.0 - sl)], axis=1)
        return dzb * sm, dzb * sl, dg

    return _rows(fn, name, [Rw(dz), Rw(proj, D_MODEL, C_G_MLA // D_MODEL), Rw(proj, D_MODEL, C_G_LRU // D_MODEL),
                            Rw(y_mla), Rw(y_lru)], [(D_MODEL, BF16), (D_MODEL, BF16), Into(dproj, C_G_MLA, 2 * D_MODEL)])


def _layer_fwd(h, w_in, more_weights, cs, t_real, tag):
    proj = _matmul(h, w_in, tag + "proj", tb=True)
    w = dict(more_weights(0, proj), w_in=w_in)
    cqn, ckvn = _mla_norms(proj, w['q_norm'], w['kv_norm'], tag + "mla_norms")
    qext = _matmul(cqn, w['w_q'], tag + "q_up")
    kv = _matmul(ckvn, w['w_kv'], tag + "kv_up")
    qc, kc, vb = _mla_pack(qext, kv, proj, cs, tag + "mla_pack")
    o, lse = _attn_fwd(qc, kc, vb, t_real, tag + "attn_fwd")
    w.update(more_weights(1, o))
    y_mla = _matmul(o, w['w_o_mla'], tag + "o_mla")
    xc = _lru_conv_fwd(proj, w['lru_conv_w'], w['lru_conv_b'], t_real, tag + "lru_conv")
    r0, r1, i0, i1, a0, a1, u0, u1 = _lru_gates_fwd(xc, w['w_g'], w['b4'], w['lru_lambda'], t_real, tag + "lru_gates")
    h0, h1, gh = _scan_fwd(a0, u0, a1, u1, proj, tag + "lru_scan")
    y_lru = _matmul(gh, w['w_o_lru'], tag + "o_lru")
    z = _mix(proj, y_mla, y_lru, tag + "mix")
    zo = _matmul(z, w['w_out'], tag + "w_out")
    hm = _ln_fwd([(DN_ALPHA, h), (1.0, zo)], w['ln1_g'], w['ln1_b'], tag + "ln1")
    w.update(more_weights(2, hm))
    up = _matmul(hm, w['w_up'], tag + "w_up", tb=True)
    m = _ffn_conv_act(up, w['ffn_conv_w'], w['ffn_conv_b'], t_real, tag + "ffn_conv")
    f = _matmul(m, w['w_down'], tag + "w_down", tk_cap=1408)
    out = _ln_fwd([(DN_ALPHA, hm), (1.0, f)], w['ln2_g'], w['ln2_b'], tag + "ln2")
    saved = dict(w=w, h=h, proj=proj, cqn=cqn, ckvn=ckvn, qc=qc, kc=kc, vb=vb, o=o, lse=lse, y_mla=y_mla, xc=xc,
                 r0=r0, r1=r1, i0=i0, i1=i1, a0=a0, a1=a1, h0=h0, h1=h1, gh=gh, y_lru=y_lru, z=z, zo=zo, hm=hm,
                 up=up, m=m, f=f)
    return out, saved


DW_MATMUL = dict(ta=True, out_dtype=BF16, tn_cap=2304, tk_cap=1408)


def _after(a, tok):
    return a if tok is None else a + tok.astype(a.dtype)


def _layer_bwd(dout_terms, s, cs, t_real, tag, emit, tok):
    w = s['w']
    g = {}
    du2, dg2, db2 = _ln_bwd(dout_terms, [(DN_ALPHA, s['hm']), (1.0, s['f'])], _after(w['ln2_g'], tok), tag + "ln2_bwd")
    g['ln2_g'], g['ln2_b'] = dg2, db2
    dm = _matmul(du2, w['w_down'], tag + "w_down_dx", tb=True)
    g['w_down'] = _matmul(s['m'], du2, tag + "w_down_dw", **DW_MATMUL)
    dup, dwg_, dwv_, dbg_, dbv_ = _ffn_conv_act_bwd(dm, s['up'], w['ffn_conv_w'], w['ffn_conv_b'], t_real, tag + "ffn_conv_bwd")
    g['ffn_conv_w'] = jnp.concatenate([dwg_, dwv_], axis=1)
    g['ffn_conv_b'] = jnp.concatenate([dbg_, dbv_], axis=1)
    dhm_mm = _matmul(dup, w['w_up'], tag + "w_up_dx", tk_cap=1408)
    g['w_up'] = _matmul(s['hm'], dup, tag + "w_up_dw", **DW_MATMUL)
    tok = emit('ffn', g)
    g = {}
    du1, dg1, db1 = _ln_bwd([(DN_ALPHA, du2), (1.0, dhm_mm)], [(DN_ALPHA, s['h']), (1.0, s['zo'])],
                            _after(w['ln1_g'], tok), tag + "ln1_bwd")
    g['ln1_g'], g['ln1_b'] = dg1, db1
    dz = _matmul(du1, w['w_out'], tag + "w_out_dx", tb=True)
    g['w_out'] = _matmul(s['z'], du1, tag + "w_out_dw", **DW_MATMUL)
    dproj = lax.empty(s['proj'].shape, BF16)
    dy_mla, dy_lru, dproj = _mix_bwd(dz, s['proj'], s['y_mla'], s['y_lru'], dproj, tag + "mix_bwd")
    do = _matmul(dy_mla, w['w_o_mla'], tag + "o_mla_dx", tb=True)
    g['w_o_mla'] = _matmul(s['o'], dy_mla, tag + "o_mla_dw", **DW_MATMUL)
    dqext, dkc, dv = _attn_bwd(s['qc'], s['kc'], s['vb'], do, s['o'], s['lse'], cs, t_real, tag + "attn_bwd")
    dkv, dkrp = _mla_unpack(dkc, dv, cs, tag + "mla_unpack")
    dcqn = _matmul(dqext, w['w_q'], tag + "q_up_dx", tb=True)
    g['w_q'] = _matmul(s['cqn'], dqext, tag + "q_up_dw", **DW_MATMUL)
    dckvn = _matmul(dkv, w['w_kv'], tag + "kv_up_dx", tb=True)
    g['w_kv'] = _matmul(s['ckvn'], dkv, tag + "kv_up_dw", **DW_MATMUL)
    dgh = _matmul(dy_lru, w['w_o_lru'], tag + "o_lru_dx", tb=True)
    g['w_o_lru'] = _matmul(s['gh'], dy_lru, tag + "o_lru_dw", **DW_MATMUL)
    dproj, dhs = _gated_h_bwd(dgh, s['proj'], s['h0'], s['h1'], dproj, tag + "lru_gate_out_bwd")
    l0, l1, da0, da1 = _scan_bwd(dhs, s['a0'], s['a1'], s['h0'], s['h1'], tag + "lru_scan_bwd")
    dxc, g['w_g'], g['b4'], g['lru_lambda'] = _lru_gates_bwd(
        l0, l1, da0, da1, s['r0'], s['r1'], s['i0'], s['i1'], s['a0'], s['a1'], s['xc'], w['w_g'], w['lru_lambda'],
        t_real, tag + "lru_gates_bwd")
    dproj, g['lru_conv_w'], g['lru_conv_b'] = _lru_conv_bwd(dxc, s['proj'], w['lru_conv_w'], dproj, t_real, tag + "lru_conv_bwd")
    tok = emit('mid', g)
    dproj, dqn, dkvn = _mla_norms_bwd(dcqn, dckvn, _after(dkrp, tok), s['proj'], w['q_norm'], w['kv_norm'], dproj,
                                      tag + "mla_norms_bwd")
    tok = emit('in', {'w_in': _matmul(s['h'], dproj, tag + "proj_dw", **DW_MATMUL), 'q_norm': dqn, 'kv_norm': dkvn})
    dh_mm = _matmul(dproj, w['w_in'], tag + "proj_dx", tk_cap=1536)
    return [(DN_ALPHA, du1), (1.0, dh_mm)], tok


def _swap_halves(a, axis=-1):
    h1, h2 = jnp.split(a, 2, axis=axis)
    return jnp.concatenate([h2, h1], axis=axis)


def _w_in_kernel(w_in_t):
    cq, ckv, kr, lg, lx, gm, gl = jnp.split(w_in_t, [256, 384, 448, 1472, 2496, 3520], axis=0)
    return jnp.concatenate([lg, lx, gm, gl, cq, ckv, kr, _swap_halves(kr, axis=0)], axis=0)


def _layer_weights(fl):
    w = {}
    if 'w_uq' in fl:
        uq = fl['w_uq']
        w['w_q'] = jnp.concatenate([uq, _swap_halves(uq[..., QK_NOPE:])], axis=-1).reshape(Q_RANK, HEADS * 2 * LANE)
        w['w_kv'] = jnp.concatenate([fl['w_uk'].reshape(KV_RANK, -1), fl['w_uv'].reshape(KV_RANK, -1)], axis=1).astype(BF16)
        w['w_g'] = jnp.moveaxis(jnp.concatenate([fl['w_rg'], fl['w_ig']], axis=0), 0, 1).astype(BF16)
        w['b4'] = jnp.concatenate([fl['b_rg'], fl['b_ig']], axis=0)
    for n in ('q_norm', 'kv_norm', 'w_o_mla', 'lru_conv_w', 'lru_conv_b', 'lru_lambda', 'w_o_lru', 'w_out', 'ln1_g',
              'ln1_b', 'w_up', 'ffn_conv_w', 'ffn_conv_b', 'w_down', 'ln2_g', 'ln2_b'):
        if n in fl:
            w[n] = fl[n]
    return w


def _layer_grads(g):
    out = {}
    if 'w_in' in g:
        lg, lx, gm, gl, cq, ckv, kr, krs = jnp.split(g['w_in'], [1024, 2048, 3072, 4096, 4352, 4480, 4544], axis=1)
        out['w_in'] = jnp.concatenate([cq, ckv, kr + _swap_halves(krs), lg, lx, gm, gl], axis=1)
    if 'w_q' in g:
        gq = g['w_q'].reshape(Q_RANK, HEADS, 2 * LANE)
        out['w_uq'] = jnp.concatenate([gq[..., :QK_NOPE], gq[..., QK_NOPE:QK_NOPE + QK_ROPE] + _swap_halves(gq[..., QK_NOPE + QK_ROPE:])], axis=-1)
    if 'w_kv' in g:
        out['w_uk'] = g['w_kv'][:, :HEADS * QK_NOPE].reshape(KV_RANK, HEADS, QK_NOPE)
        out['w_uv'] = g['w_kv'][:, HEADS * QK_NOPE:].reshape(KV_RANK, HEADS, V_HEAD)
    if 'w_g' in g:
        gg = jnp.moveaxis(g['w_g'], 1, 0)
        out['w_rg'], out['w_ig'] = gg[:2], gg[2:]
    if 'b4' in g:
        out['b_rg'], out['b_ig'] = g['b4'][:2], g['b4'][2:]
    for n in ('q_norm', 'kv_norm', 'lru_conv_b', 'ln1_g', 'ln1_b', 'ffn_conv_b', 'ln2_g', 'ln2_b'):
        if n in g:
            out[n] = g[n].reshape(-1)
    for n in ('w_o_mla', 'lru_conv_w', 'lru_lambda', 'w_o_lru', 'w_out', 'w_up', 'ffn_conv_w', 'w_down'):
        if n in g:
            out[n] = g[n]
    return out


def _rope_table(tp):
    half = QK_ROPE // 2
    inv_freq = jnp.exp(-math.log(ROPE_THETA) * jnp.arange(half, dtype=F32) / half)
    ang = jnp.arange(tp, dtype=F32)[:, None] * inv_freq[None, :]
    c, s = jnp.cos(ang), jnp.sin(ang)
    return jnp.concatenate([c, c, -s, s], axis=1)


def _local_step(x, target, meta, ln0_g, ln0_b, layer_w, t_pad, emit):
    seq = x.shape[0]
    t_real = N_META + seq
    zpad = jnp.zeros((t_pad - t_real, D_MODEL), F32)
    xin = jnp.concatenate([meta, x, zpad], axis=0)
    tgt = jnp.concatenate([jnp.zeros((N_META, D_MODEL), F32), target, zpad], axis=0)
    cs = _rope_table(t_pad)
    h = _ln_fwd([(1.0, xin)], ln0_g, ln0_b, "ln0")
    saved = []
    for l in range(DEPTH):
        w_in, rest_of_weights = layer_w[l](h)
        h, s = _layer_fwd(h, w_in, rest_of_weights, cs, t_real, "l%d_" % l)
        saved.append(s)
    dy, lossvec = _loss_head(h, tgt, t_real, "loss_head")
    terms, tok = [(1.0, dy)], None
    for l in reversed(range(DEPTH)):
        terms, tok = _layer_bwd(terms, saved[l], cs, t_real, "l%d_" % l,
                                functools.partial(lambda stage, g, l: emit(l, stage, _layer_grads(g)), l=l), tok)
    dxin, dg0, db0 = _ln_bwd(terms, [(1.0, xin)], _after(ln0_g, tok), "ln0_bwd")
    emit(None, 'head', {'meta_tokens': dxin[:N_META], 'ln0_g': dg0.reshape(-1), 'ln0_b': db0.reshape(-1), 'loss': lossvec})
    return dxin[N_META:t_real]


_HBM = pl.BlockSpec(memory_space=pltpu.HBM)
_SEM = pl.BlockSpec(memory_space=pltpu.SEMAPHORE)
_SIDE_EFFECT = pltpu.SideEffectType.DATAFLOW_SIDE_EFFECTING


def _peer_copies(src_refs, land_refs, scatters, send_sems, recv_sems):
    x, y, c = lax.axis_index("x"), lax.axis_index("y"), lax.axis_index("c")
    me = 4 * x + 2 * y + c
    copies = []
    for k in range(1, N_DEV):
        px = 1 - x if k & 4 else x
        py = 1 - y if k & 2 else y
        pc = 1 - c if k & 1 else c
        for t, (src, land) in enumerate(zip(src_refs, land_refs)):
            copies.append(pltpu.make_async_remote_copy(
                src_ref=src.at[4 * px + 2 * py + pc] if scatters[t] else src, dst_ref=land.at[me],
                send_sem=send_sems.at[7 * t + k - 1], recv_sem=recv_sems.at[7 * t + k - 1],
                device_id=(px, py, pc), device_id_type=pl.DeviceIdType.MESH))
    return me, copies


def _own_block_in_place(land, own):
    me = 4 * lax.axis_index("x") + 2 * lax.axis_index("y") + lax.axis_index("c")
    return lax.dynamic_update_slice_in_dim(land, own, me, 0)


def _gather_two_level(shards, name):
    nt = len(shards)

    def body(*refs):
        x_refs, out_refs = refs[:nt], refs[nt:2 * nt]
        token_ref, send_sems, recv_sems = refs[2 * nt:]
        x, y, c = lax.axis_index("x"), lax.axis_index("y"), lax.axis_index("c")
        me, sibling = (x, y, c), (x, y, 1 - c)
        chips = [(1 - x, y), (x, 1 - y), (1 - x, 1 - y)]

        def copy(t, k, block, to, own=False):
            px, py, pc = block
            slot = out_refs[t].at[4 * px + 2 * py + pc]
            return pltpu.make_async_remote_copy(
                src_ref=x_refs[t] if own else slot, dst_ref=slot,
                send_sem=send_sems.at[7 * t + k], recv_sem=recv_sems.at[7 * t + k],
                device_id=to, device_id_type=pl.DeviceIdType.MESH)

        sent = []
        for t in range(nt):
            first = [copy(t, 1 + j, me, (*chip, c), own=True) for j, chip in enumerate(chips)]
            first.append(copy(t, 0, me, sibling, own=True))
            for cp in first:
                cp.start()
            sent += first
        token_ref[...] = jnp.zeros_like(token_ref)
        for j, chip in enumerate(chips):
            for t in range(nt):
                copy(t, 1 + j, (*chip, c), me).wait_recv()
                passed = copy(t, 4 + j, (*chip, c), sibling)
                passed.start()
                sent.append(passed)
        for t in range(nt):
            copy(t, 0, sibling, me).wait_recv()
            for j, chip in enumerate(chips):
                copy(t, 4 + j, (*chip, 1 - c), me).wait_recv()
        for cp in sent:
            cp.wait_send()

    any_space = pl.BlockSpec(memory_space=pl.ANY)
    res = pl.pallas_call(
        body, name=name,
        out_shape=[jax.ShapeDtypeStruct((N_DEV,) + a.shape, a.dtype) for a in shards] + [jax.ShapeDtypeStruct((SUBLANE, LANE), F32)],
        in_specs=[any_space] * nt, out_specs=[any_space] * nt + [pl.BlockSpec(memory_space=pltpu.VMEM)],
        scratch_shapes=[pltpu.SemaphoreType.DMA((7 * nt,)), pltpu.SemaphoreType.DMA((7 * nt,))],
    )(*shards)
    return [_own_block_in_place(land, a[None]) for land, a in zip(res[:nt], shards)], res[nt][0, 0]


def _exchange_start(groups, name):
    flat = [it for grp in groups for it in grp]
    nt, ng = len(flat), len(groups)
    scatters = [sc for _, sc in flat]
    srcs = [pltpu.with_memory_space_constraint(a, pltpu.HBM) for a, _ in flat]
    land_shapes = [a.shape if sc else (N_DEV,) + a.shape for a, sc in flat]
    lands = [pltpu.with_memory_space_constraint(lax.empty(s, a.dtype), pltpu.HBM) for s, (a, _) in zip(land_shapes, flat)]
    bounds = [0]
    for grp in groups:
        bounds.append(bounds[-1] + len(grp))

    def body(*refs):
        src_refs, land_refs = refs[:nt], refs[nt:2 * nt]
        sem_refs = refs[2 * nt:2 * nt + 2 * ng]
        token_ref = refs[4 * nt + 2 * ng]
        for gi in range(ng):
            lo, hi = bounds[gi], bounds[gi + 1]
            _, copies = _peer_copies(src_refs[lo:hi], land_refs[lo:hi], scatters[lo:hi], sem_refs[2 * gi], sem_refs[2 * gi + 1])
            for cp in copies:
                cp.start()
        token_ref[...] = jnp.zeros_like(token_ref)

    out_shape = []
    for grp in groups:
        out_shape += [pltpu.SemaphoreType.DMA((7 * len(grp),)), pltpu.SemaphoreType.DMA((7 * len(grp),))]
    out_shape += [pltpu.HBM(a.shape, a.dtype) for a in srcs] + [pltpu.HBM(s, a.dtype) for s, a in zip(land_shapes, srcs)]
    out_shape += [jax.ShapeDtypeStruct((SUBLANE, LANE), F32)]
    res = pl.pallas_call(
        body, name=name, out_shape=out_shape,
        in_specs=[_HBM] * (2 * nt),
        out_specs=[_SEM] * (2 * ng) + [_HBM] * (2 * nt) + [pl.BlockSpec(memory_space=pltpu.VMEM)],
        input_output_aliases={t: 2 * ng + t for t in range(2 * nt)},
        compiler_params=pltpu.CompilerParams(has_side_effects=_SIDE_EFFECT),
    )(*srcs, *lands)
    sems, thru, token = res[:2 * ng], res[2 * ng:2 * ng + 2 * nt], res[-1]
    states = []
    for gi in range(ng):
        lo, hi = bounds[gi], bounds[gi + 1]
        states.append((sems[2 * gi], sems[2 * gi + 1], thru[lo:hi], thru[nt + lo:nt + hi], scatters[lo:hi]))
    return states, token[0, 0]


def _exchange_wait(state, after, name):
    send_sems, recv_sems, srcs, lands, scatters = state
    n = len(srcs)

    def body(*refs):
        _, copies = _peer_copies(refs[:n], refs[n:2 * n], scatters, refs[2 * n], refs[2 * n + 1])
        for cp in copies:
            cp.wait_send()
        for cp in copies:
            cp.wait_recv()

    res = pl.pallas_call(
        body, name=name,
        out_shape=[pltpu.HBM(a.shape, a.dtype) for a in srcs] + [pltpu.HBM(a.shape, a.dtype) for a in lands],
        in_specs=[_HBM] * (2 * n) + [_SEM, _SEM, _HBM],
        out_specs=[_HBM] * (2 * n),
        input_output_aliases={t: t for t in range(2 * n)},
        compiler_params=pltpu.CompilerParams(has_side_effects=_SIDE_EFFECT),
    )(*srcs, *lands, send_sems, recv_sems, pltpu.with_memory_space_constraint(after, pltpu.HBM))
    me = 4 * lax.axis_index("x") + 2 * lax.axis_index("y") + lax.axis_index("c")
    out = []
    for src, land, sc in zip(res[:n], res[n:], scatters):
        own = lax.dynamic_index_in_dim(src, me, 0, keepdims=True) if sc else src[None]
        out.append(lax.dynamic_update_slice_in_dim(land, own, me, 0))
    return out


def _as_rows(shape):
    return (1, shape[0]) if len(shape) == 1 else (math.prod(shape[:-1]), shape[-1])


def _sum_adamw(pieces, w, m, v, name):
    shape = w.shape
    nl = len(pieces)
    if nl > 1 and _as_rows(shape[1:])[0] % 16:
        pieces, nl = [jnp.stack(pieces, axis=1)], 1
    rows, cols = _as_rows(shape)
    rl = rows // nl
    cap = max(16, (1 << 18) // cols // 16 * 16)
    tr = _tile(rl, cap, 16)
    nb = rl // tr
    c1 = 1.0 / (1.0 - ADAM_B1 ** ADAM_STEP)
    c2 = 1.0 / (1.0 - ADAM_B2 ** ADAM_STEP)

    def body(*refs):
        p_refs = refs[:nl]
        w_ref, m_ref, v_ref, g_ref, d_ref, nm_ref, nv_ref = refs[nl:]
        li = pl.program_id(0)

        def total(p_ref):
            acc = p_ref[0].astype(F32)
            for k in range(1, N_DEV):
                acc = acc + p_ref[k].astype(F32)
            return acc

        gg = total(p_refs[0])
        for l in range(1, nl):
            gg = jnp.where(li == l, total(p_refs[l]), gg)
        nm = ADAM_B1 * m_ref[...] + (1.0 - ADAM_B1) * gg
        nv = ADAM_B2 * v_ref[...] + (1.0 - ADAM_B2) * (gg * gg)
        g_ref[...] = gg
        d_ref[...] = -ADAM_LR * ((nm * c1) / (jnp.sqrt(nv * c2) + ADAM_EPS) + ADAM_WD * w_ref[...])
        nm_ref[...] = nm
        nv_ref[...] = nv

    blk = pl.BlockSpec((tr, cols), lambda li, i: (li * nb + i, 0))
    p_specs = [pl.BlockSpec((N_DEV, tr, cols), functools.partial(lambda li, i, l: (0, jnp.where(li == l, i, 0), 0), l=l))
               for l in range(nl)]
    res = pl.pallas_call(
        body, name=name, grid=(nl, nb),
        in_specs=p_specs + [blk] * 3, out_specs=[blk] * 4,
        out_shape=[jax.ShapeDtypeStruct((rows, cols), F32)] * 4,
        compiler_params=_cparams("parallel", "parallel"),
    )(*[p.reshape(N_DEV, rl, cols) for p in pieces], *[a.reshape(rows, cols) for a in (w, m, v)])
    return [r.reshape(shape) for r in res]


def _to_shards(full, axis):
    shp = full.shape
    a = full.reshape(shp[:axis] + (N_DEV, shp[axis] // N_DEV) + shp[axis + 1:])
    return jnp.moveaxis(a, axis, 0)


def _from_shards(blocks, axis):
    a = jnp.moveaxis(blocks, 0, axis)
    shp = a.shape
    return a.reshape(shp[:axis] + (shp[axis] * shp[axis + 1],) + shp[axis + 2:])


def kernel(x, meta_tokens, ln0_g, ln0_b, w_in, q_norm, kv_norm, w_uq, w_uk, w_uv, w_o_mla, lru_conv_w, lru_conv_b, w_rg, b_rg, w_ig, b_ig, lru_lambda, w_o_lru, w_out, ln1_g, ln1_b, w_up, ffn_conv_w, ffn_conv_b, w_down, ln2_g, ln2_b, loss_target, m_meta_tokens, m_ln0_g, m_ln0_b, m_w_in, m_q_norm, m_kv_norm, m_w_uq, m_w_uk, m_w_uv, m_w_o_mla, m_lru_conv_w, m_lru_conv_b, m_w_rg, m_b_rg, m_w_ig, m_b_ig, m_lru_lambda, m_w_o_lru, m_w_out, m_ln1_g, m_ln1_b, m_w_up, m_ffn_conv_w, m_ffn_conv_b, m_w_down, m_ln2_g, m_ln2_b, v_meta_tokens, v_ln0_g, v_ln0_b, v_w_in, v_q_norm, v_kv_norm, v_w_uq, v_w_uk, v_w_uv, v_w_o_mla, v_lru_conv_w, v_lru_conv_b, v_w_rg, v_b_rg, v_w_ig, v_b_ig, v_lru_lambda, v_w_o_lru, v_w_out, v_ln1_g, v_ln1_b, v_w_up, v_ffn_conv_w, v_ffn_conv_b, v_w_down, v_ln2_g, v_ln2_b):
    args = (meta_tokens, ln0_g, ln0_b, w_in, q_norm, kv_norm, w_uq, w_uk, w_uv, w_o_mla, lru_conv_w, lru_conv_b, w_rg, b_rg, w_ig, b_ig, lru_lambda, w_o_lru, w_out, ln1_g, ln1_b, w_up, ffn_conv_w, ffn_conv_b, w_down, ln2_g, ln2_b)
    ms = (m_meta_tokens, m_ln0_g, m_ln0_b, m_w_in, m_q_norm, m_kv_norm, m_w_uq, m_w_uk, m_w_uv, m_w_o_mla, m_lru_conv_w, m_lru_conv_b, m_w_rg, m_b_rg, m_w_ig, m_b_ig, m_lru_lambda, m_w_o_lru, m_w_out, m_ln1_g, m_ln1_b, m_w_up, m_ffn_conv_w, m_ffn_conv_b, m_w_down, m_ln2_g, m_ln2_b)
    vs = (v_meta_tokens, v_ln0_g, v_ln0_b, v_w_in, v_q_norm, v_kv_norm, v_w_uq, v_w_uk, v_w_uv, v_w_o_mla, v_lru_conv_w, v_lru_conv_b, v_w_rg, v_b_rg, v_w_ig, v_b_ig, v_lru_lambda, v_w_o_lru, v_w_out, v_ln1_g, v_ln1_b, v_w_up, v_ffn_conv_w, v_ffn_conv_b, v_w_down, v_ln2_g, v_ln2_b)
    wd, md, vd = dict(zip(WEIGHTS, args)), dict(zip(WEIGHTS, ms)), dict(zip(WEIGHTS, vs))

    def shard_axis(n, l):
        return SHARD_AXIS[n] - (0 if l is None else 1)

    def shard(n, l):
        a = wd[n] if l is None else wd[n][l]
        if n in SENT_TRANSPOSED:
            a = a.T
        return a.astype(BF16) if n in BIG else a

    def whole(keys, landed):
        return {k: b.reshape(-1, b.shape[-1]) if k[0] in SENT_TRANSPOSED else _from_shards(b, shard_axis(*k))
                for k, b in zip(keys, landed)}

    first = [('meta_tokens', None), ('w_in', 0)]
    landed, token = _gather_two_level([shard(*k) for k in first], "gather_first")
    got_first = whole(first, landed)
    staged = [[(n, 0) for n in names] for names in STAGE_WEIGHTS]
    later = [(n, 1) for n in SHARDED if n != 'meta_tokens']
    groups = [[(shard(*k), False) for k in keys] for keys in staged + [later]]
    groups[0][0] = (_after(groups[0][0][0], token), False)
    gather, token = _exchange_start(groups, "gather_start")

    def arrive(gi, keys, after, name):
        return whole(keys, _exchange_wait(gather[gi], after, name))

    def layer_weights(got, l, names):
        fl = {n: wd[n][l] for n in names if n in REPLICATED}
        fl.update({n: a for (n, _), a in got.items() if n in names})
        return _layer_weights(fl)

    ln0_g = _after(wd['ln0_g'], token)

    def first_layer(h):
        def more(stage, after):
            got = arrive(stage, staged[stage], after, "gather_wait_l0_%d" % stage)
            return layer_weights(got, 0, STAGE_WEIGHTS[stage] + STAGE_REPLICATED[stage])
        return _w_in_kernel(got_first['w_in', 0]), more

    def second_layer(h):
        got = arrive(len(staged), later, h, "gather_wait_l1")
        return _w_in_kernel(got['w_in', 1]), lambda stage, after: layer_weights(got, 1, STAGE_WEIGHTS[stage] + STAGE_REPLICATED[stage])

    sent = []
    pending = []

    def send(l, stage, grads):
        for n, g in grads.items():
            if n in SHARD_AXIS:
                g = _to_shards(g, shard_axis(n, l))
                pending.append(((n, l), (g.astype(BF16) if n in BIG else g, True)))
            else:
                pending.append(((n, l), (g.astype(BF16) if n in LARGE_REPLICATED else g, False)))
        if l == DEPTH - 1 and stage != 'in':
            return None
        (state,), tok = _exchange_start([[it for _, it in pending]], "grads_start_%s_%s" % (l, stage))
        sent.append(([k for k, _ in pending], state))
        pending.clear()
        return tok

    seq = x.shape[1]
    t_pad = -(-(N_META + seq + MIN_PAD_ROWS) // LANE) * LANE
    grad_x = _local_step(x[0], loss_target[0], got_first['meta_tokens', None], ln0_g, wd['ln0_b'],
                         [first_layer, second_layer], t_pad, send)

    pieces, outs, after = {}, {}, grad_x
    for gi, (keys, state) in enumerate(sent):
        pieces.update(zip(keys, _exchange_wait(state, after, "grads_wait_%d" % gi)))
        if gi == len(sent) - 2:
            for n in WEIGHTS:
                if (n, 0) in pieces:
                    outs[n] = _sum_adamw([pieces[n, l] for l in range(DEPTH)], wd[n], md[n], vd[n], "adamw_" + n)
                    after = outs[n][1]
    loss = jnp.sum(pieces['loss', None])
    for n in WEIGHTS:
        if (n, None) in pieces:
            outs[n] = _sum_adamw([pieces[n, None]], wd[n], md[n], vd[n], "adamw_" + n)
    res = [loss, grad_x[None]]
    for k in range(4):
        res += [outs[n][k] for n in WEIGHTS]
    return tuple(res)
```

```python
import functools
import math

import jax
import jax.numpy as jnp
from jax import lax
from jax.experimental import pallas as pl
from jax.experimental.pallas import tpu as pltpu

F32 = jnp.float32
BF16 = jnp.bfloat16

N_DEV = 8
D_MODEL = 1024
N_META = 16
HEADS = 8
QK_NOPE = 128
QK_ROPE = 64
V_HEAD = 128
Q_RANK = 256
KV_RANK = 128
ROPE_THETA = 10000.0
LRU_BLOCKS = 8
LRU_C = 8.0
D_FF = 2816
DEPTH = 2
DN_ALPHA = (2.0 * DEPTH) ** 0.25
LN_EPS = 1e-5
RMS_EPS = 1e-6
LN2 = math.log(2.0)
ATT_SCALE = 1.0 / math.sqrt(QK_NOPE + QK_ROPE) / LN2
NEG_BIG = -1e30

ADAM_LR = 0.001
ADAM_B1 = 0.9
ADAM_B2 = 0.999
ADAM_EPS = 1e-08
ADAM_WD = 0.01
ADAM_STEP = 10

MIN_PAD_ROWS = 2
LANE = 128
SUBLANE = 8
VMEM_LIMIT = 56 * 1024 * 1024

PROJ_COLS = 4 * D_MODEL + Q_RANK + KV_RANK + 2 * QK_ROPE
C_LRU_G, C_LRU_X, C_G_MLA, C_G_LRU = 0, D_MODEL, 2 * D_MODEL, 3 * D_MODEL
C_CQ = 4 * D_MODEL
C_CKV = C_CQ + Q_RANK
C_KRP = C_CKV + KV_RANK

WEIGHTS = ['meta_tokens', 'ln0_g', 'ln0_b', 'w_in', 'q_norm', 'kv_norm', 'w_uq', 'w_uk', 'w_uv', 'w_o_mla',
           'lru_conv_w', 'lru_conv_b', 'w_rg', 'b_rg', 'w_ig', 'b_ig', 'lru_lambda', 'w_o_lru', 'w_out',
           'ln1_g', 'ln1_b', 'w_up', 'ffn_conv_w', 'ffn_conv_b', 'w_down', 'ln2_g', 'ln2_b']
SHARD_AXIS = {'meta_tokens': 1, 'w_in': 2, 'w_uq': 1, 'w_o_mla': 1, 'lru_conv_w': 2, 'b_rg': 2, 'b_ig': 2,
              'lru_lambda': 2, 'w_o_lru': 1, 'w_out': 1, 'w_up': 2, 'ffn_conv_w': 2, 'w_down': 1}
BIG = ['w_in', 'w_uq', 'w_o_mla', 'w_o_lru', 'w_out', 'w_up', 'w_down']
SHARDED = [n for n in WEIGHTS if n in SHARD_AXIS]
REPLICATED = [n for n in WEIGHTS if n not in SHARD_AXIS]
LARGE_REPLICATED = ['w_uk', 'w_uv', 'w_rg', 'w_ig']
SENT_TRANSPOSED = ['w_in', 'w_up']
STAGE_WEIGHTS = [['w_uq', 'lru_conv_w', 'b_rg', 'b_ig', 'lru_lambda'], ['w_o_mla', 'w_o_lru', 'w_out'], ['w_up', 'ffn_conv_w', 'w_down']]
STAGE_REPLICATED = [['q_norm', 'kv_norm', 'w_uk', 'w_uv', 'lru_conv_b', 'w_rg', 'w_ig'], ['ln1_g', 'ln1_b'], ['ffn_conv_b', 'ln2_g', 'ln2_b']]


def _cparams(*sem):
    return pltpu.CompilerParams(dimension_semantics=sem, vmem_limit_bytes=VMEM_LIMIT)


def _tile(n, cap, unit=LANE):
    best = None
    t = unit
    while t <= min(n, cap):
        if n % t == 0:
            best = t
        t += unit
    return n if best is None else best


def _sigmoid(x):
    return 1.0 / (1.0 + jnp.exp(-x))


_GELU_C = math.sqrt(2.0 / math.pi)


_GELU_A = 0.044715


def _gelu(x):
    t = jnp.tanh(x * (_GELU_C + (_GELU_C * _GELU_A) * (x * x)))
    hx = 0.5 * x
    return hx + hx * t


def _gelu_and_grad(x):
    x2 = x * x
    t = jnp.tanh(x * (_GELU_C + (_GELU_C * _GELU_A) * x2))
    hx = 0.5 * x
    dg = 0.5 + 0.5 * t + (hx * (1.0 - t * t)) * (_GELU_C + (3.0 * _GELU_C * _GELU_A) * x2)
    return hx + hx * t, dg


def _softplus_neg(lam):
    z = jnp.exp(-jnp.abs(lam))
    w = 1.0 + z
    log1p = jnp.where(w == 1.0, z, jnp.log(w) * z / (w - 1.0))
    return jnp.maximum(-lam, 0.0) + log1p


def _row_ids(shape, row0=0):
    return lax.broadcasted_iota(jnp.int32, shape, 0) + row0


def _matmul(a, b, name, ta=False, tb=False, out_dtype=F32, tm_cap=1408, tn_cap=2304, tk_cap=2048):
    if ta:
        kdim, m = a.shape
    else:
        m, kdim = a.shape
    if tb:
        n, k2 = b.shape
    else:
        k2, n = b.shape
    assert kdim == k2, (a.shape, b.shape, ta, tb)
    tm, tn, tk = _tile(m, tm_cap), _tile(n, tn_cap), _tile(kdim, tk_cap)
    nk = kdim // tk

    def body(a_ref, b_ref, o_ref, *acc):
        dn = (((0 if ta else 1,), (1 if tb else 0,)), ((), ()))
        part = lax.dot_general(a_ref[...].astype(BF16), b_ref[...].astype(BF16), dn, preferred_element_type=F32)
        if nk == 1:
            o_ref[...] = part.astype(o_ref.dtype)
            return
        acc_ref, k = acc[0], pl.program_id(2)

        @pl.when(k == 0)
        def _():
            acc_ref[...] = part

        @pl.when(k > 0)
        def _():
            acc_ref[...] += part

        @pl.when(k == nk - 1)
        def _():
            o_ref[...] = acc_ref[...].astype(o_ref.dtype)

    a_spec = pl.BlockSpec((tk, tm), lambda i, j, k: (k, i)) if ta else pl.BlockSpec((tm, tk), lambda i, j, k: (i, k))
    b_spec = pl.BlockSpec((tn, tk), lambda i, j, k: (j, k)) if tb else pl.BlockSpec((tk, tn), lambda i, j, k: (k, j))
    return pl.pallas_call(
        body, name=name,
        grid=(m // tm, n // tn, nk),
        in_specs=[a_spec, b_spec],
        out_specs=pl.BlockSpec((tm, tn), lambda i, j, k: (i, j)),
        out_shape=jax.ShapeDtypeStruct((m, n), out_dtype),
        scratch_shapes=[pltpu.VMEM((tm, tn), F32)] if nk > 1 else [],
        compiler_params=_cparams("parallel", "parallel", "arbitrary"),
    )(a, b)


class Rw:
    def __init__(self, arr, width=None, cb=0):
        self.arr, self.width, self.cb = arr, (arr.shape[1] if width is None else width), cb


class Pm:
    def __init__(self, arr):
        self.arr = arr


class Into:
    def __init__(self, arr, col0, width):
        self.arr, self.col0, self.width = arr, col0, width


def _call_with_into(body, name, grid, in_specs, operands, outs, spec_of, shape_of, extra_out_specs, extra_out_shape, sem):
    intos = [(k, o) for k, o in enumerate(outs) if isinstance(o, Into)]
    aliases = {len(operands) + n: k for n, (k, _) in enumerate(intos)}
    return pl.pallas_call(
        body, name=name, grid=grid,
        in_specs=in_specs + [pl.BlockSpec(memory_space=pl.ANY)] * len(intos),
        out_specs=[spec_of(o) for o in outs] + extra_out_specs,
        out_shape=[jax.ShapeDtypeStruct(o.arr.shape, o.arr.dtype) if isinstance(o, Into) else shape_of(o) for o in outs]
        + extra_out_shape,
        input_output_aliases=aliases,
        compiler_params=_cparams(sem),
    )(*operands, *[o.arr for _, o in intos])


def _rows(fn, name, ins, outs, accs=(), tm_cap=384):
    tp = next(o.arr.shape[0] for o in ins if isinstance(o, Rw))
    tm = _tile(tp, tm_cap)
    n_in, n_out, n_acc = len(ins), len(outs), len(accs)
    n_into = sum(isinstance(o, Into) for o in outs)

    def body(*refs):
        i = pl.program_id(0)
        res = fn(i * tm, *[r[...] for r in refs[:n_in]])
        if not isinstance(res, (tuple, list)):
            res = (res,)
        assert len(res) == n_out + n_acc, (name, len(res))
        out_refs = refs[n_in + n_into:]
        for k in range(n_out):
            out_refs[k][...] = res[k].astype(out_refs[k].dtype)
        for k in range(n_acc):
            ref = out_refs[n_out + k]

            @pl.when(i == 0)
            def _():
                ref[...] = jnp.zeros_like(ref)

            ref[...] += res[n_out + k]

    in_specs = []
    for o in ins:
        if isinstance(o, Rw):
            in_specs.append(pl.BlockSpec((tm, o.width), functools.partial(lambda i, cb: (i, cb), cb=o.cb)))
        else:
            in_specs.append(pl.BlockSpec(o.arr.shape, functools.partial(lambda i, nd: (0,) * nd, nd=o.arr.ndim)))

    def spec_of(o):
        if isinstance(o, Into):
            assert o.col0 % o.width == 0, (name, o.col0, o.width)
            return pl.BlockSpec((tm, o.width), functools.partial(lambda i, cb: (i, cb), cb=o.col0 // o.width))
        return pl.BlockSpec((tm, o[0]), lambda i: (i, 0))

    return _call_with_into(
        body, name, (tp // tm,), in_specs, [o.arr for o in ins], list(outs), spec_of,
        lambda o: jax.ShapeDtypeStruct((tp, o[0]), o[1]),
        [pl.BlockSpec(s, functools.partial(lambda i, nd: (0,) * nd, nd=len(s))) for s in accs],
        [jax.ShapeDtypeStruct(s, F32) for s in accs], "arbitrary")


class Cl:
    def __init__(self, arr, col0=0):
        self.arr, self.col0 = arr, col0


def _cols(fn, name, ins, outs, ncols, tc):
    assert ncols % tc == 0
    n_in, n_out = len(ins), len(outs)
    n_into = sum(isinstance(o, Into) for o in outs)

    def body(*refs):
        res = fn(*[r[...] for r in refs[:n_in]])
        if not isinstance(res, (tuple, list)):
            res = (res,)
        assert len(res) == n_out, (name, len(res))
        out_refs = refs[n_in + n_into:]
        for k in range(n_out):
            out_refs[k][...] = res[k].astype(out_refs[k].dtype)

    in_specs = []
    for o in ins:
        assert o.col0 % tc == 0, (name, o.col0, tc)
        in_specs.append(pl.BlockSpec((o.arr.shape[0], tc), functools.partial(lambda j, off: (0, j + off), off=o.col0 // tc)))

    def spec_of(o):
        if isinstance(o, Into):
            assert o.col0 % tc == 0 and o.width == ncols, (name, o.col0, o.width)
            return pl.BlockSpec((o.arr.shape[0], tc), functools.partial(lambda j, off: (0, j + off), off=o.col0 // tc))
        return pl.BlockSpec((o[0], tc), lambda j: (0, j))

    return _call_with_into(body, name, (ncols // tc,), in_specs, [o.arr for o in ins], list(outs), spec_of,
                           lambda o: jax.ShapeDtypeStruct((o[0], ncols), o[1]), [], [], "parallel")


def _ln_stats(u):
    mu = jnp.mean(u, axis=-1, keepdims=True)
    xc = u - mu
    var = jnp.mean(xc * xc, axis=-1, keepdims=True)
    rstd = lax.rsqrt(var + LN_EPS)
    return xc * rstd, rstd


def _ln_fwd(terms, g, b, name):
    coefs = [c for c, _ in terms]

    def fn(row0, *blk):
        xs, (gg, bb) = blk[:len(coefs)], blk[len(coefs):]
        u = sum(c * x for c, x in zip(coefs, xs))
        xhat, _ = _ln_stats(u)
        return xhat * gg + bb

    d = terms[0][1].shape[1]
    return _rows(fn, name, [Rw(x) for _, x in terms] + [Pm(g.reshape(1, d)), Pm(b.reshape(1, d))], [(d, F32)])[0]


def _ln_bwd(dy_terms, u_terms, g, name):
    dc = [c for c, _ in dy_terms]
    uc = [c for c, _ in u_terms]
    d = u_terms[0][1].shape[1]

    def fn(row0, *blk):
        dys = blk[:len(dc)]
        xs = blk[len(dc):len(dc) + len(uc)]
        gg = blk[-1]
        dy = sum(c * x for c, x in zip(dc, dys))
        u = sum(c * x for c, x in zip(uc, xs))
        xhat, rstd = _ln_stats(u)
        gdy = dy * gg
        m1 = jnp.mean(gdy, axis=-1, keepdims=True)
        m2 = jnp.mean(gdy * xhat, axis=-1, keepdims=True)
        du = rstd * (gdy - m1 - xhat * m2)
        return du, jnp.sum(dy * xhat, axis=0, keepdims=True), jnp.sum(dy, axis=0, keepdims=True)

    ins = [Rw(x) for _, x in dy_terms] + [Rw(x) for _, x in u_terms] + [Pm(g.reshape(1, d))]
    return _rows(fn, name, ins, [(d, F32)], accs=[(1, d), (1, d)])


def _loss_head(y, tgt, t_real, name):
    d = y.shape[1]

    def fn(row0, yb, tb):
        rows = _row_ids(yb.shape, row0)
        live = (rows >= N_META) & (rows < t_real)
        diff = jnp.where(live, yb - tb, 0.0)
        return diff * (1.0 / d), jnp.sum(diff * diff, axis=0, keepdims=True) * (0.5 / d)

    return _rows(fn, name, [Rw(y), Rw(tgt)], [(d, F32)], accs=[(1, d)])


def _rms(x, g):
    r = lax.rsqrt(jnp.mean(x * x, axis=-1, keepdims=True) + RMS_EPS)
    return x * r * g


def _rms_bwd(dy, x, g):
    r = lax.rsqrt(jnp.mean(x * x, axis=-1, keepdims=True) + RMS_EPS)
    gdy = dy * g
    dx = r * gdy - x * (r * r * r) * jnp.mean(gdy * x, axis=-1, keepdims=True)
    return dx, jnp.sum(dy * x * r, axis=0, keepdims=True)


def _mla_norms(proj, qn, kvn, name):
    def fn(row0, cq, ckv, g1, g2):
        return _rms(cq, g1), _rms(ckv, g2)

    return _rows(fn, name, [Rw(proj, Q_RANK, C_CQ // Q_RANK), Rw(proj, KV_RANK, C_CKV // KV_RANK),
                            Pm(qn.reshape(1, Q_RANK)), Pm(kvn.reshape(1, KV_RANK))],
                 [(Q_RANK, BF16), (KV_RANK, BF16)])


def _mla_norms_bwd(dcqn, dckvn, dkrp, proj, qn, kvn, dproj, name):
    def fn(row0, d1, d2, dkr, cq, ckv, g1, g2):
        dx1, dg1 = _rms_bwd(d1, cq, g1)
        dx2, dg2 = _rms_bwd(d2, ckv, g2)
        return jnp.concatenate([dx1, dx2, dkr], axis=1), dg1, dg2

    return _rows(fn, name, [Rw(dcqn), Rw(dckvn), Rw(dkrp), Rw(proj, Q_RANK, C_CQ // Q_RANK), Rw(proj, KV_RANK, C_CKV // KV_RANK),
                            Pm(qn.reshape(1, Q_RANK)), Pm(kvn.reshape(1, KV_RANK))],
                 [Into(dproj, C_CQ, PROJ_COLS - C_CQ)], accs=[(1, Q_RANK), (1, KV_RANK)])


def _fold_rope(z):
    return z + pltpu.roll(z, QK_ROPE, 1)


def _mla_pack(qext, kv, proj, cs, name):
    tp = qext.shape[0]
    tm = _tile(tp, 384)
    hw, nope_all = 2 * LANE, HEADS * QK_NOPE

    def body(q_ref, kv_ref, kr_ref, cs_ref, qo_ref, ko_ref, vo_ref):
        cs_ = cs_ref[...]
        low = lax.broadcasted_iota(jnp.int32, cs_.shape, 1) < QK_ROPE
        kr = _fold_rope(kr_ref[...] * cs_).astype(BF16)
        for h in range(HEADS):
            qr = jnp.where(low, _fold_rope(q_ref[:, h * hw + QK_NOPE:(h + 1) * hw] * cs_), 0.0)
            qo_ref[:, h * hw:h * hw + QK_NOPE] = (q_ref[:, h * hw:h * hw + QK_NOPE] * ATT_SCALE).astype(BF16)
            qo_ref[:, h * hw + QK_NOPE:(h + 1) * hw] = (qr * ATT_SCALE).astype(BF16)
            ko_ref[:, h * hw:h * hw + QK_NOPE] = kv_ref[:, h * QK_NOPE:(h + 1) * QK_NOPE].astype(BF16)
            ko_ref[:, h * hw + QK_NOPE:(h + 1) * hw] = kr
        vo_ref[...] = kv_ref[:, nope_all:].astype(BF16)

    row = lambda w: pl.BlockSpec((tm, w), lambda i: (i, 0))
    return pl.pallas_call(
        body, name=name, grid=(tp // tm,),
        in_specs=[row(HEADS * hw), row(2 * nope_all), pl.BlockSpec((tm, LANE), lambda i: (i, C_KRP // LANE)), row(LANE)],
        out_specs=[row(HEADS * hw), row(HEADS * hw), row(nope_all)],
        out_shape=[jax.ShapeDtypeStruct((tp, HEADS * hw), BF16),
                   jax.ShapeDtypeStruct((tp, HEADS * hw), BF16),
                   jax.ShapeDtypeStruct((tp, nope_all), BF16)],
        compiler_params=_cparams("parallel"),
    )(qext, kv, proj, cs)


def _mla_unpack(dk, dv, cs, name):
    tp = dk.shape[0]
    tm = _tile(tp, 384)
    hw, nope_all = 2 * LANE, HEADS * QK_NOPE

    def body(dk_ref, dv_ref, cs_ref, dkv_ref, dkr_ref):
        cs_ = cs_ref[...]
        low = lax.broadcasted_iota(jnp.int32, cs_.shape, 1) < QK_ROPE
        dkr = None
        for h in range(HEADS):
            dkv_ref[:, h * QK_NOPE:(h + 1) * QK_NOPE] = dk_ref[:, h * hw:h * hw + QK_NOPE].astype(BF16)
            part = jnp.where(low, dk_ref[:, h * hw + QK_NOPE:(h + 1) * hw], 0.0)
            dkr = part if h == 0 else dkr + part
        dkv_ref[:, nope_all:] = dv_ref[...].astype(BF16)
        dkr_ref[...] = _fold_rope(dkr) * cs_

    row = lambda w: pl.BlockSpec((tm, w), lambda i: (i, 0))
    return pl.pallas_call(
        body, name=name, grid=(tp // tm,),
        in_specs=[row(HEADS * hw), row(nope_all), row(LANE)],
        out_specs=[row(2 * nope_all), row(LANE)],
        out_shape=[jax.ShapeDtypeStruct((tp, 2 * nope_all), BF16), jax.ShapeDtypeStruct((tp, LANE), F32)],
        compiler_params=_cparams("parallel"),
    )(dk, dv, cs)


def _attn_fwd(q, k, v, t_real, name):
    tp = q.shape[0]
    tq = _tile(tp, 1408)
    tkc = _tile(tp, 704, 64)
    nkc = -(-t_real // tkc)

    def body(q_ref, k_ref, v_ref, o_ref, lse_ref):
        qb = q_ref[...]
        m = l = acc = None
        for c in range(nkc):
            s = lax.dot_general(qb, k_ref[c * tkc:(c + 1) * tkc, :], (((1,), (1,)), ((), ())), preferred_element_type=F32)
            if (c + 1) * tkc > t_real:
                cols = lax.broadcasted_iota(jnp.int32, s.shape, 1) + c * tkc
                s = jnp.where(cols < t_real, s, NEG_BIG)
            mc = jnp.max(s, axis=-1, keepdims=True)
            m_new = mc if c == 0 else jnp.maximum(m, mc)
            p = jnp.exp2(s - m_new)
            lc = jnp.sum(p, axis=-1, keepdims=True)
            pv = jnp.dot(p.astype(BF16), v_ref[c * tkc:(c + 1) * tkc, :], preferred_element_type=F32)
            if c == 0:
                l, acc = lc, pv
            else:
                alpha = jnp.exp2(m - m_new)
                l, acc = alpha * l + lc, alpha * acc + pv
            m = m_new
        o_ref[...] = acc / l
        lse_ref[...] = m + jnp.log2(l)

    return pl.pallas_call(
        body, name=name, grid=(HEADS, tp // tq),
        in_specs=[pl.BlockSpec((tq, 2 * LANE), lambda h, i: (i, h)),
                  pl.BlockSpec((tp, 2 * LANE), lambda h, i: (0, h)),
                  pl.BlockSpec((tp, LANE), lambda h, i: (0, h))],
        out_specs=[pl.BlockSpec((tq, LANE), lambda h, i: (i, h)),
                   pl.BlockSpec((None, tq, 1), lambda h, i: (h, i, 0))],
        out_shape=[jax.ShapeDtypeStruct((tp, HEADS * LANE), F32),
                   jax.ShapeDtypeStruct((HEADS, tp, 1), F32)],
        compiler_params=_cparams("parallel", "parallel"),
    )(q, k, v)


def _attn_bwd(q, k, v, do, o, lse, cs, t_real, name):
    tp = q.shape[0]
    tq = _tile(tp, 1408)
    tkc = _tile(tp, 704, 64)
    nkc = -(-t_real // tkc)

    def body(q_ref, k_ref, v_ref, do_ref, o_ref, lse_ref, cs_ref, dq_ref, dk_ref, dv_ref):
        i = pl.program_id(1)

        @pl.when(i == 0)
        def _():
            dk_ref[...] = jnp.zeros_like(dk_ref)
            dv_ref[...] = jnp.zeros_like(dv_ref)

        qb = q_ref[...]
        dob = do_ref[...]
        dob16 = dob.astype(BF16)
        dol2 = (dob * LN2).astype(BF16)
        delta = jnp.sum(dob * o_ref[...], axis=-1, keepdims=True) * LN2
        lse = lse_ref[...]
        dq = None
        for c in range(nkc):
            ks = slice(c * tkc, (c + 1) * tkc)
            kb = k_ref[ks, :]
            s = lax.dot_general(qb, kb, (((1,), (1,)), ((), ())), preferred_element_type=F32)
            p = jnp.exp2(s - lse)
            if (c + 1) * tkc > t_real:
                cols = lax.broadcasted_iota(jnp.int32, s.shape, 1) + c * tkc
                p = jnp.where(cols < t_real, p, 0.0)
            dp = lax.dot_general(dol2, v_ref[ks, :], (((1,), (1,)), ((), ())), preferred_element_type=F32)
            ds = (p * (dp - delta)).astype(BF16)
            dqc = jnp.dot(ds, kb, preferred_element_type=F32)
            dq = dqc if c == 0 else dq + dqc
            dk_ref[ks, :] += lax.dot_general(ds, qb, (((0,), (0,)), ((), ())), preferred_element_type=F32)
            dv_ref[ks, :] += lax.dot_general(p.astype(BF16), dob16, (((0,), (0,)), ((), ())), preferred_element_type=F32)
        cs_ = cs_ref[...]
        low = lax.broadcasted_iota(jnp.int32, cs_.shape, 1) < QK_ROPE
        dq = dq * ATT_SCALE
        dq_ref[:, :QK_NOPE] = dq[:, :QK_NOPE].astype(BF16)
        dq_ref[:, QK_NOPE:] = (_fold_rope(jnp.where(low, dq[:, QK_NOPE:], 0.0)) * cs_).astype(BF16)

    return pl.pallas_call(
        body, name=name, grid=(HEADS, tp // tq),
        in_specs=[pl.BlockSpec((tq, 2 * LANE), lambda h, i: (i, h)),
                  pl.BlockSpec((tp, 2 * LANE), lambda h, i: (0, h)),
                  pl.BlockSpec((tp, LANE), lambda h, i: (0, h)),
                  pl.BlockSpec((tq, LANE), lambda h, i: (i, h)),
                  pl.BlockSpec((tq, LANE), lambda h, i: (i, h)),
                  pl.BlockSpec((None, tq, 1), lambda h, i: (h, i, 0)),
                  pl.BlockSpec((tq, LANE), lambda h, i: (i, 0))],
        out_specs=[pl.BlockSpec((tq, 2 * LANE), lambda h, i: (i, h)),
                   pl.BlockSpec((tp, 2 * LANE), lambda h, i: (0, h)),
                   pl.BlockSpec((tp, LANE), lambda h, i: (0, h))],
        out_shape=[jax.ShapeDtypeStruct((tp, HEADS * 2 * LANE), BF16),
                   jax.ShapeDtypeStruct((tp, HEADS * 2 * LANE), F32),
                   jax.ShapeDtypeStruct((tp, HEADS * LANE), F32)],
        compiler_params=_cparams("parallel", "arbitrary"),
    )(q, k, v, do, o, lse, cs)


def _shift_rows(x, s):
    tp = x.shape[0]
    return x if s % tp == 0 else pltpu.roll(x, s % tp, 0)


def _conv_fwd_val(xm, w, b, pad_left):
    acc = b + w[0:1, :] * _shift_rows(xm, pad_left)
    for k in range(1, w.shape[0]):
        acc = acc + w[k:k + 1, :] * _shift_rows(xm, pad_left - k)
    return acc


def _conv_bwd_val(dy, xm, w, pad_left, live):
    kk = w.shape[0]
    dx = w[0:1, :] * _shift_rows(dy, -pad_left)
    dws = [jnp.sum(dy * _shift_rows(xm, pad_left), axis=0, keepdims=True)]
    for k in range(1, kk):
        dx = dx + w[k:k + 1, :] * _shift_rows(dy, k - pad_left)
        dws.append(jnp.sum(dy * _shift_rows(xm, pad_left - k), axis=0, keepdims=True))
    return jnp.where(live, dx, 0.0), jnp.concatenate(dws, axis=0), jnp.sum(dy, axis=0, keepdims=True)


def _lru_conv_fwd(proj, w, b, t_real, name):
    def fn(x, ww, bb):
        xm = jnp.where(_row_ids(x.shape) < t_real, x, 0.0)
        return _conv_fwd_val(xm, ww, bb, 2)

    return _cols(fn, name, [Cl(proj, C_LRU_X), Cl(w), Cl(b.reshape(1, -1))], [(proj.shape[0], F32)], D_MODEL, 128)[0]


def _lru_conv_bwd(dxc, proj, w, dproj, t_real, name):
    def fn(dy, x, ww):
        live = _row_ids(x.shape) < t_real
        xm = jnp.where(live, x, 0.0)
        dym = jnp.where(live, dy, 0.0)
        return _conv_bwd_val(dym, xm, ww, 2, live)

    return _cols(fn, name, [Cl(dxc), Cl(proj, C_LRU_X), Cl(w)],
                 [Into(dproj, C_LRU_X, D_MODEL), (w.shape[0], F32), (1, F32)], D_MODEL, 128)


def _ffn_conv_act(up, w, b, t_real, name):
    def fn(g, v, wg, wv, bg, bv):
        live = _row_ids(g.shape) < t_real
        gc = _conv_fwd_val(jnp.where(live, g, 0.0), wg, bg, 1)
        vc = _conv_fwd_val(jnp.where(live, v, 0.0), wv, bv, 1)
        return _gelu(gc) * vc

    b2 = b.reshape(1, -1)
    return _cols(fn, name, [Cl(up), Cl(up, D_FF), Cl(w), Cl(w, D_FF), Cl(b2), Cl(b2, D_FF)],
                 [(up.shape[0], BF16)], D_FF, 128)[0]


def _ffn_conv_act_bwd(dm, up, w, b, t_real, name):
    tp, kk = up.shape[0], w.shape[0]
    nb = D_FF // LANE
    assert nb >= 2

    def body(dm_ref, g_ref, v_ref, wg_ref, wv_ref, bg_ref, bv_ref, dup_ref, dwg_ref, dwv_ref, dbg_ref, dbv_ref, stage, sems):
        j = pl.program_id(0)
        slot = j % 2

        def copies(step, sl):
            return [pltpu.make_async_copy(stage.at[sl, half],
                                          dup_ref.at[:, pl.ds(pl.multiple_of(half * D_FF + step * LANE, LANE), LANE)],
                                          sems.at[sl, half]) for half in range(2)]

        @pl.when(j >= 2)
        def _():
            for cp in copies(j - 2, slot):
                cp.wait()

        live = _row_ids((tp, LANE)) < t_real
        gm, vm = jnp.where(live, g_ref[...], 0.0), jnp.where(live, v_ref[...], 0.0)
        gc = _conv_fwd_val(gm, wg_ref[...], bg_ref[...], 1)
        vc = _conv_fwd_val(vm, wv_ref[...], bv_ref[...], 1)
        act, dact = _gelu_and_grad(gc)
        dmm = jnp.where(live, dm_ref[...], 0.0)
        dgx, dwg_ref[...], dbg_ref[...] = _conv_bwd_val(dmm * vc * dact, gm, wg_ref[...], 1, live)
        dvx, dwv_ref[...], dbv_ref[...] = _conv_bwd_val(dmm * act, vm, wv_ref[...], 1, live)
        stage[slot, 0] = dgx.astype(BF16)
        stage[slot, 1] = dvx.astype(BF16)
        for cp in copies(j, slot):
            cp.start()

        @pl.when(j == nb - 1)
        def _():
            for cp in copies(j - 1, 1 - slot) + copies(j, slot):
                cp.wait()

    b2 = b.reshape(1, -1)
    col = lambda rows, off: pl.BlockSpec((rows, LANE), functools.partial(lambda j, o: (0, j + o), o=off))
    return pl.pallas_call(
        body, name=name, grid=(nb,),
        in_specs=[col(tp, 0), col(tp, 0), col(tp, nb), col(kk, 0), col(kk, nb), col(1, 0), col(1, nb)],
        out_specs=[pl.BlockSpec(memory_space=pl.ANY), col(kk, 0), col(kk, 0), col(1, 0), col(1, 0)],
        out_shape=[jax.ShapeDtypeStruct((tp, 2 * D_FF), BF16), jax.ShapeDtypeStruct((kk, D_FF), F32),
                   jax.ShapeDtypeStruct((kk, D_FF), F32), jax.ShapeDtypeStruct((1, D_FF), F32),
                   jax.ShapeDtypeStruct((1, D_FF), F32)],
        scratch_shapes=[pltpu.VMEM((2, 2, tp, LANE), BF16), pltpu.SemaphoreType.DMA((2, 2))],
        compiler_params=_cparams("arbitrary"),
    )(dm, up, up, w, w, b2, b2)


def _lru_gates_fwd(xc, wg, b4, lam, t_real, name):
    tp = xc.shape[0]
    tm = _tile(tp, 1408)

    def body(x_ref, w_ref, b_ref, lam_ref, r0_ref, r1_ref, i0_ref, i1_ref, a0_ref, a1_ref, u0_ref, u1_ref):
        x = x_ref[...]
        xb = x.astype(BF16)
        live = _row_ids(x.shape, pl.program_id(1) * tm) < t_real
        bb = b_ref[...]
        sp = _softplus_neg(lam_ref[...])
        gate = [_sigmoid(jnp.dot(xb, w_ref[k], preferred_element_type=F32) + bb[k:k + 1, :]) for k in range(4)]
        for d, (r_ref, i_ref, a_ref, u_ref) in enumerate(((r0_ref, i0_ref, a0_ref, u0_ref), (r1_ref, i1_ref, a1_ref, u1_ref))):
            r, ig = gate[d], gate[2 + d]
            a = jnp.exp(-LRU_C * r * sp[d:d + 1, :])
            r_ref[...] = r
            i_ref[...] = ig
            a_ref[...] = a
            u_ref[...] = jnp.where(live, jnp.sqrt(1.0 - a * a) * (ig * x), 0.0)

    blk = pl.BlockSpec((tm, LANE), lambda g, i: (i, g))
    return pl.pallas_call(
        body, name=name, grid=(LRU_BLOCKS, tp // tm),
        in_specs=[blk, pl.BlockSpec((None, 4, LANE, LANE), lambda g, i: (g, 0, 0, 0)),
                  pl.BlockSpec((4, LANE), lambda g, i: (0, g)), pl.BlockSpec((2, LANE), lambda g, i: (0, g))],
        out_specs=[blk] * 8,
        out_shape=[jax.ShapeDtypeStruct((tp, D_MODEL), F32)] * 8,
        compiler_params=_cparams("parallel", "parallel"),
    )(xc, wg, b4, lam)


def _lru_gates_bwd(l0, l1, da0, da1, r0, r1, i0, i1, a0, a1, xc, wg, lam, t_real, name):
    tp = xc.shape[0]
    tm = _tile(tp, 1408)

    rc = 32
    assert tm % rc == 0

    def fold(v):
        out = v[0:SUBLANE]
        for t in range(1, rc // SUBLANE):
            out = out + v[t * SUBLANE:(t + 1) * SUBLANE]
        return out

    def body(l0_ref, l1_ref, da0_ref, da1_ref, r0_ref, r1_ref, i0_ref, i1_ref, a0_ref, a1_ref, x_ref, w_ref, lam_ref,
             dx_ref, dw_ref, db_ref, dlam_ref, pre_s, dxp_s):
        i = pl.program_id(1)
        lam_ = lam_ref[...]
        sp = _softplus_neg(lam_)
        dsp_dlam = -_sigmoid(-lam_)

        def chunk(c, sums):
            r0 = pl.multiple_of(c * rc, rc)
            rows = pl.ds(r0, rc)
            x = x_ref[rows, :]
            live = _row_ids((rc, LANE), i * tm + r0) < t_real
            dxp = jnp.zeros_like(x)
            sums = list(sums)
            for d, (l_ref, da_ref, r_ref, i_ref, a_ref) in enumerate(((l0_ref, da0_ref, r0_ref, i0_ref, a0_ref),
                                                                      (l1_ref, da1_ref, r1_ref, i1_ref, a1_ref))):
                r, ig, a = r_ref[rows, :], i_ref[rows, :], a_ref[rows, :]
                du = jnp.where(live, l_ref[rows, :], 0.0)
                a2 = a * a
                rs = lax.rsqrt(1.0 - a2)
                dv = du * ((1.0 - a2) * rs)
                ds = du * (ig * x)
                dla = jnp.where(live, da_ref[rows, :] * a - ds * (a2 * rs), 0.0)
                dr = dla * (-LRU_C) * sp[d:d + 1, :]
                p_r = dr * r * (1.0 - r)
                p_i = dv * x * ig * (1.0 - ig)
                pre_s[d, rows, :] = p_r.astype(BF16)
                pre_s[2 + d, rows, :] = p_i.astype(BF16)
                dxp = dxp + dv * ig
                sums[d] = sums[d] + fold(p_r)
                sums[2 + d] = sums[2 + d] + fold(p_i)
                sums[4 + d] = sums[4 + d] + fold(dla * (-LRU_C) * r)
            dxp_s[rows, :] = dxp
            return tuple(sums)

        zero = jnp.zeros((SUBLANE, LANE), F32)
        sums = lax.fori_loop(0, tm // rc, chunk, (zero,) * 6)

        @pl.when(i == 0)
        def _():
            dw_ref[...] = jnp.zeros_like(dw_ref)
            db_ref[...] = jnp.zeros_like(db_ref)
            dlam_ref[...] = jnp.zeros_like(dlam_ref)

        xb = x_ref[...].astype(BF16)
        dx = dxp_s[...]
        for k in range(4):
            pk = pre_s[k]
            dx = dx + lax.dot_general(pk, w_ref[k], (((1,), (1,)), ((), ())), preferred_element_type=F32)
            dw_ref[k] += lax.dot_general(xb, pk, (((0,), (0,)), ((), ())), preferred_element_type=F32)
        db_ref[...] += jnp.concatenate([jnp.sum(sums[k], axis=0, keepdims=True) for k in range(4)], axis=0)
        dlam_ref[...] += jnp.concatenate([jnp.sum(sums[4 + d], axis=0, keepdims=True) * dsp_dlam[d:d + 1, :] for d in range(2)], axis=0)
        dx_ref[...] = dx

    blk = pl.BlockSpec((tm, LANE), lambda g, i: (i, g))
    return pl.pallas_call(
        body, name=name, grid=(LRU_BLOCKS, tp // tm),
        in_specs=[blk] * 11 + [pl.BlockSpec((None, 4, LANE, LANE), lambda g, i: (g, 0, 0, 0)),
                               pl.BlockSpec((2, LANE), lambda g, i: (0, g))],
        out_specs=[blk, pl.BlockSpec((None, 4, LANE, LANE), lambda g, i: (g, 0, 0, 0)),
                   pl.BlockSpec((4, LANE), lambda g, i: (0, g)), pl.BlockSpec((2, LANE), lambda g, i: (0, g))],
        out_shape=[jax.ShapeDtypeStruct((tp, D_MODEL), F32), jax.ShapeDtypeStruct((LRU_BLOCKS, 4, LANE, LANE), F32),
                   jax.ShapeDtypeStruct((4, D_MODEL), F32), jax.ShapeDtypeStruct((2, D_MODEL), F32)],
        scratch_shapes=[pltpu.VMEM((4, tm, LANE), BF16), pltpu.VMEM((tm, LANE), F32)],
        compiler_params=_cparams("parallel", "arbitrary"),
    )(l0, l1, da0, da1, r0, r1, i0, i1, a0, a1, xc, wg, lam)


SCAN_UNROLL = 4


def _loop_tiles(nt, step, carry):
    assert nt % SCAN_UNROLL == 0

    def trip(tt, c):
        for u in range(SCAN_UNROLL):
            c = step(tt * SCAN_UNROLL + u, c)
        return c

    return lax.fori_loop(0, nt // SCAN_UNROLL, trip, carry)


def _tile_scan(a, u, reverse):
    rows = lax.broadcasted_iota(jnp.int32, a.shape, 0)
    for s in (1, 2, 4):
        if reverse:
            keep = rows < SUBLANE - s
            a_sh, u_sh = pltpu.roll(a, SUBLANE - s, 0), pltpu.roll(u, SUBLANE - s, 0)
        else:
            keep = rows >= s
            a_sh, u_sh = pltpu.roll(a, s, 0), pltpu.roll(u, s, 0)
        u = u + a * jnp.where(keep, u_sh, 0.0)
        a = a * jnp.where(keep, a_sh, 1.0)
    return a, u


def _scan_fwd(a0, u0, a1, u1, proj, name):
    tp, d = a0.shape
    tc = 128
    nt = tp // SUBLANE

    def body(a0_ref, u0_ref, a1_ref, u1_ref, lg_ref, h0_ref, h1_ref, gh_ref):
        def step(t, carry):
            c0, c1 = carry
            f = pl.multiple_of(t * SUBLANE, SUBLANE)
            b = pl.multiple_of((nt - 1 - t) * SUBLANE, SUBLANE)
            pa, pu = _tile_scan(a0_ref[pl.ds(f, SUBLANE), :], u0_ref[pl.ds(f, SUBLANE), :], False)
            h = pu + pa * c0
            h0_ref[pl.ds(f, SUBLANE), :] = h
            c0 = h[SUBLANE - 1:SUBLANE, :]
            pa, pu = _tile_scan(a1_ref[pl.ds(b, SUBLANE), :], u1_ref[pl.ds(b, SUBLANE), :], True)
            h = pu + pa * c1
            h1_ref[pl.ds(b, SUBLANE), :] = h
            c1 = h[0:1, :]
            return c0, c1

        z = jnp.zeros((1, tc), F32)
        _loop_tiles(nt, step, (z, z))
        gh_ref[...] = (_gelu(lg_ref[...]) * (h0_ref[...] + h1_ref[...])).astype(BF16)

    blk = pl.BlockSpec((tp, tc), lambda j: (0, j))
    return pl.pallas_call(
        body, name=name, grid=(d // tc,),
        in_specs=[blk] * 4 + [pl.BlockSpec((tp, tc), lambda j: (0, j + C_LRU_G // tc))], out_specs=[blk] * 3,
        out_shape=[jax.ShapeDtypeStruct((tp, d), F32)] * 2 + [jax.ShapeDtypeStruct((tp, d), BF16)],
        compiler_params=_cparams("parallel"),
    )(a0, u0, a1, u1, proj)


def _scan_bwd(dh, a0, a1, h0, h1, name):
    tp, d = dh.shape
    tc = 128
    nt = tp // SUBLANE

    def body(dh_ref, a0_ref, a1_ref, h0_ref, h1_ref, l0_ref, l1_ref, da0_ref, da1_ref):
        rows8 = lax.broadcasted_iota(jnp.int32, (SUBLANE, tc), 0)

        def step(t, carry):
            c0, c1 = carry
            b = pl.multiple_of((nt - 1 - t) * SUBLANE, SUBLANE)
            f = pl.multiple_of(t * SUBLANE, SUBLANE)
            a = a0_ref[pl.ds(b, SUBLANE), :]
            a_next = jnp.where(rows8 < SUBLANE - 1, pltpu.roll(a, SUBLANE - 1, 0), 1.0)
            pa, pu = _tile_scan(a_next, dh_ref[pl.ds(b, SUBLANE), :], True)
            lam = pu + pa * c0
            l0_ref[pl.ds(b, SUBLANE), :] = lam
            c0 = a[0:1, :] * lam[0:1, :]
            a = a1_ref[pl.ds(f, SUBLANE), :]
            a_prev = jnp.where(rows8 >= 1, pltpu.roll(a, 1, 0), 1.0)
            pa, pu = _tile_scan(a_prev, dh_ref[pl.ds(f, SUBLANE), :], False)
            lam = pu + pa * c1
            l1_ref[pl.ds(f, SUBLANE), :] = lam
            c1 = a[SUBLANE - 1:SUBLANE, :] * lam[SUBLANE - 1:SUBLANE, :]
            return c0, c1

        z = jnp.zeros((1, tc), F32)
        _loop_tiles(nt, step, (z, z))
        rows = lax.broadcasted_iota(jnp.int32, (tp, tc), 0)
        da0_ref[...] = l0_ref[...] * jnp.where(rows >= 1, pltpu.roll(h0_ref[...], 1, 0), 0.0)
        da1_ref[...] = l1_ref[...] * jnp.where(rows < tp - 1, pltpu.roll(h1_ref[...], tp - 1, 0), 0.0)

    blk = pl.BlockSpec((tp, tc), lambda j: (0, j))
    return pl.pallas_call(
        body, name=name, grid=(d // tc,), in_specs=[blk] * 5, out_specs=[blk] * 4,
        out_shape=[jax.ShapeDtypeStruct((tp, d), F32)] * 4,
        compiler_params=_cparams("parallel"),
    )(dh, a0, a1, h0, h1)


def _gated_h_bwd(dgh, proj, h0, h1, dproj, name):
    def fn(row0, dg, lg, x0, x1):
        act, dact = _gelu_and_grad(lg)
        return dg * (x0 + x1) * dact, dg * act

    return _rows(fn, name, [Rw(dgh), Rw(proj, D_MODEL, C_LRU_G // D_MODEL), Rw(h0), Rw(h1)],
                 [Into(dproj, C_LRU_G, D_MODEL), (D_MODEL, F32)])


def _mix(proj, y_mla, y_lru, name):
    def fn(row0, gm, gl, ym, yl):
        return _sigmoid(gm) * ym + _sigmoid(gl) * yl

    return _rows(fn, name, [Rw(proj, D_MODEL, C_G_MLA // D_MODEL), Rw(proj, D_MODEL, C_G_LRU // D_MODEL), Rw(y_mla), Rw(y_lru)],
                 [(D_MODEL, BF16)])[0]


def _mix_bwd(dz, proj, y_mla, y_lru, dproj, name):
    def fn(row0, dzb, gm, gl, ym, yl):
        sm, sl = _sigmoid(gm), _sigmoid(gl)
        dg = jnp.concatenate([dzb * ym * sm * (1.0 - sm), dzb * yl * sl * (1.0 - sl)], axis=1)
        return dzb * sm, dzb * sl, dg

    return _rows(fn, name, [Rw(dz), Rw(proj, D_MODEL, C_G_MLA // D_MODEL), Rw(proj, D_MODEL, C_G_LRU // D_MODEL),
                            Rw(y_mla), Rw(y_lru)], [(D_MODEL, BF16), (D_MODEL, BF16), Into(dproj, C_G_MLA, 2 * D_MODEL)])


def _layer_fwd(h, w_in, more_weights, cs, t_real, tag):
    proj = _matmul(h, w_in, tag + "proj", tb=True)
    w = dict(more_weights(0, proj), w_in=w_in)
    cqn, ckvn = _mla_norms(proj, w['q_norm'], w['kv_norm'], tag + "mla_norms")
    qext = _matmul(cqn, w['w_q'], tag + "q_up")
    kv = _matmul(ckvn, w['w_kv'], tag + "kv_up")
    qc, kc, vb = _mla_pack(qext, kv, proj, cs, tag + "mla_pack")
    o, lse = _attn_fwd(qc, kc, vb, t_real, tag + "attn_fwd")
    w.update(more_weights(1, o))
    y_mla = _matmul(o, w['w_o_mla'], tag + "o_mla")
    xc = _lru_conv_fwd(proj, w['lru_conv_w'], w['lru_conv_b'], t_real, tag + "lru_conv")
    r0, r1, i0, i1, a0, a1, u0, u1 = _lru_gates_fwd(xc, w['w_g'], w['b4'], w['lru_lambda'], t_real, tag + "lru_gates")
    h0, h1, gh = _scan_fwd(a0, u0, a1, u1, proj, tag + "lru_scan")
    y_lru = _matmul(gh, w['w_o_lru'], tag + "o_lru")
    z = _mix(proj, y_mla, y_lru, tag + "mix")
    zo = _matmul(z, w['w_out'], tag + "w_out")
    hm = _ln_fwd([(DN_ALPHA, h), (1.0, zo)], w['ln1_g'], w['ln1_b'], tag + "ln1")
    w.update(more_weights(2, hm))
    up = _matmul(hm, w['w_up'], tag + "w_up", tb=True)
    m = _ffn_conv_act(up, w['ffn_conv_w'], w['ffn_conv_b'], t_real, tag + "ffn_conv")
    f = _matmul(m, w['w_down'], tag + "w_down", tk_cap=1408)
    out = _ln_fwd([(DN_ALPHA, hm), (1.0, f)], w['ln2_g'], w['ln2_b'], tag + "ln2")
    saved = dict(w=w, h=h, proj=proj, cqn=cqn, ckvn=ckvn, qc=qc, kc=kc, vb=vb, o=o, lse=lse, y_mla=y_mla, xc=xc,
                 r0=r0, r1=r1, i0=i0, i1=i1, a0=a0, a1=a1, h0=h0, h1=h1, gh=gh, y_lru=y_lru, z=z, zo=zo, hm=hm,
                 up=up, m=m, f=f)
    return out, saved


DW_MATMUL = dict(ta=True, out_dtype=BF16, tn_cap=2304, tk_cap=1408)


def _after(a, tok):
    return a if tok is None else a + tok.astype(a.dtype)


def _layer_bwd(dout_terms, s, cs, t_real, tag, emit, tok):
    w = s['w']
    g = {}
    du2, dg2, db2 = _ln_bwd(dout_terms, [(DN_ALPHA, s['hm']), (1.0, s['f'])], _after(w['ln2_g'], tok), tag + "ln2_bwd")
    g['ln2_g'], g['ln2_b'] = dg2, db2
    dm = _matmul(du2, w['w_down'], tag + "w_down_dx", tb=True)
    g['w_down'] = _matmul(s['m'], du2, tag + "w_down_dw", **DW_MATMUL)
    dup, dwg_, dwv_, dbg_, dbv_ = _ffn_conv_act_bwd(dm, s['up'], w['ffn_conv_w'], w['ffn_conv_b'], t_real, tag + "ffn_conv_bwd")
    g['ffn_conv_w'] = jnp.concatenate([dwg_, dwv_], axis=1)
    g['ffn_conv_b'] = jnp.concatenate([dbg_, dbv_], axis=1)
    dhm_mm = _matmul(dup, w['w_up'], tag + "w_up_dx", tk_cap=1408)
    g['w_up'] = _matmul(s['hm'], dup, tag + "w_up_dw", **DW_MATMUL)
    tok = emit('ffn', g)
    g = {}
    du1, dg1, db1 = _ln_bwd([(DN_ALPHA, du2), (1.0, dhm_mm)], [(DN_ALPHA, s['h']), (1.0, s['zo'])],
                            _after(w['ln1_g'], tok), tag + "ln1_bwd")
    g['ln1_g'], g['ln1_b'] = dg1, db1
    dz = _matmul(du1, w['w_out'], tag + "w_out_dx", tb=True)
    g['w_out'] = _matmul(s['z'], du1, tag + "w_out_dw", **DW_MATMUL)
    dproj = lax.empty(s['proj'].shape, BF16)
    dy_mla, dy_lru, dproj = _mix_bwd(dz, s['proj'], s['y_mla'], s['y_lru'], dproj, tag + "mix_bwd")
    do = _matmul(dy_mla, w['w_o_mla'], tag + "o_mla_dx", tb=True)
    g['w_o_mla'] = _matmul(s['o'], dy_mla, tag + "o_mla_dw", **DW_MATMUL)
    dqext, dkc, dv = _attn_bwd(s['qc'], s['kc'], s['vb'], do, s['o'], s['lse'], cs, t_real, tag + "attn_bwd")
    dkv, dkrp = _mla_unpack(dkc, dv, cs, tag + "mla_unpack")
    dcqn = _matmul(dqext, w['w_q'], tag + "q_up_dx", tb=True)
    g['w_q'] = _matmul(s['cqn'], dqext, tag + "q_up_dw", **DW_MATMUL)
    dckvn = _matmul(dkv, w['w_kv'], tag + "kv_up_dx", tb=True)
    g['w_kv'] = _matmul(s['ckvn'], dkv, tag + "kv_up_dw", **DW_MATMUL)
    dgh = _matmul(dy_lru, w['w_o_lru'], tag + "o_lru_dx", tb=True)
    g['w_o_lru'] = _matmul(s['gh'], dy_lru, tag + "o_lru_dw", **DW_MATMUL)
    dproj, dhs = _gated_h_bwd(dgh, s['proj'], s['h0'], s['h1'], dproj, tag + "lru_gate_out_bwd")
    l0, l1, da0, da1 = _scan_bwd(dhs, s['a0'], s['a1'], s['h0'], s['h1'], tag + "lru_scan_bwd")
    dxc, g['w_g'], g['b4'], g['lru_lambda'] = _lru_gates_bwd(
        l0, l1, da0, da1, s['r0'], s['r1'], s['i0'], s['i1'], s['a0'], s['a1'], s['xc'], w['w_g'], w['lru_lambda'],
        t_real, tag + "lru_gates_bwd")
    dproj, g['lru_conv_w'], g['lru_conv_b'] = _lru_conv_bwd(dxc, s['proj'], w['lru_conv_w'], dproj, t_real, tag + "lru_conv_bwd")
    tok = emit('mid', g)
    dproj, dqn, dkvn = _mla_norms_bwd(dcqn, dckvn, _after(dkrp, tok), s['proj'], w['q_norm'], w['kv_norm'], dproj,
                                      tag + "mla_norms_bwd")
    tok = emit('in', {'w_in': _matmul(s['h'], dproj, tag + "proj_dw", **DW_MATMUL), 'q_norm': dqn, 'kv_norm': dkvn})
    dh_mm = _matmul(dproj, w['w_in'], tag + "proj_dx", tk_cap=1536)
    return [(DN_ALPHA, du1), (1.0, dh_mm)], tok


def _swap_halves(a, axis=-1):
    h1, h2 = jnp.split(a, 2, axis=axis)
    return jnp.concatenate([h2, h1], axis=axis)


def _w_in_kernel(w_in_t):
    cq, ckv, kr, lg, lx, gm, gl = jnp.split(w_in_t, [256, 384, 448, 1472, 2496, 3520], axis=0)
    return jnp.concatenate([lg, lx, gm, gl, cq, ckv, kr, _swap_halves(kr, axis=0)], axis=0)


def _layer_weights(fl):
    w = {}
    if 'w_uq' in fl:
        uq = fl['w_uq']
        w['w_q'] = jnp.concatenate([uq, _swap_halves(uq[..., QK_NOPE:])], axis=-1).reshape(Q_RANK, HEADS * 2 * LANE)
        w['w_kv'] = jnp.concatenate([fl['w_uk'].reshape(KV_RANK, -1), fl['w_uv'].reshape(KV_RANK, -1)], axis=1).astype(BF16)
        w['w_g'] = jnp.moveaxis(jnp.concatenate([fl['w_rg'], fl['w_ig']], axis=0), 0, 1).astype(BF16)
        w['b4'] = jnp.concatenate([fl['b_rg'], fl['b_ig']], axis=0)
    for n in ('q_norm', 'kv_norm', 'w_o_mla', 'lru_conv_w', 'lru_conv_b', 'lru_lambda', 'w_o_lru', 'w_out', 'ln1_g',
              'ln1_b', 'w_up', 'ffn_conv_w', 'ffn_conv_b', 'w_down', 'ln2_g', 'ln2_b'):
        if n in fl:
            w[n] = fl[n]
    return w


def _layer_grads(g):
    out = {}
    if 'w_in' in g:
        lg, lx, gm, gl, cq, ckv, kr, krs = jnp.split(g['w_in'], [1024, 2048, 3072, 4096, 4352, 4480, 4544], axis=1)
        out['w_in'] = jnp.concatenate([cq, ckv, kr + _swap_halves(krs), lg, lx, gm, gl], axis=1)
    if 'w_q' in g:
        gq = g['w_q'].reshape(Q_RANK, HEADS, 2 * LANE)
        out['w_uq'] = jnp.concatenate([gq[..., :QK_NOPE], gq[..., QK_NOPE:QK_NOPE + QK_ROPE] + _swap_halves(gq[..., QK_NOPE + QK_ROPE:])], axis=-1)
    if 'w_kv' in g:
        out['w_uk'] = g['w_kv'][:, :HEADS * QK_NOPE].reshape(KV_RANK, HEADS, QK_NOPE)
        out['w_uv'] = g['w_kv'][:, HEADS * QK_NOPE:].reshape(KV_RANK, HEADS, V_HEAD)
    if 'w_g' in g:
        gg = jnp.moveaxis(g['w_g'], 1, 0)
        out['w_rg'], out['w_ig'] = gg[:2], gg[2:]
    if 'b4' in g:
        out['b_rg'], out['b_ig'] = g['b4'][:2], g['b4'][2:]
    for n in ('q_norm', 'kv_norm', 'lru_conv_b', 'ln1_g', 'ln1_b', 'ffn_conv_b', 'ln2_g', 'ln2_b'):
        if n in g:
            out[n] = g[n].reshape(-1)
    for n in ('w_o_mla', 'lru_conv_w', 'lru_lambda', 'w_o_lru', 'w_out', 'w_up', 'ffn_conv_w', 'w_down'):
        if n in g:
            out[n] = g[n]
    return out


def _rope_table(tp):
    half = QK_ROPE // 2
    inv_freq = jnp.exp(-math.log(ROPE_THETA) * jnp.arange(half, dtype=F32) / half)
    ang = jnp.arange(tp, dtype=F32)[:, None] * inv_freq[None, :]
    c, s = jnp.cos(ang), jnp.sin(ang)
    return jnp.concatenate([c, c, -s, s], axis=1)


def _local_step(x, target, meta, ln0_g, ln0_b, layer_w, t_pad, emit):
    seq = x.shape[0]
    t_real = N_META + seq
    zpad = jnp.zeros((t_pad - t_real, D_MODEL), F32)
    xin = jnp.concatenate([meta, x, zpad], axis=0)
    tgt = jnp.concatenate([jnp.zeros((N_META, D_MODEL), F32), target, zpad], axis=0)
    cs = _rope_table(t_pad)
    h = _ln_fwd([(1.0, xin)], ln0_g, ln0_b, "ln0")
    saved = []
    for l in range(DEPTH):
        w_in, rest_of_weights = layer_w[l](h)
        h, s = _layer_fwd(h, w_in, rest_of_weights, cs, t_real, "l%d_" % l)
        saved.append(s)
    dy, lossvec = _loss_head(h, tgt, t_real, "loss_head")
    terms, tok = [(1.0, dy)], None
    for l in reversed(range(DEPTH)):
        terms, tok = _layer_bwd(terms, saved[l], cs, t_real, "l%d_" % l,
                                functools.partial(lambda stage, g, l: emit(l, stage, _layer_grads(g)), l=l), tok)
    dxin, dg0, db0 = _ln_bwd(terms, [(1.0, xin)], _after(ln0_g, tok), "ln0_bwd")
    emit(None, 'head', {'meta_tokens': dxin[:N_META], 'ln0_g': dg0.reshape(-1), 'ln0_b': db0.reshape(-1), 'loss': lossvec})
    return dxin[N_META:t_real]


_HBM = pl.BlockSpec(memory_space=pltpu.HBM)
_SEM = pl.BlockSpec(memory_space=pltpu.SEMAPHORE)
_SIDE_EFFECT = pltpu.SideEffectType.DATAFLOW_SIDE_EFFECTING


def _peer_copies(src_refs, land_refs, scatters, send_sems, recv_sems):
    x, y, c = lax.axis_index("x"), lax.axis_index("y"), lax.axis_index("c")
    me = 4 * x + 2 * y + c
    copies = []
    for k in range(1, N_DEV):
        px = 1 - x if k & 4 else x
        py = 1 - y if k & 2 else y
        pc = 1 - c if k & 1 else c
        for t, (src, land) in enumerate(zip(src_refs, land_refs)):
            copies.append(pltpu.make_async_remote_copy(
                src_ref=src.at[4 * px + 2 * py + pc] if scatters[t] else src, dst_ref=land.at[me],
                send_sem=send_sems.at[7 * t + k - 1], recv_sem=recv_sems.at[7 * t + k - 1],
                device_id=(px, py, pc), device_id_type=pl.DeviceIdType.MESH))
    return me, copies


def _own_block_in_place(land, own):
    me = 4 * lax.axis_index("x") + 2 * lax.axis_index("y") + lax.axis_index("c")
    return lax.dynamic_update_slice_in_dim(land, own, me, 0)


def _gather_two_level(shards, name):
    nt = len(shards)

    def body(*refs):
        x_refs, out_refs = refs[:nt], refs[nt:2 * nt]
        token_ref, send_sems, recv_sems = refs[2 * nt:]
        x, y, c = lax.axis_index("x"), lax.axis_index("y"), lax.axis_index("c")
        me, sibling = (x, y, c), (x, y, 1 - c)
        chips = [(1 - x, y), (x, 1 - y), (1 - x, 1 - y)]

        def copy(t, k, block, to, own=False):
            px, py, pc = block
            slot = out_refs[t].at[4 * px + 2 * py + pc]
            return pltpu.make_async_remote_copy(
                src_ref=x_refs[t] if own else slot, dst_ref=slot,
                send_sem=send_sems.at[7 * t + k], recv_sem=recv_sems.at[7 * t + k],
                device_id=to, device_id_type=pl.DeviceIdType.MESH)

        sent = []
        for t in range(nt):
            first = [copy(t, 1 + j, me, (*chip, c), own=True) for j, chip in enumerate(chips)]
            first.append(copy(t, 0, me, sibling, own=True))
            for cp in first:
                cp.start()
            sent += first
        token_ref[...] = jnp.zeros_like(token_ref)
        for j, chip in enumerate(chips):
            for t in range(nt):
                copy(t, 1 + j, (*chip, c), me).wait_recv()
                passed = copy(t, 4 + j, (*chip, c), sibling)
                passed.start()
                sent.append(passed)
        for t in range(nt):
            copy(t, 0, sibling, me).wait_recv()
            for j, chip in enumerate(chips):
                copy(t, 4 + j, (*chip, 1 - c), me).wait_recv()
        for cp in sent:
            cp.wait_send()

    any_space = pl.BlockSpec(memory_space=pl.ANY)
    res = pl.pallas_call(
        body, name=name,
        out_shape=[jax.ShapeDtypeStruct((N_DEV,) + a.shape, a.dtype) for a in shards] + [jax.ShapeDtypeStruct((SUBLANE, LANE), F32)],
        in_specs=[any_space] * nt, out_specs=[any_space] * nt + [pl.BlockSpec(memory_space=pltpu.VMEM)],
        scratch_shapes=[pltpu.SemaphoreType.DMA((7 * nt,)), pltpu.SemaphoreType.DMA((7 * nt,))],
    )(*shards)
    return [_own_block_in_place(land, a[None]) for land, a in zip(res[:nt], shards)], res[nt][0, 0]


def _exchange_start(groups, name):
    flat = [it for grp in groups for it in grp]
    nt, ng = len(flat), len(groups)
    scatters = [sc for _, sc in flat]
    srcs = [pltpu.with_memory_space_constraint(a, pltpu.HBM) for a, _ in flat]
    land_shapes = [a.shape if sc else (N_DEV,) + a.shape for a, sc in flat]
    lands = [pltpu.with_memory_space_constraint(lax.empty(s, a.dtype), pltpu.HBM) for s, (a, _) in zip(land_shapes, flat)]
    bounds = [0]
    for grp in groups:
        bounds.append(bounds[-1] + len(grp))

    def body(*refs):
        src_refs, land_refs = refs[:nt], refs[nt:2 * nt]
        sem_refs = refs[2 * nt:2 * nt + 2 * ng]
        token_ref = refs[4 * nt + 2 * ng]
        for gi in range(ng):
            lo, hi = bounds[gi], bounds[gi + 1]
            _, copies = _peer_copies(src_refs[lo:hi], land_refs[lo:hi], scatters[lo:hi], sem_refs[2 * gi], sem_refs[2 * gi + 1])
            for cp in copies:
                cp.start()
        token_ref[...] = jnp.zeros_like(token_ref)

    out_shape = []
    for grp in groups:
        out_shape += [pltpu.SemaphoreType.DMA((7 * len(grp),)), pltpu.SemaphoreType.DMA((7 * len(grp),))]
    out_shape += [pltpu.HBM(a.shape, a.dtype) for a in srcs] + [pltpu.HBM(s, a.dtype) for s, a in zip(land_shapes, srcs)]
    out_shape += [jax.ShapeDtypeStruct((SUBLANE, LANE), F32)]
    res = pl.pallas_call(
        body, name=name, out_shape=out_shape,
        in_specs=[_HBM] * (2 * nt),
        out_specs=[_SEM] * (2 * ng) + [_HBM] * (2 * nt) + [pl.BlockSpec(memory_space=pltpu.VMEM)],
        input_output_aliases={t: 2 * ng + t for t in range(2 * nt)},
        compiler_params=pltpu.CompilerParams(has_side_effects=_SIDE_EFFECT),
    )(*srcs, *lands)
    sems, thru, token = res[:2 * ng], res[2 * ng:2 * ng + 2 * nt], res[-1]
    states = []
    for gi in range(ng):
        lo, hi = bounds[gi], bounds[gi + 1]
        states.append((sems[2 * gi], sems[2 * gi + 1], thru[lo:hi], thru[nt + lo:nt + hi], scatters[lo:hi]))
    return states, token[0, 0]


def _exchange_wait(state, after, name):
    send_sems, recv_sems, srcs, lands, scatters = state
    n = len(srcs)

    def body(*refs):
        _, copies = _peer_copies(refs[:n], refs[n:2 * n], scatters, refs[2 * n], refs[2 * n + 1])
        for cp in copies:
            cp.wait_send()
        for cp in copies:
            cp.wait_recv()

    res = pl.pallas_call(
        body, name=name,
        out_shape=[pltpu.HBM(a.shape, a.dtype) for a in srcs] + [pltpu.HBM(a.shape, a.dtype) for a in lands],
        in_specs=[_HBM] * (2 * n) + [_SEM, _SEM, _HBM],
        out_specs=[_HBM] * (2 * n),
        input_output_aliases={t: t for t in range(2 * n)},
        compiler_params=pltpu.CompilerParams(has_side_effects=_SIDE_EFFECT),
    )(*srcs, *lands, send_sems, recv_sems, pltpu.with_memory_space_constraint(after, pltpu.HBM))
    me = 4 * lax.axis_index("x") + 2 * lax.axis_index("y") + lax.axis_index("c")
    out = []
    for src, land, sc in zip(res[:n], res[n:], scatters):
        own = lax.dynamic_index_in_dim(src, me, 0, keepdims=True) if sc else src[None]
        out.append(lax.dynamic_update_slice_in_dim(land, own, me, 0))
    return out


def _as_rows(shape):
    return (1, shape[0]) if len(shape) == 1 else (math.prod(shape[:-1]), shape[-1])


def _sum_adamw(pieces, w, m, v, name):
    shape = w.shape
    nl = len(pieces)
    if nl > 1 and _as_rows(shape[1:])[0] % 16:
        pieces, nl = [jnp.stack(pieces, axis=1)], 1
    rows, cols = _as_rows(shape)
    rl = rows // nl
    cap = max(16, (1 << 18) // cols // 16 * 16)
    tr = _tile(rl, cap, 16)
    nb = rl // tr
    c1 = 1.0 / (1.0 - ADAM_B1 ** ADAM_STEP)
    c2 = 1.0 / (1.0 - ADAM_B2 ** ADAM_STEP)

    def body(*refs):
        p_refs = refs[:nl]
        w_ref, m_ref, v_ref, g_ref, d_ref, nm_ref, nv_ref = refs[nl:]
        li = pl.program_id(0)

        def total(p_ref):
            acc = p_ref[0].astype(F32)
            for k in range(1, N_DEV):
                acc = acc + p_ref[k].astype(F32)
            return acc

        gg = total(p_refs[0])
        for l in range(1, nl):
            gg = jnp.where(li == l, total(p_refs[l]), gg)
        nm = ADAM_B1 * m_ref[...] + (1.0 - ADAM_B1) * gg
        nv = ADAM_B2 * v_ref[...] + (1.0 - ADAM_B2) * (gg * gg)
        g_ref[...] = gg
        d_ref[...] = -ADAM_LR * ((nm * c1) / (jnp.sqrt(nv * c2) + ADAM_EPS) + ADAM_WD * w_ref[...])
        nm_ref[...] = nm
        nv_ref[...] = nv

    blk = pl.BlockSpec((tr, cols), lambda li, i: (li * nb + i, 0))
    p_specs = [pl.BlockSpec((N_DEV, tr, cols), functools.partial(lambda li, i, l: (0, jnp.where(li == l, i, 0), 0), l=l))
               for l in range(nl)]
    res = pl.pallas_call(
        body, name=name, grid=(nl, nb),
        in_specs=p_specs + [blk] * 3, out_specs=[blk] * 4,
        out_shape=[jax.ShapeDtypeStruct((rows, cols), F32)] * 4,
        compiler_params=_cparams("parallel", "parallel"),
    )(*[p.reshape(N_DEV, rl, cols) for p in pieces], *[a.reshape(rows, cols) for a in (w, m, v)])
    return [r.reshape(shape) for r in res]


def _to_shards(full, axis):
    shp = full.shape
    a = full.reshape(shp[:axis] + (N_DEV, shp[axis] // N_DEV) + shp[axis + 1:])
    return jnp.moveaxis(a, axis, 0)


def _from_shards(blocks, axis):
    a = jnp.moveaxis(blocks, 0, axis)
    shp = a.shape
    return a.reshape(shp[:axis] + (shp[axis] * shp[axis + 1],) + shp[axis + 2:])


def kernel(x, meta_tokens, ln0_g, ln0_b, w_in, q_norm, kv_norm, w_uq, w_uk, w_uv, w_o_mla, lru_conv_w, lru_conv_b, w_rg, b_rg, w_ig, b_ig, lru_lambda, w_o_lru, w_out, ln1_g, ln1_b, w_up, ffn_conv_w, ffn_conv_b, w_down, ln2_g, ln2_b, loss_target, m_meta_tokens, m_ln0_g, m_ln0_b, m_w_in, m_q_norm, m_kv_norm, m_w_uq, m_w_uk, m_w_uv, m_w_o_mla, m_lru_conv_w, m_lru_conv_b, m_w_rg, m_b_rg, m_w_ig, m_b_ig, m_lru_lambda, m_w_o_lru, m_w_out, m_ln1_g, m_ln1_b, m_w_up, m_ffn_conv_w, m_ffn_conv_b, m_w_down, m_ln2_g, m_ln2_b, v_meta_tokens, v_ln0_g, v_ln0_b, v_w_in, v_q_norm, v_kv_norm, v_w_uq, v_w_uk, v_w_uv, v_w_o_mla, v_lru_conv_w, v_lru_conv_b, v_w_rg, v_b_rg, v_w_ig, v_b_ig, v_lru_lambda, v_w_o_lru, v_w_out, v_ln1_g, v_ln1_b, v_w_up, v_ffn_conv_w, v_ffn_conv_b, v_w_down, v_ln2_g, v_ln2_b):
    args = (meta_tokens, ln0_g, ln0_b, w_in, q_norm, kv_norm, w_uq, w_uk, w_uv, w_o_mla, lru_conv_w, lru_conv_b, w_rg, b_rg, w_ig, b_ig, lru_lambda, w_o_lru, w_out, ln1_g, ln1_b, w_up, ffn_conv_w, ffn_conv_b, w_down, ln2_g, ln2_b)
    ms = (m_meta_tokens, m_ln0_g, m_ln0_b, m_w_in, m_q_norm, m_kv_norm, m_w_uq, m_w_uk, m_w_uv, m_w_o_mla, m_lru_conv_w, m_lru_conv_b, m_w_rg, m_b_rg, m_w_ig, m_b_ig, m_lru_lambda, m_w_o_lru, m_w_out, m_ln1_g, m_ln1_b, m_w_up, m_ffn_conv_w, m_ffn_conv_b, m_w_down, m_ln2_g, m_ln2_b)
    vs = (v_meta_tokens, v_ln0_g, v_ln0_b, v_w_in, v_q_norm, v_kv_norm, v_w_uq, v_w_uk, v_w_uv, v_w_o_mla, v_lru_conv_w, v_lru_conv_b, v_w_rg, v_b_rg, v_w_ig, v_b_ig, v_lru_lambda, v_w_o_lru, v_w_out, v_ln1_g, v_ln1_b, v_w_up, v_ffn_conv_w, v_ffn_conv_b, v_w_down, v_ln2_g, v_ln2_b)
    wd, md, vd = dict(zip(WEIGHTS, args)), dict(zip(WEIGHTS, ms)), dict(zip(WEIGHTS, vs))

    def shard_axis(n, l):
        return SHARD_AXIS[n] - (0 if l is None else 1)

    def shard(n, l):
        a = wd[n] if l is None else wd[n][l]
        if n in SENT_TRANSPOSED:
            a = a.T
        return a.astype(BF16) if n in BIG else a

    def whole(keys, landed):
        return {k: b.reshape(-1, b.shape[-1]) if k[0] in SENT_TRANSPOSED else _from_shards(b, shard_axis(*k))
                for k, b in zip(keys, landed)}

    first = [('meta_tokens', None), ('w_in', 0)]
    landed, token = _gather_two_level([shard(*k) for k in first], "gather_first")
    got_first = whole(first, landed)
    staged = [[(n, 0) for n in names] for names in STAGE_WEIGHTS]
    later = [(n, 1) for n in SHARDED if n != 'meta_tokens']
    groups = [[(shard(*k), False) for k in keys] for keys in staged + [later]]
    groups[0][0] = (_after(groups[0][0][0], token), False)
    gather, token = _exchange_start(groups, "gather_start")

    def arrive(gi, keys, after, name):
        return whole(keys, _exchange_wait(gather[gi], after, name))

    def layer_weights(got, l, names):
        fl = {n: wd[n][l] for n in names if n in REPLICATED}
        fl.update({n: a for (n, _), a in got.items() if n in names})
        return _layer_weights(fl)

    ln0_g = _after(wd['ln0_g'], token)

    def first_layer(h):
        def more(stage, after):
            got = arrive(stage, staged[stage], after, "gather_wait_l0_%d" % stage)
            return layer_weights(got, 0, STAGE_WEIGHTS[stage] + STAGE_REPLICATED[stage])
        return _w_in_kernel(got_first['w_in', 0]), more

    def second_layer(h):
        got = arrive(len(staged), later, h, "gather_wait_l1")
        return _w_in_kernel(got['w_in', 1]), lambda stage, after: layer_weights(got, 1, STAGE_WEIGHTS[stage] + STAGE_REPLICATED[stage])

    sent = []
    pending = []

    def send(l, stage, grads):
        for n, g in grads.items():
            if n in SHARD_AXIS:
                g = _to_shards(g, shard_axis(n, l))
                pending.append(((n, l), (g.astype(BF16) if n in BIG else g, True)))
            else:
                pending.append(((n, l), (g.astype(BF16) if n in LARGE_REPLICATED else g, False)))
        if l == DEPTH - 1 and stage != 'in':
            return None
        (state,), tok = _exchange_start([[it for _, it in pending]], "grads_start_%s_%s" % (l, stage))
        sent.append(([k for k, _ in pending], state))
        pending.clear()
        return tok

    seq = x.shape[1]
    t_pad = -(-(N_META + seq + MIN_PAD_ROWS) // LANE) * LANE
    grad_x = _local_step(x[0], loss_target[0], got_first['meta_tokens', None], ln0_g, wd['ln0_b'],
                         [first_layer, second_layer], t_pad, send)

    pieces, outs, after = {}, {}, grad_x
    for gi, (keys, state) in enumerate(sent):
        pieces.update(zip(keys, _exchange_wait(state, after, "grads_wait_%d" % gi)))
        if gi == len(sent) - 2:
            for n in WEIGHTS:
                if (n, 0) in pieces:
                    outs[n] = _sum_adamw([pieces[n, l] for l in range(DEPTH)], wd[n], md[n], vd[n], "adamw_" + n)
                    after = outs[n][1]
    loss = jnp.sum(pieces['loss', None])
    for n in WEIGHTS:
        if (n, None) in pieces:
            outs[n] = _sum_adamw([pieces[n, None]], wd[n], md[n], vd[n], "adamw_" + n)
    res = [loss, grad_x[None]]
    for k in range(4):
        res += [outs[n][k] for n in WEIGHTS]
    return tuple(res)
```
